```python
import math
import jax, jax.numpy as jnp
from jax import lax
import numpy as np

D_MODEL = 1024
BATCH = 8
SEQ = 8192
DEPTH = 2

ATT_HEADS = 16
ATT_KV_HEADS = 2
ATT_HEAD_DIM = 64
ATT_WIDTH = ATT_HEADS * ATT_HEAD_DIM
KV_WIDTH = ATT_KV_HEADS * ATT_HEAD_DIM
WINDOW = 128
ATT_BLOCK = 128
REL_BUCKETS = 32
REL_MAX_DIST = 128
SG_GROUPS = 8
SG_CHUNK = 128
SG_WIDTH = 1024
SG_GROUP_DIM = SG_WIDTH // SG_GROUPS
SSM_WIDTH = 2 * D_MODEL
SSM_HEAD_DIM = 64
SSM_HEADS = SSM_WIDTH // SSM_HEAD_DIM
SSM_GROUPS = 4
SSM_STATE = 128
SSM_CONV = 4
SSM_CHUNK = 128
SSM_CONV_DIM = SSM_WIDTH + 2 * SSM_GROUPS * SSM_STATE
N_BRANCHES = 3
IN_SIZES = (ATT_WIDTH, KV_WIDTH, KV_WIDTH, ATT_WIDTH,
            SG_WIDTH, SG_WIDTH, SG_WIDTH,
            SSM_WIDTH, SSM_CONV_DIM, SSM_HEADS,
            N_BRANCHES * D_MODEL)
IN_COLS = sum(IN_SIZES)
EPS = 1e-6

kernel_name = "hybrid_swa_sgu_ssd_gated_merge"


def _split_points():
    return [int(v) for v in np.cumsum(np.array(IN_SIZES))[:-1]]


def rms_norm(x, g):
    xf = x.astype(jnp.float32)
    y = xf * lax.rsqrt(jnp.mean(xf * xf, axis=-1, keepdims=True) + EPS)
    return (y * g.astype(jnp.float32)).astype(x.dtype)


def t5_causal_bucket(dist):
    max_exact = REL_BUCKETS // 2
    dist_f = jnp.maximum(dist, 1).astype(jnp.float32)
    large = max_exact + (jnp.log(dist_f / max_exact) / math.log(REL_MAX_DIST / max_exact)
                         * (REL_BUCKETS - max_exact)).astype(jnp.int32)
    large = jnp.minimum(large, REL_BUCKETS - 1)
    return jnp.where(dist < max_exact, dist, large)


def sliding_window_attention(q, k, v, sinks, rel_bias):
    bsz, seq = q.shape[:2]
    nb = seq // ATT_BLOCK
    grp = ATT_HEADS // ATT_KV_HEADS
    qb = q.reshape(bsz, nb, ATT_BLOCK, ATT_KV_HEADS, grp, ATT_HEAD_DIM) * (ATT_HEAD_DIM ** -0.5)

    def band(t):
        tb = t.reshape(bsz, nb, ATT_BLOCK, ATT_KV_HEADS, ATT_HEAD_DIM)
        prev = jnp.pad(tb, ((0, 0), (1, 0), (0, 0), (0, 0), (0, 0)))[:, :-1]
        return jnp.concatenate([prev, tb], axis=2)

    kk, vv = band(k), band(v)
    logits = jnp.einsum('bnqkgd,bnskd->bnkgqs', qb, kk).astype(jnp.float32)

    qi = jnp.arange(ATT_BLOCK, dtype=jnp.int32)[:, None]
    kj = jnp.arange(2 * ATT_BLOCK, dtype=jnp.int32)[None, :]
    dist = qi + ATT_BLOCK - kj
    in_window = (dist >= 0) & (dist < WINDOW)
    key_exists = (jnp.arange(nb)[:, None] > 0) | (kj >= ATT_BLOCK)
    mask = in_window[None] & key_exists[:, None, :]

    bias = rel_bias.astype(jnp.float32)[t5_causal_bucket(jnp.maximum(dist, 0))]
    bias = jnp.transpose(bias, (2, 0, 1)).reshape(ATT_KV_HEADS, grp, ATT_BLOCK, 2 * ATT_BLOCK)
    logits = jnp.where(mask[None, :, None, None], logits + bias[None, None], -jnp.inf)

    sink = sinks.astype(jnp.float32).reshape(ATT_KV_HEADS, grp)[None, None, :, :, None, None]
    m = jnp.maximum(jnp.max(logits, axis=-1, keepdims=True), sink)
    p = jnp.exp(logits - m)
    p = p / (jnp.sum(p, axis=-1, keepdims=True) + jnp.exp(sink - m))
    out = jnp.einsum('bnkgqs,bnskd->bnqkgd', p.astype(vv.dtype), vv)
    return out.reshape(bsz, seq, ATT_WIDTH)


def chunked_spatial_gate(u, v, ln_g, ln_b, w_s, b_s):
    bsz, seq = u.shape[:2]
    nc = seq // SG_CHUNK
    vf = v.astype(jnp.float32)
    mu = jnp.mean(vf, axis=-1, keepdims=True)
    var = jnp.mean(jnp.square(vf - mu), axis=-1, keepdims=True)
    vn = ((vf - mu) * lax.rsqrt(var + EPS) * ln_g.astype(jnp.float32) + ln_b.astype(jnp.float32)).astype(v.dtype)
    vc = vn.reshape(bsz, nc, SG_CHUNK, SG_GROUPS, SG_GROUP_DIM)
    causal = jnp.tril(jnp.ones((SG_CHUNK, SG_CHUNK), dtype=bool))
    w = jnp.where(causal[None], w_s, jnp.zeros_like(w_s))
    mixed = jnp.einsum('gts,bcsgd->bctgd', w, vc) + jnp.transpose(b_s)[None, None, :, :, None]
    return u * mixed.reshape(bsz, seq, SG_WIDTH)


def causal_depthwise_conv(x, w, b):
    ch = x.shape[-1]
    y = lax.conv_general_dilated(x, w[:, None, :].astype(x.dtype), window_strides=(1,),
                                 padding=[(SSM_CONV - 1, 0)],
                                 dimension_numbers=('NWC', 'WIO', 'NWC'),
                                 feature_group_count=ch)
    return y + b


def ssd_mixer(z, xbc, dt_raw, conv_w, conv_b, dt_bias, a_log, d_skip, norm_g):
    bsz, seq = z.shape[:2]
    nc = seq // SSM_CHUNK
    hpg = SSM_HEADS // SSM_GROUPS
    L = SSM_CHUNK
    xbc = jax.nn.silu(causal_depthwise_conv(xbc, conv_w, conv_b))
    gn = SSM_GROUPS * SSM_STATE
    xs = xbc[..., :SSM_WIDTH]
    b_in = xbc[..., SSM_WIDTH:SSM_WIDTH + gn]
    c_in = xbc[..., SSM_WIDTH + gn:]

    dt = jax.nn.softplus(dt_raw.astype(jnp.float32) + dt_bias.astype(jnp.float32))
    a = -jnp.exp(a_log.astype(jnp.float32))
    x_heads = xs.astype(jnp.float32).reshape(bsz, seq, SSM_HEADS, SSM_HEAD_DIM)
    xdt = (x_heads * dt[..., None]).reshape(bsz, nc, L, SSM_GROUPS, hpg, SSM_HEAD_DIM)
    bc = b_in.astype(jnp.float32).reshape(bsz, nc, L, SSM_GROUPS, SSM_STATE)
    cc = c_in.astype(jnp.float32).reshape(bsz, nc, L, SSM_GROUPS, SSM_STATE)

    a_dt = (dt * a).reshape(bsz, nc, L, SSM_GROUPS, hpg).transpose(0, 3, 4, 1, 2)
    a_cs = jnp.cumsum(a_dt, axis=-1)

    causal = jnp.tril(jnp.ones((L, L), dtype=bool))
    seg = a_cs[..., :, None] - a_cs[..., None, :]
    decay_in = jnp.exp(jnp.where(causal, seg, -jnp.inf))
    cb = jnp.einsum('bclgn,bcsgn->bcgls', cc, bc)
    y_diag = jnp.einsum('bcgls,bgjcls,bcsgjp->bclgjp', cb, decay_in, xdt)

    decay_to_end = jnp.exp(a_cs[..., -1:] - a_cs)
    states = jnp.einsum('bcsgn,bgjcs,bcsgjp->bcgjpn', bc, decay_to_end, xdt)
    chunk_decay = jnp.exp(a_cs[..., -1])

    def carry_state(h, inp):
        dec, st = inp
        return h * dec[..., None, None] + st, h

    init = jnp.zeros((bsz, SSM_GROUPS, hpg, SSM_HEAD_DIM, SSM_STATE), jnp.float32)
    _, prev = lax.scan(carry_state, init, (jnp.moveaxis(chunk_decay, -1, 0), jnp.moveaxis(states, 1, 0)))
    prev = jnp.moveaxis(prev, 0, 1)
    y_off = jnp.einsum('bclgn,bcgjpn,bgjcl->bclgjp', cc, prev, jnp.exp(a_cs))

    y = (y_diag + y_off).reshape(bsz, seq, SSM_HEADS, SSM_HEAD_DIM) + d_skip.astype(jnp.float32)[:, None] * x_heads
    y = y.reshape(bsz, seq, SSM_WIDTH) * jax.nn.silu(z.astype(jnp.float32))
    yg = y.reshape(bsz, seq, SSM_GROUPS, SSM_WIDTH // SSM_GROUPS)
    yg = yg * lax.rsqrt(jnp.mean(yg * yg, axis=-1, keepdims=True) + EPS)
    y = yg.reshape(bsz, seq, SSM_WIDTH) * norm_g.astype(jnp.float32)
    return y.astype(z.dtype)


def hybrid_layer(x, w_in, g_pre, g_post, rel_bias, sinks, sg_ln_g, sg_ln_b, sg_w, sg_b,
                 conv_w, conv_b, dt_bias, a_log, d_skip, ssm_norm_g,
                 w_br_att, w_br_sg, w_br_ssm, w_out):
    bsz, seq = x.shape[:2]
    h = rms_norm(x, g_pre)
    proj = jnp.einsum('bsd,dc->bsc', h, w_in)
    (q, k, v, z_a, u, v_s, z_s, z_m, xbc, dt_raw, gate_logits) = jnp.split(proj, _split_points(), axis=-1)

    q = q.reshape(bsz, seq, ATT_HEADS, ATT_HEAD_DIM)
    k = k.reshape(bsz, seq, ATT_KV_HEADS, ATT_HEAD_DIM)
    v = v.reshape(bsz, seq, ATT_KV_HEADS, ATT_HEAD_DIM)
    y_att = sliding_window_attention(q, k, v, sinks, rel_bias) * jax.nn.silu(z_a)
    y_sg = chunked_spatial_gate(u, v_s, sg_ln_g, sg_ln_b, sg_w, sg_b) * jax.nn.silu(z_s)
    y_ssm = ssd_mixer(z_m, xbc, dt_raw, conv_w, conv_b, dt_bias, a_log, d_skip, ssm_norm_g)

    gates = jax.nn.sigmoid(gate_logits.reshape(bsz, seq, N_BRANCHES, D_MODEL))
    merged = (gates[:, :, 0] * (y_att @ w_br_att)
              + gates[:, :, 1] * (y_sg @ w_br_sg)
              + gates[:, :, 2] * (y_ssm @ w_br_ssm))
    out = merged @ w_out
    return x + rms_norm(out, g_post)


def _fwd_setup_inputs(seed: int = 0) -> dict:
    key = jax.random.key(seed)
    ks = jax.random.split(key, 24)
    f32 = jnp.float32
    nrm = lambda k, shape, s: (jax.random.normal(k, shape, f32) * s)
    dt0 = jnp.exp(jax.random.uniform(ks[12], (DEPTH, SSM_HEADS), f32, math.log(1e-3), math.log(1e-1)))
    return {
        "x": nrm(ks[0], (BATCH, SEQ, D_MODEL), 1.0),
        "w_in": nrm(ks[1], (DEPTH, D_MODEL, IN_COLS), D_MODEL ** -0.5),
        "norm_pre": 1.0 + nrm(ks[2], (DEPTH, D_MODEL), 0.02),
        "norm_post": 1.0 + nrm(ks[3], (DEPTH, D_MODEL), 0.02),
        "rel_bias": nrm(ks[4], (REL_BUCKETS, ATT_HEADS), 0.5),
        "att_sinks": nrm(ks[5], (DEPTH, ATT_HEADS), 0.5),
        "sg_ln_g": 1.0 + nrm(ks[6], (DEPTH, SG_WIDTH), 0.02),
        "sg_ln_b": nrm(ks[7], (DEPTH, SG_WIDTH), 0.02),
        "sg_w": nrm(ks[8], (DEPTH, SG_GROUPS, SG_CHUNK, SG_CHUNK), SG_CHUNK ** -0.5),
        "sg_b": 1.0 + nrm(ks[9], (DEPTH, SG_GROUPS, SG_CHUNK), 0.02),
        "ssm_conv_w": nrm(ks[10], (DEPTH, SSM_CONV, SSM_CONV_DIM), SSM_CONV ** -0.5),
        "ssm_conv_b": nrm(ks[11], (DEPTH, SSM_CONV_DIM), 0.02),
        "ssm_dt_bias": dt0 + jnp.log(-jnp.expm1(-dt0)),
        "ssm_a_log": jnp.log(jax.random.uniform(ks[13], (DEPTH, SSM_HEADS), f32, 1.0, 16.0)),
        "ssm_d": 1.0 + nrm(ks[14], (DEPTH, SSM_HEADS), 0.02),
        "ssm_norm_g": 1.0 + nrm(ks[15], (DEPTH, SSM_WIDTH), 0.02),
        "w_br_att": nrm(ks[16], (DEPTH, ATT_WIDTH, D_MODEL), ATT_WIDTH ** -0.5),
        "w_br_sg": nrm(ks[17], (DEPTH, SG_WIDTH, D_MODEL), SG_WIDTH ** -0.5),
        "w_br_ssm": nrm(ks[18], (DEPTH, SSM_WIDTH, D_MODEL), SSM_WIDTH ** -0.5),
        "w_out": nrm(ks[19], (DEPTH, D_MODEL, D_MODEL), D_MODEL ** -0.5),
    }


def _fwd_reference(x, w_in, norm_pre, norm_post, rel_bias, att_sinks, sg_ln_g, sg_ln_b, sg_w, sg_b,
              ssm_conv_w, ssm_conv_b, ssm_dt_bias, ssm_a_log, ssm_d, ssm_norm_g,
              w_br_att, w_br_sg, w_br_ssm, w_out):
    for layer in range(DEPTH):
        x = hybrid_layer(x, w_in[layer], norm_pre[layer], norm_post[layer], rel_bias,
                         att_sinks[layer], sg_ln_g[layer], sg_ln_b[layer], sg_w[layer], sg_b[layer],
                         ssm_conv_w[layer], ssm_conv_b[layer], ssm_dt_bias[layer], ssm_a_log[layer],
                         ssm_d[layer], ssm_norm_g[layer],
                         w_br_att[layer], w_br_sg[layer], w_br_ssm[layer], w_out[layer])
    return x


import jax as _jax
import jax.numpy as _jnp

TWIN_FORMAT = 'train_step'
FWD_PARAMS = ['x', 'w_in', 'norm_pre', 'norm_post', 'rel_bias', 'att_sinks', 'sg_ln_g', 'sg_ln_b', 'sg_w', 'sg_b', 'ssm_conv_w', 'ssm_conv_b', 'ssm_dt_bias', 'ssm_a_log', 'ssm_d', 'ssm_norm_g', 'w_br_att', 'w_br_sg', 'w_br_ssm', 'w_out']
TWIN_WEIGHTS = ['w_in', 'norm_pre', 'norm_post', 'rel_bias', 'att_sinks', 'sg_ln_g', 'sg_ln_b', 'sg_w', 'sg_b', 'ssm_conv_w', 'ssm_conv_b', 'ssm_dt_bias', 'ssm_a_log', 'ssm_d', 'ssm_norm_g', 'w_br_att', 'w_br_sg', 'w_br_ssm', 'w_out']
TWIN_DIFF_INPUT = 'x'
TWIN_INPUTS = ['x', 'w_in', 'norm_pre', 'norm_post', 'rel_bias', 'att_sinks', 'sg_ln_g', 'sg_ln_b', 'sg_w', 'sg_b', 'ssm_conv_w', 'ssm_conv_b', 'ssm_dt_bias', 'ssm_a_log', 'ssm_d', 'ssm_norm_g', 'w_br_att', 'w_br_sg', 'w_br_ssm', 'w_out', 'loss_target', 'm_w_in', 'm_norm_pre', 'm_norm_post', 'm_rel_bias', 'm_att_sinks', 'm_sg_ln_g', 'm_sg_ln_b', 'm_sg_w', 'm_sg_b', 'm_ssm_conv_w', 'm_ssm_conv_b', 'm_ssm_dt_bias', 'm_ssm_a_log', 'm_ssm_d', 'm_ssm_norm_g', 'm_w_br_att', 'm_w_br_sg', 'm_w_br_ssm', 'm_w_out', 'v_w_in', 'v_norm_pre', 'v_norm_post', 'v_rel_bias', 'v_att_sinks', 'v_sg_ln_g', 'v_sg_ln_b', 'v_sg_w', 'v_sg_b', 'v_ssm_conv_w', 'v_ssm_conv_b', 'v_ssm_dt_bias', 'v_ssm_a_log', 'v_ssm_d', 'v_ssm_norm_g', 'v_w_br_att', 'v_w_br_sg', 'v_w_br_ssm', 'v_w_out']
TWIN_OUTPUTS = ['loss', 'grad_x', 'grad_w_in', 'grad_norm_pre', 'grad_norm_post', 'grad_rel_bias', 'grad_att_sinks', 'grad_sg_ln_g', 'grad_sg_ln_b', 'grad_sg_w', 'grad_sg_b', 'grad_ssm_conv_w', 'grad_ssm_conv_b', 'grad_ssm_dt_bias', 'grad_ssm_a_log', 'grad_ssm_d', 'grad_ssm_norm_g', 'grad_w_br_att', 'grad_w_br_sg', 'grad_w_br_ssm', 'grad_w_out', 'delta_w_in', 'delta_norm_pre', 'delta_norm_post', 'delta_rel_bias', 'delta_att_sinks', 'delta_sg_ln_g', 'delta_sg_ln_b', 'delta_sg_w', 'delta_sg_b', 'delta_ssm_conv_w', 'delta_ssm_conv_b', 'delta_ssm_dt_bias', 'delta_ssm_a_log', 'delta_ssm_d', 'delta_ssm_norm_g', 'delta_w_br_att', 'delta_w_br_sg', 'delta_w_br_ssm', 'delta_w_out', 'new_m_w_in', 'new_m_norm_pre', 'new_m_norm_post', 'new_m_rel_bias', 'new_m_att_sinks', 'new_m_sg_ln_g', 'new_m_sg_ln_b', 'new_m_sg_w', 'new_m_sg_b', 'new_m_ssm_conv_w', 'new_m_ssm_conv_b', 'new_m_ssm_dt_bias', 'new_m_ssm_a_log', 'new_m_ssm_d', 'new_m_ssm_norm_g', 'new_m_w_br_att', 'new_m_w_br_sg', 'new_m_w_br_ssm', 'new_m_w_out', 'new_v_w_in', 'new_v_norm_pre', 'new_v_norm_post', 'new_v_rel_bias', 'new_v_att_sinks', 'new_v_sg_ln_g', 'new_v_sg_ln_b', 'new_v_sg_w', 'new_v_sg_b', 'new_v_ssm_conv_w', 'new_v_ssm_conv_b', 'new_v_ssm_dt_bias', 'new_v_ssm_a_log', 'new_v_ssm_d', 'new_v_ssm_norm_g', 'new_v_w_br_att', 'new_v_w_br_sg', 'new_v_w_br_ssm', 'new_v_w_out']
TWIN_LEAF_KINDS = {'loss': 'loss', 'grad_x': 'grad_x', 'grad_w_in': 'grad_w', 'grad_norm_pre': 'grad_w', 'grad_norm_post': 'grad_w', 'grad_rel_bias': 'grad_w', 'grad_att_sinks': 'grad_w', 'grad_sg_ln_g': 'grad_w', 'grad_sg_ln_b': 'grad_w', 'grad_sg_w': 'grad_w', 'grad_sg_b': 'grad_w', 'grad_ssm_conv_w': 'grad_w', 'grad_ssm_conv_b': 'grad_w', 'grad_ssm_dt_bias': 'grad_w', 'grad_ssm_a_log': 'grad_w', 'grad_ssm_d': 'grad_w', 'grad_ssm_norm_g': 'grad_w', 'grad_w_br_att': 'grad_w', 'grad_w_br_sg': 'grad_w', 'grad_w_br_ssm': 'grad_w', 'grad_w_out': 'grad_w', 'delta_w_in': 'delta_w', 'delta_norm_pre': 'delta_w', 'delta_norm_post': 'delta_w', 'delta_rel_bias': 'delta_w', 'delta_att_sinks': 'delta_w', 'delta_sg_ln_g': 'delta_w', 'delta_sg_ln_b': 'delta_w', 'delta_sg_w': 'delta_w', 'delta_sg_b': 'delta_w', 'delta_ssm_conv_w': 'delta_w', 'delta_ssm_conv_b': 'delta_w', 'delta_ssm_dt_bias': 'delta_w', 'delta_ssm_a_log': 'delta_w', 'delta_ssm_d': 'delta_w', 'delta_ssm_norm_g': 'delta_w', 'delta_w_br_att': 'delta_w', 'delta_w_br_sg': 'delta_w', 'delta_w_br_ssm': 'delta_w', 'delta_w_out': 'delta_w', 'new_m_w_in': 'new_m', 'new_m_norm_pre': 'new_m', 'new_m_norm_post': 'new_m', 'new_m_rel_bias': 'new_m', 'new_m_att_sinks': 'new_m', 'new_m_sg_ln_g': 'new_m', 'new_m_sg_ln_b': 'new_m', 'new_m_sg_w': 'new_m', 'new_m_sg_b': 'new_m', 'new_m_ssm_conv_w': 'new_m', 'new_m_ssm_conv_b': 'new_m', 'new_m_ssm_dt_bias': 'new_m', 'new_m_ssm_a_log': 'new_m', 'new_m_ssm_d': 'new_m', 'new_m_ssm_norm_g': 'new_m', 'new_m_w_br_att': 'new_m', 'new_m_w_br_sg': 'new_m', 'new_m_w_br_ssm': 'new_m', 'new_m_w_out': 'new_m', 'new_v_w_in': 'new_v', 'new_v_norm_pre': 'new_v', 'new_v_norm_post': 'new_v', 'new_v_rel_bias': 'new_v', 'new_v_att_sinks': 'new_v', 'new_v_sg_ln_g': 'new_v', 'new_v_sg_ln_b': 'new_v', 'new_v_sg_w': 'new_v', 'new_v_sg_b': 'new_v', 'new_v_ssm_conv_w': 'new_v', 'new_v_ssm_conv_b': 'new_v', 'new_v_ssm_dt_bias': 'new_v', 'new_v_ssm_a_log': 'new_v', 'new_v_ssm_d': 'new_v', 'new_v_ssm_norm_g': 'new_v', 'new_v_w_br_att': 'new_v', 'new_v_w_br_sg': 'new_v', 'new_v_w_br_ssm': 'new_v', 'new_v_w_out': 'new_v'}


def _forward(args):
    return _fwd_reference(*[args[k] for k in FWD_PARAMS])


def _output_shape():
    def fwd():
        inp = _fwd_setup_inputs(0)
        return _fwd_reference(*[inp[k] for k in FWD_PARAMS])
    out = _jax.eval_shape(fwd)
    return out.shape, out.dtype

N_MICROBATCH = 1
ADAM_LR = 0.001
ADAM_B1 = 0.9
ADAM_B2 = 0.999
ADAM_EPS = 1e-08
ADAM_WD = 0.01
ADAM_STEP = 10
PER_EXAMPLE_BATCH_AXIS = {'x': 0, 'loss_target': 0}
SHARED_INPUTS = []
_WEIGHT_DTYPES = {'w_in': _jnp.float32, 'norm_pre': _jnp.float32, 'norm_post': _jnp.float32, 'rel_bias': _jnp.float32, 'att_sinks': _jnp.float32, 'sg_ln_g': _jnp.float32, 'sg_ln_b': _jnp.float32, 'sg_w': _jnp.float32, 'sg_b': _jnp.float32, 'ssm_conv_w': _jnp.float32, 'ssm_conv_b': _jnp.float32, 'ssm_dt_bias': _jnp.float32, 'ssm_a_log': _jnp.float32, 'ssm_d': _jnp.float32, 'ssm_norm_g': _jnp.float32, 'w_br_att': _jnp.float32, 'w_br_sg': _jnp.float32, 'w_br_ssm': _jnp.float32, 'w_out': _jnp.float32}
MOMENT_SCALE = {'w_in': 2.999765e-01, 'norm_pre': 1.047987e+00, 'norm_post': 6.403809e+01, 'rel_bias': 1.005571e-01, 'att_sinks': 5.967831e-02, 'sg_ln_g': 1.836620e-01, 'sg_ln_b': 1.878163e-01, 'sg_w': 1.838554e-01, 'sg_b': 2.786385e-01, 'ssm_conv_w': 5.363624e-01, 'ssm_conv_b': 1.829083e+00, 'ssm_dt_bias': 7.238163e-01, 'ssm_a_log': 2.958523e+00, 'ssm_d': 3.491272e+00, 'ssm_norm_g': 9.814821e-01, 'w_br_att': 7.789545e-02, 'w_br_sg': 4.319644e-01, 'w_br_ssm': 1.285437e+00, 'w_out': 1.366755e+00}


def _to_microbatches(a, axis):
    t = _jnp.moveaxis(a, axis, 0)
    t = t.reshape((N_MICROBATCH, t.shape[0] // N_MICROBATCH) + t.shape[1:])
    return _jnp.moveaxis(t, 1, axis + 1)


def setup_inputs(seed: int = 0) -> dict:
    inp = _fwd_setup_inputs(seed)
    key = _jax.random.fold_in(_jax.random.key(seed), 7919)
    shape, _ = _output_shape()
    out = dict(inp)
    out["loss_target"] = _jax.random.normal(_jax.random.fold_in(key, 0), shape, _jnp.float32)
    for i, name in enumerate(TWIN_WEIGHTS):
        w = inp[name].astype(_jnp.float32)
        if MOMENT_SCALE is None:
            s = _jnp.sqrt(_jnp.mean(_jnp.square(w)) + 1e-30)
        else:
            s = MOMENT_SCALE[name]
        km, kv = _jax.random.split(_jax.random.fold_in(key, i + 1))
        out[name] = w
        out["m_" + name] = s * _jax.random.normal(km, w.shape, _jnp.float32)
        out["v_" + name] = (s * s) * _jax.random.uniform(kv, w.shape, _jnp.float32, 0.5, 1.5)
    if N_MICROBATCH > 1:
        for name, axis in PER_EXAMPLE_BATCH_AXIS.items():
            out[name] = _to_microbatches(out[name], axis)
    return {'x': out['x'], 'w_in': out['w_in'], 'norm_pre': out['norm_pre'], 'norm_post': out['norm_post'], 'rel_bias': out['rel_bias'], 'att_sinks': out['att_sinks'], 'sg_ln_g': out['sg_ln_g'], 'sg_ln_b': out['sg_ln_b'], 'sg_w': out['sg_w'], 'sg_b': out['sg_b'], 'ssm_conv_w': out['ssm_conv_w'], 'ssm_conv_b': out['ssm_conv_b'], 'ssm_dt_bias': out['ssm_dt_bias'], 'ssm_a_log': out['ssm_a_log'], 'ssm_d': out['ssm_d'], 'ssm_norm_g': out['ssm_norm_g'], 'w_br_att': out['w_br_att'], 'w_br_sg': out['w_br_sg'], 'w_br_ssm': out['w_br_ssm'], 'w_out': out['w_out'], 'loss_target': out['loss_target'], 'm_w_in': out['m_w_in'], 'm_norm_pre': out['m_norm_pre'], 'm_norm_post': out['m_norm_post'], 'm_rel_bias': out['m_rel_bias'], 'm_att_sinks': out['m_att_sinks'], 'm_sg_ln_g': out['m_sg_ln_g'], 'm_sg_ln_b': out['m_sg_ln_b'], 'm_sg_w': out['m_sg_w'], 'm_sg_b': out['m_sg_b'], 'm_ssm_conv_w': out['m_ssm_conv_w'], 'm_ssm_conv_b': out['m_ssm_conv_b'], 'm_ssm_dt_bias': out['m_ssm_dt_bias'], 'm_ssm_a_log': out['m_ssm_a_log'], 'm_ssm_d': out['m_ssm_d'], 'm_ssm_norm_g': out['m_ssm_norm_g'], 'm_w_br_att': out['m_w_br_att'], 'm_w_br_sg': out['m_w_br_sg'], 'm_w_br_ssm': out['m_w_br_ssm'], 'm_w_out': out['m_w_out'], 'v_w_in': out['v_w_in'], 'v_norm_pre': out['v_norm_pre'], 'v_norm_post': out['v_norm_post'], 'v_rel_bias': out['v_rel_bias'], 'v_att_sinks': out['v_att_sinks'], 'v_sg_ln_g': out['v_sg_ln_g'], 'v_sg_ln_b': out['v_sg_ln_b'], 'v_sg_w': out['v_sg_w'], 'v_sg_b': out['v_sg_b'], 'v_ssm_conv_w': out['v_ssm_conv_w'], 'v_ssm_conv_b': out['v_ssm_conv_b'], 'v_ssm_dt_bias': out['v_ssm_dt_bias'], 'v_ssm_a_log': out['v_ssm_a_log'], 'v_ssm_d': out['v_ssm_d'], 'v_ssm_norm_g': out['v_ssm_norm_g'], 'v_w_br_att': out['v_w_br_att'], 'v_w_br_sg': out['v_w_br_sg'], 'v_w_br_ssm': out['v_w_br_ssm'], 'v_w_out': out['v_w_out']}


def _loss(weights, diff, rest, loss_target):
    with _jax.named_scope("forward"):
        args = {**rest, TWIN_DIFF_INPUT: diff, **{k: w.astype(_WEIGHT_DTYPES[k]) for k, w in weights.items()}}
        y = _forward(args)
    with _jax.named_scope("loss_head"):
        err = _jnp.square(y.astype(_jnp.float32) - loss_target)
        return 0.5 * _jnp.sum(_jnp.mean(err, axis=-1)) if err.ndim else 0.5 * err


def _adamw(w, g, m, v):
    m = ADAM_B1 * m + (1.0 - ADAM_B1) * g
    v = ADAM_B2 * v + (1.0 - ADAM_B2) * _jnp.square(g)
    m_hat = m / (1.0 - ADAM_B1 ** ADAM_STEP)
    v_hat = v / (1.0 - ADAM_B2 ** ADAM_STEP)
    delta = -ADAM_LR * (m_hat / (_jnp.sqrt(v_hat) + ADAM_EPS) + ADAM_WD * w)
    return delta, m, v


def reference(x, w_in, norm_pre, norm_post, rel_bias, att_sinks, sg_ln_g, sg_ln_b, sg_w, sg_b, ssm_conv_w, ssm_conv_b, ssm_dt_bias, ssm_a_log, ssm_d, ssm_norm_g, w_br_att, w_br_sg, w_br_ssm, w_out, loss_target, m_w_in, m_norm_pre, m_norm_post, m_rel_bias, m_att_sinks, m_sg_ln_g, m_sg_ln_b, m_sg_w, m_sg_b, m_ssm_conv_w, m_ssm_conv_b, m_ssm_dt_bias, m_ssm_a_log, m_ssm_d, m_ssm_norm_g, m_w_br_att, m_w_br_sg, m_w_br_ssm, m_w_out, v_w_in, v_norm_pre, v_norm_post, v_rel_bias, v_att_sinks, v_sg_ln_g, v_sg_ln_b, v_sg_w, v_sg_b, v_ssm_conv_w, v_ssm_conv_b, v_ssm_dt_bias, v_ssm_a_log, v_ssm_d, v_ssm_norm_g, v_w_br_att, v_w_br_sg, v_w_br_ssm, v_w_out):
    given = dict(x=x, w_in=w_in, norm_pre=norm_pre, norm_post=norm_post, rel_bias=rel_bias, att_sinks=att_sinks, sg_ln_g=sg_ln_g, sg_ln_b=sg_ln_b, sg_w=sg_w, sg_b=sg_b, ssm_conv_w=ssm_conv_w, ssm_conv_b=ssm_conv_b, ssm_dt_bias=ssm_dt_bias, ssm_a_log=ssm_a_log, ssm_d=ssm_d, ssm_norm_g=ssm_norm_g, w_br_att=w_br_att, w_br_sg=w_br_sg, w_br_ssm=w_br_ssm, w_out=w_out, loss_target=loss_target, m_w_in=m_w_in, m_norm_pre=m_norm_pre, m_norm_post=m_norm_post, m_rel_bias=m_rel_bias, m_att_sinks=m_att_sinks, m_sg_ln_g=m_sg_ln_g, m_sg_ln_b=m_sg_ln_b, m_sg_w=m_sg_w, m_sg_b=m_sg_b, m_ssm_conv_w=m_ssm_conv_w, m_ssm_conv_b=m_ssm_conv_b, m_ssm_dt_bias=m_ssm_dt_bias, m_ssm_a_log=m_ssm_a_log, m_ssm_d=m_ssm_d, m_ssm_norm_g=m_ssm_norm_g, m_w_br_att=m_w_br_att, m_w_br_sg=m_w_br_sg, m_w_br_ssm=m_w_br_ssm, m_w_out=m_w_out, v_w_in=v_w_in, v_norm_pre=v_norm_pre, v_norm_post=v_norm_post, v_rel_bias=v_rel_bias, v_att_sinks=v_att_sinks, v_sg_ln_g=v_sg_ln_g, v_sg_ln_b=v_sg_ln_b, v_sg_w=v_sg_w, v_sg_b=v_sg_b, v_ssm_conv_w=v_ssm_conv_w, v_ssm_conv_b=v_ssm_conv_b, v_ssm_dt_bias=v_ssm_dt_bias, v_ssm_a_log=v_ssm_a_log, v_ssm_d=v_ssm_d, v_ssm_norm_g=v_ssm_norm_g, v_w_br_att=v_w_br_att, v_w_br_sg=v_w_br_sg, v_w_br_ssm=v_w_br_ssm, v_w_out=v_w_out)
    weights = {n: given[n] for n in TWIN_WEIGHTS}
    shared = {n: given[n] for n in SHARED_INPUTS}
    per_example = {n: given[n] for n in ['x']}
    grad_fn = _jax.value_and_grad(_loss, argnums=(0, 1))

    def one_microbatch(ex, loss_target):
        ex = dict(ex)
        diff = ex.pop(TWIN_DIFF_INPUT)
        return grad_fn(weights, diff, {**shared, **ex}, loss_target)

    if N_MICROBATCH == 1:
        loss, (grad_w, grad_x) = one_microbatch(per_example, given["loss_target"])
    else:
        def body(carry, xs):
            loss_sum, grad_sum = carry
            l_k, (gw_k, gx_k) = one_microbatch(xs[0], xs[1])
            with _jax.named_scope("update"):
                return (loss_sum + l_k, _jax.tree.map(_jnp.add, grad_sum, gw_k)), gx_k

        init = (_jnp.zeros((), _jnp.float32), _jax.tree.map(_jnp.zeros_like, weights))
        (loss, grad_w), grad_x = _jax.lax.scan(body, init, (per_example, given["loss_target"]))
    with _jax.named_scope("update"):
        delta_w, new_m, new_v = {}, {}, {}
        for n in TWIN_WEIGHTS:
            delta_w[n], new_m[n], new_v[n] = _adamw(weights[n], grad_w[n], given["m_" + n], given["v_" + n])
    return (loss, grad_x, *[grad_w[n] for n in TWIN_WEIGHTS], *[delta_w[n] for n in TWIN_WEIGHTS],
            *[new_m[n] for n in TWIN_WEIGHTS], *[new_v[n] for n in TWIN_WEIGHTS])
```

```python
import functools
import math

import numpy as np
import jax
import jax.numpy as jnp
from jax import lax
from jax.experimental import pallas as pl
from jax.experimental.pallas import tpu as pltpu

F32 = jnp.float32
BF16 = jnp.bfloat16
MESH = pl.DeviceIdType.MESH

D = 1024
DEPTH = 2
EPS = 1e-6
L = 128
HEADS = 16
KV = 2
DH = 64
SSM_W = 2048
SSM_H = 32
SSM_P = 64
SSM_G = 4
SSM_N = 128
CONV_K = 4
CONV_C = 3072
NEG = -1e30
IN_COLS = 13600
NCP = 13824

O_XBC, O_GATE, O_ZM, O_Q, O_ZA, O_U, O_VS, O_ZS, O_K, O_V, O_DT = (
    0, 3072, 6144, 8192, 9216, 10240, 11264, 12288, 13312, 13440, 13568)

ADAM_LR = 0.001
ADAM_B1 = 0.9
ADAM_B2 = 0.999
ADAM_EPS = 1e-08
ADAM_WD = 0.01
ADAM_STEP = 10

VMEM_LIMIT = 56 * 1024 * 1024

PACK_ROWS = 4704
PACK_TILE = 224
SHARDS = 4


def _dot(a, b):
    return jnp.dot(a, b, preferred_element_type=F32)


def _dot_nt(a, b):
    return lax.dot_general(a, b, (((1,), (1,)), ((), ())), preferred_element_type=F32)


def _dot_tn(a_f32, b):
    return jnp.dot(a_f32.T.astype(BF16), b, preferred_element_type=F32)


def _dot_hi(a, b):
    return jnp.dot(a, b, preferred_element_type=F32, precision=lax.Precision.HIGHEST)


def _sigmoid(x):
    return 1.0 / (1.0 + jnp.exp(-x))


def _softplus(x):
    return jnp.maximum(x, 0.0) + jnp.log(1.0 + jnp.exp(-jnp.abs(x)))


def _params(sem=None, vmem=VMEM_LIMIT):
    kw = dict(vmem_limit_bytes=vmem)
    if sem is not None:
        kw["dimension_semantics"] = sem
    return pltpu.CompilerParams(**kw)


def _full(shape):
    nd = len(shape)
    return pl.BlockSpec(shape, lambda *_: (0,) * nd)


def to_padded_cols(w):
    pad = jnp.zeros(w.shape[:-1] + (NCP - O_DT - 32,), w.dtype)
    return jnp.concatenate([
        w[..., 7424:10496], w[..., 10528:13600], w[..., 5376:7424], w[..., 0:1024],
        w[..., 1280:2304], w[..., 2304:3328], w[..., 3328:4352], w[..., 4352:5376],
        w[..., 1024:1152], w[..., 1152:1280], w[..., 10496:10528], pad], axis=-1)


def from_padded_cols(g):
    return jnp.concatenate([
        g[..., O_Q:O_Q + 1024], g[..., O_K:O_K + 128], g[..., O_V:O_V + 128],
        g[..., O_ZA:O_ZA + 1024], g[..., O_U:O_U + 3072], g[..., O_ZM:O_ZM + 2048],
        g[..., O_XBC:O_XBC + 3072], g[..., O_DT:O_DT + 32], g[..., O_GATE:O_GATE + 3072]], axis=-1)


def _bucket_table():
    qi = np.arange(L)[:, None]
    kj = np.arange(2 * L)[None, :]
    dist = np.maximum(qi + L - kj, 0)
    dist_f = np.maximum(dist, 1).astype(np.float32)
    large = 16 + (np.log(dist_f / np.float32(16)) / np.float32(math.log(128 / 16)) * np.float32(16)).astype(np.int32)
    large = np.minimum(large, 31)
    return np.where(dist < 16, dist, large).astype(np.int32)


def bias_table(rel_bias):
    buckets = jnp.asarray(_bucket_table().reshape(1, L * 2 * L))

    def body(rb_ref, bk_ref, out_ref):
        onehot = (lax.broadcasted_iota(jnp.int32, (32, L * 2 * L), 0) == bk_ref[...]).astype(F32)
        out_ref[...] = lax.dot_general(rb_ref[...], onehot, (((0,), (0,)), ((), ())),
                                       preferred_element_type=F32, precision=lax.Precision.HIGHEST)

    out = pl.pallas_call(
        body, name="bias_table",
        out_shape=jax.ShapeDtypeStruct((HEADS, L * 2 * L), F32),
        compiler_params=_params(),
    )(rel_bias, buckets)
    return out.reshape(HEADS, L, 2 * L)


def bias_grad(dbias):
    buckets = jnp.asarray(_bucket_table().reshape(1, L * 2 * L))

    def body(db_ref, bk_ref, out_ref):
        onehot = (lax.broadcasted_iota(jnp.int32, (32, L * 2 * L), 0) == bk_ref[...]).astype(F32)
        out_ref[...] = lax.dot_general(onehot, db_ref[...], (((1,), (1,)), ((), ())),
                                       preferred_element_type=F32, precision=lax.Precision.HIGHEST)

    return pl.pallas_call(
        body, name="bias_grad",
        out_shape=jax.ShapeDtypeStruct((32, HEADS), F32),
        compiler_params=_params(),
    )(dbias.reshape(HEADS, L * 2 * L), buckets)


def inproj_fwd(x, g_pre, wp):
    S = x.shape[0]
    tm, tn = 512, 1536

    def body(x_ref, g_ref, w_ref, proj_ref, h_ref):
        @pl.when(pl.program_id(1) == 0)
        def _():
            xv = x_ref[...]
            r = lax.rsqrt(jnp.mean(xv * xv, axis=-1, keepdims=True) + EPS)
            h_ref[...] = (xv * r * g_ref[...]).astype(BF16)
        proj_ref[...] = _dot(h_ref[...], w_ref[...])

    return pl.pallas_call(
        body, name="inproj_fwd", grid=(S // tm, NCP // tn),
        in_specs=[pl.BlockSpec((tm, D), lambda i, j: (i, 0)), _full((1, D)),
                  pl.BlockSpec((D, tn), lambda i, j: (0, j))],
        out_specs=[pl.BlockSpec((tm, tn), lambda i, j: (i, j)), pl.BlockSpec((tm, D), lambda i, j: (i, 0))],
        out_shape=[jax.ShapeDtypeStruct((S, NCP), F32), jax.ShapeDtypeStruct((S, D), BF16)],
        compiler_params=_params(("arbitrary", "arbitrary")),
    )(x, g_pre, wp)


def inproj_bwd(dproj, wp, x, g_pre, dy):
    S = x.shape[0]
    tm, tk = 512, 1536
    nk = NCP // tk

    def body(dp_ref, w_ref, x_ref, g_ref, dy_ref, dx_ref, dg_ref, acc):
        i, k = pl.program_id(0), pl.program_id(1)

        @pl.when(k == 0)
        def _():
            acc[...] = jnp.zeros_like(acc)

        acc[...] += _dot_nt(dp_ref[...], w_ref[...])

        @pl.when((k == nk - 1) & (i == 0))
        def _():
            dg_ref[...] = jnp.zeros_like(dg_ref)

        @pl.when(k == nk - 1)
        def _():
            xv = x_ref[...]
            dh = acc[...]
            g = g_ref[...]
            r = lax.rsqrt(jnp.mean(xv * xv, axis=-1, keepdims=True) + EPS)
            dhg = dh * g
            dx_ref[...] = dy_ref[...] + r * dhg - xv * (r * r * r) * jnp.mean(dhg * xv, axis=-1, keepdims=True)
            dg_ref[...] += jnp.sum(dh * xv * r, axis=0, keepdims=True)

    return pl.pallas_call(
        body, name="inproj_bwd", grid=(S // tm, nk),
        in_specs=[pl.BlockSpec((tm, tk), lambda i, k: (i, k)), pl.BlockSpec((D, tk), lambda i, k: (0, k)),
                  pl.BlockSpec((tm, D), lambda i, k: (i, 0)), _full((1, D)),
                  pl.BlockSpec((tm, D), lambda i, k: (i, 0))],
        out_specs=[pl.BlockSpec((tm, D), lambda i, k: (i, 0)), _full((1, D))],
        out_shape=[jax.ShapeDtypeStruct((S, D), F32), jax.ShapeDtypeStruct((1, D), F32)],
        scratch_shapes=[pltpu.VMEM((tm, D), F32)],
        compiler_params=_params(("arbitrary", "arbitrary")),
    )(dproj, wp, x, g_pre, dy)


def matmul_tn(a, b, name, tn=512, ts=512):
    S, K = a.shape
    N = b.shape[1]
    ns = S // ts

    def body(a_ref, b_ref, o_ref):
        @pl.when(pl.program_id(1) == 0)
        def _():
            o_ref[...] = jnp.zeros_like(o_ref)
        o_ref[...] += _dot_tn(a_ref[...].astype(F32), b_ref[...])

    return pl.pallas_call(
        body, name=name, grid=(N // tn, ns),
        in_specs=[pl.BlockSpec((ts, K), lambda j, s: (s, 0)), pl.BlockSpec((ts, tn), lambda j, s: (s, j))],
        out_specs=pl.BlockSpec((K, tn), lambda j, s: (0, j)),
        out_shape=jax.ShapeDtypeStruct((K, N), F32),
        compiler_params=_params(("arbitrary", "arbitrary")),
    )(a, b)


def _att_mask(n):
    qi = lax.broadcasted_iota(jnp.int32, (L, 2 * L), 0)
    kj = lax.broadcasted_iota(jnp.int32, (L, 2 * L), 1)
    dist = qi + L - kj
    return (dist >= 0) & (dist < L) & ((kj >= L) | (n > 0))


def _att_in_specs(nb):
    last = nb - 1
    cur = lambda n: jnp.minimum(n, last)
    prev = lambda n: jnp.maximum(jnp.minimum(n, last) - 1, 0)
    return [
        pl.BlockSpec((L, 1024), lambda n: (cur(n), O_Q // 1024)),
        pl.BlockSpec((L, 128), lambda n: (prev(n), O_K // 128)),
        pl.BlockSpec((L, 128), lambda n: (cur(n), O_K // 128)),
        pl.BlockSpec((L, 128), lambda n: (prev(n), O_V // 128)),
        pl.BlockSpec((L, 128), lambda n: (cur(n), O_V // 128)),
        pl.BlockSpec((L, 1024), lambda n: (cur(n), O_ZA // 1024)),
        _full((HEADS, L, 2 * L)),
        pl.BlockSpec(memory_space=pltpu.SMEM),
    ]


def _att_probs(qh, kk, bias_h, mask, sk):
    logits = _dot_nt(qh, kk) + bias_h
    logits = jnp.where(mask, logits, NEG)
    m = jnp.maximum(jnp.max(logits, axis=-1, keepdims=True), sk)
    p = jnp.exp(logits - m)
    es = jnp.exp(sk - m)
    den = jnp.sum(p, axis=-1, keepdims=True) + es
    return p / den, es / den


def att_fwd(proj, bias, sinks):
    S = proj.shape[0]
    nb = S // L

    def body(q_ref, kp_ref, kc_ref, vp_ref, vc_ref, z_ref, bias_ref, s_ref, y_ref, o_scr):
        mask = _att_mask(pl.program_id(0))
        for kv in range(KV):
            sl = slice(kv * DH, (kv + 1) * DH)
            kk = jnp.concatenate([kp_ref[:, sl], kc_ref[:, sl]], axis=0).astype(BF16)
            vv = jnp.concatenate([vp_ref[:, sl], vc_ref[:, sl]], axis=0).astype(BF16)
            for g in range(HEADS // KV):
                h = kv * (HEADS // KV) + g
                hs = slice(h * DH, (h + 1) * DH)
                qh = (q_ref[:, hs] * 0.125).astype(BF16)
                P, _ = _att_probs(qh, kk, bias_ref[h], mask, s_ref[h])
                o_scr[:, hs] = _dot(P.astype(BF16), vv)
        z = z_ref[...]
        y_ref[...] = (o_scr[...] * (z * _sigmoid(z))).astype(BF16)

    return pl.pallas_call(
        body, name="att_fwd", grid=(nb,),
        in_specs=_att_in_specs(nb),
        out_specs=pl.BlockSpec((L, 1024), lambda n: (n, 0)),
        out_shape=jax.ShapeDtypeStruct((S, 1024), BF16),
        scratch_shapes=[pltpu.VMEM((L, 1024), F32)],
        compiler_params=_params(("arbitrary",)),
    )(proj, proj, proj, proj, proj, proj, bias, sinks)


def att_bwd(dy, proj, bias, sinks):
    S = proj.shape[0]
    nb = S // L
    last = nb - 1

    def body(dy_ref, q_ref, kp_ref, kc_ref, vp_ref, vc_ref, z_ref, bias_ref, s_ref,
             dq_ref, dz_ref, dk_ref, dv_ref, dbias_ref, dsink_ref, carry, band, dq_scr, dz_scr):
        n = pl.program_id(0)

        @pl.when(n == 0)
        def _():
            carry[...] = jnp.zeros_like(carry)
            dbias_ref[...] = jnp.zeros_like(dbias_ref)
            dsink_ref[...] = jnp.zeros_like(dsink_ref)

        band[...] = jnp.zeros_like(band)

        @pl.when(n < nb)
        def _():
            mask = _att_mask(n)
            lane = lax.broadcasted_iota(jnp.int32, (1, 128), 1)
            dsink = jnp.zeros((1, 128), F32)
            for kv in range(KV):
                sl = slice(kv * DH, (kv + 1) * DH)
                kk = jnp.concatenate([kp_ref[:, sl], kc_ref[:, sl]], axis=0).astype(BF16)
                vv = jnp.concatenate([vp_ref[:, sl], vc_ref[:, sl]], axis=0).astype(BF16)
                dk_acc = jnp.zeros((2 * L, DH), F32)
                dv_acc = jnp.zeros((2 * L, DH), F32)
                for g in range(HEADS // KV):
                    h = kv * (HEADS // KV) + g
                    hs = slice(h * DH, (h + 1) * DH)
                    qh = (q_ref[:, hs] * 0.125).astype(BF16)
                    P, psink = _att_probs(qh, kk, bias_ref[h], mask, s_ref[h])
                    Pb = P.astype(BF16)
                    zh = z_ref[:, hs]
                    sg = _sigmoid(zh)
                    dyh = dy_ref[:, hs]
                    O = _dot(Pb, vv)
                    dO = dyh * (zh * sg)
                    dz_scr[:, hs] = dyh * O * (sg * (1.0 + zh * (1.0 - sg)))
                    dOb = dO.astype(BF16)
                    dv_acc = dv_acc + _dot_tn(P, dOb)
                    dP = _dot_nt(dOb, vv)
                    delta = jnp.sum(P * dP, axis=-1, keepdims=True)
                    dS = P * (dP - delta)
                    dsink = dsink + jnp.where(lane == h, -jnp.sum(psink * delta), 0.0)
                    dSb = dS.astype(BF16)
                    dq_scr[:, hs] = _dot(dSb, kk) * 0.125
                    dk_acc = dk_acc + _dot_tn(dS, qh)
                    dbias_ref[h] += dS
                band[:, sl] = dk_acc
                band[:, 128 + kv * DH:128 + (kv + 1) * DH] = dv_acc
            dsink_ref[...] += dsink
            dq_ref[...] = dq_scr[...].astype(BF16)
            dz_ref[...] = dz_scr[...].astype(BF16)

        out = carry[...] + band[0:L, :]
        dk_ref[...] = out[:, 0:128].astype(BF16)
        dv_ref[...] = out[:, 128:256].astype(BF16)
        carry[...] = band[L:2 * L, :]

    cur = lambda n: jnp.minimum(n, last)
    lag = lambda n: jnp.maximum(n - 1, 0)
    return pl.pallas_call(
        body, name="att_bwd", grid=(nb + 1,),
        in_specs=[pl.BlockSpec((L, 1024), lambda n: (cur(n), 0))] + _att_in_specs(nb),
        out_specs=[pl.BlockSpec((L, 1024), lambda n: (cur(n), 0)), pl.BlockSpec((L, 1024), lambda n: (cur(n), 0)),
                   pl.BlockSpec((L, 128), lambda n: (lag(n), 0)), pl.BlockSpec((L, 128), lambda n: (lag(n), 0)),
                   _full((HEADS, L, 2 * L)), _full((1, 128))],
        out_shape=[jax.ShapeDtypeStruct((S, 1024), BF16), jax.ShapeDtypeStruct((S, 1024), BF16),
                   jax.ShapeDtypeStruct((S, 128), BF16), jax.ShapeDtypeStruct((S, 128), BF16),
                   jax.ShapeDtypeStruct((HEADS, L, 2 * L), F32), jax.ShapeDtypeStruct((1, 128), F32)],
        scratch_shapes=[pltpu.VMEM((L, 256), F32), pltpu.VMEM((2 * L, 256), F32),
                        pltpu.VMEM((L, 1024), F32), pltpu.VMEM((L, 1024), F32)],
        compiler_params=_params(("arbitrary",)),
    )(dy, proj, proj, proj, proj, proj, proj, bias, sinks)


def _sgu_in_specs():
    return [
        pl.BlockSpec((L, 1024), lambda c: (c, O_U // 1024)),
        pl.BlockSpec((L, 1024), lambda c: (c, O_VS // 1024)),
        pl.BlockSpec((L, 1024), lambda c: (c, O_ZS // 1024)),
        _full((1, 1024)), _full((1, 1024)), _full((8, L, L)), _full((L, 8)),
    ]


def _sgu_norm(v, lg, lb):
    mu = jnp.mean(v, axis=-1, keepdims=True)
    vc = v - mu
    rstd = lax.rsqrt(jnp.mean(vc * vc, axis=-1, keepdims=True) + EPS)
    xhat = vc * rstd
    return xhat * lg + lb, xhat, rstd


def _tril():
    return lax.broadcasted_iota(jnp.int32, (L, L), 0) >= lax.broadcasted_iota(jnp.int32, (L, L), 1)


def sgu_fwd(proj, ln_g, ln_b, w, b_t):
    S = proj.shape[0]

    def body(u_ref, v_ref, z_ref, lg_ref, lb_ref, w_ref, bt_ref, y_ref):
        vn, _, _ = _sgu_norm(v_ref[...], lg_ref[...], lb_ref[...])
        tri = _tril()
        parts = []
        for g in range(8):
            wg = jnp.where(tri, w_ref[g], 0.0).astype(BF16)
            parts.append(_dot(wg, vn[:, g * 128:(g + 1) * 128].astype(BF16)) + bt_ref[:, g:g + 1])
        mixed = jnp.concatenate(parts, axis=1)
        z = z_ref[...]
        y_ref[...] = (u_ref[...] * mixed * (z * _sigmoid(z))).astype(BF16)

    return pl.pallas_call(
        body, name="sgu_fwd", grid=(S // L,),
        in_specs=_sgu_in_specs(),
        out_specs=pl.BlockSpec((L, 1024), lambda c: (c, 0)),
        out_shape=jax.ShapeDtypeStruct((S, 1024), BF16),
        compiler_params=_params(("arbitrary",)),
    )(proj, proj, proj, ln_g, ln_b, w, b_t)


def sgu_bwd(dy, proj, ln_g, ln_b, w, b_t):
    S = proj.shape[0]

    def body(dy_ref, u_ref, v_ref, z_ref, lg_ref, lb_ref, w_ref, bt_ref,
             dout_ref, dw_ref, dbt_ref, dlg_ref, dlb_ref):
        @pl.when(pl.program_id(0) == 0)
        def _():
            dw_ref[...] = jnp.zeros_like(dw_ref)
            dbt_ref[...] = jnp.zeros_like(dbt_ref)
            dlg_ref[...] = jnp.zeros_like(dlg_ref)
            dlb_ref[...] = jnp.zeros_like(dlb_ref)

        lg = lg_ref[...]
        vn, xhat, rstd = _sgu_norm(v_ref[...], lg, lb_ref[...])
        tri = _tril()
        lane = lax.broadcasted_iota(jnp.int32, (L, 128), 1)
        wgs, parts = [], []
        for g in range(8):
            wg = jnp.where(tri, w_ref[g], 0.0)
            wgs.append(wg)
            parts.append(_dot(wg.astype(BF16), vn[:, g * 128:(g + 1) * 128].astype(BF16)) + bt_ref[:, g:g + 1])
        mixed = jnp.concatenate(parts, axis=1)
        z = z_ref[...]
        sg = _sigmoid(z)
        silu = z * sg
        dy_v = dy_ref[...]
        u = u_ref[...]
        dout_ref[:, 0:1024] = (dy_v * mixed * silu).astype(BF16)
        dout_ref[:, 2048:3072] = (dy_v * u * mixed * (sg * (1.0 + z * (1.0 - sg)))).astype(BF16)
        dmixed = dy_v * u * silu
        dbt = jnp.zeros((L, 128), F32)
        dvn_parts = []
        for g in range(8):
            dm = dmixed[:, g * 128:(g + 1) * 128]
            dmb = dm.astype(BF16)
            dbt = dbt + jnp.where(lane == g, jnp.sum(dm, axis=1, keepdims=True), 0.0)
            dw_ref[g] += jnp.where(tri, _dot_nt(dmb, vn[:, g * 128:(g + 1) * 128].astype(BF16)), 0.0)
            dvn_parts.append(_dot_tn(wgs[g], dmb))
        dbt_ref[...] += dbt
        dvn = jnp.concatenate(dvn_parts, axis=1)
        dlg_ref[...] += jnp.sum(dvn * xhat, axis=0, keepdims=True)
        dlb_ref[...] += jnp.sum(dvn, axis=0, keepdims=True)
        dxh = dvn * lg
        dv = rstd * (dxh - jnp.mean(dxh, axis=-1, keepdims=True)
                     - xhat * jnp.mean(dxh * xhat, axis=-1, keepdims=True))
        dout_ref[:, 1024:2048] = dv.astype(BF16)

    return pl.pallas_call(
        body, name="sgu_bwd", grid=(S // L,),
        in_specs=[pl.BlockSpec((L, 1024), lambda c: (c, 0))] + _sgu_in_specs(),
        out_specs=[pl.BlockSpec((L, 3072), lambda c: (c, 0)), _full((8, L, L)), _full((L, 128)),
                   _full((1, 1024)), _full((1, 1024))],
        out_shape=[jax.ShapeDtypeStruct((S, 3072), BF16), jax.ShapeDtypeStruct((8, L, L), F32),
                   jax.ShapeDtypeStruct((L, 128), F32), jax.ShapeDtypeStruct((1, 1024), F32),
                   jax.ShapeDtypeStruct((1, 1024), F32)],
        compiler_params=_params(("arbitrary",)),
    )(dy, proj, proj, proj, ln_g, ln_b, w, b_t)


def _expand_matrix():
    r = lax.broadcasted_iota(jnp.int32, (128, SSM_W), 0)
    c = lax.broadcasted_iota(jnp.int32, (128, SSM_W), 1)
    return ((c // SSM_P) == r).astype(F32)


def _ssd_common(ext_ref, cw_ref, cb_ref, dt_raw, dtb, alog):
    pre = cb_ref[...]
    for k in range(CONV_K):
        pre = pre + cw_ref[k:k + 1, :] * ext_ref[pl.ds(5 + k, L), :]
    sg_pre = _sigmoid(pre)
    xc = pre * sg_pre
    dt = _softplus(dt_raw + dtb)
    a = -jnp.exp(alog)
    adt = dt * a
    acs = _dot_hi(_tril().astype(F32), adt)
    return pre, sg_pre, xc, dt, a, acs


def _ssd_in_specs(rev, nc):
    cidx = (lambda c: nc - 1 - c) if rev else (lambda c: c)
    return [
        pl.BlockSpec((L, 2048), lambda c: (cidx(c), O_ZM // 2048)),
        pl.BlockSpec((L, 3072), lambda c: (cidx(c), O_XBC // 3072)),
        pl.BlockSpec((L, 128), lambda c: (cidx(c), O_DT // 128)),
        _full((8, CONV_C)), _full((1, CONV_C)), _full((1, 128)), _full((1, 128)), _full((1, 128)),
        _full((1, SSM_W)),
    ]


def ssd_fwd(proj, conv_w, conv_b, dt_bias, a_log, d_skip, norm_g):
    S = proj.shape[0]
    nc = S // L

    def body(z_ref, xbc_ref, dt_ref, cw_ref, cb_ref, dtb_ref, alog_ref, dsk_ref, ng_ref,
             y_ref, hs_ref, H, ext, ysc):
        @pl.when(pl.program_id(0) == 0)
        def _():
            H[...] = jnp.zeros_like(H)
            ext[0:8, :] = jnp.zeros((8, CONV_C), F32)

        ext[8:8 + L, :] = xbc_ref[...]
        pre, sg_pre, xc, dt, a, acs = _ssd_common(ext, cw_ref, cb_ref, dt_ref[...], dtb_ref[...], alog_ref[...])
        ext[0:8, :] = xbc_ref[L - 8:L, :]
        xs = xc[:, 0:SSM_W]
        acs_t = acs.T
        ex = _expand_matrix()
        acs_x = _dot_hi(acs, ex)
        dt_x = _dot_hi(dt, ex)
        xdt = xs * dt_x
        eacs_x = jnp.exp(acs_x)
        xw = xdt * jnp.exp(acs_x[L - 1:L, :] - acs_x)
        cd_row = jnp.exp(acs[L - 1:L, :])
        hs_ref[0] = H[...]
        tri = _tril()
        for g in range(SSM_G):
            gs = slice(g * 512, (g + 1) * 512)
            bg = xc[:, SSM_W + g * SSM_N:SSM_W + (g + 1) * SSM_N].astype(BF16)
            cg = xc[:, SSM_W + 512 + g * SSM_N:SSM_W + 512 + (g + 1) * SSM_N].astype(BF16)
            G = _dot_nt(cg, bg)
            yoff = _dot_nt(cg, H[gs, :].astype(BF16)) * eacs_x[:, gs]
            Sg = _dot_tn(xw[:, gs], bg)
            for j in range(8):
                hh = g * 8 + j
                hs = slice(hh * SSM_P, (hh + 1) * SSM_P)
                seg = acs[:, hh:hh + 1] - acs_t[hh:hh + 1, :]
                dk = jnp.where(tri, jnp.exp(jnp.minimum(seg, 0.0)), 0.0)
                yd = _dot((G * dk).astype(BF16), xdt[:, hs].astype(BF16))
                ysc[:, hs] = yd + yoff[:, j * SSM_P:(j + 1) * SSM_P]
                H[hs, :] = H[hs, :] * cd_row[:, hh:hh + 1] + Sg[j * SSM_P:(j + 1) * SSM_P, :]
        d_x = _dot_hi(jnp.broadcast_to(dsk_ref[...], (8, 128)), ex)[0:1, :]
        Y = ysc[...] + d_x * xs
        z = z_ref[...]
        yz = Y * (z * _sigmoid(z))
        ng = ng_ref[...]
        for g in range(SSM_G):
            gs = slice(g * 512, (g + 1) * 512)
            t = yz[:, gs]
            rstd = lax.rsqrt(jnp.mean(t * t, axis=-1, keepdims=True) + EPS)
            y_ref[:, gs] = (t * rstd * ng[:, gs]).astype(BF16)

    return pl.pallas_call(
        body, name="ssd_fwd", grid=(nc,),
        in_specs=_ssd_in_specs(False, nc),
        out_specs=[pl.BlockSpec((L, SSM_W), lambda c: (c, 0)), pl.BlockSpec((1, SSM_W, SSM_N), lambda c: (c, 0, 0))],
        out_shape=[jax.ShapeDtypeStruct((S, SSM_W), BF16), jax.ShapeDtypeStruct((nc, SSM_W, SSM_N), F32)],
        scratch_shapes=[pltpu.VMEM((SSM_W, SSM_N), F32), pltpu.VMEM((8 + L, CONV_C), F32),
                        pltpu.VMEM((L, SSM_W), F32)],
        compiler_params=_params(("arbitrary",)),
    )(proj, proj, proj, conv_w, conv_b, dt_bias, a_log, d_skip, norm_g)


def ssd_bwd(dy, proj, hstates, conv_w, conv_b, dt_bias, a_log, d_skip, norm_g):
    S = proj.shape[0]
    nc = S // L
    cidx = lambda c: nc - 1 - c

    def body(dy_ref, z_ref, xbc_ref, dt_ref, cw_ref, cb_ref, dtb_ref, alog_ref, dsk_ref, ng_ref,
             xprev_ref, hp_ref,
             dz_ref, dxbc_ref, ddt_ref, dcw_ref, dcb_ref, ddtb_ref, dalog_ref, ddsk_ref, dng_ref,
             dH, ext, dext, ysc, yoffsc, dxdt, dxc, tsc):
        step = pl.program_id(0)
        c = nc - 1 - step

        @pl.when(step == 0)
        def _():
            dH[...] = jnp.zeros_like(dH)
            dext[L:L + 8, :] = jnp.zeros((8, CONV_C), F32)
            for r in (dcw_ref, dcb_ref, ddtb_ref, dalog_ref, ddsk_ref, dng_ref):
                r[...] = jnp.zeros_like(r)

        ext[0:8, :] = jnp.where(c > 0, xprev_ref[L - 8:L, :], 0.0)
        ext[8:8 + L, :] = xbc_ref[...]
        dtb = dtb_ref[...]
        dt_raw = dt_ref[...]
        pre, sg_pre, xc, dt, a, acs = _ssd_common(ext, cw_ref, cb_ref, dt_raw, dtb, alog_ref[...])
        xs = xc[:, 0:SSM_W]
        acs_t = acs.T
        ex = _expand_matrix()
        acs_x = _dot_hi(acs, ex)
        dt_x = _dot_hi(dt, ex)
        xdt = xs * dt_x
        eacs_x = jnp.exp(acs_x)
        dte_x = jnp.exp(acs_x[L - 1:L, :] - acs_x)
        xw = xdt * dte_x
        cd_row = jnp.exp(acs[L - 1:L, :])
        tri = _tril()

        Gs, Cs, Bs = [], [], []
        for g in range(SSM_G):
            gs = slice(g * 512, (g + 1) * 512)
            bg = xc[:, SSM_W + g * SSM_N:SSM_W + (g + 1) * SSM_N].astype(BF16)
            cg = xc[:, SSM_W + 512 + g * SSM_N:SSM_W + 512 + (g + 1) * SSM_N].astype(BF16)
            G = _dot_nt(cg, bg)
            Gs.append(G), Cs.append(cg), Bs.append(bg)
            yoffsc[:, gs] = _dot_nt(cg, hp_ref[0, gs, :].astype(BF16)) * eacs_x[:, gs]
            for j in range(8):
                hh = g * 8 + j
                hs = slice(hh * SSM_P, (hh + 1) * SSM_P)
                seg = acs[:, hh:hh + 1] - acs_t[hh:hh + 1, :]
                dk = jnp.where(tri, jnp.exp(jnp.minimum(seg, 0.0)), 0.0)
                ysc[:, hs] = _dot((G * dk).astype(BF16), xdt[:, hs].astype(BF16))
        d_x = _dot_hi(jnp.broadcast_to(dsk_ref[...], (8, 128)), ex)[0:1, :]
        yoff = yoffsc[...]
        Y = ysc[...] + yoff + d_x * xs

        z = z_ref[...]
        sgz = _sigmoid(z)
        silu_z = z * sgz
        yz = Y * silu_z
        ng = ng_ref[...]
        dout = dy_ref[...]
        dyn = dout * ng
        dyz_parts, dng_parts = [], []
        for g in range(SSM_G):
            gs = slice(g * 512, (g + 1) * 512)
            t = yz[:, gs]
            rstd = lax.rsqrt(jnp.mean(t * t, axis=-1, keepdims=True) + EPS)
            dng_parts.append(jnp.sum(dout[:, gs] * t * rstd, axis=0, keepdims=True))
            dn = dyn[:, gs]
            dyz_parts.append(rstd * dn - t * (rstd * rstd * rstd) * jnp.mean(dn * t, axis=-1, keepdims=True))
        dng_ref[...] += jnp.concatenate(dng_parts, axis=1)
        dyz = jnp.concatenate(dyz_parts, axis=1)
        dY = dyz * silu_z
        dz_ref[...] = (dyz * Y * (sgz * (1.0 + z * (1.0 - sgz)))).astype(BF16)

        ex_t = ex.T
        ddsk_ref[...] += _dot_hi(jnp.broadcast_to(jnp.sum(dY * xs, axis=0, keepdims=True), (8, SSM_W)), ex_t)[0:1, :]

        lane = lax.broadcasted_iota(jnp.int32, (L, 128), 1)
        subl = lax.broadcasted_iota(jnp.int32, (128, L), 0)
        coll = lax.broadcasted_iota(jnp.int32, (128, L), 1)
        r_cols = jnp.zeros((L, 128), F32)
        c_rows = jnp.zeros((128, L), F32)
        for g in range(SSM_G):
            gs = slice(g * 512, (g + 1) * 512)
            G, cg, bg = Gs[g], Cs[g], Bs[g]
            hp_g = hp_ref[0, gs, :]
            dh_g = dH[gs, :]
            dY_g = dY[:, gs]
            dZ = dY_g * eacs_x[:, gs]
            dZb = dZ.astype(BF16)
            dC = _dot(dZb, hp_g.astype(BF16))
            dh_from_off = _dot_tn(dZ, cg)
            dhb = dh_g.astype(BF16)
            Q = _dot_nt(bg, dhb)
            dB = _dot(xw[:, gs].astype(BF16), dhb)
            qd = Q * dte_x[:, gs]
            dxdt[:, gs] = qd
            tsc[:, gs] = qd * xdt[:, gs]
            dG = jnp.zeros((L, L), F32)
            for j in range(8):
                hh = g * 8 + j
                hs = slice(hh * SSM_P, (hh + 1) * SSM_P)
                seg = acs[:, hh:hh + 1] - acs_t[hh:hh + 1, :]
                dk = jnp.where(tri, jnp.exp(jnp.minimum(seg, 0.0)), 0.0)
                M = G * dk
                dYh = dY[:, hs]
                dYhb = dYh.astype(BF16)
                dM = _dot_nt(dYhb, xdt[:, hs].astype(BF16))
                dxdt[:, hs] += _dot_tn(M, dYhb)
                dG = dG + dM * dk
                Wm = dM * M
                r_cols = r_cols + jnp.where(lane == hh, jnp.sum(Wm, axis=1, keepdims=True), 0.0)
                c_rows = c_rows + jnp.where(subl == hh, jnp.sum(Wm, axis=0, keepdims=True), 0.0)
                pj = slice(j * SSM_P, (j + 1) * SSM_P)
                cd_h = cd_row[:, hh:hh + 1]
                dcd = jnp.sum(dh_g[pj, :] * hp_g[pj, :]) * cd_h
                c_rows = c_rows - jnp.where((subl == hh) & (coll == L - 1), dcd, 0.0)
                dH[hs, :] = dh_g[pj, :] * cd_h + dh_from_off[pj, :]
            dGb = dG.astype(BF16)
            dC = dC + _dot(dGb, bg)
            dB = dB + _dot_tn(dG, cg)
            dxc[:, SSM_W + g * SSM_N:SSM_W + (g + 1) * SSM_N] = dB
            dxc[:, SSM_W + 512 + g * SSM_N:SSM_W + 512 + (g + 1) * SSM_N] = dC

        row = lax.broadcasted_iota(jnp.int32, (L, 128), 0)
        th = _dot_hi(tsc[...], ex_t)
        dacs = (r_cols - c_rows.T + _dot_hi(dY * yoff, ex_t) - th
                + jnp.where(row == L - 1, jnp.sum(th, axis=0, keepdims=True), 0.0))
        triu = (lax.broadcasted_iota(jnp.int32, (L, L), 0) <= lax.broadcasted_iota(jnp.int32, (L, L), 1)).astype(F32)
        dadt = _dot_hi(triu, dacs)
        dxdt_v = dxdt[...]
        ddt = _dot_hi(dxdt_v * xs, ex_t) + dadt * a
        dalog_ref[...] += jnp.sum(dadt * dt * a, axis=0, keepdims=True)
        ddt_raw = jnp.where(lane < SSM_H, ddt * _sigmoid(dt_raw + dtb), 0.0)
        ddtb_ref[...] += jnp.sum(ddt_raw, axis=0, keepdims=True)
        ddt_ref[...] = ddt_raw.astype(BF16)

        dxc[:, 0:SSM_W] = dxdt_v * dt_x + d_x * dY
        dpre = dxc[...] * (sg_pre * (1.0 + pre * (1.0 - sg_pre)))
        dcb_ref[...] += jnp.sum(dpre, axis=0, keepdims=True)
        for k in range(CONV_K):
            dcw_ref[k:k + 1, :] += jnp.sum(dpre * ext[pl.ds(5 + k, L), :], axis=0, keepdims=True)
        dext[0:L, :] = dpre
        dx = cw_ref[0:1, :] * dext[pl.ds(3, L), :]
        for k in range(1, CONV_K):
            dx = dx + cw_ref[k:k + 1, :] * dext[pl.ds(3 - k, L), :]
        dxbc_ref[...] = dx.astype(BF16)
        dext[L:L + 8, :] = dpre[0:8, :]

    big = lambda w: pl.BlockSpec((L, w), lambda c: (cidx(c), 0))
    return pl.pallas_call(
        body, name="ssd_bwd", grid=(nc,),
        in_specs=[big(SSM_W)] + _ssd_in_specs(True, nc) + [
            pl.BlockSpec((L, 3072), lambda c: (jnp.maximum(cidx(c) - 1, 0), O_XBC // 3072)),
            pl.BlockSpec((1, SSM_W, SSM_N), lambda c: (cidx(c), 0, 0))],
        out_specs=[big(SSM_W), big(CONV_C), big(128), _full((8, CONV_C)), _full((1, CONV_C)),
                   _full((1, 128)), _full((1, 128)), _full((1, 128)), _full((1, SSM_W))],
        out_shape=[jax.ShapeDtypeStruct((S, SSM_W), BF16), jax.ShapeDtypeStruct((S, CONV_C), BF16),
                   jax.ShapeDtypeStruct((S, 128), BF16), jax.ShapeDtypeStruct((8, CONV_C), F32),
                   jax.ShapeDtypeStruct((1, CONV_C), F32), jax.ShapeDtypeStruct((1, 128), F32),
                   jax.ShapeDtypeStruct((1, 128), F32), jax.ShapeDtypeStruct((1, 128), F32),
                   jax.ShapeDtypeStruct((1, SSM_W), F32)],
        scratch_shapes=[pltpu.VMEM((SSM_W, SSM_N), F32), pltpu.VMEM((8 + L, CONV_C), F32),
                        pltpu.VMEM((L + 8, CONV_C), F32), pltpu.VMEM((L, SSM_W), F32),
                        pltpu.VMEM((L, SSM_W), F32), pltpu.VMEM((L, SSM_W), F32),
                        pltpu.VMEM((L, CONV_C), F32), pltpu.VMEM((L, SSM_W), F32)],
        compiler_params=_params(("arbitrary",)),
    )(dy, proj, proj, proj, conv_w, conv_b, dt_bias, a_log, d_skip, norm_g, proj, hstates)


def _resident(shape):
    nd = len(shape)
    return pl.BlockSpec(shape, lambda *_: (0,) * nd, pipeline_mode=pl.Buffered(1))


def merge_fwd(y_att, y_sg, y_ssm, proj, x, w_a, w_s, w_m, w_o, g_post):
    S = x.shape[0]
    tm = 256

    def body(ya_ref, ys_ref, ym_ref, gate_ref, x_ref, wa_ref, ws_ref, wm_ref, wo_ref, gp_ref,
             xn_ref, bra_ref, brs_ref, brm_ref, mg_ref, out_ref):
        bra = _dot(ya_ref[...], wa_ref[...])
        brs = _dot(ys_ref[...], ws_ref[...])
        brm = _dot(ym_ref[...], wm_ref[...])
        bra_ref[...] = bra
        brs_ref[...] = brs
        brm_ref[...] = brm
        merged = (_sigmoid(gate_ref[:, 0:1024]) * bra + _sigmoid(gate_ref[:, 1024:2048]) * brs
                  + _sigmoid(gate_ref[:, 2048:3072]) * brm)
        mb = merged.astype(BF16)
        mg_ref[...] = mb
        o = _dot(mb, wo_ref[...])
        out_ref[...] = o
        r = lax.rsqrt(jnp.mean(o * o, axis=-1, keepdims=True) + EPS)
        xn_ref[...] = x_ref[...] + o * r * gp_ref[...]

    row = lambda w: pl.BlockSpec((tm, w), lambda i: (i, 0))
    return pl.pallas_call(
        body, name="merge_fwd", grid=(S // tm,),
        in_specs=[row(1024), row(1024), row(2048), pl.BlockSpec((tm, 3072), lambda i: (i, O_GATE // 3072)),
                  row(D), _resident((1024, D)), _resident((1024, D)), _resident((2048, D)), _resident((D, D)),
                  _full((1, D))],
        out_specs=[row(D)] * 6,
        out_shape=[jax.ShapeDtypeStruct((S, D), F32)] * 4 + [jax.ShapeDtypeStruct((S, D), BF16),
                                                             jax.ShapeDtypeStruct((S, D), F32)],
        compiler_params=_params(("arbitrary",)),
    )(y_att, y_sg, y_ssm, proj, x, w_a, w_s, w_m, w_o, g_post)


def merge_bwd(dy, out, g_post, proj, br_a, br_s, br_m, w_a, w_s, w_m, w_o):
    S = dy.shape[0]
    tm = 256

    def body(dy_ref, o_ref, gp_ref, gate_ref, bra_ref, brs_ref, brm_ref, wa_ref, ws_ref, wm_ref, wo_ref,
             dout_ref, dba_ref, dbs_ref, dbm_ref, dgate_ref, dya_ref, dys_ref, dym_ref, dgp_ref):
        @pl.when(pl.program_id(0) == 0)
        def _():
            dgp_ref[...] = jnp.zeros_like(dgp_ref)

        o = o_ref[...]
        dyv = dy_ref[...]
        r = lax.rsqrt(jnp.mean(o * o, axis=-1, keepdims=True) + EPS)
        dyg = dyv * gp_ref[...]
        do = r * dyg - o * (r * r * r) * jnp.mean(dyg * o, axis=-1, keepdims=True)
        dgp_ref[...] += jnp.sum(dyv * o * r, axis=0, keepdims=True)
        dob = do.astype(BF16)
        dout_ref[...] = dob
        dmerged = _dot_nt(dob, wo_ref[...])
        for idx, (br_ref, dbr_ref, w_ref, dyi_ref) in enumerate((
                (bra_ref, dba_ref, wa_ref, dya_ref), (brs_ref, dbs_ref, ws_ref, dys_ref),
                (brm_ref, dbm_ref, wm_ref, dym_ref))):
            s = _sigmoid(gate_ref[:, idx * 1024:(idx + 1) * 1024])
            dbr = (dmerged * s).astype(BF16)
            dbr_ref[...] = dbr
            dgate_ref[:, idx * 1024:(idx + 1) * 1024] = (dmerged * br_ref[...] * s * (1.0 - s)).astype(BF16)
            dyi_ref[...] = _dot_nt(dbr, w_ref[...])

    row = lambda w: pl.BlockSpec((tm, w), lambda i: (i, 0))
    return pl.pallas_call(
        body, name="merge_bwd", grid=(S // tm,),
        in_specs=[row(D), row(D), _full((1, D)), pl.BlockSpec((tm, 3072), lambda i: (i, O_GATE // 3072)),
                  row(D), row(D), row(D),
                  _resident((1024, D)), _resident((1024, D)), _resident((2048, D)), _resident((D, D))],
        out_specs=[row(D), row(D), row(D), row(D), row(3072), row(1024), row(1024), row(2048), _full((1, D))],
        out_shape=[jax.ShapeDtypeStruct((S, D), BF16)] * 4 + [
            jax.ShapeDtypeStruct((S, 3072), BF16), jax.ShapeDtypeStruct((S, 1024), F32),
            jax.ShapeDtypeStruct((S, 1024), F32), jax.ShapeDtypeStruct((S, 2048), F32),
            jax.ShapeDtypeStruct((1, D), F32)],
        compiler_params=_params(("arbitrary",)),
    )(dy, out, g_post, proj, br_a, br_s, br_m, w_a, w_s, w_m, w_o)


def loss_head(y, target):
    S = y.shape[0]
    tm = 512

    def body(y_ref, t_ref, dy_ref, loss_ref):
        @pl.when(pl.program_id(0) == 0)
        def _():
            loss_ref[...] = jnp.zeros_like(loss_ref)
        e = y_ref[...] - t_ref[...]
        dy_ref[...] = e * (1.0 / D)
        loss_ref[...] += 0.5 * jnp.sum(jnp.mean(e * e, axis=-1, keepdims=True))

    row = pl.BlockSpec((tm, D), lambda i: (i, 0))
    return pl.pallas_call(
        body, name="loss_head", grid=(S // tm,),
        in_specs=[row, row], out_specs=[row, _full((1, 128))],
        out_shape=[jax.ShapeDtypeStruct((S, D), F32), jax.ShapeDtypeStruct((1, 128), F32)],
        compiler_params=_params(("arbitrary",)),
    )(y, target)


def adamw(w, g, m, v, name):
    R, C = w.shape
    tr = R
    for cand in (512, 256, 128, 64, 32, 16, 8):
        if R % cand == 0 and cand * C * 4 <= (2 << 20):
            tr = cand
            break

    def body(w_ref, g_ref, m_ref, v_ref, d_ref, nm_ref, nv_ref):
        gv = g_ref[...]
        mn = ADAM_B1 * m_ref[...] + (1.0 - ADAM_B1) * gv
        vn = ADAM_B2 * v_ref[...] + (1.0 - ADAM_B2) * (gv * gv)
        m_hat = mn / (1.0 - ADAM_B1 ** ADAM_STEP)
        v_hat = vn / (1.0 - ADAM_B2 ** ADAM_STEP)
        d_ref[...] = -ADAM_LR * (m_hat / (jnp.sqrt(v_hat) + ADAM_EPS) + ADAM_WD * w_ref[...])
        nm_ref[...] = mn
        nv_ref[...] = vn

    blk = pl.BlockSpec((tr, C), lambda i: (i, 0))
    return pl.pallas_call(
        body, name=name, grid=(R // tr,),
        in_specs=[blk] * 4, out_specs=[blk] * 3,
        out_shape=[jax.ShapeDtypeStruct((R, C), F32)] * 3,
        compiler_params=_params(("arbitrary",)),
    )(w, g, m, v)


ANY = pl.BlockSpec(memory_space=pl.ANY)


def _place():
    x, y, c = lax.axis_index("x"), lax.axis_index("y"), lax.axis_index("c")
    others = [(1 - x, y), (x, 1 - y), (1 - x, 1 - y)]
    return x, y, c, others


def _rcopy(src, dst, ssem, rsem, to):
    return pltpu.make_async_remote_copy(src_ref=src, dst_ref=dst, send_sem=ssem, recv_sem=rsem,
                                        device_id=to, device_id_type=MESH)


def gather_weights(wpk):
    _, R, C = wpk.shape

    def body(w_ref, all_ref, ssem, rsem, lsem):
        x, y, c, others = _place()
        me = 2 * x + y
        sib = (x, y, 1 - c)
        loc = [pltpu.make_async_copy(w_ref.at[l], all_ref.at[l, me], lsem.at[l]) for l in range(2)]
        for cp in loc:
            cp.start()
        first = [_rcopy(w_ref.at[c], all_ref.at[c, me], ssem.at[k], rsem.at[k], (ox, oy, c))
                 for k, (ox, oy) in enumerate(others)]
        for cp in first:
            cp.start()
        passed = []
        for k, (ox, oy) in enumerate(others):
            slot = all_ref.at[c, 2 * ox + oy]
            _rcopy(slot, slot, ssem.at[k], rsem.at[k], sib).wait_recv()
            fw = _rcopy(slot, slot, ssem.at[3 + k], rsem.at[3 + k], sib)
            fw.start()
            passed.append(fw)
        for k, (ox, oy) in enumerate(others):
            slot = all_ref.at[1 - c, 2 * ox + oy]
            _rcopy(slot, slot, ssem.at[3 + k], rsem.at[3 + k], sib).wait_recv()
        for cp in first + passed:
            cp.wait_send()
        for cp in loc:
            cp.wait()

    return pl.pallas_call(
        body, name="gather_weights",
        in_specs=[ANY], out_specs=ANY,
        out_shape=jax.ShapeDtypeStruct((2, SHARDS, R, C), wpk.dtype),
        scratch_shapes=[pltpu.SemaphoreType.DMA((6,)), pltpu.SemaphoreType.DMA((6,)), pltpu.SemaphoreType.DMA((2,))],
    )(wpk)


def grad_sibling_exchange(g):
    _, _, R, C = g.shape

    def body(g_ref, sb_ref, ssem, rsem):
        x, y, c, _ = _place()
        cp = _rcopy(g_ref.at[1 - c], sb_ref, ssem.at[0], rsem.at[0], (x, y, 1 - c))
        cp.start()
        cp.wait()

    return pl.pallas_call(
        body, name="grad_sibling_exchange",
        in_specs=[ANY], out_specs=ANY,
        out_shape=jax.ShapeDtypeStruct((SHARDS, R, C), F32),
        scratch_shapes=[pltpu.SemaphoreType.DMA((1,)), pltpu.SemaphoreType.DMA((1,))],
    )(g)


def grad_chip_sum(g, sb, layer):
    _, _, R, C = g.shape
    tr = PACK_TILE
    grid_spec = pltpu.PrefetchScalarGridSpec(
        num_scalar_prefetch=1, grid=(SHARDS, R // tr),
        in_specs=[pl.BlockSpec((1, 1, tr, C), lambda s, r, lref: (lref[0], s, r, 0)),
                  pl.BlockSpec((1, tr, C), lambda s, r, lref: (s, r, 0))],
        out_specs=pl.BlockSpec((1, tr, C), lambda s, r, lref: (s, r, 0)))

    def body(l_ref, a_ref, b_ref, o_ref):
        o_ref[...] = a_ref[0] + b_ref[...]

    return pl.pallas_call(
        body, name="grad_chip_sum", grid_spec=grid_spec,
        out_shape=jax.ShapeDtypeStruct((SHARDS, R, C), F32),
        compiler_params=_params(("arbitrary", "arbitrary")),
    )(layer, g, sb)


def grad_chip_exchange(t):
    _, R, C = t.shape

    def body(t_ref, rb_ref, ssem, rsem, lsem):
        x, y, c, others = _place()
        me = 2 * x + y
        loc = pltpu.make_async_copy(t_ref.at[me], rb_ref.at[me], lsem.at[0])
        loc.start()
        sends = [_rcopy(t_ref.at[2 * ox + oy], rb_ref.at[me], ssem.at[k], rsem.at[k], (ox, oy, c))
                 for k, (ox, oy) in enumerate(others)]
        for cp in sends:
            cp.start()
        for k, (ox, oy) in enumerate(others):
            slot = rb_ref.at[2 * ox + oy]
            _rcopy(slot, slot, ssem.at[k], rsem.at[k], (ox, oy, c)).wait_recv()
        for cp in sends:
            cp.wait_send()
        loc.wait()

    return pl.pallas_call(
        body, name="grad_chip_exchange",
        in_specs=[ANY], out_specs=ANY,
        out_shape=jax.ShapeDtypeStruct((SHARDS, R, C), F32),
        scratch_shapes=[pltpu.SemaphoreType.DMA((3,)), pltpu.SemaphoreType.DMA((3,)), pltpu.SemaphoreType.DMA((1,))],
    )(t)


def grad_shard_sum(rb):
    _, R, C = rb.shape
    tr = PACK_TILE

    def body(r_ref, o_ref):
        o_ref[...] = ((r_ref[0] + r_ref[1]) + r_ref[2]) + r_ref[3]

    return pl.pallas_call(
        body, name="grad_shard_sum", grid=(R // tr,),
        in_specs=[pl.BlockSpec((SHARDS, tr, C), lambda r: (0, r, 0))],
        out_specs=pl.BlockSpec((tr, C), lambda r: (r, 0)),
        out_shape=jax.ShapeDtypeStruct((R, C), F32),
        compiler_params=_params(("arbitrary",)),
    )(rb)


def grad_sibling_share(f):
    R, C = f.shape

    def body(f_ref, out_ref, ssem, rsem, lsem):
        x, y, c, _ = _place()
        loc = pltpu.make_async_copy(f_ref, out_ref.at[c], lsem.at[0])
        loc.start()
        cp = _rcopy(f_ref, out_ref.at[c], ssem.at[0], rsem.at[0], (x, y, 1 - c))
        cp.start()
        theirs = out_ref.at[1 - c]
        _rcopy(theirs, theirs, ssem.at[0], rsem.at[0], (x, y, 1 - c)).wait_recv()
        cp.wait_send()
        loc.wait()

    return pl.pallas_call(
        body, name="grad_sibling_share",
        in_specs=[ANY], out_specs=ANY,
        out_shape=jax.ShapeDtypeStruct((2, R, C), F32),
        scratch_shapes=[pltpu.SemaphoreType.DMA((1,)), pltpu.SemaphoreType.DMA((1,)), pltpu.SemaphoreType.DMA((1,))],
    )(f)


def small_allreduce(buf, name):
    rows = buf.shape[0]
    VM = pl.BlockSpec(memory_space=pltpu.VMEM)

    def body(src_ref, out_ref, sib_buf, chips, ssem, rsem):
        x, y, c, others = _place()
        me = 2 * x + y
        sib = (x, y, 1 - c)
        cp = _rcopy(src_ref, sib_buf, ssem.at[0], rsem.at[0], sib)
        cp.start()
        cp.wait()
        chips[me] = src_ref[...] + sib_buf[...]
        sends = [_rcopy(chips.at[me], chips.at[me], ssem.at[1 + k], rsem.at[1 + k], (ox, oy, c))
                 for k, (ox, oy) in enumerate(others)]
        for s in sends:
            s.start()
        for k, (ox, oy) in enumerate(others):
            slot = chips.at[2 * ox + oy]
            _rcopy(slot, slot, ssem.at[1 + k], rsem.at[1 + k], (ox, oy, c)).wait_recv()
        for s in sends:
            s.wait_send()
        out_ref[...] = ((chips[0] + chips[1]) + chips[2]) + chips[3]

    return pl.pallas_call(
        body, name=name,
        in_specs=[VM], out_specs=VM,
        out_shape=jax.ShapeDtypeStruct((rows, 128), F32),
        scratch_shapes=[pltpu.VMEM((rows, 128), F32), pltpu.VMEM((SHARDS, rows, 128), F32),
                        pltpu.SemaphoreType.DMA((4,)), pltpu.SemaphoreType.DMA((4,))],
        compiler_params=_params(),
    )(buf)


def _pad_lanes(v):
    return jnp.zeros((1, 128), F32).at[0, :v.shape[0]].set(v)


def layer_fwd(x, wts, bias):
    proj, h = inproj_fwd(x, wts["g_pre"], wts["wp"])
    y_att = att_fwd(proj, bias, wts["sinks"])
    y_sg = sgu_fwd(proj, wts["ln_g"], wts["ln_b"], wts["sg_w"], wts["sg_bt"])
    y_ssm, hst = ssd_fwd(proj, wts["conv_w"], wts["conv_b"], wts["dt_bias"], wts["a_log"], wts["d_skip"],
                         wts["norm_g"])
    x_new, br_a, br_s, br_m, merged, out = merge_fwd(
        y_att, y_sg, y_ssm, proj, x, wts["w_a"], wts["w_s"], wts["w_m"], wts["w_o"], wts["g_post"])
    saved = dict(x=x, proj=proj, h=h, y_att=y_att, y_sg=y_sg, y_ssm=y_ssm, hst=hst,
                 br_a=br_a, br_s=br_s, br_m=br_m, merged=merged, out=out)
    return x_new, saved


def layer_bwd(dy, wts, bias, sv):
    proj = sv["proj"]
    dout, dba, dbs, dbm, dgates, dya, dys, dym, dg_post = merge_bwd(
        dy, sv["out"], wts["g_post"], proj, sv["br_a"], sv["br_s"], sv["br_m"],
        wts["w_a"], wts["w_s"], wts["w_m"], wts["w_o"])
    dq, dza, dk, dv, dbias, dsinks = att_bwd(dya, proj, bias, wts["sinks"])
    dsgu, dsg_w, dsg_bt, dln_g, dln_b = sgu_bwd(dys, proj, wts["ln_g"], wts["ln_b"], wts["sg_w"], wts["sg_bt"])
    dzm, dxbc, ddt, dcw, dcb, ddtb, dalog, ddsk, dng = ssd_bwd(
        dym, proj, sv["hst"], wts["conv_w"], wts["conv_b"], wts["dt_bias"], wts["a_log"], wts["d_skip"],
        wts["norm_g"])
    S = dy.shape[0]
    dproj = jnp.concatenate([dxbc, dgates, dzm, dq, dza, dsgu, dk, dv, ddt,
                             jnp.zeros((S, NCP - O_DT - 128), BF16)], axis=1)
    dx, dg_pre = inproj_bwd(dproj, wts["wp"], sv["x"], wts["g_pre"], dy)
    grads = dict(
        w_in=matmul_tn(sv["h"], dproj, "dw_in", tn=1536),
        w_a=matmul_tn(sv["y_att"], dba, "dw_att"),
        w_s=matmul_tn(sv["y_sg"], dbs, "dw_sg"),
        w_m=matmul_tn(sv["y_ssm"], dbm, "dw_ssm"),
        w_o=matmul_tn(sv["merged"], dout, "dw_out"),
        g_pre=dg_pre, g_post=dg_post, sinks=dsinks, ln_g=dln_g, ln_b=dln_b, sg_w=dsg_w, sg_bt=dsg_bt,
        conv_w=dcw, conv_b=dcb, dt_bias=ddtb, a_log=dalog, d_skip=ddsk, norm_g=dng, bias=dbias)
    return dx, grads


BIG_ROWS = (3400, 256, 256, 512, 256)
CONVW_ROW = 4680

SMALL = (("norm_pre", 8), ("norm_post", 8), ("att_sinks", 8), ("sg_ln_g", 8), ("sg_ln_b", 8), ("sg_w", 1024),
         ("sg_b", 8), ("ssm_conv_b", 24), ("ssm_dt_bias", 8), ("ssm_a_log", 8), ("ssm_d", 8), ("ssm_norm_g", 16))
SMALL_LAYER_ROWS = sum(r for _, r in SMALL)
SMALL_ROWS = 2304


def _rows128(v):
    flat = v.reshape(-1)
    n = flat.shape[0]
    rows = 8 * (-(-n // 1024))
    if rows * 128 != n:
        flat = jnp.pad(flat, (0, rows * 128 - n))
    return flat.reshape(rows, 128)


def pack_small(per_layer, rel_bias, extra_row):
    parts = []
    for l in range(DEPTH):
        for name, rows in SMALL:
            p = _rows128(per_layer[name][l])
            assert p.shape[0] == rows, (name, p.shape)
            parts.append(p)
    parts.append(_rows128(rel_bias))
    parts.append(jnp.pad(extra_row, ((0, 7), (0, 0))))
    parts.append(jnp.zeros((SMALL_ROWS - 2 * SMALL_LAYER_ROWS - 16, 128), F32))
    return jnp.concatenate(parts, axis=0)


def unpack_small(buf, shapes):
    out = {name: [] for name, _ in SMALL}
    r = 0
    for l in range(DEPTH):
        for name, rows in SMALL:
            shp = shapes[name][1:]
            n = int(np.prod(shp))
            out[name].append(buf[r:r + rows].reshape(-1)[:n].reshape(shp))
            r += rows
    rel = buf[r:r + 4].reshape(32, 16)
    extra = buf[r + 8:r + 9]
    return {k: jnp.stack(v) for k, v in out.items()}, rel, extra


def pack_big(w_in, w_a, w_s, w_m, w_o, conv_w=None):
    dt = w_in.dtype
    tail = jnp.zeros((DEPTH, PACK_ROWS - CONVW_ROW, 1024), dt)
    if conv_w is not None:
        tail = tail.at[:, 0:3].set(conv_w.reshape(DEPTH, 3, 1024))
    return jnp.concatenate([w_in.reshape(DEPTH, 3400, 1024), w_a, w_s, w_m, w_o, tail], axis=1)


def unpack_big(buf):
    o = np.cumsum((0,) + BIG_ROWS)
    return (buf[:, o[0]:o[1]].reshape(DEPTH, 1024, 3400), buf[:, o[1]:o[2]], buf[:, o[2]:o[3]],
            buf[:, o[3]:o[4]], buf[:, o[4]:o[5]], buf[:, CONVW_ROW:CONVW_ROW + 3].reshape(DEPTH, CONV_K, 768))


def kernel(x, w_in, norm_pre, norm_post, rel_bias, att_sinks, sg_ln_g, sg_ln_b, sg_w, sg_b, ssm_conv_w, ssm_conv_b, ssm_dt_bias, ssm_a_log, ssm_d, ssm_norm_g, w_br_att, w_br_sg, w_br_ssm, w_out, loss_target, m_w_in, m_norm_pre, m_norm_post, m_rel_bias, m_att_sinks, m_sg_ln_g, m_sg_ln_b, m_sg_w, m_sg_b, m_ssm_conv_w, m_ssm_conv_b, m_ssm_dt_bias, m_ssm_a_log, m_ssm_d, m_ssm_norm_g, m_w_br_att, m_w_br_sg, m_w_br_ssm, m_w_out, v_w_in, v_norm_pre, v_norm_post, v_rel_bias, v_att_sinks, v_sg_ln_g, v_sg_ln_b, v_sg_w, v_sg_b, v_ssm_conv_w, v_ssm_conv_b, v_ssm_dt_bias, v_ssm_a_log, v_ssm_d, v_ssm_norm_g, v_w_br_att, v_w_br_sg, v_w_br_ssm, v_w_out):
    cx, cy, cc = lax.axis_index("x"), lax.axis_index("y"), lax.axis_index("c")
    me = 2 * cx + cy
    xs = x[0]
    S = xs.shape[0]

    wall = gather_weights(pack_big(w_in, w_br_att, w_br_sg, w_br_ssm, w_out).astype(BF16))
    o = np.cumsum((0,) + BIG_ROWS)
    convw_slot = jnp.zeros((SHARDS, DEPTH * CONV_K * 768 // 128, 128), F32)
    convw_slot = lax.dynamic_update_index_in_dim(
        convw_slot, jnp.where(cc == 0, 1.0, 0.0) * ssm_conv_w.reshape(-1, 128), me, 0)
    convw_all = small_allreduce(convw_slot.reshape(-1, 128), "gather_conv_w")
    convw_all = convw_all.reshape(SHARDS, DEPTH, CONV_K, 768).transpose(1, 2, 0, 3).reshape(DEPTH, CONV_K, CONV_C)

    small_w = dict(norm_pre=norm_pre, norm_post=norm_post, att_sinks=att_sinks, sg_ln_g=sg_ln_g, sg_ln_b=sg_ln_b,
                   sg_w=sg_w, sg_b=sg_b, ssm_conv_b=ssm_conv_b, ssm_dt_bias=ssm_dt_bias, ssm_a_log=ssm_a_log,
                   ssm_d=ssm_d, ssm_norm_g=ssm_norm_g)
    layers = []
    for l in range(DEPTH):
        w_in_full = jnp.concatenate([wall[l, s, o[0]:o[1]].reshape(1024, 3400) for s in range(SHARDS)], axis=1)
        layers.append(dict(
            wp=to_padded_cols(w_in_full),
            w_a=wall[l, :, o[1]:o[2]].reshape(1024, D), w_s=wall[l, :, o[2]:o[3]].reshape(1024, D),
            w_m=wall[l, :, o[3]:o[4]].reshape(2048, D), w_o=wall[l, :, o[4]:o[5]].reshape(D, D),
            g_pre=norm_pre[l][None], g_post=norm_post[l][None], sinks=att_sinks[l],
            ln_g=sg_ln_g[l][None], ln_b=sg_ln_b[l][None], sg_w=sg_w[l],
            sg_bt=sg_b[l].T,
            conv_w=jnp.concatenate([convw_all[l], jnp.zeros((4, CONV_C), F32)], axis=0),
            conv_b=ssm_conv_b[l][None], dt_bias=_pad_lanes(ssm_dt_bias[l]), a_log=_pad_lanes(ssm_a_log[l]),
            d_skip=_pad_lanes(ssm_d[l]), norm_g=ssm_norm_g[l][None]))

    bias = bias_table(rel_bias)
    saved = []
    act = xs
    for l in range(DEPTH):
        act, sv = layer_fwd(act, layers[l], bias)
        saved.append(sv)
    dy, loss_part = loss_head(act, loss_target[0])
    grads = [None] * DEPTH
    for l in reversed(range(DEPTH)):
        dy, grads[l] = layer_bwd(dy, layers[l], bias, saved[l])
    grad_x = dy[None]
    grad_rel_local = bias_grad(grads[0]["bias"] + grads[1]["bias"])

    def shard_rows(gw, rows):
        return gw.reshape(SHARDS, rows, 1024)

    gparts = []
    for l in range(DEPTH):
        g = grads[l]
        gin = from_padded_cols(g["w_in"]).reshape(1024, SHARDS, 3400).transpose(1, 0, 2).reshape(SHARDS, 3400, 1024)
        gcw = g["conv_w"][0:CONV_K].reshape(CONV_K, SHARDS, 768).transpose(1, 0, 2).reshape(SHARDS, 3, 1024)
        gcw = jnp.pad(gcw, ((0, 0), (0, 5), (0, 0)))
        gparts.append(jnp.concatenate([
            gin, shard_rows(g["w_a"], 256), shard_rows(g["w_s"], 256), shard_rows(g["w_m"], 512),
            shard_rows(g["w_o"], 256), gcw, jnp.zeros((SHARDS, PACK_ROWS - CONVW_ROW - 8, 1024), F32)], axis=1))
    gbig = jnp.stack(gparts)
    sb = grad_sibling_exchange(gbig)
    chip = grad_chip_sum(gbig, sb, jnp.reshape(cc, (1,)).astype(jnp.int32))
    fsum = grad_shard_sum(grad_chip_exchange(chip))
    gred = grad_sibling_share(fsum)
    g_w_in, g_w_a, g_w_s, g_w_m, g_w_o, g_conv_w = unpack_big(gred)

    small_g = dict(
        norm_pre=jnp.stack([grads[l]["g_pre"][0] for l in range(DEPTH)]),
        norm_post=jnp.stack([grads[l]["g_post"][0] for l in range(DEPTH)]),
        att_sinks=jnp.stack([grads[l]["sinks"][0, :HEADS] for l in range(DEPTH)]),
        sg_ln_g=jnp.stack([grads[l]["ln_g"][0] for l in range(DEPTH)]),
        sg_ln_b=jnp.stack([grads[l]["ln_b"][0] for l in range(DEPTH)]),
        sg_w=jnp.stack([grads[l]["sg_w"] for l in range(DEPTH)]),
        sg_b=jnp.stack([grads[l]["sg_bt"][:, 0:8].T for l in range(DEPTH)]),
        ssm_conv_b=jnp.stack([grads[l]["conv_b"][0] for l in range(DEPTH)]),
        ssm_dt_bias=jnp.stack([grads[l]["dt_bias"][0, :SSM_H] for l in range(DEPTH)]),
        ssm_a_log=jnp.stack([grads[l]["a_log"][0, :SSM_H] for l in range(DEPTH)]),
        ssm_d=jnp.stack([grads[l]["d_skip"][0, :SSM_H] for l in range(DEPTH)]),
        ssm_norm_g=jnp.stack([grads[l]["norm_g"][0] for l in range(DEPTH)]))
    shapes = {k: v.shape for k, v in small_w.items()}
    red = small_allreduce(pack_small(small_g, grad_rel_local, loss_part), "small_allreduce")
    g_small, g_rel, loss_row = unpack_small(red, shapes)
    loss = loss_row[0, 0]

    def upd(w, g, m, v, name):
        shp = w.shape
        two = lambda a: a.reshape(-1, shp[-1])
        d, nm, nv = adamw(two(w), two(g), two(m), two(v), name)
        return d.reshape(shp), nm.reshape(shp), nv.reshape(shp)

    small_m = dict(norm_pre=m_norm_pre, norm_post=m_norm_post, att_sinks=m_att_sinks, sg_ln_g=m_sg_ln_g,
                   sg_ln_b=m_sg_ln_b, sg_w=m_sg_w, sg_b=m_sg_b, ssm_conv_b=m_ssm_conv_b, ssm_dt_bias=m_ssm_dt_bias,
                   ssm_a_log=m_ssm_a_log, ssm_d=m_ssm_d, ssm_norm_g=m_ssm_norm_g)
    small_v = dict(norm_pre=v_norm_pre, norm_post=v_norm_post, att_sinks=v_att_sinks, sg_ln_g=v_sg_ln_g,
                   sg_ln_b=v_sg_ln_b, sg_w=v_sg_w, sg_b=v_sg_b, ssm_conv_b=v_ssm_conv_b, ssm_dt_bias=v_ssm_dt_bias,
                   ssm_a_log=v_ssm_a_log, ssm_d=v_ssm_d, ssm_norm_g=v_ssm_norm_g)
    zero_row = jnp.zeros((1, 128), F32)
    sd, sm, sv_ = adamw(pack_small(small_w, rel_bias, zero_row), red,
                        pack_small(small_m, m_rel_bias, zero_row), pack_small(small_v, v_rel_bias, zero_row),
                        "adamw_small")
    d_small, d_rel, _ = unpack_small(sd, shapes)
    m_small, m_rel, _ = unpack_small(sm, shapes)
    v_small, v_rel, _ = unpack_small(sv_, shapes)

    big = dict(
        w_in=upd(w_in, g_w_in, m_w_in, v_w_in, "adamw_w_in"),
        ssm_conv_w=upd(ssm_conv_w, g_conv_w, m_ssm_conv_w, v_ssm_conv_w, "adamw_conv_w"),
        w_br_att=upd(w_br_att, g_w_a, m_w_br_att, v_w_br_att, "adamw_w_br_att"),
        w_br_sg=upd(w_br_sg, g_w_s, m_w_br_sg, v_w_br_sg, "adamw_w_br_sg"),
        w_br_ssm=upd(w_br_ssm, g_w_m, m_w_br_ssm, v_w_br_ssm, "adamw_w_br_ssm"),
        w_out=upd(w_out, g_w_o, m_w_out, v_w_out, "adamw_w_out"))
    big_g = dict(w_in=g_w_in, ssm_conv_w=g_conv_w, w_br_att=g_w_a, w_br_sg=g_w_s, w_br_ssm=g_w_m, w_out=g_w_o)

    order = ["w_in", "norm_pre", "norm_post", "rel_bias", "att_sinks", "sg_ln_g", "sg_ln_b", "sg_w", "sg_b",
             "ssm_conv_w", "ssm_conv_b", "ssm_dt_bias", "ssm_a_log", "ssm_d", "ssm_norm_g",
             "w_br_att", "w_br_sg", "w_br_ssm", "w_out"]

    def pick(kind):
        res = []
        for name in order:
            if name in big:
                res.append(big_g[name] if kind == 0 else big[name][kind - 1])
            elif name == "rel_bias":
                res.append((g_rel, d_rel, m_rel, v_rel)[kind])
            else:
                res.append((g_small, d_small, m_small, v_small)[kind][name])
        return res

    return (loss, grad_x, *pick(0), *pick(1), *pick(2), *pick(3))
```

```python
import functools
import math

import numpy as np
import jax
import jax.numpy as jnp
from jax import lax
from jax.experimental import pallas as pl
from jax.experimental.pallas import tpu as pltpu

F32 = jnp.float32
BF16 = jnp.bfloat16
MESH = pl.DeviceIdType.MESH

D = 1024
DEPTH = 2
EPS = 1e-6
L = 128
HEADS = 16
KV = 2
DH = 64
SSM_W = 2048
SSM_H = 32
SSM_P = 64
SSM_G = 4
SSM_N = 128
CONV_K = 4
CONV_C = 3072
NEG = -1e30
IN_COLS = 13600
NCP = 13824

O_XBC, O_GATE, O_ZM, O_Q, O_ZA, O_U, O_VS, O_ZS, O_K, O_V, O_DT = (
    0, 3072, 6144, 8192, 9216, 10240, 11264, 12288, 13312, 13440, 13568)

ADAM_LR = 0.001
ADAM_B1 = 0.9
ADAM_B2 = 0.999
ADAM_EPS = 1e-08
ADAM_WD = 0.01
ADAM_STEP = 10

VMEM_LIMIT = 56 * 1024 * 1024

PACK_ROWS = 4704
PACK_TILE = 224
SHARDS = 4


def _dot(a, b):
    return jnp.dot(a, b, preferred_element_type=F32)


def _dot_nt(a, b):
    return lax.dot_general(a, b, (((1,), (1,)), ((), ())), preferred_element_type=F32)


def _dot_tn(a_f32, b):
    return jnp.dot(a_f32.T.astype(BF16), b, preferred_element_type=F32)


def _dot_hi(a, b):
    return jnp.dot(a, b, preferred_element_type=F32, precision=lax.Precision.HIGHEST)


def _sigmoid(x):
    return 1.0 / (1.0 + jnp.exp(-x))


def _softplus(x):
    return jnp.maximum(x, 0.0) + jnp.log(1.0 + jnp.exp(-jnp.abs(x)))


def _params(sem=None, vmem=VMEM_LIMIT):
    kw = dict(vmem_limit_bytes=vmem)
    if sem is not None:
        kw["dimension_semantics"] = sem
    return pltpu.CompilerParams(**kw)


def _full(shape):
    nd = len(shape)
    return pl.BlockSpec(shape, lambda *_: (0,) * nd)


def to_padded_cols(w):
    pad = jnp.zeros(w.shape[:-1] + (NCP - O_DT - 32,), w.dtype)
    return jnp.concatenate([
        w[..., 7424:10496], w[..., 10528:13600], w[..., 5376:7424], w[..., 0:1024],
        w[..., 1280:2304], w[..., 2304:3328], w[..., 3328:4352], w[..., 4352:5376],
        w[..., 1024:1152], w[..., 1152:1280], w[..., 10496:10528], pad], axis=-1)


def from_padded_cols(g):
    return jnp.concatenate([
        g[..., O_Q:O_Q + 1024], g[..., O_K:O_K + 128], g[..., O_V:O_V + 128],
        g[..., O_ZA:O_ZA + 1024], g[..., O_U:O_U + 3072], g[..., O_ZM:O_ZM + 2048],
        g[..., O_XBC:O_XBC + 3072], g[..., O_DT:O_DT + 32], g[..., O_GATE:O_GATE + 3072]], axis=-1)


def _bucket_table():
    qi = np.arange(L)[:, None]
    kj = np.arange(2 * L)[None, :]
    dist = np.maximum(qi + L - kj, 0)
    dist_f = np.maximum(dist, 1).astype(np.float32)
    large = 16 + (np.log(dist_f / np.float32(16)) / np.float32(math.log(128 / 16)) * np.float32(16)).astype(np.int32)
    large = np.minimum(large, 31)
    return np.where(dist < 16, dist, large).astype(np.int32)


def bias_table(rel_bias):
    buckets = jnp.asarray(_bucket_table().reshape(1, L * 2 * L))

    def body(rb_ref, bk_ref, out_ref):
        onehot = (lax.broadcasted_iota(jnp.int32, (32, L * 2 * L), 0) == bk_ref[...]).astype(F32)
        out_ref[...] = lax.dot_general(rb_ref[...], onehot, (((0,), (0,)), ((), ())),
                                       preferred_element_type=F32, precision=lax.Precision.HIGHEST)

    out = pl.pallas_call(
        body, name="bias_table",
        out_shape=jax.ShapeDtypeStruct((HEADS, L * 2 * L), F32),
        compiler_params=_params(),
    )(rel_bias, buckets)
    return out.reshape(HEADS, L, 2 * L)


def bias_grad(dbias):
    buckets = jnp.asarray(_bucket_table().reshape(1, L * 2 * L))

    def body(db_ref, bk_ref, out_ref):
        onehot = (lax.broadcasted_iota(jnp.int32, (32, L * 2 * L), 0) == bk_ref[...]).astype(F32)
        out_ref[...] = lax.dot_general(onehot, db_ref[...], (((1,), (1,)), ((), ())),
                                       preferred_element_type=F32, precision=lax.Precision.HIGHEST)

    return pl.pallas_call(
        body, name="bias_grad",
        out_shape=jax.ShapeDtypeStruct((32, HEADS), F32),
        compiler_params=_params(),
    )(dbias.reshape(HEADS, L * 2 * L), buckets)


def inproj_fwd(x, g_pre, wp):
    S = x.shape[0]
    tm, tn = 512, 1536

    def body(x_ref, g_ref, w_ref, proj_ref, h_ref):
        @pl.when(pl.program_id(1) == 0)
        def _():
            xv = x_ref[...]
            r = lax.rsqrt(jnp.mean(xv * xv, axis=-1, keepdims=True) + EPS)
            h_ref[...] = (xv * r * g_ref[...]).astype(BF16)
        proj_ref[...] = _dot(h_ref[...], w_ref[...])

    return pl.pallas_call(
        body, name="inproj_fwd", grid=(S // tm, NCP // tn),
        in_specs=[pl.BlockSpec((tm, D), lambda i, j: (i, 0)), _full((1, D)),
                  pl.BlockSpec((D, tn), lambda i, j: (0, j))],
        out_specs=[pl.BlockSpec((tm, tn), lambda i, j: (i, j)), pl.BlockSpec((tm, D), lambda i, j: (i, 0))],
        out_shape=[jax.ShapeDtypeStruct((S, NCP), F32), jax.ShapeDtypeStruct((S, D), BF16)],
        compiler_params=_params(("arbitrary", "arbitrary")),
    )(x, g_pre, wp)


def inproj_bwd(dproj, wp, x, g_pre, dy):
    S = x.shape[0]
    tm, tk = 512, 1536
    nk = NCP // tk

    def body(dp_ref, w_ref, x_ref, g_ref, dy_ref, dx_ref, dg_ref, acc):
        i, k = pl.program_id(0), pl.program_id(1)

        @pl.when(k == 0)
        def _():
            acc[...] = jnp.zeros_like(acc)

        acc[...] += _dot_nt(dp_ref[...], w_ref[...])

        @pl.when((k == nk - 1) & (i == 0))
        def _():
            dg_ref[...] = jnp.zeros_like(dg_ref)

        @pl.when(k == nk - 1)
        def _():
            xv = x_ref[...]
            dh = acc[...]
            g = g_ref[...]
            r = lax.rsqrt(jnp.mean(xv * xv, axis=-1, keepdims=True) + EPS)
            dhg = dh * g
            dx_ref[...] = dy_ref[...] + r * dhg - xv * (r * r * r) * jnp.mean(dhg * xv, axis=-1, keepdims=True)
            dg_ref[...] += jnp.sum(dh * xv * r, axis=0, keepdims=True)

    return pl.pallas_call(
        body, name="inproj_bwd", grid=(S // tm, nk),
        in_specs=[pl.BlockSpec((tm, tk), lambda i, k: (i, k)), pl.BlockSpec((D, tk), lambda i, k: (0, k)),
                  pl.BlockSpec((tm, D), lambda i, k: (i, 0)), _full((1, D)),
                  pl.BlockSpec((tm, D), lambda i, k: (i, 0))],
        out_specs=[pl.BlockSpec((tm, D), lambda i, k: (i, 0)), _full((1, D))],
        out_shape=[jax.ShapeDtypeStruct((S, D), F32), jax.ShapeDtypeStruct((1, D), F32)],
        scratch_shapes=[pltpu.VMEM((tm, D), F32)],
        compiler_params=_params(("arbitrary", "arbitrary")),
    )(dproj, wp, x, g_pre, dy)


def matmul_tn(a, b, name, tn=512, ts=512):
    S, K = a.shape
    N = b.shape[1]
    ns = S // ts

    def body(a_ref, b_ref, o_ref):
        @pl.when(pl.program_id(1) == 0)
        def _():
            o_ref[...] = jnp.zeros_like(o_ref)
        o_ref[...] += _dot_tn(a_ref[...].astype(F32), b_ref[...])

    return pl.pallas_call(
        body, name=name, grid=(N // tn, ns),
        in_specs=[pl.BlockSpec((ts, K), lambda j, s: (s, 0)), pl.BlockSpec((ts, tn), lambda j, s: (s, j))],
        out_specs=pl.BlockSpec((K, tn), lambda j, s: (0, j)),
        out_shape=jax.ShapeDtypeStruct((K, N), F32),
        compiler_params=_params(("arbitrary", "arbitrary")),
    )(a, b)


def _att_mask(n):
    qi = lax.broadcasted_iota(jnp.int32, (L, 2 * L), 0)
    kj = lax.broadcasted_iota(jnp.int32, (L, 2 * L), 1)
    dist = qi + L - kj
    return (dist >= 0) & (dist < L) & ((kj >= L) | (n > 0))


def _att_in_specs(nb):
    last = nb - 1
    cur = lambda n: jnp.minimum(n, last)
    prev = lambda n: jnp.maximum(jnp.minimum(n, last) - 1, 0)
    return [
        pl.BlockSpec((L, 1024), lambda n: (cur(n), O_Q // 1024)),
        pl.BlockSpec((L, 128), lambda n: (prev(n), O_K // 128)),
        pl.BlockSpec((L, 128), lambda n: (cur(n), O_K // 128)),
        pl.BlockSpec((L, 128), lambda n: (prev(n), O_V // 128)),
        pl.BlockSpec((L, 128), lambda n: (cur(n), O_V // 128)),
        pl.BlockSpec((L, 1024), lambda n: (cur(n), O_ZA // 1024)),
        _full((HEADS, L, 2 * L)),
        pl.BlockSpec(memory_space=pltpu.SMEM),
    ]


def _att_probs(qh, kk, bias_h, mask, sk):
    logits = _dot_nt(qh, kk) + bias_h
    logits = jnp.where(mask, logits, NEG)
    m = jnp.maximum(jnp.max(logits, axis=-1, keepdims=True), sk)
    p = jnp.exp(logits - m)
    es = jnp.exp(sk - m)
    den = jnp.sum(p, axis=-1, keepdims=True) + es
    return p / den, es / den


def att_fwd(proj, bias, sinks):
    S = proj.shape[0]
    nb = S // L

    def body(q_ref, kp_ref, kc_ref, vp_ref, vc_ref, z_ref, bias_ref, s_ref, y_ref, o_scr):
        mask = _att_mask(pl.program_id(0))
        for kv in range(KV):
            sl = slice(kv * DH, (kv + 1) * DH)
            kk = jnp.concatenate([kp_ref[:, sl], kc_ref[:, sl]], axis=0).astype(BF16)
            vv = jnp.concatenate([vp_ref[:, sl], vc_ref[:, sl]], axis=0).astype(BF16)
            for g in range(HEADS // KV):
                h = kv * (HEADS // KV) + g
                hs = slice(h * DH, (h + 1) * DH)
                qh = (q_ref[:, hs] * 0.125).astype(BF16)
                P, _ = _att_probs(qh, kk, bias_ref[h], mask, s_ref[h])
                o_scr[:, hs] = _dot(P.astype(BF16), vv)
        z = z_ref[...]
        y_ref[...] = (o_scr[...] * (z * _sigmoid(z))).astype(BF16)

    return pl.pallas_call(
        body, name="att_fwd", grid=(nb,),
        in_specs=_att_in_specs(nb),
        out_specs=pl.BlockSpec((L, 1024), lambda n: (n, 0)),
        out_shape=jax.ShapeDtypeStruct((S, 1024), BF16),
        scratch_shapes=[pltpu.VMEM((L, 1024), F32)],
        compiler_params=_params(("arbitrary",)),
    )(proj, proj, proj, proj, proj, proj, bias, sinks)


def att_bwd(dy, proj, bias, sinks):
    S = proj.shape[0]
    nb = S // L
    last = nb - 1

    def body(dy_ref, q_ref, kp_ref, kc_ref, vp_ref, vc_ref, z_ref, bias_ref, s_ref,
             dq_ref, dz_ref, dk_ref, dv_ref, dbias_ref, dsink_ref, carry, band, dq_scr, dz_scr):
        n = pl.program_id(0)

        @pl.when(n == 0)
        def _():
            carry[...] = jnp.zeros_like(carry)
            dbias_ref[...] = jnp.zeros_like(dbias_ref)
            dsink_ref[...] = jnp.zeros_like(dsink_ref)

        band[...] = jnp.zeros_like(band)

        @pl.when(n < nb)
        def _():
            mask = _att_mask(n)
            lane = lax.broadcasted_iota(jnp.int32, (1, 128), 1)
            dsink = jnp.zeros((1, 128), F32)
            for kv in range(KV):
                sl = slice(kv * DH, (kv + 1) * DH)
                kk = jnp.concatenate([kp_ref[:, sl], kc_ref[:, sl]], axis=0).astype(BF16)
                vv = jnp.concatenate([vp_ref[:, sl], vc_ref[:, sl]], axis=0).astype(BF16)
                dk_acc = jnp.zeros((2 * L, DH), F32)
                dv_acc = jnp.zeros((2 * L, DH), F32)
                for g in range(HEADS // KV):
                    h = kv * (HEADS // KV) + g
                    hs = slice(h * DH, (h + 1) * DH)
                    qh = (q_ref[:, hs] * 0.125).astype(BF16)
                    P, psink = _att_probs(qh, kk, bias_ref[h], mask, s_ref[h])
                    Pb = P.astype(BF16)
                    zh = z_ref[:, hs]
                    sg = _sigmoid(zh)
                    dyh = dy_ref[:, hs]
                    O = _dot(Pb, vv)
                    dO = dyh * (zh * sg)
                    dz_scr[:, hs] = dyh * O * (sg * (1.0 + zh * (1.0 - sg)))
                    dOb = dO.astype(BF16)
                    dv_acc = dv_acc + _dot_tn(P, dOb)
                    dP = _dot_nt(dOb, vv)
                    delta = jnp.sum(P * dP, axis=-1, keepdims=True)
                    dS = P * (dP - delta)
                    dsink = dsink + jnp.where(lane == h, -jnp.sum(psink * delta), 0.0)
                    dSb = dS.astype(BF16)
                    dq_scr[:, hs] = _dot(dSb, kk) * 0.125
                    dk_acc = dk_acc + _dot_tn(dS, qh)
                    dbias_ref[h] += dS
                band[:, sl] = dk_acc
                band[:, 128 + kv * DH:128 + (kv + 1) * DH] = dv_acc
            dsink_ref[...] += dsink
            dq_ref[...] = dq_scr[...].astype(BF16)
            dz_ref[...] = dz_scr[...].astype(BF16)

        out = carry[...] + band[0:L, :]
        dk_ref[...] = out[:, 0:128].astype(BF16)
        dv_ref[...] = out[:, 128:256].astype(BF16)
        carry[...] = band[L:2 * L, :]

    cur = lambda n: jnp.minimum(n, last)
    lag = lambda n: jnp.maximum(n - 1, 0)
    return pl.pallas_call(
        body, name="att_bwd", grid=(nb + 1,),
        in_specs=[pl.BlockSpec((L, 1024), lambda n: (cur(n), 0))] + _att_in_specs(nb),
        out_specs=[pl.BlockSpec((L, 1024), lambda n: (cur(n), 0)), pl.BlockSpec((L, 1024), lambda n: (cur(n), 0)),
                   pl.BlockSpec((L, 128), lambda n: (lag(n), 0)), pl.BlockSpec((L, 128), lambda n: (lag(n), 0)),
                   _full((HEADS, L, 2 * L)), _full((1, 128))],
        out_shape=[jax.ShapeDtypeStruct((S, 1024), BF16), jax.ShapeDtypeStruct((S, 1024), BF16),
                   jax.ShapeDtypeStruct((S, 128), BF16), jax.ShapeDtypeStruct((S, 128), BF16),
                   jax.ShapeDtypeStruct((HEADS, L, 2 * L), F32), jax.ShapeDtypeStruct((1, 128), F32)],
        scratch_shapes=[pltpu.VMEM((L, 256), F32), pltpu.VMEM((2 * L, 256), F32),
                        pltpu.VMEM((L, 1024), F32), pltpu.VMEM((L, 1024), F32)],
        compiler_params=_params(("arbitrary",)),
    )(dy, proj, proj, proj, proj, proj, proj, bias, sinks)


def _sgu_in_specs():
    return [
        pl.BlockSpec((L, 1024), lambda c: (c, O_U // 1024)),
        pl.BlockSpec((L, 1024), lambda c: (c, O_VS // 1024)),
        pl.BlockSpec((L, 1024), lambda c: (c, O_ZS // 1024)),
        _full((1, 1024)), _full((1, 1024)), _full((8, L, L)), _full((L, 8)),
    ]


def _sgu_norm(v, lg, lb):
    mu = jnp.mean(v, axis=-1, keepdims=True)
    vc = v - mu
    rstd = lax.rsqrt(jnp.mean(vc * vc, axis=-1, keepdims=True) + EPS)
    xhat = vc * rstd
    return xhat * lg + lb, xhat, rstd


def _tril():
    return lax.broadcasted_iota(jnp.int32, (L, L), 0) >= lax.broadcasted_iota(jnp.int32, (L, L), 1)


def sgu_fwd(proj, ln_g, ln_b, w, b_t):
    S = proj.shape[0]

    def body(u_ref, v_ref, z_ref, lg_ref, lb_ref, w_ref, bt_ref, y_ref):
        vn, _, _ = _sgu_norm(v_ref[...], lg_ref[...], lb_ref[...])
        tri = _tril()
        parts = []
        for g in range(8):
            wg = jnp.where(tri, w_ref[g], 0.0).astype(BF16)
            parts.append(_dot(wg, vn[:, g * 128:(g + 1) * 128].astype(BF16)) + bt_ref[:, g:g + 1])
        mixed = jnp.concatenate(parts, axis=1)
        z = z_ref[...]
        y_ref[...] = (u_ref[...] * mixed * (z * _sigmoid(z))).astype(BF16)

    return pl.pallas_call(
        body, name="sgu_fwd", grid=(S // L,),
        in_specs=_sgu_in_specs(),
        out_specs=pl.BlockSpec((L, 1024), lambda c: (c, 0)),
        out_shape=jax.ShapeDtypeStruct((S, 1024), BF16),
        compiler_params=_params(("arbitrary",)),
    )(proj, proj, proj, ln_g, ln_b, w, b_t)


def sgu_bwd(dy, proj, ln_g, ln_b, w, b_t):
    S = proj.shape[0]

    def body(dy_ref, u_ref, v_ref, z_ref, lg_ref, lb_ref, w_ref, bt_ref,
             dout_ref, dw_ref, dbt_ref, dlg_ref, dlb_ref):
        @pl.when(pl.program_id(0) == 0)
        def _():
            dw_ref[...] = jnp.zeros_like(dw_ref)
            dbt_ref[...] = jnp.zeros_like(dbt_ref)
            dlg_ref[...] = jnp.zeros_like(dlg_ref)
            dlb_ref[...] = jnp.zeros_like(dlb_ref)

        lg = lg_ref[...]
        vn, xhat, rstd = _sgu_norm(v_ref[...], lg, lb_ref[...])
        tri = _tril()
        lane = lax.broadcasted_iota(jnp.int32, (L, 128), 1)
        wgs, parts = [], []
        for g in range(8):
            wg = jnp.where(tri, w_ref[g], 0.0)
            wgs.append(wg)
            parts.append(_dot(wg.astype(BF16), vn[:, g * 128:(g + 1) * 128].astype(BF16)) + bt_ref[:, g:g + 1])
        mixed = jnp.concatenate(parts, axis=1)
        z = z_ref[...]
        sg = _sigmoid(z)
        silu = z * sg
        dy_v = dy_ref[...]
        u = u_ref[...]
        dout_ref[:, 0:1024] = (dy_v * mixed * silu).astype(BF16)
        dout_ref[:, 2048:3072] = (dy_v * u * mixed * (sg * (1.0 + z * (1.0 - sg)))).astype(BF16)
        dmixed = dy_v * u * silu
        dbt = jnp.zeros((L, 128), F32)
        dvn_parts = []
        for g in range(8):
            dm = dmixed[:, g * 128:(g + 1) * 128]
            dmb = dm.astype(BF16)
            dbt = dbt + jnp.where(lane == g, jnp.sum(dm, axis=1, keepdims=True), 0.0)
            dw_ref[g] += jnp.where(tri, _dot_nt(dmb, vn[:, g * 128:(g + 1) * 128].astype(BF16)), 0.0)
            dvn_parts.append(_dot_tn(wgs[g], dmb))
        dbt_ref[...] += dbt
        dvn = jnp.concatenate(dvn_parts, axis=1)
        dlg_ref[...] += jnp.sum(dvn * xhat, axis=0, keepdims=True)
        dlb_ref[...] += jnp.sum(dvn, axis=0, keepdims=True)
        dxh = dvn * lg
        dv = rstd * (dxh - jnp.mean(dxh, axis=-1, keepdims=True)
                     - xhat * jnp.mean(dxh * xhat, axis=-1, keepdims=True))
        dout_ref[:, 1024:2048] = dv.astype(BF16)

    return pl.pallas_call(
        body, name="sgu_bwd", grid=(S // L,),
        in_specs=[pl.BlockSpec((L, 1024), lambda c: (c, 0))] + _sgu_in_specs(),
        out_specs=[pl.BlockSpec((L, 3072), lambda c: (c, 0)), _full((8, L, L)), _full((L, 128)),
                   _full((1, 1024)), _full((1, 1024))],
        out_shape=[jax.ShapeDtypeStruct((S, 3072), BF16), jax.ShapeDtypeStruct((8, L, L), F32),
                   jax.ShapeDtypeStruct((L, 128), F32), jax.ShapeDtypeStruct((1, 1024), F32),
                   jax.ShapeDtypeStruct((1, 1024), F32)],
        compiler_params=_params(("arbitrary",)),
    )(dy, proj, proj, proj, ln_g, ln_b, w, b_t)


def _expand_matrix():
    r = lax.broadcasted_iota(jnp.int32, (128, SSM_W), 0)
    c = lax.broadcasted_iota(jnp.int32, (128, SSM_W), 1)
    return ((c // SSM_P) == r).astype(F32)


def _ssd_common(ext_ref, cw_ref, cb_ref, dt_raw, dtb, alog):
    pre = cb_ref[...]
    for k in range(CONV_K):
        pre = pre + cw_ref[k:k + 1, :] * ext_ref[pl.ds(5 + k, L), :]
    sg_pre = _sigmoid(pre)
    xc = pre * sg_pre
    dt = _softplus(dt_raw + dtb)
    a = -jnp.exp(alog)
    adt = dt * a
    acs = _dot_hi(_tril().astype(F32), adt)
    return pre, sg_pre, xc, dt, a, acs


def _ssd_in_specs(rev, nc):
    cidx = (lambda c: nc - 1 - c) if rev else (lambda c: c)
    return [
        pl.BlockSpec((L, 2048), lambda c: (cidx(c), O_ZM // 2048)),
        pl.BlockSpec((L, 3072), lambda c: (cidx(c), O_XBC // 3072)),
        pl.BlockSpec((L, 128), lambda c: (cidx(c), O_DT // 128)),
        _full((8, CONV_C)), _full((1, CONV_C)), _full((1, 128)), _full((1, 128)), _full((1, 128)),
        _full((1, SSM_W)),
    ]


def ssd_fwd(proj, conv_w, conv_b, dt_bias, a_log, d_skip, norm_g):
    S = proj.shape[0]
    nc = S // L

    def body(z_ref, xbc_ref, dt_ref, cw_ref, cb_ref, dtb_ref, alog_ref, dsk_ref, ng_ref,
             y_ref, hs_ref, H, ext, ysc):
        @pl.when(pl.program_id(0) == 0)
        def _():
            H[...] = jnp.zeros_like(H)
            ext[0:8, :] = jnp.zeros((8, CONV_C), F32)

        ext[8:8 + L, :] = xbc_ref[...]
        pre, sg_pre, xc, dt, a, acs = _ssd_common(ext, cw_ref, cb_ref, dt_ref[...], dtb_ref[...], alog_ref[...])
        ext[0:8, :] = xbc_ref[L - 8:L, :]
        xs = xc[:, 0:SSM_W]
        acs_t = acs.T
        ex = _expand_matrix()
        acs_x = _dot_hi(acs, ex)
        dt_x = _dot_hi(dt, ex)
        xdt = xs * dt_x
        eacs_x = jnp.exp(acs_x)
        xw = xdt * jnp.exp(acs_x[L - 1:L, :] - acs_x)
        cd_row = jnp.exp(acs[L - 1:L, :])
        hs_ref[0] = H[...]
        tri = _tril()
        for g in range(SSM_G):
            gs = slice(g * 512, (g + 1) * 512)
            bg = xc[:, SSM_W + g * SSM_N:SSM_W + (g + 1) * SSM_N].astype(BF16)
            cg = xc[:, SSM_W + 512 + g * SSM_N:SSM_W + 512 + (g + 1) * SSM_N].astype(BF16)
            G = _dot_nt(cg, bg)
            yoff = _dot_nt(cg, H[gs, :].astype(BF16)) * eacs_x[:, gs]
            Sg = _dot_tn(xw[:, gs], bg)
            for j in range(8):
                hh = g * 8 + j
                hs = slice(hh * SSM_P, (hh + 1) * SSM_P)
                seg = acs[:, hh:hh + 1] - acs_t[hh:hh + 1, :]
                dk = jnp.where(tri, jnp.exp(jnp.minimum(seg, 0.0)), 0.0)
                yd = _dot((G * dk).astype(BF16), xdt[:, hs].astype(BF16))
                ysc[:, hs] = yd + yoff[:, j * SSM_P:(j + 1) * SSM_P]
                H[hs, :] = H[hs, :] * cd_row[:, hh:hh + 1] + Sg[j * SSM_P:(j + 1) * SSM_P, :]
        d_x = _dot_hi(jnp.broadcast_to(dsk_ref[...], (8, 128)), ex)[0:1, :]
        Y = ysc[...] + d_x * xs
        z = z_ref[...]
        yz = Y * (z * _sigmoid(z))
        ng = ng_ref[...]
        for g in range(SSM_G):
            gs = slice(g * 512, (g + 1) * 512)
            t = yz[:, gs]
            rstd = lax.rsqrt(jnp.mean(t * t, axis=-1, keepdims=True) + EPS)
            y_ref[:, gs] = (t * rstd * ng[:, gs]).astype(BF16)

    return pl.pallas_call(
        body, name="ssd_fwd", grid=(nc,),
        in_specs=_ssd_in_specs(False, nc),
        out_specs=[pl.BlockSpec((L, SSM_W), lambda c: (c, 0)), pl.BlockSpec((1, SSM_W, SSM_N), lambda c: (c, 0, 0))],
        out_shape=[jax.ShapeDtypeStruct((S, SSM_W), BF16), jax.ShapeDtypeStruct((nc, SSM_W, SSM_N), F32)],
        scratch_shapes=[pltpu.VMEM((SSM_W, SSM_N), F32), pltpu.VMEM((8 + L, CONV_C), F32),
                        pltpu.VMEM((L, SSM_W), F32)],
        compiler_params=_params(("arbitrary",)),
    )(proj, proj, proj, conv_w, conv_b, dt_bias, a_log, d_skip, norm_g)


def ssd_bwd(dy, proj, hstates, conv_w, conv_b, dt_bias, a_log, d_skip, norm_g):
    S = proj.shape[0]
    nc = S // L
    cidx = lambda c: nc - 1 - c

    def body(dy_ref, z_ref, xbc_ref, dt_ref, cw_ref, cb_ref, dtb_ref, alog_ref, dsk_ref, ng_ref,
             xprev_ref, hp_ref,
             dz_ref, dxbc_ref, ddt_ref, dcw_ref, dcb_ref, ddtb_ref, dalog_ref, ddsk_ref, dng_ref,
             dH, ext, dext, ysc, yoffsc, dxdt, dxc, tsc):
        step = pl.program_id(0)
        c = nc - 1 - step

        @pl.when(step == 0)
        def _():
            dH[...] = jnp.zeros_like(dH)
            dext[L:L + 8, :] = jnp.zeros((8, CONV_C), F32)
            for r in (dcw_ref, dcb_ref, ddtb_ref, dalog_ref, ddsk_ref, dng_ref):
                r[...] = jnp.zeros_like(r)

        ext[0:8, :] = jnp.where(c > 0, xprev_ref[L - 8:L, :], 0.0)
        ext[8:8 + L, :] = xbc_ref[...]
        dtb = dtb_ref[...]
        dt_raw = dt_ref[...]
        pre, sg_pre, xc, dt, a, acs = _ssd_common(ext, cw_ref, cb_ref, dt_raw, dtb, alog_ref[...])
        xs = xc[:, 0:SSM_W]
        acs_t = acs.T
        ex = _expand_matrix()
        acs_x = _dot_hi(acs, ex)
        dt_x = _dot_hi(dt, ex)
        xdt = xs * dt_x
        eacs_x = jnp.exp(acs_x)
        dte_x = jnp.exp(acs_x[L - 1:L, :] - acs_x)
        xw = xdt * dte_x
        cd_row = jnp.exp(acs[L - 1:L, :])
        tri = _tril()

        Gs, Cs, Bs = [], [], []
        for g in range(SSM_G):
            gs = slice(g * 512, (g + 1) * 512)
            bg = xc[:, SSM_W + g * SSM_N:SSM_W + (g + 1) * SSM_N].astype(BF16)
            cg = xc[:, SSM_W + 512 + g * SSM_N:SSM_W + 512 + (g + 1) * SSM_N].astype(BF16)
            G = _dot_nt(cg, bg)
            Gs.append(G), Cs.append(cg), Bs.append(bg)
            yoffsc[:, gs] = _dot_nt(cg, hp_ref[0, gs, :].astype(BF16)) * eacs_x[:, gs]
            for j in range(8):
                hh = g * 8 + j
                hs = slice(hh * SSM_P, (hh + 1) * SSM_P)
                seg = acs[:, hh:hh + 1] - acs_t[hh:hh + 1, :]
                dk = jnp.where(tri, jnp.exp(jnp.minimum(seg, 0.0)), 0.0)
                ysc[:, hs] = _dot((G * dk).astype(BF16), xdt[:, hs].astype(BF16))
        d_x = _dot_hi(jnp.broadcast_to(dsk_ref[...], (8, 128)), ex)[0:1, :]
        yoff = yoffsc[...]
        Y = ysc[...] + yoff + d_x * xs

        z = z_ref[...]
        sgz = _sigmoid(z)
        silu_z = z * sgz
        yz = Y * silu_z
        ng = ng_ref[...]
        dout = dy_ref[...]
        dyn = dout * ng
        dyz_parts, dng_parts = [], []
        for g in range(SSM_G):
            gs = slice(g * 512, (g + 1) * 512)
            t = yz[:, gs]
            rstd = lax.rsqrt(jnp.mean(t * t, axis=-1, keepdims=True) + EPS)
            dng_parts.append(jnp.sum(dout[:, gs] * t * rstd, axis=0, keepdims=True))
            dn = dyn[:, gs]
            dyz_parts.append(rstd * dn - t * (rstd * rstd * rstd) * jnp.mean(dn * t, axis=-1, keepdims=True))
        dng_ref[...] += jnp.concatenate(dng_parts, axis=1)
        dyz = jnp.concatenate(dyz_parts, axis=1)
        dY = dyz * silu_z
        dz_ref[...] = (dyz * Y * (sgz * (1.0 + z * (1.0 - sgz)))).astype(BF16)

        ex_t = ex.T
        ddsk_ref[...] += _dot_hi(jnp.broadcast_to(jnp.sum(dY * xs, axis=0, keepdims=True), (8, SSM_W)), ex_t)[0:1, :]

        lane = lax.broadcasted_iota(jnp.int32, (L, 128), 1)
        subl = lax.broadcasted_iota(jnp.int32, (128, L), 0)
        coll = lax.broadcasted_iota(jnp.int32, (128, L), 1)
        r_cols = jnp.zeros((L, 128), F32)
        c_rows = jnp.zeros((128, L), F32)
        for g in range(SSM_G):
            gs = slice(g * 512, (g + 1) * 512)
            G, cg, bg = Gs[g], Cs[g], Bs[g]
            hp_g = hp_ref[0, gs, :]
            dh_g = dH[gs, :]
            dY_g = dY[:, gs]
            dZ = dY_g * eacs_x[:, gs]
            dZb = dZ.astype(BF16)
            dC = _dot(dZb, hp_g.astype(BF16))
            dh_from_off = _dot_tn(dZ, cg)
            dhb = dh_g.astype(BF16)
            Q = _dot_nt(bg, dhb)
            dB = _dot(xw[:, gs].astype(BF16), dhb)
            qd = Q * dte_x[:, gs]
            dxdt[:, gs] = qd
            tsc[:, gs] = qd * xdt[:, gs]
            dG = jnp.zeros((L, L), F32)
            for j in range(8):
                hh = g * 8 + j
                hs = slice(hh * SSM_P, (hh + 1) * SSM_P)
                seg = acs[:, hh:hh + 1] - acs_t[hh:hh + 1, :]
                dk = jnp.where(tri, jnp.exp(jnp.minimum(seg, 0.0)), 0.0)
                M = G * dk
                dYh = dY[:, hs]
                dYhb = dYh.astype(BF16)
                dM = _dot_nt(dYhb, xdt[:, hs].astype(BF16))
                dxdt[:, hs] += _dot_tn(M, dYhb)
                dG = dG + dM * dk
                Wm = dM * M
                r_cols = r_cols + jnp.where(lane == hh, jnp.sum(Wm, axis=1, keepdims=True), 0.0)
                c_rows = c_rows + jnp.where(subl == hh, jnp.sum(Wm, axis=0, keepdims=True), 0.0)
                pj = slice(j * SSM_P, (j + 1) * SSM_P)
                cd_h = cd_row[:, hh:hh + 1]
                dcd = jnp.sum(dh_g[pj, :] * hp_g[pj, :]) * cd_h
                c_rows = c_rows - jnp.where((subl == hh) & (coll == L - 1), dcd, 0.0)
                dH[hs, :] = dh_g[pj, :] * cd_h + dh_from_off[pj, :]
            dGb = dG.astype(BF16)
            dC = dC + _dot(dGb, bg)
            dB = dB + _dot_tn(dG, cg)
            dxc[:, SSM_W + g * SSM_N:SSM_W + (g + 1) * SSM_N] = dB
            dxc[:, SSM_W + 512 + g * SSM_N:SSM_W + 512 + (g + 1) * SSM_N] = dC

        row = lax.broadcasted_iota(jnp.int32, (L, 128), 0)
        th = _dot_hi(tsc[...], ex_t)
        dacs = (r_cols - c_rows.T + _dot_hi(dY * yoff, ex_t) - th
                + jnp.where(row == L - 1, jnp.sum(th, axis=0, keepdims=True), 0.0))
        triu = (lax.broadcasted_iota(jnp.int32, (L, L), 0) <= lax.broadcasted_iota(jnp.int32, (L, L), 1)).astype(F32)
        dadt = _dot_hi(triu, dacs)
        dxdt_v = dxdt[...]
        ddt = _dot_hi(dxdt_v * xs, ex_t) + dadt * a
        dalog_ref[...] += jnp.sum(dadt * dt * a, axis=0, keepdims=True)
        ddt_raw = jnp.where(lane < SSM_H, ddt * _sigmoid(dt_raw + dtb), 0.0)
        ddtb_ref[...] += jnp.sum(ddt_raw, axis=0, keepdims=True)
        ddt_ref[...] = ddt_raw.astype(BF16)

        dxc[:, 0:SSM_W] = dxdt_v * dt_x + d_x * dY
        dpre = dxc[...] * (sg_pre * (1.0 + pre * (1.0 - sg_pre)))
        dcb_ref[...] += jnp.sum(dpre, axis=0, keepdims=True)
        for k in range(CONV_K):
            dcw_ref[k:k + 1, :] += jnp.sum(dpre * ext[pl.ds(5 + k, L), :], axis=0, keepdims=True)
        dext[0:L, :] = dpre
        dx = cw_ref[0:1, :] * dext[pl.ds(3, L), :]
        for k in range(1, CONV_K):
            dx = dx + cw_ref[k:k + 1, :] * dext[pl.ds(3 - k, L), :]
        dxbc_ref[...] = dx.astype(BF16)
        dext[L:L + 8, :] = dpre[0:8, :]

    big = lambda w: pl.BlockSpec((L, w), lambda c: (cidx(c), 0))
    return pl.pallas_call(
        body, name="ssd_bwd", grid=(nc,),
        in_specs=[big(SSM_W)] + _ssd_in_specs(True, nc) + [
            pl.BlockSpec((L, 3072), lambda c: (jnp.maximum(cidx(c) - 1, 0), O_XBC // 3072)),
            pl.BlockSpec((1, SSM_W, SSM_N), lambda c: (cidx(c), 0, 0))],
        out_specs=[big(SSM_W), big(CONV_C), big(128), _full((8, CONV_C)), _full((1, CONV_C)),
                   _full((1, 128)), _full((1, 128)), _full((1, 128)), _full((1, SSM_W))],
        out_shape=[jax.ShapeDtypeStruct((S, SSM_W), BF16), jax.ShapeDtypeStruct((S, CONV_C), BF16),
                   jax.ShapeDtypeStruct((S, 128), BF16), jax.ShapeDtypeStruct((8, CONV_C), F32),
                   jax.ShapeDtypeStruct((1, CONV_C), F32), jax.ShapeDtypeStruct((1, 128), F32),
                   jax.ShapeDtypeStruct((1, 128), F32), jax.ShapeDtypeStruct((1, 128), F32),
                   jax.ShapeDtypeStruct((1, SSM_W), F32)],
        scratch_shapes=[pltpu.VMEM((SSM_W, SSM_N), F32), pltpu.VMEM((8 + L, CONV_C), F32),
                        pltpu.VMEM((L + 8, CONV_C), F32), pltpu.VMEM((L, SSM_W), F32),
                        pltpu.VMEM((L, SSM_W), F32), pltpu.VMEM((L, SSM_W), F32),
                        pltpu.VMEM((L, CONV_C), F32), pltpu.VMEM((L, SSM_W), F32)],
        compiler_params=_params(("arbitrary",)),
    )(dy, proj, proj, proj, conv_w, conv_b, dt_bias, a_log, d_skip, norm_g, proj, hstates)


def _resident(shape):
    nd = len(shape)
    return pl.BlockSpec(shape, lambda *_: (0,) * nd, pipeline_mode=pl.Buffered(1))


def merge_fwd(y_att, y_sg, y_ssm, proj, x, w_a, w_s, w_m, w_o, g_post):
    S = x.shape[0]
    tm = 256

    def body(ya_ref, ys_ref, ym_ref, gate_ref, x_ref, wa_ref, ws_ref, wm_ref, wo_ref, gp_ref,
             xn_ref, bra_ref, brs_ref, brm_ref, mg_ref, out_ref):
        bra = _dot(ya_ref[...], wa_ref[...])
        brs = _dot(ys_ref[...], ws_ref[...])
        brm = _dot(ym_ref[...], wm_ref[...])
        bra_ref[...] = bra
        brs_ref[...] = brs
        brm_ref[...] = brm
        merged = (_sigmoid(gate_ref[:, 0:1024]) * bra + _sigmoid(gate_ref[:, 1024:2048]) * brs
                  + _sigmoid(gate_ref[:, 2048:3072]) * brm)
        mb = merged.astype(BF16)
        mg_ref[...] = mb
        o = _dot(mb, wo_ref[...])
        out_ref[...] = o
        r = lax.rsqrt(jnp.mean(o * o, axis=-1, keepdims=True) + EPS)
        xn_ref[...] = x_ref[...] + o * r * gp_ref[...]

    row = lambda w: pl.BlockSpec((tm, w), lambda i: (i, 0))
    return pl.pallas_call(
        body, name="merge_fwd", grid=(S // tm,),
        in_specs=[row(1024), row(1024), row(2048), pl.BlockSpec((tm, 3072), lambda i: (i, O_GATE // 3072)),
                  row(D), _resident((1024, D)), _resident((1024, D)), _resident((2048, D)), _resident((D, D)),
                  _full((1, D))],
        out_specs=[row(D)] * 6,
        out_shape=[jax.ShapeDtypeStruct((S, D), F32)] * 4 + [jax.ShapeDtypeStruct((S, D), BF16),
                                                             jax.ShapeDtypeStruct((S, D), F32)],
        compiler_params=_params(("arbitrary",)),
    )(y_att, y_sg, y_ssm, proj, x, w_a, w_s, w_m, w_o, g_post)


def merge_bwd(dy, out, g_post, proj, br_a, br_s, br_m, w_a, w_s, w_m, w_o):
    S = dy.shape[0]
    tm = 256

    def body(dy_ref, o_ref, gp_ref, gate_ref, bra_ref, brs_ref, brm_ref, wa_ref, ws_ref, wm_ref, wo_ref,
             dout_ref, dba_ref, dbs_ref, dbm_ref, dgate_ref, dya_ref, dys_ref, dym_ref, dgp_ref):
        @pl.when(pl.program_id(0) == 0)
        def _():
            dgp_ref[...] = jnp.zeros_like(dgp_ref)

        o = o_ref[...]
        dyv = dy_ref[...]
        r = lax.rsqrt(jnp.mean(o * o, axis=-1, keepdims=True) + EPS)
        dyg = dyv * gp_ref[...]
        do = r * dyg - o * (r * r * r) * jnp.mean(dyg * o, axis=-1, keepdims=True)
        dgp_ref[...] += jnp.sum(dyv * o * r, axis=0, keepdims=True)
        dob = do.astype(BF16)
        dout_ref[...] = dob
        dmerged = _dot_nt(dob, wo_ref[...])
        for idx, (br_ref, dbr_ref, w_ref, dyi_ref) in enumerate((
                (bra_ref, dba_ref, wa_ref, dya_ref), (brs_ref, dbs_ref, ws_ref, dys_ref),
                (brm_ref, dbm_ref, wm_ref, dym_ref))):
            s = _sigmoid(gate_ref[:, idx * 1024:(idx + 1) * 1024])
            dbr = (dmerged * s).astype(BF16)
            dbr_ref[...] = dbr
            dgate_ref[:, idx * 1024:(idx + 1) * 1024] = (dmerged * br_ref[...] * s * (1.0 - s)).astype(BF16)
            dyi_ref[...] = _dot_nt(dbr, w_ref[...])

    row = lambda w: pl.BlockSpec((tm, w), lambda i: (i, 0))
    return pl.pallas_call(
        body, name="merge_bwd", grid=(S // tm,),
        in_specs=[row(D), row(D), _full((1, D)), pl.BlockSpec((tm, 3072), lambda i: (i, O_GATE // 3072)),
                  row(D), row(D), row(D),
                  _resident((1024, D)), _resident((1024, D)), _resident((2048, D)), _resident((D, D))],
        out_specs=[row(D), row(D), row(D), row(D), row(3072), row(1024), row(1024), row(2048), _full((1, D))],
        out_shape=[jax.ShapeDtypeStruct((S, D), BF16)] * 4 + [
            jax.ShapeDtypeStruct((S, 3072), BF16), jax.ShapeDtypeStruct((S, 1024), F32),
            jax.ShapeDtypeStruct((S, 1024), F32), jax.ShapeDtypeStruct((S, 2048), F32),
            jax.ShapeDtypeStruct((1, D), F32)],
        compiler_params=_params(("arbitrary",)),
    )(dy, out, g_post, proj, br_a, br_s, br_m, w_a, w_s, w_m, w_o)


def loss_head(y, target):
    S = y.shape[0]
    tm = 512

    def body(y_ref, t_ref, dy_ref, loss_ref):
        @pl.when(pl.program_id(0) == 0)
        def _():
            loss_ref[...] = jnp.zeros_like(loss_ref)
        e = y_ref[...] - t_ref[...]
        dy_ref[...] = e * (1.0 / D)
        loss_ref[...] += 0.5 * jnp.sum(jnp.mean(e * e, axis=-1, keepdims=True))

    row = pl.BlockSpec((tm, D), lambda i: (i, 0))
    return pl.pallas_call(
        body, name="loss_head", grid=(S // tm,),
        in_specs=[row, row], out_specs=[row, _full((1, 128))],
        out_shape=[jax.ShapeDtypeStruct((S, D), F32), jax.ShapeDtypeStruct((1, 128), F32)],
        compiler_params=_params(("arbitrary",)),
    )(y, target)


def _adam(w, g, m, v):
    mn = ADAM_B1 * m + (1.0 - ADAM_B1) * g
    vn = ADAM_B2 * v + (1.0 - ADAM_B2) * (g * g)
    m_hat = mn / (1.0 - ADAM_B1 ** ADAM_STEP)
    v_hat = vn / (1.0 - ADAM_B2 ** ADAM_STEP)
    return -ADAM_LR * (m_hat / (jnp.sqrt(v_hat) + ADAM_EPS) + ADAM_WD * w), mn, vn


def adamw_big(w, m, v, f, fb, cc, name, tr, f_row0=0):
    _, R, C = w.shape
    nper = R // tr
    foff = f_row0 // tr

    def body(c_ref, w_ref, m_ref, v_ref, f_ref, fb_ref, g_ref, d_ref, nm_ref, nv_ref):
        layer = pl.program_id(0) // nper
        g = jnp.where(c_ref[0] == layer, f_ref[...], fb_ref[...])
        g_ref[0] = g
        d_ref[0], nm_ref[0], nv_ref[0] = _adam(w_ref[0], g, m_ref[0], v_ref[0])

    wblk = pl.BlockSpec((1, tr, C), lambda i, c: (i // nper, i % nper, 0))
    fblk = pl.BlockSpec((tr, C), lambda i, c: (foff + i % nper, 0))
    grid_spec = pltpu.PrefetchScalarGridSpec(
        num_scalar_prefetch=1, grid=(2 * nper,),
        in_specs=[wblk, wblk, wblk, fblk, fblk], out_specs=[wblk] * 4)
    return pl.pallas_call(
        body, name=name, grid_spec=grid_spec,
        out_shape=[jax.ShapeDtypeStruct(w.shape, F32)] * 4,
        compiler_params=_params(("arbitrary",)),
    )(cc, w, m, v, f, fb)


def adamw_plain(w, g, m, v, name):
    def body(w_ref, g_ref, m_ref, v_ref, d_ref, nm_ref, nv_ref):
        d_ref[...], nm_ref[...], nv_ref[...] = _adam(w_ref[...], g_ref[...], m_ref[...], v_ref[...])

    return pl.pallas_call(
        body, name=name, out_shape=[jax.ShapeDtypeStruct(w.shape, F32)] * 3, compiler_params=_params(),
    )(w, g, m, v)


SMALL = {"norm_pre": ("g_pre", 8), "norm_post": ("g_post", 8), "att_sinks": ("sinks", 8), "sg_ln_g": ("ln_g", 8),
         "sg_ln_b": ("ln_b", 8), "sg_w": ("sg_w", 1024), "sg_b": ("sg_bt", 8), "ssm_conv_b": ("conv_b", 24),
         "ssm_dt_bias": ("dt_bias", 8), "ssm_a_log": ("a_log", 8), "ssm_d": ("d_skip", 8), "ssm_norm_g": ("norm_g", 16)}
SMALL_LAYER_ROWS = sum(r for _, r in SMALL.values())
REL_ROW = DEPTH * SMALL_LAYER_ROWS
LOSS_ROW = REL_ROW + 32
SMALL_ROWS = LOSS_ROW + 8


def _small_rows():
    rows, r = {}, 0
    for l in range(DEPTH):
        for name, (_, n) in SMALL.items():
            rows[(l, name)] = r
            r += n
    return rows


def adamw_small(red, rel, small):
    names = list(SMALL) + ["rel_bias"]
    params = dict(small, rel_bias=rel)
    rows = _small_rows()

    def grad_of(red_ref, l, name, n):
        r0 = rows[(l, name)]
        if name == "sg_b":
            return red_ref[r0:r0 + 8, :]
        if n < 128:
            return red_ref[r0:r0 + 1, 0:n]
        return jnp.concatenate([red_ref[r0 + j:r0 + j + 1, :] for j in range(n // 128)], axis=1)

    def body(red_ref, *refs):
        ins, outs = refs[:3 * len(names)], refs[3 * len(names):]
        for i, name in enumerate(names):
            w_ref, m_ref, v_ref = ins[3 * i:3 * i + 3]
            o = outs[4 * i:4 * i + 4]
            if name == "rel_bias":
                g = red_ref[REL_ROW:REL_ROW + 32, 0:16]
                o[0][...] = g
                o[1][...], o[2][...], o[3][...] = _adam(w_ref[...], g, m_ref[...], v_ref[...])
                continue
            for l in range(DEPTH):
                if name == "sg_w":
                    for grp in range(8):
                        r0 = rows[(l, name)] + grp * 128
                        g = red_ref[r0:r0 + 128, :]
                        o[0][l, grp] = g
                        o[1][l, grp], o[2][l, grp], o[3][l, grp] = _adam(w_ref[l, grp], g, m_ref[l, grp], v_ref[l, grp])
                elif name == "sg_b":
                    g = grad_of(red_ref, l, name, 128)
                    o[0][l] = g
                    o[1][l], o[2][l], o[3][l] = _adam(w_ref[l], g, m_ref[l], v_ref[l])
                else:
                    sl = slice(l, l + 1)
                    g = grad_of(red_ref, l, name, w_ref.shape[-1])
                    o[0][sl, :] = g
                    o[1][sl, :], o[2][sl, :], o[3][sl, :] = _adam(w_ref[sl, :], g, m_ref[sl, :], v_ref[sl, :])

    flat_in = [a for name in names for a in params[name]]
    out_shape = [jax.ShapeDtypeStruct(params[name][0].shape, F32) for name in names for _ in range(4)]
    res = pl.pallas_call(body, name="adamw_small", out_shape=out_shape, compiler_params=_params())(red, *flat_in)
    return {name: tuple(res[4 * i:4 * i + 4]) for i, name in enumerate(names)}


ANY = pl.BlockSpec(memory_space=pl.ANY)


def _place():
    x, y, c = lax.axis_index("x"), lax.axis_index("y"), lax.axis_index("c")
    others = [(1 - x, y), (x, 1 - y), (1 - x, 1 - y)]
    return x, y, c, others


def _rcopy(src, dst, ssem, rsem, to):
    return pltpu.make_async_remote_copy(src_ref=src, dst_ref=dst, send_sem=ssem, recv_sem=rsem,
                                        device_id=to, device_id_type=MESH)


def gather_weights(arrs):
    n = len(arrs)

    def body(*refs):
        srcs, outs, ssem, rsem = refs[:n], refs[n:2 * n], refs[2 * n], refs[2 * n + 1]
        x, y, c, others = _place()
        me = 2 * x + y
        sib = (x, y, 1 - c)
        first = [_rcopy(srcs[i].at[c], outs[i].at[c, me], ssem.at[6 * i + k], rsem.at[6 * i + k], (ox, oy, c))
                 for i in range(n) for k, (ox, oy) in enumerate(others)]
        for cp in first:
            cp.start()
        passed = []
        for k, (ox, oy) in enumerate(others):
            for i in range(n):
                slot = outs[i].at[c, 2 * ox + oy]
                _rcopy(slot, slot, ssem.at[6 * i + k], rsem.at[6 * i + k], sib).wait_recv()
                fw = _rcopy(slot, slot, ssem.at[6 * i + 3 + k], rsem.at[6 * i + 3 + k], sib)
                fw.start()
                passed.append(fw)
        for k, (ox, oy) in enumerate(others):
            for i in range(n):
                slot = outs[i].at[1 - c, 2 * ox + oy]
                _rcopy(slot, slot, ssem.at[6 * i + 3 + k], rsem.at[6 * i + 3 + k], sib).wait_recv()
        for cp in first + passed:
            cp.wait_send()

    return pl.pallas_call(
        body, name="gather_weights",
        in_specs=[ANY] * n, out_specs=[ANY] * n,
        out_shape=[jax.ShapeDtypeStruct((2, SHARDS) + a.shape[1:], a.dtype) for a in arrs],
        scratch_shapes=[pltpu.SemaphoreType.DMA((6 * n,)), pltpu.SemaphoreType.DMA((6 * n,))],
    )(*arrs)


def grad_sibling_exchange(arrs):
    n = len(arrs)

    def body(*refs):
        srcs, outs, ssem, rsem = refs[:n], refs[n:2 * n], refs[2 * n], refs[2 * n + 1]
        x, y, c, _ = _place()
        cps = [_rcopy(srcs[i].at[1 - c], outs[i], ssem.at[i], rsem.at[i], (x, y, 1 - c)) for i in range(n)]
        for cp in cps:
            cp.start()
        for cp in cps:
            cp.wait()

    return pl.pallas_call(
        body, name="grad_sibling_exchange",
        in_specs=[ANY] * n, out_specs=[ANY] * n,
        out_shape=[jax.ShapeDtypeStruct(a.shape[1:], F32) for a in arrs],
        scratch_shapes=[pltpu.SemaphoreType.DMA((n,)), pltpu.SemaphoreType.DMA((n,))],
    )(*arrs)


def grad_chip_sum(g, sb, cc, tr, name):
    _, _, R, C = g.shape
    grid_spec = pltpu.PrefetchScalarGridSpec(
        num_scalar_prefetch=1, grid=(SHARDS, R // tr),
        in_specs=[pl.BlockSpec((1, 1, tr, C), lambda s, r, c: (c[0], s, r, 0)),
                  pl.BlockSpec((1, tr, C), lambda s, r, c: (s, r, 0))],
        out_specs=pl.BlockSpec((1, tr, C), lambda s, r, c: (s, r, 0)))

    def body(c_ref, a_ref, b_ref, o_ref):
        o_ref[...] = a_ref[0] + b_ref[...]

    return pl.pallas_call(
        body, name=name, grid_spec=grid_spec,
        out_shape=jax.ShapeDtypeStruct((SHARDS, R, C), F32),
        compiler_params=_params(("arbitrary", "arbitrary")),
    )(cc, g, sb)


def grad_chip_exchange(arrs):
    n = len(arrs)

    def body(*refs):
        srcs, outs, ssem, rsem = refs[:n], refs[n:2 * n], refs[2 * n], refs[2 * n + 1]
        x, y, c, others = _place()
        me = 2 * x + y
        sends = [_rcopy(srcs[i].at[2 * ox + oy], outs[i].at[me], ssem.at[3 * i + k], rsem.at[3 * i + k], (ox, oy, c))
                 for i in range(n) for k, (ox, oy) in enumerate(others)]
        for cp in sends:
            cp.start()
        for i in range(n):
            for k, (ox, oy) in enumerate(others):
                slot = outs[i].at[2 * ox + oy]
                _rcopy(slot, slot, ssem.at[3 * i + k], rsem.at[3 * i + k], (ox, oy, c)).wait_recv()
        for cp in sends:
            cp.wait_send()

    return pl.pallas_call(
        body, name="grad_chip_exchange",
        in_specs=[ANY] * n, out_specs=[ANY] * n,
        out_shape=[jax.ShapeDtypeStruct(a.shape, F32) for a in arrs],
        scratch_shapes=[pltpu.SemaphoreType.DMA((3 * n,)), pltpu.SemaphoreType.DMA((3 * n,))],
    )(*arrs)


def grad_shard_sum(t, rb, me, tr, name):
    _, R, C = t.shape
    grid_spec = pltpu.PrefetchScalarGridSpec(
        num_scalar_prefetch=1, grid=(R // tr,),
        in_specs=[pl.BlockSpec((1, tr, C), lambda r, m: (m[0], r, 0)),
                  pl.BlockSpec((SHARDS, tr, C), lambda r, m: (0, r, 0))],
        out_specs=pl.BlockSpec((tr, C), lambda r, m: (r, 0)))

    def body(m_ref, t_ref, r_ref, o_ref):
        part = [jnp.where(m_ref[0] == s, t_ref[0], r_ref[s]) for s in range(SHARDS)]
        o_ref[...] = ((part[0] + part[1]) + part[2]) + part[3]

    return pl.pallas_call(
        body, name=name, grid_spec=grid_spec,
        out_shape=jax.ShapeDtypeStruct((R, C), F32),
        compiler_params=_params(("arbitrary",)),
    )(me, t, rb)


def grad_sibling_share(arrs):
    n = len(arrs)

    def body(*refs):
        srcs, outs, ssem, rsem = refs[:n], refs[n:2 * n], refs[2 * n], refs[2 * n + 1]
        x, y, c, _ = _place()
        cps = [_rcopy(srcs[i], outs[i], ssem.at[i], rsem.at[i], (x, y, 1 - c)) for i in range(n)]
        for cp in cps:
            cp.start()
        for cp in cps:
            cp.wait()

    return pl.pallas_call(
        body, name="grad_sibling_share",
        in_specs=[ANY] * n, out_specs=[ANY] * n,
        out_shape=[jax.ShapeDtypeStruct(a.shape, F32) for a in arrs],
        scratch_shapes=[pltpu.SemaphoreType.DMA((n,)), pltpu.SemaphoreType.DMA((n,))],
    )(*arrs)


def _allreduce_rows(src, sib_buf, chips, out_ref, ssem, rsem):
    x, y, c, others = _place()
    me = 2 * x + y
    cp = _rcopy(src, sib_buf, ssem.at[0], rsem.at[0], (x, y, 1 - c))
    cp.start()
    cp.wait()
    chips[me] = src[...] + sib_buf[...]
    sends = [_rcopy(chips.at[me], chips.at[me], ssem.at[1 + k], rsem.at[1 + k], (ox, oy, c))
             for k, (ox, oy) in enumerate(others)]
    for s in sends:
        s.start()
    for k, (ox, oy) in enumerate(others):
        slot = chips.at[2 * ox + oy]
        _rcopy(slot, slot, ssem.at[1 + k], rsem.at[1 + k], (ox, oy, c)).wait_recv()
    for s in sends:
        s.wait_send()
    out_ref[...] = ((chips[0] + chips[1]) + chips[2]) + chips[3]


def _allreduce_scratch(rows):
    return [pltpu.VMEM((rows, 128), F32), pltpu.VMEM((SHARDS, rows, 128), F32),
            pltpu.SemaphoreType.DMA((4,)), pltpu.SemaphoreType.DMA((4,))]


def allreduce_rows(buf, name):
    rows = buf.shape[0]
    VM = pl.BlockSpec(memory_space=pltpu.VMEM)

    def body(src_ref, out_ref, sib_buf, chips, ssem, rsem):
        _allreduce_rows(src_ref, sib_buf, chips, out_ref, ssem, rsem)

    return pl.pallas_call(
        body, name=name, in_specs=[VM], out_specs=VM,
        out_shape=jax.ShapeDtypeStruct((rows, 128), F32),
        scratch_shapes=_allreduce_scratch(rows), compiler_params=_params(),
    )(buf)


def small_allreduce(grads, rel, loss_part):
    rows = _small_rows()
    keys = [(l, name) for l in range(DEPTH) for name in SMALL]
    flat = [grads[l][SMALL[name][0]] for l, name in keys] + [rel, loss_part]

    def body(*refs):
        ins = refs[:len(flat)]
        out_ref, src, sib_buf, chips, ssem, rsem = refs[len(flat):]
        src[...] = jnp.zeros_like(src)
        for (l, name), ref in zip(keys, ins):
            r0 = rows[(l, name)]
            if name == "sg_w":
                for grp in range(8):
                    src[r0 + grp * 128:r0 + (grp + 1) * 128, :] = ref[grp]
            elif name == "sg_b":
                src[r0:r0 + 8, :] = ref[...].T[0:8, :]
            else:
                for j in range(ref.shape[1] // 128):
                    src[r0 + j:r0 + j + 1, :] = ref[:, j * 128:(j + 1) * 128]
        src[REL_ROW:REL_ROW + 32, 0:16] = ins[-2][...]
        src[LOSS_ROW:LOSS_ROW + 1, :] = ins[-1][...]
        _allreduce_rows(src, sib_buf, chips, out_ref, ssem, rsem)

    return pl.pallas_call(
        body, name="small_allreduce",
        out_shape=jax.ShapeDtypeStruct((SMALL_ROWS, 128), F32),
        scratch_shapes=[pltpu.VMEM((SMALL_ROWS, 128), F32)] + _allreduce_scratch(SMALL_ROWS),
        compiler_params=_params(),
    )(*flat)


def _pad_lanes(v):
    return jnp.zeros((1, 128), F32).at[0, :v.shape[0]].set(v)


def layer_fwd(x, wts, bias):
    proj, h = inproj_fwd(x, wts["g_pre"], wts["wp"])
    y_att = att_fwd(proj, bias, wts["sinks"])
    y_sg = sgu_fwd(proj, wts["ln_g"], wts["ln_b"], wts["sg_w"], wts["sg_bt"])
    y_ssm, hst = ssd_fwd(proj, wts["conv_w"], wts["conv_b"], wts["dt_bias"], wts["a_log"], wts["d_skip"],
                         wts["norm_g"])
    x_new, br_a, br_s, br_m, merged, out = merge_fwd(
        y_att, y_sg, y_ssm, proj, x, wts["w_a"], wts["w_s"], wts["w_m"], wts["w_o"], wts["g_post"])
    saved = dict(x=x, proj=proj, h=h, y_att=y_att, y_sg=y_sg, y_ssm=y_ssm, hst=hst,
                 br_a=br_a, br_s=br_s, br_m=br_m, merged=merged, out=out)
    return x_new, saved


def layer_bwd(dy, wts, bias, sv):
    proj = sv["proj"]
    dout, dba, dbs, dbm, dgates, dya, dys, dym, dg_post = merge_bwd(
        dy, sv["out"], wts["g_post"], proj, sv["br_a"], sv["br_s"], sv["br_m"],
        wts["w_a"], wts["w_s"], wts["w_m"], wts["w_o"])
    dq, dza, dk, dv, dbias, dsinks = att_bwd(dya, proj, bias, wts["sinks"])
    dsgu, dsg_w, dsg_bt, dln_g, dln_b = sgu_bwd(dys, proj, wts["ln_g"], wts["ln_b"], wts["sg_w"], wts["sg_bt"])
    dzm, dxbc, ddt, dcw, dcb, ddtb, dalog, ddsk, dng = ssd_bwd(
        dym, proj, sv["hst"], wts["conv_w"], wts["conv_b"], wts["dt_bias"], wts["a_log"], wts["d_skip"],
        wts["norm_g"])
    S = dy.shape[0]
    dproj = jnp.concatenate([dxbc, dgates, dzm, dq, dza, dsgu, dk, dv, ddt,
                             jnp.zeros((S, NCP - O_DT - 128), BF16)], axis=1)
    dx, dg_pre = inproj_bwd(dproj, wts["wp"], sv["x"], wts["g_pre"], dy)
    grads = dict(
        w_in=matmul_tn(sv["h"], dproj, "dw_in", tn=1536),
        w_a=matmul_tn(sv["y_att"], dba, "dw_att"),
        w_s=matmul_tn(sv["y_sg"], dbs, "dw_sg"),
        w_m=matmul_tn(sv["y_ssm"], dbm, "dw_ssm"),
        w_o=matmul_tn(sv["merged"], dout, "dw_out"),
        g_pre=dg_pre, g_post=dg_post, sinks=dsinks, ln_g=dln_g, ln_b=dln_b, sg_w=dsg_w, sg_bt=dsg_bt,
        conv_w=dcw, conv_b=dcb, dt_bias=ddtb, a_log=dalog, d_skip=ddsk, norm_g=dng, bias=dbias)
    return dx, grads


REST_OFF = (0, 256, 512, 1024, 1280)
REST_ROWS = 1288


def kernel(x, w_in, norm_pre, norm_post, rel_bias, att_sinks, sg_ln_g, sg_ln_b, sg_w, sg_b, ssm_conv_w, ssm_conv_b, ssm_dt_bias, ssm_a_log, ssm_d, ssm_norm_g, w_br_att, w_br_sg, w_br_ssm, w_out, loss_target, m_w_in, m_norm_pre, m_norm_post, m_rel_bias, m_att_sinks, m_sg_ln_g, m_sg_ln_b, m_sg_w, m_sg_b, m_ssm_conv_w, m_ssm_conv_b, m_ssm_dt_bias, m_ssm_a_log, m_ssm_d, m_ssm_norm_g, m_w_br_att, m_w_br_sg, m_w_br_ssm, m_w_out, v_w_in, v_norm_pre, v_norm_post, v_rel_bias, v_att_sinks, v_sg_ln_g, v_sg_ln_b, v_sg_w, v_sg_b, v_ssm_conv_w, v_ssm_conv_b, v_ssm_dt_bias, v_ssm_a_log, v_ssm_d, v_ssm_norm_g, v_w_br_att, v_w_br_sg, v_w_br_ssm, v_w_out):
    cx, cy, cc = lax.axis_index("x"), lax.axis_index("y"), lax.axis_index("c")
    me = 2 * cx + cy
    xs = x[0]
    S = xs.shape[0]

    w_in_b = w_in.astype(BF16)
    w_rest_b = jnp.concatenate([w_br_att, w_br_sg, w_br_ssm, w_out], axis=1).astype(BF16)
    all_in, all_rest = gather_weights([w_in_b, w_rest_b])
    convw_slot = jnp.zeros((SHARDS, DEPTH * CONV_K * 768 // 128, 128), F32)
    convw_slot = lax.dynamic_update_index_in_dim(
        convw_slot, jnp.where(cc == 0, 1.0, 0.0) * ssm_conv_w.reshape(-1, 128), me, 0)
    convw_all = allreduce_rows(convw_slot.reshape(-1, 128), "gather_conv_w")
    convw_all = convw_all.reshape(SHARDS, DEPTH, CONV_K, 768).transpose(1, 2, 0, 3).reshape(DEPTH, CONV_K, CONV_C)

    def shards_of(gathered, mine, l, lo, hi):
        return [jnp.where(me == s, mine[l, lo:hi], gathered[l, s, lo:hi]) for s in range(SHARDS)]

    o = REST_OFF
    layers = []
    for l in range(DEPTH):
        w_in_full = jnp.concatenate(shards_of(all_in, w_in_b, l, 0, 1024), axis=1)
        rest = lambda k: jnp.concatenate(shards_of(all_rest, w_rest_b, l, o[k], o[k + 1]), axis=0)
        layers.append(dict(
            wp=to_padded_cols(w_in_full),
            w_a=rest(0), w_s=rest(1), w_m=rest(2), w_o=rest(3),
            g_pre=norm_pre[l][None], g_post=norm_post[l][None], sinks=att_sinks[l],
            ln_g=sg_ln_g[l][None], ln_b=sg_ln_b[l][None], sg_w=sg_w[l],
            sg_bt=sg_b[l].T,
            conv_w=jnp.concatenate([convw_all[l], jnp.zeros((4, CONV_C), F32)], axis=0),
            conv_b=ssm_conv_b[l][None], dt_bias=_pad_lanes(ssm_dt_bias[l]), a_log=_pad_lanes(ssm_a_log[l]),
            d_skip=_pad_lanes(ssm_d[l]), norm_g=ssm_norm_g[l][None]))

    bias = bias_table(rel_bias)
    saved = []
    act = xs
    for l in range(DEPTH):
        act, sv = layer_fwd(act, layers[l], bias)
        saved.append(sv)
    dy, loss_part = loss_head(act, loss_target[0])
    grads = [None] * DEPTH
    for l in reversed(range(DEPTH)):
        dy, grads[l] = layer_bwd(dy, layers[l], bias, saved[l])
    grad_x = dy[None]
    grad_rel_local = bias_grad(grads[0]["bias"] + grads[1]["bias"])

    cvec = jnp.reshape(cc, (1,)).astype(jnp.int32)
    mvec = jnp.reshape(me, (1,)).astype(jnp.int32)
    g_in, g_rest = [], []
    for l in range(DEPTH):
        g = grads[l]
        g_in.append(from_padded_cols(g["w_in"]).reshape(1024, SHARDS, 3400).transpose(1, 0, 2))
        gcw = g["conv_w"][0:CONV_K].reshape(CONV_K, SHARDS, 768).transpose(1, 0, 2).reshape(SHARDS, 3, 1024)
        g_rest.append(jnp.concatenate([
            g["w_a"].reshape(SHARDS, 256, D), g["w_s"].reshape(SHARDS, 256, D), g["w_m"].reshape(SHARDS, 512, D),
            g["w_o"].reshape(SHARDS, 256, D), jnp.pad(gcw, ((0, 0), (0, 5), (0, 0)))], axis=1))
    g_in, g_rest = jnp.stack(g_in), jnp.stack(g_rest)
    sb_in, sb_rest = grad_sibling_exchange([g_in, g_rest])
    t_in = grad_chip_sum(g_in, sb_in, cvec, 128, "chip_sum_w_in")
    t_rest = grad_chip_sum(g_rest, sb_rest, cvec, 184, "chip_sum_rest")
    rb_in, rb_rest = grad_chip_exchange([t_in, t_rest])
    f_in = grad_shard_sum(t_in, rb_in, mvec, 128, "shard_sum_w_in")
    f_rest = grad_shard_sum(t_rest, rb_rest, mvec, 184, "shard_sum_rest")
    fb_in, fb_rest = grad_sibling_share([f_in, f_rest])

    red = small_allreduce(grads, grad_rel_local, loss_part)
    loss = red[LOSS_ROW, 0]

    res = adamw_small(red, (rel_bias, m_rel_bias, v_rel_bias), dict(
        norm_pre=(norm_pre, m_norm_pre, v_norm_pre), norm_post=(norm_post, m_norm_post, v_norm_post),
        att_sinks=(att_sinks, m_att_sinks, v_att_sinks), sg_ln_g=(sg_ln_g, m_sg_ln_g, v_sg_ln_g),
        sg_ln_b=(sg_ln_b, m_sg_ln_b, v_sg_ln_b), sg_w=(sg_w, m_sg_w, v_sg_w), sg_b=(sg_b, m_sg_b, v_sg_b),
        ssm_conv_b=(ssm_conv_b, m_ssm_conv_b, v_ssm_conv_b), ssm_dt_bias=(ssm_dt_bias, m_ssm_dt_bias, v_ssm_dt_bias),
        ssm_a_log=(ssm_a_log, m_ssm_a_log, v_ssm_a_log), ssm_d=(ssm_d, m_ssm_d, v_ssm_d),
        ssm_norm_g=(ssm_norm_g, m_ssm_norm_g, v_ssm_norm_g)))
    res["w_in"] = adamw_big(w_in, m_w_in, v_w_in, f_in, fb_in, cvec, "adamw_w_in", 128)
    res["w_br_att"] = adamw_big(w_br_att, m_w_br_att, v_w_br_att, f_rest, fb_rest, cvec, "adamw_w_br_att", 256, o[0])
    res["w_br_sg"] = adamw_big(w_br_sg, m_w_br_sg, v_w_br_sg, f_rest, fb_rest, cvec, "adamw_w_br_sg", 256, o[1])
    res["w_br_ssm"] = adamw_big(w_br_ssm, m_w_br_ssm, v_w_br_ssm, f_rest, fb_rest, cvec, "adamw_w_br_ssm", 512, o[2])
    res["w_out"] = adamw_big(w_out, m_w_out, v_w_out, f_rest, fb_rest, cvec, "adamw_w_out", 256, o[3])
    cw_mine = f_rest[o[4]:o[4] + 3].reshape(CONV_K, 768)
    cw_sib = fb_rest[o[4]:o[4] + 3].reshape(CONV_K, 768)
    g_conv_w = jnp.stack([jnp.where(cc == l, cw_mine, cw_sib) for l in range(DEPTH)])
    res["ssm_conv_w"] = (g_conv_w,) + tuple(adamw_plain(ssm_conv_w, g_conv_w, m_ssm_conv_w, v_ssm_conv_w, "adamw_conv_w"))

    order = ["w_in", "norm_pre", "norm_post", "rel_bias", "att_sinks", "sg_ln_g", "sg_ln_b", "sg_w", "sg_b",
             "ssm_conv_w", "ssm_conv_b", "ssm_dt_bias", "ssm_a_log", "ssm_d", "ssm_norm_g",
             "w_br_att", "w_br_sg", "w_br_ssm", "w_out"]
    return (loss, grad_x, *[res[n][0] for n in order], *[res[n][1] for n in order],
            *[res[n][2] for n in order], *[res[n][3] for n in order])
```

```python
import functools
import math

import numpy as np
import jax
import jax.numpy as jnp
from jax import lax
from jax.experimental import pallas as pl
from jax.experimental.pallas import tpu as pltpu

F32 = jnp.float32
BF16 = jnp.bfloat16
MESH = pl.DeviceIdType.MESH

D = 1024
DEPTH = 2
EPS = 1e-6
L = 128
HEADS = 16
KV = 2
DH = 64
SSM_W = 2048
SSM_H = 32
SSM_P = 64
SSM_G = 4
SSM_N = 128
CONV_K = 4
CONV_C = 3072
NEG = -1e30
IN_COLS = 13600
NCP = 13824

O_XBC, O_GATE, O_ZM, O_Q, O_ZA, O_U, O_VS, O_ZS, O_K, O_V, O_DT = (
    0, 3072, 6144, 8192, 9216, 10240, 11264, 12288, 13312, 13440, 13568)

ADAM_LR = 0.001
ADAM_B1 = 0.9
ADAM_B2 = 0.999
ADAM_EPS = 1e-08
ADAM_WD = 0.01
ADAM_STEP = 10

VMEM_LIMIT = 56 * 1024 * 1024

PACK_ROWS = 4704
PACK_TILE = 224
SHARDS = 4


def _dot(a, b):
    return jnp.dot(a, b, preferred_element_type=F32)


def _dot_nt(a, b):
    return lax.dot_general(a, b, (((1,), (1,)), ((), ())), preferred_element_type=F32)


def _dot_tn(a_f32, b):
    return jnp.dot(a_f32.T.astype(BF16), b, preferred_element_type=F32)


def _dot_hi(a, b):
    return jnp.dot(a, b, preferred_element_type=F32, precision=lax.Precision.HIGHEST)


def _pieces(x, n):
    out = []
    for _ in range(n - 1):
        p = x.astype(BF16)
        out.append(p)
        x = x - p.astype(F32)
    out.append(x.astype(BF16))
    return out


def _dot_sel(a, sel, n):
    sel = sel.astype(BF16)
    acc = None
    for p in _pieces(a, n):
        t = _dot(p, sel)
        acc = t if acc is None else acc + t
    return acc


def _sel_dot(sel, b, n):
    sel = sel.astype(BF16)
    acc = None
    for p in _pieces(b, n):
        t = _dot(sel, p)
        acc = t if acc is None else acc + t
    return acc


def _sigmoid(x):
    return 1.0 / (1.0 + jnp.exp(-x))


def _softplus(x):
    return jnp.maximum(x, 0.0) + jnp.log(1.0 + jnp.exp(-jnp.abs(x)))


def _params(sem=None, vmem=VMEM_LIMIT):
    kw = dict(vmem_limit_bytes=vmem)
    if sem is not None:
        kw["dimension_semantics"] = sem
    return pltpu.CompilerParams(**kw)


def _full(shape):
    nd = len(shape)
    return pl.BlockSpec(shape, lambda *_: (0,) * nd)


def to_padded_cols(w):
    pad = jnp.zeros(w.shape[:-1] + (NCP - O_DT - 32,), w.dtype)
    return jnp.concatenate([
        w[..., 7424:10496], w[..., 10528:13600], w[..., 5376:7424], w[..., 0:1024],
        w[..., 1280:2304], w[..., 2304:3328], w[..., 3328:4352], w[..., 4352:5376],
        w[..., 1024:1152], w[..., 1152:1280], w[..., 10496:10528], pad], axis=-1)


def from_padded_cols(g):
    return jnp.concatenate([
        g[..., O_Q:O_Q + 1024], g[..., O_K:O_K + 128], g[..., O_V:O_V + 128],
        g[..., O_ZA:O_ZA + 1024], g[..., O_U:O_U + 3072], g[..., O_ZM:O_ZM + 2048],
        g[..., O_XBC:O_XBC + 3072], g[..., O_DT:O_DT + 32], g[..., O_GATE:O_GATE + 3072]], axis=-1)


def _bucket_table():
    qi = np.arange(L)[:, None]
    kj = np.arange(2 * L)[None, :]
    dist = np.maximum(qi + L - kj, 0)
    dist_f = np.maximum(dist, 1).astype(np.float32)
    large = 16 + (np.log(dist_f / np.float32(16)) / np.float32(math.log(128 / 16)) * np.float32(16)).astype(np.int32)
    large = np.minimum(large, 31)
    return np.where(dist < 16, dist, large).astype(np.int32)


def bias_table(rel_bias):
    buckets = jnp.asarray(_bucket_table().reshape(1, L * 2 * L))

    def body(rb_ref, bk_ref, out_ref):
        onehot = (lax.broadcasted_iota(jnp.int32, (32, L * 2 * L), 0) == bk_ref[...]).astype(F32)
        out_ref[...] = lax.dot_general(rb_ref[...], onehot, (((0,), (0,)), ((), ())),
                                       preferred_element_type=F32, precision=lax.Precision.HIGHEST)

    out = pl.pallas_call(
        body, name="bias_table",
        out_shape=jax.ShapeDtypeStruct((HEADS, L * 2 * L), F32),
        compiler_params=_params(),
    )(rel_bias, buckets)
    return out.reshape(HEADS, L, 2 * L)


def bias_grad(dbias):
    buckets = jnp.asarray(_bucket_table().reshape(1, L * 2 * L))

    def body(db_ref, bk_ref, out_ref):
        onehot = (lax.broadcasted_iota(jnp.int32, (32, L * 2 * L), 0) == bk_ref[...]).astype(F32)
        out_ref[...] = lax.dot_general(onehot, db_ref[...], (((1,), (1,)), ((), ())),
                                       preferred_element_type=F32, precision=lax.Precision.HIGHEST)

    return pl.pallas_call(
        body, name="bias_grad",
        out_shape=jax.ShapeDtypeStruct((32, HEADS), F32),
        compiler_params=_params(),
    )(dbias.reshape(HEADS, L * 2 * L), buckets)


def inproj_fwd(x, g_pre, wp):
    S = x.shape[0]
    tm, tn = (1024 if S % 1024 == 0 else 512), 1536

    def body(x_ref, g_ref, w_ref, proj_ref, h_ref):
        @pl.when(pl.program_id(1) == 0)
        def _():
            xv = x_ref[...]
            r = lax.rsqrt(jnp.mean(xv * xv, axis=-1, keepdims=True) + EPS)
            h_ref[...] = (xv * r * g_ref[...]).astype(BF16)
        proj_ref[...] = _dot(h_ref[...], w_ref[...])

    return pl.pallas_call(
        body, name="inproj_fwd", grid=(S // tm, NCP // tn),
        in_specs=[pl.BlockSpec((tm, D), lambda i, j: (i, 0)), _full((1, D)),
                  pl.BlockSpec((D, tn), lambda i, j: (0, j))],
        out_specs=[pl.BlockSpec((tm, tn), lambda i, j: (i, j)), pl.BlockSpec((tm, D), lambda i, j: (i, 0))],
        out_shape=[jax.ShapeDtypeStruct((S, NCP), F32), jax.ShapeDtypeStruct((S, D), BF16)],
        compiler_params=_params(("arbitrary", "arbitrary")),
    )(x, g_pre, wp)


def inproj_bwd(dproj, wp, x, g_pre, dy):
    S = x.shape[0]
    tm, tk = (1024 if S % 1024 == 0 else 512), 1536
    nk = NCP // tk

    def body(dp_ref, w_ref, x_ref, g_ref, dy_ref, dx_ref, dg_ref, acc):
        i, k = pl.program_id(0), pl.program_id(1)

        @pl.when(k == 0)
        def _():
            acc[...] = jnp.zeros_like(acc)

        acc[...] += _dot_nt(dp_ref[...], w_ref[...])

        @pl.when((k == nk - 1) & (i == 0))
        def _():
            dg_ref[...] = jnp.zeros_like(dg_ref)

        @pl.when(k == nk - 1)
        def _():
            xv = x_ref[...]
            dh = acc[...]
            g = g_ref[...]
            r = lax.rsqrt(jnp.mean(xv * xv, axis=-1, keepdims=True) + EPS)
            dhg = dh * g
            dx_ref[...] = dy_ref[...] + r * dhg - xv * (r * r * r) * jnp.mean(dhg * xv, axis=-1, keepdims=True)
            dg_ref[...] += jnp.sum(dh * xv * r, axis=0, keepdims=True)

    return pl.pallas_call(
        body, name="inproj_bwd", grid=(S // tm, nk),
        in_specs=[pl.BlockSpec((tm, tk), lambda i, k: (i, k)), pl.BlockSpec((D, tk), lambda i, k: (0, k)),
                  pl.BlockSpec((tm, D), lambda i, k: (i, 0)), _full((1, D)),
                  pl.BlockSpec((tm, D), lambda i, k: (i, 0))],
        out_specs=[pl.BlockSpec((tm, D), lambda i, k: (i, 0)), _full((1, D))],
        out_shape=[jax.ShapeDtypeStruct((S, D), F32), jax.ShapeDtypeStruct((1, D), F32)],
        scratch_shapes=[pltpu.VMEM((tm, D), F32)],
        compiler_params=_params(("arbitrary", "arbitrary")),
    )(dproj, wp, x, g_pre, dy)


def matmul_tn(a, b, name, tn=512, ts=512):
    S, K = a.shape
    N = b.shape[1]
    ns = S // ts

    def body(a_ref, b_ref, o_ref):
        @pl.when(pl.program_id(1) == 0)
        def _():
            o_ref[...] = jnp.zeros_like(o_ref)
        o_ref[...] += _dot_tn(a_ref[...].astype(F32), b_ref[...])

    return pl.pallas_call(
        body, name=name, grid=(N // tn, ns),
        in_specs=[pl.BlockSpec((ts, K), lambda j, s: (s, 0)), pl.BlockSpec((ts, tn), lambda j, s: (s, j))],
        out_specs=pl.BlockSpec((K, tn), lambda j, s: (0, j)),
        out_shape=jax.ShapeDtypeStruct((K, N), F32),
        compiler_params=_params(("arbitrary", "arbitrary")),
    )(a, b)


def _att_mask(n):
    qi = lax.broadcasted_iota(jnp.int32, (L, 2 * L), 0)
    kj = lax.broadcasted_iota(jnp.int32, (L, 2 * L), 1)
    dist = qi + L - kj
    return (dist >= 0) & (dist < L) & ((kj >= L) | (n > 0))


def _att_in_specs(nb):
    last = nb - 1
    cur = lambda n: jnp.minimum(n, last)
    prev = lambda n: jnp.maximum(jnp.minimum(n, last) - 1, 0)
    return [
        pl.BlockSpec((L, 1024), lambda n: (cur(n), O_Q // 1024)),
        pl.BlockSpec((L, 128), lambda n: (prev(n), O_K // 128)),
        pl.BlockSpec((L, 128), lambda n: (cur(n), O_K // 128)),
        pl.BlockSpec((L, 128), lambda n: (prev(n), O_V // 128)),
        pl.BlockSpec((L, 128), lambda n: (cur(n), O_V // 128)),
        pl.BlockSpec((L, 1024), lambda n: (cur(n), O_ZA // 1024)),
        _full((HEADS, L, 2 * L)),
        pl.BlockSpec(memory_space=pltpu.SMEM),
    ]


def _att_probs(qh, kk, bias_h, mask, sk):
    logits = _dot_nt(qh, kk) + bias_h
    logits = jnp.where(mask, logits, NEG)
    m = jnp.maximum(jnp.max(logits, axis=-1, keepdims=True), sk)
    p = jnp.exp(logits - m)
    es = jnp.exp(sk - m)
    den = jnp.sum(p, axis=-1, keepdims=True) + es
    return p / den, es / den


def att_fwd(proj, bias, sinks):
    S = proj.shape[0]
    nb = S // L

    def body(q_ref, kp_ref, kc_ref, vp_ref, vc_ref, z_ref, bias_ref, s_ref, y_ref, o_scr):
        mask = _att_mask(pl.program_id(0))
        for kv in range(KV):
            sl = slice(kv * DH, (kv + 1) * DH)
            kk = jnp.concatenate([kp_ref[:, sl], kc_ref[:, sl]], axis=0).astype(BF16)
            vv = jnp.concatenate([vp_ref[:, sl], vc_ref[:, sl]], axis=0).astype(BF16)
            for g in range(HEADS // KV):
                h = kv * (HEADS // KV) + g
                hs = slice(h * DH, (h + 1) * DH)
                qh = (q_ref[:, hs] * 0.125).astype(BF16)
                P, _ = _att_probs(qh, kk, bias_ref[h], mask, s_ref[h])
                o_scr[:, hs] = _dot(P.astype(BF16), vv)
        z = z_ref[...]
        y_ref[...] = (o_scr[...] * (z * _sigmoid(z))).astype(BF16)

    return pl.pallas_call(
        body, name="att_fwd", grid=(nb,),
        in_specs=_att_in_specs(nb),
        out_specs=pl.BlockSpec((L, 1024), lambda n: (n, 0)),
        out_shape=jax.ShapeDtypeStruct((S, 1024), BF16),
        scratch_shapes=[pltpu.VMEM((L, 1024), F32)],
        compiler_params=_params(("arbitrary",)),
    )(proj, proj, proj, proj, proj, proj, bias, sinks)


def att_bwd(dy, proj, bias, sinks):
    S = proj.shape[0]
    nb = S // L
    last = nb - 1

    def body(dy_ref, q_ref, kp_ref, kc_ref, vp_ref, vc_ref, z_ref, bias_ref, s_ref,
             dq_ref, dz_ref, dk_ref, dv_ref, dbias_ref, dsink_ref, carry, band, dq_scr, dz_scr):
        n = pl.program_id(0)

        @pl.when(n == 0)
        def _():
            carry[...] = jnp.zeros_like(carry)
            dbias_ref[...] = jnp.zeros_like(dbias_ref)
            dsink_ref[...] = jnp.zeros_like(dsink_ref)

        band[...] = jnp.zeros_like(band)

        @pl.when(n < nb)
        def _():
            mask = _att_mask(n)
            lane = lax.broadcasted_iota(jnp.int32, (1, 128), 1)
            dsink = jnp.zeros((1, 128), F32)
            for kv in range(KV):
                sl = slice(kv * DH, (kv + 1) * DH)
                kk = jnp.concatenate([kp_ref[:, sl], kc_ref[:, sl]], axis=0).astype(BF16)
                vv = jnp.concatenate([vp_ref[:, sl], vc_ref[:, sl]], axis=0).astype(BF16)
                dk_acc = jnp.zeros((2 * L, DH), F32)
                dv_acc = jnp.zeros((2 * L, DH), F32)
                for g in range(HEADS // KV):
                    h = kv * (HEADS // KV) + g
                    hs = slice(h * DH, (h + 1) * DH)
                    qh = (q_ref[:, hs] * 0.125).astype(BF16)
                    P, psink = _att_probs(qh, kk, bias_ref[h], mask, s_ref[h])
                    Pb = P.astype(BF16)
                    zh = z_ref[:, hs]
                    sg = _sigmoid(zh)
                    dyh = dy_ref[:, hs]
                    O = _dot(Pb, vv)
                    dO = dyh * (zh * sg)
                    dz_scr[:, hs] = dyh * O * (sg * (1.0 + zh * (1.0 - sg)))
                    dOb = dO.astype(BF16)
                    dv_acc = dv_acc + _dot_tn(P, dOb)
                    dP = _dot_nt(dOb, vv)
                    delta = jnp.sum(P * dP, axis=-1, keepdims=True)
                    dS = P * (dP - delta)
                    dsink = dsink + jnp.where(lane == h, -jnp.sum(psink * delta), 0.0)
                    dSb = dS.astype(BF16)
                    dq_scr[:, hs] = _dot(dSb, kk) * 0.125
                    dk_acc = dk_acc + _dot_tn(dS, qh)
                    dbias_ref[h] += dS
                band[:, sl] = dk_acc
                band[:, 128 + kv * DH:128 + (kv + 1) * DH] = dv_acc
            dsink_ref[...] += dsink
            dq_ref[...] = dq_scr[...].astype(BF16)
            dz_ref[...] = dz_scr[...].astype(BF16)

        out = carry[...] + band[0:L, :]
        dk_ref[...] = out[:, 0:128].astype(BF16)
        dv_ref[...] = out[:, 128:256].astype(BF16)
        carry[...] = band[L:2 * L, :]

    cur = lambda n: jnp.minimum(n, last)
    lag = lambda n: jnp.maximum(n - 1, 0)
    return pl.pallas_call(
        body, name="att_bwd", grid=(nb + 1,),
        in_specs=[pl.BlockSpec((L, 1024), lambda n: (cur(n), 0))] + _att_in_specs(nb),
        out_specs=[pl.BlockSpec((L, 1024), lambda n: (cur(n), 0)), pl.BlockSpec((L, 1024), lambda n: (cur(n), 0)),
                   pl.BlockSpec((L, 128), lambda n: (lag(n), 0)), pl.BlockSpec((L, 128), lambda n: (lag(n), 0)),
                   _full((HEADS, L, 2 * L)), _full((1, 128))],
        out_shape=[jax.ShapeDtypeStruct((S, 1024), BF16), jax.ShapeDtypeStruct((S, 1024), BF16),
                   jax.ShapeDtypeStruct((S, 128), BF16), jax.ShapeDtypeStruct((S, 128), BF16),
                   jax.ShapeDtypeStruct((HEADS, L, 2 * L), F32), jax.ShapeDtypeStruct((1, 128), F32)],
        scratch_shapes=[pltpu.VMEM((L, 256), F32), pltpu.VMEM((2 * L, 256), F32),
                        pltpu.VMEM((L, 1024), F32), pltpu.VMEM((L, 1024), F32)],
        compiler_params=_params(("arbitrary",)),
    )(dy, proj, proj, proj, proj, proj, proj, bias, sinks)


def _sgu_in_specs():
    return [
        pl.BlockSpec((L, 1024), lambda c: (c, O_U // 1024)),
        pl.BlockSpec((L, 1024), lambda c: (c, O_VS // 1024)),
        pl.BlockSpec((L, 1024), lambda c: (c, O_ZS // 1024)),
        _full((1, 1024)), _full((1, 1024)), _full((8, L, L)), _full((L, 8)),
    ]


def _sgu_norm(v, lg, lb):
    mu = jnp.mean(v, axis=-1, keepdims=True)
    vc = v - mu
    rstd = lax.rsqrt(jnp.mean(vc * vc, axis=-1, keepdims=True) + EPS)
    xhat = vc * rstd
    return xhat * lg + lb, xhat, rstd


def _tril():
    return lax.broadcasted_iota(jnp.int32, (L, L), 0) >= lax.broadcasted_iota(jnp.int32, (L, L), 1)


def sgu_fwd(proj, ln_g, ln_b, w, b_t):
    S = proj.shape[0]

    def body(u_ref, v_ref, z_ref, lg_ref, lb_ref, w_ref, bt_ref, y_ref):
        vn, _, _ = _sgu_norm(v_ref[...], lg_ref[...], lb_ref[...])
        tri = _tril()
        parts = []
        for g in range(8):
            wg = jnp.where(tri, w_ref[g], 0.0).astype(BF16)
            parts.append(_dot(wg, vn[:, g * 128:(g + 1) * 128].astype(BF16)) + bt_ref[:, g:g + 1])
        mixed = jnp.concatenate(parts, axis=1)
        z = z_ref[...]
        y_ref[...] = (u_ref[...] * mixed * (z * _sigmoid(z))).astype(BF16)

    return pl.pallas_call(
        body, name="sgu_fwd", grid=(S // L,),
        in_specs=_sgu_in_specs(),
        out_specs=pl.BlockSpec((L, 1024), lambda c: (c, 0)),
        out_shape=jax.ShapeDtypeStruct((S, 1024), BF16),
        compiler_params=_params(("arbitrary",)),
    )(proj, proj, proj, ln_g, ln_b, w, b_t)


def sgu_bwd(dy, proj, ln_g, ln_b, w, b_t):
    S = proj.shape[0]

    def body(dy_ref, u_ref, v_ref, z_ref, lg_ref, lb_ref, w_ref, bt_ref,
             dout_ref, dw_ref, dbt_ref, dlg_ref, dlb_ref):
        @pl.when(pl.program_id(0) == 0)
        def _():
            dw_ref[...] = jnp.zeros_like(dw_ref)
            dbt_ref[...] = jnp.zeros_like(dbt_ref)
            dlg_ref[...] = jnp.zeros_like(dlg_ref)
            dlb_ref[...] = jnp.zeros_like(dlb_ref)

        lg = lg_ref[...]
        vn, xhat, rstd = _sgu_norm(v_ref[...], lg, lb_ref[...])
        tri = _tril()
        lane = lax.broadcasted_iota(jnp.int32, (L, 128), 1)
        wgs, parts = [], []
        for g in range(8):
            wg = jnp.where(tri, w_ref[g], 0.0)
            wgs.append(wg)
            parts.append(_dot(wg.astype(BF16), vn[:, g * 128:(g + 1) * 128].astype(BF16)) + bt_ref[:, g:g + 1])
        mixed = jnp.concatenate(parts, axis=1)
        z = z_ref[...]
        sg = _sigmoid(z)
        silu = z * sg
        dy_v = dy_ref[...]
        u = u_ref[...]
        dout_ref[:, 0:1024] = (dy_v * mixed * silu).astype(BF16)
        dout_ref[:, 2048:3072] = (dy_v * u * mixed * (sg * (1.0 + z * (1.0 - sg)))).astype(BF16)
        dmixed = dy_v * u * silu
        dbt = jnp.zeros((L, 128), F32)
        dvn_parts = []
        for g in range(8):
            dm = dmixed[:, g * 128:(g + 1) * 128]
            dmb = dm.astype(BF16)
            dbt = dbt + jnp.where(lane == g, jnp.sum(dm, axis=1, keepdims=True), 0.0)
            dw_ref[g] += jnp.where(tri, _dot_nt(dmb, vn[:, g * 128:(g + 1) * 128].astype(BF16)), 0.0)
            dvn_parts.append(_dot_tn(wgs[g], dmb))
        dbt_ref[...] += dbt
        dvn = jnp.concatenate(dvn_parts, axis=1)
        dlg_ref[...] += jnp.sum(dvn * xhat, axis=0, keepdims=True)
        dlb_ref[...] += jnp.sum(dvn, axis=0, keepdims=True)
        dxh = dvn * lg
        dv = rstd * (dxh - jnp.mean(dxh, axis=-1, keepdims=True)
                     - xhat * jnp.mean(dxh * xhat, axis=-1, keepdims=True))
        dout_ref[:, 1024:2048] = dv.astype(BF16)

    return pl.pallas_call(
        body, name="sgu_bwd", grid=(S // L,),
        in_specs=[pl.BlockSpec((L, 1024), lambda c: (c, 0))] + _sgu_in_specs(),
        out_specs=[pl.BlockSpec((L, 3072), lambda c: (c, 0)), _full((8, L, L)), _full((L, 128)),
                   _full((1, 1024)), _full((1, 1024))],
        out_shape=[jax.ShapeDtypeStruct((S, 3072), BF16), jax.ShapeDtypeStruct((8, L, L), F32),
                   jax.ShapeDtypeStruct((L, 128), F32), jax.ShapeDtypeStruct((1, 1024), F32),
                   jax.ShapeDtypeStruct((1, 1024), F32)],
        compiler_params=_params(("arbitrary",)),
    )(dy, proj, proj, proj, ln_g, ln_b, w, b_t)


def _expand_matrix():
    r = lax.broadcasted_iota(jnp.int32, (128, SSM_W), 0)
    c = lax.broadcasted_iota(jnp.int32, (128, SSM_W), 1)
    return (c // SSM_P) == r


def _expand_matrix_t():
    r = lax.broadcasted_iota(jnp.int32, (SSM_W, 128), 0)
    c = lax.broadcasted_iota(jnp.int32, (SSM_W, 128), 1)
    return (r // SSM_P) == c


def _ssd_common(ext_ref, cw_ref, cb_ref, dt_raw, dtb, alog):
    taps = [ext_ref[pl.ds(5 + k, L), :] for k in range(CONV_K)]
    pre = cb_ref[...]
    for k in range(CONV_K):
        pre = pre + cw_ref[k:k + 1, :] * taps[k]
    sg_pre = _sigmoid(pre)
    xc = pre * sg_pre
    dt = _softplus(dt_raw + dtb)
    a = -jnp.exp(alog)
    adt = dt * a
    acs = _sel_dot(_tril(), adt, 3)
    return pre, sg_pre, xc, dt, a, acs, taps


def _ssd_in_specs(rev, nc):
    cidx = (lambda c: nc - 1 - c) if rev else (lambda c: c)
    return [
        pl.BlockSpec((L, 2048), lambda c: (cidx(c), O_ZM // 2048)),
        pl.BlockSpec((L, 3072), lambda c: (cidx(c), O_XBC // 3072)),
        pl.BlockSpec((L, 128), lambda c: (cidx(c), O_DT // 128)),
        _full((8, CONV_C)), _full((1, CONV_C)), _full((1, 128)), _full((1, 128)), _full((1, 128)),
        _full((1, SSM_W)),
    ]


def ssd_fwd(proj, conv_w, conv_b, dt_bias, a_log, d_skip, norm_g):
    S = proj.shape[0]
    nc = S // L

    def body(z_ref, xbc_ref, dt_ref, cw_ref, cb_ref, dtb_ref, alog_ref, dsk_ref, ng_ref,
             y_ref, hs_ref, H, ext, ysc):
        @pl.when(pl.program_id(0) == 0)
        def _():
            H[...] = jnp.zeros_like(H)
            ext[0:8, :] = jnp.zeros((8, CONV_C), F32)

        ext[8:8 + L, :] = xbc_ref[...]
        pre, sg_pre, xc, dt, a, acs, _ = _ssd_common(ext, cw_ref, cb_ref, dt_ref[...], dtb_ref[...], alog_ref[...])
        ext[0:8, :] = xbc_ref[L - 8:L, :]
        xs = xc[:, 0:SSM_W]
        acs_t = acs.T
        ex = _expand_matrix()
        acs_x = _dot_sel(acs, ex, 3)
        dt_x = _dot_sel(dt, ex, 2)
        xdt = xs * dt_x
        eacs_x = jnp.exp(acs_x)
        xw = xdt * jnp.exp(acs_x[L - 1:L, :] - acs_x)
        cd_row = jnp.exp(acs[L - 1:L, :])
        hs_ref[0] = H[...]
        tri = _tril()
        for g in range(SSM_G):
            gs = slice(g * 512, (g + 1) * 512)
            bg = xc[:, SSM_W + g * SSM_N:SSM_W + (g + 1) * SSM_N].astype(BF16)
            cg = xc[:, SSM_W + 512 + g * SSM_N:SSM_W + 512 + (g + 1) * SSM_N].astype(BF16)
            G = _dot_nt(cg, bg)
            yoff = _dot_nt(cg, H[gs, :].astype(BF16)) * eacs_x[:, gs]
            Sg = _dot_tn(xw[:, gs], bg)
            for j in range(8):
                hh = g * 8 + j
                hs = slice(hh * SSM_P, (hh + 1) * SSM_P)
                seg = acs[:, hh:hh + 1] - acs_t[hh:hh + 1, :]
                dk = jnp.where(tri, jnp.exp(jnp.minimum(seg, 0.0)), 0.0)
                yd = _dot((G * dk).astype(BF16), xdt[:, hs].astype(BF16))
                ysc[:, hs] = yd + yoff[:, j * SSM_P:(j + 1) * SSM_P]
                H[hs, :] = H[hs, :] * cd_row[:, hh:hh + 1] + Sg[j * SSM_P:(j + 1) * SSM_P, :]
        d_x = _dot_sel(jnp.broadcast_to(dsk_ref[...], (8, 128)), ex, 3)[0:1, :]
        Y = ysc[...] + d_x * xs
        z = z_ref[...]
        yz = Y * (z * _sigmoid(z))
        ng = ng_ref[...]
        for g in range(SSM_G):
            gs = slice(g * 512, (g + 1) * 512)
            t = yz[:, gs]
            rstd = lax.rsqrt(jnp.mean(t * t, axis=-1, keepdims=True) + EPS)
            y_ref[:, gs] = (t * rstd * ng[:, gs]).astype(BF16)

    return pl.pallas_call(
        body, name="ssd_fwd", grid=(nc,),
        in_specs=_ssd_in_specs(False, nc),
        out_specs=[pl.BlockSpec((L, SSM_W), lambda c: (c, 0)), pl.BlockSpec((1, SSM_W, SSM_N), lambda c: (c, 0, 0))],
        out_shape=[jax.ShapeDtypeStruct((S, SSM_W), BF16), jax.ShapeDtypeStruct((nc, SSM_W, SSM_N), F32)],
        scratch_shapes=[pltpu.VMEM((SSM_W, SSM_N), F32), pltpu.VMEM((8 + L, CONV_C), F32),
                        pltpu.VMEM((L, SSM_W), F32)],
        compiler_params=_params(("arbitrary",)),
    )(proj, proj, proj, conv_w, conv_b, dt_bias, a_log, d_skip, norm_g)


def ssd_bwd(dy, proj, hstates, conv_w, conv_b, dt_bias, a_log, d_skip, norm_g):
    S = proj.shape[0]
    nc = S // L
    cidx = lambda c: nc - 1 - c

    def body(dy_ref, z_ref, xbc_ref, dt_ref, cw_ref, cb_ref, dtb_ref, alog_ref, dsk_ref, ng_ref,
             xprev_ref, hp_ref,
             dz_ref, dxbc_ref, ddt_ref, dcw_ref, dcb_ref, ddtb_ref, dalog_ref, ddsk_ref, dng_ref,
             dH, ext, dext, ysc, yoffsc, dxdt, dxc, tsc):
        step = pl.program_id(0)
        c = nc - 1 - step

        @pl.when(step == 0)
        def _():
            dH[...] = jnp.zeros_like(dH)
            dext[L:L + 8, :] = jnp.zeros((8, CONV_C), F32)
            for r in (dcw_ref, dcb_ref, ddtb_ref, dalog_ref, ddsk_ref, dng_ref):
                r[...] = jnp.zeros_like(r)

        ext[0:8, :] = jnp.where(c > 0, xprev_ref[L - 8:L, :], 0.0)
        ext[8:8 + L, :] = xbc_ref[...]
        dtb = dtb_ref[...]
        dt_raw = dt_ref[...]
        pre, sg_pre, xc, dt, a, acs, taps = _ssd_common(ext, cw_ref, cb_ref, dt_raw, dtb, alog_ref[...])
        xs = xc[:, 0:SSM_W]
        acs_t = acs.T
        ex = _expand_matrix()
        acs_x = _dot_sel(acs, ex, 3)
        dt_x = _dot_sel(dt, ex, 2)
        xdt = xs * dt_x
        eacs_x = jnp.exp(acs_x)
        dte_x = jnp.exp(acs_x[L - 1:L, :] - acs_x)
        xw = xdt * dte_x
        cd_row = jnp.exp(acs[L - 1:L, :])
        tri = _tril()

        Gs, Cs, Bs = [], [], []
        for g in range(SSM_G):
            gs = slice(g * 512, (g + 1) * 512)
            bg = xc[:, SSM_W + g * SSM_N:SSM_W + (g + 1) * SSM_N].astype(BF16)
            cg = xc[:, SSM_W + 512 + g * SSM_N:SSM_W + 512 + (g + 1) * SSM_N].astype(BF16)
            G = _dot_nt(cg, bg)
            Gs.append(G), Cs.append(cg), Bs.append(bg)
            yoffsc[:, gs] = _dot_nt(cg, hp_ref[0, gs, :].astype(BF16)) * eacs_x[:, gs]
            for j in range(8):
                hh = g * 8 + j
                hs = slice(hh * SSM_P, (hh + 1) * SSM_P)
                seg = acs[:, hh:hh + 1] - acs_t[hh:hh + 1, :]
                dk = jnp.where(tri, jnp.exp(jnp.minimum(seg, 0.0)), 0.0)
                ysc[:, hs] = _dot((G * dk).astype(BF16), xdt[:, hs].astype(BF16))
        d_x = _dot_sel(jnp.broadcast_to(dsk_ref[...], (8, 128)), ex, 3)[0:1, :]
        yoff = yoffsc[...]
        Y = ysc[...] + yoff + d_x * xs

        z = z_ref[...]
        sgz = _sigmoid(z)
        silu_z = z * sgz
        yz = Y * silu_z
        ng = ng_ref[...]
        dout = dy_ref[...]
        dyn = dout * ng
        dyz_parts, dng_parts = [], []
        for g in range(SSM_G):
            gs = slice(g * 512, (g + 1) * 512)
            t = yz[:, gs]
            rstd = lax.rsqrt(jnp.mean(t * t, axis=-1, keepdims=True) + EPS)
            dng_parts.append(jnp.sum(dout[:, gs] * t * rstd, axis=0, keepdims=True))
            dn = dyn[:, gs]
            dyz_parts.append(rstd * dn - t * (rstd * rstd * rstd) * jnp.mean(dn * t, axis=-1, keepdims=True))
        dng_ref[...] += jnp.concatenate(dng_parts, axis=1)
        dyz = jnp.concatenate(dyz_parts, axis=1)
        dY = dyz * silu_z
        dz_ref[...] = (dyz * Y * (sgz * (1.0 + z * (1.0 - sgz)))).astype(BF16)

        ex_t = _expand_matrix_t()
        ddsk_ref[...] += _dot_sel(jnp.broadcast_to(jnp.sum(dY * xs, axis=0, keepdims=True), (8, SSM_W)), ex_t, 3)[0:1, :]

        lane = lax.broadcasted_iota(jnp.int32, (L, 128), 1)
        subl = lax.broadcasted_iota(jnp.int32, (128, L), 0)
        coll = lax.broadcasted_iota(jnp.int32, (128, L), 1)
        r_cols = jnp.zeros((L, 128), F32)
        c_rows = jnp.zeros((128, L), F32)
        for g in range(SSM_G):
            gs = slice(g * 512, (g + 1) * 512)
            G, cg, bg = Gs[g], Cs[g], Bs[g]
            hp_g = hp_ref[0, gs, :]
            dh_g = dH[gs, :]
            dY_g = dY[:, gs]
            dZ = dY_g * eacs_x[:, gs]
            dZb = dZ.astype(BF16)
            dC = _dot(dZb, hp_g.astype(BF16))
            dh_from_off = _dot_tn(dZ, cg)
            dhb = dh_g.astype(BF16)
            Q = _dot_nt(bg, dhb)
            dB = _dot(xw[:, gs].astype(BF16), dhb)
            qd = Q * dte_x[:, gs]
            dxdt[:, gs] = qd
            tsc[:, gs] = qd * xdt[:, gs]
            dG = jnp.zeros((L, L), F32)
            for j in range(8):
                hh = g * 8 + j
                hs = slice(hh * SSM_P, (hh + 1) * SSM_P)
                seg = acs[:, hh:hh + 1] - acs_t[hh:hh + 1, :]
                dk = jnp.where(tri, jnp.exp(jnp.minimum(seg, 0.0)), 0.0)
                M = G * dk
                dYh = dY[:, hs]
                dYhb = dYh.astype(BF16)
                dM = _dot_nt(dYhb, xdt[:, hs].astype(BF16))
                dxdt[:, hs] += _dot_tn(M, dYhb)
                dG = dG + dM * dk
                Wm = dM * M
                r_cols = r_cols + jnp.where(lane == hh, jnp.sum(Wm, axis=1, keepdims=True), 0.0)
                c_rows = c_rows + jnp.where(subl == hh, jnp.sum(Wm, axis=0, keepdims=True), 0.0)
                pj = slice(j * SSM_P, (j + 1) * SSM_P)
                cd_h = cd_row[:, hh:hh + 1]
                dcd = jnp.sum(dh_g[pj, :] * hp_g[pj, :]) * cd_h
                c_rows = c_rows - jnp.where((subl == hh) & (coll == L - 1), dcd, 0.0)
                dH[hs, :] = dh_g[pj, :] * cd_h + dh_from_off[pj, :]
            dGb = dG.astype(BF16)
            dC = dC + _dot(dGb, bg)
            dB = dB + _dot_tn(dG, cg)
            dxc[:, SSM_W + g * SSM_N:SSM_W + (g + 1) * SSM_N] = dB
            dxc[:, SSM_W + 512 + g * SSM_N:SSM_W + 512 + (g + 1) * SSM_N] = dC

        row = lax.broadcasted_iota(jnp.int32, (L, 128), 0)
        tv = tsc[...]
        t_last = _dot_sel(jnp.broadcast_to(jnp.sum(tv, axis=0, keepdims=True), (8, SSM_W)), ex_t, 3)[0:1, :]
        dacs = (r_cols - c_rows.T + _dot_sel(dY * yoff - tv, ex_t, 2) + jnp.where(row == L - 1, t_last, 0.0))
        triu = lax.broadcasted_iota(jnp.int32, (L, L), 0) <= lax.broadcasted_iota(jnp.int32, (L, L), 1)
        dadt = _sel_dot(triu, dacs, 3)
        dxdt_v = dxdt[...]
        ddt = _dot_sel(dxdt_v * xs, ex_t, 2) + dadt * a
        dalog_ref[...] += jnp.sum(dadt * dt * a, axis=0, keepdims=True)
        ddt_raw = jnp.where(lane < SSM_H, ddt * _sigmoid(dt_raw + dtb), 0.0)
        ddtb_ref[...] += jnp.sum(ddt_raw, axis=0, keepdims=True)
        ddt_ref[...] = ddt_raw.astype(BF16)

        dxc[:, 0:SSM_W] = dxdt_v * dt_x + d_x * dY
        dpre = dxc[...] * (sg_pre * (1.0 + pre * (1.0 - sg_pre)))
        dcb_ref[...] += jnp.sum(dpre, axis=0, keepdims=True)
        for k in range(CONV_K):
            dcw_ref[k:k + 1, :] += jnp.sum(dpre * taps[k], axis=0, keepdims=True)
        dext[0:L, :] = dpre
        dx = cw_ref[0:1, :] * dext[pl.ds(3, L), :]
        for k in range(1, CONV_K):
            dx = dx + cw_ref[k:k + 1, :] * dext[pl.ds(3 - k, L), :]
        dxbc_ref[...] = dx.astype(BF16)
        dext[L:L + 8, :] = dpre[0:8, :]

    big = lambda w: pl.BlockSpec((L, w), lambda c: (cidx(c), 0))
    return pl.pallas_call(
        body, name="ssd_bwd", grid=(nc,),
        in_specs=[big(SSM_W)] + _ssd_in_specs(True, nc) + [
            pl.BlockSpec((L, 3072), lambda c: (jnp.maximum(cidx(c) - 1, 0), O_XBC // 3072)),
            pl.BlockSpec((1, SSM_W, SSM_N), lambda c: (cidx(c), 0, 0))],
        out_specs=[big(SSM_W), big(CONV_C), big(128), _full((8, CONV_C)), _full((1, CONV_C)),
                   _full((1, 128)), _full((1, 128)), _full((1, 128)), _full((1, SSM_W))],
        out_shape=[jax.ShapeDtypeStruct((S, SSM_W), BF16), jax.ShapeDtypeStruct((S, CONV_C), BF16),
                   jax.ShapeDtypeStruct((S, 128), BF16), jax.ShapeDtypeStruct((8, CONV_C), F32),
                   jax.ShapeDtypeStruct((1, CONV_C), F32), jax.ShapeDtypeStruct((1, 128), F32),
                   jax.ShapeDtypeStruct((1, 128), F32), jax.ShapeDtypeStruct((1, 128), F32),
                   jax.ShapeDtypeStruct((1, SSM_W), F32)],
        scratch_shapes=[pltpu.VMEM((SSM_W, SSM_N), F32), pltpu.VMEM((8 + L, CONV_C), F32),
                        pltpu.VMEM((L + 8, CONV_C), F32), pltpu.VMEM((L, SSM_W), F32),
                        pltpu.VMEM((L, SSM_W), F32), pltpu.VMEM((L, SSM_W), F32),
                        pltpu.VMEM((L, CONV_C), F32), pltpu.VMEM((L, SSM_W), F32)],
        compiler_params=_params(("arbitrary",)),
    )(dy, proj, proj, proj, conv_w, conv_b, dt_bias, a_log, d_skip, norm_g, proj, hstates)


def _resident(shape):
    nd = len(shape)
    return pl.BlockSpec(shape, lambda *_: (0,) * nd, pipeline_mode=pl.Buffered(1))


def merge_fwd(y_att, y_sg, y_ssm, proj, x, w_a, w_s, w_m, w_o, g_post):
    S = x.shape[0]
    tm = 256

    def body(ya_ref, ys_ref, ym_ref, gate_ref, x_ref, wa_ref, ws_ref, wm_ref, wo_ref, gp_ref,
             xn_ref, bra_ref, brs_ref, brm_ref, mg_ref, out_ref):
        bra = _dot(ya_ref[...], wa_ref[...])
        brs = _dot(ys_ref[...], ws_ref[...])
        brm = _dot(ym_ref[...], wm_ref[...])
        bra_ref[...] = bra
        brs_ref[...] = brs
        brm_ref[...] = brm
        merged = (_sigmoid(gate_ref[:, 0:1024]) * bra + _sigmoid(gate_ref[:, 1024:2048]) * brs
                  + _sigmoid(gate_ref[:, 2048:3072]) * brm)
        mb = merged.astype(BF16)
        mg_ref[...] = mb
        o = _dot(mb, wo_ref[...])
        out_ref[...] = o
        r = lax.rsqrt(jnp.mean(o * o, axis=-1, keepdims=True) + EPS)
        xn_ref[...] = x_ref[...] + o * r * gp_ref[...]

    row = lambda w: pl.BlockSpec((tm, w), lambda i: (i, 0))
    return pl.pallas_call(
        body, name="merge_fwd", grid=(S // tm,),
        in_specs=[row(1024), row(1024), row(2048), pl.BlockSpec((tm, 3072), lambda i: (i, O_GATE // 3072)),
                  row(D), _resident((1024, D)), _resident((1024, D)), _resident((2048, D)), _resident((D, D)),
                  _full((1, D))],
        out_specs=[row(D)] * 6,
        out_shape=[jax.ShapeDtypeStruct((S, D), F32)] * 4 + [jax.ShapeDtypeStruct((S, D), BF16),
                                                             jax.ShapeDtypeStruct((S, D), F32)],
        compiler_params=_params(("arbitrary",)),
    )(y_att, y_sg, y_ssm, proj, x, w_a, w_s, w_m, w_o, g_post)


def merge_bwd(dy, out, g_post, proj, br_a, br_s, br_m, w_a, w_s, w_m, w_o):
    S = dy.shape[0]
    tm = 256

    def body(dy_ref, o_ref, gp_ref, gate_ref, bra_ref, brs_ref, brm_ref, wa_ref, ws_ref, wm_ref, wo_ref,
             dout_ref, dba_ref, dbs_ref, dbm_ref, dgate_ref, dya_ref, dys_ref, dym_ref, dgp_ref):
        @pl.when(pl.program_id(0) == 0)
        def _():
            dgp_ref[...] = jnp.zeros_like(dgp_ref)

        o = o_ref[...]
        dyv = dy_ref[...]
        r = lax.rsqrt(jnp.mean(o * o, axis=-1, keepdims=True) + EPS)
        dyg = dyv * gp_ref[...]
        do = r * dyg - o * (r * r * r) * jnp.mean(dyg * o, axis=-1, keepdims=True)
        dgp_ref[...] += jnp.sum(dyv * o * r, axis=0, keepdims=True)
        dob = do.astype(BF16)
        dout_ref[...] = dob
        dmerged = _dot_nt(dob, wo_ref[...])
        for idx, (br_ref, dbr_ref, w_ref, dyi_ref) in enumerate((
                (bra_ref, dba_ref, wa_ref, dya_ref), (brs_ref, dbs_ref, ws_ref, dys_ref),
                (brm_ref, dbm_ref, wm_ref, dym_ref))):
            s = _sigmoid(gate_ref[:, idx * 1024:(idx + 1) * 1024])
            dbr = (dmerged * s).astype(BF16)
            dbr_ref[...] = dbr
            dgate_ref[:, idx * 1024:(idx + 1) * 1024] = (dmerged * br_ref[...] * s * (1.0 - s)).astype(BF16)
            dyi_ref[...] = _dot_nt(dbr, w_ref[...])

    row = lambda w: pl.BlockSpec((tm, w), lambda i: (i, 0))
    return pl.pallas_call(
        body, name="merge_bwd", grid=(S // tm,),
        in_specs=[row(D), row(D), _full((1, D)), pl.BlockSpec((tm, 3072), lambda i: (i, O_GATE // 3072)),
                  row(D), row(D), row(D),
                  _resident((1024, D)), _resident((1024, D)), _resident((2048, D)), _resident((D, D))],
        out_specs=[row(D), row(D), row(D), row(D), row(3072), row(1024), row(1024), row(2048), _full((1, D))],
        out_shape=[jax.ShapeDtypeStruct((S, D), BF16)] * 4 + [
            jax.ShapeDtypeStruct((S, 3072), BF16), jax.ShapeDtypeStruct((S, 1024), F32),
            jax.ShapeDtypeStruct((S, 1024), F32), jax.ShapeDtypeStruct((S, 2048), F32),
            jax.ShapeDtypeStruct((1, D), F32)],
        compiler_params=_params(("arbitrary",)),
    )(dy, out, g_post, proj, br_a, br_s, br_m, w_a, w_s, w_m, w_o)


def loss_head(y, target):
    S = y.shape[0]
    tm = 512

    def body(y_ref, t_ref, dy_ref, loss_ref):
        @pl.when(pl.program_id(0) == 0)
        def _():
            loss_ref[...] = jnp.zeros_like(loss_ref)
        e = y_ref[...] - t_ref[...]
        dy_ref[...] = e * (1.0 / D)
        loss_ref[...] += 0.5 * jnp.sum(jnp.mean(e * e, axis=-1, keepdims=True))

    row = pl.BlockSpec((tm, D), lambda i: (i, 0))
    return pl.pallas_call(
        body, name="loss_head", grid=(S // tm,),
        in_specs=[row, row], out_specs=[row, _full((1, 128))],
        out_shape=[jax.ShapeDtypeStruct((S, D), F32), jax.ShapeDtypeStruct((1, 128), F32)],
        compiler_params=_params(("arbitrary",)),
    )(y, target)


def _adam(w, g, m, v):
    mn = ADAM_B1 * m + (1.0 - ADAM_B1) * g
    vn = ADAM_B2 * v + (1.0 - ADAM_B2) * (g * g)
    m_hat = mn / (1.0 - ADAM_B1 ** ADAM_STEP)
    v_hat = vn / (1.0 - ADAM_B2 ** ADAM_STEP)
    return -ADAM_LR * (m_hat / (jnp.sqrt(v_hat) + ADAM_EPS) + ADAM_WD * w), mn, vn


def adamw_big(w, m, v, f, fb, cc, name, tr, f_row0=0):
    _, R, C = w.shape
    nper = R // tr
    foff = f_row0 // tr

    def body(c_ref, w_ref, m_ref, v_ref, f_ref, fb_ref, g_ref, d_ref, nm_ref, nv_ref):
        layer = pl.program_id(0) // nper
        g = jnp.where(c_ref[0] == layer, f_ref[...], fb_ref[...])
        g_ref[0] = g
        d_ref[0], nm_ref[0], nv_ref[0] = _adam(w_ref[0], g, m_ref[0], v_ref[0])

    wblk = pl.BlockSpec((1, tr, C), lambda i, c: (i // nper, i % nper, 0))
    fblk = pl.BlockSpec((tr, C), lambda i, c: (foff + i % nper, 0))
    grid_spec = pltpu.PrefetchScalarGridSpec(
        num_scalar_prefetch=1, grid=(2 * nper,),
        in_specs=[wblk, wblk, wblk, fblk, fblk], out_specs=[wblk] * 4)
    return pl.pallas_call(
        body, name=name, grid_spec=grid_spec,
        out_shape=[jax.ShapeDtypeStruct(w.shape, F32)] * 4,
        compiler_params=_params(("arbitrary",)),
    )(cc, w, m, v, f, fb)


def adamw_plain(w, g, m, v, name):
    def body(w_ref, g_ref, m_ref, v_ref, d_ref, nm_ref, nv_ref):
        d_ref[...], nm_ref[...], nv_ref[...] = _adam(w_ref[...], g_ref[...], m_ref[...], v_ref[...])

    return pl.pallas_call(
        body, name=name, out_shape=[jax.ShapeDtypeStruct(w.shape, F32)] * 3, compiler_params=_params(),
    )(w, g, m, v)


SMALL = {"norm_pre": ("g_pre", 8), "norm_post": ("g_post", 8), "att_sinks": ("sinks", 8), "sg_ln_g": ("ln_g", 8),
         "sg_ln_b": ("ln_b", 8), "sg_w": ("sg_w", 1024), "sg_b": ("sg_bt", 8), "ssm_conv_b": ("conv_b", 24),
         "ssm_dt_bias": ("dt_bias", 8), "ssm_a_log": ("a_log", 8), "ssm_d": ("d_skip", 8), "ssm_norm_g": ("norm_g", 16)}
SMALL_LAYER_ROWS = sum(r for _, r in SMALL.values())
REL_ROW = DEPTH * SMALL_LAYER_ROWS
LOSS_ROW = REL_ROW + 32
SMALL_ROWS = LOSS_ROW + 8


def _small_rows():
    rows, r = {}, 0
    for l in range(DEPTH):
        for name, (_, n) in SMALL.items():
            rows[(l, name)] = r
            r += n
    return rows


def adamw_small(red, rel, small):
    names = list(SMALL) + ["rel_bias"]
    params = dict(small, rel_bias=rel)
    rows = _small_rows()

    def grad_of(red_ref, l, name, n):
        r0 = rows[(l, name)]
        if name == "sg_b":
            return red_ref[r0:r0 + 8, :]
        if n < 128:
            return red_ref[r0:r0 + 1, 0:n]
        return jnp.concatenate([red_ref[r0 + j:r0 + j + 1, :] for j in range(n // 128)], axis=1)

    def body(red_ref, *refs):
        ins, outs = refs[:3 * len(names)], refs[3 * len(names):]
        for i, name in enumerate(names):
            w_ref, m_ref, v_ref = ins[3 * i:3 * i + 3]
            o = outs[4 * i:4 * i + 4]
            if name == "rel_bias":
                g = red_ref[REL_ROW:REL_ROW + 32, 0:16]
                o[0][...] = g
                o[1][...], o[2][...], o[3][...] = _adam(w_ref[...], g, m_ref[...], v_ref[...])
                continue
            for l in range(DEPTH):
                if name == "sg_w":
                    for grp in range(8):
                        r0 = rows[(l, name)] + grp * 128
                        g = red_ref[r0:r0 + 128, :]
                        o[0][l, grp] = g
                        o[1][l, grp], o[2][l, grp], o[3][l, grp] = _adam(w_ref[l, grp], g, m_ref[l, grp], v_ref[l, grp])
                elif name == "sg_b":
                    g = grad_of(red_ref, l, name, 128)
                    o[0][l] = g
                    o[1][l], o[2][l], o[3][l] = _adam(w_ref[l], g, m_ref[l], v_ref[l])
                else:
                    sl = slice(l, l + 1)
                    g = grad_of(red_ref, l, name, w_ref.shape[-1])
                    o[0][sl, :] = g
                    o[1][sl, :], o[2][sl, :], o[3][sl, :] = _adam(w_ref[sl, :], g, m_ref[sl, :], v_ref[sl, :])

    flat_in = [a for name in names for a in params[name]]
    out_shape = [jax.ShapeDtypeStruct(params[name][0].shape, F32) for name in names for _ in range(4)]
    res = pl.pallas_call(body, name="adamw_small", out_shape=out_shape, compiler_params=_params())(red, *flat_in)
    return {name: tuple(res[4 * i:4 * i + 4]) for i, name in enumerate(names)}


ANY = pl.BlockSpec(memory_space=pl.ANY)


def _place():
    x, y, c = lax.axis_index("x"), lax.axis_index("y"), lax.axis_index("c")
    others = [(1 - x, y), (x, 1 - y), (1 - x, 1 - y)]
    return x, y, c, others


def _rcopy(src, dst, ssem, rsem, to):
    return pltpu.make_async_remote_copy(src_ref=src, dst_ref=dst, send_sem=ssem, recv_sem=rsem,
                                        device_id=to, device_id_type=MESH)


def gather_weights(arrs):
    n = len(arrs)

    def body(*refs):
        srcs, outs, ssem, rsem = refs[:n], refs[n:2 * n], refs[2 * n], refs[2 * n + 1]
        x, y, c, others = _place()
        me = 2 * x + y
        sib = (x, y, 1 - c)
        first = [_rcopy(srcs[i].at[c], outs[i].at[c, me], ssem.at[6 * i + k], rsem.at[6 * i + k], (ox, oy, c))
                 for i in range(n) for k, (ox, oy) in enumerate(others)]
        for cp in first:
            cp.start()
        passed = []
        for k, (ox, oy) in enumerate(others):
            for i in range(n):
                slot = outs[i].at[c, 2 * ox + oy]
                _rcopy(slot, slot, ssem.at[6 * i + k], rsem.at[6 * i + k], sib).wait_recv()
                fw = _rcopy(slot, slot, ssem.at[6 * i + 3 + k], rsem.at[6 * i + 3 + k], sib)
                fw.start()
                passed.append(fw)
        for k, (ox, oy) in enumerate(others):
            for i in range(n):
                slot = outs[i].at[1 - c, 2 * ox + oy]
                _rcopy(slot, slot, ssem.at[6 * i + 3 + k], rsem.at[6 * i + 3 + k], sib).wait_recv()
        for cp in first + passed:
            cp.wait_send()

    return pl.pallas_call(
        body, name="gather_weights",
        in_specs=[ANY] * n, out_specs=[ANY] * n,
        out_shape=[jax.ShapeDtypeStruct((2, SHARDS) + a.shape[1:], a.dtype) for a in arrs],
        scratch_shapes=[pltpu.SemaphoreType.DMA((6 * n,)), pltpu.SemaphoreType.DMA((6 * n,))],
    )(*arrs)


def grad_sibling_exchange(arrs):
    n = len(arrs)

    def body(*refs):
        srcs, outs, ssem, rsem = refs[:n], refs[n:2 * n], refs[2 * n], refs[2 * n + 1]
        x, y, c, _ = _place()
        cps = [_rcopy(srcs[i].at[1 - c], outs[i], ssem.at[i], rsem.at[i], (x, y, 1 - c)) for i in range(n)]
        for cp in cps:
            cp.start()
        for cp in cps:
            cp.wait()

    return pl.pallas_call(
        body, name="grad_sibling_exchange",
        in_specs=[ANY] * n, out_specs=[ANY] * n,
        out_shape=[jax.ShapeDtypeStruct(a.shape[1:], F32) for a in arrs],
        scratch_shapes=[pltpu.SemaphoreType.DMA((n,)), pltpu.SemaphoreType.DMA((n,))],
    )(*arrs)


def grad_chip_sum(g, sb, cc, tr, name):
    _, _, R, C = g.shape
    blk = pl.BlockSpec((1, tr, C), lambda s, r, c: (s, r, 0))
    grid_spec = pltpu.PrefetchScalarGridSpec(
        num_scalar_prefetch=1, grid=(SHARDS, R // tr),
        in_specs=[pl.BlockSpec((1, 1, tr, C), lambda s, r, c: (c[0], s, r, 0)), blk],
        out_specs=[blk, blk])

    def body(c_ref, a_ref, b_ref, o_ref, ob_ref):
        t = a_ref[0] + b_ref[...]
        o_ref[...] = t
        ob_ref[...] = t.astype(BF16)

    return pl.pallas_call(
        body, name=name, grid_spec=grid_spec,
        out_shape=[jax.ShapeDtypeStruct((SHARDS, R, C), F32), jax.ShapeDtypeStruct((SHARDS, R, C), BF16)],
        compiler_params=_params(("arbitrary", "arbitrary")),
    )(cc, g, sb)


def grad_chip_exchange(arrs):
    n = len(arrs)

    def body(*refs):
        srcs, outs, ssem, rsem = refs[:n], refs[n:2 * n], refs[2 * n], refs[2 * n + 1]
        x, y, c, others = _place()
        me = 2 * x + y
        sends = [_rcopy(srcs[i].at[2 * ox + oy], outs[i].at[me], ssem.at[3 * i + k], rsem.at[3 * i + k], (ox, oy, c))
                 for i in range(n) for k, (ox, oy) in enumerate(others)]
        for cp in sends:
            cp.start()
        for i in range(n):
            for k, (ox, oy) in enumerate(others):
                slot = outs[i].at[2 * ox + oy]
                _rcopy(slot, slot, ssem.at[3 * i + k], rsem.at[3 * i + k], (ox, oy, c)).wait_recv()
        for cp in sends:
            cp.wait_send()

    return pl.pallas_call(
        body, name="grad_chip_exchange",
        in_specs=[ANY] * n, out_specs=[ANY] * n,
        out_shape=[jax.ShapeDtypeStruct(a.shape, a.dtype) for a in arrs],
        scratch_shapes=[pltpu.SemaphoreType.DMA((3 * n,)), pltpu.SemaphoreType.DMA((3 * n,))],
    )(*arrs)


def grad_shard_sum(t, rb, me, tr, name):
    _, R, C = t.shape
    grid_spec = pltpu.PrefetchScalarGridSpec(
        num_scalar_prefetch=1, grid=(R // tr,),
        in_specs=[pl.BlockSpec((1, tr, C), lambda r, m: (m[0], r, 0)),
                  pl.BlockSpec((SHARDS, tr, C), lambda r, m: (0, r, 0))],
        out_specs=pl.BlockSpec((tr, C), lambda r, m: (r, 0)))

    def body(m_ref, t_ref, r_ref, o_ref):
        part = [jnp.where(m_ref[0] == s, t_ref[0], r_ref[s].astype(F32)) for s in range(SHARDS)]
        o_ref[...] = ((part[0] + part[1]) + part[2]) + part[3]

    return pl.pallas_call(
        body, name=name, grid_spec=grid_spec,
        out_shape=jax.ShapeDtypeStruct((R, C), F32),
        compiler_params=_params(("arbitrary",)),
    )(me, t, rb)


def grad_sibling_share(arrs):
    n = len(arrs)

    def body(*refs):
        srcs, outs, ssem, rsem = refs[:n], refs[n:2 * n], refs[2 * n], refs[2 * n + 1]
        x, y, c, _ = _place()
        cps = [_rcopy(srcs[i], outs[i], ssem.at[i], rsem.at[i], (x, y, 1 - c)) for i in range(n)]
        for cp in cps:
            cp.start()
        for cp in cps:
            cp.wait()

    return pl.pallas_call(
        body, name="grad_sibling_share",
        in_specs=[ANY] * n, out_specs=[ANY] * n,
        out_shape=[jax.ShapeDtypeStruct(a.shape, F32) for a in arrs],
        scratch_shapes=[pltpu.SemaphoreType.DMA((n,)), pltpu.SemaphoreType.DMA((n,))],
    )(*arrs)


def _allreduce_rows(src, sib_buf, chips, out_ref, ssem, rsem):
    x, y, c, others = _place()
    me = 2 * x + y
    cp = _rcopy(src, sib_buf, ssem.at[0], rsem.at[0], (x, y, 1 - c))
    cp.start()
    cp.wait()
    chips[me] = src[...] + sib_buf[...]
    sends = [_rcopy(chips.at[me], chips.at[me], ssem.at[1 + k], rsem.at[1 + k], (ox, oy, c))
             for k, (ox, oy) in enumerate(others)]
    for s in sends:
        s.start()
    for k, (ox, oy) in enumerate(others):
        slot = chips.at[2 * ox + oy]
        _rcopy(slot, slot, ssem.at[1 + k], rsem.at[1 + k], (ox, oy, c)).wait_recv()
    for s in sends:
        s.wait_send()
    out_ref[...] = ((chips[0] + chips[1]) + chips[2]) + chips[3]


def _allreduce_scratch(rows):
    return [pltpu.VMEM((rows, 128), F32), pltpu.VMEM((SHARDS, rows, 128), F32),
            pltpu.SemaphoreType.DMA((4,)), pltpu.SemaphoreType.DMA((4,))]


def allreduce_rows(buf, name):
    rows = buf.shape[0]
    VM = pl.BlockSpec(memory_space=pltpu.VMEM)

    def body(src_ref, out_ref, sib_buf, chips, ssem, rsem):
        _allreduce_rows(src_ref, sib_buf, chips, out_ref, ssem, rsem)

    return pl.pallas_call(
        body, name=name, in_specs=[VM], out_specs=VM,
        out_shape=jax.ShapeDtypeStruct((rows, 128), F32),
        scratch_shapes=_allreduce_scratch(rows), compiler_params=_params(),
    )(buf)


def small_allreduce(grads, rel, loss_part):
    rows = _small_rows()
    keys = [(l, name) for l in range(DEPTH) for name in SMALL]
    flat = [grads[l][SMALL[name][0]] for l, name in keys] + [rel, loss_part]

    def body(*refs):
        ins = refs[:len(flat)]
        out_ref, src, sib_buf, chips, ssem, rsem = refs[len(flat):]
        src[...] = jnp.zeros_like(src)
        for (l, name), ref in zip(keys, ins):
            r0 = rows[(l, name)]
            if name == "sg_w":
                for grp in range(8):
                    src[r0 + grp * 128:r0 + (grp + 1) * 128, :] = ref[grp]
            elif name == "sg_b":
                src[r0:r0 + 8, :] = ref[...].T[0:8, :]
            else:
                for j in range(ref.shape[1] // 128):
                    src[r0 + j:r0 + j + 1, :] = ref[:, j * 128:(j + 1) * 128]
        src[REL_ROW:REL_ROW + 32, 0:16] = ins[-2][...]
        src[LOSS_ROW:LOSS_ROW + 1, :] = ins[-1][...]
        _allreduce_rows(src, sib_buf, chips, out_ref, ssem, rsem)

    return pl.pallas_call(
        body, name="small_allreduce",
        out_shape=jax.ShapeDtypeStruct((SMALL_ROWS, 128), F32),
        scratch_shapes=[pltpu.VMEM((SMALL_ROWS, 128), F32)] + _allreduce_scratch(SMALL_ROWS),
        compiler_params=_params(),
    )(*flat)


def _pad_lanes(v):
    return jnp.zeros((1, 128), F32).at[0, :v.shape[0]].set(v)


def layer_fwd(x, wts, bias):
    proj, h = inproj_fwd(x, wts["g_pre"], wts["wp"])
    y_att = att_fwd(proj, bias, wts["sinks"])
    y_sg = sgu_fwd(proj, wts["ln_g"], wts["ln_b"], wts["sg_w"], wts["sg_bt"])
    y_ssm, hst = ssd_fwd(proj, wts["conv_w"], wts["conv_b"], wts["dt_bias"], wts["a_log"], wts["d_skip"],
                         wts["norm_g"])
    x_new, br_a, br_s, br_m, merged, out = merge_fwd(
        y_att, y_sg, y_ssm, proj, x, wts["w_a"], wts["w_s"], wts["w_m"], wts["w_o"], wts["g_post"])
    saved = dict(x=x, proj=proj, h=h, y_att=y_att, y_sg=y_sg, y_ssm=y_ssm, hst=hst,
                 br_a=br_a, br_s=br_s, br_m=br_m, merged=merged, out=out)
    return x_new, saved


def layer_bwd(dy, wts, bias, sv):
    proj = sv["proj"]
    dout, dba, dbs, dbm, dgates, dya, dys, dym, dg_post = merge_bwd(
        dy, sv["out"], wts["g_post"], proj, sv["br_a"], sv["br_s"], sv["br_m"],
        wts["w_a"], wts["w_s"], wts["w_m"], wts["w_o"])
    dq, dza, dk, dv, dbias, dsinks = att_bwd(dya, proj, bias, wts["sinks"])
    dsgu, dsg_w, dsg_bt, dln_g, dln_b = sgu_bwd(dys, proj, wts["ln_g"], wts["ln_b"], wts["sg_w"], wts["sg_bt"])
    dzm, dxbc, ddt, dcw, dcb, ddtb, dalog, ddsk, dng = ssd_bwd(
        dym, proj, sv["hst"], wts["conv_w"], wts["conv_b"], wts["dt_bias"], wts["a_log"], wts["d_skip"],
        wts["norm_g"])
    S = dy.shape[0]
    dproj = jnp.concatenate([dxbc, dgates, dzm, dq, dza, dsgu, dk, dv, ddt,
                             jnp.zeros((S, NCP - O_DT - 128), BF16)], axis=1)
    dx, dg_pre = inproj_bwd(dproj, wts["wp"], sv["x"], wts["g_pre"], dy)
    grads = dict(
        w_in=matmul_tn(sv["h"], dproj, "dw_in", tn=1536),
        w_a=matmul_tn(sv["y_att"], dba, "dw_att"),
        w_s=matmul_tn(sv["y_sg"], dbs, "dw_sg"),
        w_m=matmul_tn(sv["y_ssm"], dbm, "dw_ssm"),
        w_o=matmul_tn(sv["merged"], dout, "dw_out"),
        g_pre=dg_pre, g_post=dg_post, sinks=dsinks, ln_g=dln_g, ln_b=dln_b, sg_w=dsg_w, sg_bt=dsg_bt,
        conv_w=dcw, conv_b=dcb, dt_bias=ddtb, a_log=dalog, d_skip=ddsk, norm_g=dng, bias=dbias)
    return dx, grads


REST_OFF = (0, 256, 512, 1024, 1280)
REST_ROWS = 1296


def kernel(x, w_in, norm_pre, norm_post, rel_bias, att_sinks, sg_ln_g, sg_ln_b, sg_w, sg_b, ssm_conv_w, ssm_conv_b, ssm_dt_bias, ssm_a_log, ssm_d, ssm_norm_g, w_br_att, w_br_sg, w_br_ssm, w_out, loss_target, m_w_in, m_norm_pre, m_norm_post, m_rel_bias, m_att_sinks, m_sg_ln_g, m_sg_ln_b, m_sg_w, m_sg_b, m_ssm_conv_w, m_ssm_conv_b, m_ssm_dt_bias, m_ssm_a_log, m_ssm_d, m_ssm_norm_g, m_w_br_att, m_w_br_sg, m_w_br_ssm, m_w_out, v_w_in, v_norm_pre, v_norm_post, v_rel_bias, v_att_sinks, v_sg_ln_g, v_sg_ln_b, v_sg_w, v_sg_b, v_ssm_conv_w, v_ssm_conv_b, v_ssm_dt_bias, v_ssm_a_log, v_ssm_d, v_ssm_norm_g, v_w_br_att, v_w_br_sg, v_w_br_ssm, v_w_out):
    cx, cy, cc = lax.axis_index("x"), lax.axis_index("y"), lax.axis_index("c")
    me = 2 * cx + cy
    xs = x[0]
    S = xs.shape[0]

    w_in_b = w_in.astype(BF16)
    w_rest_b = jnp.concatenate([w_br_att, w_br_sg, w_br_ssm, w_out], axis=1).astype(BF16)
    all_in, all_rest = gather_weights([w_in_b, w_rest_b])
    convw_slot = jnp.zeros((SHARDS, DEPTH * CONV_K * 768 // 128, 128), F32)
    convw_slot = lax.dynamic_update_index_in_dim(
        convw_slot, jnp.where(cc == 0, 1.0, 0.0) * ssm_conv_w.reshape(-1, 128), me, 0)
    convw_all = allreduce_rows(convw_slot.reshape(-1, 128), "gather_conv_w")
    convw_all = convw_all.reshape(SHARDS, DEPTH, CONV_K, 768).transpose(1, 2, 0, 3).reshape(DEPTH, CONV_K, CONV_C)

    def shards_of(gathered, mine, l, lo, hi):
        return [jnp.where(me == s, mine[l, lo:hi], gathered[l, s, lo:hi]) for s in range(SHARDS)]

    o = REST_OFF
    layers = []
    for l in range(DEPTH):
        w_in_full = jnp.concatenate(shards_of(all_in, w_in_b, l, 0, 1024), axis=1)
        rest = lambda k: jnp.concatenate(shards_of(all_rest, w_rest_b, l, o[k], o[k + 1]), axis=0)
        layers.append(dict(
            wp=to_padded_cols(w_in_full),
            w_a=rest(0), w_s=rest(1), w_m=rest(2), w_o=rest(3),
            g_pre=norm_pre[l][None], g_post=norm_post[l][None], sinks=att_sinks[l],
            ln_g=sg_ln_g[l][None], ln_b=sg_ln_b[l][None], sg_w=sg_w[l],
            sg_bt=sg_b[l].T,
            conv_w=jnp.concatenate([convw_all[l], jnp.zeros((4, CONV_C), F32)], axis=0),
            conv_b=ssm_conv_b[l][None], dt_bias=_pad_lanes(ssm_dt_bias[l]), a_log=_pad_lanes(ssm_a_log[l]),
            d_skip=_pad_lanes(ssm_d[l]), norm_g=ssm_norm_g[l][None]))

    bias = bias_table(rel_bias)
    saved = []
    act = xs
    for l in range(DEPTH):
        act, sv = layer_fwd(act, layers[l], bias)
        saved.append(sv)
    dy, loss_part = loss_head(act, loss_target[0])
    grads = [None] * DEPTH
    for l in reversed(range(DEPTH)):
        dy, grads[l] = layer_bwd(dy, layers[l], bias, saved[l])
    grad_x = dy[None]
    grad_rel_local = bias_grad(grads[0]["bias"] + grads[1]["bias"])

    cvec = jnp.reshape(cc, (1,)).astype(jnp.int32)
    mvec = jnp.reshape(me, (1,)).astype(jnp.int32)
    g_in, g_rest = [], []
    for l in range(DEPTH):
        g = grads[l]
        g_in.append(from_padded_cols(g["w_in"]).reshape(1024, SHARDS, 3400).transpose(1, 0, 2))
        gcw = g["conv_w"][0:CONV_K].reshape(CONV_K, SHARDS, 768).transpose(1, 0, 2).reshape(SHARDS, 3, 1024)
        g_rest.append(jnp.concatenate([
            g["w_a"].reshape(SHARDS, 256, D), g["w_s"].reshape(SHARDS, 256, D), g["w_m"].reshape(SHARDS, 512, D),
            g["w_o"].reshape(SHARDS, 256, D), jnp.pad(gcw, ((0, 0), (0, REST_ROWS - REST_OFF[4] - 3), (0, 0)))],
            axis=1))
    g_in, g_rest = jnp.stack(g_in), jnp.stack(g_rest)
    sb_in, sb_rest = grad_sibling_exchange([g_in, g_rest])
    t_in, t_in_b = grad_chip_sum(g_in, sb_in, cvec, 128, "chip_sum_w_in")
    t_rest, t_rest_b = grad_chip_sum(g_rest, sb_rest, cvec, 432, "chip_sum_rest")
    rb_in, rb_rest = grad_chip_exchange([t_in_b, t_rest_b])
    f_in = grad_shard_sum(t_in, rb_in, mvec, 128, "shard_sum_w_in")
    f_rest = grad_shard_sum(t_rest, rb_rest, mvec, 432, "shard_sum_rest")
    fb_in, fb_rest = grad_sibling_share([f_in, f_rest])

    red = small_allreduce(grads, grad_rel_local, loss_part)
    loss = red[LOSS_ROW, 0]

    res = adamw_small(red, (rel_bias, m_rel_bias, v_rel_bias), dict(
        norm_pre=(norm_pre, m_norm_pre, v_norm_pre), norm_post=(norm_post, m_norm_post, v_norm_post),
        att_sinks=(att_sinks, m_att_sinks, v_att_sinks), sg_ln_g=(sg_ln_g, m_sg_ln_g, v_sg_ln_g),
        sg_ln_b=(sg_ln_b, m_sg_ln_b, v_sg_ln_b), sg_w=(sg_w, m_sg_w, v_sg_w), sg_b=(sg_b, m_sg_b, v_sg_b),
        ssm_conv_b=(ssm_conv_b, m_ssm_conv_b, v_ssm_conv_b), ssm_dt_bias=(ssm_dt_bias, m_ssm_dt_bias, v_ssm_dt_bias),
        ssm_a_log=(ssm_a_log, m_ssm_a_log, v_ssm_a_log), ssm_d=(ssm_d, m_ssm_d, v_ssm_d),
        ssm_norm_g=(ssm_norm_g, m_ssm_norm_g, v_ssm_norm_g)))
    res["w_in"] = adamw_big(w_in, m_w_in, v_w_in, f_in, fb_in, cvec, "adamw_w_in", 128)
    res["w_br_att"] = adamw_big(w_br_att, m_w_br_att, v_w_br_att, f_rest, fb_rest, cvec, "adamw_w_br_att", 256, o[0])
    res["w_br_sg"] = adamw_big(w_br_sg, m_w_br_sg, v_w_br_sg, f_rest, fb_rest, cvec, "adamw_w_br_sg", 256, o[1])
    res["w_br_ssm"] = adamw_big(w_br_ssm, m_w_br_ssm, v_w_br_ssm, f_rest, fb_rest, cvec, "adamw_w_br_ssm", 512, o[2])
    res["w_out"] = adamw_big(w_out, m_w_out, v_w_out, f_rest, fb_rest, cvec, "adamw_w_out", 256, o[3])
    cw_mine = f_rest[o[4]:o[4] + 3].reshape(CONV_K, 768)
    cw_sib = fb_rest[o[4]:o[4] + 3].reshape(CONV_K, 768)
    g_conv_w = jnp.stack([jnp.where(cc == l, cw_mine, cw_sib) for l in range(DEPTH)])
    res["ssm_conv_w"] = (g_conv_w,) + tuple(adamw_plain(ssm_conv_w, g_conv_w, m_ssm_conv_w, v_ssm_conv_w, "adamw_conv_w"))

    order = ["w_in", "norm_pre", "norm_post", "rel_bias", "att_sinks", "sg_ln_g", "sg_ln_b", "sg_w", "sg_b",
             "ssm_conv_w", "ssm_conv_b", "ssm_dt_bias", "ssm_a_log", "ssm_d", "ssm_norm_g",
             "w_br_att", "w_br_sg", "w_br_ssm", "w_out"]
    return (loss, grad_x, *[res[n][0] for n in order], *[res[n][1] for n in order],
            *[res[n][2] for n in order], *[res[n][3] for n in order])
```

```python
import functools
import math

import numpy as np
import jax
import jax.numpy as jnp
from jax import lax
from jax.experimental import pallas as pl
from jax.experimental.pallas import tpu as pltpu

F32 = jnp.float32
BF16 = jnp.bfloat16
MESH = pl.DeviceIdType.MESH

D = 1024
DEPTH = 2
EPS = 1e-6
L = 128
HEADS = 16
KV = 2
DH = 64
SSM_W = 2048
SSM_H = 32
SSM_P = 64
SSM_G = 4
SSM_N = 128
CONV_K = 4
CONV_C = 3072
NEG = -1e30
IN_COLS = 13600
NCP = 13824

O_XBC, O_GATE, O_ZM, O_Q, O_ZA, O_U, O_VS, O_ZS, O_K, O_V, O_DT = (
    0, 3072, 6144, 8192, 9216, 10240, 11264, 12288, 13312, 13440, 13568)

ADAM_LR = 0.001
ADAM_B1 = 0.9
ADAM_B2 = 0.999
ADAM_EPS = 1e-08
ADAM_WD = 0.01
ADAM_STEP = 10

VMEM_LIMIT = 56 * 1024 * 1024

PACK_ROWS = 4704
PACK_TILE = 224
SHARDS = 4


def _dot(a, b):
    return jnp.dot(a, b, preferred_element_type=F32)


def _dot_nt(a, b):
    return lax.dot_general(a, b, (((1,), (1,)), ((), ())), preferred_element_type=F32)


def _dot_tn(a_f32, b):
    return jnp.dot(a_f32.T.astype(BF16), b, preferred_element_type=F32)


def _dot_hi(a, b):
    return jnp.dot(a, b, preferred_element_type=F32, precision=lax.Precision.HIGHEST)


def _pieces(x, n):
    out = []
    for _ in range(n - 1):
        p = x.astype(BF16)
        out.append(p)
        x = x - p.astype(F32)
    out.append(x.astype(BF16))
    return out


def _dot_sel(a, sel, n):
    sel = sel.astype(BF16)
    acc = None
    for p in _pieces(a, n):
        t = _dot(p, sel)
        acc = t if acc is None else acc + t
    return acc


def _sel_dot(sel, b, n):
    sel = sel.astype(BF16)
    acc = None
    for p in _pieces(b, n):
        t = _dot(sel, p)
        acc = t if acc is None else acc + t
    return acc


def _sigmoid(x):
    return 1.0 / (1.0 + jnp.exp(-x))


def _softplus(x):
    return jnp.maximum(x, 0.0) + jnp.log(1.0 + jnp.exp(-jnp.abs(x)))


def _params(sem=None, vmem=VMEM_LIMIT):
    kw = dict(vmem_limit_bytes=vmem)
    if sem is not None:
        kw["dimension_semantics"] = sem
    return pltpu.CompilerParams(**kw)


def _full(shape):
    nd = len(shape)
    return pl.BlockSpec(shape, lambda *_: (0,) * nd)


def to_padded_cols(w):
    pad = jnp.zeros(w.shape[:-1] + (NCP - O_DT - 32,), w.dtype)
    return jnp.concatenate([
        w[..., 7424:10496], w[..., 10528:13600], w[..., 5376:7424], w[..., 0:1024],
        w[..., 1280:2304], w[..., 2304:3328], w[..., 3328:4352], w[..., 4352:5376],
        w[..., 1024:1152], w[..., 1152:1280], w[..., 10496:10528], pad], axis=-1)


def from_padded_cols(g):
    return jnp.concatenate([
        g[..., O_Q:O_Q + 1024], g[..., O_K:O_K + 128], g[..., O_V:O_V + 128],
        g[..., O_ZA:O_ZA + 1024], g[..., O_U:O_U + 3072], g[..., O_ZM:O_ZM + 2048],
        g[..., O_XBC:O_XBC + 3072], g[..., O_DT:O_DT + 32], g[..., O_GATE:O_GATE + 3072]], axis=-1)


def _bucket_table():
    qi = np.arange(L)[:, None]
    kj = np.arange(2 * L)[None, :]
    dist = np.maximum(qi + L - kj, 0)
    dist_f = np.maximum(dist, 1).astype(np.float32)
    large = 16 + (np.log(dist_f / np.float32(16)) / np.float32(math.log(128 / 16)) * np.float32(16)).astype(np.int32)
    large = np.minimum(large, 31)
    return np.where(dist < 16, dist, large).astype(np.int32)


def bias_table(rel_bias):
    buckets = jnp.asarray(_bucket_table().reshape(1, L * 2 * L))

    def body(rb_ref, bk_ref, out_ref):
        onehot = (lax.broadcasted_iota(jnp.int32, (32, L * 2 * L), 0) == bk_ref[...]).astype(F32)
        out_ref[...] = lax.dot_general(rb_ref[...], onehot, (((0,), (0,)), ((), ())),
                                       preferred_element_type=F32, precision=lax.Precision.HIGHEST)

    out = pl.pallas_call(
        body, name="bias_table",
        out_shape=jax.ShapeDtypeStruct((HEADS, L * 2 * L), F32),
        compiler_params=_params(),
    )(rel_bias, buckets)
    return out.reshape(HEADS, L, 2 * L)


def bias_grad(dbias):
    buckets = jnp.asarray(_bucket_table().reshape(1, L * 2 * L))

    def body(db_ref, bk_ref, out_ref):
        onehot = (lax.broadcasted_iota(jnp.int32, (32, L * 2 * L), 0) == bk_ref[...]).astype(F32)
        out_ref[...] = lax.dot_general(onehot, db_ref[...], (((1,), (1,)), ((), ())),
                                       preferred_element_type=F32, precision=lax.Precision.HIGHEST)

    return pl.pallas_call(
        body, name="bias_grad",
        out_shape=jax.ShapeDtypeStruct((32, HEADS), F32),
        compiler_params=_params(),
    )(dbias.reshape(HEADS, L * 2 * L), buckets)


def inproj_fwd(x, g_pre, wp):
    S = x.shape[0]
    tm, tn = (1024 if S % 1024 == 0 else 512), 1536

    def body(x_ref, g_ref, w_ref, proj_ref, h_ref):
        @pl.when(pl.program_id(1) == 0)
        def _():
            xv = x_ref[...]
            r = lax.rsqrt(jnp.mean(xv * xv, axis=-1, keepdims=True) + EPS)
            h_ref[...] = (xv * r * g_ref[...]).astype(BF16)
        proj_ref[...] = _dot(h_ref[...], w_ref[...])

    return pl.pallas_call(
        body, name="inproj_fwd", grid=(S // tm, NCP // tn),
        in_specs=[pl.BlockSpec((tm, D), lambda i, j: (i, 0)), _full((1, D)),
                  pl.BlockSpec((D, tn), lambda i, j: (0, j))],
        out_specs=[pl.BlockSpec((tm, tn), lambda i, j: (i, j)), pl.BlockSpec((tm, D), lambda i, j: (i, 0))],
        out_shape=[jax.ShapeDtypeStruct((S, NCP), F32), jax.ShapeDtypeStruct((S, D), BF16)],
        compiler_params=_params(("arbitrary", "arbitrary")),
    )(x, g_pre, wp)


def inproj_bwd(dproj, wp, x, g_pre, dy):
    S = x.shape[0]
    tm, tk = (1024 if S % 1024 == 0 else 512), 1536
    nk = NCP // tk

    def body(dp_ref, w_ref, x_ref, g_ref, dy_ref, dx_ref, dg_ref, acc):
        i, k = pl.program_id(0), pl.program_id(1)

        @pl.when(k == 0)
        def _():
            acc[...] = jnp.zeros_like(acc)

        acc[...] += _dot_nt(dp_ref[...], w_ref[...])

        @pl.when((k == nk - 1) & (i == 0))
        def _():
            dg_ref[...] = jnp.zeros_like(dg_ref)

        @pl.when(k == nk - 1)
        def _():
            xv = x_ref[...]
            dh = acc[...]
            g = g_ref[...]
            r = lax.rsqrt(jnp.mean(xv * xv, axis=-1, keepdims=True) + EPS)
            dhg = dh * g
            dx_ref[...] = dy_ref[...] + r * dhg - xv * (r * r * r) * jnp.mean(dhg * xv, axis=-1, keepdims=True)
            dg_ref[...] += jnp.sum(dh * xv * r, axis=0, keepdims=True)

    return pl.pallas_call(
        body, name="inproj_bwd", grid=(S // tm, nk),
        in_specs=[pl.BlockSpec((tm, tk), lambda i, k: (i, k)), pl.BlockSpec((D, tk), lambda i, k: (0, k)),
                  pl.BlockSpec((tm, D), lambda i, k: (i, 0)), _full((1, D)),
                  pl.BlockSpec((tm, D), lambda i, k: (i, 0))],
        out_specs=[pl.BlockSpec((tm, D), lambda i, k: (i, 0)), _full((1, D))],
        out_shape=[jax.ShapeDtypeStruct((S, D), F32), jax.ShapeDtypeStruct((1, D), F32)],
        scratch_shapes=[pltpu.VMEM((tm, D), F32)],
        compiler_params=_params(("arbitrary", "arbitrary")),
    )(dproj, wp, x, g_pre, dy)


def matmul_tn(a, b, name, tn=512, ts=512):
    S, K = a.shape
    N = b.shape[1]
    ns = S // ts

    def body(a_ref, b_ref, o_ref):
        @pl.when(pl.program_id(1) == 0)
        def _():
            o_ref[...] = jnp.zeros_like(o_ref)
        o_ref[...] += _dot_tn(a_ref[...].astype(F32), b_ref[...])

    return pl.pallas_call(
        body, name=name, grid=(N // tn, ns),
        in_specs=[pl.BlockSpec((ts, K), lambda j, s: (s, 0)), pl.BlockSpec((ts, tn), lambda j, s: (s, j))],
        out_specs=pl.BlockSpec((K, tn), lambda j, s: (0, j)),
        out_shape=jax.ShapeDtypeStruct((K, N), F32),
        compiler_params=_params(("arbitrary", "arbitrary")),
    )(a, b)


def _att_mask(n):
    qi = lax.broadcasted_iota(jnp.int32, (L, 2 * L), 0)
    kj = lax.broadcasted_iota(jnp.int32, (L, 2 * L), 1)
    dist = qi + L - kj
    return (dist >= 0) & (dist < L) & ((kj >= L) | (n > 0))


def _att_in_specs(nb):
    last = nb - 1
    cur = lambda n: jnp.minimum(n, last)
    prev = lambda n: jnp.maximum(jnp.minimum(n, last) - 1, 0)
    return [
        pl.BlockSpec((L, 1024), lambda n: (cur(n), O_Q // 1024)),
        pl.BlockSpec((L, 128), lambda n: (prev(n), O_K // 128)),
        pl.BlockSpec((L, 128), lambda n: (cur(n), O_K // 128)),
        pl.BlockSpec((L, 128), lambda n: (prev(n), O_V // 128)),
        pl.BlockSpec((L, 128), lambda n: (cur(n), O_V // 128)),
        pl.BlockSpec((L, 1024), lambda n: (cur(n), O_ZA // 1024)),
        _full((HEADS, L, 2 * L)),
        pl.BlockSpec(memory_space=pltpu.SMEM),
    ]


def _att_probs(qh, kk, bias_h, mask, sk):
    logits = _dot_nt(qh, kk) + bias_h
    logits = jnp.where(mask, logits, NEG)
    m = jnp.maximum(jnp.max(logits, axis=-1, keepdims=True), sk)
    p = jnp.exp(logits - m)
    es = jnp.exp(sk - m)
    den = jnp.sum(p, axis=-1, keepdims=True) + es
    return p / den, es / den


def att_fwd(proj, bias, sinks):
    S = proj.shape[0]
    nb = S // L

    def body(q_ref, kp_ref, kc_ref, vp_ref, vc_ref, z_ref, bias_ref, s_ref, y_ref, o_scr):
        mask = _att_mask(pl.program_id(0))
        for kv in range(KV):
            sl = slice(kv * DH, (kv + 1) * DH)
            kk = jnp.concatenate([kp_ref[:, sl], kc_ref[:, sl]], axis=0).astype(BF16)
            vv = jnp.concatenate([vp_ref[:, sl], vc_ref[:, sl]], axis=0).astype(BF16)
            for g in range(HEADS // KV):
                h = kv * (HEADS // KV) + g
                hs = slice(h * DH, (h + 1) * DH)
                qh = (q_ref[:, hs] * 0.125).astype(BF16)
                P, _ = _att_probs(qh, kk, bias_ref[h], mask, s_ref[h])
                o_scr[:, hs] = _dot(P.astype(BF16), vv)
        z = z_ref[...]
        y_ref[...] = (o_scr[...] * (z * _sigmoid(z))).astype(BF16)

    return pl.pallas_call(
        body, name="att_fwd", grid=(nb,),
        in_specs=_att_in_specs(nb),
        out_specs=pl.BlockSpec((L, 1024), lambda n: (n, 0)),
        out_shape=jax.ShapeDtypeStruct((S, 1024), BF16),
        scratch_shapes=[pltpu.VMEM((L, 1024), F32)],
        compiler_params=_params(("arbitrary",)),
    )(proj, proj, proj, proj, proj, proj, bias, sinks)


def att_bwd(dy, proj, bias, sinks):
    S = proj.shape[0]
    nb = S // L
    last = nb - 1

    def body(dy_ref, q_ref, kp_ref, kc_ref, vp_ref, vc_ref, z_ref, bias_ref, s_ref,
             dq_ref, dz_ref, dk_ref, dv_ref, dbias_ref, dsink_ref, carry, band, dq_scr, dz_scr):
        n = pl.program_id(0)

        @pl.when(n == 0)
        def _():
            carry[...] = jnp.zeros_like(carry)
            dbias_ref[...] = jnp.zeros_like(dbias_ref)
            dsink_ref[...] = jnp.zeros_like(dsink_ref)

        band[...] = jnp.zeros_like(band)

        @pl.when(n < nb)
        def _():
            mask = _att_mask(n)
            lane = lax.broadcasted_iota(jnp.int32, (1, 128), 1)
            dsink = jnp.zeros((1, 128), F32)
            for kv in range(KV):
                sl = slice(kv * DH, (kv + 1) * DH)
                kk = jnp.concatenate([kp_ref[:, sl], kc_ref[:, sl]], axis=0).astype(BF16)
                vv = jnp.concatenate([vp_ref[:, sl], vc_ref[:, sl]], axis=0).astype(BF16)
                dk_acc = jnp.zeros((2 * L, DH), F32)
                dv_acc = jnp.zeros((2 * L, DH), F32)
                for g in range(HEADS // KV):
                    h = kv * (HEADS // KV) + g
                    hs = slice(h * DH, (h + 1) * DH)
                    qh = (q_ref[:, hs] * 0.125).astype(BF16)
                    P, psink = _att_probs(qh, kk, bias_ref[h], mask, s_ref[h])
                    Pb = P.astype(BF16)
                    zh = z_ref[:, hs]
                    sg = _sigmoid(zh)
                    dyh = dy_ref[:, hs]
                    O = _dot(Pb, vv)
                    dO = dyh * (zh * sg)
                    dz_scr[:, hs] = dyh * O * (sg * (1.0 + zh * (1.0 - sg)))
                    dOb = dO.astype(BF16)
                    dv_acc = dv_acc + _dot_tn(P, dOb)
                    dP = _dot_nt(dOb, vv)
                    delta = jnp.sum(P * dP, axis=-1, keepdims=True)
                    dS = P * (dP - delta)
                    dsink = dsink + jnp.where(lane == h, -jnp.sum(psink * delta), 0.0)
                    dSb = dS.astype(BF16)
                    dq_scr[:, hs] = _dot(dSb, kk) * 0.125
                    dk_acc = dk_acc + _dot_tn(dS, qh)
                    dbias_ref[h] += dS
                band[:, sl] = dk_acc
                band[:, 128 + kv * DH:128 + (kv + 1) * DH] = dv_acc
            dsink_ref[...] += dsink
            dq_ref[...] = dq_scr[...].astype(BF16)
            dz_ref[...] = dz_scr[...].astype(BF16)

        out = carry[...] + band[0:L, :]
        dk_ref[...] = out[:, 0:128].astype(BF16)
        dv_ref[...] = out[:, 128:256].astype(BF16)
        carry[...] = band[L:2 * L, :]

    cur = lambda n: jnp.minimum(n, last)
    lag = lambda n: jnp.maximum(n - 1, 0)
    return pl.pallas_call(
        body, name="att_bwd", grid=(nb + 1,),
        in_specs=[pl.BlockSpec((L, 1024), lambda n: (cur(n), 0))] + _att_in_specs(nb),
        out_specs=[pl.BlockSpec((L, 1024), lambda n: (cur(n), 0)), pl.BlockSpec((L, 1024), lambda n: (cur(n), 0)),
                   pl.BlockSpec((L, 128), lambda n: (lag(n), 0)), pl.BlockSpec((L, 128), lambda n: (lag(n), 0)),
                   _full((HEADS, L, 2 * L)), _full((1, 128))],
        out_shape=[jax.ShapeDtypeStruct((S, 1024), BF16), jax.ShapeDtypeStruct((S, 1024), BF16),
                   jax.ShapeDtypeStruct((S, 128), BF16), jax.ShapeDtypeStruct((S, 128), BF16),
                   jax.ShapeDtypeStruct((HEADS, L, 2 * L), F32), jax.ShapeDtypeStruct((1, 128), F32)],
        scratch_shapes=[pltpu.VMEM((L, 256), F32), pltpu.VMEM((2 * L, 256), F32),
                        pltpu.VMEM((L, 1024), F32), pltpu.VMEM((L, 1024), F32)],
        compiler_params=_params(("arbitrary",)),
    )(dy, proj, proj, proj, proj, proj, proj, bias, sinks)


def _sgu_in_specs():
    return [
        pl.BlockSpec((L, 1024), lambda c: (c, O_U // 1024)),
        pl.BlockSpec((L, 1024), lambda c: (c, O_VS // 1024)),
        pl.BlockSpec((L, 1024), lambda c: (c, O_ZS // 1024)),
        _full((1, 1024)), _full((1, 1024)), _full((8, L, L)), _full((L, 8)),
    ]


def _sgu_norm(v, lg, lb):
    mu = jnp.mean(v, axis=-1, keepdims=True)
    vc = v - mu
    rstd = lax.rsqrt(jnp.mean(vc * vc, axis=-1, keepdims=True) + EPS)
    xhat = vc * rstd
    return xhat * lg + lb, xhat, rstd


def _tril():
    return lax.broadcasted_iota(jnp.int32, (L, L), 0) >= lax.broadcasted_iota(jnp.int32, (L, L), 1)


def sgu_fwd(proj, ln_g, ln_b, w, b_t):
    S = proj.shape[0]

    def body(u_ref, v_ref, z_ref, lg_ref, lb_ref, w_ref, bt_ref, y_ref):
        vn, _, _ = _sgu_norm(v_ref[...], lg_ref[...], lb_ref[...])
        tri = _tril()
        parts = []
        for g in range(8):
            wg = jnp.where(tri, w_ref[g], 0.0).astype(BF16)
            parts.append(_dot(wg, vn[:, g * 128:(g + 1) * 128].astype(BF16)) + bt_ref[:, g:g + 1])
        mixed = jnp.concatenate(parts, axis=1)
        z = z_ref[...]
        y_ref[...] = (u_ref[...] * mixed * (z * _sigmoid(z))).astype(BF16)

    return pl.pallas_call(
        body, name="sgu_fwd", grid=(S // L,),
        in_specs=_sgu_in_specs(),
        out_specs=pl.BlockSpec((L, 1024), lambda c: (c, 0)),
        out_shape=jax.ShapeDtypeStruct((S, 1024), BF16),
        compiler_params=_params(("arbitrary",)),
    )(proj, proj, proj, ln_g, ln_b, w, b_t)


def sgu_bwd(dy, proj, ln_g, ln_b, w, b_t):
    S = proj.shape[0]

    def body(dy_ref, u_ref, v_ref, z_ref, lg_ref, lb_ref, w_ref, bt_ref,
             dout_ref, dw_ref, dbt_ref, dlg_ref, dlb_ref):
        @pl.when(pl.program_id(0) == 0)
        def _():
            dw_ref[...] = jnp.zeros_like(dw_ref)
            dbt_ref[...] = jnp.zeros_like(dbt_ref)
            dlg_ref[...] = jnp.zeros_like(dlg_ref)
            dlb_ref[...] = jnp.zeros_like(dlb_ref)

        lg = lg_ref[...]
        vn, xhat, rstd = _sgu_norm(v_ref[...], lg, lb_ref[...])
        tri = _tril()
        lane = lax.broadcasted_iota(jnp.int32, (L, 128), 1)
        wgs, parts = [], []
        for g in range(8):
            wg = jnp.where(tri, w_ref[g], 0.0)
            wgs.append(wg)
            parts.append(_dot(wg.astype(BF16), vn[:, g * 128:(g + 1) * 128].astype(BF16)) + bt_ref[:, g:g + 1])
        mixed = jnp.concatenate(parts, axis=1)
        z = z_ref[...]
        sg = _sigmoid(z)
        silu = z * sg
        dy_v = dy_ref[...]
        u = u_ref[...]
        dout_ref[:, 0:1024] = (dy_v * mixed * silu).astype(BF16)
        dout_ref[:, 2048:3072] = (dy_v * u * mixed * (sg * (1.0 + z * (1.0 - sg)))).astype(BF16)
        dmixed = dy_v * u * silu
        dbt = jnp.zeros((L, 128), F32)
        dvn_parts = []
        for g in range(8):
            dm = dmixed[:, g * 128:(g + 1) * 128]
            dmb = dm.astype(BF16)
            dbt = dbt + jnp.where(lane == g, jnp.sum(dm, axis=1, keepdims=True), 0.0)
            dw_ref[g] += jnp.where(tri, _dot_nt(dmb, vn[:, g * 128:(g + 1) * 128].astype(BF16)), 0.0)
            dvn_parts.append(_dot_tn(wgs[g], dmb))
        dbt_ref[...] += dbt
        dvn = jnp.concatenate(dvn_parts, axis=1)
        dlg_ref[...] += jnp.sum(dvn * xhat, axis=0, keepdims=True)
        dlb_ref[...] += jnp.sum(dvn, axis=0, keepdims=True)
        dxh = dvn * lg
        dv = rstd * (dxh - jnp.mean(dxh, axis=-1, keepdims=True)
                     - xhat * jnp.mean(dxh * xhat, axis=-1, keepdims=True))
        dout_ref[:, 1024:2048] = dv.astype(BF16)

    return pl.pallas_call(
        body, name="sgu_bwd", grid=(S // L,),
        in_specs=[pl.BlockSpec((L, 1024), lambda c: (c, 0))] + _sgu_in_specs(),
        out_specs=[pl.BlockSpec((L, 3072), lambda c: (c, 0)), _full((8, L, L)), _full((L, 128)),
                   _full((1, 1024)), _full((1, 1024))],
        out_shape=[jax.ShapeDtypeStruct((S, 3072), BF16), jax.ShapeDtypeStruct((8, L, L), F32),
                   jax.ShapeDtypeStruct((L, 128), F32), jax.ShapeDtypeStruct((1, 1024), F32),
                   jax.ShapeDtypeStruct((1, 1024), F32)],
        compiler_params=_params(("arbitrary",)),
    )(dy, proj, proj, proj, ln_g, ln_b, w, b_t)


def _expand_matrix():
    r = lax.broadcasted_iota(jnp.int32, (128, SSM_W), 0)
    c = lax.broadcasted_iota(jnp.int32, (128, SSM_W), 1)
    return (c // SSM_P) == r


def _expand_matrix_t():
    r = lax.broadcasted_iota(jnp.int32, (SSM_W, 128), 0)
    c = lax.broadcasted_iota(jnp.int32, (SSM_W, 128), 1)
    return (r // SSM_P) == c


def _rows_from(ref, start):
    C = ref.shape[1]
    tiles = ref[...].reshape(17, 8, C)
    q, s = divmod(start, 8)
    if s == 0:
        return tiles[q:q + 16].reshape(L, C)
    rolled = pltpu.roll(tiles, 8 - s, axis=1)
    sub = lax.broadcasted_iota(jnp.int32, (16, 8, C), 1)
    return jnp.where(sub < 8 - s, rolled[q:q + 16], rolled[q + 1:q + 17]).reshape(L, C)


def _ssd_common(ext_ref, cw_ref, cb_ref, dt_raw, dtb, alog):
    taps = [_rows_from(ext_ref, 5 + k) for k in range(CONV_K)]
    pre = cb_ref[...]
    for k in range(CONV_K):
        pre = pre + cw_ref[k:k + 1, :] * taps[k]
    sg_pre = _sigmoid(pre)
    xc = pre * sg_pre
    dt = _softplus(dt_raw + dtb)
    a = -jnp.exp(alog)
    adt = dt * a
    acs = _sel_dot(_tril(), adt, 3)
    return pre, sg_pre, xc, dt, a, acs, taps


def _ssd_in_specs(rev, nc):
    cidx = (lambda c: nc - 1 - c) if rev else (lambda c: c)
    return [
        pl.BlockSpec((L, 2048), lambda c: (cidx(c), O_ZM // 2048)),
        pl.BlockSpec((L, 3072), lambda c: (cidx(c), O_XBC // 3072)),
        pl.BlockSpec((L, 128), lambda c: (cidx(c), O_DT // 128)),
        _full((8, CONV_C)), _full((1, CONV_C)), _full((1, 128)), _full((1, 128)), _full((1, 128)),
        _full((1, SSM_W)),
    ]


def ssd_fwd(proj, conv_w, conv_b, dt_bias, a_log, d_skip, norm_g):
    S = proj.shape[0]
    nc = S // L

    def body(z_ref, xbc_ref, dt_ref, cw_ref, cb_ref, dtb_ref, alog_ref, dsk_ref, ng_ref,
             y_ref, hs_ref, H, ext, ysc):
        @pl.when(pl.program_id(0) == 0)
        def _():
            H[...] = jnp.zeros_like(H)
            ext[0:8, :] = jnp.zeros((8, CONV_C), F32)

        ext[8:8 + L, :] = xbc_ref[...]
        pre, sg_pre, xc, dt, a, acs, _ = _ssd_common(ext, cw_ref, cb_ref, dt_ref[...], dtb_ref[...], alog_ref[...])
        ext[0:8, :] = xbc_ref[L - 8:L, :]
        xs = xc[:, 0:SSM_W]
        acs_t = acs.T
        ex = _expand_matrix()
        dt_x = _dot_sel(dt, ex, 2)
        xdt = xs * dt_x
        eacs_x = _dot_sel(jnp.exp(acs), ex, 2)
        xw = xdt * _dot_sel(jnp.exp(acs[L - 1:L, :] - acs), ex, 2)
        cd_row = jnp.exp(acs[L - 1:L, :])
        hs_ref[0] = H[...]
        tri = _tril()
        for g in range(SSM_G):
            gs = slice(g * 512, (g + 1) * 512)
            bg = xc[:, SSM_W + g * SSM_N:SSM_W + (g + 1) * SSM_N].astype(BF16)
            cg = xc[:, SSM_W + 512 + g * SSM_N:SSM_W + 512 + (g + 1) * SSM_N].astype(BF16)
            G = _dot_nt(cg, bg)
            yoff = _dot_nt(cg, H[gs, :].astype(BF16)) * eacs_x[:, gs]
            Sg = _dot_tn(xw[:, gs], bg)
            for j in range(8):
                hh = g * 8 + j
                hs = slice(hh * SSM_P, (hh + 1) * SSM_P)
                seg = acs[:, hh:hh + 1] - acs_t[hh:hh + 1, :]
                dk = jnp.where(tri, jnp.exp(jnp.minimum(seg, 0.0)), 0.0)
                yd = _dot((G * dk).astype(BF16), xdt[:, hs].astype(BF16))
                ysc[:, hs] = yd + yoff[:, j * SSM_P:(j + 1) * SSM_P]
                H[hs, :] = H[hs, :] * cd_row[:, hh:hh + 1] + Sg[j * SSM_P:(j + 1) * SSM_P, :]
        d_x = _dot_sel(jnp.broadcast_to(dsk_ref[...], (8, 128)), ex, 3)[0:1, :]
        Y = ysc[...] + d_x * xs
        z = z_ref[...]
        yz = Y * (z * _sigmoid(z))
        ng = ng_ref[...]
        for g in range(SSM_G):
            gs = slice(g * 512, (g + 1) * 512)
            t = yz[:, gs]
            rstd = lax.rsqrt(jnp.mean(t * t, axis=-1, keepdims=True) + EPS)
            y_ref[:, gs] = (t * rstd * ng[:, gs]).astype(BF16)

    return pl.pallas_call(
        body, name="ssd_fwd", grid=(nc,),
        in_specs=_ssd_in_specs(False, nc),
        out_specs=[pl.BlockSpec((L, SSM_W), lambda c: (c, 0)), pl.BlockSpec((1, SSM_W, SSM_N), lambda c: (c, 0, 0))],
        out_shape=[jax.ShapeDtypeStruct((S, SSM_W), BF16), jax.ShapeDtypeStruct((nc, SSM_W, SSM_N), F32)],
        scratch_shapes=[pltpu.VMEM((SSM_W, SSM_N), F32), pltpu.VMEM((8 + L, CONV_C), F32),
                        pltpu.VMEM((L, SSM_W), F32)],
        compiler_params=_params(("arbitrary",)),
    )(proj, proj, proj, conv_w, conv_b, dt_bias, a_log, d_skip, norm_g)


def ssd_bwd(dy, proj, hstates, conv_w, conv_b, dt_bias, a_log, d_skip, norm_g):
    S = proj.shape[0]
    nc = S // L
    cidx = lambda c: nc - 1 - c

    def body(dy_ref, z_ref, xbc_ref, dt_ref, cw_ref, cb_ref, dtb_ref, alog_ref, dsk_ref, ng_ref,
             xprev_ref, hp_ref,
             dz_ref, dxbc_ref, ddt_ref, dcw_ref, dcb_ref, ddtb_ref, dalog_ref, ddsk_ref, dng_ref,
             dH, ext, dext, ysc, yoffsc, dxdt, dxc, tsc):
        step = pl.program_id(0)
        c = nc - 1 - step

        @pl.when(step == 0)
        def _():
            dH[...] = jnp.zeros_like(dH)
            dext[L:L + 8, :] = jnp.zeros((8, CONV_C), F32)
            for r in (dcw_ref, dcb_ref, ddtb_ref, dalog_ref, ddsk_ref, dng_ref):
                r[...] = jnp.zeros_like(r)

        ext[0:8, :] = jnp.where(c > 0, xprev_ref[L - 8:L, :], 0.0)
        ext[8:8 + L, :] = xbc_ref[...]
        dtb = dtb_ref[...]
        dt_raw = dt_ref[...]
        pre, sg_pre, xc, dt, a, acs, taps = _ssd_common(ext, cw_ref, cb_ref, dt_raw, dtb, alog_ref[...])
        xs = xc[:, 0:SSM_W]
        acs_t = acs.T
        ex = _expand_matrix()
        dt_x = _dot_sel(dt, ex, 2)
        xdt = xs * dt_x
        eacs_x = _dot_sel(jnp.exp(acs), ex, 2)
        dte_x = _dot_sel(jnp.exp(acs[L - 1:L, :] - acs), ex, 2)
        xw = xdt * dte_x
        cd_row = jnp.exp(acs[L - 1:L, :])
        tri = _tril()

        Gs, Cs, Bs = [], [], []
        for g in range(SSM_G):
            gs = slice(g * 512, (g + 1) * 512)
            bg = xc[:, SSM_W + g * SSM_N:SSM_W + (g + 1) * SSM_N].astype(BF16)
            cg = xc[:, SSM_W + 512 + g * SSM_N:SSM_W + 512 + (g + 1) * SSM_N].astype(BF16)
            G = _dot_nt(cg, bg)
            Gs.append(G), Cs.append(cg), Bs.append(bg)
            yoffsc[:, gs] = _dot_nt(cg, hp_ref[0, gs, :].astype(BF16)) * eacs_x[:, gs]
            for j in range(8):
                hh = g * 8 + j
                hs = slice(hh * SSM_P, (hh + 1) * SSM_P)
                seg = acs[:, hh:hh + 1] - acs_t[hh:hh + 1, :]
                dk = jnp.where(tri, jnp.exp(jnp.minimum(seg, 0.0)), 0.0)
                ysc[:, hs] = _dot((G * dk).astype(BF16), xdt[:, hs].astype(BF16))
        d_x = _dot_sel(jnp.broadcast_to(dsk_ref[...], (8, 128)), ex, 3)[0:1, :]
        yoff = yoffsc[...]
        Y = ysc[...] + yoff + d_x * xs

        z = z_ref[...]
        sgz = _sigmoid(z)
        silu_z = z * sgz
        yz = Y * silu_z
        ng = ng_ref[...]
        dout = dy_ref[...]
        dyn = dout * ng
        dyz_parts, dng_parts = [], []
        for g in range(SSM_G):
            gs = slice(g * 512, (g + 1) * 512)
            t = yz[:, gs]
            rstd = lax.rsqrt(jnp.mean(t * t, axis=-1, keepdims=True) + EPS)
            dng_parts.append(jnp.sum(dout[:, gs] * t * rstd, axis=0, keepdims=True))
            dn = dyn[:, gs]
            dyz_parts.append(rstd * dn - t * (rstd * rstd * rstd) * jnp.mean(dn * t, axis=-1, keepdims=True))
        dng_ref[...] += jnp.concatenate(dng_parts, axis=1)
        dyz = jnp.concatenate(dyz_parts, axis=1)
        dY = dyz * silu_z
        dz_ref[...] = (dyz * Y * (sgz * (1.0 + z * (1.0 - sgz)))).astype(BF16)

        ex_t = _expand_matrix_t()
        ddsk_ref[...] += _dot_sel(jnp.broadcast_to(jnp.sum(dY * xs, axis=0, keepdims=True), (8, SSM_W)), ex_t, 3)[0:1, :]

        lane = lax.broadcasted_iota(jnp.int32, (L, 128), 1)
        subl = lax.broadcasted_iota(jnp.int32, (128, L), 0)
        coll = lax.broadcasted_iota(jnp.int32, (128, L), 1)
        r_cols = jnp.zeros((L, 128), F32)
        c_rows = jnp.zeros((128, L), F32)
        for g in range(SSM_G):
            gs = slice(g * 512, (g + 1) * 512)
            G, cg, bg = Gs[g], Cs[g], Bs[g]
            hp_g = hp_ref[0, gs, :]
            dh_g = dH[gs, :]
            dY_g = dY[:, gs]
            dZ = dY_g * eacs_x[:, gs]
            dZb = dZ.astype(BF16)
            dC = _dot(dZb, hp_g.astype(BF16))
            dh_from_off = _dot_tn(dZ, cg)
            dhb = dh_g.astype(BF16)
            Q = _dot_nt(bg, dhb)
            dB = _dot(xw[:, gs].astype(BF16), dhb)
            qd = Q * dte_x[:, gs]
            dxdt[:, gs] = qd
            tsc[:, gs] = qd * xdt[:, gs]
            dG = jnp.zeros((L, L), F32)
            for j in range(8):
                hh = g * 8 + j
                hs = slice(hh * SSM_P, (hh + 1) * SSM_P)
                seg = acs[:, hh:hh + 1] - acs_t[hh:hh + 1, :]
                dk = jnp.where(tri, jnp.exp(jnp.minimum(seg, 0.0)), 0.0)
                M = G * dk
                dYh = dY[:, hs]
                dYhb = dYh.astype(BF16)
                dM = _dot_nt(dYhb, xdt[:, hs].astype(BF16))
                dxdt[:, hs] += _dot_tn(M, dYhb)
                dG = dG + dM * dk
                Wm = dM * M
                r_cols = r_cols + jnp.where(lane == hh, jnp.sum(Wm, axis=1, keepdims=True), 0.0)
                c_rows = c_rows + jnp.where(subl == hh, jnp.sum(Wm, axis=0, keepdims=True), 0.0)
                pj = slice(j * SSM_P, (j + 1) * SSM_P)
                cd_h = cd_row[:, hh:hh + 1]
                dcd = jnp.sum(dh_g[pj, :] * hp_g[pj, :]) * cd_h
                c_rows = c_rows - jnp.where((subl == hh) & (coll == L - 1), dcd, 0.0)
                dH[hs, :] = dh_g[pj, :] * cd_h + dh_from_off[pj, :]
            dGb = dG.astype(BF16)
            dC = dC + _dot(dGb, bg)
            dB = dB + _dot_tn(dG, cg)
            dxc[:, SSM_W + g * SSM_N:SSM_W + (g + 1) * SSM_N] = dB
            dxc[:, SSM_W + 512 + g * SSM_N:SSM_W + 512 + (g + 1) * SSM_N] = dC

        row = lax.broadcasted_iota(jnp.int32, (L, 128), 0)
        tv = tsc[...]
        t_last = _dot_sel(jnp.broadcast_to(jnp.sum(tv, axis=0, keepdims=True), (8, SSM_W)), ex_t, 3)[0:1, :]
        dacs = (r_cols - c_rows.T + _dot_sel(dY * yoff - tv, ex_t, 2) + jnp.where(row == L - 1, t_last, 0.0))
        triu = lax.broadcasted_iota(jnp.int32, (L, L), 0) <= lax.broadcasted_iota(jnp.int32, (L, L), 1)
        dadt = _sel_dot(triu, dacs, 3)
        dxdt_v = dxdt[...]
        ddt = _dot_sel(dxdt_v * xs, ex_t, 2) + dadt * a
        dalog_ref[...] += jnp.sum(dadt * dt * a, axis=0, keepdims=True)
        ddt_raw = jnp.where(lane < SSM_H, ddt * _sigmoid(dt_raw + dtb), 0.0)
        ddtb_ref[...] += jnp.sum(ddt_raw, axis=0, keepdims=True)
        ddt_ref[...] = ddt_raw.astype(BF16)

        dxc[:, 0:SSM_W] = dxdt_v * dt_x + d_x * dY
        dpre = dxc[...] * (sg_pre * (1.0 + pre * (1.0 - sg_pre)))
        dcb_ref[...] += jnp.sum(dpre, axis=0, keepdims=True)
        dext[0:L, :] = dpre
        x_cur = xbc_ref[...]
        dx = None
        for k in range(CONV_K):
            dsh = _rows_from(dext, 3 - k)
            term = cw_ref[k:k + 1, :] * dsh
            dx = term if dx is None else dx + term
            dcw_ref[k:k + 1, :] += jnp.sum(dsh * x_cur, axis=0, keepdims=True)
        dxbc_ref[...] = dx.astype(BF16)
        dext[L:L + 8, :] = dpre[0:8, :]

    big = lambda w: pl.BlockSpec((L, w), lambda c: (cidx(c), 0))
    return pl.pallas_call(
        body, name="ssd_bwd", grid=(nc,),
        in_specs=[big(SSM_W)] + _ssd_in_specs(True, nc) + [
            pl.BlockSpec((L, 3072), lambda c: (jnp.maximum(cidx(c) - 1, 0), O_XBC // 3072)),
            pl.BlockSpec((1, SSM_W, SSM_N), lambda c: (cidx(c), 0, 0))],
        out_specs=[big(SSM_W), big(CONV_C), big(128), _full((8, CONV_C)), _full((1, CONV_C)),
                   _full((1, 128)), _full((1, 128)), _full((1, 128)), _full((1, SSM_W))],
        out_shape=[jax.ShapeDtypeStruct((S, SSM_W), BF16), jax.ShapeDtypeStruct((S, CONV_C), BF16),
                   jax.ShapeDtypeStruct((S, 128), BF16), jax.ShapeDtypeStruct((8, CONV_C), F32),
                   jax.ShapeDtypeStruct((1, CONV_C), F32), jax.ShapeDtypeStruct((1, 128), F32),
                   jax.ShapeDtypeStruct((1, 128), F32), jax.ShapeDtypeStruct((1, 128), F32),
                   jax.ShapeDtypeStruct((1, SSM_W), F32)],
        scratch_shapes=[pltpu.VMEM((SSM_W, SSM_N), F32), pltpu.VMEM((8 + L, CONV_C), F32),
                        pltpu.VMEM((L + 8, CONV_C), F32), pltpu.VMEM((L, SSM_W), F32),
                        pltpu.VMEM((L, SSM_W), F32), pltpu.VMEM((L, SSM_W), F32),
                        pltpu.VMEM((L, CONV_C), F32), pltpu.VMEM((L, SSM_W), F32)],
        compiler_params=_params(("arbitrary",)),
    )(dy, proj, proj, proj, conv_w, conv_b, dt_bias, a_log, d_skip, norm_g, proj, hstates)


def _resident(shape):
    nd = len(shape)
    return pl.BlockSpec(shape, lambda *_: (0,) * nd, pipeline_mode=pl.Buffered(1))


def merge_fwd(y_att, y_sg, y_ssm, proj, x, w_a, w_s, w_m, w_o, g_post):
    S = x.shape[0]
    tm = 256

    def body(ya_ref, ys_ref, ym_ref, gate_ref, x_ref, wa_ref, ws_ref, wm_ref, wo_ref, gp_ref,
             xn_ref, bra_ref, brs_ref, brm_ref, mg_ref, out_ref):
        bra = _dot(ya_ref[...], wa_ref[...])
        brs = _dot(ys_ref[...], ws_ref[...])
        brm = _dot(ym_ref[...], wm_ref[...])
        bra_ref[...] = bra
        brs_ref[...] = brs
        brm_ref[...] = brm
        merged = (_sigmoid(gate_ref[:, 0:1024]) * bra + _sigmoid(gate_ref[:, 1024:2048]) * brs
                  + _sigmoid(gate_ref[:, 2048:3072]) * brm)
        mb = merged.astype(BF16)
        mg_ref[...] = mb
        o = _dot(mb, wo_ref[...])
        out_ref[...] = o
        r = lax.rsqrt(jnp.mean(o * o, axis=-1, keepdims=True) + EPS)
        xn_ref[...] = x_ref[...] + o * r * gp_ref[...]

    row = lambda w: pl.BlockSpec((tm, w), lambda i: (i, 0))
    return pl.pallas_call(
        body, name="merge_fwd", grid=(S // tm,),
        in_specs=[row(1024), row(1024), row(2048), pl.BlockSpec((tm, 3072), lambda i: (i, O_GATE // 3072)),
                  row(D), _resident((1024, D)), _resident((1024, D)), _resident((2048, D)), _resident((D, D)),
                  _full((1, D))],
        out_specs=[row(D)] * 6,
        out_shape=[jax.ShapeDtypeStruct((S, D), F32)] * 4 + [jax.ShapeDtypeStruct((S, D), BF16),
                                                             jax.ShapeDtypeStruct((S, D), F32)],
        compiler_params=_params(("arbitrary",)),
    )(y_att, y_sg, y_ssm, proj, x, w_a, w_s, w_m, w_o, g_post)


def merge_bwd(dy, out, g_post, proj, br_a, br_s, br_m, w_a, w_s, w_m, w_o):
    S = dy.shape[0]
    tm = 256

    def body(dy_ref, o_ref, gp_ref, gate_ref, bra_ref, brs_ref, brm_ref, wa_ref, ws_ref, wm_ref, wo_ref,
             dout_ref, dba_ref, dbs_ref, dbm_ref, dgate_ref, dya_ref, dys_ref, dym_ref, dgp_ref):
        @pl.when(pl.program_id(0) == 0)
        def _():
            dgp_ref[...] = jnp.zeros_like(dgp_ref)

        o = o_ref[...]
        dyv = dy_ref[...]
        r = lax.rsqrt(jnp.mean(o * o, axis=-1, keepdims=True) + EPS)
        dyg = dyv * gp_ref[...]
        do = r * dyg - o * (r * r * r) * jnp.mean(dyg * o, axis=-1, keepdims=True)
        dgp_ref[...] += jnp.sum(dyv * o * r, axis=0, keepdims=True)
        dob = do.astype(BF16)
        dout_ref[...] = dob
        dmerged = _dot_nt(dob, wo_ref[...])
        for idx, (br_ref, dbr_ref, w_ref, dyi_ref) in enumerate((
                (bra_ref, dba_ref, wa_ref, dya_ref), (brs_ref, dbs_ref, ws_ref, dys_ref),
                (brm_ref, dbm_ref, wm_ref, dym_ref))):
            s = _sigmoid(gate_ref[:, idx * 1024:(idx + 1) * 1024])
            dbr = (dmerged * s).astype(BF16)
            dbr_ref[...] = dbr
            dgate_ref[:, idx * 1024:(idx + 1) * 1024] = (dmerged * br_ref[...] * s * (1.0 - s)).astype(BF16)
            dyi_ref[...] = _dot_nt(dbr, w_ref[...])

    row = lambda w: pl.BlockSpec((tm, w), lambda i: (i, 0))
    return pl.pallas_call(
        body, name="merge_bwd", grid=(S // tm,),
        in_specs=[row(D), row(D), _full((1, D)), pl.BlockSpec((tm, 3072), lambda i: (i, O_GATE // 3072)),
                  row(D), row(D), row(D),
                  _resident((1024, D)), _resident((1024, D)), _resident((2048, D)), _resident((D, D))],
        out_specs=[row(D), row(D), row(D), row(D), row(3072), row(1024), row(1024), row(2048), _full((1, D))],
        out_shape=[jax.ShapeDtypeStruct((S, D), BF16)] * 4 + [
            jax.ShapeDtypeStruct((S, 3072), BF16), jax.ShapeDtypeStruct((S, 1024), F32),
            jax.ShapeDtypeStruct((S, 1024), F32), jax.ShapeDtypeStruct((S, 2048), F32),
            jax.ShapeDtypeStruct((1, D), F32)],
        compiler_params=_params(("arbitrary",)),
    )(dy, out, g_post, proj, br_a, br_s, br_m, w_a, w_s, w_m, w_o)


def loss_head(y, target):
    S = y.shape[0]
    tm = 512

    def body(y_ref, t_ref, dy_ref, loss_ref):
        @pl.when(pl.program_id(0) == 0)
        def _():
            loss_ref[...] = jnp.zeros_like(loss_ref)
        e = y_ref[...] - t_ref[...]
        dy_ref[...] = e * (1.0 / D)
        loss_ref[...] += 0.5 * jnp.sum(jnp.mean(e * e, axis=-1, keepdims=True))

    row = pl.BlockSpec((tm, D), lambda i: (i, 0))
    return pl.pallas_call(
        body, name="loss_head", grid=(S // tm,),
        in_specs=[row, row], out_specs=[row, _full((1, 128))],
        out_shape=[jax.ShapeDtypeStruct((S, D), F32), jax.ShapeDtypeStruct((1, 128), F32)],
        compiler_params=_params(("arbitrary",)),
    )(y, target)


def _adam(w, g, m, v):
    mn = ADAM_B1 * m + (1.0 - ADAM_B1) * g
    vn = ADAM_B2 * v + (1.0 - ADAM_B2) * (g * g)
    m_hat = mn / (1.0 - ADAM_B1 ** ADAM_STEP)
    v_hat = vn / (1.0 - ADAM_B2 ** ADAM_STEP)
    return -ADAM_LR * (m_hat / (jnp.sqrt(v_hat) + ADAM_EPS) + ADAM_WD * w), mn, vn


def adamw_big(w, m, v, f, fb, cc, name, tr, f_row0=0):
    _, R, C = w.shape
    nper = R // tr
    foff = f_row0 // tr

    def body(c_ref, w_ref, m_ref, v_ref, f_ref, fb_ref, g_ref, d_ref, nm_ref, nv_ref):
        layer = pl.program_id(0) // nper
        g = jnp.where(c_ref[0] == layer, f_ref[...], fb_ref[...])
        g_ref[0] = g
        d_ref[0], nm_ref[0], nv_ref[0] = _adam(w_ref[0], g, m_ref[0], v_ref[0])

    wblk = pl.BlockSpec((1, tr, C), lambda i, c: (i // nper, i % nper, 0))
    fblk = pl.BlockSpec((tr, C), lambda i, c: (foff + i % nper, 0))
    grid_spec = pltpu.PrefetchScalarGridSpec(
        num_scalar_prefetch=1, grid=(2 * nper,),
        in_specs=[wblk, wblk, wblk, fblk, fblk], out_specs=[wblk] * 4)
    return pl.pallas_call(
        body, name=name, grid_spec=grid_spec,
        out_shape=[jax.ShapeDtypeStruct(w.shape, F32)] * 4,
        compiler_params=_params(("arbitrary",)),
    )(cc, w, m, v, f, fb)


def adamw_plain(w, g, m, v, name):
    def body(w_ref, g_ref, m_ref, v_ref, d_ref, nm_ref, nv_ref):
        d_ref[...], nm_ref[...], nv_ref[...] = _adam(w_ref[...], g_ref[...], m_ref[...], v_ref[...])

    return pl.pallas_call(
        body, name=name, out_shape=[jax.ShapeDtypeStruct(w.shape, F32)] * 3, compiler_params=_params(),
    )(w, g, m, v)


SMALL = {"norm_pre": ("g_pre", 8), "norm_post": ("g_post", 8), "att_sinks": ("sinks", 8), "sg_ln_g": ("ln_g", 8),
         "sg_ln_b": ("ln_b", 8), "sg_w": ("sg_w", 1024), "sg_b": ("sg_bt", 8), "ssm_conv_b": ("conv_b", 24),
         "ssm_dt_bias": ("dt_bias", 8), "ssm_a_log": ("a_log", 8), "ssm_d": ("d_skip", 8), "ssm_norm_g": ("norm_g", 16)}
SMALL_LAYER_ROWS = sum(r for _, r in SMALL.values())
REL_ROW = DEPTH * SMALL_LAYER_ROWS
LOSS_ROW = REL_ROW + 32
SMALL_ROWS = LOSS_ROW + 8


def _small_rows():
    rows, r = {}, 0
    for l in range(DEPTH):
        for name, (_, n) in SMALL.items():
            rows[(l, name)] = r
            r += n
    return rows


def adamw_small(red, rel, small):
    names = list(SMALL) + ["rel_bias"]
    params = dict(small, rel_bias=rel)
    rows = _small_rows()

    def grad_of(red_ref, l, name, n):
        r0 = rows[(l, name)]
        if name == "sg_b":
            return red_ref[r0:r0 + 8, :]
        if n < 128:
            return red_ref[r0:r0 + 1, 0:n]
        return jnp.concatenate([red_ref[r0 + j:r0 + j + 1, :] for j in range(n // 128)], axis=1)

    def body(red_ref, *refs):
        ins, outs = refs[:3 * len(names)], refs[3 * len(names):]
        for i, name in enumerate(names):
            w_ref, m_ref, v_ref = ins[3 * i:3 * i + 3]
            o = outs[4 * i:4 * i + 4]
            if name == "rel_bias":
                g = red_ref[REL_ROW:REL_ROW + 32, 0:16]
                o[0][...] = g
                o[1][...], o[2][...], o[3][...] = _adam(w_ref[...], g, m_ref[...], v_ref[...])
                continue
            for l in range(DEPTH):
                if name == "sg_w":
                    for grp in range(8):
                        r0 = rows[(l, name)] + grp * 128
                        g = red_ref[r0:r0 + 128, :]
                        o[0][l, grp] = g
                        o[1][l, grp], o[2][l, grp], o[3][l, grp] = _adam(w_ref[l, grp], g, m_ref[l, grp], v_ref[l, grp])
                elif name == "sg_b":
                    g = grad_of(red_ref, l, name, 128)
                    o[0][l] = g
                    o[1][l], o[2][l], o[3][l] = _adam(w_ref[l], g, m_ref[l], v_ref[l])
                else:
                    sl = slice(l, l + 1)
                    g = grad_of(red_ref, l, name, w_ref.shape[-1])
                    o[0][sl, :] = g
                    o[1][sl, :], o[2][sl, :], o[3][sl, :] = _adam(w_ref[sl, :], g, m_ref[sl, :], v_ref[sl, :])

    flat_in = [a for name in names for a in params[name]]
    out_shape = [jax.ShapeDtypeStruct(params[name][0].shape, F32) for name in names for _ in range(4)]
    res = pl.pallas_call(body, name="adamw_small", out_shape=out_shape, compiler_params=_params())(red, *flat_in)
    return {name: tuple(res[4 * i:4 * i + 4]) for i, name in enumerate(names)}


ANY = pl.BlockSpec(memory_space=pl.ANY)


def _place():
    x, y, c = lax.axis_index("x"), lax.axis_index("y"), lax.axis_index("c")
    others = [(1 - x, y), (x, 1 - y), (1 - x, 1 - y)]
    return x, y, c, others


def _rcopy(src, dst, ssem, rsem, to):
    return pltpu.make_async_remote_copy(src_ref=src, dst_ref=dst, send_sem=ssem, recv_sem=rsem,
                                        device_id=to, device_id_type=MESH)


def gather_weights(arrs):
    n = len(arrs)

    def body(*refs):
        srcs, outs, ssem, rsem = refs[:n], refs[n:2 * n], refs[2 * n], refs[2 * n + 1]
        x, y, c, others = _place()
        me = 2 * x + y
        sib = (x, y, 1 - c)
        first = [_rcopy(srcs[i].at[c], outs[i].at[c, me], ssem.at[6 * i + k], rsem.at[6 * i + k], (ox, oy, c))
                 for i in range(n) for k, (ox, oy) in enumerate(others)]
        for cp in first:
            cp.start()
        passed = []
        for k, (ox, oy) in enumerate(others):
            for i in range(n):
                slot = outs[i].at[c, 2 * ox + oy]
                _rcopy(slot, slot, ssem.at[6 * i + k], rsem.at[6 * i + k], sib).wait_recv()
                fw = _rcopy(slot, slot, ssem.at[6 * i + 3 + k], rsem.at[6 * i + 3 + k], sib)
                fw.start()
                passed.append(fw)
        for k, (ox, oy) in enumerate(others):
            for i in range(n):
                slot = outs[i].at[1 - c, 2 * ox + oy]
                _rcopy(slot, slot, ssem.at[6 * i + 3 + k], rsem.at[6 * i + 3 + k], sib).wait_recv()
        for cp in first + passed:
            cp.wait_send()

    return pl.pallas_call(
        body, name="gather_weights",
        in_specs=[ANY] * n, out_specs=[ANY] * n,
        out_shape=[jax.ShapeDtypeStruct((2, SHARDS) + a.shape[1:], a.dtype) for a in arrs],
        scratch_shapes=[pltpu.SemaphoreType.DMA((6 * n,)), pltpu.SemaphoreType.DMA((6 * n,))],
    )(*arrs)


def grad_sibling_exchange(arrs):
    n = len(arrs)

    def body(*refs):
        srcs, outs, ssem, rsem = refs[:n], refs[n:2 * n], refs[2 * n], refs[2 * n + 1]
        x, y, c, _ = _place()
        cps = [_rcopy(srcs[i].at[1 - c], outs[i], ssem.at[i], rsem.at[i], (x, y, 1 - c)) for i in range(n)]
        for cp in cps:
            cp.start()
        for cp in cps:
            cp.wait()

    return pl.pallas_call(
        body, name="grad_sibling_exchange",
        in_specs=[ANY] * n, out_specs=[ANY] * n,
        out_shape=[jax.ShapeDtypeStruct(a.shape[1:], F32) for a in arrs],
        scratch_shapes=[pltpu.SemaphoreType.DMA((n,)), pltpu.SemaphoreType.DMA((n,))],
    )(*arrs)


def grad_chip_sum(g, sb, cc, tr, name):
    _, _, R, C = g.shape
    blk = pl.BlockSpec((1, tr, C), lambda s, r, c: (s, r, 0))
    grid_spec = pltpu.PrefetchScalarGridSpec(
        num_scalar_prefetch=1, grid=(SHARDS, R // tr),
        in_specs=[pl.BlockSpec((1, 1, tr, C), lambda s, r, c: (c[0], s, r, 0)), blk],
        out_specs=[blk, blk])

    def body(c_ref, a_ref, b_ref, o_ref, ob_ref):
        t = a_ref[0] + b_ref[...]
        o_ref[...] = t
        ob_ref[...] = t.astype(BF16)

    return pl.pallas_call(
        body, name=name, grid_spec=grid_spec,
        out_shape=[jax.ShapeDtypeStruct((SHARDS, R, C), F32), jax.ShapeDtypeStruct((SHARDS, R, C), BF16)],
        compiler_params=_params(("arbitrary", "arbitrary")),
    )(cc, g, sb)


def grad_chip_exchange(arrs):
    n = len(arrs)

    def body(*refs):
        srcs, outs, ssem, rsem = refs[:n], refs[n:2 * n], refs[2 * n], refs[2 * n + 1]
        x, y, c, others = _place()
        me = 2 * x + y
        sends = [_rcopy(srcs[i].at[2 * ox + oy], outs[i].at[me], ssem.at[3 * i + k], rsem.at[3 * i + k], (ox, oy, c))
                 for i in range(n) for k, (ox, oy) in enumerate(others)]
        for cp in sends:
            cp.start()
        for i in range(n):
            for k, (ox, oy) in enumerate(others):
                slot = outs[i].at[2 * ox + oy]
                _rcopy(slot, slot, ssem.at[3 * i + k], rsem.at[3 * i + k], (ox, oy, c)).wait_recv()
        for cp in sends:
            cp.wait_send()

    return pl.pallas_call(
        body, name="grad_chip_exchange",
        in_specs=[ANY] * n, out_specs=[ANY] * n,
        out_shape=[jax.ShapeDtypeStruct(a.shape, a.dtype) for a in arrs],
        scratch_shapes=[pltpu.SemaphoreType.DMA((3 * n,)), pltpu.SemaphoreType.DMA((3 * n,))],
    )(*arrs)


def grad_shard_sum(t, rb, me, tr, name):
    _, R, C = t.shape
    grid_spec = pltpu.PrefetchScalarGridSpec(
        num_scalar_prefetch=1, grid=(R // tr,),
        in_specs=[pl.BlockSpec((1, tr, C), lambda r, m: (m[0], r, 0)),
                  pl.BlockSpec((SHARDS, tr, C), lambda r, m: (0, r, 0))],
        out_specs=pl.BlockSpec((tr, C), lambda r, m: (r, 0)))

    def body(m_ref, t_ref, r_ref, o_ref):
        part = [jnp.where(m_ref[0] == s, t_ref[0], r_ref[s].astype(F32)) for s in range(SHARDS)]
        o_ref[...] = ((part[0] + part[1]) + part[2]) + part[3]

    return pl.pallas_call(
        body, name=name, grid_spec=grid_spec,
        out_shape=jax.ShapeDtypeStruct((R, C), F32),
        compiler_params=_params(("arbitrary",)),
    )(me, t, rb)


def grad_sibling_share(arrs):
    n = len(arrs)

    def body(*refs):
        srcs, outs, ssem, rsem = refs[:n], refs[n:2 * n], refs[2 * n], refs[2 * n + 1]
        x, y, c, _ = _place()
        cps = [_rcopy(srcs[i], outs[i], ssem.at[i], rsem.at[i], (x, y, 1 - c)) for i in range(n)]
        for cp in cps:
            cp.start()
        for cp in cps:
            cp.wait()

    return pl.pallas_call(
        body, name="grad_sibling_share",
        in_specs=[ANY] * n, out_specs=[ANY] * n,
        out_shape=[jax.ShapeDtypeStruct(a.shape, F32) for a in arrs],
        scratch_shapes=[pltpu.SemaphoreType.DMA((n,)), pltpu.SemaphoreType.DMA((n,))],
    )(*arrs)


def _allreduce_rows(src, sib_buf, chips, out_ref, ssem, rsem):
    x, y, c, others = _place()
    me = 2 * x + y
    cp = _rcopy(src, sib_buf, ssem.at[0], rsem.at[0], (x, y, 1 - c))
    cp.start()
    cp.wait()
    chips[me] = src[...] + sib_buf[...]
    sends = [_rcopy(chips.at[me], chips.at[me], ssem.at[1 + k], rsem.at[1 + k], (ox, oy, c))
             for k, (ox, oy) in enumerate(others)]
    for s in sends:
        s.start()
    for k, (ox, oy) in enumerate(others):
        slot = chips.at[2 * ox + oy]
        _rcopy(slot, slot, ssem.at[1 + k], rsem.at[1 + k], (ox, oy, c)).wait_recv()
    for s in sends:
        s.wait_send()
    out_ref[...] = ((chips[0] + chips[1]) + chips[2]) + chips[3]


def _allreduce_scratch(rows):
    return [pltpu.VMEM((rows, 128), F32), pltpu.VMEM((SHARDS, rows, 128), F32),
            pltpu.SemaphoreType.DMA((4,)), pltpu.SemaphoreType.DMA((4,))]


def allreduce_rows(buf, name):
    rows = buf.shape[0]
    VM = pl.BlockSpec(memory_space=pltpu.VMEM)

    def body(src_ref, out_ref, sib_buf, chips, ssem, rsem):
        _allreduce_rows(src_ref, sib_buf, chips, out_ref, ssem, rsem)

    return pl.pallas_call(
        body, name=name, in_specs=[VM], out_specs=VM,
        out_shape=jax.ShapeDtypeStruct((rows, 128), F32),
        scratch_shapes=_allreduce_scratch(rows), compiler_params=_params(),
    )(buf)


def small_allreduce(grads, rel, loss_part):
    rows = _small_rows()
    keys = [(l, name) for l in range(DEPTH) for name in SMALL]
    flat = [grads[l][SMALL[name][0]] for l, name in keys] + [rel, loss_part]

    def body(*refs):
        ins = refs[:len(flat)]
        out_ref, src, sib_buf, chips, ssem, rsem = refs[len(flat):]
        src[...] = jnp.zeros_like(src)
        for (l, name), ref in zip(keys, ins):
            r0 = rows[(l, name)]
            if name == "sg_w":
                for grp in range(8):
                    src[r0 + grp * 128:r0 + (grp + 1) * 128, :] = ref[grp]
            elif name == "sg_b":
                src[r0:r0 + 8, :] = ref[...].T[0:8, :]
            else:
                for j in range(ref.shape[1] // 128):
                    src[r0 + j:r0 + j + 1, :] = ref[:, j * 128:(j + 1) * 128]
        src[REL_ROW:REL_ROW + 32, 0:16] = ins[-2][...]
        src[LOSS_ROW:LOSS_ROW + 1, :] = ins[-1][...]
        _allreduce_rows(src, sib_buf, chips, out_ref, ssem, rsem)

    return pl.pallas_call(
        body, name="small_allreduce",
        out_shape=jax.ShapeDtypeStruct((SMALL_ROWS, 128), F32),
        scratch_shapes=[pltpu.VMEM((SMALL_ROWS, 128), F32)] + _allreduce_scratch(SMALL_ROWS),
        compiler_params=_params(),
    )(*flat)


def _pad_lanes(v):
    return jnp.zeros((1, 128), F32).at[0, :v.shape[0]].set(v)


def layer_fwd(x, wts, bias):
    proj, h = inproj_fwd(x, wts["g_pre"], wts["wp"])
    y_att = att_fwd(proj, bias, wts["sinks"])
    y_sg = sgu_fwd(proj, wts["ln_g"], wts["ln_b"], wts["sg_w"], wts["sg_bt"])
    y_ssm, hst = ssd_fwd(proj, wts["conv_w"], wts["conv_b"], wts["dt_bias"], wts["a_log"], wts["d_skip"],
                         wts["norm_g"])
    x_new, br_a, br_s, br_m, merged, out = merge_fwd(
        y_att, y_sg, y_ssm, proj, x, wts["w_a"], wts["w_s"], wts["w_m"], wts["w_o"], wts["g_post"])
    saved = dict(x=x, proj=proj, h=h, y_att=y_att, y_sg=y_sg, y_ssm=y_ssm, hst=hst,
                 br_a=br_a, br_s=br_s, br_m=br_m, merged=merged, out=out)
    return x_new, saved


def layer_bwd(dy, wts, bias, sv):
    proj = sv["proj"]
    dout, dba, dbs, dbm, dgates, dya, dys, dym, dg_post = merge_bwd(
        dy, sv["out"], wts["g_post"], proj, sv["br_a"], sv["br_s"], sv["br_m"],
        wts["w_a"], wts["w_s"], wts["w_m"], wts["w_o"])
    dq, dza, dk, dv, dbias, dsinks = att_bwd(dya, proj, bias, wts["sinks"])
    dsgu, dsg_w, dsg_bt, dln_g, dln_b = sgu_bwd(dys, proj, wts["ln_g"], wts["ln_b"], wts["sg_w"], wts["sg_bt"])
    dzm, dxbc, ddt, dcw, dcb, ddtb, dalog, ddsk, dng = ssd_bwd(
        dym, proj, sv["hst"], wts["conv_w"], wts["conv_b"], wts["dt_bias"], wts["a_log"], wts["d_skip"],
        wts["norm_g"])
    S = dy.shape[0]
    dproj = jnp.concatenate([dxbc, dgates, dzm, dq, dza, dsgu, dk, dv, ddt,
                             jnp.zeros((S, NCP - O_DT - 128), BF16)], axis=1)
    dx, dg_pre = inproj_bwd(dproj, wts["wp"], sv["x"], wts["g_pre"], dy)
    grads = dict(
        w_in=matmul_tn(sv["h"], dproj, "dw_in", tn=1536),
        w_a=matmul_tn(sv["y_att"], dba, "dw_att"),
        w_s=matmul_tn(sv["y_sg"], dbs, "dw_sg"),
        w_m=matmul_tn(sv["y_ssm"], dbm, "dw_ssm"),
        w_o=matmul_tn(sv["merged"], dout, "dw_out"),
        g_pre=dg_pre, g_post=dg_post, sinks=dsinks, ln_g=dln_g, ln_b=dln_b, sg_w=dsg_w, sg_bt=dsg_bt,
        conv_w=dcw, conv_b=dcb, dt_bias=ddtb, a_log=dalog, d_skip=ddsk, norm_g=dng, bias=dbias)
    return dx, grads


REST_OFF = (0, 256, 512, 1024, 1280)
REST_ROWS = 1296


def kernel(x, w_in, norm_pre, norm_post, rel_bias, att_sinks, sg_ln_g, sg_ln_b, sg_w, sg_b, ssm_conv_w, ssm_conv_b, ssm_dt_bias, ssm_a_log, ssm_d, ssm_norm_g, w_br_att, w_br_sg, w_br_ssm, w_out, loss_target, m_w_in, m_norm_pre, m_norm_post, m_rel_bias, m_att_sinks, m_sg_ln_g, m_sg_ln_b, m_sg_w, m_sg_b, m_ssm_conv_w, m_ssm_conv_b, m_ssm_dt_bias, m_ssm_a_log, m_ssm_d, m_ssm_norm_g, m_w_br_att, m_w_br_sg, m_w_br_ssm, m_w_out, v_w_in, v_norm_pre, v_norm_post, v_rel_bias, v_att_sinks, v_sg_ln_g, v_sg_ln_b, v_sg_w, v_sg_b, v_ssm_conv_w, v_ssm_conv_b, v_ssm_dt_bias, v_ssm_a_log, v_ssm_d, v_ssm_norm_g, v_w_br_att, v_w_br_sg, v_w_br_ssm, v_w_out):
    cx, cy, cc = lax.axis_index("x"), lax.axis_index("y"), lax.axis_index("c")
    me = 2 * cx + cy
    xs = x[0]
    S = xs.shape[0]

    w_in_b = w_in.astype(BF16)
    w_rest_b = jnp.concatenate([w_br_att, w_br_sg, w_br_ssm, w_out], axis=1).astype(BF16)
    all_in, all_rest = gather_weights([w_in_b, w_rest_b])
    convw_slot = jnp.zeros((SHARDS, DEPTH * CONV_K * 768 // 128, 128), F32)
    convw_slot = lax.dynamic_update_index_in_dim(
        convw_slot, jnp.where(cc == 0, 1.0, 0.0) * ssm_conv_w.reshape(-1, 128), me, 0)
    convw_all = allreduce_rows(convw_slot.reshape(-1, 128), "gather_conv_w")
    convw_all = convw_all.reshape(SHARDS, DEPTH, CONV_K, 768).transpose(1, 2, 0, 3).reshape(DEPTH, CONV_K, CONV_C)

    def shards_of(gathered, mine, l, lo, hi):
        return [jnp.where(me == s, mine[l, lo:hi], gathered[l, s, lo:hi]) for s in range(SHARDS)]

    o = REST_OFF
    layers = []
    for l in range(DEPTH):
        w_in_full = jnp.concatenate(shards_of(all_in, w_in_b, l, 0, 1024), axis=1)
        rest = lambda k: jnp.concatenate(shards_of(all_rest, w_rest_b, l, o[k], o[k + 1]), axis=0)
        layers.append(dict(
            wp=to_padded_cols(w_in_full),
            w_a=rest(0), w_s=rest(1), w_m=rest(2), w_o=rest(3),
            g_pre=norm_pre[l][None], g_post=norm_post[l][None], sinks=att_sinks[l],
            ln_g=sg_ln_g[l][None], ln_b=sg_ln_b[l][None], sg_w=sg_w[l],
            sg_bt=sg_b[l].T,
            conv_w=jnp.concatenate([convw_all[l], jnp.zeros((4, CONV_C), F32)], axis=0),
            conv_b=ssm_conv_b[l][None], dt_bias=_pad_lanes(ssm_dt_bias[l]), a_log=_pad_lanes(ssm_a_log[l]),
            d_skip=_pad_lanes(ssm_d[l]), norm_g=ssm_norm_g[l][None]))

    bias = bias_table(rel_bias)
    saved = []
    act = xs
    for l in range(DEPTH):
        act, sv = layer_fwd(act, layers[l], bias)
        saved.append(sv)
    dy, loss_part = loss_head(act, loss_target[0])
    grads = [None] * DEPTH
    for l in reversed(range(DEPTH)):
        dy, grads[l] = layer_bwd(dy, layers[l], bias, saved[l])
    grad_x = dy[None]
    grad_rel_local = bias_grad(grads[0]["bias"] + grads[1]["bias"])

    cvec = jnp.reshape(cc, (1,)).astype(jnp.int32)
    mvec = jnp.reshape(me, (1,)).astype(jnp.int32)
    g_in, g_rest = [], []
    for l in range(DEPTH):
        g = grads[l]
        g_in.append(from_padded_cols(g["w_in"]).reshape(1024, SHARDS, 3400).transpose(1, 0, 2))
        gcw = g["conv_w"][0:CONV_K].reshape(CONV_K, SHARDS, 768).transpose(1, 0, 2).reshape(SHARDS, 3, 1024)
        g_rest.append(jnp.concatenate([
            g["w_a"].reshape(SHARDS, 256, D), g["w_s"].reshape(SHARDS, 256, D), g["w_m"].reshape(SHARDS, 512, D),
            g["w_o"].reshape(SHARDS, 256, D), jnp.pad(gcw, ((0, 0), (0, REST_ROWS - REST_OFF[4] - 3), (0, 0)))],
            axis=1))
    g_in, g_rest = jnp.stack(g_in), jnp.stack(g_rest)
    sb_in, sb_rest = grad_sibling_exchange([g_in, g_rest])
    t_in, t_in_b = grad_chip_sum(g_in, sb_in, cvec, 128, "chip_sum_w_in")
    t_rest, t_rest_b = grad_chip_sum(g_rest, sb_rest, cvec, 432, "chip_sum_rest")
    rb_in, rb_rest = grad_chip_exchange([t_in_b, t_rest_b])
    f_in = grad_shard_sum(t_in, rb_in, mvec, 128, "shard_sum_w_in")
    f_rest = grad_shard_sum(t_rest, rb_rest, mvec, 432, "shard_sum_rest")
    fb_in, fb_rest = grad_sibling_share([f_in, f_rest])

    red = small_allreduce(grads, grad_rel_local, loss_part)
    loss = red[LOSS_ROW, 0]

    res = adamw_small(red, (rel_bias, m_rel_bias, v_rel_bias), dict(
        norm_pre=(norm_pre, m_norm_pre, v_norm_pre), norm_post=(norm_post, m_norm_post, v_norm_post),
        att_sinks=(att_sinks, m_att_sinks, v_att_sinks), sg_ln_g=(sg_ln_g, m_sg_ln_g, v_sg_ln_g),
        sg_ln_b=(sg_ln_b, m_sg_ln_b, v_sg_ln_b), sg_w=(sg_w, m_sg_w, v_sg_w), sg_b=(sg_b, m_sg_b, v_sg_b),
        ssm_conv_b=(ssm_conv_b, m_ssm_conv_b, v_ssm_conv_b), ssm_dt_bias=(ssm_dt_bias, m_ssm_dt_bias, v_ssm_dt_bias),
        ssm_a_log=(ssm_a_log, m_ssm_a_log, v_ssm_a_log), ssm_d=(ssm_d, m_ssm_d, v_ssm_d),
        ssm_norm_g=(ssm_norm_g, m_ssm_norm_g, v_ssm_norm_g)))
    res["w_in"] = adamw_big(w_in, m_w_in, v_w_in, f_in, fb_in, cvec, "adamw_w_in", 128)
    res["w_br_att"] = adamw_big(w_br_att, m_w_br_att, v_w_br_att, f_rest, fb_rest, cvec, "adamw_w_br_att", 256, o[0])
    res["w_br_sg"] = adamw_big(w_br_sg, m_w_br_sg, v_w_br_sg, f_rest, fb_rest, cvec, "adamw_w_br_sg", 256, o[1])
    res["w_br_ssm"] = adamw_big(w_br_ssm, m_w_br_ssm, v_w_br_ssm, f_rest, fb_rest, cvec, "adamw_w_br_ssm", 512, o[2])
    res["w_out"] = adamw_big(w_out, m_w_out, v_w_out, f_rest, fb_rest, cvec, "adamw_w_out", 256, o[3])
    cw_mine = f_rest[o[4]:o[4] + 3].reshape(CONV_K, 768)
    cw_sib = fb_rest[o[4]:o[4] + 3].reshape(CONV_K, 768)
    g_conv_w = jnp.stack([jnp.where(cc == l, cw_mine, cw_sib) for l in range(DEPTH)])
    res["ssm_conv_w"] = (g_conv_w,) + tuple(adamw_plain(ssm_conv_w, g_conv_w, m_ssm_conv_w, v_ssm_conv_w, "adamw_conv_w"))

    order = ["w_in", "norm_pre", "norm_post", "rel_bias", "att_sinks", "sg_ln_g", "sg_ln_b", "sg_w", "sg_b",
             "ssm_conv_w", "ssm_conv_b", "ssm_dt_bias", "ssm_a_log", "ssm_d", "ssm_norm_g",
             "w_br_att", "w_br_sg", "w_br_ssm", "w_out"]
    return (loss, grad_x, *[res[n][0] for n in order], *[res[n][1] for n in order],
            *[res[n][2] for n in order], *[res[n][3] for n in order])
```

```python
import functools
import math

import numpy as np
import jax
import jax.numpy as jnp
from jax import lax
from jax.experimental import pallas as pl
from jax.experimental.pallas import tpu as pltpu

F32 = jnp.float32
BF16 = jnp.bfloat16
MESH = pl.DeviceIdType.MESH

D = 1024
DEPTH = 2
EPS = 1e-6
L = 128
HEADS = 16
KV = 2
DH = 64
SSM_W = 2048
SSM_H = 32
SSM_P = 64
SSM_G = 4
SSM_N = 128
CONV_K = 4
CONV_C = 3072
NEG = -1e30
IN_COLS = 13600
NCP = 13824
TCOL = 768

GROUPS = (("gate", 3072), ("sgu", 3072), ("att", 2304), ("ssd", 5376))
G_OFF = {"gate": 0, "sgu": 3072, "att": 6144, "ssd": 8448}

ADAM_LR = 0.001
ADAM_B1 = 0.9
ADAM_B2 = 0.999
ADAM_EPS = 1e-08
ADAM_WD = 0.01
ADAM_STEP = 10

VMEM_LIMIT = 56 * 1024 * 1024

PACK_ROWS = 4704
PACK_TILE = 224
SHARDS = 4


def _dot(a, b):
    return jnp.dot(a, b, preferred_element_type=F32)


def _dot_nt(a, b):
    return lax.dot_general(a, b, (((1,), (1,)), ((), ())), preferred_element_type=F32)


def _dot_tn(a_f32, b):
    return jnp.dot(a_f32.T.astype(BF16), b, preferred_element_type=F32)


def _dot_hi(a, b):
    return jnp.dot(a, b, preferred_element_type=F32, precision=lax.Precision.HIGHEST)


def _pieces(x, n):
    out = []
    for _ in range(n - 1):
        p = x.astype(BF16)
        out.append(p)
        x = x - p.astype(F32)
    out.append(x.astype(BF16))
    return out


def _dot_sel(a, sel, n):
    sel = sel.astype(BF16)
    acc = None
    for p in _pieces(a, n):
        t = _dot(p, sel)
        acc = t if acc is None else acc + t
    return acc


def _sel_dot(sel, b, n):
    sel = sel.astype(BF16)
    acc = None
    for p in _pieces(b, n):
        t = _dot(sel, p)
        acc = t if acc is None else acc + t
    return acc


def _sigmoid(x):
    return 1.0 / (1.0 + jnp.exp(-x))


def _softplus(x):
    return jnp.maximum(x, 0.0) + jnp.log(1.0 + jnp.exp(-jnp.abs(x)))


def _params(sem=None, vmem=VMEM_LIMIT):
    kw = dict(vmem_limit_bytes=vmem)
    if sem is not None:
        kw["dimension_semantics"] = sem
    return pltpu.CompilerParams(**kw)


def _full(shape):
    nd = len(shape)
    return pl.BlockSpec(shape, lambda *_: (0,) * nd)


def to_padded_cols(w):
    pad = jnp.zeros(w.shape[:-1] + (224,), w.dtype)
    return jnp.concatenate([
        w[..., 10528:13600],
        w[..., 2304:5376],
        w[..., 0:1024], w[..., 1280:2304], w[..., 1024:1152], w[..., 1152:1280],
        w[..., 5376:7424], w[..., 7424:10496], w[..., 10496:10528], pad], axis=-1)


def from_padded_cols(g):
    a, s = G_OFF["att"], G_OFF["ssd"]
    return jnp.concatenate([
        g[..., a:a + 1024], g[..., a + 2048:a + 2304], g[..., a + 1024:a + 2048],
        g[..., 3072:6144], g[..., s:s + 5120], g[..., s + 5120:s + 5152],
        g[..., 0:3072]], axis=-1)


def _bucket_table():
    qi = np.arange(L)[:, None]
    kj = np.arange(2 * L)[None, :]
    dist = np.maximum(qi + L - kj, 0)
    dist_f = np.maximum(dist, 1).astype(np.float32)
    large = 16 + (np.log(dist_f / np.float32(16)) / np.float32(math.log(128 / 16)) * np.float32(16)).astype(np.int32)
    large = np.minimum(large, 31)
    return np.where(dist < 16, dist, large).astype(np.int32)


def bias_table(rel_bias):
    buckets = jnp.asarray(_bucket_table().reshape(1, L * 2 * L))

    def body(rb_ref, bk_ref, out_ref):
        onehot = (lax.broadcasted_iota(jnp.int32, (32, L * 2 * L), 0) == bk_ref[...]).astype(F32)
        out_ref[...] = lax.dot_general(rb_ref[...], onehot, (((0,), (0,)), ((), ())),
                                       preferred_element_type=F32, precision=lax.Precision.HIGHEST)

    out = pl.pallas_call(
        body, name="bias_table",
        out_shape=jax.ShapeDtypeStruct((HEADS, L * 2 * L), F32),
        compiler_params=_params(),
    )(rel_bias, buckets)
    return out.reshape(HEADS, L, 2 * L)


def bias_grad(dbias):
    buckets = jnp.asarray(_bucket_table().reshape(1, L * 2 * L))

    def body(db_ref, bk_ref, out_ref):
        onehot = (lax.broadcasted_iota(jnp.int32, (32, L * 2 * L), 0) == bk_ref[...]).astype(F32)
        out_ref[...] = lax.dot_general(onehot, db_ref[...], (((1,), (1,)), ((), ())),
                                       preferred_element_type=F32, precision=lax.Precision.HIGHEST)

    return pl.pallas_call(
        body, name="bias_grad",
        out_shape=jax.ShapeDtypeStruct((32, HEADS), F32),
        compiler_params=_params(),
    )(dbias.reshape(HEADS, L * 2 * L), buckets)


def _group_tiles():
    out, t = [], 0
    for _, w in GROUPS:
        out.append((t, w // TCOL))
        t += w // TCOL
    return out


def _in_group(j, t0, n):
    return (j >= t0) & (j < t0 + n)


def inproj_fwd(x, g_pre, wp):
    S = x.shape[0]
    tm = 1024 if S % 1024 == 0 else 512
    tiles = _group_tiles()

    def body(x_ref, g_ref, w_ref, *outs):
        h_ref = outs[-1]
        j = pl.program_id(1)

        @pl.when(j == 0)
        def _():
            xv = x_ref[...]
            r = lax.rsqrt(jnp.mean(xv * xv, axis=-1, keepdims=True) + EPS)
            h_ref[...] = (xv * r * g_ref[...]).astype(BF16)

        res = _dot(h_ref[...], w_ref[...])
        for (t0, n), o_ref in zip(tiles, outs[:-1]):
            @pl.when(_in_group(j, t0, n))
            def _(o_ref=o_ref):
                o_ref[...] = res

    gspec = lambda t0, n: pl.BlockSpec((tm, TCOL), lambda i, j: (i, jnp.clip(j - t0, 0, n - 1)))
    return pl.pallas_call(
        body, name="inproj_fwd", grid=(S // tm, NCP // TCOL),
        in_specs=[pl.BlockSpec((tm, D), lambda i, j: (i, 0)), _full((1, D)),
                  pl.BlockSpec((D, TCOL), lambda i, j: (0, j))],
        out_specs=[gspec(t0, n) for t0, n in tiles] + [pl.BlockSpec((tm, D), lambda i, j: (i, 0))],
        out_shape=[jax.ShapeDtypeStruct((S, w), F32) for _, w in GROUPS] + [jax.ShapeDtypeStruct((S, D), BF16)],
        compiler_params=_params(("arbitrary", "arbitrary")),
    )(x, g_pre, wp)


def inproj_bwd(dps, wp, x, g_pre, dy):
    S = x.shape[0]
    tm, tk = (1024 if S % 1024 == 0 else 512), TCOL
    nk = NCP // tk
    tiles = _group_tiles()

    def body(*refs):
        dp_refs = refs[:len(tiles)]
        w_ref, x_ref, g_ref, dy_ref, dx_ref, dg_ref, acc = refs[len(tiles):]
        i, k = pl.program_id(0), pl.program_id(1)

        @pl.when(k == 0)
        def _():
            acc[...] = jnp.zeros_like(acc)

        for (t0, n), dp_ref in zip(tiles, dp_refs):
            @pl.when(_in_group(k, t0, n))
            def _(dp_ref=dp_ref):
                acc[...] += _dot_nt(dp_ref[...], w_ref[...])

        @pl.when((k == nk - 1) & (i == 0))
        def _():
            dg_ref[...] = jnp.zeros_like(dg_ref)

        @pl.when(k == nk - 1)
        def _():
            xv = x_ref[...]
            dh = acc[...]
            g = g_ref[...]
            r = lax.rsqrt(jnp.mean(xv * xv, axis=-1, keepdims=True) + EPS)
            dhg = dh * g
            dx_ref[...] = dy_ref[...] + r * dhg - xv * (r * r * r) * jnp.mean(dhg * xv, axis=-1, keepdims=True)
            dg_ref[...] += jnp.sum(dh * xv * r, axis=0, keepdims=True)

    gspec = lambda t0, n: pl.BlockSpec((tm, tk), lambda i, k: (i, jnp.clip(k - t0, 0, n - 1)))
    return pl.pallas_call(
        body, name="inproj_bwd", grid=(S // tm, nk),
        in_specs=[gspec(t0, n) for t0, n in tiles] + [
            pl.BlockSpec((D, tk), lambda i, k: (0, k)),
            pl.BlockSpec((tm, D), lambda i, k: (i, 0)), _full((1, D)),
            pl.BlockSpec((tm, D), lambda i, k: (i, 0))],
        out_specs=[pl.BlockSpec((tm, D), lambda i, k: (i, 0)), _full((1, D))],
        out_shape=[jax.ShapeDtypeStruct((S, D), F32), jax.ShapeDtypeStruct((1, D), F32)],
        scratch_shapes=[pltpu.VMEM((tm, D), F32)],
        compiler_params=_params(("arbitrary", "arbitrary")),
    )(*dps, wp, x, g_pre, dy)


def matmul_tn_groups(a, bs, name, ts=512):
    S, K = a.shape
    ns = S // ts
    tiles = _group_tiles()

    def body(a_ref, *refs):
        b_refs, o_ref = refs[:-1], refs[-1]
        j = pl.program_id(0)

        @pl.when(pl.program_id(1) == 0)
        def _():
            o_ref[...] = jnp.zeros_like(o_ref)

        at = a_ref[...].astype(F32).T.astype(BF16)
        for (t0, n), b_ref in zip(tiles, b_refs):
            @pl.when(_in_group(j, t0, n))
            def _(b_ref=b_ref):
                o_ref[...] += _dot(at, b_ref[...])

    gspec = lambda t0, n: pl.BlockSpec(
        (ts, TCOL), lambda j, s: (jnp.where(_in_group(j, t0, n), s, 0), jnp.clip(j - t0, 0, n - 1)))
    return pl.pallas_call(
        body, name=name, grid=(NCP // TCOL, ns),
        in_specs=[pl.BlockSpec((ts, K), lambda j, s: (s, 0))] + [gspec(t0, n) for t0, n in tiles],
        out_specs=pl.BlockSpec((K, TCOL), lambda j, s: (0, j)),
        out_shape=jax.ShapeDtypeStruct((K, NCP), F32),
        compiler_params=_params(("arbitrary", "arbitrary")),
    )(a, *bs)


def matmul_tn(a, b, name, tn=512, ts=512):
    S, K = a.shape
    N = b.shape[1]
    ns = S // ts

    def body(a_ref, b_ref, o_ref):
        @pl.when(pl.program_id(1) == 0)
        def _():
            o_ref[...] = jnp.zeros_like(o_ref)
        o_ref[...] += _dot_tn(a_ref[...].astype(F32), b_ref[...])

    return pl.pallas_call(
        body, name=name, grid=(N // tn, ns),
        in_specs=[pl.BlockSpec((ts, K), lambda j, s: (s, 0)), pl.BlockSpec((ts, tn), lambda j, s: (s, j))],
        out_specs=pl.BlockSpec((K, tn), lambda j, s: (0, j)),
        out_shape=jax.ShapeDtypeStruct((K, N), F32),
        compiler_params=_params(("arbitrary", "arbitrary")),
    )(a, b)


def _att_mask(n):
    qi = lax.broadcasted_iota(jnp.int32, (L, 2 * L), 0)
    kj = lax.broadcasted_iota(jnp.int32, (L, 2 * L), 1)
    dist = qi + L - kj
    return (dist >= 0) & (dist < L) & ((kj >= L) | (n > 0))


def _att_in_specs(nb):
    last = nb - 1
    cur = lambda n: jnp.minimum(n, last)
    prev = lambda n: jnp.maximum(jnp.minimum(n, last) - 1, 0)
    return [
        pl.BlockSpec((L, 1024), lambda n: (cur(n), 0)),
        pl.BlockSpec((L, 128), lambda n: (prev(n), 16)),
        pl.BlockSpec((L, 128), lambda n: (cur(n), 16)),
        pl.BlockSpec((L, 128), lambda n: (prev(n), 17)),
        pl.BlockSpec((L, 128), lambda n: (cur(n), 17)),
        pl.BlockSpec((L, 1024), lambda n: (cur(n), 1)),
        _full((HEADS, L, 2 * L)),
        pl.BlockSpec(memory_space=pltpu.SMEM),
    ]


def _att_probs(qh, kk, bias_h, mask, sk):
    logits = _dot_nt(qh, kk) + bias_h
    logits = jnp.where(mask, logits, NEG)
    m = jnp.maximum(jnp.max(logits, axis=-1, keepdims=True), sk)
    p = jnp.exp(logits - m)
    es = jnp.exp(sk - m)
    den = jnp.sum(p, axis=-1, keepdims=True) + es
    return p / den, es / den


def att_fwd(proj, bias, sinks):
    S = proj.shape[0]
    nb = S // L

    def body(q_ref, kp_ref, kc_ref, vp_ref, vc_ref, z_ref, bias_ref, s_ref, y_ref, o_scr):
        mask = _att_mask(pl.program_id(0))
        for kv in range(KV):
            sl = slice(kv * DH, (kv + 1) * DH)
            kk = jnp.concatenate([kp_ref[:, sl], kc_ref[:, sl]], axis=0).astype(BF16)
            vv = jnp.concatenate([vp_ref[:, sl], vc_ref[:, sl]], axis=0).astype(BF16)
            for g in range(HEADS // KV):
                h = kv * (HEADS // KV) + g
                hs = slice(h * DH, (h + 1) * DH)
                qh = (q_ref[:, hs] * 0.125).astype(BF16)
                P, _ = _att_probs(qh, kk, bias_ref[h], mask, s_ref[h])
                o_scr[:, hs] = _dot(P.astype(BF16), vv)
        z = z_ref[...]
        y_ref[...] = (o_scr[...] * (z * _sigmoid(z))).astype(BF16)

    return pl.pallas_call(
        body, name="att_fwd", grid=(nb,),
        in_specs=_att_in_specs(nb),
        out_specs=pl.BlockSpec((L, 1024), lambda n: (n, 0)),
        out_shape=jax.ShapeDtypeStruct((S, 1024), BF16),
        scratch_shapes=[pltpu.VMEM((L, 1024), F32)],
        compiler_params=_params(("arbitrary",)),
    )(proj, proj, proj, proj, proj, proj, bias, sinks)


def att_bwd(dy, proj, bias, sinks):
    S = proj.shape[0]
    nb = S // L
    last = nb - 1

    def body(dy_ref, q_ref, kp_ref, kc_ref, vp_ref, vc_ref, z_ref, bias_ref, s_ref,
             dout_ref, dbias_ref, dsink_ref, carry, band, dq_scr, dz_scr):
        n = pl.program_id(0)

        @pl.when(n == 0)
        def _():
            carry[...] = jnp.zeros_like(carry)
            dq_scr[...] = jnp.zeros_like(dq_scr)
            dz_scr[...] = jnp.zeros_like(dz_scr)
            dbias_ref[...] = jnp.zeros_like(dbias_ref)
            dsink_ref[...] = jnp.zeros_like(dsink_ref)

        dout_ref[:, 0:1024] = dq_scr[...].astype(BF16)
        dout_ref[:, 1024:2048] = dz_scr[...].astype(BF16)
        band[...] = jnp.zeros_like(band)

        @pl.when(n < nb)
        def _():
            mask = _att_mask(n)
            lane = lax.broadcasted_iota(jnp.int32, (1, 128), 1)
            dsink = jnp.zeros((1, 128), F32)
            for kv in range(KV):
                sl = slice(kv * DH, (kv + 1) * DH)
                kk = jnp.concatenate([kp_ref[:, sl], kc_ref[:, sl]], axis=0).astype(BF16)
                vv = jnp.concatenate([vp_ref[:, sl], vc_ref[:, sl]], axis=0).astype(BF16)
                dk_acc = jnp.zeros((2 * L, DH), F32)
                dv_acc = jnp.zeros((2 * L, DH), F32)
                for g in range(HEADS // KV):
                    h = kv * (HEADS // KV) + g
                    hs = slice(h * DH, (h + 1) * DH)
                    qh = (q_ref[:, hs] * 0.125).astype(BF16)
                    P, psink = _att_probs(qh, kk, bias_ref[h], mask, s_ref[h])
                    Pb = P.astype(BF16)
                    zh = z_ref[:, hs]
                    sg = _sigmoid(zh)
                    dyh = dy_ref[:, hs]
                    O = _dot(Pb, vv)
                    dO = dyh * (zh * sg)
                    dz_scr[:, hs] = dyh * O * (sg * (1.0 + zh * (1.0 - sg)))
                    dOb = dO.astype(BF16)
                    dv_acc = dv_acc + _dot_tn(P, dOb)
                    dP = _dot_nt(dOb, vv)
                    delta = jnp.sum(P * dP, axis=-1, keepdims=True)
                    dS = P * (dP - delta)
                    dsink = dsink + jnp.where(lane == h, -jnp.sum(psink * delta), 0.0)
                    dSb = dS.astype(BF16)
                    dq_scr[:, hs] = _dot(dSb, kk) * 0.125
                    dk_acc = dk_acc + _dot_tn(dS, qh)
                    dbias_ref[h] += dS
                band[:, sl] = dk_acc
                band[:, 128 + kv * DH:128 + (kv + 1) * DH] = dv_acc
            dsink_ref[...] += dsink

        out = carry[...] + band[0:L, :]
        dout_ref[:, 2048:2304] = out.astype(BF16)
        carry[...] = band[L:2 * L, :]

    cur = lambda n: jnp.minimum(n, last)
    lag = lambda n: jnp.maximum(n - 1, 0)
    return pl.pallas_call(
        body, name="att_bwd", grid=(nb + 1,),
        in_specs=[pl.BlockSpec((L, 1024), lambda n: (cur(n), 0))] + _att_in_specs(nb),
        out_specs=[pl.BlockSpec((L, 2304), lambda n: (lag(n), 0)), _full((HEADS, L, 2 * L)), _full((1, 128))],
        out_shape=[jax.ShapeDtypeStruct((S, 2304), BF16),
                   jax.ShapeDtypeStruct((HEADS, L, 2 * L), F32), jax.ShapeDtypeStruct((1, 128), F32)],
        scratch_shapes=[pltpu.VMEM((L, 256), F32), pltpu.VMEM((2 * L, 256), F32),
                        pltpu.VMEM((L, 1024), F32), pltpu.VMEM((L, 1024), F32)],
        compiler_params=_params(("arbitrary",)),
    )(dy, proj, proj, proj, proj, proj, proj, bias, sinks)


def _sgu_in_specs():
    return [
        pl.BlockSpec((L, 1024), lambda c: (c, 0)),
        pl.BlockSpec((L, 1024), lambda c: (c, 1)),
        pl.BlockSpec((L, 1024), lambda c: (c, 2)),
        _full((1, 1024)), _full((1, 1024)), _full((8, L, L)), _full((L, 8)),
    ]


def _sgu_norm(v, lg, lb):
    mu = jnp.mean(v, axis=-1, keepdims=True)
    vc = v - mu
    rstd = lax.rsqrt(jnp.mean(vc * vc, axis=-1, keepdims=True) + EPS)
    xhat = vc * rstd
    return xhat * lg + lb, xhat, rstd


def _tril():
    return lax.broadcasted_iota(jnp.int32, (L, L), 0) >= lax.broadcasted_iota(jnp.int32, (L, L), 1)


def sgu_fwd(proj, ln_g, ln_b, w, b_t):
    S = proj.shape[0]

    def body(u_ref, v_ref, z_ref, lg_ref, lb_ref, w_ref, bt_ref, y_ref):
        vn, _, _ = _sgu_norm(v_ref[...], lg_ref[...], lb_ref[...])
        tri = _tril()
        parts = []
        for g in range(8):
            wg = jnp.where(tri, w_ref[g], 0.0).astype(BF16)
            parts.append(_dot(wg, vn[:, g * 128:(g + 1) * 128].astype(BF16)) + bt_ref[:, g:g + 1])
        mixed = jnp.concatenate(parts, axis=1)
        z = z_ref[...]
        y_ref[...] = (u_ref[...] * mixed * (z * _sigmoid(z))).astype(BF16)

    return pl.pallas_call(
        body, name="sgu_fwd", grid=(S // L,),
        in_specs=_sgu_in_specs(),
        out_specs=pl.BlockSpec((L, 1024), lambda c: (c, 0)),
        out_shape=jax.ShapeDtypeStruct((S, 1024), BF16),
        compiler_params=_params(("arbitrary",)),
    )(proj, proj, proj, ln_g, ln_b, w, b_t)


def sgu_bwd(dy, proj, ln_g, ln_b, w, b_t):
    S = proj.shape[0]

    def body(dy_ref, u_ref, v_ref, z_ref, lg_ref, lb_ref, w_ref, bt_ref,
             dout_ref, dw_ref, dbt_ref, dlg_ref, dlb_ref):
        @pl.when(pl.program_id(0) == 0)
        def _():
            dw_ref[...] = jnp.zeros_like(dw_ref)
            dbt_ref[...] = jnp.zeros_like(dbt_ref)
            dlg_ref[...] = jnp.zeros_like(dlg_ref)
            dlb_ref[...] = jnp.zeros_like(dlb_ref)

        lg = lg_ref[...]
        vn, xhat, rstd = _sgu_norm(v_ref[...], lg, lb_ref[...])
        tri = _tril()
        lane = lax.broadcasted_iota(jnp.int32, (L, 128), 1)
        wgs, parts = [], []
        for g in range(8):
            wg = jnp.where(tri, w_ref[g], 0.0)
            wgs.append(wg)
            parts.append(_dot(wg.astype(BF16), vn[:, g * 128:(g + 1) * 128].astype(BF16)) + bt_ref[:, g:g + 1])
        mixed = jnp.concatenate(parts, axis=1)
        z = z_ref[...]
        sg = _sigmoid(z)
        silu = z * sg
        dy_v = dy_ref[...]
        u = u_ref[...]
        dout_ref[:, 0:1024] = (dy_v * mixed * silu).astype(BF16)
        dout_ref[:, 2048:3072] = (dy_v * u * mixed * (sg * (1.0 + z * (1.0 - sg)))).astype(BF16)
        dmixed = dy_v * u * silu
        dbt = jnp.zeros((L, 128), F32)
        dvn_parts = []
        for g in range(8):
            dm = dmixed[:, g * 128:(g + 1) * 128]
            dmb = dm.astype(BF16)
            dbt = dbt + jnp.where(lane == g, jnp.sum(dm, axis=1, keepdims=True), 0.0)
            dw_ref[g] += jnp.where(tri, _dot_nt(dmb, vn[:, g * 128:(g + 1) * 128].astype(BF16)), 0.0)
            dvn_parts.append(_dot_tn(wgs[g], dmb))
        dbt_ref[...] += dbt
        dvn = jnp.concatenate(dvn_parts, axis=1)
        dlg_ref[...] += jnp.sum(dvn * xhat, axis=0, keepdims=True)
        dlb_ref[...] += jnp.sum(dvn, axis=0, keepdims=True)
        dxh = dvn * lg
        dv = rstd * (dxh - jnp.mean(dxh, axis=-1, keepdims=True)
                     - xhat * jnp.mean(dxh * xhat, axis=-1, keepdims=True))
        dout_ref[:, 1024:2048] = dv.astype(BF16)

    return pl.pallas_call(
        body, name="sgu_bwd", grid=(S // L,),
        in_specs=[pl.BlockSpec((L, 1024), lambda c: (c, 0))] + _sgu_in_specs(),
        out_specs=[pl.BlockSpec((L, 3072), lambda c: (c, 0)), _full((8, L, L)), _full((L, 128)),
                   _full((1, 1024)), _full((1, 1024))],
        out_shape=[jax.ShapeDtypeStruct((S, 3072), BF16), jax.ShapeDtypeStruct((8, L, L), F32),
                   jax.ShapeDtypeStruct((L, 128), F32), jax.ShapeDtypeStruct((1, 1024), F32),
                   jax.ShapeDtypeStruct((1, 1024), F32)],
        compiler_params=_params(("arbitrary",)),
    )(dy, proj, proj, proj, ln_g, ln_b, w, b_t)


def _expand_matrix():
    r = lax.broadcasted_iota(jnp.int32, (128, SSM_W), 0)
    c = lax.broadcasted_iota(jnp.int32, (128, SSM_W), 1)
    return (c // SSM_P) == r


def _expand_matrix_t():
    r = lax.broadcasted_iota(jnp.int32, (SSM_W, 128), 0)
    c = lax.broadcasted_iota(jnp.int32, (SSM_W, 128), 1)
    return (r // SSM_P) == c


def _rows_from(ref, start):
    C = ref.shape[1]
    tiles = ref[...].reshape(17, 8, C)
    q, s = divmod(start, 8)
    if s == 0:
        return tiles[q:q + 16].reshape(L, C)
    rolled = pltpu.roll(tiles, 8 - s, axis=1)
    sub = lax.broadcasted_iota(jnp.int32, (16, 8, C), 1)
    return jnp.where(sub < 8 - s, rolled[q:q + 16], rolled[q + 1:q + 17]).reshape(L, C)


def _ssd_common(ext_ref, cw_ref, cb_ref, dt_raw, dtb, alog):
    taps = [_rows_from(ext_ref, 5 + k) for k in range(CONV_K)]
    pre = cb_ref[...]
    for k in range(CONV_K):
        pre = pre + cw_ref[k:k + 1, :] * taps[k]
    sg_pre = _sigmoid(pre)
    xc = pre * sg_pre
    dt = _softplus(dt_raw + dtb)
    a = -jnp.exp(alog)
    adt = dt * a
    acs = _sel_dot(_tril(), adt, 3)
    return pre, sg_pre, xc, dt, a, acs, taps


def _ssd_in_specs(rev, nc):
    cidx = (lambda c: nc - 1 - c) if rev else (lambda c: c)
    return [
        pl.BlockSpec((L, 2048), lambda c: (cidx(c), 0)),
        pl.BlockSpec((L, 1024), lambda c: (cidx(c), 2)),
        pl.BlockSpec((L, 1024), lambda c: (cidx(c), 3)),
        pl.BlockSpec((L, 1024), lambda c: (cidx(c), 4)),
        pl.BlockSpec((L, 128), lambda c: (cidx(c), 40)),
        _full((8, CONV_C)), _full((1, CONV_C)), _full((1, 128)), _full((1, 128)), _full((1, 128)),
        _full((1, SSM_W)),
    ]


def ssd_fwd(proj, conv_w, conv_b, dt_bias, a_log, d_skip, norm_g):
    S = proj.shape[0]
    nc = S // L

    def body(z_ref, xa_ref, xb_ref, xc_ref, dt_ref, cw_ref, cb_ref, dtb_ref, alog_ref, dsk_ref, ng_ref,
             y_ref, hs_ref, H, ext, ysc):
        @pl.when(pl.program_id(0) == 0)
        def _():
            H[...] = jnp.zeros_like(H)
            ext[0:8, :] = jnp.zeros((8, CONV_C), F32)

        for k, ref in enumerate((xa_ref, xb_ref, xc_ref)):
            ext[8:8 + L, k * 1024:(k + 1) * 1024] = ref[...]
        pre, sg_pre, xc, dt, a, acs, _ = _ssd_common(ext, cw_ref, cb_ref, dt_ref[...], dtb_ref[...], alog_ref[...])
        for k, ref in enumerate((xa_ref, xb_ref, xc_ref)):
            ext[0:8, k * 1024:(k + 1) * 1024] = ref[L - 8:L, :]
        xs = xc[:, 0:SSM_W]
        acs_t = acs.T
        ex = _expand_matrix()
        dt_x = _dot_sel(dt, ex, 2)
        xdt = xs * dt_x
        eacs_x = _dot_sel(jnp.exp(acs), ex, 2)
        xw = xdt * _dot_sel(jnp.exp(acs[L - 1:L, :] - acs), ex, 2)
        cd_row = jnp.exp(acs[L - 1:L, :])
        hs_ref[0] = H[...]
        tri = _tril()
        for g in range(SSM_G):
            gs = slice(g * 512, (g + 1) * 512)
            bg = xc[:, SSM_W + g * SSM_N:SSM_W + (g + 1) * SSM_N].astype(BF16)
            cg = xc[:, SSM_W + 512 + g * SSM_N:SSM_W + 512 + (g + 1) * SSM_N].astype(BF16)
            G = _dot_nt(cg, bg)
            yoff = _dot_nt(cg, H[gs, :].astype(BF16)) * eacs_x[:, gs]
            Sg = _dot_tn(xw[:, gs], bg)
            for j in range(8):
                hh = g * 8 + j
                hs = slice(hh * SSM_P, (hh + 1) * SSM_P)
                seg = acs[:, hh:hh + 1] - acs_t[hh:hh + 1, :]
                dk = jnp.where(tri, jnp.exp(jnp.minimum(seg, 0.0)), 0.0)
                yd = _dot((G * dk).astype(BF16), xdt[:, hs].astype(BF16))
                ysc[:, hs] = yd + yoff[:, j * SSM_P:(j + 1) * SSM_P]
                H[hs, :] = H[hs, :] * cd_row[:, hh:hh + 1] + Sg[j * SSM_P:(j + 1) * SSM_P, :]
        d_x = _dot_sel(jnp.broadcast_to(dsk_ref[...], (8, 128)), ex, 3)[0:1, :]
        Y = ysc[...] + d_x * xs
        z = z_ref[...]
        yz = Y * (z * _sigmoid(z))
        ng = ng_ref[...]
        for g in range(SSM_G):
            gs = slice(g * 512, (g + 1) * 512)
            t = yz[:, gs]
            rstd = lax.rsqrt(jnp.mean(t * t, axis=-1, keepdims=True) + EPS)
            y_ref[:, gs] = (t * rstd * ng[:, gs]).astype(BF16)

    return pl.pallas_call(
        body, name="ssd_fwd", grid=(nc,),
        in_specs=_ssd_in_specs(False, nc),
        out_specs=[pl.BlockSpec((L, SSM_W), lambda c: (c, 0)), pl.BlockSpec((1, SSM_W, SSM_N), lambda c: (c, 0, 0))],
        out_shape=[jax.ShapeDtypeStruct((S, SSM_W), BF16), jax.ShapeDtypeStruct((nc, SSM_W, SSM_N), F32)],
        scratch_shapes=[pltpu.VMEM((SSM_W, SSM_N), F32), pltpu.VMEM((8 + L, CONV_C), F32),
                        pltpu.VMEM((L, SSM_W), F32)],
        compiler_params=_params(("arbitrary",)),
    )(proj, proj, proj, proj, proj, conv_w, conv_b, dt_bias, a_log, d_skip, norm_g)


def ssd_bwd(dy, proj, hstates, conv_w, conv_b, dt_bias, a_log, d_skip, norm_g):
    S = proj.shape[0]
    nc = S // L
    cidx = lambda c: nc - 1 - c

    def body(dy_ref, z_ref, xa_ref, xb_ref, xc_ref, dt_ref, cw_ref, cb_ref, dtb_ref, alog_ref, dsk_ref, ng_ref,
             pa_ref, pb_ref, pc_ref, hp_ref,
             dout_ref, dcw_ref, dcb_ref, ddtb_ref, dalog_ref, ddsk_ref, dng_ref,
             dH, ext, dext, ysc, yoffsc, dxdt, dxc, tsc):
        step = pl.program_id(0)
        c = nc - 1 - step

        @pl.when(step == 0)
        def _():
            dH[...] = jnp.zeros_like(dH)
            dext[L:L + 8, :] = jnp.zeros((8, CONV_C), F32)
            for r in (dcw_ref, dcb_ref, ddtb_ref, dalog_ref, ddsk_ref, dng_ref):
                r[...] = jnp.zeros_like(r)

        for k, (ref, prev) in enumerate(((xa_ref, pa_ref), (xb_ref, pb_ref), (xc_ref, pc_ref))):
            ext[0:8, k * 1024:(k + 1) * 1024] = jnp.where(c > 0, prev[...], 0.0)
            ext[8:8 + L, k * 1024:(k + 1) * 1024] = ref[...]
        dtb = dtb_ref[...]
        dt_raw = dt_ref[...]
        pre, sg_pre, xc, dt, a, acs, taps = _ssd_common(ext, cw_ref, cb_ref, dt_raw, dtb, alog_ref[...])
        xs = xc[:, 0:SSM_W]
        acs_t = acs.T
        ex = _expand_matrix()
        dt_x = _dot_sel(dt, ex, 2)
        xdt = xs * dt_x
        eacs_x = _dot_sel(jnp.exp(acs), ex, 2)
        dte_x = _dot_sel(jnp.exp(acs[L - 1:L, :] - acs), ex, 2)
        xw = xdt * dte_x
        cd_row = jnp.exp(acs[L - 1:L, :])
        tri = _tril()

        Gs, Cs, Bs = [], [], []
        for g in range(SSM_G):
            gs = slice(g * 512, (g + 1) * 512)
            bg = xc[:, SSM_W + g * SSM_N:SSM_W + (g + 1) * SSM_N].astype(BF16)
            cg = xc[:, SSM_W + 512 + g * SSM_N:SSM_W + 512 + (g + 1) * SSM_N].astype(BF16)
            G = _dot_nt(cg, bg)
            Gs.append(G), Cs.append(cg), Bs.append(bg)
            yoffsc[:, gs] = _dot_nt(cg, hp_ref[0, gs, :].astype(BF16)) * eacs_x[:, gs]
            for j in range(8):
                hh = g * 8 + j
                hs = slice(hh * SSM_P, (hh + 1) * SSM_P)
                seg = acs[:, hh:hh + 1] - acs_t[hh:hh + 1, :]
                dk = jnp.where(tri, jnp.exp(jnp.minimum(seg, 0.0)), 0.0)
                ysc[:, hs] = _dot((G * dk).astype(BF16), xdt[:, hs].astype(BF16))
        d_x = _dot_sel(jnp.broadcast_to(dsk_ref[...], (8, 128)), ex, 3)[0:1, :]
        yoff = yoffsc[...]
        Y = ysc[...] + yoff + d_x * xs

        z = z_ref[...]
        sgz = _sigmoid(z)
        silu_z = z * sgz
        yz = Y * silu_z
        ng = ng_ref[...]
        dout = dy_ref[...]
        dyn = dout * ng
        dyz_parts, dng_parts = [], []
        for g in range(SSM_G):
            gs = slice(g * 512, (g + 1) * 512)
            t = yz[:, gs]
            rstd = lax.rsqrt(jnp.mean(t * t, axis=-1, keepdims=True) + EPS)
            dng_parts.append(jnp.sum(dout[:, gs] * t * rstd, axis=0, keepdims=True))
            dn = dyn[:, gs]
            dyz_parts.append(rstd * dn - t * (rstd * rstd * rstd) * jnp.mean(dn * t, axis=-1, keepdims=True))
        dng_ref[...] += jnp.concatenate(dng_parts, axis=1)
        dyz = jnp.concatenate(dyz_parts, axis=1)
        dY = dyz * silu_z
        dout_ref[:, 0:SSM_W] = (dyz * Y * (sgz * (1.0 + z * (1.0 - sgz)))).astype(BF16)

        ex_t = _expand_matrix_t()
        ddsk_ref[...] += _dot_sel(jnp.broadcast_to(jnp.sum(dY * xs, axis=0, keepdims=True), (8, SSM_W)), ex_t, 3)[0:1, :]

        lane = lax.broadcasted_iota(jnp.int32, (L, 128), 1)
        subl = lax.broadcasted_iota(jnp.int32, (128, L), 0)
        coll = lax.broadcasted_iota(jnp.int32, (128, L), 1)
        r_cols = jnp.zeros((L, 128), F32)
        c_rows = jnp.zeros((128, L), F32)
        for g in range(SSM_G):
            gs = slice(g * 512, (g + 1) * 512)
            G, cg, bg = Gs[g], Cs[g], Bs[g]
            hp_g = hp_ref[0, gs, :]
            dh_g = dH[gs, :]
            dY_g = dY[:, gs]
            dZ = dY_g * eacs_x[:, gs]
            dZb = dZ.astype(BF16)
            dC = _dot(dZb, hp_g.astype(BF16))
            dh_from_off = _dot_tn(dZ, cg)
            dhb = dh_g.astype(BF16)
            Q = _dot_nt(bg, dhb)
            dB = _dot(xw[:, gs].astype(BF16), dhb)
            qd = Q * dte_x[:, gs]
            dxdt[:, gs] = qd
            tsc[:, gs] = qd * xdt[:, gs]
            dG = jnp.zeros((L, L), F32)
            for j in range(8):
                hh = g * 8 + j
                hs = slice(hh * SSM_P, (hh + 1) * SSM_P)
                seg = acs[:, hh:hh + 1] - acs_t[hh:hh + 1, :]
                dk = jnp.where(tri, jnp.exp(jnp.minimum(seg, 0.0)), 0.0)
                M = G * dk
                dYh = dY[:, hs]
                dYhb = dYh.astype(BF16)
                dM = _dot_nt(dYhb, xdt[:, hs].astype(BF16))
                dxdt[:, hs] += _dot_tn(M, dYhb)
                dG = dG + dM * dk
                Wm = dM * M
                r_cols = r_cols + jnp.where(lane == hh, jnp.sum(Wm, axis=1, keepdims=True), 0.0)
                c_rows = c_rows + jnp.where(subl == hh, jnp.sum(Wm, axis=0, keepdims=True), 0.0)
                pj = slice(j * SSM_P, (j + 1) * SSM_P)
                cd_h = cd_row[:, hh:hh + 1]
                dcd = jnp.sum(dh_g[pj, :] * hp_g[pj, :]) * cd_h
                c_rows = c_rows - jnp.where((subl == hh) & (coll == L - 1), dcd, 0.0)
                dH[hs, :] = dh_g[pj, :] * cd_h + dh_from_off[pj, :]
            dGb = dG.astype(BF16)
            dC = dC + _dot(dGb, bg)
            dB = dB + _dot_tn(dG, cg)
            dxc[:, SSM_W + g * SSM_N:SSM_W + (g + 1) * SSM_N] = dB
            dxc[:, SSM_W + 512 + g * SSM_N:SSM_W + 512 + (g + 1) * SSM_N] = dC

        row = lax.broadcasted_iota(jnp.int32, (L, 128), 0)
        tv = tsc[...]
        t_last = _dot_sel(jnp.broadcast_to(jnp.sum(tv, axis=0, keepdims=True), (8, SSM_W)), ex_t, 3)[0:1, :]
        dacs = (r_cols - c_rows.T + _dot_sel(dY * yoff - tv, ex_t, 2) + jnp.where(row == L - 1, t_last, 0.0))
        triu = lax.broadcasted_iota(jnp.int32, (L, L), 0) <= lax.broadcasted_iota(jnp.int32, (L, L), 1)
        dadt = _sel_dot(triu, dacs, 3)
        dxdt_v = dxdt[...]
        ddt = _dot_sel(dxdt_v * xs, ex_t, 2) + dadt * a
        dalog_ref[...] += jnp.sum(dadt * dt * a, axis=0, keepdims=True)
        ddt_raw = jnp.where(lane < SSM_H, ddt * _sigmoid(dt_raw + dtb), 0.0)
        ddtb_ref[...] += jnp.sum(ddt_raw, axis=0, keepdims=True)
        dout_ref[:, 5120:5248] = ddt_raw.astype(BF16)
        dout_ref[:, 5248:5376] = jnp.zeros((L, 128), BF16)

        dxc[:, 0:SSM_W] = dxdt_v * dt_x + d_x * dY
        dpre = dxc[...] * (sg_pre * (1.0 + pre * (1.0 - sg_pre)))
        dcb_ref[...] += jnp.sum(dpre, axis=0, keepdims=True)
        dext[0:L, :] = dpre
        x_cur = ext[8:8 + L, :]
        dx = None
        for k in range(CONV_K):
            dsh = _rows_from(dext, 3 - k)
            term = cw_ref[k:k + 1, :] * dsh
            dx = term if dx is None else dx + term
            dcw_ref[k:k + 1, :] += jnp.sum(dsh * x_cur, axis=0, keepdims=True)
        dout_ref[:, SSM_W:SSM_W + CONV_C] = dx.astype(BF16)
        dext[L:L + 8, :] = dpre[0:8, :]

    big = lambda w: pl.BlockSpec((L, w), lambda c: (cidx(c), 0))
    return pl.pallas_call(
        body, name="ssd_bwd", grid=(nc,),
        in_specs=[big(SSM_W)] + _ssd_in_specs(True, nc) + [
            pl.BlockSpec((8, 1024), lambda c, k=k: (jnp.maximum(16 * cidx(c) - 1, 0), k)) for k in (2, 3, 4)] + [
            pl.BlockSpec((1, SSM_W, SSM_N), lambda c: (cidx(c), 0, 0))],
        out_specs=[big(5376), _full((8, CONV_C)), _full((1, CONV_C)),
                   _full((1, 128)), _full((1, 128)), _full((1, 128)), _full((1, SSM_W))],
        out_shape=[jax.ShapeDtypeStruct((S, 5376), BF16), jax.ShapeDtypeStruct((8, CONV_C), F32),
                   jax.ShapeDtypeStruct((1, CONV_C), F32), jax.ShapeDtypeStruct((1, 128), F32),
                   jax.ShapeDtypeStruct((1, 128), F32), jax.ShapeDtypeStruct((1, 128), F32),
                   jax.ShapeDtypeStruct((1, SSM_W), F32)],
        scratch_shapes=[pltpu.VMEM((SSM_W, SSM_N), F32), pltpu.VMEM((8 + L, CONV_C), F32),
                        pltpu.VMEM((L + 8, CONV_C), F32), pltpu.VMEM((L, SSM_W), F32),
                        pltpu.VMEM((L, SSM_W), F32), pltpu.VMEM((L, SSM_W), F32),
                        pltpu.VMEM((L, CONV_C), F32), pltpu.VMEM((L, SSM_W), F32)],
        compiler_params=_params(("arbitrary",)),
    )(dy, proj, proj, proj, proj, proj, conv_w, conv_b, dt_bias, a_log, d_skip, norm_g, proj, proj, proj, hstates)


def _resident(shape):
    nd = len(shape)
    return pl.BlockSpec(shape, lambda *_: (0,) * nd, pipeline_mode=pl.Buffered(1))


def merge_fwd(y_att, y_sg, y_ssm, proj, x, w_a, w_s, w_m, w_o, g_post):
    S = x.shape[0]
    tm = 256

    def body(ya_ref, ys_ref, ym_ref, gate_ref, x_ref, wa_ref, ws_ref, wm_ref, wo_ref, gp_ref,
             xn_ref, bra_ref, brs_ref, brm_ref, mg_ref, out_ref):
        bra = _dot(ya_ref[...], wa_ref[...])
        brs = _dot(ys_ref[...], ws_ref[...])
        brm = _dot(ym_ref[...], wm_ref[...])
        bra_ref[...] = bra
        brs_ref[...] = brs
        brm_ref[...] = brm
        merged = (_sigmoid(gate_ref[:, 0:1024]) * bra + _sigmoid(gate_ref[:, 1024:2048]) * brs
                  + _sigmoid(gate_ref[:, 2048:3072]) * brm)
        mb = merged.astype(BF16)
        mg_ref[...] = mb
        o = _dot(mb, wo_ref[...])
        out_ref[...] = o
        r = lax.rsqrt(jnp.mean(o * o, axis=-1, keepdims=True) + EPS)
        xn_ref[...] = x_ref[...] + o * r * gp_ref[...]

    row = lambda w: pl.BlockSpec((tm, w), lambda i: (i, 0))
    return pl.pallas_call(
        body, name="merge_fwd", grid=(S // tm,),
        in_specs=[row(1024), row(1024), row(2048), pl.BlockSpec((tm, 3072), lambda i: (i, 0)),
                  row(D), _resident((1024, D)), _resident((1024, D)), _resident((2048, D)), _resident((D, D)),
                  _full((1, D))],
        out_specs=[row(D)] * 6,
        out_shape=[jax.ShapeDtypeStruct((S, D), F32)] * 4 + [jax.ShapeDtypeStruct((S, D), BF16),
                                                             jax.ShapeDtypeStruct((S, D), F32)],
        compiler_params=_params(("arbitrary",)),
    )(y_att, y_sg, y_ssm, proj, x, w_a, w_s, w_m, w_o, g_post)


def merge_bwd(dy, out, g_post, proj, br_a, br_s, br_m, w_a, w_s, w_m, w_o):
    S = dy.shape[0]
    tm = 256

    def body(dy_ref, o_ref, gp_ref, gate_ref, bra_ref, brs_ref, brm_ref, wa_ref, ws_ref, wm_ref, wo_ref,
             dout_ref, dba_ref, dbs_ref, dbm_ref, dgate_ref, dya_ref, dys_ref, dym_ref, dgp_ref):
        @pl.when(pl.program_id(0) == 0)
        def _():
            dgp_ref[...] = jnp.zeros_like(dgp_ref)

        o = o_ref[...]
        dyv = dy_ref[...]
        r = lax.rsqrt(jnp.mean(o * o, axis=-1, keepdims=True) + EPS)
        dyg = dyv * gp_ref[...]
        do = r * dyg - o * (r * r * r) * jnp.mean(dyg * o, axis=-1, keepdims=True)
        dgp_ref[...] += jnp.sum(dyv * o * r, axis=0, keepdims=True)
        dob = do.astype(BF16)
        dout_ref[...] = dob
        dmerged = _dot_nt(dob, wo_ref[...])
        for idx, (br_ref, dbr_ref, w_ref, dyi_ref) in enumerate((
                (bra_ref, dba_ref, wa_ref, dya_ref), (brs_ref, dbs_ref, ws_ref, dys_ref),
                (brm_ref, dbm_ref, wm_ref, dym_ref))):
            s = _sigmoid(gate_ref[:, idx * 1024:(idx + 1) * 1024])
            dbr = (dmerged * s).astype(BF16)
            dbr_ref[...] = dbr
            dgate_ref[:, idx * 1024:(idx + 1) * 1024] = (dmerged * br_ref[...] * s * (1.0 - s)).astype(BF16)
            dyi_ref[...] = _dot_nt(dbr, w_ref[...])

    row = lambda w: pl.BlockSpec((tm, w), lambda i: (i, 0))
    return pl.pallas_call(
        body, name="merge_bwd", grid=(S // tm,),
        in_specs=[row(D), row(D), _full((1, D)), pl.BlockSpec((tm, 3072), lambda i: (i, 0)),
                  row(D), row(D), row(D),
                  _resident((1024, D)), _resident((1024, D)), _resident((2048, D)), _resident((D, D))],
        out_specs=[row(D), row(D), row(D), row(D), row(3072), row(1024), row(1024), row(2048), _full((1, D))],
        out_shape=[jax.ShapeDtypeStruct((S, D), BF16)] * 4 + [
            jax.ShapeDtypeStruct((S, 3072), BF16), jax.ShapeDtypeStruct((S, 1024), F32),
            jax.ShapeDtypeStruct((S, 1024), F32), jax.ShapeDtypeStruct((S, 2048), F32),
            jax.ShapeDtypeStruct((1, D), F32)],
        compiler_params=_params(("arbitrary",)),
    )(dy, out, g_post, proj, br_a, br_s, br_m, w_a, w_s, w_m, w_o)


def loss_head(y, target):
    S = y.shape[0]
    tm = 512

    def body(y_ref, t_ref, dy_ref, loss_ref):
        @pl.when(pl.program_id(0) == 0)
        def _():
            loss_ref[...] = jnp.zeros_like(loss_ref)
        e = y_ref[...] - t_ref[...]
        dy_ref[...] = e * (1.0 / D)
        loss_ref[...] += 0.5 * jnp.sum(jnp.mean(e * e, axis=-1, keepdims=True))

    row = pl.BlockSpec((tm, D), lambda i: (i, 0))
    return pl.pallas_call(
        body, name="loss_head", grid=(S // tm,),
        in_specs=[row, row], out_specs=[row, _full((1, 128))],
        out_shape=[jax.ShapeDtypeStruct((S, D), F32), jax.ShapeDtypeStruct((1, 128), F32)],
        compiler_params=_params(("arbitrary",)),
    )(y, target)


def _adam(w, g, m, v):
    mn = ADAM_B1 * m + (1.0 - ADAM_B1) * g
    vn = ADAM_B2 * v + (1.0 - ADAM_B2) * (g * g)
    m_hat = mn / (1.0 - ADAM_B1 ** ADAM_STEP)
    v_hat = vn / (1.0 - ADAM_B2 ** ADAM_STEP)
    return -ADAM_LR * (m_hat / (jnp.sqrt(v_hat) + ADAM_EPS) + ADAM_WD * w), mn, vn


def adamw_big(w, m, v, f, fb, cc, name, tr, f_row0=0):
    _, R, C = w.shape
    nper = R // tr
    foff = f_row0 // tr

    def body(c_ref, w_ref, m_ref, v_ref, f_ref, fb_ref, g_ref, d_ref, nm_ref, nv_ref):
        layer = pl.program_id(0) // nper
        g = jnp.where(c_ref[0] == layer, f_ref[...], fb_ref[...])
        g_ref[0] = g
        d_ref[0], nm_ref[0], nv_ref[0] = _adam(w_ref[0], g, m_ref[0], v_ref[0])

    wblk = pl.BlockSpec((1, tr, C), lambda i, c: (i // nper, i % nper, 0))
    fblk = pl.BlockSpec((tr, C), lambda i, c: (foff + i % nper, 0))
    grid_spec = pltpu.PrefetchScalarGridSpec(
        num_scalar_prefetch=1, grid=(2 * nper,),
        in_specs=[wblk, wblk, wblk, fblk, fblk], out_specs=[wblk] * 4)
    return pl.pallas_call(
        body, name=name, grid_spec=grid_spec,
        out_shape=[jax.ShapeDtypeStruct(w.shape, F32)] * 4,
        compiler_params=_params(("arbitrary",)),
    )(cc, w, m, v, f, fb)


def adamw_plain(w, g, m, v, name):
    def body(w_ref, g_ref, m_ref, v_ref, d_ref, nm_ref, nv_ref):
        d_ref[...], nm_ref[...], nv_ref[...] = _adam(w_ref[...], g_ref[...], m_ref[...], v_ref[...])

    return pl.pallas_call(
        body, name=name, out_shape=[jax.ShapeDtypeStruct(w.shape, F32)] * 3, compiler_params=_params(),
    )(w, g, m, v)


SMALL = {"norm_pre": ("g_pre", 8), "norm_post": ("g_post", 8), "att_sinks": ("sinks", 8), "sg_ln_g": ("ln_g", 8),
         "sg_ln_b": ("ln_b", 8), "sg_w": ("sg_w", 1024), "sg_b": ("sg_bt", 8), "ssm_conv_b": ("conv_b", 24),
         "ssm_dt_bias": ("dt_bias", 8), "ssm_a_log": ("a_log", 8), "ssm_d": ("d_skip", 8), "ssm_norm_g": ("norm_g", 16)}
SMALL_LAYER_ROWS = sum(r for _, r in SMALL.values())
REL_ROW = DEPTH * SMALL_LAYER_ROWS
LOSS_ROW = REL_ROW + 32
SMALL_ROWS = LOSS_ROW + 8


def _small_rows():
    rows, r = {}, 0
    for l in range(DEPTH):
        for name, (_, n) in SMALL.items():
            rows[(l, name)] = r
            r += n
    return rows


def adamw_small(red, rel, small):
    names = list(SMALL) + ["rel_bias"]
    params = dict(small, rel_bias=rel)
    rows = _small_rows()

    def grad_of(red_ref, l, name, n):
        r0 = rows[(l, name)]
        if name == "sg_b":
            return red_ref[r0:r0 + 8, :]
        if n < 128:
            return red_ref[r0:r0 + 1, 0:n]
        return jnp.concatenate([red_ref[r0 + j:r0 + j + 1, :] for j in range(n // 128)], axis=1)

    def body(red_ref, *refs):
        ins, outs = refs[:3 * len(names)], refs[3 * len(names):]
        for i, name in enumerate(names):
            w_ref, m_ref, v_ref = ins[3 * i:3 * i + 3]
            o = outs[4 * i:4 * i + 4]
            if name == "rel_bias":
                g = red_ref[REL_ROW:REL_ROW + 32, 0:16]
                o[0][...] = g
                o[1][...], o[2][...], o[3][...] = _adam(w_ref[...], g, m_ref[...], v_ref[...])
                continue
            for l in range(DEPTH):
                if name == "sg_w":
                    for grp in range(8):
                        r0 = rows[(l, name)] + grp * 128
                        g = red_ref[r0:r0 + 128, :]
                        o[0][l, grp] = g
                        o[1][l, grp], o[2][l, grp], o[3][l, grp] = _adam(w_ref[l, grp], g, m_ref[l, grp], v_ref[l, grp])
                elif name == "sg_b":
                    g = grad_of(red_ref, l, name, 128)
                    o[0][l] = g
                    o[1][l], o[2][l], o[3][l] = _adam(w_ref[l], g, m_ref[l], v_ref[l])
                else:
                    sl = slice(l, l + 1)
                    g = grad_of(red_ref, l, name, w_ref.shape[-1])
                    o[0][sl, :] = g
                    o[1][sl, :], o[2][sl, :], o[3][sl, :] = _adam(w_ref[sl, :], g, m_ref[sl, :], v_ref[sl, :])

    flat_in = [a for name in names for a in params[name]]
    out_shape = [jax.ShapeDtypeStruct(params[name][0].shape, F32) for name in names for _ in range(4)]
    res = pl.pallas_call(body, name="adamw_small", out_shape=out_shape, compiler_params=_params())(red, *flat_in)
    return {name: tuple(res[4 * i:4 * i + 4]) for i, name in enumerate(names)}


ANY = pl.BlockSpec(memory_space=pl.ANY)


def _place():
    x, y, c = lax.axis_index("x"), lax.axis_index("y"), lax.axis_index("c")
    others = [(1 - x, y), (x, 1 - y), (1 - x, 1 - y)]
    return x, y, c, others


def _rcopy(src, dst, ssem, rsem, to):
    return pltpu.make_async_remote_copy(src_ref=src, dst_ref=dst, send_sem=ssem, recv_sem=rsem,
                                        device_id=to, device_id_type=MESH)


def gather_weights(arrs):
    n = len(arrs)

    def body(*refs):
        srcs, outs, ssem, rsem = refs[:n], refs[n:2 * n], refs[2 * n], refs[2 * n + 1]
        x, y, c, others = _place()
        me = 2 * x + y
        sib = (x, y, 1 - c)
        first = [_rcopy(srcs[i].at[c], outs[i].at[c, me], ssem.at[6 * i + k], rsem.at[6 * i + k], (ox, oy, c))
                 for i in range(n) for k, (ox, oy) in enumerate(others)]
        for cp in first:
            cp.start()
        passed = []
        for k, (ox, oy) in enumerate(others):
            for i in range(n):
                slot = outs[i].at[c, 2 * ox + oy]
                _rcopy(slot, slot, ssem.at[6 * i + k], rsem.at[6 * i + k], sib).wait_recv()
                fw = _rcopy(slot, slot, ssem.at[6 * i + 3 + k], rsem.at[6 * i + 3 + k], sib)
                fw.start()
                passed.append(fw)
        for k, (ox, oy) in enumerate(others):
            for i in range(n):
                slot = outs[i].at[1 - c, 2 * ox + oy]
                _rcopy(slot, slot, ssem.at[6 * i + 3 + k], rsem.at[6 * i + 3 + k], sib).wait_recv()
        for cp in first + passed:
            cp.wait_send()

    return pl.pallas_call(
        body, name="gather_weights",
        in_specs=[ANY] * n, out_specs=[ANY] * n,
        out_shape=[jax.ShapeDtypeStruct((2, SHARDS) + a.shape[1:], a.dtype) for a in arrs],
        scratch_shapes=[pltpu.SemaphoreType.DMA((6 * n,)), pltpu.SemaphoreType.DMA((6 * n,))],
    )(*arrs)


def grad_sibling_exchange(arrs):
    n = len(arrs)

    def body(*refs):
        srcs, outs, ssem, rsem = refs[:n], refs[n:2 * n], refs[2 * n], refs[2 * n + 1]
        x, y, c, _ = _place()
        cps = [_rcopy(srcs[i].at[1 - c], outs[i], ssem.at[i], rsem.at[i], (x, y, 1 - c)) for i in range(n)]
        for cp in cps:
            cp.start()
        for cp in cps:
            cp.wait()

    return pl.pallas_call(
        body, name="grad_sibling_exchange",
        in_specs=[ANY] * n, out_specs=[ANY] * n,
        out_shape=[jax.ShapeDtypeStruct(a.shape[1:], F32) for a in arrs],
        scratch_shapes=[pltpu.SemaphoreType.DMA((n,)), pltpu.SemaphoreType.DMA((n,))],
    )(*arrs)


def grad_chip_sum(g, sb, cc, tr, name):
    _, _, R, C = g.shape
    blk = pl.BlockSpec((1, tr, C), lambda s, r, c: (s, r, 0))
    grid_spec = pltpu.PrefetchScalarGridSpec(
        num_scalar_prefetch=1, grid=(SHARDS, R // tr),
        in_specs=[pl.BlockSpec((1, 1, tr, C), lambda s, r, c: (c[0], s, r, 0)), blk],
        out_specs=[blk, blk])

    def body(c_ref, a_ref, b_ref, o_ref, ob_ref):
        t = a_ref[0] + b_ref[...]
        o_ref[...] = t
        ob_ref[...] = t.astype(BF16)

    return pl.pallas_call(
        body, name=name, grid_spec=grid_spec,
        out_shape=[jax.ShapeDtypeStruct((SHARDS, R, C), F32), jax.ShapeDtypeStruct((SHARDS, R, C), BF16)],
        compiler_params=_params(("arbitrary", "arbitrary")),
    )(cc, g, sb)


def grad_chip_exchange(arrs):
    n = len(arrs)

    def body(*refs):
        srcs, outs, ssem, rsem = refs[:n], refs[n:2 * n], refs[2 * n], refs[2 * n + 1]
        x, y, c, others = _place()
        me = 2 * x + y
        sends = [_rcopy(srcs[i].at[2 * ox + oy], outs[i].at[me], ssem.at[3 * i + k], rsem.at[3 * i + k], (ox, oy, c))
                 for i in range(n) for k, (ox, oy) in enumerate(others)]
        for cp in sends:
            cp.start()
        for i in range(n):
            for k, (ox, oy) in enumerate(others):
                slot = outs[i].at[2 * ox + oy]
                _rcopy(slot, slot, ssem.at[3 * i + k], rsem.at[3 * i + k], (ox, oy, c)).wait_recv()
        for cp in sends:
            cp.wait_send()

    return pl.pallas_call(
        body, name="grad_chip_exchange",
        in_specs=[ANY] * n, out_specs=[ANY] * n,
        out_shape=[jax.ShapeDtypeStruct(a.shape, a.dtype) for a in arrs],
        scratch_shapes=[pltpu.SemaphoreType.DMA((3 * n,)), pltpu.SemaphoreType.DMA((3 * n,))],
    )(*arrs)


def grad_shard_sum(t, rb, me, tr, name):
    _, R, C = t.shape
    grid_spec = pltpu.PrefetchScalarGridSpec(
        num_scalar_prefetch=1, grid=(R // tr,),
        in_specs=[pl.BlockSpec((1, tr, C), lambda r, m: (m[0], r, 0)),
                  pl.BlockSpec((SHARDS, tr, C), lambda r, m: (0, r, 0))],
        out_specs=pl.BlockSpec((tr, C), lambda r, m: (r, 0)))

    def body(m_ref, t_ref, r_ref, o_ref):
        part = [jnp.where(m_ref[0] == s, t_ref[0], r_ref[s].astype(F32)) for s in range(SHARDS)]
        o_ref[...] = ((part[0] + part[1]) + part[2]) + part[3]

    return pl.pallas_call(
        body, name=name, grid_spec=grid_spec,
        out_shape=jax.ShapeDtypeStruct((R, C), F32),
        compiler_params=_params(("arbitrary",)),
    )(me, t, rb)


def grad_sibling_share(arrs):
    n = len(arrs)

    def body(*refs):
        srcs, outs, ssem, rsem = refs[:n], refs[n:2 * n], refs[2 * n], refs[2 * n + 1]
        x, y, c, _ = _place()
        cps = [_rcopy(srcs[i], outs[i], ssem.at[i], rsem.at[i], (x, y, 1 - c)) for i in range(n)]
        for cp in cps:
            cp.start()
        for cp in cps:
            cp.wait()

    return pl.pallas_call(
        body, name="grad_sibling_share",
        in_specs=[ANY] * n, out_specs=[ANY] * n,
        out_shape=[jax.ShapeDtypeStruct(a.shape, F32) for a in arrs],
        scratch_shapes=[pltpu.SemaphoreType.DMA((n,)), pltpu.SemaphoreType.DMA((n,))],
    )(*arrs)


def _allreduce_rows(src, sib_buf, chips, out_ref, ssem, rsem):
    x, y, c, others = _place()
    me = 2 * x + y
    cp = _rcopy(src, sib_buf, ssem.at[0], rsem.at[0], (x, y, 1 - c))
    cp.start()
    cp.wait()
    chips[me] = src[...] + sib_buf[...]
    sends = [_rcopy(chips.at[me], chips.at[me], ssem.at[1 + k], rsem.at[1 + k], (ox, oy, c))
             for k, (ox, oy) in enumerate(others)]
    for s in sends:
        s.start()
    for k, (ox, oy) in enumerate(others):
        slot = chips.at[2 * ox + oy]
        _rcopy(slot, slot, ssem.at[1 + k], rsem.at[1 + k], (ox, oy, c)).wait_recv()
    for s in sends:
        s.wait_send()
    out_ref[...] = ((chips[0] + chips[1]) + chips[2]) + chips[3]


def _allreduce_scratch(rows):
    return [pltpu.VMEM((rows, 128), F32), pltpu.VMEM((SHARDS, rows, 128), F32),
            pltpu.SemaphoreType.DMA((4,)), pltpu.SemaphoreType.DMA((4,))]


def allreduce_rows(buf, name):
    rows = buf.shape[0]
    VM = pl.BlockSpec(memory_space=pltpu.VMEM)

    def body(src_ref, out_ref, sib_buf, chips, ssem, rsem):
        _allreduce_rows(src_ref, sib_buf, chips, out_ref, ssem, rsem)

    return pl.pallas_call(
        body, name=name, in_specs=[VM], out_specs=VM,
        out_shape=jax.ShapeDtypeStruct((rows, 128), F32),
        scratch_shapes=_allreduce_scratch(rows), compiler_params=_params(),
    )(buf)


def small_allreduce(grads, rel, loss_part):
    rows = _small_rows()
    keys = [(l, name) for l in range(DEPTH) for name in SMALL]
    flat = [grads[l][SMALL[name][0]] for l, name in keys] + [rel, loss_part]

    def body(*refs):
        ins = refs[:len(flat)]
        out_ref, src, sib_buf, chips, ssem, rsem = refs[len(flat):]
        src[...] = jnp.zeros_like(src)
        for (l, name), ref in zip(keys, ins):
            r0 = rows[(l, name)]
            if name == "sg_w":
                for grp in range(8):
                    src[r0 + grp * 128:r0 + (grp + 1) * 128, :] = ref[grp]
            elif name == "sg_b":
                src[r0:r0 + 8, :] = ref[...].T[0:8, :]
            else:
                for j in range(ref.shape[1] // 128):
                    src[r0 + j:r0 + j + 1, :] = ref[:, j * 128:(j + 1) * 128]
        src[REL_ROW:REL_ROW + 32, 0:16] = ins[-2][...]
        src[LOSS_ROW:LOSS_ROW + 1, :] = ins[-1][...]
        _allreduce_rows(src, sib_buf, chips, out_ref, ssem, rsem)

    return pl.pallas_call(
        body, name="small_allreduce",
        out_shape=jax.ShapeDtypeStruct((SMALL_ROWS, 128), F32),
        scratch_shapes=[pltpu.VMEM((SMALL_ROWS, 128), F32)] + _allreduce_scratch(SMALL_ROWS),
        compiler_params=_params(),
    )(*flat)


def _pad_lanes(v):
    return jnp.zeros((1, 128), F32).at[0, :v.shape[0]].set(v)


def layer_fwd(x, wts, bias):
    p_gate, p_sgu, p_att, p_ssd, h = inproj_fwd(x, wts["g_pre"], wts["wp"])
    y_att = att_fwd(p_att, bias, wts["sinks"])
    y_sg = sgu_fwd(p_sgu, wts["ln_g"], wts["ln_b"], wts["sg_w"], wts["sg_bt"])
    y_ssm, hst = ssd_fwd(p_ssd, wts["conv_w"], wts["conv_b"], wts["dt_bias"], wts["a_log"], wts["d_skip"],
                         wts["norm_g"])
    x_new, br_a, br_s, br_m, merged, out = merge_fwd(
        y_att, y_sg, y_ssm, p_gate, x, wts["w_a"], wts["w_s"], wts["w_m"], wts["w_o"], wts["g_post"])
    saved = dict(x=x, p_gate=p_gate, p_sgu=p_sgu, p_att=p_att, p_ssd=p_ssd, h=h,
                 y_att=y_att, y_sg=y_sg, y_ssm=y_ssm, hst=hst,
                 br_a=br_a, br_s=br_s, br_m=br_m, merged=merged, out=out)
    return x_new, saved


def layer_bwd(dy, wts, bias, sv):
    dout, dba, dbs, dbm, d_gate, dya, dys, dym, dg_post = merge_bwd(
        dy, sv["out"], wts["g_post"], sv["p_gate"], sv["br_a"], sv["br_s"], sv["br_m"],
        wts["w_a"], wts["w_s"], wts["w_m"], wts["w_o"])
    d_att, dbias, dsinks = att_bwd(dya, sv["p_att"], bias, wts["sinks"])
    d_sgu, dsg_w, dsg_bt, dln_g, dln_b = sgu_bwd(dys, sv["p_sgu"], wts["ln_g"], wts["ln_b"], wts["sg_w"],
                                                 wts["sg_bt"])
    d_ssd, dcw, dcb, ddtb, dalog, ddsk, dng = ssd_bwd(
        dym, sv["p_ssd"], sv["hst"], wts["conv_w"], wts["conv_b"], wts["dt_bias"], wts["a_log"], wts["d_skip"],
        wts["norm_g"])
    dps = (d_gate, d_sgu, d_att, d_ssd)
    dx, dg_pre = inproj_bwd(dps, wts["wp"], sv["x"], wts["g_pre"], dy)
    grads = dict(
        w_in=matmul_tn_groups(sv["h"], dps, "dw_in", ts=1024 if dy.shape[0] % 1024 == 0 else 512),
        w_a=matmul_tn(sv["y_att"], dba, "dw_att"),
        w_s=matmul_tn(sv["y_sg"], dbs, "dw_sg"),
        w_m=matmul_tn(sv["y_ssm"], dbm, "dw_ssm"),
        w_o=matmul_tn(sv["merged"], dout, "dw_out"),
        g_pre=dg_pre, g_post=dg_post, sinks=dsinks, ln_g=dln_g, ln_b=dln_b, sg_w=dsg_w, sg_bt=dsg_bt,
        conv_w=dcw, conv_b=dcb, dt_bias=ddtb, a_log=dalog, d_skip=ddsk, norm_g=dng, bias=dbias)
    return dx, grads


REST_OFF = (0, 256, 512, 1024, 1280)
REST_ROWS = 1296


def kernel(x, w_in, norm_pre, norm_post, rel_bias, att_sinks, sg_ln_g, sg_ln_b, sg_w, sg_b, ssm_conv_w, ssm_conv_b, ssm_dt_bias, ssm_a_log, ssm_d, ssm_norm_g, w_br_att, w_br_sg, w_br_ssm, w_out, loss_target, m_w_in, m_norm_pre, m_norm_post, m_rel_bias, m_att_sinks, m_sg_ln_g, m_sg_ln_b, m_sg_w, m_sg_b, m_ssm_conv_w, m_ssm_conv_b, m_ssm_dt_bias, m_ssm_a_log, m_ssm_d, m_ssm_norm_g, m_w_br_att, m_w_br_sg, m_w_br_ssm, m_w_out, v_w_in, v_norm_pre, v_norm_post, v_rel_bias, v_att_sinks, v_sg_ln_g, v_sg_ln_b, v_sg_w, v_sg_b, v_ssm_conv_w, v_ssm_conv_b, v_ssm_dt_bias, v_ssm_a_log, v_ssm_d, v_ssm_norm_g, v_w_br_att, v_w_br_sg, v_w_br_ssm, v_w_out):
    cx, cy, cc = lax.axis_index("x"), lax.axis_index("y"), lax.axis_index("c")
    me = 2 * cx + cy
    xs = x[0]
    S = xs.shape[0]

    w_in_b = w_in.astype(BF16)
    w_rest_b = jnp.concatenate([w_br_att, w_br_sg, w_br_ssm, w_out], axis=1).astype(BF16)
    all_in, all_rest = gather_weights([w_in_b, w_rest_b])
    convw_slot = jnp.zeros((SHARDS, DEPTH * CONV_K * 768 // 128, 128), F32)
    convw_slot = lax.dynamic_update_index_in_dim(
        convw_slot, jnp.where(cc == 0, 1.0, 0.0) * ssm_conv_w.reshape(-1, 128), me, 0)
    convw_all = allreduce_rows(convw_slot.reshape(-1, 128), "gather_conv_w")
    convw_all = convw_all.reshape(SHARDS, DEPTH, CONV_K, 768).transpose(1, 2, 0, 3).reshape(DEPTH, CONV_K, CONV_C)

    def shards_of(gathered, mine, l, lo, hi):
        return [jnp.where(me == s, mine[l, lo:hi], gathered[l, s, lo:hi]) for s in range(SHARDS)]

    o = REST_OFF
    layers = []
    for l in range(DEPTH):
        w_in_full = jnp.concatenate(shards_of(all_in, w_in_b, l, 0, 1024), axis=1)
        rest = lambda k: jnp.concatenate(shards_of(all_rest, w_rest_b, l, o[k], o[k + 1]), axis=0)
        layers.append(dict(
            wp=to_padded_cols(w_in_full),
            w_a=rest(0), w_s=rest(1), w_m=rest(2), w_o=rest(3),
            g_pre=norm_pre[l][None], g_post=norm_post[l][None], sinks=att_sinks[l],
            ln_g=sg_ln_g[l][None], ln_b=sg_ln_b[l][None], sg_w=sg_w[l],
            sg_bt=sg_b[l].T,
            conv_w=jnp.concatenate([convw_all[l], jnp.zeros((4, CONV_C), F32)], axis=0),
            conv_b=ssm_conv_b[l][None], dt_bias=_pad_lanes(ssm_dt_bias[l]), a_log=_pad_lanes(ssm_a_log[l]),
            d_skip=_pad_lanes(ssm_d[l]), norm_g=ssm_norm_g[l][None]))

    bias = bias_table(rel_bias)
    saved = []
    act = xs
    for l in range(DEPTH):
        act, sv = layer_fwd(act, layers[l], bias)
        saved.append(sv)
    dy, loss_part = loss_head(act, loss_target[0])
    grads = [None] * DEPTH
    for l in reversed(range(DEPTH)):
        dy, grads[l] = layer_bwd(dy, layers[l], bias, saved[l])
    grad_x = dy[None]
    grad_rel_local = bias_grad(grads[0]["bias"] + grads[1]["bias"])

    cvec = jnp.reshape(cc, (1,)).astype(jnp.int32)
    mvec = jnp.reshape(me, (1,)).astype(jnp.int32)
    g_in, g_rest = [], []
    for l in range(DEPTH):
        g = grads[l]
        g_in.append(from_padded_cols(g["w_in"]).reshape(1024, SHARDS, 3400).transpose(1, 0, 2))
        gcw = g["conv_w"][0:CONV_K].reshape(CONV_K, SHARDS, 768).transpose(1, 0, 2).reshape(SHARDS, 3, 1024)
        g_rest.append(jnp.concatenate([
            g["w_a"].reshape(SHARDS, 256, D), g["w_s"].reshape(SHARDS, 256, D), g["w_m"].reshape(SHARDS, 512, D),
            g["w_o"].reshape(SHARDS, 256, D), jnp.pad(gcw, ((0, 0), (0, REST_ROWS - REST_OFF[4] - 3), (0, 0)))],
            axis=1))
    g_in, g_rest = jnp.stack(g_in), jnp.stack(g_rest)
    sb_in, sb_rest = grad_sibling_exchange([g_in, g_rest])
    t_in, t_in_b = grad_chip_sum(g_in, sb_in, cvec, 128, "chip_sum_w_in")
    t_rest, t_rest_b = grad_chip_sum(g_rest, sb_rest, cvec, 432, "chip_sum_rest")
    rb_in, rb_rest = grad_chip_exchange([t_in_b, t_rest_b])
    f_in = grad_shard_sum(t_in, rb_in, mvec, 128, "shard_sum_w_in")
    f_rest = grad_shard_sum(t_rest, rb_rest, mvec, 432, "shard_sum_rest")
    fb_in, fb_rest = grad_sibling_share([f_in, f_rest])

    red = small_allreduce(grads, grad_rel_local, loss_part)
    loss = red[LOSS_ROW, 0]

    res = adamw_small(red, (rel_bias, m_rel_bias, v_rel_bias), dict(
        norm_pre=(norm_pre, m_norm_pre, v_norm_pre), norm_post=(norm_post, m_norm_post, v_norm_post),
        att_sinks=(att_sinks, m_att_sinks, v_att_sinks), sg_ln_g=(sg_ln_g, m_sg_ln_g, v_sg_ln_g),
        sg_ln_b=(sg_ln_b, m_sg_ln_b, v_sg_ln_b), sg_w=(sg_w, m_sg_w, v_sg_w), sg_b=(sg_b, m_sg_b, v_sg_b),
        ssm_conv_b=(ssm_conv_b, m_ssm_conv_b, v_ssm_conv_b), ssm_dt_bias=(ssm_dt_bias, m_ssm_dt_bias, v_ssm_dt_bias),
        ssm_a_log=(ssm_a_log, m_ssm_a_log, v_ssm_a_log), ssm_d=(ssm_d, m_ssm_d, v_ssm_d),
        ssm_norm_g=(ssm_norm_g, m_ssm_norm_g, v_ssm_norm_g)))
    res["w_in"] = adamw_big(w_in, m_w_in, v_w_in, f_in, fb_in, cvec, "adamw_w_in", 128)
    res["w_br_att"] = adamw_big(w_br_att, m_w_br_att, v_w_br_att, f_rest, fb_rest, cvec, "adamw_w_br_att", 256, o[0])
    res["w_br_sg"] = adamw_big(w_br_sg, m_w_br_sg, v_w_br_sg, f_rest, fb_rest, cvec, "adamw_w_br_sg", 256, o[1])
    res["w_br_ssm"] = adamw_big(w_br_ssm, m_w_br_ssm, v_w_br_ssm, f_rest, fb_rest, cvec, "adamw_w_br_ssm", 512, o[2])
    res["w_out"] = adamw_big(w_out, m_w_out, v_w_out, f_rest, fb_rest, cvec, "adamw_w_out", 256, o[3])
    cw_mine = f_rest[o[4]:o[4] + 3].reshape(CONV_K, 768)
    cw_sib = fb_rest[o[4]:o[4] + 3].reshape(CONV_K, 768)
    g_conv_w = jnp.stack([jnp.where(cc == l, cw_mine, cw_sib) for l in range(DEPTH)])
    res["ssm_conv_w"] = (g_conv_w,) + tuple(adamw_plain(ssm_conv_w, g_conv_w, m_ssm_conv_w, v_ssm_conv_w, "adamw_conv_w"))

    order = ["w_in", "norm_pre", "norm_post", "rel_bias", "att_sinks", "sg_ln_g", "sg_ln_b", "sg_w", "sg_b",
             "ssm_conv_w", "ssm_conv_b", "ssm_dt_bias", "ssm_a_log", "ssm_d", "ssm_norm_g",
             "w_br_att", "w_br_sg", "w_br_ssm", "w_out"]
    return (loss, grad_x, *[res[n][0] for n in order], *[res[n][1] for n in order],
            *[res[n][2] for n in order], *[res[n][3] for n in order])
```

```python
import functools
import math

import numpy as np
import jax
import jax.numpy as jnp
from jax import lax
from jax.experimental import pallas as pl
from jax.experimental.pallas import tpu as pltpu

F32 = jnp.float32
BF16 = jnp.bfloat16
MESH = pl.DeviceIdType.MESH

D = 1024
DEPTH = 2
EPS = 1e-6
L = 128
HEADS = 16
KV = 2
DH = 64
SSM_W = 2048
SSM_H = 32
SSM_P = 64
SSM_G = 4
SSM_N = 128
CONV_K = 4
CONV_C = 3072
NEG = -1e30
IN_COLS = 13600

GROUPS = (("gate", 3072, 1536), ("sgu", 3072, 1536), ("att", 2304, 2304), ("ssd", 5376, 1792))
W_IN_ROWS = 3456

ADAM_LR = 0.001
ADAM_B1 = 0.9
ADAM_B2 = 0.999
ADAM_EPS = 1e-08
ADAM_WD = 0.01
ADAM_STEP = 10

VMEM_LIMIT = 56 * 1024 * 1024

PACK_ROWS = 4704
PACK_TILE = 224
SHARDS = 4


def _dot(a, b):
    return jnp.dot(a, b, preferred_element_type=F32)


def _dot_nt(a, b):
    return lax.dot_general(a, b, (((1,), (1,)), ((), ())), preferred_element_type=F32)


def _dot_tn(a_f32, b):
    return jnp.dot(a_f32.T.astype(BF16), b, preferred_element_type=F32)


def _dot_hi(a, b):
    return jnp.dot(a, b, preferred_element_type=F32, precision=lax.Precision.HIGHEST)


def _pieces(x, n):
    out = []
    for _ in range(n - 1):
        p = x.astype(BF16)
        out.append(p)
        x = x - p.astype(F32)
    out.append(x.astype(BF16))
    return out


def _dot_sel(a, sel, n):
    sel = sel.astype(BF16)
    acc = None
    for p in _pieces(a, n):
        t = _dot(p, sel)
        acc = t if acc is None else acc + t
    return acc


def _sel_dot(sel, b, n):
    sel = sel.astype(BF16)
    acc = None
    for p in _pieces(b, n):
        t = _dot(sel, p)
        acc = t if acc is None else acc + t
    return acc


def _sigmoid(x):
    return 1.0 / (1.0 + jnp.exp(-x))


def _softplus(x):
    return jnp.maximum(x, 0.0) + jnp.log(1.0 + jnp.exp(-jnp.abs(x)))


def _params(sem=None, vmem=VMEM_LIMIT):
    kw = dict(vmem_limit_bytes=vmem)
    if sem is not None:
        kw["dimension_semantics"] = sem
    return pltpu.CompilerParams(**kw)


def _full(shape):
    nd = len(shape)
    return pl.BlockSpec(shape, lambda *_: (0,) * nd)


def group_weights(wt):
    return dict(
        gate=wt[10528:13600],
        sgu=wt[2304:5376],
        att=jnp.concatenate([wt[0:1024], wt[1280:2304], wt[1024:1280]], axis=0),
        ssd=jnp.concatenate([wt[5376:10496], wt[10496:10528], jnp.zeros((224, D), wt.dtype)], axis=0))


def ungroup_grads(g):
    a, s = g["att"], g["ssd"]
    return jnp.concatenate([a[0:1024], a[2048:2304], a[1024:2048], g["sgu"], s[0:5152], g["gate"]], axis=0)


def _bucket_table():
    qi = np.arange(L)[:, None]
    kj = np.arange(2 * L)[None, :]
    dist = np.maximum(qi + L - kj, 0)
    dist_f = np.maximum(dist, 1).astype(np.float32)
    large = 16 + (np.log(dist_f / np.float32(16)) / np.float32(math.log(128 / 16)) * np.float32(16)).astype(np.int32)
    large = np.minimum(large, 31)
    return np.where(dist < 16, dist, large).astype(np.int32)


def bias_table(rel_bias):
    buckets = jnp.asarray(_bucket_table().reshape(1, L * 2 * L))

    def body(rb_ref, bk_ref, out_ref):
        onehot = (lax.broadcasted_iota(jnp.int32, (32, L * 2 * L), 0) == bk_ref[...]).astype(F32)
        out_ref[...] = lax.dot_general(rb_ref[...], onehot, (((0,), (0,)), ((), ())),
                                       preferred_element_type=F32, precision=lax.Precision.HIGHEST)

    out = pl.pallas_call(
        body, name="bias_table",
        out_shape=jax.ShapeDtypeStruct((HEADS, L * 2 * L), F32),
        compiler_params=_params(),
    )(rel_bias, buckets)
    return out.reshape(HEADS, L, 2 * L)


def bias_grad(dbias):
    buckets = jnp.asarray(_bucket_table().reshape(1, L * 2 * L))

    def body(db_ref, bk_ref, out_ref):
        onehot = (lax.broadcasted_iota(jnp.int32, (32, L * 2 * L), 0) == bk_ref[...]).astype(F32)
        out_ref[...] = lax.dot_general(onehot, db_ref[...], (((1,), (1,)), ((), ())),
                                       preferred_element_type=F32, precision=lax.Precision.HIGHEST)

    return pl.pallas_call(
        body, name="bias_grad",
        out_shape=jax.ShapeDtypeStruct((32, HEADS), F32),
        compiler_params=_params(),
    )(dbias.reshape(HEADS, L * 2 * L), buckets)


def _row_tile(S):
    return 1024 if S % 1024 == 0 else 512


def inproj_first(x, g_pre, wt, tn, name):
    S, W = x.shape[0], wt.shape[0]
    tm = _row_tile(S)

    def body(x_ref, g_ref, w_ref, o_ref, h_ref):
        @pl.when(pl.program_id(1) == 0)
        def _():
            xv = x_ref[...]
            r = lax.rsqrt(jnp.mean(xv * xv, axis=-1, keepdims=True) + EPS)
            h_ref[...] = (xv * r * g_ref[...]).astype(BF16)
        o_ref[...] = _dot_nt(h_ref[...], w_ref[...])

    return pl.pallas_call(
        body, name=name, grid=(S // tm, W // tn),
        in_specs=[pl.BlockSpec((tm, D), lambda i, j: (i, 0)), _full((1, D)),
                  pl.BlockSpec((tn, D), lambda i, j: (j, 0))],
        out_specs=[pl.BlockSpec((tm, tn), lambda i, j: (i, j)), pl.BlockSpec((tm, D), lambda i, j: (i, 0))],
        out_shape=[jax.ShapeDtypeStruct((S, W), F32), jax.ShapeDtypeStruct((S, D), BF16)],
        compiler_params=_params(("arbitrary", "arbitrary")),
    )(x, g_pre, wt)


def inproj_group(h, wt, tn, name):
    S, W = h.shape[0], wt.shape[0]
    tm = _row_tile(S)

    def body(h_ref, w_ref, o_ref):
        o_ref[...] = _dot_nt(h_ref[...], w_ref[...])

    return pl.pallas_call(
        body, name=name, grid=(S // tm, W // tn),
        in_specs=[pl.BlockSpec((tm, D), lambda i, j: (i, 0)), pl.BlockSpec((tn, D), lambda i, j: (j, 0))],
        out_specs=pl.BlockSpec((tm, tn), lambda i, j: (i, j)),
        out_shape=jax.ShapeDtypeStruct((S, W), F32),
        compiler_params=_params(("arbitrary", "arbitrary")),
    )(h, wt)


def dh_group(dp, wt, acc, tk, name):
    S, W = dp.shape
    tm = _row_tile(S)

    def body(*refs):
        dp_ref, w_ref, o_ref = refs[0], refs[1], refs[-1]
        first = pl.program_id(1) == 0
        if acc is None:
            @pl.when(first)
            def _():
                o_ref[...] = jnp.zeros_like(o_ref)
        else:
            @pl.when(first)
            def _():
                o_ref[...] = refs[2][...]
        o_ref[...] += _dot(dp_ref[...], w_ref[...])

    row = pl.BlockSpec((tm, D), lambda i, k: (i, 0))
    return pl.pallas_call(
        body, name=name, grid=(S // tm, W // tk),
        in_specs=[pl.BlockSpec((tm, tk), lambda i, k: (i, k)), pl.BlockSpec((tk, D), lambda i, k: (k, 0))]
        + ([] if acc is None else [row]),
        out_specs=row, out_shape=jax.ShapeDtypeStruct((S, D), F32),
        input_output_aliases={} if acc is None else {2: 0},
        compiler_params=_params(("arbitrary", "arbitrary")),
    )(*((dp, wt) if acc is None else (dp, wt, acc)))


def dh_last(dp, wt, acc_in, x, g_pre, dy, tk, name):
    S, W = dp.shape
    tm = 512
    nk = W // tk

    def body(dp_ref, w_ref, a_ref, x_ref, g_ref, dy_ref, dx_ref, dg_ref, acc):
        i, k = pl.program_id(0), pl.program_id(1)

        @pl.when(k == 0)
        def _():
            acc[...] = a_ref[...]

        acc[...] += _dot(dp_ref[...], w_ref[...])

        @pl.when((k == nk - 1) & (i == 0))
        def _():
            dg_ref[...] = jnp.zeros_like(dg_ref)

        @pl.when(k == nk - 1)
        def _():
            xv = x_ref[...]
            dh = acc[...]
            g = g_ref[...]
            r = lax.rsqrt(jnp.mean(xv * xv, axis=-1, keepdims=True) + EPS)
            dhg = dh * g
            dx_ref[...] = dy_ref[...] + r * dhg - xv * (r * r * r) * jnp.mean(dhg * xv, axis=-1, keepdims=True)
            dg_ref[...] += jnp.sum(dh * xv * r, axis=0, keepdims=True)

    row = pl.BlockSpec((tm, D), lambda i, k: (i, 0))
    return pl.pallas_call(
        body, name=name, grid=(S // tm, nk),
        in_specs=[pl.BlockSpec((tm, tk), lambda i, k: (i, k)), pl.BlockSpec((tk, D), lambda i, k: (k, 0)),
                  row, row, _full((1, D)), row],
        out_specs=[row, _full((1, D))],
        out_shape=[jax.ShapeDtypeStruct((S, D), F32), jax.ShapeDtypeStruct((1, D), F32)],
        scratch_shapes=[pltpu.VMEM((tm, D), F32)],
        compiler_params=_params(("arbitrary", "arbitrary")),
    )(dp, wt, acc_in, x, g_pre, dy)


def dw_group(dp, h, tn, name, ts=512):
    S, W = dp.shape

    def body(dp_ref, h_ref, o_ref):
        @pl.when(pl.program_id(1) == 0)
        def _():
            o_ref[...] = jnp.zeros_like(o_ref)
        o_ref[...] += _dot_tn(dp_ref[...].astype(F32), h_ref[...])

    return pl.pallas_call(
        body, name=name, grid=(W // tn, S // ts),
        in_specs=[pl.BlockSpec((ts, tn), lambda j, s: (s, j)), pl.BlockSpec((ts, D), lambda j, s: (s, 0))],
        out_specs=pl.BlockSpec((tn, D), lambda j, s: (j, 0)),
        out_shape=jax.ShapeDtypeStruct((W, D), F32),
        compiler_params=_params(("arbitrary", "arbitrary")),
    )(dp, h)


def matmul_tn(a, b, name, tn=512, ts=512):
    S, K = a.shape
    N = b.shape[1]
    ns = S // ts

    def body(a_ref, b_ref, o_ref):
        @pl.when(pl.program_id(1) == 0)
        def _():
            o_ref[...] = jnp.zeros_like(o_ref)
        o_ref[...] += _dot_tn(a_ref[...].astype(F32), b_ref[...])

    return pl.pallas_call(
        body, name=name, grid=(N // tn, ns),
        in_specs=[pl.BlockSpec((ts, K), lambda j, s: (s, 0)), pl.BlockSpec((ts, tn), lambda j, s: (s, j))],
        out_specs=pl.BlockSpec((K, tn), lambda j, s: (0, j)),
        out_shape=jax.ShapeDtypeStruct((K, N), F32),
        compiler_params=_params(("arbitrary", "arbitrary")),
    )(a, b)


def _att_mask(n):
    qi = lax.broadcasted_iota(jnp.int32, (L, 2 * L), 0)
    kj = lax.broadcasted_iota(jnp.int32, (L, 2 * L), 1)
    dist = qi + L - kj
    return (dist >= 0) & (dist < L) & ((kj >= L) | (n > 0))


def _att_in_specs(nb):
    last = nb - 1
    cur = lambda n: jnp.minimum(n, last)
    prev = lambda n: jnp.maximum(jnp.minimum(n, last) - 1, 0)
    return [
        pl.BlockSpec((L, 1024), lambda n: (cur(n), 0)),
        pl.BlockSpec((L, 128), lambda n: (prev(n), 16)),
        pl.BlockSpec((L, 128), lambda n: (cur(n), 16)),
        pl.BlockSpec((L, 128), lambda n: (prev(n), 17)),
        pl.BlockSpec((L, 128), lambda n: (cur(n), 17)),
        pl.BlockSpec((L, 1024), lambda n: (cur(n), 1)),
        _full((HEADS, L, 2 * L)),
        pl.BlockSpec(memory_space=pltpu.SMEM),
    ]


def _att_probs(qh, kk, bias_h, mask, sk):
    logits = _dot_nt(qh, kk) + bias_h
    logits = jnp.where(mask, logits, NEG)
    m = jnp.maximum(jnp.max(logits, axis=-1, keepdims=True), sk)
    p = jnp.exp(logits - m)
    es = jnp.exp(sk - m)
    den = jnp.sum(p, axis=-1, keepdims=True) + es
    return p / den, es / den


def att_fwd(proj, bias, sinks):
    S = proj.shape[0]
    nb = S // L

    def body(q_ref, kp_ref, kc_ref, vp_ref, vc_ref, z_ref, bias_ref, s_ref, y_ref, o_scr):
        mask = _att_mask(pl.program_id(0))
        for kv in range(KV):
            sl = slice(kv * DH, (kv + 1) * DH)
            kk = jnp.concatenate([kp_ref[:, sl], kc_ref[:, sl]], axis=0).astype(BF16)
            vv = jnp.concatenate([vp_ref[:, sl], vc_ref[:, sl]], axis=0).astype(BF16)
            for g in range(HEADS // KV):
                h = kv * (HEADS // KV) + g
                hs = slice(h * DH, (h + 1) * DH)
                qh = (q_ref[:, hs] * 0.125).astype(BF16)
                P, _ = _att_probs(qh, kk, bias_ref[h], mask, s_ref[h])
                o_scr[:, hs] = _dot(P.astype(BF16), vv)
        z = z_ref[...]
        y_ref[...] = (o_scr[...] * (z * _sigmoid(z))).astype(BF16)

    return pl.pallas_call(
        body, name="att_fwd", grid=(nb,),
        in_specs=_att_in_specs(nb),
        out_specs=pl.BlockSpec((L, 1024), lambda n: (n, 0)),
        out_shape=jax.ShapeDtypeStruct((S, 1024), BF16),
        scratch_shapes=[pltpu.VMEM((L, 1024), F32)],
        compiler_params=_params(("arbitrary",)),
    )(proj, proj, proj, proj, proj, proj, bias, sinks)


def att_bwd(dy, proj, bias, sinks):
    S = proj.shape[0]
    nb = S // L
    last = nb - 1

    def body(dy_ref, q_ref, kp_ref, kc_ref, vp_ref, vc_ref, z_ref, bias_ref, s_ref,
             dout_ref, dbias_ref, dsink_ref, carry, band, dq_scr, dz_scr):
        n = pl.program_id(0)

        @pl.when(n == 0)
        def _():
            carry[...] = jnp.zeros_like(carry)
            dq_scr[...] = jnp.zeros_like(dq_scr)
            dz_scr[...] = jnp.zeros_like(dz_scr)
            dbias_ref[...] = jnp.zeros_like(dbias_ref)
            dsink_ref[...] = jnp.zeros_like(dsink_ref)

        dout_ref[:, 0:1024] = dq_scr[...].astype(BF16)
        dout_ref[:, 1024:2048] = dz_scr[...].astype(BF16)
        band[...] = jnp.zeros_like(band)

        @pl.when(n < nb)
        def _():
            mask = _att_mask(n)
            lane = lax.broadcasted_iota(jnp.int32, (1, 128), 1)
            dsink = jnp.zeros((1, 128), F32)
            for kv in range(KV):
                sl = slice(kv * DH, (kv + 1) * DH)
                kk = jnp.concatenate([kp_ref[:, sl], kc_ref[:, sl]], axis=0).astype(BF16)
                vv = jnp.concatenate([vp_ref[:, sl], vc_ref[:, sl]], axis=0).astype(BF16)
                dk_acc = jnp.zeros((2 * L, DH), F32)
                dv_acc = jnp.zeros((2 * L, DH), F32)
                for g in range(HEADS // KV):
                    h = kv * (HEADS // KV) + g
                    hs = slice(h * DH, (h + 1) * DH)
                    qh = (q_ref[:, hs] * 0.125).astype(BF16)
                    P, psink = _att_probs(qh, kk, bias_ref[h], mask, s_ref[h])
                    Pb = P.astype(BF16)
                    zh = z_ref[:, hs]
                    sg = _sigmoid(zh)
                    dyh = dy_ref[:, hs]
                    O = _dot(Pb, vv)
                    dO = dyh * (zh * sg)
                    dz_scr[:, hs] = dyh * O * (sg * (1.0 + zh * (1.0 - sg)))
                    dOb = dO.astype(BF16)
                    dv_acc = dv_acc + _dot_tn(P, dOb)
                    dP = _dot_nt(dOb, vv)
                    delta = jnp.sum(P * dP, axis=-1, keepdims=True)
                    dS = P * (dP - delta)
                    dsink = dsink + jnp.where(lane == h, -jnp.sum(psink * delta), 0.0)
                    dSb = dS.astype(BF16)
                    dq_scr[:, hs] = _dot(dSb, kk) * 0.125
                    dk_acc = dk_acc + _dot_tn(dS, qh)
                    dbias_ref[h] += dS
                band[:, sl] = dk_acc
                band[:, 128 + kv * DH:128 + (kv + 1) * DH] = dv_acc
            dsink_ref[...] += dsink

        out = carry[...] + band[0:L, :]
        dout_ref[:, 2048:2304] = out.astype(BF16)
        carry[...] = band[L:2 * L, :]

    cur = lambda n: jnp.minimum(n, last)
    lag = lambda n: jnp.maximum(n - 1, 0)
    return pl.pallas_call(
        body, name="att_bwd", grid=(nb + 1,),
        in_specs=[pl.BlockSpec((L, 1024), lambda n: (cur(n), 0))] + _att_in_specs(nb),
        out_specs=[pl.BlockSpec((L, 2304), lambda n: (lag(n), 0)), _full((HEADS, L, 2 * L)), _full((1, 128))],
        out_shape=[jax.ShapeDtypeStruct((S, 2304), BF16),
                   jax.ShapeDtypeStruct((HEADS, L, 2 * L), F32), jax.ShapeDtypeStruct((1, 128), F32)],
        scratch_shapes=[pltpu.VMEM((L, 256), F32), pltpu.VMEM((2 * L, 256), F32),
                        pltpu.VMEM((L, 1024), F32), pltpu.VMEM((L, 1024), F32)],
        compiler_params=_params(("arbitrary",)),
    )(dy, proj, proj, proj, proj, proj, proj, bias, sinks)


def _sgu_in_specs():
    return [
        pl.BlockSpec((L, 1024), lambda c: (c, 0)),
        pl.BlockSpec((L, 1024), lambda c: (c, 1)),
        pl.BlockSpec((L, 1024), lambda c: (c, 2)),
        _full((1, 1024)), _full((1, 1024)), _full((8, L, L)), _full((L, 8)),
    ]


def _sgu_norm(v, lg, lb):
    mu = jnp.mean(v, axis=-1, keepdims=True)
    vc = v - mu
    rstd = lax.rsqrt(jnp.mean(vc * vc, axis=-1, keepdims=True) + EPS)
    xhat = vc * rstd
    return xhat * lg + lb, xhat, rstd


def _tril():
    return lax.broadcasted_iota(jnp.int32, (L, L), 0) >= lax.broadcasted_iota(jnp.int32, (L, L), 1)


def sgu_fwd(proj, ln_g, ln_b, w, b_t):
    S = proj.shape[0]

    def body(u_ref, v_ref, z_ref, lg_ref, lb_ref, w_ref, bt_ref, y_ref):
        vn, _, _ = _sgu_norm(v_ref[...], lg_ref[...], lb_ref[...])
        tri = _tril()
        parts = []
        for g in range(8):
            wg = jnp.where(tri, w_ref[g], 0.0).astype(BF16)
            parts.append(_dot(wg, vn[:, g * 128:(g + 1) * 128].astype(BF16)) + bt_ref[:, g:g + 1])
        mixed = jnp.concatenate(parts, axis=1)
        z = z_ref[...]
        y_ref[...] = (u_ref[...] * mixed * (z * _sigmoid(z))).astype(BF16)

    return pl.pallas_call(
        body, name="sgu_fwd", grid=(S // L,),
        in_specs=_sgu_in_specs(),
        out_specs=pl.BlockSpec((L, 1024), lambda c: (c, 0)),
        out_shape=jax.ShapeDtypeStruct((S, 1024), BF16),
        compiler_params=_params(("arbitrary",)),
    )(proj, proj, proj, ln_g, ln_b, w, b_t)


def sgu_bwd(dy, proj, ln_g, ln_b, w, b_t):
    S = proj.shape[0]

    def body(dy_ref, u_ref, v_ref, z_ref, lg_ref, lb_ref, w_ref, bt_ref,
             dout_ref, dw_ref, dbt_ref, dlg_ref, dlb_ref):
        @pl.when(pl.program_id(0) == 0)
        def _():
            dw_ref[...] = jnp.zeros_like(dw_ref)
            dbt_ref[...] = jnp.zeros_like(dbt_ref)
            dlg_ref[...] = jnp.zeros_like(dlg_ref)
            dlb_ref[...] = jnp.zeros_like(dlb_ref)

        lg = lg_ref[...]
        vn, xhat, rstd = _sgu_norm(v_ref[...], lg, lb_ref[...])
        tri = _tril()
        lane = lax.broadcasted_iota(jnp.int32, (L, 128), 1)
        wgs, parts = [], []
        for g in range(8):
            wg = jnp.where(tri, w_ref[g], 0.0)
            wgs.append(wg)
            parts.append(_dot(wg.astype(BF16), vn[:, g * 128:(g + 1) * 128].astype(BF16)) + bt_ref[:, g:g + 1])
        mixed = jnp.concatenate(parts, axis=1)
        z = z_ref[...]
        sg = _sigmoid(z)
        silu = z * sg
        dy_v = dy_ref[...]
        u = u_ref[...]
        dout_ref[:, 0:1024] = (dy_v * mixed * silu).astype(BF16)
        dout_ref[:, 2048:3072] = (dy_v * u * mixed * (sg * (1.0 + z * (1.0 - sg)))).astype(BF16)
        dmixed = dy_v * u * silu
        dbt = jnp.zeros((L, 128), F32)
        dvn_parts = []
        for g in range(8):
            dm = dmixed[:, g * 128:(g + 1) * 128]
            dmb = dm.astype(BF16)
            dbt = dbt + jnp.where(lane == g, jnp.sum(dm, axis=1, keepdims=True), 0.0)
            dw_ref[g] += jnp.where(tri, _dot_nt(dmb, vn[:, g * 128:(g + 1) * 128].astype(BF16)), 0.0)
            dvn_parts.append(_dot_tn(wgs[g], dmb))
        dbt_ref[...] += dbt
        dvn = jnp.concatenate(dvn_parts, axis=1)
        dlg_ref[...] += jnp.sum(dvn * xhat, axis=0, keepdims=True)
        dlb_ref[...] += jnp.sum(dvn, axis=0, keepdims=True)
        dxh = dvn * lg
        dv = rstd * (dxh - jnp.mean(dxh, axis=-1, keepdims=True)
                     - xhat * jnp.mean(dxh * xhat, axis=-1, keepdims=True))
        dout_ref[:, 1024:2048] = dv.astype(BF16)

    return pl.pallas_call(
        body, name="sgu_bwd", grid=(S // L,),
        in_specs=[pl.BlockSpec((L, 1024), lambda c: (c, 0))] + _sgu_in_specs(),
        out_specs=[pl.BlockSpec((L, 3072), lambda c: (c, 0)), _full((8, L, L)), _full((L, 128)),
                   _full((1, 1024)), _full((1, 1024))],
        out_shape=[jax.ShapeDtypeStruct((S, 3072), BF16), jax.ShapeDtypeStruct((8, L, L), F32),
                   jax.ShapeDtypeStruct((L, 128), F32), jax.ShapeDtypeStruct((1, 1024), F32),
                   jax.ShapeDtypeStruct((1, 1024), F32)],
        compiler_params=_params(("arbitrary",)),
    )(dy, proj, proj, proj, ln_g, ln_b, w, b_t)


def _expand_matrix():
    r = lax.broadcasted_iota(jnp.int32, (128, SSM_W), 0)
    c = lax.broadcasted_iota(jnp.int32, (128, SSM_W), 1)
    return (c // SSM_P) == r


def _expand_matrix_t():
    r = lax.broadcasted_iota(jnp.int32, (SSM_W, 128), 0)
    c = lax.broadcasted_iota(jnp.int32, (SSM_W, 128), 1)
    return (r // SSM_P) == c


def _rows_from(ref, start):
    C = ref.shape[1]
    tiles = ref[...].reshape(17, 8, C)
    q, s = divmod(start, 8)
    if s == 0:
        return tiles[q:q + 16].reshape(L, C)
    rolled = pltpu.roll(tiles, 8 - s, axis=1)
    sub = lax.broadcasted_iota(jnp.int32, (16, 8, C), 1)
    return jnp.where(sub < 8 - s, rolled[q:q + 16], rolled[q + 1:q + 17]).reshape(L, C)


def _ssd_common(ext_ref, cw_ref, cb_ref, dt_raw, dtb, alog):
    taps = [_rows_from(ext_ref, 5 + k) for k in range(CONV_K)]
    pre = cb_ref[...]
    for k in range(CONV_K):
        pre = pre + cw_ref[k:k + 1, :] * taps[k]
    sg_pre = _sigmoid(pre)
    xc = pre * sg_pre
    dt = _softplus(dt_raw + dtb)
    a = -jnp.exp(alog)
    adt = dt * a
    acs = _sel_dot(_tril(), adt, 3)
    return pre, sg_pre, xc, dt, a, acs, taps


def _ssd_in_specs(rev, nc):
    cidx = (lambda c: nc - 1 - c) if rev else (lambda c: c)
    return [
        pl.BlockSpec((L, 2048), lambda c: (cidx(c), 0)),
        pl.BlockSpec((L, 1024), lambda c: (cidx(c), 2)),
        pl.BlockSpec((L, 1024), lambda c: (cidx(c), 3)),
        pl.BlockSpec((L, 1024), lambda c: (cidx(c), 4)),
        pl.BlockSpec((L, 128), lambda c: (cidx(c), 40)),
        _full((8, CONV_C)), _full((1, CONV_C)), _full((1, 128)), _full((1, 128)), _full((1, 128)),
        _full((1, SSM_W)),
    ]


def ssd_fwd(proj, conv_w, conv_b, dt_bias, a_log, d_skip, norm_g):
    S = proj.shape[0]
    nc = S // L

    def body(z_ref, xa_ref, xb_ref, xc_ref, dt_ref, cw_ref, cb_ref, dtb_ref, alog_ref, dsk_ref, ng_ref,
             y_ref, hs_ref, H, ext, ysc):
        @pl.when(pl.program_id(0) == 0)
        def _():
            H[...] = jnp.zeros_like(H)
            ext[0:8, :] = jnp.zeros((8, CONV_C), F32)

        for k, ref in enumerate((xa_ref, xb_ref, xc_ref)):
            ext[8:8 + L, k * 1024:(k + 1) * 1024] = ref[...]
        pre, sg_pre, xc, dt, a, acs, _ = _ssd_common(ext, cw_ref, cb_ref, dt_ref[...], dtb_ref[...], alog_ref[...])
        for k, ref in enumerate((xa_ref, xb_ref, xc_ref)):
            ext[0:8, k * 1024:(k + 1) * 1024] = ref[L - 8:L, :]
        xs = xc[:, 0:SSM_W]
        acs_t = acs.T
        ex = _expand_matrix()
        dt_x = _dot_sel(dt, ex, 2)
        xdt = xs * dt_x
        eacs_x = _dot_sel(jnp.exp(acs), ex, 2)
        xw = xdt * _dot_sel(jnp.exp(acs[L - 1:L, :] - acs), ex, 2)
        cd_row = jnp.exp(acs[L - 1:L, :])
        hs_ref[0] = H[...]
        tri = _tril()
        for g in range(SSM_G):
            gs = slice(g * 512, (g + 1) * 512)
            bg = xc[:, SSM_W + g * SSM_N:SSM_W + (g + 1) * SSM_N].astype(BF16)
            cg = xc[:, SSM_W + 512 + g * SSM_N:SSM_W + 512 + (g + 1) * SSM_N].astype(BF16)
            G = _dot_nt(cg, bg)
            yoff = _dot_nt(cg, H[gs, :].astype(BF16)) * eacs_x[:, gs]
            Sg = _dot_tn(xw[:, gs], bg)
            for j in range(8):
                hh = g * 8 + j
                hs = slice(hh * SSM_P, (hh + 1) * SSM_P)
                seg = acs[:, hh:hh + 1] - acs_t[hh:hh + 1, :]
                dk = jnp.where(tri, jnp.exp(jnp.minimum(seg, 0.0)), 0.0)
                yd = _dot((G * dk).astype(BF16), xdt[:, hs].astype(BF16))
                ysc[:, hs] = yd + yoff[:, j * SSM_P:(j + 1) * SSM_P]
                H[hs, :] = H[hs, :] * cd_row[:, hh:hh + 1] + Sg[j * SSM_P:(j + 1) * SSM_P, :]
        d_x = _dot_sel(jnp.broadcast_to(dsk_ref[...], (8, 128)), ex, 3)[0:1, :]
        Y = ysc[...] + d_x * xs
        z = z_ref[...]
        yz = Y * (z * _sigmoid(z))
        ng = ng_ref[...]
        for g in range(SSM_G):
            gs = slice(g * 512, (g + 1) * 512)
            t = yz[:, gs]
            rstd = lax.rsqrt(jnp.mean(t * t, axis=-1, keepdims=True) + EPS)
            y_ref[:, gs] = (t * rstd * ng[:, gs]).astype(BF16)

    return pl.pallas_call(
        body, name="ssd_fwd", grid=(nc,),
        in_specs=_ssd_in_specs(False, nc),
        out_specs=[pl.BlockSpec((L, SSM_W), lambda c: (c, 0)), pl.BlockSpec((1, SSM_W, SSM_N), lambda c: (c, 0, 0))],
        out_shape=[jax.ShapeDtypeStruct((S, SSM_W), BF16), jax.ShapeDtypeStruct((nc, SSM_W, SSM_N), F32)],
        scratch_shapes=[pltpu.VMEM((SSM_W, SSM_N), F32), pltpu.VMEM((8 + L, CONV_C), F32),
                        pltpu.VMEM((L, SSM_W), F32)],
        compiler_params=_params(("arbitrary",)),
    )(proj, proj, proj, proj, proj, conv_w, conv_b, dt_bias, a_log, d_skip, norm_g)


def ssd_bwd(dy, proj, hstates, conv_w, conv_b, dt_bias, a_log, d_skip, norm_g):
    S = proj.shape[0]
    nc = S // L
    cidx = lambda c: nc - 1 - c

    def body(dy_ref, z_ref, xa_ref, xb_ref, xc_ref, dt_ref, cw_ref, cb_ref, dtb_ref, alog_ref, dsk_ref, ng_ref,
             pa_ref, pb_ref, pc_ref, hp_ref,
             dout_ref, dcw_ref, dcb_ref, ddtb_ref, dalog_ref, ddsk_ref, dng_ref,
             dH, ext, dext, ysc, yoffsc, dxdt, dxc, tsc):
        step = pl.program_id(0)
        c = nc - 1 - step

        @pl.when(step == 0)
        def _():
            dH[...] = jnp.zeros_like(dH)
            dext[L:L + 8, :] = jnp.zeros((8, CONV_C), F32)
            for r in (dcw_ref, dcb_ref, ddtb_ref, dalog_ref, ddsk_ref, dng_ref):
                r[...] = jnp.zeros_like(r)

        for k, (ref, prev) in enumerate(((xa_ref, pa_ref), (xb_ref, pb_ref), (xc_ref, pc_ref))):
            ext[0:8, k * 1024:(k + 1) * 1024] = jnp.where(c > 0, prev[...], 0.0)
            ext[8:8 + L, k * 1024:(k + 1) * 1024] = ref[...]
        dtb = dtb_ref[...]
        dt_raw = dt_ref[...]
        pre, sg_pre, xc, dt, a, acs, taps = _ssd_common(ext, cw_ref, cb_ref, dt_raw, dtb, alog_ref[...])
        xs = xc[:, 0:SSM_W]
        acs_t = acs.T
        ex = _expand_matrix()
        dt_x = _dot_sel(dt, ex, 2)
        xdt = xs * dt_x
        eacs_x = _dot_sel(jnp.exp(acs), ex, 2)
        dte_x = _dot_sel(jnp.exp(acs[L - 1:L, :] - acs), ex, 2)
        xw = xdt * dte_x
        cd_row = jnp.exp(acs[L - 1:L, :])
        tri = _tril()

        Gs, Cs, Bs = [], [], []
        for g in range(SSM_G):
            gs = slice(g * 512, (g + 1) * 512)
            bg = xc[:, SSM_W + g * SSM_N:SSM_W + (g + 1) * SSM_N].astype(BF16)
            cg = xc[:, SSM_W + 512 + g * SSM_N:SSM_W + 512 + (g + 1) * SSM_N].astype(BF16)
            G = _dot_nt(cg, bg)
            Gs.append(G), Cs.append(cg), Bs.append(bg)
            yoffsc[:, gs] = _dot_nt(cg, hp_ref[0, gs, :].astype(BF16)) * eacs_x[:, gs]
            for j in range(8):
                hh = g * 8 + j
                hs = slice(hh * SSM_P, (hh + 1) * SSM_P)
                seg = acs[:, hh:hh + 1] - acs_t[hh:hh + 1, :]
                dk = jnp.where(tri, jnp.exp(jnp.minimum(seg, 0.0)), 0.0)
                ysc[:, hs] = _dot((G * dk).astype(BF16), xdt[:, hs].astype(BF16))
        d_x = _dot_sel(jnp.broadcast_to(dsk_ref[...], (8, 128)), ex, 3)[0:1, :]
        yoff = yoffsc[...]
        Y = ysc[...] + yoff + d_x * xs

        z = z_ref[...]
        sgz = _sigmoid(z)
        silu_z = z * sgz
        yz = Y * silu_z
        ng = ng_ref[...]
        dout = dy_ref[...]
        dyn = dout * ng
        dyz_parts, dng_parts = [], []
        for g in range(SSM_G):
            gs = slice(g * 512, (g + 1) * 512)
            t = yz[:, gs]
            rstd = lax.rsqrt(jnp.mean(t * t, axis=-1, keepdims=True) + EPS)
            dng_parts.append(jnp.sum(dout[:, gs] * t * rstd, axis=0, keepdims=True))
            dn = dyn[:, gs]
            dyz_parts.append(rstd * dn - t * (rstd * rstd * rstd) * jnp.mean(dn * t, axis=-1, keepdims=True))
        dng_ref[...] += jnp.concatenate(dng_parts, axis=1)
        dyz = jnp.concatenate(dyz_parts, axis=1)
        dY = dyz * silu_z
        dout_ref[:, 0:SSM_W] = (dyz * Y * (sgz * (1.0 + z * (1.0 - sgz)))).astype(BF16)

        ex_t = _expand_matrix_t()
        ddsk_ref[...] += _dot_sel(jnp.broadcast_to(jnp.sum(dY * xs, axis=0, keepdims=True), (8, SSM_W)), ex_t, 3)[0:1, :]

        lane = lax.broadcasted_iota(jnp.int32, (L, 128), 1)
        subl = lax.broadcasted_iota(jnp.int32, (128, L), 0)
        coll = lax.broadcasted_iota(jnp.int32, (128, L), 1)
        r_cols = jnp.zeros((L, 128), F32)
        c_rows = jnp.zeros((128, L), F32)
        for g in range(SSM_G):
            gs = slice(g * 512, (g + 1) * 512)
            G, cg, bg = Gs[g], Cs[g], Bs[g]
            hp_g = hp_ref[0, gs, :]
            dh_g = dH[gs, :]
            dY_g = dY[:, gs]
            dZ = dY_g * eacs_x[:, gs]
            dZb = dZ.astype(BF16)
            dC = _dot(dZb, hp_g.astype(BF16))
            dh_from_off = _dot_tn(dZ, cg)
            dhb = dh_g.astype(BF16)
            Q = _dot_nt(bg, dhb)
            dB = _dot(xw[:, gs].astype(BF16), dhb)
            qd = Q * dte_x[:, gs]
            dxdt[:, gs] = qd
            tsc[:, gs] = qd * xdt[:, gs]
            dG = jnp.zeros((L, L), F32)
            for j in range(8):
                hh = g * 8 + j
                hs = slice(hh * SSM_P, (hh + 1) * SSM_P)
                seg = acs[:, hh:hh + 1] - acs_t[hh:hh + 1, :]
                dk = jnp.where(tri, jnp.exp(jnp.minimum(seg, 0.0)), 0.0)
                M = G * dk
                dYh = dY[:, hs]
                dYhb = dYh.astype(BF16)
                dM = _dot_nt(dYhb, xdt[:, hs].astype(BF16))
                dxdt[:, hs] += _dot_tn(M, dYhb)
                dG = dG + dM * dk
                Wm = dM * M
                r_cols = r_cols + jnp.where(lane == hh, jnp.sum(Wm, axis=1, keepdims=True), 0.0)
                c_rows = c_rows + jnp.where(subl == hh, jnp.sum(Wm, axis=0, keepdims=True), 0.0)
                pj = slice(j * SSM_P, (j + 1) * SSM_P)
                cd_h = cd_row[:, hh:hh + 1]
                dcd = jnp.sum(dh_g[pj, :] * hp_g[pj, :]) * cd_h
                c_rows = c_rows - jnp.where((subl == hh) & (coll == L - 1), dcd, 0.0)
                dH[hs, :] = dh_g[pj, :] * cd_h + dh_from_off[pj, :]
            dGb = dG.astype(BF16)
            dC = dC + _dot(dGb, bg)
            dB = dB + _dot_tn(dG, cg)
            dxc[:, SSM_W + g * SSM_N:SSM_W + (g + 1) * SSM_N] = dB
            dxc[:, SSM_W + 512 + g * SSM_N:SSM_W + 512 + (g + 1) * SSM_N] = dC

        row = lax.broadcasted_iota(jnp.int32, (L, 128), 0)
        tv = tsc[...]
        t_last = _dot_sel(jnp.broadcast_to(jnp.sum(tv, axis=0, keepdims=True), (8, SSM_W)), ex_t, 3)[0:1, :]
        dacs = (r_cols - c_rows.T + _dot_sel(dY * yoff - tv, ex_t, 2) + jnp.where(row == L - 1, t_last, 0.0))
        triu = lax.broadcasted_iota(jnp.int32, (L, L), 0) <= lax.broadcasted_iota(jnp.int32, (L, L), 1)
        dadt = _sel_dot(triu, dacs, 3)
        dxdt_v = dxdt[...]
        ddt = _dot_sel(dxdt_v * xs, ex_t, 2) + dadt * a
        dalog_ref[...] += jnp.sum(dadt * dt * a, axis=0, keepdims=True)
        ddt_raw = jnp.where(lane < SSM_H, ddt * _sigmoid(dt_raw + dtb), 0.0)
        ddtb_ref[...] += jnp.sum(ddt_raw, axis=0, keepdims=True)
        dout_ref[:, 5120:5248] = ddt_raw.astype(BF16)
        dout_ref[:, 5248:5376] = jnp.zeros((L, 128), BF16)

        dxc[:, 0:SSM_W] = dxdt_v * dt_x + d_x * dY
        dpre = dxc[...] * (sg_pre * (1.0 + pre * (1.0 - sg_pre)))
        dcb_ref[...] += jnp.sum(dpre, axis=0, keepdims=True)
        dext[0:L, :] = dpre
        x_cur = ext[8:8 + L, :]
        dx = None
        for k in range(CONV_K):
            dsh = _rows_from(dext, 3 - k)
            term = cw_ref[k:k + 1, :] * dsh
            dx = term if dx is None else dx + term
            dcw_ref[k:k + 1, :] += jnp.sum(dsh * x_cur, axis=0, keepdims=True)
        dout_ref[:, SSM_W:SSM_W + CONV_C] = dx.astype(BF16)
        dext[L:L + 8, :] = dpre[0:8, :]

    big = lambda w: pl.BlockSpec((L, w), lambda c: (cidx(c), 0))
    return pl.pallas_call(
        body, name="ssd_bwd", grid=(nc,),
        in_specs=[big(SSM_W)] + _ssd_in_specs(True, nc) + [
            pl.BlockSpec((8, 1024), lambda c, k=k: (jnp.maximum(16 * cidx(c) - 1, 0), k)) for k in (2, 3, 4)] + [
            pl.BlockSpec((1, SSM_W, SSM_N), lambda c: (cidx(c), 0, 0))],
        out_specs=[big(5376), _full((8, CONV_C)), _full((1, CONV_C)),
                   _full((1, 128)), _full((1, 128)), _full((1, 128)), _full((1, SSM_W))],
        out_shape=[jax.ShapeDtypeStruct((S, 5376), BF16), jax.ShapeDtypeStruct((8, CONV_C), F32),
                   jax.ShapeDtypeStruct((1, CONV_C), F32), jax.ShapeDtypeStruct((1, 128), F32),
                   jax.ShapeDtypeStruct((1, 128), F32), jax.ShapeDtypeStruct((1, 128), F32),
                   jax.ShapeDtypeStruct((1, SSM_W), F32)],
        scratch_shapes=[pltpu.VMEM((SSM_W, SSM_N), F32), pltpu.VMEM((8 + L, CONV_C), F32),
                        pltpu.VMEM((L + 8, CONV_C), F32), pltpu.VMEM((L, SSM_W), F32),
                        pltpu.VMEM((L, SSM_W), F32), pltpu.VMEM((L, SSM_W), F32),
                        pltpu.VMEM((L, CONV_C), F32), pltpu.VMEM((L, SSM_W), F32)],
        compiler_params=_params(("arbitrary",)),
    )(dy, proj, proj, proj, proj, proj, conv_w, conv_b, dt_bias, a_log, d_skip, norm_g, proj, proj, proj, hstates)


def _resident(shape):
    nd = len(shape)
    return pl.BlockSpec(shape, lambda *_: (0,) * nd, pipeline_mode=pl.Buffered(1))


def merge_fwd(y_att, y_sg, y_ssm, proj, x, w_a, w_s, w_m, w_o, g_post):
    S = x.shape[0]
    tm = 256

    def body(ya_ref, ys_ref, ym_ref, gate_ref, x_ref, wa_ref, ws_ref, wm_ref, wo_ref, gp_ref,
             xn_ref, bra_ref, brs_ref, brm_ref, mg_ref, out_ref):
        bra = _dot(ya_ref[...], wa_ref[...])
        brs = _dot(ys_ref[...], ws_ref[...])
        brm = _dot(ym_ref[...], wm_ref[...])
        bra_ref[...] = bra
        brs_ref[...] = brs
        brm_ref[...] = brm
        merged = (_sigmoid(gate_ref[:, 0:1024]) * bra + _sigmoid(gate_ref[:, 1024:2048]) * brs
                  + _sigmoid(gate_ref[:, 2048:3072]) * brm)
        mb = merged.astype(BF16)
        mg_ref[...] = mb
        o = _dot(mb, wo_ref[...])
        out_ref[...] = o
        r = lax.rsqrt(jnp.mean(o * o, axis=-1, keepdims=True) + EPS)
        xn_ref[...] = x_ref[...] + o * r * gp_ref[...]

    row = lambda w: pl.BlockSpec((tm, w), lambda i: (i, 0))
    return pl.pallas_call(
        body, name="merge_fwd", grid=(S // tm,),
        in_specs=[row(1024), row(1024), row(2048), pl.BlockSpec((tm, 3072), lambda i: (i, 0)),
                  row(D), _resident((1024, D)), _resident((1024, D)), _resident((2048, D)), _resident((D, D)),
                  _full((1, D))],
        out_specs=[row(D)] * 6,
        out_shape=[jax.ShapeDtypeStruct((S, D), F32)] * 4 + [jax.ShapeDtypeStruct((S, D), BF16),
                                                             jax.ShapeDtypeStruct((S, D), F32)],
        compiler_params=_params(("arbitrary",)),
    )(y_att, y_sg, y_ssm, proj, x, w_a, w_s, w_m, w_o, g_post)


def merge_bwd(dy, out, g_post, proj, br_a, br_s, br_m, w_a, w_s, w_m, w_o):
    S = dy.shape[0]
    tm = 256

    def body(dy_ref, o_ref, gp_ref, gate_ref, bra_ref, brs_ref, brm_ref, wa_ref, ws_ref, wm_ref, wo_ref,
             dout_ref, dba_ref, dbs_ref, dbm_ref, dgate_ref, dya_ref, dys_ref, dym_ref, dgp_ref):
        @pl.when(pl.program_id(0) == 0)
        def _():
            dgp_ref[...] = jnp.zeros_like(dgp_ref)

        o = o_ref[...]
        dyv = dy_ref[...]
        r = lax.rsqrt(jnp.mean(o * o, axis=-1, keepdims=True) + EPS)
        dyg = dyv * gp_ref[...]
        do = r * dyg - o * (r * r * r) * jnp.mean(dyg * o, axis=-1, keepdims=True)
        dgp_ref[...] += jnp.sum(dyv * o * r, axis=0, keepdims=True)
        dob = do.astype(BF16)
        dout_ref[...] = dob
        dmerged = _dot_nt(dob, wo_ref[...])
        for idx, (br_ref, dbr_ref, w_ref, dyi_ref) in enumerate((
                (bra_ref, dba_ref, wa_ref, dya_ref), (brs_ref, dbs_ref, ws_ref, dys_ref),
                (brm_ref, dbm_ref, wm_ref, dym_ref))):
            s = _sigmoid(gate_ref[:, idx * 1024:(idx + 1) * 1024])
            dbr = (dmerged * s).astype(BF16)
            dbr_ref[...] = dbr
            dgate_ref[:, idx * 1024:(idx + 1) * 1024] = (dmerged * br_ref[...] * s * (1.0 - s)).astype(BF16)
            dyi_ref[...] = _dot_nt(dbr, w_ref[...])

    row = lambda w: pl.BlockSpec((tm, w), lambda i: (i, 0))
    return pl.pallas_call(
        body, name="merge_bwd", grid=(S // tm,),
        in_specs=[row(D), row(D), _full((1, D)), pl.BlockSpec((tm, 3072), lambda i: (i, 0)),
                  row(D), row(D), row(D),
                  _resident((1024, D)), _resident((1024, D)), _resident((2048, D)), _resident((D, D))],
        out_specs=[row(D), row(D), row(D), row(D), row(3072), row(1024), row(1024), row(2048), _full((1, D))],
        out_shape=[jax.ShapeDtypeStruct((S, D), BF16)] * 4 + [
            jax.ShapeDtypeStruct((S, 3072), BF16), jax.ShapeDtypeStruct((S, 1024), F32),
            jax.ShapeDtypeStruct((S, 1024), F32), jax.ShapeDtypeStruct((S, 2048), F32),
            jax.ShapeDtypeStruct((1, D), F32)],
        compiler_params=_params(("arbitrary",)),
    )(dy, out, g_post, proj, br_a, br_s, br_m, w_a, w_s, w_m, w_o)


def loss_head(y, target):
    S = y.shape[0]
    tm = 512

    def body(y_ref, t_ref, dy_ref, loss_ref):
        @pl.when(pl.program_id(0) == 0)
        def _():
            loss_ref[...] = jnp.zeros_like(loss_ref)
        e = y_ref[...] - t_ref[...]
        dy_ref[...] = e * (1.0 / D)
        loss_ref[...] += 0.5 * jnp.sum(jnp.mean(e * e, axis=-1, keepdims=True))

    row = pl.BlockSpec((tm, D), lambda i: (i, 0))
    return pl.pallas_call(
        body, name="loss_head", grid=(S // tm,),
        in_specs=[row, row], out_specs=[row, _full((1, 128))],
        out_shape=[jax.ShapeDtypeStruct((S, D), F32), jax.ShapeDtypeStruct((1, 128), F32)],
        compiler_params=_params(("arbitrary",)),
    )(y, target)


def _adam(w, g, m, v):
    mn = ADAM_B1 * m + (1.0 - ADAM_B1) * g
    vn = ADAM_B2 * v + (1.0 - ADAM_B2) * (g * g)
    m_hat = mn / (1.0 - ADAM_B1 ** ADAM_STEP)
    v_hat = vn / (1.0 - ADAM_B2 ** ADAM_STEP)
    return -ADAM_LR * (m_hat / (jnp.sqrt(v_hat) + ADAM_EPS) + ADAM_WD * w), mn, vn


def adamw_big(w, m, v, f, fb, cc, name, tr, f_row0=0):
    _, R, C = w.shape
    nper = R // tr
    foff = f_row0 // tr

    def body(c_ref, w_ref, m_ref, v_ref, f_ref, fb_ref, g_ref, d_ref, nm_ref, nv_ref):
        layer = pl.program_id(0) // nper
        g = jnp.where(c_ref[0] == layer, f_ref[...], fb_ref[...])
        g_ref[0] = g
        d_ref[0], nm_ref[0], nv_ref[0] = _adam(w_ref[0], g, m_ref[0], v_ref[0])

    wblk = pl.BlockSpec((1, tr, C), lambda i, c: (i // nper, i % nper, 0))
    fblk = pl.BlockSpec((tr, C), lambda i, c: (foff + i % nper, 0))
    grid_spec = pltpu.PrefetchScalarGridSpec(
        num_scalar_prefetch=1, grid=(2 * nper,),
        in_specs=[wblk, wblk, wblk, fblk, fblk], out_specs=[wblk] * 4)
    return pl.pallas_call(
        body, name=name, grid_spec=grid_spec,
        out_shape=[jax.ShapeDtypeStruct(w.shape, F32)] * 4,
        compiler_params=_params(("arbitrary",)),
    )(cc, w, m, v, f, fb)


def adamw_plain(w, g, m, v, name):
    def body(w_ref, g_ref, m_ref, v_ref, d_ref, nm_ref, nv_ref):
        d_ref[...], nm_ref[...], nv_ref[...] = _adam(w_ref[...], g_ref[...], m_ref[...], v_ref[...])

    return pl.pallas_call(
        body, name=name, out_shape=[jax.ShapeDtypeStruct(w.shape, F32)] * 3, compiler_params=_params(),
    )(w, g, m, v)


SMALL = {"norm_pre": ("g_pre", 8), "norm_post": ("g_post", 8), "att_sinks": ("sinks", 8), "sg_ln_g": ("ln_g", 8),
         "sg_ln_b": ("ln_b", 8), "sg_w": ("sg_w", 1024), "sg_b": ("sg_bt", 8), "ssm_conv_b": ("conv_b", 24),
         "ssm_dt_bias": ("dt_bias", 8), "ssm_a_log": ("a_log", 8), "ssm_d": ("d_skip", 8), "ssm_norm_g": ("norm_g", 16)}
SMALL_LAYER_ROWS = sum(r for _, r in SMALL.values())
REL_ROW = DEPTH * SMALL_LAYER_ROWS
LOSS_ROW = REL_ROW + 32
SMALL_ROWS = LOSS_ROW + 8


def _small_rows():
    rows, r = {}, 0
    for l in range(DEPTH):
        for name, (_, n) in SMALL.items():
            rows[(l, name)] = r
            r += n
    return rows


def adamw_small(red, rel, small):
    names = list(SMALL) + ["rel_bias"]
    params = dict(small, rel_bias=rel)
    rows = _small_rows()

    def grad_of(red_ref, l, name, n):
        r0 = rows[(l, name)]
        if name == "sg_b":
            return red_ref[r0:r0 + 8, :]
        if n < 128:
            return red_ref[r0:r0 + 1, 0:n]
        return jnp.concatenate([red_ref[r0 + j:r0 + j + 1, :] for j in range(n // 128)], axis=1)

    def body(red_ref, *refs):
        ins, outs = refs[:3 * len(names)], refs[3 * len(names):]
        for i, name in enumerate(names):
            w_ref, m_ref, v_ref = ins[3 * i:3 * i + 3]
            o = outs[4 * i:4 * i + 4]
            if name == "rel_bias":
                g = red_ref[REL_ROW:REL_ROW + 32, 0:16]
                o[0][...] = g
                o[1][...], o[2][...], o[3][...] = _adam(w_ref[...], g, m_ref[...], v_ref[...])
                continue
            for l in range(DEPTH):
                if name == "sg_w":
                    for grp in range(8):
                        r0 = rows[(l, name)] + grp * 128
                        g = red_ref[r0:r0 + 128, :]
                        o[0][l, grp] = g
                        o[1][l, grp], o[2][l, grp], o[3][l, grp] = _adam(w_ref[l, grp], g, m_ref[l, grp], v_ref[l, grp])
                elif name == "sg_b":
                    g = grad_of(red_ref, l, name, 128)
                    o[0][l] = g
                    o[1][l], o[2][l], o[3][l] = _adam(w_ref[l], g, m_ref[l], v_ref[l])
                else:
                    sl = slice(l, l + 1)
                    g = grad_of(red_ref, l, name, w_ref.shape[-1])
                    o[0][sl, :] = g
                    o[1][sl, :], o[2][sl, :], o[3][sl, :] = _adam(w_ref[sl, :], g, m_ref[sl, :], v_ref[sl, :])

    flat_in = [a for name in names for a in params[name]]
    out_shape = [jax.ShapeDtypeStruct(params[name][0].shape, F32) for name in names for _ in range(4)]
    res = pl.pallas_call(body, name="adamw_small", out_shape=out_shape, compiler_params=_params())(red, *flat_in)
    return {name: tuple(res[4 * i:4 * i + 4]) for i, name in enumerate(names)}


ANY = pl.BlockSpec(memory_space=pl.ANY)


def _place():
    x, y, c = lax.axis_index("x"), lax.axis_index("y"), lax.axis_index("c")
    others = [(1 - x, y), (x, 1 - y), (1 - x, 1 - y)]
    return x, y, c, others


def _rcopy(src, dst, ssem, rsem, to):
    return pltpu.make_async_remote_copy(src_ref=src, dst_ref=dst, send_sem=ssem, recv_sem=rsem,
                                        device_id=to, device_id_type=MESH)


def gather_weights(arrs):
    n = len(arrs)

    def body(*refs):
        srcs, outs, ssem, rsem = refs[:n], refs[n:2 * n], refs[2 * n], refs[2 * n + 1]
        x, y, c, others = _place()
        me = 2 * x + y
        sib = (x, y, 1 - c)
        first = [_rcopy(srcs[i].at[c], outs[i].at[c, me], ssem.at[6 * i + k], rsem.at[6 * i + k], (ox, oy, c))
                 for i in range(n) for k, (ox, oy) in enumerate(others)]
        for cp in first:
            cp.start()
        passed = []
        for k, (ox, oy) in enumerate(others):
            for i in range(n):
                slot = outs[i].at[c, 2 * ox + oy]
                _rcopy(slot, slot, ssem.at[6 * i + k], rsem.at[6 * i + k], sib).wait_recv()
                fw = _rcopy(slot, slot, ssem.at[6 * i + 3 + k], rsem.at[6 * i + 3 + k], sib)
                fw.start()
                passed.append(fw)
        for k, (ox, oy) in enumerate(others):
            for i in range(n):
                slot = outs[i].at[1 - c, 2 * ox + oy]
                _rcopy(slot, slot, ssem.at[6 * i + 3 + k], rsem.at[6 * i + 3 + k], sib).wait_recv()
        for cp in first + passed:
            cp.wait_send()

    return pl.pallas_call(
        body, name="gather_weights",
        in_specs=[ANY] * n, out_specs=[ANY] * n,
        out_shape=[jax.ShapeDtypeStruct((2, SHARDS) + a.shape[1:], a.dtype) for a in arrs],
        scratch_shapes=[pltpu.SemaphoreType.DMA((6 * n,)), pltpu.SemaphoreType.DMA((6 * n,))],
    )(*arrs)


def grad_sibling_exchange(arrs):
    n = len(arrs)

    def body(*refs):
        srcs, outs, ssem, rsem = refs[:n], refs[n:2 * n], refs[2 * n], refs[2 * n + 1]
        x, y, c, _ = _place()
        cps = [_rcopy(srcs[i].at[1 - c], outs[i], ssem.at[i], rsem.at[i], (x, y, 1 - c)) for i in range(n)]
        for cp in cps:
            cp.start()
        for cp in cps:
            cp.wait()

    return pl.pallas_call(
        body, name="grad_sibling_exchange",
        in_specs=[ANY] * n, out_specs=[ANY] * n,
        out_shape=[jax.ShapeDtypeStruct(a.shape[1:], F32) for a in arrs],
        scratch_shapes=[pltpu.SemaphoreType.DMA((n,)), pltpu.SemaphoreType.DMA((n,))],
    )(*arrs)


def grad_chip_sum(g, sb, cc, tr, name):
    _, _, R, C = g.shape
    blk = pl.BlockSpec((1, tr, C), lambda s, r, c: (s, r, 0))
    grid_spec = pltpu.PrefetchScalarGridSpec(
        num_scalar_prefetch=1, grid=(SHARDS, R // tr),
        in_specs=[pl.BlockSpec((1, 1, tr, C), lambda s, r, c: (c[0], s, r, 0)), blk],
        out_specs=[blk, blk])

    def body(c_ref, a_ref, b_ref, o_ref, ob_ref):
        t = a_ref[0] + b_ref[...]
        o_ref[...] = t
        ob_ref[...] = t.astype(BF16)

    return pl.pallas_call(
        body, name=name, grid_spec=grid_spec,
        out_shape=[jax.ShapeDtypeStruct((SHARDS, R, C), F32), jax.ShapeDtypeStruct((SHARDS, R, C), BF16)],
        compiler_params=_params(("arbitrary", "arbitrary")),
    )(cc, g, sb)


def grad_chip_exchange(arrs):
    n = len(arrs)

    def body(*refs):
        srcs, outs, ssem, rsem = refs[:n], refs[n:2 * n], refs[2 * n], refs[2 * n + 1]
        x, y, c, others = _place()
        me = 2 * x + y
        sends = [_rcopy(srcs[i].at[2 * ox + oy], outs[i].at[me], ssem.at[3 * i + k], rsem.at[3 * i + k], (ox, oy, c))
                 for i in range(n) for k, (ox, oy) in enumerate(others)]
        for cp in sends:
            cp.start()
        for i in range(n):
            for k, (ox, oy) in enumerate(others):
                slot = outs[i].at[2 * ox + oy]
                _rcopy(slot, slot, ssem.at[3 * i + k], rsem.at[3 * i + k], (ox, oy, c)).wait_recv()
        for cp in sends:
            cp.wait_send()

    return pl.pallas_call(
        body, name="grad_chip_exchange",
        in_specs=[ANY] * n, out_specs=[ANY] * n,
        out_shape=[jax.ShapeDtypeStruct(a.shape, a.dtype) for a in arrs],
        scratch_shapes=[pltpu.SemaphoreType.DMA((3 * n,)), pltpu.SemaphoreType.DMA((3 * n,))],
    )(*arrs)


def grad_shard_sum(t, rb, me, tr, name):
    _, R, C = t.shape
    grid_spec = pltpu.PrefetchScalarGridSpec(
        num_scalar_prefetch=1, grid=(R // tr,),
        in_specs=[pl.BlockSpec((1, tr, C), lambda r, m: (m[0], r, 0)),
                  pl.BlockSpec((SHARDS, tr, C), lambda r, m: (0, r, 0))],
        out_specs=pl.BlockSpec((tr, C), lambda r, m: (r, 0)))

    def body(m_ref, t_ref, r_ref, o_ref):
        part = [jnp.where(m_ref[0] == s, t_ref[0], r_ref[s].astype(F32)) for s in range(SHARDS)]
        o_ref[...] = ((part[0] + part[1]) + part[2]) + part[3]

    return pl.pallas_call(
        body, name=name, grid_spec=grid_spec,
        out_shape=jax.ShapeDtypeStruct((R, C), F32),
        compiler_params=_params(("arbitrary",)),
    )(me, t, rb)


def grad_sibling_share(arrs):
    n = len(arrs)

    def body(*refs):
        srcs, outs, ssem, rsem = refs[:n], refs[n:2 * n], refs[2 * n], refs[2 * n + 1]
        x, y, c, _ = _place()
        cps = [_rcopy(srcs[i], outs[i], ssem.at[i], rsem.at[i], (x, y, 1 - c)) for i in range(n)]
        for cp in cps:
            cp.start()
        for cp in cps:
            cp.wait()

    return pl.pallas_call(
        body, name="grad_sibling_share",
        in_specs=[ANY] * n, out_specs=[ANY] * n,
        out_shape=[jax.ShapeDtypeStruct(a.shape, F32) for a in arrs],
        scratch_shapes=[pltpu.SemaphoreType.DMA((n,)), pltpu.SemaphoreType.DMA((n,))],
    )(*arrs)


def _allreduce_rows(src, sib_buf, chips, out_ref, ssem, rsem):
    x, y, c, others = _place()
    me = 2 * x + y
    cp = _rcopy(src, sib_buf, ssem.at[0], rsem.at[0], (x, y, 1 - c))
    cp.start()
    cp.wait()
    chips[me] = src[...] + sib_buf[...]
    sends = [_rcopy(chips.at[me], chips.at[me], ssem.at[1 + k], rsem.at[1 + k], (ox, oy, c))
             for k, (ox, oy) in enumerate(others)]
    for s in sends:
        s.start()
    for k, (ox, oy) in enumerate(others):
        slot = chips.at[2 * ox + oy]
        _rcopy(slot, slot, ssem.at[1 + k], rsem.at[1 + k], (ox, oy, c)).wait_recv()
    for s in sends:
        s.wait_send()
    out_ref[...] = ((chips[0] + chips[1]) + chips[2]) + chips[3]


def _allreduce_scratch(rows):
    return [pltpu.VMEM((rows, 128), F32), pltpu.VMEM((SHARDS, rows, 128), F32),
            pltpu.SemaphoreType.DMA((4,)), pltpu.SemaphoreType.DMA((4,))]


def allreduce_rows(buf, name):
    rows = buf.shape[0]
    VM = pl.BlockSpec(memory_space=pltpu.VMEM)

    def body(src_ref, out_ref, sib_buf, chips, ssem, rsem):
        _allreduce_rows(src_ref, sib_buf, chips, out_ref, ssem, rsem)

    return pl.pallas_call(
        body, name=name, in_specs=[VM], out_specs=VM,
        out_shape=jax.ShapeDtypeStruct((rows, 128), F32),
        scratch_shapes=_allreduce_scratch(rows), compiler_params=_params(),
    )(buf)


def small_allreduce(grads, rel, loss_part):
    rows = _small_rows()
    keys = [(l, name) for l in range(DEPTH) for name in SMALL]
    flat = [grads[l][SMALL[name][0]] for l, name in keys] + [rel, loss_part]

    def body(*refs):
        ins = refs[:len(flat)]
        out_ref, src, sib_buf, chips, ssem, rsem = refs[len(flat):]
        src[...] = jnp.zeros_like(src)
        for (l, name), ref in zip(keys, ins):
            r0 = rows[(l, name)]
            if name == "sg_w":
                for grp in range(8):
                    src[r0 + grp * 128:r0 + (grp + 1) * 128, :] = ref[grp]
            elif name == "sg_b":
                src[r0:r0 + 8, :] = ref[...].T[0:8, :]
            else:
                for j in range(ref.shape[1] // 128):
                    src[r0 + j:r0 + j + 1, :] = ref[:, j * 128:(j + 1) * 128]
        src[REL_ROW:REL_ROW + 32, 0:16] = ins[-2][...]
        src[LOSS_ROW:LOSS_ROW + 1, :] = ins[-1][...]
        _allreduce_rows(src, sib_buf, chips, out_ref, ssem, rsem)

    return pl.pallas_call(
        body, name="small_allreduce",
        out_shape=jax.ShapeDtypeStruct((SMALL_ROWS, 128), F32),
        scratch_shapes=[pltpu.VMEM((SMALL_ROWS, 128), F32)] + _allreduce_scratch(SMALL_ROWS),
        compiler_params=_params(),
    )(*flat)


def _pad_lanes(v):
    return jnp.zeros((1, 128), F32).at[0, :v.shape[0]].set(v)


def layer_fwd(x, wts, bias):
    wt = wts["wt"]
    tn = {name: t for name, _, t in GROUPS}
    p_gate, h = inproj_first(x, wts["g_pre"], wt["gate"], tn["gate"], "inproj_gate")
    p_sgu, p_att, p_ssd = (inproj_group(h, wt[n], tn[n], "inproj_" + n) for n in ("sgu", "att", "ssd"))
    y_att = att_fwd(p_att, bias, wts["sinks"])
    y_sg = sgu_fwd(p_sgu, wts["ln_g"], wts["ln_b"], wts["sg_w"], wts["sg_bt"])
    y_ssm, hst = ssd_fwd(p_ssd, wts["conv_w"], wts["conv_b"], wts["dt_bias"], wts["a_log"], wts["d_skip"],
                         wts["norm_g"])
    x_new, br_a, br_s, br_m, merged, out = merge_fwd(
        y_att, y_sg, y_ssm, p_gate, x, wts["w_a"], wts["w_s"], wts["w_m"], wts["w_o"], wts["g_post"])
    saved = dict(x=x, p_gate=p_gate, p_sgu=p_sgu, p_att=p_att, p_ssd=p_ssd, h=h,
                 y_att=y_att, y_sg=y_sg, y_ssm=y_ssm, hst=hst,
                 br_a=br_a, br_s=br_s, br_m=br_m, merged=merged, out=out)
    return x_new, saved


def layer_bwd(dy, wts, bias, sv):
    dout, dba, dbs, dbm, d_gate, dya, dys, dym, dg_post = merge_bwd(
        dy, sv["out"], wts["g_post"], sv["p_gate"], sv["br_a"], sv["br_s"], sv["br_m"],
        wts["w_a"], wts["w_s"], wts["w_m"], wts["w_o"])
    d_att, dbias, dsinks = att_bwd(dya, sv["p_att"], bias, wts["sinks"])
    d_sgu, dsg_w, dsg_bt, dln_g, dln_b = sgu_bwd(dys, sv["p_sgu"], wts["ln_g"], wts["ln_b"], wts["sg_w"],
                                                 wts["sg_bt"])
    d_ssd, dcw, dcb, ddtb, dalog, ddsk, dng = ssd_bwd(
        dym, sv["p_ssd"], sv["hst"], wts["conv_w"], wts["conv_b"], wts["dt_bias"], wts["a_log"], wts["d_skip"],
        wts["norm_g"])
    dps = dict(gate=d_gate, sgu=d_sgu, att=d_att, ssd=d_ssd)
    wt = wts["wt"]
    tn = {name: t for name, _, t in GROUPS}
    acc = None
    for n in ("gate", "sgu", "ssd"):
        acc = dh_group(dps[n], wt[n], acc, tn[n], "dh_" + n)
    dx, dg_pre = dh_last(dps["att"], wt["att"], acc, sv["x"], wts["g_pre"], dy, tn["att"], "dh_att")
    grads = dict(
        w_in={n: dw_group(dps[n], sv["h"], tn[n], "dw_in_" + n) for n in dps},
        w_a=matmul_tn(sv["y_att"], dba, "dw_att"),
        w_s=matmul_tn(sv["y_sg"], dbs, "dw_sg"),
        w_m=matmul_tn(sv["y_ssm"], dbm, "dw_ssm"),
        w_o=matmul_tn(sv["merged"], dout, "dw_out"),
        g_pre=dg_pre, g_post=dg_post, sinks=dsinks, ln_g=dln_g, ln_b=dln_b, sg_w=dsg_w, sg_bt=dsg_bt,
        conv_w=dcw, conv_b=dcb, dt_bias=ddtb, a_log=dalog, d_skip=ddsk, norm_g=dng, bias=dbias)
    return dx, grads


REST_OFF = (0, 256, 512, 1024, 1280)
REST_ROWS = 1296


def kernel(x, w_in, norm_pre, norm_post, rel_bias, att_sinks, sg_ln_g, sg_ln_b, sg_w, sg_b, ssm_conv_w, ssm_conv_b, ssm_dt_bias, ssm_a_log, ssm_d, ssm_norm_g, w_br_att, w_br_sg, w_br_ssm, w_out, loss_target, m_w_in, m_norm_pre, m_norm_post, m_rel_bias, m_att_sinks, m_sg_ln_g, m_sg_ln_b, m_sg_w, m_sg_b, m_ssm_conv_w, m_ssm_conv_b, m_ssm_dt_bias, m_ssm_a_log, m_ssm_d, m_ssm_norm_g, m_w_br_att, m_w_br_sg, m_w_br_ssm, m_w_out, v_w_in, v_norm_pre, v_norm_post, v_rel_bias, v_att_sinks, v_sg_ln_g, v_sg_ln_b, v_sg_w, v_sg_b, v_ssm_conv_w, v_ssm_conv_b, v_ssm_dt_bias, v_ssm_a_log, v_ssm_d, v_ssm_norm_g, v_w_br_att, v_w_br_sg, v_w_br_ssm, v_w_out):
    cx, cy, cc = lax.axis_index("x"), lax.axis_index("y"), lax.axis_index("c")
    me = 2 * cx + cy
    xs = x[0]
    S = xs.shape[0]

    tr = lambda a: jnp.transpose(a, (0, 2, 1))
    w_in_b = tr(w_in).astype(BF16)
    w_rest_b = jnp.concatenate([w_br_att, w_br_sg, w_br_ssm, w_out], axis=1).astype(BF16)
    all_in, all_rest = gather_weights([w_in_b, w_rest_b])
    convw_slot = jnp.zeros((SHARDS, DEPTH * CONV_K * 768 // 128, 128), F32)
    convw_slot = lax.dynamic_update_index_in_dim(
        convw_slot, jnp.where(cc == 0, 1.0, 0.0) * ssm_conv_w.reshape(-1, 128), me, 0)
    convw_all = allreduce_rows(convw_slot.reshape(-1, 128), "gather_conv_w")
    convw_all = convw_all.reshape(SHARDS, DEPTH, CONV_K, 768).transpose(1, 2, 0, 3).reshape(DEPTH, CONV_K, CONV_C)

    def shards_of(gathered, mine, l, lo, hi):
        return [jnp.where(me == s, mine[l, lo:hi], gathered[l, s, lo:hi]) for s in range(SHARDS)]

    o = REST_OFF
    layers = []
    for l in range(DEPTH):
        w_in_t = jnp.concatenate(shards_of(all_in, w_in_b, l, 0, 3400), axis=0)
        rest = lambda k: jnp.concatenate(shards_of(all_rest, w_rest_b, l, o[k], o[k + 1]), axis=0)
        layers.append(dict(
            wt=group_weights(w_in_t),
            w_a=rest(0), w_s=rest(1), w_m=rest(2), w_o=rest(3),
            g_pre=norm_pre[l][None], g_post=norm_post[l][None], sinks=att_sinks[l],
            ln_g=sg_ln_g[l][None], ln_b=sg_ln_b[l][None], sg_w=sg_w[l],
            sg_bt=sg_b[l].T,
            conv_w=jnp.concatenate([convw_all[l], jnp.zeros((4, CONV_C), F32)], axis=0),
            conv_b=ssm_conv_b[l][None], dt_bias=_pad_lanes(ssm_dt_bias[l]), a_log=_pad_lanes(ssm_a_log[l]),
            d_skip=_pad_lanes(ssm_d[l]), norm_g=ssm_norm_g[l][None]))

    bias = bias_table(rel_bias)
    saved = []
    act = xs
    for l in range(DEPTH):
        act, sv = layer_fwd(act, layers[l], bias)
        saved.append(sv)
    dy, loss_part = loss_head(act, loss_target[0])
    grads = [None] * DEPTH
    for l in reversed(range(DEPTH)):
        dy, grads[l] = layer_bwd(dy, layers[l], bias, saved[l])
    grad_x = dy[None]
    grad_rel_local = bias_grad(grads[0]["bias"] + grads[1]["bias"])

    cvec = jnp.reshape(cc, (1,)).astype(jnp.int32)
    mvec = jnp.reshape(me, (1,)).astype(jnp.int32)
    g_in, g_rest = [], []
    for l in range(DEPTH):
        g = grads[l]
        g_in.append(jnp.pad(ungroup_grads(g["w_in"]).reshape(SHARDS, 3400, D),
                            ((0, 0), (0, W_IN_ROWS - 3400), (0, 0))))
        gcw = g["conv_w"][0:CONV_K].reshape(CONV_K, SHARDS, 768).transpose(1, 0, 2).reshape(SHARDS, 3, 1024)
        g_rest.append(jnp.concatenate([
            g["w_a"].reshape(SHARDS, 256, D), g["w_s"].reshape(SHARDS, 256, D), g["w_m"].reshape(SHARDS, 512, D),
            g["w_o"].reshape(SHARDS, 256, D), jnp.pad(gcw, ((0, 0), (0, REST_ROWS - REST_OFF[4] - 3), (0, 0)))],
            axis=1))
    g_in, g_rest = jnp.stack(g_in), jnp.stack(g_rest)
    sb_in, sb_rest = grad_sibling_exchange([g_in, g_rest])
    t_in, t_in_b = grad_chip_sum(g_in, sb_in, cvec, 384, "chip_sum_w_in")
    t_rest, t_rest_b = grad_chip_sum(g_rest, sb_rest, cvec, 432, "chip_sum_rest")
    rb_in, rb_rest = grad_chip_exchange([t_in_b, t_rest_b])
    f_in = grad_shard_sum(t_in, rb_in, mvec, 384, "shard_sum_w_in")
    f_rest = grad_shard_sum(t_rest, rb_rest, mvec, 432, "shard_sum_rest")
    fb_in, fb_rest = grad_sibling_share([f_in, f_rest])

    red = small_allreduce(grads, grad_rel_local, loss_part)
    loss = red[LOSS_ROW, 0]

    res = adamw_small(red, (rel_bias, m_rel_bias, v_rel_bias), dict(
        norm_pre=(norm_pre, m_norm_pre, v_norm_pre), norm_post=(norm_post, m_norm_post, v_norm_post),
        att_sinks=(att_sinks, m_att_sinks, v_att_sinks), sg_ln_g=(sg_ln_g, m_sg_ln_g, v_sg_ln_g),
        sg_ln_b=(sg_ln_b, m_sg_ln_b, v_sg_ln_b), sg_w=(sg_w, m_sg_w, v_sg_w), sg_b=(sg_b, m_sg_b, v_sg_b),
        ssm_conv_b=(ssm_conv_b, m_ssm_conv_b, v_ssm_conv_b), ssm_dt_bias=(ssm_dt_bias, m_ssm_dt_bias, v_ssm_dt_bias),
        ssm_a_log=(ssm_a_log, m_ssm_a_log, v_ssm_a_log), ssm_d=(ssm_d, m_ssm_d, v_ssm_d),
        ssm_norm_g=(ssm_norm_g, m_ssm_norm_g, v_ssm_norm_g)))
    res["w_in"] = tuple(tr(a) for a in adamw_big(tr(w_in), tr(m_w_in), tr(v_w_in), f_in, fb_in, cvec, "adamw_w_in", 200))
    res["w_br_att"] = adamw_big(w_br_att, m_w_br_att, v_w_br_att, f_rest, fb_rest, cvec, "adamw_w_br_att", 256, o[0])
    res["w_br_sg"] = adamw_big(w_br_sg, m_w_br_sg, v_w_br_sg, f_rest, fb_rest, cvec, "adamw_w_br_sg", 256, o[1])
    res["w_br_ssm"] = adamw_big(w_br_ssm, m_w_br_ssm, v_w_br_ssm, f_rest, fb_rest, cvec, "adamw_w_br_ssm", 512, o[2])
    res["w_out"] = adamw_big(w_out, m_w_out, v_w_out, f_rest, fb_rest, cvec, "adamw_w_out", 256, o[3])
    cw_mine = f_rest[o[4]:o[4] + 3].reshape(CONV_K, 768)
    cw_sib = fb_rest[o[4]:o[4] + 3].reshape(CONV_K, 768)
    g_conv_w = jnp.stack([jnp.where(cc == l, cw_mine, cw_sib) for l in range(DEPTH)])
    res["ssm_conv_w"] = (g_conv_w,) + tuple(adamw_plain(ssm_conv_w, g_conv_w, m_ssm_conv_w, v_ssm_conv_w, "adamw_conv_w"))

    order = ["w_in", "norm_pre", "norm_post", "rel_bias", "att_sinks", "sg_ln_g", "sg_ln_b", "sg_w", "sg_b",
             "ssm_conv_w", "ssm_conv_b", "ssm_dt_bias", "ssm_a_log", "ssm_d", "ssm_norm_g",
             "w_br_att", "w_br_sg", "w_br_ssm", "w_out"]
    return (loss, grad_x, *[res[n][0] for n in order], *[res[n][1] for n in order],
            *[res[n][2] for n in order], *[res[n][3] for n in order])
```

```python
import functools
import math

import numpy as np
import jax
import jax.numpy as jnp
from jax import lax
from jax.experimental import pallas as pl
from jax.experimental.pallas import tpu as pltpu

F32 = jnp.float32
BF16 = jnp.bfloat16
MESH = pl.DeviceIdType.MESH

D = 1024
DEPTH = 2
EPS = 1e-6
L = 128
HEADS = 16
KV = 2
DH = 64
SSM_W = 2048
SSM_H = 32
SSM_P = 64
SSM_G = 4
SSM_N = 128
CONV_K = 4
CONV_C = 3072
NEG = -1e30
IN_COLS = 13600

GROUPS = (("gate", 3072, 1536), ("sgu", 3072, 1536), ("att", 2304, 2304), ("ssd", 5376, 1792))
W_IN_ROWS = 3456

ADAM_LR = 0.001
ADAM_B1 = 0.9
ADAM_B2 = 0.999
ADAM_EPS = 1e-08
ADAM_WD = 0.01
ADAM_STEP = 10

VMEM_LIMIT = 56 * 1024 * 1024

PACK_ROWS = 4704
PACK_TILE = 224
SHARDS = 4


def _dot(a, b):
    return jnp.dot(a, b, preferred_element_type=F32)


def _dot_nt(a, b):
    return lax.dot_general(a, b, (((1,), (1,)), ((), ())), preferred_element_type=F32)


def _dot_tn(a_f32, b):
    return jnp.dot(a_f32.T.astype(BF16), b, preferred_element_type=F32)


def _dot_hi(a, b):
    return jnp.dot(a, b, preferred_element_type=F32, precision=lax.Precision.HIGHEST)


def _pieces(x, n):
    out = []
    for _ in range(n - 1):
        p = x.astype(BF16)
        out.append(p)
        x = x - p.astype(F32)
    out.append(x.astype(BF16))
    return out


def _dot_sel(a, sel, n):
    sel = sel.astype(BF16)
    acc = None
    for p in _pieces(a, n):
        t = _dot(p, sel)
        acc = t if acc is None else acc + t
    return acc


def _sel_dot(sel, b, n):
    sel = sel.astype(BF16)
    acc = None
    for p in _pieces(b, n):
        t = _dot(sel, p)
        acc = t if acc is None else acc + t
    return acc


def _sigmoid(x):
    return 1.0 / (1.0 + jnp.exp(-x))


def _softplus(x):
    return jnp.maximum(x, 0.0) + jnp.log(1.0 + jnp.exp(-jnp.abs(x)))


def _params(sem=None, vmem=VMEM_LIMIT):
    kw = dict(vmem_limit_bytes=vmem)
    if sem is not None:
        kw["dimension_semantics"] = sem
    return pltpu.CompilerParams(**kw)


def _full(shape):
    nd = len(shape)
    return pl.BlockSpec(shape, lambda *_: (0,) * nd)


def group_weights(wt):
    return dict(
        gate=wt[10528:13600],
        sgu=wt[2304:5376],
        att=jnp.concatenate([wt[0:1024], wt[1280:2304], wt[1024:1280]], axis=0),
        ssd=jnp.concatenate([wt[5376:10496], wt[10496:10528], jnp.zeros((224, D), wt.dtype)], axis=0))


def ungroup_grads(g):
    a, s = g["att"], g["ssd"]
    return jnp.concatenate([a[0:1024], a[2048:2304], a[1024:2048], g["sgu"], s[0:5152], g["gate"]], axis=0)


def _bucket_table():
    qi = np.arange(L)[:, None]
    kj = np.arange(2 * L)[None, :]
    dist = np.maximum(qi + L - kj, 0)
    dist_f = np.maximum(dist, 1).astype(np.float32)
    large = 16 + (np.log(dist_f / np.float32(16)) / np.float32(math.log(128 / 16)) * np.float32(16)).astype(np.int32)
    large = np.minimum(large, 31)
    return np.where(dist < 16, dist, large).astype(np.int32)


def bias_table(rel_bias):
    buckets = jnp.asarray(_bucket_table().reshape(1, L * 2 * L))

    def body(rb_ref, bk_ref, out_ref):
        onehot = (lax.broadcasted_iota(jnp.int32, (32, L * 2 * L), 0) == bk_ref[...]).astype(F32)
        out_ref[...] = lax.dot_general(rb_ref[...], onehot, (((0,), (0,)), ((), ())),
                                       preferred_element_type=F32, precision=lax.Precision.HIGHEST)

    out = pl.pallas_call(
        body, name="bias_table",
        out_shape=jax.ShapeDtypeStruct((HEADS, L * 2 * L), F32),
        compiler_params=_params(),
    )(rel_bias, buckets)
    return out.reshape(HEADS, L, 2 * L)


def bias_grad(dbias):
    buckets = jnp.asarray(_bucket_table().reshape(1, L * 2 * L))

    def body(db_ref, bk_ref, out_ref):
        onehot = (lax.broadcasted_iota(jnp.int32, (32, L * 2 * L), 0) == bk_ref[...]).astype(F32)
        out_ref[...] = lax.dot_general(onehot, db_ref[...], (((1,), (1,)), ((), ())),
                                       preferred_element_type=F32, precision=lax.Precision.HIGHEST)

    return pl.pallas_call(
        body, name="bias_grad",
        out_shape=jax.ShapeDtypeStruct((32, HEADS), F32),
        compiler_params=_params(),
    )(dbias.reshape(HEADS, L * 2 * L), buckets)


def _row_tile(S):
    return 1024 if S % 1024 == 0 else 512


def inproj_first(x, g_pre, wt, tn, name):
    S, W = x.shape[0], wt.shape[0]
    tm = _row_tile(S)

    def body(x_ref, g_ref, w_ref, o_ref, h_ref):
        @pl.when(pl.program_id(1) == 0)
        def _():
            xv = x_ref[...]
            r = lax.rsqrt(jnp.mean(xv * xv, axis=-1, keepdims=True) + EPS)
            h_ref[...] = (xv * r * g_ref[...]).astype(BF16)
        o_ref[...] = _dot_nt(h_ref[...], w_ref[...])

    return pl.pallas_call(
        body, name=name, grid=(S // tm, W // tn),
        in_specs=[pl.BlockSpec((tm, D), lambda i, j: (i, 0)), _full((1, D)),
                  pl.BlockSpec((tn, D), lambda i, j: (j, 0))],
        out_specs=[pl.BlockSpec((tm, tn), lambda i, j: (i, j)), pl.BlockSpec((tm, D), lambda i, j: (i, 0))],
        out_shape=[jax.ShapeDtypeStruct((S, W), F32), jax.ShapeDtypeStruct((S, D), BF16)],
        compiler_params=_params(("arbitrary", "arbitrary")),
    )(x, g_pre, wt)


def inproj_group(h, wt, tn, name):
    S, W = h.shape[0], wt.shape[0]
    tm = _row_tile(S)

    def body(h_ref, w_ref, o_ref):
        o_ref[...] = _dot_nt(h_ref[...], w_ref[...])

    return pl.pallas_call(
        body, name=name, grid=(S // tm, W // tn),
        in_specs=[pl.BlockSpec((tm, D), lambda i, j: (i, 0)), pl.BlockSpec((tn, D), lambda i, j: (j, 0))],
        out_specs=pl.BlockSpec((tm, tn), lambda i, j: (i, j)),
        out_shape=jax.ShapeDtypeStruct((S, W), F32),
        compiler_params=_params(("arbitrary", "arbitrary")),
    )(h, wt)


def dh_group(dp, wt, acc, tk, name):
    S, W = dp.shape
    tm = _row_tile(S)

    def body(*refs):
        dp_ref, w_ref, o_ref = refs[0], refs[1], refs[-1]
        first = pl.program_id(1) == 0
        if acc is None:
            @pl.when(first)
            def _():
                o_ref[...] = jnp.zeros_like(o_ref)
        else:
            @pl.when(first)
            def _():
                o_ref[...] = refs[2][...]
        o_ref[...] += _dot(dp_ref[...], w_ref[...])

    row = pl.BlockSpec((tm, D), lambda i, k: (i, 0))
    return pl.pallas_call(
        body, name=name, grid=(S // tm, W // tk),
        in_specs=[pl.BlockSpec((tm, tk), lambda i, k: (i, k)), pl.BlockSpec((tk, D), lambda i, k: (k, 0))]
        + ([] if acc is None else [row]),
        out_specs=row, out_shape=jax.ShapeDtypeStruct((S, D), F32),
        input_output_aliases={} if acc is None else {2: 0},
        compiler_params=_params(("arbitrary", "arbitrary")),
    )(*((dp, wt) if acc is None else (dp, wt, acc)))


def dh_last(dp, wt, acc_in, x, g_pre, dy, tk, name):
    S, W = dp.shape
    tm = 512
    nk = W // tk

    def body(dp_ref, w_ref, a_ref, x_ref, g_ref, dy_ref, dx_ref, dg_ref, acc):
        i, k = pl.program_id(0), pl.program_id(1)

        @pl.when(k == 0)
        def _():
            acc[...] = a_ref[...]

        acc[...] += _dot(dp_ref[...], w_ref[...])

        @pl.when((k == nk - 1) & (i == 0))
        def _():
            dg_ref[...] = jnp.zeros_like(dg_ref)

        @pl.when(k == nk - 1)
        def _():
            xv = x_ref[...]
            dh = acc[...]
            g = g_ref[...]
            r = lax.rsqrt(jnp.mean(xv * xv, axis=-1, keepdims=True) + EPS)
            dhg = dh * g
            dx_ref[...] = dy_ref[...] + r * dhg - xv * (r * r * r) * jnp.mean(dhg * xv, axis=-1, keepdims=True)
            dg_ref[...] += jnp.sum(dh * xv * r, axis=0, keepdims=True)

    row = pl.BlockSpec((tm, D), lambda i, k: (i, 0))
    return pl.pallas_call(
        body, name=name, grid=(S // tm, nk),
        in_specs=[pl.BlockSpec((tm, tk), lambda i, k: (i, k)), pl.BlockSpec((tk, D), lambda i, k: (k, 0)),
                  row, row, _full((1, D)), row],
        out_specs=[row, _full((1, D))],
        out_shape=[jax.ShapeDtypeStruct((S, D), F32), jax.ShapeDtypeStruct((1, D), F32)],
        scratch_shapes=[pltpu.VMEM((tm, D), F32)],
        compiler_params=_params(("arbitrary", "arbitrary")),
    )(dp, wt, acc_in, x, g_pre, dy)


def dw_group(dp, h, tn, name, ts=512):
    S, W = dp.shape

    def body(dp_ref, h_ref, o_ref):
        @pl.when(pl.program_id(1) == 0)
        def _():
            o_ref[...] = jnp.zeros_like(o_ref)
        o_ref[...] += _dot_tn(dp_ref[...].astype(F32), h_ref[...])

    return pl.pallas_call(
        body, name=name, grid=(W // tn, S // ts),
        in_specs=[pl.BlockSpec((ts, tn), lambda j, s: (s, j)), pl.BlockSpec((ts, D), lambda j, s: (s, 0))],
        out_specs=pl.BlockSpec((tn, D), lambda j, s: (j, 0)),
        out_shape=jax.ShapeDtypeStruct((W, D), F32),
        compiler_params=_params(("arbitrary", "arbitrary")),
    )(dp, h)


def matmul_tn(a, b, name, tn=512, ts=512):
    S, K = a.shape
    N = b.shape[1]
    ns = S // ts

    def body(a_ref, b_ref, o_ref):
        @pl.when(pl.program_id(1) == 0)
        def _():
            o_ref[...] = jnp.zeros_like(o_ref)
        o_ref[...] += _dot_tn(a_ref[...].astype(F32), b_ref[...])

    return pl.pallas_call(
        body, name=name, grid=(N // tn, ns),
        in_specs=[pl.BlockSpec((ts, K), lambda j, s: (s, 0)), pl.BlockSpec((ts, tn), lambda j, s: (s, j))],
        out_specs=pl.BlockSpec((K, tn), lambda j, s: (0, j)),
        out_shape=jax.ShapeDtypeStruct((K, N), F32),
        compiler_params=_params(("arbitrary", "arbitrary")),
    )(a, b)


def _att_mask(n):
    qi = lax.broadcasted_iota(jnp.int32, (L, 2 * L), 0)
    kj = lax.broadcasted_iota(jnp.int32, (L, 2 * L), 1)
    dist = qi + L - kj
    return (dist >= 0) & (dist < L) & ((kj >= L) | (n > 0))


def _att_in_specs(nb):
    last = nb - 1
    cur = lambda n: jnp.minimum(n, last)
    prev = lambda n: jnp.maximum(jnp.minimum(n, last) - 1, 0)
    return [
        pl.BlockSpec((L, 1024), lambda n: (cur(n), 0)),
        pl.BlockSpec((L, 128), lambda n: (prev(n), 16)),
        pl.BlockSpec((L, 128), lambda n: (cur(n), 16)),
        pl.BlockSpec((L, 128), lambda n: (prev(n), 17)),
        pl.BlockSpec((L, 128), lambda n: (cur(n), 17)),
        pl.BlockSpec((L, 1024), lambda n: (cur(n), 1)),
        _full((HEADS, L, 2 * L)),
        pl.BlockSpec(memory_space=pltpu.SMEM),
    ]


def _att_probs(qh, kk, bias_h, mask, sk):
    logits = _dot_nt(qh, kk) + bias_h
    logits = jnp.where(mask, logits, NEG)
    m = jnp.maximum(jnp.max(logits, axis=-1, keepdims=True), sk)
    p = jnp.exp(logits - m)
    es = jnp.exp(sk - m)
    den = jnp.sum(p, axis=-1, keepdims=True) + es
    return p / den, es / den


def att_fwd(proj, bias, sinks):
    S = proj.shape[0]
    nb = S // L

    def body(q_ref, kp_ref, kc_ref, vp_ref, vc_ref, z_ref, bias_ref, s_ref, y_ref, o_scr):
        mask = _att_mask(pl.program_id(0))
        for kv in range(KV):
            sl = slice(kv * DH, (kv + 1) * DH)
            kk = jnp.concatenate([kp_ref[:, sl], kc_ref[:, sl]], axis=0).astype(BF16)
            vv = jnp.concatenate([vp_ref[:, sl], vc_ref[:, sl]], axis=0).astype(BF16)
            for g in range(HEADS // KV):
                h = kv * (HEADS // KV) + g
                hs = slice(h * DH, (h + 1) * DH)
                qh = (q_ref[:, hs] * 0.125).astype(BF16)
                P, _ = _att_probs(qh, kk, bias_ref[h], mask, s_ref[h])
                o_scr[:, hs] = _dot(P.astype(BF16), vv)
        z = z_ref[...]
        y_ref[...] = (o_scr[...] * (z * _sigmoid(z))).astype(BF16)

    return pl.pallas_call(
        body, name="att_fwd", grid=(nb,),
        in_specs=_att_in_specs(nb),
        out_specs=pl.BlockSpec((L, 1024), lambda n: (n, 0)),
        out_shape=jax.ShapeDtypeStruct((S, 1024), BF16),
        scratch_shapes=[pltpu.VMEM((L, 1024), F32)],
        compiler_params=_params(("arbitrary",)),
    )(proj, proj, proj, proj, proj, proj, bias, sinks)


def att_bwd(dy, proj, bias, sinks):
    S = proj.shape[0]
    nb = S // L
    last = nb - 1

    def body(dy_ref, q_ref, kp_ref, kc_ref, vp_ref, vc_ref, z_ref, bias_ref, s_ref,
             dout_ref, dbias_ref, dsink_ref, carry, band, dq_scr, dz_scr):
        n = pl.program_id(0)

        @pl.when(n == 0)
        def _():
            carry[...] = jnp.zeros_like(carry)
            dq_scr[...] = jnp.zeros_like(dq_scr)
            dz_scr[...] = jnp.zeros_like(dz_scr)
            dbias_ref[...] = jnp.zeros_like(dbias_ref)
            dsink_ref[...] = jnp.zeros_like(dsink_ref)

        dout_ref[:, 0:1024] = dq_scr[...].astype(BF16)
        dout_ref[:, 1024:2048] = dz_scr[...].astype(BF16)
        band[...] = jnp.zeros_like(band)

        @pl.when(n < nb)
        def _():
            mask = _att_mask(n)
            lane = lax.broadcasted_iota(jnp.int32, (1, 128), 1)
            dsink = jnp.zeros((1, 128), F32)
            for kv in range(KV):
                sl = slice(kv * DH, (kv + 1) * DH)
                kk = jnp.concatenate([kp_ref[:, sl], kc_ref[:, sl]], axis=0).astype(BF16)
                vv = jnp.concatenate([vp_ref[:, sl], vc_ref[:, sl]], axis=0).astype(BF16)
                dk_acc = jnp.zeros((2 * L, DH), F32)
                dv_acc = jnp.zeros((2 * L, DH), F32)
                for g in range(HEADS // KV):
                    h = kv * (HEADS // KV) + g
                    hs = slice(h * DH, (h + 1) * DH)
                    qh = (q_ref[:, hs] * 0.125).astype(BF16)
                    P, psink = _att_probs(qh, kk, bias_ref[h], mask, s_ref[h])
                    Pb = P.astype(BF16)
                    zh = z_ref[:, hs]
                    sg = _sigmoid(zh)
                    dyh = dy_ref[:, hs]
                    O = _dot(Pb, vv)
                    dO = dyh * (zh * sg)
                    dz_scr[:, hs] = dyh * O * (sg * (1.0 + zh * (1.0 - sg)))
                    dOb = dO.astype(BF16)
                    dv_acc = dv_acc + _dot_tn(P, dOb)
                    dP = _dot_nt(dOb, vv)
                    delta = jnp.sum(P * dP, axis=-1, keepdims=True)
                    dS = P * (dP - delta)
                    dsink = dsink + jnp.where(lane == h, -jnp.sum(psink * delta), 0.0)
                    dSb = dS.astype(BF16)
                    dq_scr[:, hs] = _dot(dSb, kk) * 0.125
                    dk_acc = dk_acc + _dot_tn(dS, qh)
                    dbias_ref[h] += dS
                band[:, sl] = dk_acc
                band[:, 128 + kv * DH:128 + (kv + 1) * DH] = dv_acc
            dsink_ref[...] += dsink

        out = carry[...] + band[0:L, :]
        dout_ref[:, 2048:2304] = out.astype(BF16)
        carry[...] = band[L:2 * L, :]

    cur = lambda n: jnp.minimum(n, last)
    lag = lambda n: jnp.maximum(n - 1, 0)
    return pl.pallas_call(
        body, name="att_bwd", grid=(nb + 1,),
        in_specs=[pl.BlockSpec((L, 1024), lambda n: (cur(n), 0))] + _att_in_specs(nb),
        out_specs=[pl.BlockSpec((L, 2304), lambda n: (lag(n), 0)), _full((HEADS, L, 2 * L)), _full((1, 128))],
        out_shape=[jax.ShapeDtypeStruct((S, 2304), BF16),
                   jax.ShapeDtypeStruct((HEADS, L, 2 * L), F32), jax.ShapeDtypeStruct((1, 128), F32)],
        scratch_shapes=[pltpu.VMEM((L, 256), F32), pltpu.VMEM((2 * L, 256), F32),
                        pltpu.VMEM((L, 1024), F32), pltpu.VMEM((L, 1024), F32)],
        compiler_params=_params(("arbitrary",)),
    )(dy, proj, proj, proj, proj, proj, proj, bias, sinks)


def _sgu_in_specs():
    return [
        pl.BlockSpec((L, 1024), lambda c: (c, 0)),
        pl.BlockSpec((L, 1024), lambda c: (c, 1)),
        pl.BlockSpec((L, 1024), lambda c: (c, 2)),
        _full((1, 1024)), _full((1, 1024)), _full((8, L, L)), _full((L, 8)),
    ]


def _sgu_norm(v, lg, lb):
    mu = jnp.mean(v, axis=-1, keepdims=True)
    vc = v - mu
    rstd = lax.rsqrt(jnp.mean(vc * vc, axis=-1, keepdims=True) + EPS)
    xhat = vc * rstd
    return xhat * lg + lb, xhat, rstd


def _tril():
    return lax.broadcasted_iota(jnp.int32, (L, L), 0) >= lax.broadcasted_iota(jnp.int32, (L, L), 1)


def sgu_fwd(proj, ln_g, ln_b, w, b_t):
    S = proj.shape[0]

    def body(u_ref, v_ref, z_ref, lg_ref, lb_ref, w_ref, bt_ref, y_ref):
        vn, _, _ = _sgu_norm(v_ref[...], lg_ref[...], lb_ref[...])
        tri = _tril()
        parts = []
        for g in range(8):
            wg = jnp.where(tri, w_ref[g], 0.0).astype(BF16)
            parts.append(_dot(wg, vn[:, g * 128:(g + 1) * 128].astype(BF16)) + bt_ref[:, g:g + 1])
        mixed = jnp.concatenate(parts, axis=1)
        z = z_ref[...]
        y_ref[...] = (u_ref[...] * mixed * (z * _sigmoid(z))).astype(BF16)

    return pl.pallas_call(
        body, name="sgu_fwd", grid=(S // L,),
        in_specs=_sgu_in_specs(),
        out_specs=pl.BlockSpec((L, 1024), lambda c: (c, 0)),
        out_shape=jax.ShapeDtypeStruct((S, 1024), BF16),
        compiler_params=_params(("arbitrary",)),
    )(proj, proj, proj, ln_g, ln_b, w, b_t)


def sgu_bwd(dy, proj, ln_g, ln_b, w, b_t):
    S = proj.shape[0]

    def body(dy_ref, u_ref, v_ref, z_ref, lg_ref, lb_ref, w_ref, bt_ref,
             dout_ref, dw_ref, dbt_ref, dlg_ref, dlb_ref):
        @pl.when(pl.program_id(0) == 0)
        def _():
            dw_ref[...] = jnp.zeros_like(dw_ref)
            dbt_ref[...] = jnp.zeros_like(dbt_ref)
            dlg_ref[...] = jnp.zeros_like(dlg_ref)
            dlb_ref[...] = jnp.zeros_like(dlb_ref)

        lg = lg_ref[...]
        vn, xhat, rstd = _sgu_norm(v_ref[...], lg, lb_ref[...])
        tri = _tril()
        lane = lax.broadcasted_iota(jnp.int32, (L, 128), 1)
        wgs, parts = [], []
        for g in range(8):
            wg = jnp.where(tri, w_ref[g], 0.0)
            wgs.append(wg)
            parts.append(_dot(wg.astype(BF16), vn[:, g * 128:(g + 1) * 128].astype(BF16)) + bt_ref[:, g:g + 1])
        mixed = jnp.concatenate(parts, axis=1)
        z = z_ref[...]
        sg = _sigmoid(z)
        silu = z * sg
        dy_v = dy_ref[...]
        u = u_ref[...]
        dout_ref[:, 0:1024] = (dy_v * mixed * silu).astype(BF16)
        dout_ref[:, 2048:3072] = (dy_v * u * mixed * (sg * (1.0 + z * (1.0 - sg)))).astype(BF16)
        dmixed = dy_v * u * silu
        dbt = jnp.zeros((L, 128), F32)
        dvn_parts = []
        for g in range(8):
            dm = dmixed[:, g * 128:(g + 1) * 128]
            dmb = dm.astype(BF16)
            dbt = dbt + jnp.where(lane == g, jnp.sum(dm, axis=1, keepdims=True), 0.0)
            dw_ref[g] += jnp.where(tri, _dot_nt(dmb, vn[:, g * 128:(g + 1) * 128].astype(BF16)), 0.0)
            dvn_parts.append(_dot_tn(wgs[g], dmb))
        dbt_ref[...] += dbt
        dvn = jnp.concatenate(dvn_parts, axis=1)
        dlg_ref[...] += jnp.sum(dvn * xhat, axis=0, keepdims=True)
        dlb_ref[...] += jnp.sum(dvn, axis=0, keepdims=True)
        dxh = dvn * lg
        dv = rstd * (dxh - jnp.mean(dxh, axis=-1, keepdims=True)
                     - xhat * jnp.mean(dxh * xhat, axis=-1, keepdims=True))
        dout_ref[:, 1024:2048] = dv.astype(BF16)

    return pl.pallas_call(
        body, name="sgu_bwd", grid=(S // L,),
        in_specs=[pl.BlockSpec((L, 1024), lambda c: (c, 0))] + _sgu_in_specs(),
        out_specs=[pl.BlockSpec((L, 3072), lambda c: (c, 0)), _full((8, L, L)), _full((L, 128)),
                   _full((1, 1024)), _full((1, 1024))],
        out_shape=[jax.ShapeDtypeStruct((S, 3072), BF16), jax.ShapeDtypeStruct((8, L, L), F32),
                   jax.ShapeDtypeStruct((L, 128), F32), jax.ShapeDtypeStruct((1, 1024), F32),
                   jax.ShapeDtypeStruct((1, 1024), F32)],
        compiler_params=_params(("arbitrary",)),
    )(dy, proj, proj, proj, ln_g, ln_b, w, b_t)


def _expand_matrix():
    r = lax.broadcasted_iota(jnp.int32, (128, SSM_W), 0)
    c = lax.broadcasted_iota(jnp.int32, (128, SSM_W), 1)
    return (c // SSM_P) == r


def _expand_matrix_t():
    r = lax.broadcasted_iota(jnp.int32, (SSM_W, 128), 0)
    c = lax.broadcasted_iota(jnp.int32, (SSM_W, 128), 1)
    return (r // SSM_P) == c


def _rows_from(ref, start):
    C = ref.shape[1]
    tiles = ref[...].reshape(17, 8, C)
    q, s = divmod(start, 8)
    if s == 0:
        return tiles[q:q + 16].reshape(L, C)
    rolled = pltpu.roll(tiles, 8 - s, axis=1)
    sub = lax.broadcasted_iota(jnp.int32, (16, 8, C), 1)
    return jnp.where(sub < 8 - s, rolled[q:q + 16], rolled[q + 1:q + 17]).reshape(L, C)


def _ssd_common(ext_ref, cw_ref, cb_ref, dt_raw, dtb, alog):
    taps = [_rows_from(ext_ref, 5 + k) for k in range(CONV_K)]
    pre = cb_ref[...]
    for k in range(CONV_K):
        pre = pre + cw_ref[k:k + 1, :] * taps[k]
    sg_pre = _sigmoid(pre)
    xc = pre * sg_pre
    dt = _softplus(dt_raw + dtb)
    a = -jnp.exp(alog)
    adt = dt * a
    acs = _sel_dot(_tril(), adt, 3)
    return pre, sg_pre, xc, dt, a, acs, taps


def _ssd_in_specs(rev, nc):
    cidx = (lambda c: nc - 1 - c) if rev else (lambda c: c)
    return [
        pl.BlockSpec((L, 2048), lambda c: (cidx(c), 0)),
        pl.BlockSpec((L, 1024), lambda c: (cidx(c), 2)),
        pl.BlockSpec((L, 1024), lambda c: (cidx(c), 3)),
        pl.BlockSpec((L, 1024), lambda c: (cidx(c), 4)),
        pl.BlockSpec((L, 128), lambda c: (cidx(c), 40)),
        _full((8, CONV_C)), _full((1, CONV_C)), _full((1, 128)), _full((1, 128)), _full((1, 128)),
        _full((1, SSM_W)),
    ]


def ssd_fwd(proj, conv_w, conv_b, dt_bias, a_log, d_skip, norm_g):
    S = proj.shape[0]
    nc = S // L

    def body(z_ref, xa_ref, xb_ref, xc_ref, dt_ref, cw_ref, cb_ref, dtb_ref, alog_ref, dsk_ref, ng_ref,
             y_ref, hs_ref, H, ext, ysc):
        @pl.when(pl.program_id(0) == 0)
        def _():
            H[...] = jnp.zeros_like(H)
            ext[0:8, :] = jnp.zeros((8, CONV_C), F32)

        for k, ref in enumerate((xa_ref, xb_ref, xc_ref)):
            ext[8:8 + L, k * 1024:(k + 1) * 1024] = ref[...]
        pre, sg_pre, xc, dt, a, acs, _ = _ssd_common(ext, cw_ref, cb_ref, dt_ref[...], dtb_ref[...], alog_ref[...])
        for k, ref in enumerate((xa_ref, xb_ref, xc_ref)):
            ext[0:8, k * 1024:(k + 1) * 1024] = ref[L - 8:L, :]
        xs = xc[:, 0:SSM_W]
        acs_t = acs.T
        ex = _expand_matrix()
        dt_x = _dot_sel(dt, ex, 2)
        xdt = xs * dt_x
        eacs_x = _dot_sel(jnp.exp(acs), ex, 2)
        xw = xdt * _dot_sel(jnp.exp(acs[L - 1:L, :] - acs), ex, 2)
        cd_row = jnp.exp(acs[L - 1:L, :])
        hs_ref[0] = H[...]
        tri = _tril()
        for g in range(SSM_G):
            gs = slice(g * 512, (g + 1) * 512)
            bg = xc[:, SSM_W + g * SSM_N:SSM_W + (g + 1) * SSM_N].astype(BF16)
            cg = xc[:, SSM_W + 512 + g * SSM_N:SSM_W + 512 + (g + 1) * SSM_N].astype(BF16)
            G = _dot_nt(cg, bg)
            yoff = _dot_nt(cg, H[gs, :].astype(BF16)) * eacs_x[:, gs]
            Sg = _dot_tn(xw[:, gs], bg)
            for j in range(8):
                hh = g * 8 + j
                hs = slice(hh * SSM_P, (hh + 1) * SSM_P)
                seg = acs[:, hh:hh + 1] - acs_t[hh:hh + 1, :]
                dk = jnp.where(tri, jnp.exp(jnp.minimum(seg, 0.0)), 0.0)
                yd = _dot((G * dk).astype(BF16), xdt[:, hs].astype(BF16))
                ysc[:, hs] = yd + yoff[:, j * SSM_P:(j + 1) * SSM_P]
                H[hs, :] = H[hs, :] * cd_row[:, hh:hh + 1] + Sg[j * SSM_P:(j + 1) * SSM_P, :]
        d_x = _dot_sel(jnp.broadcast_to(dsk_ref[...], (8, 128)), ex, 3)[0:1, :]
        Y = ysc[...] + d_x * xs
        z = z_ref[...]
        yz = Y * (z * _sigmoid(z))
        ng = ng_ref[...]
        for g in range(SSM_G):
            gs = slice(g * 512, (g + 1) * 512)
            t = yz[:, gs]
            rstd = lax.rsqrt(jnp.mean(t * t, axis=-1, keepdims=True) + EPS)
            y_ref[:, gs] = (t * rstd * ng[:, gs]).astype(BF16)

    return pl.pallas_call(
        body, name="ssd_fwd", grid=(nc,),
        in_specs=_ssd_in_specs(False, nc),
        out_specs=[pl.BlockSpec((L, SSM_W), lambda c: (c, 0)), pl.BlockSpec((1, SSM_W, SSM_N), lambda c: (c, 0, 0))],
        out_shape=[jax.ShapeDtypeStruct((S, SSM_W), BF16), jax.ShapeDtypeStruct((nc, SSM_W, SSM_N), F32)],
        scratch_shapes=[pltpu.VMEM((SSM_W, SSM_N), F32), pltpu.VMEM((8 + L, CONV_C), F32),
                        pltpu.VMEM((L, SSM_W), F32)],
        compiler_params=_params(("arbitrary",)),
    )(proj, proj, proj, proj, proj, conv_w, conv_b, dt_bias, a_log, d_skip, norm_g)


def ssd_bwd(dy, proj, hstates, conv_w, conv_b, dt_bias, a_log, d_skip, norm_g):
    S = proj.shape[0]
    nc = S // L
    cidx = lambda c: nc - 1 - c

    def body(dy_ref, z_ref, xa_ref, xb_ref, xc_ref, dt_ref, cw_ref, cb_ref, dtb_ref, alog_ref, dsk_ref, ng_ref,
             pa_ref, pb_ref, pc_ref, hp_ref,
             dout_ref, dcw_ref, dcb_ref, ddtb_ref, dalog_ref, ddsk_ref, dng_ref,
             dH, ext, dext, ysc, yoffsc, dxdt, dxc, tsc):
        step = pl.program_id(0)
        c = nc - 1 - step

        @pl.when(step == 0)
        def _():
            dH[...] = jnp.zeros_like(dH)
            dext[L:L + 8, :] = jnp.zeros((8, CONV_C), F32)
            for r in (dcw_ref, dcb_ref, ddtb_ref, dalog_ref, ddsk_ref, dng_ref):
                r[...] = jnp.zeros_like(r)

        for k, (ref, prev) in enumerate(((xa_ref, pa_ref), (xb_ref, pb_ref), (xc_ref, pc_ref))):
            ext[0:8, k * 1024:(k + 1) * 1024] = jnp.where(c > 0, prev[...], 0.0)
            ext[8:8 + L, k * 1024:(k + 1) * 1024] = ref[...]
        dtb = dtb_ref[...]
        dt_raw = dt_ref[...]
        pre, sg_pre, xc, dt, a, acs, taps = _ssd_common(ext, cw_ref, cb_ref, dt_raw, dtb, alog_ref[...])
        xs = xc[:, 0:SSM_W]
        acs_t = acs.T
        ex = _expand_matrix()
        dt_x = _dot_sel(dt, ex, 2)
        xdt = xs * dt_x
        eacs_x = _dot_sel(jnp.exp(acs), ex, 2)
        dte_x = _dot_sel(jnp.exp(acs[L - 1:L, :] - acs), ex, 2)
        xw = xdt * dte_x
        cd_row = jnp.exp(acs[L - 1:L, :])
        tri = _tril()

        Gs, Cs, Bs = [], [], []
        for g in range(SSM_G):
            gs = slice(g * 512, (g + 1) * 512)
            bg = xc[:, SSM_W + g * SSM_N:SSM_W + (g + 1) * SSM_N].astype(BF16)
            cg = xc[:, SSM_W + 512 + g * SSM_N:SSM_W + 512 + (g + 1) * SSM_N].astype(BF16)
            G = _dot_nt(cg, bg)
            Gs.append(G), Cs.append(cg), Bs.append(bg)
            yoffsc[:, gs] = _dot_nt(cg, hp_ref[0, gs, :].astype(BF16)) * eacs_x[:, gs]
            for j in range(8):
                hh = g * 8 + j
                hs = slice(hh * SSM_P, (hh + 1) * SSM_P)
                seg = acs[:, hh:hh + 1] - acs_t[hh:hh + 1, :]
                dk = jnp.where(tri, jnp.exp(jnp.minimum(seg, 0.0)), 0.0)
                ysc[:, hs] = _dot((G * dk).astype(BF16), xdt[:, hs].astype(BF16))
        d_x = _dot_sel(jnp.broadcast_to(dsk_ref[...], (8, 128)), ex, 3)[0:1, :]
        yoff = yoffsc[...]
        Y = ysc[...] + yoff + d_x * xs

        z = z_ref[...]
        sgz = _sigmoid(z)
        silu_z = z * sgz
        yz = Y * silu_z
        ng = ng_ref[...]
        dout = dy_ref[...]
        dyn = dout * ng
        dyz_parts, dng_parts = [], []
        for g in range(SSM_G):
            gs = slice(g * 512, (g + 1) * 512)
            t = yz[:, gs]
            rstd = lax.rsqrt(jnp.mean(t * t, axis=-1, keepdims=True) + EPS)
            dng_parts.append(jnp.sum(dout[:, gs] * t * rstd, axis=0, keepdims=True))
            dn = dyn[:, gs]
            dyz_parts.append(rstd * dn - t * (rstd * rstd * rstd) * jnp.mean(dn * t, axis=-1, keepdims=True))
        dng_ref[...] += jnp.concatenate(dng_parts, axis=1)
        dyz = jnp.concatenate(dyz_parts, axis=1)
        dY = dyz * silu_z
        dout_ref[:, 0:SSM_W] = (dyz * Y * (sgz * (1.0 + z * (1.0 - sgz)))).astype(BF16)

        ex_t = _expand_matrix_t()
        ddsk_ref[...] += _dot_sel(jnp.broadcast_to(jnp.sum(dY * xs, axis=0, keepdims=True), (8, SSM_W)), ex_t, 3)[0:1, :]

        lane = lax.broadcasted_iota(jnp.int32, (L, 128), 1)
        subl = lax.broadcasted_iota(jnp.int32, (128, L), 0)
        coll = lax.broadcasted_iota(jnp.int32, (128, L), 1)
        r_cols = jnp.zeros((L, 128), F32)
        c_rows = jnp.zeros((128, L), F32)
        for g in range(SSM_G):
            gs = slice(g * 512, (g + 1) * 512)
            G, cg, bg = Gs[g], Cs[g], Bs[g]
            hp_g = hp_ref[0, gs, :]
            dh_g = dH[gs, :]
            dY_g = dY[:, gs]
            dZ = dY_g * eacs_x[:, gs]
            dZb = dZ.astype(BF16)
            dC = _dot(dZb, hp_g.astype(BF16))
            dh_from_off = _dot_tn(dZ, cg)
            dhb = dh_g.astype(BF16)
            Q = _dot_nt(bg, dhb)
            dB = _dot(xw[:, gs].astype(BF16), dhb)
            qd = Q * dte_x[:, gs]
            dxdt[:, gs] = qd
            tsc[:, gs] = qd * xdt[:, gs]
            dG = jnp.zeros((L, L), F32)
            for j in range(8):
                hh = g * 8 + j
                hs = slice(hh * SSM_P, (hh + 1) * SSM_P)
                seg = acs[:, hh:hh + 1] - acs_t[hh:hh + 1, :]
                dk = jnp.where(tri, jnp.exp(jnp.minimum(seg, 0.0)), 0.0)
                M = G * dk
                dYh = dY[:, hs]
                dYhb = dYh.astype(BF16)
                dM = _dot_nt(dYhb, xdt[:, hs].astype(BF16))
                dxdt[:, hs] += _dot_tn(M, dYhb)
                dG = dG + dM * dk
                Wm = dM * M
                r_cols = r_cols + jnp.where(lane == hh, jnp.sum(Wm, axis=1, keepdims=True), 0.0)
                c_rows = c_rows + jnp.where(subl == hh, jnp.sum(Wm, axis=0, keepdims=True), 0.0)
                pj = slice(j * SSM_P, (j + 1) * SSM_P)
                cd_h = cd_row[:, hh:hh + 1]
                dcd = jnp.sum(dh_g[pj, :] * hp_g[pj, :]) * cd_h
                c_rows = c_rows - jnp.where((subl == hh) & (coll == L - 1), dcd, 0.0)
                dH[hs, :] = dh_g[pj, :] * cd_h + dh_from_off[pj, :]
            dGb = dG.astype(BF16)
            dC = dC + _dot(dGb, bg)
            dB = dB + _dot_tn(dG, cg)
            dxc[:, SSM_W + g * SSM_N:SSM_W + (g + 1) * SSM_N] = dB
            dxc[:, SSM_W + 512 + g * SSM_N:SSM_W + 512 + (g + 1) * SSM_N] = dC

        row = lax.broadcasted_iota(jnp.int32, (L, 128), 0)
        tv = tsc[...]
        t_last = _dot_sel(jnp.broadcast_to(jnp.sum(tv, axis=0, keepdims=True), (8, SSM_W)), ex_t, 3)[0:1, :]
        dacs = (r_cols - c_rows.T + _dot_sel(dY * yoff - tv, ex_t, 2) + jnp.where(row == L - 1, t_last, 0.0))
        triu = lax.broadcasted_iota(jnp.int32, (L, L), 0) <= lax.broadcasted_iota(jnp.int32, (L, L), 1)
        dadt = _sel_dot(triu, dacs, 3)
        dxdt_v = dxdt[...]
        ddt = _dot_sel(dxdt_v * xs, ex_t, 2) + dadt * a
        dalog_ref[...] += jnp.sum(dadt * dt * a, axis=0, keepdims=True)
        ddt_raw = jnp.where(lane < SSM_H, ddt * _sigmoid(dt_raw + dtb), 0.0)
        ddtb_ref[...] += jnp.sum(ddt_raw, axis=0, keepdims=True)
        dout_ref[:, 5120:5248] = ddt_raw.astype(BF16)
        dout_ref[:, 5248:5376] = jnp.zeros((L, 128), BF16)

        dxc[:, 0:SSM_W] = dxdt_v * dt_x + d_x * dY
        dpre = dxc[...] * (sg_pre * (1.0 + pre * (1.0 - sg_pre)))
        dcb_ref[...] += jnp.sum(dpre, axis=0, keepdims=True)
        dext[0:L, :] = dpre
        x_cur = ext[8:8 + L, :]
        dx = None
        for k in range(CONV_K):
            dsh = _rows_from(dext, 3 - k)
            term = cw_ref[k:k + 1, :] * dsh
            dx = term if dx is None else dx + term
            dcw_ref[k:k + 1, :] += jnp.sum(dsh * x_cur, axis=0, keepdims=True)
        dout_ref[:, SSM_W:SSM_W + CONV_C] = dx.astype(BF16)
        dext[L:L + 8, :] = dpre[0:8, :]

    big = lambda w: pl.BlockSpec((L, w), lambda c: (cidx(c), 0))
    return pl.pallas_call(
        body, name="ssd_bwd", grid=(nc,),
        in_specs=[big(SSM_W)] + _ssd_in_specs(True, nc) + [
            pl.BlockSpec((8, 1024), lambda c, k=k: (jnp.maximum(16 * cidx(c) - 1, 0), k)) for k in (2, 3, 4)] + [
            pl.BlockSpec((1, SSM_W, SSM_N), lambda c: (cidx(c), 0, 0))],
        out_specs=[big(5376), _full((8, CONV_C)), _full((1, CONV_C)),
                   _full((1, 128)), _full((1, 128)), _full((1, 128)), _full((1, SSM_W))],
        out_shape=[jax.ShapeDtypeStruct((S, 5376), BF16), jax.ShapeDtypeStruct((8, CONV_C), F32),
                   jax.ShapeDtypeStruct((1, CONV_C), F32), jax.ShapeDtypeStruct((1, 128), F32),
                   jax.ShapeDtypeStruct((1, 128), F32), jax.ShapeDtypeStruct((1, 128), F32),
                   jax.ShapeDtypeStruct((1, SSM_W), F32)],
        scratch_shapes=[pltpu.VMEM((SSM_W, SSM_N), F32), pltpu.VMEM((8 + L, CONV_C), F32),
                        pltpu.VMEM((L + 8, CONV_C), F32), pltpu.VMEM((L, SSM_W), F32),
                        pltpu.VMEM((L, SSM_W), F32), pltpu.VMEM((L, SSM_W), F32),
                        pltpu.VMEM((L, CONV_C), F32), pltpu.VMEM((L, SSM_W), F32)],
        compiler_params=_params(("arbitrary",)),
    )(dy, proj, proj, proj, proj, proj, conv_w, conv_b, dt_bias, a_log, d_skip, norm_g, proj, proj, proj, hstates)


def _resident(shape):
    nd = len(shape)
    return pl.BlockSpec(shape, lambda *_: (0,) * nd, pipeline_mode=pl.Buffered(1))


def merge_fwd(y_att, y_sg, y_ssm, proj, x, w_a, w_s, w_m, w_o, g_post):
    S = x.shape[0]
    tm = 256

    def body(ya_ref, ys_ref, ym_ref, gate_ref, x_ref, wa_ref, ws_ref, wm_ref, wo_ref, gp_ref,
             xn_ref, bra_ref, brs_ref, brm_ref, mg_ref, out_ref):
        bra = _dot(ya_ref[...], wa_ref[...])
        brs = _dot(ys_ref[...], ws_ref[...])
        brm = _dot(ym_ref[...], wm_ref[...])
        bra_ref[...] = bra
        brs_ref[...] = brs
        brm_ref[...] = brm
        merged = (_sigmoid(gate_ref[:, 0:1024]) * bra + _sigmoid(gate_ref[:, 1024:2048]) * brs
                  + _sigmoid(gate_ref[:, 2048:3072]) * brm)
        mb = merged.astype(BF16)
        mg_ref[...] = mb
        o = _dot(mb, wo_ref[...])
        out_ref[...] = o
        r = lax.rsqrt(jnp.mean(o * o, axis=-1, keepdims=True) + EPS)
        xn_ref[...] = x_ref[...] + o * r * gp_ref[...]

    row = lambda w: pl.BlockSpec((tm, w), lambda i: (i, 0))
    return pl.pallas_call(
        body, name="merge_fwd", grid=(S // tm,),
        in_specs=[row(1024), row(1024), row(2048), pl.BlockSpec((tm, 3072), lambda i: (i, 0)),
                  row(D), _resident((1024, D)), _resident((1024, D)), _resident((2048, D)), _resident((D, D)),
                  _full((1, D))],
        out_specs=[row(D)] * 6,
        out_shape=[jax.ShapeDtypeStruct((S, D), F32)] * 4 + [jax.ShapeDtypeStruct((S, D), BF16),
                                                             jax.ShapeDtypeStruct((S, D), F32)],
        compiler_params=_params(("arbitrary",)),
    )(y_att, y_sg, y_ssm, proj, x, w_a, w_s, w_m, w_o, g_post)


def merge_bwd(dy, out, g_post, proj, br_a, br_s, br_m, w_a, w_s, w_m, w_o):
    S = dy.shape[0]
    tm = 256

    def body(dy_ref, o_ref, gp_ref, gate_ref, bra_ref, brs_ref, brm_ref, wa_ref, ws_ref, wm_ref, wo_ref,
             dout_ref, dba_ref, dbs_ref, dbm_ref, dgate_ref, dya_ref, dys_ref, dym_ref, dgp_ref):
        @pl.when(pl.program_id(0) == 0)
        def _():
            dgp_ref[...] = jnp.zeros_like(dgp_ref)

        o = o_ref[...]
        dyv = dy_ref[...]
        r = lax.rsqrt(jnp.mean(o * o, axis=-1, keepdims=True) + EPS)
        dyg = dyv * gp_ref[...]
        do = r * dyg - o * (r * r * r) * jnp.mean(dyg * o, axis=-1, keepdims=True)
        dgp_ref[...] += jnp.sum(dyv * o * r, axis=0, keepdims=True)
        dob = do.astype(BF16)
        dout_ref[...] = dob
        dmerged = _dot_nt(dob, wo_ref[...])
        for idx, (br_ref, dbr_ref, w_ref, dyi_ref) in enumerate((
                (bra_ref, dba_ref, wa_ref, dya_ref), (brs_ref, dbs_ref, ws_ref, dys_ref),
                (brm_ref, dbm_ref, wm_ref, dym_ref))):
            s = _sigmoid(gate_ref[:, idx * 1024:(idx + 1) * 1024])
            dbr = (dmerged * s).astype(BF16)
            dbr_ref[...] = dbr
            dgate_ref[:, idx * 1024:(idx + 1) * 1024] = (dmerged * br_ref[...] * s * (1.0 - s)).astype(BF16)
            dyi_ref[...] = _dot_nt(dbr, w_ref[...])

    row = lambda w: pl.BlockSpec((tm, w), lambda i: (i, 0))
    return pl.pallas_call(
        body, name="merge_bwd", grid=(S // tm,),
        in_specs=[row(D), row(D), _full((1, D)), pl.BlockSpec((tm, 3072), lambda i: (i, 0)),
                  row(D), row(D), row(D),
                  _resident((1024, D)), _resident((1024, D)), _resident((2048, D)), _resident((D, D))],
        out_specs=[row(D), row(D), row(D), row(D), row(3072), row(1024), row(1024), row(2048), _full((1, D))],
        out_shape=[jax.ShapeDtypeStruct((S, D), BF16)] * 4 + [
            jax.ShapeDtypeStruct((S, 3072), BF16), jax.ShapeDtypeStruct((S, 1024), F32),
            jax.ShapeDtypeStruct((S, 1024), F32), jax.ShapeDtypeStruct((S, 2048), F32),
            jax.ShapeDtypeStruct((1, D), F32)],
        compiler_params=_params(("arbitrary",)),
    )(dy, out, g_post, proj, br_a, br_s, br_m, w_a, w_s, w_m, w_o)


def loss_head(y, target):
    S = y.shape[0]
    tm = 512

    def body(y_ref, t_ref, dy_ref, loss_ref):
        @pl.when(pl.program_id(0) == 0)
        def _():
            loss_ref[...] = jnp.zeros_like(loss_ref)
        e = y_ref[...] - t_ref[...]
        dy_ref[...] = e * (1.0 / D)
        loss_ref[...] += 0.5 * jnp.sum(jnp.mean(e * e, axis=-1, keepdims=True))

    row = pl.BlockSpec((tm, D), lambda i: (i, 0))
    return pl.pallas_call(
        body, name="loss_head", grid=(S // tm,),
        in_specs=[row, row], out_specs=[row, _full((1, 128))],
        out_shape=[jax.ShapeDtypeStruct((S, D), F32), jax.ShapeDtypeStruct((1, 128), F32)],
        compiler_params=_params(("arbitrary",)),
    )(y, target)


def _adam(w, g, m, v):
    mn = ADAM_B1 * m + (1.0 - ADAM_B1) * g
    vn = ADAM_B2 * v + (1.0 - ADAM_B2) * (g * g)
    m_hat = mn / (1.0 - ADAM_B1 ** ADAM_STEP)
    v_hat = vn / (1.0 - ADAM_B2 ** ADAM_STEP)
    return -ADAM_LR * (m_hat / (jnp.sqrt(v_hat) + ADAM_EPS) + ADAM_WD * w), mn, vn


def adamw_big(w, m, v, f, fb, cc, name, tr, f_row0=0):
    _, R, C = w.shape
    nper = R // tr
    foff = f_row0 // tr

    def body(c_ref, w_ref, m_ref, v_ref, f_ref, fb_ref, g_ref, d_ref, nm_ref, nv_ref):
        layer = pl.program_id(0) // nper
        g = jnp.where(c_ref[0] == layer, f_ref[...], fb_ref[...])
        g_ref[0] = g
        d_ref[0], nm_ref[0], nv_ref[0] = _adam(w_ref[0], g, m_ref[0], v_ref[0])

    wblk = pl.BlockSpec((1, tr, C), lambda i, c: (i // nper, i % nper, 0))
    fblk = pl.BlockSpec((tr, C), lambda i, c: (foff + i % nper, 0))
    grid_spec = pltpu.PrefetchScalarGridSpec(
        num_scalar_prefetch=1, grid=(2 * nper,),
        in_specs=[wblk, wblk, wblk, fblk, fblk], out_specs=[wblk] * 4)
    return pl.pallas_call(
        body, name=name, grid_spec=grid_spec,
        out_shape=[jax.ShapeDtypeStruct(w.shape, F32)] * 4,
        compiler_params=_params(("arbitrary",)),
    )(cc, w, m, v, f, fb)


def adamw_plain(w, g, m, v, name):
    def body(w_ref, g_ref, m_ref, v_ref, d_ref, nm_ref, nv_ref):
        d_ref[...], nm_ref[...], nv_ref[...] = _adam(w_ref[...], g_ref[...], m_ref[...], v_ref[...])

    return pl.pallas_call(
        body, name=name, out_shape=[jax.ShapeDtypeStruct(w.shape, F32)] * 3, compiler_params=_params(),
    )(w, g, m, v)


SMALL = {"norm_pre": ("g_pre", 8), "norm_post": ("g_post", 8), "att_sinks": ("sinks", 8), "sg_ln_g": ("ln_g", 8),
         "sg_ln_b": ("ln_b", 8), "sg_w": ("sg_w", 1024), "sg_b": ("sg_bt", 8), "ssm_conv_b": ("conv_b", 24),
         "ssm_dt_bias": ("dt_bias", 8), "ssm_a_log": ("a_log", 8), "ssm_d": ("d_skip", 8), "ssm_norm_g": ("norm_g", 16)}
SMALL_LAYER_ROWS = sum(r for _, r in SMALL.values())
REL_ROW = DEPTH * SMALL_LAYER_ROWS
LOSS_ROW = REL_ROW + 32
SMALL_ROWS = LOSS_ROW + 8


def _small_rows():
    rows, r = {}, 0
    for l in range(DEPTH):
        for name, (_, n) in SMALL.items():
            rows[(l, name)] = r
            r += n
    return rows


def adamw_small(red, rel, small):
    names = list(SMALL) + ["rel_bias"]
    params = dict(small, rel_bias=rel)
    rows = _small_rows()

    def grad_of(red_ref, l, name, n):
        r0 = rows[(l, name)]
        if name == "sg_b":
            return red_ref[r0:r0 + 8, :]
        if n < 128:
            return red_ref[r0:r0 + 1, 0:n]
        return jnp.concatenate([red_ref[r0 + j:r0 + j + 1, :] for j in range(n // 128)], axis=1)

    def body(red_ref, *refs):
        ins, outs = refs[:3 * len(names)], refs[3 * len(names):]
        for i, name in enumerate(names):
            w_ref, m_ref, v_ref = ins[3 * i:3 * i + 3]
            o = outs[4 * i:4 * i + 4]
            if name == "rel_bias":
                g = red_ref[REL_ROW:REL_ROW + 32, 0:16]
                o[0][...] = g
                o[1][...], o[2][...], o[3][...] = _adam(w_ref[...], g, m_ref[...], v_ref[...])
                continue
            for l in range(DEPTH):
                if name == "sg_w":
                    for grp in range(8):
                        r0 = rows[(l, name)] + grp * 128
                        g = red_ref[r0:r0 + 128, :]
                        o[0][l, grp] = g
                        o[1][l, grp], o[2][l, grp], o[3][l, grp] = _adam(w_ref[l, grp], g, m_ref[l, grp], v_ref[l, grp])
                elif name == "sg_b":
                    g = grad_of(red_ref, l, name, 128)
                    o[0][l] = g
                    o[1][l], o[2][l], o[3][l] = _adam(w_ref[l], g, m_ref[l], v_ref[l])
                else:
                    sl = slice(l, l + 1)
                    g = grad_of(red_ref, l, name, w_ref.shape[-1])
                    o[0][sl, :] = g
                    o[1][sl, :], o[2][sl, :], o[3][sl, :] = _adam(w_ref[sl, :], g, m_ref[sl, :], v_ref[sl, :])

    flat_in = [a for name in names for a in params[name]]
    out_shape = [jax.ShapeDtypeStruct(params[name][0].shape, F32) for name in names for _ in range(4)]
    res = pl.pallas_call(body, name="adamw_small", out_shape=out_shape, compiler_params=_params())(red, *flat_in)
    return {name: tuple(res[4 * i:4 * i + 4]) for i, name in enumerate(names)}


ANY = pl.BlockSpec(memory_space=pl.ANY)


def _place():
    x, y, c = lax.axis_index("x"), lax.axis_index("y"), lax.axis_index("c")
    others = [(1 - x, y), (x, 1 - y), (1 - x, 1 - y)]
    return x, y, c, others


def _rcopy(src, dst, ssem, rsem, to):
    return pltpu.make_async_remote_copy(src_ref=src, dst_ref=dst, send_sem=ssem, recv_sem=rsem,
                                        device_id=to, device_id_type=MESH)


def gather_weights(arrs):
    n = len(arrs)

    def body(*refs):
        srcs, outs, ssem, rsem = refs[:n], refs[n:2 * n], refs[2 * n], refs[2 * n + 1]
        x, y, c, others = _place()
        me = 2 * x + y
        sib = (x, y, 1 - c)
        first = [_rcopy(srcs[i].at[c], outs[i].at[c, me], ssem.at[6 * i + k], rsem.at[6 * i + k], (ox, oy, c))
                 for i in range(n) for k, (ox, oy) in enumerate(others)]
        for cp in first:
            cp.start()
        passed = []
        for k, (ox, oy) in enumerate(others):
            for i in range(n):
                slot = outs[i].at[c, 2 * ox + oy]
                _rcopy(slot, slot, ssem.at[6 * i + k], rsem.at[6 * i + k], sib).wait_recv()
                fw = _rcopy(slot, slot, ssem.at[6 * i + 3 + k], rsem.at[6 * i + 3 + k], sib)
                fw.start()
                passed.append(fw)
        for k, (ox, oy) in enumerate(others):
            for i in range(n):
                slot = outs[i].at[1 - c, 2 * ox + oy]
                _rcopy(slot, slot, ssem.at[6 * i + 3 + k], rsem.at[6 * i + 3 + k], sib).wait_recv()
        for cp in first + passed:
            cp.wait_send()

    return pl.pallas_call(
        body, name="gather_weights",
        in_specs=[ANY] * n, out_specs=[ANY] * n,
        out_shape=[jax.ShapeDtypeStruct((2, SHARDS) + a.shape[1:], a.dtype) for a in arrs],
        scratch_shapes=[pltpu.SemaphoreType.DMA((6 * n,)), pltpu.SemaphoreType.DMA((6 * n,))],
    )(*arrs)


HBM = pl.BlockSpec(memory_space=pltpu.HBM)
SEM = pl.BlockSpec(memory_space=pltpu.SEMAPHORE)
EFFECT = pltpu.SideEffectType.DATAFLOW_SIDE_EFFECTING


def _in_hbm(a):
    return pltpu.with_memory_space_constraint(a, pltpu.HBM)


def gather_start(srcs, after, name):
    n = len(srcs)
    lands = [_in_hbm(lax.empty((SHARDS,) + a.shape, a.dtype)) for a in srcs]

    def body(*refs):
        src, land = refs[:n], refs[n:2 * n]
        ssem, rsem, token = refs[2 * n + 1], refs[2 * n + 2], refs[-1]
        x, y, c, others = _place()
        me = 2 * x + y
        for i in range(n):
            for k, (ox, oy) in enumerate(others):
                _rcopy(src[i], land[i].at[me], ssem.at[3 * i + k], rsem.at[3 * i + k], (ox, oy, c)).start()
        token[...] = jnp.zeros_like(token)

    bufs = [_in_hbm(a) for a in srcs] + lands
    out = pl.pallas_call(
        body, name=name,
        out_shape=(pltpu.SemaphoreType.DMA((3 * n,)), pltpu.SemaphoreType.DMA((3 * n,)),
                   *[pltpu.HBM(b.shape, b.dtype) for b in bufs], jax.ShapeDtypeStruct((8, 128), F32)),
        in_specs=[HBM] * (2 * n) + [ANY],
        out_specs=(SEM, SEM, *[HBM] * (2 * n), pl.BlockSpec(memory_space=pltpu.VMEM)),
        input_output_aliases={i: 2 + i for i in range(2 * n)},
        compiler_params=pltpu.CompilerParams(has_side_effects=EFFECT),
    )(*bufs, after)
    return out[0], out[1], list(out[2:2 + n]), list(out[2 + n:2 + 2 * n]), out[-1]


def gather_wait(ssem, rsem, srcs, lands, after, name):
    n = len(srcs)

    def body(*refs):
        src, land = refs[:n], refs[n:2 * n]
        s_sem, r_sem = refs[2 * n], refs[2 * n + 1]
        x, y, c, others = _place()
        for i in range(n):
            for k, (ox, oy) in enumerate(others):
                cp = _rcopy(src[i], land[i].at[2 * ox + oy], s_sem.at[3 * i + k], r_sem.at[3 * i + k], (ox, oy, c))
                cp.wait_send()
                cp.wait_recv()

    bufs = list(srcs) + list(lands)
    out = pl.pallas_call(
        body, name=name,
        out_shape=tuple(pltpu.HBM(b.shape, b.dtype) for b in bufs),
        in_specs=[HBM] * (2 * n) + [SEM, SEM, ANY],
        out_specs=tuple([HBM] * (2 * n)),
        input_output_aliases={i: i for i in range(2 * n)},
        compiler_params=pltpu.CompilerParams(has_side_effects=EFFECT),
    )(*bufs, ssem, rsem, after)
    return list(out[n:2 * n])


def grad_sibling_exchange(arrs):
    n = len(arrs)

    def body(*refs):
        srcs, outs, ssem, rsem = refs[:n], refs[n:2 * n], refs[2 * n], refs[2 * n + 1]
        x, y, c, _ = _place()
        cps = [_rcopy(srcs[i].at[1 - c], outs[i], ssem.at[i], rsem.at[i], (x, y, 1 - c)) for i in range(n)]
        for cp in cps:
            cp.start()
        for cp in cps:
            cp.wait()

    return pl.pallas_call(
        body, name="grad_sibling_exchange",
        in_specs=[ANY] * n, out_specs=[ANY] * n,
        out_shape=[jax.ShapeDtypeStruct(a.shape[1:], F32) for a in arrs],
        scratch_shapes=[pltpu.SemaphoreType.DMA((n,)), pltpu.SemaphoreType.DMA((n,))],
    )(*arrs)


def grad_chip_sum(g, sb, cc, tr, name):
    _, _, R, C = g.shape
    blk = pl.BlockSpec((1, tr, C), lambda s, r, c: (s, r, 0))
    grid_spec = pltpu.PrefetchScalarGridSpec(
        num_scalar_prefetch=1, grid=(SHARDS, R // tr),
        in_specs=[pl.BlockSpec((1, 1, tr, C), lambda s, r, c: (c[0], s, r, 0)), blk],
        out_specs=[blk, blk])

    def body(c_ref, a_ref, b_ref, o_ref, ob_ref):
        t = a_ref[0] + b_ref[...]
        o_ref[...] = t
        ob_ref[...] = t.astype(BF16)

    return pl.pallas_call(
        body, name=name, grid_spec=grid_spec,
        out_shape=[jax.ShapeDtypeStruct((SHARDS, R, C), F32), jax.ShapeDtypeStruct((SHARDS, R, C), BF16)],
        compiler_params=_params(("arbitrary", "arbitrary")),
    )(cc, g, sb)


def grad_chip_exchange(arrs):
    n = len(arrs)

    def body(*refs):
        srcs, outs, ssem, rsem = refs[:n], refs[n:2 * n], refs[2 * n], refs[2 * n + 1]
        x, y, c, others = _place()
        me = 2 * x + y
        sends = [_rcopy(srcs[i].at[2 * ox + oy], outs[i].at[me], ssem.at[3 * i + k], rsem.at[3 * i + k], (ox, oy, c))
                 for i in range(n) for k, (ox, oy) in enumerate(others)]
        for cp in sends:
            cp.start()
        for i in range(n):
            for k, (ox, oy) in enumerate(others):
                slot = outs[i].at[2 * ox + oy]
                _rcopy(slot, slot, ssem.at[3 * i + k], rsem.at[3 * i + k], (ox, oy, c)).wait_recv()
        for cp in sends:
            cp.wait_send()

    return pl.pallas_call(
        body, name="grad_chip_exchange",
        in_specs=[ANY] * n, out_specs=[ANY] * n,
        out_shape=[jax.ShapeDtypeStruct(a.shape, a.dtype) for a in arrs],
        scratch_shapes=[pltpu.SemaphoreType.DMA((3 * n,)), pltpu.SemaphoreType.DMA((3 * n,))],
    )(*arrs)


def grad_shard_sum(t, rb, me, tr, name):
    _, R, C = t.shape
    grid_spec = pltpu.PrefetchScalarGridSpec(
        num_scalar_prefetch=1, grid=(R // tr,),
        in_specs=[pl.BlockSpec((1, tr, C), lambda r, m: (m[0], r, 0)),
                  pl.BlockSpec((SHARDS, tr, C), lambda r, m: (0, r, 0))],
        out_specs=pl.BlockSpec((tr, C), lambda r, m: (r, 0)))

    def body(m_ref, t_ref, r_ref, o_ref):
        part = [jnp.where(m_ref[0] == s, t_ref[0], r_ref[s].astype(F32)) for s in range(SHARDS)]
        o_ref[...] = ((part[0] + part[1]) + part[2]) + part[3]

    return pl.pallas_call(
        body, name=name, grid_spec=grid_spec,
        out_shape=jax.ShapeDtypeStruct((R, C), F32),
        compiler_params=_params(("arbitrary",)),
    )(me, t, rb)


def grad_sibling_share(arrs):
    n = len(arrs)

    def body(*refs):
        srcs, outs, ssem, rsem = refs[:n], refs[n:2 * n], refs[2 * n], refs[2 * n + 1]
        x, y, c, _ = _place()
        cps = [_rcopy(srcs[i], outs[i], ssem.at[i], rsem.at[i], (x, y, 1 - c)) for i in range(n)]
        for cp in cps:
            cp.start()
        for cp in cps:
            cp.wait()

    return pl.pallas_call(
        body, name="grad_sibling_share",
        in_specs=[ANY] * n, out_specs=[ANY] * n,
        out_shape=[jax.ShapeDtypeStruct(a.shape, F32) for a in arrs],
        scratch_shapes=[pltpu.SemaphoreType.DMA((n,)), pltpu.SemaphoreType.DMA((n,))],
    )(*arrs)


def _allreduce_rows(src, sib_buf, chips, out_ref, ssem, rsem):
    x, y, c, others = _place()
    me = 2 * x + y
    cp = _rcopy(src, sib_buf, ssem.at[0], rsem.at[0], (x, y, 1 - c))
    cp.start()
    cp.wait()
    chips[me] = src[...] + sib_buf[...]
    sends = [_rcopy(chips.at[me], chips.at[me], ssem.at[1 + k], rsem.at[1 + k], (ox, oy, c))
             for k, (ox, oy) in enumerate(others)]
    for s in sends:
        s.start()
    for k, (ox, oy) in enumerate(others):
        slot = chips.at[2 * ox + oy]
        _rcopy(slot, slot, ssem.at[1 + k], rsem.at[1 + k], (ox, oy, c)).wait_recv()
    for s in sends:
        s.wait_send()
    out_ref[...] = ((chips[0] + chips[1]) + chips[2]) + chips[3]


def _allreduce_scratch(rows):
    return [pltpu.VMEM((rows, 128), F32), pltpu.VMEM((SHARDS, rows, 128), F32),
            pltpu.SemaphoreType.DMA((4,)), pltpu.SemaphoreType.DMA((4,))]


def allreduce_rows(buf, name):
    rows = buf.shape[0]
    VM = pl.BlockSpec(memory_space=pltpu.VMEM)

    def body(src_ref, out_ref, sib_buf, chips, ssem, rsem):
        _allreduce_rows(src_ref, sib_buf, chips, out_ref, ssem, rsem)

    return pl.pallas_call(
        body, name=name, in_specs=[VM], out_specs=VM,
        out_shape=jax.ShapeDtypeStruct((rows, 128), F32),
        scratch_shapes=_allreduce_scratch(rows), compiler_params=_params(),
    )(buf)


def small_allreduce(grads, rel, loss_part):
    rows = _small_rows()
    keys = [(l, name) for l in range(DEPTH) for name in SMALL]
    flat = [grads[l][SMALL[name][0]] for l, name in keys] + [rel, loss_part]

    def body(*refs):
        ins = refs[:len(flat)]
        out_ref, src, sib_buf, chips, ssem, rsem = refs[len(flat):]
        src[...] = jnp.zeros_like(src)
        for (l, name), ref in zip(keys, ins):
            r0 = rows[(l, name)]
            if name == "sg_w":
                for grp in range(8):
                    src[r0 + grp * 128:r0 + (grp + 1) * 128, :] = ref[grp]
            elif name == "sg_b":
                src[r0:r0 + 8, :] = ref[...].T[0:8, :]
            else:
                for j in range(ref.shape[1] // 128):
                    src[r0 + j:r0 + j + 1, :] = ref[:, j * 128:(j + 1) * 128]
        src[REL_ROW:REL_ROW + 32, 0:16] = ins[-2][...]
        src[LOSS_ROW:LOSS_ROW + 1, :] = ins[-1][...]
        _allreduce_rows(src, sib_buf, chips, out_ref, ssem, rsem)

    return pl.pallas_call(
        body, name="small_allreduce",
        out_shape=jax.ShapeDtypeStruct((SMALL_ROWS, 128), F32),
        scratch_shapes=[pltpu.VMEM((SMALL_ROWS, 128), F32)] + _allreduce_scratch(SMALL_ROWS),
        compiler_params=_params(),
    )(*flat)


def _pad_lanes(v):
    return jnp.zeros((1, 128), F32).at[0, :v.shape[0]].set(v)


def layer_fwd(x, wts, bias):
    wt = wts["wt"]
    tn = {name: t for name, _, t in GROUPS}
    p_gate, h = inproj_first(x, wts["g_pre"], wt["gate"], tn["gate"], "inproj_gate")
    p_sgu, p_att, p_ssd = (inproj_group(h, wt[n], tn[n], "inproj_" + n) for n in ("sgu", "att", "ssd"))
    y_att = att_fwd(p_att, bias, wts["sinks"])
    y_sg = sgu_fwd(p_sgu, wts["ln_g"], wts["ln_b"], wts["sg_w"], wts["sg_bt"])
    y_ssm, hst = ssd_fwd(p_ssd, wts["conv_w"], wts["conv_b"], wts["dt_bias"], wts["a_log"], wts["d_skip"],
                         wts["norm_g"])
    x_new, br_a, br_s, br_m, merged, out = merge_fwd(
        y_att, y_sg, y_ssm, p_gate, x, wts["w_a"], wts["w_s"], wts["w_m"], wts["w_o"], wts["g_post"])
    saved = dict(x=x, p_gate=p_gate, p_sgu=p_sgu, p_att=p_att, p_ssd=p_ssd, h=h,
                 y_att=y_att, y_sg=y_sg, y_ssm=y_ssm, hst=hst,
                 br_a=br_a, br_s=br_s, br_m=br_m, merged=merged, out=out)
    return x_new, saved


def layer_bwd(dy, wts, bias, sv):
    dout, dba, dbs, dbm, d_gate, dya, dys, dym, dg_post = merge_bwd(
        dy, sv["out"], wts["g_post"], sv["p_gate"], sv["br_a"], sv["br_s"], sv["br_m"],
        wts["w_a"], wts["w_s"], wts["w_m"], wts["w_o"])
    d_att, dbias, dsinks = att_bwd(dya, sv["p_att"], bias, wts["sinks"])
    d_sgu, dsg_w, dsg_bt, dln_g, dln_b = sgu_bwd(dys, sv["p_sgu"], wts["ln_g"], wts["ln_b"], wts["sg_w"],
                                                 wts["sg_bt"])
    d_ssd, dcw, dcb, ddtb, dalog, ddsk, dng = ssd_bwd(
        dym, sv["p_ssd"], sv["hst"], wts["conv_w"], wts["conv_b"], wts["dt_bias"], wts["a_log"], wts["d_skip"],
        wts["norm_g"])
    dps = dict(gate=d_gate, sgu=d_sgu, att=d_att, ssd=d_ssd)
    wt = wts["wt"]
    tn = {name: t for name, _, t in GROUPS}
    acc = None
    for n in ("gate", "sgu", "ssd"):
        acc = dh_group(dps[n], wt[n], acc, tn[n], "dh_" + n)
    dx, dg_pre = dh_last(dps["att"], wt["att"], acc, sv["x"], wts["g_pre"], dy, tn["att"], "dh_att")
    grads = dict(
        w_in={n: dw_group(dps[n], sv["h"], tn[n], "dw_in_" + n) for n in dps},
        w_a=matmul_tn(sv["y_att"], dba, "dw_att"),
        w_s=matmul_tn(sv["y_sg"], dbs, "dw_sg"),
        w_m=matmul_tn(sv["y_ssm"], dbm, "dw_ssm"),
        w_o=matmul_tn(sv["merged"], dout, "dw_out"),
        g_pre=dg_pre, g_post=dg_post, sinks=dsinks, ln_g=dln_g, ln_b=dln_b, sg_w=dsg_w, sg_bt=dsg_bt,
        conv_w=dcw, conv_b=dcb, dt_bias=ddtb, a_log=dalog, d_skip=ddsk, norm_g=dng, bias=dbias)
    return dx, grads


REST_OFF = (0, 256, 512, 1024, 1280)
REST_ROWS = 1296


def kernel(x, w_in, norm_pre, norm_post, rel_bias, att_sinks, sg_ln_g, sg_ln_b, sg_w, sg_b, ssm_conv_w, ssm_conv_b, ssm_dt_bias, ssm_a_log, ssm_d, ssm_norm_g, w_br_att, w_br_sg, w_br_ssm, w_out, loss_target, m_w_in, m_norm_pre, m_norm_post, m_rel_bias, m_att_sinks, m_sg_ln_g, m_sg_ln_b, m_sg_w, m_sg_b, m_ssm_conv_w, m_ssm_conv_b, m_ssm_dt_bias, m_ssm_a_log, m_ssm_d, m_ssm_norm_g, m_w_br_att, m_w_br_sg, m_w_br_ssm, m_w_out, v_w_in, v_norm_pre, v_norm_post, v_rel_bias, v_att_sinks, v_sg_ln_g, v_sg_ln_b, v_sg_w, v_sg_b, v_ssm_conv_w, v_ssm_conv_b, v_ssm_dt_bias, v_ssm_a_log, v_ssm_d, v_ssm_norm_g, v_w_br_att, v_w_br_sg, v_w_br_ssm, v_w_out):
    cx, cy, cc = lax.axis_index("x"), lax.axis_index("y"), lax.axis_index("c")
    me = 2 * cx + cy
    xs = x[0]
    S = xs.shape[0]

    tr = lambda a: jnp.transpose(a, (0, 2, 1))
    w_in_b = tr(w_in).astype(BF16)
    w_rest_b = jnp.concatenate([w_br_att, w_br_sg, w_br_ssm, w_out], axis=1).astype(BF16)
    halves = lambda a: a.reshape(2, a.shape[0] // 2, a.shape[1])
    all0_in, all0_rest = gather_weights([halves(w_in_b[0]), halves(w_rest_b[0])])
    g1_ssem, g1_rsem, g1_srcs, g1_lands, g1_token = gather_start([w_in_b[1], w_rest_b[1]], all0_rest, "gather_l1_start")
    convw_slot = jnp.zeros((SHARDS, DEPTH * CONV_K * 768 // 128, 128), F32)
    convw_slot = lax.dynamic_update_index_in_dim(
        convw_slot, jnp.where(cc == 0, 1.0, 0.0) * ssm_conv_w.reshape(-1, 128), me, 0)
    convw_all = allreduce_rows(convw_slot.reshape(-1, 128), "gather_conv_w")
    convw_all = convw_all.reshape(SHARDS, DEPTH, CONV_K, 768).transpose(1, 2, 0, 3).reshape(DEPTH, CONV_K, CONV_C)

    o = REST_OFF

    def layer_weights(l, gathered_in, gathered_rest, g_pre):
        sh_in = [jnp.where(me == s, w_in_b[l], gathered_in[s]) for s in range(SHARDS)]
        sh_rest = [jnp.where(me == s, w_rest_b[l], gathered_rest[s]) for s in range(SHARDS)]
        rest = lambda k: jnp.concatenate([r[o[k]:o[k + 1]] for r in sh_rest], axis=0)
        return dict(
            wt=group_weights(jnp.concatenate(sh_in, axis=0)),
            w_a=rest(0), w_s=rest(1), w_m=rest(2), w_o=rest(3),
            g_pre=g_pre, g_post=norm_post[l][None], sinks=att_sinks[l],
            ln_g=sg_ln_g[l][None], ln_b=sg_ln_b[l][None], sg_w=sg_w[l],
            sg_bt=sg_b[l].T,
            conv_w=jnp.concatenate([convw_all[l], jnp.zeros((4, CONV_C), F32)], axis=0),
            conv_b=ssm_conv_b[l][None], dt_bias=_pad_lanes(ssm_dt_bias[l]), a_log=_pad_lanes(ssm_a_log[l]),
            d_skip=_pad_lanes(ssm_d[l]), norm_g=ssm_norm_g[l][None])

    bias = bias_table(rel_bias)
    layers = [layer_weights(0, [all0_in[:, s].reshape(3400, D) for s in range(SHARDS)],
                            [all0_rest[:, s].reshape(1280, D) for s in range(SHARDS)],
                            (norm_pre[0] + g1_token[0, 0])[None])]
    act, sv0 = layer_fwd(xs, layers[0], bias)
    land_in, land_rest = gather_wait(g1_ssem, g1_rsem, g1_srcs, g1_lands, act, "gather_l1_wait")
    layers.append(layer_weights(1, land_in, land_rest, norm_pre[1][None]))
    act, sv1 = layer_fwd(act, layers[1], bias)
    saved = [sv0, sv1]
    dy, loss_part = loss_head(act, loss_target[0])
    grads = [None] * DEPTH
    for l in reversed(range(DEPTH)):
        dy, grads[l] = layer_bwd(dy, layers[l], bias, saved[l])
    grad_x = dy[None]
    grad_rel_local = bias_grad(grads[0]["bias"] + grads[1]["bias"])

    cvec = jnp.reshape(cc, (1,)).astype(jnp.int32)
    mvec = jnp.reshape(me, (1,)).astype(jnp.int32)
    g_in, g_rest = [], []
    for l in range(DEPTH):
        g = grads[l]
        g_in.append(jnp.pad(ungroup_grads(g["w_in"]).reshape(SHARDS, 3400, D),
                            ((0, 0), (0, W_IN_ROWS - 3400), (0, 0))))
        gcw = g["conv_w"][0:CONV_K].reshape(CONV_K, SHARDS, 768).transpose(1, 0, 2).reshape(SHARDS, 3, 1024)
        g_rest.append(jnp.concatenate([
            g["w_a"].reshape(SHARDS, 256, D), g["w_s"].reshape(SHARDS, 256, D), g["w_m"].reshape(SHARDS, 512, D),
            g["w_o"].reshape(SHARDS, 256, D), jnp.pad(gcw, ((0, 0), (0, REST_ROWS - REST_OFF[4] - 3), (0, 0)))],
            axis=1))
    g_in, g_rest = jnp.stack(g_in), jnp.stack(g_rest)
    sb_in, sb_rest = grad_sibling_exchange([g_in, g_rest])
    t_in, t_in_b = grad_chip_sum(g_in, sb_in, cvec, 384, "chip_sum_w_in")
    t_rest, t_rest_b = grad_chip_sum(g_rest, sb_rest, cvec, 432, "chip_sum_rest")
    rb_in, rb_rest = grad_chip_exchange([t_in_b, t_rest_b])
    f_in = grad_shard_sum(t_in, rb_in, mvec, 384, "shard_sum_w_in")
    f_rest = grad_shard_sum(t_rest, rb_rest, mvec, 432, "shard_sum_rest")
    fb_in, fb_rest = grad_sibling_share([f_in, f_rest])

    red = small_allreduce(grads, grad_rel_local, loss_part)
    loss = red[LOSS_ROW, 0]

    res = adamw_small(red, (rel_bias, m_rel_bias, v_rel_bias), dict(
        norm_pre=(norm_pre, m_norm_pre, v_norm_pre), norm_post=(norm_post, m_norm_post, v_norm_post),
        att_sinks=(att_sinks, m_att_sinks, v_att_sinks), sg_ln_g=(sg_ln_g, m_sg_ln_g, v_sg_ln_g),
        sg_ln_b=(sg_ln_b, m_sg_ln_b, v_sg_ln_b), sg_w=(sg_w, m_sg_w, v_sg_w), sg_b=(sg_b, m_sg_b, v_sg_b),
        ssm_conv_b=(ssm_conv_b, m_ssm_conv_b, v_ssm_conv_b), ssm_dt_bias=(ssm_dt_bias, m_ssm_dt_bias, v_ssm_dt_bias),
        ssm_a_log=(ssm_a_log, m_ssm_a_log, v_ssm_a_log), ssm_d=(ssm_d, m_ssm_d, v_ssm_d),
        ssm_norm_g=(ssm_norm_g, m_ssm_norm_g, v_ssm_norm_g)))
    res["w_in"] = tuple(tr(a) for a in adamw_big(tr(w_in), tr(m_w_in), tr(v_w_in), f_in, fb_in, cvec, "adamw_w_in", 200))
    res["w_br_att"] = adamw_big(w_br_att, m_w_br_att, v_w_br_att, f_rest, fb_rest, cvec, "adamw_w_br_att", 256, o[0])
    res["w_br_sg"] = adamw_big(w_br_sg, m_w_br_sg, v_w_br_sg, f_rest, fb_rest, cvec, "adamw_w_br_sg", 256, o[1])
    res["w_br_ssm"] = adamw_big(w_br_ssm, m_w_br_ssm, v_w_br_ssm, f_rest, fb_rest, cvec, "adamw_w_br_ssm", 512, o[2])
    res["w_out"] = adamw_big(w_out, m_w_out, v_w_out, f_rest, fb_rest, cvec, "adamw_w_out", 256, o[3])
    cw_mine = f_rest[o[4]:o[4] + 3].reshape(CONV_K, 768)
    cw_sib = fb_rest[o[4]:o[4] + 3].reshape(CONV_K, 768)
    g_conv_w = jnp.stack([jnp.where(cc == l, cw_mine, cw_sib) for l in range(DEPTH)])
    res["ssm_conv_w"] = (g_conv_w,) + tuple(adamw_plain(ssm_conv_w, g_conv_w, m_ssm_conv_w, v_ssm_conv_w, "adamw_conv_w"))

    order = ["w_in", "norm_pre", "norm_post", "rel_bias", "att_sinks", "sg_ln_g", "sg_ln_b", "sg_w", "sg_b",
             "ssm_conv_w", "ssm_conv_b", "ssm_dt_bias", "ssm_a_log", "ssm_d", "ssm_norm_g",
             "w_br_att", "w_br_sg", "w_br_ssm", "w_out"]
    return (loss, grad_x, *[res[n][0] for n in order], *[res[n][1] for n in order],
            *[res[n][2] for n in order], *[res[n][3] for n in order])
```

```python
import functools
import math

import numpy as np
import jax
import jax.numpy as jnp
from jax import lax
from jax.experimental import pallas as pl
from jax.experimental.pallas import tpu as pltpu

F32 = jnp.float32
BF16 = jnp.bfloat16
MESH = pl.DeviceIdType.MESH

D = 1024
DEPTH = 2
EPS = 1e-6
L = 128
HEADS = 16
KV = 2
DH = 64
SSM_W = 2048
SSM_H = 32
SSM_P = 64
SSM_G = 4
SSM_N = 128
CONV_K = 4
CONV_C = 3072
NEG = -1e30
IN_COLS = 13600

GROUPS = (("gate", 3072, 1536), ("sgu", 3072, 1536), ("att", 2304, 2304), ("ssd", 5376, 1792))
W_IN_ROWS = 3456

ADAM_LR = 0.001
ADAM_B1 = 0.9
ADAM_B2 = 0.999
ADAM_EPS = 1e-08
ADAM_WD = 0.01
ADAM_STEP = 10

VMEM_LIMIT = 56 * 1024 * 1024

PACK_ROWS = 4704
PACK_TILE = 224
SHARDS = 4


def _dot(a, b):
    return jnp.dot(a, b, preferred_element_type=F32)


def _dot_nt(a, b):
    return lax.dot_general(a, b, (((1,), (1,)), ((), ())), preferred_element_type=F32)


def _dot_tn(a_f32, b):
    return jnp.dot(a_f32.T.astype(BF16), b, preferred_element_type=F32)


def _dot_hi(a, b):
    return jnp.dot(a, b, preferred_element_type=F32, precision=lax.Precision.HIGHEST)


def _pieces(x, n):
    out = []
    for _ in range(n - 1):
        p = x.astype(BF16)
        out.append(p)
        x = x - p.astype(F32)
    out.append(x.astype(BF16))
    return out


def _dot_sel(a, sel, n):
    sel = sel.astype(BF16)
    acc = None
    for p in _pieces(a, n):
        t = _dot(p, sel)
        acc = t if acc is None else acc + t
    return acc


def _sel_dot(sel, b, n):
    sel = sel.astype(BF16)
    acc = None
    for p in _pieces(b, n):
        t = _dot(sel, p)
        acc = t if acc is None else acc + t
    return acc


def _sigmoid(x):
    return 1.0 / (1.0 + jnp.exp(-x))


def _softplus(x):
    return jnp.maximum(x, 0.0) + jnp.log(1.0 + jnp.exp(-jnp.abs(x)))


def _params(sem=None, vmem=VMEM_LIMIT):
    kw = dict(vmem_limit_bytes=vmem)
    if sem is not None:
        kw["dimension_semantics"] = sem
    return pltpu.CompilerParams(**kw)


def _full(shape):
    nd = len(shape)
    return pl.BlockSpec(shape, lambda *_: (0,) * nd)


def group_weights(wt):
    return dict(
        gate=wt[10528:13600],
        sgu=wt[2304:5376],
        att=jnp.concatenate([wt[0:1024], wt[1280:2304], wt[1024:1280]], axis=0),
        ssd=jnp.concatenate([wt[5376:10496], wt[10496:10528], jnp.zeros((224, D), wt.dtype)], axis=0))


def ungroup_grads(g):
    a, s = g["att"], g["ssd"]
    return jnp.concatenate([a[0:1024], a[2048:2304], a[1024:2048], g["sgu"], s[0:5152], g["gate"]], axis=0)


def _bucket_table():
    qi = np.arange(L)[:, None]
    kj = np.arange(2 * L)[None, :]
    dist = np.maximum(qi + L - kj, 0)
    dist_f = np.maximum(dist, 1).astype(np.float32)
    large = 16 + (np.log(dist_f / np.float32(16)) / np.float32(math.log(128 / 16)) * np.float32(16)).astype(np.int32)
    large = np.minimum(large, 31)
    return np.where(dist < 16, dist, large).astype(np.int32)


def bias_table(rel_bias):
    buckets = jnp.asarray(_bucket_table().reshape(1, L * 2 * L))

    def body(rb_ref, bk_ref, out_ref):
        onehot = (lax.broadcasted_iota(jnp.int32, (32, L * 2 * L), 0) == bk_ref[...]).astype(F32)
        out_ref[...] = lax.dot_general(rb_ref[...], onehot, (((0,), (0,)), ((), ())),
                                       preferred_element_type=F32, precision=lax.Precision.HIGHEST)

    out = pl.pallas_call(
        body, name="bias_table",
        out_shape=jax.ShapeDtypeStruct((HEADS, L * 2 * L), F32),
        compiler_params=_params(),
    )(rel_bias, buckets)
    return out.reshape(HEADS, L, 2 * L)


def bias_grad(dbias):
    buckets = jnp.asarray(_bucket_table().reshape(1, L * 2 * L))

    def body(db_ref, bk_ref, out_ref):
        onehot = (lax.broadcasted_iota(jnp.int32, (32, L * 2 * L), 0) == bk_ref[...]).astype(F32)
        out_ref[...] = lax.dot_general(onehot, db_ref[...], (((1,), (1,)), ((), ())),
                                       preferred_element_type=F32, precision=lax.Precision.HIGHEST)

    return pl.pallas_call(
        body, name="bias_grad",
        out_shape=jax.ShapeDtypeStruct((32, HEADS), F32),
        compiler_params=_params(),
    )(dbias.reshape(HEADS, L * 2 * L), buckets)


def _row_tile(S):
    return 1024 if S % 1024 == 0 else 512


def inproj_first(x, g_pre, wt, tn, name):
    S, W = x.shape[0], wt.shape[0]
    tm = _row_tile(S)

    def body(x_ref, g_ref, w_ref, o_ref, h_ref):
        @pl.when(pl.program_id(1) == 0)
        def _():
            xv = x_ref[...]
            r = lax.rsqrt(jnp.mean(xv * xv, axis=-1, keepdims=True) + EPS)
            h_ref[...] = (xv * r * g_ref[...]).astype(BF16)
        o_ref[...] = _dot_nt(h_ref[...], w_ref[...])

    return pl.pallas_call(
        body, name=name, grid=(S // tm, W // tn),
        in_specs=[pl.BlockSpec((tm, D), lambda i, j: (i, 0)), _full((1, D)),
                  pl.BlockSpec((tn, D), lambda i, j: (j, 0))],
        out_specs=[pl.BlockSpec((tm, tn), lambda i, j: (i, j)), pl.BlockSpec((tm, D), lambda i, j: (i, 0))],
        out_shape=[jax.ShapeDtypeStruct((S, W), F32), jax.ShapeDtypeStruct((S, D), BF16)],
        compiler_params=_params(("arbitrary", "arbitrary")),
    )(x, g_pre, wt)


def inproj_group(h, wt, tn, name):
    S, W = h.shape[0], wt.shape[0]
    tm = _row_tile(S)

    def body(h_ref, w_ref, o_ref):
        o_ref[...] = _dot_nt(h_ref[...], w_ref[...])

    return pl.pallas_call(
        body, name=name, grid=(S // tm, W // tn),
        in_specs=[pl.BlockSpec((tm, D), lambda i, j: (i, 0)), pl.BlockSpec((tn, D), lambda i, j: (j, 0))],
        out_specs=pl.BlockSpec((tm, tn), lambda i, j: (i, j)),
        out_shape=jax.ShapeDtypeStruct((S, W), F32),
        compiler_params=_params(("arbitrary", "arbitrary")),
    )(h, wt)


def dh_group(dp, wt, acc, tk, name):
    S, W = dp.shape
    tm = _row_tile(S)

    def body(*refs):
        dp_ref, w_ref, o_ref = refs[0], refs[1], refs[-1]
        first = pl.program_id(1) == 0
        if acc is None:
            @pl.when(first)
            def _():
                o_ref[...] = jnp.zeros_like(o_ref)
        else:
            @pl.when(first)
            def _():
                o_ref[...] = refs[2][...]
        o_ref[...] += _dot(dp_ref[...], w_ref[...])

    row = pl.BlockSpec((tm, D), lambda i, k: (i, 0))
    return pl.pallas_call(
        body, name=name, grid=(S // tm, W // tk),
        in_specs=[pl.BlockSpec((tm, tk), lambda i, k: (i, k)), pl.BlockSpec((tk, D), lambda i, k: (k, 0))]
        + ([] if acc is None else [row]),
        out_specs=row, out_shape=jax.ShapeDtypeStruct((S, D), F32),
        input_output_aliases={} if acc is None else {2: 0},
        compiler_params=_params(("arbitrary", "arbitrary")),
    )(*((dp, wt) if acc is None else (dp, wt, acc)))


def dh_last(dp, wt, acc_in, x, g_pre, dy, tk, name):
    S, W = dp.shape
    tm = 512
    nk = W // tk

    def body(dp_ref, w_ref, a_ref, x_ref, g_ref, dy_ref, dx_ref, dg_ref, acc):
        i, k = pl.program_id(0), pl.program_id(1)

        @pl.when(k == 0)
        def _():
            acc[...] = a_ref[...]

        acc[...] += _dot(dp_ref[...], w_ref[...])

        @pl.when((k == nk - 1) & (i == 0))
        def _():
            dg_ref[...] = jnp.zeros_like(dg_ref)

        @pl.when(k == nk - 1)
        def _():
            xv = x_ref[...]
            dh = acc[...]
            g = g_ref[...]
            r = lax.rsqrt(jnp.mean(xv * xv, axis=-1, keepdims=True) + EPS)
            dhg = dh * g
            dx_ref[...] = dy_ref[...] + r * dhg - xv * (r * r * r) * jnp.mean(dhg * xv, axis=-1, keepdims=True)
            dg_ref[...] += jnp.sum(dh * xv * r, axis=0, keepdims=True)

    row = pl.BlockSpec((tm, D), lambda i, k: (i, 0))
    return pl.pallas_call(
        body, name=name, grid=(S // tm, nk),
        in_specs=[pl.BlockSpec((tm, tk), lambda i, k: (i, k)), pl.BlockSpec((tk, D), lambda i, k: (k, 0)),
                  row, row, _full((1, D)), row],
        out_specs=[row, _full((1, D))],
        out_shape=[jax.ShapeDtypeStruct((S, D), F32), jax.ShapeDtypeStruct((1, D), F32)],
        scratch_shapes=[pltpu.VMEM((tm, D), F32)],
        compiler_params=_params(("arbitrary", "arbitrary")),
    )(dp, wt, acc_in, x, g_pre, dy)


def dw_group(dp, h, tn, name, ts=512):
    S, W = dp.shape

    def body(dp_ref, h_ref, o_ref):
        @pl.when(pl.program_id(1) == 0)
        def _():
            o_ref[...] = jnp.zeros_like(o_ref)
        o_ref[...] += _dot_tn(dp_ref[...].astype(F32), h_ref[...])

    return pl.pallas_call(
        body, name=name, grid=(W // tn, S // ts),
        in_specs=[pl.BlockSpec((ts, tn), lambda j, s: (s, j)), pl.BlockSpec((ts, D), lambda j, s: (s, 0))],
        out_specs=pl.BlockSpec((tn, D), lambda j, s: (j, 0)),
        out_shape=jax.ShapeDtypeStruct((W, D), F32),
        compiler_params=_params(("arbitrary", "arbitrary")),
    )(dp, h)


def matmul_tn(a, b, name, tn=512, ts=512):
    S, K = a.shape
    N = b.shape[1]
    ns = S // ts

    def body(a_ref, b_ref, o_ref):
        @pl.when(pl.program_id(1) == 0)
        def _():
            o_ref[...] = jnp.zeros_like(o_ref)
        o_ref[...] += _dot_tn(a_ref[...].astype(F32), b_ref[...])

    return pl.pallas_call(
        body, name=name, grid=(N // tn, ns),
        in_specs=[pl.BlockSpec((ts, K), lambda j, s: (s, 0)), pl.BlockSpec((ts, tn), lambda j, s: (s, j))],
        out_specs=pl.BlockSpec((K, tn), lambda j, s: (0, j)),
        out_shape=jax.ShapeDtypeStruct((K, N), F32),
        compiler_params=_params(("arbitrary", "arbitrary")),
    )(a, b)


def _att_mask(n):
    qi = lax.broadcasted_iota(jnp.int32, (L, 2 * L), 0)
    kj = lax.broadcasted_iota(jnp.int32, (L, 2 * L), 1)
    dist = qi + L - kj
    return (dist >= 0) & (dist < L) & ((kj >= L) | (n > 0))


def _att_in_specs(nb):
    last = nb - 1
    cur = lambda n: jnp.minimum(n, last)
    prev = lambda n: jnp.maximum(jnp.minimum(n, last) - 1, 0)
    return [
        pl.BlockSpec((L, 1024), lambda n: (cur(n), 0)),
        pl.BlockSpec((L, 128), lambda n: (prev(n), 16)),
        pl.BlockSpec((L, 128), lambda n: (cur(n), 16)),
        pl.BlockSpec((L, 128), lambda n: (prev(n), 17)),
        pl.BlockSpec((L, 128), lambda n: (cur(n), 17)),
        pl.BlockSpec((L, 1024), lambda n: (cur(n), 1)),
        _full((HEADS, L, 2 * L)),
        pl.BlockSpec(memory_space=pltpu.SMEM),
    ]


def _att_probs(qh, kk, bias_h, mask, sk):
    logits = _dot_nt(qh, kk) + bias_h
    logits = jnp.where(mask, logits, NEG)
    m = jnp.maximum(jnp.max(logits, axis=-1, keepdims=True), sk)
    p = jnp.exp(logits - m)
    es = jnp.exp(sk - m)
    den = jnp.sum(p, axis=-1, keepdims=True) + es
    return p / den, es / den


def att_fwd(proj, bias, sinks):
    S = proj.shape[0]
    nb = S // L

    def body(q_ref, kp_ref, kc_ref, vp_ref, vc_ref, z_ref, bias_ref, s_ref, y_ref, o_scr):
        mask = _att_mask(pl.program_id(0))
        for kv in range(KV):
            sl = slice(kv * DH, (kv + 1) * DH)
            kk = jnp.concatenate([kp_ref[:, sl], kc_ref[:, sl]], axis=0).astype(BF16)
            vv = jnp.concatenate([vp_ref[:, sl], vc_ref[:, sl]], axis=0).astype(BF16)
            for g in range(HEADS // KV):
                h = kv * (HEADS // KV) + g
                hs = slice(h * DH, (h + 1) * DH)
                qh = (q_ref[:, hs] * 0.125).astype(BF16)
                P, _ = _att_probs(qh, kk, bias_ref[h], mask, s_ref[h])
                o_scr[:, hs] = _dot(P.astype(BF16), vv)
        z = z_ref[...]
        y_ref[...] = (o_scr[...] * (z * _sigmoid(z))).astype(BF16)

    return pl.pallas_call(
        body, name="att_fwd", grid=(nb,),
        in_specs=_att_in_specs(nb),
        out_specs=pl.BlockSpec((L, 1024), lambda n: (n, 0)),
        out_shape=jax.ShapeDtypeStruct((S, 1024), BF16),
        scratch_shapes=[pltpu.VMEM((L, 1024), F32)],
        compiler_params=_params(("arbitrary",)),
    )(proj, proj, proj, proj, proj, proj, bias, sinks)


def att_bwd(dy, proj, bias, sinks):
    S = proj.shape[0]
    nb = S // L
    last = nb - 1

    def body(dy_ref, q_ref, kp_ref, kc_ref, vp_ref, vc_ref, z_ref, bias_ref, s_ref,
             dout_ref, dbias_ref, dsink_ref, carry, band, dq_scr, dz_scr):
        n = pl.program_id(0)

        @pl.when(n == 0)
        def _():
            carry[...] = jnp.zeros_like(carry)
            dq_scr[...] = jnp.zeros_like(dq_scr)
            dz_scr[...] = jnp.zeros_like(dz_scr)
            dbias_ref[...] = jnp.zeros_like(dbias_ref)
            dsink_ref[...] = jnp.zeros_like(dsink_ref)

        dout_ref[:, 0:1024] = dq_scr[...].astype(BF16)
        dout_ref[:, 1024:2048] = dz_scr[...].astype(BF16)
        band[...] = jnp.zeros_like(band)

        @pl.when(n < nb)
        def _():
            mask = _att_mask(n)
            lane = lax.broadcasted_iota(jnp.int32, (1, 128), 1)
            dsink = jnp.zeros((1, 128), F32)
            for kv in range(KV):
                sl = slice(kv * DH, (kv + 1) * DH)
                kk = jnp.concatenate([kp_ref[:, sl], kc_ref[:, sl]], axis=0).astype(BF16)
                vv = jnp.concatenate([vp_ref[:, sl], vc_ref[:, sl]], axis=0).astype(BF16)
                dk_acc = jnp.zeros((2 * L, DH), F32)
                dv_acc = jnp.zeros((2 * L, DH), F32)
                for g in range(HEADS // KV):
                    h = kv * (HEADS // KV) + g
                    hs = slice(h * DH, (h + 1) * DH)
                    qh = (q_ref[:, hs] * 0.125).astype(BF16)
                    P, psink = _att_probs(qh, kk, bias_ref[h], mask, s_ref[h])
                    Pb = P.astype(BF16)
                    zh = z_ref[:, hs]
                    sg = _sigmoid(zh)
                    dyh = dy_ref[:, hs]
                    O = _dot(Pb, vv)
                    dO = dyh * (zh * sg)
                    dz_scr[:, hs] = dyh * O * (sg * (1.0 + zh * (1.0 - sg)))
                    dOb = dO.astype(BF16)
                    dv_acc = dv_acc + _dot_tn(P, dOb)
                    dP = _dot_nt(dOb, vv)
                    delta = jnp.sum(P * dP, axis=-1, keepdims=True)
                    dS = P * (dP - delta)
                    dsink = dsink + jnp.where(lane == h, -jnp.sum(psink * delta), 0.0)
                    dSb = dS.astype(BF16)
                    dq_scr[:, hs] = _dot(dSb, kk) * 0.125
                    dk_acc = dk_acc + _dot_tn(dS, qh)
                    dbias_ref[h] += dS
                band[:, sl] = dk_acc
                band[:, 128 + kv * DH:128 + (kv + 1) * DH] = dv_acc
            dsink_ref[...] += dsink

        out = carry[...] + band[0:L, :]
        dout_ref[:, 2048:2304] = out.astype(BF16)
        carry[...] = band[L:2 * L, :]

    cur = lambda n: jnp.minimum(n, last)
    lag = lambda n: jnp.maximum(n - 1, 0)
    return pl.pallas_call(
        body, name="att_bwd", grid=(nb + 1,),
        in_specs=[pl.BlockSpec((L, 1024), lambda n: (cur(n), 0))] + _att_in_specs(nb),
        out_specs=[pl.BlockSpec((L, 2304), lambda n: (lag(n), 0)), _full((HEADS, L, 2 * L)), _full((1, 128))],
        out_shape=[jax.ShapeDtypeStruct((S, 2304), BF16),
                   jax.ShapeDtypeStruct((HEADS, L, 2 * L), F32), jax.ShapeDtypeStruct((1, 128), F32)],
        scratch_shapes=[pltpu.VMEM((L, 256), F32), pltpu.VMEM((2 * L, 256), F32),
                        pltpu.VMEM((L, 1024), F32), pltpu.VMEM((L, 1024), F32)],
        compiler_params=_params(("arbitrary",)),
    )(dy, proj, proj, proj, proj, proj, proj, bias, sinks)


def _sgu_in_specs():
    return [
        pl.BlockSpec((L, 1024), lambda c: (c, 0)),
        pl.BlockSpec((L, 1024), lambda c: (c, 1)),
        pl.BlockSpec((L, 1024), lambda c: (c, 2)),
        _full((1, 1024)), _full((1, 1024)), _full((8, L, L)), _full((L, 8)),
    ]


def _sgu_norm(v, lg, lb):
    mu = jnp.mean(v, axis=-1, keepdims=True)
    vc = v - mu
    rstd = lax.rsqrt(jnp.mean(vc * vc, axis=-1, keepdims=True) + EPS)
    xhat = vc * rstd
    return xhat * lg + lb, xhat, rstd


def _tril():
    return lax.broadcasted_iota(jnp.int32, (L, L), 0) >= lax.broadcasted_iota(jnp.int32, (L, L), 1)


def sgu_fwd(proj, ln_g, ln_b, w, b_t):
    S = proj.shape[0]

    def body(u_ref, v_ref, z_ref, lg_ref, lb_ref, w_ref, bt_ref, y_ref):
        vn, _, _ = _sgu_norm(v_ref[...], lg_ref[...], lb_ref[...])
        tri = _tril()
        parts = []
        for g in range(8):
            wg = jnp.where(tri, w_ref[g], 0.0).astype(BF16)
            parts.append(_dot(wg, vn[:, g * 128:(g + 1) * 128].astype(BF16)) + bt_ref[:, g:g + 1])
        mixed = jnp.concatenate(parts, axis=1)
        z = z_ref[...]
        y_ref[...] = (u_ref[...] * mixed * (z * _sigmoid(z))).astype(BF16)

    return pl.pallas_call(
        body, name="sgu_fwd", grid=(S // L,),
        in_specs=_sgu_in_specs(),
        out_specs=pl.BlockSpec((L, 1024), lambda c: (c, 0)),
        out_shape=jax.ShapeDtypeStruct((S, 1024), BF16),
        compiler_params=_params(("arbitrary",)),
    )(proj, proj, proj, ln_g, ln_b, w, b_t)


def sgu_bwd(dy, proj, ln_g, ln_b, w, b_t):
    S = proj.shape[0]

    def body(dy_ref, u_ref, v_ref, z_ref, lg_ref, lb_ref, w_ref, bt_ref,
             dout_ref, dw_ref, dbt_ref, dlg_ref, dlb_ref):
        @pl.when(pl.program_id(0) == 0)
        def _():
            dw_ref[...] = jnp.zeros_like(dw_ref)
            dbt_ref[...] = jnp.zeros_like(dbt_ref)
            dlg_ref[...] = jnp.zeros_like(dlg_ref)
            dlb_ref[...] = jnp.zeros_like(dlb_ref)

        lg = lg_ref[...]
        vn, xhat, rstd = _sgu_norm(v_ref[...], lg, lb_ref[...])
        tri = _tril()
        lane = lax.broadcasted_iota(jnp.int32, (L, 128), 1)
        wgs, parts = [], []
        for g in range(8):
            wg = jnp.where(tri, w_ref[g], 0.0)
            wgs.append(wg)
            parts.append(_dot(wg.astype(BF16), vn[:, g * 128:(g + 1) * 128].astype(BF16)) + bt_ref[:, g:g + 1])
        mixed = jnp.concatenate(parts, axis=1)
        z = z_ref[...]
        sg = _sigmoid(z)
        silu = z * sg
        dy_v = dy_ref[...]
        u = u_ref[...]
        dout_ref[:, 0:1024] = (dy_v * mixed * silu).astype(BF16)
        dout_ref[:, 2048:3072] = (dy_v * u * mixed * (sg * (1.0 + z * (1.0 - sg)))).astype(BF16)
        dmixed = dy_v * u * silu
        dbt = jnp.zeros((L, 128), F32)
        dvn_parts = []
        for g in range(8):
            dm = dmixed[:, g * 128:(g + 1) * 128]
            dmb = dm.astype(BF16)
            dbt = dbt + jnp.where(lane == g, jnp.sum(dm, axis=1, keepdims=True), 0.0)
            dw_ref[g] += jnp.where(tri, _dot_nt(dmb, vn[:, g * 128:(g + 1) * 128].astype(BF16)), 0.0)
            dvn_parts.append(_dot_tn(wgs[g], dmb))
        dbt_ref[...] += dbt
        dvn = jnp.concatenate(dvn_parts, axis=1)
        dlg_ref[...] += jnp.sum(dvn * xhat, axis=0, keepdims=True)
        dlb_ref[...] += jnp.sum(dvn, axis=0, keepdims=True)
        dxh = dvn * lg
        dv = rstd * (dxh - jnp.mean(dxh, axis=-1, keepdims=True)
                     - xhat * jnp.mean(dxh * xhat, axis=-1, keepdims=True))
        dout_ref[:, 1024:2048] = dv.astype(BF16)

    return pl.pallas_call(
        body, name="sgu_bwd", grid=(S // L,),
        in_specs=[pl.BlockSpec((L, 1024), lambda c: (c, 0))] + _sgu_in_specs(),
        out_specs=[pl.BlockSpec((L, 3072), lambda c: (c, 0)), _full((8, L, L)), _full((L, 128)),
                   _full((1, 1024)), _full((1, 1024))],
        out_shape=[jax.ShapeDtypeStruct((S, 3072), BF16), jax.ShapeDtypeStruct((8, L, L), F32),
                   jax.ShapeDtypeStruct((L, 128), F32), jax.ShapeDtypeStruct((1, 1024), F32),
                   jax.ShapeDtypeStruct((1, 1024), F32)],
        compiler_params=_params(("arbitrary",)),
    )(dy, proj, proj, proj, ln_g, ln_b, w, b_t)


def _expand_matrix():
    r = lax.broadcasted_iota(jnp.int32, (128, SSM_W), 0)
    c = lax.broadcasted_iota(jnp.int32, (128, SSM_W), 1)
    return (c // SSM_P) == r


def _expand_matrix_t():
    r = lax.broadcasted_iota(jnp.int32, (SSM_W, 128), 0)
    c = lax.broadcasted_iota(jnp.int32, (SSM_W, 128), 1)
    return (r // SSM_P) == c


def _rows_from(ref, start):
    C = ref.shape[1]
    tiles = ref[...].reshape(17, 8, C)
    q, s = divmod(start, 8)
    if s == 0:
        return tiles[q:q + 16].reshape(L, C)
    rolled = pltpu.roll(tiles, 8 - s, axis=1)
    sub = lax.broadcasted_iota(jnp.int32, (16, 8, C), 1)
    return jnp.where(sub < 8 - s, rolled[q:q + 16], rolled[q + 1:q + 17]).reshape(L, C)


def _ssd_common(ext_ref, cw_ref, cb_ref, dt_raw, dtb, alog):
    taps = [_rows_from(ext_ref, 5 + k) for k in range(CONV_K)]
    pre = cb_ref[...]
    for k in range(CONV_K):
        pre = pre + cw_ref[k:k + 1, :] * taps[k]
    sg_pre = _sigmoid(pre)
    xc = pre * sg_pre
    dt = _softplus(dt_raw + dtb)
    a = -jnp.exp(alog)
    adt = dt * a
    acs = _sel_dot(_tril(), adt, 3)
    return pre, sg_pre, xc, dt, a, acs, taps


def _ssd_in_specs(rev, nc):
    cidx = (lambda c: nc - 1 - c) if rev else (lambda c: c)
    return [
        pl.BlockSpec((L, 2048), lambda c: (cidx(c), 0)),
        pl.BlockSpec((L, 1024), lambda c: (cidx(c), 2)),
        pl.BlockSpec((L, 1024), lambda c: (cidx(c), 3)),
        pl.BlockSpec((L, 1024), lambda c: (cidx(c), 4)),
        pl.BlockSpec((L, 128), lambda c: (cidx(c), 40)),
        _full((8, CONV_C)), _full((1, CONV_C)), _full((1, 128)), _full((1, 128)), _full((1, 128)),
        _full((1, SSM_W)),
    ]


def ssd_fwd(proj, conv_w, conv_b, dt_bias, a_log, d_skip, norm_g):
    S = proj.shape[0]
    nc = S // L

    def body(z_ref, xa_ref, xb_ref, xc_ref, dt_ref, cw_ref, cb_ref, dtb_ref, alog_ref, dsk_ref, ng_ref,
             y_ref, hs_ref, H, ext, ysc):
        @pl.when(pl.program_id(0) == 0)
        def _():
            H[...] = jnp.zeros_like(H)
            ext[0:8, :] = jnp.zeros((8, CONV_C), F32)

        for k, ref in enumerate((xa_ref, xb_ref, xc_ref)):
            ext[8:8 + L, k * 1024:(k + 1) * 1024] = ref[...]
        pre, sg_pre, xc, dt, a, acs, _ = _ssd_common(ext, cw_ref, cb_ref, dt_ref[...], dtb_ref[...], alog_ref[...])
        for k, ref in enumerate((xa_ref, xb_ref, xc_ref)):
            ext[0:8, k * 1024:(k + 1) * 1024] = ref[L - 8:L, :]
        xs = xc[:, 0:SSM_W]
        acs_t = acs.T
        ex = _expand_matrix()
        dt_x = _dot_sel(dt, ex, 2)
        xdt = xs * dt_x
        eacs_x = _dot_sel(jnp.exp(acs), ex, 2)
        xw = xdt * _dot_sel(jnp.exp(acs[L - 1:L, :] - acs), ex, 2)
        cd_row = jnp.exp(acs[L - 1:L, :])
        hs_ref[0] = H[...]
        tri = _tril()
        for g in range(SSM_G):
            gs = slice(g * 512, (g + 1) * 512)
            bg = xc[:, SSM_W + g * SSM_N:SSM_W + (g + 1) * SSM_N].astype(BF16)
            cg = xc[:, SSM_W + 512 + g * SSM_N:SSM_W + 512 + (g + 1) * SSM_N].astype(BF16)
            G = _dot_nt(cg, bg)
            yoff = _dot_nt(cg, H[gs, :].astype(BF16)) * eacs_x[:, gs]
            Sg = _dot_tn(xw[:, gs], bg)
            for j in range(8):
                hh = g * 8 + j
                hs = slice(hh * SSM_P, (hh + 1) * SSM_P)
                seg = acs[:, hh:hh + 1] - acs_t[hh:hh + 1, :]
                dk = jnp.where(tri, jnp.exp(jnp.minimum(seg, 0.0)), 0.0)
                yd = _dot((G * dk).astype(BF16), xdt[:, hs].astype(BF16))
                ysc[:, hs] = yd + yoff[:, j * SSM_P:(j + 1) * SSM_P]
                H[hs, :] = H[hs, :] * cd_row[:, hh:hh + 1] + Sg[j * SSM_P:(j + 1) * SSM_P, :]
        d_x = _dot_sel(jnp.broadcast_to(dsk_ref[...], (8, 128)), ex, 3)[0:1, :]
        Y = ysc[...] + d_x * xs
        z = z_ref[...]
        yz = Y * (z * _sigmoid(z))
        ng = ng_ref[...]
        for g in range(SSM_G):
            gs = slice(g * 512, (g + 1) * 512)
            t = yz[:, gs]
            rstd = lax.rsqrt(jnp.mean(t * t, axis=-1, keepdims=True) + EPS)
            y_ref[:, gs] = (t * rstd * ng[:, gs]).astype(BF16)

    return pl.pallas_call(
        body, name="ssd_fwd", grid=(nc,),
        in_specs=_ssd_in_specs(False, nc),
        out_specs=[pl.BlockSpec((L, SSM_W), lambda c: (c, 0)), pl.BlockSpec((1, SSM_W, SSM_N), lambda c: (c, 0, 0))],
        out_shape=[jax.ShapeDtypeStruct((S, SSM_W), BF16), jax.ShapeDtypeStruct((nc, SSM_W, SSM_N), F32)],
        scratch_shapes=[pltpu.VMEM((SSM_W, SSM_N), F32), pltpu.VMEM((8 + L, CONV_C), F32),
                        pltpu.VMEM((L, SSM_W), F32)],
        compiler_params=_params(("arbitrary",)),
    )(proj, proj, proj, proj, proj, conv_w, conv_b, dt_bias, a_log, d_skip, norm_g)


def ssd_bwd(dy, proj, hstates, conv_w, conv_b, dt_bias, a_log, d_skip, norm_g):
    S = proj.shape[0]
    nc = S // L
    cidx = lambda c: nc - 1 - c

    def body(dy_ref, z_ref, xa_ref, xb_ref, xc_ref, dt_ref, cw_ref, cb_ref, dtb_ref, alog_ref, dsk_ref, ng_ref,
             pa_ref, pb_ref, pc_ref, hp_ref,
             dout_ref, dcw_ref, dcb_ref, ddtb_ref, dalog_ref, ddsk_ref, dng_ref,
             dH, ext, dext, ysc, yoffsc, dxdt, dxc, tsc):
        step = pl.program_id(0)
        c = nc - 1 - step

        @pl.when(step == 0)
        def _():
            dH[...] = jnp.zeros_like(dH)
            dext[L:L + 8, :] = jnp.zeros((8, CONV_C), F32)
            for r in (dcw_ref, dcb_ref, ddtb_ref, dalog_ref, ddsk_ref, dng_ref):
                r[...] = jnp.zeros_like(r)

        for k, (ref, prev) in enumerate(((xa_ref, pa_ref), (xb_ref, pb_ref), (xc_ref, pc_ref))):
            ext[0:8, k * 1024:(k + 1) * 1024] = jnp.where(c > 0, prev[...], 0.0)
            ext[8:8 + L, k * 1024:(k + 1) * 1024] = ref[...]
        dtb = dtb_ref[...]
        dt_raw = dt_ref[...]
        pre, sg_pre, xc, dt, a, acs, taps = _ssd_common(ext, cw_ref, cb_ref, dt_raw, dtb, alog_ref[...])
        xs = xc[:, 0:SSM_W]
        acs_t = acs.T
        ex = _expand_matrix()
        dt_x = _dot_sel(dt, ex, 2)
        xdt = xs * dt_x
        eacs_x = _dot_sel(jnp.exp(acs), ex, 2)
        dte_x = _dot_sel(jnp.exp(acs[L - 1:L, :] - acs), ex, 2)
        xw = xdt * dte_x
        cd_row = jnp.exp(acs[L - 1:L, :])
        tri = _tril()

        Gs, Cs, Bs = [], [], []
        for g in range(SSM_G):
            gs = slice(g * 512, (g + 1) * 512)
            bg = xc[:, SSM_W + g * SSM_N:SSM_W + (g + 1) * SSM_N].astype(BF16)
            cg = xc[:, SSM_W + 512 + g * SSM_N:SSM_W + 512 + (g + 1) * SSM_N].astype(BF16)
            G = _dot_nt(cg, bg)
            Gs.append(G), Cs.append(cg), Bs.append(bg)
            yoffsc[:, gs] = _dot_nt(cg, hp_ref[0, gs, :].astype(BF16)) * eacs_x[:, gs]
            for j in range(8):
                hh = g * 8 + j
                hs = slice(hh * SSM_P, (hh + 1) * SSM_P)
                seg = acs[:, hh:hh + 1] - acs_t[hh:hh + 1, :]
                dk = jnp.where(tri, jnp.exp(jnp.minimum(seg, 0.0)), 0.0)
                ysc[:, hs] = _dot((G * dk).astype(BF16), xdt[:, hs].astype(BF16))
        d_x = _dot_sel(jnp.broadcast_to(dsk_ref[...], (8, 128)), ex, 3)[0:1, :]
        yoff = yoffsc[...]
        Y = ysc[...] + yoff + d_x * xs

        z = z_ref[...]
        sgz = _sigmoid(z)
        silu_z = z * sgz
        yz = Y * silu_z
        ng = ng_ref[...]
        dout = dy_ref[...]
        dyn = dout * ng
        dyz_parts, dng_parts = [], []
        for g in range(SSM_G):
            gs = slice(g * 512, (g + 1) * 512)
            t = yz[:, gs]
            rstd = lax.rsqrt(jnp.mean(t * t, axis=-1, keepdims=True) + EPS)
            dng_parts.append(jnp.sum(dout[:, gs] * t * rstd, axis=0, keepdims=True))
            dn = dyn[:, gs]
            dyz_parts.append(rstd * dn - t * (rstd * rstd * rstd) * jnp.mean(dn * t, axis=-1, keepdims=True))
        dng_ref[...] += jnp.concatenate(dng_parts, axis=1)
        dyz = jnp.concatenate(dyz_parts, axis=1)
        dY = dyz * silu_z
        dout_ref[:, 0:SSM_W] = (dyz * Y * (sgz * (1.0 + z * (1.0 - sgz)))).astype(BF16)

        ex_t = _expand_matrix_t()
        ddsk_ref[...] += _dot_sel(jnp.broadcast_to(jnp.sum(dY * xs, axis=0, keepdims=True), (8, SSM_W)), ex_t, 3)[0:1, :]

        lane = lax.broadcasted_iota(jnp.int32, (L, 128), 1)
        subl = lax.broadcasted_iota(jnp.int32, (128, L), 0)
        coll = lax.broadcasted_iota(jnp.int32, (128, L), 1)
        r_cols = jnp.zeros((L, 128), F32)
        c_rows = jnp.zeros((128, L), F32)
        for g in range(SSM_G):
            gs = slice(g * 512, (g + 1) * 512)
            G, cg, bg = Gs[g], Cs[g], Bs[g]
            hp_g = hp_ref[0, gs, :]
            dh_g = dH[gs, :]
            dY_g = dY[:, gs]
            dZ = dY_g * eacs_x[:, gs]
            dZb = dZ.astype(BF16)
            dC = _dot(dZb, hp_g.astype(BF16))
            dh_from_off = _dot_tn(dZ, cg)
            dhb = dh_g.astype(BF16)
            Q = _dot_nt(bg, dhb)
            dB = _dot(xw[:, gs].astype(BF16), dhb)
            qd = Q * dte_x[:, gs]
            dxdt[:, gs] = qd
            tsc[:, gs] = qd * xdt[:, gs]
            dG = jnp.zeros((L, L), F32)
            for j in range(8):
                hh = g * 8 + j
                hs = slice(hh * SSM_P, (hh + 1) * SSM_P)
                seg = acs[:, hh:hh + 1] - acs_t[hh:hh + 1, :]
                dk = jnp.where(tri, jnp.exp(jnp.minimum(seg, 0.0)), 0.0)
                M = G * dk
                dYh = dY[:, hs]
                dYhb = dYh.astype(BF16)
                dM = _dot_nt(dYhb, xdt[:, hs].astype(BF16))
                dxdt[:, hs] += _dot_tn(M, dYhb)
                dG = dG + dM * dk
                Wm = dM * M
                r_cols = r_cols + jnp.where(lane == hh, jnp.sum(Wm, axis=1, keepdims=True), 0.0)
                c_rows = c_rows + jnp.where(subl == hh, jnp.sum(Wm, axis=0, keepdims=True), 0.0)
                pj = slice(j * SSM_P, (j + 1) * SSM_P)
                cd_h = cd_row[:, hh:hh + 1]
                dcd = jnp.sum(dh_g[pj, :] * hp_g[pj, :]) * cd_h
                c_rows = c_rows - jnp.where((subl == hh) & (coll == L - 1), dcd, 0.0)
                dH[hs, :] = dh_g[pj, :] * cd_h + dh_from_off[pj, :]
            dGb = dG.astype(BF16)
            dC = dC + _dot(dGb, bg)
            dB = dB + _dot_tn(dG, cg)
            dxc[:, SSM_W + g * SSM_N:SSM_W + (g + 1) * SSM_N] = dB
            dxc[:, SSM_W + 512 + g * SSM_N:SSM_W + 512 + (g + 1) * SSM_N] = dC

        row = lax.broadcasted_iota(jnp.int32, (L, 128), 0)
        tv = tsc[...]
        t_last = _dot_sel(jnp.broadcast_to(jnp.sum(tv, axis=0, keepdims=True), (8, SSM_W)), ex_t, 3)[0:1, :]
        dacs = (r_cols - c_rows.T + _dot_sel(dY * yoff - tv, ex_t, 2) + jnp.where(row == L - 1, t_last, 0.0))
        triu = lax.broadcasted_iota(jnp.int32, (L, L), 0) <= lax.broadcasted_iota(jnp.int32, (L, L), 1)
        dadt = _sel_dot(triu, dacs, 3)
        dxdt_v = dxdt[...]
        ddt = _dot_sel(dxdt_v * xs, ex_t, 2) + dadt * a
        dalog_ref[...] += jnp.sum(dadt * dt * a, axis=0, keepdims=True)
        ddt_raw = jnp.where(lane < SSM_H, ddt * _sigmoid(dt_raw + dtb), 0.0)
        ddtb_ref[...] += jnp.sum(ddt_raw, axis=0, keepdims=True)
        dout_ref[:, 5120:5248] = ddt_raw.astype(BF16)
        dout_ref[:, 5248:5376] = jnp.zeros((L, 128), BF16)

        dxc[:, 0:SSM_W] = dxdt_v * dt_x + d_x * dY
        dpre = dxc[...] * (sg_pre * (1.0 + pre * (1.0 - sg_pre)))
        dcb_ref[...] += jnp.sum(dpre, axis=0, keepdims=True)
        dext[0:L, :] = dpre
        x_cur = ext[8:8 + L, :]
        dx = None
        for k in range(CONV_K):
            dsh = _rows_from(dext, 3 - k)
            term = cw_ref[k:k + 1, :] * dsh
            dx = term if dx is None else dx + term
            dcw_ref[k:k + 1, :] += jnp.sum(dsh * x_cur, axis=0, keepdims=True)
        dout_ref[:, SSM_W:SSM_W + CONV_C] = dx.astype(BF16)
        dext[L:L + 8, :] = dpre[0:8, :]

    big = lambda w: pl.BlockSpec((L, w), lambda c: (cidx(c), 0))
    return pl.pallas_call(
        body, name="ssd_bwd", grid=(nc,),
        in_specs=[big(SSM_W)] + _ssd_in_specs(True, nc) + [
            pl.BlockSpec((8, 1024), lambda c, k=k: (jnp.maximum(16 * cidx(c) - 1, 0), k)) for k in (2, 3, 4)] + [
            pl.BlockSpec((1, SSM_W, SSM_N), lambda c: (cidx(c), 0, 0))],
        out_specs=[big(5376), _full((8, CONV_C)), _full((1, CONV_C)),
                   _full((1, 128)), _full((1, 128)), _full((1, 128)), _full((1, SSM_W))],
        out_shape=[jax.ShapeDtypeStruct((S, 5376), BF16), jax.ShapeDtypeStruct((8, CONV_C), F32),
                   jax.ShapeDtypeStruct((1, CONV_C), F32), jax.ShapeDtypeStruct((1, 128), F32),
                   jax.ShapeDtypeStruct((1, 128), F32), jax.ShapeDtypeStruct((1, 128), F32),
                   jax.ShapeDtypeStruct((1, SSM_W), F32)],
        scratch_shapes=[pltpu.VMEM((SSM_W, SSM_N), F32), pltpu.VMEM((8 + L, CONV_C), F32),
                        pltpu.VMEM((L + 8, CONV_C), F32), pltpu.VMEM((L, SSM_W), F32),
                        pltpu.VMEM((L, SSM_W), F32), pltpu.VMEM((L, SSM_W), F32),
                        pltpu.VMEM((L, CONV_C), F32), pltpu.VMEM((L, SSM_W), F32)],
        compiler_params=_params(("arbitrary",)),
    )(dy, proj, proj, proj, proj, proj, conv_w, conv_b, dt_bias, a_log, d_skip, norm_g, proj, proj, proj, hstates)


def _resident(shape):
    nd = len(shape)
    return pl.BlockSpec(shape, lambda *_: (0,) * nd, pipeline_mode=pl.Buffered(1))


def merge_fwd(y_att, y_sg, y_ssm, proj, x, w_a, w_s, w_m, w_o, g_post):
    S = x.shape[0]
    tm = 256

    def body(ya_ref, ys_ref, ym_ref, gate_ref, x_ref, wa_ref, ws_ref, wm_ref, wo_ref, gp_ref,
             xn_ref, bra_ref, brs_ref, brm_ref, mg_ref, out_ref):
        bra = _dot(ya_ref[...], wa_ref[...])
        brs = _dot(ys_ref[...], ws_ref[...])
        brm = _dot(ym_ref[...], wm_ref[...])
        bra_ref[...] = bra
        brs_ref[...] = brs
        brm_ref[...] = brm
        merged = (_sigmoid(gate_ref[:, 0:1024]) * bra + _sigmoid(gate_ref[:, 1024:2048]) * brs
                  + _sigmoid(gate_ref[:, 2048:3072]) * brm)
        mb = merged.astype(BF16)
        mg_ref[...] = mb
        o = _dot(mb, wo_ref[...])
        out_ref[...] = o
        r = lax.rsqrt(jnp.mean(o * o, axis=-1, keepdims=True) + EPS)
        xn_ref[...] = x_ref[...] + o * r * gp_ref[...]

    row = lambda w: pl.BlockSpec((tm, w), lambda i: (i, 0))
    return pl.pallas_call(
        body, name="merge_fwd", grid=(S // tm,),
        in_specs=[row(1024), row(1024), row(2048), pl.BlockSpec((tm, 3072), lambda i: (i, 0)),
                  row(D), _resident((1024, D)), _resident((1024, D)), _resident((2048, D)), _resident((D, D)),
                  _full((1, D))],
        out_specs=[row(D)] * 6,
        out_shape=[jax.ShapeDtypeStruct((S, D), F32)] * 4 + [jax.ShapeDtypeStruct((S, D), BF16),
                                                             jax.ShapeDtypeStruct((S, D), F32)],
        compiler_params=_params(("arbitrary",)),
    )(y_att, y_sg, y_ssm, proj, x, w_a, w_s, w_m, w_o, g_post)


def merge_bwd(dy, out, g_post, proj, br_a, br_s, br_m, w_a, w_s, w_m, w_o):
    S = dy.shape[0]
    tm = 256

    def body(dy_ref, o_ref, gp_ref, gate_ref, bra_ref, brs_ref, brm_ref, wa_ref, ws_ref, wm_ref, wo_ref,
             dout_ref, dba_ref, dbs_ref, dbm_ref, dgate_ref, dya_ref, dys_ref, dym_ref, dgp_ref):
        @pl.when(pl.program_id(0) == 0)
        def _():
            dgp_ref[...] = jnp.zeros_like(dgp_ref)

        o = o_ref[...]
        dyv = dy_ref[...]
        r = lax.rsqrt(jnp.mean(o * o, axis=-1, keepdims=True) + EPS)
        dyg = dyv * gp_ref[...]
        do = r * dyg - o * (r * r * r) * jnp.mean(dyg * o, axis=-1, keepdims=True)
        dgp_ref[...] += jnp.sum(dyv * o * r, axis=0, keepdims=True)
        dob = do.astype(BF16)
        dout_ref[...] = dob
        dmerged = _dot_nt(dob, wo_ref[...])
        for idx, (br_ref, dbr_ref, w_ref, dyi_ref) in enumerate((
                (bra_ref, dba_ref, wa_ref, dya_ref), (brs_ref, dbs_ref, ws_ref, dys_ref),
                (brm_ref, dbm_ref, wm_ref, dym_ref))):
            s = _sigmoid(gate_ref[:, idx * 1024:(idx + 1) * 1024])
            dbr = (dmerged * s).astype(BF16)
            dbr_ref[...] = dbr
            dgate_ref[:, idx * 1024:(idx + 1) * 1024] = (dmerged * br_ref[...] * s * (1.0 - s)).astype(BF16)
            dyi_ref[...] = _dot_nt(dbr, w_ref[...])

    row = lambda w: pl.BlockSpec((tm, w), lambda i: (i, 0))
    return pl.pallas_call(
        body, name="merge_bwd", grid=(S // tm,),
        in_specs=[row(D), row(D), _full((1, D)), pl.BlockSpec((tm, 3072), lambda i: (i, 0)),
                  row(D), row(D), row(D),
                  _resident((1024, D)), _resident((1024, D)), _resident((2048, D)), _resident((D, D))],
        out_specs=[row(D), row(D), row(D), row(D), row(3072), row(1024), row(1024), row(2048), _full((1, D))],
        out_shape=[jax.ShapeDtypeStruct((S, D), BF16)] * 4 + [
            jax.ShapeDtypeStruct((S, 3072), BF16), jax.ShapeDtypeStruct((S, 1024), F32),
            jax.ShapeDtypeStruct((S, 1024), F32), jax.ShapeDtypeStruct((S, 2048), F32),
            jax.ShapeDtypeStruct((1, D), F32)],
        compiler_params=_params(("arbitrary",)),
    )(dy, out, g_post, proj, br_a, br_s, br_m, w_a, w_s, w_m, w_o)


def loss_head(y, target):
    S = y.shape[0]
    tm = 512

    def body(y_ref, t_ref, dy_ref, loss_ref):
        @pl.when(pl.program_id(0) == 0)
        def _():
            loss_ref[...] = jnp.zeros_like(loss_ref)
        e = y_ref[...] - t_ref[...]
        dy_ref[...] = e * (1.0 / D)
        loss_ref[...] += 0.5 * jnp.sum(jnp.mean(e * e, axis=-1, keepdims=True))

    row = pl.BlockSpec((tm, D), lambda i: (i, 0))
    return pl.pallas_call(
        body, name="loss_head", grid=(S // tm,),
        in_specs=[row, row], out_specs=[row, _full((1, 128))],
        out_shape=[jax.ShapeDtypeStruct((S, D), F32), jax.ShapeDtypeStruct((1, 128), F32)],
        compiler_params=_params(("arbitrary",)),
    )(y, target)


def _adam(w, g, m, v):
    mn = ADAM_B1 * m + (1.0 - ADAM_B1) * g
    vn = ADAM_B2 * v + (1.0 - ADAM_B2) * (g * g)
    m_hat = mn / (1.0 - ADAM_B1 ** ADAM_STEP)
    v_hat = vn / (1.0 - ADAM_B2 ** ADAM_STEP)
    return -ADAM_LR * (m_hat / (jnp.sqrt(v_hat) + ADAM_EPS) + ADAM_WD * w), mn, vn


def adamw_big(w, m, v, f, fb, cc, name, tr, f_row0=0):
    _, R, C = w.shape
    nper = R // tr
    foff = f_row0 // tr

    def body(c_ref, w_ref, m_ref, v_ref, f_ref, fb_ref, g_ref, d_ref, nm_ref, nv_ref):
        layer = pl.program_id(0) // nper
        g = jnp.where(c_ref[0] == layer, f_ref[...], fb_ref[...])
        g_ref[0] = g
        d_ref[0], nm_ref[0], nv_ref[0] = _adam(w_ref[0], g, m_ref[0], v_ref[0])

    wblk = pl.BlockSpec((1, tr, C), lambda i, c: (i // nper, i % nper, 0))
    fblk = pl.BlockSpec((tr, C), lambda i, c: (foff + i % nper, 0))
    grid_spec = pltpu.PrefetchScalarGridSpec(
        num_scalar_prefetch=1, grid=(2 * nper,),
        in_specs=[wblk, wblk, wblk, fblk, fblk], out_specs=[wblk] * 4)
    return pl.pallas_call(
        body, name=name, grid_spec=grid_spec,
        out_shape=[jax.ShapeDtypeStruct(w.shape, F32)] * 4,
        compiler_params=_params(("arbitrary",)),
    )(cc, w, m, v, f, fb)


def adamw_plain(w, g, m, v, name):
    def body(w_ref, g_ref, m_ref, v_ref, d_ref, nm_ref, nv_ref):
        d_ref[...], nm_ref[...], nv_ref[...] = _adam(w_ref[...], g_ref[...], m_ref[...], v_ref[...])

    return pl.pallas_call(
        body, name=name, out_shape=[jax.ShapeDtypeStruct(w.shape, F32)] * 3, compiler_params=_params(),
    )(w, g, m, v)


SMALL = {"norm_pre": ("g_pre", 8), "norm_post": ("g_post", 8), "att_sinks": ("sinks", 8), "sg_ln_g": ("ln_g", 8),
         "sg_ln_b": ("ln_b", 8), "sg_w": ("sg_w", 1024), "sg_b": ("sg_bt", 8), "ssm_conv_b": ("conv_b", 24),
         "ssm_dt_bias": ("dt_bias", 8), "ssm_a_log": ("a_log", 8), "ssm_d": ("d_skip", 8), "ssm_norm_g": ("norm_g", 16)}
SMALL_LAYER_ROWS = sum(r for _, r in SMALL.values())
REL_ROW = DEPTH * SMALL_LAYER_ROWS
LOSS_ROW = REL_ROW + 32
SMALL_ROWS = LOSS_ROW + 8


def _small_rows():
    rows, r = {}, 0
    for l in range(DEPTH):
        for name, (_, n) in SMALL.items():
            rows[(l, name)] = r
            r += n
    return rows


def adamw_small(red, rel, small):
    names = list(SMALL) + ["rel_bias"]
    params = dict(small, rel_bias=rel)
    rows = _small_rows()

    def grad_of(red_ref, l, name, n):
        r0 = rows[(l, name)]
        if name == "sg_b":
            return red_ref[r0:r0 + 8, :]
        if n < 128:
            return red_ref[r0:r0 + 1, 0:n]
        return jnp.concatenate([red_ref[r0 + j:r0 + j + 1, :] for j in range(n // 128)], axis=1)

    def body(red_ref, *refs):
        ins, outs = refs[:3 * len(names)], refs[3 * len(names):]
        for i, name in enumerate(names):
            w_ref, m_ref, v_ref = ins[3 * i:3 * i + 3]
            o = outs[4 * i:4 * i + 4]
            if name == "rel_bias":
                g = red_ref[REL_ROW:REL_ROW + 32, 0:16]
                o[0][...] = g
                o[1][...], o[2][...], o[3][...] = _adam(w_ref[...], g, m_ref[...], v_ref[...])
                continue
            for l in range(DEPTH):
                if name == "sg_w":
                    for grp in range(8):
                        r0 = rows[(l, name)] + grp * 128
                        g = red_ref[r0:r0 + 128, :]
                        o[0][l, grp] = g
                        o[1][l, grp], o[2][l, grp], o[3][l, grp] = _adam(w_ref[l, grp], g, m_ref[l, grp], v_ref[l, grp])
                elif name == "sg_b":
                    g = grad_of(red_ref, l, name, 128)
                    o[0][l] = g
                    o[1][l], o[2][l], o[3][l] = _adam(w_ref[l], g, m_ref[l], v_ref[l])
                else:
                    sl = slice(l, l + 1)
                    g = grad_of(red_ref, l, name, w_ref.shape[-1])
                    o[0][sl, :] = g
                    o[1][sl, :], o[2][sl, :], o[3][sl, :] = _adam(w_ref[sl, :], g, m_ref[sl, :], v_ref[sl, :])

    flat_in = [a for name in names for a in params[name]]
    out_shape = [jax.ShapeDtypeStruct(params[name][0].shape, F32) for name in names for _ in range(4)]
    res = pl.pallas_call(body, name="adamw_small", out_shape=out_shape, compiler_params=_params())(red, *flat_in)
    return {name: tuple(res[4 * i:4 * i + 4]) for i, name in enumerate(names)}


ANY = pl.BlockSpec(memory_space=pl.ANY)


def _place():
    x, y, c = lax.axis_index("x"), lax.axis_index("y"), lax.axis_index("c")
    others = [(1 - x, y), (x, 1 - y), (1 - x, 1 - y)]
    return x, y, c, others


def _rcopy(src, dst, ssem, rsem, to):
    return pltpu.make_async_remote_copy(src_ref=src, dst_ref=dst, send_sem=ssem, recv_sem=rsem,
                                        device_id=to, device_id_type=MESH)


def gather_weights(arrs):
    n = len(arrs)

    def body(*refs):
        srcs, outs, ssem, rsem = refs[:n], refs[n:2 * n], refs[2 * n], refs[2 * n + 1]
        x, y, c, others = _place()
        me = 2 * x + y
        sib = (x, y, 1 - c)
        first = [_rcopy(srcs[i].at[c], outs[i].at[c, me], ssem.at[6 * i + k], rsem.at[6 * i + k], (ox, oy, c))
                 for i in range(n) for k, (ox, oy) in enumerate(others)]
        for cp in first:
            cp.start()
        passed = []
        for k, (ox, oy) in enumerate(others):
            for i in range(n):
                slot = outs[i].at[c, 2 * ox + oy]
                _rcopy(slot, slot, ssem.at[6 * i + k], rsem.at[6 * i + k], sib).wait_recv()
                fw = _rcopy(slot, slot, ssem.at[6 * i + 3 + k], rsem.at[6 * i + 3 + k], sib)
                fw.start()
                passed.append(fw)
        for k, (ox, oy) in enumerate(others):
            for i in range(n):
                slot = outs[i].at[1 - c, 2 * ox + oy]
                _rcopy(slot, slot, ssem.at[6 * i + 3 + k], rsem.at[6 * i + 3 + k], sib).wait_recv()
        for cp in first + passed:
            cp.wait_send()

    return pl.pallas_call(
        body, name="gather_weights",
        in_specs=[ANY] * n, out_specs=[ANY] * n,
        out_shape=[jax.ShapeDtypeStruct((2, SHARDS) + a.shape[1:], a.dtype) for a in arrs],
        scratch_shapes=[pltpu.SemaphoreType.DMA((6 * n,)), pltpu.SemaphoreType.DMA((6 * n,))],
    )(*arrs)


HBM = pl.BlockSpec(memory_space=pltpu.HBM)
SEM = pl.BlockSpec(memory_space=pltpu.SEMAPHORE)
EFFECT = pltpu.SideEffectType.DATAFLOW_SIDE_EFFECTING


def _in_hbm(a):
    return pltpu.with_memory_space_constraint(a, pltpu.HBM)


def gather_start(srcs, after, name):
    n = len(srcs)
    lands = [_in_hbm(lax.empty((SHARDS,) + a.shape, a.dtype)) for a in srcs]

    na = len(after)

    def body(*refs):
        src, land = refs[:n], refs[n:2 * n]
        ssem, rsem, token = refs[2 * n + na], refs[2 * n + na + 1], refs[-1]
        x, y, c, others = _place()
        me = 2 * x + y
        for i in range(n):
            for k, (ox, oy) in enumerate(others):
                _rcopy(src[i], land[i].at[me], ssem.at[3 * i + k], rsem.at[3 * i + k], (ox, oy, c)).start()
        token[...] = jnp.zeros_like(token)

    bufs = [_in_hbm(a) for a in srcs] + lands
    out = pl.pallas_call(
        body, name=name,
        out_shape=(pltpu.SemaphoreType.DMA((3 * n,)), pltpu.SemaphoreType.DMA((3 * n,)),
                   *[pltpu.HBM(b.shape, b.dtype) for b in bufs], jax.ShapeDtypeStruct((8, 128), F32)),
        in_specs=[HBM] * (2 * n) + [ANY] * na,
        out_specs=(SEM, SEM, *[HBM] * (2 * n), pl.BlockSpec(memory_space=pltpu.VMEM)),
        input_output_aliases={i: 2 + i for i in range(2 * n)},
        compiler_params=pltpu.CompilerParams(has_side_effects=EFFECT),
    )(*bufs, *after)
    return out[0], out[1], list(out[2:2 + n]), list(out[2 + n:2 + 2 * n]), out[-1]


def gather_wait(ssem, rsem, srcs, lands, after, name):
    n = len(srcs)

    def body(*refs):
        src, land = refs[:n], refs[n:2 * n]
        s_sem, r_sem = refs[2 * n], refs[2 * n + 1]
        x, y, c, others = _place()
        for i in range(n):
            for k, (ox, oy) in enumerate(others):
                cp = _rcopy(src[i], land[i].at[2 * ox + oy], s_sem.at[3 * i + k], r_sem.at[3 * i + k], (ox, oy, c))
                cp.wait_send()
                cp.wait_recv()

    bufs = list(srcs) + list(lands)
    out = pl.pallas_call(
        body, name=name,
        out_shape=tuple(pltpu.HBM(b.shape, b.dtype) for b in bufs),
        in_specs=[HBM] * (2 * n) + [SEM, SEM, ANY],
        out_specs=tuple([HBM] * (2 * n)),
        input_output_aliases={i: i for i in range(2 * n)},
        compiler_params=pltpu.CompilerParams(has_side_effects=EFFECT),
    )(*bufs, ssem, rsem, after)
    return list(out[n:2 * n])


def grad_sibling_exchange(arrs):
    n = len(arrs)

    def body(*refs):
        srcs, outs, ssem, rsem = refs[:n], refs[n:2 * n], refs[2 * n], refs[2 * n + 1]
        x, y, c, _ = _place()
        cps = [_rcopy(srcs[i].at[1 - c], outs[i], ssem.at[i], rsem.at[i], (x, y, 1 - c)) for i in range(n)]
        for cp in cps:
            cp.start()
        for cp in cps:
            cp.wait()

    return pl.pallas_call(
        body, name="grad_sibling_exchange",
        in_specs=[ANY] * n, out_specs=[ANY] * n,
        out_shape=[jax.ShapeDtypeStruct(a.shape[1:], F32) for a in arrs],
        scratch_shapes=[pltpu.SemaphoreType.DMA((n,)), pltpu.SemaphoreType.DMA((n,))],
    )(*arrs)


def grad_chip_sum(g, sb, cc, tr, name):
    _, _, R, C = g.shape
    blk = pl.BlockSpec((1, tr, C), lambda s, r, c: (s, r, 0))
    grid_spec = pltpu.PrefetchScalarGridSpec(
        num_scalar_prefetch=1, grid=(SHARDS, R // tr),
        in_specs=[pl.BlockSpec((1, 1, tr, C), lambda s, r, c: (c[0], s, r, 0)), blk],
        out_specs=[blk, blk])

    def body(c_ref, a_ref, b_ref, o_ref, ob_ref):
        t = a_ref[0] + b_ref[...]
        o_ref[...] = t
        ob_ref[...] = t.astype(BF16)

    return pl.pallas_call(
        body, name=name, grid_spec=grid_spec,
        out_shape=[jax.ShapeDtypeStruct((SHARDS, R, C), F32), jax.ShapeDtypeStruct((SHARDS, R, C), BF16)],
        compiler_params=_params(("arbitrary", "arbitrary")),
    )(cc, g, sb)


def grad_chip_exchange(arrs):
    n = len(arrs)

    def body(*refs):
        srcs, outs, ssem, rsem = refs[:n], refs[n:2 * n], refs[2 * n], refs[2 * n + 1]
        x, y, c, others = _place()
        me = 2 * x + y
        sends = [_rcopy(srcs[i].at[2 * ox + oy], outs[i].at[me], ssem.at[3 * i + k], rsem.at[3 * i + k], (ox, oy, c))
                 for i in range(n) for k, (ox, oy) in enumerate(others)]
        for cp in sends:
            cp.start()
        for i in range(n):
            for k, (ox, oy) in enumerate(others):
                slot = outs[i].at[2 * ox + oy]
                _rcopy(slot, slot, ssem.at[3 * i + k], rsem.at[3 * i + k], (ox, oy, c)).wait_recv()
        for cp in sends:
            cp.wait_send()

    return pl.pallas_call(
        body, name="grad_chip_exchange",
        in_specs=[ANY] * n, out_specs=[ANY] * n,
        out_shape=[jax.ShapeDtypeStruct(a.shape, a.dtype) for a in arrs],
        scratch_shapes=[pltpu.SemaphoreType.DMA((3 * n,)), pltpu.SemaphoreType.DMA((3 * n,))],
    )(*arrs)


def grad_shard_sum(t, rb, me, tr, name):
    _, R, C = t.shape
    grid_spec = pltpu.PrefetchScalarGridSpec(
        num_scalar_prefetch=1, grid=(R // tr,),
        in_specs=[pl.BlockSpec((1, tr, C), lambda r, m: (m[0], r, 0)),
                  pl.BlockSpec((SHARDS, tr, C), lambda r, m: (0, r, 0))],
        out_specs=pl.BlockSpec((tr, C), lambda r, m: (r, 0)))

    def body(m_ref, t_ref, r_ref, o_ref):
        part = [jnp.where(m_ref[0] == s, t_ref[0], r_ref[s].astype(F32)) for s in range(SHARDS)]
        o_ref[...] = ((part[0] + part[1]) + part[2]) + part[3]

    return pl.pallas_call(
        body, name=name, grid_spec=grid_spec,
        out_shape=jax.ShapeDtypeStruct((R, C), F32),
        compiler_params=_params(("arbitrary",)),
    )(me, t, rb)


def grad_sibling_share(arrs):
    n = len(arrs)

    def body(*refs):
        srcs, outs, ssem, rsem = refs[:n], refs[n:2 * n], refs[2 * n], refs[2 * n + 1]
        x, y, c, _ = _place()
        cps = [_rcopy(srcs[i], outs[i], ssem.at[i], rsem.at[i], (x, y, 1 - c)) for i in range(n)]
        for cp in cps:
            cp.start()
        for cp in cps:
            cp.wait()

    return pl.pallas_call(
        body, name="grad_sibling_share",
        in_specs=[ANY] * n, out_specs=[ANY] * n,
        out_shape=[jax.ShapeDtypeStruct(a.shape, F32) for a in arrs],
        scratch_shapes=[pltpu.SemaphoreType.DMA((n,)), pltpu.SemaphoreType.DMA((n,))],
    )(*arrs)


def _allreduce_rows(src, sib_buf, chips, out_ref, ssem, rsem):
    x, y, c, others = _place()
    me = 2 * x + y
    cp = _rcopy(src, sib_buf, ssem.at[0], rsem.at[0], (x, y, 1 - c))
    cp.start()
    cp.wait()
    chips[me] = src[...] + sib_buf[...]
    sends = [_rcopy(chips.at[me], chips.at[me], ssem.at[1 + k], rsem.at[1 + k], (ox, oy, c))
             for k, (ox, oy) in enumerate(others)]
    for s in sends:
        s.start()
    for k, (ox, oy) in enumerate(others):
        slot = chips.at[2 * ox + oy]
        _rcopy(slot, slot, ssem.at[1 + k], rsem.at[1 + k], (ox, oy, c)).wait_recv()
    for s in sends:
        s.wait_send()
    out_ref[...] = ((chips[0] + chips[1]) + chips[2]) + chips[3]


def _allreduce_scratch(rows):
    return [pltpu.VMEM((rows, 128), F32), pltpu.VMEM((SHARDS, rows, 128), F32),
            pltpu.SemaphoreType.DMA((4,)), pltpu.SemaphoreType.DMA((4,))]


def allreduce_rows(buf, name):
    rows = buf.shape[0]
    VM = pl.BlockSpec(memory_space=pltpu.VMEM)

    def body(src_ref, out_ref, sib_buf, chips, ssem, rsem):
        _allreduce_rows(src_ref, sib_buf, chips, out_ref, ssem, rsem)

    return pl.pallas_call(
        body, name=name, in_specs=[VM], out_specs=VM,
        out_shape=jax.ShapeDtypeStruct((rows, 128), F32),
        scratch_shapes=_allreduce_scratch(rows), compiler_params=_params(),
    )(buf)


def small_allreduce(grads, rel, loss_part):
    rows = _small_rows()
    keys = [(l, name) for l in range(DEPTH) for name in SMALL]
    flat = [grads[l][SMALL[name][0]] for l, name in keys] + [rel, loss_part]

    def body(*refs):
        ins = refs[:len(flat)]
        out_ref, src, sib_buf, chips, ssem, rsem = refs[len(flat):]
        src[...] = jnp.zeros_like(src)
        for (l, name), ref in zip(keys, ins):
            r0 = rows[(l, name)]
            if name == "sg_w":
                for grp in range(8):
                    src[r0 + grp * 128:r0 + (grp + 1) * 128, :] = ref[grp]
            elif name == "sg_b":
                src[r0:r0 + 8, :] = ref[...].T[0:8, :]
            else:
                for j in range(ref.shape[1] // 128):
                    src[r0 + j:r0 + j + 1, :] = ref[:, j * 128:(j + 1) * 128]
        src[REL_ROW:REL_ROW + 32, 0:16] = ins[-2][...]
        src[LOSS_ROW:LOSS_ROW + 1, :] = ins[-1][...]
        _allreduce_rows(src, sib_buf, chips, out_ref, ssem, rsem)

    return pl.pallas_call(
        body, name="small_allreduce",
        out_shape=jax.ShapeDtypeStruct((SMALL_ROWS, 128), F32),
        scratch_shapes=[pltpu.VMEM((SMALL_ROWS, 128), F32)] + _allreduce_scratch(SMALL_ROWS),
        compiler_params=_params(),
    )(*flat)


def _pad_lanes(v):
    return jnp.zeros((1, 128), F32).at[0, :v.shape[0]].set(v)


def layer_fwd(x, wts, bias):
    wt = wts["wt"]
    tn = {name: t for name, _, t in GROUPS}
    p_gate, h = inproj_first(x, wts["g_pre"], wt["gate"], tn["gate"], "inproj_gate")
    p_sgu, p_att, p_ssd = (inproj_group(h, wt[n], tn[n], "inproj_" + n) for n in ("sgu", "att", "ssd"))
    y_att = att_fwd(p_att, bias, wts["sinks"])
    y_sg = sgu_fwd(p_sgu, wts["ln_g"], wts["ln_b"], wts["sg_w"], wts["sg_bt"])
    y_ssm, hst = ssd_fwd(p_ssd, wts["conv_w"], wts["conv_b"], wts["dt_bias"], wts["a_log"], wts["d_skip"],
                         wts["norm_g"])
    x_new, br_a, br_s, br_m, merged, out = merge_fwd(
        y_att, y_sg, y_ssm, p_gate, x, wts["w_a"], wts["w_s"], wts["w_m"], wts["w_o"], wts["g_post"])
    saved = dict(x=x, p_gate=p_gate, p_sgu=p_sgu, p_att=p_att, p_ssd=p_ssd, h=h,
                 y_att=y_att, y_sg=y_sg, y_ssm=y_ssm, hst=hst,
                 br_a=br_a, br_s=br_s, br_m=br_m, merged=merged, out=out)
    return x_new, saved


def layer_bwd(dy, wts, bias, sv):
    dout, dba, dbs, dbm, d_gate, dya, dys, dym, dg_post = merge_bwd(
        dy, sv["out"], wts["g_post"], sv["p_gate"], sv["br_a"], sv["br_s"], sv["br_m"],
        wts["w_a"], wts["w_s"], wts["w_m"], wts["w_o"])
    d_att, dbias, dsinks = att_bwd(dya, sv["p_att"], bias, wts["sinks"])
    d_sgu, dsg_w, dsg_bt, dln_g, dln_b = sgu_bwd(dys, sv["p_sgu"], wts["ln_g"], wts["ln_b"], wts["sg_w"],
                                                 wts["sg_bt"])
    d_ssd, dcw, dcb, ddtb, dalog, ddsk, dng = ssd_bwd(
        dym, sv["p_ssd"], sv["hst"], wts["conv_w"], wts["conv_b"], wts["dt_bias"], wts["a_log"], wts["d_skip"],
        wts["norm_g"])
    dps = dict(gate=d_gate, sgu=d_sgu, att=d_att, ssd=d_ssd)
    wt = wts["wt"]
    tn = {name: t for name, _, t in GROUPS}
    acc = None
    for n in ("gate", "sgu", "ssd"):
        acc = dh_group(dps[n], wt[n], acc, tn[n], "dh_" + n)
    dx, dg_pre = dh_last(dps["att"], wt["att"], acc, sv["x"], wts["g_pre"], dy, tn["att"], "dh_att")
    grads = dict(
        w_in={n: dw_group(dps[n], sv["h"], tn[n], "dw_in_" + n) for n in dps},
        w_a=matmul_tn(sv["y_att"], dba, "dw_att"),
        w_s=matmul_tn(sv["y_sg"], dbs, "dw_sg"),
        w_m=matmul_tn(sv["y_ssm"], dbm, "dw_ssm"),
        w_o=matmul_tn(sv["merged"], dout, "dw_out"),
        g_pre=dg_pre, g_post=dg_post, sinks=dsinks, ln_g=dln_g, ln_b=dln_b, sg_w=dsg_w, sg_bt=dsg_bt,
        conv_w=dcw, conv_b=dcb, dt_bias=ddtb, a_log=dalog, d_skip=ddsk, norm_g=dng, bias=dbias)
    return dx, grads


REST_OFF = (0, 256, 512, 1024, 1280)
REST_ROWS = 1296


def kernel(x, w_in, norm_pre, norm_post, rel_bias, att_sinks, sg_ln_g, sg_ln_b, sg_w, sg_b, ssm_conv_w, ssm_conv_b, ssm_dt_bias, ssm_a_log, ssm_d, ssm_norm_g, w_br_att, w_br_sg, w_br_ssm, w_out, loss_target, m_w_in, m_norm_pre, m_norm_post, m_rel_bias, m_att_sinks, m_sg_ln_g, m_sg_ln_b, m_sg_w, m_sg_b, m_ssm_conv_w, m_ssm_conv_b, m_ssm_dt_bias, m_ssm_a_log, m_ssm_d, m_ssm_norm_g, m_w_br_att, m_w_br_sg, m_w_br_ssm, m_w_out, v_w_in, v_norm_pre, v_norm_post, v_rel_bias, v_att_sinks, v_sg_ln_g, v_sg_ln_b, v_sg_w, v_sg_b, v_ssm_conv_w, v_ssm_conv_b, v_ssm_dt_bias, v_ssm_a_log, v_ssm_d, v_ssm_norm_g, v_w_br_att, v_w_br_sg, v_w_br_ssm, v_w_out):
    cx, cy, cc = lax.axis_index("x"), lax.axis_index("y"), lax.axis_index("c")
    me = 2 * cx + cy
    xs = x[0]
    S = xs.shape[0]

    tr = lambda a: jnp.transpose(a, (0, 2, 1))
    w_in_b = tr(w_in).astype(BF16)
    w_rest_b = jnp.concatenate([w_br_att, w_br_sg, w_br_ssm, w_out], axis=1).astype(BF16)
    halves = lambda a: a.reshape(2, a.shape[0] // 2, a.shape[1])
    all0_in, all0_rest = gather_weights([halves(w_in_b[0]), halves(w_rest_b[0])])
    convw_slot = jnp.zeros((SHARDS, DEPTH * CONV_K * 768 // 128, 128), F32)
    convw_slot = lax.dynamic_update_index_in_dim(
        convw_slot, jnp.where(cc == 0, 1.0, 0.0) * ssm_conv_w.reshape(-1, 128), me, 0)
    convw_rows = allreduce_rows(convw_slot.reshape(-1, 128), "gather_conv_w")
    convw_all = convw_rows.reshape(SHARDS, DEPTH, CONV_K, 768).transpose(1, 2, 0, 3).reshape(DEPTH, CONV_K, CONV_C)
    g1_ssem, g1_rsem, g1_srcs, g1_lands, g1_token = gather_start(
        [w_in_b[1], w_rest_b[1]], [convw_rows, all0_rest], "gather_l1_start")

    o = REST_OFF

    def layer_weights(l, gathered_in, gathered_rest, g_pre):
        sh_in = [jnp.where(me == s, w_in_b[l], gathered_in[s]) for s in range(SHARDS)]
        sh_rest = [jnp.where(me == s, w_rest_b[l], gathered_rest[s]) for s in range(SHARDS)]
        rest = lambda k: jnp.concatenate([r[o[k]:o[k + 1]] for r in sh_rest], axis=0)
        return dict(
            wt=group_weights(jnp.concatenate(sh_in, axis=0)),
            w_a=rest(0), w_s=rest(1), w_m=rest(2), w_o=rest(3),
            g_pre=g_pre, g_post=norm_post[l][None], sinks=att_sinks[l],
            ln_g=sg_ln_g[l][None], ln_b=sg_ln_b[l][None], sg_w=sg_w[l],
            sg_bt=sg_b[l].T,
            conv_w=jnp.concatenate([convw_all[l], jnp.zeros((4, CONV_C), F32)], axis=0),
            conv_b=ssm_conv_b[l][None], dt_bias=_pad_lanes(ssm_dt_bias[l]), a_log=_pad_lanes(ssm_a_log[l]),
            d_skip=_pad_lanes(ssm_d[l]), norm_g=ssm_norm_g[l][None])

    bias = bias_table(rel_bias)
    layers = [layer_weights(0, [all0_in[:, s].reshape(3400, D) for s in range(SHARDS)],
                            [all0_rest[:, s].reshape(1280, D) for s in range(SHARDS)],
                            (norm_pre[0] + g1_token[0, 0])[None])]
    act, sv0 = layer_fwd(xs, layers[0], bias)
    land_in, land_rest = gather_wait(g1_ssem, g1_rsem, g1_srcs, g1_lands, act, "gather_l1_wait")
    layers.append(layer_weights(1, land_in, land_rest, norm_pre[1][None]))
    act, sv1 = layer_fwd(act, layers[1], bias)
    saved = [sv0, sv1]
    dy, loss_part = loss_head(act, loss_target[0])
    grads = [None] * DEPTH
    for l in reversed(range(DEPTH)):
        dy, grads[l] = layer_bwd(dy, layers[l], bias, saved[l])
    grad_x = dy[None]
    grad_rel_local = bias_grad(grads[0]["bias"] + grads[1]["bias"])

    cvec = jnp.reshape(cc, (1,)).astype(jnp.int32)
    mvec = jnp.reshape(me, (1,)).astype(jnp.int32)
    g_in, g_rest = [], []
    for l in range(DEPTH):
        g = grads[l]
        g_in.append(jnp.pad(ungroup_grads(g["w_in"]).reshape(SHARDS, 3400, D),
                            ((0, 0), (0, W_IN_ROWS - 3400), (0, 0))))
        gcw = g["conv_w"][0:CONV_K].reshape(CONV_K, SHARDS, 768).transpose(1, 0, 2).reshape(SHARDS, 3, 1024)
        g_rest.append(jnp.concatenate([
            g["w_a"].reshape(SHARDS, 256, D), g["w_s"].reshape(SHARDS, 256, D), g["w_m"].reshape(SHARDS, 512, D),
            g["w_o"].reshape(SHARDS, 256, D), jnp.pad(gcw, ((0, 0), (0, REST_ROWS - REST_OFF[4] - 3), (0, 0)))],
            axis=1))
    g_in, g_rest = jnp.stack(g_in), jnp.stack(g_rest)
    sb_in, sb_rest = grad_sibling_exchange([g_in, g_rest])
    t_in, t_in_b = grad_chip_sum(g_in, sb_in, cvec, 384, "chip_sum_w_in")
    t_rest, t_rest_b = grad_chip_sum(g_rest, sb_rest, cvec, 432, "chip_sum_rest")
    rb_in, rb_rest = grad_chip_exchange([t_in_b, t_rest_b])
    f_in = grad_shard_sum(t_in, rb_in, mvec, 384, "shard_sum_w_in")
    f_rest = grad_shard_sum(t_rest, rb_rest, mvec, 432, "shard_sum_rest")
    fb_in, fb_rest = grad_sibling_share([f_in, f_rest])

    red = small_allreduce(grads, grad_rel_local, loss_part)
    loss = red[LOSS_ROW, 0]

    res = adamw_small(red, (rel_bias, m_rel_bias, v_rel_bias), dict(
        norm_pre=(norm_pre, m_norm_pre, v_norm_pre), norm_post=(norm_post, m_norm_post, v_norm_post),
        att_sinks=(att_sinks, m_att_sinks, v_att_sinks), sg_ln_g=(sg_ln_g, m_sg_ln_g, v_sg_ln_g),
        sg_ln_b=(sg_ln_b, m_sg_ln_b, v_sg_ln_b), sg_w=(sg_w, m_sg_w, v_sg_w), sg_b=(sg_b, m_sg_b, v_sg_b),
        ssm_conv_b=(ssm_conv_b, m_ssm_conv_b, v_ssm_conv_b), ssm_dt_bias=(ssm_dt_bias, m_ssm_dt_bias, v_ssm_dt_bias),
        ssm_a_log=(ssm_a_log, m_ssm_a_log, v_ssm_a_log), ssm_d=(ssm_d, m_ssm_d, v_ssm_d),
        ssm_norm_g=(ssm_norm_g, m_ssm_norm_g, v_ssm_norm_g)))
    res["w_in"] = tuple(tr(a) for a in adamw_big(tr(w_in), tr(m_w_in), tr(v_w_in), f_in, fb_in, cvec, "adamw_w_in", 200))
    res["w_br_att"] = adamw_big(w_br_att, m_w_br_att, v_w_br_att, f_rest, fb_rest, cvec, "adamw_w_br_att", 256, o[0])
    res["w_br_sg"] = adamw_big(w_br_sg, m_w_br_sg, v_w_br_sg, f_rest, fb_rest, cvec, "adamw_w_br_sg", 256, o[1])
    res["w_br_ssm"] = adamw_big(w_br_ssm, m_w_br_ssm, v_w_br_ssm, f_rest, fb_rest, cvec, "adamw_w_br_ssm", 512, o[2])
    res["w_out"] = adamw_big(w_out, m_w_out, v_w_out, f_rest, fb_rest, cvec, "adamw_w_out", 256, o[3])
    cw_mine = f_rest[o[4]:o[4] + 3].reshape(CONV_K, 768)
    cw_sib = fb_rest[o[4]:o[4] + 3].reshape(CONV_K, 768)
    g_conv_w = jnp.stack([jnp.where(cc == l, cw_mine, cw_sib) for l in range(DEPTH)])
    res["ssm_conv_w"] = (g_conv_w,) + tuple(adamw_plain(ssm_conv_w, g_conv_w, m_ssm_conv_w, v_ssm_conv_w, "adamw_conv_w"))

    order = ["w_in", "norm_pre", "norm_post", "rel_bias", "att_sinks", "sg_ln_g", "sg_ln_b", "sg_w", "sg_b",
             "ssm_conv_w", "ssm_conv_b", "ssm_dt_bias", "ssm_a_log", "ssm_d", "ssm_norm_g",
             "w_br_att", "w_br_sg", "w_br_ssm", "w_out"]
    return (loss, grad_x, *[res[n][0] for n in order], *[res[n][1] for n in order],
            *[res[n][2] for n in order], *[res[n][3] for n in order])
```

```python
import functools
import math

import numpy as np
import jax
import jax.numpy as jnp
from jax import lax
from jax.experimental import pallas as pl
from jax.experimental.pallas import tpu as pltpu

F32 = jnp.float32
BF16 = jnp.bfloat16
MESH = pl.DeviceIdType.MESH

D = 1024
DEPTH = 2
EPS = 1e-6
L = 128
HEADS = 16
KV = 2
DH = 64
SSM_W = 2048
SSM_H = 32
SSM_P = 64
SSM_G = 4
SSM_N = 128
CONV_K = 4
CONV_C = 3072
NEG = -1e30
IN_COLS = 13600

GROUPS = (("gate", 3072, 1536), ("sgu", 3072, 1536), ("att", 2304, 2304), ("ssd", 5376, 1792))
W_IN_ROWS = 3456

ADAM_LR = 0.001
ADAM_B1 = 0.9
ADAM_B2 = 0.999
ADAM_EPS = 1e-08
ADAM_WD = 0.01
ADAM_STEP = 10

VMEM_LIMIT = 56 * 1024 * 1024

SHARDS = 4


def _dot(a, b):
    return jnp.dot(a, b, preferred_element_type=F32)


def _dot_nt(a, b):
    return lax.dot_general(a, b, (((1,), (1,)), ((), ())), preferred_element_type=F32)


def _dot_tn(a_f32, b):
    return jnp.dot(a_f32.T.astype(BF16), b, preferred_element_type=F32)


def _dot_hi(a, b):
    return jnp.dot(a, b, preferred_element_type=F32, precision=lax.Precision.HIGHEST)


def _pieces(x, n):
    out = []
    for _ in range(n - 1):
        p = x.astype(BF16)
        out.append(p)
        x = x - p.astype(F32)
    out.append(x.astype(BF16))
    return out


def _dot_sel(a, sel, n):
    sel = sel.astype(BF16)
    acc = None
    for p in _pieces(a, n):
        t = _dot(p, sel)
        acc = t if acc is None else acc + t
    return acc


def _sel_dot(sel, b, n):
    sel = sel.astype(BF16)
    acc = None
    for p in _pieces(b, n):
        t = _dot(sel, p)
        acc = t if acc is None else acc + t
    return acc


def _sigmoid(x):
    return 1.0 / (1.0 + jnp.exp(-x))


def _softplus(x):
    return jnp.maximum(x, 0.0) + jnp.log(1.0 + jnp.exp(-jnp.abs(x)))


def _params(sem=None, vmem=VMEM_LIMIT):
    kw = dict(vmem_limit_bytes=vmem)
    if sem is not None:
        kw["dimension_semantics"] = sem
    return pltpu.CompilerParams(**kw)


def _full(shape):
    nd = len(shape)
    return pl.BlockSpec(shape, lambda *_: (0,) * nd)


def group_weights(wt):
    return dict(
        gate=wt[10528:13600],
        sgu=wt[2304:5376],
        att=jnp.concatenate([wt[0:1024], wt[1280:2304], wt[1024:1280]], axis=0),
        ssd=jnp.concatenate([wt[5376:10496], wt[10496:10528], jnp.zeros((224, D), wt.dtype)], axis=0))


def ungroup_grads(g):
    a, s = g["att"], g["ssd"]
    return jnp.concatenate([a[0:1024], a[2048:2304], a[1024:2048], g["sgu"], s[0:5152], g["gate"]], axis=0)


def _bucket_table():
    qi = np.arange(L)[:, None]
    kj = np.arange(2 * L)[None, :]
    dist = np.maximum(qi + L - kj, 0)
    dist_f = np.maximum(dist, 1).astype(np.float32)
    large = 16 + (np.log(dist_f / np.float32(16)) / np.float32(math.log(128 / 16)) * np.float32(16)).astype(np.int32)
    large = np.minimum(large, 31)
    return np.where(dist < 16, dist, large).astype(np.int32)


def bias_table(rel_bias):
    buckets = jnp.asarray(_bucket_table().reshape(1, L * 2 * L))

    def body(rb_ref, bk_ref, out_ref):
        onehot = (lax.broadcasted_iota(jnp.int32, (32, L * 2 * L), 0) == bk_ref[...]).astype(F32)
        out_ref[...] = lax.dot_general(rb_ref[...], onehot, (((0,), (0,)), ((), ())),
                                       preferred_element_type=F32, precision=lax.Precision.HIGHEST)

    out = pl.pallas_call(
        body, name="bias_table",
        out_shape=jax.ShapeDtypeStruct((HEADS, L * 2 * L), F32),
        compiler_params=_params(),
    )(rel_bias, buckets)
    return out.reshape(HEADS, L, 2 * L)


def bias_grad(dbias):
    buckets = jnp.asarray(_bucket_table().reshape(1, L * 2 * L))

    def body(db_ref, bk_ref, out_ref):
        onehot = (lax.broadcasted_iota(jnp.int32, (32, L * 2 * L), 0) == bk_ref[...]).astype(F32)
        out_ref[...] = lax.dot_general(onehot, db_ref[...], (((1,), (1,)), ((), ())),
                                       preferred_element_type=F32, precision=lax.Precision.HIGHEST)

    return pl.pallas_call(
        body, name="bias_grad",
        out_shape=jax.ShapeDtypeStruct((32, HEADS), F32),
        compiler_params=_params(),
    )(dbias.reshape(HEADS, L * 2 * L), buckets)


def _row_tile(S):
    return 1024 if S % 1024 == 0 else 512


def inproj_first(x, g_pre, wt, tn, name):
    S, W = x.shape[0], wt.shape[0]
    tm = _row_tile(S)

    def body(x_ref, g_ref, w_ref, o_ref, h_ref):
        @pl.when(pl.program_id(1) == 0)
        def _():
            xv = x_ref[...]
            r = lax.rsqrt(jnp.mean(xv * xv, axis=-1, keepdims=True) + EPS)
            h_ref[...] = (xv * r * g_ref[...]).astype(BF16)
        o_ref[...] = _dot_nt(h_ref[...], w_ref[...])

    return pl.pallas_call(
        body, name=name, grid=(S // tm, W // tn),
        in_specs=[pl.BlockSpec((tm, D), lambda i, j: (i, 0)), _full((1, D)),
                  pl.BlockSpec((tn, D), lambda i, j: (j, 0))],
        out_specs=[pl.BlockSpec((tm, tn), lambda i, j: (i, j)), pl.BlockSpec((tm, D), lambda i, j: (i, 0))],
        out_shape=[jax.ShapeDtypeStruct((S, W), F32), jax.ShapeDtypeStruct((S, D), BF16)],
        compiler_params=_params(("arbitrary", "arbitrary")),
    )(x, g_pre, wt)


def inproj_group(h, wt, tn, name):
    S, W = h.shape[0], wt.shape[0]
    tm = _row_tile(S)

    def body(h_ref, w_ref, o_ref):
        o_ref[...] = _dot_nt(h_ref[...], w_ref[...])

    return pl.pallas_call(
        body, name=name, grid=(S // tm, W // tn),
        in_specs=[pl.BlockSpec((tm, D), lambda i, j: (i, 0)), pl.BlockSpec((tn, D), lambda i, j: (j, 0))],
        out_specs=pl.BlockSpec((tm, tn), lambda i, j: (i, j)),
        out_shape=jax.ShapeDtypeStruct((S, W), F32),
        compiler_params=_params(("arbitrary", "arbitrary")),
    )(h, wt)


def dh_group(dp, wt, acc, tk, name):
    S, W = dp.shape
    tm = _row_tile(S)

    def body(*refs):
        dp_ref, w_ref, o_ref = refs[0], refs[1], refs[-1]
        first = pl.program_id(1) == 0
        if acc is None:
            @pl.when(first)
            def _():
                o_ref[...] = jnp.zeros_like(o_ref)
        else:
            @pl.when(first)
            def _():
                o_ref[...] = refs[2][...]
        o_ref[...] += _dot(dp_ref[...], w_ref[...])

    row = pl.BlockSpec((tm, D), lambda i, k: (i, 0))
    return pl.pallas_call(
        body, name=name, grid=(S // tm, W // tk),
        in_specs=[pl.BlockSpec((tm, tk), lambda i, k: (i, k)), pl.BlockSpec((tk, D), lambda i, k: (k, 0))]
        + ([] if acc is None else [row]),
        out_specs=row, out_shape=jax.ShapeDtypeStruct((S, D), F32),
        input_output_aliases={} if acc is None else {2: 0},
        compiler_params=_params(("arbitrary", "arbitrary")),
    )(*((dp, wt) if acc is None else (dp, wt, acc)))


def dh_last(dp, wt, acc_in, x, g_pre, dy, tk, name):
    S, W = dp.shape
    tm = 512
    nk = W // tk

    def body(dp_ref, w_ref, a_ref, x_ref, g_ref, dy_ref, dx_ref, dg_ref, acc):
        i, k = pl.program_id(0), pl.program_id(1)

        @pl.when(k == 0)
        def _():
            acc[...] = a_ref[...]

        acc[...] += _dot(dp_ref[...], w_ref[...])

        @pl.when((k == nk - 1) & (i == 0))
        def _():
            dg_ref[...] = jnp.zeros_like(dg_ref)

        @pl.when(k == nk - 1)
        def _():
            xv = x_ref[...]
            dh = acc[...]
            g = g_ref[...]
            r = lax.rsqrt(jnp.mean(xv * xv, axis=-1, keepdims=True) + EPS)
            dhg = dh * g
            dx_ref[...] = dy_ref[...] + r * dhg - xv * (r * r * r) * jnp.mean(dhg * xv, axis=-1, keepdims=True)
            dg_ref[...] += jnp.sum(dh * xv * r, axis=0, keepdims=True)

    row = pl.BlockSpec((tm, D), lambda i, k: (i, 0))
    return pl.pallas_call(
        body, name=name, grid=(S // tm, nk),
        in_specs=[pl.BlockSpec((tm, tk), lambda i, k: (i, k)), pl.BlockSpec((tk, D), lambda i, k: (k, 0)),
                  row, row, _full((1, D)), row],
        out_specs=[row, _full((1, D))],
        out_shape=[jax.ShapeDtypeStruct((S, D), F32), jax.ShapeDtypeStruct((1, D), F32)],
        scratch_shapes=[pltpu.VMEM((tm, D), F32)],
        compiler_params=_params(("arbitrary", "arbitrary")),
    )(dp, wt, acc_in, x, g_pre, dy)


def dw_group(dp, h, tn, name, ts=512):
    S, W = dp.shape

    def body(dp_ref, h_ref, o_ref):
        @pl.when(pl.program_id(1) == 0)
        def _():
            o_ref[...] = jnp.zeros_like(o_ref)
        o_ref[...] += _dot_tn(dp_ref[...].astype(F32), h_ref[...])

    return pl.pallas_call(
        body, name=name, grid=(W // tn, S // ts),
        in_specs=[pl.BlockSpec((ts, tn), lambda j, s: (s, j)), pl.BlockSpec((ts, D), lambda j, s: (s, 0))],
        out_specs=pl.BlockSpec((tn, D), lambda j, s: (j, 0)),
        out_shape=jax.ShapeDtypeStruct((W, D), F32),
        compiler_params=_params(("arbitrary", "arbitrary")),
    )(dp, h)


def matmul_tn(a, b, name, tn=512, ts=512):
    S, K = a.shape
    N = b.shape[1]
    ns = S // ts

    def body(a_ref, b_ref, o_ref):
        @pl.when(pl.program_id(1) == 0)
        def _():
            o_ref[...] = jnp.zeros_like(o_ref)
        o_ref[...] += _dot_tn(a_ref[...].astype(F32), b_ref[...])

    return pl.pallas_call(
        body, name=name, grid=(N // tn, ns),
        in_specs=[pl.BlockSpec((ts, K), lambda j, s: (s, 0)), pl.BlockSpec((ts, tn), lambda j, s: (s, j))],
        out_specs=pl.BlockSpec((K, tn), lambda j, s: (0, j)),
        out_shape=jax.ShapeDtypeStruct((K, N), F32),
        compiler_params=_params(("arbitrary", "arbitrary")),
    )(a, b)


def _att_mask(n):
    qi = lax.broadcasted_iota(jnp.int32, (L, 2 * L), 0)
    kj = lax.broadcasted_iota(jnp.int32, (L, 2 * L), 1)
    dist = qi + L - kj
    return (dist >= 0) & (dist < L) & ((kj >= L) | (n > 0))


def _att_in_specs(nb):
    last = nb - 1
    cur = lambda n: jnp.minimum(n, last)
    prev = lambda n: jnp.maximum(jnp.minimum(n, last) - 1, 0)
    return [
        pl.BlockSpec((L, 1024), lambda n: (cur(n), 0)),
        pl.BlockSpec((L, 128), lambda n: (prev(n), 16)),
        pl.BlockSpec((L, 128), lambda n: (cur(n), 16)),
        pl.BlockSpec((L, 128), lambda n: (prev(n), 17)),
        pl.BlockSpec((L, 128), lambda n: (cur(n), 17)),
        pl.BlockSpec((L, 1024), lambda n: (cur(n), 1)),
        _full((HEADS, L, 2 * L)),
        pl.BlockSpec(memory_space=pltpu.SMEM),
    ]


def _att_probs(qh, kk, bias_h, mask, sk):
    logits = _dot_nt(qh, kk) + bias_h
    logits = jnp.where(mask, logits, NEG)
    m = jnp.maximum(jnp.max(logits, axis=-1, keepdims=True), sk)
    p = jnp.exp(logits - m)
    es = jnp.exp(sk - m)
    den = jnp.sum(p, axis=-1, keepdims=True) + es
    return p / den, es / den


def att_fwd(proj, bias, sinks):
    S = proj.shape[0]
    nb = S // L

    def body(q_ref, kp_ref, kc_ref, vp_ref, vc_ref, z_ref, bias_ref, s_ref, y_ref, o_scr):
        mask = _att_mask(pl.program_id(0))
        for kv in range(KV):
            sl = slice(kv * DH, (kv + 1) * DH)
            kk = jnp.concatenate([kp_ref[:, sl], kc_ref[:, sl]], axis=0).astype(BF16)
            vv = jnp.concatenate([vp_ref[:, sl], vc_ref[:, sl]], axis=0).astype(BF16)
            for g in range(HEADS // KV):
                h = kv * (HEADS // KV) + g
                hs = slice(h * DH, (h + 1) * DH)
                qh = (q_ref[:, hs] * 0.125).astype(BF16)
                P, _ = _att_probs(qh, kk, bias_ref[h], mask, s_ref[h])
                o_scr[:, hs] = _dot(P.astype(BF16), vv)
        z = z_ref[...]
        y_ref[...] = (o_scr[...] * (z * _sigmoid(z))).astype(BF16)

    return pl.pallas_call(
        body, name="att_fwd", grid=(nb,),
        in_specs=_att_in_specs(nb),
        out_specs=pl.BlockSpec((L, 1024), lambda n: (n, 0)),
        out_shape=jax.ShapeDtypeStruct((S, 1024), BF16),
        scratch_shapes=[pltpu.VMEM((L, 1024), F32)],
        compiler_params=_params(("arbitrary",)),
    )(proj, proj, proj, proj, proj, proj, bias, sinks)


def att_bwd(dy, proj, bias, sinks):
    S = proj.shape[0]
    nb = S // L
    last = nb - 1

    def body(dy_ref, q_ref, kp_ref, kc_ref, vp_ref, vc_ref, z_ref, bias_ref, s_ref,
             dout_ref, dbias_ref, dsink_ref, carry, band, dq_scr, dz_scr):
        n = pl.program_id(0)

        @pl.when(n == 0)
        def _():
            carry[...] = jnp.zeros_like(carry)
            dq_scr[...] = jnp.zeros_like(dq_scr)
            dz_scr[...] = jnp.zeros_like(dz_scr)
            dbias_ref[...] = jnp.zeros_like(dbias_ref)
            dsink_ref[...] = jnp.zeros_like(dsink_ref)

        dout_ref[:, 0:1024] = dq_scr[...].astype(BF16)
        dout_ref[:, 1024:2048] = dz_scr[...].astype(BF16)
        band[...] = jnp.zeros_like(band)

        @pl.when(n < nb)
        def _():
            mask = _att_mask(n)
            lane = lax.broadcasted_iota(jnp.int32, (1, 128), 1)
            dsink = jnp.zeros((1, 128), F32)
            for kv in range(KV):
                sl = slice(kv * DH, (kv + 1) * DH)
                kk = jnp.concatenate([kp_ref[:, sl], kc_ref[:, sl]], axis=0).astype(BF16)
                vv = jnp.concatenate([vp_ref[:, sl], vc_ref[:, sl]], axis=0).astype(BF16)
                dk_acc = jnp.zeros((2 * L, DH), F32)
                dv_acc = jnp.zeros((2 * L, DH), F32)
                for g in range(HEADS // KV):
                    h = kv * (HEADS // KV) + g
                    hs = slice(h * DH, (h + 1) * DH)
                    qh = (q_ref[:, hs] * 0.125).astype(BF16)
                    P, psink = _att_probs(qh, kk, bias_ref[h], mask, s_ref[h])
                    Pb = P.astype(BF16)
                    zh = z_ref[:, hs]
                    sg = _sigmoid(zh)
                    dyh = dy_ref[:, hs]
                    O = _dot(Pb, vv)
                    dO = dyh * (zh * sg)
                    dz_scr[:, hs] = dyh * O * (sg * (1.0 + zh * (1.0 - sg)))
                    dOb = dO.astype(BF16)
                    dv_acc = dv_acc + _dot_tn(P, dOb)
                    dP = _dot_nt(dOb, vv)
                    delta = jnp.sum(P * dP, axis=-1, keepdims=True)
                    dS = P * (dP - delta)
                    dsink = dsink + jnp.where(lane == h, -jnp.sum(psink * delta), 0.0)
                    dSb = dS.astype(BF16)
                    dq_scr[:, hs] = _dot(dSb, kk) * 0.125
                    dk_acc = dk_acc + _dot_tn(dS, qh)
                    dbias_ref[h] += dS
                band[:, sl] = dk_acc
                band[:, 128 + kv * DH:128 + (kv + 1) * DH] = dv_acc
            dsink_ref[...] += dsink

        out = carry[...] + band[0:L, :]
        dout_ref[:, 2048:2304] = out.astype(BF16)
        carry[...] = band[L:2 * L, :]

    cur = lambda n: jnp.minimum(n, last)
    lag = lambda n: jnp.maximum(n - 1, 0)
    return pl.pallas_call(
        body, name="att_bwd", grid=(nb + 1,),
        in_specs=[pl.BlockSpec((L, 1024), lambda n: (cur(n), 0))] + _att_in_specs(nb),
        out_specs=[pl.BlockSpec((L, 2304), lambda n: (lag(n), 0)), _full((HEADS, L, 2 * L)), _full((1, 128))],
        out_shape=[jax.ShapeDtypeStruct((S, 2304), BF16),
                   jax.ShapeDtypeStruct((HEADS, L, 2 * L), F32), jax.ShapeDtypeStruct((1, 128), F32)],
        scratch_shapes=[pltpu.VMEM((L, 256), F32), pltpu.VMEM((2 * L, 256), F32),
                        pltpu.VMEM((L, 1024), F32), pltpu.VMEM((L, 1024), F32)],
        compiler_params=_params(("arbitrary",)),
    )(dy, proj, proj, proj, proj, proj, proj, bias, sinks)


def _sgu_in_specs():
    return [
        pl.BlockSpec((L, 1024), lambda c: (c, 0)),
        pl.BlockSpec((L, 1024), lambda c: (c, 1)),
        pl.BlockSpec((L, 1024), lambda c: (c, 2)),
        _full((1, 1024)), _full((1, 1024)), _full((8, L, L)), _full((L, 8)),
    ]


def _sgu_norm(v, lg, lb):
    mu = jnp.mean(v, axis=-1, keepdims=True)
    vc = v - mu
    rstd = lax.rsqrt(jnp.mean(vc * vc, axis=-1, keepdims=True) + EPS)
    xhat = vc * rstd
    return xhat * lg + lb, xhat, rstd


def _tril():
    return lax.broadcasted_iota(jnp.int32, (L, L), 0) >= lax.broadcasted_iota(jnp.int32, (L, L), 1)


def sgu_fwd(proj, ln_g, ln_b, w, b_t):
    S = proj.shape[0]

    def body(u_ref, v_ref, z_ref, lg_ref, lb_ref, w_ref, bt_ref, y_ref):
        vn, _, _ = _sgu_norm(v_ref[...], lg_ref[...], lb_ref[...])
        tri = _tril()
        parts = []
        for g in range(8):
            wg = jnp.where(tri, w_ref[g], 0.0).astype(BF16)
            parts.append(_dot(wg, vn[:, g * 128:(g + 1) * 128].astype(BF16)) + bt_ref[:, g:g + 1])
        mixed = jnp.concatenate(parts, axis=1)
        z = z_ref[...]
        y_ref[...] = (u_ref[...] * mixed * (z * _sigmoid(z))).astype(BF16)

    return pl.pallas_call(
        body, name="sgu_fwd", grid=(S // L,),
        in_specs=_sgu_in_specs(),
        out_specs=pl.BlockSpec((L, 1024), lambda c: (c, 0)),
        out_shape=jax.ShapeDtypeStruct((S, 1024), BF16),
        compiler_params=_params(("arbitrary",)),
    )(proj, proj, proj, ln_g, ln_b, w, b_t)


def sgu_bwd(dy, proj, ln_g, ln_b, w, b_t):
    S = proj.shape[0]

    def body(dy_ref, u_ref, v_ref, z_ref, lg_ref, lb_ref, w_ref, bt_ref,
             dout_ref, dw_ref, dbt_ref, dlg_ref, dlb_ref):
        @pl.when(pl.program_id(0) == 0)
        def _():
            dw_ref[...] = jnp.zeros_like(dw_ref)
            dbt_ref[...] = jnp.zeros_like(dbt_ref)
            dlg_ref[...] = jnp.zeros_like(dlg_ref)
            dlb_ref[...] = jnp.zeros_like(dlb_ref)

        lg = lg_ref[...]
        vn, xhat, rstd = _sgu_norm(v_ref[...], lg, lb_ref[...])
        tri = _tril()
        lane = lax.broadcasted_iota(jnp.int32, (L, 128), 1)
        wgs, parts = [], []
        for g in range(8):
            wg = jnp.where(tri, w_ref[g], 0.0)
            wgs.append(wg)
            parts.append(_dot(wg.astype(BF16), vn[:, g * 128:(g + 1) * 128].astype(BF16)) + bt_ref[:, g:g + 1])
        mixed = jnp.concatenate(parts, axis=1)
        z = z_ref[...]
        sg = _sigmoid(z)
        silu = z * sg
        dy_v = dy_ref[...]
        u = u_ref[...]
        dout_ref[:, 0:1024] = (dy_v * mixed * silu).astype(BF16)
        dout_ref[:, 2048:3072] = (dy_v * u * mixed * (sg * (1.0 + z * (1.0 - sg)))).astype(BF16)
        dmixed = dy_v * u * silu
        dbt = jnp.zeros((L, 128), F32)
        dvn_parts = []
        for g in range(8):
            dm = dmixed[:, g * 128:(g + 1) * 128]
            dmb = dm.astype(BF16)
            dbt = dbt + jnp.where(lane == g, jnp.sum(dm, axis=1, keepdims=True), 0.0)
            dw_ref[g] += jnp.where(tri, _dot_nt(dmb, vn[:, g * 128:(g + 1) * 128].astype(BF16)), 0.0)
            dvn_parts.append(_dot_tn(wgs[g], dmb))
        dbt_ref[...] += dbt
        dvn = jnp.concatenate(dvn_parts, axis=1)
        dlg_ref[...] += jnp.sum(dvn * xhat, axis=0, keepdims=True)
        dlb_ref[...] += jnp.sum(dvn, axis=0, keepdims=True)
        dxh = dvn * lg
        dv = rstd * (dxh - jnp.mean(dxh, axis=-1, keepdims=True)
                     - xhat * jnp.mean(dxh * xhat, axis=-1, keepdims=True))
        dout_ref[:, 1024:2048] = dv.astype(BF16)

    return pl.pallas_call(
        body, name="sgu_bwd", grid=(S // L,),
        in_specs=[pl.BlockSpec((L, 1024), lambda c: (c, 0))] + _sgu_in_specs(),
        out_specs=[pl.BlockSpec((L, 3072), lambda c: (c, 0)), _full((8, L, L)), _full((L, 128)),
                   _full((1, 1024)), _full((1, 1024))],
        out_shape=[jax.ShapeDtypeStruct((S, 3072), BF16), jax.ShapeDtypeStruct((8, L, L), F32),
                   jax.ShapeDtypeStruct((L, 128), F32), jax.ShapeDtypeStruct((1, 1024), F32),
                   jax.ShapeDtypeStruct((1, 1024), F32)],
        compiler_params=_params(("arbitrary",)),
    )(dy, proj, proj, proj, ln_g, ln_b, w, b_t)


def _expand_matrix():
    r = lax.broadcasted_iota(jnp.int32, (128, SSM_W), 0)
    c = lax.broadcasted_iota(jnp.int32, (128, SSM_W), 1)
    return (c // SSM_P) == r


def _expand_matrix_t():
    r = lax.broadcasted_iota(jnp.int32, (SSM_W, 128), 0)
    c = lax.broadcasted_iota(jnp.int32, (SSM_W, 128), 1)
    return (r // SSM_P) == c


def _rows_from(ref, start):
    C = ref.shape[1]
    tiles = ref[...].reshape(17, 8, C)
    q, s = divmod(start, 8)
    if s == 0:
        return tiles[q:q + 16].reshape(L, C)
    rolled = pltpu.roll(tiles, 8 - s, axis=1)
    sub = lax.broadcasted_iota(jnp.int32, (16, 8, C), 1)
    return jnp.where(sub < 8 - s, rolled[q:q + 16], rolled[q + 1:q + 17]).reshape(L, C)


def _ssd_common(ext_ref, cw_ref, cb_ref, dt_raw, dtb, alog):
    taps = [_rows_from(ext_ref, 5 + k) for k in range(CONV_K)]
    pre = cb_ref[...]
    for k in range(CONV_K):
        pre = pre + cw_ref[k:k + 1, :] * taps[k]
    sg_pre = _sigmoid(pre)
    xc = pre * sg_pre
    dt = _softplus(dt_raw + dtb)
    a = -jnp.exp(alog)
    adt = dt * a
    acs = _sel_dot(_tril(), adt, 3)
    return pre, sg_pre, xc, dt, a, acs, taps


def _ssd_in_specs(rev, nc):
    cidx = (lambda c: nc - 1 - c) if rev else (lambda c: c)
    return [
        pl.BlockSpec((L, 2048), lambda c: (cidx(c), 0)),
        pl.BlockSpec((L, 1024), lambda c: (cidx(c), 2)),
        pl.BlockSpec((L, 1024), lambda c: (cidx(c), 3)),
        pl.BlockSpec((L, 1024), lambda c: (cidx(c), 4)),
        pl.BlockSpec((L, 128), lambda c: (cidx(c), 40)),
        _full((8, CONV_C)), _full((1, CONV_C)), _full((1, 128)), _full((1, 128)), _full((1, 128)),
        _full((1, SSM_W)),
    ]


def ssd_fwd(proj, conv_w, conv_b, dt_bias, a_log, d_skip, norm_g):
    S = proj.shape[0]
    nc = S // L

    def body(z_ref, xa_ref, xb_ref, xc_ref, dt_ref, cw_ref, cb_ref, dtb_ref, alog_ref, dsk_ref, ng_ref,
             y_ref, hs_ref, H, ext, ysc):
        @pl.when(pl.program_id(0) == 0)
        def _():
            H[...] = jnp.zeros_like(H)
            ext[0:8, :] = jnp.zeros((8, CONV_C), F32)

        for k, ref in enumerate((xa_ref, xb_ref, xc_ref)):
            ext[8:8 + L, k * 1024:(k + 1) * 1024] = ref[...]
        pre, sg_pre, xc, dt, a, acs, _ = _ssd_common(ext, cw_ref, cb_ref, dt_ref[...], dtb_ref[...], alog_ref[...])
        for k, ref in enumerate((xa_ref, xb_ref, xc_ref)):
            ext[0:8, k * 1024:(k + 1) * 1024] = ref[L - 8:L, :]
        xs = xc[:, 0:SSM_W]
        acs_t = acs.T
        ex = _expand_matrix()
        dt_x = _dot_sel(dt, ex, 2)
        xdt = xs * dt_x
        eacs_x = _dot_sel(jnp.exp(acs), ex, 2)
        xw = xdt * _dot_sel(jnp.exp(acs[L - 1:L, :] - acs), ex, 2)
        cd_row = jnp.exp(acs[L - 1:L, :])
        hs_ref[0] = H[...]
        tri = _tril()
        for g in range(SSM_G):
            gs = slice(g * 512, (g + 1) * 512)
            bg = xc[:, SSM_W + g * SSM_N:SSM_W + (g + 1) * SSM_N].astype(BF16)
            cg = xc[:, SSM_W + 512 + g * SSM_N:SSM_W + 512 + (g + 1) * SSM_N].astype(BF16)
            G = _dot_nt(cg, bg)
            yoff = _dot_nt(cg, H[gs, :].astype(BF16)) * eacs_x[:, gs]
            Sg = _dot_tn(xw[:, gs], bg)
            for j in range(8):
                hh = g * 8 + j
                hs = slice(hh * SSM_P, (hh + 1) * SSM_P)
                seg = acs[:, hh:hh + 1] - acs_t[hh:hh + 1, :]
                dk = jnp.where(tri, jnp.exp(jnp.minimum(seg, 0.0)), 0.0)
                yd = _dot((G * dk).astype(BF16), xdt[:, hs].astype(BF16))
                ysc[:, hs] = yd + yoff[:, j * SSM_P:(j + 1) * SSM_P]
                H[hs, :] = H[hs, :] * cd_row[:, hh:hh + 1] + Sg[j * SSM_P:(j + 1) * SSM_P, :]
        d_x = _dot_sel(jnp.broadcast_to(dsk_ref[...], (8, 128)), ex, 3)[0:1, :]
        Y = ysc[...] + d_x * xs
        z = z_ref[...]
        yz = Y * (z * _sigmoid(z))
        ng = ng_ref[...]
        for g in range(SSM_G):
            gs = slice(g * 512, (g + 1) * 512)
            t = yz[:, gs]
            rstd = lax.rsqrt(jnp.mean(t * t, axis=-1, keepdims=True) + EPS)
            y_ref[:, gs] = (t * rstd * ng[:, gs]).astype(BF16)

    return pl.pallas_call(
        body, name="ssd_fwd", grid=(nc,),
        in_specs=_ssd_in_specs(False, nc),
        out_specs=[pl.BlockSpec((L, SSM_W), lambda c: (c, 0)), pl.BlockSpec((1, SSM_W, SSM_N), lambda c: (c, 0, 0))],
        out_shape=[jax.ShapeDtypeStruct((S, SSM_W), BF16), jax.ShapeDtypeStruct((nc, SSM_W, SSM_N), F32)],
        scratch_shapes=[pltpu.VMEM((SSM_W, SSM_N), F32), pltpu.VMEM((8 + L, CONV_C), F32),
                        pltpu.VMEM((L, SSM_W), F32)],
        compiler_params=_params(("arbitrary",)),
    )(proj, proj, proj, proj, proj, conv_w, conv_b, dt_bias, a_log, d_skip, norm_g)


def ssd_bwd(dy, proj, hstates, conv_w, conv_b, dt_bias, a_log, d_skip, norm_g):
    S = proj.shape[0]
    nc = S // L
    cidx = lambda c: nc - 1 - c

    def body(dy_ref, z_ref, xa_ref, xb_ref, xc_ref, dt_ref, cw_ref, cb_ref, dtb_ref, alog_ref, dsk_ref, ng_ref,
             pa_ref, pb_ref, pc_ref, hp_ref,
             dout_ref, dcw_ref, dcb_ref, ddtb_ref, dalog_ref, ddsk_ref, dng_ref,
             dH, ext, dext, ysc, yoffsc, dxdt, dxc, tsc):
        step = pl.program_id(0)
        c = nc - 1 - step

        @pl.when(step == 0)
        def _():
            dH[...] = jnp.zeros_like(dH)
            dext[L:L + 8, :] = jnp.zeros((8, CONV_C), F32)
            for r in (dcw_ref, dcb_ref, ddtb_ref, dalog_ref, ddsk_ref, dng_ref):
                r[...] = jnp.zeros_like(r)

        for k, (ref, prev) in enumerate(((xa_ref, pa_ref), (xb_ref, pb_ref), (xc_ref, pc_ref))):
            ext[0:8, k * 1024:(k + 1) * 1024] = jnp.where(c > 0, prev[...], 0.0)
            ext[8:8 + L, k * 1024:(k + 1) * 1024] = ref[...]
        dtb = dtb_ref[...]
        dt_raw = dt_ref[...]
        pre, sg_pre, xc, dt, a, acs, taps = _ssd_common(ext, cw_ref, cb_ref, dt_raw, dtb, alog_ref[...])
        xs = xc[:, 0:SSM_W]
        acs_t = acs.T
        ex = _expand_matrix()
        dt_x = _dot_sel(dt, ex, 2)
        xdt = xs * dt_x
        eacs_x = _dot_sel(jnp.exp(acs), ex, 2)
        dte_x = _dot_sel(jnp.exp(acs[L - 1:L, :] - acs), ex, 2)
        xw = xdt * dte_x
        cd_row = jnp.exp(acs[L - 1:L, :])
        tri = _tril()

        Gs, Cs, Bs = [], [], []
        for g in range(SSM_G):
            gs = slice(g * 512, (g + 1) * 512)
            bg = xc[:, SSM_W + g * SSM_N:SSM_W + (g + 1) * SSM_N].astype(BF16)
            cg = xc[:, SSM_W + 512 + g * SSM_N:SSM_W + 512 + (g + 1) * SSM_N].astype(BF16)
            G = _dot_nt(cg, bg)
            Gs.append(G), Cs.append(cg), Bs.append(bg)
            yoffsc[:, gs] = _dot_nt(cg, hp_ref[0, gs, :].astype(BF16)) * eacs_x[:, gs]
            for j in range(8):
                hh = g * 8 + j
                hs = slice(hh * SSM_P, (hh + 1) * SSM_P)
                seg = acs[:, hh:hh + 1] - acs_t[hh:hh + 1, :]
                dk = jnp.where(tri, jnp.exp(jnp.minimum(seg, 0.0)), 0.0)
                ysc[:, hs] = _dot((G * dk).astype(BF16), xdt[:, hs].astype(BF16))
        d_x = _dot_sel(jnp.broadcast_to(dsk_ref[...], (8, 128)), ex, 3)[0:1, :]
        yoff = yoffsc[...]
        Y = ysc[...] + yoff + d_x * xs

        z = z_ref[...]
        sgz = _sigmoid(z)
        silu_z = z * sgz
        yz = Y * silu_z
        ng = ng_ref[...]
        dout = dy_ref[...]
        dyn = dout * ng
        dyz_parts, dng_parts = [], []
        for g in range(SSM_G):
            gs = slice(g * 512, (g + 1) * 512)
            t = yz[:, gs]
            rstd = lax.rsqrt(jnp.mean(t * t, axis=-1, keepdims=True) + EPS)
            dng_parts.append(jnp.sum(dout[:, gs] * t * rstd, axis=0, keepdims=True))
            dn = dyn[:, gs]
            dyz_parts.append(rstd * dn - t * (rstd * rstd * rstd) * jnp.mean(dn * t, axis=-1, keepdims=True))
        dng_ref[...] += jnp.concatenate(dng_parts, axis=1)
        dyz = jnp.concatenate(dyz_parts, axis=1)
        dY = dyz * silu_z
        dout_ref[:, 0:SSM_W] = (dyz * Y * (sgz * (1.0 + z * (1.0 - sgz)))).astype(BF16)

        ex_t = _expand_matrix_t()
        ddsk_ref[...] += _dot_sel(jnp.broadcast_to(jnp.sum(dY * xs, axis=0, keepdims=True), (8, SSM_W)), ex_t, 3)[0:1, :]

        lane = lax.broadcasted_iota(jnp.int32, (L, 128), 1)
        subl = lax.broadcasted_iota(jnp.int32, (128, L), 0)
        coll = lax.broadcasted_iota(jnp.int32, (128, L), 1)
        r_cols = jnp.zeros((L, 128), F32)
        c_rows = jnp.zeros((128, L), F32)
        for g in range(SSM_G):
            gs = slice(g * 512, (g + 1) * 512)
            G, cg, bg = Gs[g], Cs[g], Bs[g]
            hp_g = hp_ref[0, gs, :]
            dh_g = dH[gs, :]
            dY_g = dY[:, gs]
            dZ = dY_g * eacs_x[:, gs]
            dZb = dZ.astype(BF16)
            dC = _dot(dZb, hp_g.astype(BF16))
            dh_from_off = _dot_tn(dZ, cg)
            dhb = dh_g.astype(BF16)
            Q = _dot_nt(bg, dhb)
            dB = _dot(xw[:, gs].astype(BF16), dhb)
            qd = Q * dte_x[:, gs]
            dxdt[:, gs] = qd
            tsc[:, gs] = qd * xdt[:, gs]
            dG = jnp.zeros((L, L), F32)
            for j in range(8):
                hh = g * 8 + j
                hs = slice(hh * SSM_P, (hh + 1) * SSM_P)
                seg = acs[:, hh:hh + 1] - acs_t[hh:hh + 1, :]
                dk = jnp.where(tri, jnp.exp(jnp.minimum(seg, 0.0)), 0.0)
                M = G * dk
                dYh = dY[:, hs]
                dYhb = dYh.astype(BF16)
                dM = _dot_nt(dYhb, xdt[:, hs].astype(BF16))
                dxdt[:, hs] += _dot_tn(M, dYhb)
                dG = dG + dM * dk
                Wm = dM * M
                r_cols = r_cols + jnp.where(lane == hh, jnp.sum(Wm, axis=1, keepdims=True), 0.0)
                c_rows = c_rows + jnp.where(subl == hh, jnp.sum(Wm, axis=0, keepdims=True), 0.0)
                pj = slice(j * SSM_P, (j + 1) * SSM_P)
                cd_h = cd_row[:, hh:hh + 1]
                dcd = jnp.sum(dh_g[pj, :] * hp_g[pj, :]) * cd_h
                c_rows = c_rows - jnp.where((subl == hh) & (coll == L - 1), dcd, 0.0)
                dH[hs, :] = dh_g[pj, :] * cd_h + dh_from_off[pj, :]
            dGb = dG.astype(BF16)
            dC = dC + _dot(dGb, bg)
            dB = dB + _dot_tn(dG, cg)
            dxc[:, SSM_W + g * SSM_N:SSM_W + (g + 1) * SSM_N] = dB
            dxc[:, SSM_W + 512 + g * SSM_N:SSM_W + 512 + (g + 1) * SSM_N] = dC

        row = lax.broadcasted_iota(jnp.int32, (L, 128), 0)
        tv = tsc[...]
        t_last = _dot_sel(jnp.broadcast_to(jnp.sum(tv, axis=0, keepdims=True), (8, SSM_W)), ex_t, 3)[0:1, :]
        dacs = (r_cols - c_rows.T + _dot_sel(dY * yoff - tv, ex_t, 2) + jnp.where(row == L - 1, t_last, 0.0))
        triu = lax.broadcasted_iota(jnp.int32, (L, L), 0) <= lax.broadcasted_iota(jnp.int32, (L, L), 1)
        dadt = _sel_dot(triu, dacs, 3)
        dxdt_v = dxdt[...]
        ddt = _dot_sel(dxdt_v * xs, ex_t, 2) + dadt * a
        dalog_ref[...] += jnp.sum(dadt * dt * a, axis=0, keepdims=True)
        ddt_raw = jnp.where(lane < SSM_H, ddt * _sigmoid(dt_raw + dtb), 0.0)
        ddtb_ref[...] += jnp.sum(ddt_raw, axis=0, keepdims=True)
        dout_ref[:, 5120:5248] = ddt_raw.astype(BF16)
        dout_ref[:, 5248:5376] = jnp.zeros((L, 128), BF16)

        dxc[:, 0:SSM_W] = dxdt_v * dt_x + d_x * dY
        dpre = dxc[...] * (sg_pre * (1.0 + pre * (1.0 - sg_pre)))
        dcb_ref[...] += jnp.sum(dpre, axis=0, keepdims=True)
        dext[0:L, :] = dpre
        x_cur = ext[8:8 + L, :]
        dx = None
        for k in range(CONV_K):
            dsh = _rows_from(dext, 3 - k)
            term = cw_ref[k:k + 1, :] * dsh
            dx = term if dx is None else dx + term
            dcw_ref[k:k + 1, :] += jnp.sum(dsh * x_cur, axis=0, keepdims=True)
        dout_ref[:, SSM_W:SSM_W + CONV_C] = dx.astype(BF16)
        dext[L:L + 8, :] = dpre[0:8, :]

    big = lambda w: pl.BlockSpec((L, w), lambda c: (cidx(c), 0))
    return pl.pallas_call(
        body, name="ssd_bwd", grid=(nc,),
        in_specs=[big(SSM_W)] + _ssd_in_specs(True, nc) + [
            pl.BlockSpec((8, 1024), lambda c, k=k: (jnp.maximum(16 * cidx(c) - 1, 0), k)) for k in (2, 3, 4)] + [
            pl.BlockSpec((1, SSM_W, SSM_N), lambda c: (cidx(c), 0, 0))],
        out_specs=[big(5376), _full((8, CONV_C)), _full((1, CONV_C)),
                   _full((1, 128)), _full((1, 128)), _full((1, 128)), _full((1, SSM_W))],
        out_shape=[jax.ShapeDtypeStruct((S, 5376), BF16), jax.ShapeDtypeStruct((8, CONV_C), F32),
                   jax.ShapeDtypeStruct((1, CONV_C), F32), jax.ShapeDtypeStruct((1, 128), F32),
                   jax.ShapeDtypeStruct((1, 128), F32), jax.ShapeDtypeStruct((1, 128), F32),
                   jax.ShapeDtypeStruct((1, SSM_W), F32)],
        scratch_shapes=[pltpu.VMEM((SSM_W, SSM_N), F32), pltpu.VMEM((8 + L, CONV_C), F32),
                        pltpu.VMEM((L + 8, CONV_C), F32), pltpu.VMEM((L, SSM_W), F32),
                        pltpu.VMEM((L, SSM_W), F32), pltpu.VMEM((L, SSM_W), F32),
                        pltpu.VMEM((L, CONV_C), F32), pltpu.VMEM((L, SSM_W), F32)],
        compiler_params=_params(("arbitrary",)),
    )(dy, proj, proj, proj, proj, proj, conv_w, conv_b, dt_bias, a_log, d_skip, norm_g, proj, proj, proj, hstates)


def _resident(shape):
    nd = len(shape)
    return pl.BlockSpec(shape, lambda *_: (0,) * nd, pipeline_mode=pl.Buffered(1))


def merge_fwd(y_att, y_sg, y_ssm, proj, x, w_a, w_s, w_m, w_o, g_post):
    S = x.shape[0]
    tm = 256

    def body(ya_ref, ys_ref, ym_ref, gate_ref, x_ref, wa_ref, ws_ref, wm_ref, wo_ref, gp_ref,
             xn_ref, bra_ref, brs_ref, brm_ref, mg_ref, out_ref):
        bra = _dot(ya_ref[...], wa_ref[...])
        brs = _dot(ys_ref[...], ws_ref[...])
        brm = _dot(ym_ref[...], wm_ref[...])
        bra_ref[...] = bra
        brs_ref[...] = brs
        brm_ref[...] = brm
        merged = (_sigmoid(gate_ref[:, 0:1024]) * bra + _sigmoid(gate_ref[:, 1024:2048]) * brs
                  + _sigmoid(gate_ref[:, 2048:3072]) * brm)
        mb = merged.astype(BF16)
        mg_ref[...] = mb
        o = _dot(mb, wo_ref[...])
        out_ref[...] = o
        r = lax.rsqrt(jnp.mean(o * o, axis=-1, keepdims=True) + EPS)
        xn_ref[...] = x_ref[...] + o * r * gp_ref[...]

    row = lambda w: pl.BlockSpec((tm, w), lambda i: (i, 0))
    return pl.pallas_call(
        body, name="merge_fwd", grid=(S // tm,),
        in_specs=[row(1024), row(1024), row(2048), pl.BlockSpec((tm, 3072), lambda i: (i, 0)),
                  row(D), _resident((1024, D)), _resident((1024, D)), _resident((2048, D)), _resident((D, D)),
                  _full((1, D))],
        out_specs=[row(D)] * 6,
        out_shape=[jax.ShapeDtypeStruct((S, D), F32)] * 4 + [jax.ShapeDtypeStruct((S, D), BF16),
                                                             jax.ShapeDtypeStruct((S, D), F32)],
        compiler_params=_params(("arbitrary",)),
    )(y_att, y_sg, y_ssm, proj, x, w_a, w_s, w_m, w_o, g_post)


def merge_bwd(dy, out, g_post, proj, br_a, br_s, br_m, w_a, w_s, w_m, w_o):
    S = dy.shape[0]
    tm = 256

    def body(dy_ref, o_ref, gp_ref, gate_ref, bra_ref, brs_ref, brm_ref, wa_ref, ws_ref, wm_ref, wo_ref,
             dout_ref, dba_ref, dbs_ref, dbm_ref, dgate_ref, dya_ref, dys_ref, dym_ref, dgp_ref):
        @pl.when(pl.program_id(0) == 0)
        def _():
            dgp_ref[...] = jnp.zeros_like(dgp_ref)

        o = o_ref[...]
        dyv = dy_ref[...]
        r = lax.rsqrt(jnp.mean(o * o, axis=-1, keepdims=True) + EPS)
        dyg = dyv * gp_ref[...]
        do = r * dyg - o * (r * r * r) * jnp.mean(dyg * o, axis=-1, keepdims=True)
        dgp_ref[...] += jnp.sum(dyv * o * r, axis=0, keepdims=True)
        dob = do.astype(BF16)
        dout_ref[...] = dob
        dmerged = _dot_nt(dob, wo_ref[...])
        for idx, (br_ref, dbr_ref, w_ref, dyi_ref) in enumerate((
                (bra_ref, dba_ref, wa_ref, dya_ref), (brs_ref, dbs_ref, ws_ref, dys_ref),
                (brm_ref, dbm_ref, wm_ref, dym_ref))):
            s = _sigmoid(gate_ref[:, idx * 1024:(idx + 1) * 1024])
            dbr = (dmerged * s).astype(BF16)
            dbr_ref[...] = dbr
            dgate_ref[:, idx * 1024:(idx + 1) * 1024] = (dmerged * br_ref[...] * s * (1.0 - s)).astype(BF16)
            dyi_ref[...] = _dot_nt(dbr, w_ref[...])

    row = lambda w: pl.BlockSpec((tm, w), lambda i: (i, 0))
    return pl.pallas_call(
        body, name="merge_bwd", grid=(S // tm,),
        in_specs=[row(D), row(D), _full((1, D)), pl.BlockSpec((tm, 3072), lambda i: (i, 0)),
                  row(D), row(D), row(D),
                  _resident((1024, D)), _resident((1024, D)), _resident((2048, D)), _resident((D, D))],
        out_specs=[row(D), row(D), row(D), row(D), row(3072), row(1024), row(1024), row(2048), _full((1, D))],
        out_shape=[jax.ShapeDtypeStruct((S, D), BF16)] * 4 + [
            jax.ShapeDtypeStruct((S, 3072), BF16), jax.ShapeDtypeStruct((S, 1024), F32),
            jax.ShapeDtypeStruct((S, 1024), F32), jax.ShapeDtypeStruct((S, 2048), F32),
            jax.ShapeDtypeStruct((1, D), F32)],
        compiler_params=_params(("arbitrary",)),
    )(dy, out, g_post, proj, br_a, br_s, br_m, w_a, w_s, w_m, w_o)


def loss_head(y, target):
    S = y.shape[0]
    tm = 512

    def body(y_ref, t_ref, dy_ref, loss_ref):
        @pl.when(pl.program_id(0) == 0)
        def _():
            loss_ref[...] = jnp.zeros_like(loss_ref)
        e = y_ref[...] - t_ref[...]
        dy_ref[...] = e * (1.0 / D)
        loss_ref[...] += 0.5 * jnp.sum(jnp.mean(e * e, axis=-1, keepdims=True))

    row = pl.BlockSpec((tm, D), lambda i: (i, 0))
    return pl.pallas_call(
        body, name="loss_head", grid=(S // tm,),
        in_specs=[row, row], out_specs=[row, _full((1, 128))],
        out_shape=[jax.ShapeDtypeStruct((S, D), F32), jax.ShapeDtypeStruct((1, 128), F32)],
        compiler_params=_params(("arbitrary",)),
    )(y, target)


def _adam(w, g, m, v):
    mn = ADAM_B1 * m + (1.0 - ADAM_B1) * g
    vn = ADAM_B2 * v + (1.0 - ADAM_B2) * (g * g)
    m_hat = mn / (1.0 - ADAM_B1 ** ADAM_STEP)
    v_hat = vn / (1.0 - ADAM_B2 ** ADAM_STEP)
    return -ADAM_LR * (m_hat / (jnp.sqrt(v_hat) + ADAM_EPS) + ADAM_WD * w), mn, vn


def adamw_big(w, m, v, halves0, sum1, cc, name, tr):
    _, R, C = w.shape
    nper = R // tr
    f, fb, n0, off_a, off_b = halves0
    p, pb, off1 = sum1

    def body(c_ref, w_ref, m_ref, v_ref, f_ref, fb_ref, p_ref, pb_ref, g_ref, d_ref, nm_ref, nv_ref):
        i = pl.program_id(0)
        half = jnp.where(i % nper >= n0, 1, 0)
        g0 = jnp.where(c_ref[0] == half, f_ref[...], fb_ref[...])
        g = jnp.where(i < nper, g0, p_ref[...] + pb_ref[...])
        g_ref[0] = g
        d_ref[0], nm_ref[0], nv_ref[0] = _adam(w_ref[0], g, m_ref[0], v_ref[0])

    def blk0(i, c):
        il = jnp.minimum(i, nper - 1)
        return (jnp.where(il >= n0, off_b + il - n0, off_a + il), 0)

    wblk = pl.BlockSpec((1, tr, C), lambda i, c: (i // nper, i % nper, 0))
    b0 = pl.BlockSpec((tr, C), blk0)
    b1 = pl.BlockSpec((tr, C), lambda i, c: (off1 + jnp.maximum(i - nper, 0), 0))
    grid_spec = pltpu.PrefetchScalarGridSpec(
        num_scalar_prefetch=1, grid=(2 * nper,),
        in_specs=[wblk, wblk, wblk, b0, b0, b1, b1], out_specs=[wblk] * 4)
    return pl.pallas_call(
        body, name=name, grid_spec=grid_spec,
        out_shape=[jax.ShapeDtypeStruct(w.shape, F32)] * 4,
        compiler_params=_params(("arbitrary",)),
    )(cc, w, m, v, f, fb, p, pb)


def adamw_plain(w, g, m, v, name):
    def body(w_ref, g_ref, m_ref, v_ref, d_ref, nm_ref, nv_ref):
        d_ref[...], nm_ref[...], nv_ref[...] = _adam(w_ref[...], g_ref[...], m_ref[...], v_ref[...])

    return pl.pallas_call(
        body, name=name, out_shape=[jax.ShapeDtypeStruct(w.shape, F32)] * 3, compiler_params=_params(),
    )(w, g, m, v)


SMALL = {"norm_pre": ("g_pre", 8), "norm_post": ("g_post", 8), "att_sinks": ("sinks", 8), "sg_ln_g": ("ln_g", 8),
         "sg_ln_b": ("ln_b", 8), "sg_w": ("sg_w", 1024), "sg_b": ("sg_bt", 8), "ssm_conv_b": ("conv_b", 24),
         "ssm_dt_bias": ("dt_bias", 8), "ssm_a_log": ("a_log", 8), "ssm_d": ("d_skip", 8), "ssm_norm_g": ("norm_g", 16)}
SMALL_LAYER_ROWS = sum(r for _, r in SMALL.values())
REL_ROW = DEPTH * SMALL_LAYER_ROWS
LOSS_ROW = REL_ROW + 32
SMALL_ROWS = LOSS_ROW + 8


def _small_rows():
    rows, r = {}, 0
    for l in range(DEPTH):
        for name, (_, n) in SMALL.items():
            rows[(l, name)] = r
            r += n
    return rows


def adamw_small(red, rel, small):
    names = list(SMALL) + ["rel_bias"]
    params = dict(small, rel_bias=rel)
    rows = _small_rows()

    def grad_of(red_ref, l, name, n):
        r0 = rows[(l, name)]
        if name == "sg_b":
            return red_ref[r0:r0 + 8, :]
        if n < 128:
            return red_ref[r0:r0 + 1, 0:n]
        return jnp.concatenate([red_ref[r0 + j:r0 + j + 1, :] for j in range(n // 128)], axis=1)

    def body(red_ref, *refs):
        ins, outs = refs[:3 * len(names)], refs[3 * len(names):]
        for i, name in enumerate(names):
            w_ref, m_ref, v_ref = ins[3 * i:3 * i + 3]
            o = outs[4 * i:4 * i + 4]
            if name == "rel_bias":
                g = red_ref[REL_ROW:REL_ROW + 32, 0:16]
                o[0][...] = g
                o[1][...], o[2][...], o[3][...] = _adam(w_ref[...], g, m_ref[...], v_ref[...])
                continue
            for l in range(DEPTH):
                if name == "sg_w":
                    for grp in range(8):
                        r0 = rows[(l, name)] + grp * 128
                        g = red_ref[r0:r0 + 128, :]
                        o[0][l, grp] = g
                        o[1][l, grp], o[2][l, grp], o[3][l, grp] = _adam(w_ref[l, grp], g, m_ref[l, grp], v_ref[l, grp])
                elif name == "sg_b":
                    g = grad_of(red_ref, l, name, 128)
                    o[0][l] = g
                    o[1][l], o[2][l], o[3][l] = _adam(w_ref[l], g, m_ref[l], v_ref[l])
                else:
                    sl = slice(l, l + 1)
                    g = grad_of(red_ref, l, name, w_ref.shape[-1])
                    o[0][sl, :] = g
                    o[1][sl, :], o[2][sl, :], o[3][sl, :] = _adam(w_ref[sl, :], g, m_ref[sl, :], v_ref[sl, :])

    flat_in = [a for name in names for a in params[name]]
    out_shape = [jax.ShapeDtypeStruct(params[name][0].shape, F32) for name in names for _ in range(4)]
    res = pl.pallas_call(body, name="adamw_small", out_shape=out_shape, compiler_params=_params())(red, *flat_in)
    return {name: tuple(res[4 * i:4 * i + 4]) for i, name in enumerate(names)}


ANY = pl.BlockSpec(memory_space=pl.ANY)


def _place():
    x, y, c = lax.axis_index("x"), lax.axis_index("y"), lax.axis_index("c")
    others = [(1 - x, y), (x, 1 - y), (1 - x, 1 - y)]
    return x, y, c, others


def _rcopy(src, dst, ssem, rsem, to):
    return pltpu.make_async_remote_copy(src_ref=src, dst_ref=dst, send_sem=ssem, recv_sem=rsem,
                                        device_id=to, device_id_type=MESH)


def gather_weights(arrs):
    n = len(arrs)

    def body(*refs):
        srcs, outs, ssem, rsem = refs[:n], refs[n:2 * n], refs[2 * n], refs[2 * n + 1]
        x, y, c, others = _place()
        me = 2 * x + y
        sib = (x, y, 1 - c)
        first = [_rcopy(srcs[i].at[c], outs[i].at[c, me], ssem.at[6 * i + k], rsem.at[6 * i + k], (ox, oy, c))
                 for i in range(n) for k, (ox, oy) in enumerate(others)]
        for cp in first:
            cp.start()
        passed = []
        for k, (ox, oy) in enumerate(others):
            for i in range(n):
                slot = outs[i].at[c, 2 * ox + oy]
                _rcopy(slot, slot, ssem.at[6 * i + k], rsem.at[6 * i + k], sib).wait_recv()
                fw = _rcopy(slot, slot, ssem.at[6 * i + 3 + k], rsem.at[6 * i + 3 + k], sib)
                fw.start()
                passed.append(fw)
        for k, (ox, oy) in enumerate(others):
            for i in range(n):
                slot = outs[i].at[1 - c, 2 * ox + oy]
                _rcopy(slot, slot, ssem.at[6 * i + 3 + k], rsem.at[6 * i + 3 + k], sib).wait_recv()
        for cp in first + passed:
            cp.wait_send()

    return pl.pallas_call(
        body, name="gather_weights",
        in_specs=[ANY] * n, out_specs=[ANY] * n,
        out_shape=[jax.ShapeDtypeStruct((2, SHARDS) + a.shape[1:], a.dtype) for a in arrs],
        scratch_shapes=[pltpu.SemaphoreType.DMA((6 * n,)), pltpu.SemaphoreType.DMA((6 * n,))],
    )(*arrs)


HBM = pl.BlockSpec(memory_space=pltpu.HBM)
SEM = pl.BlockSpec(memory_space=pltpu.SEMAPHORE)
EFFECT = pltpu.SideEffectType.DATAFLOW_SIDE_EFFECTING


def _in_hbm(a):
    return pltpu.with_memory_space_constraint(a, pltpu.HBM)


def gather_start(srcs, after, name, by_dest=False):
    n = len(srcs)
    lands = [_in_hbm(lax.empty((SHARDS,) + a.shape[-2:], a.dtype)) for a in srcs]
    na = len(after)

    def body(*refs):
        src, land = refs[:n], refs[n:2 * n]
        ssem, rsem, token = refs[2 * n + na], refs[2 * n + na + 1], refs[-1]
        x, y, c, others = _place()
        me = 2 * x + y
        for i in range(n):
            for k, (ox, oy) in enumerate(others):
                s = src[i].at[2 * ox + oy] if by_dest else src[i]
                _rcopy(s, land[i].at[me], ssem.at[3 * i + k], rsem.at[3 * i + k], (ox, oy, c)).start()
        token[...] = jnp.zeros_like(token)

    bufs = [_in_hbm(a) for a in srcs] + lands
    out = pl.pallas_call(
        body, name=name,
        out_shape=(pltpu.SemaphoreType.DMA((3 * n,)), pltpu.SemaphoreType.DMA((3 * n,)),
                   *[pltpu.HBM(b.shape, b.dtype) for b in bufs], jax.ShapeDtypeStruct((8, 128), F32)),
        in_specs=[HBM] * (2 * n) + [ANY] * na,
        out_specs=(SEM, SEM, *[HBM] * (2 * n), pl.BlockSpec(memory_space=pltpu.VMEM)),
        input_output_aliases={i: 2 + i for i in range(2 * n)},
        compiler_params=pltpu.CompilerParams(has_side_effects=EFFECT),
    )(*bufs, *after)
    return out[0], out[1], list(out[2:2 + n]), list(out[2 + n:2 + 2 * n]), out[-1]


def gather_wait(ssem, rsem, srcs, lands, after, name, by_dest=False):
    n = len(srcs)

    def body(*refs):
        src, land = refs[:n], refs[n:2 * n]
        s_sem, r_sem = refs[2 * n], refs[2 * n + 1]
        x, y, c, others = _place()
        for i in range(n):
            for k, (ox, oy) in enumerate(others):
                s = src[i].at[2 * ox + oy] if by_dest else src[i]
                cp = _rcopy(s, land[i].at[2 * ox + oy], s_sem.at[3 * i + k], r_sem.at[3 * i + k], (ox, oy, c))
                cp.wait_send()
                cp.wait_recv()

    bufs = list(srcs) + list(lands)
    out = pl.pallas_call(
        body, name=name,
        out_shape=tuple(pltpu.HBM(b.shape, b.dtype) for b in bufs),
        in_specs=[HBM] * (2 * n) + [SEM, SEM, ANY],
        out_specs=tuple([HBM] * (2 * n)),
        input_output_aliases={i: i for i in range(2 * n)},
        compiler_params=pltpu.CompilerParams(has_side_effects=EFFECT),
    )(*bufs, ssem, rsem, after)
    return list(out[n:2 * n])


def grad_sibling_exchange(arrs):
    n = len(arrs)

    def body(*refs):
        srcs, outs, ssem, rsem = refs[:n], refs[n:2 * n], refs[2 * n], refs[2 * n + 1]
        x, y, c, _ = _place()
        cps = [_rcopy(srcs[i].at[1 - c], outs[i], ssem.at[i], rsem.at[i], (x, y, 1 - c)) for i in range(n)]
        for cp in cps:
            cp.start()
        for cp in cps:
            cp.wait()

    return pl.pallas_call(
        body, name="grad_sibling_exchange",
        in_specs=[ANY] * n, out_specs=[ANY] * n,
        out_shape=[jax.ShapeDtypeStruct(a.shape[1:], F32) for a in arrs],
        scratch_shapes=[pltpu.SemaphoreType.DMA((n,)), pltpu.SemaphoreType.DMA((n,))],
    )(*arrs)


def grad_chip_sum(g, sb, cc, tr, name):
    _, _, R, C = g.shape
    blk = pl.BlockSpec((1, tr, C), lambda s, r, c: (s, r, 0))
    grid_spec = pltpu.PrefetchScalarGridSpec(
        num_scalar_prefetch=1, grid=(SHARDS, R // tr),
        in_specs=[pl.BlockSpec((1, 1, tr, C), lambda s, r, c: (c[0], s, r, 0)), blk],
        out_specs=[blk, blk])

    def body(c_ref, a_ref, b_ref, o_ref, ob_ref):
        t = a_ref[0] + b_ref[...]
        o_ref[...] = t
        ob_ref[...] = t.astype(BF16)

    return pl.pallas_call(
        body, name=name, grid_spec=grid_spec,
        out_shape=[jax.ShapeDtypeStruct((SHARDS, R, C), F32), jax.ShapeDtypeStruct((SHARDS, R, C), BF16)],
        compiler_params=_params(("arbitrary", "arbitrary")),
    )(cc, g, sb)


def grad_chip_exchange(arrs):
    n = len(arrs)

    def body(*refs):
        srcs, outs, ssem, rsem = refs[:n], refs[n:2 * n], refs[2 * n], refs[2 * n + 1]
        x, y, c, others = _place()
        me = 2 * x + y
        sends = [_rcopy(srcs[i].at[2 * ox + oy], outs[i].at[me], ssem.at[3 * i + k], rsem.at[3 * i + k], (ox, oy, c))
                 for i in range(n) for k, (ox, oy) in enumerate(others)]
        for cp in sends:
            cp.start()
        for i in range(n):
            for k, (ox, oy) in enumerate(others):
                slot = outs[i].at[2 * ox + oy]
                _rcopy(slot, slot, ssem.at[3 * i + k], rsem.at[3 * i + k], (ox, oy, c)).wait_recv()
        for cp in sends:
            cp.wait_send()

    return pl.pallas_call(
        body, name="grad_chip_exchange",
        in_specs=[ANY] * n, out_specs=[ANY] * n,
        out_shape=[jax.ShapeDtypeStruct(a.shape, a.dtype) for a in arrs],
        scratch_shapes=[pltpu.SemaphoreType.DMA((3 * n,)), pltpu.SemaphoreType.DMA((3 * n,))],
    )(*arrs)


def grad_shard_sum(t, rb, me, tr, name):
    _, R, C = t.shape
    grid_spec = pltpu.PrefetchScalarGridSpec(
        num_scalar_prefetch=1, grid=(R // tr,),
        in_specs=[pl.BlockSpec((1, tr, C), lambda r, m: (m[0], r, 0)),
                  pl.BlockSpec((SHARDS, tr, C), lambda r, m: (0, r, 0))],
        out_specs=pl.BlockSpec((tr, C), lambda r, m: (r, 0)))

    def body(m_ref, t_ref, r_ref, o_ref):
        part = [jnp.where(m_ref[0] == s, t_ref[0], r_ref[s].astype(F32)) for s in range(SHARDS)]
        o_ref[...] = ((part[0] + part[1]) + part[2]) + part[3]

    return pl.pallas_call(
        body, name=name, grid_spec=grid_spec,
        out_shape=jax.ShapeDtypeStruct((R, C), F32),
        compiler_params=_params(("arbitrary",)),
    )(me, t, rb)


def grad_sibling_share(arrs, name):
    n = len(arrs)

    def body(*refs):
        srcs, outs, ssem, rsem = refs[:n], refs[n:2 * n], refs[2 * n], refs[2 * n + 1]
        x, y, c, _ = _place()
        cps = [_rcopy(srcs[i], outs[i], ssem.at[i], rsem.at[i], (x, y, 1 - c)) for i in range(n)]
        for cp in cps:
            cp.start()
        for cp in cps:
            cp.wait()

    return pl.pallas_call(
        body, name=name,
        in_specs=[ANY] * n, out_specs=[ANY] * n,
        out_shape=[jax.ShapeDtypeStruct(a.shape, F32) for a in arrs],
        scratch_shapes=[pltpu.SemaphoreType.DMA((n,)), pltpu.SemaphoreType.DMA((n,))],
    )(*arrs)


def _allreduce_rows(src, sib_buf, chips, out_ref, ssem, rsem):
    x, y, c, others = _place()
    me = 2 * x + y
    cp = _rcopy(src, sib_buf, ssem.at[0], rsem.at[0], (x, y, 1 - c))
    cp.start()
    cp.wait()
    chips[me] = src[...] + sib_buf[...]
    sends = [_rcopy(chips.at[me], chips.at[me], ssem.at[1 + k], rsem.at[1 + k], (ox, oy, c))
             for k, (ox, oy) in enumerate(others)]
    for s in sends:
        s.start()
    for k, (ox, oy) in enumerate(others):
        slot = chips.at[2 * ox + oy]
        _rcopy(slot, slot, ssem.at[1 + k], rsem.at[1 + k], (ox, oy, c)).wait_recv()
    for s in sends:
        s.wait_send()
    out_ref[...] = ((chips[0] + chips[1]) + chips[2]) + chips[3]


def _allreduce_scratch(rows):
    return [pltpu.VMEM((rows, 128), F32), pltpu.VMEM((SHARDS, rows, 128), F32),
            pltpu.SemaphoreType.DMA((4,)), pltpu.SemaphoreType.DMA((4,))]


def allreduce_rows(buf, name):
    rows = buf.shape[0]
    VM = pl.BlockSpec(memory_space=pltpu.VMEM)

    def body(src_ref, out_ref, sib_buf, chips, ssem, rsem):
        _allreduce_rows(src_ref, sib_buf, chips, out_ref, ssem, rsem)

    return pl.pallas_call(
        body, name=name, in_specs=[VM], out_specs=VM,
        out_shape=jax.ShapeDtypeStruct((rows, 128), F32),
        scratch_shapes=_allreduce_scratch(rows), compiler_params=_params(),
    )(buf)


def small_allreduce(grads, rel, loss_part):
    rows = _small_rows()
    keys = [(l, name) for l in range(DEPTH) for name in SMALL]
    flat = [grads[l][SMALL[name][0]] for l, name in keys] + [rel, loss_part]

    def body(*refs):
        ins = refs[:len(flat)]
        out_ref, src, sib_buf, chips, ssem, rsem = refs[len(flat):]
        src[...] = jnp.zeros_like(src)
        for (l, name), ref in zip(keys, ins):
            r0 = rows[(l, name)]
            if name == "sg_w":
                for grp in range(8):
                    src[r0 + grp * 128:r0 + (grp + 1) * 128, :] = ref[grp]
            elif name == "sg_b":
                src[r0:r0 + 8, :] = ref[...].T[0:8, :]
            else:
                for j in range(ref.shape[1] // 128):
                    src[r0 + j:r0 + j + 1, :] = ref[:, j * 128:(j + 1) * 128]
        src[REL_ROW:REL_ROW + 32, 0:16] = ins[-2][...]
        src[LOSS_ROW:LOSS_ROW + 1, :] = ins[-1][...]
        _allreduce_rows(src, sib_buf, chips, out_ref, ssem, rsem)

    return pl.pallas_call(
        body, name="small_allreduce",
        out_shape=jax.ShapeDtypeStruct((SMALL_ROWS, 128), F32),
        scratch_shapes=[pltpu.VMEM((SMALL_ROWS, 128), F32)] + _allreduce_scratch(SMALL_ROWS),
        compiler_params=_params(),
    )(*flat)


def _pad_lanes(v):
    return jnp.zeros((1, 128), F32).at[0, :v.shape[0]].set(v)


def layer_fwd(x, wts, bias):
    wt = wts["wt"]
    tn = {name: t for name, _, t in GROUPS}
    p_gate, h = inproj_first(x, wts["g_pre"], wt["gate"], tn["gate"], "inproj_gate")
    p_sgu, p_att, p_ssd = (inproj_group(h, wt[n], tn[n], "inproj_" + n) for n in ("sgu", "att", "ssd"))
    y_att = att_fwd(p_att, bias, wts["sinks"])
    y_sg = sgu_fwd(p_sgu, wts["ln_g"], wts["ln_b"], wts["sg_w"], wts["sg_bt"])
    y_ssm, hst = ssd_fwd(p_ssd, wts["conv_w"], wts["conv_b"], wts["dt_bias"], wts["a_log"], wts["d_skip"],
                         wts["norm_g"])
    x_new, br_a, br_s, br_m, merged, out = merge_fwd(
        y_att, y_sg, y_ssm, p_gate, x, wts["w_a"], wts["w_s"], wts["w_m"], wts["w_o"], wts["g_post"])
    saved = dict(x=x, p_gate=p_gate, p_sgu=p_sgu, p_att=p_att, p_ssd=p_ssd, h=h,
                 y_att=y_att, y_sg=y_sg, y_ssm=y_ssm, hst=hst,
                 br_a=br_a, br_s=br_s, br_m=br_m, merged=merged, out=out)
    return x_new, saved


def layer_bwd(dy, wts, bias, sv):
    dout, dba, dbs, dbm, d_gate, dya, dys, dym, dg_post = merge_bwd(
        dy, sv["out"], wts["g_post"], sv["p_gate"], sv["br_a"], sv["br_s"], sv["br_m"],
        wts["w_a"], wts["w_s"], wts["w_m"], wts["w_o"])
    d_att, dbias, dsinks = att_bwd(dya, sv["p_att"], bias, wts["sinks"])
    d_sgu, dsg_w, dsg_bt, dln_g, dln_b = sgu_bwd(dys, sv["p_sgu"], wts["ln_g"], wts["ln_b"], wts["sg_w"],
                                                 wts["sg_bt"])
    d_ssd, dcw, dcb, ddtb, dalog, ddsk, dng = ssd_bwd(
        dym, sv["p_ssd"], sv["hst"], wts["conv_w"], wts["conv_b"], wts["dt_bias"], wts["a_log"], wts["d_skip"],
        wts["norm_g"])
    dps = dict(gate=d_gate, sgu=d_sgu, att=d_att, ssd=d_ssd)
    wt = wts["wt"]
    tn = {name: t for name, _, t in GROUPS}
    acc = None
    for n in ("gate", "sgu", "ssd"):
        acc = dh_group(dps[n], wt[n], acc, tn[n], "dh_" + n)
    dx, dg_pre = dh_last(dps["att"], wt["att"], acc, sv["x"], wts["g_pre"], dy, tn["att"], "dh_att")
    grads = dict(
        w_in={n: dw_group(dps[n], sv["h"], tn[n], "dw_in_" + n) for n in dps},
        w_a=matmul_tn(sv["y_att"], dba, "dw_att"),
        w_s=matmul_tn(sv["y_sg"], dbs, "dw_sg"),
        w_m=matmul_tn(sv["y_ssm"], dbm, "dw_ssm"),
        w_o=matmul_tn(sv["merged"], dout, "dw_out"),
        g_pre=dg_pre, g_post=dg_post, sinks=dsinks, ln_g=dln_g, ln_b=dln_b, sg_w=dsg_w, sg_bt=dsg_bt,
        conv_w=dcw, conv_b=dcb, dt_bias=ddtb, a_log=dalog, d_skip=ddsk, norm_g=dng, bias=dbias)
    return dx, grads


REST_OFF = (0, 256, 512, 1024, 1280)
GR_ROWS = 1536
GR_CONV = 1280
W_IN_SPLIT = 1600
W_IN_HALF = 1824


def kernel(x, w_in, norm_pre, norm_post, rel_bias, att_sinks, sg_ln_g, sg_ln_b, sg_w, sg_b, ssm_conv_w, ssm_conv_b, ssm_dt_bias, ssm_a_log, ssm_d, ssm_norm_g, w_br_att, w_br_sg, w_br_ssm, w_out, loss_target, m_w_in, m_norm_pre, m_norm_post, m_rel_bias, m_att_sinks, m_sg_ln_g, m_sg_ln_b, m_sg_w, m_sg_b, m_ssm_conv_w, m_ssm_conv_b, m_ssm_dt_bias, m_ssm_a_log, m_ssm_d, m_ssm_norm_g, m_w_br_att, m_w_br_sg, m_w_br_ssm, m_w_out, v_w_in, v_norm_pre, v_norm_post, v_rel_bias, v_att_sinks, v_sg_ln_g, v_sg_ln_b, v_sg_w, v_sg_b, v_ssm_conv_w, v_ssm_conv_b, v_ssm_dt_bias, v_ssm_a_log, v_ssm_d, v_ssm_norm_g, v_w_br_att, v_w_br_sg, v_w_br_ssm, v_w_out):
    cx, cy, cc = lax.axis_index("x"), lax.axis_index("y"), lax.axis_index("c")
    me = 2 * cx + cy
    xs = x[0]
    S = xs.shape[0]

    tr = lambda a: jnp.transpose(a, (0, 2, 1))
    w_in_b = tr(w_in).astype(BF16)
    w_rest_b = jnp.concatenate([w_br_att, w_br_sg, w_br_ssm, w_out], axis=1).astype(BF16)
    halves = lambda a: a.reshape(2, a.shape[0] // 2, a.shape[1])
    all0_in, all0_rest = gather_weights([halves(w_in_b[0]), halves(w_rest_b[0])])
    convw_slot = jnp.zeros((SHARDS, DEPTH * CONV_K * 768 // 128, 128), F32)
    convw_slot = lax.dynamic_update_index_in_dim(
        convw_slot, jnp.where(cc == 0, 1.0, 0.0) * ssm_conv_w.reshape(-1, 128), me, 0)
    convw_rows = allreduce_rows(convw_slot.reshape(-1, 128), "gather_conv_w")
    convw_all = convw_rows.reshape(SHARDS, DEPTH, CONV_K, 768).transpose(1, 2, 0, 3).reshape(DEPTH, CONV_K, CONV_C)
    g1_ssem, g1_rsem, g1_srcs, g1_lands, g1_token = gather_start(
        [w_in_b[1], w_rest_b[1]], [convw_rows, all0_rest], "gather_l1_start")

    o = REST_OFF

    def layer_weights(l, gathered_in, gathered_rest, g_pre):
        sh_in = [jnp.where(me == s, w_in_b[l], gathered_in[s]) for s in range(SHARDS)]
        sh_rest = [jnp.where(me == s, w_rest_b[l], gathered_rest[s]) for s in range(SHARDS)]
        rest = lambda k: jnp.concatenate([r[o[k]:o[k + 1]] for r in sh_rest], axis=0)
        return dict(
            wt=group_weights(jnp.concatenate(sh_in, axis=0)),
            w_a=rest(0), w_s=rest(1), w_m=rest(2), w_o=rest(3),
            g_pre=g_pre, g_post=norm_post[l][None], sinks=att_sinks[l],
            ln_g=sg_ln_g[l][None], ln_b=sg_ln_b[l][None], sg_w=sg_w[l],
            sg_bt=sg_b[l].T,
            conv_w=jnp.concatenate([convw_all[l], jnp.zeros((4, CONV_C), F32)], axis=0),
            conv_b=ssm_conv_b[l][None], dt_bias=_pad_lanes(ssm_dt_bias[l]), a_log=_pad_lanes(ssm_a_log[l]),
            d_skip=_pad_lanes(ssm_d[l]), norm_g=ssm_norm_g[l][None])

    bias = bias_table(rel_bias)
    layers = [layer_weights(0, [all0_in[:, s].reshape(3400, D) for s in range(SHARDS)],
                            [all0_rest[:, s].reshape(1280, D) for s in range(SHARDS)],
                            (norm_pre[0] + g1_token[0, 0])[None])]
    act, sv0 = layer_fwd(xs, layers[0], bias)
    land_in, land_rest = gather_wait(g1_ssem, g1_rsem, g1_srcs, g1_lands, act, "gather_l1_wait")
    layers.append(layer_weights(1, land_in, land_rest, norm_pre[1][None]))
    act, sv1 = layer_fwd(act, layers[1], bias)
    saved = [sv0, sv1]
    dy, loss_part = loss_head(act, loss_target[0])
    cvec = jnp.reshape(cc, (1,)).astype(jnp.int32)
    mvec = jnp.reshape(me, (1,)).astype(jnp.int32)

    def by_shard(g):
        gcw = g["conv_w"][0:CONV_K].reshape(CONV_K, SHARDS, 768).transpose(1, 0, 2).reshape(SHARDS, 3, 1024)
        rest = jnp.concatenate([
            g["w_a"].reshape(SHARDS, 256, D), g["w_s"].reshape(SHARDS, 256, D), g["w_o"].reshape(SHARDS, 256, D),
            g["w_m"].reshape(SHARDS, 512, D), jnp.pad(gcw, ((0, 0), (0, GR_ROWS - GR_CONV - 3), (0, 0)))], axis=1)
        return ungroup_grads(g["w_in"]).reshape(SHARDS, 3400, D), rest

    grads = [None] * DEPTH
    dy, grads[1] = layer_bwd(dy, layers[1], bias, saved[1])
    g1_in, g1_rest = by_shard(grads[1])
    g1_in = jnp.pad(g1_in, ((0, 0), (0, W_IN_ROWS - 3400), (0, 0)))
    x1_ssem, x1_rsem, x1_srcs, x1_lands, x1_token = gather_start(
        [g1_in.astype(BF16), g1_rest.astype(BF16)], [], "grads_l1_start", by_dest=True)
    wts0 = dict(layers[0], g_post=layers[0]["g_post"] + x1_token[0, 0])
    dy, grads[0] = layer_bwd(dy, wts0, bias, saved[0])
    grad_x = dy[None]
    r1_in, r1_rest = gather_wait(x1_ssem, x1_rsem, x1_srcs, x1_lands, dy, "grads_l1_wait", by_dest=True)
    p_in = grad_shard_sum(g1_in, r1_in, mvec, 384, "l1_sum_w_in")
    p_rest = grad_shard_sum(g1_rest, r1_rest, mvec, 512, "l1_sum_rest")
    pb_in, pb_rest = grad_sibling_share([p_in, p_rest], "l1_sibling_share")
    grad_rel_local = bias_grad(grads[0]["bias"] + grads[1]["bias"])

    g0_in, g0_rest = by_shard(grads[0])
    pad_to = lambda a, rows: jnp.pad(a, ((0, 0), (0, rows - a.shape[1]), (0, 0)))
    g0_in = jnp.stack([pad_to(g0_in[:, 0:W_IN_SPLIT], W_IN_HALF), pad_to(g0_in[:, W_IN_SPLIT:3400], W_IN_HALF)])
    g0_rest = jnp.stack([g0_rest[:, 0:GR_ROWS // 2], g0_rest[:, GR_ROWS // 2:GR_ROWS]])
    sb_in, sb_rest = grad_sibling_exchange([g0_in, g0_rest])
    t_in, t_in_b = grad_chip_sum(g0_in, sb_in, cvec, 608, "chip_sum_w_in")
    t_rest, t_rest_b = grad_chip_sum(g0_rest, sb_rest, cvec, 384, "chip_sum_rest")
    rb_in, rb_rest = grad_chip_exchange([t_in_b, t_rest_b])
    f_in = grad_shard_sum(t_in, rb_in, mvec, 608, "shard_sum_w_in")
    f_rest = grad_shard_sum(t_rest, rb_rest, mvec, 384, "shard_sum_rest")
    fb_in, fb_rest = grad_sibling_share([f_in, f_rest], "l0_sibling_share")

    red = small_allreduce(grads, grad_rel_local, loss_part)
    loss = red[LOSS_ROW, 0]

    res = adamw_small(red, (rel_bias, m_rel_bias, v_rel_bias), dict(
        norm_pre=(norm_pre, m_norm_pre, v_norm_pre), norm_post=(norm_post, m_norm_post, v_norm_post),
        att_sinks=(att_sinks, m_att_sinks, v_att_sinks), sg_ln_g=(sg_ln_g, m_sg_ln_g, v_sg_ln_g),
        sg_ln_b=(sg_ln_b, m_sg_ln_b, v_sg_ln_b), sg_w=(sg_w, m_sg_w, v_sg_w), sg_b=(sg_b, m_sg_b, v_sg_b),
        ssm_conv_b=(ssm_conv_b, m_ssm_conv_b, v_ssm_conv_b), ssm_dt_bias=(ssm_dt_bias, m_ssm_dt_bias, v_ssm_dt_bias),
        ssm_a_log=(ssm_a_log, m_ssm_a_log, v_ssm_a_log), ssm_d=(ssm_d, m_ssm_d, v_ssm_d),
        ssm_norm_g=(ssm_norm_g, m_ssm_norm_g, v_ssm_norm_g)))
    res["w_in"] = tuple(tr(a) for a in adamw_big(
        tr(w_in), tr(m_w_in), tr(v_w_in), (f_in, fb_in, W_IN_SPLIT // 200, 0, 0), (p_in, pb_in, 0), cvec, "adamw_w_in", 200))
    rest_upd = lambda w, m, v, name, n0, off0, off1: adamw_big(
        w, m, v, (f_rest, fb_rest, n0, off0, off0), (p_rest, pb_rest, off1), cvec, name, 256)
    res["w_br_att"] = rest_upd(w_br_att, m_w_br_att, v_w_br_att, "adamw_w_br_att", 1, 0, 0)
    res["w_br_sg"] = rest_upd(w_br_sg, m_w_br_sg, v_w_br_sg, "adamw_w_br_sg", 1, 1, 1)
    res["w_out"] = rest_upd(w_out, m_w_out, v_w_out, "adamw_w_out", 1, 2, 2)
    res["w_br_ssm"] = rest_upd(w_br_ssm, m_w_br_ssm, v_w_br_ssm, "adamw_w_br_ssm", 0, 0, 3)
    cw0 = jnp.where(cc == 1, f_rest, fb_rest)[GR_CONV - GR_ROWS // 2:GR_CONV - GR_ROWS // 2 + 3]
    cw1 = (p_rest + pb_rest)[GR_CONV:GR_CONV + 3]
    g_conv_w = jnp.stack([cw0.reshape(CONV_K, 768), cw1.reshape(CONV_K, 768)])
    res["ssm_conv_w"] = (g_conv_w,) + tuple(adamw_plain(ssm_conv_w, g_conv_w, m_ssm_conv_w, v_ssm_conv_w, "adamw_conv_w"))

    order = ["w_in", "norm_pre", "norm_post", "rel_bias", "att_sinks", "sg_ln_g", "sg_ln_b", "sg_w", "sg_b",
             "ssm_conv_w", "ssm_conv_b", "ssm_dt_bias", "ssm_a_log", "ssm_d", "ssm_norm_g",
             "w_br_att", "w_br_sg", "w_br_ssm", "w_out"]
    return (loss, grad_x, *[res[n][0] for n in order], *[res[n][1] for n in order],
            *[res[n][2] for n in order], *[res[n][3] for n in order])
```

```python
import functools
import math

import numpy as np
import jax
import jax.numpy as jnp
from jax import lax
from jax.experimental import pallas as pl
from jax.experimental.pallas import tpu as pltpu

F32 = jnp.float32
BF16 = jnp.bfloat16
MESH = pl.DeviceIdType.MESH

D = 1024
DEPTH = 2
EPS = 1e-6
L = 128
HEADS = 16
KV = 2
DH = 64
SSM_W = 2048
SSM_H = 32
SSM_P = 64
SSM_G = 4
SSM_N = 128
CONV_K = 4
CONV_C = 3072
NEG = -1e30
IN_COLS = 13600

GROUPS = (("gate", 3072, 1536), ("sgu", 3072, 1536), ("att", 2304, 2304), ("ssd", 5376, 1792))
W_IN_ROWS = 3456

ADAM_LR = 0.001
ADAM_B1 = 0.9
ADAM_B2 = 0.999
ADAM_EPS = 1e-08
ADAM_WD = 0.01
ADAM_STEP = 10

VMEM_LIMIT = 56 * 1024 * 1024

SHARDS = 4


def _dot(a, b):
    return jnp.dot(a, b, preferred_element_type=F32)


def _dot_nt(a, b):
    return lax.dot_general(a, b, (((1,), (1,)), ((), ())), preferred_element_type=F32)


def _dot_tn(a_f32, b):
    return jnp.dot(a_f32.T.astype(BF16), b, preferred_element_type=F32)


def _dot_hi(a, b):
    return jnp.dot(a, b, preferred_element_type=F32, precision=lax.Precision.HIGHEST)


def _pieces(x, n):
    out = []
    for _ in range(n - 1):
        p = x.astype(BF16)
        out.append(p)
        x = x - p.astype(F32)
    out.append(x.astype(BF16))
    return out


def _dot_sel(a, sel, n):
    sel = sel.astype(BF16)
    acc = None
    for p in _pieces(a, n):
        t = _dot(p, sel)
        acc = t if acc is None else acc + t
    return acc


def _sel_dot(sel, b, n):
    sel = sel.astype(BF16)
    acc = None
    for p in _pieces(b, n):
        t = _dot(sel, p)
        acc = t if acc is None else acc + t
    return acc


def _sigmoid(x):
    return 1.0 / (1.0 + jnp.exp(-x))


def _softplus(x):
    return jnp.maximum(x, 0.0) + jnp.log(1.0 + jnp.exp(-jnp.abs(x)))


def _params(sem=None, vmem=VMEM_LIMIT):
    kw = dict(vmem_limit_bytes=vmem)
    if sem is not None:
        kw["dimension_semantics"] = sem
    return pltpu.CompilerParams(**kw)


def _full(shape):
    nd = len(shape)
    return pl.BlockSpec(shape, lambda *_: (0,) * nd)


def group_weights(wt):
    return dict(
        gate=wt[10528:13600],
        sgu=wt[2304:5376],
        att=jnp.concatenate([wt[0:1024], wt[1280:2304], wt[1024:1280]], axis=0),
        ssd=jnp.concatenate([wt[5376:10496], wt[10496:10528], jnp.zeros((224, D), wt.dtype)], axis=0))


def ungroup_grads(g):
    a, s = g["att"], g["ssd"]
    return jnp.concatenate([a[0:1024], a[2048:2304], a[1024:2048], g["sgu"], s[0:5152], g["gate"]], axis=0)


def _bucket_table():
    qi = np.arange(L)[:, None]
    kj = np.arange(2 * L)[None, :]
    dist = np.maximum(qi + L - kj, 0)
    dist_f = np.maximum(dist, 1).astype(np.float32)
    large = 16 + (np.log(dist_f / np.float32(16)) / np.float32(math.log(128 / 16)) * np.float32(16)).astype(np.int32)
    large = np.minimum(large, 31)
    return np.where(dist < 16, dist, large).astype(np.int32)


def bias_table(rel_bias):
    buckets = jnp.asarray(_bucket_table().reshape(1, L * 2 * L))

    def body(rb_ref, bk_ref, out_ref):
        onehot = (lax.broadcasted_iota(jnp.int32, (32, L * 2 * L), 0) == bk_ref[...]).astype(F32)
        out_ref[...] = lax.dot_general(rb_ref[...], onehot, (((0,), (0,)), ((), ())),
                                       preferred_element_type=F32, precision=lax.Precision.HIGHEST)

    out = pl.pallas_call(
        body, name="bias_table",
        out_shape=jax.ShapeDtypeStruct((HEADS, L * 2 * L), F32),
        compiler_params=_params(),
    )(rel_bias, buckets)
    return out.reshape(HEADS, L, 2 * L)


def bias_grad(dbias):
    buckets = jnp.asarray(_bucket_table().reshape(1, L * 2 * L))

    def body(db_ref, bk_ref, out_ref):
        onehot = (lax.broadcasted_iota(jnp.int32, (32, L * 2 * L), 0) == bk_ref[...]).astype(F32)
        out_ref[...] = lax.dot_general(onehot, db_ref[...], (((1,), (1,)), ((), ())),
                                       preferred_element_type=F32, precision=lax.Precision.HIGHEST)

    return pl.pallas_call(
        body, name="bias_grad",
        out_shape=jax.ShapeDtypeStruct((32, HEADS), F32),
        compiler_params=_params(),
    )(dbias.reshape(HEADS, L * 2 * L), buckets)


def _row_tile(S):
    return 1024 if S % 1024 == 0 else 512


def inproj_first(x, g_pre, wt, tn, name):
    S, W = x.shape[0], wt.shape[0]
    tm = _row_tile(S)

    def body(x_ref, g_ref, w_ref, o_ref, h_ref):
        @pl.when(pl.program_id(1) == 0)
        def _():
            xv = x_ref[...]
            r = lax.rsqrt(jnp.mean(xv * xv, axis=-1, keepdims=True) + EPS)
            h_ref[...] = (xv * r * g_ref[...]).astype(BF16)
        o_ref[...] = _dot_nt(h_ref[...], w_ref[...])

    return pl.pallas_call(
        body, name=name, grid=(S // tm, W // tn),
        in_specs=[pl.BlockSpec((tm, D), lambda i, j: (i, 0)), _full((1, D)),
                  pl.BlockSpec((tn, D), lambda i, j: (j, 0))],
        out_specs=[pl.BlockSpec((tm, tn), lambda i, j: (i, j)), pl.BlockSpec((tm, D), lambda i, j: (i, 0))],
        out_shape=[jax.ShapeDtypeStruct((S, W), F32), jax.ShapeDtypeStruct((S, D), BF16)],
        compiler_params=_params(("arbitrary", "arbitrary")),
    )(x, g_pre, wt)


def inproj_group(h, wt, tn, name):
    S, W = h.shape[0], wt.shape[0]
    tm = _row_tile(S)

    def body(h_ref, w_ref, o_ref):
        o_ref[...] = _dot_nt(h_ref[...], w_ref[...])

    return pl.pallas_call(
        body, name=name, grid=(S // tm, W // tn),
        in_specs=[pl.BlockSpec((tm, D), lambda i, j: (i, 0)), pl.BlockSpec((tn, D), lambda i, j: (j, 0))],
        out_specs=pl.BlockSpec((tm, tn), lambda i, j: (i, j)),
        out_shape=jax.ShapeDtypeStruct((S, W), F32),
        compiler_params=_params(("arbitrary", "arbitrary")),
    )(h, wt)


def dh_group(dp, wt, acc, tk, name):
    S, W = dp.shape
    tm = _row_tile(S)

    def body(*refs):
        dp_ref, w_ref, o_ref = refs[0], refs[1], refs[-1]
        first = pl.program_id(1) == 0
        if acc is None:
            @pl.when(first)
            def _():
                o_ref[...] = jnp.zeros_like(o_ref)
        else:
            @pl.when(first)
            def _():
                o_ref[...] = refs[2][...]
        o_ref[...] += _dot(dp_ref[...], w_ref[...])

    row = pl.BlockSpec((tm, D), lambda i, k: (i, 0))
    return pl.pallas_call(
        body, name=name, grid=(S // tm, W // tk),
        in_specs=[pl.BlockSpec((tm, tk), lambda i, k: (i, k)), pl.BlockSpec((tk, D), lambda i, k: (k, 0))]
        + ([] if acc is None else [row]),
        out_specs=row, out_shape=jax.ShapeDtypeStruct((S, D), F32),
        input_output_aliases={} if acc is None else {2: 0},
        compiler_params=_params(("arbitrary", "arbitrary")),
    )(*((dp, wt) if acc is None else (dp, wt, acc)))


def dh_last(dp, wt, acc_in, x, g_pre, dy, tk, name):
    S, W = dp.shape
    tm = 512
    nk = W // tk

    def body(dp_ref, w_ref, a_ref, x_ref, g_ref, dy_ref, dx_ref, dg_ref, acc):
        i, k = pl.program_id(0), pl.program_id(1)

        @pl.when(k == 0)
        def _():
            acc[...] = a_ref[...]

        acc[...] += _dot(dp_ref[...], w_ref[...])

        @pl.when((k == nk - 1) & (i == 0))
        def _():
            dg_ref[...] = jnp.zeros_like(dg_ref)

        @pl.when(k == nk - 1)
        def _():
            xv = x_ref[...]
            dh = acc[...]
            g = g_ref[...]
            r = lax.rsqrt(jnp.mean(xv * xv, axis=-1, keepdims=True) + EPS)
            dhg = dh * g
            dx_ref[...] = dy_ref[...] + r * dhg - xv * (r * r * r) * jnp.mean(dhg * xv, axis=-1, keepdims=True)
            dg_ref[...] += jnp.sum(dh * xv * r, axis=0, keepdims=True)

    row = pl.BlockSpec((tm, D), lambda i, k: (i, 0))
    return pl.pallas_call(
        body, name=name, grid=(S // tm, nk),
        in_specs=[pl.BlockSpec((tm, tk), lambda i, k: (i, k)), pl.BlockSpec((tk, D), lambda i, k: (k, 0)),
                  row, row, _full((1, D)), row],
        out_specs=[row, _full((1, D))],
        out_shape=[jax.ShapeDtypeStruct((S, D), F32), jax.ShapeDtypeStruct((1, D), F32)],
        scratch_shapes=[pltpu.VMEM((tm, D), F32)],
        compiler_params=_params(("arbitrary", "arbitrary")),
    )(dp, wt, acc_in, x, g_pre, dy)


def dw_group(dp, h, tn, name, ts=512):
    S, W = dp.shape

    def body(dp_ref, h_ref, o_ref):
        @pl.when(pl.program_id(1) == 0)
        def _():
            o_ref[...] = jnp.zeros_like(o_ref)
        o_ref[...] += _dot_tn(dp_ref[...].astype(F32), h_ref[...])

    return pl.pallas_call(
        body, name=name, grid=(W // tn, S // ts),
        in_specs=[pl.BlockSpec((ts, tn), lambda j, s: (s, j)), pl.BlockSpec((ts, D), lambda j, s: (s, 0))],
        out_specs=pl.BlockSpec((tn, D), lambda j, s: (j, 0)),
        out_shape=jax.ShapeDtypeStruct((W, D), F32),
        compiler_params=_params(("arbitrary", "arbitrary")),
    )(dp, h)


def matmul_tn(a, b, name, tn=512, ts=512):
    S, K = a.shape
    N = b.shape[1]
    ns = S // ts

    def body(a_ref, b_ref, o_ref):
        @pl.when(pl.program_id(1) == 0)
        def _():
            o_ref[...] = jnp.zeros_like(o_ref)
        o_ref[...] += _dot_tn(a_ref[...].astype(F32), b_ref[...])

    return pl.pallas_call(
        body, name=name, grid=(N // tn, ns),
        in_specs=[pl.BlockSpec((ts, K), lambda j, s: (s, 0)), pl.BlockSpec((ts, tn), lambda j, s: (s, j))],
        out_specs=pl.BlockSpec((K, tn), lambda j, s: (0, j)),
        out_shape=jax.ShapeDtypeStruct((K, N), F32),
        compiler_params=_params(("arbitrary", "arbitrary")),
    )(a, b)


def _att_mask(n):
    qi = lax.broadcasted_iota(jnp.int32, (L, 2 * L), 0)
    kj = lax.broadcasted_iota(jnp.int32, (L, 2 * L), 1)
    dist = qi + L - kj
    return (dist >= 0) & (dist < L) & ((kj >= L) | (n > 0))


def _att_in_specs(nb):
    last = nb - 1
    cur = lambda n: jnp.minimum(n, last)
    prev = lambda n: jnp.maximum(jnp.minimum(n, last) - 1, 0)
    return [
        pl.BlockSpec((L, 1024), lambda n: (cur(n), 0)),
        pl.BlockSpec((L, 128), lambda n: (prev(n), 16)),
        pl.BlockSpec((L, 128), lambda n: (cur(n), 16)),
        pl.BlockSpec((L, 128), lambda n: (prev(n), 17)),
        pl.BlockSpec((L, 128), lambda n: (cur(n), 17)),
        pl.BlockSpec((L, 1024), lambda n: (cur(n), 1)),
        _full((HEADS, L, 2 * L)),
        pl.BlockSpec(memory_space=pltpu.SMEM),
    ]


GH = HEADS // KV
GB = 8


def _att_mask_rows(n, nh):
    qi = lax.broadcasted_iota(jnp.int32, (nh * L, 2 * L), 0) & (L - 1)
    kj = lax.broadcasted_iota(jnp.int32, (nh * L, 2 * L), 1)
    dist = qi + L - kj
    return (dist >= 0) & (dist < L) & ((kj >= L) | (n > 0))


def _stack_heads(ref, h0, nh, scr):
    for g in range(nh):
        scr[(h0 + g) * L:(h0 + g + 1) * L, :] = ref[:, (h0 + g) * DH:(h0 + g + 1) * DH]
    return scr[h0 * L:(h0 + nh) * L, :]


def _unstack_heads(val, h0, nh, ref):
    for g in range(nh):
        ref[:, (h0 + g) * DH:(h0 + g + 1) * DH] = val[g * L:(g + 1) * L, :]


def _sink_rows(s_ref, h0, nh):
    return jnp.concatenate([jnp.full((L, 1), s_ref[h0 + g], F32) for g in range(nh)], axis=0)


def _att_probs(qh, kk, bias_h, mask, sk):
    logits = _dot_nt(qh, kk) + bias_h
    logits = jnp.where(mask, logits, NEG)
    m = jnp.maximum(jnp.max(logits, axis=-1, keepdims=True), sk)
    p = jnp.exp(logits - m)
    es = jnp.exp(sk - m)
    den = jnp.sum(p, axis=-1, keepdims=True) + es
    return p / den, es / den


def att_fwd(proj, bias, sinks):
    S = proj.shape[0]
    nb = S // L

    def body(q_ref, kp_ref, kc_ref, vp_ref, vc_ref, z_ref, bias_ref, s_ref, y_ref, o_scr):
        mask = _att_mask(pl.program_id(0))
        for kv in range(KV):
            sl = slice(kv * DH, (kv + 1) * DH)
            kk = jnp.concatenate([kp_ref[:, sl], kc_ref[:, sl]], axis=0).astype(BF16)
            vv = jnp.concatenate([vp_ref[:, sl], vc_ref[:, sl]], axis=0).astype(BF16)
            for g in range(GH):
                h = kv * GH + g
                hs = slice(h * DH, (h + 1) * DH)
                qh = (q_ref[:, hs] * 0.125).astype(BF16)
                P, _ = _att_probs(qh, kk, bias_ref[h], mask, s_ref[h])
                o_scr[:, hs] = _dot(P.astype(BF16), vv)
        z = z_ref[...]
        y_ref[...] = (o_scr[...] * (z * _sigmoid(z))).astype(BF16)

    return pl.pallas_call(
        body, name="att_fwd", grid=(nb,),
        in_specs=_att_in_specs(nb),
        out_specs=pl.BlockSpec((L, 1024), lambda n: (n, 0)),
        out_shape=jax.ShapeDtypeStruct((S, 1024), BF16),
        scratch_shapes=[pltpu.VMEM((L, 1024), F32)],
        compiler_params=_params(("arbitrary",)),
    )(proj, proj, proj, proj, proj, proj, bias, sinks)


def att_bwd(dy, proj, bias, sinks):
    S = proj.shape[0]
    nb = S // L
    last = nb - 1

    def body(dy_ref, q_ref, kp_ref, kc_ref, vp_ref, vc_ref, z_ref, bias_ref, s_ref,
             dout_ref, dbias_ref, dsink_ref, carry, band, dq_scr, dz_scr, qs_scr, zs_scr, dys_scr):
        n = pl.program_id(0)

        @pl.when(n == 0)
        def _():
            carry[...] = jnp.zeros_like(carry)
            dq_scr[...] = jnp.zeros_like(dq_scr)
            dz_scr[...] = jnp.zeros_like(dz_scr)
            dbias_ref[...] = jnp.zeros_like(dbias_ref)
            dsink_ref[...] = jnp.zeros_like(dsink_ref)

        dout_ref[:, 0:1024] = dq_scr[...].astype(BF16)
        dout_ref[:, 1024:2048] = dz_scr[...].astype(BF16)
        band[...] = jnp.zeros_like(band)

        @pl.when(n < nb)
        def _():
            mask = _att_mask_rows(n, GB)
            lane = lax.broadcasted_iota(jnp.int32, (1, 128), 1)
            dsink = jnp.zeros((1, 128), F32)
            for kv in range(KV):
                sl = slice(kv * DH, (kv + 1) * DH)
                kk = jnp.concatenate([kp_ref[:, sl], kc_ref[:, sl]], axis=0).astype(BF16)
                vv = jnp.concatenate([vp_ref[:, sl], vc_ref[:, sl]], axis=0).astype(BF16)
                dk_acc = jnp.zeros((2 * L, DH), F32)
                dv_acc = jnp.zeros((2 * L, DH), F32)
                for h0 in range(kv * GH, (kv + 1) * GH, GB):
                    qs = (_stack_heads(q_ref, h0, GB, qs_scr) * 0.125).astype(BF16)
                    bias_g = bias_ref[h0:h0 + GB].reshape(GB * L, 2 * L)
                    P, psink = _att_probs(qs, kk, bias_g, mask, _sink_rows(s_ref, h0, GB))
                    zs = _stack_heads(z_ref, h0, GB, zs_scr)
                    dys = _stack_heads(dy_ref, h0, GB, dys_scr)
                    sg = _sigmoid(zs)
                    O = _dot(P.astype(BF16), vv)
                    _unstack_heads(dys * O * (sg * (1.0 + zs * (1.0 - sg))), h0, GB, dz_scr)
                    dOb = (dys * (zs * sg)).astype(BF16)
                    dP = _dot_nt(dOb, vv)
                    delta = jnp.sum(P * dP, axis=-1, keepdims=True)
                    dS = P * (dP - delta)
                    sd = psink * delta
                    for g in range(GB):
                        dsink = dsink + jnp.where(lane == h0 + g, -jnp.sum(sd[g * L:(g + 1) * L, :]), 0.0)
                    _unstack_heads(_dot(dS.astype(BF16), kk) * 0.125, h0, GB, dq_scr)
                    dbias_ref[h0:h0 + GB] += dS.reshape(GB, L, 2 * L)
                    dk_acc = dk_acc + _dot_tn(dS, qs)
                    dv_acc = dv_acc + _dot_tn(P, dOb)
                band[:, sl] = dk_acc
                band[:, 128 + kv * DH:128 + (kv + 1) * DH] = dv_acc
            dsink_ref[...] += dsink

        out = carry[...] + band[0:L, :]
        dout_ref[:, 2048:2304] = out.astype(BF16)
        carry[...] = band[L:2 * L, :]

    cur = lambda n: jnp.minimum(n, last)
    lag = lambda n: jnp.maximum(n - 1, 0)
    return pl.pallas_call(
        body, name="att_bwd", grid=(nb + 1,),
        in_specs=[pl.BlockSpec((L, 1024), lambda n: (cur(n), 0))] + _att_in_specs(nb),
        out_specs=[pl.BlockSpec((L, 2304), lambda n: (lag(n), 0)), _full((HEADS, L, 2 * L)), _full((1, 128))],
        out_shape=[jax.ShapeDtypeStruct((S, 2304), BF16),
                   jax.ShapeDtypeStruct((HEADS, L, 2 * L), F32), jax.ShapeDtypeStruct((1, 128), F32)],
        scratch_shapes=[pltpu.VMEM((L, 256), F32), pltpu.VMEM((2 * L, 256), F32),
                        pltpu.VMEM((L, 1024), F32), pltpu.VMEM((L, 1024), F32)]
        + [pltpu.VMEM((HEADS * L, DH), F32)] * 3,
        compiler_params=_params(("arbitrary",)),
    )(dy, proj, proj, proj, proj, proj, proj, bias, sinks)


def _sgu_in_specs():
    return [
        pl.BlockSpec((L, 1024), lambda c: (c, 0)),
        pl.BlockSpec((L, 1024), lambda c: (c, 1)),
        pl.BlockSpec((L, 1024), lambda c: (c, 2)),
        _full((1, 1024)), _full((1, 1024)), _full((8, L, L)), _full((L, 8)),
    ]


def _sgu_norm(v, lg, lb):
    mu = jnp.mean(v, axis=-1, keepdims=True)
    vc = v - mu
    rstd = lax.rsqrt(jnp.mean(vc * vc, axis=-1, keepdims=True) + EPS)
    xhat = vc * rstd
    return xhat * lg + lb, xhat, rstd


def _tril():
    return lax.broadcasted_iota(jnp.int32, (L, L), 0) >= lax.broadcasted_iota(jnp.int32, (L, L), 1)


def sgu_fwd(proj, ln_g, ln_b, w, b_t):
    S = proj.shape[0]

    def body(u_ref, v_ref, z_ref, lg_ref, lb_ref, w_ref, bt_ref, y_ref):
        vn, _, _ = _sgu_norm(v_ref[...], lg_ref[...], lb_ref[...])
        tri = _tril()
        parts = []
        for g in range(8):
            wg = jnp.where(tri, w_ref[g], 0.0).astype(BF16)
            parts.append(_dot(wg, vn[:, g * 128:(g + 1) * 128].astype(BF16)) + bt_ref[:, g:g + 1])
        mixed = jnp.concatenate(parts, axis=1)
        z = z_ref[...]
        y_ref[...] = (u_ref[...] * mixed * (z * _sigmoid(z))).astype(BF16)

    return pl.pallas_call(
        body, name="sgu_fwd", grid=(S // L,),
        in_specs=_sgu_in_specs(),
        out_specs=pl.BlockSpec((L, 1024), lambda c: (c, 0)),
        out_shape=jax.ShapeDtypeStruct((S, 1024), BF16),
        compiler_params=_params(("arbitrary",)),
    )(proj, proj, proj, ln_g, ln_b, w, b_t)


def sgu_bwd(dy, proj, ln_g, ln_b, w, b_t):
    S = proj.shape[0]

    def body(dy_ref, u_ref, v_ref, z_ref, lg_ref, lb_ref, w_ref, bt_ref,
             dout_ref, dw_ref, dbt_ref, dlg_ref, dlb_ref):
        @pl.when(pl.program_id(0) == 0)
        def _():
            dw_ref[...] = jnp.zeros_like(dw_ref)
            dbt_ref[...] = jnp.zeros_like(dbt_ref)
            dlg_ref[...] = jnp.zeros_like(dlg_ref)
            dlb_ref[...] = jnp.zeros_like(dlb_ref)

        lg = lg_ref[...]
        vn, xhat, rstd = _sgu_norm(v_ref[...], lg, lb_ref[...])
        tri = _tril()
        lane = lax.broadcasted_iota(jnp.int32, (L, 128), 1)
        wgs, parts = [], []
        for g in range(8):
            wg = jnp.where(tri, w_ref[g], 0.0)
            wgs.append(wg)
            parts.append(_dot(wg.astype(BF16), vn[:, g * 128:(g + 1) * 128].astype(BF16)) + bt_ref[:, g:g + 1])
        mixed = jnp.concatenate(parts, axis=1)
        z = z_ref[...]
        sg = _sigmoid(z)
        silu = z * sg
        dy_v = dy_ref[...]
        u = u_ref[...]
        dout_ref[:, 0:1024] = (dy_v * mixed * silu).astype(BF16)
        dout_ref[:, 2048:3072] = (dy_v * u * mixed * (sg * (1.0 + z * (1.0 - sg)))).astype(BF16)
        dmixed = dy_v * u * silu
        dbt = jnp.zeros((L, 128), F32)
        dvn_parts = []
        for g in range(8):
            dm = dmixed[:, g * 128:(g + 1) * 128]
            dmb = dm.astype(BF16)
            dbt = dbt + jnp.where(lane == g, jnp.sum(dm, axis=1, keepdims=True), 0.0)
            dw_ref[g] += jnp.where(tri, _dot_nt(dmb, vn[:, g * 128:(g + 1) * 128].astype(BF16)), 0.0)
            dvn_parts.append(_dot_tn(wgs[g], dmb))
        dbt_ref[...] += dbt
        dvn = jnp.concatenate(dvn_parts, axis=1)
        dlg_ref[...] += jnp.sum(dvn * xhat, axis=0, keepdims=True)
        dlb_ref[...] += jnp.sum(dvn, axis=0, keepdims=True)
        dxh = dvn * lg
        dv = rstd * (dxh - jnp.mean(dxh, axis=-1, keepdims=True)
                     - xhat * jnp.mean(dxh * xhat, axis=-1, keepdims=True))
        dout_ref[:, 1024:2048] = dv.astype(BF16)

    return pl.pallas_call(
        body, name="sgu_bwd", grid=(S // L,),
        in_specs=[pl.BlockSpec((L, 1024), lambda c: (c, 0))] + _sgu_in_specs(),
        out_specs=[pl.BlockSpec((L, 3072), lambda c: (c, 0)), _full((8, L, L)), _full((L, 128)),
                   _full((1, 1024)), _full((1, 1024))],
        out_shape=[jax.ShapeDtypeStruct((S, 3072), BF16), jax.ShapeDtypeStruct((8, L, L), F32),
                   jax.ShapeDtypeStruct((L, 128), F32), jax.ShapeDtypeStruct((1, 1024), F32),
                   jax.ShapeDtypeStruct((1, 1024), F32)],
        compiler_params=_params(("arbitrary",)),
    )(dy, proj, proj, proj, ln_g, ln_b, w, b_t)


def _expand_matrix():
    r = lax.broadcasted_iota(jnp.int32, (128, SSM_W), 0)
    c = lax.broadcasted_iota(jnp.int32, (128, SSM_W), 1)
    return (c // SSM_P) == r


def _expand_matrix_t():
    r = lax.broadcasted_iota(jnp.int32, (SSM_W, 128), 0)
    c = lax.broadcasted_iota(jnp.int32, (SSM_W, 128), 1)
    return (r // SSM_P) == c


def _rows_from(ref, start):
    C = ref.shape[1]
    tiles = ref[...].reshape(17, 8, C)
    q, s = divmod(start, 8)
    if s == 0:
        return tiles[q:q + 16].reshape(L, C)
    rolled = pltpu.roll(tiles, 8 - s, axis=1)
    sub = lax.broadcasted_iota(jnp.int32, (16, 8, C), 1)
    return jnp.where(sub < 8 - s, rolled[q:q + 16], rolled[q + 1:q + 17]).reshape(L, C)


def _ssd_common(ext_ref, cw_ref, cb_ref, dt_raw, dtb, alog):
    taps = [_rows_from(ext_ref, 5 + k) for k in range(CONV_K)]
    pre = cb_ref[...]
    for k in range(CONV_K):
        pre = pre + cw_ref[k:k + 1, :] * taps[k]
    sg_pre = _sigmoid(pre)
    xc = pre * sg_pre
    dt = _softplus(dt_raw + dtb)
    a = -jnp.exp(alog)
    adt = dt * a
    acs = _sel_dot(_tril(), adt, 3)
    return pre, sg_pre, xc, dt, a, acs, taps


def _ssd_in_specs(rev, nc):
    cidx = (lambda c: nc - 1 - c) if rev else (lambda c: c)
    return [
        pl.BlockSpec((L, 2048), lambda c: (cidx(c), 0)),
        pl.BlockSpec((L, 1024), lambda c: (cidx(c), 2)),
        pl.BlockSpec((L, 1024), lambda c: (cidx(c), 3)),
        pl.BlockSpec((L, 1024), lambda c: (cidx(c), 4)),
        pl.BlockSpec((L, 128), lambda c: (cidx(c), 40)),
        _full((8, CONV_C)), _full((1, CONV_C)), _full((1, 128)), _full((1, 128)), _full((1, 128)),
        _full((1, SSM_W)),
    ]


def ssd_fwd(proj, conv_w, conv_b, dt_bias, a_log, d_skip, norm_g):
    S = proj.shape[0]
    nc = S // L

    def body(z_ref, xa_ref, xb_ref, xc_ref, dt_ref, cw_ref, cb_ref, dtb_ref, alog_ref, dsk_ref, ng_ref,
             y_ref, hs_ref, H, ext, ysc):
        @pl.when(pl.program_id(0) == 0)
        def _():
            H[...] = jnp.zeros_like(H)
            ext[0:8, :] = jnp.zeros((8, CONV_C), F32)

        for k, ref in enumerate((xa_ref, xb_ref, xc_ref)):
            ext[8:8 + L, k * 1024:(k + 1) * 1024] = ref[...]
        pre, sg_pre, xc, dt, a, acs, _ = _ssd_common(ext, cw_ref, cb_ref, dt_ref[...], dtb_ref[...], alog_ref[...])
        for k, ref in enumerate((xa_ref, xb_ref, xc_ref)):
            ext[0:8, k * 1024:(k + 1) * 1024] = ref[L - 8:L, :]
        xs = xc[:, 0:SSM_W]
        acs_t = acs.T
        ex = _expand_matrix()
        dt_x = _dot_sel(dt, ex, 2)
        xdt = xs * dt_x
        eacs_x = _dot_sel(jnp.exp(acs), ex, 2)
        xw = xdt * _dot_sel(jnp.exp(acs[L - 1:L, :] - acs), ex, 2)
        cd_row = jnp.exp(acs[L - 1:L, :])
        hs_ref[0] = H[...]
        tri = _tril()
        for g in range(SSM_G):
            gs = slice(g * 512, (g + 1) * 512)
            bg = xc[:, SSM_W + g * SSM_N:SSM_W + (g + 1) * SSM_N].astype(BF16)
            cg = xc[:, SSM_W + 512 + g * SSM_N:SSM_W + 512 + (g + 1) * SSM_N].astype(BF16)
            G = _dot_nt(cg, bg)
            yoff = _dot_nt(cg, H[gs, :].astype(BF16)) * eacs_x[:, gs]
            Sg = _dot_tn(xw[:, gs], bg)
            for j in range(8):
                hh = g * 8 + j
                hs = slice(hh * SSM_P, (hh + 1) * SSM_P)
                seg = acs[:, hh:hh + 1] - acs_t[hh:hh + 1, :]
                dk = jnp.where(tri, jnp.exp(jnp.minimum(seg, 0.0)), 0.0)
                yd = _dot((G * dk).astype(BF16), xdt[:, hs].astype(BF16))
                ysc[:, hs] = yd + yoff[:, j * SSM_P:(j + 1) * SSM_P]
                H[hs, :] = H[hs, :] * cd_row[:, hh:hh + 1] + Sg[j * SSM_P:(j + 1) * SSM_P, :]
        d_x = _dot_sel(jnp.broadcast_to(dsk_ref[...], (8, 128)), ex, 3)[0:1, :]
        Y = ysc[...] + d_x * xs
        z = z_ref[...]
        yz = Y * (z * _sigmoid(z))
        ng = ng_ref[...]
        for g in range(SSM_G):
            gs = slice(g * 512, (g + 1) * 512)
            t = yz[:, gs]
            rstd = lax.rsqrt(jnp.mean(t * t, axis=-1, keepdims=True) + EPS)
            y_ref[:, gs] = (t * rstd * ng[:, gs]).astype(BF16)

    return pl.pallas_call(
        body, name="ssd_fwd", grid=(nc,),
        in_specs=_ssd_in_specs(False, nc),
        out_specs=[pl.BlockSpec((L, SSM_W), lambda c: (c, 0)), pl.BlockSpec((1, SSM_W, SSM_N), lambda c: (c, 0, 0))],
        out_shape=[jax.ShapeDtypeStruct((S, SSM_W), BF16), jax.ShapeDtypeStruct((nc, SSM_W, SSM_N), F32)],
        scratch_shapes=[pltpu.VMEM((SSM_W, SSM_N), F32), pltpu.VMEM((8 + L, CONV_C), F32),
                        pltpu.VMEM((L, SSM_W), F32)],
        compiler_params=_params(("arbitrary",)),
    )(proj, proj, proj, proj, proj, conv_w, conv_b, dt_bias, a_log, d_skip, norm_g)


def ssd_bwd(dy, proj, hstates, conv_w, conv_b, dt_bias, a_log, d_skip, norm_g):
    S = proj.shape[0]
    nc = S // L
    cidx = lambda c: nc - 1 - c

    def body(dy_ref, z_ref, xa_ref, xb_ref, xc_ref, dt_ref, cw_ref, cb_ref, dtb_ref, alog_ref, dsk_ref, ng_ref,
             pa_ref, pb_ref, pc_ref, hp_ref,
             dout_ref, dcw_ref, dcb_ref, ddtb_ref, dalog_ref, ddsk_ref, dng_ref,
             dH, ext, dext, ysc, yoffsc, dxdt, dxc, tsc):
        step = pl.program_id(0)
        c = nc - 1 - step

        @pl.when(step == 0)
        def _():
            dH[...] = jnp.zeros_like(dH)
            dext[L:L + 8, :] = jnp.zeros((8, CONV_C), F32)
            for r in (dcw_ref, dcb_ref, ddtb_ref, dalog_ref, ddsk_ref, dng_ref):
                r[...] = jnp.zeros_like(r)

        for k, (ref, prev) in enumerate(((xa_ref, pa_ref), (xb_ref, pb_ref), (xc_ref, pc_ref))):
            ext[0:8, k * 1024:(k + 1) * 1024] = jnp.where(c > 0, prev[...], 0.0)
            ext[8:8 + L, k * 1024:(k + 1) * 1024] = ref[...]
        dtb = dtb_ref[...]
        dt_raw = dt_ref[...]
        pre, sg_pre, xc, dt, a, acs, taps = _ssd_common(ext, cw_ref, cb_ref, dt_raw, dtb, alog_ref[...])
        xs = xc[:, 0:SSM_W]
        acs_t = acs.T
        ex = _expand_matrix()
        dt_x = _dot_sel(dt, ex, 2)
        xdt = xs * dt_x
        eacs_x = _dot_sel(jnp.exp(acs), ex, 2)
        dte_x = _dot_sel(jnp.exp(acs[L - 1:L, :] - acs), ex, 2)
        xw = xdt * dte_x
        cd_row = jnp.exp(acs[L - 1:L, :])
        tri = _tril()

        Gs, Cs, Bs = [], [], []
        for g in range(SSM_G):
            gs = slice(g * 512, (g + 1) * 512)
            bg = xc[:, SSM_W + g * SSM_N:SSM_W + (g + 1) * SSM_N].astype(BF16)
            cg = xc[:, SSM_W + 512 + g * SSM_N:SSM_W + 512 + (g + 1) * SSM_N].astype(BF16)
            G = _dot_nt(cg, bg)
            Gs.append(G), Cs.append(cg), Bs.append(bg)
            yoffsc[:, gs] = _dot_nt(cg, hp_ref[0, gs, :].astype(BF16)) * eacs_x[:, gs]
            for j in range(8):
                hh = g * 8 + j
                hs = slice(hh * SSM_P, (hh + 1) * SSM_P)
                seg = acs[:, hh:hh + 1] - acs_t[hh:hh + 1, :]
                dk = jnp.where(tri, jnp.exp(jnp.minimum(seg, 0.0)), 0.0)
                ysc[:, hs] = _dot((G * dk).astype(BF16), xdt[:, hs].astype(BF16))
        d_x = _dot_sel(jnp.broadcast_to(dsk_ref[...], (8, 128)), ex, 3)[0:1, :]
        yoff = yoffsc[...]
        Y = ysc[...] + yoff + d_x * xs

        z = z_ref[...]
        sgz = _sigmoid(z)
        silu_z = z * sgz
        yz = Y * silu_z
        ng = ng_ref[...]
        dout = dy_ref[...]
        dyn = dout * ng
        dyz_parts, dng_parts = [], []
        for g in range(SSM_G):
            gs = slice(g * 512, (g + 1) * 512)
            t = yz[:, gs]
            rstd = lax.rsqrt(jnp.mean(t * t, axis=-1, keepdims=True) + EPS)
            dng_parts.append(jnp.sum(dout[:, gs] * t * rstd, axis=0, keepdims=True))
            dn = dyn[:, gs]
            dyz_parts.append(rstd * dn - t * (rstd * rstd * rstd) * jnp.mean(dn * t, axis=-1, keepdims=True))
        dng_ref[...] += jnp.concatenate(dng_parts, axis=1)
        dyz = jnp.concatenate(dyz_parts, axis=1)
        dY = dyz * silu_z
        dout_ref[:, 0:SSM_W] = (dyz * Y * (sgz * (1.0 + z * (1.0 - sgz)))).astype(BF16)

        ex_t = _expand_matrix_t()
        ddsk_ref[...] += _dot_sel(jnp.broadcast_to(jnp.sum(dY * xs, axis=0, keepdims=True), (8, SSM_W)), ex_t, 3)[0:1, :]

        lane = lax.broadcasted_iota(jnp.int32, (L, 128), 1)
        subl = lax.broadcasted_iota(jnp.int32, (128, L), 0)
        coll = lax.broadcasted_iota(jnp.int32, (128, L), 1)
        r_cols = jnp.zeros((L, 128), F32)
        c_rows = jnp.zeros((128, L), F32)
        for g in range(SSM_G):
            gs = slice(g * 512, (g + 1) * 512)
            G, cg, bg = Gs[g], Cs[g], Bs[g]
            hp_g = hp_ref[0, gs, :]
            dh_g = dH[gs, :]
            dY_g = dY[:, gs]
            dZ = dY_g * eacs_x[:, gs]
            dZb = dZ.astype(BF16)
            dC = _dot(dZb, hp_g.astype(BF16))
            dh_from_off = _dot_tn(dZ, cg)
            dhb = dh_g.astype(BF16)
            Q = _dot_nt(bg, dhb)
            dB = _dot(xw[:, gs].astype(BF16), dhb)
            qd = Q * dte_x[:, gs]
            dxdt[:, gs] = qd
            tsc[:, gs] = qd * xdt[:, gs]
            dG = jnp.zeros((L, L), F32)
            for j in range(8):
                hh = g * 8 + j
                hs = slice(hh * SSM_P, (hh + 1) * SSM_P)
                seg = acs[:, hh:hh + 1] - acs_t[hh:hh + 1, :]
                dk = jnp.where(tri, jnp.exp(jnp.minimum(seg, 0.0)), 0.0)
                M = G * dk
                dYh = dY[:, hs]
                dYhb = dYh.astype(BF16)
                dM = _dot_nt(dYhb, xdt[:, hs].astype(BF16))
                dxdt[:, hs] += _dot_tn(M, dYhb)
                dG = dG + dM * dk
                Wm = dM * M
                r_cols = r_cols + jnp.where(lane == hh, jnp.sum(Wm, axis=1, keepdims=True), 0.0)
                c_rows = c_rows + jnp.where(subl == hh, jnp.sum(Wm, axis=0, keepdims=True), 0.0)
                pj = slice(j * SSM_P, (j + 1) * SSM_P)
                cd_h = cd_row[:, hh:hh + 1]
                dcd = jnp.sum(dh_g[pj, :] * hp_g[pj, :]) * cd_h
                c_rows = c_rows - jnp.where((subl == hh) & (coll == L - 1), dcd, 0.0)
                dH[hs, :] = dh_g[pj, :] * cd_h + dh_from_off[pj, :]
            dGb = dG.astype(BF16)
            dC = dC + _dot(dGb, bg)
            dB = dB + _dot_tn(dG, cg)
            dxc[:, SSM_W + g * SSM_N:SSM_W + (g + 1) * SSM_N] = dB
            dxc[:, SSM_W + 512 + g * SSM_N:SSM_W + 512 + (g + 1) * SSM_N] = dC

        row = lax.broadcasted_iota(jnp.int32, (L, 128), 0)
        tv = tsc[...]
        t_last = _dot_sel(jnp.broadcast_to(jnp.sum(tv, axis=0, keepdims=True), (8, SSM_W)), ex_t, 3)[0:1, :]
        dacs = (r_cols - c_rows.T + _dot_sel(dY * yoff - tv, ex_t, 2) + jnp.where(row == L - 1, t_last, 0.0))
        triu = lax.broadcasted_iota(jnp.int32, (L, L), 0) <= lax.broadcasted_iota(jnp.int32, (L, L), 1)
        dadt = _sel_dot(triu, dacs, 3)
        dxdt_v = dxdt[...]
        ddt = _dot_sel(dxdt_v * xs, ex_t, 2) + dadt * a
        dalog_ref[...] += jnp.sum(dadt * dt * a, axis=0, keepdims=True)
        ddt_raw = jnp.where(lane < SSM_H, ddt * _sigmoid(dt_raw + dtb), 0.0)
        ddtb_ref[...] += jnp.sum(ddt_raw, axis=0, keepdims=True)
        dout_ref[:, 5120:5248] = ddt_raw.astype(BF16)
        dout_ref[:, 5248:5376] = jnp.zeros((L, 128), BF16)

        dxc[:, 0:SSM_W] = dxdt_v * dt_x + d_x * dY
        dpre = dxc[...] * (sg_pre * (1.0 + pre * (1.0 - sg_pre)))
        dcb_ref[...] += jnp.sum(dpre, axis=0, keepdims=True)
        dext[0:L, :] = dpre
        x_cur = ext[8:8 + L, :]
        dx = None
        for k in range(CONV_K):
            dsh = _rows_from(dext, 3 - k)
            term = cw_ref[k:k + 1, :] * dsh
            dx = term if dx is None else dx + term
            dcw_ref[k:k + 1, :] += jnp.sum(dsh * x_cur, axis=0, keepdims=True)
        dout_ref[:, SSM_W:SSM_W + CONV_C] = dx.astype(BF16)
        dext[L:L + 8, :] = dpre[0:8, :]

    big = lambda w: pl.BlockSpec((L, w), lambda c: (cidx(c), 0))
    return pl.pallas_call(
        body, name="ssd_bwd", grid=(nc,),
        in_specs=[big(SSM_W)] + _ssd_in_specs(True, nc) + [
            pl.BlockSpec((8, 1024), lambda c, k=k: (jnp.maximum(16 * cidx(c) - 1, 0), k)) for k in (2, 3, 4)] + [
            pl.BlockSpec((1, SSM_W, SSM_N), lambda c: (cidx(c), 0, 0))],
        out_specs=[big(5376), _full((8, CONV_C)), _full((1, CONV_C)),
                   _full((1, 128)), _full((1, 128)), _full((1, 128)), _full((1, SSM_W))],
        out_shape=[jax.ShapeDtypeStruct((S, 5376), BF16), jax.ShapeDtypeStruct((8, CONV_C), F32),
                   jax.ShapeDtypeStruct((1, CONV_C), F32), jax.ShapeDtypeStruct((1, 128), F32),
                   jax.ShapeDtypeStruct((1, 128), F32), jax.ShapeDtypeStruct((1, 128), F32),
                   jax.ShapeDtypeStruct((1, SSM_W), F32)],
        scratch_shapes=[pltpu.VMEM((SSM_W, SSM_N), F32), pltpu.VMEM((8 + L, CONV_C), F32),
                        pltpu.VMEM((L + 8, CONV_C), F32), pltpu.VMEM((L, SSM_W), F32),
                        pltpu.VMEM((L, SSM_W), F32), pltpu.VMEM((L, SSM_W), F32),
                        pltpu.VMEM((L, CONV_C), F32), pltpu.VMEM((L, SSM_W), F32)],
        compiler_params=_params(("arbitrary",)),
    )(dy, proj, proj, proj, proj, proj, conv_w, conv_b, dt_bias, a_log, d_skip, norm_g, proj, proj, proj, hstates)


def _resident(shape):
    nd = len(shape)
    return pl.BlockSpec(shape, lambda *_: (0,) * nd, pipeline_mode=pl.Buffered(1))


def merge_fwd(y_att, y_sg, y_ssm, proj, x, w_a, w_s, w_m, w_o, g_post):
    S = x.shape[0]
    tm = 256

    def body(ya_ref, ys_ref, ym_ref, gate_ref, x_ref, wa_ref, ws_ref, wm_ref, wo_ref, gp_ref,
             xn_ref, bra_ref, brs_ref, brm_ref, mg_ref, out_ref):
        bra = _dot(ya_ref[...], wa_ref[...])
        brs = _dot(ys_ref[...], ws_ref[...])
        brm = _dot(ym_ref[...], wm_ref[...])
        bra_ref[...] = bra
        brs_ref[...] = brs
        brm_ref[...] = brm
        merged = (_sigmoid(gate_ref[:, 0:1024]) * bra + _sigmoid(gate_ref[:, 1024:2048]) * brs
                  + _sigmoid(gate_ref[:, 2048:3072]) * brm)
        mb = merged.astype(BF16)
        mg_ref[...] = mb
        o = _dot(mb, wo_ref[...])
        out_ref[...] = o
        r = lax.rsqrt(jnp.mean(o * o, axis=-1, keepdims=True) + EPS)
        xn_ref[...] = x_ref[...] + o * r * gp_ref[...]

    row = lambda w: pl.BlockSpec((tm, w), lambda i: (i, 0))
    return pl.pallas_call(
        body, name="merge_fwd", grid=(S // tm,),
        in_specs=[row(1024), row(1024), row(2048), pl.BlockSpec((tm, 3072), lambda i: (i, 0)),
                  row(D), _resident((1024, D)), _resident((1024, D)), _resident((2048, D)), _resident((D, D)),
                  _full((1, D))],
        out_specs=[row(D)] * 6,
        out_shape=[jax.ShapeDtypeStruct((S, D), F32)] * 4 + [jax.ShapeDtypeStruct((S, D), BF16),
                                                             jax.ShapeDtypeStruct((S, D), F32)],
        compiler_params=_params(("arbitrary",)),
    )(y_att, y_sg, y_ssm, proj, x, w_a, w_s, w_m, w_o, g_post)


def merge_bwd(dy, out, g_post, proj, br_a, br_s, br_m, w_a, w_s, w_m, w_o):
    S = dy.shape[0]
    tm = 256

    def body(dy_ref, o_ref, gp_ref, gate_ref, bra_ref, brs_ref, brm_ref, wa_ref, ws_ref, wm_ref, wo_ref,
             dout_ref, dba_ref, dbs_ref, dbm_ref, dgate_ref, dya_ref, dys_ref, dym_ref, dgp_ref):
        @pl.when(pl.program_id(0) == 0)
        def _():
            dgp_ref[...] = jnp.zeros_like(dgp_ref)

        o = o_ref[...]
        dyv = dy_ref[...]
        r = lax.rsqrt(jnp.mean(o * o, axis=-1, keepdims=True) + EPS)
        dyg = dyv * gp_ref[...]
        do = r * dyg - o * (r * r * r) * jnp.mean(dyg * o, axis=-1, keepdims=True)
        dgp_ref[...] += jnp.sum(dyv * o * r, axis=0, keepdims=True)
        dob = do.astype(BF16)
        dout_ref[...] = dob
        dmerged = _dot_nt(dob, wo_ref[...])
        for idx, (br_ref, dbr_ref, w_ref, dyi_ref) in enumerate((
                (bra_ref, dba_ref, wa_ref, dya_ref), (brs_ref, dbs_ref, ws_ref, dys_ref),
                (brm_ref, dbm_ref, wm_ref, dym_ref))):
            s = _sigmoid(gate_ref[:, idx * 1024:(idx + 1) * 1024])
            dbr = (dmerged * s).astype(BF16)
            dbr_ref[...] = dbr
            dgate_ref[:, idx * 1024:(idx + 1) * 1024] = (dmerged * br_ref[...] * s * (1.0 - s)).astype(BF16)
            dyi_ref[...] = _dot_nt(dbr, w_ref[...])

    row = lambda w: pl.BlockSpec((tm, w), lambda i: (i, 0))
    return pl.pallas_call(
        body, name="merge_bwd", grid=(S // tm,),
        in_specs=[row(D), row(D), _full((1, D)), pl.BlockSpec((tm, 3072), lambda i: (i, 0)),
                  row(D), row(D), row(D),
                  _resident((1024, D)), _resident((1024, D)), _resident((2048, D)), _resident((D, D))],
        out_specs=[row(D), row(D), row(D), row(D), row(3072), row(1024), row(1024), row(2048), _full((1, D))],
        out_shape=[jax.ShapeDtypeStruct((S, D), BF16)] * 4 + [
            jax.ShapeDtypeStruct((S, 3072), BF16), jax.ShapeDtypeStruct((S, 1024), F32),
            jax.ShapeDtypeStruct((S, 1024), F32), jax.ShapeDtypeStruct((S, 2048), F32),
            jax.ShapeDtypeStruct((1, D), F32)],
        compiler_params=_params(("arbitrary",)),
    )(dy, out, g_post, proj, br_a, br_s, br_m, w_a, w_s, w_m, w_o)


def loss_head(y, target):
    S = y.shape[0]
    tm = 512

    def body(y_ref, t_ref, dy_ref, loss_ref):
        @pl.when(pl.program_id(0) == 0)
        def _():
            loss_ref[...] = jnp.zeros_like(loss_ref)
        e = y_ref[...] - t_ref[...]
        dy_ref[...] = e * (1.0 / D)
        loss_ref[...] += 0.5 * jnp.sum(jnp.mean(e * e, axis=-1, keepdims=True))

    row = pl.BlockSpec((tm, D), lambda i: (i, 0))
    return pl.pallas_call(
        body, name="loss_head", grid=(S // tm,),
        in_specs=[row, row], out_specs=[row, _full((1, 128))],
        out_shape=[jax.ShapeDtypeStruct((S, D), F32), jax.ShapeDtypeStruct((1, 128), F32)],
        compiler_params=_params(("arbitrary",)),
    )(y, target)


def _adam(w, g, m, v):
    mn = ADAM_B1 * m + (1.0 - ADAM_B1) * g
    vn = ADAM_B2 * v + (1.0 - ADAM_B2) * (g * g)
    m_hat = mn / (1.0 - ADAM_B1 ** ADAM_STEP)
    v_hat = vn / (1.0 - ADAM_B2 ** ADAM_STEP)
    return -ADAM_LR * (m_hat / (jnp.sqrt(v_hat) + ADAM_EPS) + ADAM_WD * w), mn, vn


def adamw_big(w, m, v, halves0, sum1, cc, name, tr):
    _, R, C = w.shape
    nper = R // tr
    f, fb, n0, off_a, off_b = halves0
    p, pb, off1 = sum1

    def body(c_ref, w_ref, m_ref, v_ref, f_ref, fb_ref, p_ref, pb_ref, g_ref, d_ref, nm_ref, nv_ref):
        i = pl.program_id(0)
        half = jnp.where(i % nper >= n0, 1, 0)
        g0 = jnp.where(c_ref[0] == half, f_ref[...], fb_ref[...])
        g = jnp.where(i < nper, g0, p_ref[...] + pb_ref[...])
        g_ref[0] = g
        d_ref[0], nm_ref[0], nv_ref[0] = _adam(w_ref[0], g, m_ref[0], v_ref[0])

    def blk0(i, c):
        il = jnp.minimum(i, nper - 1)
        return (jnp.where(il >= n0, off_b + il - n0, off_a + il), 0)

    wblk = pl.BlockSpec((1, tr, C), lambda i, c: (i // nper, i % nper, 0))
    b0 = pl.BlockSpec((tr, C), blk0)
    b1 = pl.BlockSpec((tr, C), lambda i, c: (off1 + jnp.maximum(i - nper, 0), 0))
    grid_spec = pltpu.PrefetchScalarGridSpec(
        num_scalar_prefetch=1, grid=(2 * nper,),
        in_specs=[wblk, wblk, wblk, b0, b0, b1, b1], out_specs=[wblk] * 4)
    return pl.pallas_call(
        body, name=name, grid_spec=grid_spec,
        out_shape=[jax.ShapeDtypeStruct(w.shape, F32)] * 4,
        compiler_params=_params(("arbitrary",)),
    )(cc, w, m, v, f, fb, p, pb)


def adamw_plain(w, g, m, v, name):
    def body(w_ref, g_ref, m_ref, v_ref, d_ref, nm_ref, nv_ref):
        d_ref[...], nm_ref[...], nv_ref[...] = _adam(w_ref[...], g_ref[...], m_ref[...], v_ref[...])

    return pl.pallas_call(
        body, name=name, out_shape=[jax.ShapeDtypeStruct(w.shape, F32)] * 3, compiler_params=_params(),
    )(w, g, m, v)


SMALL = {"norm_pre": ("g_pre", 8), "norm_post": ("g_post", 8), "att_sinks": ("sinks", 8), "sg_ln_g": ("ln_g", 8),
         "sg_ln_b": ("ln_b", 8), "sg_w": ("sg_w", 1024), "sg_b": ("sg_bt", 8), "ssm_conv_b": ("conv_b", 24),
         "ssm_dt_bias": ("dt_bias", 8), "ssm_a_log": ("a_log", 8), "ssm_d": ("d_skip", 8), "ssm_norm_g": ("norm_g", 16)}
SMALL_LAYER_ROWS = sum(r for _, r in SMALL.values())
REL_ROW = DEPTH * SMALL_LAYER_ROWS
LOSS_ROW = REL_ROW + 32
SMALL_ROWS = LOSS_ROW + 8


def _small_rows():
    rows, r = {}, 0
    for l in range(DEPTH):
        for name, (_, n) in SMALL.items():
            rows[(l, name)] = r
            r += n
    return rows


def adamw_small(red, rel, small):
    names = list(SMALL) + ["rel_bias"]
    params = dict(small, rel_bias=rel)
    rows = _small_rows()

    def grad_of(red_ref, l, name, n):
        r0 = rows[(l, name)]
        if name == "sg_b":
            return red_ref[r0:r0 + 8, :]
        if n < 128:
            return red_ref[r0:r0 + 1, 0:n]
        return jnp.concatenate([red_ref[r0 + j:r0 + j + 1, :] for j in range(n // 128)], axis=1)

    def body(red_ref, *refs):
        ins, outs = refs[:3 * len(names)], refs[3 * len(names):]
        for i, name in enumerate(names):
            w_ref, m_ref, v_ref = ins[3 * i:3 * i + 3]
            o = outs[4 * i:4 * i + 4]
            if name == "rel_bias":
                g = red_ref[REL_ROW:REL_ROW + 32, 0:16]
                o[0][...] = g
                o[1][...], o[2][...], o[3][...] = _adam(w_ref[...], g, m_ref[...], v_ref[...])
                continue
            for l in range(DEPTH):
                if name == "sg_w":
                    for grp in range(8):
                        r0 = rows[(l, name)] + grp * 128
                        g = red_ref[r0:r0 + 128, :]
                        o[0][l, grp] = g
                        o[1][l, grp], o[2][l, grp], o[3][l, grp] = _adam(w_ref[l, grp], g, m_ref[l, grp], v_ref[l, grp])
                elif name == "sg_b":
                    g = grad_of(red_ref, l, name, 128)
                    o[0][l] = g
                    o[1][l], o[2][l], o[3][l] = _adam(w_ref[l], g, m_ref[l], v_ref[l])
                else:
                    sl = slice(l, l + 1)
                    g = grad_of(red_ref, l, name, w_ref.shape[-1])
                    o[0][sl, :] = g
                    o[1][sl, :], o[2][sl, :], o[3][sl, :] = _adam(w_ref[sl, :], g, m_ref[sl, :], v_ref[sl, :])

    flat_in = [a for name in names for a in params[name]]
    out_shape = [jax.ShapeDtypeStruct(params[name][0].shape, F32) for name in names for _ in range(4)]
    res = pl.pallas_call(body, name="adamw_small", out_shape=out_shape, compiler_params=_params())(red, *flat_in)
    return {name: tuple(res[4 * i:4 * i + 4]) for i, name in enumerate(names)}


ANY = pl.BlockSpec(memory_space=pl.ANY)


def _place():
    x, y, c = lax.axis_index("x"), lax.axis_index("y"), lax.axis_index("c")
    others = [(1 - x, y), (x, 1 - y), (1 - x, 1 - y)]
    return x, y, c, others


def _rcopy(src, dst, ssem, rsem, to):
    return pltpu.make_async_remote_copy(src_ref=src, dst_ref=dst, send_sem=ssem, recv_sem=rsem,
                                        device_id=to, device_id_type=MESH)


def gather_weights(arrs):
    n = len(arrs)

    def body(*refs):
        srcs, outs, ssem, rsem = refs[:n], refs[n:2 * n], refs[2 * n], refs[2 * n + 1]
        x, y, c, others = _place()
        me = 2 * x + y
        sib = (x, y, 1 - c)
        first = [_rcopy(srcs[i].at[c], outs[i].at[c, me], ssem.at[6 * i + k], rsem.at[6 * i + k], (ox, oy, c))
                 for i in range(n) for k, (ox, oy) in enumerate(others)]
        for cp in first:
            cp.start()
        passed = []
        for k, (ox, oy) in enumerate(others):
            for i in range(n):
                slot = outs[i].at[c, 2 * ox + oy]
                _rcopy(slot, slot, ssem.at[6 * i + k], rsem.at[6 * i + k], sib).wait_recv()
                fw = _rcopy(slot, slot, ssem.at[6 * i + 3 + k], rsem.at[6 * i + 3 + k], sib)
                fw.start()
                passed.append(fw)
        for k, (ox, oy) in enumerate(others):
            for i in range(n):
                slot = outs[i].at[1 - c, 2 * ox + oy]
                _rcopy(slot, slot, ssem.at[6 * i + 3 + k], rsem.at[6 * i + 3 + k], sib).wait_recv()
        for cp in first + passed:
            cp.wait_send()

    return pl.pallas_call(
        body, name="gather_weights",
        in_specs=[ANY] * n, out_specs=[ANY] * n,
        out_shape=[jax.ShapeDtypeStruct((2, SHARDS) + a.shape[1:], a.dtype) for a in arrs],
        scratch_shapes=[pltpu.SemaphoreType.DMA((6 * n,)), pltpu.SemaphoreType.DMA((6 * n,))],
    )(*arrs)


HBM = pl.BlockSpec(memory_space=pltpu.HBM)
SEM = pl.BlockSpec(memory_space=pltpu.SEMAPHORE)
EFFECT = pltpu.SideEffectType.DATAFLOW_SIDE_EFFECTING


def _in_hbm(a):
    return pltpu.with_memory_space_constraint(a, pltpu.HBM)


def gather_start(srcs, after, name, by_dest=False):
    n = len(srcs)
    lands = [_in_hbm(lax.empty((SHARDS,) + a.shape[-2:], a.dtype)) for a in srcs]
    na = len(after)

    def body(*refs):
        src, land = refs[:n], refs[n:2 * n]
        ssem, rsem, token = refs[2 * n + na], refs[2 * n + na + 1], refs[-1]
        x, y, c, others = _place()
        me = 2 * x + y
        for i in range(n):
            for k, (ox, oy) in enumerate(others):
                s = src[i].at[2 * ox + oy] if by_dest else src[i]
                _rcopy(s, land[i].at[me], ssem.at[3 * i + k], rsem.at[3 * i + k], (ox, oy, c)).start()
        token[...] = jnp.zeros_like(token)

    bufs = [_in_hbm(a) for a in srcs] + lands
    out = pl.pallas_call(
        body, name=name,
        out_shape=(pltpu.SemaphoreType.DMA((3 * n,)), pltpu.SemaphoreType.DMA((3 * n,)),
                   *[pltpu.HBM(b.shape, b.dtype) for b in bufs], jax.ShapeDtypeStruct((8, 128), F32)),
        in_specs=[HBM] * (2 * n) + [ANY] * na,
        out_specs=(SEM, SEM, *[HBM] * (2 * n), pl.BlockSpec(memory_space=pltpu.VMEM)),
        input_output_aliases={i: 2 + i for i in range(2 * n)},
        compiler_params=pltpu.CompilerParams(has_side_effects=EFFECT),
    )(*bufs, *after)
    return out[0], out[1], list(out[2:2 + n]), list(out[2 + n:2 + 2 * n]), out[-1]


def gather_wait(ssem, rsem, srcs, lands, after, name, by_dest=False):
    n = len(srcs)

    def body(*refs):
        src, land = refs[:n], refs[n:2 * n]
        s_sem, r_sem = refs[2 * n], refs[2 * n + 1]
        x, y, c, others = _place()
        for i in range(n):
            for k, (ox, oy) in enumerate(others):
                s = src[i].at[2 * ox + oy] if by_dest else src[i]
                cp = _rcopy(s, land[i].at[2 * ox + oy], s_sem.at[3 * i + k], r_sem.at[3 * i + k], (ox, oy, c))
                cp.wait_send()
                cp.wait_recv()

    bufs = list(srcs) + list(lands)
    out = pl.pallas_call(
        body, name=name,
        out_shape=tuple(pltpu.HBM(b.shape, b.dtype) for b in bufs),
        in_specs=[HBM] * (2 * n) + [SEM, SEM, ANY],
        out_specs=tuple([HBM] * (2 * n)),
        input_output_aliases={i: i for i in range(2 * n)},
        compiler_params=pltpu.CompilerParams(has_side_effects=EFFECT),
    )(*bufs, ssem, rsem, after)
    return list(out[n:2 * n])


def grad_sibling_exchange(arrs):
    n = len(arrs)

    def body(*refs):
        srcs, outs, ssem, rsem = refs[:n], refs[n:2 * n], refs[2 * n], refs[2 * n + 1]
        x, y, c, _ = _place()
        cps = [_rcopy(srcs[i].at[1 - c], outs[i], ssem.at[i], rsem.at[i], (x, y, 1 - c)) for i in range(n)]
        for cp in cps:
            cp.start()
        for cp in cps:
            cp.wait()

    return pl.pallas_call(
        body, name="grad_sibling_exchange",
        in_specs=[ANY] * n, out_specs=[ANY] * n,
        out_shape=[jax.ShapeDtypeStruct(a.shape[1:], F32) for a in arrs],
        scratch_shapes=[pltpu.SemaphoreType.DMA((n,)), pltpu.SemaphoreType.DMA((n,))],
    )(*arrs)


def grad_chip_sum(g, sb, cc, tr, name):
    _, _, R, C = g.shape
    blk = pl.BlockSpec((1, tr, C), lambda s, r, c: (s, r, 0))
    grid_spec = pltpu.PrefetchScalarGridSpec(
        num_scalar_prefetch=1, grid=(SHARDS, R // tr),
        in_specs=[pl.BlockSpec((1, 1, tr, C), lambda s, r, c: (c[0], s, r, 0)), blk],
        out_specs=[blk, blk])

    def body(c_ref, a_ref, b_ref, o_ref, ob_ref):
        t = a_ref[0] + b_ref[...]
        o_ref[...] = t
        ob_ref[...] = t.astype(BF16)

    return pl.pallas_call(
        body, name=name, grid_spec=grid_spec,
        out_shape=[jax.ShapeDtypeStruct((SHARDS, R, C), F32), jax.ShapeDtypeStruct((SHARDS, R, C), BF16)],
        compiler_params=_params(("arbitrary", "arbitrary")),
    )(cc, g, sb)


def grad_chip_exchange(arrs):
    n = len(arrs)

    def body(*refs):
        srcs, outs, ssem, rsem = refs[:n], refs[n:2 * n], refs[2 * n], refs[2 * n + 1]
        x, y, c, others = _place()
        me = 2 * x + y
        sends = [_rcopy(srcs[i].at[2 * ox + oy], outs[i].at[me], ssem.at[3 * i + k], rsem.at[3 * i + k], (ox, oy, c))
                 for i in range(n) for k, (ox, oy) in enumerate(others)]
        for cp in sends:
            cp.start()
        for i in range(n):
            for k, (ox, oy) in enumerate(others):
                slot = outs[i].at[2 * ox + oy]
                _rcopy(slot, slot, ssem.at[3 * i + k], rsem.at[3 * i + k], (ox, oy, c)).wait_recv()
        for cp in sends:
            cp.wait_send()

    return pl.pallas_call(
        body, name="grad_chip_exchange",
        in_specs=[ANY] * n, out_specs=[ANY] * n,
        out_shape=[jax.ShapeDtypeStruct(a.shape, a.dtype) for a in arrs],
        scratch_shapes=[pltpu.SemaphoreType.DMA((3 * n,)), pltpu.SemaphoreType.DMA((3 * n,))],
    )(*arrs)


def grad_shard_sum(t, rb, me, tr, name):
    _, R, C = t.shape
    grid_spec = pltpu.PrefetchScalarGridSpec(
        num_scalar_prefetch=1, grid=(R // tr,),
        in_specs=[pl.BlockSpec((1, tr, C), lambda r, m: (m[0], r, 0)),
                  pl.BlockSpec((SHARDS, tr, C), lambda r, m: (0, r, 0))],
        out_specs=pl.BlockSpec((tr, C), lambda r, m: (r, 0)))

    def body(m_ref, t_ref, r_ref, o_ref):
        part = [jnp.where(m_ref[0] == s, t_ref[0], r_ref[s].astype(F32)) for s in range(SHARDS)]
        o_ref[...] = ((part[0] + part[1]) + part[2]) + part[3]

    return pl.pallas_call(
        body, name=name, grid_spec=grid_spec,
        out_shape=jax.ShapeDtypeStruct((R, C), F32),
        compiler_params=_params(("arbitrary",)),
    )(me, t, rb)


def grad_sibling_share(arrs, name):
    n = len(arrs)

    def body(*refs):
        srcs, outs, ssem, rsem = refs[:n], refs[n:2 * n], refs[2 * n], refs[2 * n + 1]
        x, y, c, _ = _place()
        cps = [_rcopy(srcs[i], outs[i], ssem.at[i], rsem.at[i], (x, y, 1 - c)) for i in range(n)]
        for cp in cps:
            cp.start()
        for cp in cps:
            cp.wait()

    return pl.pallas_call(
        body, name=name,
        in_specs=[ANY] * n, out_specs=[ANY] * n,
        out_shape=[jax.ShapeDtypeStruct(a.shape, F32) for a in arrs],
        scratch_shapes=[pltpu.SemaphoreType.DMA((n,)), pltpu.SemaphoreType.DMA((n,))],
    )(*arrs)


def _allreduce_rows(src, sib_buf, chips, out_ref, ssem, rsem):
    x, y, c, others = _place()
    me = 2 * x + y
    cp = _rcopy(src, sib_buf, ssem.at[0], rsem.at[0], (x, y, 1 - c))
    cp.start()
    cp.wait()
    chips[me] = src[...] + sib_buf[...]
    sends = [_rcopy(chips.at[me], chips.at[me], ssem.at[1 + k], rsem.at[1 + k], (ox, oy, c))
             for k, (ox, oy) in enumerate(others)]
    for s in sends:
        s.start()
    for k, (ox, oy) in enumerate(others):
        slot = chips.at[2 * ox + oy]
        _rcopy(slot, slot, ssem.at[1 + k], rsem.at[1 + k], (ox, oy, c)).wait_recv()
    for s in sends:
        s.wait_send()
    out_ref[...] = ((chips[0] + chips[1]) + chips[2]) + chips[3]


def _allreduce_scratch(rows):
    return [pltpu.VMEM((rows, 128), F32), pltpu.VMEM((SHARDS, rows, 128), F32),
            pltpu.SemaphoreType.DMA((4,)), pltpu.SemaphoreType.DMA((4,))]


def allreduce_rows(buf, name):
    rows = buf.shape[0]
    VM = pl.BlockSpec(memory_space=pltpu.VMEM)

    def body(src_ref, out_ref, sib_buf, chips, ssem, rsem):
        _allreduce_rows(src_ref, sib_buf, chips, out_ref, ssem, rsem)

    return pl.pallas_call(
        body, name=name, in_specs=[VM], out_specs=VM,
        out_shape=jax.ShapeDtypeStruct((rows, 128), F32),
        scratch_shapes=_allreduce_scratch(rows), compiler_params=_params(),
    )(buf)


def small_allreduce(grads, rel, loss_part):
    rows = _small_rows()
    keys = [(l, name) for l in range(DEPTH) for name in SMALL]
    flat = [grads[l][SMALL[name][0]] for l, name in keys] + [rel, loss_part]

    def body(*refs):
        ins = refs[:len(flat)]
        out_ref, src, sib_buf, chips, ssem, rsem = refs[len(flat):]
        src[...] = jnp.zeros_like(src)
        for (l, name), ref in zip(keys, ins):
            r0 = rows[(l, name)]
            if name == "sg_w":
                for grp in range(8):
                    src[r0 + grp * 128:r0 + (grp + 1) * 128, :] = ref[grp]
            elif name == "sg_b":
                src[r0:r0 + 8, :] = ref[...].T[0:8, :]
            else:
                for j in range(ref.shape[1] // 128):
                    src[r0 + j:r0 + j + 1, :] = ref[:, j * 128:(j + 1) * 128]
        src[REL_ROW:REL_ROW + 32, 0:16] = ins[-2][...]
        src[LOSS_ROW:LOSS_ROW + 1, :] = ins[-1][...]
        _allreduce_rows(src, sib_buf, chips, out_ref, ssem, rsem)

    return pl.pallas_call(
        body, name="small_allreduce",
        out_shape=jax.ShapeDtypeStruct((SMALL_ROWS, 128), F32),
        scratch_shapes=[pltpu.VMEM((SMALL_ROWS, 128), F32)] + _allreduce_scratch(SMALL_ROWS),
        compiler_params=_params(),
    )(*flat)


def _pad_lanes(v):
    return jnp.zeros((1, 128), F32).at[0, :v.shape[0]].set(v)


def layer_fwd(x, wts, bias):
    wt = wts["wt"]
    tn = {name: t for name, _, t in GROUPS}
    p_gate, h = inproj_first(x, wts["g_pre"], wt["gate"], tn["gate"], "inproj_gate")
    p_sgu, p_att, p_ssd = (inproj_group(h, wt[n], tn[n], "inproj_" + n) for n in ("sgu", "att", "ssd"))
    y_att = att_fwd(p_att, bias, wts["sinks"])
    y_sg = sgu_fwd(p_sgu, wts["ln_g"], wts["ln_b"], wts["sg_w"], wts["sg_bt"])
    y_ssm, hst = ssd_fwd(p_ssd, wts["conv_w"], wts["conv_b"], wts["dt_bias"], wts["a_log"], wts["d_skip"],
                         wts["norm_g"])
    x_new, br_a, br_s, br_m, merged, out = merge_fwd(
        y_att, y_sg, y_ssm, p_gate, x, wts["w_a"], wts["w_s"], wts["w_m"], wts["w_o"], wts["g_post"])
    saved = dict(x=x, p_gate=p_gate, p_sgu=p_sgu, p_att=p_att, p_ssd=p_ssd, h=h,
                 y_att=y_att, y_sg=y_sg, y_ssm=y_ssm, hst=hst,
                 br_a=br_a, br_s=br_s, br_m=br_m, merged=merged, out=out)
    return x_new, saved


def layer_bwd(dy, wts, bias, sv):
    dout, dba, dbs, dbm, d_gate, dya, dys, dym, dg_post = merge_bwd(
        dy, sv["out"], wts["g_post"], sv["p_gate"], sv["br_a"], sv["br_s"], sv["br_m"],
        wts["w_a"], wts["w_s"], wts["w_m"], wts["w_o"])
    d_att, dbias, dsinks = att_bwd(dya, sv["p_att"], bias, wts["sinks"])
    d_sgu, dsg_w, dsg_bt, dln_g, dln_b = sgu_bwd(dys, sv["p_sgu"], wts["ln_g"], wts["ln_b"], wts["sg_w"],
                                                 wts["sg_bt"])
    d_ssd, dcw, dcb, ddtb, dalog, ddsk, dng = ssd_bwd(
        dym, sv["p_ssd"], sv["hst"], wts["conv_w"], wts["conv_b"], wts["dt_bias"], wts["a_log"], wts["d_skip"],
        wts["norm_g"])
    dps = dict(gate=d_gate, sgu=d_sgu, att=d_att, ssd=d_ssd)
    wt = wts["wt"]
    tn = {name: t for name, _, t in GROUPS}
    acc = None
    for n in ("gate", "sgu", "ssd"):
        acc = dh_group(dps[n], wt[n], acc, tn[n], "dh_" + n)
    dx, dg_pre = dh_last(dps["att"], wt["att"], acc, sv["x"], wts["g_pre"], dy, tn["att"], "dh_att")
    grads = dict(
        w_in={n: dw_group(dps[n], sv["h"], tn[n], "dw_in_" + n) for n in dps},
        w_a=matmul_tn(sv["y_att"], dba, "dw_att"),
        w_s=matmul_tn(sv["y_sg"], dbs, "dw_sg"),
        w_m=matmul_tn(sv["y_ssm"], dbm, "dw_ssm"),
        w_o=matmul_tn(sv["merged"], dout, "dw_out"),
        g_pre=dg_pre, g_post=dg_post, sinks=dsinks, ln_g=dln_g, ln_b=dln_b, sg_w=dsg_w, sg_bt=dsg_bt,
        conv_w=dcw, conv_b=dcb, dt_bias=ddtb, a_log=dalog, d_skip=ddsk, norm_g=dng, bias=dbias)
    return dx, grads


REST_OFF = (0, 256, 512, 1024, 1280)
GR_ROWS = 1536
GR_CONV = 1280
W_IN_SPLIT = 1600
W_IN_HALF = 1824


def kernel(x, w_in, norm_pre, norm_post, rel_bias, att_sinks, sg_ln_g, sg_ln_b, sg_w, sg_b, ssm_conv_w, ssm_conv_b, ssm_dt_bias, ssm_a_log, ssm_d, ssm_norm_g, w_br_att, w_br_sg, w_br_ssm, w_out, loss_target, m_w_in, m_norm_pre, m_norm_post, m_rel_bias, m_att_sinks, m_sg_ln_g, m_sg_ln_b, m_sg_w, m_sg_b, m_ssm_conv_w, m_ssm_conv_b, m_ssm_dt_bias, m_ssm_a_log, m_ssm_d, m_ssm_norm_g, m_w_br_att, m_w_br_sg, m_w_br_ssm, m_w_out, v_w_in, v_norm_pre, v_norm_post, v_rel_bias, v_att_sinks, v_sg_ln_g, v_sg_ln_b, v_sg_w, v_sg_b, v_ssm_conv_w, v_ssm_conv_b, v_ssm_dt_bias, v_ssm_a_log, v_ssm_d, v_ssm_norm_g, v_w_br_att, v_w_br_sg, v_w_br_ssm, v_w_out):
    cx, cy, cc = lax.axis_index("x"), lax.axis_index("y"), lax.axis_index("c")
    me = 2 * cx + cy
    xs = x[0]
    S = xs.shape[0]

    tr = lambda a: jnp.transpose(a, (0, 2, 1))
    w_in_b = tr(w_in).astype(BF16)
    w_rest_b = jnp.concatenate([w_br_att, w_br_sg, w_br_ssm, w_out], axis=1).astype(BF16)
    halves = lambda a: a.reshape(2, a.shape[0] // 2, a.shape[1])
    all0_in, all0_rest = gather_weights([halves(w_in_b[0]), halves(w_rest_b[0])])
    convw_slot = jnp.zeros((SHARDS, DEPTH * CONV_K * 768 // 128, 128), F32)
    convw_slot = lax.dynamic_update_index_in_dim(
        convw_slot, jnp.where(cc == 0, 1.0, 0.0) * ssm_conv_w.reshape(-1, 128), me, 0)
    convw_rows = allreduce_rows(convw_slot.reshape(-1, 128), "gather_conv_w")
    convw_all = convw_rows.reshape(SHARDS, DEPTH, CONV_K, 768).transpose(1, 2, 0, 3).reshape(DEPTH, CONV_K, CONV_C)
    g1_ssem, g1_rsem, g1_srcs, g1_lands, g1_token = gather_start(
        [w_in_b[1], w_rest_b[1]], [convw_rows, all0_rest], "gather_l1_start")

    o = REST_OFF

    def layer_weights(l, gathered_in, gathered_rest, g_pre):
        sh_in = [jnp.where(me == s, w_in_b[l], gathered_in[s]) for s in range(SHARDS)]
        sh_rest = [jnp.where(me == s, w_rest_b[l], gathered_rest[s]) for s in range(SHARDS)]
        rest = lambda k: jnp.concatenate([r[o[k]:o[k + 1]] for r in sh_rest], axis=0)
        return dict(
            wt=group_weights(jnp.concatenate(sh_in, axis=0)),
            w_a=rest(0), w_s=rest(1), w_m=rest(2), w_o=rest(3),
            g_pre=g_pre, g_post=norm_post[l][None], sinks=att_sinks[l],
            ln_g=sg_ln_g[l][None], ln_b=sg_ln_b[l][None], sg_w=sg_w[l],
            sg_bt=sg_b[l].T,
            conv_w=jnp.concatenate([convw_all[l], jnp.zeros((4, CONV_C), F32)], axis=0),
            conv_b=ssm_conv_b[l][None], dt_bias=_pad_lanes(ssm_dt_bias[l]), a_log=_pad_lanes(ssm_a_log[l]),
            d_skip=_pad_lanes(ssm_d[l]), norm_g=ssm_norm_g[l][None])

    bias = bias_table(rel_bias)
    layers = [layer_weights(0, [all0_in[:, s].reshape(3400, D) for s in range(SHARDS)],
                            [all0_rest[:, s].reshape(1280, D) for s in range(SHARDS)],
                            (norm_pre[0] + g1_token[0, 0])[None])]
    act, sv0 = layer_fwd(xs, layers[0], bias)
    land_in, land_rest = gather_wait(g1_ssem, g1_rsem, g1_srcs, g1_lands, act, "gather_l1_wait")
    layers.append(layer_weights(1, land_in, land_rest, norm_pre[1][None]))
    act, sv1 = layer_fwd(act, layers[1], bias)
    saved = [sv0, sv1]
    dy, loss_part = loss_head(act, loss_target[0])
    cvec = jnp.reshape(cc, (1,)).astype(jnp.int32)
    mvec = jnp.reshape(me, (1,)).astype(jnp.int32)

    def by_shard(g):
        gcw = g["conv_w"][0:CONV_K].reshape(CONV_K, SHARDS, 768).transpose(1, 0, 2).reshape(SHARDS, 3, 1024)
        rest = jnp.concatenate([
            g["w_a"].reshape(SHARDS, 256, D), g["w_s"].reshape(SHARDS, 256, D), g["w_o"].reshape(SHARDS, 256, D),
            g["w_m"].reshape(SHARDS, 512, D), jnp.pad(gcw, ((0, 0), (0, GR_ROWS - GR_CONV - 3), (0, 0)))], axis=1)
        return ungroup_grads(g["w_in"]).reshape(SHARDS, 3400, D), rest

    grads = [None] * DEPTH
    dy, grads[1] = layer_bwd(dy, layers[1], bias, saved[1])
    g1_in, g1_rest = by_shard(grads[1])
    g1_in = jnp.pad(g1_in, ((0, 0), (0, W_IN_ROWS - 3400), (0, 0)))
    x1_ssem, x1_rsem, x1_srcs, x1_lands, x1_token = gather_start(
        [g1_in.astype(BF16), g1_rest.astype(BF16)], [], "grads_l1_start", by_dest=True)
    wts0 = dict(layers[0], g_post=layers[0]["g_post"] + x1_token[0, 0])
    dy, grads[0] = layer_bwd(dy, wts0, bias, saved[0])
    grad_x = dy[None]
    r1_in, r1_rest = gather_wait(x1_ssem, x1_rsem, x1_srcs, x1_lands, dy, "grads_l1_wait", by_dest=True)
    p_in = grad_shard_sum(g1_in, r1_in, mvec, 384, "l1_sum_w_in")
    p_rest = grad_shard_sum(g1_rest, r1_rest, mvec, 512, "l1_sum_rest")
    pb_in, pb_rest = grad_sibling_share([p_in, p_rest], "l1_sibling_share")
    grad_rel_local = bias_grad(grads[0]["bias"] + grads[1]["bias"])

    g0_in, g0_rest = by_shard(grads[0])
    pad_to = lambda a, rows: jnp.pad(a, ((0, 0), (0, rows - a.shape[1]), (0, 0)))
    g0_in = jnp.stack([pad_to(g0_in[:, 0:W_IN_SPLIT], W_IN_HALF), pad_to(g0_in[:, W_IN_SPLIT:3400], W_IN_HALF)])
    g0_rest = jnp.stack([g0_rest[:, 0:GR_ROWS // 2], g0_rest[:, GR_ROWS // 2:GR_ROWS]])
    sb_in, sb_rest = grad_sibling_exchange([g0_in, g0_rest])
    t_in, t_in_b = grad_chip_sum(g0_in, sb_in, cvec, 608, "chip_sum_w_in")
    t_rest, t_rest_b = grad_chip_sum(g0_rest, sb_rest, cvec, 384, "chip_sum_rest")
    rb_in, rb_rest = grad_chip_exchange([t_in_b, t_rest_b])
    f_in = grad_shard_sum(t_in, rb_in, mvec, 608, "shard_sum_w_in")
    f_rest = grad_shard_sum(t_rest, rb_rest, mvec, 384, "shard_sum_rest")
    fb_in, fb_rest = grad_sibling_share([f_in, f_rest], "l0_sibling_share")

    red = small_allreduce(grads, grad_rel_local, loss_part)
    loss = red[LOSS_ROW, 0]

    res = adamw_small(red, (rel_bias, m_rel_bias, v_rel_bias), dict(
        norm_pre=(norm_pre, m_norm_pre, v_norm_pre), norm_post=(norm_post, m_norm_post, v_norm_post),
        att_sinks=(att_sinks, m_att_sinks, v_att_sinks), sg_ln_g=(sg_ln_g, m_sg_ln_g, v_sg_ln_g),
        sg_ln_b=(sg_ln_b, m_sg_ln_b, v_sg_ln_b), sg_w=(sg_w, m_sg_w, v_sg_w), sg_b=(sg_b, m_sg_b, v_sg_b),
        ssm_conv_b=(ssm_conv_b, m_ssm_conv_b, v_ssm_conv_b), ssm_dt_bias=(ssm_dt_bias, m_ssm_dt_bias, v_ssm_dt_bias),
        ssm_a_log=(ssm_a_log, m_ssm_a_log, v_ssm_a_log), ssm_d=(ssm_d, m_ssm_d, v_ssm_d),
        ssm_norm_g=(ssm_norm_g, m_ssm_norm_g, v_ssm_norm_g)))
    res["w_in"] = tuple(tr(a) for a in adamw_big(
        tr(w_in), tr(m_w_in), tr(v_w_in), (f_in, fb_in, W_IN_SPLIT // 200, 0, 0), (p_in, pb_in, 0), cvec, "adamw_w_in", 200))
    rest_upd = lambda w, m, v, name, n0, off0, off1: adamw_big(
        w, m, v, (f_rest, fb_rest, n0, off0, off0), (p_rest, pb_rest, off1), cvec, name, 256)
    res["w_br_att"] = rest_upd(w_br_att, m_w_br_att, v_w_br_att, "adamw_w_br_att", 1, 0, 0)
    res["w_br_sg"] = rest_upd(w_br_sg, m_w_br_sg, v_w_br_sg, "adamw_w_br_sg", 1, 1, 1)
    res["w_out"] = rest_upd(w_out, m_w_out, v_w_out, "adamw_w_out", 1, 2, 2)
    res["w_br_ssm"] = rest_upd(w_br_ssm, m_w_br_ssm, v_w_br_ssm, "adamw_w_br_ssm", 0, 0, 3)
    cw0 = jnp.where(cc == 1, f_rest, fb_rest)[GR_CONV - GR_ROWS // 2:GR_CONV - GR_ROWS // 2 + 3]
    cw1 = (p_rest + pb_rest)[GR_CONV:GR_CONV + 3]
    g_conv_w = jnp.stack([cw0.reshape(CONV_K, 768), cw1.reshape(CONV_K, 768)])
    res["ssm_conv_w"] = (g_conv_w,) + tuple(adamw_plain(ssm_conv_w, g_conv_w, m_ssm_conv_w, v_ssm_conv_w, "adamw_conv_w"))

    order = ["w_in", "norm_pre", "norm_post", "rel_bias", "att_sinks", "sg_ln_g", "sg_ln_b", "sg_w", "sg_b",
             "ssm_conv_w", "ssm_conv_b", "ssm_dt_bias", "ssm_a_log", "ssm_d", "ssm_norm_g",
             "w_br_att", "w_br_sg", "w_br_ssm", "w_out"]
    return (loss, grad_x, *[res[n][0] for n in order], *[res[n][1] for n in order],
            *[res[n][2] for n in order], *[res[n][3] for n in order])
```

```python
import functools
import math

import numpy as np
import jax
import jax.numpy as jnp
from jax import lax
from jax.experimental import pallas as pl
from jax.experimental.pallas import tpu as pltpu

F32 = jnp.float32
BF16 = jnp.bfloat16
MESH = pl.DeviceIdType.MESH

D = 1024
DEPTH = 2
EPS = 1e-6
L = 128
HEADS = 16
KV = 2
DH = 64
SSM_W = 2048
SSM_H = 32
SSM_P = 64
SSM_G = 4
SSM_N = 128
CONV_K = 4
CONV_C = 3072
NEG = -1e30
IN_COLS = 13600

GROUPS = (("gate", 3072, 1536), ("sgu", 3072, 1536), ("att", 2304, 2304), ("ssd", 5376, 1792))
W_IN_ROWS = 3456

ADAM_LR = 0.001
ADAM_B1 = 0.9
ADAM_B2 = 0.999
ADAM_EPS = 1e-08
ADAM_WD = 0.01
ADAM_STEP = 10

VMEM_LIMIT = 56 * 1024 * 1024

SHARDS = 4


def _dot(a, b):
    return jnp.dot(a, b, preferred_element_type=F32)


def _dot_nt(a, b):
    return lax.dot_general(a, b, (((1,), (1,)), ((), ())), preferred_element_type=F32)


def _dot_tn(a_f32, b):
    return jnp.dot(a_f32.T.astype(BF16), b, preferred_element_type=F32)


def _dot_hi(a, b):
    return jnp.dot(a, b, preferred_element_type=F32, precision=lax.Precision.HIGHEST)


def _pieces(x, n):
    out = []
    for _ in range(n - 1):
        p = x.astype(BF16)
        out.append(p)
        x = x - p.astype(F32)
    out.append(x.astype(BF16))
    return out


def _dot_sel(a, sel, n):
    sel = sel.astype(BF16)
    acc = None
    for p in _pieces(a, n):
        t = _dot(p, sel)
        acc = t if acc is None else acc + t
    return acc


def _sel_dot(sel, b, n):
    sel = sel.astype(BF16)
    acc = None
    for p in _pieces(b, n):
        t = _dot(sel, p)
        acc = t if acc is None else acc + t
    return acc


def _sigmoid(x):
    return 1.0 / (1.0 + jnp.exp(-x))


def _softplus(x):
    return jnp.maximum(x, 0.0) + jnp.log(1.0 + jnp.exp(-jnp.abs(x)))


def _params(sem=None, vmem=VMEM_LIMIT):
    kw = dict(vmem_limit_bytes=vmem)
    if sem is not None:
        kw["dimension_semantics"] = sem
    return pltpu.CompilerParams(**kw)


def _full(shape):
    nd = len(shape)
    return pl.BlockSpec(shape, lambda *_: (0,) * nd)


def group_weights(wt):
    return dict(
        gate=wt[10528:13600],
        sgu=wt[2304:5376],
        att=jnp.concatenate([wt[0:1024], wt[1280:2304], wt[1024:1280]], axis=0),
        ssd=jnp.concatenate([wt[5376:10496], wt[10496:10528], jnp.zeros((224, D), wt.dtype)], axis=0))


def ungroup_grads(g):
    a, s = g["att"], g["ssd"]
    return jnp.concatenate([a[0:1024], a[2048:2304], a[1024:2048], g["sgu"], s[0:5152], g["gate"]], axis=0)


def _bucket_table():
    qi = np.arange(L)[:, None]
    kj = np.arange(2 * L)[None, :]
    dist = np.maximum(qi + L - kj, 0)
    dist_f = np.maximum(dist, 1).astype(np.float32)
    large = 16 + (np.log(dist_f / np.float32(16)) / np.float32(math.log(128 / 16)) * np.float32(16)).astype(np.int32)
    large = np.minimum(large, 31)
    return np.where(dist < 16, dist, large).astype(np.int32)


def bias_table(rel_bias):
    buckets = jnp.asarray(_bucket_table().reshape(1, L * 2 * L))

    def body(rb_ref, bk_ref, out_ref):
        onehot = (lax.broadcasted_iota(jnp.int32, (32, L * 2 * L), 0) == bk_ref[...]).astype(F32)
        out_ref[...] = lax.dot_general(rb_ref[...], onehot, (((0,), (0,)), ((), ())),
                                       preferred_element_type=F32, precision=lax.Precision.HIGHEST)

    out = pl.pallas_call(
        body, name="bias_table",
        out_shape=jax.ShapeDtypeStruct((HEADS, L * 2 * L), F32),
        compiler_params=_params(),
    )(rel_bias, buckets)
    return out.reshape(HEADS, L, 2 * L)


def bias_grad(dbias):
    buckets = jnp.asarray(_bucket_table().reshape(1, L * 2 * L))

    def body(db_ref, bk_ref, out_ref):
        onehot = (lax.broadcasted_iota(jnp.int32, (32, L * 2 * L), 0) == bk_ref[...]).astype(F32)
        out_ref[...] = lax.dot_general(onehot, db_ref[...], (((1,), (1,)), ((), ())),
                                       preferred_element_type=F32, precision=lax.Precision.HIGHEST)

    return pl.pallas_call(
        body, name="bias_grad",
        out_shape=jax.ShapeDtypeStruct((32, HEADS), F32),
        compiler_params=_params(),
    )(dbias.reshape(HEADS, L * 2 * L), buckets)


def _row_tile(S):
    return 1024 if S % 1024 == 0 else 512


def inproj_first(x, g_pre, wt, tn, name):
    S, W = x.shape[0], wt.shape[0]
    tm = _row_tile(S)

    def body(x_ref, g_ref, w_ref, o_ref, h_ref):
        @pl.when(pl.program_id(1) == 0)
        def _():
            xv = x_ref[...]
            r = lax.rsqrt(jnp.mean(xv * xv, axis=-1, keepdims=True) + EPS)
            h_ref[...] = (xv * r * g_ref[...]).astype(BF16)
        o_ref[...] = _dot_nt(h_ref[...], w_ref[...]).astype(BF16)

    return pl.pallas_call(
        body, name=name, grid=(S // tm, W // tn),
        in_specs=[pl.BlockSpec((tm, D), lambda i, j: (i, 0)), _full((1, D)),
                  pl.BlockSpec((tn, D), lambda i, j: (j, 0))],
        out_specs=[pl.BlockSpec((tm, tn), lambda i, j: (i, j)), pl.BlockSpec((tm, D), lambda i, j: (i, 0))],
        out_shape=[jax.ShapeDtypeStruct((S, W), BF16), jax.ShapeDtypeStruct((S, D), BF16)],
        compiler_params=_params(("arbitrary", "arbitrary")),
    )(x, g_pre, wt)


def inproj_group(h, wt, tn, name):
    S, W = h.shape[0], wt.shape[0]
    tm = _row_tile(S)

    def body(h_ref, w_ref, o_ref):
        o_ref[...] = _dot_nt(h_ref[...], w_ref[...]).astype(BF16)

    return pl.pallas_call(
        body, name=name, grid=(S // tm, W // tn),
        in_specs=[pl.BlockSpec((tm, D), lambda i, j: (i, 0)), pl.BlockSpec((tn, D), lambda i, j: (j, 0))],
        out_specs=pl.BlockSpec((tm, tn), lambda i, j: (i, j)),
        out_shape=jax.ShapeDtypeStruct((S, W), BF16),
        compiler_params=_params(("arbitrary", "arbitrary")),
    )(h, wt)


def dh_group(dp, wt, acc, tk, name):
    S, W = dp.shape
    tm = _row_tile(S)

    def body(*refs):
        dp_ref, w_ref, o_ref = refs[0], refs[1], refs[-1]
        first = pl.program_id(1) == 0
        if acc is None:
            @pl.when(first)
            def _():
                o_ref[...] = jnp.zeros_like(o_ref)
        else:
            @pl.when(first)
            def _():
                o_ref[...] = refs[2][...]
        o_ref[...] += _dot(dp_ref[...], w_ref[...])

    row = pl.BlockSpec((tm, D), lambda i, k: (i, 0))
    return pl.pallas_call(
        body, name=name, grid=(S // tm, W // tk),
        in_specs=[pl.BlockSpec((tm, tk), lambda i, k: (i, k)), pl.BlockSpec((tk, D), lambda i, k: (k, 0))]
        + ([] if acc is None else [row]),
        out_specs=row, out_shape=jax.ShapeDtypeStruct((S, D), F32),
        input_output_aliases={} if acc is None else {2: 0},
        compiler_params=_params(("arbitrary", "arbitrary")),
    )(*((dp, wt) if acc is None else (dp, wt, acc)))


def dh_last(dp, wt, acc_in, x, g_pre, dy, tk, name):
    S, W = dp.shape
    tm = 512
    nk = W // tk

    def body(dp_ref, w_ref, a_ref, x_ref, g_ref, dy_ref, dx_ref, dg_ref, acc):
        i, k = pl.program_id(0), pl.program_id(1)

        @pl.when(k == 0)
        def _():
            acc[...] = a_ref[...]

        acc[...] += _dot(dp_ref[...], w_ref[...])

        @pl.when((k == nk - 1) & (i == 0))
        def _():
            dg_ref[...] = jnp.zeros_like(dg_ref)

        @pl.when(k == nk - 1)
        def _():
            xv = x_ref[...]
            dh = acc[...]
            g = g_ref[...]
            r = lax.rsqrt(jnp.mean(xv * xv, axis=-1, keepdims=True) + EPS)
            dhg = dh * g
            dx_ref[...] = dy_ref[...] + r * dhg - xv * (r * r * r) * jnp.mean(dhg * xv, axis=-1, keepdims=True)
            dg_ref[...] += jnp.sum(dh * xv * r, axis=0, keepdims=True)

    row = pl.BlockSpec((tm, D), lambda i, k: (i, 0))
    return pl.pallas_call(
        body, name=name, grid=(S // tm, nk),
        in_specs=[pl.BlockSpec((tm, tk), lambda i, k: (i, k)), pl.BlockSpec((tk, D), lambda i, k: (k, 0)),
                  row, row, _full((1, D)), row],
        out_specs=[row, _full((1, D))],
        out_shape=[jax.ShapeDtypeStruct((S, D), F32), jax.ShapeDtypeStruct((1, D), F32)],
        scratch_shapes=[pltpu.VMEM((tm, D), F32)],
        compiler_params=_params(("arbitrary", "arbitrary")),
    )(dp, wt, acc_in, x, g_pre, dy)


def dw_group(dp, h, tn, name, ts=512):
    S, W = dp.shape

    def body(dp_ref, h_ref, o_ref):
        @pl.when(pl.program_id(1) == 0)
        def _():
            o_ref[...] = jnp.zeros_like(o_ref)
        o_ref[...] += _dot_tn(dp_ref[...].astype(F32), h_ref[...])

    return pl.pallas_call(
        body, name=name, grid=(W // tn, S // ts),
        in_specs=[pl.BlockSpec((ts, tn), lambda j, s: (s, j)), pl.BlockSpec((ts, D), lambda j, s: (s, 0))],
        out_specs=pl.BlockSpec((tn, D), lambda j, s: (j, 0)),
        out_shape=jax.ShapeDtypeStruct((W, D), F32),
        compiler_params=_params(("arbitrary", "arbitrary")),
    )(dp, h)


def matmul_tn(a, b, name, tn=512, ts=512):
    S, K = a.shape
    N = b.shape[1]
    ns = S // ts

    def body(a_ref, b_ref, o_ref):
        @pl.when(pl.program_id(1) == 0)
        def _():
            o_ref[...] = jnp.zeros_like(o_ref)
        o_ref[...] += _dot_tn(a_ref[...].astype(F32), b_ref[...])

    return pl.pallas_call(
        body, name=name, grid=(N // tn, ns),
        in_specs=[pl.BlockSpec((ts, K), lambda j, s: (s, 0)), pl.BlockSpec((ts, tn), lambda j, s: (s, j))],
        out_specs=pl.BlockSpec((K, tn), lambda j, s: (0, j)),
        out_shape=jax.ShapeDtypeStruct((K, N), F32),
        compiler_params=_params(("arbitrary", "arbitrary")),
    )(a, b)


def _att_mask(n):
    qi = lax.broadcasted_iota(jnp.int32, (L, 2 * L), 0)
    kj = lax.broadcasted_iota(jnp.int32, (L, 2 * L), 1)
    dist = qi + L - kj
    return (dist >= 0) & (dist < L) & ((kj >= L) | (n > 0))


def _att_in_specs(nb):
    last = nb - 1
    cur = lambda n: jnp.minimum(n, last)
    prev = lambda n: jnp.maximum(jnp.minimum(n, last) - 1, 0)
    return [
        pl.BlockSpec((L, 1024), lambda n: (cur(n), 0)),
        pl.BlockSpec((L, 128), lambda n: (prev(n), 16)),
        pl.BlockSpec((L, 128), lambda n: (cur(n), 16)),
        pl.BlockSpec((L, 128), lambda n: (prev(n), 17)),
        pl.BlockSpec((L, 128), lambda n: (cur(n), 17)),
        pl.BlockSpec((L, 1024), lambda n: (cur(n), 1)),
        _full((HEADS, L, 2 * L)),
        pl.BlockSpec(memory_space=pltpu.SMEM),
    ]


GH = HEADS // KV
GB = 8


def _att_mask_rows(n, nh):
    qi = lax.broadcasted_iota(jnp.int32, (nh * L, 2 * L), 0) & (L - 1)
    kj = lax.broadcasted_iota(jnp.int32, (nh * L, 2 * L), 1)
    dist = qi + L - kj
    return (dist >= 0) & (dist < L) & ((kj >= L) | (n > 0))


def _stack_heads(ref, h0, nh, scr):
    for g in range(nh):
        scr[(h0 + g) * L:(h0 + g + 1) * L, :] = ref[:, (h0 + g) * DH:(h0 + g + 1) * DH].astype(F32)
    return scr[h0 * L:(h0 + nh) * L, :]


def _unstack_heads(val, h0, nh, ref):
    for g in range(nh):
        ref[:, (h0 + g) * DH:(h0 + g + 1) * DH] = val[g * L:(g + 1) * L, :]


def _sink_rows(s_ref, h0, nh):
    return jnp.concatenate([jnp.full((L, 1), s_ref[h0 + g], F32) for g in range(nh)], axis=0)


def _att_probs(qh, kk, bias_h, mask, sk):
    logits = _dot_nt(qh, kk) + bias_h
    logits = jnp.where(mask, logits, NEG)
    m = jnp.maximum(jnp.max(logits, axis=-1, keepdims=True), sk)
    p = jnp.exp(logits - m)
    es = jnp.exp(sk - m)
    den = jnp.sum(p, axis=-1, keepdims=True) + es
    return p / den, es / den


def att_fwd(proj, bias, sinks):
    S = proj.shape[0]
    nb = S // L

    def body(q_ref, kp_ref, kc_ref, vp_ref, vc_ref, z_ref, bias_ref, s_ref, y_ref, o_scr):
        mask = _att_mask(pl.program_id(0))
        for kv in range(KV):
            sl = slice(kv * DH, (kv + 1) * DH)
            kk = jnp.concatenate([kp_ref[:, sl], kc_ref[:, sl]], axis=0).astype(BF16)
            vv = jnp.concatenate([vp_ref[:, sl], vc_ref[:, sl]], axis=0).astype(BF16)
            for g in range(GH):
                h = kv * GH + g
                hs = slice(h * DH, (h + 1) * DH)
                qh = (q_ref[:, hs] * 0.125).astype(BF16)
                P, _ = _att_probs(qh, kk, bias_ref[h], mask, s_ref[h])
                o_scr[:, hs] = _dot(P.astype(BF16), vv)
        z = z_ref[...].astype(F32)
        y_ref[...] = (o_scr[...] * (z * _sigmoid(z))).astype(BF16)

    return pl.pallas_call(
        body, name="att_fwd", grid=(nb,),
        in_specs=_att_in_specs(nb),
        out_specs=pl.BlockSpec((L, 1024), lambda n: (n, 0)),
        out_shape=jax.ShapeDtypeStruct((S, 1024), BF16),
        scratch_shapes=[pltpu.VMEM((L, 1024), F32)],
        compiler_params=_params(("arbitrary",)),
    )(proj, proj, proj, proj, proj, proj, bias, sinks)


def att_bwd(dy, proj, bias, sinks):
    S = proj.shape[0]
    nb = S // L
    last = nb - 1

    def body(dy_ref, q_ref, kp_ref, kc_ref, vp_ref, vc_ref, z_ref, bias_ref, s_ref,
             dout_ref, dbias_ref, dsink_ref, carry, band, dq_scr, dz_scr, qs_scr, zs_scr, dys_scr):
        n = pl.program_id(0)

        @pl.when(n == 0)
        def _():
            carry[...] = jnp.zeros_like(carry)
            dq_scr[...] = jnp.zeros_like(dq_scr)
            dz_scr[...] = jnp.zeros_like(dz_scr)
            dbias_ref[...] = jnp.zeros_like(dbias_ref)
            dsink_ref[...] = jnp.zeros_like(dsink_ref)

        dout_ref[:, 0:1024] = dq_scr[...].astype(BF16)
        dout_ref[:, 1024:2048] = dz_scr[...].astype(BF16)
        band[...] = jnp.zeros_like(band)

        @pl.when(n < nb)
        def _():
            mask = _att_mask_rows(n, GB)
            lane = lax.broadcasted_iota(jnp.int32, (1, 128), 1)
            dsink = jnp.zeros((1, 128), F32)
            for kv in range(KV):
                sl = slice(kv * DH, (kv + 1) * DH)
                kk = jnp.concatenate([kp_ref[:, sl], kc_ref[:, sl]], axis=0).astype(BF16)
                vv = jnp.concatenate([vp_ref[:, sl], vc_ref[:, sl]], axis=0).astype(BF16)
                dk_acc = jnp.zeros((2 * L, DH), F32)
                dv_acc = jnp.zeros((2 * L, DH), F32)
                for h0 in range(kv * GH, (kv + 1) * GH, GB):
                    qs = (_stack_heads(q_ref, h0, GB, qs_scr) * 0.125).astype(BF16)
                    bias_g = bias_ref[h0:h0 + GB].reshape(GB * L, 2 * L)
                    P, psink = _att_probs(qs, kk, bias_g, mask, _sink_rows(s_ref, h0, GB))
                    zs = _stack_heads(z_ref, h0, GB, zs_scr)
                    dys = _stack_heads(dy_ref, h0, GB, dys_scr)
                    sg = _sigmoid(zs)
                    O = _dot(P.astype(BF16), vv)
                    _unstack_heads(dys * O * (sg * (1.0 + zs * (1.0 - sg))), h0, GB, dz_scr)
                    dOb = (dys * (zs * sg)).astype(BF16)
                    dP = _dot_nt(dOb, vv)
                    delta = jnp.sum(P * dP, axis=-1, keepdims=True)
                    dS = P * (dP - delta)
                    sd = psink * delta
                    for g in range(GB):
                        dsink = dsink + jnp.where(lane == h0 + g, -jnp.sum(sd[g * L:(g + 1) * L, :]), 0.0)
                    _unstack_heads(_dot(dS.astype(BF16), kk) * 0.125, h0, GB, dq_scr)
                    dbias_ref[h0:h0 + GB] += dS.reshape(GB, L, 2 * L)
                    dk_acc = dk_acc + _dot_tn(dS, qs)
                    dv_acc = dv_acc + _dot_tn(P, dOb)
                band[:, sl] = dk_acc
                band[:, 128 + kv * DH:128 + (kv + 1) * DH] = dv_acc
            dsink_ref[...] += dsink

        out = carry[...] + band[0:L, :]
        dout_ref[:, 2048:2304] = out.astype(BF16)
        carry[...] = band[L:2 * L, :]

    cur = lambda n: jnp.minimum(n, last)
    lag = lambda n: jnp.maximum(n - 1, 0)
    return pl.pallas_call(
        body, name="att_bwd", grid=(nb + 1,),
        in_specs=[pl.BlockSpec((L, 1024), lambda n: (cur(n), 0))] + _att_in_specs(nb),
        out_specs=[pl.BlockSpec((L, 2304), lambda n: (lag(n), 0)), _full((HEADS, L, 2 * L)), _full((1, 128))],
        out_shape=[jax.ShapeDtypeStruct((S, 2304), BF16),
                   jax.ShapeDtypeStruct((HEADS, L, 2 * L), F32), jax.ShapeDtypeStruct((1, 128), F32)],
        scratch_shapes=[pltpu.VMEM((L, 256), F32), pltpu.VMEM((2 * L, 256), F32),
                        pltpu.VMEM((L, 1024), F32), pltpu.VMEM((L, 1024), F32)]
        + [pltpu.VMEM((HEADS * L, DH), F32)] * 3,
        compiler_params=_params(("arbitrary",)),
    )(dy, proj, proj, proj, proj, proj, proj, bias, sinks)


def _sgu_in_specs():
    return [
        pl.BlockSpec((L, 1024), lambda c: (c, 0)),
        pl.BlockSpec((L, 1024), lambda c: (c, 1)),
        pl.BlockSpec((L, 1024), lambda c: (c, 2)),
        _full((1, 1024)), _full((1, 1024)), _full((8, L, L)), _full((L, 8)),
    ]


def _sgu_norm(v, lg, lb):
    mu = jnp.mean(v, axis=-1, keepdims=True)
    vc = v - mu
    rstd = lax.rsqrt(jnp.mean(vc * vc, axis=-1, keepdims=True) + EPS)
    xhat = vc * rstd
    return xhat * lg + lb, xhat, rstd


def _tril():
    return lax.broadcasted_iota(jnp.int32, (L, L), 0) >= lax.broadcasted_iota(jnp.int32, (L, L), 1)


def sgu_fwd(proj, ln_g, ln_b, w, b_t):
    S = proj.shape[0]

    def body(u_ref, v_ref, z_ref, lg_ref, lb_ref, w_ref, bt_ref, y_ref):
        vn, _, _ = _sgu_norm(v_ref[...].astype(F32), lg_ref[...], lb_ref[...])
        tri = _tril()
        parts = []
        for g in range(8):
            wg = jnp.where(tri, w_ref[g], 0.0).astype(BF16)
            parts.append(_dot(wg, vn[:, g * 128:(g + 1) * 128].astype(BF16)) + bt_ref[:, g:g + 1])
        mixed = jnp.concatenate(parts, axis=1)
        z = z_ref[...].astype(F32)
        y_ref[...] = (u_ref[...].astype(F32) * mixed * (z * _sigmoid(z))).astype(BF16)

    return pl.pallas_call(
        body, name="sgu_fwd", grid=(S // L,),
        in_specs=_sgu_in_specs(),
        out_specs=pl.BlockSpec((L, 1024), lambda c: (c, 0)),
        out_shape=jax.ShapeDtypeStruct((S, 1024), BF16),
        compiler_params=_params(("arbitrary",)),
    )(proj, proj, proj, ln_g, ln_b, w, b_t)


def sgu_bwd(dy, proj, ln_g, ln_b, w, b_t):
    S = proj.shape[0]

    def body(dy_ref, u_ref, v_ref, z_ref, lg_ref, lb_ref, w_ref, bt_ref,
             dout_ref, dw_ref, dbt_ref, dlg_ref, dlb_ref):
        @pl.when(pl.program_id(0) == 0)
        def _():
            dw_ref[...] = jnp.zeros_like(dw_ref)
            dbt_ref[...] = jnp.zeros_like(dbt_ref)
            dlg_ref[...] = jnp.zeros_like(dlg_ref)
            dlb_ref[...] = jnp.zeros_like(dlb_ref)

        lg = lg_ref[...]
        vn, xhat, rstd = _sgu_norm(v_ref[...].astype(F32), lg, lb_ref[...])
        tri = _tril()
        lane = lax.broadcasted_iota(jnp.int32, (L, 128), 1)
        wgs, parts = [], []
        for g in range(8):
            wg = jnp.where(tri, w_ref[g], 0.0)
            wgs.append(wg)
            parts.append(_dot(wg.astype(BF16), vn[:, g * 128:(g + 1) * 128].astype(BF16)) + bt_ref[:, g:g + 1])
        mixed = jnp.concatenate(parts, axis=1)
        z = z_ref[...].astype(F32)
        sg = _sigmoid(z)
        silu = z * sg
        dy_v = dy_ref[...]
        u = u_ref[...].astype(F32)
        dout_ref[:, 0:1024] = (dy_v * mixed * silu).astype(BF16)
        dout_ref[:, 2048:3072] = (dy_v * u * mixed * (sg * (1.0 + z * (1.0 - sg)))).astype(BF16)
        dmixed = dy_v * u * silu
        dbt = jnp.zeros((L, 128), F32)
        dvn_parts = []
        for g in range(8):
            dm = dmixed[:, g * 128:(g + 1) * 128]
            dmb = dm.astype(BF16)
            dbt = dbt + jnp.where(lane == g, jnp.sum(dm, axis=1, keepdims=True), 0.0)
            dw_ref[g] += jnp.where(tri, _dot_nt(dmb, vn[:, g * 128:(g + 1) * 128].astype(BF16)), 0.0)
            dvn_parts.append(_dot_tn(wgs[g], dmb))
        dbt_ref[...] += dbt
        dvn = jnp.concatenate(dvn_parts, axis=1)
        dlg_ref[...] += jnp.sum(dvn * xhat, axis=0, keepdims=True)
        dlb_ref[...] += jnp.sum(dvn, axis=0, keepdims=True)
        dxh = dvn * lg
        dv = rstd * (dxh - jnp.mean(dxh, axis=-1, keepdims=True)
                     - xhat * jnp.mean(dxh * xhat, axis=-1, keepdims=True))
        dout_ref[:, 1024:2048] = dv.astype(BF16)

    return pl.pallas_call(
        body, name="sgu_bwd", grid=(S // L,),
        in_specs=[pl.BlockSpec((L, 1024), lambda c: (c, 0))] + _sgu_in_specs(),
        out_specs=[pl.BlockSpec((L, 3072), lambda c: (c, 0)), _full((8, L, L)), _full((L, 128)),
                   _full((1, 1024)), _full((1, 1024))],
        out_shape=[jax.ShapeDtypeStruct((S, 3072), BF16), jax.ShapeDtypeStruct((8, L, L), F32),
                   jax.ShapeDtypeStruct((L, 128), F32), jax.ShapeDtypeStruct((1, 1024), F32),
                   jax.ShapeDtypeStruct((1, 1024), F32)],
        compiler_params=_params(("arbitrary",)),
    )(dy, proj, proj, proj, ln_g, ln_b, w, b_t)


def _expand_matrix():
    r = lax.broadcasted_iota(jnp.int32, (128, SSM_W), 0)
    c = lax.broadcasted_iota(jnp.int32, (128, SSM_W), 1)
    return (c // SSM_P) == r


def _expand_matrix_t():
    r = lax.broadcasted_iota(jnp.int32, (SSM_W, 128), 0)
    c = lax.broadcasted_iota(jnp.int32, (SSM_W, 128), 1)
    return (r // SSM_P) == c


def _rows_from(ref, start):
    C = ref.shape[1]
    tiles = ref[...].reshape(17, 8, C)
    q, s = divmod(start, 8)
    if s == 0:
        return tiles[q:q + 16].reshape(L, C)
    rolled = pltpu.roll(tiles, 8 - s, axis=1)
    sub = lax.broadcasted_iota(jnp.int32, (16, 8, C), 1)
    return jnp.where(sub < 8 - s, rolled[q:q + 16], rolled[q + 1:q + 17]).reshape(L, C)


def _ssd_common(ext_ref, cw_ref, cb_ref, dt_raw, dtb, alog):
    taps = [_rows_from(ext_ref, 5 + k) for k in range(CONV_K)]
    pre = cb_ref[...]
    for k in range(CONV_K):
        pre = pre + cw_ref[k:k + 1, :] * taps[k]
    sg_pre = _sigmoid(pre)
    xc = pre * sg_pre
    dt = _softplus(dt_raw + dtb)
    a = -jnp.exp(alog)
    adt = dt * a
    acs = _sel_dot(_tril(), adt, 3)
    return pre, sg_pre, xc, dt, a, acs, taps


def _ssd_in_specs(rev, nc):
    cidx = (lambda c: nc - 1 - c) if rev else (lambda c: c)
    return [
        pl.BlockSpec((L, 2048), lambda c: (cidx(c), 0)),
        pl.BlockSpec((L, 1024), lambda c: (cidx(c), 2)),
        pl.BlockSpec((L, 1024), lambda c: (cidx(c), 3)),
        pl.BlockSpec((L, 1024), lambda c: (cidx(c), 4)),
        pl.BlockSpec((L, 128), lambda c: (cidx(c), 40)),
        _full((8, CONV_C)), _full((1, CONV_C)), _full((1, 128)), _full((1, 128)), _full((1, 128)),
        _full((1, SSM_W)),
    ]


def ssd_fwd(proj, conv_w, conv_b, dt_bias, a_log, d_skip, norm_g):
    S = proj.shape[0]
    nc = S // L

    def body(z_ref, xa_ref, xb_ref, xc_ref, dt_ref, cw_ref, cb_ref, dtb_ref, alog_ref, dsk_ref, ng_ref,
             y_ref, hs_ref, H, ext, ysc):
        @pl.when(pl.program_id(0) == 0)
        def _():
            H[...] = jnp.zeros_like(H)
            ext[0:8, :] = jnp.zeros((8, CONV_C), F32)

        for k, ref in enumerate((xa_ref, xb_ref, xc_ref)):
            ext[8:8 + L, k * 1024:(k + 1) * 1024] = ref[...].astype(F32)
        pre, sg_pre, xc, dt, a, acs, _ = _ssd_common(ext, cw_ref, cb_ref, dt_ref[...].astype(F32), dtb_ref[...],
                                                     alog_ref[...])
        for k, ref in enumerate((xa_ref, xb_ref, xc_ref)):
            ext[0:8, k * 1024:(k + 1) * 1024] = ref[L - 8:L, :].astype(F32)
        xs = xc[:, 0:SSM_W]
        acs_t = acs.T
        ex = _expand_matrix()
        dt_x = _dot_sel(dt, ex, 2)
        xdt = xs * dt_x
        eacs_x = _dot_sel(jnp.exp(acs), ex, 2)
        xw = xdt * _dot_sel(jnp.exp(acs[L - 1:L, :] - acs), ex, 2)
        cd_row = jnp.exp(acs[L - 1:L, :])
        hs_ref[0] = H[...]
        tri = _tril()
        for g in range(SSM_G):
            gs = slice(g * 512, (g + 1) * 512)
            bg = xc[:, SSM_W + g * SSM_N:SSM_W + (g + 1) * SSM_N].astype(BF16)
            cg = xc[:, SSM_W + 512 + g * SSM_N:SSM_W + 512 + (g + 1) * SSM_N].astype(BF16)
            G = _dot_nt(cg, bg)
            yoff = _dot_nt(cg, H[gs, :].astype(BF16)) * eacs_x[:, gs]
            Sg = _dot_tn(xw[:, gs], bg)
            for j in range(8):
                hh = g * 8 + j
                hs = slice(hh * SSM_P, (hh + 1) * SSM_P)
                seg = acs[:, hh:hh + 1] - acs_t[hh:hh + 1, :]
                dk = jnp.where(tri, jnp.exp(jnp.minimum(seg, 0.0)), 0.0)
                yd = _dot((G * dk).astype(BF16), xdt[:, hs].astype(BF16))
                ysc[:, hs] = yd + yoff[:, j * SSM_P:(j + 1) * SSM_P]
                H[hs, :] = H[hs, :] * cd_row[:, hh:hh + 1] + Sg[j * SSM_P:(j + 1) * SSM_P, :]
        d_x = _dot_sel(jnp.broadcast_to(dsk_ref[...], (8, 128)), ex, 3)[0:1, :]
        Y = ysc[...] + d_x * xs
        z = z_ref[...].astype(F32)
        yz = Y * (z * _sigmoid(z))
        ng = ng_ref[...]
        for g in range(SSM_G):
            gs = slice(g * 512, (g + 1) * 512)
            t = yz[:, gs]
            rstd = lax.rsqrt(jnp.mean(t * t, axis=-1, keepdims=True) + EPS)
            y_ref[:, gs] = (t * rstd * ng[:, gs]).astype(BF16)

    return pl.pallas_call(
        body, name="ssd_fwd", grid=(nc,),
        in_specs=_ssd_in_specs(False, nc),
        out_specs=[pl.BlockSpec((L, SSM_W), lambda c: (c, 0)), pl.BlockSpec((1, SSM_W, SSM_N), lambda c: (c, 0, 0))],
        out_shape=[jax.ShapeDtypeStruct((S, SSM_W), BF16), jax.ShapeDtypeStruct((nc, SSM_W, SSM_N), F32)],
        scratch_shapes=[pltpu.VMEM((SSM_W, SSM_N), F32), pltpu.VMEM((8 + L, CONV_C), F32),
                        pltpu.VMEM((L, SSM_W), F32)],
        compiler_params=_params(("arbitrary",)),
    )(proj, proj, proj, proj, proj, conv_w, conv_b, dt_bias, a_log, d_skip, norm_g)


def ssd_bwd(dy, proj, hstates, conv_w, conv_b, dt_bias, a_log, d_skip, norm_g):
    S = proj.shape[0]
    nc = S // L
    cidx = lambda c: nc - 1 - c

    def body(dy_ref, z_ref, xa_ref, xb_ref, xc_ref, dt_ref, cw_ref, cb_ref, dtb_ref, alog_ref, dsk_ref, ng_ref,
             pa_ref, pb_ref, pc_ref, hp_ref,
             dout_ref, dcw_ref, dcb_ref, ddtb_ref, dalog_ref, ddsk_ref, dng_ref,
             dH, ext, dext, ysc, yoffsc, dxdt, dxc, tsc):
        step = pl.program_id(0)
        c = nc - 1 - step

        @pl.when(step == 0)
        def _():
            dH[...] = jnp.zeros_like(dH)
            dext[L:L + 8, :] = jnp.zeros((8, CONV_C), F32)
            for r in (dcw_ref, dcb_ref, ddtb_ref, dalog_ref, ddsk_ref, dng_ref):
                r[...] = jnp.zeros_like(r)

        for k, (ref, prev) in enumerate(((xa_ref, pa_ref), (xb_ref, pb_ref), (xc_ref, pc_ref))):
            ext[0:8, k * 1024:(k + 1) * 1024] = jnp.where(c > 0, prev[8:16, :].astype(F32), 0.0)
            ext[8:8 + L, k * 1024:(k + 1) * 1024] = ref[...].astype(F32)
        dtb = dtb_ref[...]
        dt_raw = dt_ref[...].astype(F32)
        pre, sg_pre, xc, dt, a, acs, taps = _ssd_common(ext, cw_ref, cb_ref, dt_raw, dtb, alog_ref[...])
        xs = xc[:, 0:SSM_W]
        acs_t = acs.T
        ex = _expand_matrix()
        dt_x = _dot_sel(dt, ex, 2)
        xdt = xs * dt_x
        eacs_x = _dot_sel(jnp.exp(acs), ex, 2)
        dte_x = _dot_sel(jnp.exp(acs[L - 1:L, :] - acs), ex, 2)
        xw = xdt * dte_x
        cd_row = jnp.exp(acs[L - 1:L, :])
        tri = _tril()

        Gs, Cs, Bs = [], [], []
        for g in range(SSM_G):
            gs = slice(g * 512, (g + 1) * 512)
            bg = xc[:, SSM_W + g * SSM_N:SSM_W + (g + 1) * SSM_N].astype(BF16)
            cg = xc[:, SSM_W + 512 + g * SSM_N:SSM_W + 512 + (g + 1) * SSM_N].astype(BF16)
            G = _dot_nt(cg, bg)
            Gs.append(G), Cs.append(cg), Bs.append(bg)
            yoffsc[:, gs] = _dot_nt(cg, hp_ref[0, gs, :].astype(BF16)) * eacs_x[:, gs]
            for j in range(8):
                hh = g * 8 + j
                hs = slice(hh * SSM_P, (hh + 1) * SSM_P)
                seg = acs[:, hh:hh + 1] - acs_t[hh:hh + 1, :]
                dk = jnp.where(tri, jnp.exp(jnp.minimum(seg, 0.0)), 0.0)
                ysc[:, hs] = _dot((G * dk).astype(BF16), xdt[:, hs].astype(BF16))
        d_x = _dot_sel(jnp.broadcast_to(dsk_ref[...], (8, 128)), ex, 3)[0:1, :]
        yoff = yoffsc[...]
        Y = ysc[...] + yoff + d_x * xs

        z = z_ref[...].astype(F32)
        sgz = _sigmoid(z)
        silu_z = z * sgz
        yz = Y * silu_z
        ng = ng_ref[...]
        dout = dy_ref[...]
        dyn = dout * ng
        dyz_parts, dng_parts = [], []
        for g in range(SSM_G):
            gs = slice(g * 512, (g + 1) * 512)
            t = yz[:, gs]
            rstd = lax.rsqrt(jnp.mean(t * t, axis=-1, keepdims=True) + EPS)
            dng_parts.append(jnp.sum(dout[:, gs] * t * rstd, axis=0, keepdims=True))
            dn = dyn[:, gs]
            dyz_parts.append(rstd * dn - t * (rstd * rstd * rstd) * jnp.mean(dn * t, axis=-1, keepdims=True))
        dng_ref[...] += jnp.concatenate(dng_parts, axis=1)
        dyz = jnp.concatenate(dyz_parts, axis=1)
        dY = dyz * silu_z
        dout_ref[:, 0:SSM_W] = (dyz * Y * (sgz * (1.0 + z * (1.0 - sgz)))).astype(BF16)

        ex_t = _expand_matrix_t()
        ddsk_ref[...] += _dot_sel(jnp.broadcast_to(jnp.sum(dY * xs, axis=0, keepdims=True), (8, SSM_W)), ex_t, 3)[0:1, :]

        lane = lax.broadcasted_iota(jnp.int32, (L, 128), 1)
        subl = lax.broadcasted_iota(jnp.int32, (128, L), 0)
        coll = lax.broadcasted_iota(jnp.int32, (128, L), 1)
        r_cols = jnp.zeros((L, 128), F32)
        c_rows = jnp.zeros((128, L), F32)
        for g in range(SSM_G):
            gs = slice(g * 512, (g + 1) * 512)
            G, cg, bg = Gs[g], Cs[g], Bs[g]
            hp_g = hp_ref[0, gs, :]
            dh_g = dH[gs, :]
            dY_g = dY[:, gs]
            dZ = dY_g * eacs_x[:, gs]
            dZb = dZ.astype(BF16)
            dC = _dot(dZb, hp_g.astype(BF16))
            dh_from_off = _dot_tn(dZ, cg)
            dhb = dh_g.astype(BF16)
            Q = _dot_nt(bg, dhb)
            dB = _dot(xw[:, gs].astype(BF16), dhb)
            qd = Q * dte_x[:, gs]
            dxdt[:, gs] = qd
            tsc[:, gs] = qd * xdt[:, gs]
            dG = jnp.zeros((L, L), F32)
            for j in range(8):
                hh = g * 8 + j
                hs = slice(hh * SSM_P, (hh + 1) * SSM_P)
                seg = acs[:, hh:hh + 1] - acs_t[hh:hh + 1, :]
                dk = jnp.where(tri, jnp.exp(jnp.minimum(seg, 0.0)), 0.0)
                M = G * dk
                dYh = dY[:, hs]
                dYhb = dYh.astype(BF16)
                dM = _dot_nt(dYhb, xdt[:, hs].astype(BF16))
                dxdt[:, hs] += _dot_tn(M, dYhb)
                dG = dG + dM * dk
                Wm = dM * M
                r_cols = r_cols + jnp.where(lane == hh, jnp.sum(Wm, axis=1, keepdims=True), 0.0)
                c_rows = c_rows + jnp.where(subl == hh, jnp.sum(Wm, axis=0, keepdims=True), 0.0)
                pj = slice(j * SSM_P, (j + 1) * SSM_P)
                cd_h = cd_row[:, hh:hh + 1]
                dcd = jnp.sum(dh_g[pj, :] * hp_g[pj, :]) * cd_h
                c_rows = c_rows - jnp.where((subl == hh) & (coll == L - 1), dcd, 0.0)
                dH[hs, :] = dh_g[pj, :] * cd_h + dh_from_off[pj, :]
            dGb = dG.astype(BF16)
            dC = dC + _dot(dGb, bg)
            dB = dB + _dot_tn(dG, cg)
            dxc[:, SSM_W + g * SSM_N:SSM_W + (g + 1) * SSM_N] = dB
            dxc[:, SSM_W + 512 + g * SSM_N:SSM_W + 512 + (g + 1) * SSM_N] = dC

        row = lax.broadcasted_iota(jnp.int32, (L, 128), 0)
        tv = tsc[...]
        t_last = _dot_sel(jnp.broadcast_to(jnp.sum(tv, axis=0, keepdims=True), (8, SSM_W)), ex_t, 3)[0:1, :]
        dacs = (r_cols - c_rows.T + _dot_sel(dY * yoff - tv, ex_t, 2) + jnp.where(row == L - 1, t_last, 0.0))
        triu = lax.broadcasted_iota(jnp.int32, (L, L), 0) <= lax.broadcasted_iota(jnp.int32, (L, L), 1)
        dadt = _sel_dot(triu, dacs, 3)
        dxdt_v = dxdt[...]
        ddt = _dot_sel(dxdt_v * xs, ex_t, 2) + dadt * a
        dalog_ref[...] += jnp.sum(dadt * dt * a, axis=0, keepdims=True)
        ddt_raw = jnp.where(lane < SSM_H, ddt * _sigmoid(dt_raw + dtb), 0.0)
        ddtb_ref[...] += jnp.sum(ddt_raw, axis=0, keepdims=True)
        dout_ref[:, 5120:5248] = ddt_raw.astype(BF16)
        dout_ref[:, 5248:5376] = jnp.zeros((L, 128), BF16)

        dxc[:, 0:SSM_W] = dxdt_v * dt_x + d_x * dY
        dpre = dxc[...] * (sg_pre * (1.0 + pre * (1.0 - sg_pre)))
        dcb_ref[...] += jnp.sum(dpre, axis=0, keepdims=True)
        dext[0:L, :] = dpre
        x_cur = ext[8:8 + L, :]
        dx = None
        for k in range(CONV_K):
            dsh = _rows_from(dext, 3 - k)
            term = cw_ref[k:k + 1, :] * dsh
            dx = term if dx is None else dx + term
            dcw_ref[k:k + 1, :] += jnp.sum(dsh * x_cur, axis=0, keepdims=True)
        dout_ref[:, SSM_W:SSM_W + CONV_C] = dx.astype(BF16)
        dext[L:L + 8, :] = dpre[0:8, :]

    big = lambda w: pl.BlockSpec((L, w), lambda c: (cidx(c), 0))
    return pl.pallas_call(
        body, name="ssd_bwd", grid=(nc,),
        in_specs=[big(SSM_W)] + _ssd_in_specs(True, nc) + [
            pl.BlockSpec((16, 1024), lambda c, k=k: (jnp.maximum(8 * cidx(c) - 1, 0), k)) for k in (2, 3, 4)] + [
            pl.BlockSpec((1, SSM_W, SSM_N), lambda c: (cidx(c), 0, 0))],
        out_specs=[big(5376), _full((8, CONV_C)), _full((1, CONV_C)),
                   _full((1, 128)), _full((1, 128)), _full((1, 128)), _full((1, SSM_W))],
        out_shape=[jax.ShapeDtypeStruct((S, 5376), BF16), jax.ShapeDtypeStruct((8, CONV_C), F32),
                   jax.ShapeDtypeStruct((1, CONV_C), F32), jax.ShapeDtypeStruct((1, 128), F32),
                   jax.ShapeDtypeStruct((1, 128), F32), jax.ShapeDtypeStruct((1, 128), F32),
                   jax.ShapeDtypeStruct((1, SSM_W), F32)],
        scratch_shapes=[pltpu.VMEM((SSM_W, SSM_N), F32), pltpu.VMEM((8 + L, CONV_C), F32),
                        pltpu.VMEM((L + 8, CONV_C), F32), pltpu.VMEM((L, SSM_W), F32),
                        pltpu.VMEM((L, SSM_W), F32), pltpu.VMEM((L, SSM_W), F32),
                        pltpu.VMEM((L, CONV_C), F32), pltpu.VMEM((L, SSM_W), F32)],
        compiler_params=_params(("arbitrary",)),
    )(dy, proj, proj, proj, proj, proj, conv_w, conv_b, dt_bias, a_log, d_skip, norm_g, proj, proj, proj, hstates)


def _resident(shape):
    nd = len(shape)
    return pl.BlockSpec(shape, lambda *_: (0,) * nd, pipeline_mode=pl.Buffered(1))


def merge_fwd(y_att, y_sg, y_ssm, proj, x, w_a, w_s, w_m, w_o, g_post):
    S = x.shape[0]
    tm = 256

    def body(ya_ref, ys_ref, ym_ref, gate_ref, x_ref, wa_ref, ws_ref, wm_ref, wo_ref, gp_ref,
             xn_ref, bra_ref, brs_ref, brm_ref, mg_ref, out_ref):
        bra = _dot(ya_ref[...], wa_ref[...])
        brs = _dot(ys_ref[...], ws_ref[...])
        brm = _dot(ym_ref[...], wm_ref[...])
        bra_ref[...] = bra.astype(BF16)
        brs_ref[...] = brs.astype(BF16)
        brm_ref[...] = brm.astype(BF16)
        gate = gate_ref[...].astype(F32)
        merged = (_sigmoid(gate[:, 0:1024]) * bra + _sigmoid(gate[:, 1024:2048]) * brs
                  + _sigmoid(gate[:, 2048:3072]) * brm)
        mb = merged.astype(BF16)
        mg_ref[...] = mb
        o = _dot(mb, wo_ref[...])
        out_ref[...] = o
        r = lax.rsqrt(jnp.mean(o * o, axis=-1, keepdims=True) + EPS)
        xn_ref[...] = x_ref[...] + o * r * gp_ref[...]

    row = lambda w: pl.BlockSpec((tm, w), lambda i: (i, 0))
    return pl.pallas_call(
        body, name="merge_fwd", grid=(S // tm,),
        in_specs=[row(1024), row(1024), row(2048), pl.BlockSpec((tm, 3072), lambda i: (i, 0)),
                  row(D), _resident((1024, D)), _resident((1024, D)), _resident((2048, D)), _resident((D, D)),
                  _full((1, D))],
        out_specs=[row(D)] * 6,
        out_shape=[jax.ShapeDtypeStruct((S, D), F32)] + [jax.ShapeDtypeStruct((S, D), BF16)] * 4
        + [jax.ShapeDtypeStruct((S, D), F32)],
        compiler_params=_params(("arbitrary",)),
    )(y_att, y_sg, y_ssm, proj, x, w_a, w_s, w_m, w_o, g_post)


def merge_bwd(dy, out, g_post, proj, br_a, br_s, br_m, w_a, w_s, w_m, w_o):
    S = dy.shape[0]
    tm = 256

    def body(dy_ref, o_ref, gp_ref, gate_ref, bra_ref, brs_ref, brm_ref, wa_ref, ws_ref, wm_ref, wo_ref,
             dout_ref, dba_ref, dbs_ref, dbm_ref, dgate_ref, dya_ref, dys_ref, dym_ref, dgp_ref):
        @pl.when(pl.program_id(0) == 0)
        def _():
            dgp_ref[...] = jnp.zeros_like(dgp_ref)

        o = o_ref[...]
        dyv = dy_ref[...]
        r = lax.rsqrt(jnp.mean(o * o, axis=-1, keepdims=True) + EPS)
        dyg = dyv * gp_ref[...]
        do = r * dyg - o * (r * r * r) * jnp.mean(dyg * o, axis=-1, keepdims=True)
        dgp_ref[...] += jnp.sum(dyv * o * r, axis=0, keepdims=True)
        dob = do.astype(BF16)
        dout_ref[...] = dob
        dmerged = _dot_nt(dob, wo_ref[...])
        for idx, (br_ref, dbr_ref, w_ref, dyi_ref) in enumerate((
                (bra_ref, dba_ref, wa_ref, dya_ref), (brs_ref, dbs_ref, ws_ref, dys_ref),
                (brm_ref, dbm_ref, wm_ref, dym_ref))):
            s = _sigmoid(gate_ref[:, idx * 1024:(idx + 1) * 1024].astype(F32))
            dbr = (dmerged * s).astype(BF16)
            dbr_ref[...] = dbr
            dgate_ref[:, idx * 1024:(idx + 1) * 1024] = (dmerged * br_ref[...].astype(F32) * s * (1.0 - s)).astype(BF16)
            dyi_ref[...] = _dot_nt(dbr, w_ref[...])

    row = lambda w: pl.BlockSpec((tm, w), lambda i: (i, 0))
    return pl.pallas_call(
        body, name="merge_bwd", grid=(S // tm,),
        in_specs=[row(D), row(D), _full((1, D)), pl.BlockSpec((tm, 3072), lambda i: (i, 0)),
                  row(D), row(D), row(D),
                  _resident((1024, D)), _resident((1024, D)), _resident((2048, D)), _resident((D, D))],
        out_specs=[row(D), row(D), row(D), row(D), row(3072), row(1024), row(1024), row(2048), _full((1, D))],
        out_shape=[jax.ShapeDtypeStruct((S, D), BF16)] * 4 + [
            jax.ShapeDtypeStruct((S, 3072), BF16), jax.ShapeDtypeStruct((S, 1024), F32),
            jax.ShapeDtypeStruct((S, 1024), F32), jax.ShapeDtypeStruct((S, 2048), F32),
            jax.ShapeDtypeStruct((1, D), F32)],
        compiler_params=_params(("arbitrary",)),
    )(dy, out, g_post, proj, br_a, br_s, br_m, w_a, w_s, w_m, w_o)


def loss_head(y, target):
    S = y.shape[0]
    tm = 512

    def body(y_ref, t_ref, dy_ref, loss_ref):
        @pl.when(pl.program_id(0) == 0)
        def _():
            loss_ref[...] = jnp.zeros_like(loss_ref)
        e = y_ref[...] - t_ref[...]
        dy_ref[...] = e * (1.0 / D)
        loss_ref[...] += 0.5 * jnp.sum(jnp.mean(e * e, axis=-1, keepdims=True))

    row = pl.BlockSpec((tm, D), lambda i: (i, 0))
    return pl.pallas_call(
        body, name="loss_head", grid=(S // tm,),
        in_specs=[row, row], out_specs=[row, _full((1, 128))],
        out_shape=[jax.ShapeDtypeStruct((S, D), F32), jax.ShapeDtypeStruct((1, 128), F32)],
        compiler_params=_params(("arbitrary",)),
    )(y, target)


def _adam(w, g, m, v):
    mn = ADAM_B1 * m + (1.0 - ADAM_B1) * g
    vn = ADAM_B2 * v + (1.0 - ADAM_B2) * (g * g)
    m_hat = mn / (1.0 - ADAM_B1 ** ADAM_STEP)
    v_hat = vn / (1.0 - ADAM_B2 ** ADAM_STEP)
    return -ADAM_LR * (m_hat / (jnp.sqrt(v_hat) + ADAM_EPS) + ADAM_WD * w), mn, vn


def adamw_big(w, m, v, halves0, sum1, cc, name, tr):
    _, R, C = w.shape
    nper = R // tr
    f, fb, n0, off_a, off_b = halves0
    p, pb, off1 = sum1

    def body(c_ref, w_ref, m_ref, v_ref, f_ref, fb_ref, p_ref, pb_ref, g_ref, d_ref, nm_ref, nv_ref):
        i = pl.program_id(0)
        half = jnp.where(i % nper >= n0, 1, 0)
        g0 = jnp.where(c_ref[0] == half, f_ref[...], fb_ref[...])
        g = jnp.where(i < nper, g0, p_ref[...] + pb_ref[...])
        g_ref[0] = g
        d_ref[0], nm_ref[0], nv_ref[0] = _adam(w_ref[0], g, m_ref[0], v_ref[0])

    def blk0(i, c):
        il = jnp.minimum(i, nper - 1)
        return (jnp.where(il >= n0, off_b + il - n0, off_a + il), 0)

    wblk = pl.BlockSpec((1, tr, C), lambda i, c: (i // nper, i % nper, 0))
    b0 = pl.BlockSpec((tr, C), blk0)
    b1 = pl.BlockSpec((tr, C), lambda i, c: (off1 + jnp.maximum(i - nper, 0), 0))
    grid_spec = pltpu.PrefetchScalarGridSpec(
        num_scalar_prefetch=1, grid=(2 * nper,),
        in_specs=[wblk, wblk, wblk, b0, b0, b1, b1], out_specs=[wblk] * 4)
    return pl.pallas_call(
        body, name=name, grid_spec=grid_spec,
        out_shape=[jax.ShapeDtypeStruct(w.shape, F32)] * 4,
        compiler_params=_params(("arbitrary",)),
    )(cc, w, m, v, f, fb, p, pb)


def adamw_plain(w, g, m, v, name):
    def body(w_ref, g_ref, m_ref, v_ref, d_ref, nm_ref, nv_ref):
        d_ref[...], nm_ref[...], nv_ref[...] = _adam(w_ref[...], g_ref[...], m_ref[...], v_ref[...])

    return pl.pallas_call(
        body, name=name, out_shape=[jax.ShapeDtypeStruct(w.shape, F32)] * 3, compiler_params=_params(),
    )(w, g, m, v)


SMALL = {"norm_pre": ("g_pre", 8), "norm_post": ("g_post", 8), "att_sinks": ("sinks", 8), "sg_ln_g": ("ln_g", 8),
         "sg_ln_b": ("ln_b", 8), "sg_w": ("sg_w", 1024), "sg_b": ("sg_bt", 8), "ssm_conv_b": ("conv_b", 24),
         "ssm_dt_bias": ("dt_bias", 8), "ssm_a_log": ("a_log", 8), "ssm_d": ("d_skip", 8), "ssm_norm_g": ("norm_g", 16)}
SMALL_LAYER_ROWS = sum(r for _, r in SMALL.values())
REL_ROW = DEPTH * SMALL_LAYER_ROWS
LOSS_ROW = REL_ROW + 32
SMALL_ROWS = LOSS_ROW + 8


def _small_rows():
    rows, r = {}, 0
    for l in range(DEPTH):
        for name, (_, n) in SMALL.items():
            rows[(l, name)] = r
            r += n
    return rows


def adamw_small(red, rel, small):
    names = list(SMALL) + ["rel_bias"]
    params = dict(small, rel_bias=rel)
    rows = _small_rows()

    def grad_of(red_ref, l, name, n):
        r0 = rows[(l, name)]
        if name == "sg_b":
            return red_ref[r0:r0 + 8, :]
        if n < 128:
            return red_ref[r0:r0 + 1, 0:n]
        return jnp.concatenate([red_ref[r0 + j:r0 + j + 1, :] for j in range(n // 128)], axis=1)

    def body(red_ref, *refs):
        ins, outs = refs[:3 * len(names)], refs[3 * len(names):]
        for i, name in enumerate(names):
            w_ref, m_ref, v_ref = ins[3 * i:3 * i + 3]
            o = outs[4 * i:4 * i + 4]
            if name == "rel_bias":
                g = red_ref[REL_ROW:REL_ROW + 32, 0:16]
                o[0][...] = g
                o[1][...], o[2][...], o[3][...] = _adam(w_ref[...], g, m_ref[...], v_ref[...])
                continue
            for l in range(DEPTH):
                if name == "sg_w":
                    for grp in range(8):
                        r0 = rows[(l, name)] + grp * 128
                        g = red_ref[r0:r0 + 128, :]
                        o[0][l, grp] = g
                        o[1][l, grp], o[2][l, grp], o[3][l, grp] = _adam(w_ref[l, grp], g, m_ref[l, grp], v_ref[l, grp])
                elif name == "sg_b":
                    g = grad_of(red_ref, l, name, 128)
                    o[0][l] = g
                    o[1][l], o[2][l], o[3][l] = _adam(w_ref[l], g, m_ref[l], v_ref[l])
                else:
                    sl = slice(l, l + 1)
                    g = grad_of(red_ref, l, name, w_ref.shape[-1])
                    o[0][sl, :] = g
                    o[1][sl, :], o[2][sl, :], o[3][sl, :] = _adam(w_ref[sl, :], g, m_ref[sl, :], v_ref[sl, :])

    flat_in = [a for name in names for a in params[name]]
    out_shape = [jax.ShapeDtypeStruct(params[name][0].shape, F32) for name in names for _ in range(4)]
    res = pl.pallas_call(body, name="adamw_small", out_shape=out_shape, compiler_params=_params())(red, *flat_in)
    return {name: tuple(res[4 * i:4 * i + 4]) for i, name in enumerate(names)}


ANY = pl.BlockSpec(memory_space=pl.ANY)


def _place():
    x, y, c = lax.axis_index("x"), lax.axis_index("y"), lax.axis_index("c")
    others = [(1 - x, y), (x, 1 - y), (1 - x, 1 - y)]
    return x, y, c, others


def _rcopy(src, dst, ssem, rsem, to):
    return pltpu.make_async_remote_copy(src_ref=src, dst_ref=dst, send_sem=ssem, recv_sem=rsem,
                                        device_id=to, device_id_type=MESH)


def gather_weights(arrs):
    n = len(arrs)

    def body(*refs):
        srcs, outs, ssem, rsem = refs[:n], refs[n:2 * n], refs[2 * n], refs[2 * n + 1]
        x, y, c, others = _place()
        me = 2 * x + y
        sib = (x, y, 1 - c)
        first = [_rcopy(srcs[i].at[c], outs[i].at[c, me], ssem.at[6 * i + k], rsem.at[6 * i + k], (ox, oy, c))
                 for i in range(n) for k, (ox, oy) in enumerate(others)]
        for cp in first:
            cp.start()
        passed = []
        for k, (ox, oy) in enumerate(others):
            for i in range(n):
                slot = outs[i].at[c, 2 * ox + oy]
                _rcopy(slot, slot, ssem.at[6 * i + k], rsem.at[6 * i + k], sib).wait_recv()
                fw = _rcopy(slot, slot, ssem.at[6 * i + 3 + k], rsem.at[6 * i + 3 + k], sib)
                fw.start()
                passed.append(fw)
        for k, (ox, oy) in enumerate(others):
            for i in range(n):
                slot = outs[i].at[1 - c, 2 * ox + oy]
                _rcopy(slot, slot, ssem.at[6 * i + 3 + k], rsem.at[6 * i + 3 + k], sib).wait_recv()
        for cp in first + passed:
            cp.wait_send()

    return pl.pallas_call(
        body, name="gather_weights",
        in_specs=[ANY] * n, out_specs=[ANY] * n,
        out_shape=[jax.ShapeDtypeStruct((2, SHARDS) + a.shape[1:], a.dtype) for a in arrs],
        scratch_shapes=[pltpu.SemaphoreType.DMA((6 * n,)), pltpu.SemaphoreType.DMA((6 * n,))],
    )(*arrs)


HBM = pl.BlockSpec(memory_space=pltpu.HBM)
SEM = pl.BlockSpec(memory_space=pltpu.SEMAPHORE)
EFFECT = pltpu.SideEffectType.DATAFLOW_SIDE_EFFECTING


def _in_hbm(a):
    return pltpu.with_memory_space_constraint(a, pltpu.HBM)


def gather_start(srcs, after, name, by_dest=False):
    n = len(srcs)
    lands = [_in_hbm(lax.empty((SHARDS,) + a.shape[-2:], a.dtype)) for a in srcs]
    na = len(after)

    def body(*refs):
        src, land = refs[:n], refs[n:2 * n]
        ssem, rsem, token = refs[2 * n + na], refs[2 * n + na + 1], refs[-1]
        x, y, c, others = _place()
        me = 2 * x + y
        for i in range(n):
            for k, (ox, oy) in enumerate(others):
                s = src[i].at[2 * ox + oy] if by_dest else src[i]
                _rcopy(s, land[i].at[me], ssem.at[3 * i + k], rsem.at[3 * i + k], (ox, oy, c)).start()
        token[...] = jnp.zeros_like(token)

    bufs = [_in_hbm(a) for a in srcs] + lands
    out = pl.pallas_call(
        body, name=name,
        out_shape=(pltpu.SemaphoreType.DMA((3 * n,)), pltpu.SemaphoreType.DMA((3 * n,)),
                   *[pltpu.HBM(b.shape, b.dtype) for b in bufs], jax.ShapeDtypeStruct((8, 128), F32)),
        in_specs=[HBM] * (2 * n) + [ANY] * na,
        out_specs=(SEM, SEM, *[HBM] * (2 * n), pl.BlockSpec(memory_space=pltpu.VMEM)),
        input_output_aliases={i: 2 + i for i in range(2 * n)},
        compiler_params=pltpu.CompilerParams(has_side_effects=EFFECT),
    )(*bufs, *after)
    return out[0], out[1], list(out[2:2 + n]), list(out[2 + n:2 + 2 * n]), out[-1]


def gather_wait(ssem, rsem, srcs, lands, after, name, by_dest=False):
    n = len(srcs)

    def body(*refs):
        src, land = refs[:n], refs[n:2 * n]
        s_sem, r_sem = refs[2 * n], refs[2 * n + 1]
        x, y, c, others = _place()
        for i in range(n):
            for k, (ox, oy) in enumerate(others):
                s = src[i].at[2 * ox + oy] if by_dest else src[i]
                cp = _rcopy(s, land[i].at[2 * ox + oy], s_sem.at[3 * i + k], r_sem.at[3 * i + k], (ox, oy, c))
                cp.wait_send()
                cp.wait_recv()

    bufs = list(srcs) + list(lands)
    out = pl.pallas_call(
        body, name=name,
        out_shape=tuple(pltpu.HBM(b.shape, b.dtype) for b in bufs),
        in_specs=[HBM] * (2 * n) + [SEM, SEM, ANY],
        out_specs=tuple([HBM] * (2 * n)),
        input_output_aliases={i: i for i in range(2 * n)},
        compiler_params=pltpu.CompilerParams(has_side_effects=EFFECT),
    )(*bufs, ssem, rsem, after)
    return list(out[n:2 * n])


def grad_sibling_exchange(arrs):
    n = len(arrs)

    def body(*refs):
        srcs, outs, ssem, rsem = refs[:n], refs[n:2 * n], refs[2 * n], refs[2 * n + 1]
        x, y, c, _ = _place()
        cps = [_rcopy(srcs[i].at[1 - c], outs[i], ssem.at[i], rsem.at[i], (x, y, 1 - c)) for i in range(n)]
        for cp in cps:
            cp.start()
        for cp in cps:
            cp.wait()

    return pl.pallas_call(
        body, name="grad_sibling_exchange",
        in_specs=[ANY] * n, out_specs=[ANY] * n,
        out_shape=[jax.ShapeDtypeStruct(a.shape[1:], F32) for a in arrs],
        scratch_shapes=[pltpu.SemaphoreType.DMA((n,)), pltpu.SemaphoreType.DMA((n,))],
    )(*arrs)


def grad_chip_sum(g, sb, cc, tr, name):
    _, _, R, C = g.shape
    blk = pl.BlockSpec((1, tr, C), lambda s, r, c: (s, r, 0))
    grid_spec = pltpu.PrefetchScalarGridSpec(
        num_scalar_prefetch=1, grid=(SHARDS, R // tr),
        in_specs=[pl.BlockSpec((1, 1, tr, C), lambda s, r, c: (c[0], s, r, 0)), blk],
        out_specs=[blk, blk])

    def body(c_ref, a_ref, b_ref, o_ref, ob_ref):
        t = a_ref[0] + b_ref[...]
        o_ref[...] = t
        ob_ref[...] = t.astype(BF16)

    return pl.pallas_call(
        body, name=name, grid_spec=grid_spec,
        out_shape=[jax.ShapeDtypeStruct((SHARDS, R, C), F32), jax.ShapeDtypeStruct((SHARDS, R, C), BF16)],
        compiler_params=_params(("arbitrary", "arbitrary")),
    )(cc, g, sb)


def grad_chip_exchange(arrs):
    n = len(arrs)

    def body(*refs):
        srcs, outs, ssem, rsem = refs[:n], refs[n:2 * n], refs[2 * n], refs[2 * n + 1]
        x, y, c, others = _place()
        me = 2 * x + y
        sends = [_rcopy(srcs[i].at[2 * ox + oy], outs[i].at[me], ssem.at[3 * i + k], rsem.at[3 * i + k], (ox, oy, c))
                 for i in range(n) for k, (ox, oy) in enumerate(others)]
        for cp in sends:
            cp.start()
        for i in range(n):
            for k, (ox, oy) in enumerate(others):
                slot = outs[i].at[2 * ox + oy]
                _rcopy(slot, slot, ssem.at[3 * i + k], rsem.at[3 * i + k], (ox, oy, c)).wait_recv()
        for cp in sends:
            cp.wait_send()

    return pl.pallas_call(
        body, name="grad_chip_exchange",
        in_specs=[ANY] * n, out_specs=[ANY] * n,
        out_shape=[jax.ShapeDtypeStruct(a.shape, a.dtype) for a in arrs],
        scratch_shapes=[pltpu.SemaphoreType.DMA((3 * n,)), pltpu.SemaphoreType.DMA((3 * n,))],
    )(*arrs)


def grad_shard_sum(t, rb, me, tr, name):
    _, R, C = t.shape
    grid_spec = pltpu.PrefetchScalarGridSpec(
        num_scalar_prefetch=1, grid=(R // tr,),
        in_specs=[pl.BlockSpec((1, tr, C), lambda r, m: (m[0], r, 0)),
                  pl.BlockSpec((SHARDS, tr, C), lambda r, m: (0, r, 0))],
        out_specs=pl.BlockSpec((tr, C), lambda r, m: (r, 0)))

    def body(m_ref, t_ref, r_ref, o_ref):
        part = [jnp.where(m_ref[0] == s, t_ref[0], r_ref[s].astype(F32)) for s in range(SHARDS)]
        o_ref[...] = ((part[0] + part[1]) + part[2]) + part[3]

    return pl.pallas_call(
        body, name=name, grid_spec=grid_spec,
        out_shape=jax.ShapeDtypeStruct((R, C), F32),
        compiler_params=_params(("arbitrary",)),
    )(me, t, rb)


def grad_sibling_share(arrs, name):
    n = len(arrs)

    def body(*refs):
        srcs, outs, ssem, rsem = refs[:n], refs[n:2 * n], refs[2 * n], refs[2 * n + 1]
        x, y, c, _ = _place()
        cps = [_rcopy(srcs[i], outs[i], ssem.at[i], rsem.at[i], (x, y, 1 - c)) for i in range(n)]
        for cp in cps:
            cp.start()
        for cp in cps:
            cp.wait()

    return pl.pallas_call(
        body, name=name,
        in_specs=[ANY] * n, out_specs=[ANY] * n,
        out_shape=[jax.ShapeDtypeStruct(a.shape, F32) for a in arrs],
        scratch_shapes=[pltpu.SemaphoreType.DMA((n,)), pltpu.SemaphoreType.DMA((n,))],
    )(*arrs)


def _allreduce_rows(src, sib_buf, chips, out_ref, ssem, rsem):
    x, y, c, others = _place()
    me = 2 * x + y
    cp = _rcopy(src, sib_buf, ssem.at[0], rsem.at[0], (x, y, 1 - c))
    cp.start()
    cp.wait()
    chips[me] = src[...] + sib_buf[...]
    sends = [_rcopy(chips.at[me], chips.at[me], ssem.at[1 + k], rsem.at[1 + k], (ox, oy, c))
             for k, (ox, oy) in enumerate(others)]
    for s in sends:
        s.start()
    for k, (ox, oy) in enumerate(others):
        slot = chips.at[2 * ox + oy]
        _rcopy(slot, slot, ssem.at[1 + k], rsem.at[1 + k], (ox, oy, c)).wait_recv()
    for s in sends:
        s.wait_send()
    out_ref[...] = ((chips[0] + chips[1]) + chips[2]) + chips[3]


def _allreduce_scratch(rows):
    return [pltpu.VMEM((rows, 128), F32), pltpu.VMEM((SHARDS, rows, 128), F32),
            pltpu.SemaphoreType.DMA((4,)), pltpu.SemaphoreType.DMA((4,))]


def allreduce_rows(buf, name):
    rows = buf.shape[0]
    VM = pl.BlockSpec(memory_space=pltpu.VMEM)

    def body(src_ref, out_ref, sib_buf, chips, ssem, rsem):
        _allreduce_rows(src_ref, sib_buf, chips, out_ref, ssem, rsem)

    return pl.pallas_call(
        body, name=name, in_specs=[VM], out_specs=VM,
        out_shape=jax.ShapeDtypeStruct((rows, 128), F32),
        scratch_shapes=_allreduce_scratch(rows), compiler_params=_params(),
    )(buf)


def small_allreduce(grads, rel, loss_part):
    rows = _small_rows()
    keys = [(l, name) for l in range(DEPTH) for name in SMALL]
    flat = [grads[l][SMALL[name][0]] for l, name in keys] + [rel, loss_part]

    def body(*refs):
        ins = refs[:len(flat)]
        out_ref, src, sib_buf, chips, ssem, rsem = refs[len(flat):]
        src[...] = jnp.zeros_like(src)
        for (l, name), ref in zip(keys, ins):
            r0 = rows[(l, name)]
            if name == "sg_w":
                for grp in range(8):
                    src[r0 + grp * 128:r0 + (grp + 1) * 128, :] = ref[grp]
            elif name == "sg_b":
                src[r0:r0 + 8, :] = ref[...].T[0:8, :]
            else:
                for j in range(ref.shape[1] // 128):
                    src[r0 + j:r0 + j + 1, :] = ref[:, j * 128:(j + 1) * 128]
        src[REL_ROW:REL_ROW + 32, 0:16] = ins[-2][...]
        src[LOSS_ROW:LOSS_ROW + 1, :] = ins[-1][...]
        _allreduce_rows(src, sib_buf, chips, out_ref, ssem, rsem)

    return pl.pallas_call(
        body, name="small_allreduce",
        out_shape=jax.ShapeDtypeStruct((SMALL_ROWS, 128), F32),
        scratch_shapes=[pltpu.VMEM((SMALL_ROWS, 128), F32)] + _allreduce_scratch(SMALL_ROWS),
        compiler_params=_params(),
    )(*flat)


def _pad_lanes(v):
    return jnp.zeros((1, 128), F32).at[0, :v.shape[0]].set(v)


def layer_fwd(x, wts, bias):
    wt = wts["wt"]
    tn = {name: t for name, _, t in GROUPS}
    p_gate, h = inproj_first(x, wts["g_pre"], wt["gate"], tn["gate"], "inproj_gate")
    p_sgu, p_att, p_ssd = (inproj_group(h, wt[n], tn[n], "inproj_" + n) for n in ("sgu", "att", "ssd"))
    y_att = att_fwd(p_att, bias, wts["sinks"])
    y_sg = sgu_fwd(p_sgu, wts["ln_g"], wts["ln_b"], wts["sg_w"], wts["sg_bt"])
    y_ssm, hst = ssd_fwd(p_ssd, wts["conv_w"], wts["conv_b"], wts["dt_bias"], wts["a_log"], wts["d_skip"],
                         wts["norm_g"])
    x_new, br_a, br_s, br_m, merged, out = merge_fwd(
        y_att, y_sg, y_ssm, p_gate, x, wts["w_a"], wts["w_s"], wts["w_m"], wts["w_o"], wts["g_post"])
    saved = dict(x=x, p_gate=p_gate, p_sgu=p_sgu, p_att=p_att, p_ssd=p_ssd, h=h,
                 y_att=y_att, y_sg=y_sg, y_ssm=y_ssm, hst=hst,
                 br_a=br_a, br_s=br_s, br_m=br_m, merged=merged, out=out)
    return x_new, saved


def layer_bwd(dy, wts, bias, sv):
    dout, dba, dbs, dbm, d_gate, dya, dys, dym, dg_post = merge_bwd(
        dy, sv["out"], wts["g_post"], sv["p_gate"], sv["br_a"], sv["br_s"], sv["br_m"],
        wts["w_a"], wts["w_s"], wts["w_m"], wts["w_o"])
    d_att, dbias, dsinks = att_bwd(dya, sv["p_att"], bias, wts["sinks"])
    d_sgu, dsg_w, dsg_bt, dln_g, dln_b = sgu_bwd(dys, sv["p_sgu"], wts["ln_g"], wts["ln_b"], wts["sg_w"],
                                                 wts["sg_bt"])
    d_ssd, dcw, dcb, ddtb, dalog, ddsk, dng = ssd_bwd(
        dym, sv["p_ssd"], sv["hst"], wts["conv_w"], wts["conv_b"], wts["dt_bias"], wts["a_log"], wts["d_skip"],
        wts["norm_g"])
    dps = dict(gate=d_gate, sgu=d_sgu, att=d_att, ssd=d_ssd)
    wt = wts["wt"]
    tn = {name: t for name, _, t in GROUPS}
    acc = None
    for n in ("gate", "sgu", "ssd"):
        acc = dh_group(dps[n], wt[n], acc, tn[n], "dh_" + n)
    dx, dg_pre = dh_last(dps["att"], wt["att"], acc, sv["x"], wts["g_pre"], dy, tn["att"], "dh_att")
    grads = dict(
        w_in={n: dw_group(dps[n], sv["h"], tn[n], "dw_in_" + n) for n in dps},
        w_a=matmul_tn(sv["y_att"], dba, "dw_att"),
        w_s=matmul_tn(sv["y_sg"], dbs, "dw_sg"),
        w_m=matmul_tn(sv["y_ssm"], dbm, "dw_ssm"),
        w_o=matmul_tn(sv["merged"], dout, "dw_out"),
        g_pre=dg_pre, g_post=dg_post, sinks=dsinks, ln_g=dln_g, ln_b=dln_b, sg_w=dsg_w, sg_bt=dsg_bt,
        conv_w=dcw, conv_b=dcb, dt_bias=ddtb, a_log=dalog, d_skip=ddsk, norm_g=dng, bias=dbias)
    return dx, grads


REST_OFF = (0, 256, 512, 1024, 1280)
GR_ROWS = 1536
GR_CONV = 1280
W_IN_SPLIT = 1600
W_IN_HALF = 1824


def kernel(x, w_in, norm_pre, norm_post, rel_bias, att_sinks, sg_ln_g, sg_ln_b, sg_w, sg_b, ssm_conv_w, ssm_conv_b, ssm_dt_bias, ssm_a_log, ssm_d, ssm_norm_g, w_br_att, w_br_sg, w_br_ssm, w_out, loss_target, m_w_in, m_norm_pre, m_norm_post, m_rel_bias, m_att_sinks, m_sg_ln_g, m_sg_ln_b, m_sg_w, m_sg_b, m_ssm_conv_w, m_ssm_conv_b, m_ssm_dt_bias, m_ssm_a_log, m_ssm_d, m_ssm_norm_g, m_w_br_att, m_w_br_sg, m_w_br_ssm, m_w_out, v_w_in, v_norm_pre, v_norm_post, v_rel_bias, v_att_sinks, v_sg_ln_g, v_sg_ln_b, v_sg_w, v_sg_b, v_ssm_conv_w, v_ssm_conv_b, v_ssm_dt_bias, v_ssm_a_log, v_ssm_d, v_ssm_norm_g, v_w_br_att, v_w_br_sg, v_w_br_ssm, v_w_out):
    cx, cy, cc = lax.axis_index("x"), lax.axis_index("y"), lax.axis_index("c")
    me = 2 * cx + cy
    xs = x[0]
    S = xs.shape[0]

    tr = lambda a: jnp.transpose(a, (0, 2, 1))
    w_in_b = tr(w_in).astype(BF16)
    w_rest_b = jnp.concatenate([w_br_att, w_br_sg, w_br_ssm, w_out], axis=1).astype(BF16)
    halves = lambda a: a.reshape(2, a.shape[0] // 2, a.shape[1])
    all0_in, all0_rest = gather_weights([halves(w_in_b[0]), halves(w_rest_b[0])])
    convw_slot = jnp.zeros((SHARDS, DEPTH * CONV_K * 768 // 128, 128), F32)
    convw_slot = lax.dynamic_update_index_in_dim(
        convw_slot, jnp.where(cc == 0, 1.0, 0.0) * ssm_conv_w.reshape(-1, 128), me, 0)
    convw_rows = allreduce_rows(convw_slot.reshape(-1, 128), "gather_conv_w")
    convw_all = convw_rows.reshape(SHARDS, DEPTH, CONV_K, 768).transpose(1, 2, 0, 3).reshape(DEPTH, CONV_K, CONV_C)
    g1_ssem, g1_rsem, g1_srcs, g1_lands, g1_token = gather_start(
        [w_in_b[1], w_rest_b[1]], [convw_rows, all0_rest], "gather_l1_start")

    o = REST_OFF

    def layer_weights(l, gathered_in, gathered_rest, g_pre):
        sh_in = [jnp.where(me == s, w_in_b[l], gathered_in[s]) for s in range(SHARDS)]
        sh_rest = [jnp.where(me == s, w_rest_b[l], gathered_rest[s]) for s in range(SHARDS)]
        rest = lambda k: jnp.concatenate([r[o[k]:o[k + 1]] for r in sh_rest], axis=0)
        return dict(
            wt=group_weights(jnp.concatenate(sh_in, axis=0)),
            w_a=rest(0), w_s=rest(1), w_m=rest(2), w_o=rest(3),
            g_pre=g_pre, g_post=norm_post[l][None], sinks=att_sinks[l],
            ln_g=sg_ln_g[l][None], ln_b=sg_ln_b[l][None], sg_w=sg_w[l],
            sg_bt=sg_b[l].T,
            conv_w=jnp.concatenate([convw_all[l], jnp.zeros((4, CONV_C), F32)], axis=0),
            conv_b=ssm_conv_b[l][None], dt_bias=_pad_lanes(ssm_dt_bias[l]), a_log=_pad_lanes(ssm_a_log[l]),
            d_skip=_pad_lanes(ssm_d[l]), norm_g=ssm_norm_g[l][None])

    bias = bias_table(rel_bias)
    layers = [layer_weights(0, [all0_in[:, s].reshape(3400, D) for s in range(SHARDS)],
                            [all0_rest[:, s].reshape(1280, D) for s in range(SHARDS)],
                            (norm_pre[0] + g1_token[0, 0])[None])]
    act, sv0 = layer_fwd(xs, layers[0], bias)
    land_in, land_rest = gather_wait(g1_ssem, g1_rsem, g1_srcs, g1_lands, act, "gather_l1_wait")
    layers.append(layer_weights(1, land_in, land_rest, norm_pre[1][None]))
    act, sv1 = layer_fwd(act, layers[1], bias)
    saved = [sv0, sv1]
    dy, loss_part = loss_head(act, loss_target[0])
    cvec = jnp.reshape(cc, (1,)).astype(jnp.int32)
    mvec = jnp.reshape(me, (1,)).astype(jnp.int32)

    def by_shard(g):
        gcw = g["conv_w"][0:CONV_K].reshape(CONV_K, SHARDS, 768).transpose(1, 0, 2).reshape(SHARDS, 3, 1024)
        rest = jnp.concatenate([
            g["w_a"].reshape(SHARDS, 256, D), g["w_s"].reshape(SHARDS, 256, D), g["w_o"].reshape(SHARDS, 256, D),
            g["w_m"].reshape(SHARDS, 512, D), jnp.pad(gcw, ((0, 0), (0, GR_ROWS - GR_CONV - 3), (0, 0)))], axis=1)
        return ungroup_grads(g["w_in"]).reshape(SHARDS, 3400, D), rest

    grads = [None] * DEPTH
    dy, grads[1] = layer_bwd(dy, layers[1], bias, saved[1])
    g1_in, g1_rest = by_shard(grads[1])
    g1_in = jnp.pad(g1_in, ((0, 0), (0, W_IN_ROWS - 3400), (0, 0)))
    x1_ssem, x1_rsem, x1_srcs, x1_lands, x1_token = gather_start(
        [g1_in.astype(BF16), g1_rest.astype(BF16)], [], "grads_l1_start", by_dest=True)
    wts0 = dict(layers[0], g_post=layers[0]["g_post"] + x1_token[0, 0])
    dy, grads[0] = layer_bwd(dy, wts0, bias, saved[0])
    grad_x = dy[None]
    r1_in, r1_rest = gather_wait(x1_ssem, x1_rsem, x1_srcs, x1_lands, dy, "grads_l1_wait", by_dest=True)
    p_in = grad_shard_sum(g1_in, r1_in, mvec, 384, "l1_sum_w_in")
    p_rest = grad_shard_sum(g1_rest, r1_rest, mvec, 512, "l1_sum_rest")
    pb_in, pb_rest = grad_sibling_share([p_in, p_rest], "l1_sibling_share")
    grad_rel_local = bias_grad(grads[0]["bias"] + grads[1]["bias"])

    g0_in, g0_rest = by_shard(grads[0])
    pad_to = lambda a, rows: jnp.pad(a, ((0, 0), (0, rows - a.shape[1]), (0, 0)))
    g0_in = jnp.stack([pad_to(g0_in[:, 0:W_IN_SPLIT], W_IN_HALF), pad_to(g0_in[:, W_IN_SPLIT:3400], W_IN_HALF)])
    g0_rest = jnp.stack([g0_rest[:, 0:GR_ROWS // 2], g0_rest[:, GR_ROWS // 2:GR_ROWS]])
    sb_in, sb_rest = grad_sibling_exchange([g0_in, g0_rest])
    t_in, t_in_b = grad_chip_sum(g0_in, sb_in, cvec, 608, "chip_sum_w_in")
    t_rest, t_rest_b = grad_chip_sum(g0_rest, sb_rest, cvec, 384, "chip_sum_rest")
    rb_in, rb_rest = grad_chip_exchange([t_in_b, t_rest_b])
    f_in = grad_shard_sum(t_in, rb_in, mvec, 608, "shard_sum_w_in")
    f_rest = grad_shard_sum(t_rest, rb_rest, mvec, 384, "shard_sum_rest")
    fb_in, fb_rest = grad_sibling_share([f_in, f_rest], "l0_sibling_share")

    red = small_allreduce(grads, grad_rel_local, loss_part)
    loss = red[LOSS_ROW, 0]

    res = adamw_small(red, (rel_bias, m_rel_bias, v_rel_bias), dict(
        norm_pre=(norm_pre, m_norm_pre, v_norm_pre), norm_post=(norm_post, m_norm_post, v_norm_post),
        att_sinks=(att_sinks, m_att_sinks, v_att_sinks), sg_ln_g=(sg_ln_g, m_sg_ln_g, v_sg_ln_g),
        sg_ln_b=(sg_ln_b, m_sg_ln_b, v_sg_ln_b), sg_w=(sg_w, m_sg_w, v_sg_w), sg_b=(sg_b, m_sg_b, v_sg_b),
        ssm_conv_b=(ssm_conv_b, m_ssm_conv_b, v_ssm_conv_b), ssm_dt_bias=(ssm_dt_bias, m_ssm_dt_bias, v_ssm_dt_bias),
        ssm_a_log=(ssm_a_log, m_ssm_a_log, v_ssm_a_log), ssm_d=(ssm_d, m_ssm_d, v_ssm_d),
        ssm_norm_g=(ssm_norm_g, m_ssm_norm_g, v_ssm_norm_g)))
    res["w_in"] = tuple(tr(a) for a in adamw_big(
        tr(w_in), tr(m_w_in), tr(v_w_in), (f_in, fb_in, W_IN_SPLIT // 200, 0, 0), (p_in, pb_in, 0), cvec, "adamw_w_in", 200))
    rest_upd = lambda w, m, v, name, n0, off0, off1: adamw_big(
        w, m, v, (f_rest, fb_rest, n0, off0, off0), (p_rest, pb_rest, off1), cvec, name, 256)
    res["w_br_att"] = rest_upd(w_br_att, m_w_br_att, v_w_br_att, "adamw_w_br_att", 1, 0, 0)
    res["w_br_sg"] = rest_upd(w_br_sg, m_w_br_sg, v_w_br_sg, "adamw_w_br_sg", 1, 1, 1)
    res["w_out"] = rest_upd(w_out, m_w_out, v_w_out, "adamw_w_out", 1, 2, 2)
    res["w_br_ssm"] = rest_upd(w_br_ssm, m_w_br_ssm, v_w_br_ssm, "adamw_w_br_ssm", 0, 0, 3)
    cw0 = jnp.where(cc == 1, f_rest, fb_rest)[GR_CONV - GR_ROWS // 2:GR_CONV - GR_ROWS // 2 + 3]
    cw1 = (p_rest + pb_rest)[GR_CONV:GR_CONV + 3]
    g_conv_w = jnp.stack([cw0.reshape(CONV_K, 768), cw1.reshape(CONV_K, 768)])
    res["ssm_conv_w"] = (g_conv_w,) + tuple(adamw_plain(ssm_conv_w, g_conv_w, m_ssm_conv_w, v_ssm_conv_w, "adamw_conv_w"))

    order = ["w_in", "norm_pre", "norm_post", "rel_bias", "att_sinks", "sg_ln_g", "sg_ln_b", "sg_w", "sg_b",
             "ssm_conv_w", "ssm_conv_b", "ssm_dt_bias", "ssm_a_log", "ssm_d", "ssm_norm_g",
             "w_br_att", "w_br_sg", "w_br_ssm", "w_out"]
    return (loss, grad_x, *[res[n][0] for n in order], *[res[n][1] for n in order],
            *[res[n][2] for n in order], *[res[n][3] for n in order])
```

```python
import functools
import math

import numpy as np
import jax
import jax.numpy as jnp
from jax import lax
from jax.experimental import pallas as pl
from jax.experimental.pallas import tpu as pltpu

F32 = jnp.float32
BF16 = jnp.bfloat16
MESH = pl.DeviceIdType.MESH

D = 1024
DEPTH = 2
EPS = 1e-6
L = 128
HEADS = 16
KV = 2
DH = 64
SSM_W = 2048
SSM_H = 32
SSM_P = 64
SSM_G = 4
SSM_N = 128
CONV_K = 4
CONV_C = 3072
NEG = -1e30
IN_COLS = 13600

GROUPS = (("gate", 3072, 1536), ("sgu", 3072, 1536), ("att", 2304, 2304), ("ssd", 5376, 1792))
W_IN_ROWS = 3456

ADAM_LR = 0.001
ADAM_B1 = 0.9
ADAM_B2 = 0.999
ADAM_EPS = 1e-08
ADAM_WD = 0.01
ADAM_STEP = 10

VMEM_LIMIT = 56 * 1024 * 1024

SHARDS = 4


def _dot(a, b):
    return jnp.dot(a, b, preferred_element_type=F32)


def _dot_nt(a, b):
    return lax.dot_general(a, b, (((1,), (1,)), ((), ())), preferred_element_type=F32)


def _dot_tn(a_f32, b):
    return jnp.dot(a_f32.T.astype(BF16), b, preferred_element_type=F32)


def _dot_hi(a, b):
    return jnp.dot(a, b, preferred_element_type=F32, precision=lax.Precision.HIGHEST)


def _pieces(x, n):
    out = []
    for _ in range(n - 1):
        p = x.astype(BF16)
        out.append(p)
        x = x - p.astype(F32)
    out.append(x.astype(BF16))
    return out


def _dot_sel(a, sel, n):
    sel = sel.astype(BF16)
    acc = None
    for p in _pieces(a, n):
        t = _dot(p, sel)
        acc = t if acc is None else acc + t
    return acc


def _sel_dot(sel, b, n):
    sel = sel.astype(BF16)
    acc = None
    for p in _pieces(b, n):
        t = _dot(sel, p)
        acc = t if acc is None else acc + t
    return acc


def _sigmoid(x):
    return 1.0 / (1.0 + jnp.exp(-x))


def _softplus(x):
    return jnp.maximum(x, 0.0) + jnp.log(1.0 + jnp.exp(-jnp.abs(x)))


def _params(sem=None, vmem=VMEM_LIMIT):
    kw = dict(vmem_limit_bytes=vmem)
    if sem is not None:
        kw["dimension_semantics"] = sem
    return pltpu.CompilerParams(**kw)


def _full(shape):
    nd = len(shape)
    return pl.BlockSpec(shape, lambda *_: (0,) * nd)


def group_weights(wt):
    return dict(
        gate=wt[10528:13600],
        sgu=wt[2304:5376],
        att=jnp.concatenate([wt[0:1024], wt[1280:2304], wt[1024:1280]], axis=0),
        ssd=jnp.concatenate([wt[5376:10496], wt[10496:10528], jnp.zeros((224, D), wt.dtype)], axis=0))


def ungroup_grads(g):
    a, s = g["att"], g["ssd"]
    return jnp.concatenate([a[0:1024], a[2048:2304], a[1024:2048], g["sgu"], s[0:5152], g["gate"]], axis=0)


def _bucket_table():
    qi = np.arange(L)[:, None]
    kj = np.arange(2 * L)[None, :]
    dist = np.maximum(qi + L - kj, 0)
    dist_f = np.maximum(dist, 1).astype(np.float32)
    large = 16 + (np.log(dist_f / np.float32(16)) / np.float32(math.log(128 / 16)) * np.float32(16)).astype(np.int32)
    large = np.minimum(large, 31)
    return np.where(dist < 16, dist, large).astype(np.int32)


def bias_table(rel_bias):
    buckets = jnp.asarray(_bucket_table().reshape(1, L * 2 * L))

    def body(rb_ref, bk_ref, out_ref):
        onehot = (lax.broadcasted_iota(jnp.int32, (32, L * 2 * L), 0) == bk_ref[...]).astype(F32)
        out_ref[...] = lax.dot_general(rb_ref[...], onehot, (((0,), (0,)), ((), ())),
                                       preferred_element_type=F32, precision=lax.Precision.HIGHEST)

    out = pl.pallas_call(
        body, name="bias_table",
        out_shape=jax.ShapeDtypeStruct((HEADS, L * 2 * L), F32),
        compiler_params=_params(),
    )(rel_bias, buckets)
    return out.reshape(HEADS, L, 2 * L)


def bias_grad(dbias):
    buckets = jnp.asarray(_bucket_table().reshape(1, L * 2 * L))

    def body(db_ref, bk_ref, out_ref):
        onehot = (lax.broadcasted_iota(jnp.int32, (32, L * 2 * L), 0) == bk_ref[...]).astype(F32)
        out_ref[...] = lax.dot_general(onehot, db_ref[...], (((1,), (1,)), ((), ())),
                                       preferred_element_type=F32, precision=lax.Precision.HIGHEST)

    return pl.pallas_call(
        body, name="bias_grad",
        out_shape=jax.ShapeDtypeStruct((32, HEADS), F32),
        compiler_params=_params(),
    )(dbias.reshape(HEADS, L * 2 * L), buckets)


def _row_tile(S):
    return 1024 if S % 1024 == 0 else 512


def inproj_first(x, g_pre, wt, tn, name):
    S, W = x.shape[0], wt.shape[0]
    tm = _row_tile(S)

    def body(x_ref, g_ref, w_ref, o_ref, h_ref):
        @pl.when(pl.program_id(1) == 0)
        def _():
            xv = x_ref[...]
            r = lax.rsqrt(jnp.mean(xv * xv, axis=-1, keepdims=True) + EPS)
            h_ref[...] = (xv * r * g_ref[...]).astype(BF16)
        o_ref[...] = _dot_nt(h_ref[...], w_ref[...]).astype(BF16)

    return pl.pallas_call(
        body, name=name, grid=(S // tm, W // tn),
        in_specs=[pl.BlockSpec((tm, D), lambda i, j: (i, 0)), _full((1, D)),
                  pl.BlockSpec((tn, D), lambda i, j: (j, 0))],
        out_specs=[pl.BlockSpec((tm, tn), lambda i, j: (i, j)), pl.BlockSpec((tm, D), lambda i, j: (i, 0))],
        out_shape=[jax.ShapeDtypeStruct((S, W), BF16), jax.ShapeDtypeStruct((S, D), BF16)],
        compiler_params=_params(("arbitrary", "arbitrary")),
    )(x, g_pre, wt)


def inproj_group(h, wt, tn, name, dtype):
    S, W = h.shape[0], wt.shape[0]
    tm = _row_tile(S)

    def body(h_ref, w_ref, o_ref):
        o_ref[...] = _dot_nt(h_ref[...], w_ref[...]).astype(dtype)

    return pl.pallas_call(
        body, name=name, grid=(S // tm, W // tn),
        in_specs=[pl.BlockSpec((tm, D), lambda i, j: (i, 0)), pl.BlockSpec((tn, D), lambda i, j: (j, 0))],
        out_specs=pl.BlockSpec((tm, tn), lambda i, j: (i, j)),
        out_shape=jax.ShapeDtypeStruct((S, W), dtype),
        compiler_params=_params(("arbitrary", "arbitrary")),
    )(h, wt)


def dh_group(dp, wt, acc, tk, name):
    S, W = dp.shape
    tm = _row_tile(S)

    def body(*refs):
        dp_ref, w_ref, o_ref = refs[0], refs[1], refs[-1]
        first = pl.program_id(1) == 0
        if acc is None:
            @pl.when(first)
            def _():
                o_ref[...] = jnp.zeros_like(o_ref)
        else:
            @pl.when(first)
            def _():
                o_ref[...] = refs[2][...]
        o_ref[...] += _dot(dp_ref[...], w_ref[...])

    row = pl.BlockSpec((tm, D), lambda i, k: (i, 0))
    return pl.pallas_call(
        body, name=name, grid=(S // tm, W // tk),
        in_specs=[pl.BlockSpec((tm, tk), lambda i, k: (i, k)), pl.BlockSpec((tk, D), lambda i, k: (k, 0))]
        + ([] if acc is None else [row]),
        out_specs=row, out_shape=jax.ShapeDtypeStruct((S, D), F32),
        input_output_aliases={} if acc is None else {2: 0},
        compiler_params=_params(("arbitrary", "arbitrary")),
    )(*((dp, wt) if acc is None else (dp, wt, acc)))


def dh_last(dp, wt, acc_in, x, g_pre, dy, tk, name):
    S, W = dp.shape
    tm = 512
    nk = W // tk

    def body(dp_ref, w_ref, a_ref, x_ref, g_ref, dy_ref, dx_ref, dg_ref, acc):
        i, k = pl.program_id(0), pl.program_id(1)

        @pl.when(k == 0)
        def _():
            acc[...] = a_ref[...]

        acc[...] += _dot(dp_ref[...], w_ref[...])

        @pl.when((k == nk - 1) & (i == 0))
        def _():
            dg_ref[...] = jnp.zeros_like(dg_ref)

        @pl.when(k == nk - 1)
        def _():
            xv = x_ref[...]
            dh = acc[...]
            g = g_ref[...]
            r = lax.rsqrt(jnp.mean(xv * xv, axis=-1, keepdims=True) + EPS)
            dhg = dh * g
            dx_ref[...] = dy_ref[...] + r * dhg - xv * (r * r * r) * jnp.mean(dhg * xv, axis=-1, keepdims=True)
            dg_ref[...] += jnp.sum(dh * xv * r, axis=0, keepdims=True)

    row = pl.BlockSpec((tm, D), lambda i, k: (i, 0))
    return pl.pallas_call(
        body, name=name, grid=(S // tm, nk),
        in_specs=[pl.BlockSpec((tm, tk), lambda i, k: (i, k)), pl.BlockSpec((tk, D), lambda i, k: (k, 0)),
                  row, row, _full((1, D)), row],
        out_specs=[row, _full((1, D))],
        out_shape=[jax.ShapeDtypeStruct((S, D), F32), jax.ShapeDtypeStruct((1, D), F32)],
        scratch_shapes=[pltpu.VMEM((tm, D), F32)],
        compiler_params=_params(("arbitrary", "arbitrary")),
    )(dp, wt, acc_in, x, g_pre, dy)


def dw_group(dp, h, tn, name, ts=512):
    S, W = dp.shape

    def body(dp_ref, h_ref, o_ref):
        @pl.when(pl.program_id(1) == 0)
        def _():
            o_ref[...] = jnp.zeros_like(o_ref)
        o_ref[...] += _dot_tn(dp_ref[...].astype(F32), h_ref[...])

    return pl.pallas_call(
        body, name=name, grid=(W // tn, S // ts),
        in_specs=[pl.BlockSpec((ts, tn), lambda j, s: (s, j)), pl.BlockSpec((ts, D), lambda j, s: (s, 0))],
        out_specs=pl.BlockSpec((tn, D), lambda j, s: (j, 0)),
        out_shape=jax.ShapeDtypeStruct((W, D), F32),
        compiler_params=_params(("arbitrary", "arbitrary")),
    )(dp, h)


def matmul_tn(a, b, name, tn=512, ts=512):
    S, K = a.shape
    N = b.shape[1]
    ns = S // ts

    def body(a_ref, b_ref, o_ref):
        @pl.when(pl.program_id(1) == 0)
        def _():
            o_ref[...] = jnp.zeros_like(o_ref)
        o_ref[...] += _dot_tn(a_ref[...].astype(F32), b_ref[...])

    return pl.pallas_call(
        body, name=name, grid=(N // tn, ns),
        in_specs=[pl.BlockSpec((ts, K), lambda j, s: (s, 0)), pl.BlockSpec((ts, tn), lambda j, s: (s, j))],
        out_specs=pl.BlockSpec((K, tn), lambda j, s: (0, j)),
        out_shape=jax.ShapeDtypeStruct((K, N), F32),
        compiler_params=_params(("arbitrary", "arbitrary")),
    )(a, b)


def _att_mask(n):
    qi = lax.broadcasted_iota(jnp.int32, (L, 2 * L), 0)
    kj = lax.broadcasted_iota(jnp.int32, (L, 2 * L), 1)
    dist = qi + L - kj
    return (dist >= 0) & (dist < L) & ((kj >= L) | (n > 0))


def _att_in_specs(nb):
    last = nb - 1
    cur = lambda n: jnp.minimum(n, last)
    prev = lambda n: jnp.maximum(jnp.minimum(n, last) - 1, 0)
    return [
        pl.BlockSpec((L, 1024), lambda n: (cur(n), 0)),
        pl.BlockSpec((L, 128), lambda n: (prev(n), 16)),
        pl.BlockSpec((L, 128), lambda n: (cur(n), 16)),
        pl.BlockSpec((L, 128), lambda n: (prev(n), 17)),
        pl.BlockSpec((L, 128), lambda n: (cur(n), 17)),
        pl.BlockSpec((L, 1024), lambda n: (cur(n), 1)),
        _full((HEADS, L, 2 * L)),
        pl.BlockSpec(memory_space=pltpu.SMEM),
    ]


GH = HEADS // KV
GB = 8


def _att_mask_rows(n, nh):
    qi = lax.broadcasted_iota(jnp.int32, (nh * L, 2 * L), 0) & (L - 1)
    kj = lax.broadcasted_iota(jnp.int32, (nh * L, 2 * L), 1)
    dist = qi + L - kj
    return (dist >= 0) & (dist < L) & ((kj >= L) | (n > 0))


def _stack_heads(ref, h0, nh, scr):
    for g in range(nh):
        scr[(h0 + g) * L:(h0 + g + 1) * L, :] = ref[:, (h0 + g) * DH:(h0 + g + 1) * DH].astype(F32)
    return scr[h0 * L:(h0 + nh) * L, :]


def _unstack_heads(val, h0, nh, ref):
    for g in range(nh):
        ref[:, (h0 + g) * DH:(h0 + g + 1) * DH] = val[g * L:(g + 1) * L, :]


def _sink_rows(s_ref, h0, nh):
    return jnp.concatenate([jnp.full((L, 1), s_ref[h0 + g], F32) for g in range(nh)], axis=0)


def _att_probs(qh, kk, bias_h, mask, sk):
    logits = _dot_nt(qh, kk) + bias_h
    logits = jnp.where(mask, logits, NEG)
    m = jnp.maximum(jnp.max(logits, axis=-1, keepdims=True), sk)
    p = jnp.exp(logits - m)
    es = jnp.exp(sk - m)
    den = jnp.sum(p, axis=-1, keepdims=True) + es
    return p / den, es / den


def att_fwd(proj, bias, sinks):
    S = proj.shape[0]
    nb = S // L

    def body(q_ref, kp_ref, kc_ref, vp_ref, vc_ref, z_ref, bias_ref, s_ref, y_ref, o_scr):
        mask = _att_mask(pl.program_id(0))
        for kv in range(KV):
            sl = slice(kv * DH, (kv + 1) * DH)
            kk = jnp.concatenate([kp_ref[:, sl], kc_ref[:, sl]], axis=0).astype(BF16)
            vv = jnp.concatenate([vp_ref[:, sl], vc_ref[:, sl]], axis=0).astype(BF16)
            for g in range(GH):
                h = kv * GH + g
                hs = slice(h * DH, (h + 1) * DH)
                qh = (q_ref[:, hs] * 0.125).astype(BF16)
                P, _ = _att_probs(qh, kk, bias_ref[h], mask, s_ref[h])
                o_scr[:, hs] = _dot(P.astype(BF16), vv)
        z = z_ref[...].astype(F32)
        y_ref[...] = (o_scr[...] * (z * _sigmoid(z))).astype(BF16)

    return pl.pallas_call(
        body, name="att_fwd", grid=(nb,),
        in_specs=_att_in_specs(nb),
        out_specs=pl.BlockSpec((L, 1024), lambda n: (n, 0)),
        out_shape=jax.ShapeDtypeStruct((S, 1024), BF16),
        scratch_shapes=[pltpu.VMEM((L, 1024), F32)],
        compiler_params=_params(("arbitrary",)),
    )(proj, proj, proj, proj, proj, proj, bias, sinks)


def att_bwd(dy, proj, bias, sinks):
    S = proj.shape[0]
    nb = S // L
    last = nb - 1

    def body(dy_ref, q_ref, kp_ref, kc_ref, vp_ref, vc_ref, z_ref, bias_ref, s_ref,
             dout_ref, dbias_ref, dsink_ref, carry, band, dq_scr, dz_scr, qs_scr, zs_scr, dys_scr):
        n = pl.program_id(0)

        @pl.when(n == 0)
        def _():
            carry[...] = jnp.zeros_like(carry)
            dq_scr[...] = jnp.zeros_like(dq_scr)
            dz_scr[...] = jnp.zeros_like(dz_scr)
            dbias_ref[...] = jnp.zeros_like(dbias_ref)
            dsink_ref[...] = jnp.zeros_like(dsink_ref)

        dout_ref[:, 0:1024] = dq_scr[...].astype(BF16)
        dout_ref[:, 1024:2048] = dz_scr[...].astype(BF16)
        band[...] = jnp.zeros_like(band)

        @pl.when(n < nb)
        def _():
            mask = _att_mask_rows(n, GB)
            lane = lax.broadcasted_iota(jnp.int32, (1, 128), 1)
            dsink = jnp.zeros((1, 128), F32)
            for kv in range(KV):
                sl = slice(kv * DH, (kv + 1) * DH)
                kk = jnp.concatenate([kp_ref[:, sl], kc_ref[:, sl]], axis=0).astype(BF16)
                vv = jnp.concatenate([vp_ref[:, sl], vc_ref[:, sl]], axis=0).astype(BF16)
                dk_acc = jnp.zeros((2 * L, DH), F32)
                dv_acc = jnp.zeros((2 * L, DH), F32)
                for h0 in range(kv * GH, (kv + 1) * GH, GB):
                    qs = (_stack_heads(q_ref, h0, GB, qs_scr) * 0.125).astype(BF16)
                    bias_g = bias_ref[h0:h0 + GB].reshape(GB * L, 2 * L)
                    P, psink = _att_probs(qs, kk, bias_g, mask, _sink_rows(s_ref, h0, GB))
                    zs = _stack_heads(z_ref, h0, GB, zs_scr)
                    dys = _stack_heads(dy_ref, h0, GB, dys_scr)
                    sg = _sigmoid(zs)
                    O = _dot(P.astype(BF16), vv)
                    _unstack_heads(dys * O * (sg * (1.0 + zs * (1.0 - sg))), h0, GB, dz_scr)
                    dOb = (dys * (zs * sg)).astype(BF16)
                    dP = _dot_nt(dOb, vv)
                    delta = jnp.sum(P * dP, axis=-1, keepdims=True)
                    dS = P * (dP - delta)
                    sd = psink * delta
                    for g in range(GB):
                        dsink = dsink + jnp.where(lane == h0 + g, -jnp.sum(sd[g * L:(g + 1) * L, :]), 0.0)
                    _unstack_heads(_dot(dS.astype(BF16), kk) * 0.125, h0, GB, dq_scr)
                    dbias_ref[h0:h0 + GB] += dS.reshape(GB, L, 2 * L)
                    dk_acc = dk_acc + _dot_tn(dS, qs)
                    dv_acc = dv_acc + _dot_tn(P, dOb)
                band[:, sl] = dk_acc
                band[:, 128 + kv * DH:128 + (kv + 1) * DH] = dv_acc
            dsink_ref[...] += dsink

        out = carry[...] + band[0:L, :]
        dout_ref[:, 2048:2304] = out.astype(BF16)
        carry[...] = band[L:2 * L, :]

    cur = lambda n: jnp.minimum(n, last)
    lag = lambda n: jnp.maximum(n - 1, 0)
    return pl.pallas_call(
        body, name="att_bwd", grid=(nb + 1,),
        in_specs=[pl.BlockSpec((L, 1024), lambda n: (cur(n), 0))] + _att_in_specs(nb),
        out_specs=[pl.BlockSpec((L, 2304), lambda n: (lag(n), 0)), _full((HEADS, L, 2 * L)), _full((1, 128))],
        out_shape=[jax.ShapeDtypeStruct((S, 2304), BF16),
                   jax.ShapeDtypeStruct((HEADS, L, 2 * L), F32), jax.ShapeDtypeStruct((1, 128), F32)],
        scratch_shapes=[pltpu.VMEM((L, 256), F32), pltpu.VMEM((2 * L, 256), F32),
                        pltpu.VMEM((L, 1024), F32), pltpu.VMEM((L, 1024), F32)]
        + [pltpu.VMEM((HEADS * L, DH), F32)] * 3,
        compiler_params=_params(("arbitrary",)),
    )(dy, proj, proj, proj, proj, proj, proj, bias, sinks)


def _sgu_in_specs():
    return [
        pl.BlockSpec((L, 1024), lambda c: (c, 0)),
        pl.BlockSpec((L, 1024), lambda c: (c, 1)),
        pl.BlockSpec((L, 1024), lambda c: (c, 2)),
        _full((1, 1024)), _full((1, 1024)), _full((8, L, L)), _full((L, 8)),
    ]


def _sgu_norm(v, lg, lb):
    mu = jnp.mean(v, axis=-1, keepdims=True)
    vc = v - mu
    rstd = lax.rsqrt(jnp.mean(vc * vc, axis=-1, keepdims=True) + EPS)
    xhat = vc * rstd
    return xhat * lg + lb, xhat, rstd


def _tril():
    return lax.broadcasted_iota(jnp.int32, (L, L), 0) >= lax.broadcasted_iota(jnp.int32, (L, L), 1)


def sgu_fwd(proj, ln_g, ln_b, w, b_t):
    S = proj.shape[0]

    def body(u_ref, v_ref, z_ref, lg_ref, lb_ref, w_ref, bt_ref, y_ref):
        vn, _, _ = _sgu_norm(v_ref[...].astype(F32), lg_ref[...], lb_ref[...])
        tri = _tril()
        parts = []
        for g in range(8):
            wg = jnp.where(tri, w_ref[g], 0.0).astype(BF16)
            parts.append(_dot(wg, vn[:, g * 128:(g + 1) * 128].astype(BF16)) + bt_ref[:, g:g + 1])
        mixed = jnp.concatenate(parts, axis=1)
        z = z_ref[...].astype(F32)
        y_ref[...] = (u_ref[...].astype(F32) * mixed * (z * _sigmoid(z))).astype(BF16)

    return pl.pallas_call(
        body, name="sgu_fwd", grid=(S // L,),
        in_specs=_sgu_in_specs(),
        out_specs=pl.BlockSpec((L, 1024), lambda c: (c, 0)),
        out_shape=jax.ShapeDtypeStruct((S, 1024), BF16),
        compiler_params=_params(("arbitrary",)),
    )(proj, proj, proj, ln_g, ln_b, w, b_t)


def sgu_bwd(dy, proj, ln_g, ln_b, w, b_t):
    S = proj.shape[0]

    def body(dy_ref, u_ref, v_ref, z_ref, lg_ref, lb_ref, w_ref, bt_ref,
             dout_ref, dw_ref, dbt_ref, dlg_ref, dlb_ref):
        @pl.when(pl.program_id(0) == 0)
        def _():
            dw_ref[...] = jnp.zeros_like(dw_ref)
            dbt_ref[...] = jnp.zeros_like(dbt_ref)
            dlg_ref[...] = jnp.zeros_like(dlg_ref)
            dlb_ref[...] = jnp.zeros_like(dlb_ref)

        lg = lg_ref[...]
        vn, xhat, rstd = _sgu_norm(v_ref[...].astype(F32), lg, lb_ref[...])
        tri = _tril()
        lane = lax.broadcasted_iota(jnp.int32, (L, 128), 1)
        wgs, parts = [], []
        for g in range(8):
            wg = jnp.where(tri, w_ref[g], 0.0)
            wgs.append(wg)
            parts.append(_dot(wg.astype(BF16), vn[:, g * 128:(g + 1) * 128].astype(BF16)) + bt_ref[:, g:g + 1])
        mixed = jnp.concatenate(parts, axis=1)
        z = z_ref[...].astype(F32)
        sg = _sigmoid(z)
        silu = z * sg
        dy_v = dy_ref[...]
        u = u_ref[...].astype(F32)
        dout_ref[:, 0:1024] = (dy_v * mixed * silu).astype(BF16)
        dout_ref[:, 2048:3072] = (dy_v * u * mixed * (sg * (1.0 + z * (1.0 - sg)))).astype(BF16)
        dmixed = dy_v * u * silu
        dbt = jnp.zeros((L, 128), F32)
        dvn_parts = []
        for g in range(8):
            dm = dmixed[:, g * 128:(g + 1) * 128]
            dmb = dm.astype(BF16)
            dbt = dbt + jnp.where(lane == g, jnp.sum(dm, axis=1, keepdims=True), 0.0)
            dw_ref[g] += jnp.where(tri, _dot_nt(dmb, vn[:, g * 128:(g + 1) * 128].astype(BF16)), 0.0)
            dvn_parts.append(_dot_tn(wgs[g], dmb))
        dbt_ref[...] += dbt
        dvn = jnp.concatenate(dvn_parts, axis=1)
        dlg_ref[...] += jnp.sum(dvn * xhat, axis=0, keepdims=True)
        dlb_ref[...] += jnp.sum(dvn, axis=0, keepdims=True)
        dxh = dvn * lg
        dv = rstd * (dxh - jnp.mean(dxh, axis=-1, keepdims=True)
                     - xhat * jnp.mean(dxh * xhat, axis=-1, keepdims=True))
        dout_ref[:, 1024:2048] = dv.astype(BF16)

    return pl.pallas_call(
        body, name="sgu_bwd", grid=(S // L,),
        in_specs=[pl.BlockSpec((L, 1024), lambda c: (c, 0))] + _sgu_in_specs(),
        out_specs=[pl.BlockSpec((L, 3072), lambda c: (c, 0)), _full((8, L, L)), _full((L, 128)),
                   _full((1, 1024)), _full((1, 1024))],
        out_shape=[jax.ShapeDtypeStruct((S, 3072), BF16), jax.ShapeDtypeStruct((8, L, L), F32),
                   jax.ShapeDtypeStruct((L, 128), F32), jax.ShapeDtypeStruct((1, 1024), F32),
                   jax.ShapeDtypeStruct((1, 1024), F32)],
        compiler_params=_params(("arbitrary",)),
    )(dy, proj, proj, proj, ln_g, ln_b, w, b_t)


def _expand_matrix():
    r = lax.broadcasted_iota(jnp.int32, (128, SSM_W), 0)
    c = lax.broadcasted_iota(jnp.int32, (128, SSM_W), 1)
    return (c // SSM_P) == r


def _expand_matrix_t():
    r = lax.broadcasted_iota(jnp.int32, (SSM_W, 128), 0)
    c = lax.broadcasted_iota(jnp.int32, (SSM_W, 128), 1)
    return (r // SSM_P) == c


def _rows_from(ref, start):
    C = ref.shape[1]
    tiles = ref[...].reshape(17, 8, C)
    q, s = divmod(start, 8)
    if s == 0:
        return tiles[q:q + 16].reshape(L, C)
    rolled = pltpu.roll(tiles, 8 - s, axis=1)
    sub = lax.broadcasted_iota(jnp.int32, (16, 8, C), 1)
    return jnp.where(sub < 8 - s, rolled[q:q + 16], rolled[q + 1:q + 17]).reshape(L, C)


def _ssd_common(ext_ref, cw_ref, cb_ref, dt_raw, dtb, alog):
    taps = [_rows_from(ext_ref, 5 + k) for k in range(CONV_K)]
    pre = cb_ref[...]
    for k in range(CONV_K):
        pre = pre + cw_ref[k:k + 1, :] * taps[k]
    sg_pre = _sigmoid(pre)
    xc = pre * sg_pre
    dt = _softplus(dt_raw + dtb)
    a = -jnp.exp(alog)
    adt = dt * a
    acs = _sel_dot(_tril(), adt, 3)
    return pre, sg_pre, xc, dt, a, acs, taps


def _ssd_in_specs(rev, nc):
    cidx = (lambda c: nc - 1 - c) if rev else (lambda c: c)
    return [
        pl.BlockSpec((L, 2048), lambda c: (cidx(c), 0)),
        pl.BlockSpec((L, 1024), lambda c: (cidx(c), 2)),
        pl.BlockSpec((L, 1024), lambda c: (cidx(c), 3)),
        pl.BlockSpec((L, 1024), lambda c: (cidx(c), 4)),
        pl.BlockSpec((L, 128), lambda c: (cidx(c), 40)),
        _full((8, CONV_C)), _full((1, CONV_C)), _full((1, 128)), _full((1, 128)), _full((1, 128)),
        _full((1, SSM_W)),
    ]


def ssd_fwd(proj, conv_w, conv_b, dt_bias, a_log, d_skip, norm_g):
    S = proj.shape[0]
    nc = S // L

    def body(z_ref, xa_ref, xb_ref, xc_ref, dt_ref, cw_ref, cb_ref, dtb_ref, alog_ref, dsk_ref, ng_ref,
             y_ref, hs_ref, H, ext, ysc):
        @pl.when(pl.program_id(0) == 0)
        def _():
            H[...] = jnp.zeros_like(H)
            ext[0:8, :] = jnp.zeros((8, CONV_C), F32)

        for k, ref in enumerate((xa_ref, xb_ref, xc_ref)):
            ext[8:8 + L, k * 1024:(k + 1) * 1024] = ref[...].astype(F32)
        pre, sg_pre, xc, dt, a, acs, _ = _ssd_common(ext, cw_ref, cb_ref, dt_ref[...].astype(F32), dtb_ref[...],
                                                     alog_ref[...])
        for k, ref in enumerate((xa_ref, xb_ref, xc_ref)):
            ext[0:8, k * 1024:(k + 1) * 1024] = ref[L - 8:L, :].astype(F32)
        xs = xc[:, 0:SSM_W]
        acs_t = acs.T
        ex = _expand_matrix()
        dt_x = _dot_sel(dt, ex, 2)
        xdt = xs * dt_x
        eacs_x = _dot_sel(jnp.exp(acs), ex, 2)
        xw = xdt * _dot_sel(jnp.exp(acs[L - 1:L, :] - acs), ex, 2)
        cd_row = jnp.exp(acs[L - 1:L, :])
        hs_ref[0] = H[...]
        tri = _tril()
        for g in range(SSM_G):
            gs = slice(g * 512, (g + 1) * 512)
            bg = xc[:, SSM_W + g * SSM_N:SSM_W + (g + 1) * SSM_N].astype(BF16)
            cg = xc[:, SSM_W + 512 + g * SSM_N:SSM_W + 512 + (g + 1) * SSM_N].astype(BF16)
            G = _dot_nt(cg, bg)
            yoff = _dot_nt(cg, H[gs, :].astype(BF16)) * eacs_x[:, gs]
            Sg = _dot_tn(xw[:, gs], bg)
            for j in range(8):
                hh = g * 8 + j
                hs = slice(hh * SSM_P, (hh + 1) * SSM_P)
                seg = acs[:, hh:hh + 1] - acs_t[hh:hh + 1, :]
                dk = jnp.where(tri, jnp.exp(jnp.minimum(seg, 0.0)), 0.0)
                yd = _dot((G * dk).astype(BF16), xdt[:, hs].astype(BF16))
                ysc[:, hs] = yd + yoff[:, j * SSM_P:(j + 1) * SSM_P]
                H[hs, :] = H[hs, :] * cd_row[:, hh:hh + 1] + Sg[j * SSM_P:(j + 1) * SSM_P, :]
        d_x = _dot_sel(jnp.broadcast_to(dsk_ref[...], (8, 128)), ex, 3)[0:1, :]
        Y = ysc[...] + d_x * xs
        z = z_ref[...].astype(F32)
        yz = Y * (z * _sigmoid(z))
        ng = ng_ref[...]
        for g in range(SSM_G):
            gs = slice(g * 512, (g + 1) * 512)
            t = yz[:, gs]
            rstd = lax.rsqrt(jnp.mean(t * t, axis=-1, keepdims=True) + EPS)
            y_ref[:, gs] = (t * rstd * ng[:, gs]).astype(BF16)

    return pl.pallas_call(
        body, name="ssd_fwd", grid=(nc,),
        in_specs=_ssd_in_specs(False, nc),
        out_specs=[pl.BlockSpec((L, SSM_W), lambda c: (c, 0)), pl.BlockSpec((1, SSM_W, SSM_N), lambda c: (c, 0, 0))],
        out_shape=[jax.ShapeDtypeStruct((S, SSM_W), BF16), jax.ShapeDtypeStruct((nc, SSM_W, SSM_N), F32)],
        scratch_shapes=[pltpu.VMEM((SSM_W, SSM_N), F32), pltpu.VMEM((8 + L, CONV_C), F32),
                        pltpu.VMEM((L, SSM_W), F32)],
        compiler_params=_params(("arbitrary",)),
    )(proj, proj, proj, proj, proj, conv_w, conv_b, dt_bias, a_log, d_skip, norm_g)


def ssd_bwd(dy, proj, hstates, conv_w, conv_b, dt_bias, a_log, d_skip, norm_g):
    S = proj.shape[0]
    nc = S // L
    cidx = lambda c: nc - 1 - c

    def body(dy_ref, z_ref, xa_ref, xb_ref, xc_ref, dt_ref, cw_ref, cb_ref, dtb_ref, alog_ref, dsk_ref, ng_ref,
             pa_ref, pb_ref, pc_ref, hp_ref,
             dout_ref, dcw_ref, dcb_ref, ddtb_ref, dalog_ref, ddsk_ref, dng_ref,
             dH, ext, dext, ysc, yoffsc, dxdt, dxc, tsc):
        step = pl.program_id(0)
        c = nc - 1 - step

        @pl.when(step == 0)
        def _():
            dH[...] = jnp.zeros_like(dH)
            dext[L:L + 8, :] = jnp.zeros((8, CONV_C), F32)
            for r in (dcw_ref, dcb_ref, ddtb_ref, dalog_ref, ddsk_ref, dng_ref):
                r[...] = jnp.zeros_like(r)

        for k, (ref, prev) in enumerate(((xa_ref, pa_ref), (xb_ref, pb_ref), (xc_ref, pc_ref))):
            ext[0:8, k * 1024:(k + 1) * 1024] = jnp.where(c > 0, prev[8:16, :].astype(F32), 0.0)
            ext[8:8 + L, k * 1024:(k + 1) * 1024] = ref[...].astype(F32)
        dtb = dtb_ref[...]
        dt_raw = dt_ref[...].astype(F32)
        pre, sg_pre, xc, dt, a, acs, taps = _ssd_common(ext, cw_ref, cb_ref, dt_raw, dtb, alog_ref[...])
        xs = xc[:, 0:SSM_W]
        acs_t = acs.T
        ex = _expand_matrix()
        dt_x = _dot_sel(dt, ex, 2)
        xdt = xs * dt_x
        eacs_x = _dot_sel(jnp.exp(acs), ex, 2)
        dte_x = _dot_sel(jnp.exp(acs[L - 1:L, :] - acs), ex, 2)
        xw = xdt * dte_x
        cd_row = jnp.exp(acs[L - 1:L, :])
        tri = _tril()

        Gs, Cs, Bs = [], [], []
        for g in range(SSM_G):
            gs = slice(g * 512, (g + 1) * 512)
            bg = xc[:, SSM_W + g * SSM_N:SSM_W + (g + 1) * SSM_N].astype(BF16)
            cg = xc[:, SSM_W + 512 + g * SSM_N:SSM_W + 512 + (g + 1) * SSM_N].astype(BF16)
            G = _dot_nt(cg, bg)
            Gs.append(G), Cs.append(cg), Bs.append(bg)
            yoffsc[:, gs] = _dot_nt(cg, hp_ref[0, gs, :].astype(BF16)) * eacs_x[:, gs]
            for j in range(8):
                hh = g * 8 + j
                hs = slice(hh * SSM_P, (hh + 1) * SSM_P)
                seg = acs[:, hh:hh + 1] - acs_t[hh:hh + 1, :]
                dk = jnp.where(tri, jnp.exp(jnp.minimum(seg, 0.0)), 0.0)
                ysc[:, hs] = _dot((G * dk).astype(BF16), xdt[:, hs].astype(BF16))
        d_x = _dot_sel(jnp.broadcast_to(dsk_ref[...], (8, 128)), ex, 3)[0:1, :]
        yoff = yoffsc[...]
        Y = ysc[...] + yoff + d_x * xs

        z = z_ref[...].astype(F32)
        sgz = _sigmoid(z)
        silu_z = z * sgz
        yz = Y * silu_z
        ng = ng_ref[...]
        dout = dy_ref[...]
        dyn = dout * ng
        dyz_parts, dng_parts = [], []
        for g in range(SSM_G):
            gs = slice(g * 512, (g + 1) * 512)
            t = yz[:, gs]
            rstd = lax.rsqrt(jnp.mean(t * t, axis=-1, keepdims=True) + EPS)
            dng_parts.append(jnp.sum(dout[:, gs] * t * rstd, axis=0, keepdims=True))
            dn = dyn[:, gs]
            dyz_parts.append(rstd * dn - t * (rstd * rstd * rstd) * jnp.mean(dn * t, axis=-1, keepdims=True))
        dng_ref[...] += jnp.concatenate(dng_parts, axis=1)
        dyz = jnp.concatenate(dyz_parts, axis=1)
        dY = dyz * silu_z
        dout_ref[:, 0:SSM_W] = (dyz * Y * (sgz * (1.0 + z * (1.0 - sgz)))).astype(BF16)

        ex_t = _expand_matrix_t()
        ddsk_ref[...] += _dot_sel(jnp.broadcast_to(jnp.sum(dY * xs, axis=0, keepdims=True), (8, SSM_W)), ex_t, 3)[0:1, :]

        lane = lax.broadcasted_iota(jnp.int32, (L, 128), 1)
        subl = lax.broadcasted_iota(jnp.int32, (128, L), 0)
        coll = lax.broadcasted_iota(jnp.int32, (128, L), 1)
        r_cols = jnp.zeros((L, 128), F32)
        c_rows = jnp.zeros((128, L), F32)
        for g in range(SSM_G):
            gs = slice(g * 512, (g + 1) * 512)
            G, cg, bg = Gs[g], Cs[g], Bs[g]
            hp_g = hp_ref[0, gs, :]
            dh_g = dH[gs, :]
            dY_g = dY[:, gs]
            dZ = dY_g * eacs_x[:, gs]
            dZb = dZ.astype(BF16)
            dC = _dot(dZb, hp_g.astype(BF16))
            dh_from_off = _dot_tn(dZ, cg)
            dhb = dh_g.astype(BF16)
            Q = _dot_nt(bg, dhb)
            dB = _dot(xw[:, gs].astype(BF16), dhb)
            qd = Q * dte_x[:, gs]
            dxdt[:, gs] = qd
            tsc[:, gs] = qd * xdt[:, gs]
            dG = jnp.zeros((L, L), F32)
            for j in range(8):
                hh = g * 8 + j
                hs = slice(hh * SSM_P, (hh + 1) * SSM_P)
                seg = acs[:, hh:hh + 1] - acs_t[hh:hh + 1, :]
                dk = jnp.where(tri, jnp.exp(jnp.minimum(seg, 0.0)), 0.0)
                M = G * dk
                dYh = dY[:, hs]
                dYhb = dYh.astype(BF16)
                dM = _dot_nt(dYhb, xdt[:, hs].astype(BF16))
                dxdt[:, hs] += _dot_tn(M, dYhb)
                dG = dG + dM * dk
                Wm = dM * M
                r_cols = r_cols + jnp.where(lane == hh, jnp.sum(Wm, axis=1, keepdims=True), 0.0)
                c_rows = c_rows + jnp.where(subl == hh, jnp.sum(Wm, axis=0, keepdims=True), 0.0)
                pj = slice(j * SSM_P, (j + 1) * SSM_P)
                cd_h = cd_row[:, hh:hh + 1]
                dcd = jnp.sum(dh_g[pj, :] * hp_g[pj, :]) * cd_h
                c_rows = c_rows - jnp.where((subl == hh) & (coll == L - 1), dcd, 0.0)
                dH[hs, :] = dh_g[pj, :] * cd_h + dh_from_off[pj, :]
            dGb = dG.astype(BF16)
            dC = dC + _dot(dGb, bg)
            dB = dB + _dot_tn(dG, cg)
            dxc[:, SSM_W + g * SSM_N:SSM_W + (g + 1) * SSM_N] = dB
            dxc[:, SSM_W + 512 + g * SSM_N:SSM_W + 512 + (g + 1) * SSM_N] = dC

        row = lax.broadcasted_iota(jnp.int32, (L, 128), 0)
        tv = tsc[...]
        t_last = _dot_sel(jnp.broadcast_to(jnp.sum(tv, axis=0, keepdims=True), (8, SSM_W)), ex_t, 3)[0:1, :]
        dacs = (r_cols - c_rows.T + _dot_sel(dY * yoff - tv, ex_t, 2) + jnp.where(row == L - 1, t_last, 0.0))
        triu = lax.broadcasted_iota(jnp.int32, (L, L), 0) <= lax.broadcasted_iota(jnp.int32, (L, L), 1)
        dadt = _sel_dot(triu, dacs, 3)
        dxdt_v = dxdt[...]
        ddt = _dot_sel(dxdt_v * xs, ex_t, 2) + dadt * a
        dalog_ref[...] += jnp.sum(dadt * dt * a, axis=0, keepdims=True)
        ddt_raw = jnp.where(lane < SSM_H, ddt * _sigmoid(dt_raw + dtb), 0.0)
        ddtb_ref[...] += jnp.sum(ddt_raw, axis=0, keepdims=True)
        dout_ref[:, 5120:5248] = ddt_raw.astype(BF16)
        dout_ref[:, 5248:5376] = jnp.zeros((L, 128), BF16)

        dxc[:, 0:SSM_W] = dxdt_v * dt_x + d_x * dY
        dpre = dxc[...] * (sg_pre * (1.0 + pre * (1.0 - sg_pre)))
        dcb_ref[...] += jnp.sum(dpre, axis=0, keepdims=True)
        dext[0:L, :] = dpre
        x_cur = ext[8:8 + L, :]
        dx = None
        for k in range(CONV_K):
            dsh = _rows_from(dext, 3 - k)
            term = cw_ref[k:k + 1, :] * dsh
            dx = term if dx is None else dx + term
            dcw_ref[k:k + 1, :] += jnp.sum(dsh * x_cur, axis=0, keepdims=True)
        dout_ref[:, SSM_W:SSM_W + CONV_C] = dx.astype(BF16)
        dext[L:L + 8, :] = dpre[0:8, :]

    big = lambda w: pl.BlockSpec((L, w), lambda c: (cidx(c), 0))
    return pl.pallas_call(
        body, name="ssd_bwd", grid=(nc,),
        in_specs=[big(SSM_W)] + _ssd_in_specs(True, nc) + [
            pl.BlockSpec((16, 1024), lambda c, k=k: (jnp.maximum(8 * cidx(c) - 1, 0), k)) for k in (2, 3, 4)] + [
            pl.BlockSpec((1, SSM_W, SSM_N), lambda c: (cidx(c), 0, 0))],
        out_specs=[big(5376), _full((8, CONV_C)), _full((1, CONV_C)),
                   _full((1, 128)), _full((1, 128)), _full((1, 128)), _full((1, SSM_W))],
        out_shape=[jax.ShapeDtypeStruct((S, 5376), BF16), jax.ShapeDtypeStruct((8, CONV_C), F32),
                   jax.ShapeDtypeStruct((1, CONV_C), F32), jax.ShapeDtypeStruct((1, 128), F32),
                   jax.ShapeDtypeStruct((1, 128), F32), jax.ShapeDtypeStruct((1, 128), F32),
                   jax.ShapeDtypeStruct((1, SSM_W), F32)],
        scratch_shapes=[pltpu.VMEM((SSM_W, SSM_N), F32), pltpu.VMEM((8 + L, CONV_C), F32),
                        pltpu.VMEM((L + 8, CONV_C), F32), pltpu.VMEM((L, SSM_W), F32),
                        pltpu.VMEM((L, SSM_W), F32), pltpu.VMEM((L, SSM_W), F32),
                        pltpu.VMEM((L, CONV_C), F32), pltpu.VMEM((L, SSM_W), F32)],
        compiler_params=_params(("arbitrary",)),
    )(dy, proj, proj, proj, proj, proj, conv_w, conv_b, dt_bias, a_log, d_skip, norm_g, proj, proj, proj, hstates)


def _resident(shape):
    nd = len(shape)
    return pl.BlockSpec(shape, lambda *_: (0,) * nd, pipeline_mode=pl.Buffered(1))


def merge_fwd(y_att, y_sg, y_ssm, proj, x, w_a, w_s, w_m, w_o, g_post):
    S = x.shape[0]
    tm = 256

    def body(ya_ref, ys_ref, ym_ref, gate_ref, x_ref, wa_ref, ws_ref, wm_ref, wo_ref, gp_ref,
             xn_ref, bra_ref, brs_ref, brm_ref, mg_ref, out_ref):
        bra = _dot(ya_ref[...], wa_ref[...])
        brs = _dot(ys_ref[...], ws_ref[...])
        brm = _dot(ym_ref[...], wm_ref[...])
        bra_ref[...] = bra.astype(BF16)
        brs_ref[...] = brs.astype(BF16)
        brm_ref[...] = brm.astype(BF16)
        gate = gate_ref[...].astype(F32)
        merged = (_sigmoid(gate[:, 0:1024]) * bra + _sigmoid(gate[:, 1024:2048]) * brs
                  + _sigmoid(gate[:, 2048:3072]) * brm)
        mb = merged.astype(BF16)
        mg_ref[...] = mb
        o = _dot(mb, wo_ref[...])
        out_ref[...] = o
        r = lax.rsqrt(jnp.mean(o * o, axis=-1, keepdims=True) + EPS)
        xn_ref[...] = x_ref[...] + o * r * gp_ref[...]

    row = lambda w: pl.BlockSpec((tm, w), lambda i: (i, 0))
    return pl.pallas_call(
        body, name="merge_fwd", grid=(S // tm,),
        in_specs=[row(1024), row(1024), row(2048), pl.BlockSpec((tm, 3072), lambda i: (i, 0)),
                  row(D), _resident((1024, D)), _resident((1024, D)), _resident((2048, D)), _resident((D, D)),
                  _full((1, D))],
        out_specs=[row(D)] * 6,
        out_shape=[jax.ShapeDtypeStruct((S, D), F32)] + [jax.ShapeDtypeStruct((S, D), BF16)] * 4
        + [jax.ShapeDtypeStruct((S, D), F32)],
        compiler_params=_params(("arbitrary",)),
    )(y_att, y_sg, y_ssm, proj, x, w_a, w_s, w_m, w_o, g_post)


def merge_bwd(dy, out, g_post, proj, br_a, br_s, br_m, w_a, w_s, w_m, w_o):
    S = dy.shape[0]
    tm = 256

    def body(dy_ref, o_ref, gp_ref, gate_ref, bra_ref, brs_ref, brm_ref, wa_ref, ws_ref, wm_ref, wo_ref,
             dout_ref, dba_ref, dbs_ref, dbm_ref, dgate_ref, dya_ref, dys_ref, dym_ref, dgp_ref):
        @pl.when(pl.program_id(0) == 0)
        def _():
            dgp_ref[...] = jnp.zeros_like(dgp_ref)

        o = o_ref[...]
        dyv = dy_ref[...]
        r = lax.rsqrt(jnp.mean(o * o, axis=-1, keepdims=True) + EPS)
        dyg = dyv * gp_ref[...]
        do = r * dyg - o * (r * r * r) * jnp.mean(dyg * o, axis=-1, keepdims=True)
        dgp_ref[...] += jnp.sum(dyv * o * r, axis=0, keepdims=True)
        dob = do.astype(BF16)
        dout_ref[...] = dob
        dmerged = _dot_nt(dob, wo_ref[...])
        for idx, (br_ref, dbr_ref, w_ref, dyi_ref) in enumerate((
                (bra_ref, dba_ref, wa_ref, dya_ref), (brs_ref, dbs_ref, ws_ref, dys_ref),
                (brm_ref, dbm_ref, wm_ref, dym_ref))):
            s = _sigmoid(gate_ref[:, idx * 1024:(idx + 1) * 1024].astype(F32))
            dbr = (dmerged * s).astype(BF16)
            dbr_ref[...] = dbr
            dgate_ref[:, idx * 1024:(idx + 1) * 1024] = (dmerged * br_ref[...].astype(F32) * s * (1.0 - s)).astype(BF16)
            dyi_ref[...] = _dot_nt(dbr, w_ref[...])

    row = lambda w: pl.BlockSpec((tm, w), lambda i: (i, 0))
    return pl.pallas_call(
        body, name="merge_bwd", grid=(S // tm,),
        in_specs=[row(D), row(D), _full((1, D)), pl.BlockSpec((tm, 3072), lambda i: (i, 0)),
                  row(D), row(D), row(D),
                  _resident((1024, D)), _resident((1024, D)), _resident((2048, D)), _resident((D, D))],
        out_specs=[row(D), row(D), row(D), row(D), row(3072), row(1024), row(1024), row(2048), _full((1, D))],
        out_shape=[jax.ShapeDtypeStruct((S, D), BF16)] * 4 + [
            jax.ShapeDtypeStruct((S, 3072), BF16), jax.ShapeDtypeStruct((S, 1024), F32),
            jax.ShapeDtypeStruct((S, 1024), F32), jax.ShapeDtypeStruct((S, 2048), F32),
            jax.ShapeDtypeStruct((1, D), F32)],
        compiler_params=_params(("arbitrary",)),
    )(dy, out, g_post, proj, br_a, br_s, br_m, w_a, w_s, w_m, w_o)


def loss_head(y, target):
    S = y.shape[0]
    tm = 512

    def body(y_ref, t_ref, dy_ref, loss_ref):
        @pl.when(pl.program_id(0) == 0)
        def _():
            loss_ref[...] = jnp.zeros_like(loss_ref)
        e = y_ref[...] - t_ref[...]
        dy_ref[...] = e * (1.0 / D)
        loss_ref[...] += 0.5 * jnp.sum(jnp.mean(e * e, axis=-1, keepdims=True))

    row = pl.BlockSpec((tm, D), lambda i: (i, 0))
    return pl.pallas_call(
        body, name="loss_head", grid=(S // tm,),
        in_specs=[row, row], out_specs=[row, _full((1, 128))],
        out_shape=[jax.ShapeDtypeStruct((S, D), F32), jax.ShapeDtypeStruct((1, 128), F32)],
        compiler_params=_params(("arbitrary",)),
    )(y, target)


def _adam(w, g, m, v):
    mn = ADAM_B1 * m + (1.0 - ADAM_B1) * g
    vn = ADAM_B2 * v + (1.0 - ADAM_B2) * (g * g)
    m_hat = mn / (1.0 - ADAM_B1 ** ADAM_STEP)
    v_hat = vn / (1.0 - ADAM_B2 ** ADAM_STEP)
    return -ADAM_LR * (m_hat / (jnp.sqrt(v_hat) + ADAM_EPS) + ADAM_WD * w), mn, vn


def adamw_big(w, m, v, halves0, sum1, cc, name, tr):
    _, R, C = w.shape
    nper = R // tr
    f, fb, n0, off_a, off_b = halves0
    p, pb, off1 = sum1

    def body(c_ref, w_ref, m_ref, v_ref, f_ref, fb_ref, p_ref, pb_ref, g_ref, d_ref, nm_ref, nv_ref):
        i = pl.program_id(0)
        half = jnp.where(i % nper >= n0, 1, 0)
        g0 = jnp.where(c_ref[0] == half, f_ref[...], fb_ref[...])
        g = jnp.where(i < nper, g0, p_ref[...] + pb_ref[...])
        g_ref[0] = g
        d_ref[0], nm_ref[0], nv_ref[0] = _adam(w_ref[0], g, m_ref[0], v_ref[0])

    def blk0(i, c):
        il = jnp.minimum(i, nper - 1)
        return (jnp.where(il >= n0, off_b + il - n0, off_a + il), 0)

    wblk = pl.BlockSpec((1, tr, C), lambda i, c: (i // nper, i % nper, 0))
    b0 = pl.BlockSpec((tr, C), blk0)
    b1 = pl.BlockSpec((tr, C), lambda i, c: (off1 + jnp.maximum(i - nper, 0), 0))
    grid_spec = pltpu.PrefetchScalarGridSpec(
        num_scalar_prefetch=1, grid=(2 * nper,),
        in_specs=[wblk, wblk, wblk, b0, b0, b1, b1], out_specs=[wblk] * 4)
    return pl.pallas_call(
        body, name=name, grid_spec=grid_spec,
        out_shape=[jax.ShapeDtypeStruct(w.shape, F32)] * 4,
        compiler_params=_params(("arbitrary",)),
    )(cc, w, m, v, f, fb, p, pb)


def adamw_plain(w, g, m, v, name):
    def body(w_ref, g_ref, m_ref, v_ref, d_ref, nm_ref, nv_ref):
        d_ref[...], nm_ref[...], nv_ref[...] = _adam(w_ref[...], g_ref[...], m_ref[...], v_ref[...])

    return pl.pallas_call(
        body, name=name, out_shape=[jax.ShapeDtypeStruct(w.shape, F32)] * 3, compiler_params=_params(),
    )(w, g, m, v)


SMALL = {"norm_pre": ("g_pre", 8), "norm_post": ("g_post", 8), "att_sinks": ("sinks", 8), "sg_ln_g": ("ln_g", 8),
         "sg_ln_b": ("ln_b", 8), "sg_w": ("sg_w", 1024), "sg_b": ("sg_bt", 8), "ssm_conv_b": ("conv_b", 24),
         "ssm_dt_bias": ("dt_bias", 8), "ssm_a_log": ("a_log", 8), "ssm_d": ("d_skip", 8), "ssm_norm_g": ("norm_g", 16)}
SMALL_LAYER_ROWS = sum(r for _, r in SMALL.values())
REL_ROW = DEPTH * SMALL_LAYER_ROWS
LOSS_ROW = REL_ROW + 32
SMALL_ROWS = LOSS_ROW + 8


def _small_rows():
    rows, r = {}, 0
    for l in range(DEPTH):
        for name, (_, n) in SMALL.items():
            rows[(l, name)] = r
            r += n
    return rows


def adamw_small(red, rel, small):
    names = list(SMALL) + ["rel_bias"]
    params = dict(small, rel_bias=rel)
    rows = _small_rows()

    def grad_of(red_ref, l, name, n):
        r0 = rows[(l, name)]
        if name == "sg_b":
            return red_ref[r0:r0 + 8, :]
        if n < 128:
            return red_ref[r0:r0 + 1, 0:n]
        return jnp.concatenate([red_ref[r0 + j:r0 + j + 1, :] for j in range(n // 128)], axis=1)

    def body(red_ref, *refs):
        ins, outs = refs[:3 * len(names)], refs[3 * len(names):]
        for i, name in enumerate(names):
            w_ref, m_ref, v_ref = ins[3 * i:3 * i + 3]
            o = outs[4 * i:4 * i + 4]
            if name == "rel_bias":
                g = red_ref[REL_ROW:REL_ROW + 32, 0:16]
                o[0][...] = g
                o[1][...], o[2][...], o[3][...] = _adam(w_ref[...], g, m_ref[...], v_ref[...])
                continue
            for l in range(DEPTH):
                if name == "sg_w":
                    for grp in range(8):
                        r0 = rows[(l, name)] + grp * 128
                        g = red_ref[r0:r0 + 128, :]
                        o[0][l, grp] = g
                        o[1][l, grp], o[2][l, grp], o[3][l, grp] = _adam(w_ref[l, grp], g, m_ref[l, grp], v_ref[l, grp])
                elif name == "sg_b":
                    g = grad_of(red_ref, l, name, 128)
                    o[0][l] = g
                    o[1][l], o[2][l], o[3][l] = _adam(w_ref[l], g, m_ref[l], v_ref[l])
                else:
                    sl = slice(l, l + 1)
                    g = grad_of(red_ref, l, name, w_ref.shape[-1])
                    o[0][sl, :] = g
                    o[1][sl, :], o[2][sl, :], o[3][sl, :] = _adam(w_ref[sl, :], g, m_ref[sl, :], v_ref[sl, :])

    flat_in = [a for name in names for a in params[name]]
    out_shape = [jax.ShapeDtypeStruct(params[name][0].shape, F32) for name in names for _ in range(4)]
    res = pl.pallas_call(body, name="adamw_small", out_shape=out_shape, compiler_params=_params())(red, *flat_in)
    return {name: tuple(res[4 * i:4 * i + 4]) for i, name in enumerate(names)}


ANY = pl.BlockSpec(memory_space=pl.ANY)


def _place():
    x, y, c = lax.axis_index("x"), lax.axis_index("y"), lax.axis_index("c")
    others = [(1 - x, y), (x, 1 - y), (1 - x, 1 - y)]
    return x, y, c, others


def _rcopy(src, dst, ssem, rsem, to):
    return pltpu.make_async_remote_copy(src_ref=src, dst_ref=dst, send_sem=ssem, recv_sem=rsem,
                                        device_id=to, device_id_type=MESH)


def gather_weights(arrs):
    n = len(arrs)

    def body(*refs):
        srcs, outs, ssem, rsem = refs[:n], refs[n:2 * n], refs[2 * n], refs[2 * n + 1]
        x, y, c, others = _place()
        me = 2 * x + y
        sib = (x, y, 1 - c)
        first = [_rcopy(srcs[i].at[c], outs[i].at[c, me], ssem.at[6 * i + k], rsem.at[6 * i + k], (ox, oy, c))
                 for i in range(n) for k, (ox, oy) in enumerate(others)]
        for cp in first:
            cp.start()
        passed = []
        for k, (ox, oy) in enumerate(others):
            for i in range(n):
                slot = outs[i].at[c, 2 * ox + oy]
                _rcopy(slot, slot, ssem.at[6 * i + k], rsem.at[6 * i + k], sib).wait_recv()
                fw = _rcopy(slot, slot, ssem.at[6 * i + 3 + k], rsem.at[6 * i + 3 + k], sib)
                fw.start()
                passed.append(fw)
        for k, (ox, oy) in enumerate(others):
            for i in range(n):
                slot = outs[i].at[1 - c, 2 * ox + oy]
                _rcopy(slot, slot, ssem.at[6 * i + 3 + k], rsem.at[6 * i + 3 + k], sib).wait_recv()
        for cp in first + passed:
            cp.wait_send()

    return pl.pallas_call(
        body, name="gather_weights",
        in_specs=[ANY] * n, out_specs=[ANY] * n,
        out_shape=[jax.ShapeDtypeStruct((2, SHARDS) + a.shape[1:], a.dtype) for a in arrs],
        scratch_shapes=[pltpu.SemaphoreType.DMA((6 * n,)), pltpu.SemaphoreType.DMA((6 * n,))],
    )(*arrs)


HBM = pl.BlockSpec(memory_space=pltpu.HBM)
SEM = pl.BlockSpec(memory_space=pltpu.SEMAPHORE)
EFFECT = pltpu.SideEffectType.DATAFLOW_SIDE_EFFECTING


def _in_hbm(a):
    return pltpu.with_memory_space_constraint(a, pltpu.HBM)


def gather_start(srcs, after, name, by_dest=False):
    n = len(srcs)
    lands = [_in_hbm(lax.empty((SHARDS,) + a.shape[-2:], a.dtype)) for a in srcs]
    na = len(after)

    def body(*refs):
        src, land = refs[:n], refs[n:2 * n]
        ssem, rsem, token = refs[2 * n + na], refs[2 * n + na + 1], refs[-1]
        x, y, c, others = _place()
        me = 2 * x + y
        for i in range(n):
            for k, (ox, oy) in enumerate(others):
                s = src[i].at[2 * ox + oy] if by_dest else src[i]
                _rcopy(s, land[i].at[me], ssem.at[3 * i + k], rsem.at[3 * i + k], (ox, oy, c)).start()
        token[...] = jnp.zeros_like(token)

    bufs = [_in_hbm(a) for a in srcs] + lands
    out = pl.pallas_call(
        body, name=name,
        out_shape=(pltpu.SemaphoreType.DMA((3 * n,)), pltpu.SemaphoreType.DMA((3 * n,)),
                   *[pltpu.HBM(b.shape, b.dtype) for b in bufs], jax.ShapeDtypeStruct((8, 128), F32)),
        in_specs=[HBM] * (2 * n) + [ANY] * na,
        out_specs=(SEM, SEM, *[HBM] * (2 * n), pl.BlockSpec(memory_space=pltpu.VMEM)),
        input_output_aliases={i: 2 + i for i in range(2 * n)},
        compiler_params=pltpu.CompilerParams(has_side_effects=EFFECT),
    )(*bufs, *after)
    return out[0], out[1], list(out[2:2 + n]), list(out[2 + n:2 + 2 * n]), out[-1]


def gather_wait(ssem, rsem, srcs, lands, after, name, by_dest=False):
    n = len(srcs)

    def body(*refs):
        src, land = refs[:n], refs[n:2 * n]
        s_sem, r_sem = refs[2 * n], refs[2 * n + 1]
        x, y, c, others = _place()
        for i in range(n):
            for k, (ox, oy) in enumerate(others):
                s = src[i].at[2 * ox + oy] if by_dest else src[i]
                cp = _rcopy(s, land[i].at[2 * ox + oy], s_sem.at[3 * i + k], r_sem.at[3 * i + k], (ox, oy, c))
                cp.wait_send()
                cp.wait_recv()

    bufs = list(srcs) + list(lands)
    out = pl.pallas_call(
        body, name=name,
        out_shape=tuple(pltpu.HBM(b.shape, b.dtype) for b in bufs),
        in_specs=[HBM] * (2 * n) + [SEM, SEM, ANY],
        out_specs=tuple([HBM] * (2 * n)),
        input_output_aliases={i: i for i in range(2 * n)},
        compiler_params=pltpu.CompilerParams(has_side_effects=EFFECT),
    )(*bufs, ssem, rsem, after)
    return list(out[n:2 * n])


def grad_sibling_exchange(arrs):
    n = len(arrs)

    def body(*refs):
        srcs, outs, ssem, rsem = refs[:n], refs[n:2 * n], refs[2 * n], refs[2 * n + 1]
        x, y, c, _ = _place()
        cps = [_rcopy(srcs[i].at[1 - c], outs[i], ssem.at[i], rsem.at[i], (x, y, 1 - c)) for i in range(n)]
        for cp in cps:
            cp.start()
        for cp in cps:
            cp.wait()

    return pl.pallas_call(
        body, name="grad_sibling_exchange",
        in_specs=[ANY] * n, out_specs=[ANY] * n,
        out_shape=[jax.ShapeDtypeStruct(a.shape[1:], F32) for a in arrs],
        scratch_shapes=[pltpu.SemaphoreType.DMA((n,)), pltpu.SemaphoreType.DMA((n,))],
    )(*arrs)


def grad_chip_sum(g, sb, cc, tr, name):
    _, _, R, C = g.shape
    blk = pl.BlockSpec((1, tr, C), lambda s, r, c: (s, r, 0))
    grid_spec = pltpu.PrefetchScalarGridSpec(
        num_scalar_prefetch=1, grid=(SHARDS, R // tr),
        in_specs=[pl.BlockSpec((1, 1, tr, C), lambda s, r, c: (c[0], s, r, 0)), blk],
        out_specs=[blk, blk])

    def body(c_ref, a_ref, b_ref, o_ref, ob_ref):
        t = a_ref[0] + b_ref[...]
        o_ref[...] = t
        ob_ref[...] = t.astype(BF16)

    return pl.pallas_call(
        body, name=name, grid_spec=grid_spec,
        out_shape=[jax.ShapeDtypeStruct((SHARDS, R, C), F32), jax.ShapeDtypeStruct((SHARDS, R, C), BF16)],
        compiler_params=_params(("arbitrary", "arbitrary")),
    )(cc, g, sb)


def grad_chip_exchange(arrs):
    n = len(arrs)

    def body(*refs):
        srcs, outs, ssem, rsem = refs[:n], refs[n:2 * n], refs[2 * n], refs[2 * n + 1]
        x, y, c, others = _place()
        me = 2 * x + y
        sends = [_rcopy(srcs[i].at[2 * ox + oy], outs[i].at[me], ssem.at[3 * i + k], rsem.at[3 * i + k], (ox, oy, c))
                 for i in range(n) for k, (ox, oy) in enumerate(others)]
        for cp in sends:
            cp.start()
        for i in range(n):
            for k, (ox, oy) in enumerate(others):
                slot = outs[i].at[2 * ox + oy]
                _rcopy(slot, slot, ssem.at[3 * i + k], rsem.at[3 * i + k], (ox, oy, c)).wait_recv()
        for cp in sends:
            cp.wait_send()

    return pl.pallas_call(
        body, name="grad_chip_exchange",
        in_specs=[ANY] * n, out_specs=[ANY] * n,
        out_shape=[jax.ShapeDtypeStruct(a.shape, a.dtype) for a in arrs],
        scratch_shapes=[pltpu.SemaphoreType.DMA((3 * n,)), pltpu.SemaphoreType.DMA((3 * n,))],
    )(*arrs)


def grad_shard_sum(t, rb, me, tr, name):
    _, R, C = t.shape
    grid_spec = pltpu.PrefetchScalarGridSpec(
        num_scalar_prefetch=1, grid=(R // tr,),
        in_specs=[pl.BlockSpec((1, tr, C), lambda r, m: (m[0], r, 0)),
                  pl.BlockSpec((SHARDS, tr, C), lambda r, m: (0, r, 0))],
        out_specs=pl.BlockSpec((tr, C), lambda r, m: (r, 0)))

    def body(m_ref, t_ref, r_ref, o_ref):
        part = [jnp.where(m_ref[0] == s, t_ref[0], r_ref[s].astype(F32)) for s in range(SHARDS)]
        o_ref[...] = ((part[0] + part[1]) + part[2]) + part[3]

    return pl.pallas_call(
        body, name=name, grid_spec=grid_spec,
        out_shape=jax.ShapeDtypeStruct((R, C), F32),
        compiler_params=_params(("arbitrary",)),
    )(me, t, rb)


def grad_sibling_share(arrs, name):
    n = len(arrs)

    def body(*refs):
        srcs, outs, ssem, rsem = refs[:n], refs[n:2 * n], refs[2 * n], refs[2 * n + 1]
        x, y, c, _ = _place()
        cps = [_rcopy(srcs[i], outs[i], ssem.at[i], rsem.at[i], (x, y, 1 - c)) for i in range(n)]
        for cp in cps:
            cp.start()
        for cp in cps:
            cp.wait()

    return pl.pallas_call(
        body, name=name,
        in_specs=[ANY] * n, out_specs=[ANY] * n,
        out_shape=[jax.ShapeDtypeStruct(a.shape, F32) for a in arrs],
        scratch_shapes=[pltpu.SemaphoreType.DMA((n,)), pltpu.SemaphoreType.DMA((n,))],
    )(*arrs)


def _allreduce_rows(src, sib_buf, chips, out_ref, ssem, rsem):
    x, y, c, others = _place()
    me = 2 * x + y
    cp = _rcopy(src, sib_buf, ssem.at[0], rsem.at[0], (x, y, 1 - c))
    cp.start()
    cp.wait()
    chips[me] = src[...] + sib_buf[...]
    sends = [_rcopy(chips.at[me], chips.at[me], ssem.at[1 + k], rsem.at[1 + k], (ox, oy, c))
             for k, (ox, oy) in enumerate(others)]
    for s in sends:
        s.start()
    for k, (ox, oy) in enumerate(others):
        slot = chips.at[2 * ox + oy]
        _rcopy(slot, slot, ssem.at[1 + k], rsem.at[1 + k], (ox, oy, c)).wait_recv()
    for s in sends:
        s.wait_send()
    out_ref[...] = ((chips[0] + chips[1]) + chips[2]) + chips[3]


def _allreduce_scratch(rows):
    return [pltpu.VMEM((rows, 128), F32), pltpu.VMEM((SHARDS, rows, 128), F32),
            pltpu.SemaphoreType.DMA((4,)), pltpu.SemaphoreType.DMA((4,))]


def allreduce_rows(buf, name):
    rows = buf.shape[0]
    VM = pl.BlockSpec(memory_space=pltpu.VMEM)

    def body(src_ref, out_ref, sib_buf, chips, ssem, rsem):
        _allreduce_rows(src_ref, sib_buf, chips, out_ref, ssem, rsem)

    return pl.pallas_call(
        body, name=name, in_specs=[VM], out_specs=VM,
        out_shape=jax.ShapeDtypeStruct((rows, 128), F32),
        scratch_shapes=_allreduce_scratch(rows), compiler_params=_params(),
    )(buf)


def small_allreduce(grads, rel, loss_part):
    rows = _small_rows()
    keys = [(l, name) for l in range(DEPTH) for name in SMALL]
    flat = [grads[l][SMALL[name][0]] for l, name in keys] + [rel, loss_part]

    def body(*refs):
        ins = refs[:len(flat)]
        out_ref, src, sib_buf, chips, ssem, rsem = refs[len(flat):]
        src[...] = jnp.zeros_like(src)
        for (l, name), ref in zip(keys, ins):
            r0 = rows[(l, name)]
            if name == "sg_w":
                for grp in range(8):
                    src[r0 + grp * 128:r0 + (grp + 1) * 128, :] = ref[grp]
            elif name == "sg_b":
                src[r0:r0 + 8, :] = ref[...].T[0:8, :]
            else:
                for j in range(ref.shape[1] // 128):
                    src[r0 + j:r0 + j + 1, :] = ref[:, j * 128:(j + 1) * 128]
        src[REL_ROW:REL_ROW + 32, 0:16] = ins[-2][...]
        src[LOSS_ROW:LOSS_ROW + 1, :] = ins[-1][...]
        _allreduce_rows(src, sib_buf, chips, out_ref, ssem, rsem)

    return pl.pallas_call(
        body, name="small_allreduce",
        out_shape=jax.ShapeDtypeStruct((SMALL_ROWS, 128), F32),
        scratch_shapes=[pltpu.VMEM((SMALL_ROWS, 128), F32)] + _allreduce_scratch(SMALL_ROWS),
        compiler_params=_params(),
    )(*flat)


def _pad_lanes(v):
    return jnp.zeros((1, 128), F32).at[0, :v.shape[0]].set(v)


def layer_fwd(x, wts, bias):
    wt = wts["wt"]
    tn = {name: t for name, _, t in GROUPS}
    p_gate, h = inproj_first(x, wts["g_pre"], wt["gate"], tn["gate"], "inproj_gate")
    p_sgu, p_att, p_ssd = (inproj_group(h, wt[n], tn[n], "inproj_" + n, F32 if n == "att" else BF16)
                           for n in ("sgu", "att", "ssd"))
    y_att = att_fwd(p_att, bias, wts["sinks"])
    y_sg = sgu_fwd(p_sgu, wts["ln_g"], wts["ln_b"], wts["sg_w"], wts["sg_bt"])
    y_ssm, hst = ssd_fwd(p_ssd, wts["conv_w"], wts["conv_b"], wts["dt_bias"], wts["a_log"], wts["d_skip"],
                         wts["norm_g"])
    x_new, br_a, br_s, br_m, merged, out = merge_fwd(
        y_att, y_sg, y_ssm, p_gate, x, wts["w_a"], wts["w_s"], wts["w_m"], wts["w_o"], wts["g_post"])
    saved = dict(x=x, p_gate=p_gate, p_sgu=p_sgu, p_att=p_att, p_ssd=p_ssd, h=h,
                 y_att=y_att, y_sg=y_sg, y_ssm=y_ssm, hst=hst,
                 br_a=br_a, br_s=br_s, br_m=br_m, merged=merged, out=out)
    return x_new, saved


def layer_bwd(dy, wts, bias, sv):
    dout, dba, dbs, dbm, d_gate, dya, dys, dym, dg_post = merge_bwd(
        dy, sv["out"], wts["g_post"], sv["p_gate"], sv["br_a"], sv["br_s"], sv["br_m"],
        wts["w_a"], wts["w_s"], wts["w_m"], wts["w_o"])
    d_att, dbias, dsinks = att_bwd(dya, sv["p_att"], bias, wts["sinks"])
    d_sgu, dsg_w, dsg_bt, dln_g, dln_b = sgu_bwd(dys, sv["p_sgu"], wts["ln_g"], wts["ln_b"], wts["sg_w"],
                                                 wts["sg_bt"])
    d_ssd, dcw, dcb, ddtb, dalog, ddsk, dng = ssd_bwd(
        dym, sv["p_ssd"], sv["hst"], wts["conv_w"], wts["conv_b"], wts["dt_bias"], wts["a_log"], wts["d_skip"],
        wts["norm_g"])
    dps = dict(gate=d_gate, sgu=d_sgu, att=d_att, ssd=d_ssd)
    wt = wts["wt"]
    tn = {name: t for name, _, t in GROUPS}
    acc = None
    for n in ("gate", "sgu", "ssd"):
        acc = dh_group(dps[n], wt[n], acc, tn[n], "dh_" + n)
    dx, dg_pre = dh_last(dps["att"], wt["att"], acc, sv["x"], wts["g_pre"], dy, tn["att"], "dh_att")
    grads = dict(
        w_in={n: dw_group(dps[n], sv["h"], tn[n], "dw_in_" + n) for n in dps},
        w_a=matmul_tn(sv["y_att"], dba, "dw_att"),
        w_s=matmul_tn(sv["y_sg"], dbs, "dw_sg"),
        w_m=matmul_tn(sv["y_ssm"], dbm, "dw_ssm"),
        w_o=matmul_tn(sv["merged"], dout, "dw_out"),
        g_pre=dg_pre, g_post=dg_post, sinks=dsinks, ln_g=dln_g, ln_b=dln_b, sg_w=dsg_w, sg_bt=dsg_bt,
        conv_w=dcw, conv_b=dcb, dt_bias=ddtb, a_log=dalog, d_skip=ddsk, norm_g=dng, bias=dbias)
    return dx, grads


REST_OFF = (0, 256, 512, 1024, 1280)
GR_ROWS = 1536
GR_CONV = 1280
W_IN_SPLIT = 1600
W_IN_HALF = 1824


def kernel(x, w_in, norm_pre, norm_post, rel_bias, att_sinks, sg_ln_g, sg_ln_b, sg_w, sg_b, ssm_conv_w, ssm_conv_b, ssm_dt_bias, ssm_a_log, ssm_d, ssm_norm_g, w_br_att, w_br_sg, w_br_ssm, w_out, loss_target, m_w_in, m_norm_pre, m_norm_post, m_rel_bias, m_att_sinks, m_sg_ln_g, m_sg_ln_b, m_sg_w, m_sg_b, m_ssm_conv_w, m_ssm_conv_b, m_ssm_dt_bias, m_ssm_a_log, m_ssm_d, m_ssm_norm_g, m_w_br_att, m_w_br_sg, m_w_br_ssm, m_w_out, v_w_in, v_norm_pre, v_norm_post, v_rel_bias, v_att_sinks, v_sg_ln_g, v_sg_ln_b, v_sg_w, v_sg_b, v_ssm_conv_w, v_ssm_conv_b, v_ssm_dt_bias, v_ssm_a_log, v_ssm_d, v_ssm_norm_g, v_w_br_att, v_w_br_sg, v_w_br_ssm, v_w_out):
    cx, cy, cc = lax.axis_index("x"), lax.axis_index("y"), lax.axis_index("c")
    me = 2 * cx + cy
    xs = x[0]
    S = xs.shape[0]

    tr = lambda a: jnp.transpose(a, (0, 2, 1))
    w_in_b = tr(w_in).astype(BF16)
    w_rest_b = jnp.concatenate([w_br_att, w_br_sg, w_br_ssm, w_out], axis=1).astype(BF16)
    halves = lambda a: a.reshape(2, a.shape[0] // 2, a.shape[1])
    all0_in, all0_rest = gather_weights([halves(w_in_b[0]), halves(w_rest_b[0])])
    convw_slot = jnp.zeros((SHARDS, DEPTH * CONV_K * 768 // 128, 128), F32)
    convw_slot = lax.dynamic_update_index_in_dim(
        convw_slot, jnp.where(cc == 0, 1.0, 0.0) * ssm_conv_w.reshape(-1, 128), me, 0)
    convw_rows = allreduce_rows(convw_slot.reshape(-1, 128), "gather_conv_w")
    convw_all = convw_rows.reshape(SHARDS, DEPTH, CONV_K, 768).transpose(1, 2, 0, 3).reshape(DEPTH, CONV_K, CONV_C)
    g1_ssem, g1_rsem, g1_srcs, g1_lands, g1_token = gather_start(
        [w_in_b[1], w_rest_b[1]], [convw_rows, all0_rest], "gather_l1_start")

    o = REST_OFF

    def layer_weights(l, gathered_in, gathered_rest, g_pre):
        sh_in = [jnp.where(me == s, w_in_b[l], gathered_in[s]) for s in range(SHARDS)]
        sh_rest = [jnp.where(me == s, w_rest_b[l], gathered_rest[s]) for s in range(SHARDS)]
        rest = lambda k: jnp.concatenate([r[o[k]:o[k + 1]] for r in sh_rest], axis=0)
        return dict(
            wt=group_weights(jnp.concatenate(sh_in, axis=0)),
            w_a=rest(0), w_s=rest(1), w_m=rest(2), w_o=rest(3),
            g_pre=g_pre, g_post=norm_post[l][None], sinks=att_sinks[l],
            ln_g=sg_ln_g[l][None], ln_b=sg_ln_b[l][None], sg_w=sg_w[l],
            sg_bt=sg_b[l].T,
            conv_w=jnp.concatenate([convw_all[l], jnp.zeros((4, CONV_C), F32)], axis=0),
            conv_b=ssm_conv_b[l][None], dt_bias=_pad_lanes(ssm_dt_bias[l]), a_log=_pad_lanes(ssm_a_log[l]),
            d_skip=_pad_lanes(ssm_d[l]), norm_g=ssm_norm_g[l][None])

    bias = bias_table(rel_bias)
    layers = [layer_weights(0, [all0_in[:, s].reshape(3400, D) for s in range(SHARDS)],
                            [all0_rest[:, s].reshape(1280, D) for s in range(SHARDS)],
                            (norm_pre[0] + g1_token[0, 0])[None])]
    act, sv0 = layer_fwd(xs, layers[0], bias)
    land_in, land_rest = gather_wait(g1_ssem, g1_rsem, g1_srcs, g1_lands, act, "gather_l1_wait")
    layers.append(layer_weights(1, land_in, land_rest, norm_pre[1][None]))
    act, sv1 = layer_fwd(act, layers[1], bias)
    saved = [sv0, sv1]
    dy, loss_part = loss_head(act, loss_target[0])
    cvec = jnp.reshape(cc, (1,)).astype(jnp.int32)
    mvec = jnp.reshape(me, (1,)).astype(jnp.int32)

    def by_shard(g):
        gcw = g["conv_w"][0:CONV_K].reshape(CONV_K, SHARDS, 768).transpose(1, 0, 2).reshape(SHARDS, 3, 1024)
        rest = jnp.concatenate([
            g["w_a"].reshape(SHARDS, 256, D), g["w_s"].reshape(SHARDS, 256, D), g["w_o"].reshape(SHARDS, 256, D),
            g["w_m"].reshape(SHARDS, 512, D), jnp.pad(gcw, ((0, 0), (0, GR_ROWS - GR_CONV - 3), (0, 0)))], axis=1)
        return ungroup_grads(g["w_in"]).reshape(SHARDS, 3400, D), rest

    grads = [None] * DEPTH
    dy, grads[1] = layer_bwd(dy, layers[1], bias, saved[1])
    g1_in, g1_rest = by_shard(grads[1])
    g1_in = jnp.pad(g1_in, ((0, 0), (0, W_IN_ROWS - 3400), (0, 0)))
    x1_ssem, x1_rsem, x1_srcs, x1_lands, x1_token = gather_start(
        [g1_in.astype(BF16), g1_rest.astype(BF16)], [], "grads_l1_start", by_dest=True)
    wts0 = dict(layers[0], g_post=layers[0]["g_post"] + x1_token[0, 0])
    dy, grads[0] = layer_bwd(dy, wts0, bias, saved[0])
    grad_x = dy[None]
    r1_in, r1_rest = gather_wait(x1_ssem, x1_rsem, x1_srcs, x1_lands, dy, "grads_l1_wait", by_dest=True)
    p_in = grad_shard_sum(g1_in, r1_in, mvec, 384, "l1_sum_w_in")
    p_rest = grad_shard_sum(g1_rest, r1_rest, mvec, 512, "l1_sum_rest")
    pb_in, pb_rest = grad_sibling_share([p_in, p_rest], "l1_sibling_share")
    grad_rel_local = bias_grad(grads[0]["bias"] + grads[1]["bias"])

    g0_in, g0_rest = by_shard(grads[0])
    pad_to = lambda a, rows: jnp.pad(a, ((0, 0), (0, rows - a.shape[1]), (0, 0)))
    g0_in = jnp.stack([pad_to(g0_in[:, 0:W_IN_SPLIT], W_IN_HALF), pad_to(g0_in[:, W_IN_SPLIT:3400], W_IN_HALF)])
    g0_rest = jnp.stack([g0_rest[:, 0:GR_ROWS // 2], g0_rest[:, GR_ROWS // 2:GR_ROWS]])
    sb_in, sb_rest = grad_sibling_exchange([g0_in, g0_rest])
    t_in, t_in_b = grad_chip_sum(g0_in, sb_in, cvec, 608, "chip_sum_w_in")
    t_rest, t_rest_b = grad_chip_sum(g0_rest, sb_rest, cvec, 384, "chip_sum_rest")
    rb_in, rb_rest = grad_chip_exchange([t_in_b, t_rest_b])
    f_in = grad_shard_sum(t_in, rb_in, mvec, 608, "shard_sum_w_in")
    f_rest = grad_shard_sum(t_rest, rb_rest, mvec, 384, "shard_sum_rest")
    fb_in, fb_rest = grad_sibling_share([f_in, f_rest], "l0_sibling_share")

    red = small_allreduce(grads, grad_rel_local, loss_part)
    loss = red[LOSS_ROW, 0]

    res = adamw_small(red, (rel_bias, m_rel_bias, v_rel_bias), dict(
        norm_pre=(norm_pre, m_norm_pre, v_norm_pre), norm_post=(norm_post, m_norm_post, v_norm_post),
        att_sinks=(att_sinks, m_att_sinks, v_att_sinks), sg_ln_g=(sg_ln_g, m_sg_ln_g, v_sg_ln_g),
        sg_ln_b=(sg_ln_b, m_sg_ln_b, v_sg_ln_b), sg_w=(sg_w, m_sg_w, v_sg_w), sg_b=(sg_b, m_sg_b, v_sg_b),
        ssm_conv_b=(ssm_conv_b, m_ssm_conv_b, v_ssm_conv_b), ssm_dt_bias=(ssm_dt_bias, m_ssm_dt_bias, v_ssm_dt_bias),
        ssm_a_log=(ssm_a_log, m_ssm_a_log, v_ssm_a_log), ssm_d=(ssm_d, m_ssm_d, v_ssm_d),
        ssm_norm_g=(ssm_norm_g, m_ssm_norm_g, v_ssm_norm_g)))
    res["w_in"] = tuple(tr(a) for a in adamw_big(
        tr(w_in), tr(m_w_in), tr(v_w_in), (f_in, fb_in, W_IN_SPLIT // 200, 0, 0), (p_in, pb_in, 0), cvec, "adamw_w_in", 200))
    rest_upd = lambda w, m, v, name, n0, off0, off1: adamw_big(
        w, m, v, (f_rest, fb_rest, n0, off0, off0), (p_rest, pb_rest, off1), cvec, name, 256)
    res["w_br_att"] = rest_upd(w_br_att, m_w_br_att, v_w_br_att, "adamw_w_br_att", 1, 0, 0)
    res["w_br_sg"] = rest_upd(w_br_sg, m_w_br_sg, v_w_br_sg, "adamw_w_br_sg", 1, 1, 1)
    res["w_out"] = rest_upd(w_out, m_w_out, v_w_out, "adamw_w_out", 1, 2, 2)
    res["w_br_ssm"] = rest_upd(w_br_ssm, m_w_br_ssm, v_w_br_ssm, "adamw_w_br_ssm", 0, 0, 3)
    cw0 = jnp.where(cc == 1, f_rest, fb_rest)[GR_CONV - GR_ROWS // 2:GR_CONV - GR_ROWS // 2 + 3]
    cw1 = (p_rest + pb_rest)[GR_CONV:GR_CONV + 3]
    g_conv_w = jnp.stack([cw0.reshape(CONV_K, 768), cw1.reshape(CONV_K, 768)])
    res["ssm_conv_w"] = (g_conv_w,) + tuple(adamw_plain(ssm_conv_w, g_conv_w, m_ssm_conv_w, v_ssm_conv_w, "adamw_conv_w"))

    order = ["w_in", "norm_pre", "norm_post", "rel_bias", "att_sinks", "sg_ln_g", "sg_ln_b", "sg_w", "sg_b",
             "ssm_conv_w", "ssm_conv_b", "ssm_dt_bias", "ssm_a_log", "ssm_d", "ssm_norm_g",
             "w_br_att", "w_br_sg", "w_br_ssm", "w_out"]
    return (loss, grad_x, *[res[n][0] for n in order], *[res[n][1] for n in order],
            *[res[n][2] for n in order], *[res[n][3] for n in order])
```

```python
import functools
import math

import numpy as np
import jax
import jax.numpy as jnp
from jax import lax
from jax.experimental import pallas as pl
from jax.experimental.pallas import tpu as pltpu

F32 = jnp.float32
BF16 = jnp.bfloat16
MESH = pl.DeviceIdType.MESH

D = 1024
DEPTH = 2
EPS = 1e-6
L = 128
HEADS = 16
KV = 2
DH = 64
SSM_W = 2048
SSM_H = 32
SSM_P = 64
SSM_G = 4
SSM_N = 128
CONV_K = 4
CONV_C = 3072
NEG = -1e30
IN_COLS = 13600

GROUPS = (("gate", 3072, 1536), ("sgu", 3072, 1536), ("att", 2304, 2304), ("ssd", 5376, 1792))
W_IN_ROWS = 3456

ADAM_LR = 0.001
ADAM_B1 = 0.9
ADAM_B2 = 0.999
ADAM_EPS = 1e-08
ADAM_WD = 0.01
ADAM_STEP = 10

VMEM_LIMIT = 56 * 1024 * 1024

SHARDS = 4


def _dot(a, b):
    return jnp.dot(a, b, preferred_element_type=F32)


def _dot_nt(a, b):
    return lax.dot_general(a, b, (((1,), (1,)), ((), ())), preferred_element_type=F32)


def _dot_tn(a_f32, b):
    return jnp.dot(a_f32.T.astype(BF16), b, preferred_element_type=F32)


def _dot_hi(a, b):
    return jnp.dot(a, b, preferred_element_type=F32, precision=lax.Precision.HIGHEST)


def _pieces(x, n):
    out = []
    for _ in range(n - 1):
        p = x.astype(BF16)
        out.append(p)
        x = x - p.astype(F32)
    out.append(x.astype(BF16))
    return out


def _dot_sel(a, sel, n):
    sel = sel.astype(BF16)
    acc = None
    for p in _pieces(a, n):
        t = _dot(p, sel)
        acc = t if acc is None else acc + t
    return acc


def _sel_dot(sel, b, n):
    sel = sel.astype(BF16)
    acc = None
    for p in _pieces(b, n):
        t = _dot(sel, p)
        acc = t if acc is None else acc + t
    return acc


def _sigmoid(x):
    return 1.0 / (1.0 + jnp.exp(-x))


def _softplus(x):
    return jnp.maximum(x, 0.0) + jnp.log(1.0 + jnp.exp(-jnp.abs(x)))


def _params(sem=None, vmem=VMEM_LIMIT):
    kw = dict(vmem_limit_bytes=vmem)
    if sem is not None:
        kw["dimension_semantics"] = sem
    return pltpu.CompilerParams(**kw)


def _full(shape):
    nd = len(shape)
    return pl.BlockSpec(shape, lambda *_: (0,) * nd)


def group_weights(wt):
    return dict(
        gate=wt[10528:13600],
        sgu=wt[2304:5376],
        att=jnp.concatenate([wt[0:1024], wt[1280:2304], wt[1024:1280]], axis=0),
        ssd=jnp.concatenate([wt[5376:10496], wt[10496:10528], jnp.zeros((224, D), wt.dtype)], axis=0))


def ungroup_grads(g):
    a, s = g["att"], g["ssd"]
    return jnp.concatenate([a[0:1024], a[2048:2304], a[1024:2048], g["sgu"], s[0:5152], g["gate"]], axis=0)


def _bucket_table():
    qi = np.arange(L)[:, None]
    kj = np.arange(2 * L)[None, :]
    dist = np.maximum(qi + L - kj, 0)
    dist_f = np.maximum(dist, 1).astype(np.float32)
    large = 16 + (np.log(dist_f / np.float32(16)) / np.float32(math.log(128 / 16)) * np.float32(16)).astype(np.int32)
    large = np.minimum(large, 31)
    return np.where(dist < 16, dist, large).astype(np.int32)


def bias_table(rel_bias):
    buckets = jnp.asarray(_bucket_table().reshape(1, L * 2 * L))

    def body(rb_ref, bk_ref, out_ref):
        onehot = (lax.broadcasted_iota(jnp.int32, (32, L * 2 * L), 0) == bk_ref[...]).astype(F32)
        out_ref[...] = lax.dot_general(rb_ref[...], onehot, (((0,), (0,)), ((), ())),
                                       preferred_element_type=F32, precision=lax.Precision.HIGHEST)

    out = pl.pallas_call(
        body, name="bias_table",
        out_shape=jax.ShapeDtypeStruct((HEADS, L * 2 * L), F32),
        compiler_params=_params(),
    )(rel_bias, buckets)
    return out.reshape(HEADS, L, 2 * L)


def bias_grad(dbias):
    buckets = jnp.asarray(_bucket_table().reshape(1, L * 2 * L))

    def body(db_ref, bk_ref, out_ref):
        onehot = (lax.broadcasted_iota(jnp.int32, (32, L * 2 * L), 0) == bk_ref[...]).astype(F32)
        out_ref[...] = lax.dot_general(onehot, db_ref[...], (((1,), (1,)), ((), ())),
                                       preferred_element_type=F32, precision=lax.Precision.HIGHEST)

    return pl.pallas_call(
        body, name="bias_grad",
        out_shape=jax.ShapeDtypeStruct((32, HEADS), F32),
        compiler_params=_params(),
    )(dbias.reshape(HEADS, L * 2 * L), buckets)


def _row_tile(S):
    return 1024 if S % 1024 == 0 else 512


def inproj_first(x, g_pre, wt, tn, name):
    S, W = x.shape[0], wt.shape[0]
    tm = _row_tile(S)

    def body(x_ref, g_ref, w_ref, o_ref, h_ref):
        @pl.when(pl.program_id(1) == 0)
        def _():
            xv = x_ref[...]
            r = lax.rsqrt(jnp.mean(xv * xv, axis=-1, keepdims=True) + EPS)
            h_ref[...] = (xv * r * g_ref[...]).astype(BF16)
        o_ref[...] = _dot_nt(h_ref[...], w_ref[...]).astype(BF16)

    return pl.pallas_call(
        body, name=name, grid=(S // tm, W // tn),
        in_specs=[pl.BlockSpec((tm, D), lambda i, j: (i, 0)), _full((1, D)),
                  pl.BlockSpec((tn, D), lambda i, j: (j, 0))],
        out_specs=[pl.BlockSpec((tm, tn), lambda i, j: (i, j)), pl.BlockSpec((tm, D), lambda i, j: (i, 0))],
        out_shape=[jax.ShapeDtypeStruct((S, W), BF16), jax.ShapeDtypeStruct((S, D), BF16)],
        compiler_params=_params(("arbitrary", "arbitrary")),
    )(x, g_pre, wt)


def inproj_group(h, wt, tn, name, dtype):
    S, W = h.shape[0], wt.shape[0]
    tm = _row_tile(S)

    def body(h_ref, w_ref, o_ref):
        o_ref[...] = _dot_nt(h_ref[...], w_ref[...]).astype(dtype)

    return pl.pallas_call(
        body, name=name, grid=(S // tm, W // tn),
        in_specs=[pl.BlockSpec((tm, D), lambda i, j: (i, 0)), pl.BlockSpec((tn, D), lambda i, j: (j, 0))],
        out_specs=pl.BlockSpec((tm, tn), lambda i, j: (i, j)),
        out_shape=jax.ShapeDtypeStruct((S, W), dtype),
        compiler_params=_params(("arbitrary", "arbitrary")),
    )(h, wt)


def dh_group(dp, wt, acc, tk, name):
    S, W = dp.shape
    tm = _row_tile(S)

    def body(*refs):
        dp_ref, w_ref, o_ref = refs[0], refs[1], refs[-1]
        first = pl.program_id(1) == 0
        if acc is None:
            @pl.when(first)
            def _():
                o_ref[...] = jnp.zeros_like(o_ref)
        else:
            @pl.when(first)
            def _():
                o_ref[...] = refs[2][...]
        o_ref[...] += _dot(dp_ref[...], w_ref[...])

    row = pl.BlockSpec((tm, D), lambda i, k: (i, 0))
    return pl.pallas_call(
        body, name=name, grid=(S // tm, W // tk),
        in_specs=[pl.BlockSpec((tm, tk), lambda i, k: (i, k)), pl.BlockSpec((tk, D), lambda i, k: (k, 0))]
        + ([] if acc is None else [row]),
        out_specs=row, out_shape=jax.ShapeDtypeStruct((S, D), F32),
        input_output_aliases={} if acc is None else {2: 0},
        compiler_params=_params(("arbitrary", "arbitrary")),
    )(*((dp, wt) if acc is None else (dp, wt, acc)))


def dh_last(dp, wt, acc_in, x, g_pre, dy, tk, name):
    S, W = dp.shape
    tm = 512
    nk = W // tk

    def body(dp_ref, w_ref, a_ref, x_ref, g_ref, dy_ref, dx_ref, dg_ref, acc):
        i, k = pl.program_id(0), pl.program_id(1)

        @pl.when(k == 0)
        def _():
            acc[...] = a_ref[...]

        acc[...] += _dot(dp_ref[...], w_ref[...])

        @pl.when((k == nk - 1) & (i == 0))
        def _():
            dg_ref[...] = jnp.zeros_like(dg_ref)

        @pl.when(k == nk - 1)
        def _():
            xv = x_ref[...]
            dh = acc[...]
            g = g_ref[...]
            r = lax.rsqrt(jnp.mean(xv * xv, axis=-1, keepdims=True) + EPS)
            dhg = dh * g
            dx_ref[...] = dy_ref[...] + r * dhg - xv * (r * r * r) * jnp.mean(dhg * xv, axis=-1, keepdims=True)
            dg_ref[...] += jnp.sum(dh * xv * r, axis=0, keepdims=True)

    row = pl.BlockSpec((tm, D), lambda i, k: (i, 0))
    return pl.pallas_call(
        body, name=name, grid=(S // tm, nk),
        in_specs=[pl.BlockSpec((tm, tk), lambda i, k: (i, k)), pl.BlockSpec((tk, D), lambda i, k: (k, 0)),
                  row, row, _full((1, D)), row],
        out_specs=[row, _full((1, D))],
        out_shape=[jax.ShapeDtypeStruct((S, D), F32), jax.ShapeDtypeStruct((1, D), F32)],
        scratch_shapes=[pltpu.VMEM((tm, D), F32)],
        compiler_params=_params(("arbitrary", "arbitrary")),
    )(dp, wt, acc_in, x, g_pre, dy)


def dw_group(dp, h, tn, name, ts=512):
    S, W = dp.shape

    def body(dp_ref, h_ref, o_ref):
        @pl.when(pl.program_id(1) == 0)
        def _():
            o_ref[...] = jnp.zeros_like(o_ref)
        o_ref[...] += _dot_tn(dp_ref[...].astype(F32), h_ref[...])

    return pl.pallas_call(
        body, name=name, grid=(W // tn, S // ts),
        in_specs=[pl.BlockSpec((ts, tn), lambda j, s: (s, j)), pl.BlockSpec((ts, D), lambda j, s: (s, 0))],
        out_specs=pl.BlockSpec((tn, D), lambda j, s: (j, 0)),
        out_shape=jax.ShapeDtypeStruct((W, D), F32),
        compiler_params=_params(("arbitrary", "arbitrary")),
    )(dp, h)


def matmul_tn(a, b, name, tn=512, ts=512):
    S, K = a.shape
    N = b.shape[1]
    ns = S // ts

    def body(a_ref, b_ref, o_ref):
        @pl.when(pl.program_id(1) == 0)
        def _():
            o_ref[...] = jnp.zeros_like(o_ref)
        o_ref[...] += _dot_tn(a_ref[...].astype(F32), b_ref[...])

    return pl.pallas_call(
        body, name=name, grid=(N // tn, ns),
        in_specs=[pl.BlockSpec((ts, K), lambda j, s: (s, 0)), pl.BlockSpec((ts, tn), lambda j, s: (s, j))],
        out_specs=pl.BlockSpec((K, tn), lambda j, s: (0, j)),
        out_shape=jax.ShapeDtypeStruct((K, N), F32),
        compiler_params=_params(("arbitrary", "arbitrary")),
    )(a, b)


def _att_mask(n):
    qi = lax.broadcasted_iota(jnp.int32, (L, 2 * L), 0)
    kj = lax.broadcasted_iota(jnp.int32, (L, 2 * L), 1)
    dist = qi + L - kj
    return (dist >= 0) & (dist < L) & ((kj >= L) | (n > 0))


def _att_in_specs(nb):
    last = nb - 1
    cur = lambda n: jnp.minimum(n, last)
    prev = lambda n: jnp.maximum(jnp.minimum(n, last) - 1, 0)
    return [
        pl.BlockSpec((L, 1024), lambda n: (cur(n), 0)),
        pl.BlockSpec((L, 128), lambda n: (prev(n), 16)),
        pl.BlockSpec((L, 128), lambda n: (cur(n), 16)),
        pl.BlockSpec((L, 128), lambda n: (prev(n), 17)),
        pl.BlockSpec((L, 128), lambda n: (cur(n), 17)),
        pl.BlockSpec((L, 1024), lambda n: (cur(n), 1)),
        _full((HEADS, L, 2 * L)),
        pl.BlockSpec(memory_space=pltpu.SMEM),
    ]


GH = HEADS // KV
GB = 8


def _att_mask_rows(n, nh):
    qi = lax.broadcasted_iota(jnp.int32, (nh * L, 2 * L), 0) & (L - 1)
    kj = lax.broadcasted_iota(jnp.int32, (nh * L, 2 * L), 1)
    dist = qi + L - kj
    return (dist >= 0) & (dist < L) & ((kj >= L) | (n > 0))


def _stack_heads(ref, h0, nh, scr):
    for g in range(nh):
        scr[(h0 + g) * L:(h0 + g + 1) * L, :] = ref[:, (h0 + g) * DH:(h0 + g + 1) * DH].astype(F32)
    return scr[h0 * L:(h0 + nh) * L, :]


def _unstack_heads(val, h0, nh, ref):
    for g in range(nh):
        ref[:, (h0 + g) * DH:(h0 + g + 1) * DH] = val[g * L:(g + 1) * L, :]


def _sink_rows(s_ref, h0, nh):
    return jnp.concatenate([jnp.full((L, 1), s_ref[h0 + g], F32) for g in range(nh)], axis=0)


def _att_probs(qh, kk, bias_h, mask, sk):
    logits = _dot_nt(qh, kk) + bias_h
    logits = jnp.where(mask, logits, NEG)
    m = jnp.maximum(jnp.max(logits, axis=-1, keepdims=True), sk)
    p = jnp.exp(logits - m)
    es = jnp.exp(sk - m)
    den = jnp.sum(p, axis=-1, keepdims=True) + es
    return p / den, es / den


def att_fwd(proj, bias, sinks):
    S = proj.shape[0]
    nb = S // L

    def body(q_ref, kp_ref, kc_ref, vp_ref, vc_ref, z_ref, bias_ref, s_ref, y_ref, o_scr):
        mask = _att_mask(pl.program_id(0))
        for kv in range(KV):
            sl = slice(kv * DH, (kv + 1) * DH)
            kk = jnp.concatenate([kp_ref[:, sl], kc_ref[:, sl]], axis=0).astype(BF16)
            vv = jnp.concatenate([vp_ref[:, sl], vc_ref[:, sl]], axis=0).astype(BF16)
            for g in range(GH):
                h = kv * GH + g
                hs = slice(h * DH, (h + 1) * DH)
                qh = (q_ref[:, hs] * 0.125).astype(BF16)
                P, _ = _att_probs(qh, kk, bias_ref[h], mask, s_ref[h])
                o_scr[:, hs] = _dot(P.astype(BF16), vv)
        z = z_ref[...].astype(F32)
        y_ref[...] = (o_scr[...] * (z * _sigmoid(z))).astype(BF16)

    return pl.pallas_call(
        body, name="att_fwd", grid=(nb,),
        in_specs=_att_in_specs(nb),
        out_specs=pl.BlockSpec((L, 1024), lambda n: (n, 0)),
        out_shape=jax.ShapeDtypeStruct((S, 1024), BF16),
        scratch_shapes=[pltpu.VMEM((L, 1024), F32)],
        compiler_params=_params(("arbitrary",)),
    )(proj, proj, proj, proj, proj, proj, bias, sinks)


def att_bwd(dy, proj, bias, sinks):
    S = proj.shape[0]
    nb = S // L
    last = nb - 1

    def body(dy_ref, q_ref, kp_ref, kc_ref, vp_ref, vc_ref, z_ref, bias_ref, s_ref,
             dout_ref, dbias_ref, dsink_ref, carry, band, dq_scr, dz_scr, qs_scr, zs_scr, dys_scr):
        n = pl.program_id(0)

        @pl.when(n == 0)
        def _():
            carry[...] = jnp.zeros_like(carry)
            dq_scr[...] = jnp.zeros_like(dq_scr)
            dz_scr[...] = jnp.zeros_like(dz_scr)
            dbias_ref[...] = jnp.zeros_like(dbias_ref)
            dsink_ref[...] = jnp.zeros_like(dsink_ref)

        dout_ref[:, 0:1024] = dq_scr[...].astype(BF16)
        dout_ref[:, 1024:2048] = dz_scr[...].astype(BF16)
        band[...] = jnp.zeros_like(band)

        @pl.when(n < nb)
        def _():
            mask = _att_mask_rows(n, GB)
            lane = lax.broadcasted_iota(jnp.int32, (1, 128), 1)
            dsink = jnp.zeros((1, 128), F32)
            for kv in range(KV):
                sl = slice(kv * DH, (kv + 1) * DH)
                kk = jnp.concatenate([kp_ref[:, sl], kc_ref[:, sl]], axis=0).astype(BF16)
                vv = jnp.concatenate([vp_ref[:, sl], vc_ref[:, sl]], axis=0).astype(BF16)
                dk_acc = jnp.zeros((2 * L, DH), F32)
                dv_acc = jnp.zeros((2 * L, DH), F32)
                for h0 in range(kv * GH, (kv + 1) * GH, GB):
                    qs = (_stack_heads(q_ref, h0, GB, qs_scr) * 0.125).astype(BF16)
                    bias_g = bias_ref[h0:h0 + GB].reshape(GB * L, 2 * L)
                    P, psink = _att_probs(qs, kk, bias_g, mask, _sink_rows(s_ref, h0, GB))
                    zs = _stack_heads(z_ref, h0, GB, zs_scr)
                    dys = _stack_heads(dy_ref, h0, GB, dys_scr)
                    sg = _sigmoid(zs)
                    O = _dot(P.astype(BF16), vv)
                    _unstack_heads(dys * O * (sg * (1.0 + zs * (1.0 - sg))), h0, GB, dz_scr)
                    dOb = (dys * (zs * sg)).astype(BF16)
                    dP = _dot_nt(dOb, vv)
                    delta = jnp.sum(P * dP, axis=-1, keepdims=True)
                    dS = P * (dP - delta)
                    sd = psink * delta
                    for g in range(GB):
                        dsink = dsink + jnp.where(lane == h0 + g, -jnp.sum(sd[g * L:(g + 1) * L, :]), 0.0)
                    _unstack_heads(_dot(dS.astype(BF16), kk) * 0.125, h0, GB, dq_scr)
                    dbias_ref[h0:h0 + GB] += dS.reshape(GB, L, 2 * L)
                    dk_acc = dk_acc + _dot_tn(dS, qs)
                    dv_acc = dv_acc + _dot_tn(P, dOb)
                band[:, sl] = dk_acc
                band[:, 128 + kv * DH:128 + (kv + 1) * DH] = dv_acc
            dsink_ref[...] += dsink

        out = carry[...] + band[0:L, :]
        dout_ref[:, 2048:2304] = out.astype(BF16)
        carry[...] = band[L:2 * L, :]

    cur = lambda n: jnp.minimum(n, last)
    lag = lambda n: jnp.maximum(n - 1, 0)
    return pl.pallas_call(
        body, name="att_bwd", grid=(nb + 1,),
        in_specs=[pl.BlockSpec((L, 1024), lambda n: (cur(n), 0))] + _att_in_specs(nb),
        out_specs=[pl.BlockSpec((L, 2304), lambda n: (lag(n), 0)), _full((HEADS, L, 2 * L)), _full((1, 128))],
        out_shape=[jax.ShapeDtypeStruct((S, 2304), BF16),
                   jax.ShapeDtypeStruct((HEADS, L, 2 * L), F32), jax.ShapeDtypeStruct((1, 128), F32)],
        scratch_shapes=[pltpu.VMEM((L, 256), F32), pltpu.VMEM((2 * L, 256), F32),
                        pltpu.VMEM((L, 1024), F32), pltpu.VMEM((L, 1024), F32)]
        + [pltpu.VMEM((HEADS * L, DH), F32)] * 3,
        compiler_params=_params(("arbitrary",)),
    )(dy, proj, proj, proj, proj, proj, proj, bias, sinks)


def _sgu_in_specs():
    return [
        pl.BlockSpec((L, 1024), lambda c: (c, 0)),
        pl.BlockSpec((L, 1024), lambda c: (c, 1)),
        pl.BlockSpec((L, 1024), lambda c: (c, 2)),
        _full((1, 1024)), _full((1, 1024)), _full((8, L, L)), _full((L, 8)),
    ]


def _sgu_norm(v, lg, lb):
    mu = jnp.mean(v, axis=-1, keepdims=True)
    vc = v - mu
    rstd = lax.rsqrt(jnp.mean(vc * vc, axis=-1, keepdims=True) + EPS)
    xhat = vc * rstd
    return xhat * lg + lb, xhat, rstd


def _tril():
    return lax.broadcasted_iota(jnp.int32, (L, L), 0) >= lax.broadcasted_iota(jnp.int32, (L, L), 1)


def sgu_fwd(proj, ln_g, ln_b, w, b_t):
    S = proj.shape[0]

    def body(u_ref, v_ref, z_ref, lg_ref, lb_ref, w_ref, bt_ref, y_ref):
        vn, _, _ = _sgu_norm(v_ref[...].astype(F32), lg_ref[...], lb_ref[...])
        tri = _tril()
        parts = []
        for g in range(8):
            wg = jnp.where(tri, w_ref[g], 0.0).astype(BF16)
            parts.append(_dot(wg, vn[:, g * 128:(g + 1) * 128].astype(BF16)) + bt_ref[:, g:g + 1])
        mixed = jnp.concatenate(parts, axis=1)
        z = z_ref[...].astype(F32)
        y_ref[...] = (u_ref[...].astype(F32) * mixed * (z * _sigmoid(z))).astype(BF16)

    return pl.pallas_call(
        body, name="sgu_fwd", grid=(S // L,),
        in_specs=_sgu_in_specs(),
        out_specs=pl.BlockSpec((L, 1024), lambda c: (c, 0)),
        out_shape=jax.ShapeDtypeStruct((S, 1024), BF16),
        compiler_params=_params(("arbitrary",)),
    )(proj, proj, proj, ln_g, ln_b, w, b_t)


def sgu_bwd(dy, proj, ln_g, ln_b, w, b_t):
    S = proj.shape[0]

    def body(dy_ref, u_ref, v_ref, z_ref, lg_ref, lb_ref, w_ref, bt_ref,
             dout_ref, dw_ref, dbt_ref, dlg_ref, dlb_ref):
        @pl.when(pl.program_id(0) == 0)
        def _():
            dw_ref[...] = jnp.zeros_like(dw_ref)
            dbt_ref[...] = jnp.zeros_like(dbt_ref)
            dlg_ref[...] = jnp.zeros_like(dlg_ref)
            dlb_ref[...] = jnp.zeros_like(dlb_ref)

        lg = lg_ref[...]
        vn, xhat, rstd = _sgu_norm(v_ref[...].astype(F32), lg, lb_ref[...])
        tri = _tril()
        lane = lax.broadcasted_iota(jnp.int32, (L, 128), 1)
        wgs, parts = [], []
        for g in range(8):
            wg = jnp.where(tri, w_ref[g], 0.0)
            wgs.append(wg)
            parts.append(_dot(wg.astype(BF16), vn[:, g * 128:(g + 1) * 128].astype(BF16)) + bt_ref[:, g:g + 1])
        mixed = jnp.concatenate(parts, axis=1)
        z = z_ref[...].astype(F32)
        sg = _sigmoid(z)
        silu = z * sg
        dy_v = dy_ref[...]
        u = u_ref[...].astype(F32)
        dout_ref[:, 0:1024] = (dy_v * mixed * silu).astype(BF16)
        dout_ref[:, 2048:3072] = (dy_v * u * mixed * (sg * (1.0 + z * (1.0 - sg)))).astype(BF16)
        dmixed = dy_v * u * silu
        dbt = jnp.zeros((L, 128), F32)
        dvn_parts = []
        for g in range(8):
            dm = dmixed[:, g * 128:(g + 1) * 128]
            dmb = dm.astype(BF16)
            dbt = dbt + jnp.where(lane == g, jnp.sum(dm, axis=1, keepdims=True), 0.0)
            dw_ref[g] += jnp.where(tri, _dot_nt(dmb, vn[:, g * 128:(g + 1) * 128].astype(BF16)), 0.0)
            dvn_parts.append(_dot_tn(wgs[g], dmb))
        dbt_ref[...] += dbt
        dvn = jnp.concatenate(dvn_parts, axis=1)
        dlg_ref[...] += jnp.sum(dvn * xhat, axis=0, keepdims=True)
        dlb_ref[...] += jnp.sum(dvn, axis=0, keepdims=True)
        dxh = dvn * lg
        dv = rstd * (dxh - jnp.mean(dxh, axis=-1, keepdims=True)
                     - xhat * jnp.mean(dxh * xhat, axis=-1, keepdims=True))
        dout_ref[:, 1024:2048] = dv.astype(BF16)

    return pl.pallas_call(
        body, name="sgu_bwd", grid=(S // L,),
        in_specs=[pl.BlockSpec((L, 1024), lambda c: (c, 0))] + _sgu_in_specs(),
        out_specs=[pl.BlockSpec((L, 3072), lambda c: (c, 0)), _full((8, L, L)), _full((L, 128)),
                   _full((1, 1024)), _full((1, 1024))],
        out_shape=[jax.ShapeDtypeStruct((S, 3072), BF16), jax.ShapeDtypeStruct((8, L, L), F32),
                   jax.ShapeDtypeStruct((L, 128), F32), jax.ShapeDtypeStruct((1, 1024), F32),
                   jax.ShapeDtypeStruct((1, 1024), F32)],
        compiler_params=_params(("arbitrary",)),
    )(dy, proj, proj, proj, ln_g, ln_b, w, b_t)


def _expand_matrix():
    r = lax.broadcasted_iota(jnp.int32, (128, SSM_W), 0)
    c = lax.broadcasted_iota(jnp.int32, (128, SSM_W), 1)
    return (c // SSM_P) == r


def _expand_matrix_t():
    r = lax.broadcasted_iota(jnp.int32, (SSM_W, 128), 0)
    c = lax.broadcasted_iota(jnp.int32, (SSM_W, 128), 1)
    return (r // SSM_P) == c


def _rows_from(ref, start):
    C = ref.shape[1]
    tiles = ref[...].reshape(17, 8, C)
    q, s = divmod(start, 8)
    if s == 0:
        return tiles[q:q + 16].reshape(L, C)
    rolled = pltpu.roll(tiles, 8 - s, axis=1)
    sub = lax.broadcasted_iota(jnp.int32, (16, 8, C), 1)
    return jnp.where(sub < 8 - s, rolled[q:q + 16], rolled[q + 1:q + 17]).reshape(L, C)


def _ssd_common(ext_ref, cw_ref, cb_ref, dt_raw, dtb, alog):
    taps = [_rows_from(ext_ref, 5 + k) for k in range(CONV_K)]
    pre = cb_ref[...]
    for k in range(CONV_K):
        pre = pre + cw_ref[k:k + 1, :] * taps[k]
    sg_pre = _sigmoid(pre)
    xc = pre * sg_pre
    dt = _softplus(dt_raw + dtb)
    a = -jnp.exp(alog)
    adt = dt * a
    acs = _sel_dot(_tril(), adt, 3)
    return pre, sg_pre, xc, dt, a, acs, taps


def _ssd_in_specs(rev, nc):
    cidx = (lambda c: nc - 1 - c) if rev else (lambda c: c)
    return [
        pl.BlockSpec((L, 2048), lambda c: (cidx(c), 0)),
        pl.BlockSpec((L, 1024), lambda c: (cidx(c), 2)),
        pl.BlockSpec((L, 1024), lambda c: (cidx(c), 3)),
        pl.BlockSpec((L, 1024), lambda c: (cidx(c), 4)),
        pl.BlockSpec((L, 128), lambda c: (cidx(c), 40)),
        _full((8, CONV_C)), _full((1, CONV_C)), _full((1, 128)), _full((1, 128)), _full((1, 128)),
        _full((1, SSM_W)),
    ]


def ssd_fwd(proj, conv_w, conv_b, dt_bias, a_log, d_skip, norm_g):
    S = proj.shape[0]
    nc = S // L

    def body(z_ref, xa_ref, xb_ref, xc_ref, dt_ref, cw_ref, cb_ref, dtb_ref, alog_ref, dsk_ref, ng_ref,
             y_ref, hs_ref, H, ext, ysc):
        @pl.when(pl.program_id(0) == 0)
        def _():
            H[...] = jnp.zeros_like(H)
            ext[0:8, :] = jnp.zeros((8, CONV_C), F32)

        for k, ref in enumerate((xa_ref, xb_ref, xc_ref)):
            ext[8:8 + L, k * 1024:(k + 1) * 1024] = ref[...].astype(F32)
        pre, sg_pre, xc, dt, a, acs, _ = _ssd_common(ext, cw_ref, cb_ref, dt_ref[...].astype(F32), dtb_ref[...],
                                                     alog_ref[...])
        for k, ref in enumerate((xa_ref, xb_ref, xc_ref)):
            ext[0:8, k * 1024:(k + 1) * 1024] = ref[L - 8:L, :].astype(F32)
        xs = xc[:, 0:SSM_W]
        acs_t = acs.T
        ex = _expand_matrix()
        dt_x = _dot_sel(dt, ex, 2)
        xdt = xs * dt_x
        eacs_x = _dot_sel(jnp.exp(acs), ex, 2)
        xw = xdt * _dot_sel(jnp.exp(acs[L - 1:L, :] - acs), ex, 2)
        cd_row = jnp.exp(acs[L - 1:L, :])
        hs_ref[0] = H[...]
        tri = _tril()
        for g in range(SSM_G):
            gs = slice(g * 512, (g + 1) * 512)
            bg = xc[:, SSM_W + g * SSM_N:SSM_W + (g + 1) * SSM_N].astype(BF16)
            cg = xc[:, SSM_W + 512 + g * SSM_N:SSM_W + 512 + (g + 1) * SSM_N].astype(BF16)
            G = _dot_nt(cg, bg)
            yoff = _dot_nt(cg, H[gs, :].astype(BF16)) * eacs_x[:, gs]
            Sg = _dot_tn(xw[:, gs], bg)
            for j in range(8):
                hh = g * 8 + j
                hs = slice(hh * SSM_P, (hh + 1) * SSM_P)
                seg = acs[:, hh:hh + 1] - acs_t[hh:hh + 1, :]
                dk = jnp.where(tri, jnp.exp(jnp.minimum(seg, 0.0)), 0.0)
                yd = _dot((G * dk).astype(BF16), xdt[:, hs].astype(BF16))
                ysc[:, hs] = yd + yoff[:, j * SSM_P:(j + 1) * SSM_P]
                H[hs, :] = H[hs, :] * cd_row[:, hh:hh + 1] + Sg[j * SSM_P:(j + 1) * SSM_P, :]
        d_x = _dot_sel(jnp.broadcast_to(dsk_ref[...], (8, 128)), ex, 3)[0:1, :]
        Y = ysc[...] + d_x * xs
        z = z_ref[...].astype(F32)
        yz = Y * (z * _sigmoid(z))
        ng = ng_ref[...]
        for g in range(SSM_G):
            gs = slice(g * 512, (g + 1) * 512)
            t = yz[:, gs]
            rstd = lax.rsqrt(jnp.mean(t * t, axis=-1, keepdims=True) + EPS)
            y_ref[:, gs] = (t * rstd * ng[:, gs]).astype(BF16)

    return pl.pallas_call(
        body, name="ssd_fwd", grid=(nc,),
        in_specs=_ssd_in_specs(False, nc),
        out_specs=[pl.BlockSpec((L, SSM_W), lambda c: (c, 0)), pl.BlockSpec((1, SSM_W, SSM_N), lambda c: (c, 0, 0))],
        out_shape=[jax.ShapeDtypeStruct((S, SSM_W), BF16), jax.ShapeDtypeStruct((nc, SSM_W, SSM_N), F32)],
        scratch_shapes=[pltpu.VMEM((SSM_W, SSM_N), F32), pltpu.VMEM((8 + L, CONV_C), F32),
                        pltpu.VMEM((L, SSM_W), F32)],
        compiler_params=_params(("arbitrary",)),
    )(proj, proj, proj, proj, proj, conv_w, conv_b, dt_bias, a_log, d_skip, norm_g)


def ssd_bwd(dy, proj, hstates, conv_w, conv_b, dt_bias, a_log, d_skip, norm_g):
    S = proj.shape[0]
    nc = S // L
    cidx = lambda c: nc - 1 - c

    def body(dy_ref, z_ref, xa_ref, xb_ref, xc_ref, dt_ref, cw_ref, cb_ref, dtb_ref, alog_ref, dsk_ref, ng_ref,
             pa_ref, pb_ref, pc_ref, hp_ref,
             dout_ref, dcw_ref, dcb_ref, ddtb_ref, dalog_ref, ddsk_ref, dng_ref,
             dH, ext, dext, ysc, yoffsc, dxdt, dxc, tsc):
        step = pl.program_id(0)
        c = nc - 1 - step

        @pl.when(step == 0)
        def _():
            dH[...] = jnp.zeros_like(dH)
            dext[L:L + 8, :] = jnp.zeros((8, CONV_C), F32)
            for r in (dcw_ref, dcb_ref, ddtb_ref, dalog_ref, ddsk_ref, dng_ref):
                r[...] = jnp.zeros_like(r)

        for k, (ref, prev) in enumerate(((xa_ref, pa_ref), (xb_ref, pb_ref), (xc_ref, pc_ref))):
            ext[0:8, k * 1024:(k + 1) * 1024] = jnp.where(c > 0, prev[8:16, :].astype(F32), 0.0)
            ext[8:8 + L, k * 1024:(k + 1) * 1024] = ref[...].astype(F32)
        dtb = dtb_ref[...]
        dt_raw = dt_ref[...].astype(F32)
        pre, sg_pre, xc, dt, a, acs, taps = _ssd_common(ext, cw_ref, cb_ref, dt_raw, dtb, alog_ref[...])
        xs = xc[:, 0:SSM_W]
        acs_t = acs.T
        ex = _expand_matrix()
        dt_x = _dot_sel(dt, ex, 2)
        xdt = xs * dt_x
        eacs_x = _dot_sel(jnp.exp(acs), ex, 2)
        dte_x = _dot_sel(jnp.exp(acs[L - 1:L, :] - acs), ex, 2)
        xw = xdt * dte_x
        cd_row = jnp.exp(acs[L - 1:L, :])
        tri = _tril()

        Gs, Cs, Bs = [], [], []
        for g in range(SSM_G):
            gs = slice(g * 512, (g + 1) * 512)
            bg = xc[:, SSM_W + g * SSM_N:SSM_W + (g + 1) * SSM_N].astype(BF16)
            cg = xc[:, SSM_W + 512 + g * SSM_N:SSM_W + 512 + (g + 1) * SSM_N].astype(BF16)
            G = _dot_nt(cg, bg)
            Gs.append(G), Cs.append(cg), Bs.append(bg)
            yoffsc[:, gs] = _dot_nt(cg, hp_ref[0, gs, :].astype(BF16)) * eacs_x[:, gs]
            for j in range(8):
                hh = g * 8 + j
                hs = slice(hh * SSM_P, (hh + 1) * SSM_P)
                seg = acs[:, hh:hh + 1] - acs_t[hh:hh + 1, :]
                dk = jnp.where(tri, jnp.exp(jnp.minimum(seg, 0.0)), 0.0)
                ysc[:, hs] = _dot((G * dk).astype(BF16), xdt[:, hs].astype(BF16))
        d_x = _dot_sel(jnp.broadcast_to(dsk_ref[...], (8, 128)), ex, 3)[0:1, :]
        yoff = yoffsc[...]
        Y = ysc[...] + yoff + d_x * xs

        z = z_ref[...].astype(F32)
        sgz = _sigmoid(z)
        silu_z = z * sgz
        yz = Y * silu_z
        ng = ng_ref[...]
        dout = dy_ref[...]
        dyn = dout * ng
        dyz_parts, dng_parts = [], []
        for g in range(SSM_G):
            gs = slice(g * 512, (g + 1) * 512)
            t = yz[:, gs]
            rstd = lax.rsqrt(jnp.mean(t * t, axis=-1, keepdims=True) + EPS)
            dng_parts.append(jnp.sum(dout[:, gs] * t * rstd, axis=0, keepdims=True))
            dn = dyn[:, gs]
            dyz_parts.append(rstd * dn - t * (rstd * rstd * rstd) * jnp.mean(dn * t, axis=-1, keepdims=True))
        dng_ref[...] += jnp.concatenate(dng_parts, axis=1)
        dyz = jnp.concatenate(dyz_parts, axis=1)
        dY = dyz * silu_z
        dout_ref[:, 0:SSM_W] = (dyz * Y * (sgz * (1.0 + z * (1.0 - sgz)))).astype(BF16)

        ex_t = _expand_matrix_t()
        ddsk_ref[...] += _dot_sel(jnp.broadcast_to(jnp.sum(dY * xs, axis=0, keepdims=True), (8, SSM_W)), ex_t, 3)[0:1, :]

        lane = lax.broadcasted_iota(jnp.int32, (L, 128), 1)
        subl = lax.broadcasted_iota(jnp.int32, (128, L), 0)
        coll = lax.broadcasted_iota(jnp.int32, (128, L), 1)
        r_cols = jnp.zeros((L, 128), F32)
        c_rows = jnp.zeros((128, L), F32)
        for g in range(SSM_G):
            gs = slice(g * 512, (g + 1) * 512)
            G, cg, bg = Gs[g], Cs[g], Bs[g]
            hp_g = hp_ref[0, gs, :]
            dh_g = dH[gs, :]
            dY_g = dY[:, gs]
            dZ = dY_g * eacs_x[:, gs]
            dZb = dZ.astype(BF16)
            dC = _dot(dZb, hp_g.astype(BF16))
            dh_from_off = _dot_tn(dZ, cg)
            dhb = dh_g.astype(BF16)
            Q = _dot_nt(bg, dhb)
            dB = _dot(xw[:, gs].astype(BF16), dhb)
            qd = Q * dte_x[:, gs]
            dxdt[:, gs] = qd
            tsc[:, gs] = qd * xdt[:, gs]
            dG = jnp.zeros((L, L), F32)
            for j in range(8):
                hh = g * 8 + j
                hs = slice(hh * SSM_P, (hh + 1) * SSM_P)
                seg = acs[:, hh:hh + 1] - acs_t[hh:hh + 1, :]
                dk = jnp.where(tri, jnp.exp(jnp.minimum(seg, 0.0)), 0.0)
                M = G * dk
                dYh = dY[:, hs]
                dYhb = dYh.astype(BF16)
                dM = _dot_nt(dYhb, xdt[:, hs].astype(BF16))
                dxdt[:, hs] += _dot_tn(M, dYhb)
                dG = dG + dM * dk
                Wm = dM * M
                r_cols = r_cols + jnp.where(lane == hh, jnp.sum(Wm, axis=1, keepdims=True), 0.0)
                c_rows = c_rows + jnp.where(subl == hh, jnp.sum(Wm, axis=0, keepdims=True), 0.0)
                pj = slice(j * SSM_P, (j + 1) * SSM_P)
                cd_h = cd_row[:, hh:hh + 1]
                dcd = jnp.sum(dh_g[pj, :] * hp_g[pj, :]) * cd_h
                c_rows = c_rows - jnp.where((subl == hh) & (coll == L - 1), dcd, 0.0)
                dH[hs, :] = dh_g[pj, :] * cd_h + dh_from_off[pj, :]
            dGb = dG.astype(BF16)
            dC = dC + _dot(dGb, bg)
            dB = dB + _dot_tn(dG, cg)
            dxc[:, SSM_W + g * SSM_N:SSM_W + (g + 1) * SSM_N] = dB
            dxc[:, SSM_W + 512 + g * SSM_N:SSM_W + 512 + (g + 1) * SSM_N] = dC

        row = lax.broadcasted_iota(jnp.int32, (L, 128), 0)
        tv = tsc[...]
        t_last = _dot_sel(jnp.broadcast_to(jnp.sum(tv, axis=0, keepdims=True), (8, SSM_W)), ex_t, 3)[0:1, :]
        dacs = (r_cols - c_rows.T + _dot_sel(dY * yoff - tv, ex_t, 2) + jnp.where(row == L - 1, t_last, 0.0))
        triu = lax.broadcasted_iota(jnp.int32, (L, L), 0) <= lax.broadcasted_iota(jnp.int32, (L, L), 1)
        dadt = _sel_dot(triu, dacs, 3)
        dxdt_v = dxdt[...]
        ddt = _dot_sel(dxdt_v * xs, ex_t, 2) + dadt * a
        dalog_ref[...] += jnp.sum(dadt * dt * a, axis=0, keepdims=True)
        ddt_raw = jnp.where(lane < SSM_H, ddt * _sigmoid(dt_raw + dtb), 0.0)
        ddtb_ref[...] += jnp.sum(ddt_raw, axis=0, keepdims=True)
        dout_ref[:, 5120:5248] = ddt_raw.astype(BF16)
        dout_ref[:, 5248:5376] = jnp.zeros((L, 128), BF16)

        dxc[:, 0:SSM_W] = dxdt_v * dt_x + d_x * dY
        dpre = dxc[...] * (sg_pre * (1.0 + pre * (1.0 - sg_pre)))
        dcb_ref[...] += jnp.sum(dpre, axis=0, keepdims=True)
        dext[0:L, :] = dpre
        x_cur = ext[8:8 + L, :]
        dx = None
        for k in range(CONV_K):
            dsh = _rows_from(dext, 3 - k)
            term = cw_ref[k:k + 1, :] * dsh
            dx = term if dx is None else dx + term
            dcw_ref[k:k + 1, :] += jnp.sum(dsh * x_cur, axis=0, keepdims=True)
        dout_ref[:, SSM_W:SSM_W + CONV_C] = dx.astype(BF16)
        dext[L:L + 8, :] = dpre[0:8, :]

    big = lambda w: pl.BlockSpec((L, w), lambda c: (cidx(c), 0))
    return pl.pallas_call(
        body, name="ssd_bwd", grid=(nc,),
        in_specs=[big(SSM_W)] + _ssd_in_specs(True, nc) + [
            pl.BlockSpec((16, 1024), lambda c, k=k: (jnp.maximum(8 * cidx(c) - 1, 0), k)) for k in (2, 3, 4)] + [
            pl.BlockSpec((1, SSM_W, SSM_N), lambda c: (cidx(c), 0, 0))],
        out_specs=[big(5376), _full((8, CONV_C)), _full((1, CONV_C)),
                   _full((1, 128)), _full((1, 128)), _full((1, 128)), _full((1, SSM_W))],
        out_shape=[jax.ShapeDtypeStruct((S, 5376), BF16), jax.ShapeDtypeStruct((8, CONV_C), F32),
                   jax.ShapeDtypeStruct((1, CONV_C), F32), jax.ShapeDtypeStruct((1, 128), F32),
                   jax.ShapeDtypeStruct((1, 128), F32), jax.ShapeDtypeStruct((1, 128), F32),
                   jax.ShapeDtypeStruct((1, SSM_W), F32)],
        scratch_shapes=[pltpu.VMEM((SSM_W, SSM_N), F32), pltpu.VMEM((8 + L, CONV_C), F32),
                        pltpu.VMEM((L + 8, CONV_C), F32), pltpu.VMEM((L, SSM_W), F32),
                        pltpu.VMEM((L, SSM_W), F32), pltpu.VMEM((L, SSM_W), F32),
                        pltpu.VMEM((L, CONV_C), F32), pltpu.VMEM((L, SSM_W), F32)],
        compiler_params=_params(("arbitrary",)),
    )(dy, proj, proj, proj, proj, proj, conv_w, conv_b, dt_bias, a_log, d_skip, norm_g, proj, proj, proj, hstates)


def _resident(shape):
    nd = len(shape)
    return pl.BlockSpec(shape, lambda *_: (0,) * nd, pipeline_mode=pl.Buffered(1))


def merge_fwd(y_att, y_sg, y_ssm, proj, x, w_a, w_s, w_m, w_o, g_post):
    S = x.shape[0]
    tm = 256

    def body(ya_ref, ys_ref, ym_ref, gate_ref, x_ref, wa_ref, ws_ref, wm_ref, wo_ref, gp_ref,
             xn_ref, bra_ref, brs_ref, brm_ref, mg_ref, out_ref):
        bra = _dot(ya_ref[...], wa_ref[...])
        brs = _dot(ys_ref[...], ws_ref[...])
        brm = _dot(ym_ref[...], wm_ref[...])
        bra_ref[...] = bra.astype(BF16)
        brs_ref[...] = brs.astype(BF16)
        brm_ref[...] = brm.astype(BF16)
        gate = gate_ref[...].astype(F32)
        merged = (_sigmoid(gate[:, 0:1024]) * bra + _sigmoid(gate[:, 1024:2048]) * brs
                  + _sigmoid(gate[:, 2048:3072]) * brm)
        mb = merged.astype(BF16)
        mg_ref[...] = mb
        o = _dot(mb, wo_ref[...])
        out_ref[...] = o
        r = lax.rsqrt(jnp.mean(o * o, axis=-1, keepdims=True) + EPS)
        xn_ref[...] = x_ref[...] + o * r * gp_ref[...]

    row = lambda w: pl.BlockSpec((tm, w), lambda i: (i, 0))
    return pl.pallas_call(
        body, name="merge_fwd", grid=(S // tm,),
        in_specs=[row(1024), row(1024), row(2048), pl.BlockSpec((tm, 3072), lambda i: (i, 0)),
                  row(D), _resident((1024, D)), _resident((1024, D)), _resident((2048, D)), _resident((D, D)),
                  _full((1, D))],
        out_specs=[row(D)] * 6,
        out_shape=[jax.ShapeDtypeStruct((S, D), F32)] + [jax.ShapeDtypeStruct((S, D), BF16)] * 4
        + [jax.ShapeDtypeStruct((S, D), F32)],
        compiler_params=_params(("arbitrary",)),
    )(y_att, y_sg, y_ssm, proj, x, w_a, w_s, w_m, w_o, g_post)


def merge_bwd(dy, out, g_post, proj, br_a, br_s, br_m, w_a, w_s, w_m, w_o):
    S = dy.shape[0]
    tm = 256

    def body(dy_ref, o_ref, gp_ref, gate_ref, bra_ref, brs_ref, brm_ref, wa_ref, ws_ref, wm_ref, wo_ref,
             dout_ref, dba_ref, dbs_ref, dbm_ref, dgate_ref, dya_ref, dys_ref, dym_ref, dgp_ref):
        @pl.when(pl.program_id(0) == 0)
        def _():
            dgp_ref[...] = jnp.zeros_like(dgp_ref)

        o = o_ref[...]
        dyv = dy_ref[...]
        r = lax.rsqrt(jnp.mean(o * o, axis=-1, keepdims=True) + EPS)
        dyg = dyv * gp_ref[...]
        do = r * dyg - o * (r * r * r) * jnp.mean(dyg * o, axis=-1, keepdims=True)
        dgp_ref[...] += jnp.sum(dyv * o * r, axis=0, keepdims=True)
        dob = do.astype(BF16)
        dout_ref[...] = dob
        dmerged = _dot_nt(dob, wo_ref[...])
        for idx, (br_ref, dbr_ref, w_ref, dyi_ref) in enumerate((
                (bra_ref, dba_ref, wa_ref, dya_ref), (brs_ref, dbs_ref, ws_ref, dys_ref),
                (brm_ref, dbm_ref, wm_ref, dym_ref))):
            s = _sigmoid(gate_ref[:, idx * 1024:(idx + 1) * 1024].astype(F32))
            dbr = (dmerged * s).astype(BF16)
            dbr_ref[...] = dbr
            dgate_ref[:, idx * 1024:(idx + 1) * 1024] = (dmerged * br_ref[...].astype(F32) * s * (1.0 - s)).astype(BF16)
            dyi_ref[...] = _dot_nt(dbr, w_ref[...])

    row = lambda w: pl.BlockSpec((tm, w), lambda i: (i, 0))
    return pl.pallas_call(
        body, name="merge_bwd", grid=(S // tm,),
        in_specs=[row(D), row(D), _full((1, D)), pl.BlockSpec((tm, 3072), lambda i: (i, 0)),
                  row(D), row(D), row(D),
                  _resident((1024, D)), _resident((1024, D)), _resident((2048, D)), _resident((D, D))],
        out_specs=[row(D), row(D), row(D), row(D), row(3072), row(1024), row(1024), row(2048), _full((1, D))],
        out_shape=[jax.ShapeDtypeStruct((S, D), BF16)] * 4 + [
            jax.ShapeDtypeStruct((S, 3072), BF16), jax.ShapeDtypeStruct((S, 1024), F32),
            jax.ShapeDtypeStruct((S, 1024), F32), jax.ShapeDtypeStruct((S, 2048), F32),
            jax.ShapeDtypeStruct((1, D), F32)],
        compiler_params=_params(("arbitrary",)),
    )(dy, out, g_post, proj, br_a, br_s, br_m, w_a, w_s, w_m, w_o)


def loss_head(y, target):
    S = y.shape[0]
    tm = 512

    def body(y_ref, t_ref, dy_ref, loss_ref):
        @pl.when(pl.program_id(0) == 0)
        def _():
            loss_ref[...] = jnp.zeros_like(loss_ref)
        e = y_ref[...] - t_ref[...]
        dy_ref[...] = e * (1.0 / D)
        loss_ref[...] += 0.5 * jnp.sum(jnp.mean(e * e, axis=-1, keepdims=True))

    row = pl.BlockSpec((tm, D), lambda i: (i, 0))
    return pl.pallas_call(
        body, name="loss_head", grid=(S // tm,),
        in_specs=[row, row], out_specs=[row, _full((1, 128))],
        out_shape=[jax.ShapeDtypeStruct((S, D), F32), jax.ShapeDtypeStruct((1, 128), F32)],
        compiler_params=_params(("arbitrary",)),
    )(y, target)


def _adam(w, g, m, v):
    mn = ADAM_B1 * m + (1.0 - ADAM_B1) * g
    vn = ADAM_B2 * v + (1.0 - ADAM_B2) * (g * g)
    m_hat = mn / (1.0 - ADAM_B1 ** ADAM_STEP)
    v_hat = vn / (1.0 - ADAM_B2 ** ADAM_STEP)
    return -ADAM_LR * (m_hat / (jnp.sqrt(v_hat) + ADAM_EPS) + ADAM_WD * w), mn, vn


def adamw_big(w, m, v, halves0, sum1, cc, name, tr):
    _, R, C = w.shape
    nper = R // tr
    f, fb, n0, off_a, off_b = halves0
    p, pb, off1 = sum1

    def body(c_ref, w_ref, m_ref, v_ref, f_ref, fb_ref, p_ref, pb_ref, g_ref, d_ref, nm_ref, nv_ref):
        i = pl.program_id(0)
        half = jnp.where(i % nper >= n0, 1, 0)
        g0 = jnp.where(c_ref[0] == half, f_ref[...], fb_ref[...])
        g = jnp.where(i < nper, g0, p_ref[...] + pb_ref[...])
        g_ref[0] = g
        d_ref[0], nm_ref[0], nv_ref[0] = _adam(w_ref[0], g, m_ref[0], v_ref[0])

    def blk0(i, c):
        il = jnp.minimum(i, nper - 1)
        return (jnp.where(il >= n0, off_b + il - n0, off_a + il), 0)

    wblk = pl.BlockSpec((1, tr, C), lambda i, c: (i // nper, i % nper, 0))
    b0 = pl.BlockSpec((tr, C), blk0)
    b1 = pl.BlockSpec((tr, C), lambda i, c: (off1 + jnp.maximum(i - nper, 0), 0))
    grid_spec = pltpu.PrefetchScalarGridSpec(
        num_scalar_prefetch=1, grid=(2 * nper,),
        in_specs=[wblk, wblk, wblk, b0, b0, b1, b1], out_specs=[wblk] * 4)
    return pl.pallas_call(
        body, name=name, grid_spec=grid_spec,
        out_shape=[jax.ShapeDtypeStruct(w.shape, F32)] * 4,
        compiler_params=_params(("arbitrary",)),
    )(cc, w, m, v, f, fb, p, pb)


def adamw_plain(w, g, m, v, name):
    def body(w_ref, g_ref, m_ref, v_ref, d_ref, nm_ref, nv_ref):
        d_ref[...], nm_ref[...], nv_ref[...] = _adam(w_ref[...], g_ref[...], m_ref[...], v_ref[...])

    return pl.pallas_call(
        body, name=name, out_shape=[jax.ShapeDtypeStruct(w.shape, F32)] * 3, compiler_params=_params(),
    )(w, g, m, v)


SMALL = {"norm_pre": ("g_pre", 8), "norm_post": ("g_post", 8), "att_sinks": ("sinks", 8), "sg_ln_g": ("ln_g", 8),
         "sg_ln_b": ("ln_b", 8), "sg_w": ("sg_w", 1024), "sg_b": ("sg_bt", 8), "ssm_conv_b": ("conv_b", 24),
         "ssm_dt_bias": ("dt_bias", 8), "ssm_a_log": ("a_log", 8), "ssm_d": ("d_skip", 8), "ssm_norm_g": ("norm_g", 16)}
SMALL_LAYER_ROWS = sum(r for _, r in SMALL.values())
REL_ROW = DEPTH * SMALL_LAYER_ROWS
LOSS_ROW = REL_ROW + 32
SMALL_ROWS = LOSS_ROW + 8


def _small_rows():
    rows, r = {}, 0
    for l in range(DEPTH):
        for name, (_, n) in SMALL.items():
            rows[(l, name)] = r
            r += n
    return rows


def adamw_small(red, rel, small):
    names = list(SMALL) + ["rel_bias"]
    params = dict(small, rel_bias=rel)
    rows = _small_rows()

    def grad_of(red_ref, l, name, n):
        r0 = rows[(l, name)]
        if name == "sg_b":
            return red_ref[r0:r0 + 8, :]
        if n < 128:
            return red_ref[r0:r0 + 1, 0:n]
        return jnp.concatenate([red_ref[r0 + j:r0 + j + 1, :] for j in range(n // 128)], axis=1)

    def body(red_ref, *refs):
        ins, outs = refs[:3 * len(names)], refs[3 * len(names):]
        for i, name in enumerate(names):
            w_ref, m_ref, v_ref = ins[3 * i:3 * i + 3]
            o = outs[4 * i:4 * i + 4]
            if name == "rel_bias":
                g = red_ref[REL_ROW:REL_ROW + 32, 0:16]
                o[0][...] = g
                o[1][...], o[2][...], o[3][...] = _adam(w_ref[...], g, m_ref[...], v_ref[...])
                continue
            for l in range(DEPTH):
                if name == "sg_w":
                    for grp in range(8):
                        r0 = rows[(l, name)] + grp * 128
                        g = red_ref[r0:r0 + 128, :]
                        o[0][l, grp] = g
                        o[1][l, grp], o[2][l, grp], o[3][l, grp] = _adam(w_ref[l, grp], g, m_ref[l, grp], v_ref[l, grp])
                elif name == "sg_b":
                    g = grad_of(red_ref, l, name, 128)
                    o[0][l] = g
                    o[1][l], o[2][l], o[3][l] = _adam(w_ref[l], g, m_ref[l], v_ref[l])
                else:
                    sl = slice(l, l + 1)
                    g = grad_of(red_ref, l, name, w_ref.shape[-1])
                    o[0][sl, :] = g
                    o[1][sl, :], o[2][sl, :], o[3][sl, :] = _adam(w_ref[sl, :], g, m_ref[sl, :], v_ref[sl, :])

    flat_in = [a for name in names for a in params[name]]
    out_shape = [jax.ShapeDtypeStruct(params[name][0].shape, F32) for name in names for _ in range(4)]
    res = pl.pallas_call(body, name="adamw_small", out_shape=out_shape, compiler_params=_params())(red, *flat_in)
    return {name: tuple(res[4 * i:4 * i + 4]) for i, name in enumerate(names)}


ANY = pl.BlockSpec(memory_space=pl.ANY)


def _place():
    x, y, c = lax.axis_index("x"), lax.axis_index("y"), lax.axis_index("c")
    others = [(1 - x, y), (x, 1 - y), (1 - x, 1 - y)]
    return x, y, c, others


def _rcopy(src, dst, ssem, rsem, to):
    return pltpu.make_async_remote_copy(src_ref=src, dst_ref=dst, send_sem=ssem, recv_sem=rsem,
                                        device_id=to, device_id_type=MESH)


def gather_weights(arrs):
    n = len(arrs)

    def body(*refs):
        srcs, outs, ssem, rsem = refs[:n], refs[n:2 * n], refs[2 * n], refs[2 * n + 1]
        x, y, c, others = _place()
        me = 2 * x + y
        sib = (x, y, 1 - c)
        first = [_rcopy(srcs[i].at[c], outs[i].at[c, me], ssem.at[6 * i + k], rsem.at[6 * i + k], (ox, oy, c))
                 for i in range(n) for k, (ox, oy) in enumerate(others)]
        for cp in first:
            cp.start()
        passed = []
        for k, (ox, oy) in enumerate(others):
            for i in range(n):
                slot = outs[i].at[c, 2 * ox + oy]
                _rcopy(slot, slot, ssem.at[6 * i + k], rsem.at[6 * i + k], sib).wait_recv()
                fw = _rcopy(slot, slot, ssem.at[6 * i + 3 + k], rsem.at[6 * i + 3 + k], sib)
                fw.start()
                passed.append(fw)
        for k, (ox, oy) in enumerate(others):
            for i in range(n):
                slot = outs[i].at[1 - c, 2 * ox + oy]
                _rcopy(slot, slot, ssem.at[6 * i + 3 + k], rsem.at[6 * i + 3 + k], sib).wait_recv()
        for cp in first + passed:
            cp.wait_send()

    return pl.pallas_call(
        body, name="gather_weights",
        in_specs=[ANY] * n, out_specs=[ANY] * n,
        out_shape=[jax.ShapeDtypeStruct((2, SHARDS) + a.shape[1:], a.dtype) for a in arrs],
        scratch_shapes=[pltpu.SemaphoreType.DMA((6 * n,)), pltpu.SemaphoreType.DMA((6 * n,))],
    )(*arrs)


HBM = pl.BlockSpec(memory_space=pltpu.HBM)
SEM = pl.BlockSpec(memory_space=pltpu.SEMAPHORE)
EFFECT = pltpu.SideEffectType.DATAFLOW_SIDE_EFFECTING


def _in_hbm(a):
    return pltpu.with_memory_space_constraint(a, pltpu.HBM)


def gather_start(srcs, after, name, by_dest=False):
    n = len(srcs)
    lands = [_in_hbm(lax.empty((SHARDS,) + a.shape[-2:], a.dtype)) for a in srcs]
    na = len(after)

    def body(*refs):
        src, land = refs[:n], refs[n:2 * n]
        ssem, rsem, token = refs[2 * n + na], refs[2 * n + na + 1], refs[-1]
        x, y, c, others = _place()
        me = 2 * x + y
        for i in range(n):
            for k, (ox, oy) in enumerate(others):
                s = src[i].at[2 * ox + oy] if by_dest else src[i]
                _rcopy(s, land[i].at[me], ssem.at[3 * i + k], rsem.at[3 * i + k], (ox, oy, c)).start()
        token[...] = jnp.zeros_like(token)

    bufs = [_in_hbm(a) for a in srcs] + lands
    out = pl.pallas_call(
        body, name=name,
        out_shape=(pltpu.SemaphoreType.DMA((3 * n,)), pltpu.SemaphoreType.DMA((3 * n,)),
                   *[pltpu.HBM(b.shape, b.dtype) for b in bufs], jax.ShapeDtypeStruct((8, 128), F32)),
        in_specs=[HBM] * (2 * n) + [ANY] * na,
        out_specs=(SEM, SEM, *[HBM] * (2 * n), pl.BlockSpec(memory_space=pltpu.VMEM)),
        input_output_aliases={i: 2 + i for i in range(2 * n)},
        compiler_params=pltpu.CompilerParams(has_side_effects=EFFECT),
    )(*bufs, *after)
    return out[0], out[1], list(out[2:2 + n]), list(out[2 + n:2 + 2 * n]), out[-1]


def gather_wait(ssem, rsem, srcs, lands, after, name, by_dest=False):
    n = len(srcs)

    def body(*refs):
        src, land = refs[:n], refs[n:2 * n]
        s_sem, r_sem = refs[2 * n], refs[2 * n + 1]
        x, y, c, others = _place()
        for i in range(n):
            for k, (ox, oy) in enumerate(others):
                s = src[i].at[2 * ox + oy] if by_dest else src[i]
                cp = _rcopy(s, land[i].at[2 * ox + oy], s_sem.at[3 * i + k], r_sem.at[3 * i + k], (ox, oy, c))
                cp.wait_send()
                cp.wait_recv()

    bufs = list(srcs) + list(lands)
    out = pl.pallas_call(
        body, name=name,
        out_shape=tuple(pltpu.HBM(b.shape, b.dtype) for b in bufs),
        in_specs=[HBM] * (2 * n) + [SEM, SEM, ANY],
        out_specs=tuple([HBM] * (2 * n)),
        input_output_aliases={i: i for i in range(2 * n)},
        compiler_params=pltpu.CompilerParams(has_side_effects=EFFECT),
    )(*bufs, ssem, rsem, after)
    return list(out[n:2 * n])


def grad_sibling_exchange(arrs):
    n = len(arrs)

    def body(*refs):
        srcs, outs, ssem, rsem = refs[:n], refs[n:2 * n], refs[2 * n], refs[2 * n + 1]
        x, y, c, _ = _place()
        cps = [_rcopy(srcs[i].at[1 - c], outs[i], ssem.at[i], rsem.at[i], (x, y, 1 - c)) for i in range(n)]
        for cp in cps:
            cp.start()
        for cp in cps:
            cp.wait()

    return pl.pallas_call(
        body, name="grad_sibling_exchange",
        in_specs=[ANY] * n, out_specs=[ANY] * n,
        out_shape=[jax.ShapeDtypeStruct(a.shape[1:], F32) for a in arrs],
        scratch_shapes=[pltpu.SemaphoreType.DMA((n,)), pltpu.SemaphoreType.DMA((n,))],
    )(*arrs)


def grad_chip_sum(g, sb, cc, tr, name):
    _, _, R, C = g.shape
    blk = pl.BlockSpec((1, tr, C), lambda s, r, c: (s, r, 0))
    grid_spec = pltpu.PrefetchScalarGridSpec(
        num_scalar_prefetch=1, grid=(SHARDS, R // tr),
        in_specs=[pl.BlockSpec((1, 1, tr, C), lambda s, r, c: (c[0], s, r, 0)), blk],
        out_specs=[blk, blk])

    def body(c_ref, a_ref, b_ref, o_ref, ob_ref):
        t = a_ref[0] + b_ref[...]
        o_ref[...] = t
        ob_ref[...] = t.astype(BF16)

    return pl.pallas_call(
        body, name=name, grid_spec=grid_spec,
        out_shape=[jax.ShapeDtypeStruct((SHARDS, R, C), F32), jax.ShapeDtypeStruct((SHARDS, R, C), BF16)],
        compiler_params=_params(("arbitrary", "arbitrary")),
    )(cc, g, sb)


def grad_shard_sum(t, rb, me, tr, name):
    _, R, C = t.shape
    grid_spec = pltpu.PrefetchScalarGridSpec(
        num_scalar_prefetch=1, grid=(R // tr,),
        in_specs=[pl.BlockSpec((1, tr, C), lambda r, m: (m[0], r, 0)),
                  pl.BlockSpec((SHARDS, tr, C), lambda r, m: (0, r, 0))],
        out_specs=pl.BlockSpec((tr, C), lambda r, m: (r, 0)))

    def body(m_ref, t_ref, r_ref, o_ref):
        part = [jnp.where(m_ref[0] == s, t_ref[0], r_ref[s].astype(F32)) for s in range(SHARDS)]
        o_ref[...] = ((part[0] + part[1]) + part[2]) + part[3]

    return pl.pallas_call(
        body, name=name, grid_spec=grid_spec,
        out_shape=jax.ShapeDtypeStruct((R, C), F32),
        compiler_params=_params(("arbitrary",)),
    )(me, t, rb)


def grad_sibling_share(arrs, name):
    n = len(arrs)

    def body(*refs):
        srcs, outs, ssem, rsem = refs[:n], refs[n:2 * n], refs[2 * n], refs[2 * n + 1]
        x, y, c, _ = _place()
        cps = [_rcopy(srcs[i], outs[i], ssem.at[i], rsem.at[i], (x, y, 1 - c)) for i in range(n)]
        for cp in cps:
            cp.start()
        for cp in cps:
            cp.wait()

    return pl.pallas_call(
        body, name=name,
        in_specs=[ANY] * n, out_specs=[ANY] * n,
        out_shape=[jax.ShapeDtypeStruct(a.shape, F32) for a in arrs],
        scratch_shapes=[pltpu.SemaphoreType.DMA((n,)), pltpu.SemaphoreType.DMA((n,))],
    )(*arrs)


def _allreduce_rows(src, sib_buf, chips, out_ref, ssem, rsem):
    x, y, c, others = _place()
    me = 2 * x + y
    cp = _rcopy(src, sib_buf, ssem.at[0], rsem.at[0], (x, y, 1 - c))
    cp.start()
    cp.wait()
    chips[me] = src[...] + sib_buf[...]
    sends = [_rcopy(chips.at[me], chips.at[me], ssem.at[1 + k], rsem.at[1 + k], (ox, oy, c))
             for k, (ox, oy) in enumerate(others)]
    for s in sends:
        s.start()
    for k, (ox, oy) in enumerate(others):
        slot = chips.at[2 * ox + oy]
        _rcopy(slot, slot, ssem.at[1 + k], rsem.at[1 + k], (ox, oy, c)).wait_recv()
    for s in sends:
        s.wait_send()
    out_ref[...] = ((chips[0] + chips[1]) + chips[2]) + chips[3]


def _allreduce_scratch(rows):
    return [pltpu.VMEM((rows, 128), F32), pltpu.VMEM((SHARDS, rows, 128), F32),
            pltpu.SemaphoreType.DMA((4,)), pltpu.SemaphoreType.DMA((4,))]


def allreduce_rows(buf, name):
    rows = buf.shape[0]
    VM = pl.BlockSpec(memory_space=pltpu.VMEM)

    def body(src_ref, out_ref, sib_buf, chips, ssem, rsem):
        _allreduce_rows(src_ref, sib_buf, chips, out_ref, ssem, rsem)

    return pl.pallas_call(
        body, name=name, in_specs=[VM], out_specs=VM,
        out_shape=jax.ShapeDtypeStruct((rows, 128), F32),
        scratch_shapes=_allreduce_scratch(rows), compiler_params=_params(),
    )(buf)


def small_allreduce(grads, rel, loss_part):
    rows = _small_rows()
    keys = [(l, name) for l in range(DEPTH) for name in SMALL]
    flat = [grads[l][SMALL[name][0]] for l, name in keys] + [rel, loss_part]

    def body(*refs):
        ins = refs[:len(flat)]
        out_ref, src, sib_buf, chips, ssem, rsem = refs[len(flat):]
        src[...] = jnp.zeros_like(src)
        for (l, name), ref in zip(keys, ins):
            r0 = rows[(l, name)]
            if name == "sg_w":
                for grp in range(8):
                    src[r0 + grp * 128:r0 + (grp + 1) * 128, :] = ref[grp]
            elif name == "sg_b":
                src[r0:r0 + 8, :] = ref[...].T[0:8, :]
            else:
                for j in range(ref.shape[1] // 128):
                    src[r0 + j:r0 + j + 1, :] = ref[:, j * 128:(j + 1) * 128]
        src[REL_ROW:REL_ROW + 32, 0:16] = ins[-2][...]
        src[LOSS_ROW:LOSS_ROW + 1, :] = ins[-1][...]
        _allreduce_rows(src, sib_buf, chips, out_ref, ssem, rsem)

    return pl.pallas_call(
        body, name="small_allreduce",
        out_shape=jax.ShapeDtypeStruct((SMALL_ROWS, 128), F32),
        scratch_shapes=[pltpu.VMEM((SMALL_ROWS, 128), F32)] + _allreduce_scratch(SMALL_ROWS),
        compiler_params=_params(),
    )(*flat)


def _pad_lanes(v):
    return jnp.zeros((1, 128), F32).at[0, :v.shape[0]].set(v)


def layer_fwd(x, wts, bias):
    wt = wts["wt"]
    tn = {name: t for name, _, t in GROUPS}
    p_gate, h = inproj_first(x, wts["g_pre"], wt["gate"], tn["gate"], "inproj_gate")
    p_sgu, p_att, p_ssd = (inproj_group(h, wt[n], tn[n], "inproj_" + n, F32 if n == "att" else BF16)
                           for n in ("sgu", "att", "ssd"))
    y_att = att_fwd(p_att, bias, wts["sinks"])
    y_sg = sgu_fwd(p_sgu, wts["ln_g"], wts["ln_b"], wts["sg_w"], wts["sg_bt"])
    y_ssm, hst = ssd_fwd(p_ssd, wts["conv_w"], wts["conv_b"], wts["dt_bias"], wts["a_log"], wts["d_skip"],
                         wts["norm_g"])
    x_new, br_a, br_s, br_m, merged, out = merge_fwd(
        y_att, y_sg, y_ssm, p_gate, x, wts["w_a"], wts["w_s"], wts["w_m"], wts["w_o"], wts["g_post"])
    saved = dict(x=x, p_gate=p_gate, p_sgu=p_sgu, p_att=p_att, p_ssd=p_ssd, h=h,
                 y_att=y_att, y_sg=y_sg, y_ssm=y_ssm, hst=hst,
                 br_a=br_a, br_s=br_s, br_m=br_m, merged=merged, out=out)
    return x_new, saved


def layer_bwd(dy, wts, bias, sv):
    dps, grads = layer_bwd_params(dy, wts, bias, sv)
    dx, grads["g_pre"] = layer_bwd_input(dy, dps, wts, sv, wts["g_pre"])
    return dx, grads


def layer_bwd_input(dy, dps, wts, sv, g_pre):
    wt = wts["wt"]
    tn = {name: t for name, _, t in GROUPS}
    acc = None
    for n in ("gate", "sgu", "ssd"):
        acc = dh_group(dps[n], wt[n], acc, tn[n], "dh_" + n)
    return dh_last(dps["att"], wt["att"], acc, sv["x"], g_pre, dy, tn["att"], "dh_att")


def layer_bwd_params(dy, wts, bias, sv):
    dout, dba, dbs, dbm, d_gate, dya, dys, dym, dg_post = merge_bwd(
        dy, sv["out"], wts["g_post"], sv["p_gate"], sv["br_a"], sv["br_s"], sv["br_m"],
        wts["w_a"], wts["w_s"], wts["w_m"], wts["w_o"])
    d_att, dbias, dsinks = att_bwd(dya, sv["p_att"], bias, wts["sinks"])
    d_sgu, dsg_w, dsg_bt, dln_g, dln_b = sgu_bwd(dys, sv["p_sgu"], wts["ln_g"], wts["ln_b"], wts["sg_w"],
                                                 wts["sg_bt"])
    d_ssd, dcw, dcb, ddtb, dalog, ddsk, dng = ssd_bwd(
        dym, sv["p_ssd"], sv["hst"], wts["conv_w"], wts["conv_b"], wts["dt_bias"], wts["a_log"], wts["d_skip"],
        wts["norm_g"])
    dps = dict(gate=d_gate, sgu=d_sgu, att=d_att, ssd=d_ssd)
    tn = {name: t for name, _, t in GROUPS}
    grads = dict(
        w_in={n: dw_group(dps[n], sv["h"], tn[n], "dw_in_" + n) for n in dps},
        w_a=matmul_tn(sv["y_att"], dba, "dw_att"),
        w_s=matmul_tn(sv["y_sg"], dbs, "dw_sg"),
        w_m=matmul_tn(sv["y_ssm"], dbm, "dw_ssm"),
        w_o=matmul_tn(sv["merged"], dout, "dw_out"),
        g_post=dg_post, sinks=dsinks, ln_g=dln_g, ln_b=dln_b, sg_w=dsg_w, sg_bt=dsg_bt,
        conv_w=dcw, conv_b=dcb, dt_bias=ddtb, a_log=dalog, d_skip=ddsk, norm_g=dng, bias=dbias)
    return dps, grads


REST_OFF = (0, 256, 512, 1024, 1280)
GR_ROWS = 1536
GR_CONV = 1280
W_IN_SPLIT = 1600
W_IN_HALF = 1824


def kernel(x, w_in, norm_pre, norm_post, rel_bias, att_sinks, sg_ln_g, sg_ln_b, sg_w, sg_b, ssm_conv_w, ssm_conv_b, ssm_dt_bias, ssm_a_log, ssm_d, ssm_norm_g, w_br_att, w_br_sg, w_br_ssm, w_out, loss_target, m_w_in, m_norm_pre, m_norm_post, m_rel_bias, m_att_sinks, m_sg_ln_g, m_sg_ln_b, m_sg_w, m_sg_b, m_ssm_conv_w, m_ssm_conv_b, m_ssm_dt_bias, m_ssm_a_log, m_ssm_d, m_ssm_norm_g, m_w_br_att, m_w_br_sg, m_w_br_ssm, m_w_out, v_w_in, v_norm_pre, v_norm_post, v_rel_bias, v_att_sinks, v_sg_ln_g, v_sg_ln_b, v_sg_w, v_sg_b, v_ssm_conv_w, v_ssm_conv_b, v_ssm_dt_bias, v_ssm_a_log, v_ssm_d, v_ssm_norm_g, v_w_br_att, v_w_br_sg, v_w_br_ssm, v_w_out):
    cx, cy, cc = lax.axis_index("x"), lax.axis_index("y"), lax.axis_index("c")
    me = 2 * cx + cy
    xs = x[0]
    S = xs.shape[0]

    tr = lambda a: jnp.transpose(a, (0, 2, 1))
    w_in_b = tr(w_in).astype(BF16)
    w_rest_b = jnp.concatenate([w_br_att, w_br_sg, w_br_ssm, w_out], axis=1).astype(BF16)
    halves = lambda a: a.reshape(2, a.shape[0] // 2, a.shape[1])
    all0_in, all0_rest = gather_weights([halves(w_in_b[0]), halves(w_rest_b[0])])
    convw_slot = jnp.zeros((SHARDS, DEPTH * CONV_K * 768 // 128, 128), F32)
    convw_slot = lax.dynamic_update_index_in_dim(
        convw_slot, jnp.where(cc == 0, 1.0, 0.0) * ssm_conv_w.reshape(-1, 128), me, 0)
    convw_rows = allreduce_rows(convw_slot.reshape(-1, 128), "gather_conv_w")
    convw_all = convw_rows.reshape(SHARDS, DEPTH, CONV_K, 768).transpose(1, 2, 0, 3).reshape(DEPTH, CONV_K, CONV_C)
    g1_ssem, g1_rsem, g1_srcs, g1_lands, g1_token = gather_start(
        [w_in_b[1], w_rest_b[1]], [convw_rows, all0_rest], "gather_l1_start")

    o = REST_OFF

    def layer_weights(l, gathered_in, gathered_rest, g_pre):
        sh_in = [jnp.where(me == s, w_in_b[l], gathered_in[s]) for s in range(SHARDS)]
        sh_rest = [jnp.where(me == s, w_rest_b[l], gathered_rest[s]) for s in range(SHARDS)]
        rest = lambda k: jnp.concatenate([r[o[k]:o[k + 1]] for r in sh_rest], axis=0)
        return dict(
            wt=group_weights(jnp.concatenate(sh_in, axis=0)),
            w_a=rest(0), w_s=rest(1), w_m=rest(2), w_o=rest(3),
            g_pre=g_pre, g_post=norm_post[l][None], sinks=att_sinks[l],
            ln_g=sg_ln_g[l][None], ln_b=sg_ln_b[l][None], sg_w=sg_w[l],
            sg_bt=sg_b[l].T,
            conv_w=jnp.concatenate([convw_all[l], jnp.zeros((4, CONV_C), F32)], axis=0),
            conv_b=ssm_conv_b[l][None], dt_bias=_pad_lanes(ssm_dt_bias[l]), a_log=_pad_lanes(ssm_a_log[l]),
            d_skip=_pad_lanes(ssm_d[l]), norm_g=ssm_norm_g[l][None])

    bias = bias_table(rel_bias)
    layers = [layer_weights(0, [all0_in[:, s].reshape(3400, D) for s in range(SHARDS)],
                            [all0_rest[:, s].reshape(1280, D) for s in range(SHARDS)],
                            (norm_pre[0] + g1_token[0, 0])[None])]
    act, sv0 = layer_fwd(xs, layers[0], bias)
    land_in, land_rest = gather_wait(g1_ssem, g1_rsem, g1_srcs, g1_lands, act, "gather_l1_wait")
    layers.append(layer_weights(1, land_in, land_rest, norm_pre[1][None]))
    act, sv1 = layer_fwd(act, layers[1], bias)
    saved = [sv0, sv1]
    dy, loss_part = loss_head(act, loss_target[0])
    cvec = jnp.reshape(cc, (1,)).astype(jnp.int32)
    mvec = jnp.reshape(me, (1,)).astype(jnp.int32)

    def by_shard(g):
        gcw = g["conv_w"][0:CONV_K].reshape(CONV_K, SHARDS, 768).transpose(1, 0, 2).reshape(SHARDS, 3, 1024)
        rest = jnp.concatenate([
            g["w_a"].reshape(SHARDS, 256, D), g["w_s"].reshape(SHARDS, 256, D), g["w_o"].reshape(SHARDS, 256, D),
            g["w_m"].reshape(SHARDS, 512, D), jnp.pad(gcw, ((0, 0), (0, GR_ROWS - GR_CONV - 3), (0, 0)))], axis=1)
        return ungroup_grads(g["w_in"]).reshape(SHARDS, 3400, D), rest

    grads = [None] * DEPTH
    dy, grads[1] = layer_bwd(dy, layers[1], bias, saved[1])
    g1_in, g1_rest = by_shard(grads[1])
    g1_in = jnp.pad(g1_in, ((0, 0), (0, W_IN_ROWS - 3400), (0, 0)))
    x1_ssem, x1_rsem, x1_srcs, x1_lands, x1_token = gather_start(
        [g1_in.astype(BF16), g1_rest.astype(BF16)], [], "grads_l1_start", by_dest=True)
    wts0 = dict(layers[0], g_post=layers[0]["g_post"] + x1_token[0, 0])
    dps0, grads[0] = layer_bwd_params(dy, wts0, bias, saved[0])
    r1_in, r1_rest = gather_wait(x1_ssem, x1_rsem, x1_srcs, x1_lands, grads[0]["w_in"]["ssd"], "grads_l1_wait",
                                 by_dest=True)
    p_in = grad_shard_sum(g1_in, r1_in, mvec, 384, "l1_sum_w_in")
    p_rest = grad_shard_sum(g1_rest, r1_rest, mvec, 512, "l1_sum_rest")
    pb_in, pb_rest = grad_sibling_share([p_in, p_rest], "l1_sibling_share")

    g0_in, g0_rest = by_shard(grads[0])
    pad_to = lambda a, rows: jnp.pad(a, ((0, 0), (0, rows - a.shape[1]), (0, 0)))
    g0_in = jnp.stack([pad_to(g0_in[:, 0:W_IN_SPLIT], W_IN_HALF), pad_to(g0_in[:, W_IN_SPLIT:3400], W_IN_HALF)])
    g0_rest = jnp.stack([g0_rest[:, 0:GR_ROWS // 2], g0_rest[:, GR_ROWS // 2:GR_ROWS]])
    sb_in, sb_rest = grad_sibling_exchange([g0_in, g0_rest])
    t_in, t_in_b = grad_chip_sum(g0_in, sb_in, cvec, 608, "chip_sum_w_in")
    t_rest, t_rest_b = grad_chip_sum(g0_rest, sb_rest, cvec, 384, "chip_sum_rest")
    x0_ssem, x0_rsem, x0_srcs, x0_lands, x0_token = gather_start([t_in_b, t_rest_b], [], "grads_l0_start", by_dest=True)
    dy, grads[0]["g_pre"] = layer_bwd_input(dy, dps0, layers[0], saved[0], layers[0]["g_pre"] + x0_token[0, 0])
    grad_x = dy[None]
    rb_in, rb_rest = gather_wait(x0_ssem, x0_rsem, x0_srcs, x0_lands, dy, "grads_l0_wait", by_dest=True)
    grad_rel_local = bias_grad(grads[0]["bias"] + grads[1]["bias"])
    f_in = grad_shard_sum(t_in, rb_in, mvec, 608, "shard_sum_w_in")
    f_rest = grad_shard_sum(t_rest, rb_rest, mvec, 384, "shard_sum_rest")
    fb_in, fb_rest = grad_sibling_share([f_in, f_rest], "l0_sibling_share")

    red = small_allreduce(grads, grad_rel_local, loss_part + 0.0 * f_rest[0:1, 0:128])
    loss = red[LOSS_ROW, 0]

    res = adamw_small(red, (rel_bias, m_rel_bias, v_rel_bias), dict(
        norm_pre=(norm_pre, m_norm_pre, v_norm_pre), norm_post=(norm_post, m_norm_post, v_norm_post),
        att_sinks=(att_sinks, m_att_sinks, v_att_sinks), sg_ln_g=(sg_ln_g, m_sg_ln_g, v_sg_ln_g),
        sg_ln_b=(sg_ln_b, m_sg_ln_b, v_sg_ln_b), sg_w=(sg_w, m_sg_w, v_sg_w), sg_b=(sg_b, m_sg_b, v_sg_b),
        ssm_conv_b=(ssm_conv_b, m_ssm_conv_b, v_ssm_conv_b), ssm_dt_bias=(ssm_dt_bias, m_ssm_dt_bias, v_ssm_dt_bias),
        ssm_a_log=(ssm_a_log, m_ssm_a_log, v_ssm_a_log), ssm_d=(ssm_d, m_ssm_d, v_ssm_d),
        ssm_norm_g=(ssm_norm_g, m_ssm_norm_g, v_ssm_norm_g)))
    res["w_in"] = tuple(tr(a) for a in adamw_big(
        tr(w_in), tr(m_w_in), tr(v_w_in), (f_in, fb_in, W_IN_SPLIT // 200, 0, 0), (p_in, pb_in, 0), cvec, "adamw_w_in", 200))
    rest_upd = lambda w, m, v, name, n0, off0, off1: adamw_big(
        w, m, v, (f_rest, fb_rest, n0, off0, off0), (p_rest, pb_rest, off1), cvec, name, 256)
    res["w_br_att"] = rest_upd(w_br_att, m_w_br_att, v_w_br_att, "adamw_w_br_att", 1, 0, 0)
    res["w_br_sg"] = rest_upd(w_br_sg, m_w_br_sg, v_w_br_sg, "adamw_w_br_sg", 1, 1, 1)
    res["w_out"] = rest_upd(w_out, m_w_out, v_w_out, "adamw_w_out", 1, 2, 2)
    res["w_br_ssm"] = rest_upd(w_br_ssm, m_w_br_ssm, v_w_br_ssm, "adamw_w_br_ssm", 0, 0, 3)
    cw0 = jnp.where(cc == 1, f_rest, fb_rest)[GR_CONV - GR_ROWS // 2:GR_CONV - GR_ROWS // 2 + 3]
    cw1 = (p_rest + pb_rest)[GR_CONV:GR_CONV + 3]
    g_conv_w = jnp.stack([cw0.reshape(CONV_K, 768), cw1.reshape(CONV_K, 768)])
    res["ssm_conv_w"] = (g_conv_w,) + tuple(adamw_plain(ssm_conv_w, g_conv_w, m_ssm_conv_w, v_ssm_conv_w, "adamw_conv_w"))

    order = ["w_in", "norm_pre", "norm_post", "rel_bias", "att_sinks", "sg_ln_g", "sg_ln_b", "sg_w", "sg_b",
             "ssm_conv_w", "ssm_conv_b", "ssm_dt_bias", "ssm_a_log", "ssm_d", "ssm_norm_g",
             "w_br_att", "w_br_sg", "w_br_ssm", "w_out"]
    return (loss, grad_x, *[res[n][0] for n in order], *[res[n][1] for n in order],
            *[res[n][2] for n in order], *[res[n][3] for n in order])
```

```python
import functools
import math

import numpy as np
import jax
import jax.numpy as jnp
from jax import lax
from jax.experimental import pallas as pl
from jax.experimental.pallas import tpu as pltpu

F32 = jnp.float32
BF16 = jnp.bfloat16
MESH = pl.DeviceIdType.MESH

D = 1024
DEPTH = 2
EPS = 1e-6
L = 128
HEADS = 16
KV = 2
DH = 64
SSM_W = 2048
SSM_H = 32
SSM_P = 64
SSM_G = 4
SSM_N = 128
CONV_K = 4
CONV_C = 3072
NEG = -1e30
IN_COLS = 13600

GROUPS = (("gate", 3072, 1536), ("sgu", 3072, 1536), ("att", 2304, 2304), ("ssd", 5376, 1792))
W_IN_ROWS = 3456

ADAM_LR = 0.001
ADAM_B1 = 0.9
ADAM_B2 = 0.999
ADAM_EPS = 1e-08
ADAM_WD = 0.01
ADAM_STEP = 10

VMEM_LIMIT = 56 * 1024 * 1024

SHARDS = 4


def _dot(a, b):
    return jnp.dot(a, b, preferred_element_type=F32)


def _dot_nt(a, b):
    return lax.dot_general(a, b, (((1,), (1,)), ((), ())), preferred_element_type=F32)


def _dot_tn(a_f32, b):
    return jnp.dot(a_f32.T.astype(BF16), b, preferred_element_type=F32)


def _dot_t(a, b):
    return lax.dot_general(a, b, (((0,), (0,)), ((), ())), preferred_element_type=F32)


def _dot_hi(a, b):
    return jnp.dot(a, b, preferred_element_type=F32, precision=lax.Precision.HIGHEST)


def _pieces(x, n):
    out = []
    for _ in range(n - 1):
        p = x.astype(BF16)
        out.append(p)
        x = x - p.astype(F32)
    out.append(x.astype(BF16))
    return out


def _dot_sel(a, sel, n):
    sel = sel.astype(BF16)
    acc = None
    for p in _pieces(a, n):
        t = _dot(p, sel)
        acc = t if acc is None else acc + t
    return acc


def _sel_dot(sel, b, n):
    sel = sel.astype(BF16)
    acc = None
    for p in _pieces(b, n):
        t = _dot(sel, p)
        acc = t if acc is None else acc + t
    return acc


def _sigmoid(x):
    return 1.0 / (1.0 + jnp.exp(-x))


def _softplus(x):
    return jnp.maximum(x, 0.0) + jnp.log(1.0 + jnp.exp(-jnp.abs(x)))


def _params(sem=None, vmem=VMEM_LIMIT):
    kw = dict(vmem_limit_bytes=vmem)
    if sem is not None:
        kw["dimension_semantics"] = sem
    return pltpu.CompilerParams(**kw)


def _full(shape):
    nd = len(shape)
    return pl.BlockSpec(shape, lambda *_: (0,) * nd)


def group_weights(wt):
    return dict(
        gate=wt[10528:13600],
        sgu=wt[2304:5376],
        att=jnp.concatenate([wt[0:1024], wt[1280:2304], wt[1024:1280]], axis=0),
        ssd=jnp.concatenate([wt[5376:10496], wt[10496:10528], jnp.zeros((224, D), wt.dtype)], axis=0))


def ungroup_grads(g):
    a, s = g["att"], g["ssd"]
    return jnp.concatenate([a[0:1024], a[2048:2304], a[1024:2048], g["sgu"], s[0:5152], g["gate"]], axis=0)


def _bucket_table():
    qi = np.arange(L)[:, None]
    kj = np.arange(2 * L)[None, :]
    dist = np.maximum(qi + L - kj, 0)
    dist_f = np.maximum(dist, 1).astype(np.float32)
    large = 16 + (np.log(dist_f / np.float32(16)) / np.float32(math.log(128 / 16)) * np.float32(16)).astype(np.int32)
    large = np.minimum(large, 31)
    return np.where(dist < 16, dist, large).astype(np.int32)


def bias_table(rel_bias):
    buckets = jnp.asarray(_bucket_table().reshape(1, L * 2 * L))

    def body(rb_ref, bk_ref, out_ref):
        onehot = (lax.broadcasted_iota(jnp.int32, (32, L * 2 * L), 0) == bk_ref[...]).astype(F32)
        out_ref[...] = lax.dot_general(rb_ref[...], onehot, (((0,), (0,)), ((), ())),
                                       preferred_element_type=F32, precision=lax.Precision.HIGHEST)

    out = pl.pallas_call(
        body, name="bias_table",
        out_shape=jax.ShapeDtypeStruct((HEADS, L * 2 * L), F32),
        compiler_params=_params(),
    )(rel_bias, buckets)
    return out.reshape(HEADS, L, 2 * L)


def bias_grad(dbias):
    buckets = jnp.asarray(_bucket_table().reshape(1, L * 2 * L))

    def body(db_ref, bk_ref, out_ref):
        onehot = (lax.broadcasted_iota(jnp.int32, (32, L * 2 * L), 0) == bk_ref[...]).astype(F32)
        out_ref[...] = lax.dot_general(onehot, db_ref[...], (((1,), (1,)), ((), ())),
                                       preferred_element_type=F32, precision=lax.Precision.HIGHEST)

    return pl.pallas_call(
        body, name="bias_grad",
        out_shape=jax.ShapeDtypeStruct((32, HEADS), F32),
        compiler_params=_params(),
    )(dbias.reshape(HEADS, L * 2 * L), buckets)


def _row_tile(S):
    return 1024 if S % 1024 == 0 else 512


def inproj_first(x, g_pre, wt, tn, name):
    S, W = x.shape[0], wt.shape[0]
    tm = _row_tile(S)

    def body(x_ref, g_ref, w_ref, o_ref, h_ref):
        @pl.when(pl.program_id(1) == 0)
        def _():
            xv = x_ref[...]
            r = lax.rsqrt(jnp.mean(xv * xv, axis=-1, keepdims=True) + EPS)
            h_ref[...] = (xv * r * g_ref[...]).astype(BF16)
        o_ref[...] = _dot_nt(h_ref[...], w_ref[...]).astype(BF16)

    return pl.pallas_call(
        body, name=name, grid=(S // tm, W // tn),
        in_specs=[pl.BlockSpec((tm, D), lambda i, j: (i, 0)), _full((1, D)),
                  pl.BlockSpec((tn, D), lambda i, j: (j, 0))],
        out_specs=[pl.BlockSpec((tm, tn), lambda i, j: (i, j)), pl.BlockSpec((tm, D), lambda i, j: (i, 0))],
        out_shape=[jax.ShapeDtypeStruct((S, W), BF16), jax.ShapeDtypeStruct((S, D), BF16)],
        compiler_params=_params(("arbitrary", "arbitrary")),
    )(x, g_pre, wt)


def inproj_group(h, wt, tn, name, dtype):
    S, W = h.shape[0], wt.shape[0]
    tm = _row_tile(S)

    def body(h_ref, w_ref, o_ref):
        o_ref[...] = _dot_nt(h_ref[...], w_ref[...]).astype(dtype)

    return pl.pallas_call(
        body, name=name, grid=(S // tm, W // tn),
        in_specs=[pl.BlockSpec((tm, D), lambda i, j: (i, 0)), pl.BlockSpec((tn, D), lambda i, j: (j, 0))],
        out_specs=pl.BlockSpec((tm, tn), lambda i, j: (i, j)),
        out_shape=jax.ShapeDtypeStruct((S, W), dtype),
        compiler_params=_params(("arbitrary", "arbitrary")),
    )(h, wt)


def dh_group(dp, wt, acc, tk, name):
    S, W = dp.shape
    tm = _row_tile(S)

    def body(*refs):
        dp_ref, w_ref, o_ref = refs[0], refs[1], refs[-1]
        first = pl.program_id(1) == 0
        if acc is None:
            @pl.when(first)
            def _():
                o_ref[...] = jnp.zeros_like(o_ref)
        else:
            @pl.when(first)
            def _():
                o_ref[...] = refs[2][...]
        o_ref[...] += _dot(dp_ref[...], w_ref[...])

    row = pl.BlockSpec((tm, D), lambda i, k: (i, 0))
    return pl.pallas_call(
        body, name=name, grid=(S // tm, W // tk),
        in_specs=[pl.BlockSpec((tm, tk), lambda i, k: (i, k)), pl.BlockSpec((tk, D), lambda i, k: (k, 0))]
        + ([] if acc is None else [row]),
        out_specs=row, out_shape=jax.ShapeDtypeStruct((S, D), F32),
        input_output_aliases={} if acc is None else {2: 0},
        compiler_params=_params(("arbitrary", "arbitrary")),
    )(*((dp, wt) if acc is None else (dp, wt, acc)))


def dh_last(dp, wt, acc_in, x, g_pre, dy, tk, name):
    S, W = dp.shape
    tm = 512
    nk = W // tk

    def body(dp_ref, w_ref, a_ref, x_ref, g_ref, dy_ref, dx_ref, dg_ref, acc):
        i, k = pl.program_id(0), pl.program_id(1)

        @pl.when(k == 0)
        def _():
            acc[...] = a_ref[...]

        acc[...] += _dot(dp_ref[...], w_ref[...])

        @pl.when((k == nk - 1) & (i == 0))
        def _():
            dg_ref[...] = jnp.zeros_like(dg_ref)

        @pl.when(k == nk - 1)
        def _():
            xv = x_ref[...]
            dh = acc[...]
            g = g_ref[...]
            r = lax.rsqrt(jnp.mean(xv * xv, axis=-1, keepdims=True) + EPS)
            dhg = dh * g
            dx_ref[...] = dy_ref[...] + r * dhg - xv * (r * r * r) * jnp.mean(dhg * xv, axis=-1, keepdims=True)
            dg_ref[...] += jnp.sum(dh * xv * r, axis=0, keepdims=True)

    row = pl.BlockSpec((tm, D), lambda i, k: (i, 0))
    return pl.pallas_call(
        body, name=name, grid=(S // tm, nk),
        in_specs=[pl.BlockSpec((tm, tk), lambda i, k: (i, k)), pl.BlockSpec((tk, D), lambda i, k: (k, 0)),
                  row, row, _full((1, D)), row],
        out_specs=[row, _full((1, D))],
        out_shape=[jax.ShapeDtypeStruct((S, D), F32), jax.ShapeDtypeStruct((1, D), F32)],
        scratch_shapes=[pltpu.VMEM((tm, D), F32)],
        compiler_params=_params(("arbitrary", "arbitrary")),
    )(dp, wt, acc_in, x, g_pre, dy)


def dw_group(dp, h, tn, name):
    S, W = dp.shape
    ts = _row_tile(S)

    def body(dp_ref, h_ref, o_ref):
        @pl.when(pl.program_id(1) == 0)
        def _():
            o_ref[...] = jnp.zeros_like(o_ref)
        o_ref[...] += _dot_t(dp_ref[...], h_ref[...])

    return pl.pallas_call(
        body, name=name, grid=(W // tn, S // ts),
        in_specs=[pl.BlockSpec((ts, tn), lambda j, s: (s, j)), pl.BlockSpec((ts, D), lambda j, s: (s, 0))],
        out_specs=pl.BlockSpec((tn, D), lambda j, s: (j, 0)),
        out_shape=jax.ShapeDtypeStruct((W, D), F32),
        compiler_params=_params(("arbitrary", "arbitrary")),
    )(dp, h)


def matmul_tn(a, b, name, tn=1024):
    S, K = a.shape
    N = b.shape[1]
    ts = _row_tile(S)
    ns = S // ts

    def body(a_ref, b_ref, o_ref):
        @pl.when(pl.program_id(1) == 0)
        def _():
            o_ref[...] = jnp.zeros_like(o_ref)
        o_ref[...] += _dot_t(a_ref[...], b_ref[...])

    return pl.pallas_call(
        body, name=name, grid=(N // tn, ns),
        in_specs=[pl.BlockSpec((ts, K), lambda j, s: (s, 0)), pl.BlockSpec((ts, tn), lambda j, s: (s, j))],
        out_specs=pl.BlockSpec((K, tn), lambda j, s: (0, j)),
        out_shape=jax.ShapeDtypeStruct((K, N), F32),
        compiler_params=_params(("arbitrary", "arbitrary")),
    )(a, b)


def _att_mask(n):
    qi = lax.broadcasted_iota(jnp.int32, (L, 2 * L), 0)
    kj = lax.broadcasted_iota(jnp.int32, (L, 2 * L), 1)
    dist = qi + L - kj
    return (dist >= 0) & (dist < L) & ((kj >= L) | (n > 0))


def _att_in_specs(nb):
    last = nb - 1
    cur = lambda n: jnp.minimum(n, last)
    prev = lambda n: jnp.maximum(jnp.minimum(n, last) - 1, 0)
    return [
        pl.BlockSpec((L, 1024), lambda n: (cur(n), 0)),
        pl.BlockSpec((L, 128), lambda n: (prev(n), 16)),
        pl.BlockSpec((L, 128), lambda n: (cur(n), 16)),
        pl.BlockSpec((L, 128), lambda n: (prev(n), 17)),
        pl.BlockSpec((L, 128), lambda n: (cur(n), 17)),
        pl.BlockSpec((L, 1024), lambda n: (cur(n), 1)),
        _full((HEADS, L, 2 * L)),
        pl.BlockSpec(memory_space=pltpu.SMEM),
    ]


GH = HEADS // KV
GB = 8


def _att_mask_rows(n, nh):
    qi = lax.broadcasted_iota(jnp.int32, (nh * L, 2 * L), 0) & (L - 1)
    kj = lax.broadcasted_iota(jnp.int32, (nh * L, 2 * L), 1)
    dist = qi + L - kj
    return (dist >= 0) & (dist < L) & ((kj >= L) | (n > 0))


def _stack_heads(ref, h0, nh, scr):
    for g in range(nh):
        scr[(h0 + g) * L:(h0 + g + 1) * L, :] = ref[:, (h0 + g) * DH:(h0 + g + 1) * DH].astype(F32)
    return scr[h0 * L:(h0 + nh) * L, :]


def _unstack_heads(val, h0, nh, ref):
    for g in range(nh):
        ref[:, (h0 + g) * DH:(h0 + g + 1) * DH] = val[g * L:(g + 1) * L, :]


def _sink_rows(s_ref, h0, nh):
    return jnp.concatenate([jnp.full((L, 1), s_ref[h0 + g], F32) for g in range(nh)], axis=0)


def _att_probs(qh, kk, bias_h, mask, sk):
    logits = _dot_nt(qh, kk) + bias_h
    logits = jnp.where(mask, logits, NEG)
    m = jnp.maximum(jnp.max(logits, axis=-1, keepdims=True), sk)
    p = jnp.exp(logits - m)
    es = jnp.exp(sk - m)
    den = jnp.sum(p, axis=-1, keepdims=True) + es
    return p / den, es / den


def att_fwd(proj, bias, sinks):
    S = proj.shape[0]
    nb = S // L

    def body(q_ref, kp_ref, kc_ref, vp_ref, vc_ref, z_ref, bias_ref, s_ref, y_ref, o_scr):
        mask = _att_mask(pl.program_id(0))
        for kv in range(KV):
            sl = slice(kv * DH, (kv + 1) * DH)
            kk = jnp.concatenate([kp_ref[:, sl], kc_ref[:, sl]], axis=0).astype(BF16)
            vv = jnp.concatenate([vp_ref[:, sl], vc_ref[:, sl]], axis=0).astype(BF16)
            for g in range(GH):
                h = kv * GH + g
                hs = slice(h * DH, (h + 1) * DH)
                qh = (q_ref[:, hs] * 0.125).astype(BF16)
                P, _ = _att_probs(qh, kk, bias_ref[h], mask, s_ref[h])
                o_scr[:, hs] = _dot(P.astype(BF16), vv)
        z = z_ref[...].astype(F32)
        y_ref[...] = (o_scr[...] * (z * _sigmoid(z))).astype(BF16)

    return pl.pallas_call(
        body, name="att_fwd", grid=(nb,),
        in_specs=_att_in_specs(nb),
        out_specs=pl.BlockSpec((L, 1024), lambda n: (n, 0)),
        out_shape=jax.ShapeDtypeStruct((S, 1024), BF16),
        scratch_shapes=[pltpu.VMEM((L, 1024), F32)],
        compiler_params=_params(("arbitrary",)),
    )(proj, proj, proj, proj, proj, proj, bias, sinks)


def att_bwd(dy, proj, bias, sinks):
    S = proj.shape[0]
    nb = S // L
    last = nb - 1

    def body(dy_ref, q_ref, kp_ref, kc_ref, vp_ref, vc_ref, z_ref, bias_ref, s_ref,
             dout_ref, dbias_ref, dsink_ref, carry, band, dq_scr, dz_scr, qs_scr, zs_scr, dys_scr):
        n = pl.program_id(0)

        @pl.when(n == 0)
        def _():
            carry[...] = jnp.zeros_like(carry)
            dq_scr[...] = jnp.zeros_like(dq_scr)
            dz_scr[...] = jnp.zeros_like(dz_scr)
            dbias_ref[...] = jnp.zeros_like(dbias_ref)
            dsink_ref[...] = jnp.zeros_like(dsink_ref)

        dout_ref[:, 0:1024] = dq_scr[...].astype(BF16)
        dout_ref[:, 1024:2048] = dz_scr[...].astype(BF16)
        band[...] = jnp.zeros_like(band)

        @pl.when(n < nb)
        def _():
            mask = _att_mask_rows(n, GB)
            lane = lax.broadcasted_iota(jnp.int32, (1, 128), 1)
            dsink = jnp.zeros((1, 128), F32)
            for kv in range(KV):
                sl = slice(kv * DH, (kv + 1) * DH)
                kk = jnp.concatenate([kp_ref[:, sl], kc_ref[:, sl]], axis=0).astype(BF16)
                vv = jnp.concatenate([vp_ref[:, sl], vc_ref[:, sl]], axis=0).astype(BF16)
                dk_acc = jnp.zeros((2 * L, DH), F32)
                dv_acc = jnp.zeros((2 * L, DH), F32)
                for h0 in range(kv * GH, (kv + 1) * GH, GB):
                    qs = (_stack_heads(q_ref, h0, GB, qs_scr) * 0.125).astype(BF16)
                    bias_g = bias_ref[h0:h0 + GB].reshape(GB * L, 2 * L)
                    P, psink = _att_probs(qs, kk, bias_g, mask, _sink_rows(s_ref, h0, GB))
                    zs = _stack_heads(z_ref, h0, GB, zs_scr)
                    dys = _stack_heads(dy_ref, h0, GB, dys_scr)
                    sg = _sigmoid(zs)
                    O = _dot(P.astype(BF16), vv)
                    _unstack_heads(dys * O * (sg * (1.0 + zs * (1.0 - sg))), h0, GB, dz_scr)
                    dOb = (dys * (zs * sg)).astype(BF16)
                    dP = _dot_nt(dOb, vv)
                    delta = jnp.sum(P * dP, axis=-1, keepdims=True)
                    dS = P * (dP - delta)
                    sd = psink * delta
                    for g in range(GB):
                        dsink = dsink + jnp.where(lane == h0 + g, -jnp.sum(sd[g * L:(g + 1) * L, :]), 0.0)
                    _unstack_heads(_dot(dS.astype(BF16), kk) * 0.125, h0, GB, dq_scr)
                    dbias_ref[h0:h0 + GB] += dS.reshape(GB, L, 2 * L)
                    dk_acc = dk_acc + _dot_tn(dS, qs)
                    dv_acc = dv_acc + _dot_tn(P, dOb)
                band[:, sl] = dk_acc
                band[:, 128 + kv * DH:128 + (kv + 1) * DH] = dv_acc
            dsink_ref[...] += dsink

        out = carry[...] + band[0:L, :]
        dout_ref[:, 2048:2304] = out.astype(BF16)
        carry[...] = band[L:2 * L, :]

    cur = lambda n: jnp.minimum(n, last)
    lag = lambda n: jnp.maximum(n - 1, 0)
    return pl.pallas_call(
        body, name="att_bwd", grid=(nb + 1,),
        in_specs=[pl.BlockSpec((L, 1024), lambda n: (cur(n), 0))] + _att_in_specs(nb),
        out_specs=[pl.BlockSpec((L, 2304), lambda n: (lag(n), 0)), _full((HEADS, L, 2 * L)), _full((1, 128))],
        out_shape=[jax.ShapeDtypeStruct((S, 2304), BF16),
                   jax.ShapeDtypeStruct((HEADS, L, 2 * L), F32), jax.ShapeDtypeStruct((1, 128), F32)],
        scratch_shapes=[pltpu.VMEM((L, 256), F32), pltpu.VMEM((2 * L, 256), F32),
                        pltpu.VMEM((L, 1024), F32), pltpu.VMEM((L, 1024), F32)]
        + [pltpu.VMEM((HEADS * L, DH), F32)] * 3,
        compiler_params=_params(("arbitrary",)),
    )(dy, proj, proj, proj, proj, proj, proj, bias, sinks)


def _sgu_in_specs():
    return [
        pl.BlockSpec((L, 1024), lambda c: (c, 0)),
        pl.BlockSpec((L, 1024), lambda c: (c, 1)),
        pl.BlockSpec((L, 1024), lambda c: (c, 2)),
        _full((1, 1024)), _full((1, 1024)), _full((8, L, L)), _full((L, 8)),
    ]


def _sgu_norm(v, lg, lb):
    mu = jnp.mean(v, axis=-1, keepdims=True)
    vc = v - mu
    rstd = lax.rsqrt(jnp.mean(vc * vc, axis=-1, keepdims=True) + EPS)
    xhat = vc * rstd
    return xhat * lg + lb, xhat, rstd


def _tril():
    return lax.broadcasted_iota(jnp.int32, (L, L), 0) >= lax.broadcasted_iota(jnp.int32, (L, L), 1)


def sgu_fwd(proj, ln_g, ln_b, w, b_t):
    S = proj.shape[0]

    def body(u_ref, v_ref, z_ref, lg_ref, lb_ref, w_ref, bt_ref, y_ref):
        vn, _, _ = _sgu_norm(v_ref[...].astype(F32), lg_ref[...], lb_ref[...])
        tri = _tril()
        parts = []
        for g in range(8):
            wg = jnp.where(tri, w_ref[g], 0.0).astype(BF16)
            parts.append(_dot(wg, vn[:, g * 128:(g + 1) * 128].astype(BF16)) + bt_ref[:, g:g + 1])
        mixed = jnp.concatenate(parts, axis=1)
        z = z_ref[...].astype(F32)
        y_ref[...] = (u_ref[...].astype(F32) * mixed * (z * _sigmoid(z))).astype(BF16)

    return pl.pallas_call(
        body, name="sgu_fwd", grid=(S // L,),
        in_specs=_sgu_in_specs(),
        out_specs=pl.BlockSpec((L, 1024), lambda c: (c, 0)),
        out_shape=jax.ShapeDtypeStruct((S, 1024), BF16),
        compiler_params=_params(("arbitrary",)),
    )(proj, proj, proj, ln_g, ln_b, w, b_t)


def sgu_bwd(dy, proj, ln_g, ln_b, w, b_t):
    S = proj.shape[0]

    def body(dy_ref, u_ref, v_ref, z_ref, lg_ref, lb_ref, w_ref, bt_ref,
             dout_ref, dw_ref, dbt_ref, dlg_ref, dlb_ref):
        @pl.when(pl.program_id(0) == 0)
        def _():
            dw_ref[...] = jnp.zeros_like(dw_ref)
            dbt_ref[...] = jnp.zeros_like(dbt_ref)
            dlg_ref[...] = jnp.zeros_like(dlg_ref)
            dlb_ref[...] = jnp.zeros_like(dlb_ref)

        lg = lg_ref[...]
        vn, xhat, rstd = _sgu_norm(v_ref[...].astype(F32), lg, lb_ref[...])
        tri = _tril()
        lane = lax.broadcasted_iota(jnp.int32, (L, 128), 1)
        wgs, parts = [], []
        for g in range(8):
            wg = jnp.where(tri, w_ref[g], 0.0)
            wgs.append(wg)
            parts.append(_dot(wg.astype(BF16), vn[:, g * 128:(g + 1) * 128].astype(BF16)) + bt_ref[:, g:g + 1])
        mixed = jnp.concatenate(parts, axis=1)
        z = z_ref[...].astype(F32)
        sg = _sigmoid(z)
        silu = z * sg
        dy_v = dy_ref[...]
        u = u_ref[...].astype(F32)
        dout_ref[:, 0:1024] = (dy_v * mixed * silu).astype(BF16)
        dout_ref[:, 2048:3072] = (dy_v * u * mixed * (sg * (1.0 + z * (1.0 - sg)))).astype(BF16)
        dmixed = dy_v * u * silu
        dbt = jnp.zeros((L, 128), F32)
        dvn_parts = []
        for g in range(8):
            dm = dmixed[:, g * 128:(g + 1) * 128]
            dmb = dm.astype(BF16)
            dbt = dbt + jnp.where(lane == g, jnp.sum(dm, axis=1, keepdims=True), 0.0)
            dw_ref[g] += jnp.where(tri, _dot_nt(dmb, vn[:, g * 128:(g + 1) * 128].astype(BF16)), 0.0)
            dvn_parts.append(_dot_tn(wgs[g], dmb))
        dbt_ref[...] += dbt
        dvn = jnp.concatenate(dvn_parts, axis=1)
        dlg_ref[...] += jnp.sum(dvn * xhat, axis=0, keepdims=True)
        dlb_ref[...] += jnp.sum(dvn, axis=0, keepdims=True)
        dxh = dvn * lg
        dv = rstd * (dxh - jnp.mean(dxh, axis=-1, keepdims=True)
                     - xhat * jnp.mean(dxh * xhat, axis=-1, keepdims=True))
        dout_ref[:, 1024:2048] = dv.astype(BF16)

    return pl.pallas_call(
        body, name="sgu_bwd", grid=(S // L,),
        in_specs=[pl.BlockSpec((L, 1024), lambda c: (c, 0))] + _sgu_in_specs(),
        out_specs=[pl.BlockSpec((L, 3072), lambda c: (c, 0)), _full((8, L, L)), _full((L, 128)),
                   _full((1, 1024)), _full((1, 1024))],
        out_shape=[jax.ShapeDtypeStruct((S, 3072), BF16), jax.ShapeDtypeStruct((8, L, L), F32),
                   jax.ShapeDtypeStruct((L, 128), F32), jax.ShapeDtypeStruct((1, 1024), F32),
                   jax.ShapeDtypeStruct((1, 1024), F32)],
        compiler_params=_params(("arbitrary",)),
    )(dy, proj, proj, proj, ln_g, ln_b, w, b_t)


def _expand_matrix():
    r = lax.broadcasted_iota(jnp.int32, (128, SSM_W), 0)
    c = lax.broadcasted_iota(jnp.int32, (128, SSM_W), 1)
    return (c // SSM_P) == r


def _expand_matrix_t():
    r = lax.broadcasted_iota(jnp.int32, (SSM_W, 128), 0)
    c = lax.broadcasted_iota(jnp.int32, (SSM_W, 128), 1)
    return (r // SSM_P) == c


def _rows_from(ref, start):
    C = ref.shape[1]
    tiles = ref[...].reshape(17, 8, C)
    q, s = divmod(start, 8)
    if s == 0:
        return tiles[q:q + 16].reshape(L, C)
    rolled = pltpu.roll(tiles, 8 - s, axis=1)
    sub = lax.broadcasted_iota(jnp.int32, (16, 8, C), 1)
    return jnp.where(sub < 8 - s, rolled[q:q + 16], rolled[q + 1:q + 17]).reshape(L, C)


def _ssd_common(ext_ref, cw_ref, cb_ref, dt_raw, dtb, alog):
    taps = [_rows_from(ext_ref, 5 + k) for k in range(CONV_K)]
    pre = cb_ref[...]
    for k in range(CONV_K):
        pre = pre + cw_ref[k:k + 1, :] * taps[k]
    sg_pre = _sigmoid(pre)
    xc = pre * sg_pre
    dt = _softplus(dt_raw + dtb)
    a = -jnp.exp(alog)
    adt = dt * a
    acs = _sel_dot(_tril(), adt, 3)
    return pre, sg_pre, xc, dt, a, acs, taps


def _ssd_in_specs(rev, nc):
    cidx = (lambda c: nc - 1 - c) if rev else (lambda c: c)
    return [
        pl.BlockSpec((L, 2048), lambda c: (cidx(c), 0)),
        pl.BlockSpec((L, 1024), lambda c: (cidx(c), 2)),
        pl.BlockSpec((L, 1024), lambda c: (cidx(c), 3)),
        pl.BlockSpec((L, 1024), lambda c: (cidx(c), 4)),
        pl.BlockSpec((L, 128), lambda c: (cidx(c), 40)),
        _full((8, CONV_C)), _full((1, CONV_C)), _full((1, 128)), _full((1, 128)), _full((1, 128)),
        _full((1, SSM_W)),
    ]


def ssd_fwd(proj, conv_w, conv_b, dt_bias, a_log, d_skip, norm_g):
    S = proj.shape[0]
    nc = S // L

    def body(z_ref, xa_ref, xb_ref, xc_ref, dt_ref, cw_ref, cb_ref, dtb_ref, alog_ref, dsk_ref, ng_ref,
             y_ref, hs_ref, H, ext, ysc):
        @pl.when(pl.program_id(0) == 0)
        def _():
            H[...] = jnp.zeros_like(H)
            ext[0:8, :] = jnp.zeros((8, CONV_C), F32)

        for k, ref in enumerate((xa_ref, xb_ref, xc_ref)):
            ext[8:8 + L, k * 1024:(k + 1) * 1024] = ref[...].astype(F32)
        pre, sg_pre, xc, dt, a, acs, _ = _ssd_common(ext, cw_ref, cb_ref, dt_ref[...].astype(F32), dtb_ref[...],
                                                     alog_ref[...])
        for k, ref in enumerate((xa_ref, xb_ref, xc_ref)):
            ext[0:8, k * 1024:(k + 1) * 1024] = ref[L - 8:L, :].astype(F32)
        xs = xc[:, 0:SSM_W]
        acs_t = acs.T
        ex = _expand_matrix()
        dt_x = _dot_sel(dt, ex, 2)
        xdt = xs * dt_x
        eacs_x = _dot_sel(jnp.exp(acs), ex, 2)
        xw = xdt * _dot_sel(jnp.exp(acs[L - 1:L, :] - acs), ex, 2)
        cd_row = jnp.exp(acs[L - 1:L, :])
        hs_ref[0] = H[...]
        tri = _tril()
        for g in range(SSM_G):
            gs = slice(g * 512, (g + 1) * 512)
            bg = xc[:, SSM_W + g * SSM_N:SSM_W + (g + 1) * SSM_N].astype(BF16)
            cg = xc[:, SSM_W + 512 + g * SSM_N:SSM_W + 512 + (g + 1) * SSM_N].astype(BF16)
            G = _dot_nt(cg, bg)
            yoff = _dot_nt(cg, H[gs, :].astype(BF16)) * eacs_x[:, gs]
            Sg = _dot_tn(xw[:, gs], bg)
            for j in range(8):
                hh = g * 8 + j
                hs = slice(hh * SSM_P, (hh + 1) * SSM_P)
                seg = acs[:, hh:hh + 1] - acs_t[hh:hh + 1, :]
                dk = jnp.where(tri, jnp.exp(jnp.minimum(seg, 0.0)), 0.0)
                yd = _dot((G * dk).astype(BF16), xdt[:, hs].astype(BF16))
                ysc[:, hs] = yd + yoff[:, j * SSM_P:(j + 1) * SSM_P]
                H[hs, :] = H[hs, :] * cd_row[:, hh:hh + 1] + Sg[j * SSM_P:(j + 1) * SSM_P, :]
        d_x = _dot_sel(jnp.broadcast_to(dsk_ref[...], (8, 128)), ex, 3)[0:1, :]
        Y = ysc[...] + d_x * xs
        z = z_ref[...].astype(F32)
        yz = Y * (z * _sigmoid(z))
        ng = ng_ref[...]
        for g in range(SSM_G):
            gs = slice(g * 512, (g + 1) * 512)
            t = yz[:, gs]
            rstd = lax.rsqrt(jnp.mean(t * t, axis=-1, keepdims=True) + EPS)
            y_ref[:, gs] = (t * rstd * ng[:, gs]).astype(BF16)

    return pl.pallas_call(
        body, name="ssd_fwd", grid=(nc,),
        in_specs=_ssd_in_specs(False, nc),
        out_specs=[pl.BlockSpec((L, SSM_W), lambda c: (c, 0)), pl.BlockSpec((1, SSM_W, SSM_N), lambda c: (c, 0, 0))],
        out_shape=[jax.ShapeDtypeStruct((S, SSM_W), BF16), jax.ShapeDtypeStruct((nc, SSM_W, SSM_N), F32)],
        scratch_shapes=[pltpu.VMEM((SSM_W, SSM_N), F32), pltpu.VMEM((8 + L, CONV_C), F32),
                        pltpu.VMEM((L, SSM_W), F32)],
        compiler_params=_params(("arbitrary",)),
    )(proj, proj, proj, proj, proj, conv_w, conv_b, dt_bias, a_log, d_skip, norm_g)


def ssd_bwd(dy, proj, hstates, conv_w, conv_b, dt_bias, a_log, d_skip, norm_g):
    S = proj.shape[0]
    nc = S // L
    cidx = lambda c: nc - 1 - c

    def body(dy_ref, z_ref, xa_ref, xb_ref, xc_ref, dt_ref, cw_ref, cb_ref, dtb_ref, alog_ref, dsk_ref, ng_ref,
             pa_ref, pb_ref, pc_ref, hp_ref,
             dout_ref, dcw_ref, dcb_ref, ddtb_ref, dalog_ref, ddsk_ref, dng_ref,
             dH, ext, dext, ysc, yoffsc, dxdt, dxc, tsc):
        step = pl.program_id(0)
        c = nc - 1 - step

        @pl.when(step == 0)
        def _():
            dH[...] = jnp.zeros_like(dH)
            dext[L:L + 8, :] = jnp.zeros((8, CONV_C), F32)
            for r in (dcw_ref, dcb_ref, ddtb_ref, dalog_ref, ddsk_ref, dng_ref):
                r[...] = jnp.zeros_like(r)

        for k, (ref, prev) in enumerate(((xa_ref, pa_ref), (xb_ref, pb_ref), (xc_ref, pc_ref))):
            ext[0:8, k * 1024:(k + 1) * 1024] = jnp.where(c > 0, prev[8:16, :].astype(F32), 0.0)
            ext[8:8 + L, k * 1024:(k + 1) * 1024] = ref[...].astype(F32)
        dtb = dtb_ref[...]
        dt_raw = dt_ref[...].astype(F32)
        pre, sg_pre, xc, dt, a, acs, taps = _ssd_common(ext, cw_ref, cb_ref, dt_raw, dtb, alog_ref[...])
        xs = xc[:, 0:SSM_W]
        acs_t = acs.T
        ex = _expand_matrix()
        dt_x = _dot_sel(dt, ex, 2)
        xdt = xs * dt_x
        eacs_x = _dot_sel(jnp.exp(acs), ex, 2)
        dte_x = _dot_sel(jnp.exp(acs[L - 1:L, :] - acs), ex, 2)
        xw = xdt * dte_x
        cd_row = jnp.exp(acs[L - 1:L, :])
        tri = _tril()

        Gs, Cs, Bs = [], [], []
        for g in range(SSM_G):
            gs = slice(g * 512, (g + 1) * 512)
            bg = xc[:, SSM_W + g * SSM_N:SSM_W + (g + 1) * SSM_N].astype(BF16)
            cg = xc[:, SSM_W + 512 + g * SSM_N:SSM_W + 512 + (g + 1) * SSM_N].astype(BF16)
            G = _dot_nt(cg, bg)
            Gs.append(G), Cs.append(cg), Bs.append(bg)
            yoffsc[:, gs] = _dot_nt(cg, hp_ref[0, gs, :].astype(BF16)) * eacs_x[:, gs]
            for j in range(8):
                hh = g * 8 + j
                hs = slice(hh * SSM_P, (hh + 1) * SSM_P)
                seg = acs[:, hh:hh + 1] - acs_t[hh:hh + 1, :]
                dk = jnp.where(tri, jnp.exp(jnp.minimum(seg, 0.0)), 0.0)
                ysc[:, hs] = _dot((G * dk).astype(BF16), xdt[:, hs].astype(BF16))
        d_x = _dot_sel(jnp.broadcast_to(dsk_ref[...], (8, 128)), ex, 3)[0:1, :]
        yoff = yoffsc[...]
        Y = ysc[...] + yoff + d_x * xs

        z = z_ref[...].astype(F32)
        sgz = _sigmoid(z)
        silu_z = z * sgz
        yz = Y * silu_z
        ng = ng_ref[...]
        dout = dy_ref[...]
        dyn = dout * ng
        dyz_parts, dng_parts = [], []
        for g in range(SSM_G):
            gs = slice(g * 512, (g + 1) * 512)
            t = yz[:, gs]
            rstd = lax.rsqrt(jnp.mean(t * t, axis=-1, keepdims=True) + EPS)
            dng_parts.append(jnp.sum(dout[:, gs] * t * rstd, axis=0, keepdims=True))
            dn = dyn[:, gs]
            dyz_parts.append(rstd * dn - t * (rstd * rstd * rstd) * jnp.mean(dn * t, axis=-1, keepdims=True))
        dng_ref[...] += jnp.concatenate(dng_parts, axis=1)
        dyz = jnp.concatenate(dyz_parts, axis=1)
        dY = dyz * silu_z
        dout_ref[:, 0:SSM_W] = (dyz * Y * (sgz * (1.0 + z * (1.0 - sgz)))).astype(BF16)

        ex_t = _expand_matrix_t()
        ddsk_ref[...] += _dot_sel(jnp.broadcast_to(jnp.sum(dY * xs, axis=0, keepdims=True), (8, SSM_W)), ex_t, 3)[0:1, :]

        lane = lax.broadcasted_iota(jnp.int32, (L, 128), 1)
        subl = lax.broadcasted_iota(jnp.int32, (128, L), 0)
        coll = lax.broadcasted_iota(jnp.int32, (128, L), 1)
        r_cols = jnp.zeros((L, 128), F32)
        c_rows = jnp.zeros((128, L), F32)
        for g in range(SSM_G):
            gs = slice(g * 512, (g + 1) * 512)
            G, cg, bg = Gs[g], Cs[g], Bs[g]
            hp_g = hp_ref[0, gs, :]
            dh_g = dH[gs, :]
            dY_g = dY[:, gs]
            dZ = dY_g * eacs_x[:, gs]
            dZb = dZ.astype(BF16)
            dC = _dot(dZb, hp_g.astype(BF16))
            dh_from_off = _dot_tn(dZ, cg)
            dhb = dh_g.astype(BF16)
            Q = _dot_nt(bg, dhb)
            dB = _dot(xw[:, gs].astype(BF16), dhb)
            qd = Q * dte_x[:, gs]
            dxdt[:, gs] = qd
            tsc[:, gs] = qd * xdt[:, gs]
            dG = jnp.zeros((L, L), F32)
            for j in range(8):
                hh = g * 8 + j
                hs = slice(hh * SSM_P, (hh + 1) * SSM_P)
                seg = acs[:, hh:hh + 1] - acs_t[hh:hh + 1, :]
                dk = jnp.where(tri, jnp.exp(jnp.minimum(seg, 0.0)), 0.0)
                M = G * dk
                dYh = dY[:, hs]
                dYhb = dYh.astype(BF16)
                dM = _dot_nt(dYhb, xdt[:, hs].astype(BF16))
                dxdt[:, hs] += _dot_tn(M, dYhb)
                dG = dG + dM * dk
                Wm = dM * M
                r_cols = r_cols + jnp.where(lane == hh, jnp.sum(Wm, axis=1, keepdims=True), 0.0)
                c_rows = c_rows + jnp.where(subl == hh, jnp.sum(Wm, axis=0, keepdims=True), 0.0)
                pj = slice(j * SSM_P, (j + 1) * SSM_P)
                cd_h = cd_row[:, hh:hh + 1]
                dcd = jnp.sum(dh_g[pj, :] * hp_g[pj, :]) * cd_h
                c_rows = c_rows - jnp.where((subl == hh) & (coll == L - 1), dcd, 0.0)
                dH[hs, :] = dh_g[pj, :] * cd_h + dh_from_off[pj, :]
            dGb = dG.astype(BF16)
            dC = dC + _dot(dGb, bg)
            dB = dB + _dot_tn(dG, cg)
            dxc[:, SSM_W + g * SSM_N:SSM_W + (g + 1) * SSM_N] = dB
            dxc[:, SSM_W + 512 + g * SSM_N:SSM_W + 512 + (g + 1) * SSM_N] = dC

        row = lax.broadcasted_iota(jnp.int32, (L, 128), 0)
        tv = tsc[...]
        t_last = _dot_sel(jnp.broadcast_to(jnp.sum(tv, axis=0, keepdims=True), (8, SSM_W)), ex_t, 3)[0:1, :]
        dacs = (r_cols - c_rows.T + _dot_sel(dY * yoff - tv, ex_t, 2) + jnp.where(row == L - 1, t_last, 0.0))
        triu = lax.broadcasted_iota(jnp.int32, (L, L), 0) <= lax.broadcasted_iota(jnp.int32, (L, L), 1)
        dadt = _sel_dot(triu, dacs, 3)
        dxdt_v = dxdt[...]
        ddt = _dot_sel(dxdt_v * xs, ex_t, 2) + dadt * a
        dalog_ref[...] += jnp.sum(dadt * dt * a, axis=0, keepdims=True)
        ddt_raw = jnp.where(lane < SSM_H, ddt * _sigmoid(dt_raw + dtb), 0.0)
        ddtb_ref[...] += jnp.sum(ddt_raw, axis=0, keepdims=True)
        dout_ref[:, 5120:5248] = ddt_raw.astype(BF16)
        dout_ref[:, 5248:5376] = jnp.zeros((L, 128), BF16)

        dxc[:, 0:SSM_W] = dxdt_v * dt_x + d_x * dY
        dpre = dxc[...] * (sg_pre * (1.0 + pre * (1.0 - sg_pre)))
        dcb_ref[...] += jnp.sum(dpre, axis=0, keepdims=True)
        dext[0:L, :] = dpre
        x_cur = ext[8:8 + L, :]
        dx = None
        for k in range(CONV_K):
            dsh = _rows_from(dext, 3 - k)
            term = cw_ref[k:k + 1, :] * dsh
            dx = term if dx is None else dx + term
            dcw_ref[k:k + 1, :] += jnp.sum(dsh * x_cur, axis=0, keepdims=True)
        dout_ref[:, SSM_W:SSM_W + CONV_C] = dx.astype(BF16)
        dext[L:L + 8, :] = dpre[0:8, :]

    big = lambda w: pl.BlockSpec((L, w), lambda c: (cidx(c), 0))
    return pl.pallas_call(
        body, name="ssd_bwd", grid=(nc,),
        in_specs=[big(SSM_W)] + _ssd_in_specs(True, nc) + [
            pl.BlockSpec((16, 1024), lambda c, k=k: (jnp.maximum(8 * cidx(c) - 1, 0), k)) for k in (2, 3, 4)] + [
            pl.BlockSpec((1, SSM_W, SSM_N), lambda c: (cidx(c), 0, 0))],
        out_specs=[big(5376), _full((8, CONV_C)), _full((1, CONV_C)),
                   _full((1, 128)), _full((1, 128)), _full((1, 128)), _full((1, SSM_W))],
        out_shape=[jax.ShapeDtypeStruct((S, 5376), BF16), jax.ShapeDtypeStruct((8, CONV_C), F32),
                   jax.ShapeDtypeStruct((1, CONV_C), F32), jax.ShapeDtypeStruct((1, 128), F32),
                   jax.ShapeDtypeStruct((1, 128), F32), jax.ShapeDtypeStruct((1, 128), F32),
                   jax.ShapeDtypeStruct((1, SSM_W), F32)],
        scratch_shapes=[pltpu.VMEM((SSM_W, SSM_N), F32), pltpu.VMEM((8 + L, CONV_C), F32),
                        pltpu.VMEM((L + 8, CONV_C), F32), pltpu.VMEM((L, SSM_W), F32),
                        pltpu.VMEM((L, SSM_W), F32), pltpu.VMEM((L, SSM_W), F32),
                        pltpu.VMEM((L, CONV_C), F32), pltpu.VMEM((L, SSM_W), F32)],
        compiler_params=_params(("arbitrary",)),
    )(dy, proj, proj, proj, proj, proj, conv_w, conv_b, dt_bias, a_log, d_skip, norm_g, proj, proj, proj, hstates)


def _resident(shape):
    nd = len(shape)
    return pl.BlockSpec(shape, lambda *_: (0,) * nd, pipeline_mode=pl.Buffered(1))


def merge_fwd(y_att, y_sg, y_ssm, proj, x, w_a, w_s, w_m, w_o, g_post):
    S = x.shape[0]
    tm = 256

    def body(ya_ref, ys_ref, ym_ref, gate_ref, x_ref, wa_ref, ws_ref, wm_ref, wo_ref, gp_ref,
             xn_ref, bra_ref, brs_ref, brm_ref, mg_ref, out_ref):
        bra = _dot(ya_ref[...], wa_ref[...])
        brs = _dot(ys_ref[...], ws_ref[...])
        brm = _dot(ym_ref[...], wm_ref[...])
        bra_ref[...] = bra.astype(BF16)
        brs_ref[...] = brs.astype(BF16)
        brm_ref[...] = brm.astype(BF16)
        gate = gate_ref[...].astype(F32)
        merged = (_sigmoid(gate[:, 0:1024]) * bra + _sigmoid(gate[:, 1024:2048]) * brs
                  + _sigmoid(gate[:, 2048:3072]) * brm)
        mb = merged.astype(BF16)
        mg_ref[...] = mb
        o = _dot(mb, wo_ref[...])
        out_ref[...] = o
        r = lax.rsqrt(jnp.mean(o * o, axis=-1, keepdims=True) + EPS)
        xn_ref[...] = x_ref[...] + o * r * gp_ref[...]

    row = lambda w: pl.BlockSpec((tm, w), lambda i: (i, 0))
    return pl.pallas_call(
        body, name="merge_fwd", grid=(S // tm,),
        in_specs=[row(1024), row(1024), row(2048), pl.BlockSpec((tm, 3072), lambda i: (i, 0)),
                  row(D), _resident((1024, D)), _resident((1024, D)), _resident((2048, D)), _resident((D, D)),
                  _full((1, D))],
        out_specs=[row(D)] * 6,
        out_shape=[jax.ShapeDtypeStruct((S, D), F32)] + [jax.ShapeDtypeStruct((S, D), BF16)] * 4
        + [jax.ShapeDtypeStruct((S, D), F32)],
        compiler_params=_params(("arbitrary",)),
    )(y_att, y_sg, y_ssm, proj, x, w_a, w_s, w_m, w_o, g_post)


def merge_bwd(dy, out, g_post, proj, br_a, br_s, br_m, w_a, w_s, w_m, w_o):
    S = dy.shape[0]
    tm = 256

    def body(dy_ref, o_ref, gp_ref, gate_ref, bra_ref, brs_ref, brm_ref, wa_ref, ws_ref, wm_ref, wo_ref,
             dout_ref, dba_ref, dbs_ref, dbm_ref, dgate_ref, dya_ref, dys_ref, dym_ref, dgp_ref):
        @pl.when(pl.program_id(0) == 0)
        def _():
            dgp_ref[...] = jnp.zeros_like(dgp_ref)

        o = o_ref[...]
        dyv = dy_ref[...]
        r = lax.rsqrt(jnp.mean(o * o, axis=-1, keepdims=True) + EPS)
        dyg = dyv * gp_ref[...]
        do = r * dyg - o * (r * r * r) * jnp.mean(dyg * o, axis=-1, keepdims=True)
        dgp_ref[...] += jnp.sum(dyv * o * r, axis=0, keepdims=True)
        dob = do.astype(BF16)
        dout_ref[...] = dob
        dmerged = _dot_nt(dob, wo_ref[...])
        for idx, (br_ref, dbr_ref, w_ref, dyi_ref) in enumerate((
                (bra_ref, dba_ref, wa_ref, dya_ref), (brs_ref, dbs_ref, ws_ref, dys_ref),
                (brm_ref, dbm_ref, wm_ref, dym_ref))):
            s = _sigmoid(gate_ref[:, idx * 1024:(idx + 1) * 1024].astype(F32))
            dbr = (dmerged * s).astype(BF16)
            dbr_ref[...] = dbr
            dgate_ref[:, idx * 1024:(idx + 1) * 1024] = (dmerged * br_ref[...].astype(F32) * s * (1.0 - s)).astype(BF16)
            dyi_ref[...] = _dot_nt(dbr, w_ref[...])

    row = lambda w: pl.BlockSpec((tm, w), lambda i: (i, 0))
    return pl.pallas_call(
        body, name="merge_bwd", grid=(S // tm,),
        in_specs=[row(D), row(D), _full((1, D)), pl.BlockSpec((tm, 3072), lambda i: (i, 0)),
                  row(D), row(D), row(D),
                  _resident((1024, D)), _resident((1024, D)), _resident((2048, D)), _resident((D, D))],
        out_specs=[row(D), row(D), row(D), row(D), row(3072), row(1024), row(1024), row(2048), _full((1, D))],
        out_shape=[jax.ShapeDtypeStruct((S, D), BF16)] * 4 + [
            jax.ShapeDtypeStruct((S, 3072), BF16), jax.ShapeDtypeStruct((S, 1024), F32),
            jax.ShapeDtypeStruct((S, 1024), F32), jax.ShapeDtypeStruct((S, 2048), F32),
            jax.ShapeDtypeStruct((1, D), F32)],
        compiler_params=_params(("arbitrary",)),
    )(dy, out, g_post, proj, br_a, br_s, br_m, w_a, w_s, w_m, w_o)


def loss_head(y, target):
    S = y.shape[0]
    tm = 512

    def body(y_ref, t_ref, dy_ref, loss_ref):
        @pl.when(pl.program_id(0) == 0)
        def _():
            loss_ref[...] = jnp.zeros_like(loss_ref)
        e = y_ref[...] - t_ref[...]
        dy_ref[...] = e * (1.0 / D)
        loss_ref[...] += 0.5 * jnp.sum(jnp.mean(e * e, axis=-1, keepdims=True))

    row = pl.BlockSpec((tm, D), lambda i: (i, 0))
    return pl.pallas_call(
        body, name="loss_head", grid=(S // tm,),
        in_specs=[row, row], out_specs=[row, _full((1, 128))],
        out_shape=[jax.ShapeDtypeStruct((S, D), F32), jax.ShapeDtypeStruct((1, 128), F32)],
        compiler_params=_params(("arbitrary",)),
    )(y, target)


def _adam(w, g, m, v):
    mn = ADAM_B1 * m + (1.0 - ADAM_B1) * g
    vn = ADAM_B2 * v + (1.0 - ADAM_B2) * (g * g)
    m_hat = mn / (1.0 - ADAM_B1 ** ADAM_STEP)
    v_hat = vn / (1.0 - ADAM_B2 ** ADAM_STEP)
    return -ADAM_LR * (m_hat / (jnp.sqrt(v_hat) + ADAM_EPS) + ADAM_WD * w), mn, vn


def adamw_big(w, m, v, halves0, sum1, cc, name, tr):
    _, R, C = w.shape
    nper = R // tr
    f, fb, n0, off_a, off_b = halves0
    p, pb, off1 = sum1

    def body(c_ref, w_ref, m_ref, v_ref, f_ref, fb_ref, p_ref, pb_ref, g_ref, d_ref, nm_ref, nv_ref):
        i = pl.program_id(0)
        half = jnp.where(i % nper >= n0, 1, 0)
        g0 = jnp.where(c_ref[0] == half, f_ref[...], fb_ref[...])
        g = jnp.where(i < nper, g0, p_ref[...] + pb_ref[...])
        g_ref[0] = g
        d_ref[0], nm_ref[0], nv_ref[0] = _adam(w_ref[0], g, m_ref[0], v_ref[0])

    def blk0(i, c):
        il = jnp.minimum(i, nper - 1)
        return (jnp.where(il >= n0, off_b + il - n0, off_a + il), 0)

    wblk = pl.BlockSpec((1, tr, C), lambda i, c: (i // nper, i % nper, 0))
    b0 = pl.BlockSpec((tr, C), blk0)
    b1 = pl.BlockSpec((tr, C), lambda i, c: (off1 + jnp.maximum(i - nper, 0), 0))
    grid_spec = pltpu.PrefetchScalarGridSpec(
        num_scalar_prefetch=1, grid=(2 * nper,),
        in_specs=[wblk, wblk, wblk, b0, b0, b1, b1], out_specs=[wblk] * 4)
    return pl.pallas_call(
        body, name=name, grid_spec=grid_spec,
        out_shape=[jax.ShapeDtypeStruct(w.shape, F32)] * 4,
        compiler_params=_params(("arbitrary",)),
    )(cc, w, m, v, f, fb, p, pb)


def adamw_plain(w, g, m, v, name):
    def body(w_ref, g_ref, m_ref, v_ref, d_ref, nm_ref, nv_ref):
        d_ref[...], nm_ref[...], nv_ref[...] = _adam(w_ref[...], g_ref[...], m_ref[...], v_ref[...])

    return pl.pallas_call(
        body, name=name, out_shape=[jax.ShapeDtypeStruct(w.shape, F32)] * 3, compiler_params=_params(),
    )(w, g, m, v)


SMALL = {"norm_pre": ("g_pre", 8), "norm_post": ("g_post", 8), "att_sinks": ("sinks", 8), "sg_ln_g": ("ln_g", 8),
         "sg_ln_b": ("ln_b", 8), "sg_w": ("sg_w", 1024), "sg_b": ("sg_bt", 8), "ssm_conv_b": ("conv_b", 24),
         "ssm_dt_bias": ("dt_bias", 8), "ssm_a_log": ("a_log", 8), "ssm_d": ("d_skip", 8), "ssm_norm_g": ("norm_g", 16)}
SMALL_LAYER_ROWS = sum(r for _, r in SMALL.values())
REL_ROW = DEPTH * SMALL_LAYER_ROWS
LOSS_ROW = REL_ROW + 32
SMALL_ROWS = LOSS_ROW + 8


def _small_rows():
    rows, r = {}, 0
    for l in range(DEPTH):
        for name, (_, n) in SMALL.items():
            rows[(l, name)] = r
            r += n
    return rows


def adamw_small(red, rel, small):
    names = list(SMALL) + ["rel_bias"]
    params = dict(small, rel_bias=rel)
    rows = _small_rows()

    def grad_of(red_ref, l, name, n):
        r0 = rows[(l, name)]
        if name == "sg_b":
            return red_ref[r0:r0 + 8, :]
        if n < 128:
            return red_ref[r0:r0 + 1, 0:n]
        return jnp.concatenate([red_ref[r0 + j:r0 + j + 1, :] for j in range(n // 128)], axis=1)

    def body(red_ref, *refs):
        ins, outs = refs[:3 * len(names)], refs[3 * len(names):]
        for i, name in enumerate(names):
            w_ref, m_ref, v_ref = ins[3 * i:3 * i + 3]
            o = outs[4 * i:4 * i + 4]
            if name == "rel_bias":
                g = red_ref[REL_ROW:REL_ROW + 32, 0:16]
                o[0][...] = g
                o[1][...], o[2][...], o[3][...] = _adam(w_ref[...], g, m_ref[...], v_ref[...])
                continue
            for l in range(DEPTH):
                if name == "sg_w":
                    for grp in range(8):
                        r0 = rows[(l, name)] + grp * 128
                        g = red_ref[r0:r0 + 128, :]
                        o[0][l, grp] = g
                        o[1][l, grp], o[2][l, grp], o[3][l, grp] = _adam(w_ref[l, grp], g, m_ref[l, grp], v_ref[l, grp])
                elif name == "sg_b":
                    g = grad_of(red_ref, l, name, 128)
                    o[0][l] = g
                    o[1][l], o[2][l], o[3][l] = _adam(w_ref[l], g, m_ref[l], v_ref[l])
                else:
                    sl = slice(l, l + 1)
                    g = grad_of(red_ref, l, name, w_ref.shape[-1])
                    o[0][sl, :] = g
                    o[1][sl, :], o[2][sl, :], o[3][sl, :] = _adam(w_ref[sl, :], g, m_ref[sl, :], v_ref[sl, :])

    flat_in = [a for name in names for a in params[name]]
    out_shape = [jax.ShapeDtypeStruct(params[name][0].shape, F32) for name in names for _ in range(4)]
    res = pl.pallas_call(body, name="adamw_small", out_shape=out_shape, compiler_params=_params())(red, *flat_in)
    return {name: tuple(res[4 * i:4 * i + 4]) for i, name in enumerate(names)}


ANY = pl.BlockSpec(memory_space=pl.ANY)


def _place():
    x, y, c = lax.axis_index("x"), lax.axis_index("y"), lax.axis_index("c")
    others = [(1 - x, y), (x, 1 - y), (1 - x, 1 - y)]
    return x, y, c, others


def _rcopy(src, dst, ssem, rsem, to):
    return pltpu.make_async_remote_copy(src_ref=src, dst_ref=dst, send_sem=ssem, recv_sem=rsem,
                                        device_id=to, device_id_type=MESH)


def gather_weights(arrs):
    n = len(arrs)

    def body(*refs):
        srcs, outs, ssem, rsem = refs[:n], refs[n:2 * n], refs[2 * n], refs[2 * n + 1]
        x, y, c, others = _place()
        me = 2 * x + y
        sib = (x, y, 1 - c)
        first = [_rcopy(srcs[i].at[c], outs[i].at[c, me], ssem.at[6 * i + k], rsem.at[6 * i + k], (ox, oy, c))
                 for i in range(n) for k, (ox, oy) in enumerate(others)]
        for cp in first:
            cp.start()
        passed = []
        for k, (ox, oy) in enumerate(others):
            for i in range(n):
                slot = outs[i].at[c, 2 * ox + oy]
                _rcopy(slot, slot, ssem.at[6 * i + k], rsem.at[6 * i + k], sib).wait_recv()
                fw = _rcopy(slot, slot, ssem.at[6 * i + 3 + k], rsem.at[6 * i + 3 + k], sib)
                fw.start()
                passed.append(fw)
        for k, (ox, oy) in enumerate(others):
            for i in range(n):
                slot = outs[i].at[1 - c, 2 * ox + oy]
                _rcopy(slot, slot, ssem.at[6 * i + 3 + k], rsem.at[6 * i + 3 + k], sib).wait_recv()
        for cp in first + passed:
            cp.wait_send()

    return pl.pallas_call(
        body, name="gather_weights",
        in_specs=[ANY] * n, out_specs=[ANY] * n,
        out_shape=[jax.ShapeDtypeStruct((2, SHARDS) + a.shape[1:], a.dtype) for a in arrs],
        scratch_shapes=[pltpu.SemaphoreType.DMA((6 * n,)), pltpu.SemaphoreType.DMA((6 * n,))],
    )(*arrs)


HBM = pl.BlockSpec(memory_space=pltpu.HBM)
SEM = pl.BlockSpec(memory_space=pltpu.SEMAPHORE)
EFFECT = pltpu.SideEffectType.DATAFLOW_SIDE_EFFECTING


def _in_hbm(a):
    return pltpu.with_memory_space_constraint(a, pltpu.HBM)


def gather_start(srcs, after, name, by_dest=False):
    n = len(srcs)
    lands = [_in_hbm(lax.empty((SHARDS,) + a.shape[-2:], a.dtype)) for a in srcs]
    na = len(after)

    def body(*refs):
        src, land = refs[:n], refs[n:2 * n]
        ssem, rsem, token = refs[2 * n + na], refs[2 * n + na + 1], refs[-1]
        x, y, c, others = _place()
        me = 2 * x + y
        for i in range(n):
            for k, (ox, oy) in enumerate(others):
                s = src[i].at[2 * ox + oy] if by_dest else src[i]
                _rcopy(s, land[i].at[me], ssem.at[3 * i + k], rsem.at[3 * i + k], (ox, oy, c)).start()
        token[...] = jnp.zeros_like(token)

    bufs = [_in_hbm(a) for a in srcs] + lands
    out = pl.pallas_call(
        body, name=name,
        out_shape=(pltpu.SemaphoreType.DMA((3 * n,)), pltpu.SemaphoreType.DMA((3 * n,)),
                   *[pltpu.HBM(b.shape, b.dtype) for b in bufs], jax.ShapeDtypeStruct((8, 128), F32)),
        in_specs=[HBM] * (2 * n) + [ANY] * na,
        out_specs=(SEM, SEM, *[HBM] * (2 * n), pl.BlockSpec(memory_space=pltpu.VMEM)),
        input_output_aliases={i: 2 + i for i in range(2 * n)},
        compiler_params=pltpu.CompilerParams(has_side_effects=EFFECT),
    )(*bufs, *after)
    return out[0], out[1], list(out[2:2 + n]), list(out[2 + n:2 + 2 * n]), out[-1]


def gather_wait(ssem, rsem, srcs, lands, after, name, by_dest=False):
    n = len(srcs)

    def body(*refs):
        src, land = refs[:n], refs[n:2 * n]
        s_sem, r_sem = refs[2 * n], refs[2 * n + 1]
        x, y, c, others = _place()
        for i in range(n):
            for k, (ox, oy) in enumerate(others):
                s = src[i].at[2 * ox + oy] if by_dest else src[i]
                cp = _rcopy(s, land[i].at[2 * ox + oy], s_sem.at[3 * i + k], r_sem.at[3 * i + k], (ox, oy, c))
                cp.wait_send()
                cp.wait_recv()

    bufs = list(srcs) + list(lands)
    out = pl.pallas_call(
        body, name=name,
        out_shape=tuple(pltpu.HBM(b.shape, b.dtype) for b in bufs),
        in_specs=[HBM] * (2 * n) + [SEM, SEM, ANY],
        out_specs=tuple([HBM] * (2 * n)),
        input_output_aliases={i: i for i in range(2 * n)},
        compiler_params=pltpu.CompilerParams(has_side_effects=EFFECT),
    )(*bufs, ssem, rsem, after)
    return list(out[n:2 * n])


def grad_sibling_exchange(arrs):
    n = len(arrs)

    def body(*refs):
        srcs, outs, ssem, rsem = refs[:n], refs[n:2 * n], refs[2 * n], refs[2 * n + 1]
        x, y, c, _ = _place()
        cps = [_rcopy(srcs[i].at[1 - c], outs[i], ssem.at[i], rsem.at[i], (x, y, 1 - c)) for i in range(n)]
        for cp in cps:
            cp.start()
        for cp in cps:
            cp.wait()

    return pl.pallas_call(
        body, name="grad_sibling_exchange",
        in_specs=[ANY] * n, out_specs=[ANY] * n,
        out_shape=[jax.ShapeDtypeStruct(a.shape[1:], F32) for a in arrs],
        scratch_shapes=[pltpu.SemaphoreType.DMA((n,)), pltpu.SemaphoreType.DMA((n,))],
    )(*arrs)


def grad_chip_sum(g, sb, cc, tr, name):
    _, _, R, C = g.shape
    blk = pl.BlockSpec((1, tr, C), lambda s, r, c: (s, r, 0))
    grid_spec = pltpu.PrefetchScalarGridSpec(
        num_scalar_prefetch=1, grid=(SHARDS, R // tr),
        in_specs=[pl.BlockSpec((1, 1, tr, C), lambda s, r, c: (c[0], s, r, 0)), blk],
        out_specs=[blk, blk])

    def body(c_ref, a_ref, b_ref, o_ref, ob_ref):
        t = a_ref[0] + b_ref[...]
        o_ref[...] = t
        ob_ref[...] = t.astype(BF16)

    return pl.pallas_call(
        body, name=name, grid_spec=grid_spec,
        out_shape=[jax.ShapeDtypeStruct((SHARDS, R, C), F32), jax.ShapeDtypeStruct((SHARDS, R, C), BF16)],
        compiler_params=_params(("arbitrary", "arbitrary")),
    )(cc, g, sb)


def grad_shard_sum(t, rb, me, tr, name):
    _, R, C = t.shape
    grid_spec = pltpu.PrefetchScalarGridSpec(
        num_scalar_prefetch=1, grid=(R // tr,),
        in_specs=[pl.BlockSpec((1, tr, C), lambda r, m: (m[0], r, 0)),
                  pl.BlockSpec((SHARDS, tr, C), lambda r, m: (0, r, 0))],
        out_specs=pl.BlockSpec((tr, C), lambda r, m: (r, 0)))

    def body(m_ref, t_ref, r_ref, o_ref):
        part = [jnp.where(m_ref[0] == s, t_ref[0], r_ref[s].astype(F32)) for s in range(SHARDS)]
        o_ref[...] = ((part[0] + part[1]) + part[2]) + part[3]

    return pl.pallas_call(
        body, name=name, grid_spec=grid_spec,
        out_shape=jax.ShapeDtypeStruct((R, C), F32),
        compiler_params=_params(("arbitrary",)),
    )(me, t, rb)


def grad_sibling_share(arrs, name):
    n = len(arrs)

    def body(*refs):
        srcs, outs, ssem, rsem = refs[:n], refs[n:2 * n], refs[2 * n], refs[2 * n + 1]
        x, y, c, _ = _place()
        cps = [_rcopy(srcs[i], outs[i], ssem.at[i], rsem.at[i], (x, y, 1 - c)) for i in range(n)]
        for cp in cps:
            cp.start()
        for cp in cps:
            cp.wait()

    return pl.pallas_call(
        body, name=name,
        in_specs=[ANY] * n, out_specs=[ANY] * n,
        out_shape=[jax.ShapeDtypeStruct(a.shape, F32) for a in arrs],
        scratch_shapes=[pltpu.SemaphoreType.DMA((n,)), pltpu.SemaphoreType.DMA((n,))],
    )(*arrs)


def _allreduce_rows(src, sib_buf, chips, out_ref, ssem, rsem):
    x, y, c, others = _place()
    me = 2 * x + y
    cp = _rcopy(src, sib_buf, ssem.at[0], rsem.at[0], (x, y, 1 - c))
    cp.start()
    cp.wait()
    chips[me] = src[...] + sib_buf[...]
    sends = [_rcopy(chips.at[me], chips.at[me], ssem.at[1 + k], rsem.at[1 + k], (ox, oy, c))
             for k, (ox, oy) in enumerate(others)]
    for s in sends:
        s.start()
    for k, (ox, oy) in enumerate(others):
        slot = chips.at[2 * ox + oy]
        _rcopy(slot, slot, ssem.at[1 + k], rsem.at[1 + k], (ox, oy, c)).wait_recv()
    for s in sends:
        s.wait_send()
    out_ref[...] = ((chips[0] + chips[1]) + chips[2]) + chips[3]


def _allreduce_scratch(rows):
    return [pltpu.VMEM((rows, 128), F32), pltpu.VMEM((SHARDS, rows, 128), F32),
            pltpu.SemaphoreType.DMA((4,)), pltpu.SemaphoreType.DMA((4,))]


def allreduce_rows(buf, name):
    rows = buf.shape[0]
    VM = pl.BlockSpec(memory_space=pltpu.VMEM)

    def body(src_ref, out_ref, sib_buf, chips, ssem, rsem):
        _allreduce_rows(src_ref, sib_buf, chips, out_ref, ssem, rsem)

    return pl.pallas_call(
        body, name=name, in_specs=[VM], out_specs=VM,
        out_shape=jax.ShapeDtypeStruct((rows, 128), F32),
        scratch_shapes=_allreduce_scratch(rows), compiler_params=_params(),
    )(buf)


def small_allreduce(grads, rel, loss_part):
    rows = _small_rows()
    keys = [(l, name) for l in range(DEPTH) for name in SMALL]
    flat = [grads[l][SMALL[name][0]] for l, name in keys] + [rel, loss_part]

    def body(*refs):
        ins = refs[:len(flat)]
        out_ref, src, sib_buf, chips, ssem, rsem = refs[len(flat):]
        src[...] = jnp.zeros_like(src)
        for (l, name), ref in zip(keys, ins):
            r0 = rows[(l, name)]
            if name == "sg_w":
                for grp in range(8):
                    src[r0 + grp * 128:r0 + (grp + 1) * 128, :] = ref[grp]
            elif name == "sg_b":
                src[r0:r0 + 8, :] = ref[...].T[0:8, :]
            else:
                for j in range(ref.shape[1] // 128):
                    src[r0 + j:r0 + j + 1, :] = ref[:, j * 128:(j + 1) * 128]
        src[REL_ROW:REL_ROW + 32, 0:16] = ins[-2][...]
        src[LOSS_ROW:LOSS_ROW + 1, :] = ins[-1][...]
        _allreduce_rows(src, sib_buf, chips, out_ref, ssem, rsem)

    return pl.pallas_call(
        body, name="small_allreduce",
        out_shape=jax.ShapeDtypeStruct((SMALL_ROWS, 128), F32),
        scratch_shapes=[pltpu.VMEM((SMALL_ROWS, 128), F32)] + _allreduce_scratch(SMALL_ROWS),
        compiler_params=_params(),
    )(*flat)


def _pad_lanes(v):
    return jnp.zeros((1, 128), F32).at[0, :v.shape[0]].set(v)


def layer_fwd(x, wts, bias):
    wt = wts["wt"]
    tn = {name: t for name, _, t in GROUPS}
    p_gate, h = inproj_first(x, wts["g_pre"], wt["gate"], tn["gate"], "inproj_gate")
    p_sgu, p_att, p_ssd = (inproj_group(h, wt[n], tn[n], "inproj_" + n, F32 if n == "att" else BF16)
                           for n in ("sgu", "att", "ssd"))
    y_att = att_fwd(p_att, bias, wts["sinks"])
    y_sg = sgu_fwd(p_sgu, wts["ln_g"], wts["ln_b"], wts["sg_w"], wts["sg_bt"])
    y_ssm, hst = ssd_fwd(p_ssd, wts["conv_w"], wts["conv_b"], wts["dt_bias"], wts["a_log"], wts["d_skip"],
                         wts["norm_g"])
    x_new, br_a, br_s, br_m, merged, out = merge_fwd(
        y_att, y_sg, y_ssm, p_gate, x, wts["w_a"], wts["w_s"], wts["w_m"], wts["w_o"], wts["g_post"])
    saved = dict(x=x, p_gate=p_gate, p_sgu=p_sgu, p_att=p_att, p_ssd=p_ssd, h=h,
                 y_att=y_att, y_sg=y_sg, y_ssm=y_ssm, hst=hst,
                 br_a=br_a, br_s=br_s, br_m=br_m, merged=merged, out=out)
    return x_new, saved


def layer_bwd(dy, wts, bias, sv):
    dps, grads = layer_bwd_params(dy, wts, bias, sv)
    dx, grads["g_pre"] = layer_bwd_input(dy, dps, wts, sv, wts["g_pre"])
    return dx, grads


def layer_bwd_input(dy, dps, wts, sv, g_pre):
    wt = wts["wt"]
    tn = {name: t for name, _, t in GROUPS}
    acc = None
    for n in ("gate", "sgu", "ssd"):
        acc = dh_group(dps[n], wt[n], acc, tn[n], "dh_" + n)
    return dh_last(dps["att"], wt["att"], acc, sv["x"], g_pre, dy, tn["att"], "dh_att")


def layer_bwd_params(dy, wts, bias, sv):
    dout, dba, dbs, dbm, d_gate, dya, dys, dym, dg_post = merge_bwd(
        dy, sv["out"], wts["g_post"], sv["p_gate"], sv["br_a"], sv["br_s"], sv["br_m"],
        wts["w_a"], wts["w_s"], wts["w_m"], wts["w_o"])
    d_att, dbias, dsinks = att_bwd(dya, sv["p_att"], bias, wts["sinks"])
    d_sgu, dsg_w, dsg_bt, dln_g, dln_b = sgu_bwd(dys, sv["p_sgu"], wts["ln_g"], wts["ln_b"], wts["sg_w"],
                                                 wts["sg_bt"])
    d_ssd, dcw, dcb, ddtb, dalog, ddsk, dng = ssd_bwd(
        dym, sv["p_ssd"], sv["hst"], wts["conv_w"], wts["conv_b"], wts["dt_bias"], wts["a_log"], wts["d_skip"],
        wts["norm_g"])
    dps = dict(gate=d_gate, sgu=d_sgu, att=d_att, ssd=d_ssd)
    tn = {name: t for name, _, t in GROUPS}
    grads = dict(
        w_in={n: dw_group(dps[n], sv["h"], tn[n], "dw_in_" + n) for n in dps},
        w_a=matmul_tn(sv["y_att"], dba, "dw_att"),
        w_s=matmul_tn(sv["y_sg"], dbs, "dw_sg"),
        w_m=matmul_tn(sv["y_ssm"], dbm, "dw_ssm"),
        w_o=matmul_tn(sv["merged"], dout, "dw_out"),
        g_post=dg_post, sinks=dsinks, ln_g=dln_g, ln_b=dln_b, sg_w=dsg_w, sg_bt=dsg_bt,
        conv_w=dcw, conv_b=dcb, dt_bias=ddtb, a_log=dalog, d_skip=ddsk, norm_g=dng, bias=dbias)
    return dps, grads


REST_OFF = (0, 256, 512, 1024, 1280)
GR_ROWS = 1536
GR_CONV = 1280
W_IN_SPLIT = 1600
W_IN_HALF = 1824


def kernel(x, w_in, norm_pre, norm_post, rel_bias, att_sinks, sg_ln_g, sg_ln_b, sg_w, sg_b, ssm_conv_w, ssm_conv_b, ssm_dt_bias, ssm_a_log, ssm_d, ssm_norm_g, w_br_att, w_br_sg, w_br_ssm, w_out, loss_target, m_w_in, m_norm_pre, m_norm_post, m_rel_bias, m_att_sinks, m_sg_ln_g, m_sg_ln_b, m_sg_w, m_sg_b, m_ssm_conv_w, m_ssm_conv_b, m_ssm_dt_bias, m_ssm_a_log, m_ssm_d, m_ssm_norm_g, m_w_br_att, m_w_br_sg, m_w_br_ssm, m_w_out, v_w_in, v_norm_pre, v_norm_post, v_rel_bias, v_att_sinks, v_sg_ln_g, v_sg_ln_b, v_sg_w, v_sg_b, v_ssm_conv_w, v_ssm_conv_b, v_ssm_dt_bias, v_ssm_a_log, v_ssm_d, v_ssm_norm_g, v_w_br_att, v_w_br_sg, v_w_br_ssm, v_w_out):
    cx, cy, cc = lax.axis_index("x"), lax.axis_index("y"), lax.axis_index("c")
    me = 2 * cx + cy
    xs = x[0]
    S = xs.shape[0]

    tr = lambda a: jnp.transpose(a, (0, 2, 1))
    w_in_b = tr(w_in).astype(BF16)
    w_rest_b = jnp.concatenate([w_br_att, w_br_sg, w_br_ssm, w_out], axis=1).astype(BF16)
    halves = lambda a: a.reshape(2, a.shape[0] // 2, a.shape[1])
    all0_in, all0_rest = gather_weights([halves(w_in_b[0]), halves(w_rest_b[0])])
    convw_slot = jnp.zeros((SHARDS, DEPTH * CONV_K * 768 // 128, 128), F32)
    convw_slot = lax.dynamic_update_index_in_dim(
        convw_slot, jnp.where(cc == 0, 1.0, 0.0) * ssm_conv_w.reshape(-1, 128), me, 0)
    convw_rows = allreduce_rows(convw_slot.reshape(-1, 128), "gather_conv_w")
    convw_all = convw_rows.reshape(SHARDS, DEPTH, CONV_K, 768).transpose(1, 2, 0, 3).reshape(DEPTH, CONV_K, CONV_C)
    g1_ssem, g1_rsem, g1_srcs, g1_lands, g1_token = gather_start(
        [w_in_b[1], w_rest_b[1]], [convw_rows, all0_rest], "gather_l1_start")

    o = REST_OFF

    def layer_weights(l, gathered_in, gathered_rest, g_pre):
        sh_in = [jnp.where(me == s, w_in_b[l], gathered_in[s]) for s in range(SHARDS)]
        sh_rest = [jnp.where(me == s, w_rest_b[l], gathered_rest[s]) for s in range(SHARDS)]
        rest = lambda k: jnp.concatenate([r[o[k]:o[k + 1]] for r in sh_rest], axis=0)
        return dict(
            wt=group_weights(jnp.concatenate(sh_in, axis=0)),
            w_a=rest(0), w_s=rest(1), w_m=rest(2), w_o=rest(3),
            g_pre=g_pre, g_post=norm_post[l][None], sinks=att_sinks[l],
            ln_g=sg_ln_g[l][None], ln_b=sg_ln_b[l][None], sg_w=sg_w[l],
            sg_bt=sg_b[l].T,
            conv_w=jnp.concatenate([convw_all[l], jnp.zeros((4, CONV_C), F32)], axis=0),
            conv_b=ssm_conv_b[l][None], dt_bias=_pad_lanes(ssm_dt_bias[l]), a_log=_pad_lanes(ssm_a_log[l]),
            d_skip=_pad_lanes(ssm_d[l]), norm_g=ssm_norm_g[l][None])

    bias = bias_table(rel_bias)
    layers = [layer_weights(0, [all0_in[:, s].reshape(3400, D) for s in range(SHARDS)],
                            [all0_rest[:, s].reshape(1280, D) for s in range(SHARDS)],
                            (norm_pre[0] + g1_token[0, 0])[None])]
    act, sv0 = layer_fwd(xs, layers[0], bias)
    land_in, land_rest = gather_wait(g1_ssem, g1_rsem, g1_srcs, g1_lands, act, "gather_l1_wait")
    layers.append(layer_weights(1, land_in, land_rest, norm_pre[1][None]))
    act, sv1 = layer_fwd(act, layers[1], bias)
    saved = [sv0, sv1]
    dy, loss_part = loss_head(act, loss_target[0])
    cvec = jnp.reshape(cc, (1,)).astype(jnp.int32)
    mvec = jnp.reshape(me, (1,)).astype(jnp.int32)

    def by_shard(g):
        gcw = g["conv_w"][0:CONV_K].reshape(CONV_K, SHARDS, 768).transpose(1, 0, 2).reshape(SHARDS, 3, 1024)
        rest = jnp.concatenate([
            g["w_a"].reshape(SHARDS, 256, D), g["w_s"].reshape(SHARDS, 256, D), g["w_o"].reshape(SHARDS, 256, D),
            g["w_m"].reshape(SHARDS, 512, D), jnp.pad(gcw, ((0, 0), (0, GR_ROWS - GR_CONV - 3), (0, 0)))], axis=1)
        return ungroup_grads(g["w_in"]).reshape(SHARDS, 3400, D), rest

    grads = [None] * DEPTH
    dy, grads[1] = layer_bwd(dy, layers[1], bias, saved[1])
    g1_in, g1_rest = by_shard(grads[1])
    g1_in = jnp.pad(g1_in, ((0, 0), (0, W_IN_ROWS - 3400), (0, 0)))
    x1_ssem, x1_rsem, x1_srcs, x1_lands, x1_token = gather_start(
        [g1_in.astype(BF16), g1_rest.astype(BF16)], [], "grads_l1_start", by_dest=True)
    wts0 = dict(layers[0], g_post=layers[0]["g_post"] + x1_token[0, 0])
    dps0, grads[0] = layer_bwd_params(dy, wts0, bias, saved[0])
    r1_in, r1_rest = gather_wait(x1_ssem, x1_rsem, x1_srcs, x1_lands, grads[0]["w_in"]["ssd"], "grads_l1_wait",
                                 by_dest=True)
    p_in = grad_shard_sum(g1_in, r1_in, mvec, 384, "l1_sum_w_in")
    p_rest = grad_shard_sum(g1_rest, r1_rest, mvec, 512, "l1_sum_rest")
    pb_in, pb_rest = grad_sibling_share([p_in, p_rest], "l1_sibling_share")

    g0_in, g0_rest = by_shard(grads[0])
    pad_to = lambda a, rows: jnp.pad(a, ((0, 0), (0, rows - a.shape[1]), (0, 0)))
    g0_in = jnp.stack([pad_to(g0_in[:, 0:W_IN_SPLIT], W_IN_HALF), pad_to(g0_in[:, W_IN_SPLIT:3400], W_IN_HALF)])
    g0_rest = jnp.stack([g0_rest[:, 0:GR_ROWS // 2], g0_rest[:, GR_ROWS // 2:GR_ROWS]])
    sb_in, sb_rest = grad_sibling_exchange([g0_in, g0_rest])
    t_in, t_in_b = grad_chip_sum(g0_in, sb_in, cvec, 608, "chip_sum_w_in")
    t_rest, t_rest_b = grad_chip_sum(g0_rest, sb_rest, cvec, 384, "chip_sum_rest")
    x0_ssem, x0_rsem, x0_srcs, x0_lands, x0_token = gather_start([t_in_b, t_rest_b], [], "grads_l0_start", by_dest=True)
    dy, grads[0]["g_pre"] = layer_bwd_input(dy, dps0, layers[0], saved[0], layers[0]["g_pre"] + x0_token[0, 0])
    grad_x = dy[None]
    rb_in, rb_rest = gather_wait(x0_ssem, x0_rsem, x0_srcs, x0_lands, dy, "grads_l0_wait", by_dest=True)
    grad_rel_local = bias_grad(grads[0]["bias"] + grads[1]["bias"])
    f_in = grad_shard_sum(t_in, rb_in, mvec, 608, "shard_sum_w_in")
    f_rest = grad_shard_sum(t_rest, rb_rest, mvec, 384, "shard_sum_rest")
    fb_in, fb_rest = grad_sibling_share([f_in, f_rest], "l0_sibling_share")

    red = small_allreduce(grads, grad_rel_local, loss_part + 0.0 * f_rest[0:1, 0:128])
    loss = red[LOSS_ROW, 0]

    res = adamw_small(red, (rel_bias, m_rel_bias, v_rel_bias), dict(
        norm_pre=(norm_pre, m_norm_pre, v_norm_pre), norm_post=(norm_post, m_norm_post, v_norm_post),
        att_sinks=(att_sinks, m_att_sinks, v_att_sinks), sg_ln_g=(sg_ln_g, m_sg_ln_g, v_sg_ln_g),
        sg_ln_b=(sg_ln_b, m_sg_ln_b, v_sg_ln_b), sg_w=(sg_w, m_sg_w, v_sg_w), sg_b=(sg_b, m_sg_b, v_sg_b),
        ssm_conv_b=(ssm_conv_b, m_ssm_conv_b, v_ssm_conv_b), ssm_dt_bias=(ssm_dt_bias, m_ssm_dt_bias, v_ssm_dt_bias),
        ssm_a_log=(ssm_a_log, m_ssm_a_log, v_ssm_a_log), ssm_d=(ssm_d, m_ssm_d, v_ssm_d),
        ssm_norm_g=(ssm_norm_g, m_ssm_norm_g, v_ssm_norm_g)))
    res["w_in"] = tuple(tr(a) for a in adamw_big(
        tr(w_in), tr(m_w_in), tr(v_w_in), (f_in, fb_in, W_IN_SPLIT // 200, 0, 0), (p_in, pb_in, 0), cvec, "adamw_w_in", 200))
    rest_upd = lambda w, m, v, name, n0, off0, off1: adamw_big(
        w, m, v, (f_rest, fb_rest, n0, off0, off0), (p_rest, pb_rest, off1), cvec, name, 256)
    res["w_br_att"] = rest_upd(w_br_att, m_w_br_att, v_w_br_att, "adamw_w_br_att", 1, 0, 0)
    res["w_br_sg"] = rest_upd(w_br_sg, m_w_br_sg, v_w_br_sg, "adamw_w_br_sg", 1, 1, 1)
    res["w_out"] = rest_upd(w_out, m_w_out, v_w_out, "adamw_w_out", 1, 2, 2)
    res["w_br_ssm"] = rest_upd(w_br_ssm, m_w_br_ssm, v_w_br_ssm, "adamw_w_br_ssm", 0, 0, 3)
    cw0 = jnp.where(cc == 1, f_rest, fb_rest)[GR_CONV - GR_ROWS // 2:GR_CONV - GR_ROWS // 2 + 3]
    cw1 = (p_rest + pb_rest)[GR_CONV:GR_CONV + 3]
    g_conv_w = jnp.stack([cw0.reshape(CONV_K, 768), cw1.reshape(CONV_K, 768)])
    res["ssm_conv_w"] = (g_conv_w,) + tuple(adamw_plain(ssm_conv_w, g_conv_w, m_ssm_conv_w, v_ssm_conv_w, "adamw_conv_w"))

    order = ["w_in", "norm_pre", "norm_post", "rel_bias", "att_sinks", "sg_ln_g", "sg_ln_b", "sg_w", "sg_b",
             "ssm_conv_w", "ssm_conv_b", "ssm_dt_bias", "ssm_a_log", "ssm_d", "ssm_norm_g",
             "w_br_att", "w_br_sg", "w_br_ssm", "w_out"]
    return (loss, grad_x, *[res[n][0] for n in order], *[res[n][1] for n in order],
            *[res[n][2] for n in order], *[res[n][3] for n in order])
```

```python
import functools
import math

import numpy as np
import jax
import jax.numpy as jnp
from jax import lax
from jax.experimental import pallas as pl
from jax.experimental.pallas import tpu as pltpu

F32 = jnp.float32
BF16 = jnp.bfloat16
MESH = pl.DeviceIdType.MESH

D = 1024
DEPTH = 2
EPS = 1e-6
L = 128
HEADS = 16
KV = 2
DH = 64
SSM_W = 2048
SSM_H = 32
SSM_P = 64
SSM_G = 4
SSM_N = 128
CONV_K = 4
CONV_C = 3072
NEG = -1e30
IN_COLS = 13600

GROUPS = (("gate", 3072, 1536), ("sgu", 3072, 1536), ("att", 2304, 2304), ("ssd", 5376, 1792))
W_IN_ROWS = 3456

ADAM_LR = 0.001
ADAM_B1 = 0.9
ADAM_B2 = 0.999
ADAM_EPS = 1e-08
ADAM_WD = 0.01
ADAM_STEP = 10

VMEM_LIMIT = 56 * 1024 * 1024

SHARDS = 4


def _dot(a, b):
    return jnp.dot(a, b, preferred_element_type=F32)


def _dot_nt(a, b):
    return lax.dot_general(a, b, (((1,), (1,)), ((), ())), preferred_element_type=F32)


def _dot_tn(a_f32, b):
    return jnp.dot(a_f32.T.astype(BF16), b, preferred_element_type=F32)


def _dot_t(a, b):
    return lax.dot_general(a, b, (((0,), (0,)), ((), ())), preferred_element_type=F32)


def _dot_hi(a, b):
    return jnp.dot(a, b, preferred_element_type=F32, precision=lax.Precision.HIGHEST)


def _pieces(x, n):
    out = []
    for _ in range(n - 1):
        p = x.astype(BF16)
        out.append(p)
        x = x - p.astype(F32)
    out.append(x.astype(BF16))
    return out


def _dot_sel(a, sel, n):
    sel = sel.astype(BF16)
    acc = None
    for p in _pieces(a, n):
        t = _dot(p, sel)
        acc = t if acc is None else acc + t
    return acc


def _sel_dot(sel, b, n):
    sel = sel.astype(BF16)
    acc = None
    for p in _pieces(b, n):
        t = _dot(sel, p)
        acc = t if acc is None else acc + t
    return acc


def _sigmoid(x):
    return 1.0 / (1.0 + jnp.exp(-x))


def _softplus(x):
    return jnp.maximum(x, 0.0) + jnp.log(1.0 + jnp.exp(-jnp.abs(x)))


def _params(sem=None, vmem=VMEM_LIMIT):
    kw = dict(vmem_limit_bytes=vmem)
    if sem is not None:
        kw["dimension_semantics"] = sem
    return pltpu.CompilerParams(**kw)


def _full(shape):
    nd = len(shape)
    return pl.BlockSpec(shape, lambda *_: (0,) * nd)


def group_weights(wt):
    return dict(
        gate=wt[10528:13600],
        sgu=wt[2304:5376],
        att=jnp.concatenate([wt[0:1024], wt[1280:2304], wt[1024:1280]], axis=0),
        ssd=jnp.concatenate([wt[5376:10496], wt[10496:10528], jnp.zeros((224, D), wt.dtype)], axis=0))


def ungroup_grads(g):
    a, s = g["att"], g["ssd"]
    return jnp.concatenate([a[0:1024], a[2048:2304], a[1024:2048], g["sgu"], s[0:5152], g["gate"]], axis=0)


def _bucket_table():
    qi = np.arange(L)[:, None]
    kj = np.arange(2 * L)[None, :]
    dist = np.maximum(qi + L - kj, 0)
    dist_f = np.maximum(dist, 1).astype(np.float32)
    large = 16 + (np.log(dist_f / np.float32(16)) / np.float32(math.log(128 / 16)) * np.float32(16)).astype(np.int32)
    large = np.minimum(large, 31)
    return np.where(dist < 16, dist, large).astype(np.int32)


def bias_table(rel_bias):
    buckets = jnp.asarray(_bucket_table().reshape(1, L * 2 * L))

    def body(rb_ref, bk_ref, out_ref):
        onehot = (lax.broadcasted_iota(jnp.int32, (32, L * 2 * L), 0) == bk_ref[...]).astype(F32)
        out_ref[...] = lax.dot_general(rb_ref[...], onehot, (((0,), (0,)), ((), ())),
                                       preferred_element_type=F32, precision=lax.Precision.HIGHEST)

    out = pl.pallas_call(
        body, name="bias_table",
        out_shape=jax.ShapeDtypeStruct((HEADS, L * 2 * L), F32),
        compiler_params=_params(),
    )(rel_bias, buckets)
    out = out.reshape(HEADS, L, 2 * L)
    win = _window_mask()
    first = win & (np.arange(2 * L)[None, :] >= L)
    return jnp.stack([jnp.where(first, out, NEG), jnp.where(win, out, NEG)])


def _window_mask():
    dist = np.arange(L)[:, None] + L - np.arange(2 * L)[None, :]
    return (dist >= 0) & (dist < L)


def bias_grad(dbias):
    buckets = jnp.asarray(_bucket_table().reshape(1, L * 2 * L))

    def body(db_ref, bk_ref, out_ref):
        onehot = (lax.broadcasted_iota(jnp.int32, (32, L * 2 * L), 0) == bk_ref[...]).astype(F32)
        out_ref[...] = lax.dot_general(onehot, db_ref[...], (((1,), (1,)), ((), ())),
                                       preferred_element_type=F32, precision=lax.Precision.HIGHEST)

    return pl.pallas_call(
        body, name="bias_grad",
        out_shape=jax.ShapeDtypeStruct((32, HEADS), F32),
        compiler_params=_params(),
    )(dbias.reshape(HEADS, L * 2 * L), buckets)


def _row_tile(S):
    return 1024 if S % 1024 == 0 else 512


def inproj_first(x, g_pre, wt, tn, name):
    S, W = x.shape[0], wt.shape[0]
    tm = _row_tile(S)

    def body(x_ref, g_ref, w_ref, o_ref, h_ref):
        @pl.when(pl.program_id(1) == 0)
        def _():
            xv = x_ref[...]
            r = lax.rsqrt(jnp.mean(xv * xv, axis=-1, keepdims=True) + EPS)
            h_ref[...] = (xv * r * g_ref[...]).astype(BF16)
        o_ref[...] = _dot_nt(h_ref[...], w_ref[...]).astype(BF16)

    return pl.pallas_call(
        body, name=name, grid=(S // tm, W // tn),
        in_specs=[pl.BlockSpec((tm, D), lambda i, j: (i, 0)), _full((1, D)),
                  pl.BlockSpec((tn, D), lambda i, j: (j, 0))],
        out_specs=[pl.BlockSpec((tm, tn), lambda i, j: (i, j)), pl.BlockSpec((tm, D), lambda i, j: (i, 0))],
        out_shape=[jax.ShapeDtypeStruct((S, W), BF16), jax.ShapeDtypeStruct((S, D), BF16)],
        compiler_params=_params(("arbitrary", "arbitrary")),
    )(x, g_pre, wt)


def inproj_group(h, wt, tn, name, dtype):
    S, W = h.shape[0], wt.shape[0]
    tm = _row_tile(S)

    def body(h_ref, w_ref, o_ref):
        o_ref[...] = _dot_nt(h_ref[...], w_ref[...]).astype(dtype)

    return pl.pallas_call(
        body, name=name, grid=(S // tm, W // tn),
        in_specs=[pl.BlockSpec((tm, D), lambda i, j: (i, 0)), pl.BlockSpec((tn, D), lambda i, j: (j, 0))],
        out_specs=pl.BlockSpec((tm, tn), lambda i, j: (i, j)),
        out_shape=jax.ShapeDtypeStruct((S, W), dtype),
        compiler_params=_params(("arbitrary", "arbitrary")),
    )(h, wt)


def dh_group(dp, wt, acc, tk, name):
    S, W = dp.shape
    tm = _row_tile(S)

    def body(*refs):
        dp_ref, w_ref, o_ref = refs[0], refs[1], refs[-1]
        first = pl.program_id(1) == 0
        if acc is None:
            @pl.when(first)
            def _():
                o_ref[...] = jnp.zeros_like(o_ref)
        else:
            @pl.when(first)
            def _():
                o_ref[...] = refs[2][...]
        o_ref[...] += _dot(dp_ref[...], w_ref[...])

    row = pl.BlockSpec((tm, D), lambda i, k: (i, 0))
    return pl.pallas_call(
        body, name=name, grid=(S // tm, W // tk),
        in_specs=[pl.BlockSpec((tm, tk), lambda i, k: (i, k)), pl.BlockSpec((tk, D), lambda i, k: (k, 0))]
        + ([] if acc is None else [row]),
        out_specs=row, out_shape=jax.ShapeDtypeStruct((S, D), F32),
        input_output_aliases={} if acc is None else {2: 0},
        compiler_params=_params(("arbitrary", "arbitrary")),
    )(*((dp, wt) if acc is None else (dp, wt, acc)))


def dh_last(dp, wt, acc_in, x, g_pre, dy, tk, name):
    S, W = dp.shape
    tm = 512
    nk = W // tk

    def body(dp_ref, w_ref, a_ref, x_ref, g_ref, dy_ref, dx_ref, dg_ref, acc):
        i, k = pl.program_id(0), pl.program_id(1)

        @pl.when(k == 0)
        def _():
            acc[...] = a_ref[...]

        acc[...] += _dot(dp_ref[...], w_ref[...])

        @pl.when((k == nk - 1) & (i == 0))
        def _():
            dg_ref[...] = jnp.zeros_like(dg_ref)

        @pl.when(k == nk - 1)
        def _():
            xv = x_ref[...]
            dh = acc[...]
            g = g_ref[...]
            r = lax.rsqrt(jnp.mean(xv * xv, axis=-1, keepdims=True) + EPS)
            dhg = dh * g
            dx_ref[...] = dy_ref[...] + r * dhg - xv * (r * r * r) * jnp.mean(dhg * xv, axis=-1, keepdims=True)
            dg_ref[...] += jnp.sum(dh * xv * r, axis=0, keepdims=True)

    row = pl.BlockSpec((tm, D), lambda i, k: (i, 0))
    return pl.pallas_call(
        body, name=name, grid=(S // tm, nk),
        in_specs=[pl.BlockSpec((tm, tk), lambda i, k: (i, k)), pl.BlockSpec((tk, D), lambda i, k: (k, 0)),
                  row, row, _full((1, D)), row],
        out_specs=[row, _full((1, D))],
        out_shape=[jax.ShapeDtypeStruct((S, D), F32), jax.ShapeDtypeStruct((1, D), F32)],
        scratch_shapes=[pltpu.VMEM((tm, D), F32)],
        compiler_params=_params(("arbitrary", "arbitrary")),
    )(dp, wt, acc_in, x, g_pre, dy)


def dw_group(dp, h, tn, name):
    S, W = dp.shape
    ts = _row_tile(S)

    def body(dp_ref, h_ref, o_ref):
        @pl.when(pl.program_id(1) == 0)
        def _():
            o_ref[...] = jnp.zeros_like(o_ref)
        o_ref[...] += _dot_t(dp_ref[...], h_ref[...])

    return pl.pallas_call(
        body, name=name, grid=(W // tn, S // ts),
        in_specs=[pl.BlockSpec((ts, tn), lambda j, s: (s, j)), pl.BlockSpec((ts, D), lambda j, s: (s, 0))],
        out_specs=pl.BlockSpec((tn, D), lambda j, s: (j, 0)),
        out_shape=jax.ShapeDtypeStruct((W, D), F32),
        compiler_params=_params(("arbitrary", "arbitrary")),
    )(dp, h)


def matmul_tn(a, b, name, tn=1024):
    S, K = a.shape
    N = b.shape[1]
    ts = _row_tile(S)
    ns = S // ts

    def body(a_ref, b_ref, o_ref):
        @pl.when(pl.program_id(1) == 0)
        def _():
            o_ref[...] = jnp.zeros_like(o_ref)
        o_ref[...] += _dot_t(a_ref[...], b_ref[...])

    return pl.pallas_call(
        body, name=name, grid=(N // tn, ns),
        in_specs=[pl.BlockSpec((ts, K), lambda j, s: (s, 0)), pl.BlockSpec((ts, tn), lambda j, s: (s, j))],
        out_specs=pl.BlockSpec((K, tn), lambda j, s: (0, j)),
        out_shape=jax.ShapeDtypeStruct((K, N), F32),
        compiler_params=_params(("arbitrary", "arbitrary")),
    )(a, b)


def _att_in_specs(nb):
    last = nb - 1
    cur = lambda n: jnp.minimum(n, last)
    prev = lambda n: jnp.maximum(jnp.minimum(n, last) - 1, 0)
    return [
        pl.BlockSpec((L, 1024), lambda n: (cur(n), 0)),
        pl.BlockSpec((L, 128), lambda n: (prev(n), 16)),
        pl.BlockSpec((L, 128), lambda n: (cur(n), 16)),
        pl.BlockSpec((L, 128), lambda n: (prev(n), 17)),
        pl.BlockSpec((L, 128), lambda n: (cur(n), 17)),
        pl.BlockSpec((L, 1024), lambda n: (cur(n), 1)),
        _full((2, HEADS, L, 2 * L)),
        pl.BlockSpec(memory_space=pltpu.SMEM),
    ]


GH = HEADS // KV
GB = 8


def _stack_heads(ref, h0, nh, scr):
    for g in range(nh):
        scr[(h0 + g) * L:(h0 + g + 1) * L, :] = ref[:, (h0 + g) * DH:(h0 + g + 1) * DH].astype(F32)
    return scr[h0 * L:(h0 + nh) * L, :]


def _unstack_heads(val, h0, nh, ref):
    for g in range(nh):
        ref[:, (h0 + g) * DH:(h0 + g + 1) * DH] = val[g * L:(g + 1) * L, :]


def _sink_rows(s_ref, h0, nh):
    return jnp.concatenate([jnp.full((L, 1), s_ref[h0 + g], F32) for g in range(nh)], axis=0)


def _att_probs(qh, kk, bias_h, sk):
    logits = _dot_nt(qh, kk) + bias_h
    m =jnp.maximum(jnp.max(logits, axis=-1, keepdims=True), sk)
    p = jnp.exp(logits - m)
    es = jnp.exp(sk - m)
    den = jnp.sum(p, axis=-1, keepdims=True) + es
    return p / den, es / den


def att_fwd(proj, bias, sinks):
    S = proj.shape[0]
    nb = S // L

    def body(q_ref, kp_ref, kc_ref, vp_ref, vc_ref, z_ref, bias_ref, s_ref, y_ref, o_scr):
        table = jnp.where(pl.program_id(0) > 0, 1, 0)
        for kv in range(KV):
            sl = slice(kv * DH, (kv + 1) * DH)
            kk = jnp.concatenate([kp_ref[:, sl], kc_ref[:, sl]], axis=0).astype(BF16)
            vv = jnp.concatenate([vp_ref[:, sl], vc_ref[:, sl]], axis=0).astype(BF16)
            for g in range(GH):
                h = kv * GH + g
                hs = slice(h * DH, (h + 1) * DH)
                qh = (q_ref[:, hs] * 0.125).astype(BF16)
                P, _ = _att_probs(qh, kk, bias_ref[table, h], s_ref[h])
                o_scr[:, hs] = _dot(P.astype(BF16), vv)
        z = z_ref[...].astype(F32)
        y_ref[...] = (o_scr[...] * (z * _sigmoid(z))).astype(BF16)

    return pl.pallas_call(
        body, name="att_fwd", grid=(nb,),
        in_specs=_att_in_specs(nb),
        out_specs=pl.BlockSpec((L, 1024), lambda n: (n, 0)),
        out_shape=jax.ShapeDtypeStruct((S, 1024), BF16),
        scratch_shapes=[pltpu.VMEM((L, 1024), F32)],
        compiler_params=_params(("arbitrary",)),
    )(proj, proj, proj, proj, proj, proj, bias, sinks)


def att_bwd(dy, proj, bias, sinks):
    S = proj.shape[0]
    nb = S // L
    last = nb - 1

    def body(dy_ref, q_ref, kp_ref, kc_ref, vp_ref, vc_ref, z_ref, bias_ref, s_ref,
             dout_ref, dbias_ref, dsink_ref, carry, band, dq_scr, dz_scr, qs_scr, zs_scr, dys_scr):
        n = pl.program_id(0)

        @pl.when(n == 0)
        def _():
            carry[...] = jnp.zeros_like(carry)
            dq_scr[...] = jnp.zeros_like(dq_scr)
            dz_scr[...] = jnp.zeros_like(dz_scr)
            dbias_ref[...] = jnp.zeros_like(dbias_ref)
            dsink_ref[...] = jnp.zeros_like(dsink_ref)

        dout_ref[:, 0:1024] = dq_scr[...].astype(BF16)
        dout_ref[:, 1024:2048] = dz_scr[...].astype(BF16)
        band[...] = jnp.zeros_like(band)

        @pl.when(n < nb)
        def _():
            table = jnp.where(n > 0, 1, 0)
            lane = lax.broadcasted_iota(jnp.int32, (1, 128), 1)
            dsink = jnp.zeros((1, 128), F32)
            for kv in range(KV):
                sl = slice(kv * DH, (kv + 1) * DH)
                kk = jnp.concatenate([kp_ref[:, sl], kc_ref[:, sl]], axis=0).astype(BF16)
                vv = jnp.concatenate([vp_ref[:, sl], vc_ref[:, sl]], axis=0).astype(BF16)
                dk_acc = jnp.zeros((2 * L, DH), F32)
                dv_acc = jnp.zeros((2 * L, DH), F32)
                for h0 in range(kv * GH, (kv + 1) * GH, GB):
                    qs = (_stack_heads(q_ref, h0, GB, qs_scr) * 0.125).astype(BF16)
                    bias_g = bias_ref[table, h0:h0 + GB].reshape(GB * L, 2 * L)
                    P, psink = _att_probs(qs, kk, bias_g, _sink_rows(s_ref, h0, GB))
                    zs = _stack_heads(z_ref, h0, GB, zs_scr)
                    dys = _stack_heads(dy_ref, h0, GB, dys_scr)
                    sg = _sigmoid(zs)
                    O = _dot(P.astype(BF16), vv)
                    _unstack_heads(dys * O * (sg * (1.0 + zs * (1.0 - sg))), h0, GB, dz_scr)
                    dOb = (dys * (zs * sg)).astype(BF16)
                    dP = _dot_nt(dOb, vv)
                    delta = jnp.sum(P * dP, axis=-1, keepdims=True)
                    dS = P * (dP - delta)
                    sd = psink * delta
                    for g in range(GB):
                        dsink = dsink + jnp.where(lane == h0 + g, -jnp.sum(sd[g * L:(g + 1) * L, :]), 0.0)
                    _unstack_heads(_dot(dS.astype(BF16), kk) * 0.125, h0, GB, dq_scr)
                    dbias_ref[h0:h0 + GB] += dS.reshape(GB, L, 2 * L)
                    dk_acc = dk_acc + _dot_tn(dS, qs)
                    dv_acc = dv_acc + _dot_tn(P, dOb)
                band[:, sl] = dk_acc
                band[:, 128 + kv * DH:128 + (kv + 1) * DH] = dv_acc
            dsink_ref[...] += dsink

        out = carry[...] + band[0:L, :]
        dout_ref[:, 2048:2304] = out.astype(BF16)
        carry[...] = band[L:2 * L, :]

    cur = lambda n: jnp.minimum(n, last)
    lag = lambda n: jnp.maximum(n - 1, 0)
    return pl.pallas_call(
        body, name="att_bwd", grid=(nb + 1,),
        in_specs=[pl.BlockSpec((L, 1024), lambda n: (cur(n), 0))] + _att_in_specs(nb),
        out_specs=[pl.BlockSpec((L, 2304), lambda n: (lag(n), 0)), _full((HEADS, L, 2 * L)), _full((1, 128))],
        out_shape=[jax.ShapeDtypeStruct((S, 2304), BF16),
                   jax.ShapeDtypeStruct((HEADS, L, 2 * L), F32), jax.ShapeDtypeStruct((1, 128), F32)],
        scratch_shapes=[pltpu.VMEM((L, 256), F32), pltpu.VMEM((2 * L, 256), F32),
                        pltpu.VMEM((L, 1024), F32), pltpu.VMEM((L, 1024), F32)]
        + [pltpu.VMEM((HEADS * L, DH), F32)] * 3,
        compiler_params=_params(("arbitrary",)),
    )(dy, proj, proj, proj, proj, proj, proj, bias, sinks)


def _sgu_in_specs():
    return [
        pl.BlockSpec((L, 1024), lambda c: (c, 0)),
        pl.BlockSpec((L, 1024), lambda c: (c, 1)),
        pl.BlockSpec((L, 1024), lambda c: (c, 2)),
        _full((1, 1024)), _full((1, 1024)), _full((8, L, L)), _full((L, 8)),
    ]


def _sgu_norm(v, lg, lb):
    mu = jnp.mean(v, axis=-1, keepdims=True)
    vc = v - mu
    rstd = lax.rsqrt(jnp.mean(vc * vc, axis=-1, keepdims=True) + EPS)
    xhat = vc * rstd
    return xhat * lg + lb, xhat, rstd


def _tril():
    return lax.broadcasted_iota(jnp.int32, (L, L), 0) >= lax.broadcasted_iota(jnp.int32, (L, L), 1)


def sgu_fwd(proj, ln_g, ln_b, w, b_t):
    S = proj.shape[0]

    def body(u_ref, v_ref, z_ref, lg_ref, lb_ref, w_ref, bt_ref, y_ref):
        vn, _, _ = _sgu_norm(v_ref[...].astype(F32), lg_ref[...], lb_ref[...])
        tri = _tril()
        parts = []
        for g in range(8):
            wg = jnp.where(tri, w_ref[g], 0.0).astype(BF16)
            parts.append(_dot(wg, vn[:, g * 128:(g + 1) * 128].astype(BF16)) + bt_ref[:, g:g + 1])
        mixed = jnp.concatenate(parts, axis=1)
        z = z_ref[...].astype(F32)
        y_ref[...] = (u_ref[...].astype(F32) * mixed * (z * _sigmoid(z))).astype(BF16)

    return pl.pallas_call(
        body, name="sgu_fwd", grid=(S // L,),
        in_specs=_sgu_in_specs(),
        out_specs=pl.BlockSpec((L, 1024), lambda c: (c, 0)),
        out_shape=jax.ShapeDtypeStruct((S, 1024), BF16),
        compiler_params=_params(("arbitrary",)),
    )(proj, proj, proj, ln_g, ln_b, w, b_t)


def sgu_bwd(dy, proj, ln_g, ln_b, w, b_t):
    S = proj.shape[0]

    def body(dy_ref, u_ref, v_ref, z_ref, lg_ref, lb_ref, w_ref, bt_ref,
             dout_ref, dw_ref, dbt_ref, dlg_ref, dlb_ref):
        @pl.when(pl.program_id(0) == 0)
        def _():
            dw_ref[...] = jnp.zeros_like(dw_ref)
            dbt_ref[...] = jnp.zeros_like(dbt_ref)
            dlg_ref[...] = jnp.zeros_like(dlg_ref)
            dlb_ref[...] = jnp.zeros_like(dlb_ref)

        lg = lg_ref[...]
        vn, xhat, rstd = _sgu_norm(v_ref[...].astype(F32), lg, lb_ref[...])
        tri = _tril()
        lane = lax.broadcasted_iota(jnp.int32, (L, 128), 1)
        wgs, parts = [], []
        for g in range(8):
            wg = jnp.where(tri, w_ref[g], 0.0)
            wgs.append(wg)
            parts.append(_dot(wg.astype(BF16), vn[:, g * 128:(g + 1) * 128].astype(BF16)) + bt_ref[:, g:g + 1])
        mixed = jnp.concatenate(parts, axis=1)
        z = z_ref[...].astype(F32)
        sg = _sigmoid(z)
        silu = z * sg
        dy_v = dy_ref[...]
        u = u_ref[...].astype(F32)
        dout_ref[:, 0:1024] = (dy_v * mixed * silu).astype(BF16)
        dout_ref[:, 2048:3072] = (dy_v * u * mixed * (sg * (1.0 + z * (1.0 - sg)))).astype(BF16)
        dmixed = dy_v * u * silu
        dbt = jnp.zeros((L, 128), F32)
        dvn_parts = []
        for g in range(8):
            dm = dmixed[:, g * 128:(g + 1) * 128]
            dmb = dm.astype(BF16)
            dbt = dbt + jnp.where(lane == g, jnp.sum(dm, axis=1, keepdims=True), 0.0)
            dw_ref[g] += jnp.where(tri, _dot_nt(dmb, vn[:, g * 128:(g + 1) * 128].astype(BF16)), 0.0)
            dvn_parts.append(_dot_tn(wgs[g], dmb))
        dbt_ref[...] += dbt
        dvn = jnp.concatenate(dvn_parts, axis=1)
        dlg_ref[...] += jnp.sum(dvn * xhat, axis=0, keepdims=True)
        dlb_ref[...] += jnp.sum(dvn, axis=0, keepdims=True)
        dxh = dvn * lg
        dv = rstd * (dxh - jnp.mean(dxh, axis=-1, keepdims=True)
                     - xhat * jnp.mean(dxh * xhat, axis=-1, keepdims=True))
        dout_ref[:, 1024:2048] = dv.astype(BF16)

    return pl.pallas_call(
        body, name="sgu_bwd", grid=(S // L,),
        in_specs=[pl.BlockSpec((L, 1024), lambda c: (c, 0))] + _sgu_in_specs(),
        out_specs=[pl.BlockSpec((L, 3072), lambda c: (c, 0)), _full((8, L, L)), _full((L, 128)),
                   _full((1, 1024)), _full((1, 1024))],
        out_shape=[jax.ShapeDtypeStruct((S, 3072), BF16), jax.ShapeDtypeStruct((8, L, L), F32),
                   jax.ShapeDtypeStruct((L, 128), F32), jax.ShapeDtypeStruct((1, 1024), F32),
                   jax.ShapeDtypeStruct((1, 1024), F32)],
        compiler_params=_params(("arbitrary",)),
    )(dy, proj, proj, proj, ln_g, ln_b, w, b_t)


def _expand_matrices():
    e = (np.arange(SSM_W)[None, :] // SSM_P == np.arange(128)[:, None]).astype(np.float32)
    return jnp.asarray(e, BF16), jnp.asarray(e.T, BF16)


def _rows_from(ref, start):
    C = ref.shape[1]
    tiles = ref[...].reshape(17, 8, C)
    q, s = divmod(start, 8)
    if s == 0:
        return tiles[q:q + 16].reshape(L, C)
    rolled = pltpu.roll(tiles, 8 - s, axis=1)
    sub = lax.broadcasted_iota(jnp.int32, (16, 8, C), 1)
    return jnp.where(sub < 8 - s, rolled[q:q + 16], rolled[q + 1:q + 17]).reshape(L, C)


def _ssd_common(ext_ref, cw_ref, cb_ref, dt_raw, dtb, alog):
    taps = [_rows_from(ext_ref, 5 + k) for k in range(CONV_K)]
    pre = cb_ref[...]
    for k in range(CONV_K):
        pre = pre + cw_ref[k:k + 1, :] * taps[k]
    sg_pre = _sigmoid(pre)
    xc = pre * sg_pre
    dt = _softplus(dt_raw + dtb)
    a = -jnp.exp(alog)
    adt = dt * a
    acs = _sel_dot(_tril(), adt, 3)
    return pre, sg_pre, xc, dt, a, acs, taps


def _ssd_in_specs(rev, nc):
    cidx = (lambda c: nc - 1 - c) if rev else (lambda c: c)
    return [
        pl.BlockSpec((L, 2048), lambda c: (cidx(c), 0)),
        pl.BlockSpec((L, 1024), lambda c: (cidx(c), 2)),
        pl.BlockSpec((L, 1024), lambda c: (cidx(c), 3)),
        pl.BlockSpec((L, 1024), lambda c: (cidx(c), 4)),
        pl.BlockSpec((L, 128), lambda c: (cidx(c), 40)),
        _full((8, CONV_C)), _full((1, CONV_C)), _full((1, 128)), _full((1, 128)), _full((1, 128)),
        _full((1, SSM_W)), _full((128, SSM_W)), _full((SSM_W, 128)),
    ]


def ssd_fwd(proj, conv_w, conv_b, dt_bias, a_log, d_skip, norm_g):
    S = proj.shape[0]
    nc = S // L

    def body(z_ref, xa_ref, xb_ref, xc_ref, dt_ref, cw_ref, cb_ref, dtb_ref, alog_ref, dsk_ref, ng_ref,
             ex_ref, ext_ref, y_ref, hs_ref, H, ext, ysc):
        @pl.when(pl.program_id(0) == 0)
        def _():
            H[...] = jnp.zeros_like(H)
            ext[0:8, :] = jnp.zeros((8, CONV_C), F32)

        for k, ref in enumerate((xa_ref, xb_ref, xc_ref)):
            ext[8:8 + L, k * 1024:(k + 1) * 1024] = ref[...].astype(F32)
        pre, sg_pre, xc, dt, a, acs, _ = _ssd_common(ext, cw_ref, cb_ref, dt_ref[...].astype(F32), dtb_ref[...],
                                                     alog_ref[...])
        for k, ref in enumerate((xa_ref, xb_ref, xc_ref)):
            ext[0:8, k * 1024:(k + 1) * 1024] = ref[L - 8:L, :].astype(F32)
        xs = xc[:, 0:SSM_W]
        acs_t = acs.T
        ex = ex_ref[...]
        dt_x = _dot_sel(dt, ex, 2)
        xdt = xs * dt_x
        eacs_x = _dot_sel(jnp.exp(acs), ex, 2)
        xw = xdt * _dot_sel(jnp.exp(acs[L - 1:L, :] - acs), ex, 2)
        cd_row = jnp.exp(acs[L - 1:L, :])
        hs_ref[0] = H[...]
        tri = _tril()
        for g in range(SSM_G):
            gs = slice(g * 512, (g + 1) * 512)
            bg = xc[:, SSM_W + g * SSM_N:SSM_W + (g + 1) * SSM_N].astype(BF16)
            cg = xc[:, SSM_W + 512 + g * SSM_N:SSM_W + 512 + (g + 1) * SSM_N].astype(BF16)
            G = _dot_nt(cg, bg)
            yoff = _dot_nt(cg, H[gs, :].astype(BF16)) * eacs_x[:, gs]
            Sg = _dot_tn(xw[:, gs], bg)
            for j in range(8):
                hh = g * 8 + j
                hs = slice(hh * SSM_P, (hh + 1) * SSM_P)
                seg = acs[:, hh:hh + 1] - acs_t[hh:hh + 1, :]
                dk = jnp.where(tri, jnp.exp(jnp.minimum(seg, 0.0)), 0.0)
                yd = _dot((G * dk).astype(BF16), xdt[:, hs].astype(BF16))
                ysc[:, hs] = yd + yoff[:, j * SSM_P:(j + 1) * SSM_P]
                H[hs, :] = H[hs, :] * cd_row[:, hh:hh + 1] + Sg[j * SSM_P:(j + 1) * SSM_P, :]
        d_x = _dot_sel(jnp.broadcast_to(dsk_ref[...], (8, 128)), ex, 3)[0:1, :]
        Y = ysc[...] + d_x * xs
        z = z_ref[...].astype(F32)
        yz = Y * (z * _sigmoid(z))
        ng = ng_ref[...]
        for g in range(SSM_G):
            gs = slice(g * 512, (g + 1) * 512)
            t = yz[:, gs]
            rstd = lax.rsqrt(jnp.mean(t * t, axis=-1, keepdims=True) + EPS)
            y_ref[:, gs] = (t * rstd * ng[:, gs]).astype(BF16)

    return pl.pallas_call(
        body, name="ssd_fwd", grid=(nc,),
        in_specs=_ssd_in_specs(False, nc),
        out_specs=[pl.BlockSpec((L, SSM_W), lambda c: (c, 0)), pl.BlockSpec((1, SSM_W, SSM_N), lambda c: (c, 0, 0))],
        out_shape=[jax.ShapeDtypeStruct((S, SSM_W), BF16), jax.ShapeDtypeStruct((nc, SSM_W, SSM_N), F32)],
        scratch_shapes=[pltpu.VMEM((SSM_W, SSM_N), F32), pltpu.VMEM((8 + L, CONV_C), F32),
                        pltpu.VMEM((L, SSM_W), F32)],
        compiler_params=_params(("arbitrary",)),
    )(proj, proj, proj, proj, proj, conv_w, conv_b, dt_bias, a_log, d_skip, norm_g, *_expand_matrices())


def ssd_bwd(dy, proj, hstates, conv_w, conv_b, dt_bias, a_log, d_skip, norm_g):
    S = proj.shape[0]
    nc = S // L
    cidx = lambda c: nc - 1 - c

    def body(dy_ref, z_ref, xa_ref, xb_ref, xc_ref, dt_ref, cw_ref, cb_ref, dtb_ref, alog_ref, dsk_ref, ng_ref,
             ex_ref, ext_ref, pa_ref, pb_ref, pc_ref, hp_ref,
             dout_ref, dcw_ref, dcb_ref, ddtb_ref, dalog_ref, ddsk_ref, dng_ref,
             dH, ext, dext, ysc, yoffsc, dxdt, dxc, tsc):
        step = pl.program_id(0)
        c = nc - 1 - step

        @pl.when(step == 0)
        def _():
            dH[...] = jnp.zeros_like(dH)
            dext[L:L + 8, :] = jnp.zeros((8, CONV_C), F32)
            for r in (dcw_ref, dcb_ref, ddtb_ref, dalog_ref, ddsk_ref, dng_ref):
                r[...] = jnp.zeros_like(r)

        for k, (ref, prev) in enumerate(((xa_ref, pa_ref), (xb_ref, pb_ref), (xc_ref, pc_ref))):
            ext[0:8, k * 1024:(k + 1) * 1024] = jnp.where(c > 0, prev[8:16, :].astype(F32), 0.0)
            ext[8:8 + L, k * 1024:(k + 1) * 1024] = ref[...].astype(F32)
        dtb = dtb_ref[...]
        dt_raw = dt_ref[...].astype(F32)
        pre, sg_pre, xc, dt, a, acs, taps = _ssd_common(ext, cw_ref, cb_ref, dt_raw, dtb, alog_ref[...])
        xs = xc[:, 0:SSM_W]
        acs_t = acs.T
        ex = ex_ref[...]
        dt_x = _dot_sel(dt, ex, 2)
        xdt = xs * dt_x
        eacs_x = _dot_sel(jnp.exp(acs), ex, 2)
        dte_x = _dot_sel(jnp.exp(acs[L - 1:L, :] - acs), ex, 2)
        xw = xdt * dte_x
        cd_row = jnp.exp(acs[L - 1:L, :])
        tri = _tril()

        Gs, Cs, Bs = [], [], []
        for g in range(SSM_G):
            gs = slice(g * 512, (g + 1) * 512)
            bg = xc[:, SSM_W + g * SSM_N:SSM_W + (g + 1) * SSM_N].astype(BF16)
            cg = xc[:, SSM_W + 512 + g * SSM_N:SSM_W + 512 + (g + 1) * SSM_N].astype(BF16)
            G = _dot_nt(cg, bg)
            Gs.append(G), Cs.append(cg), Bs.append(bg)
            yoffsc[:, gs] = _dot_nt(cg, hp_ref[0, gs, :].astype(BF16)) * eacs_x[:, gs]
            for j in range(8):
                hh = g * 8 + j
                hs = slice(hh * SSM_P, (hh + 1) * SSM_P)
                seg = acs[:, hh:hh + 1] - acs_t[hh:hh + 1, :]
                dk = jnp.where(tri, jnp.exp(jnp.minimum(seg, 0.0)), 0.0)
                ysc[:, hs] = _dot((G * dk).astype(BF16), xdt[:, hs].astype(BF16))
        d_x = _dot_sel(jnp.broadcast_to(dsk_ref[...], (8, 128)), ex, 3)[0:1, :]
        yoff = yoffsc[...]
        Y = ysc[...] + yoff + d_x * xs

        z = z_ref[...].astype(F32)
        sgz = _sigmoid(z)
        silu_z = z * sgz
        yz = Y * silu_z
        ng = ng_ref[...]
        dout = dy_ref[...]
        dyn = dout * ng
        dyz_parts, dng_parts = [], []
        for g in range(SSM_G):
            gs = slice(g * 512, (g + 1) * 512)
            t = yz[:, gs]
            rstd = lax.rsqrt(jnp.mean(t * t, axis=-1, keepdims=True) + EPS)
            dng_parts.append(jnp.sum(dout[:, gs] * t * rstd, axis=0, keepdims=True))
            dn = dyn[:, gs]
            dyz_parts.append(rstd * dn - t * (rstd * rstd * rstd) * jnp.mean(dn * t, axis=-1, keepdims=True))
        dng_ref[...] += jnp.concatenate(dng_parts, axis=1)
        dyz = jnp.concatenate(dyz_parts, axis=1)
        dY = dyz * silu_z
        dout_ref[:, 0:SSM_W] = (dyz * Y * (sgz * (1.0 + z * (1.0 - sgz)))).astype(BF16)

        ex_t = ext_ref[...]
        ddsk_ref[...] += _dot_sel(jnp.broadcast_to(jnp.sum(dY * xs, axis=0, keepdims=True), (8, SSM_W)), ex_t, 3)[0:1, :]

        lane = lax.broadcasted_iota(jnp.int32, (L, 128), 1)
        subl = lax.broadcasted_iota(jnp.int32, (128, L), 0)
        coll = lax.broadcasted_iota(jnp.int32, (128, L), 1)
        r_cols = jnp.zeros((L, 128), F32)
        c_rows = jnp.zeros((128, L), F32)
        for g in range(SSM_G):
            gs = slice(g * 512, (g + 1) * 512)
            G, cg, bg = Gs[g], Cs[g], Bs[g]
            hp_g = hp_ref[0, gs, :]
            dh_g = dH[gs, :]
            dY_g = dY[:, gs]
            dZ = dY_g * eacs_x[:, gs]
            dZb = dZ.astype(BF16)
            dC = _dot(dZb, hp_g.astype(BF16))
            dh_from_off = _dot_tn(dZ, cg)
            dhb = dh_g.astype(BF16)
            Q = _dot_nt(bg, dhb)
            dB = _dot(xw[:, gs].astype(BF16), dhb)
            qd = Q * dte_x[:, gs]
            dxdt[:, gs] = qd
            tsc[:, gs] = qd * xdt[:, gs]
            dG = jnp.zeros((L, L), F32)
            for j in range(8):
                hh = g * 8 + j
                hs = slice(hh * SSM_P, (hh + 1) * SSM_P)
                seg = acs[:, hh:hh + 1] - acs_t[hh:hh + 1, :]
                dk = jnp.where(tri, jnp.exp(jnp.minimum(seg, 0.0)), 0.0)
                M = G * dk
                dYh = dY[:, hs]
                dYhb = dYh.astype(BF16)
                dM = _dot_nt(dYhb, xdt[:, hs].astype(BF16))
                dxdt[:, hs] += _dot_tn(M, dYhb)
                dG = dG + dM * dk
                Wm = dM * M
                r_cols = r_cols + jnp.where(lane == hh, jnp.sum(Wm, axis=1, keepdims=True), 0.0)
                c_rows = c_rows + jnp.where(subl == hh, jnp.sum(Wm, axis=0, keepdims=True), 0.0)
                pj = slice(j * SSM_P, (j + 1) * SSM_P)
                cd_h = cd_row[:, hh:hh + 1]
                dcd = jnp.sum(dh_g[pj, :] * hp_g[pj, :]) * cd_h
                c_rows = c_rows - jnp.where((subl == hh) & (coll == L - 1), dcd, 0.0)
                dH[hs, :] = dh_g[pj, :] * cd_h + dh_from_off[pj, :]
            dGb = dG.astype(BF16)
            dC = dC + _dot(dGb, bg)
            dB = dB + _dot_tn(dG, cg)
            dxc[:, SSM_W + g * SSM_N:SSM_W + (g + 1) * SSM_N] = dB
            dxc[:, SSM_W + 512 + g * SSM_N:SSM_W + 512 + (g + 1) * SSM_N] = dC

        row = lax.broadcasted_iota(jnp.int32, (L, 128), 0)
        tv = tsc[...]
        t_last = _dot_sel(jnp.broadcast_to(jnp.sum(tv, axis=0, keepdims=True), (8, SSM_W)), ex_t, 3)[0:1, :]
        dacs = (r_cols - c_rows.T + _dot_sel(dY * yoff - tv, ex_t, 2) + jnp.where(row == L - 1, t_last, 0.0))
        triu = lax.broadcasted_iota(jnp.int32, (L, L), 0) <= lax.broadcasted_iota(jnp.int32, (L, L), 1)
        dadt = _sel_dot(triu, dacs, 3)
        dxdt_v = dxdt[...]
        ddt = _dot_sel(dxdt_v * xs, ex_t, 2) + dadt * a
        dalog_ref[...] += jnp.sum(dadt * dt * a, axis=0, keepdims=True)
        ddt_raw = jnp.where(lane < SSM_H, ddt * _sigmoid(dt_raw + dtb), 0.0)
        ddtb_ref[...] += jnp.sum(ddt_raw, axis=0, keepdims=True)
        dout_ref[:, 5120:5248] = ddt_raw.astype(BF16)
        dout_ref[:, 5248:5376] = jnp.zeros((L, 128), BF16)

        dxc[:, 0:SSM_W] = dxdt_v * dt_x + d_x * dY
        dpre = dxc[...] * (sg_pre * (1.0 + pre * (1.0 - sg_pre)))
        dcb_ref[...] += jnp.sum(dpre, axis=0, keepdims=True)
        dext[0:L, :] = dpre
        x_cur = ext[8:8 + L, :]
        dx = None
        for k in range(CONV_K):
            dsh = _rows_from(dext, 3 - k)
            term = cw_ref[k:k + 1, :] * dsh
            dx = term if dx is None else dx + term
            dcw_ref[k:k + 1, :] += jnp.sum(dsh * x_cur, axis=0, keepdims=True)
        dout_ref[:, SSM_W:SSM_W + CONV_C] = dx.astype(BF16)
        dext[L:L + 8, :] = dpre[0:8, :]

    big = lambda w: pl.BlockSpec((L, w), lambda c: (cidx(c), 0))
    return pl.pallas_call(
        body, name="ssd_bwd", grid=(nc,),
        in_specs=[big(SSM_W)] + _ssd_in_specs(True, nc) + [
            pl.BlockSpec((16, 1024), lambda c, k=k: (jnp.maximum(8 * cidx(c) - 1, 0), k)) for k in (2, 3, 4)] + [
            pl.BlockSpec((1, SSM_W, SSM_N), lambda c: (cidx(c), 0, 0))],
        out_specs=[big(5376), _full((8, CONV_C)), _full((1, CONV_C)),
                   _full((1, 128)), _full((1, 128)), _full((1, 128)), _full((1, SSM_W))],
        out_shape=[jax.ShapeDtypeStruct((S, 5376), BF16), jax.ShapeDtypeStruct((8, CONV_C), F32),
                   jax.ShapeDtypeStruct((1, CONV_C), F32), jax.ShapeDtypeStruct((1, 128), F32),
                   jax.ShapeDtypeStruct((1, 128), F32), jax.ShapeDtypeStruct((1, 128), F32),
                   jax.ShapeDtypeStruct((1, SSM_W), F32)],
        scratch_shapes=[pltpu.VMEM((SSM_W, SSM_N), F32), pltpu.VMEM((8 + L, CONV_C), F32),
                        pltpu.VMEM((L + 8, CONV_C), F32), pltpu.VMEM((L, SSM_W), F32),
                        pltpu.VMEM((L, SSM_W), F32), pltpu.VMEM((L, SSM_W), F32),
                        pltpu.VMEM((L, CONV_C), F32), pltpu.VMEM((L, SSM_W), F32)],
        compiler_params=_params(("arbitrary",)),
    )(dy, proj, proj, proj, proj, proj, conv_w, conv_b, dt_bias, a_log, d_skip, norm_g, *_expand_matrices(),
      proj, proj, proj, hstates)


def _resident(shape):
    nd = len(shape)
    return pl.BlockSpec(shape, lambda *_: (0,) * nd, pipeline_mode=pl.Buffered(1))


def merge_fwd(y_att, y_sg, y_ssm, proj, x, w_a, w_s, w_m, w_o, g_post):
    S = x.shape[0]
    tm = 256

    def body(ya_ref, ys_ref, ym_ref, gate_ref, x_ref, wa_ref, ws_ref, wm_ref, wo_ref, gp_ref,
             xn_ref, bra_ref, brs_ref, brm_ref, mg_ref, out_ref):
        bra = _dot(ya_ref[...], wa_ref[...])
        brs = _dot(ys_ref[...], ws_ref[...])
        brm = _dot(ym_ref[...], wm_ref[...])
        bra_ref[...] = bra.astype(BF16)
        brs_ref[...] = brs.astype(BF16)
        brm_ref[...] = brm.astype(BF16)
        gate = gate_ref[...].astype(F32)
        merged = (_sigmoid(gate[:, 0:1024]) * bra + _sigmoid(gate[:, 1024:2048]) * brs
                  + _sigmoid(gate[:, 2048:3072]) * brm)
        mb = merged.astype(BF16)
        mg_ref[...] = mb
        o = _dot(mb, wo_ref[...])
        out_ref[...] = o
        r = lax.rsqrt(jnp.mean(o * o, axis=-1, keepdims=True) + EPS)
        xn_ref[...] = x_ref[...] + o * r * gp_ref[...]

    row = lambda w: pl.BlockSpec((tm, w), lambda i: (i, 0))
    return pl.pallas_call(
        body, name="merge_fwd", grid=(S // tm,),
        in_specs=[row(1024), row(1024), row(2048), pl.BlockSpec((tm, 3072), lambda i: (i, 0)),
                  row(D), _resident((1024, D)), _resident((1024, D)), _resident((2048, D)), _resident((D, D)),
                  _full((1, D))],
        out_specs=[row(D)] * 6,
        out_shape=[jax.ShapeDtypeStruct((S, D), F32)] + [jax.ShapeDtypeStruct((S, D), BF16)] * 4
        + [jax.ShapeDtypeStruct((S, D), F32)],
        compiler_params=_params(("arbitrary",)),
    )(y_att, y_sg, y_ssm, proj, x, w_a, w_s, w_m, w_o, g_post)


def merge_bwd(dy, out, g_post, proj, br_a, br_s, br_m, w_a, w_s, w_m, w_o):
    S = dy.shape[0]
    tm = 256

    def body(dy_ref, o_ref, gp_ref, gate_ref, bra_ref, brs_ref, brm_ref, wa_ref, ws_ref, wm_ref, wo_ref,
             dout_ref, dba_ref, dbs_ref, dbm_ref, dgate_ref, dya_ref, dys_ref, dym_ref, dgp_ref):
        @pl.when(pl.program_id(0) == 0)
        def _():
            dgp_ref[...] = jnp.zeros_like(dgp_ref)

        o = o_ref[...]
        dyv = dy_ref[...]
        r = lax.rsqrt(jnp.mean(o * o, axis=-1, keepdims=True) + EPS)
        dyg = dyv * gp_ref[...]
        do = r * dyg - o * (r * r * r) * jnp.mean(dyg * o, axis=-1, keepdims=True)
        dgp_ref[...] += jnp.sum(dyv * o * r, axis=0, keepdims=True)
        dob = do.astype(BF16)
        dout_ref[...] = dob
        dmerged = _dot_nt(dob, wo_ref[...])
        for idx, (br_ref, dbr_ref, w_ref, dyi_ref) in enumerate((
                (bra_ref, dba_ref, wa_ref, dya_ref), (brs_ref, dbs_ref, ws_ref, dys_ref),
                (brm_ref, dbm_ref, wm_ref, dym_ref))):
            s = _sigmoid(gate_ref[:, idx * 1024:(idx + 1) * 1024].astype(F32))
            dbr = (dmerged * s).astype(BF16)
            dbr_ref[...] = dbr
            dgate_ref[:, idx * 1024:(idx + 1) * 1024] = (dmerged * br_ref[...].astype(F32) * s * (1.0 - s)).astype(BF16)
            dyi_ref[...] = _dot_nt(dbr, w_ref[...])

    row = lambda w: pl.BlockSpec((tm, w), lambda i: (i, 0))
    return pl.pallas_call(
        body, name="merge_bwd", grid=(S // tm,),
        in_specs=[row(D), row(D), _full((1, D)), pl.BlockSpec((tm, 3072), lambda i: (i, 0)),
                  row(D), row(D), row(D),
                  _resident((1024, D)), _resident((1024, D)), _resident((2048, D)), _resident((D, D))],
        out_specs=[row(D), row(D), row(D), row(D), row(3072), row(1024), row(1024), row(2048), _full((1, D))],
        out_shape=[jax.ShapeDtypeStruct((S, D), BF16)] * 4 + [
            jax.ShapeDtypeStruct((S, 3072), BF16), jax.ShapeDtypeStruct((S, 1024), F32),
            jax.ShapeDtypeStruct((S, 1024), F32), jax.ShapeDtypeStruct((S, 2048), F32),
            jax.ShapeDtypeStruct((1, D), F32)],
        compiler_params=_params(("arbitrary",)),
    )(dy, out, g_post, proj, br_a, br_s, br_m, w_a, w_s, w_m, w_o)


def loss_head(y, target):
    S = y.shape[0]
    tm = 512

    def body(y_ref, t_ref, dy_ref, loss_ref):
        @pl.when(pl.program_id(0) == 0)
        def _():
            loss_ref[...] = jnp.zeros_like(loss_ref)
        e = y_ref[...] - t_ref[...]
        dy_ref[...] = e * (1.0 / D)
        loss_ref[...] += 0.5 * jnp.sum(jnp.mean(e * e, axis=-1, keepdims=True))

    row = pl.BlockSpec((tm, D), lambda i: (i, 0))
    return pl.pallas_call(
        body, name="loss_head", grid=(S // tm,),
        in_specs=[row, row], out_specs=[row, _full((1, 128))],
        out_shape=[jax.ShapeDtypeStruct((S, D), F32), jax.ShapeDtypeStruct((1, 128), F32)],
        compiler_params=_params(("arbitrary",)),
    )(y, target)


def _adam(w, g, m, v):
    mn = ADAM_B1 * m + (1.0 - ADAM_B1) * g
    vn = ADAM_B2 * v + (1.0 - ADAM_B2) * (g * g)
    m_hat = mn / (1.0 - ADAM_B1 ** ADAM_STEP)
    v_hat = vn / (1.0 - ADAM_B2 ** ADAM_STEP)
    return -ADAM_LR * (m_hat / (jnp.sqrt(v_hat) + ADAM_EPS) + ADAM_WD * w), mn, vn


def adamw_big(w, m, v, halves0, sum1, cc, name, tr):
    _, R, C = w.shape
    nper = R // tr
    f, fb, n0, off_a, off_b = halves0
    p, pb, off1 = sum1

    def body(c_ref, w_ref, m_ref, v_ref, f_ref, fb_ref, p_ref, pb_ref, g_ref, d_ref, nm_ref, nv_ref):
        i = pl.program_id(0)
        half = jnp.where(i % nper >= n0, 1, 0)
        g0 = jnp.where(c_ref[0] == half, f_ref[...], fb_ref[...])
        g = jnp.where(i < nper, g0, p_ref[...] + pb_ref[...])
        g_ref[0] = g
        d_ref[0], nm_ref[0], nv_ref[0] = _adam(w_ref[0], g, m_ref[0], v_ref[0])

    def blk0(i, c):
        il = jnp.minimum(i, nper - 1)
        return (jnp.where(il >= n0, off_b + il - n0, off_a + il), 0)

    wblk = pl.BlockSpec((1, tr, C), lambda i, c: (i // nper, i % nper, 0))
    b0 = pl.BlockSpec((tr, C), blk0)
    b1 = pl.BlockSpec((tr, C), lambda i, c: (off1 + jnp.maximum(i - nper, 0), 0))
    grid_spec = pltpu.PrefetchScalarGridSpec(
        num_scalar_prefetch=1, grid=(2 * nper,),
        in_specs=[wblk, wblk, wblk, b0, b0, b1, b1], out_specs=[wblk] * 4)
    return pl.pallas_call(
        body, name=name, grid_spec=grid_spec,
        out_shape=[jax.ShapeDtypeStruct(w.shape, F32)] * 4,
        compiler_params=_params(("arbitrary",)),
    )(cc, w, m, v, f, fb, p, pb)


def adamw_plain(w, g, m, v, name):
    def body(w_ref, g_ref, m_ref, v_ref, d_ref, nm_ref, nv_ref):
        d_ref[...], nm_ref[...], nv_ref[...] = _adam(w_ref[...], g_ref[...], m_ref[...], v_ref[...])

    return pl.pallas_call(
        body, name=name, out_shape=[jax.ShapeDtypeStruct(w.shape, F32)] * 3, compiler_params=_params(),
    )(w, g, m, v)


SMALL = {"norm_pre": ("g_pre", 8), "norm_post": ("g_post", 8), "att_sinks": ("sinks", 8), "sg_ln_g": ("ln_g", 8),
         "sg_ln_b": ("ln_b", 8), "sg_w": ("sg_w", 1024), "sg_b": ("sg_bt", 8), "ssm_conv_b": ("conv_b", 24),
         "ssm_dt_bias": ("dt_bias", 8), "ssm_a_log": ("a_log", 8), "ssm_d": ("d_skip", 8), "ssm_norm_g": ("norm_g", 16)}
SMALL_LAYER_ROWS = sum(r for _, r in SMALL.values())
REL_ROW = DEPTH * SMALL_LAYER_ROWS
LOSS_ROW = REL_ROW + 32
SMALL_ROWS = LOSS_ROW + 8


def _small_rows():
    rows, r = {}, 0
    for l in range(DEPTH):
        for name, (_, n) in SMALL.items():
            rows[(l, name)] = r
            r += n
    return rows


def adamw_small(red, rel, small):
    names = list(SMALL) + ["rel_bias"]
    params = dict(small, rel_bias=rel)
    rows = _small_rows()

    def grad_of(red_ref, l, name, n):
        r0 = rows[(l, name)]
        if name == "sg_b":
            return red_ref[r0:r0 + 8, :]
        if n < 128:
            return red_ref[r0:r0 + 1, 0:n]
        return jnp.concatenate([red_ref[r0 + j:r0 + j + 1, :] for j in range(n // 128)], axis=1)

    def body(red_ref, *refs):
        ins, outs = refs[:3 * len(names)], refs[3 * len(names):]
        for i, name in enumerate(names):
            w_ref, m_ref, v_ref = ins[3 * i:3 * i + 3]
            o = outs[4 * i:4 * i + 4]
            if name == "rel_bias":
                g = red_ref[REL_ROW:REL_ROW + 32, 0:16]
                o[0][...] = g
                o[1][...], o[2][...], o[3][...] = _adam(w_ref[...], g, m_ref[...], v_ref[...])
                continue
            for l in range(DEPTH):
                if name == "sg_w":
                    for grp in range(8):
                        r0 = rows[(l, name)] + grp * 128
                        g = red_ref[r0:r0 + 128, :]
                        o[0][l, grp] = g
                        o[1][l, grp], o[2][l, grp], o[3][l, grp] = _adam(w_ref[l, grp], g, m_ref[l, grp], v_ref[l, grp])
                elif name == "sg_b":
                    g = grad_of(red_ref, l, name, 128)
                    o[0][l] = g
                    o[1][l], o[2][l], o[3][l] = _adam(w_ref[l], g, m_ref[l], v_ref[l])
                else:
                    sl = slice(l, l + 1)
                    g = grad_of(red_ref, l, name, w_ref.shape[-1])
                    o[0][sl, :] = g
                    o[1][sl, :], o[2][sl, :], o[3][sl, :] = _adam(w_ref[sl, :], g, m_ref[sl, :], v_ref[sl, :])

    flat_in = [a for name in names for a in params[name]]
    out_shape = [jax.ShapeDtypeStruct(params[name][0].shape, F32) for name in names for _ in range(4)]
    res = pl.pallas_call(body, name="adamw_small", out_shape=out_shape, compiler_params=_params())(red, *flat_in)
    return {name: tuple(res[4 * i:4 * i + 4]) for i, name in enumerate(names)}


ANY = pl.BlockSpec(memory_space=pl.ANY)


def _place():
    x, y, c = lax.axis_index("x"), lax.axis_index("y"), lax.axis_index("c")
    others = [(1 - x, y), (x, 1 - y), (1 - x, 1 - y)]
    return x, y, c, others


def _rcopy(src, dst, ssem, rsem, to):
    return pltpu.make_async_remote_copy(src_ref=src, dst_ref=dst, send_sem=ssem, recv_sem=rsem,
                                        device_id=to, device_id_type=MESH)


def gather_weights(arrs):
    n = len(arrs)

    def body(*refs):
        srcs, outs, ssem, rsem = refs[:n], refs[n:2 * n], refs[2 * n], refs[2 * n + 1]
        x, y, c, others = _place()
        me = 2 * x + y
        sib = (x, y, 1 - c)
        first = [_rcopy(srcs[i].at[c], outs[i].at[c, me], ssem.at[6 * i + k], rsem.at[6 * i + k], (ox, oy, c))
                 for i in range(n) for k, (ox, oy) in enumerate(others)]
        for cp in first:
            cp.start()
        passed = []
        for k, (ox, oy) in enumerate(others):
            for i in range(n):
                slot = outs[i].at[c, 2 * ox + oy]
                _rcopy(slot, slot, ssem.at[6 * i + k], rsem.at[6 * i + k], sib).wait_recv()
                fw = _rcopy(slot, slot, ssem.at[6 * i + 3 + k], rsem.at[6 * i + 3 + k], sib)
                fw.start()
                passed.append(fw)
        for k, (ox, oy) in enumerate(others):
            for i in range(n):
                slot = outs[i].at[1 - c, 2 * ox + oy]
                _rcopy(slot, slot, ssem.at[6 * i + 3 + k], rsem.at[6 * i + 3 + k], sib).wait_recv()
        for cp in first + passed:
            cp.wait_send()

    return pl.pallas_call(
        body, name="gather_weights",
        in_specs=[ANY] * n, out_specs=[ANY] * n,
        out_shape=[jax.ShapeDtypeStruct((2, SHARDS) + a.shape[1:], a.dtype) for a in arrs],
        scratch_shapes=[pltpu.SemaphoreType.DMA((6 * n,)), pltpu.SemaphoreType.DMA((6 * n,))],
    )(*arrs)


HBM = pl.BlockSpec(memory_space=pltpu.HBM)
SEM = pl.BlockSpec(memory_space=pltpu.SEMAPHORE)
EFFECT = pltpu.SideEffectType.DATAFLOW_SIDE_EFFECTING


def _in_hbm(a):
    return pltpu.with_memory_space_constraint(a, pltpu.HBM)


def gather_start(srcs, after, name, by_dest=False):
    n = len(srcs)
    lands = [_in_hbm(lax.empty((SHARDS,) + a.shape[-2:], a.dtype)) for a in srcs]
    na = len(after)

    def body(*refs):
        src, land = refs[:n], refs[n:2 * n]
        ssem, rsem, token = refs[2 * n + na], refs[2 * n + na + 1], refs[-1]
        x, y, c, others = _place()
        me = 2 * x + y
        for i in range(n):
            for k, (ox, oy) in enumerate(others):
                s = src[i].at[2 * ox + oy] if by_dest else src[i]
                _rcopy(s, land[i].at[me], ssem.at[3 * i + k], rsem.at[3 * i + k], (ox, oy, c)).start()
        token[...] = jnp.zeros_like(token)

    bufs = [_in_hbm(a) for a in srcs] + lands
    out = pl.pallas_call(
        body, name=name,
        out_shape=(pltpu.SemaphoreType.DMA((3 * n,)), pltpu.SemaphoreType.DMA((3 * n,)),
                   *[pltpu.HBM(b.shape, b.dtype) for b in bufs], jax.ShapeDtypeStruct((8, 128), F32)),
        in_specs=[HBM] * (2 * n) + [ANY] * na,
        out_specs=(SEM, SEM, *[HBM] * (2 * n), pl.BlockSpec(memory_space=pltpu.VMEM)),
        input_output_aliases={i: 2 + i for i in range(2 * n)},
        compiler_params=pltpu.CompilerParams(has_side_effects=EFFECT),
    )(*bufs, *after)
    return out[0], out[1], list(out[2:2 + n]), list(out[2 + n:2 + 2 * n]), out[-1]


def gather_wait(ssem, rsem, srcs, lands, after, name, by_dest=False):
    n = len(srcs)

    def body(*refs):
        src, land = refs[:n], refs[n:2 * n]
        s_sem, r_sem = refs[2 * n], refs[2 * n + 1]
        x, y, c, others = _place()
        for i in range(n):
            for k, (ox, oy) in enumerate(others):
                s = src[i].at[2 * ox + oy] if by_dest else src[i]
                cp = _rcopy(s, land[i].at[2 * ox + oy], s_sem.at[3 * i + k], r_sem.at[3 * i + k], (ox, oy, c))
                cp.wait_send()
                cp.wait_recv()

    bufs = list(srcs) + list(lands)
    out = pl.pallas_call(
        body, name=name,
        out_shape=tuple(pltpu.HBM(b.shape, b.dtype) for b in bufs),
        in_specs=[HBM] * (2 * n) + [SEM, SEM, ANY],
        out_specs=tuple([HBM] * (2 * n)),
        input_output_aliases={i: i for i in range(2 * n)},
        compiler_params=pltpu.CompilerParams(has_side_effects=EFFECT),
    )(*bufs, ssem, rsem, after)
    return list(out[n:2 * n])


def grad_sibling_exchange(arrs):
    n = len(arrs)

    def body(*refs):
        srcs, outs, ssem, rsem = refs[:n], refs[n:2 * n], refs[2 * n], refs[2 * n + 1]
        x, y, c, _ = _place()
        cps = [_rcopy(srcs[i].at[1 - c], outs[i], ssem.at[i], rsem.at[i], (x, y, 1 - c)) for i in range(n)]
        for cp in cps:
            cp.start()
        for cp in cps:
            cp.wait()

    return pl.pallas_call(
        body, name="grad_sibling_exchange",
        in_specs=[ANY] * n, out_specs=[ANY] * n,
        out_shape=[jax.ShapeDtypeStruct(a.shape[1:], F32) for a in arrs],
        scratch_shapes=[pltpu.SemaphoreType.DMA((n,)), pltpu.SemaphoreType.DMA((n,))],
    )(*arrs)


def grad_chip_sum(g, sb, cc, tr, name):
    _, _, R, C = g.shape
    blk = pl.BlockSpec((1, tr, C), lambda s, r, c: (s, r, 0))
    grid_spec = pltpu.PrefetchScalarGridSpec(
        num_scalar_prefetch=1, grid=(SHARDS, R // tr),
        in_specs=[pl.BlockSpec((1, 1, tr, C), lambda s, r, c: (c[0], s, r, 0)), blk],
        out_specs=[blk, blk])

    def body(c_ref, a_ref, b_ref, o_ref, ob_ref):
        t = a_ref[0] + b_ref[...]
        o_ref[...] = t
        ob_ref[...] = t.astype(BF16)

    return pl.pallas_call(
        body, name=name, grid_spec=grid_spec,
        out_shape=[jax.ShapeDtypeStruct((SHARDS, R, C), F32), jax.ShapeDtypeStruct((SHARDS, R, C), BF16)],
        compiler_params=_params(("arbitrary", "arbitrary")),
    )(cc, g, sb)


def grad_shard_sum(t, rb, me, tr, name):
    _, R, C = t.shape
    grid_spec = pltpu.PrefetchScalarGridSpec(
        num_scalar_prefetch=1, grid=(R // tr,),
        in_specs=[pl.BlockSpec((1, tr, C), lambda r, m: (m[0], r, 0)),
                  pl.BlockSpec((SHARDS, tr, C), lambda r, m: (0, r, 0))],
        out_specs=pl.BlockSpec((tr, C), lambda r, m: (r, 0)))

    def body(m_ref, t_ref, r_ref, o_ref):
        part = [jnp.where(m_ref[0] == s, t_ref[0], r_ref[s].astype(F32)) for s in range(SHARDS)]
        o_ref[...] = ((part[0] + part[1]) + part[2]) + part[3]

    return pl.pallas_call(
        body, name=name, grid_spec=grid_spec,
        out_shape=jax.ShapeDtypeStruct((R, C), F32),
        compiler_params=_params(("arbitrary",)),
    )(me, t, rb)


def grad_sibling_share(arrs, name):
    n = len(arrs)

    def body(*refs):
        srcs, outs, ssem, rsem = refs[:n], refs[n:2 * n], refs[2 * n], refs[2 * n + 1]
        x, y, c, _ = _place()
        cps = [_rcopy(srcs[i], outs[i], ssem.at[i], rsem.at[i], (x, y, 1 - c)) for i in range(n)]
        for cp in cps:
            cp.start()
        for cp in cps:
            cp.wait()

    return pl.pallas_call(
        body, name=name,
        in_specs=[ANY] * n, out_specs=[ANY] * n,
        out_shape=[jax.ShapeDtypeStruct(a.shape, F32) for a in arrs],
        scratch_shapes=[pltpu.SemaphoreType.DMA((n,)), pltpu.SemaphoreType.DMA((n,))],
    )(*arrs)


def _allreduce_rows(src, sib_buf, chips, out_ref, ssem, rsem):
    x, y, c, others = _place()
    me = 2 * x + y
    cp = _rcopy(src, sib_buf, ssem.at[0], rsem.at[0], (x, y, 1 - c))
    cp.start()
    cp.wait()
    chips[me] = src[...] + sib_buf[...]
    sends = [_rcopy(chips.at[me], chips.at[me], ssem.at[1 + k], rsem.at[1 + k], (ox, oy, c))
             for k, (ox, oy) in enumerate(others)]
    for s in sends:
        s.start()
    for k, (ox, oy) in enumerate(others):
        slot = chips.at[2 * ox + oy]
        _rcopy(slot, slot, ssem.at[1 + k], rsem.at[1 + k], (ox, oy, c)).wait_recv()
    for s in sends:
        s.wait_send()
    out_ref[...] = ((chips[0] + chips[1]) + chips[2]) + chips[3]


def _allreduce_scratch(rows):
    return [pltpu.VMEM((rows, 128), F32), pltpu.VMEM((SHARDS, rows, 128), F32),
            pltpu.SemaphoreType.DMA((4,)), pltpu.SemaphoreType.DMA((4,))]


def allreduce_rows(buf, name):
    rows = buf.shape[0]
    VM = pl.BlockSpec(memory_space=pltpu.VMEM)

    def body(src_ref, out_ref, sib_buf, chips, ssem, rsem):
        _allreduce_rows(src_ref, sib_buf, chips, out_ref, ssem, rsem)

    return pl.pallas_call(
        body, name=name, in_specs=[VM], out_specs=VM,
        out_shape=jax.ShapeDtypeStruct((rows, 128), F32),
        scratch_shapes=_allreduce_scratch(rows), compiler_params=_params(),
    )(buf)


def small_allreduce(grads, rel, loss_part):
    rows = _small_rows()
    keys = [(l, name) for l in range(DEPTH) for name in SMALL]
    flat = [grads[l][SMALL[name][0]] for l, name in keys] + [rel, loss_part]

    def body(*refs):
        ins = refs[:len(flat)]
        out_ref, src, sib_buf, chips, ssem, rsem = refs[len(flat):]
        src[...] = jnp.zeros_like(src)
        for (l, name), ref in zip(keys, ins):
            r0 = rows[(l, name)]
            if name == "sg_w":
                for grp in range(8):
                    src[r0 + grp * 128:r0 + (grp + 1) * 128, :] = ref[grp]
            elif name == "sg_b":
                src[r0:r0 + 8, :] = ref[...].T[0:8, :]
            else:
                for j in range(ref.shape[1] // 128):
                    src[r0 + j:r0 + j + 1, :] = ref[:, j * 128:(j + 1) * 128]
        src[REL_ROW:REL_ROW + 32, 0:16] = ins[-2][...]
        src[LOSS_ROW:LOSS_ROW + 1, :] = ins[-1][...]
        _allreduce_rows(src, sib_buf, chips, out_ref, ssem, rsem)

    return pl.pallas_call(
        body, name="small_allreduce",
        out_shape=jax.ShapeDtypeStruct((SMALL_ROWS, 128), F32),
        scratch_shapes=[pltpu.VMEM((SMALL_ROWS, 128), F32)] + _allreduce_scratch(SMALL_ROWS),
        compiler_params=_params(),
    )(*flat)


def _pad_lanes(v):
    return jnp.zeros((1, 128), F32).at[0, :v.shape[0]].set(v)


def layer_fwd(x, wts, bias):
    wt = wts["wt"]
    tn = {name: t for name, _, t in GROUPS}
    p_gate, h = inproj_first(x, wts["g_pre"], wt["gate"], tn["gate"], "inproj_gate")
    p_sgu, p_att, p_ssd = (inproj_group(h, wt[n], tn[n], "inproj_" + n, F32 if n == "att" else BF16)
                           for n in ("sgu", "att", "ssd"))
    y_att = att_fwd(p_att, bias, wts["sinks"])
    y_sg = sgu_fwd(p_sgu, wts["ln_g"], wts["ln_b"], wts["sg_w"], wts["sg_bt"])
    y_ssm, hst = ssd_fwd(p_ssd, wts["conv_w"], wts["conv_b"], wts["dt_bias"], wts["a_log"], wts["d_skip"],
                         wts["norm_g"])
    x_new, br_a, br_s, br_m, merged, out = merge_fwd(
        y_att, y_sg, y_ssm, p_gate, x, wts["w_a"], wts["w_s"], wts["w_m"], wts["w_o"], wts["g_post"])
    saved = dict(x=x, p_gate=p_gate, p_sgu=p_sgu, p_att=p_att, p_ssd=p_ssd, h=h,
                 y_att=y_att, y_sg=y_sg, y_ssm=y_ssm, hst=hst,
                 br_a=br_a, br_s=br_s, br_m=br_m, merged=merged, out=out)
    return x_new, saved


def layer_bwd(dy, wts, bias, sv):
    dps, grads = layer_bwd_params(dy, wts, bias, sv)
    dx, grads["g_pre"] = layer_bwd_input(dy, dps, wts, sv, wts["g_pre"])
    return dx, grads


def layer_bwd_input(dy, dps, wts, sv, g_pre):
    wt = wts["wt"]
    tn = {name: t for name, _, t in GROUPS}
    acc = None
    for n in ("gate", "sgu", "ssd"):
        acc = dh_group(dps[n], wt[n], acc, tn[n], "dh_" + n)
    return dh_last(dps["att"], wt["att"], acc, sv["x"], g_pre, dy, tn["att"], "dh_att")


def layer_bwd_params(dy, wts, bias, sv):
    dout, dba, dbs, dbm, d_gate, dya, dys, dym, dg_post = merge_bwd(
        dy, sv["out"], wts["g_post"], sv["p_gate"], sv["br_a"], sv["br_s"], sv["br_m"],
        wts["w_a"], wts["w_s"], wts["w_m"], wts["w_o"])
    d_att, dbias, dsinks = att_bwd(dya, sv["p_att"], bias, wts["sinks"])
    d_sgu, dsg_w, dsg_bt, dln_g, dln_b = sgu_bwd(dys, sv["p_sgu"], wts["ln_g"], wts["ln_b"], wts["sg_w"],
                                                 wts["sg_bt"])
    d_ssd, dcw, dcb, ddtb, dalog, ddsk, dng = ssd_bwd(
        dym, sv["p_ssd"], sv["hst"], wts["conv_w"], wts["conv_b"], wts["dt_bias"], wts["a_log"], wts["d_skip"],
        wts["norm_g"])
    dps = dict(gate=d_gate, sgu=d_sgu, att=d_att, ssd=d_ssd)
    tn = {name: t for name, _, t in GROUPS}
    grads = dict(
        w_in={n: dw_group(dps[n], sv["h"], tn[n], "dw_in_" + n) for n in dps},
        w_a=matmul_tn(sv["y_att"], dba, "dw_att"),
        w_s=matmul_tn(sv["y_sg"], dbs, "dw_sg"),
        w_m=matmul_tn(sv["y_ssm"], dbm, "dw_ssm"),
        w_o=matmul_tn(sv["merged"], dout, "dw_out"),
        g_post=dg_post, sinks=dsinks, ln_g=dln_g, ln_b=dln_b, sg_w=dsg_w, sg_bt=dsg_bt,
        conv_w=dcw, conv_b=dcb, dt_bias=ddtb, a_log=dalog, d_skip=ddsk, norm_g=dng, bias=dbias)
    return dps, grads


REST_OFF = (0, 256, 512, 1024, 1280)
GR_ROWS = 1536
GR_CONV = 1280
W_IN_SPLIT = 1600
W_IN_HALF = 1824


def kernel(x, w_in, norm_pre, norm_post, rel_bias, att_sinks, sg_ln_g, sg_ln_b, sg_w, sg_b, ssm_conv_w, ssm_conv_b, ssm_dt_bias, ssm_a_log, ssm_d, ssm_norm_g, w_br_att, w_br_sg, w_br_ssm, w_out, loss_target, m_w_in, m_norm_pre, m_norm_post, m_rel_bias, m_att_sinks, m_sg_ln_g, m_sg_ln_b, m_sg_w, m_sg_b, m_ssm_conv_w, m_ssm_conv_b, m_ssm_dt_bias, m_ssm_a_log, m_ssm_d, m_ssm_norm_g, m_w_br_att, m_w_br_sg, m_w_br_ssm, m_w_out, v_w_in, v_norm_pre, v_norm_post, v_rel_bias, v_att_sinks, v_sg_ln_g, v_sg_ln_b, v_sg_w, v_sg_b, v_ssm_conv_w, v_ssm_conv_b, v_ssm_dt_bias, v_ssm_a_log, v_ssm_d, v_ssm_norm_g, v_w_br_att, v_w_br_sg, v_w_br_ssm, v_w_out):
    cx, cy, cc = lax.axis_index("x"), lax.axis_index("y"), lax.axis_index("c")
    me = 2 * cx + cy
    xs = x[0]
    S = xs.shape[0]

    tr = lambda a: jnp.transpose(a, (0, 2, 1))
    w_in_b = tr(w_in).astype(BF16)
    w_rest_b = jnp.concatenate([w_br_att, w_br_sg, w_br_ssm, w_out], axis=1).astype(BF16)
    halves = lambda a: a.reshape(2, a.shape[0] // 2, a.shape[1])
    all0_in, all0_rest = gather_weights([halves(w_in_b[0]), halves(w_rest_b[0])])
    convw_slot = jnp.zeros((SHARDS, DEPTH * CONV_K * 768 // 128, 128), F32)
    convw_slot = lax.dynamic_update_index_in_dim(
        convw_slot, jnp.where(cc == 0, 1.0, 0.0) * ssm_conv_w.reshape(-1, 128), me, 0)
    convw_rows = allreduce_rows(convw_slot.reshape(-1, 128), "gather_conv_w")
    convw_all = convw_rows.reshape(SHARDS, DEPTH, CONV_K, 768).transpose(1, 2, 0, 3).reshape(DEPTH, CONV_K, CONV_C)
    g1_ssem, g1_rsem, g1_srcs, g1_lands, g1_token = gather_start(
        [w_in_b[1], w_rest_b[1]], [convw_rows, all0_rest], "gather_l1_start")

    o = REST_OFF

    def layer_weights(l, gathered_in, gathered_rest, g_pre):
        sh_in = [jnp.where(me == s, w_in_b[l], gathered_in[s]) for s in range(SHARDS)]
        sh_rest = [jnp.where(me == s, w_rest_b[l], gathered_rest[s]) for s in range(SHARDS)]
        rest = lambda k: jnp.concatenate([r[o[k]:o[k + 1]] for r in sh_rest], axis=0)
        return dict(
            wt=group_weights(jnp.concatenate(sh_in, axis=0)),
            w_a=rest(0), w_s=rest(1), w_m=rest(2), w_o=rest(3),
            g_pre=g_pre, g_post=norm_post[l][None], sinks=att_sinks[l],
            ln_g=sg_ln_g[l][None], ln_b=sg_ln_b[l][None], sg_w=sg_w[l],
            sg_bt=sg_b[l].T,
            conv_w=jnp.concatenate([convw_all[l], jnp.zeros((4, CONV_C), F32)], axis=0),
            conv_b=ssm_conv_b[l][None], dt_bias=_pad_lanes(ssm_dt_bias[l]), a_log=_pad_lanes(ssm_a_log[l]),
            d_skip=_pad_lanes(ssm_d[l]), norm_g=ssm_norm_g[l][None])

    bias = bias_table(rel_bias)
    layers = [layer_weights(0, [all0_in[:, s].reshape(3400, D) for s in range(SHARDS)],
                            [all0_rest[:, s].reshape(1280, D) for s in range(SHARDS)],
                            (norm_pre[0] + g1_token[0, 0])[None])]
    act, sv0 = layer_fwd(xs, layers[0], bias)
    land_in, land_rest = gather_wait(g1_ssem, g1_rsem, g1_srcs, g1_lands, act, "gather_l1_wait")
    layers.append(layer_weights(1, land_in, land_rest, norm_pre[1][None]))
    act, sv1 = layer_fwd(act, layers[1], bias)
    saved = [sv0, sv1]
    dy, loss_part = loss_head(act, loss_target[0])
    cvec = jnp.reshape(cc, (1,)).astype(jnp.int32)
    mvec = jnp.reshape(me, (1,)).astype(jnp.int32)

    def by_shard(g):
        gcw = g["conv_w"][0:CONV_K].reshape(CONV_K, SHARDS, 768).transpose(1, 0, 2).reshape(SHARDS, 3, 1024)
        rest = jnp.concatenate([
            g["w_a"].reshape(SHARDS, 256, D), g["w_s"].reshape(SHARDS, 256, D), g["w_o"].reshape(SHARDS, 256, D),
            g["w_m"].reshape(SHARDS, 512, D), jnp.pad(gcw, ((0, 0), (0, GR_ROWS - GR_CONV - 3), (0, 0)))], axis=1)
        return ungroup_grads(g["w_in"]).reshape(SHARDS, 3400, D), rest

    grads = [None] * DEPTH
    dy, grads[1] = layer_bwd(dy, layers[1], bias, saved[1])
    g1_in, g1_rest = by_shard(grads[1])
    g1_in = jnp.pad(g1_in, ((0, 0), (0, W_IN_ROWS - 3400), (0, 0)))
    x1_ssem, x1_rsem, x1_srcs, x1_lands, x1_token = gather_start(
        [g1_in.astype(BF16), g1_rest.astype(BF16)], [], "grads_l1_start", by_dest=True)
    wts0 = dict(layers[0], g_post=layers[0]["g_post"] + x1_token[0, 0])
    dps0, grads[0] = layer_bwd_params(dy, wts0, bias, saved[0])
    r1_in, r1_rest = gather_wait(x1_ssem, x1_rsem, x1_srcs, x1_lands, grads[0]["w_in"]["ssd"], "grads_l1_wait",
                                 by_dest=True)
    p_in = grad_shard_sum(g1_in, r1_in, mvec, 384, "l1_sum_w_in")
    p_rest = grad_shard_sum(g1_rest, r1_rest, mvec, 512, "l1_sum_rest")
    pb_in, pb_rest = grad_sibling_share([p_in, p_rest], "l1_sibling_share")

    g0_in, g0_rest = by_shard(grads[0])
    pad_to = lambda a, rows: jnp.pad(a, ((0, 0), (0, rows - a.shape[1]), (0, 0)))
    g0_in = jnp.stack([pad_to(g0_in[:, 0:W_IN_SPLIT], W_IN_HALF), pad_to(g0_in[:, W_IN_SPLIT:3400], W_IN_HALF)])
    g0_rest = jnp.stack([g0_rest[:, 0:GR_ROWS // 2], g0_rest[:, GR_ROWS // 2:GR_ROWS]])
    sb_in, sb_rest = grad_sibling_exchange([g0_in, g0_rest])
    t_in, t_in_b = grad_chip_sum(g0_in, sb_in, cvec, 608, "chip_sum_w_in")
    t_rest, t_rest_b = grad_chip_sum(g0_rest, sb_rest, cvec, 384, "chip_sum_rest")
    x0_ssem, x0_rsem, x0_srcs, x0_lands, x0_token = gather_start([t_in_b, t_rest_b], [], "grads_l0_start", by_dest=True)
    dy, grads[0]["g_pre"] = layer_bwd_input(dy, dps0, layers[0], saved[0], layers[0]["g_pre"] + x0_token[0, 0])
    grad_x = dy[None]
    rb_in, rb_rest = gather_wait(x0_ssem, x0_rsem, x0_srcs, x0_lands, dy, "grads_l0_wait", by_dest=True)
    grad_rel_local = bias_grad(grads[0]["bias"] + grads[1]["bias"])
    f_in = grad_shard_sum(t_in, rb_in, mvec, 608, "shard_sum_w_in")
    f_rest = grad_shard_sum(t_rest, rb_rest, mvec, 384, "shard_sum_rest")
    fb_in, fb_rest = grad_sibling_share([f_in, f_rest], "l0_sibling_share")

    red = small_allreduce(grads, grad_rel_local, loss_part + 0.0 * f_rest[0:1, 0:128])
    loss = red[LOSS_ROW, 0]

    res = adamw_small(red, (rel_bias, m_rel_bias, v_rel_bias), dict(
        norm_pre=(norm_pre, m_norm_pre, v_norm_pre), norm_post=(norm_post, m_norm_post, v_norm_post),
        att_sinks=(att_sinks, m_att_sinks, v_att_sinks), sg_ln_g=(sg_ln_g, m_sg_ln_g, v_sg_ln_g),
        sg_ln_b=(sg_ln_b, m_sg_ln_b, v_sg_ln_b), sg_w=(sg_w, m_sg_w, v_sg_w), sg_b=(sg_b, m_sg_b, v_sg_b),
        ssm_conv_b=(ssm_conv_b, m_ssm_conv_b, v_ssm_conv_b), ssm_dt_bias=(ssm_dt_bias, m_ssm_dt_bias, v_ssm_dt_bias),
        ssm_a_log=(ssm_a_log, m_ssm_a_log, v_ssm_a_log), ssm_d=(ssm_d, m_ssm_d, v_ssm_d),
        ssm_norm_g=(ssm_norm_g, m_ssm_norm_g, v_ssm_norm_g)))
    res["w_in"] = tuple(tr(a) for a in adamw_big(
        tr(w_in), tr(m_w_in), tr(v_w_in), (f_in, fb_in, W_IN_SPLIT // 200, 0, 0), (p_in, pb_in, 0), cvec, "adamw_w_in", 200))
    rest_upd = lambda w, m, v, name, n0, off0, off1: adamw_big(
        w, m, v, (f_rest, fb_rest, n0, off0, off0), (p_rest, pb_rest, off1), cvec, name, 256)
    res["w_br_att"] = rest_upd(w_br_att, m_w_br_att, v_w_br_att, "adamw_w_br_att", 1, 0, 0)
    res["w_br_sg"] = rest_upd(w_br_sg, m_w_br_sg, v_w_br_sg, "adamw_w_br_sg", 1, 1, 1)
    res["w_out"] = rest_upd(w_out, m_w_out, v_w_out, "adamw_w_out", 1, 2, 2)
    res["w_br_ssm"] = rest_upd(w_br_ssm, m_w_br_ssm, v_w_br_ssm, "adamw_w_br_ssm", 0, 0, 3)
    cw0 = jnp.where(cc == 1, f_rest, fb_rest)[GR_CONV - GR_ROWS // 2:GR_CONV - GR_ROWS // 2 + 3]
    cw1 = (p_rest + pb_rest)[GR_CONV:GR_CONV + 3]
    g_conv_w = jnp.stack([cw0.reshape(CONV_K, 768), cw1.reshape(CONV_K, 768)])
    res["ssm_conv_w"] = (g_conv_w,) + tuple(adamw_plain(ssm_conv_w, g_conv_w, m_ssm_conv_w, v_ssm_conv_w, "adamw_conv_w"))

    order = ["w_in", "norm_pre", "norm_post", "rel_bias", "att_sinks", "sg_ln_g", "sg_ln_b", "sg_w", "sg_b",
             "ssm_conv_w", "ssm_conv_b", "ssm_dt_bias", "ssm_a_log", "ssm_d", "ssm_norm_g",
             "w_br_att", "w_br_sg", "w_br_ssm", "w_out"]
    return (loss, grad_x, *[res[n][0] for n in order], *[res[n][1] for n in order],
            *[res[n][2] for n in order], *[res[n][3] for n in order])
```

```python
import functools
import math

import numpy as np
import jax
import jax.numpy as jnp
from jax import lax
from jax.experimental import pallas as pl
from jax.experimental.pallas import tpu as pltpu

F32 = jnp.float32
BF16 = jnp.bfloat16
MESH = pl.DeviceIdType.MESH

D = 1024
DEPTH = 2
EPS = 1e-6
L = 128
HEADS = 16
KV = 2
DH = 64
SSM_W = 2048
SSM_H = 32
SSM_P = 64
SSM_G = 4
SSM_N = 128
CONV_K = 4
CONV_C = 3072
NEG = -1e30
IN_COLS = 13600

GROUPS = (("gate", 3072, 1536), ("sgu", 3072, 1536), ("att", 2304, 2304), ("ssd", 5376, 1792))
W_IN_ROWS = 3456
DH_TILE = {"gate": 3072, "sgu": 3072, "ssd": 2688}

ADAM_LR = 0.001
ADAM_B1 = 0.9
ADAM_B2 = 0.999
ADAM_EPS = 1e-08
ADAM_WD = 0.01
ADAM_STEP = 10

VMEM_LIMIT = 56 * 1024 * 1024

SHARDS = 4


def _dot(a, b):
    return jnp.dot(a, b, preferred_element_type=F32)


def _dot_nt(a, b):
    return lax.dot_general(a, b, (((1,), (1,)), ((), ())), preferred_element_type=F32)


def _dot_tn(a_f32, b):
    return jnp.dot(a_f32.T.astype(BF16), b, preferred_element_type=F32)


def _dot_t(a, b):
    return lax.dot_general(a, b, (((0,), (0,)), ((), ())), preferred_element_type=F32)


def _dot_hi(a, b):
    return jnp.dot(a, b, preferred_element_type=F32, precision=lax.Precision.HIGHEST)


def _pieces(x, n):
    out = []
    for _ in range(n - 1):
        p = x.astype(BF16)
        out.append(p)
        x = x - p.astype(F32)
    out.append(x.astype(BF16))
    return out


def _dot_sel(a, sel, n):
    sel = sel.astype(BF16)
    acc = None
    for p in _pieces(a, n):
        t = _dot(p, sel)
        acc = t if acc is None else acc + t
    return acc


def _sel_dot(sel, b, n):
    sel = sel.astype(BF16)
    acc = None
    for p in _pieces(b, n):
        t = _dot(sel, p)
        acc = t if acc is None else acc + t
    return acc


def _sigmoid(x):
    return 1.0 / (1.0 + jnp.exp(-x))


def _softplus(x):
    return jnp.maximum(x, 0.0) + jnp.log(1.0 + jnp.exp(-jnp.abs(x)))


def _params(sem=None, vmem=VMEM_LIMIT):
    kw = dict(vmem_limit_bytes=vmem)
    if sem is not None:
        kw["dimension_semantics"] = sem
    return pltpu.CompilerParams(**kw)


def _full(shape):
    nd = len(shape)
    return pl.BlockSpec(shape, lambda *_: (0,) * nd)


def group_weights(wt):
    return dict(
        gate=wt[10528:13600],
        sgu=wt[2304:5376],
        att=jnp.concatenate([wt[0:1024], wt[1280:2304], wt[1024:1280]], axis=0),
        ssd=jnp.concatenate([wt[5376:10496], wt[10496:10528], jnp.zeros((224, D), wt.dtype)], axis=0))


def ungroup_grads(g):
    a, s = g["att"], g["ssd"]
    return jnp.concatenate([a[0:1024], a[2048:2304], a[1024:2048], g["sgu"], s[0:5152], g["gate"]], axis=0)


def _bucket_table():
    qi = np.arange(L)[:, None]
    kj = np.arange(2 * L)[None, :]
    dist = np.maximum(qi + L - kj, 0)
    dist_f = np.maximum(dist, 1).astype(np.float32)
    large = 16 + (np.log(dist_f / np.float32(16)) / np.float32(math.log(128 / 16)) * np.float32(16)).astype(np.int32)
    large = np.minimum(large, 31)
    return np.where(dist < 16, dist, large).astype(np.int32)


def bias_table(rel_bias):
    buckets = jnp.asarray(_bucket_table().reshape(1, L * 2 * L))

    def body(rb_ref, bk_ref, out_ref):
        onehot = (lax.broadcasted_iota(jnp.int32, (32, L * 2 * L), 0) == bk_ref[...]).astype(F32)
        out_ref[...] = lax.dot_general(rb_ref[...], onehot, (((0,), (0,)), ((), ())),
                                       preferred_element_type=F32, precision=lax.Precision.HIGHEST)

    out = pl.pallas_call(
        body, name="bias_table",
        out_shape=jax.ShapeDtypeStruct((HEADS, L * 2 * L), F32),
        compiler_params=_params(),
    )(rel_bias, buckets)
    out = out.reshape(HEADS, L, 2 * L)
    win = _window_mask()
    first = win & (np.arange(2 * L)[None, :] >= L)
    return jnp.stack([jnp.where(first, out, NEG), jnp.where(win, out, NEG)])


def _window_mask():
    dist = np.arange(L)[:, None] + L - np.arange(2 * L)[None, :]
    return (dist >= 0) & (dist < L)


def bias_grad(dbias):
    buckets = jnp.asarray(_bucket_table().reshape(1, L * 2 * L))

    def body(db_ref, bk_ref, out_ref):
        onehot = (lax.broadcasted_iota(jnp.int32, (32, L * 2 * L), 0) == bk_ref[...]).astype(F32)
        out_ref[...] = lax.dot_general(onehot, db_ref[...], (((1,), (1,)), ((), ())),
                                       preferred_element_type=F32, precision=lax.Precision.HIGHEST)

    return pl.pallas_call(
        body, name="bias_grad",
        out_shape=jax.ShapeDtypeStruct((32, HEADS), F32),
        compiler_params=_params(),
    )(dbias.reshape(HEADS, L * 2 * L), buckets)


def _row_tile(S):
    return 1024 if S % 1024 == 0 else 512


def inproj_first(x, g_pre, wt, tn, name):
    S, W = x.shape[0], wt.shape[0]
    tm = _row_tile(S)

    def body(x_ref, g_ref, w_ref, o_ref, h_ref):
        @pl.when(pl.program_id(1) == 0)
        def _():
            xv = x_ref[...]
            r = lax.rsqrt(jnp.mean(xv * xv, axis=-1, keepdims=True) + EPS)
            h_ref[...] = (xv * r * g_ref[...]).astype(BF16)
        o_ref[...] = _dot_nt(h_ref[...], w_ref[...]).astype(BF16)

    return pl.pallas_call(
        body, name=name, grid=(S // tm, W // tn),
        in_specs=[pl.BlockSpec((tm, D), lambda i, j: (i, 0)), _full((1, D)),
                  pl.BlockSpec((tn, D), lambda i, j: (j, 0))],
        out_specs=[pl.BlockSpec((tm, tn), lambda i, j: (i, j)), pl.BlockSpec((tm, D), lambda i, j: (i, 0))],
        out_shape=[jax.ShapeDtypeStruct((S, W), BF16), jax.ShapeDtypeStruct((S, D), BF16)],
        compiler_params=_params(("arbitrary", "arbitrary")),
    )(x, g_pre, wt)


def inproj_group(h, wt, tn, name, dtype):
    S, W = h.shape[0], wt.shape[0]
    tm = _row_tile(S)

    def body(h_ref, w_ref, o_ref):
        o_ref[...] = _dot_nt(h_ref[...], w_ref[...]).astype(dtype)

    return pl.pallas_call(
        body, name=name, grid=(S // tm, W // tn),
        in_specs=[pl.BlockSpec((tm, D), lambda i, j: (i, 0)), pl.BlockSpec((tn, D), lambda i, j: (j, 0))],
        out_specs=pl.BlockSpec((tm, tn), lambda i, j: (i, j)),
        out_shape=jax.ShapeDtypeStruct((S, W), dtype),
        compiler_params=_params(("arbitrary", "arbitrary")),
    )(h, wt)


def dh_group(dp, wt, acc, tk, name):
    S, W = dp.shape
    tm = _row_tile(S)

    def body(*refs):
        dp_ref, w_ref, o_ref = refs[0], refs[1], refs[-1]
        first = pl.program_id(1) == 0
        if acc is None:
            @pl.when(first)
            def _():
                o_ref[...] = jnp.zeros_like(o_ref)
        else:
            @pl.when(first)
            def _():
                o_ref[...] = refs[2][...]
        o_ref[...] += _dot(dp_ref[...], w_ref[...])

    row = pl.BlockSpec((tm, D), lambda i, k: (i, 0))
    return pl.pallas_call(
        body, name=name, grid=(S // tm, W // tk),
        in_specs=[pl.BlockSpec((tm, tk), lambda i, k: (i, k)), pl.BlockSpec((tk, D), lambda i, k: (k, 0))]
        + ([] if acc is None else [row]),
        out_specs=row, out_shape=jax.ShapeDtypeStruct((S, D), F32),
        input_output_aliases={} if acc is None else {2: 0},
        compiler_params=_params(("arbitrary", "arbitrary")),
    )(*((dp, wt) if acc is None else (dp, wt, acc)))


def dh_last(dp, wt, acc_in, x, g_pre, dy, tk, name):
    S, W = dp.shape
    tm = 512
    nk = W // tk

    def body(dp_ref, w_ref, a_ref, x_ref, g_ref, dy_ref, dx_ref, dg_ref, acc):
        i, k = pl.program_id(0), pl.program_id(1)

        @pl.when(k == 0)
        def _():
            acc[...] = a_ref[...]

        acc[...] += _dot(dp_ref[...], w_ref[...])

        @pl.when((k == nk - 1) & (i == 0))
        def _():
            dg_ref[...] = jnp.zeros_like(dg_ref)

        @pl.when(k == nk - 1)
        def _():
            xv = x_ref[...]
            dh = acc[...]
            g = g_ref[...]
            r = lax.rsqrt(jnp.mean(xv * xv, axis=-1, keepdims=True) + EPS)
            dhg = dh * g
            dx_ref[...] = dy_ref[...] + r * dhg - xv * (r * r * r) * jnp.mean(dhg * xv, axis=-1, keepdims=True)
            dg_ref[...] += jnp.sum(dh * xv * r, axis=0, keepdims=True)

    row = pl.BlockSpec((tm, D), lambda i, k: (i, 0))
    return pl.pallas_call(
        body, name=name, grid=(S // tm, nk),
        in_specs=[pl.BlockSpec((tm, tk), lambda i, k: (i, k)), pl.BlockSpec((tk, D), lambda i, k: (k, 0)),
                  row, row, _full((1, D)), row],
        out_specs=[row, _full((1, D))],
        out_shape=[jax.ShapeDtypeStruct((S, D), F32), jax.ShapeDtypeStruct((1, D), F32)],
        scratch_shapes=[pltpu.VMEM((tm, D), F32)],
        compiler_params=_params(("arbitrary", "arbitrary")),
    )(dp, wt, acc_in, x, g_pre, dy)


def dw_group(dp, h, tn, name):
    S, W = dp.shape
    ts = _row_tile(S)

    def body(dp_ref, h_ref, o_ref):
        @pl.when(pl.program_id(1) == 0)
        def _():
            o_ref[...] = jnp.zeros_like(o_ref)
        o_ref[...] += _dot_t(dp_ref[...], h_ref[...])

    return pl.pallas_call(
        body, name=name, grid=(W // tn, S // ts),
        in_specs=[pl.BlockSpec((ts, tn), lambda j, s: (s, j)), pl.BlockSpec((ts, D), lambda j, s: (s, 0))],
        out_specs=pl.BlockSpec((tn, D), lambda j, s: (j, 0)),
        out_shape=jax.ShapeDtypeStruct((W, D), F32),
        compiler_params=_params(("arbitrary", "arbitrary")),
    )(dp, h)


def matmul_tn(a, b, name, tn=1024):
    S, K = a.shape
    N = b.shape[1]
    ts = _row_tile(S)
    ns = S // ts

    def body(a_ref, b_ref, o_ref):
        @pl.when(pl.program_id(1) == 0)
        def _():
            o_ref[...] = jnp.zeros_like(o_ref)
        o_ref[...] += _dot_t(a_ref[...], b_ref[...])

    return pl.pallas_call(
        body, name=name, grid=(N // tn, ns),
        in_specs=[pl.BlockSpec((ts, K), lambda j, s: (s, 0)), pl.BlockSpec((ts, tn), lambda j, s: (s, j))],
        out_specs=pl.BlockSpec((K, tn), lambda j, s: (0, j)),
        out_shape=jax.ShapeDtypeStruct((K, N), F32),
        compiler_params=_params(("arbitrary", "arbitrary")),
    )(a, b)


def _att_in_specs(nb):
    last = nb - 1
    cur = lambda n: jnp.minimum(n, last)
    prev = lambda n: jnp.maximum(jnp.minimum(n, last) - 1, 0)
    return [
        pl.BlockSpec((L, 1024), lambda n: (cur(n), 0)),
        pl.BlockSpec((L, 128), lambda n: (prev(n), 16)),
        pl.BlockSpec((L, 128), lambda n: (cur(n), 16)),
        pl.BlockSpec((L, 128), lambda n: (prev(n), 17)),
        pl.BlockSpec((L, 128), lambda n: (cur(n), 17)),
        pl.BlockSpec((L, 1024), lambda n: (cur(n), 1)),
        _full((2, HEADS, L, 2 * L)),
        pl.BlockSpec(memory_space=pltpu.SMEM),
    ]


GH = HEADS // KV
GB = 8


def _stack_heads(ref, h0, nh, scr):
    for g in range(nh):
        scr[(h0 + g) * L:(h0 + g + 1) * L, :] = ref[:, (h0 + g) * DH:(h0 + g + 1) * DH].astype(F32)
    return scr[h0 * L:(h0 + nh) * L, :]


def _unstack_heads(val, h0, nh, ref):
    for g in range(nh):
        ref[:, (h0 + g) * DH:(h0 + g + 1) * DH] = val[g * L:(g + 1) * L, :]


def _sink_rows(s_ref, h0, nh):
    return jnp.concatenate([jnp.full((L, 1), s_ref[h0 + g], F32) for g in range(nh)], axis=0)


def _att_probs(qh, kk, bias_h, sk):
    logits = _dot_nt(qh, kk) + bias_h
    m =jnp.maximum(jnp.max(logits, axis=-1, keepdims=True), sk)
    p = jnp.exp(logits - m)
    es = jnp.exp(sk - m)
    den = jnp.sum(p, axis=-1, keepdims=True) + es
    return p / den, es / den


def att_fwd(proj, bias, sinks):
    S = proj.shape[0]
    nb = S // L

    def body(q_ref, kp_ref, kc_ref, vp_ref, vc_ref, z_ref, bias_ref, s_ref, y_ref, o_scr):
        table = jnp.where(pl.program_id(0) > 0, 1, 0)
        for kv in range(KV):
            sl = slice(kv * DH, (kv + 1) * DH)
            kk = jnp.concatenate([kp_ref[:, sl], kc_ref[:, sl]], axis=0).astype(BF16)
            vv = jnp.concatenate([vp_ref[:, sl], vc_ref[:, sl]], axis=0).astype(BF16)
            for g in range(GH):
                h = kv * GH + g
                hs = slice(h * DH, (h + 1) * DH)
                qh = (q_ref[:, hs] * 0.125).astype(BF16)
                P, _ = _att_probs(qh, kk, bias_ref[table, h], s_ref[h])
                o_scr[:, hs] = _dot(P.astype(BF16), vv)
        z = z_ref[...].astype(F32)
        y_ref[...] = (o_scr[...] * (z * _sigmoid(z))).astype(BF16)

    return pl.pallas_call(
        body, name="att_fwd", grid=(nb,),
        in_specs=_att_in_specs(nb),
        out_specs=pl.BlockSpec((L, 1024), lambda n: (n, 0)),
        out_shape=jax.ShapeDtypeStruct((S, 1024), BF16),
        scratch_shapes=[pltpu.VMEM((L, 1024), F32)],
        compiler_params=_params(("arbitrary",)),
    )(proj, proj, proj, proj, proj, proj, bias, sinks)


def att_bwd(dy, proj, bias, sinks):
    S = proj.shape[0]
    nb = S // L
    last = nb - 1

    def body(dy_ref, q_ref, kp_ref, kc_ref, vp_ref, vc_ref, z_ref, bias_ref, s_ref,
             dout_ref, dbias_ref, dsink_ref, carry, band, dq_scr, dz_scr, qs_scr, zs_scr, dys_scr):
        n = pl.program_id(0)

        @pl.when(n == 0)
        def _():
            carry[...] = jnp.zeros_like(carry)
            dq_scr[...] = jnp.zeros_like(dq_scr)
            dz_scr[...] = jnp.zeros_like(dz_scr)
            dbias_ref[...] = jnp.zeros_like(dbias_ref)
            dsink_ref[...] = jnp.zeros_like(dsink_ref)

        dout_ref[:, 0:1024] = dq_scr[...].astype(BF16)
        dout_ref[:, 1024:2048] = dz_scr[...].astype(BF16)
        band[...] = jnp.zeros_like(band)

        @pl.when(n < nb)
        def _():
            table = jnp.where(n > 0, 1, 0)
            lane = lax.broadcasted_iota(jnp.int32, (1, 128), 1)
            dsink = jnp.zeros((1, 128), F32)
            for kv in range(KV):
                sl = slice(kv * DH, (kv + 1) * DH)
                kk = jnp.concatenate([kp_ref[:, sl], kc_ref[:, sl]], axis=0).astype(BF16)
                vv = jnp.concatenate([vp_ref[:, sl], vc_ref[:, sl]], axis=0).astype(BF16)
                dk_acc = jnp.zeros((2 * L, DH), F32)
                dv_acc = jnp.zeros((2 * L, DH), F32)
                for h0 in range(kv * GH, (kv + 1) * GH, GB):
                    qs = (_stack_heads(q_ref, h0, GB, qs_scr) * 0.125).astype(BF16)
                    bias_g = bias_ref[table, h0:h0 + GB].reshape(GB * L, 2 * L)
                    P, psink = _att_probs(qs, kk, bias_g, _sink_rows(s_ref, h0, GB))
                    zs = _stack_heads(z_ref, h0, GB, zs_scr)
                    dys = _stack_heads(dy_ref, h0, GB, dys_scr)
                    sg = _sigmoid(zs)
                    O = _dot(P.astype(BF16), vv)
                    _unstack_heads(dys * O * (sg * (1.0 + zs * (1.0 - sg))), h0, GB, dz_scr)
                    dOb = (dys * (zs * sg)).astype(BF16)
                    dP = _dot_nt(dOb, vv)
                    delta = jnp.sum(P * dP, axis=-1, keepdims=True)
                    dS = P * (dP - delta)
                    sd = psink * delta
                    for g in range(GB):
                        dsink = dsink + jnp.where(lane == h0 + g, -jnp.sum(sd[g * L:(g + 1) * L, :]), 0.0)
                    _unstack_heads(_dot(dS.astype(BF16), kk) * 0.125, h0, GB, dq_scr)
                    dbias_ref[h0:h0 + GB] += dS.reshape(GB, L, 2 * L)
                    dk_acc = dk_acc + _dot_tn(dS, qs)
                    dv_acc = dv_acc + _dot_tn(P, dOb)
                band[:, sl] = dk_acc
                band[:, 128 + kv * DH:128 + (kv + 1) * DH] = dv_acc
            dsink_ref[...] += dsink

        out = carry[...] + band[0:L, :]
        dout_ref[:, 2048:2304] = out.astype(BF16)
        carry[...] = band[L:2 * L, :]

    cur = lambda n: jnp.minimum(n, last)
    lag = lambda n: jnp.maximum(n - 1, 0)
    return pl.pallas_call(
        body, name="att_bwd", grid=(nb + 1,),
        in_specs=[pl.BlockSpec((L, 1024), lambda n: (cur(n), 0))] + _att_in_specs(nb),
        out_specs=[pl.BlockSpec((L, 2304), lambda n: (lag(n), 0)), _full((HEADS, L, 2 * L)), _full((1, 128))],
        out_shape=[jax.ShapeDtypeStruct((S, 2304), BF16),
                   jax.ShapeDtypeStruct((HEADS, L, 2 * L), F32), jax.ShapeDtypeStruct((1, 128), F32)],
        scratch_shapes=[pltpu.VMEM((L, 256), F32), pltpu.VMEM((2 * L, 256), F32),
                        pltpu.VMEM((L, 1024), F32), pltpu.VMEM((L, 1024), F32)]
        + [pltpu.VMEM((HEADS * L, DH), F32)] * 3,
        compiler_params=_params(("arbitrary",)),
    )(dy, proj, proj, proj, proj, proj, proj, bias, sinks)


def _sgu_in_specs():
    return [
        pl.BlockSpec((L, 1024), lambda c: (c, 0)),
        pl.BlockSpec((L, 1024), lambda c: (c, 1)),
        pl.BlockSpec((L, 1024), lambda c: (c, 2)),
        _full((1, 1024)), _full((1, 1024)), _full((8, L, L)), _full((L, 8)),
    ]


def _sgu_norm(v, lg, lb):
    mu = jnp.mean(v, axis=-1, keepdims=True)
    vc = v - mu
    rstd = lax.rsqrt(jnp.mean(vc * vc, axis=-1, keepdims=True) + EPS)
    xhat = vc * rstd
    return xhat * lg + lb, xhat, rstd


def _tril():
    return lax.broadcasted_iota(jnp.int32, (L, L), 0) >= lax.broadcasted_iota(jnp.int32, (L, L), 1)


def sgu_fwd(proj, ln_g, ln_b, w, b_t):
    S = proj.shape[0]

    def body(u_ref, v_ref, z_ref, lg_ref, lb_ref, w_ref, bt_ref, y_ref):
        vn, _, _ = _sgu_norm(v_ref[...].astype(F32), lg_ref[...], lb_ref[...])
        tri = _tril()
        parts = []
        for g in range(8):
            wg = jnp.where(tri, w_ref[g], 0.0).astype(BF16)
            parts.append(_dot(wg, vn[:, g * 128:(g + 1) * 128].astype(BF16)) + bt_ref[:, g:g + 1])
        mixed = jnp.concatenate(parts, axis=1)
        z = z_ref[...].astype(F32)
        y_ref[...] = (u_ref[...].astype(F32) * mixed * (z * _sigmoid(z))).astype(BF16)

    return pl.pallas_call(
        body, name="sgu_fwd", grid=(S // L,),
        in_specs=_sgu_in_specs(),
        out_specs=pl.BlockSpec((L, 1024), lambda c: (c, 0)),
        out_shape=jax.ShapeDtypeStruct((S, 1024), BF16),
        compiler_params=_params(("arbitrary",)),
    )(proj, proj, proj, ln_g, ln_b, w, b_t)


def sgu_bwd(dy, proj, ln_g, ln_b, w, b_t):
    S = proj.shape[0]

    def body(dy_ref, u_ref, v_ref, z_ref, lg_ref, lb_ref, w_ref, bt_ref,
             dout_ref, dw_ref, dbt_ref, dlg_ref, dlb_ref):
        @pl.when(pl.program_id(0) == 0)
        def _():
            dw_ref[...] = jnp.zeros_like(dw_ref)
            dbt_ref[...] = jnp.zeros_like(dbt_ref)
            dlg_ref[...] = jnp.zeros_like(dlg_ref)
            dlb_ref[...] = jnp.zeros_like(dlb_ref)

        lg = lg_ref[...]
        vn, xhat, rstd = _sgu_norm(v_ref[...].astype(F32), lg, lb_ref[...])
        tri = _tril()
        lane = lax.broadcasted_iota(jnp.int32, (L, 128), 1)
        wgs, parts = [], []
        for g in range(8):
            wg = jnp.where(tri, w_ref[g], 0.0)
            wgs.append(wg)
            parts.append(_dot(wg.astype(BF16), vn[:, g * 128:(g + 1) * 128].astype(BF16)) + bt_ref[:, g:g + 1])
        mixed = jnp.concatenate(parts, axis=1)
        z = z_ref[...].astype(F32)
        sg = _sigmoid(z)
        silu = z * sg
        dy_v = dy_ref[...]
        u = u_ref[...].astype(F32)
        dout_ref[:, 0:1024] = (dy_v * mixed * silu).astype(BF16)
        dout_ref[:, 2048:3072] = (dy_v * u * mixed * (sg * (1.0 + z * (1.0 - sg)))).astype(BF16)
        dmixed = dy_v * u * silu
        dbt = jnp.zeros((L, 128), F32)
        dvn_parts = []
        for g in range(8):
            dm = dmixed[:, g * 128:(g + 1) * 128]
            dmb = dm.astype(BF16)
            dbt = dbt + jnp.where(lane == g, jnp.sum(dm, axis=1, keepdims=True), 0.0)
            dw_ref[g] += jnp.where(tri, _dot_nt(dmb, vn[:, g * 128:(g + 1) * 128].astype(BF16)), 0.0)
            dvn_parts.append(_dot_tn(wgs[g], dmb))
        dbt_ref[...] += dbt
        dvn = jnp.concatenate(dvn_parts, axis=1)
        dlg_ref[...] += jnp.sum(dvn * xhat, axis=0, keepdims=True)
        dlb_ref[...] += jnp.sum(dvn, axis=0, keepdims=True)
        dxh = dvn * lg
        dv = rstd * (dxh - jnp.mean(dxh, axis=-1, keepdims=True)
                     - xhat * jnp.mean(dxh * xhat, axis=-1, keepdims=True))
        dout_ref[:, 1024:2048] = dv.astype(BF16)

    return pl.pallas_call(
        body, name="sgu_bwd", grid=(S // L,),
        in_specs=[pl.BlockSpec((L, 1024), lambda c: (c, 0))] + _sgu_in_specs(),
        out_specs=[pl.BlockSpec((L, 3072), lambda c: (c, 0)), _full((8, L, L)), _full((L, 128)),
                   _full((1, 1024)), _full((1, 1024))],
        out_shape=[jax.ShapeDtypeStruct((S, 3072), BF16), jax.ShapeDtypeStruct((8, L, L), F32),
                   jax.ShapeDtypeStruct((L, 128), F32), jax.ShapeDtypeStruct((1, 1024), F32),
                   jax.ShapeDtypeStruct((1, 1024), F32)],
        compiler_params=_params(("arbitrary",)),
    )(dy, proj, proj, proj, ln_g, ln_b, w, b_t)


def _expand_matrices():
    e = (np.arange(SSM_W)[None, :] // SSM_P == np.arange(128)[:, None]).astype(np.float32)
    return jnp.asarray(e, BF16), jnp.asarray(e.T, BF16)


def _rows_from(ref, start):
    C = ref.shape[1]
    tiles = ref[...].reshape(17, 8, C)
    q, s = divmod(start, 8)
    if s == 0:
        return tiles[q:q + 16].reshape(L, C)
    rolled = pltpu.roll(tiles, 8 - s, axis=1)
    sub = lax.broadcasted_iota(jnp.int32, (16, 8, C), 1)
    return jnp.where(sub < 8 - s, rolled[q:q + 16], rolled[q + 1:q + 17]).reshape(L, C)


def _ssd_common(ext_ref, cw_ref, cb_ref, dt_raw, dtb, alog):
    taps = [_rows_from(ext_ref, 5 + k) for k in range(CONV_K)]
    pre = cb_ref[...]
    for k in range(CONV_K):
        pre = pre + cw_ref[k:k + 1, :] * taps[k]
    sg_pre = _sigmoid(pre)
    xc = pre * sg_pre
    dt = _softplus(dt_raw + dtb)
    a = -jnp.exp(alog)
    adt = dt * a
    acs = _sel_dot(_tril(), adt, 3)
    return pre, sg_pre, xc, dt, a, acs, taps


def _ssd_in_specs(rev, nc):
    cidx = (lambda c: nc - 1 - c) if rev else (lambda c: c)
    return [
        pl.BlockSpec((L, 2048), lambda c: (cidx(c), 0)),
        pl.BlockSpec((L, 1024), lambda c: (cidx(c), 2)),
        pl.BlockSpec((L, 1024), lambda c: (cidx(c), 3)),
        pl.BlockSpec((L, 1024), lambda c: (cidx(c), 4)),
        pl.BlockSpec((L, 128), lambda c: (cidx(c), 40)),
        _full((8, CONV_C)), _full((1, CONV_C)), _full((1, 128)), _full((1, 128)), _full((1, 128)),
        _full((1, SSM_W)), _full((128, SSM_W)), _full((SSM_W, 128)),
    ]


def ssd_fwd(proj, conv_w, conv_b, dt_bias, a_log, d_skip, norm_g):
    S = proj.shape[0]
    nc = S // L

    def body(z_ref, xa_ref, xb_ref, xc_ref, dt_ref, cw_ref, cb_ref, dtb_ref, alog_ref, dsk_ref, ng_ref,
             ex_ref, ext_ref, y_ref, hs_ref, H, ext, ysc):
        @pl.when(pl.program_id(0) == 0)
        def _():
            H[...] = jnp.zeros_like(H)
            ext[0:8, :] = jnp.zeros((8, CONV_C), F32)

        for k, ref in enumerate((xa_ref, xb_ref, xc_ref)):
            ext[8:8 + L, k * 1024:(k + 1) * 1024] = ref[...].astype(F32)
        pre, sg_pre, xc, dt, a, acs, _ = _ssd_common(ext, cw_ref, cb_ref, dt_ref[...].astype(F32), dtb_ref[...],
                                                     alog_ref[...])
        for k, ref in enumerate((xa_ref, xb_ref, xc_ref)):
            ext[0:8, k * 1024:(k + 1) * 1024] = ref[L - 8:L, :].astype(F32)
        xs = xc[:, 0:SSM_W]
        acs_t = acs.T
        ex = ex_ref[...]
        dt_x = _dot_sel(dt, ex, 2)
        xdt = xs * dt_x
        eacs_x = _dot_sel(jnp.exp(acs), ex, 2)
        xw = xdt * _dot_sel(jnp.exp(acs[L - 1:L, :] - acs), ex, 2)
        cd_row = jnp.exp(acs[L - 1:L, :])
        hs_ref[0] = H[...]
        tri = _tril()
        for g in range(SSM_G):
            gs = slice(g * 512, (g + 1) * 512)
            bg = xc[:, SSM_W + g * SSM_N:SSM_W + (g + 1) * SSM_N].astype(BF16)
            cg = xc[:, SSM_W + 512 + g * SSM_N:SSM_W + 512 + (g + 1) * SSM_N].astype(BF16)
            G = _dot_nt(cg, bg)
            yoff = _dot_nt(cg, H[gs, :].astype(BF16)) * eacs_x[:, gs]
            Sg = _dot_tn(xw[:, gs], bg)
            for j in range(8):
                hh = g * 8 + j
                hs = slice(hh * SSM_P, (hh + 1) * SSM_P)
                seg = acs[:, hh:hh + 1] - acs_t[hh:hh + 1, :]
                dk = jnp.where(tri, jnp.exp(seg), 0.0)
                yd = _dot((G * dk).astype(BF16), xdt[:, hs].astype(BF16))
                ysc[:, hs] = yd + yoff[:, j * SSM_P:(j + 1) * SSM_P]
                H[hs, :] = H[hs, :] * cd_row[:, hh:hh + 1] + Sg[j * SSM_P:(j + 1) * SSM_P, :]
        d_x = _dot_sel(jnp.broadcast_to(dsk_ref[...], (8, 128)), ex, 3)[0:1, :]
        Y = ysc[...] + d_x * xs
        z = z_ref[...].astype(F32)
        yz = Y * (z * _sigmoid(z))
        ng = ng_ref[...]
        for g in range(SSM_G):
            gs = slice(g * 512, (g + 1) * 512)
            t = yz[:, gs]
            rstd = lax.rsqrt(jnp.mean(t * t, axis=-1, keepdims=True) + EPS)
            y_ref[:, gs] = (t * rstd * ng[:, gs]).astype(BF16)

    return pl.pallas_call(
        body, name="ssd_fwd", grid=(nc,),
        in_specs=_ssd_in_specs(False, nc),
        out_specs=[pl.BlockSpec((L, SSM_W), lambda c: (c, 0)), pl.BlockSpec((1, SSM_W, SSM_N), lambda c: (c, 0, 0))],
        out_shape=[jax.ShapeDtypeStruct((S, SSM_W), BF16), jax.ShapeDtypeStruct((nc, SSM_W, SSM_N), F32)],
        scratch_shapes=[pltpu.VMEM((SSM_W, SSM_N), F32), pltpu.VMEM((8 + L, CONV_C), F32),
                        pltpu.VMEM((L, SSM_W), F32)],
        compiler_params=_params(("arbitrary",)),
    )(proj, proj, proj, proj, proj, conv_w, conv_b, dt_bias, a_log, d_skip, norm_g, *_expand_matrices())


def ssd_bwd(dy, proj, hstates, conv_w, conv_b, dt_bias, a_log, d_skip, norm_g):
    S = proj.shape[0]
    nc = S // L
    cidx = lambda c: nc - 1 - c

    def body(dy_ref, z_ref, xa_ref, xb_ref, xc_ref, dt_ref, cw_ref, cb_ref, dtb_ref, alog_ref, dsk_ref, ng_ref,
             ex_ref, ext_ref, pa_ref, pb_ref, pc_ref, hp_ref,
             dout_ref, dcw_ref, dcb_ref, ddtb_ref, dalog_ref, ddsk_ref, dng_ref,
             dH, ext, dext, ysc, yoffsc, dxdt, dxc, tsc):
        step = pl.program_id(0)
        c = nc - 1 - step

        @pl.when(step == 0)
        def _():
            dH[...] = jnp.zeros_like(dH)
            dext[L:L + 8, :] = jnp.zeros((8, CONV_C), F32)
            for r in (dcw_ref, dcb_ref, ddtb_ref, dalog_ref, ddsk_ref, dng_ref):
                r[...] = jnp.zeros_like(r)

        for k, (ref, prev) in enumerate(((xa_ref, pa_ref), (xb_ref, pb_ref), (xc_ref, pc_ref))):
            ext[0:8, k * 1024:(k + 1) * 1024] = jnp.where(c > 0, prev[8:16, :].astype(F32), 0.0)
            ext[8:8 + L, k * 1024:(k + 1) * 1024] = ref[...].astype(F32)
        dtb = dtb_ref[...]
        dt_raw = dt_ref[...].astype(F32)
        pre, sg_pre, xc, dt, a, acs, taps = _ssd_common(ext, cw_ref, cb_ref, dt_raw, dtb, alog_ref[...])
        xs = xc[:, 0:SSM_W]
        acs_t = acs.T
        ex = ex_ref[...]
        dt_x = _dot_sel(dt, ex, 2)
        xdt = xs * dt_x
        eacs_x = _dot_sel(jnp.exp(acs), ex, 2)
        dte_x = _dot_sel(jnp.exp(acs[L - 1:L, :] - acs), ex, 2)
        xw = xdt * dte_x
        cd_row = jnp.exp(acs[L - 1:L, :])
        tri = _tril()

        Gs, Cs, Bs = [], [], []
        for g in range(SSM_G):
            gs = slice(g * 512, (g + 1) * 512)
            bg = xc[:, SSM_W + g * SSM_N:SSM_W + (g + 1) * SSM_N].astype(BF16)
            cg = xc[:, SSM_W + 512 + g * SSM_N:SSM_W + 512 + (g + 1) * SSM_N].astype(BF16)
            G = _dot_nt(cg, bg)
            Gs.append(G), Cs.append(cg), Bs.append(bg)
            yoffsc[:, gs] = _dot_nt(cg, hp_ref[0, gs, :].astype(BF16)) * eacs_x[:, gs]
            for j in range(8):
                hh = g * 8 + j
                hs = slice(hh * SSM_P, (hh + 1) * SSM_P)
                seg = acs[:, hh:hh + 1] - acs_t[hh:hh + 1, :]
                dk = jnp.where(tri, jnp.exp(seg), 0.0)
                ysc[:, hs] = _dot((G * dk).astype(BF16), xdt[:, hs].astype(BF16))
        d_x = _dot_sel(jnp.broadcast_to(dsk_ref[...], (8, 128)), ex, 3)[0:1, :]
        yoff = yoffsc[...]
        Y = ysc[...] + yoff + d_x * xs

        z = z_ref[...].astype(F32)
        sgz = _sigmoid(z)
        silu_z = z * sgz
        yz = Y * silu_z
        ng = ng_ref[...]
        dout = dy_ref[...]
        dyn = dout * ng
        dyz_parts, dng_parts = [], []
        for g in range(SSM_G):
            gs = slice(g * 512, (g + 1) * 512)
            t = yz[:, gs]
            rstd = lax.rsqrt(jnp.mean(t * t, axis=-1, keepdims=True) + EPS)
            dng_parts.append(jnp.sum(dout[:, gs] * t * rstd, axis=0, keepdims=True))
            dn = dyn[:, gs]
            dyz_parts.append(rstd * dn - t * (rstd * rstd * rstd) * jnp.mean(dn * t, axis=-1, keepdims=True))
        dng_ref[...] += jnp.concatenate(dng_parts, axis=1)
        dyz = jnp.concatenate(dyz_parts, axis=1)
        dY = dyz * silu_z
        dout_ref[:, 0:SSM_W] = (dyz * Y * (sgz * (1.0 + z * (1.0 - sgz)))).astype(BF16)

        ex_t = ext_ref[...]
        ddsk_ref[...] += _dot_sel(jnp.broadcast_to(jnp.sum(dY * xs, axis=0, keepdims=True), (8, SSM_W)), ex_t, 3)[0:1, :]

        lane = lax.broadcasted_iota(jnp.int32, (L, 128), 1)
        subl = lax.broadcasted_iota(jnp.int32, (128, L), 0)
        coll = lax.broadcasted_iota(jnp.int32, (128, L), 1)
        r_cols = jnp.zeros((L, 128), F32)
        c_rows = jnp.zeros((128, L), F32)
        for g in range(SSM_G):
            gs = slice(g * 512, (g + 1) * 512)
            G, cg, bg = Gs[g], Cs[g], Bs[g]
            hp_g = hp_ref[0, gs, :]
            dh_g = dH[gs, :]
            dY_g = dY[:, gs]
            dZ = dY_g * eacs_x[:, gs]
            dZb = dZ.astype(BF16)
            dC = _dot(dZb, hp_g.astype(BF16))
            dh_from_off = _dot_tn(dZ, cg)
            dhb = dh_g.astype(BF16)
            Q = _dot_nt(bg, dhb)
            dB = _dot(xw[:, gs].astype(BF16), dhb)
            qd = Q * dte_x[:, gs]
            dxdt[:, gs] = qd
            tsc[:, gs] = qd * xdt[:, gs]
            dG = jnp.zeros((L, L), F32)
            for j in range(8):
                hh = g * 8 + j
                hs = slice(hh * SSM_P, (hh + 1) * SSM_P)
                seg = acs[:, hh:hh + 1] - acs_t[hh:hh + 1, :]
                dk = jnp.where(tri, jnp.exp(seg), 0.0)
                M = G * dk
                dYh = dY[:, hs]
                dYhb = dYh.astype(BF16)
                dM = _dot_nt(dYhb, xdt[:, hs].astype(BF16))
                dxdt[:, hs] += _dot_tn(M, dYhb)
                dG = dG + dM * dk
                Wm = dM * M
                r_cols = r_cols + jnp.where(lane == hh, jnp.sum(Wm, axis=1, keepdims=True), 0.0)
                c_rows = c_rows + jnp.where(subl == hh, jnp.sum(Wm, axis=0, keepdims=True), 0.0)
                pj = slice(j * SSM_P, (j + 1) * SSM_P)
                cd_h = cd_row[:, hh:hh + 1]
                dcd = jnp.sum(dh_g[pj, :] * hp_g[pj, :]) * cd_h
                c_rows = c_rows - jnp.where((subl == hh) & (coll == L - 1), dcd, 0.0)
                dH[hs, :] = dh_g[pj, :] * cd_h + dh_from_off[pj, :]
            dGb = dG.astype(BF16)
            dC = dC + _dot(dGb, bg)
            dB = dB + _dot_tn(dG, cg)
            dxc[:, SSM_W + g * SSM_N:SSM_W + (g + 1) * SSM_N] = dB
            dxc[:, SSM_W + 512 + g * SSM_N:SSM_W + 512 + (g + 1) * SSM_N] = dC

        row = lax.broadcasted_iota(jnp.int32, (L, 128), 0)
        tv = tsc[...]
        t_last = _dot_sel(jnp.broadcast_to(jnp.sum(tv, axis=0, keepdims=True), (8, SSM_W)), ex_t, 3)[0:1, :]
        dacs = (r_cols - c_rows.T + _dot_sel(dY * yoff - tv, ex_t, 2) + jnp.where(row == L - 1, t_last, 0.0))
        triu = lax.broadcasted_iota(jnp.int32, (L, L), 0) <= lax.broadcasted_iota(jnp.int32, (L, L), 1)
        dadt = _sel_dot(triu, dacs, 3)
        dxdt_v = dxdt[...]
        ddt = _dot_sel(dxdt_v * xs, ex_t, 2) + dadt * a
        dalog_ref[...] += jnp.sum(dadt * dt * a, axis=0, keepdims=True)
        ddt_raw = jnp.where(lane < SSM_H, ddt * _sigmoid(dt_raw + dtb), 0.0)
        ddtb_ref[...] += jnp.sum(ddt_raw, axis=0, keepdims=True)
        dout_ref[:, 5120:5248] = ddt_raw.astype(BF16)
        dout_ref[:, 5248:5376] = jnp.zeros((L, 128), BF16)

        dxc[:, 0:SSM_W] = dxdt_v * dt_x + d_x * dY
        dpre = dxc[...] * (sg_pre * (1.0 + pre * (1.0 - sg_pre)))
        dcb_ref[...] += jnp.sum(dpre, axis=0, keepdims=True)
        dext[0:L, :] = dpre
        x_cur = ext[8:8 + L, :]
        dx = None
        for k in range(CONV_K):
            dsh = _rows_from(dext, 3 - k)
            term = cw_ref[k:k + 1, :] * dsh
            dx = term if dx is None else dx + term
            dcw_ref[k:k + 1, :] += jnp.sum(dsh * x_cur, axis=0, keepdims=True)
        dout_ref[:, SSM_W:SSM_W + CONV_C] = dx.astype(BF16)
        dext[L:L + 8, :] = dpre[0:8, :]

    big = lambda w: pl.BlockSpec((L, w), lambda c: (cidx(c), 0))
    return pl.pallas_call(
        body, name="ssd_bwd", grid=(nc,),
        in_specs=[big(SSM_W)] + _ssd_in_specs(True, nc) + [
            pl.BlockSpec((16, 1024), lambda c, k=k: (jnp.maximum(8 * cidx(c) - 1, 0), k)) for k in (2, 3, 4)] + [
            pl.BlockSpec((1, SSM_W, SSM_N), lambda c: (cidx(c), 0, 0))],
        out_specs=[big(5376), _full((8, CONV_C)), _full((1, CONV_C)),
                   _full((1, 128)), _full((1, 128)), _full((1, 128)), _full((1, SSM_W))],
        out_shape=[jax.ShapeDtypeStruct((S, 5376), BF16), jax.ShapeDtypeStruct((8, CONV_C), F32),
                   jax.ShapeDtypeStruct((1, CONV_C), F32), jax.ShapeDtypeStruct((1, 128), F32),
                   jax.ShapeDtypeStruct((1, 128), F32), jax.ShapeDtypeStruct((1, 128), F32),
                   jax.ShapeDtypeStruct((1, SSM_W), F32)],
        scratch_shapes=[pltpu.VMEM((SSM_W, SSM_N), F32), pltpu.VMEM((8 + L, CONV_C), F32),
                        pltpu.VMEM((L + 8, CONV_C), F32), pltpu.VMEM((L, SSM_W), F32),
                        pltpu.VMEM((L, SSM_W), F32), pltpu.VMEM((L, SSM_W), F32),
                        pltpu.VMEM((L, CONV_C), F32), pltpu.VMEM((L, SSM_W), F32)],
        compiler_params=_params(("arbitrary",)),
    )(dy, proj, proj, proj, proj, proj, conv_w, conv_b, dt_bias, a_log, d_skip, norm_g, *_expand_matrices(),
      proj, proj, proj, hstates)


def _resident(shape):
    nd = len(shape)
    return pl.BlockSpec(shape, lambda *_: (0,) * nd, pipeline_mode=pl.Buffered(1))


def merge_fwd(y_att, y_sg, y_ssm, proj, x, w_a, w_s, w_m, w_o, g_post):
    S = x.shape[0]
    tm = 256

    def body(ya_ref, ys_ref, ym_ref, gate_ref, x_ref, wa_ref, ws_ref, wm_ref, wo_ref, gp_ref,
             xn_ref, bra_ref, brs_ref, brm_ref, mg_ref, out_ref):
        bra = _dot(ya_ref[...], wa_ref[...])
        brs = _dot(ys_ref[...], ws_ref[...])
        brm = _dot(ym_ref[...], wm_ref[...])
        bra_ref[...] = bra.astype(BF16)
        brs_ref[...] = brs.astype(BF16)
        brm_ref[...] = brm.astype(BF16)
        gate = gate_ref[...].astype(F32)
        merged = (_sigmoid(gate[:, 0:1024]) * bra + _sigmoid(gate[:, 1024:2048]) * brs
                  + _sigmoid(gate[:, 2048:3072]) * brm)
        mb = merged.astype(BF16)
        mg_ref[...] = mb
        o = _dot(mb, wo_ref[...])
        out_ref[...] = o
        r = lax.rsqrt(jnp.mean(o * o, axis=-1, keepdims=True) + EPS)
        xn_ref[...] = x_ref[...] + o * r * gp_ref[...]

    row = lambda w: pl.BlockSpec((tm, w), lambda i: (i, 0))
    return pl.pallas_call(
        body, name="merge_fwd", grid=(S // tm,),
        in_specs=[row(1024), row(1024), row(2048), pl.BlockSpec((tm, 3072), lambda i: (i, 0)),
                  row(D), _resident((1024, D)), _resident((1024, D)), _resident((2048, D)), _resident((D, D)),
                  _full((1, D))],
        out_specs=[row(D)] * 6,
        out_shape=[jax.ShapeDtypeStruct((S, D), F32)] + [jax.ShapeDtypeStruct((S, D), BF16)] * 4
        + [jax.ShapeDtypeStruct((S, D), F32)],
        compiler_params=_params(("arbitrary",)),
    )(y_att, y_sg, y_ssm, proj, x, w_a, w_s, w_m, w_o, g_post)


def merge_bwd(dy, out, g_post, proj, br_a, br_s, br_m, w_a, w_s, w_m, w_o):
    S = dy.shape[0]
    tm = 256

    def body(dy_ref, o_ref, gp_ref, gate_ref, bra_ref, brs_ref, brm_ref, wa_ref, ws_ref, wm_ref, wo_ref,
             dout_ref, dba_ref, dbs_ref, dbm_ref, dgate_ref, dya_ref, dys_ref, dym_ref, dgp_ref):
        @pl.when(pl.program_id(0) == 0)
        def _():
            dgp_ref[...] = jnp.zeros_like(dgp_ref)

        o = o_ref[...]
        dyv = dy_ref[...]
        r = lax.rsqrt(jnp.mean(o * o, axis=-1, keepdims=True) + EPS)
        dyg = dyv * gp_ref[...]
        do = r * dyg - o * (r * r * r) * jnp.mean(dyg * o, axis=-1, keepdims=True)
        dgp_ref[...] += jnp.sum(dyv * o * r, axis=0, keepdims=True)
        dob = do.astype(BF16)
        dout_ref[...] = dob
        dmerged = _dot_nt(dob, wo_ref[...])
        for idx, (br_ref, dbr_ref, w_ref, dyi_ref) in enumerate((
                (bra_ref, dba_ref, wa_ref, dya_ref), (brs_ref, dbs_ref, ws_ref, dys_ref),
                (brm_ref, dbm_ref, wm_ref, dym_ref))):
            s = _sigmoid(gate_ref[:, idx * 1024:(idx + 1) * 1024].astype(F32))
            dbr = (dmerged * s).astype(BF16)
            dbr_ref[...] = dbr
            dgate_ref[:, idx * 1024:(idx + 1) * 1024] = (dmerged * br_ref[...].astype(F32) * s * (1.0 - s)).astype(BF16)
            dyi_ref[...] = _dot_nt(dbr, w_ref[...])

    row = lambda w: pl.BlockSpec((tm, w), lambda i: (i, 0))
    return pl.pallas_call(
        body, name="merge_bwd", grid=(S // tm,),
        in_specs=[row(D), row(D), _full((1, D)), pl.BlockSpec((tm, 3072), lambda i: (i, 0)),
                  row(D), row(D), row(D),
                  _resident((1024, D)), _resident((1024, D)), _resident((2048, D)), _resident((D, D))],
        out_specs=[row(D), row(D), row(D), row(D), row(3072), row(1024), row(1024), row(2048), _full((1, D))],
        out_shape=[jax.ShapeDtypeStruct((S, D), BF16)] * 4 + [
            jax.ShapeDtypeStruct((S, 3072), BF16), jax.ShapeDtypeStruct((S, 1024), F32),
            jax.ShapeDtypeStruct((S, 1024), F32), jax.ShapeDtypeStruct((S, 2048), F32),
            jax.ShapeDtypeStruct((1, D), F32)],
        compiler_params=_params(("arbitrary",)),
    )(dy, out, g_post, proj, br_a, br_s, br_m, w_a, w_s, w_m, w_o)


def loss_head(y, target):
    S = y.shape[0]
    tm = 512

    def body(y_ref, t_ref, dy_ref, loss_ref):
        @pl.when(pl.program_id(0) == 0)
        def _():
            loss_ref[...] = jnp.zeros_like(loss_ref)
        e = y_ref[...] - t_ref[...]
        dy_ref[...] = e * (1.0 / D)
        loss_ref[...] += 0.5 * jnp.sum(jnp.mean(e * e, axis=-1, keepdims=True))

    row = pl.BlockSpec((tm, D), lambda i: (i, 0))
    return pl.pallas_call(
        body, name="loss_head", grid=(S // tm,),
        in_specs=[row, row], out_specs=[row, _full((1, 128))],
        out_shape=[jax.ShapeDtypeStruct((S, D), F32), jax.ShapeDtypeStruct((1, 128), F32)],
        compiler_params=_params(("arbitrary",)),
    )(y, target)


def _adam(w, g, m, v):
    mn = ADAM_B1 * m + (1.0 - ADAM_B1) * g
    vn = ADAM_B2 * v + (1.0 - ADAM_B2) * (g * g)
    m_hat = mn / (1.0 - ADAM_B1 ** ADAM_STEP)
    v_hat = vn / (1.0 - ADAM_B2 ** ADAM_STEP)
    return -ADAM_LR * (m_hat / (jnp.sqrt(v_hat) + ADAM_EPS) + ADAM_WD * w), mn, vn


def adamw_big(w, m, v, halves0, sum1, cc, name, tr):
    _, R, C = w.shape
    nper = R // tr
    f, fb, n0, off_a, off_b = halves0
    p, pb, off1 = sum1

    def body(c_ref, w_ref, m_ref, v_ref, f_ref, fb_ref, p_ref, pb_ref, g_ref, d_ref, nm_ref, nv_ref):
        i = pl.program_id(0)
        half = jnp.where(i % nper >= n0, 1, 0)
        g0 = jnp.where(c_ref[0] == half, f_ref[...], fb_ref[...])
        g = jnp.where(i < nper, g0, p_ref[...] + pb_ref[...])
        g_ref[0] = g
        d_ref[0], nm_ref[0], nv_ref[0] = _adam(w_ref[0], g, m_ref[0], v_ref[0])

    def blk0(i, c):
        il = jnp.minimum(i, nper - 1)
        return (jnp.where(il >= n0, off_b + il - n0, off_a + il), 0)

    wblk = pl.BlockSpec((1, tr, C), lambda i, c: (i // nper, i % nper, 0))
    b0 = pl.BlockSpec((tr, C), blk0)
    b1 = pl.BlockSpec((tr, C), lambda i, c: (off1 + jnp.maximum(i - nper, 0), 0))
    grid_spec = pltpu.PrefetchScalarGridSpec(
        num_scalar_prefetch=1, grid=(2 * nper,),
        in_specs=[wblk, wblk, wblk, b0, b0, b1, b1], out_specs=[wblk] * 4)
    return pl.pallas_call(
        body, name=name, grid_spec=grid_spec,
        out_shape=[jax.ShapeDtypeStruct(w.shape, F32)] * 4,
        compiler_params=_params(("arbitrary",)),
    )(cc, w, m, v, f, fb, p, pb)


def adamw_plain(w, g, m, v, name):
    def body(w_ref, g_ref, m_ref, v_ref, d_ref, nm_ref, nv_ref):
        d_ref[...], nm_ref[...], nv_ref[...] = _adam(w_ref[...], g_ref[...], m_ref[...], v_ref[...])

    return pl.pallas_call(
        body, name=name, out_shape=[jax.ShapeDtypeStruct(w.shape, F32)] * 3, compiler_params=_params(),
    )(w, g, m, v)


SMALL = {"norm_pre": ("g_pre", 8), "norm_post": ("g_post", 8), "att_sinks": ("sinks", 8), "sg_ln_g": ("ln_g", 8),
         "sg_ln_b": ("ln_b", 8), "sg_w": ("sg_w", 1024), "sg_b": ("sg_bt", 8), "ssm_conv_b": ("conv_b", 24),
         "ssm_dt_bias": ("dt_bias", 8), "ssm_a_log": ("a_log", 8), "ssm_d": ("d_skip", 8), "ssm_norm_g": ("norm_g", 16)}
SMALL_LAYER_ROWS = sum(r for _, r in SMALL.values())
REL_ROW = DEPTH * SMALL_LAYER_ROWS
LOSS_ROW = REL_ROW + 32
SMALL_ROWS = LOSS_ROW + 8


def _small_rows():
    rows, r = {}, 0
    for l in range(DEPTH):
        for name, (_, n) in SMALL.items():
            rows[(l, name)] = r
            r += n
    return rows


def adamw_small(red, rel, small):
    names = list(SMALL) + ["rel_bias"]
    params = dict(small, rel_bias=rel)
    rows = _small_rows()

    def grad_of(red_ref, l, name, n):
        r0 = rows[(l, name)]
        if name == "sg_b":
            return red_ref[r0:r0 + 8, :]
        if n < 128:
            return red_ref[r0:r0 + 1, 0:n]
        return jnp.concatenate([red_ref[r0 + j:r0 + j + 1, :] for j in range(n // 128)], axis=1)

    def body(red_ref, *refs):
        ins, outs = refs[:3 * len(names)], refs[3 * len(names):]
        for i, name in enumerate(names):
            w_ref, m_ref, v_ref = ins[3 * i:3 * i + 3]
            o = outs[4 * i:4 * i + 4]
            if name == "rel_bias":
                g = red_ref[REL_ROW:REL_ROW + 32, 0:16]
                o[0][...] = g
                o[1][...], o[2][...], o[3][...] = _adam(w_ref[...], g, m_ref[...], v_ref[...])
                continue
            for l in range(DEPTH):
                if name == "sg_w":
                    for grp in range(8):
                        r0 = rows[(l, name)] + grp * 128
                        g = red_ref[r0:r0 + 128, :]
                        o[0][l, grp] = g
                        o[1][l, grp], o[2][l, grp], o[3][l, grp] = _adam(w_ref[l, grp], g, m_ref[l, grp], v_ref[l, grp])
                elif name == "sg_b":
                    g = grad_of(red_ref, l, name, 128)
                    o[0][l] = g
                    o[1][l], o[2][l], o[3][l] = _adam(w_ref[l], g, m_ref[l], v_ref[l])
                else:
                    sl = slice(l, l + 1)
                    g = grad_of(red_ref, l, name, w_ref.shape[-1])
                    o[0][sl, :] = g
                    o[1][sl, :], o[2][sl, :], o[3][sl, :] = _adam(w_ref[sl, :], g, m_ref[sl, :], v_ref[sl, :])

    flat_in = [a for name in names for a in params[name]]
    out_shape = [jax.ShapeDtypeStruct(params[name][0].shape, F32) for name in names for _ in range(4)]
    res = pl.pallas_call(body, name="adamw_small", out_shape=out_shape, compiler_params=_params())(red, *flat_in)
    return {name: tuple(res[4 * i:4 * i + 4]) for i, name in enumerate(names)}


ANY = pl.BlockSpec(memory_space=pl.ANY)


def _place():
    x, y, c = lax.axis_index("x"), lax.axis_index("y"), lax.axis_index("c")
    others = [(1 - x, y), (x, 1 - y), (1 - x, 1 - y)]
    return x, y, c, others


def _rcopy(src, dst, ssem, rsem, to):
    return pltpu.make_async_remote_copy(src_ref=src, dst_ref=dst, send_sem=ssem, recv_sem=rsem,
                                        device_id=to, device_id_type=MESH)


def gather_weights(arrs):
    n = len(arrs)

    def body(*refs):
        srcs, outs, ssem, rsem = refs[:n], refs[n:2 * n], refs[2 * n], refs[2 * n + 1]
        x, y, c, others = _place()
        me = 2 * x + y
        sib = (x, y, 1 - c)
        first = [_rcopy(srcs[i].at[c], outs[i].at[c, me], ssem.at[6 * i + k], rsem.at[6 * i + k], (ox, oy, c))
                 for i in range(n) for k, (ox, oy) in enumerate(others)]
        for cp in first:
            cp.start()
        passed = []
        for k, (ox, oy) in enumerate(others):
            for i in range(n):
                slot = outs[i].at[c, 2 * ox + oy]
                _rcopy(slot, slot, ssem.at[6 * i + k], rsem.at[6 * i + k], sib).wait_recv()
                fw = _rcopy(slot, slot, ssem.at[6 * i + 3 + k], rsem.at[6 * i + 3 + k], sib)
                fw.start()
                passed.append(fw)
        for k, (ox, oy) in enumerate(others):
            for i in range(n):
                slot = outs[i].at[1 - c, 2 * ox + oy]
                _rcopy(slot, slot, ssem.at[6 * i + 3 + k], rsem.at[6 * i + 3 + k], sib).wait_recv()
        for cp in first + passed:
            cp.wait_send()

    return pl.pallas_call(
        body, name="gather_weights",
        in_specs=[ANY] * n, out_specs=[ANY] * n,
        out_shape=[jax.ShapeDtypeStruct((2, SHARDS) + a.shape[1:], a.dtype) for a in arrs],
        scratch_shapes=[pltpu.SemaphoreType.DMA((6 * n,)), pltpu.SemaphoreType.DMA((6 * n,))],
    )(*arrs)


HBM = pl.BlockSpec(memory_space=pltpu.HBM)
SEM = pl.BlockSpec(memory_space=pltpu.SEMAPHORE)
EFFECT = pltpu.SideEffectType.DATAFLOW_SIDE_EFFECTING


def _in_hbm(a):
    return pltpu.with_memory_space_constraint(a, pltpu.HBM)


def gather_start(srcs, after, name, by_dest=False):
    n = len(srcs)
    lands = [_in_hbm(lax.empty((SHARDS,) + a.shape[-2:], a.dtype)) for a in srcs]
    na = len(after)

    def body(*refs):
        src, land = refs[:n], refs[n:2 * n]
        ssem, rsem, token = refs[2 * n + na], refs[2 * n + na + 1], refs[-1]
        x, y, c, others = _place()
        me = 2 * x + y
        for i in range(n):
            for k, (ox, oy) in enumerate(others):
                s = src[i].at[2 * ox + oy] if by_dest else src[i]
                _rcopy(s, land[i].at[me], ssem.at[3 * i + k], rsem.at[3 * i + k], (ox, oy, c)).start()
        token[...] = jnp.zeros_like(token)

    bufs = [_in_hbm(a) for a in srcs] + lands
    out = pl.pallas_call(
        body, name=name,
        out_shape=(pltpu.SemaphoreType.DMA((3 * n,)), pltpu.SemaphoreType.DMA((3 * n,)),
                   *[pltpu.HBM(b.shape, b.dtype) for b in bufs], jax.ShapeDtypeStruct((8, 128), F32)),
        in_specs=[HBM] * (2 * n) + [ANY] * na,
        out_specs=(SEM, SEM, *[HBM] * (2 * n), pl.BlockSpec(memory_space=pltpu.VMEM)),
        input_output_aliases={i: 2 + i for i in range(2 * n)},
        compiler_params=pltpu.CompilerParams(has_side_effects=EFFECT),
    )(*bufs, *after)
    return out[0], out[1], list(out[2:2 + n]), list(out[2 + n:2 + 2 * n]), out[-1]


def gather_wait(ssem, rsem, srcs, lands, after, name, by_dest=False):
    n = len(srcs)

    def body(*refs):
        src, land = refs[:n], refs[n:2 * n]
        s_sem, r_sem = refs[2 * n], refs[2 * n + 1]
        x, y, c, others = _place()
        for i in range(n):
            for k, (ox, oy) in enumerate(others):
                s = src[i].at[2 * ox + oy] if by_dest else src[i]
                cp = _rcopy(s, land[i].at[2 * ox + oy], s_sem.at[3 * i + k], r_sem.at[3 * i + k], (ox, oy, c))
                cp.wait_send()
                cp.wait_recv()

    bufs = list(srcs) + list(lands)
    out = pl.pallas_call(
        body, name=name,
        out_shape=tuple(pltpu.HBM(b.shape, b.dtype) for b in bufs),
        in_specs=[HBM] * (2 * n) + [SEM, SEM, ANY],
        out_specs=tuple([HBM] * (2 * n)),
        input_output_aliases={i: i for i in range(2 * n)},
        compiler_params=pltpu.CompilerParams(has_side_effects=EFFECT),
    )(*bufs, ssem, rsem, after)
    return list(out[n:2 * n])


def grad_sibling_exchange(arrs):
    n = len(arrs)

    def body(*refs):
        srcs, outs, ssem, rsem = refs[:n], refs[n:2 * n], refs[2 * n], refs[2 * n + 1]
        x, y, c, _ = _place()
        cps = [_rcopy(srcs[i].at[1 - c], outs[i], ssem.at[i], rsem.at[i], (x, y, 1 - c)) for i in range(n)]
        for cp in cps:
            cp.start()
        for cp in cps:
            cp.wait()

    return pl.pallas_call(
        body, name="grad_sibling_exchange",
        in_specs=[ANY] * n, out_specs=[ANY] * n,
        out_shape=[jax.ShapeDtypeStruct(a.shape[1:], F32) for a in arrs],
        scratch_shapes=[pltpu.SemaphoreType.DMA((n,)), pltpu.SemaphoreType.DMA((n,))],
    )(*arrs)


def grad_chip_sum(g, sb, cc, tr, name):
    _, _, R, C = g.shape
    blk = pl.BlockSpec((1, tr, C), lambda s, r, c: (s, r, 0))
    grid_spec = pltpu.PrefetchScalarGridSpec(
        num_scalar_prefetch=1, grid=(SHARDS, R // tr),
        in_specs=[pl.BlockSpec((1, 1, tr, C), lambda s, r, c: (c[0], s, r, 0)), blk],
        out_specs=[blk, blk])

    def body(c_ref, a_ref, b_ref, o_ref, ob_ref):
        t = a_ref[0] + b_ref[...]
        o_ref[...] = t
        ob_ref[...] = t.astype(BF16)

    return pl.pallas_call(
        body, name=name, grid_spec=grid_spec,
        out_shape=[jax.ShapeDtypeStruct((SHARDS, R, C), F32), jax.ShapeDtypeStruct((SHARDS, R, C), BF16)],
        compiler_params=_params(("arbitrary", "arbitrary")),
    )(cc, g, sb)


def grad_shard_sum(t, rb, me, tr, name):
    _, R, C = t.shape
    grid_spec = pltpu.PrefetchScalarGridSpec(
        num_scalar_prefetch=1, grid=(R // tr,),
        in_specs=[pl.BlockSpec((1, tr, C), lambda r, m: (m[0], r, 0)),
                  pl.BlockSpec((SHARDS, tr, C), lambda r, m: (0, r, 0))],
        out_specs=pl.BlockSpec((tr, C), lambda r, m: (r, 0)))

    def body(m_ref, t_ref, r_ref, o_ref):
        part = [jnp.where(m_ref[0] == s, t_ref[0], r_ref[s].astype(F32)) for s in range(SHARDS)]
        o_ref[...] = ((part[0] + part[1]) + part[2]) + part[3]

    return pl.pallas_call(
        body, name=name, grid_spec=grid_spec,
        out_shape=jax.ShapeDtypeStruct((R, C), F32),
        compiler_params=_params(("arbitrary",)),
    )(me, t, rb)


def grad_sibling_share(arrs, name):
    n = len(arrs)

    def body(*refs):
        srcs, outs, ssem, rsem = refs[:n], refs[n:2 * n], refs[2 * n], refs[2 * n + 1]
        x, y, c, _ = _place()
        cps = [_rcopy(srcs[i], outs[i], ssem.at[i], rsem.at[i], (x, y, 1 - c)) for i in range(n)]
        for cp in cps:
            cp.start()
        for cp in cps:
            cp.wait()

    return pl.pallas_call(
        body, name=name,
        in_specs=[ANY] * n, out_specs=[ANY] * n,
        out_shape=[jax.ShapeDtypeStruct(a.shape, F32) for a in arrs],
        scratch_shapes=[pltpu.SemaphoreType.DMA((n,)), pltpu.SemaphoreType.DMA((n,))],
    )(*arrs)


def _allreduce_rows(src, sib_buf, chips, out_ref, ssem, rsem):
    x, y, c, others = _place()
    me = 2 * x + y
    cp = _rcopy(src, sib_buf, ssem.at[0], rsem.at[0], (x, y, 1 - c))
    cp.start()
    cp.wait()
    chips[me] = src[...] + sib_buf[...]
    sends = [_rcopy(chips.at[me], chips.at[me], ssem.at[1 + k], rsem.at[1 + k], (ox, oy, c))
             for k, (ox, oy) in enumerate(others)]
    for s in sends:
        s.start()
    for k, (ox, oy) in enumerate(others):
        slot = chips.at[2 * ox + oy]
        _rcopy(slot, slot, ssem.at[1 + k], rsem.at[1 + k], (ox, oy, c)).wait_recv()
    for s in sends:
        s.wait_send()
    out_ref[...] = ((chips[0] + chips[1]) + chips[2]) + chips[3]


def _allreduce_scratch(rows):
    return [pltpu.VMEM((rows, 128), F32), pltpu.VMEM((SHARDS, rows, 128), F32),
            pltpu.SemaphoreType.DMA((4,)), pltpu.SemaphoreType.DMA((4,))]


def allreduce_rows(buf, name):
    rows = buf.shape[0]
    VM = pl.BlockSpec(memory_space=pltpu.VMEM)

    def body(src_ref, out_ref, sib_buf, chips, ssem, rsem):
        _allreduce_rows(src_ref, sib_buf, chips, out_ref, ssem, rsem)

    return pl.pallas_call(
        body, name=name, in_specs=[VM], out_specs=VM,
        out_shape=jax.ShapeDtypeStruct((rows, 128), F32),
        scratch_shapes=_allreduce_scratch(rows), compiler_params=_params(),
    )(buf)


def small_allreduce(grads, rel, loss_part):
    rows = _small_rows()
    keys = [(l, name) for l in range(DEPTH) for name in SMALL]
    flat = [grads[l][SMALL[name][0]] for l, name in keys] + [rel, loss_part]

    def body(*refs):
        ins = refs[:len(flat)]
        out_ref, src, sib_buf, chips, ssem, rsem = refs[len(flat):]
        src[...] = jnp.zeros_like(src)
        for (l, name), ref in zip(keys, ins):
            r0 = rows[(l, name)]
            if name == "sg_w":
                for grp in range(8):
                    src[r0 + grp * 128:r0 + (grp + 1) * 128, :] = ref[grp]
            elif name == "sg_b":
                src[r0:r0 + 8, :] = ref[...].T[0:8, :]
            else:
                for j in range(ref.shape[1] // 128):
                    src[r0 + j:r0 + j + 1, :] = ref[:, j * 128:(j + 1) * 128]
        src[REL_ROW:REL_ROW + 32, 0:16] = ins[-2][...]
        src[LOSS_ROW:LOSS_ROW + 1, :] = ins[-1][...]
        _allreduce_rows(src, sib_buf, chips, out_ref, ssem, rsem)

    return pl.pallas_call(
        body, name="small_allreduce",
        out_shape=jax.ShapeDtypeStruct((SMALL_ROWS, 128), F32),
        scratch_shapes=[pltpu.VMEM((SMALL_ROWS, 128), F32)] + _allreduce_scratch(SMALL_ROWS),
        compiler_params=_params(),
    )(*flat)


def _pad_lanes(v):
    return jnp.zeros((1, 128), F32).at[0, :v.shape[0]].set(v)


def layer_fwd(x, wts, bias):
    wt = wts["wt"]
    tn = {name: t for name, _, t in GROUPS}
    p_gate, h = inproj_first(x, wts["g_pre"], wt["gate"], tn["gate"], "inproj_gate")
    p_sgu, p_att, p_ssd = (inproj_group(h, wt[n], tn[n], "inproj_" + n, F32 if n == "att" else BF16)
                           for n in ("sgu", "att", "ssd"))
    y_att = att_fwd(p_att, bias, wts["sinks"])
    y_sg = sgu_fwd(p_sgu, wts["ln_g"], wts["ln_b"], wts["sg_w"], wts["sg_bt"])
    y_ssm, hst = ssd_fwd(p_ssd, wts["conv_w"], wts["conv_b"], wts["dt_bias"], wts["a_log"], wts["d_skip"],
                         wts["norm_g"])
    x_new, br_a, br_s, br_m, merged, out = merge_fwd(
        y_att, y_sg, y_ssm, p_gate, x, wts["w_a"], wts["w_s"], wts["w_m"], wts["w_o"], wts["g_post"])
    saved = dict(x=x, p_gate=p_gate, p_sgu=p_sgu, p_att=p_att, p_ssd=p_ssd, h=h,
                 y_att=y_att, y_sg=y_sg, y_ssm=y_ssm, hst=hst,
                 br_a=br_a, br_s=br_s, br_m=br_m, merged=merged, out=out)
    return x_new, saved


def layer_bwd(dy, wts, bias, sv):
    dps, grads = layer_bwd_params(dy, wts, bias, sv)
    dx, grads["g_pre"] = layer_bwd_input(dy, dps, wts, sv, wts["g_pre"])
    return dx, grads


def layer_bwd_input(dy, dps, wts, sv, g_pre):
    wt = wts["wt"]
    tn = {name: t for name, _, t in GROUPS}
    acc = None
    for n in ("gate", "sgu", "ssd"):
        acc = dh_group(dps[n], wt[n], acc, DH_TILE[n], "dh_" + n)
    return dh_last(dps["att"], wt["att"], acc, sv["x"], g_pre, dy, tn["att"], "dh_att")


def layer_bwd_params(dy, wts, bias, sv):
    dout, dba, dbs, dbm, d_gate, dya, dys, dym, dg_post = merge_bwd(
        dy, sv["out"], wts["g_post"], sv["p_gate"], sv["br_a"], sv["br_s"], sv["br_m"],
        wts["w_a"], wts["w_s"], wts["w_m"], wts["w_o"])
    d_att, dbias, dsinks = att_bwd(dya, sv["p_att"], bias, wts["sinks"])
    d_sgu, dsg_w, dsg_bt, dln_g, dln_b = sgu_bwd(dys, sv["p_sgu"], wts["ln_g"], wts["ln_b"], wts["sg_w"],
                                                 wts["sg_bt"])
    d_ssd, dcw, dcb, ddtb, dalog, ddsk, dng = ssd_bwd(
        dym, sv["p_ssd"], sv["hst"], wts["conv_w"], wts["conv_b"], wts["dt_bias"], wts["a_log"], wts["d_skip"],
        wts["norm_g"])
    dps = dict(gate=d_gate, sgu=d_sgu, att=d_att, ssd=d_ssd)
    tn = {name: t for name, _, t in GROUPS}
    grads = dict(
        w_in={n: dw_group(dps[n], sv["h"], tn[n], "dw_in_" + n) for n in dps},
        w_a=matmul_tn(sv["y_att"], dba, "dw_att"),
        w_s=matmul_tn(sv["y_sg"], dbs, "dw_sg"),
        w_m=matmul_tn(sv["y_ssm"], dbm, "dw_ssm"),
        w_o=matmul_tn(sv["merged"], dout, "dw_out"),
        g_post=dg_post, sinks=dsinks, ln_g=dln_g, ln_b=dln_b, sg_w=dsg_w, sg_bt=dsg_bt,
        conv_w=dcw, conv_b=dcb, dt_bias=ddtb, a_log=dalog, d_skip=ddsk, norm_g=dng, bias=dbias)
    return dps, grads


REST_OFF = (0, 256, 512, 1024, 1280)
GR_ROWS = 1536
GR_CONV = 1280
W_IN_SPLIT = 1600
W_IN_HALF = 1824


def kernel(x, w_in, norm_pre, norm_post, rel_bias, att_sinks, sg_ln_g, sg_ln_b, sg_w, sg_b, ssm_conv_w, ssm_conv_b, ssm_dt_bias, ssm_a_log, ssm_d, ssm_norm_g, w_br_att, w_br_sg, w_br_ssm, w_out, loss_target, m_w_in, m_norm_pre, m_norm_post, m_rel_bias, m_att_sinks, m_sg_ln_g, m_sg_ln_b, m_sg_w, m_sg_b, m_ssm_conv_w, m_ssm_conv_b, m_ssm_dt_bias, m_ssm_a_log, m_ssm_d, m_ssm_norm_g, m_w_br_att, m_w_br_sg, m_w_br_ssm, m_w_out, v_w_in, v_norm_pre, v_norm_post, v_rel_bias, v_att_sinks, v_sg_ln_g, v_sg_ln_b, v_sg_w, v_sg_b, v_ssm_conv_w, v_ssm_conv_b, v_ssm_dt_bias, v_ssm_a_log, v_ssm_d, v_ssm_norm_g, v_w_br_att, v_w_br_sg, v_w_br_ssm, v_w_out):
    cx, cy, cc = lax.axis_index("x"), lax.axis_index("y"), lax.axis_index("c")
    me = 2 * cx + cy
    xs = x[0]
    S = xs.shape[0]

    tr = lambda a: jnp.transpose(a, (0, 2, 1))
    w_in_b = tr(w_in).astype(BF16)
    w_rest_b = jnp.concatenate([w_br_att, w_br_sg, w_br_ssm, w_out], axis=1).astype(BF16)
    halves = lambda a: a.reshape(2, a.shape[0] // 2, a.shape[1])
    w_in0 = jnp.pad(w_in_b[0], ((0, W_IN_ROWS - 3400), (0, 0)))
    all0_in, all0_rest = gather_weights([halves(w_in0), halves(w_rest_b[0])])
    convw_slot = jnp.zeros((SHARDS, DEPTH * CONV_K * 768 // 128, 128), F32)
    convw_slot = lax.dynamic_update_index_in_dim(
        convw_slot, jnp.where(cc == 0, 1.0, 0.0) * ssm_conv_w.reshape(-1, 128), me, 0)
    convw_rows = allreduce_rows(convw_slot.reshape(-1, 128), "gather_conv_w")
    convw_all = convw_rows.reshape(SHARDS, DEPTH, CONV_K, 768).transpose(1, 2, 0, 3).reshape(DEPTH, CONV_K, CONV_C)
    g1_ssem, g1_rsem, g1_srcs, g1_lands, g1_token = gather_start(
        [w_in_b[1], w_rest_b[1]], [convw_rows, all0_rest], "gather_l1_start")

    o = REST_OFF

    def layer_weights(l, gathered_in, gathered_rest, g_pre):
        sh_in = [jnp.where(me == s, w_in_b[l], gathered_in[s]) for s in range(SHARDS)]
        sh_rest = [jnp.where(me == s, w_rest_b[l], gathered_rest[s]) for s in range(SHARDS)]
        rest = lambda k: jnp.concatenate([r[o[k]:o[k + 1]] for r in sh_rest], axis=0)
        return dict(
            wt=group_weights(jnp.concatenate(sh_in, axis=0)),
            w_a=rest(0), w_s=rest(1), w_m=rest(2), w_o=rest(3),
            g_pre=g_pre, g_post=norm_post[l][None], sinks=att_sinks[l],
            ln_g=sg_ln_g[l][None], ln_b=sg_ln_b[l][None], sg_w=sg_w[l],
            sg_bt=sg_b[l].T,
            conv_w=jnp.concatenate([convw_all[l], jnp.zeros((4, CONV_C), F32)], axis=0),
            conv_b=ssm_conv_b[l][None], dt_bias=_pad_lanes(ssm_dt_bias[l]), a_log=_pad_lanes(ssm_a_log[l]),
            d_skip=_pad_lanes(ssm_d[l]), norm_g=ssm_norm_g[l][None])

    bias = bias_table(rel_bias)
    layers = [layer_weights(0, [all0_in[:, s].reshape(W_IN_ROWS, D)[0:3400] for s in range(SHARDS)],
                            [all0_rest[:, s].reshape(1280, D) for s in range(SHARDS)],
                            (norm_pre[0] + g1_token[0, 0])[None])]
    act, sv0 = layer_fwd(xs, layers[0], bias)
    land_in, land_rest = gather_wait(g1_ssem, g1_rsem, g1_srcs, g1_lands, act, "gather_l1_wait")
    layers.append(layer_weights(1, land_in, land_rest, norm_pre[1][None]))
    act, sv1 = layer_fwd(act, layers[1], bias)
    saved = [sv0, sv1]
    dy, loss_part = loss_head(act, loss_target[0])
    cvec = jnp.reshape(cc, (1,)).astype(jnp.int32)
    mvec = jnp.reshape(me, (1,)).astype(jnp.int32)

    def by_shard(g):
        gcw = g["conv_w"][0:CONV_K].reshape(CONV_K, SHARDS, 768).transpose(1, 0, 2).reshape(SHARDS, 3, 1024)
        rest = jnp.concatenate([
            g["w_a"].reshape(SHARDS, 256, D), g["w_s"].reshape(SHARDS, 256, D), g["w_o"].reshape(SHARDS, 256, D),
            g["w_m"].reshape(SHARDS, 512, D), jnp.pad(gcw, ((0, 0), (0, GR_ROWS - GR_CONV - 3), (0, 0)))], axis=1)
        return ungroup_grads(g["w_in"]).reshape(SHARDS, 3400, D), rest

    grads = [None] * DEPTH
    dy, grads[1] = layer_bwd(dy, layers[1], bias, saved[1])
    g1_in, g1_rest = by_shard(grads[1])
    g1_in = jnp.pad(g1_in, ((0, 0), (0, W_IN_ROWS - 3400), (0, 0)))
    x1_ssem, x1_rsem, x1_srcs, x1_lands, x1_token = gather_start(
        [g1_in.astype(BF16), g1_rest.astype(BF16)], [], "grads_l1_start", by_dest=True)
    wts0 = dict(layers[0], g_post=layers[0]["g_post"] + x1_token[0, 0])
    dps0, grads[0] = layer_bwd_params(dy, wts0, bias, saved[0])
    r1_in, r1_rest = gather_wait(x1_ssem, x1_rsem, x1_srcs, x1_lands, grads[0]["w_in"]["ssd"], "grads_l1_wait",
                                 by_dest=True)
    p_in = grad_shard_sum(g1_in, r1_in, mvec, 384, "l1_sum_w_in")
    p_rest = grad_shard_sum(g1_rest, r1_rest, mvec, 512, "l1_sum_rest")
    pb_in, pb_rest = grad_sibling_share([p_in, p_rest], "l1_sibling_share")

    g0_in, g0_rest = by_shard(grads[0])
    pad_to = lambda a, rows: jnp.pad(a, ((0, 0), (0, rows - a.shape[1]), (0, 0)))
    g0_in = jnp.stack([pad_to(g0_in[:, 0:W_IN_SPLIT], W_IN_HALF), pad_to(g0_in[:, W_IN_SPLIT:3400], W_IN_HALF)])
    g0_rest = jnp.stack([g0_rest[:, 0:GR_ROWS // 2], g0_rest[:, GR_ROWS // 2:GR_ROWS]])
    sb_in, sb_rest = grad_sibling_exchange([g0_in, g0_rest])
    t_in, t_in_b = grad_chip_sum(g0_in, sb_in, cvec, 608, "chip_sum_w_in")
    t_rest, t_rest_b = grad_chip_sum(g0_rest, sb_rest, cvec, 384, "chip_sum_rest")
    x0_ssem, x0_rsem, x0_srcs, x0_lands, x0_token = gather_start([t_in_b, t_rest_b], [], "grads_l0_start", by_dest=True)
    dy, grads[0]["g_pre"] = layer_bwd_input(dy, dps0, layers[0], saved[0], layers[0]["g_pre"] + x0_token[0, 0])
    grad_x = dy[None]
    rb_in, rb_rest = gather_wait(x0_ssem, x0_rsem, x0_srcs, x0_lands, dy, "grads_l0_wait", by_dest=True)
    grad_rel_local = bias_grad(grads[0]["bias"] + grads[1]["bias"])
    f_in = grad_shard_sum(t_in, rb_in, mvec, 608, "shard_sum_w_in")
    f_rest = grad_shard_sum(t_rest, rb_rest, mvec, 384, "shard_sum_rest")
    fb_in, fb_rest = grad_sibling_share([f_in, f_rest], "l0_sibling_share")

    red = small_allreduce(grads, grad_rel_local, loss_part + 0.0 * f_rest[0:1, 0:128])
    loss = red[LOSS_ROW, 0]

    res = adamw_small(red, (rel_bias, m_rel_bias, v_rel_bias), dict(
        norm_pre=(norm_pre, m_norm_pre, v_norm_pre), norm_post=(norm_post, m_norm_post, v_norm_post),
        att_sinks=(att_sinks, m_att_sinks, v_att_sinks), sg_ln_g=(sg_ln_g, m_sg_ln_g, v_sg_ln_g),
        sg_ln_b=(sg_ln_b, m_sg_ln_b, v_sg_ln_b), sg_w=(sg_w, m_sg_w, v_sg_w), sg_b=(sg_b, m_sg_b, v_sg_b),
        ssm_conv_b=(ssm_conv_b, m_ssm_conv_b, v_ssm_conv_b), ssm_dt_bias=(ssm_dt_bias, m_ssm_dt_bias, v_ssm_dt_bias),
        ssm_a_log=(ssm_a_log, m_ssm_a_log, v_ssm_a_log), ssm_d=(ssm_d, m_ssm_d, v_ssm_d),
        ssm_norm_g=(ssm_norm_g, m_ssm_norm_g, v_ssm_norm_g)))
    res["w_in"] = tuple(tr(a) for a in adamw_big(
        tr(w_in), tr(m_w_in), tr(v_w_in), (f_in, fb_in, W_IN_SPLIT // 200, 0, 0), (p_in, pb_in, 0), cvec, "adamw_w_in", 200))
    rest_upd = lambda w, m, v, name, n0, off0, off1: adamw_big(
        w, m, v, (f_rest, fb_rest, n0, off0, off0), (p_rest, pb_rest, off1), cvec, name, 256)
    res["w_br_att"] = rest_upd(w_br_att, m_w_br_att, v_w_br_att, "adamw_w_br_att", 1, 0, 0)
    res["w_br_sg"] = rest_upd(w_br_sg, m_w_br_sg, v_w_br_sg, "adamw_w_br_sg", 1, 1, 1)
    res["w_out"] = rest_upd(w_out, m_w_out, v_w_out, "adamw_w_out", 1, 2, 2)
    res["w_br_ssm"] = rest_upd(w_br_ssm, m_w_br_ssm, v_w_br_ssm, "adamw_w_br_ssm", 0, 0, 3)
    cw0 = jnp.where(cc == 1, f_rest, fb_rest)[GR_CONV - GR_ROWS // 2:GR_CONV - GR_ROWS // 2 + 3]
    cw1 = (p_rest + pb_rest)[GR_CONV:GR_CONV + 3]
    g_conv_w = jnp.stack([cw0.reshape(CONV_K, 768), cw1.reshape(CONV_K, 768)])
    res["ssm_conv_w"] = (g_conv_w,) + tuple(adamw_plain(ssm_conv_w, g_conv_w, m_ssm_conv_w, v_ssm_conv_w, "adamw_conv_w"))

    order = ["w_in", "norm_pre", "norm_post", "rel_bias", "att_sinks", "sg_ln_g", "sg_ln_b", "sg_w", "sg_b",
             "ssm_conv_w", "ssm_conv_b", "ssm_dt_bias", "ssm_a_log", "ssm_d", "ssm_norm_g",
             "w_br_att", "w_br_sg", "w_br_ssm", "w_out"]
    return (loss, grad_x, *[res[n][0] for n in order], *[res[n][1] for n in order],
            *[res[n][2] for n in order], *[res[n][3] for n in order])
```

```python
import functools
import math

import numpy as np
import jax
import jax.numpy as jnp
from jax import lax
from jax.experimental import pallas as pl
from jax.experimental.pallas import tpu as pltpu

F32 = jnp.float32
BF16 = jnp.bfloat16
MESH = pl.DeviceIdType.MESH

D = 1024
DEPTH = 2
EPS = 1e-6
L = 128
HEADS = 16
KV = 2
DH = 64
SSM_W = 2048
SSM_H = 32
SSM_P = 64
SSM_G = 4
SSM_N = 128
CONV_K = 4
CONV_C = 3072
NEG = -1e30
IN_COLS = 13600

GROUPS = (("gate", 3072, 1536), ("sgu", 3072, 1536), ("att", 2304, 2304), ("ssd", 5376, 1792))
W_IN_ROWS = 3456
DH_TILE = {"gate": 3072, "sgu": 3072, "ssd": 2688}

ADAM_LR = 0.001
ADAM_B1 = 0.9
ADAM_B2 = 0.999
ADAM_EPS = 1e-08
ADAM_WD = 0.01
ADAM_STEP = 10

VMEM_LIMIT = 56 * 1024 * 1024

SHARDS = 4


def _dot(a, b):
    return jnp.dot(a, b, preferred_element_type=F32)


def _dot_nt(a, b):
    return lax.dot_general(a, b, (((1,), (1,)), ((), ())), preferred_element_type=F32)


def _dot_tn(a_f32, b):
    return jnp.dot(a_f32.T.astype(BF16), b, preferred_element_type=F32)


def _dot_t(a, b):
    return lax.dot_general(a, b, (((0,), (0,)), ((), ())), preferred_element_type=F32)


def _dot_hi(a, b):
    return jnp.dot(a, b, preferred_element_type=F32, precision=lax.Precision.HIGHEST)


def _pieces(x, n):
    out = []
    for _ in range(n - 1):
        p = x.astype(BF16)
        out.append(p)
        x = x - p.astype(F32)
    out.append(x.astype(BF16))
    return out


def _dot_sel(a, sel, n):
    sel = sel.astype(BF16)
    acc = None
    for p in _pieces(a, n):
        t = _dot(p, sel)
        acc = t if acc is None else acc + t
    return acc


def _sel_dot(sel, b, n):
    sel = sel.astype(BF16)
    acc = None
    for p in _pieces(b, n):
        t = _dot(sel, p)
        acc = t if acc is None else acc + t
    return acc


def _sigmoid(x):
    return 1.0 / (1.0 + jnp.exp(-x))


def _softplus(x):
    return jnp.maximum(x, 0.0) + jnp.log(1.0 + jnp.exp(-jnp.abs(x)))


def _params(sem=None, vmem=VMEM_LIMIT):
    kw = dict(vmem_limit_bytes=vmem)
    if sem is not None:
        kw["dimension_semantics"] = sem
    return pltpu.CompilerParams(**kw)


def _full(shape):
    nd = len(shape)
    return pl.BlockSpec(shape, lambda *_: (0,) * nd)


def group_weights(wt):
    return dict(
        gate=wt[10528:13600],
        sgu=wt[2304:5376],
        att=jnp.concatenate([wt[0:1024], wt[1280:2304], wt[1024:1280]], axis=0),
        ssd=jnp.concatenate([wt[5376:10496], wt[10496:10528], jnp.zeros((224, D), wt.dtype)], axis=0))


def ungroup_grads(g):
    a, s = g["att"], g["ssd"]
    return jnp.concatenate([a[0:1024], a[2048:2304], a[1024:2048], g["sgu"], s[0:5152], g["gate"]], axis=0)


def _bucket_table():
    qi = np.arange(L)[:, None]
    kj = np.arange(2 * L)[None, :]
    dist = np.maximum(qi + L - kj, 0)
    dist_f = np.maximum(dist, 1).astype(np.float32)
    large = 16 + (np.log(dist_f / np.float32(16)) / np.float32(math.log(128 / 16)) * np.float32(16)).astype(np.int32)
    large = np.minimum(large, 31)
    return np.where(dist < 16, dist, large).astype(np.int32)


def bias_table(rel_bias):
    buckets = jnp.asarray(_bucket_table().reshape(1, L * 2 * L))

    def body(rb_ref, bk_ref, out_ref):
        onehot = (lax.broadcasted_iota(jnp.int32, (32, L * 2 * L), 0) == bk_ref[...]).astype(F32)
        out_ref[...] = lax.dot_general(rb_ref[...], onehot, (((0,), (0,)), ((), ())),
                                       preferred_element_type=F32, precision=lax.Precision.HIGHEST)

    out = pl.pallas_call(
        body, name="bias_table",
        out_shape=jax.ShapeDtypeStruct((HEADS, L * 2 * L), F32),
        compiler_params=_params(),
    )(rel_bias, buckets)
    out = out.reshape(HEADS, L, 2 * L)
    win = _window_mask()
    first = win & (np.arange(2 * L)[None, :] >= L)
    return jnp.stack([jnp.where(first, out, NEG), jnp.where(win, out, NEG)])


def _window_mask():
    dist = np.arange(L)[:, None] + L - np.arange(2 * L)[None, :]
    return (dist >= 0) & (dist < L)


def bias_grad(dbias):
    buckets = jnp.asarray(_bucket_table().reshape(1, L * 2 * L))

    def body(db_ref, bk_ref, out_ref):
        onehot = (lax.broadcasted_iota(jnp.int32, (32, L * 2 * L), 0) == bk_ref[...]).astype(F32)
        out_ref[...] = lax.dot_general(onehot, db_ref[...], (((1,), (1,)), ((), ())),
                                       preferred_element_type=F32, precision=lax.Precision.HIGHEST)

    return pl.pallas_call(
        body, name="bias_grad",
        out_shape=jax.ShapeDtypeStruct((32, HEADS), F32),
        compiler_params=_params(),
    )(dbias.reshape(HEADS, L * 2 * L), buckets)


def _row_tile(S):
    return 1024 if S % 1024 == 0 else 512


def inproj_first(x, g_pre, wt, tn, name):
    S, W = x.shape[0], wt.shape[0]
    tm = _row_tile(S)

    def body(x_ref, g_ref, w_ref, o_ref, h_ref):
        @pl.when(pl.program_id(1) == 0)
        def _():
            xv = x_ref[...]
            r = lax.rsqrt(jnp.mean(xv * xv, axis=-1, keepdims=True) + EPS)
            h_ref[...] = (xv * r * g_ref[...]).astype(BF16)
        o_ref[...] = _dot_nt(h_ref[...], w_ref[...]).astype(BF16)

    return pl.pallas_call(
        body, name=name, grid=(S // tm, W // tn),
        in_specs=[pl.BlockSpec((tm, D), lambda i, j: (i, 0)), _full((1, D)),
                  pl.BlockSpec((tn, D), lambda i, j: (j, 0))],
        out_specs=[pl.BlockSpec((tm, tn), lambda i, j: (i, j)), pl.BlockSpec((tm, D), lambda i, j: (i, 0))],
        out_shape=[jax.ShapeDtypeStruct((S, W), BF16), jax.ShapeDtypeStruct((S, D), BF16)],
        compiler_params=_params(("arbitrary", "arbitrary")),
    )(x, g_pre, wt)


def inproj_group(h, wt, tn, name, dtype):
    S, W = h.shape[0], wt.shape[0]
    tm = _row_tile(S)

    def body(h_ref, w_ref, o_ref):
        o_ref[...] = _dot_nt(h_ref[...], w_ref[...]).astype(dtype)

    return pl.pallas_call(
        body, name=name, grid=(S // tm, W // tn),
        in_specs=[pl.BlockSpec((tm, D), lambda i, j: (i, 0)), pl.BlockSpec((tn, D), lambda i, j: (j, 0))],
        out_specs=pl.BlockSpec((tm, tn), lambda i, j: (i, j)),
        out_shape=jax.ShapeDtypeStruct((S, W), dtype),
        compiler_params=_params(("arbitrary", "arbitrary")),
    )(h, wt)


def dh_group(dp, wt, acc, tk, name):
    S, W = dp.shape
    tm = _row_tile(S)

    def body(*refs):
        dp_ref, w_ref, o_ref = refs[0], refs[1], refs[-1]
        first = pl.program_id(1) == 0
        if acc is None:
            @pl.when(first)
            def _():
                o_ref[...] = jnp.zeros_like(o_ref)
        else:
            @pl.when(first)
            def _():
                o_ref[...] = refs[2][...]
        o_ref[...] += _dot(dp_ref[...], w_ref[...])

    row = pl.BlockSpec((tm, D), lambda i, k: (i, 0))
    return pl.pallas_call(
        body, name=name, grid=(S // tm, W // tk),
        in_specs=[pl.BlockSpec((tm, tk), lambda i, k: (i, k)), pl.BlockSpec((tk, D), lambda i, k: (k, 0))]
        + ([] if acc is None else [row]),
        out_specs=row, out_shape=jax.ShapeDtypeStruct((S, D), F32),
        input_output_aliases={} if acc is None else {2: 0},
        compiler_params=_params(("arbitrary", "arbitrary")),
    )(*((dp, wt) if acc is None else (dp, wt, acc)))


def dh_last(dp, wt, acc_in, x, g_pre, dy, tk, name):
    S, W = dp.shape
    tm = 512
    nk = W // tk

    def body(dp_ref, w_ref, a_ref, x_ref, g_ref, dy_ref, dx_ref, dg_ref, acc):
        i, k = pl.program_id(0), pl.program_id(1)

        @pl.when(k == 0)
        def _():
            acc[...] = a_ref[...]

        acc[...] += _dot(dp_ref[...], w_ref[...])

        @pl.when((k == nk - 1) & (i == 0))
        def _():
            dg_ref[...] = jnp.zeros_like(dg_ref)

        @pl.when(k == nk - 1)
        def _():
            xv = x_ref[...]
            dh = acc[...]
            g = g_ref[...]
            r = lax.rsqrt(jnp.mean(xv * xv, axis=-1, keepdims=True) + EPS)
            dhg = dh * g
            dx_ref[...] = dy_ref[...] + r * dhg - xv * (r * r * r) * jnp.mean(dhg * xv, axis=-1, keepdims=True)
            dg_ref[...] += jnp.sum(dh * xv * r, axis=0, keepdims=True)

    row = pl.BlockSpec((tm, D), lambda i, k: (i, 0))
    return pl.pallas_call(
        body, name=name, grid=(S // tm, nk),
        in_specs=[pl.BlockSpec((tm, tk), lambda i, k: (i, k)), pl.BlockSpec((tk, D), lambda i, k: (k, 0)),
                  row, row, _full((1, D)), row],
        out_specs=[row, _full((1, D))],
        out_shape=[jax.ShapeDtypeStruct((S, D), F32), jax.ShapeDtypeStruct((1, D), F32)],
        scratch_shapes=[pltpu.VMEM((tm, D), F32)],
        compiler_params=_params(("arbitrary", "arbitrary")),
    )(dp, wt, acc_in, x, g_pre, dy)


def dw_group(dp, h, tn, name):
    S, W = dp.shape
    ts = _row_tile(S)

    def body(dp_ref, h_ref, o_ref):
        @pl.when(pl.program_id(1) == 0)
        def _():
            o_ref[...] = jnp.zeros_like(o_ref)
        o_ref[...] += _dot_t(dp_ref[...], h_ref[...])

    return pl.pallas_call(
        body, name=name, grid=(W // tn, S // ts),
        in_specs=[pl.BlockSpec((ts, tn), lambda j, s: (s, j)), pl.BlockSpec((ts, D), lambda j, s: (s, 0))],
        out_specs=pl.BlockSpec((tn, D), lambda j, s: (j, 0)),
        out_shape=jax.ShapeDtypeStruct((W, D), F32),
        compiler_params=_params(("arbitrary", "arbitrary")),
    )(dp, h)


def matmul_tn(a, b, name, tn=1024):
    S, K = a.shape
    N = b.shape[1]
    ts = _row_tile(S)
    ns = S // ts

    def body(a_ref, b_ref, o_ref):
        @pl.when(pl.program_id(1) == 0)
        def _():
            o_ref[...] = jnp.zeros_like(o_ref)
        o_ref[...] += _dot_t(a_ref[...], b_ref[...])

    return pl.pallas_call(
        body, name=name, grid=(N // tn, ns),
        in_specs=[pl.BlockSpec((ts, K), lambda j, s: (s, 0)), pl.BlockSpec((ts, tn), lambda j, s: (s, j))],
        out_specs=pl.BlockSpec((K, tn), lambda j, s: (0, j)),
        out_shape=jax.ShapeDtypeStruct((K, N), F32),
        compiler_params=_params(("arbitrary", "arbitrary")),
    )(a, b)


def _att_in_specs(nb):
    last = nb - 1
    cur = lambda n: jnp.minimum(n, last)
    prev = lambda n: jnp.maximum(jnp.minimum(n, last) - 1, 0)
    return [
        pl.BlockSpec((L, 1024), lambda n: (cur(n), 0)),
        pl.BlockSpec((L, 128), lambda n: (prev(n), 16)),
        pl.BlockSpec((L, 128), lambda n: (cur(n), 16)),
        pl.BlockSpec((L, 128), lambda n: (prev(n), 17)),
        pl.BlockSpec((L, 128), lambda n: (cur(n), 17)),
        pl.BlockSpec((L, 1024), lambda n: (cur(n), 1)),
        _full((2, HEADS, L, 2 * L)),
        pl.BlockSpec(memory_space=pltpu.SMEM),
    ]


GH = HEADS // KV
GB = 8


def _stack_heads(ref, h0, nh, scr):
    for g in range(nh):
        scr[(h0 + g) * L:(h0 + g + 1) * L, :] = ref[:, (h0 + g) * DH:(h0 + g + 1) * DH].astype(F32)
    return scr[h0 * L:(h0 + nh) * L, :]


def _unstack_heads(val, h0, nh, ref):
    for g in range(nh):
        ref[:, (h0 + g) * DH:(h0 + g + 1) * DH] = val[g * L:(g + 1) * L, :]


def _sink_rows(s_ref, h0, nh):
    return jnp.concatenate([jnp.full((L, 1), s_ref[h0 + g], F32) for g in range(nh)], axis=0)


def _att_probs(qh, kk, bias_h, sk):
    logits = _dot_nt(qh, kk) + bias_h
    m =jnp.maximum(jnp.max(logits, axis=-1, keepdims=True), sk)
    p = jnp.exp(logits - m)
    es = jnp.exp(sk - m)
    den = jnp.sum(p, axis=-1, keepdims=True) + es
    return p / den, es / den


def att_fwd(proj, bias, sinks):
    S = proj.shape[0]
    nb = S // L

    def body(q_ref, kp_ref, kc_ref, vp_ref, vc_ref, z_ref, bias_ref, s_ref, y_ref, o_scr):
        table = jnp.where(pl.program_id(0) > 0, 1, 0)
        for kv in range(KV):
            sl = slice(kv * DH, (kv + 1) * DH)
            kk = jnp.concatenate([kp_ref[:, sl], kc_ref[:, sl]], axis=0).astype(BF16)
            vv = jnp.concatenate([vp_ref[:, sl], vc_ref[:, sl]], axis=0).astype(BF16)
            for g in range(GH):
                h = kv * GH + g
                hs = slice(h * DH, (h + 1) * DH)
                qh = (q_ref[:, hs] * 0.125).astype(BF16)
                P, _ = _att_probs(qh, kk, bias_ref[table, h], s_ref[h])
                o_scr[:, hs] = _dot(P.astype(BF16), vv)
        z = z_ref[...].astype(F32)
        y_ref[...] = (o_scr[...] * (z * _sigmoid(z))).astype(BF16)

    return pl.pallas_call(
        body, name="att_fwd", grid=(nb,),
        in_specs=_att_in_specs(nb),
        out_specs=pl.BlockSpec((L, 1024), lambda n: (n, 0)),
        out_shape=jax.ShapeDtypeStruct((S, 1024), BF16),
        scratch_shapes=[pltpu.VMEM((L, 1024), F32)],
        compiler_params=_params(("arbitrary",)),
    )(proj, proj, proj, proj, proj, proj, bias, sinks)


def att_bwd(dy, proj, bias, sinks):
    S = proj.shape[0]
    nb = S // L
    last = nb - 1

    def body(dy_ref, q_ref, kp_ref, kc_ref, vp_ref, vc_ref, z_ref, bias_ref, s_ref,
             dout_ref, dbias_ref, dsink_ref, carry, band, dq_scr, dz_scr, qs_scr, zs_scr, dys_scr):
        n = pl.program_id(0)

        @pl.when(n == 0)
        def _():
            carry[...] = jnp.zeros_like(carry)
            dq_scr[...] = jnp.zeros_like(dq_scr)
            dz_scr[...] = jnp.zeros_like(dz_scr)
            dbias_ref[...] = jnp.zeros_like(dbias_ref)
            dsink_ref[...] = jnp.zeros_like(dsink_ref)

        dout_ref[:, 0:1024] = dq_scr[...].astype(BF16)
        dout_ref[:, 1024:2048] = dz_scr[...].astype(BF16)
        band[...] = jnp.zeros_like(band)

        @pl.when(n < nb)
        def _():
            table = jnp.where(n > 0, 1, 0)
            lane = lax.broadcasted_iota(jnp.int32, (1, 128), 1)
            dsink = jnp.zeros((1, 128), F32)
            for kv in range(KV):
                sl = slice(kv * DH, (kv + 1) * DH)
                kk = jnp.concatenate([kp_ref[:, sl], kc_ref[:, sl]], axis=0).astype(BF16)
                vv = jnp.concatenate([vp_ref[:, sl], vc_ref[:, sl]], axis=0).astype(BF16)
                dk_acc = jnp.zeros((2 * L, DH), F32)
                dv_acc = jnp.zeros((2 * L, DH), F32)
                for h0 in range(kv * GH, (kv + 1) * GH, GB):
                    qs = (_stack_heads(q_ref, h0, GB, qs_scr) * 0.125).astype(BF16)
                    bias_g = bias_ref[table, h0:h0 + GB].reshape(GB * L, 2 * L)
                    P, psink = _att_probs(qs, kk, bias_g, _sink_rows(s_ref, h0, GB))
                    zs = _stack_heads(z_ref, h0, GB, zs_scr)
                    dys = _stack_heads(dy_ref, h0, GB, dys_scr)
                    sg = _sigmoid(zs)
                    O = _dot(P.astype(BF16), vv)
                    _unstack_heads(dys * O * (sg * (1.0 + zs * (1.0 - sg))), h0, GB, dz_scr)
                    dOb = (dys * (zs * sg)).astype(BF16)
                    dP = _dot_nt(dOb, vv)
                    delta = jnp.sum(P * dP, axis=-1, keepdims=True)
                    dS = P * (dP - delta)
                    sd = psink * delta
                    for g in range(GB):
                        dsink = dsink + jnp.where(lane == h0 + g, -jnp.sum(sd[g * L:(g + 1) * L, :]), 0.0)
                    _unstack_heads(_dot(dS.astype(BF16), kk) * 0.125, h0, GB, dq_scr)
                    dbias_ref[h0:h0 + GB] += dS.reshape(GB, L, 2 * L)
                    dk_acc = dk_acc + _dot_tn(dS, qs)
                    dv_acc = dv_acc + _dot_tn(P, dOb)
                band[:, sl] = dk_acc
                band[:, 128 + kv * DH:128 + (kv + 1) * DH] = dv_acc
            dsink_ref[...] += dsink

        out = carry[...] + band[0:L, :]
        dout_ref[:, 2048:2304] = out.astype(BF16)
        carry[...] = band[L:2 * L, :]

    cur = lambda n: jnp.minimum(n, last)
    lag = lambda n: jnp.maximum(n - 1, 0)
    return pl.pallas_call(
        body, name="att_bwd", grid=(nb + 1,),
        in_specs=[pl.BlockSpec((L, 1024), lambda n: (cur(n), 0))] + _att_in_specs(nb),
        out_specs=[pl.BlockSpec((L, 2304), lambda n: (lag(n), 0)), _full((HEADS, L, 2 * L)), _full((1, 128))],
        out_shape=[jax.ShapeDtypeStruct((S, 2304), BF16),
                   jax.ShapeDtypeStruct((HEADS, L, 2 * L), F32), jax.ShapeDtypeStruct((1, 128), F32)],
        scratch_shapes=[pltpu.VMEM((L, 256), F32), pltpu.VMEM((2 * L, 256), F32),
                        pltpu.VMEM((L, 1024), F32), pltpu.VMEM((L, 1024), F32)]
        + [pltpu.VMEM((HEADS * L, DH), F32)] * 3,
        compiler_params=_params(("arbitrary",)),
    )(dy, proj, proj, proj, proj, proj, proj, bias, sinks)


def _sgu_in_specs():
    return [
        pl.BlockSpec((L, 1024), lambda c: (c, 0)),
        pl.BlockSpec((L, 1024), lambda c: (c, 1)),
        pl.BlockSpec((L, 1024), lambda c: (c, 2)),
        _full((1, 1024)), _full((1, 1024)), _full((8, L, L)), _full((L, 8)),
    ]


def _sgu_norm(v, lg, lb):
    mu = jnp.mean(v, axis=-1, keepdims=True)
    vc = v - mu
    rstd = lax.rsqrt(jnp.mean(vc * vc, axis=-1, keepdims=True) + EPS)
    xhat = vc * rstd
    return xhat * lg + lb, xhat, rstd


def _tril():
    return lax.broadcasted_iota(jnp.int32, (L, L), 0) >= lax.broadcasted_iota(jnp.int32, (L, L), 1)


def sgu_fwd(proj, ln_g, ln_b, w, b_t):
    S = proj.shape[0]

    def body(u_ref, v_ref, z_ref, lg_ref, lb_ref, w_ref, bt_ref, y_ref):
        vn, _, _ = _sgu_norm(v_ref[...].astype(F32), lg_ref[...], lb_ref[...])
        tri = _tril()
        parts = []
        for g in range(8):
            wg = jnp.where(tri, w_ref[g], 0.0).astype(BF16)
            parts.append(_dot(wg, vn[:, g * 128:(g + 1) * 128].astype(BF16)) + bt_ref[:, g:g + 1])
        mixed = jnp.concatenate(parts, axis=1)
        z = z_ref[...].astype(F32)
        y_ref[...] = (u_ref[...].astype(F32) * mixed * (z * _sigmoid(z))).astype(BF16)

    return pl.pallas_call(
        body, name="sgu_fwd", grid=(S // L,),
        in_specs=_sgu_in_specs(),
        out_specs=pl.BlockSpec((L, 1024), lambda c: (c, 0)),
        out_shape=jax.ShapeDtypeStruct((S, 1024), BF16),
        compiler_params=_params(("arbitrary",)),
    )(proj, proj, proj, ln_g, ln_b, w, b_t)


def sgu_bwd(dy, proj, ln_g, ln_b, w, b_t):
    S = proj.shape[0]

    def body(dy_ref, u_ref, v_ref, z_ref, lg_ref, lb_ref, w_ref, bt_ref,
             dout_ref, dw_ref, dbt_ref, dlg_ref, dlb_ref):
        @pl.when(pl.program_id(0) == 0)
        def _():
            dw_ref[...] = jnp.zeros_like(dw_ref)
            dbt_ref[...] = jnp.zeros_like(dbt_ref)
            dlg_ref[...] = jnp.zeros_like(dlg_ref)
            dlb_ref[...] = jnp.zeros_like(dlb_ref)

        lg = lg_ref[...]
        vn, xhat, rstd = _sgu_norm(v_ref[...].astype(F32), lg, lb_ref[...])
        tri = _tril()
        lane = lax.broadcasted_iota(jnp.int32, (L, 128), 1)
        wgs, parts = [], []
        for g in range(8):
            wg = jnp.where(tri, w_ref[g], 0.0)
            wgs.append(wg)
            parts.append(_dot(wg.astype(BF16), vn[:, g * 128:(g + 1) * 128].astype(BF16)) + bt_ref[:, g:g + 1])
        mixed = jnp.concatenate(parts, axis=1)
        z = z_ref[...].astype(F32)
        sg = _sigmoid(z)
        silu = z * sg
        dy_v = dy_ref[...]
        u = u_ref[...].astype(F32)
        dout_ref[:, 0:1024] = (dy_v * mixed * silu).astype(BF16)
        dout_ref[:, 2048:3072] = (dy_v * u * mixed * (sg * (1.0 + z * (1.0 - sg)))).astype(BF16)
        dmixed = dy_v * u * silu
        dbt = jnp.zeros((L, 128), F32)
        dvn_parts = []
        for g in range(8):
            dm = dmixed[:, g * 128:(g + 1) * 128]
            dmb = dm.astype(BF16)
            dbt = dbt + jnp.where(lane == g, jnp.sum(dm, axis=1, keepdims=True), 0.0)
            dw_ref[g] += jnp.where(tri, _dot_nt(dmb, vn[:, g * 128:(g + 1) * 128].astype(BF16)), 0.0)
            dvn_parts.append(_dot_tn(wgs[g], dmb))
        dbt_ref[...] += dbt
        dvn = jnp.concatenate(dvn_parts, axis=1)
        dlg_ref[...] += jnp.sum(dvn * xhat, axis=0, keepdims=True)
        dlb_ref[...] += jnp.sum(dvn, axis=0, keepdims=True)
        dxh = dvn * lg
        dv = rstd * (dxh - jnp.mean(dxh, axis=-1, keepdims=True)
                     - xhat * jnp.mean(dxh * xhat, axis=-1, keepdims=True))
        dout_ref[:, 1024:2048] = dv.astype(BF16)

    return pl.pallas_call(
        body, name="sgu_bwd", grid=(S // L,),
        in_specs=[pl.BlockSpec((L, 1024), lambda c: (c, 0))] + _sgu_in_specs(),
        out_specs=[pl.BlockSpec((L, 3072), lambda c: (c, 0)), _full((8, L, L)), _full((L, 128)),
                   _full((1, 1024)), _full((1, 1024))],
        out_shape=[jax.ShapeDtypeStruct((S, 3072), BF16), jax.ShapeDtypeStruct((8, L, L), F32),
                   jax.ShapeDtypeStruct((L, 128), F32), jax.ShapeDtypeStruct((1, 1024), F32),
                   jax.ShapeDtypeStruct((1, 1024), F32)],
        compiler_params=_params(("arbitrary",)),
    )(dy, proj, proj, proj, ln_g, ln_b, w, b_t)


def _expand_matrices():
    e = (np.arange(SSM_W)[None, :] // SSM_P == np.arange(128)[:, None]).astype(np.float32)
    return jnp.asarray(e, BF16), jnp.asarray(e.T, BF16)


def _rows_from(ref, start):
    C = ref.shape[1]
    tiles = ref[...].reshape(17, 8, C)
    q, s = divmod(start, 8)
    if s == 0:
        return tiles[q:q + 16].reshape(L, C)
    rolled = pltpu.roll(tiles, 8 - s, axis=1)
    sub = lax.broadcasted_iota(jnp.int32, (16, 8, C), 1)
    return jnp.where(sub < 8 - s, rolled[q:q + 16], rolled[q + 1:q + 17]).reshape(L, C)


def _ssd_common(ext_ref, cw_ref, cb_ref, dt_raw, dtb, alog):
    taps = [_rows_from(ext_ref, 5 + k) for k in range(CONV_K)]
    pre = cb_ref[...]
    for k in range(CONV_K):
        pre = pre + cw_ref[k:k + 1, :] * taps[k]
    sg_pre = _sigmoid(pre)
    xc = pre * sg_pre
    dt = _softplus(dt_raw + dtb)
    a = -jnp.exp(alog)
    adt = dt * a
    acs = _sel_dot(_tril(), adt, 3)
    return pre, sg_pre, xc, dt, a, acs, taps


def _ssd_in_specs(rev, nc):
    cidx = (lambda c: nc - 1 - c) if rev else (lambda c: c)
    return [
        pl.BlockSpec((L, 2048), lambda c: (cidx(c), 0)),
        pl.BlockSpec((L, 1024), lambda c: (cidx(c), 2)),
        pl.BlockSpec((L, 1024), lambda c: (cidx(c), 3)),
        pl.BlockSpec((L, 1024), lambda c: (cidx(c), 4)),
        pl.BlockSpec((L, 128), lambda c: (cidx(c), 40)),
        _full((8, CONV_C)), _full((1, CONV_C)), _full((1, 128)), _full((1, 128)), _full((1, 128)),
        _full((1, SSM_W)), _full((128, SSM_W)), _full((SSM_W, 128)),
    ]


def ssd_fwd(proj, conv_w, conv_b, dt_bias, a_log, d_skip, norm_g):
    S = proj.shape[0]
    nc = S // L

    def body(z_ref, xa_ref, xb_ref, xc_ref, dt_ref, cw_ref, cb_ref, dtb_ref, alog_ref, dsk_ref, ng_ref,
             ex_ref, ext_ref, y_ref, hs_ref, H, ext, ysc):
        @pl.when(pl.program_id(0) == 0)
        def _():
            H[...] = jnp.zeros_like(H)
            ext[0:8, :] = jnp.zeros((8, CONV_C), F32)

        for k, ref in enumerate((xa_ref, xb_ref, xc_ref)):
            ext[8:8 + L, k * 1024:(k + 1) * 1024] = ref[...].astype(F32)
        pre, sg_pre, xc, dt, a, acs, _ = _ssd_common(ext, cw_ref, cb_ref, dt_ref[...].astype(F32), dtb_ref[...],
                                                     alog_ref[...])
        for k, ref in enumerate((xa_ref, xb_ref, xc_ref)):
            ext[0:8, k * 1024:(k + 1) * 1024] = ref[L - 8:L, :].astype(F32)
        xs = xc[:, 0:SSM_W]
        acs_t = acs.T
        ex = ex_ref[...]
        dt_x = _dot_sel(dt, ex, 2)
        xdt = xs * dt_x
        eacs_x = _dot_sel(jnp.exp(acs), ex, 2)
        xw = xdt * _dot_sel(jnp.exp(acs[L - 1:L, :] - acs), ex, 2)
        cd_row = jnp.exp(acs[L - 1:L, :])
        hs_ref[0] = H[...]
        tri = _tril()
        for g in range(SSM_G):
            gs = slice(g * 512, (g + 1) * 512)
            bg = xc[:, SSM_W + g * SSM_N:SSM_W + (g + 1) * SSM_N].astype(BF16)
            cg = xc[:, SSM_W + 512 + g * SSM_N:SSM_W + 512 + (g + 1) * SSM_N].astype(BF16)
            G = _dot_nt(cg, bg)
            yoff = _dot_nt(cg, H[gs, :].astype(BF16)) * eacs_x[:, gs]
            Sg = _dot_tn(xw[:, gs], bg)
            for j in range(8):
                hh = g * 8 + j
                hs = slice(hh * SSM_P, (hh + 1) * SSM_P)
                seg = acs[:, hh:hh + 1] - acs_t[hh:hh + 1, :]
                dk = jnp.where(tri, jnp.exp(seg), 0.0)
                yd = _dot((G * dk).astype(BF16), xdt[:, hs].astype(BF16))
                ysc[:, hs] = yd + yoff[:, j * SSM_P:(j + 1) * SSM_P]
                H[hs, :] = H[hs, :] * cd_row[:, hh:hh + 1] + Sg[j * SSM_P:(j + 1) * SSM_P, :]
        d_x = _dot_sel(jnp.broadcast_to(dsk_ref[...], (8, 128)), ex, 3)[0:1, :]
        Y = ysc[...] + d_x * xs
        z = z_ref[...].astype(F32)
        yz = Y * (z * _sigmoid(z))
        ng = ng_ref[...]
        for g in range(SSM_G):
            gs = slice(g * 512, (g + 1) * 512)
            t = yz[:, gs]
            rstd = lax.rsqrt(jnp.mean(t * t, axis=-1, keepdims=True) + EPS)
            y_ref[:, gs] = (t * rstd * ng[:, gs]).astype(BF16)

    return pl.pallas_call(
        body, name="ssd_fwd", grid=(nc,),
        in_specs=_ssd_in_specs(False, nc),
        out_specs=[pl.BlockSpec((L, SSM_W), lambda c: (c, 0)), pl.BlockSpec((1, SSM_W, SSM_N), lambda c: (c, 0, 0))],
        out_shape=[jax.ShapeDtypeStruct((S, SSM_W), BF16), jax.ShapeDtypeStruct((nc, SSM_W, SSM_N), F32)],
        scratch_shapes=[pltpu.VMEM((SSM_W, SSM_N), F32), pltpu.VMEM((8 + L, CONV_C), F32),
                        pltpu.VMEM((L, SSM_W), F32)],
        compiler_params=_params(("arbitrary",)),
    )(proj, proj, proj, proj, proj, conv_w, conv_b, dt_bias, a_log, d_skip, norm_g, *_expand_matrices())


def ssd_bwd(dy, proj, hstates, conv_w, conv_b, dt_bias, a_log, d_skip, norm_g):
    S = proj.shape[0]
    nc = S // L
    cidx = lambda c: nc - 1 - c

    def body(dy_ref, z_ref, xa_ref, xb_ref, xc_ref, dt_ref, cw_ref, cb_ref, dtb_ref, alog_ref, dsk_ref, ng_ref,
             ex_ref, ext_ref, pa_ref, pb_ref, pc_ref, hp_ref,
             dout_ref, dcw_ref, dcb_ref, ddtb_ref, dalog_ref, ddsk_ref, dng_ref,
             dH, ext, dext, ysc, yoffsc, dxdt, dxc, tsc, rsum, csum):
        step = pl.program_id(0)
        c = nc - 1 - step

        @pl.when(step == 0)
        def _():
            dH[...] = jnp.zeros_like(dH)
            dext[L:L + 8, :] = jnp.zeros((8, CONV_C), F32)
            rsum[...] = jnp.zeros_like(rsum)
            csum[...] = jnp.zeros_like(csum)
            for r in (dcw_ref, dcb_ref, ddtb_ref, dalog_ref, ddsk_ref, dng_ref):
                r[...] = jnp.zeros_like(r)

        for k, (ref, prev) in enumerate(((xa_ref, pa_ref), (xb_ref, pb_ref), (xc_ref, pc_ref))):
            ext[0:8, k * 1024:(k + 1) * 1024] = jnp.where(c > 0, prev[8:16, :].astype(F32), 0.0)
            ext[8:8 + L, k * 1024:(k + 1) * 1024] = ref[...].astype(F32)
        dtb = dtb_ref[...]
        dt_raw = dt_ref[...].astype(F32)
        pre, sg_pre, xc, dt, a, acs, taps = _ssd_common(ext, cw_ref, cb_ref, dt_raw, dtb, alog_ref[...])
        xs = xc[:, 0:SSM_W]
        acs_t = acs.T
        ex = ex_ref[...]
        dt_x = _dot_sel(dt, ex, 2)
        xdt = xs * dt_x
        eacs_x = _dot_sel(jnp.exp(acs), ex, 2)
        dte_x = _dot_sel(jnp.exp(acs[L - 1:L, :] - acs), ex, 2)
        xw = xdt * dte_x
        cd_row = jnp.exp(acs[L - 1:L, :])
        tri = _tril()

        Gs, Cs, Bs = [], [], []
        for g in range(SSM_G):
            gs = slice(g * 512, (g + 1) * 512)
            bg = xc[:, SSM_W + g * SSM_N:SSM_W + (g + 1) * SSM_N].astype(BF16)
            cg = xc[:, SSM_W + 512 + g * SSM_N:SSM_W + 512 + (g + 1) * SSM_N].astype(BF16)
            G = _dot_nt(cg, bg)
            Gs.append(G), Cs.append(cg), Bs.append(bg)
            yoffsc[:, gs] = _dot_nt(cg, hp_ref[0, gs, :].astype(BF16)) * eacs_x[:, gs]
            for j in range(8):
                hh = g * 8 + j
                hs = slice(hh * SSM_P, (hh + 1) * SSM_P)
                seg = acs[:, hh:hh + 1] - acs_t[hh:hh + 1, :]
                dk = jnp.where(tri, jnp.exp(seg), 0.0)
                ysc[:, hs] = _dot((G * dk).astype(BF16), xdt[:, hs].astype(BF16))
        d_x = _dot_sel(jnp.broadcast_to(dsk_ref[...], (8, 128)), ex, 3)[0:1, :]
        yoff = yoffsc[...]
        Y = ysc[...] + yoff + d_x * xs

        z = z_ref[...].astype(F32)
        sgz = _sigmoid(z)
        silu_z = z * sgz
        yz = Y * silu_z
        ng = ng_ref[...]
        dout = dy_ref[...]
        dyn = dout * ng
        dyz_parts, dng_parts = [], []
        for g in range(SSM_G):
            gs = slice(g * 512, (g + 1) * 512)
            t = yz[:, gs]
            rstd = lax.rsqrt(jnp.mean(t * t, axis=-1, keepdims=True) + EPS)
            dng_parts.append(jnp.sum(dout[:, gs] * t * rstd, axis=0, keepdims=True))
            dn = dyn[:, gs]
            dyz_parts.append(rstd * dn - t * (rstd * rstd * rstd) * jnp.mean(dn * t, axis=-1, keepdims=True))
        dng_ref[...] += jnp.concatenate(dng_parts, axis=1)
        dyz = jnp.concatenate(dyz_parts, axis=1)
        dY = dyz * silu_z
        dout_ref[:, 0:SSM_W] = (dyz * Y * (sgz * (1.0 + z * (1.0 - sgz)))).astype(BF16)

        ex_t = ext_ref[...]
        ddsk_ref[...] += _dot_sel(jnp.broadcast_to(jnp.sum(dY * xs, axis=0, keepdims=True), (8, SSM_W)), ex_t, 3)[0:1, :]

        lane = lax.broadcasted_iota(jnp.int32, (L, 128), 1)
        last_col = lax.broadcasted_iota(jnp.int32, (1, L), 1) == L - 1
        for g in range(SSM_G):
            gs = slice(g * 512, (g + 1) * 512)
            G, cg, bg = Gs[g], Cs[g], Bs[g]
            hp_g = hp_ref[0, gs, :]
            dh_g = dH[gs, :]
            dY_g = dY[:, gs]
            dZ = dY_g * eacs_x[:, gs]
            dZb = dZ.astype(BF16)
            dC = _dot(dZb, hp_g.astype(BF16))
            dh_from_off = _dot_tn(dZ, cg)
            dhb = dh_g.astype(BF16)
            Q = _dot_nt(bg, dhb)
            dB = _dot(xw[:, gs].astype(BF16), dhb)
            qd = Q * dte_x[:, gs]
            dxdt[:, gs] = qd
            tsc[:, gs] = qd * xdt[:, gs]
            dG = jnp.zeros((L, L), F32)
            for j in range(8):
                hh = g * 8 + j
                hs = slice(hh * SSM_P, (hh + 1) * SSM_P)
                seg = acs[:, hh:hh + 1] - acs_t[hh:hh + 1, :]
                dk = jnp.where(tri, jnp.exp(seg), 0.0)
                M = G * dk
                dYh = dY[:, hs]
                dYhb = dYh.astype(BF16)
                dM = _dot_nt(dYhb, xdt[:, hs].astype(BF16))
                dxdt[:, hs] += _dot_tn(M, dYhb)
                dG = dG + dM * dk
                Wm = dM * M
                pj = slice(j * SSM_P, (j + 1) * SSM_P)
                cd_h = cd_row[:, hh:hh + 1]
                dcd = jnp.sum(dh_g[pj, :] * hp_g[pj, :]) * cd_h
                rsum[:, hh:hh + 1] = jnp.sum(Wm, axis=1, keepdims=True)
                csum[hh:hh + 1, :] = jnp.sum(Wm, axis=0, keepdims=True) - jnp.where(last_col, dcd, 0.0)
                dH[hs, :] = dh_g[pj, :] * cd_h + dh_from_off[pj, :]
            dGb = dG.astype(BF16)
            dC = dC + _dot(dGb, bg)
            dB = dB + _dot_tn(dG, cg)
            dxc[:, SSM_W + g * SSM_N:SSM_W + (g + 1) * SSM_N] = dB
            dxc[:, SSM_W + 512 + g * SSM_N:SSM_W + 512 + (g + 1) * SSM_N] = dC

        row = lax.broadcasted_iota(jnp.int32, (L, 128), 0)
        tv = tsc[...]
        t_last = _dot_sel(jnp.broadcast_to(jnp.sum(tv, axis=0, keepdims=True), (8, SSM_W)), ex_t, 3)[0:1, :]
        dacs = (rsum[...] - csum[...].T + _dot_sel(dY * yoff - tv, ex_t, 2) + jnp.where(row == L - 1, t_last, 0.0))
        triu = lax.broadcasted_iota(jnp.int32, (L, L), 0) <= lax.broadcasted_iota(jnp.int32, (L, L), 1)
        dadt = _sel_dot(triu, dacs, 3)
        dxdt_v = dxdt[...]
        ddt = _dot_sel(dxdt_v * xs, ex_t, 2) + dadt * a
        dalog_ref[...] += jnp.sum(dadt * dt * a, axis=0, keepdims=True)
        ddt_raw = jnp.where(lane < SSM_H, ddt * _sigmoid(dt_raw + dtb), 0.0)
        ddtb_ref[...] += jnp.sum(ddt_raw, axis=0, keepdims=True)
        dout_ref[:, 5120:5248] = ddt_raw.astype(BF16)
        dout_ref[:, 5248:5376] = jnp.zeros((L, 128), BF16)

        dxc[:, 0:SSM_W] = dxdt_v * dt_x + d_x * dY
        dpre = dxc[...] * (sg_pre * (1.0 + pre * (1.0 - sg_pre)))
        dcb_ref[...] += jnp.sum(dpre, axis=0, keepdims=True)
        dext[0:L, :] = dpre
        x_cur = ext[8:8 + L, :]
        dx = None
        for k in range(CONV_K):
            dsh = _rows_from(dext, 3 - k)
            term = cw_ref[k:k + 1, :] * dsh
            dx = term if dx is None else dx + term
            dcw_ref[k:k + 1, :] += jnp.sum(dsh * x_cur, axis=0, keepdims=True)
        dout_ref[:, SSM_W:SSM_W + CONV_C] = dx.astype(BF16)
        dext[L:L + 8, :] = dpre[0:8, :]

    big = lambda w: pl.BlockSpec((L, w), lambda c: (cidx(c), 0))
    return pl.pallas_call(
        body, name="ssd_bwd", grid=(nc,),
        in_specs=[big(SSM_W)] + _ssd_in_specs(True, nc) + [
            pl.BlockSpec((16, 1024), lambda c, k=k: (jnp.maximum(8 * cidx(c) - 1, 0), k)) for k in (2, 3, 4)] + [
            pl.BlockSpec((1, SSM_W, SSM_N), lambda c: (cidx(c), 0, 0))],
        out_specs=[big(5376), _full((8, CONV_C)), _full((1, CONV_C)),
                   _full((1, 128)), _full((1, 128)), _full((1, 128)), _full((1, SSM_W))],
        out_shape=[jax.ShapeDtypeStruct((S, 5376), BF16), jax.ShapeDtypeStruct((8, CONV_C), F32),
                   jax.ShapeDtypeStruct((1, CONV_C), F32), jax.ShapeDtypeStruct((1, 128), F32),
                   jax.ShapeDtypeStruct((1, 128), F32), jax.ShapeDtypeStruct((1, 128), F32),
                   jax.ShapeDtypeStruct((1, SSM_W), F32)],
        scratch_shapes=[pltpu.VMEM((SSM_W, SSM_N), F32), pltpu.VMEM((8 + L, CONV_C), F32),
                        pltpu.VMEM((L + 8, CONV_C), F32), pltpu.VMEM((L, SSM_W), F32),
                        pltpu.VMEM((L, SSM_W), F32), pltpu.VMEM((L, SSM_W), F32),
                        pltpu.VMEM((L, CONV_C), F32), pltpu.VMEM((L, SSM_W), F32),
                        pltpu.VMEM((L, 128), F32), pltpu.VMEM((128, L), F32)],
        compiler_params=_params(("arbitrary",)),
    )(dy, proj, proj, proj, proj, proj, conv_w, conv_b, dt_bias, a_log, d_skip, norm_g, *_expand_matrices(),
      proj, proj, proj, hstates)


def _resident(shape):
    nd = len(shape)
    return pl.BlockSpec(shape, lambda *_: (0,) * nd, pipeline_mode=pl.Buffered(1))


def merge_fwd(y_att, y_sg, y_ssm, proj, x, w_a, w_s, w_m, w_o, g_post):
    S = x.shape[0]
    tm = 256

    def body(ya_ref, ys_ref, ym_ref, gate_ref, x_ref, wa_ref, ws_ref, wm_ref, wo_ref, gp_ref,
             xn_ref, bra_ref, brs_ref, brm_ref, mg_ref, out_ref):
        bra = _dot(ya_ref[...], wa_ref[...])
        brs = _dot(ys_ref[...], ws_ref[...])
        brm = _dot(ym_ref[...], wm_ref[...])
        bra_ref[...] = bra.astype(BF16)
        brs_ref[...] = brs.astype(BF16)
        brm_ref[...] = brm.astype(BF16)
        gate = gate_ref[...].astype(F32)
        merged = (_sigmoid(gate[:, 0:1024]) * bra + _sigmoid(gate[:, 1024:2048]) * brs
                  + _sigmoid(gate[:, 2048:3072]) * brm)
        mb = merged.astype(BF16)
        mg_ref[...] = mb
        o = _dot(mb, wo_ref[...])
        out_ref[...] = o
        r = lax.rsqrt(jnp.mean(o * o, axis=-1, keepdims=True) + EPS)
        xn_ref[...] = x_ref[...] + o * r * gp_ref[...]

    row = lambda w: pl.BlockSpec((tm, w), lambda i: (i, 0))
    return pl.pallas_call(
        body, name="merge_fwd", grid=(S // tm,),
        in_specs=[row(1024), row(1024), row(2048), pl.BlockSpec((tm, 3072), lambda i: (i, 0)),
                  row(D), _resident((1024, D)), _resident((1024, D)), _resident((2048, D)), _resident((D, D)),
                  _full((1, D))],
        out_specs=[row(D)] * 6,
        out_shape=[jax.ShapeDtypeStruct((S, D), F32)] + [jax.ShapeDtypeStruct((S, D), BF16)] * 4
        + [jax.ShapeDtypeStruct((S, D), F32)],
        compiler_params=_params(("arbitrary",)),
    )(y_att, y_sg, y_ssm, proj, x, w_a, w_s, w_m, w_o, g_post)


def merge_bwd(dy, out, g_post, proj, br_a, br_s, br_m, w_a, w_s, w_m, w_o):
    S = dy.shape[0]
    tm = 256

    def body(dy_ref, o_ref, gp_ref, gate_ref, bra_ref, brs_ref, brm_ref, wa_ref, ws_ref, wm_ref, wo_ref,
             dout_ref, dba_ref, dbs_ref, dbm_ref, dgate_ref, dya_ref, dys_ref, dym_ref, dgp_ref):
        @pl.when(pl.program_id(0) == 0)
        def _():
            dgp_ref[...] = jnp.zeros_like(dgp_ref)

        o = o_ref[...]
        dyv = dy_ref[...]
        r = lax.rsqrt(jnp.mean(o * o, axis=-1, keepdims=True) + EPS)
        dyg = dyv * gp_ref[...]
        do = r * dyg - o * (r * r * r) * jnp.mean(dyg * o, axis=-1, keepdims=True)
        dgp_ref[...] += jnp.sum(dyv * o * r, axis=0, keepdims=True)
        dob = do.astype(BF16)
        dout_ref[...] = dob
        dmerged = _dot_nt(dob, wo_ref[...])
        for idx, (br_ref, dbr_ref, w_ref, dyi_ref) in enumerate((
                (bra_ref, dba_ref, wa_ref, dya_ref), (brs_ref, dbs_ref, ws_ref, dys_ref),
                (brm_ref, dbm_ref, wm_ref, dym_ref))):
            s = _sigmoid(gate_ref[:, idx * 1024:(idx + 1) * 1024].astype(F32))
            dbr = (dmerged * s).astype(BF16)
            dbr_ref[...] = dbr
            dgate_ref[:, idx * 1024:(idx + 1) * 1024] = (dmerged * br_ref[...].astype(F32) * s * (1.0 - s)).astype(BF16)
            dyi_ref[...] = _dot_nt(dbr, w_ref[...])

    row = lambda w: pl.BlockSpec((tm, w), lambda i: (i, 0))
    return pl.pallas_call(
        body, name="merge_bwd", grid=(S // tm,),
        in_specs=[row(D), row(D), _full((1, D)), pl.BlockSpec((tm, 3072), lambda i: (i, 0)),
                  row(D), row(D), row(D),
                  _resident((1024, D)), _resident((1024, D)), _resident((2048, D)), _resident((D, D))],
        out_specs=[row(D), row(D), row(D), row(D), row(3072), row(1024), row(1024), row(2048), _full((1, D))],
        out_shape=[jax.ShapeDtypeStruct((S, D), BF16)] * 4 + [
            jax.ShapeDtypeStruct((S, 3072), BF16), jax.ShapeDtypeStruct((S, 1024), F32),
            jax.ShapeDtypeStruct((S, 1024), F32), jax.ShapeDtypeStruct((S, 2048), F32),
            jax.ShapeDtypeStruct((1, D), F32)],
        compiler_params=_params(("arbitrary",)),
    )(dy, out, g_post, proj, br_a, br_s, br_m, w_a, w_s, w_m, w_o)


def loss_head(y, target):
    S = y.shape[0]
    tm = 512

    def body(y_ref, t_ref, dy_ref, loss_ref):
        @pl.when(pl.program_id(0) == 0)
        def _():
            loss_ref[...] = jnp.zeros_like(loss_ref)
        e = y_ref[...] - t_ref[...]
        dy_ref[...] = e * (1.0 / D)
        loss_ref[...] += 0.5 * jnp.sum(jnp.mean(e * e, axis=-1, keepdims=True))

    row = pl.BlockSpec((tm, D), lambda i: (i, 0))
    return pl.pallas_call(
        body, name="loss_head", grid=(S // tm,),
        in_specs=[row, row], out_specs=[row, _full((1, 128))],
        out_shape=[jax.ShapeDtypeStruct((S, D), F32), jax.ShapeDtypeStruct((1, 128), F32)],
        compiler_params=_params(("arbitrary",)),
    )(y, target)


def _adam(w, g, m, v):
    mn = ADAM_B1 * m + (1.0 - ADAM_B1) * g
    vn = ADAM_B2 * v + (1.0 - ADAM_B2) * (g * g)
    m_hat = mn / (1.0 - ADAM_B1 ** ADAM_STEP)
    v_hat = vn / (1.0 - ADAM_B2 ** ADAM_STEP)
    return -ADAM_LR * (m_hat / (jnp.sqrt(v_hat) + ADAM_EPS) + ADAM_WD * w), mn, vn


def adamw_big(w, m, v, halves0, sum1, cc, name, tr):
    _, R, C = w.shape
    nper = R // tr
    f, fb, n0, off_a, off_b = halves0
    p, pb, off1 = sum1

    def body(c_ref, w_ref, m_ref, v_ref, f_ref, fb_ref, p_ref, pb_ref, g_ref, d_ref, nm_ref, nv_ref):
        i = pl.program_id(0)
        half = jnp.where(i % nper >= n0, 1, 0)
        g0 = jnp.where(c_ref[0] == half, f_ref[...], fb_ref[...])
        g = jnp.where(i < nper, g0, p_ref[...] + pb_ref[...])
        g_ref[0] = g
        d_ref[0], nm_ref[0], nv_ref[0] = _adam(w_ref[0], g, m_ref[0], v_ref[0])

    def blk0(i, c):
        il = jnp.minimum(i, nper - 1)
        return (jnp.where(il >= n0, off_b + il - n0, off_a + il), 0)

    wblk = pl.BlockSpec((1, tr, C), lambda i, c: (i // nper, i % nper, 0))
    b0 = pl.BlockSpec((tr, C), blk0)
    b1 = pl.BlockSpec((tr, C), lambda i, c: (off1 + jnp.maximum(i - nper, 0), 0))
    grid_spec = pltpu.PrefetchScalarGridSpec(
        num_scalar_prefetch=1, grid=(2 * nper,),
        in_specs=[wblk, wblk, wblk, b0, b0, b1, b1], out_specs=[wblk] * 4)
    return pl.pallas_call(
        body, name=name, grid_spec=grid_spec,
        out_shape=[jax.ShapeDtypeStruct(w.shape, F32)] * 4,
        compiler_params=_params(("arbitrary",)),
    )(cc, w, m, v, f, fb, p, pb)


def adamw_plain(w, g, m, v, name):
    def body(w_ref, g_ref, m_ref, v_ref, d_ref, nm_ref, nv_ref):
        d_ref[...], nm_ref[...], nv_ref[...] = _adam(w_ref[...], g_ref[...], m_ref[...], v_ref[...])

    return pl.pallas_call(
        body, name=name, out_shape=[jax.ShapeDtypeStruct(w.shape, F32)] * 3, compiler_params=_params(),
    )(w, g, m, v)


SMALL = {"norm_pre": ("g_pre", 8), "norm_post": ("g_post", 8), "att_sinks": ("sinks", 8), "sg_ln_g": ("ln_g", 8),
         "sg_ln_b": ("ln_b", 8), "sg_w": ("sg_w", 1024), "sg_b": ("sg_bt", 8), "ssm_conv_b": ("conv_b", 24),
         "ssm_dt_bias": ("dt_bias", 8), "ssm_a_log": ("a_log", 8), "ssm_d": ("d_skip", 8), "ssm_norm_g": ("norm_g", 16)}
SMALL_LAYER_ROWS = sum(r for _, r in SMALL.values())
REL_ROW = DEPTH * SMALL_LAYER_ROWS
LOSS_ROW = REL_ROW + 32
SMALL_ROWS = LOSS_ROW + 8


def _small_rows():
    rows, r = {}, 0
    for l in range(DEPTH):
        for name, (_, n) in SMALL.items():
            rows[(l, name)] = r
            r += n
    return rows


def adamw_small(red, rel, small):
    names = list(SMALL) + ["rel_bias"]
    params = dict(small, rel_bias=rel)
    rows = _small_rows()

    def grad_of(red_ref, l, name, n):
        r0 = rows[(l, name)]
        if name == "sg_b":
            return red_ref[r0:r0 + 8, :]
        if n < 128:
            return red_ref[r0:r0 + 1, 0:n]
        return jnp.concatenate([red_ref[r0 + j:r0 + j + 1, :] for j in range(n // 128)], axis=1)

    def body(red_ref, *refs):
        ins, outs = refs[:3 * len(names)], refs[3 * len(names):]
        for i, name in enumerate(names):
            w_ref, m_ref, v_ref = ins[3 * i:3 * i + 3]
            o = outs[4 * i:4 * i + 4]
            if name == "rel_bias":
                g = red_ref[REL_ROW:REL_ROW + 32, 0:16]
                o[0][...] = g
                o[1][...], o[2][...], o[3][...] = _adam(w_ref[...], g, m_ref[...], v_ref[...])
                continue
            for l in range(DEPTH):
                if name == "sg_w":
                    for grp in range(8):
                        r0 = rows[(l, name)] + grp * 128
                        g = red_ref[r0:r0 + 128, :]
                        o[0][l, grp] = g
                        o[1][l, grp], o[2][l, grp], o[3][l, grp] = _adam(w_ref[l, grp], g, m_ref[l, grp], v_ref[l, grp])
                elif name == "sg_b":
                    g = grad_of(red_ref, l, name, 128)
                    o[0][l] = g
                    o[1][l], o[2][l], o[3][l] = _adam(w_ref[l], g, m_ref[l], v_ref[l])
                else:
                    sl = slice(l, l + 1)
                    g = grad_of(red_ref, l, name, w_ref.shape[-1])
                    o[0][sl, :] = g
                    o[1][sl, :], o[2][sl, :], o[3][sl, :] = _adam(w_ref[sl, :], g, m_ref[sl, :], v_ref[sl, :])

    flat_in = [a for name in names for a in params[name]]
    out_shape = [jax.ShapeDtypeStruct(params[name][0].shape, F32) for name in names for _ in range(4)]
    res = pl.pallas_call(body, name="adamw_small", out_shape=out_shape, compiler_params=_params())(red, *flat_in)
    return {name: tuple(res[4 * i:4 * i + 4]) for i, name in enumerate(names)}


ANY = pl.BlockSpec(memory_space=pl.ANY)


def _place():
    x, y, c = lax.axis_index("x"), lax.axis_index("y"), lax.axis_index("c")
    others = [(1 - x, y), (x, 1 - y), (1 - x, 1 - y)]
    return x, y, c, others


def _rcopy(src, dst, ssem, rsem, to):
    return pltpu.make_async_remote_copy(src_ref=src, dst_ref=dst, send_sem=ssem, recv_sem=rsem,
                                        device_id=to, device_id_type=MESH)


def gather_weights(arrs):
    n = len(arrs)

    def body(*refs):
        srcs, outs, ssem, rsem = refs[:n], refs[n:2 * n], refs[2 * n], refs[2 * n + 1]
        x, y, c, others = _place()
        me = 2 * x + y
        sib = (x, y, 1 - c)
        first = [_rcopy(srcs[i].at[c], outs[i].at[c, me], ssem.at[6 * i + k], rsem.at[6 * i + k], (ox, oy, c))
                 for i in range(n) for k, (ox, oy) in enumerate(others)]
        for cp in first:
            cp.start()
        passed = []
        for k, (ox, oy) in enumerate(others):
            for i in range(n):
                slot = outs[i].at[c, 2 * ox + oy]
                _rcopy(slot, slot, ssem.at[6 * i + k], rsem.at[6 * i + k], sib).wait_recv()
                fw = _rcopy(slot, slot, ssem.at[6 * i + 3 + k], rsem.at[6 * i + 3 + k], sib)
                fw.start()
                passed.append(fw)
        for k, (ox, oy) in enumerate(others):
            for i in range(n):
                slot = outs[i].at[1 - c, 2 * ox + oy]
                _rcopy(slot, slot, ssem.at[6 * i + 3 + k], rsem.at[6 * i + 3 + k], sib).wait_recv()
        for cp in first + passed:
            cp.wait_send()

    return pl.pallas_call(
        body, name="gather_weights",
        in_specs=[ANY] * n, out_specs=[ANY] * n,
        out_shape=[jax.ShapeDtypeStruct((2, SHARDS) + a.shape[1:], a.dtype) for a in arrs],
        scratch_shapes=[pltpu.SemaphoreType.DMA((6 * n,)), pltpu.SemaphoreType.DMA((6 * n,))],
    )(*arrs)


HBM = pl.BlockSpec(memory_space=pltpu.HBM)
SEM = pl.BlockSpec(memory_space=pltpu.SEMAPHORE)
EFFECT = pltpu.SideEffectType.DATAFLOW_SIDE_EFFECTING


def _in_hbm(a):
    return pltpu.with_memory_space_constraint(a, pltpu.HBM)


def gather_start(srcs, after, name, by_dest=False):
    n = len(srcs)
    lands = [_in_hbm(lax.empty((SHARDS,) + a.shape[-2:], a.dtype)) for a in srcs]
    na = len(after)

    def body(*refs):
        src, land = refs[:n], refs[n:2 * n]
        ssem, rsem, token = refs[2 * n + na], refs[2 * n + na + 1], refs[-1]
        x, y, c, others = _place()
        me = 2 * x + y
        for i in range(n):
            for k, (ox, oy) in enumerate(others):
                s = src[i].at[2 * ox + oy] if by_dest else src[i]
                _rcopy(s, land[i].at[me], ssem.at[3 * i + k], rsem.at[3 * i + k], (ox, oy, c)).start()
        token[...] = jnp.zeros_like(token)

    bufs = [_in_hbm(a) for a in srcs] + lands
    out = pl.pallas_call(
        body, name=name,
        out_shape=(pltpu.SemaphoreType.DMA((3 * n,)), pltpu.SemaphoreType.DMA((3 * n,)),
                   *[pltpu.HBM(b.shape, b.dtype) for b in bufs], jax.ShapeDtypeStruct((8, 128), F32)),
        in_specs=[HBM] * (2 * n) + [ANY] * na,
        out_specs=(SEM, SEM, *[HBM] * (2 * n), pl.BlockSpec(memory_space=pltpu.VMEM)),
        input_output_aliases={i: 2 + i for i in range(2 * n)},
        compiler_params=pltpu.CompilerParams(has_side_effects=EFFECT),
    )(*bufs, *after)
    return out[0], out[1], list(out[2:2 + n]), list(out[2 + n:2 + 2 * n]), out[-1]


def gather_wait(ssem, rsem, srcs, lands, after, name, by_dest=False):
    n = len(srcs)

    def body(*refs):
        src, land = refs[:n], refs[n:2 * n]
        s_sem, r_sem = refs[2 * n], refs[2 * n + 1]
        x, y, c, others = _place()
        for i in range(n):
            for k, (ox, oy) in enumerate(others):
                s = src[i].at[2 * ox + oy] if by_dest else src[i]
                cp = _rcopy(s, land[i].at[2 * ox + oy], s_sem.at[3 * i + k], r_sem.at[3 * i + k], (ox, oy, c))
                cp.wait_send()
                cp.wait_recv()

    bufs = list(srcs) + list(lands)
    out = pl.pallas_call(
        body, name=name,
        out_shape=tuple(pltpu.HBM(b.shape, b.dtype) for b in bufs),
        in_specs=[HBM] * (2 * n) + [SEM, SEM, ANY],
        out_specs=tuple([HBM] * (2 * n)),
        input_output_aliases={i: i for i in range(2 * n)},
        compiler_params=pltpu.CompilerParams(has_side_effects=EFFECT),
    )(*bufs, ssem, rsem, after)
    return list(out[n:2 * n])


def grad_sibling_exchange(arrs):
    n = len(arrs)

    def body(*refs):
        srcs, outs, ssem, rsem = refs[:n], refs[n:2 * n], refs[2 * n], refs[2 * n + 1]
        x, y, c, _ = _place()
        cps = [_rcopy(srcs[i].at[1 - c], outs[i], ssem.at[i], rsem.at[i], (x, y, 1 - c)) for i in range(n)]
        for cp in cps:
            cp.start()
        for cp in cps:
            cp.wait()

    return pl.pallas_call(
        body, name="grad_sibling_exchange",
        in_specs=[ANY] * n, out_specs=[ANY] * n,
        out_shape=[jax.ShapeDtypeStruct(a.shape[1:], F32) for a in arrs],
        scratch_shapes=[pltpu.SemaphoreType.DMA((n,)), pltpu.SemaphoreType.DMA((n,))],
    )(*arrs)


def grad_chip_sum(g, sb, cc, tr, name):
    _, _, R, C = g.shape
    blk = pl.BlockSpec((1, tr, C), lambda s, r, c: (s, r, 0))
    grid_spec = pltpu.PrefetchScalarGridSpec(
        num_scalar_prefetch=1, grid=(SHARDS, R // tr),
        in_specs=[pl.BlockSpec((1, 1, tr, C), lambda s, r, c: (c[0], s, r, 0)), blk],
        out_specs=[blk, blk])

    def body(c_ref, a_ref, b_ref, o_ref, ob_ref):
        t = a_ref[0] + b_ref[...]
        o_ref[...] = t
        ob_ref[...] = t.astype(BF16)

    return pl.pallas_call(
        body, name=name, grid_spec=grid_spec,
        out_shape=[jax.ShapeDtypeStruct((SHARDS, R, C), F32), jax.ShapeDtypeStruct((SHARDS, R, C), BF16)],
        compiler_params=_params(("arbitrary", "arbitrary")),
    )(cc, g, sb)


def grad_shard_sum(t, rb, me, tr, name):
    _, R, C = t.shape
    grid_spec = pltpu.PrefetchScalarGridSpec(
        num_scalar_prefetch=1, grid=(R // tr,),
        in_specs=[pl.BlockSpec((1, tr, C), lambda r, m: (m[0], r, 0)),
                  pl.BlockSpec((SHARDS, tr, C), lambda r, m: (0, r, 0))],
        out_specs=pl.BlockSpec((tr, C), lambda r, m: (r, 0)))

    def body(m_ref, t_ref, r_ref, o_ref):
        part = [jnp.where(m_ref[0] == s, t_ref[0], r_ref[s].astype(F32)) for s in range(SHARDS)]
        o_ref[...] = ((part[0] + part[1]) + part[2]) + part[3]

    return pl.pallas_call(
        body, name=name, grid_spec=grid_spec,
        out_shape=jax.ShapeDtypeStruct((R, C), F32),
        compiler_params=_params(("arbitrary",)),
    )(me, t, rb)


def grad_sibling_share(arrs, name):
    n = len(arrs)

    def body(*refs):
        srcs, outs, ssem, rsem = refs[:n], refs[n:2 * n], refs[2 * n], refs[2 * n + 1]
        x, y, c, _ = _place()
        cps = [_rcopy(srcs[i], outs[i], ssem.at[i], rsem.at[i], (x, y, 1 - c)) for i in range(n)]
        for cp in cps:
            cp.start()
        for cp in cps:
            cp.wait()

    return pl.pallas_call(
        body, name=name,
        in_specs=[ANY] * n, out_specs=[ANY] * n,
        out_shape=[jax.ShapeDtypeStruct(a.shape, F32) for a in arrs],
        scratch_shapes=[pltpu.SemaphoreType.DMA((n,)), pltpu.SemaphoreType.DMA((n,))],
    )(*arrs)


def _allreduce_rows(src, sib_buf, chips, out_ref, ssem, rsem):
    x, y, c, others = _place()
    me = 2 * x + y
    cp = _rcopy(src, sib_buf, ssem.at[0], rsem.at[0], (x, y, 1 - c))
    cp.start()
    cp.wait()
    chips[me] = src[...] + sib_buf[...]
    sends = [_rcopy(chips.at[me], chips.at[me], ssem.at[1 + k], rsem.at[1 + k], (ox, oy, c))
             for k, (ox, oy) in enumerate(others)]
    for s in sends:
        s.start()
    for k, (ox, oy) in enumerate(others):
        slot = chips.at[2 * ox + oy]
        _rcopy(slot, slot, ssem.at[1 + k], rsem.at[1 + k], (ox, oy, c)).wait_recv()
    for s in sends:
        s.wait_send()
    out_ref[...] = ((chips[0] + chips[1]) + chips[2]) + chips[3]


def _allreduce_scratch(rows):
    return [pltpu.VMEM((rows, 128), F32), pltpu.VMEM((SHARDS, rows, 128), F32),
            pltpu.SemaphoreType.DMA((4,)), pltpu.SemaphoreType.DMA((4,))]


def allreduce_rows(buf, name):
    rows = buf.shape[0]
    VM = pl.BlockSpec(memory_space=pltpu.VMEM)

    def body(src_ref, out_ref, sib_buf, chips, ssem, rsem):
        _allreduce_rows(src_ref, sib_buf, chips, out_ref, ssem, rsem)

    return pl.pallas_call(
        body, name=name, in_specs=[VM], out_specs=VM,
        out_shape=jax.ShapeDtypeStruct((rows, 128), F32),
        scratch_shapes=_allreduce_scratch(rows), compiler_params=_params(),
    )(buf)


def small_allreduce(grads, rel, loss_part):
    rows = _small_rows()
    keys = [(l, name) for l in range(DEPTH) for name in SMALL]
    flat = [grads[l][SMALL[name][0]] for l, name in keys] + [rel, loss_part]

    def body(*refs):
        ins = refs[:len(flat)]
        out_ref, src, sib_buf, chips, ssem, rsem = refs[len(flat):]
        src[...] = jnp.zeros_like(src)
        for (l, name), ref in zip(keys, ins):
            r0 = rows[(l, name)]
            if name == "sg_w":
                for grp in range(8):
                    src[r0 + grp * 128:r0 + (grp + 1) * 128, :] = ref[grp]
            elif name == "sg_b":
                src[r0:r0 + 8, :] = ref[...].T[0:8, :]
            else:
                for j in range(ref.shape[1] // 128):
                    src[r0 + j:r0 + j + 1, :] = ref[:, j * 128:(j + 1) * 128]
        src[REL_ROW:REL_ROW + 32, 0:16] = ins[-2][...]
        src[LOSS_ROW:LOSS_ROW + 1, :] = ins[-1][...]
        _allreduce_rows(src, sib_buf, chips, out_ref, ssem, rsem)

    return pl.pallas_call(
        body, name="small_allreduce",
        out_shape=jax.ShapeDtypeStruct((SMALL_ROWS, 128), F32),
        scratch_shapes=[pltpu.VMEM((SMALL_ROWS, 128), F32)] + _allreduce_scratch(SMALL_ROWS),
        compiler_params=_params(),
    )(*flat)


def _pad_lanes(v):
    return jnp.zeros((1, 128), F32).at[0, :v.shape[0]].set(v)


def layer_fwd(x, wts, bias):
    wt = wts["wt"]
    tn = {name: t for name, _, t in GROUPS}
    p_gate, h = inproj_first(x, wts["g_pre"], wt["gate"], tn["gate"], "inproj_gate")
    p_sgu, p_att, p_ssd = (inproj_group(h, wt[n], tn[n], "inproj_" + n, F32 if n == "att" else BF16)
                           for n in ("sgu", "att", "ssd"))
    y_att = att_fwd(p_att, bias, wts["sinks"])
    y_sg = sgu_fwd(p_sgu, wts["ln_g"], wts["ln_b"], wts["sg_w"], wts["sg_bt"])
    y_ssm, hst = ssd_fwd(p_ssd, wts["conv_w"], wts["conv_b"], wts["dt_bias"], wts["a_log"], wts["d_skip"],
                         wts["norm_g"])
    x_new, br_a, br_s, br_m, merged, out = merge_fwd(
        y_att, y_sg, y_ssm, p_gate, x, wts["w_a"], wts["w_s"], wts["w_m"], wts["w_o"], wts["g_post"])
    saved = dict(x=x, p_gate=p_gate, p_sgu=p_sgu, p_att=p_att, p_ssd=p_ssd, h=h,
                 y_att=y_att, y_sg=y_sg, y_ssm=y_ssm, hst=hst,
                 br_a=br_a, br_s=br_s, br_m=br_m, merged=merged, out=out)
    return x_new, saved


def layer_bwd(dy, wts, bias, sv):
    dps, grads = layer_bwd_params(dy, wts, bias, sv)
    dx, grads["g_pre"] = layer_bwd_input(dy, dps, wts, sv, wts["g_pre"])
    return dx, grads


def layer_bwd_input(dy, dps, wts, sv, g_pre):
    wt = wts["wt"]
    tn = {name: t for name, _, t in GROUPS}
    acc = None
    for n in ("gate", "sgu", "ssd"):
        acc = dh_group(dps[n], wt[n], acc, DH_TILE[n], "dh_" + n)
    return dh_last(dps["att"], wt["att"], acc, sv["x"], g_pre, dy, tn["att"], "dh_att")


def layer_bwd_params(dy, wts, bias, sv):
    dout, dba, dbs, dbm, d_gate, dya, dys, dym, dg_post = merge_bwd(
        dy, sv["out"], wts["g_post"], sv["p_gate"], sv["br_a"], sv["br_s"], sv["br_m"],
        wts["w_a"], wts["w_s"], wts["w_m"], wts["w_o"])
    d_att, dbias, dsinks = att_bwd(dya, sv["p_att"], bias, wts["sinks"])
    d_sgu, dsg_w, dsg_bt, dln_g, dln_b = sgu_bwd(dys, sv["p_sgu"], wts["ln_g"], wts["ln_b"], wts["sg_w"],
                                                 wts["sg_bt"])
    d_ssd, dcw, dcb, ddtb, dalog, ddsk, dng = ssd_bwd(
        dym, sv["p_ssd"], sv["hst"], wts["conv_w"], wts["conv_b"], wts["dt_bias"], wts["a_log"], wts["d_skip"],
        wts["norm_g"])
    dps = dict(gate=d_gate, sgu=d_sgu, att=d_att, ssd=d_ssd)
    tn = {name: t for name, _, t in GROUPS}
    grads = dict(
        w_in={n: dw_group(dps[n], sv["h"], tn[n], "dw_in_" + n) for n in dps},
        w_a=matmul_tn(sv["y_att"], dba, "dw_att"),
        w_s=matmul_tn(sv["y_sg"], dbs, "dw_sg"),
        w_m=matmul_tn(sv["y_ssm"], dbm, "dw_ssm"),
        w_o=matmul_tn(sv["merged"], dout, "dw_out"),
        g_post=dg_post, sinks=dsinks, ln_g=dln_g, ln_b=dln_b, sg_w=dsg_w, sg_bt=dsg_bt,
        conv_w=dcw, conv_b=dcb, dt_bias=ddtb, a_log=dalog, d_skip=ddsk, norm_g=dng, bias=dbias)
    return dps, grads


REST_OFF = (0, 256, 512, 1024, 1280)
GR_ROWS = 1536
GR_CONV = 1280
W_IN_SPLIT = 1600
W_IN_HALF = 1824


def kernel(x, w_in, norm_pre, norm_post, rel_bias, att_sinks, sg_ln_g, sg_ln_b, sg_w, sg_b, ssm_conv_w, ssm_conv_b, ssm_dt_bias, ssm_a_log, ssm_d, ssm_norm_g, w_br_att, w_br_sg, w_br_ssm, w_out, loss_target, m_w_in, m_norm_pre, m_norm_post, m_rel_bias, m_att_sinks, m_sg_ln_g, m_sg_ln_b, m_sg_w, m_sg_b, m_ssm_conv_w, m_ssm_conv_b, m_ssm_dt_bias, m_ssm_a_log, m_ssm_d, m_ssm_norm_g, m_w_br_att, m_w_br_sg, m_w_br_ssm, m_w_out, v_w_in, v_norm_pre, v_norm_post, v_rel_bias, v_att_sinks, v_sg_ln_g, v_sg_ln_b, v_sg_w, v_sg_b, v_ssm_conv_w, v_ssm_conv_b, v_ssm_dt_bias, v_ssm_a_log, v_ssm_d, v_ssm_norm_g, v_w_br_att, v_w_br_sg, v_w_br_ssm, v_w_out):
    cx, cy, cc = lax.axis_index("x"), lax.axis_index("y"), lax.axis_index("c")
    me = 2 * cx + cy
    xs = x[0]
    S = xs.shape[0]

    tr = lambda a: jnp.transpose(a, (0, 2, 1))
    w_in_b = tr(w_in).astype(BF16)
    w_rest_b = jnp.concatenate([w_br_att, w_br_sg, w_br_ssm, w_out], axis=1).astype(BF16)
    halves = lambda a: a.reshape(2, a.shape[0] // 2, a.shape[1])
    w_in0 = jnp.pad(w_in_b[0], ((0, W_IN_ROWS - 3400), (0, 0)))
    all0_in, all0_rest = gather_weights([halves(w_in0), halves(w_rest_b[0])])
    convw_slot = jnp.zeros((SHARDS, DEPTH * CONV_K * 768 // 128, 128), F32)
    convw_slot = lax.dynamic_update_index_in_dim(
        convw_slot, jnp.where(cc == 0, 1.0, 0.0) * ssm_conv_w.reshape(-1, 128), me, 0)
    convw_rows = allreduce_rows(convw_slot.reshape(-1, 128), "gather_conv_w")
    convw_all = convw_rows.reshape(SHARDS, DEPTH, CONV_K, 768).transpose(1, 2, 0, 3).reshape(DEPTH, CONV_K, CONV_C)
    g1_ssem, g1_rsem, g1_srcs, g1_lands, g1_token = gather_start(
        [w_in_b[1], w_rest_b[1]], [convw_rows, all0_rest], "gather_l1_start")

    o = REST_OFF

    def layer_weights(l, gathered_in, gathered_rest, g_pre):
        sh_in = [jnp.where(me == s, w_in_b[l], gathered_in[s]) for s in range(SHARDS)]
        sh_rest = [jnp.where(me == s, w_rest_b[l], gathered_rest[s]) for s in range(SHARDS)]
        rest = lambda k: jnp.concatenate([r[o[k]:o[k + 1]] for r in sh_rest], axis=0)
        return dict(
            wt=group_weights(jnp.concatenate(sh_in, axis=0)),
            w_a=rest(0), w_s=rest(1), w_m=rest(2), w_o=rest(3),
            g_pre=g_pre, g_post=norm_post[l][None], sinks=att_sinks[l],
            ln_g=sg_ln_g[l][None], ln_b=sg_ln_b[l][None], sg_w=sg_w[l],
            sg_bt=sg_b[l].T,
            conv_w=jnp.concatenate([convw_all[l], jnp.zeros((4, CONV_C), F32)], axis=0),
            conv_b=ssm_conv_b[l][None], dt_bias=_pad_lanes(ssm_dt_bias[l]), a_log=_pad_lanes(ssm_a_log[l]),
            d_skip=_pad_lanes(ssm_d[l]), norm_g=ssm_norm_g[l][None])

    bias = bias_table(rel_bias)
    layers = [layer_weights(0, [all0_in[:, s].reshape(W_IN_ROWS, D)[0:3400] for s in range(SHARDS)],
                            [all0_rest[:, s].reshape(1280, D) for s in range(SHARDS)],
                            (norm_pre[0] + g1_token[0, 0])[None])]
    act, sv0 = layer_fwd(xs, layers[0], bias)
    land_in, land_rest = gather_wait(g1_ssem, g1_rsem, g1_srcs, g1_lands, act, "gather_l1_wait")
    layers.append(layer_weights(1, land_in, land_rest, norm_pre[1][None]))
    act, sv1 = layer_fwd(act, layers[1], bias)
    saved = [sv0, sv1]
    dy, loss_part = loss_head(act, loss_target[0])
    cvec = jnp.reshape(cc, (1,)).astype(jnp.int32)
    mvec = jnp.reshape(me, (1,)).astype(jnp.int32)

    def by_shard(g):
        gcw = g["conv_w"][0:CONV_K].reshape(CONV_K, SHARDS, 768).transpose(1, 0, 2).reshape(SHARDS, 3, 1024)
        rest = jnp.concatenate([
            g["w_a"].reshape(SHARDS, 256, D), g["w_s"].reshape(SHARDS, 256, D), g["w_o"].reshape(SHARDS, 256, D),
            g["w_m"].reshape(SHARDS, 512, D), jnp.pad(gcw, ((0, 0), (0, GR_ROWS - GR_CONV - 3), (0, 0)))], axis=1)
        return ungroup_grads(g["w_in"]).reshape(SHARDS, 3400, D), rest

    grads = [None] * DEPTH
    dy, grads[1] = layer_bwd(dy, layers[1], bias, saved[1])
    g1_in, g1_rest = by_shard(grads[1])
    g1_in = jnp.pad(g1_in, ((0, 0), (0, W_IN_ROWS - 3400), (0, 0)))
    x1_ssem, x1_rsem, x1_srcs, x1_lands, x1_token = gather_start(
        [g1_in.astype(BF16), g1_rest.astype(BF16)], [], "grads_l1_start", by_dest=True)
    wts0 = dict(layers[0], g_post=layers[0]["g_post"] + x1_token[0, 0])
    dps0, grads[0] = layer_bwd_params(dy, wts0, bias, saved[0])
    r1_in, r1_rest = gather_wait(x1_ssem, x1_rsem, x1_srcs, x1_lands, grads[0]["w_in"]["ssd"], "grads_l1_wait",
                                 by_dest=True)
    p_in = grad_shard_sum(g1_in, r1_in, mvec, 384, "l1_sum_w_in")
    p_rest = grad_shard_sum(g1_rest, r1_rest, mvec, 512, "l1_sum_rest")
    pb_in, pb_rest = grad_sibling_share([p_in, p_rest], "l1_sibling_share")

    g0_in, g0_rest = by_shard(grads[0])
    pad_to = lambda a, rows: jnp.pad(a, ((0, 0), (0, rows - a.shape[1]), (0, 0)))
    g0_in = jnp.stack([pad_to(g0_in[:, 0:W_IN_SPLIT], W_IN_HALF), pad_to(g0_in[:, W_IN_SPLIT:3400], W_IN_HALF)])
    g0_rest = jnp.stack([g0_rest[:, 0:GR_ROWS // 2], g0_rest[:, GR_ROWS // 2:GR_ROWS]])
    sb_in, sb_rest = grad_sibling_exchange([g0_in, g0_rest])
    t_in, t_in_b = grad_chip_sum(g0_in, sb_in, cvec, 608, "chip_sum_w_in")
    t_rest, t_rest_b = grad_chip_sum(g0_rest, sb_rest, cvec, 384, "chip_sum_rest")
    x0_ssem, x0_rsem, x0_srcs, x0_lands, x0_token = gather_start([t_in_b, t_rest_b], [], "grads_l0_start", by_dest=True)
    dy, grads[0]["g_pre"] = layer_bwd_input(dy, dps0, layers[0], saved[0], layers[0]["g_pre"] + x0_token[0, 0])
    grad_x = dy[None]
    rb_in, rb_rest = gather_wait(x0_ssem, x0_rsem, x0_srcs, x0_lands, dy, "grads_l0_wait", by_dest=True)
    grad_rel_local = bias_grad(grads[0]["bias"] + grads[1]["bias"])
    f_in = grad_shard_sum(t_in, rb_in, mvec, 608, "shard_sum_w_in")
    f_rest = grad_shard_sum(t_rest, rb_rest, mvec, 384, "shard_sum_rest")
    fb_in, fb_rest = grad_sibling_share([f_in, f_rest], "l0_sibling_share")

    red = small_allreduce(grads, grad_rel_local, loss_part + 0.0 * f_rest[0:1, 0:128])
    loss = red[LOSS_ROW, 0]

    res = adamw_small(red, (rel_bias, m_rel_bias, v_rel_bias), dict(
        norm_pre=(norm_pre, m_norm_pre, v_norm_pre), norm_post=(norm_post, m_norm_post, v_norm_post),
        att_sinks=(att_sinks, m_att_sinks, v_att_sinks), sg_ln_g=(sg_ln_g, m_sg_ln_g, v_sg_ln_g),
        sg_ln_b=(sg_ln_b, m_sg_ln_b, v_sg_ln_b), sg_w=(sg_w, m_sg_w, v_sg_w), sg_b=(sg_b, m_sg_b, v_sg_b),
        ssm_conv_b=(ssm_conv_b, m_ssm_conv_b, v_ssm_conv_b), ssm_dt_bias=(ssm_dt_bias, m_ssm_dt_bias, v_ssm_dt_bias),
        ssm_a_log=(ssm_a_log, m_ssm_a_log, v_ssm_a_log), ssm_d=(ssm_d, m_ssm_d, v_ssm_d),
        ssm_norm_g=(ssm_norm_g, m_ssm_norm_g, v_ssm_norm_g)))
    res["w_in"] = tuple(tr(a) for a in adamw_big(
        tr(w_in), tr(m_w_in), tr(v_w_in), (f_in, fb_in, W_IN_SPLIT // 200, 0, 0), (p_in, pb_in, 0), cvec, "adamw_w_in", 200))
    rest_upd = lambda w, m, v, name, n0, off0, off1: adamw_big(
        w, m, v, (f_rest, fb_rest, n0, off0, off0), (p_rest, pb_rest, off1), cvec, name, 256)
    res["w_br_att"] = rest_upd(w_br_att, m_w_br_att, v_w_br_att, "adamw_w_br_att", 1, 0, 0)
    res["w_br_sg"] = rest_upd(w_br_sg, m_w_br_sg, v_w_br_sg, "adamw_w_br_sg", 1, 1, 1)
    res["w_out"] = rest_upd(w_out, m_w_out, v_w_out, "adamw_w_out", 1, 2, 2)
    res["w_br_ssm"] = rest_upd(w_br_ssm, m_w_br_ssm, v_w_br_ssm, "adamw_w_br_ssm", 0, 0, 3)
    cw0 = jnp.where(cc == 1, f_rest, fb_rest)[GR_CONV - GR_ROWS // 2:GR_CONV - GR_ROWS // 2 + 3]
    cw1 = (p_rest + pb_rest)[GR_CONV:GR_CONV + 3]
    g_conv_w = jnp.stack([cw0.reshape(CONV_K, 768), cw1.reshape(CONV_K, 768)])
    res["ssm_conv_w"] = (g_conv_w,) + tuple(adamw_plain(ssm_conv_w, g_conv_w, m_ssm_conv_w, v_ssm_conv_w, "adamw_conv_w"))

    order = ["w_in", "norm_pre", "norm_post", "rel_bias", "att_sinks", "sg_ln_g", "sg_ln_b", "sg_w", "sg_b",
             "ssm_conv_w", "ssm_conv_b", "ssm_dt_bias", "ssm_a_log", "ssm_d", "ssm_norm_g",
             "w_br_att", "w_br_sg", "w_br_ssm", "w_out"]
    return (loss, grad_x, *[res[n][0] for n in order], *[res[n][1] for n in order],
            *[res[n][2] for n in order], *[res[n][3] for n in order])
```

```python
import functools
import math

import numpy as np
import jax
import jax.numpy as jnp
from jax import lax
from jax.experimental import pallas as pl
from jax.experimental.pallas import tpu as pltpu

F32 = jnp.float32
BF16 = jnp.bfloat16
MESH = pl.DeviceIdType.MESH

D = 1024
DEPTH = 2
EPS = 1e-6
L = 128
HEADS = 16
KV = 2
DH = 64
SSM_W = 2048
SSM_H = 32
SSM_P = 64
SSM_G = 4
SSM_N = 128
CONV_K = 4
CONV_C = 3072
NEG = -1e30
IN_COLS = 13600

GROUPS = (("gate", 3072, 1536), ("sgu", 3072, 1536), ("att", 2304, 2304), ("ssd", 5376, 1792))
W_IN_ROWS = 3456
DH_TILE = {"gate": 3072, "sgu": 3072, "ssd": 2688}

ADAM_LR = 0.001
ADAM_B1 = 0.9
ADAM_B2 = 0.999
ADAM_EPS = 1e-08
ADAM_WD = 0.01
ADAM_STEP = 10

VMEM_LIMIT = 56 * 1024 * 1024

SHARDS = 4


def _dot(a, b):
    return jnp.dot(a, b, preferred_element_type=F32)


def _dot_nt(a, b):
    return lax.dot_general(a, b, (((1,), (1,)), ((), ())), preferred_element_type=F32)


def _dot_tn(a_f32, b):
    return jnp.dot(a_f32.T.astype(BF16), b, preferred_element_type=F32)


def _dot_t(a, b):
    return lax.dot_general(a, b, (((0,), (0,)), ((), ())), preferred_element_type=F32)


def _dot_hi(a, b):
    return jnp.dot(a, b, preferred_element_type=F32, precision=lax.Precision.HIGHEST)


def _pieces(x, n):
    out = []
    for _ in range(n - 1):
        p = x.astype(BF16)
        out.append(p)
        x = x - p.astype(F32)
    out.append(x.astype(BF16))
    return out


def _dot_sel(a, sel, n):
    sel = sel.astype(BF16)
    acc = None
    for p in _pieces(a, n):
        t = _dot(p, sel)
        acc = t if acc is None else acc + t
    return acc


def _sel_dot(sel, b, n):
    sel = sel.astype(BF16)
    acc = None
    for p in _pieces(b, n):
        t = _dot(sel, p)
        acc = t if acc is None else acc + t
    return acc


def _sigmoid(x):
    return 1.0 / (1.0 + jnp.exp(-x))


def _softplus(x):
    return jnp.maximum(x, 0.0) + jnp.log(1.0 + jnp.exp(-jnp.abs(x)))


def _params(sem=None, vmem=VMEM_LIMIT):
    kw = dict(vmem_limit_bytes=vmem)
    if sem is not None:
        kw["dimension_semantics"] = sem
    return pltpu.CompilerParams(**kw)


def _full(shape):
    nd = len(shape)
    return pl.BlockSpec(shape, lambda *_: (0,) * nd)


def group_weights(wt):
    return dict(
        gate=wt[10528:13600],
        sgu=wt[2304:5376],
        att=jnp.concatenate([wt[0:1024], wt[1280:2304], wt[1024:1280]], axis=0),
        ssd=jnp.concatenate([wt[5376:10496], wt[10496:10528], jnp.zeros((224, D), wt.dtype)], axis=0))


def ungroup_grads(g):
    a, s = g["att"], g["ssd"]
    return jnp.concatenate([a[0:1024], a[2048:2304], a[1024:2048], g["sgu"], s[0:5152], g["gate"]], axis=0)


def _bucket_table():
    qi = np.arange(L)[:, None]
    kj = np.arange(2 * L)[None, :]
    dist = np.maximum(qi + L - kj, 0)
    dist_f = np.maximum(dist, 1).astype(np.float32)
    large = 16 + (np.log(dist_f / np.float32(16)) / np.float32(math.log(128 / 16)) * np.float32(16)).astype(np.int32)
    large = np.minimum(large, 31)
    return np.where(dist < 16, dist, large).astype(np.int32)


def bias_table(rel_bias):
    buckets = jnp.asarray(_bucket_table().reshape(1, L * 2 * L))

    def body(rb_ref, bk_ref, out_ref):
        onehot = (lax.broadcasted_iota(jnp.int32, (32, L * 2 * L), 0) == bk_ref[...]).astype(F32)
        out_ref[...] = lax.dot_general(rb_ref[...], onehot, (((0,), (0,)), ((), ())),
                                       preferred_element_type=F32, precision=lax.Precision.HIGHEST)

    out = pl.pallas_call(
        body, name="bias_table",
        out_shape=jax.ShapeDtypeStruct((HEADS, L * 2 * L), F32),
        compiler_params=_params(),
    )(rel_bias, buckets)
    out = out.reshape(HEADS, L, 2 * L)
    win = _window_mask()
    first = win & (np.arange(2 * L)[None, :] >= L)
    return jnp.stack([jnp.where(first, out, NEG), jnp.where(win, out, NEG)])


def _window_mask():
    dist = np.arange(L)[:, None] + L - np.arange(2 * L)[None, :]
    return (dist >= 0) & (dist < L)


def bias_grad(dbias):
    buckets = jnp.asarray(_bucket_table().reshape(1, L * 2 * L))

    def body(db_ref, bk_ref, out_ref):
        onehot = (lax.broadcasted_iota(jnp.int32, (32, L * 2 * L), 0) == bk_ref[...]).astype(F32)
        out_ref[...] = lax.dot_general(onehot, db_ref[...], (((1,), (1,)), ((), ())),
                                       preferred_element_type=F32, precision=lax.Precision.HIGHEST)

    return pl.pallas_call(
        body, name="bias_grad",
        out_shape=jax.ShapeDtypeStruct((32, HEADS), F32),
        compiler_params=_params(),
    )(dbias.reshape(HEADS, L * 2 * L), buckets)


def _row_tile(S):
    return 1024 if S % 1024 == 0 else 512


def inproj_first(x, g_pre, wt, tn, name):
    S, W = x.shape[0], wt.shape[0]
    tm = _row_tile(S)

    def body(x_ref, g_ref, w_ref, o_ref, h_ref):
        @pl.when(pl.program_id(1) == 0)
        def _():
            xv = x_ref[...]
            r = lax.rsqrt(jnp.mean(xv * xv, axis=-1, keepdims=True) + EPS)
            h_ref[...] = (xv * r * g_ref[...]).astype(BF16)
        o_ref[...] = _dot_nt(h_ref[...], w_ref[...]).astype(BF16)

    return pl.pallas_call(
        body, name=name, grid=(S // tm, W // tn),
        in_specs=[pl.BlockSpec((tm, D), lambda i, j: (i, 0)), _full((1, D)),
                  pl.BlockSpec((tn, D), lambda i, j: (j, 0))],
        out_specs=[pl.BlockSpec((tm, tn), lambda i, j: (i, j)), pl.BlockSpec((tm, D), lambda i, j: (i, 0))],
        out_shape=[jax.ShapeDtypeStruct((S, W), BF16), jax.ShapeDtypeStruct((S, D), BF16)],
        compiler_params=_params(("arbitrary", "arbitrary")),
    )(x, g_pre, wt)


def inproj_group(h, wt, tn, name, dtype):
    S, W = h.shape[0], wt.shape[0]
    tm = _row_tile(S)

    def body(h_ref, w_ref, o_ref):
        o_ref[...] = _dot_nt(h_ref[...], w_ref[...]).astype(dtype)

    return pl.pallas_call(
        body, name=name, grid=(S // tm, W // tn),
        in_specs=[pl.BlockSpec((tm, D), lambda i, j: (i, 0)), pl.BlockSpec((tn, D), lambda i, j: (j, 0))],
        out_specs=pl.BlockSpec((tm, tn), lambda i, j: (i, j)),
        out_shape=jax.ShapeDtypeStruct((S, W), dtype),
        compiler_params=_params(("arbitrary", "arbitrary")),
    )(h, wt)


def dh_group(dp, wt, acc, tk, name):
    S, W = dp.shape
    tm = _row_tile(S)

    def body(*refs):
        dp_ref, w_ref, o_ref = refs[0], refs[1], refs[-1]
        first = pl.program_id(1) == 0
        if acc is None:
            @pl.when(first)
            def _():
                o_ref[...] = jnp.zeros_like(o_ref)
        else:
            @pl.when(first)
            def _():
                o_ref[...] = refs[2][...]
        o_ref[...] += _dot(dp_ref[...], w_ref[...])

    row = pl.BlockSpec((tm, D), lambda i, k: (i, 0))
    return pl.pallas_call(
        body, name=name, grid=(S // tm, W // tk),
        in_specs=[pl.BlockSpec((tm, tk), lambda i, k: (i, k)), pl.BlockSpec((tk, D), lambda i, k: (k, 0))]
        + ([] if acc is None else [row]),
        out_specs=row, out_shape=jax.ShapeDtypeStruct((S, D), F32),
        input_output_aliases={} if acc is None else {2: 0},
        compiler_params=_params(("arbitrary", "arbitrary")),
    )(*((dp, wt) if acc is None else (dp, wt, acc)))


def dh_last(dp, wt, acc_in, x, g_pre, dy, tk, name):
    S, W = dp.shape
    tm = 512
    nk = W // tk

    def body(dp_ref, w_ref, a_ref, x_ref, g_ref, dy_ref, dx_ref, dg_ref, acc):
        i, k = pl.program_id(0), pl.program_id(1)

        @pl.when(k == 0)
        def _():
            acc[...] = a_ref[...]

        acc[...] += _dot(dp_ref[...], w_ref[...])

        @pl.when((k == nk - 1) & (i == 0))
        def _():
            dg_ref[...] = jnp.zeros_like(dg_ref)

        @pl.when(k == nk - 1)
        def _():
            xv = x_ref[...]
            dh = acc[...]
            g = g_ref[...]
            r = lax.rsqrt(jnp.mean(xv * xv, axis=-1, keepdims=True) + EPS)
            dhg = dh * g
            dx_ref[...] = dy_ref[...] + r * dhg - xv * (r * r * r) * jnp.mean(dhg * xv, axis=-1, keepdims=True)
            dg_ref[...] += jnp.sum(dh * xv * r, axis=0, keepdims=True)

    row = pl.BlockSpec((tm, D), lambda i, k: (i, 0))
    return pl.pallas_call(
        body, name=name, grid=(S // tm, nk),
        in_specs=[pl.BlockSpec((tm, tk), lambda i, k: (i, k)), pl.BlockSpec((tk, D), lambda i, k: (k, 0)),
                  row, row, _full((1, D)), row],
        out_specs=[row, _full((1, D))],
        out_shape=[jax.ShapeDtypeStruct((S, D), F32), jax.ShapeDtypeStruct((1, D), F32)],
        scratch_shapes=[pltpu.VMEM((tm, D), F32)],
        compiler_params=_params(("arbitrary", "arbitrary")),
    )(dp, wt, acc_in, x, g_pre, dy)


def dw_group(dp, h, tn, name):
    S, W = dp.shape
    ts = _row_tile(S)

    def body(dp_ref, h_ref, o_ref):
        @pl.when(pl.program_id(1) == 0)
        def _():
            o_ref[...] = jnp.zeros_like(o_ref)
        o_ref[...] += _dot_t(dp_ref[...], h_ref[...])

    return pl.pallas_call(
        body, name=name, grid=(W // tn, S // ts),
        in_specs=[pl.BlockSpec((ts, tn), lambda j, s: (s, j)), pl.BlockSpec((ts, D), lambda j, s: (s, 0))],
        out_specs=pl.BlockSpec((tn, D), lambda j, s: (j, 0)),
        out_shape=jax.ShapeDtypeStruct((W, D), F32),
        compiler_params=_params(("arbitrary", "arbitrary")),
    )(dp, h)


def matmul_tn(a, b, name, tn=1024):
    S, K = a.shape
    N = b.shape[1]
    ts = _row_tile(S)
    ns = S // ts

    def body(a_ref, b_ref, o_ref):
        @pl.when(pl.program_id(1) == 0)
        def _():
            o_ref[...] = jnp.zeros_like(o_ref)
        o_ref[...] += _dot_t(a_ref[...], b_ref[...])

    return pl.pallas_call(
        body, name=name, grid=(N // tn, ns),
        in_specs=[pl.BlockSpec((ts, K), lambda j, s: (s, 0)), pl.BlockSpec((ts, tn), lambda j, s: (s, j))],
        out_specs=pl.BlockSpec((K, tn), lambda j, s: (0, j)),
        out_shape=jax.ShapeDtypeStruct((K, N), F32),
        compiler_params=_params(("arbitrary", "arbitrary")),
    )(a, b)


def _att_in_specs(nb):
    last = nb - 1
    cur = lambda n: jnp.minimum(n, last)
    prev = lambda n: jnp.maximum(jnp.minimum(n, last) - 1, 0)
    return [
        pl.BlockSpec((L, 1024), lambda n: (cur(n), 0)),
        pl.BlockSpec((L, 128), lambda n: (prev(n), 16)),
        pl.BlockSpec((L, 128), lambda n: (cur(n), 16)),
        pl.BlockSpec((L, 128), lambda n: (prev(n), 17)),
        pl.BlockSpec((L, 128), lambda n: (cur(n), 17)),
        pl.BlockSpec((L, 1024), lambda n: (cur(n), 1)),
        _full((2, HEADS, L, 2 * L)),
        pl.BlockSpec(memory_space=pltpu.SMEM),
    ]


GH = HEADS // KV
GB = 8


def _stack_heads(ref, h0, nh, scr):
    for g in range(nh):
        scr[(h0 + g) * L:(h0 + g + 1) * L, :] = ref[:, (h0 + g) * DH:(h0 + g + 1) * DH].astype(F32)
    return scr[h0 * L:(h0 + nh) * L, :]


def _unstack_heads(val, h0, nh, ref):
    for g in range(nh):
        ref[:, (h0 + g) * DH:(h0 + g + 1) * DH] = val[g * L:(g + 1) * L, :]


def _sink_rows(s_ref, h0, nh):
    return jnp.concatenate([jnp.full((L, 1), s_ref[h0 + g], F32) for g in range(nh)], axis=0)


def _att_probs(qh, kk, bias_h, sk):
    logits = _dot_nt(qh, kk) + bias_h
    m =jnp.maximum(jnp.max(logits, axis=-1, keepdims=True), sk)
    p = jnp.exp(logits - m)
    es = jnp.exp(sk - m)
    den = jnp.sum(p, axis=-1, keepdims=True) + es
    return p / den, es / den


def att_fwd(proj, bias, sinks):
    S = proj.shape[0]
    nb = S // L

    def body(q_ref, kp_ref, kc_ref, vp_ref, vc_ref, z_ref, bias_ref, s_ref, y_ref, o_scr):
        table = jnp.where(pl.program_id(0) > 0, 1, 0)
        for kv in range(KV):
            sl = slice(kv * DH, (kv + 1) * DH)
            kk = jnp.concatenate([kp_ref[:, sl], kc_ref[:, sl]], axis=0).astype(BF16)
            vv = jnp.concatenate([vp_ref[:, sl], vc_ref[:, sl]], axis=0).astype(BF16)
            for g in range(GH):
                h = kv * GH + g
                hs = slice(h * DH, (h + 1) * DH)
                qh = (q_ref[:, hs] * 0.125).astype(BF16)
                P, _ = _att_probs(qh, kk, bias_ref[table, h], s_ref[h])
                o_scr[:, hs] = _dot(P.astype(BF16), vv)
        z = z_ref[...].astype(F32)
        y_ref[...] = (o_scr[...] * (z * _sigmoid(z))).astype(BF16)

    return pl.pallas_call(
        body, name="att_fwd", grid=(nb,),
        in_specs=_att_in_specs(nb),
        out_specs=pl.BlockSpec((L, 1024), lambda n: (n, 0)),
        out_shape=jax.ShapeDtypeStruct((S, 1024), BF16),
        scratch_shapes=[pltpu.VMEM((L, 1024), F32)],
        compiler_params=_params(("arbitrary",)),
    )(proj, proj, proj, proj, proj, proj, bias, sinks)


def att_bwd(dy, proj, bias, sinks):
    S = proj.shape[0]
    nb = S // L
    last = nb - 1

    def body(dy_ref, q_ref, kp_ref, kc_ref, vp_ref, vc_ref, z_ref, bias_ref, s_ref,
             dout_ref, dbias_ref, dsink_ref, carry, band, dq_scr, dz_scr, qs_scr, zs_scr, dys_scr):
        n = pl.program_id(0)

        @pl.when(n == 0)
        def _():
            carry[...] = jnp.zeros_like(carry)
            dq_scr[...] = jnp.zeros_like(dq_scr)
            dz_scr[...] = jnp.zeros_like(dz_scr)
            dbias_ref[...] = jnp.zeros_like(dbias_ref)
            dsink_ref[...] = jnp.zeros_like(dsink_ref)

        dout_ref[:, 0:1024] = dq_scr[...].astype(BF16)
        dout_ref[:, 1024:2048] = dz_scr[...].astype(BF16)
        band[...] = jnp.zeros_like(band)

        @pl.when(n < nb)
        def _():
            table = jnp.where(n > 0, 1, 0)
            lane = lax.broadcasted_iota(jnp.int32, (1, 128), 1)
            dsink = jnp.zeros((1, 128), F32)
            for kv in range(KV):
                sl = slice(kv * DH, (kv + 1) * DH)
                kk = jnp.concatenate([kp_ref[:, sl], kc_ref[:, sl]], axis=0).astype(BF16)
                vv = jnp.concatenate([vp_ref[:, sl], vc_ref[:, sl]], axis=0).astype(BF16)
                dk_acc = jnp.zeros((2 * L, DH), F32)
                dv_acc = jnp.zeros((2 * L, DH), F32)
                for h0 in range(kv * GH, (kv + 1) * GH, GB):
                    qs = (_stack_heads(q_ref, h0, GB, qs_scr) * 0.125).astype(BF16)
                    bias_g = bias_ref[table, h0:h0 + GB].reshape(GB * L, 2 * L)
                    P, psink = _att_probs(qs, kk, bias_g, _sink_rows(s_ref, h0, GB))
                    zs = _stack_heads(z_ref, h0, GB, zs_scr)
                    dys = _stack_heads(dy_ref, h0, GB, dys_scr)
                    sg = _sigmoid(zs)
                    O = _dot(P.astype(BF16), vv)
                    _unstack_heads(dys * O * (sg * (1.0 + zs * (1.0 - sg))), h0, GB, dz_scr)
                    dOb = (dys * (zs * sg)).astype(BF16)
                    dP = _dot_nt(dOb, vv)
                    delta = jnp.sum(P * dP, axis=-1, keepdims=True)
                    dS = P * (dP - delta)
                    sd = psink * delta
                    for g in range(GB):
                        dsink = dsink + jnp.where(lane == h0 + g, -jnp.sum(sd[g * L:(g + 1) * L, :]), 0.0)
                    _unstack_heads(_dot(dS.astype(BF16), kk) * 0.125, h0, GB, dq_scr)
                    dbias_ref[h0:h0 + GB] += dS.reshape(GB, L, 2 * L)
                    dk_acc = dk_acc + _dot_tn(dS, qs)
                    dv_acc = dv_acc + _dot_tn(P, dOb)
                band[:, sl] = dk_acc
                band[:, 128 + kv * DH:128 + (kv + 1) * DH] = dv_acc
            dsink_ref[...] += dsink

        out = carry[...] + band[0:L, :]
        dout_ref[:, 2048:2304] = out.astype(BF16)
        carry[...] = band[L:2 * L, :]

    cur = lambda n: jnp.minimum(n, last)
    lag = lambda n: jnp.maximum(n - 1, 0)
    return pl.pallas_call(
        body, name="att_bwd", grid=(nb + 1,),
        in_specs=[pl.BlockSpec((L, 1024), lambda n: (cur(n), 0))] + _att_in_specs(nb),
        out_specs=[pl.BlockSpec((L, 2304), lambda n: (lag(n), 0)), _full((HEADS, L, 2 * L)), _full((1, 128))],
        out_shape=[jax.ShapeDtypeStruct((S, 2304), BF16),
                   jax.ShapeDtypeStruct((HEADS, L, 2 * L), F32), jax.ShapeDtypeStruct((1, 128), F32)],
        scratch_shapes=[pltpu.VMEM((L, 256), F32), pltpu.VMEM((2 * L, 256), F32),
                        pltpu.VMEM((L, 1024), F32), pltpu.VMEM((L, 1024), F32)]
        + [pltpu.VMEM((HEADS * L, DH), F32)] * 3,
        compiler_params=_params(("arbitrary",)),
    )(dy, proj, proj, proj, proj, proj, proj, bias, sinks)


def _sgu_in_specs():
    return [
        pl.BlockSpec((L, 1024), lambda c: (c, 0)),
        pl.BlockSpec((L, 1024), lambda c: (c, 1)),
        pl.BlockSpec((L, 1024), lambda c: (c, 2)),
        _full((1, 1024)), _full((1, 1024)), _full((8, L, L)), _full((L, 8)),
    ]


def _sgu_norm(v, lg, lb):
    mu = jnp.mean(v, axis=-1, keepdims=True)
    vc = v - mu
    rstd = lax.rsqrt(jnp.mean(vc * vc, axis=-1, keepdims=True) + EPS)
    xhat = vc * rstd
    return xhat * lg + lb, xhat, rstd


def _tril():
    return lax.broadcasted_iota(jnp.int32, (L, L), 0) >= lax.broadcasted_iota(jnp.int32, (L, L), 1)


def sgu_fwd(proj, ln_g, ln_b, w, b_t):
    S = proj.shape[0]

    def body(u_ref, v_ref, z_ref, lg_ref, lb_ref, w_ref, bt_ref, y_ref):
        vn, _, _ = _sgu_norm(v_ref[...].astype(F32), lg_ref[...], lb_ref[...])
        tri = _tril()
        parts = []
        for g in range(8):
            wg = jnp.where(tri, w_ref[g], 0.0).astype(BF16)
            parts.append(_dot(wg, vn[:, g * 128:(g + 1) * 128].astype(BF16)) + bt_ref[:, g:g + 1])
        mixed = jnp.concatenate(parts, axis=1)
        z = z_ref[...].astype(F32)
        y_ref[...] = (u_ref[...].astype(F32) * mixed * (z * _sigmoid(z))).astype(BF16)

    return pl.pallas_call(
        body, name="sgu_fwd", grid=(S // L,),
        in_specs=_sgu_in_specs(),
        out_specs=pl.BlockSpec((L, 1024), lambda c: (c, 0)),
        out_shape=jax.ShapeDtypeStruct((S, 1024), BF16),
        compiler_params=_params(("arbitrary",)),
    )(proj, proj, proj, ln_g, ln_b, w, b_t)


def sgu_bwd(dy, proj, ln_g, ln_b, w, b_t):
    S = proj.shape[0]

    def body(dy_ref, u_ref, v_ref, z_ref, lg_ref, lb_ref, w_ref, bt_ref,
             dout_ref, dw_ref, dbt_ref, dlg_ref, dlb_ref):
        @pl.when(pl.program_id(0) == 0)
        def _():
            dw_ref[...] = jnp.zeros_like(dw_ref)
            dbt_ref[...] = jnp.zeros_like(dbt_ref)
            dlg_ref[...] = jnp.zeros_like(dlg_ref)
            dlb_ref[...] = jnp.zeros_like(dlb_ref)

        lg = lg_ref[...]
        vn, xhat, rstd = _sgu_norm(v_ref[...].astype(F32), lg, lb_ref[...])
        tri = _tril()
        lane = lax.broadcasted_iota(jnp.int32, (L, 128), 1)
        wgs, parts = [], []
        for g in range(8):
            wg = jnp.where(tri, w_ref[g], 0.0)
            wgs.append(wg)
            parts.append(_dot(wg.astype(BF16), vn[:, g * 128:(g + 1) * 128].astype(BF16)) + bt_ref[:, g:g + 1])
        mixed = jnp.concatenate(parts, axis=1)
        z = z_ref[...].astype(F32)
        sg = _sigmoid(z)
        silu = z * sg
        dy_v = dy_ref[...]
        u = u_ref[...].astype(F32)
        dout_ref[:, 0:1024] = (dy_v * mixed * silu).astype(BF16)
        dout_ref[:, 2048:3072] = (dy_v * u * mixed * (sg * (1.0 + z * (1.0 - sg)))).astype(BF16)
        dmixed = dy_v * u * silu
        dbt = jnp.zeros((L, 128), F32)
        dvn_parts = []
        for g in range(8):
            dm = dmixed[:, g * 128:(g + 1) * 128]
            dmb = dm.astype(BF16)
            dbt = dbt + jnp.where(lane == g, jnp.sum(dm, axis=1, keepdims=True), 0.0)
            dw_ref[g] += jnp.where(tri, _dot_nt(dmb, vn[:, g * 128:(g + 1) * 128].astype(BF16)), 0.0)
            dvn_parts.append(_dot_tn(wgs[g], dmb))
        dbt_ref[...] += dbt
        dvn = jnp.concatenate(dvn_parts, axis=1)
        dlg_ref[...] += jnp.sum(dvn * xhat, axis=0, keepdims=True)
        dlb_ref[...] += jnp.sum(dvn, axis=0, keepdims=True)
        dxh = dvn * lg
        dv = rstd * (dxh - jnp.mean(dxh, axis=-1, keepdims=True)
                     - xhat * jnp.mean(dxh * xhat, axis=-1, keepdims=True))
        dout_ref[:, 1024:2048] = dv.astype(BF16)

    return pl.pallas_call(
        body, name="sgu_bwd", grid=(S // L,),
        in_specs=[pl.BlockSpec((L, 1024), lambda c: (c, 0))] + _sgu_in_specs(),
        out_specs=[pl.BlockSpec((L, 3072), lambda c: (c, 0)), _full((8, L, L)), _full((L, 128)),
                   _full((1, 1024)), _full((1, 1024))],
        out_shape=[jax.ShapeDtypeStruct((S, 3072), BF16), jax.ShapeDtypeStruct((8, L, L), F32),
                   jax.ShapeDtypeStruct((L, 128), F32), jax.ShapeDtypeStruct((1, 1024), F32),
                   jax.ShapeDtypeStruct((1, 1024), F32)],
        compiler_params=_params(("arbitrary",)),
    )(dy, proj, proj, proj, ln_g, ln_b, w, b_t)


def _expand_matrices():
    e = (np.arange(SSM_W)[None, :] // SSM_P == np.arange(128)[:, None]).astype(np.float32)
    return jnp.asarray(e, BF16), jnp.asarray(e.T, BF16)


def _rows_from(ref, start):
    C = ref.shape[1]
    tiles = ref[...].reshape(17, 8, C)
    q, s = divmod(start, 8)
    if s == 0:
        return tiles[q:q + 16].reshape(L, C)
    rolled = pltpu.roll(tiles, 8 - s, axis=1)
    sub = lax.broadcasted_iota(jnp.int32, (16, 8, C), 1)
    return jnp.where(sub < 8 - s, rolled[q:q + 16], rolled[q + 1:q + 17]).reshape(L, C)


def _ssd_common(ext_ref, cw_ref, cb_ref, dt_raw, dtb, alog):
    taps = [_rows_from(ext_ref, 5 + k) for k in range(CONV_K)]
    pre = cb_ref[...]
    for k in range(CONV_K):
        pre = pre + cw_ref[k:k + 1, :] * taps[k]
    sg_pre = _sigmoid(pre)
    xc = pre * sg_pre
    dt = _softplus(dt_raw + dtb)
    a = -jnp.exp(alog)
    adt = dt * a
    acs = _sel_dot(_tril(), adt, 3)
    return pre, sg_pre, xc, dt, a, acs, taps


def _ssd_in_specs(rev, nc):
    cidx = (lambda c: nc - 1 - c) if rev else (lambda c: c)
    return [
        pl.BlockSpec((L, 2048), lambda c: (cidx(c), 0)),
        pl.BlockSpec((L, 1024), lambda c: (cidx(c), 2)),
        pl.BlockSpec((L, 1024), lambda c: (cidx(c), 3)),
        pl.BlockSpec((L, 1024), lambda c: (cidx(c), 4)),
        pl.BlockSpec((L, 128), lambda c: (cidx(c), 40)),
        _full((8, CONV_C)), _full((1, CONV_C)), _full((1, 128)), _full((1, 128)), _full((1, 128)),
        _full((1, SSM_W)), _full((128, SSM_W)), _full((SSM_W, 128)),
    ]


def ssd_fwd(proj, conv_w, conv_b, dt_bias, a_log, d_skip, norm_g):
    S = proj.shape[0]
    nc = S // L

    def body(z_ref, xa_ref, xb_ref, xc_ref, dt_ref, cw_ref, cb_ref, dtb_ref, alog_ref, dsk_ref, ng_ref,
             ex_ref, ext_ref, y_ref, hs_ref, H, ext, ysc):
        @pl.when(pl.program_id(0) == 0)
        def _():
            H[...] = jnp.zeros_like(H)
            ext[0:8, :] = jnp.zeros((8, CONV_C), F32)

        for k, ref in enumerate((xa_ref, xb_ref, xc_ref)):
            ext[8:8 + L, k * 1024:(k + 1) * 1024] = ref[...].astype(F32)
        pre, sg_pre, xc, dt, a, acs, _ = _ssd_common(ext, cw_ref, cb_ref, dt_ref[...].astype(F32), dtb_ref[...],
                                                     alog_ref[...])
        for k, ref in enumerate((xa_ref, xb_ref, xc_ref)):
            ext[0:8, k * 1024:(k + 1) * 1024] = ref[L - 8:L, :].astype(F32)
        xs = xc[:, 0:SSM_W]
        acs_t = acs.T
        ex = ex_ref[...]
        dt_x = _dot_sel(dt, ex, 2)
        xdt = xs * dt_x
        eacs_x = _dot_sel(jnp.exp(acs), ex, 2)
        xw = xdt * _dot_sel(jnp.exp(acs[L - 1:L, :] - acs), ex, 2)
        cd_row = jnp.exp(acs[L - 1:L, :])
        hs_ref[0] = H[...]
        tri = _tril()
        for g in range(SSM_G):
            gs = slice(g * 512, (g + 1) * 512)
            bg = xc[:, SSM_W + g * SSM_N:SSM_W + (g + 1) * SSM_N].astype(BF16)
            cg = xc[:, SSM_W + 512 + g * SSM_N:SSM_W + 512 + (g + 1) * SSM_N].astype(BF16)
            G = _dot_nt(cg, bg)
            yoff = _dot_nt(cg, H[gs, :].astype(BF16)) * eacs_x[:, gs]
            Sg = _dot_tn(xw[:, gs], bg)
            for j in range(8):
                hh = g * 8 + j
                hs = slice(hh * SSM_P, (hh + 1) * SSM_P)
                seg = acs[:, hh:hh + 1] - acs_t[hh:hh + 1, :]
                dk = jnp.where(tri, jnp.exp(seg), 0.0)
                yd = _dot((G * dk).astype(BF16), xdt[:, hs].astype(BF16))
                ysc[:, hs] = yd + yoff[:, j * SSM_P:(j + 1) * SSM_P]
                H[hs, :] = H[hs, :] * cd_row[:, hh:hh + 1] + Sg[j * SSM_P:(j + 1) * SSM_P, :]
        d_x = _dot_sel(jnp.broadcast_to(dsk_ref[...], (8, 128)), ex, 3)[0:1, :]
        Y = ysc[...] + d_x * xs
        z = z_ref[...].astype(F32)
        yz = Y * (z * _sigmoid(z))
        ng = ng_ref[...]
        for g in range(SSM_G):
            gs = slice(g * 512, (g + 1) * 512)
            t = yz[:, gs]
            rstd = lax.rsqrt(jnp.mean(t * t, axis=-1, keepdims=True) + EPS)
            y_ref[:, gs] = (t * rstd * ng[:, gs]).astype(BF16)

    return pl.pallas_call(
        body, name="ssd_fwd", grid=(nc,),
        in_specs=_ssd_in_specs(False, nc),
        out_specs=[pl.BlockSpec((L, SSM_W), lambda c: (c, 0)), pl.BlockSpec((1, SSM_W, SSM_N), lambda c: (c, 0, 0))],
        out_shape=[jax.ShapeDtypeStruct((S, SSM_W), BF16), jax.ShapeDtypeStruct((nc, SSM_W, SSM_N), F32)],
        scratch_shapes=[pltpu.VMEM((SSM_W, SSM_N), F32), pltpu.VMEM((8 + L, CONV_C), F32),
                        pltpu.VMEM((L, SSM_W), F32)],
        compiler_params=_params(("arbitrary",)),
    )(proj, proj, proj, proj, proj, conv_w, conv_b, dt_bias, a_log, d_skip, norm_g, *_expand_matrices())


def ssd_bwd(dy, proj, hstates, conv_w, conv_b, dt_bias, a_log, d_skip, norm_g):
    S = proj.shape[0]
    nc = S // L
    cidx = lambda c: nc - 1 - c

    def body(dy_ref, z_ref, xa_ref, xb_ref, xc_ref, dt_ref, cw_ref, cb_ref, dtb_ref, alog_ref, dsk_ref, ng_ref,
             ex_ref, ext_ref, pa_ref, pb_ref, pc_ref, hp_ref,
             dout_ref, dcw_ref, dcb_ref, ddtb_ref, dalog_ref, ddsk_ref, dng_ref,
             dH, ext, dext, ysc, yoffsc, dxdt, dxc, tsc, rsum, csum):
        step = pl.program_id(0)
        c = nc - 1 - step

        @pl.when(step == 0)
        def _():
            dH[...] = jnp.zeros_like(dH)
            dext[L:L + 8, :] = jnp.zeros((8, CONV_C), F32)
            rsum[...] = jnp.zeros_like(rsum)
            csum[...] = jnp.zeros_like(csum)
            for r in (dcw_ref, dcb_ref, ddtb_ref, dalog_ref, ddsk_ref, dng_ref):
                r[...] = jnp.zeros_like(r)

        for k, (ref, prev) in enumerate(((xa_ref, pa_ref), (xb_ref, pb_ref), (xc_ref, pc_ref))):
            ext[0:8, k * 1024:(k + 1) * 1024] = jnp.where(c > 0, prev[8:16, :].astype(F32), 0.0)
            ext[8:8 + L, k * 1024:(k + 1) * 1024] = ref[...].astype(F32)
        dtb = dtb_ref[...]
        dt_raw = dt_ref[...].astype(F32)
        pre, sg_pre, xc, dt, a, acs, taps = _ssd_common(ext, cw_ref, cb_ref, dt_raw, dtb, alog_ref[...])
        xs = xc[:, 0:SSM_W]
        acs_t = acs.T
        ex = ex_ref[...]
        dt_x = _dot_sel(dt, ex, 2)
        xdt = xs * dt_x
        eacs_x = _dot_sel(jnp.exp(acs), ex, 2)
        dte_x = _dot_sel(jnp.exp(acs[L - 1:L, :] - acs), ex, 2)
        xw = xdt * dte_x
        cd_row = jnp.exp(acs[L - 1:L, :])
        tri = _tril()

        Gs, Cs, Bs = [], [], []
        for g in range(SSM_G):
            gs = slice(g * 512, (g + 1) * 512)
            bg = xc[:, SSM_W + g * SSM_N:SSM_W + (g + 1) * SSM_N].astype(BF16)
            cg = xc[:, SSM_W + 512 + g * SSM_N:SSM_W + 512 + (g + 1) * SSM_N].astype(BF16)
            G = _dot_nt(cg, bg)
            Gs.append(G), Cs.append(cg), Bs.append(bg)
            yoffsc[:, gs] = _dot_nt(cg, hp_ref[0, gs, :].astype(BF16)) * eacs_x[:, gs]
            for j in range(8):
                hh = g * 8 + j
                hs = slice(hh * SSM_P, (hh + 1) * SSM_P)
                seg = acs[:, hh:hh + 1] - acs_t[hh:hh + 1, :]
                dk = jnp.where(tri, jnp.exp(seg), 0.0)
                ysc[:, hs] = _dot((G * dk).astype(BF16), xdt[:, hs].astype(BF16))
        d_x = _dot_sel(jnp.broadcast_to(dsk_ref[...], (8, 128)), ex, 3)[0:1, :]
        yoff = yoffsc[...]
        Y = ysc[...] + yoff + d_x * xs

        z = z_ref[...].astype(F32)
        sgz = _sigmoid(z)
        silu_z = z * sgz
        yz = Y * silu_z
        ng = ng_ref[...]
        dout = dy_ref[...]
        dyn = dout * ng
        dyz_parts, dng_parts = [], []
        for g in range(SSM_G):
            gs = slice(g * 512, (g + 1) * 512)
            t = yz[:, gs]
            rstd = lax.rsqrt(jnp.mean(t * t, axis=-1, keepdims=True) + EPS)
            dng_parts.append(jnp.sum(dout[:, gs] * t * rstd, axis=0, keepdims=True))
            dn = dyn[:, gs]
            dyz_parts.append(rstd * dn - t * (rstd * rstd * rstd) * jnp.mean(dn * t, axis=-1, keepdims=True))
        dng_ref[...] += jnp.concatenate(dng_parts, axis=1)
        dyz = jnp.concatenate(dyz_parts, axis=1)
        dY = dyz * silu_z
        dout_ref[:, 0:SSM_W] = (dyz * Y * (sgz * (1.0 + z * (1.0 - sgz)))).astype(BF16)

        ex_t = ext_ref[...]
        ddsk_ref[...] += _dot_sel(jnp.broadcast_to(jnp.sum(dY * xs, axis=0, keepdims=True), (8, SSM_W)), ex_t, 3)[0:1, :]

        lane = lax.broadcasted_iota(jnp.int32, (L, 128), 1)
        last_col = lax.broadcasted_iota(jnp.int32, (1, L), 1) == L - 1
        for g in range(SSM_G):
            gs = slice(g * 512, (g + 1) * 512)
            G, cg, bg = Gs[g], Cs[g], Bs[g]
            hp_g = hp_ref[0, gs, :]
            dh_g = dH[gs, :]
            dY_g = dY[:, gs]
            dZ = dY_g * eacs_x[:, gs]
            dZb = dZ.astype(BF16)
            dC = _dot(dZb, hp_g.astype(BF16))
            dh_from_off = _dot_tn(dZ, cg)
            dhb = dh_g.astype(BF16)
            Q = _dot_nt(bg, dhb)
            dB = _dot(xw[:, gs].astype(BF16), dhb)
            qd = Q * dte_x[:, gs]
            dxdt[:, gs] = qd
            tsc[:, gs] = qd * xdt[:, gs]
            dG = jnp.zeros((L, L), F32)
            for j in range(8):
                hh = g * 8 + j
                hs = slice(hh * SSM_P, (hh + 1) * SSM_P)
                seg = acs[:, hh:hh + 1] - acs_t[hh:hh + 1, :]
                dk = jnp.where(tri, jnp.exp(seg), 0.0)
                M = G * dk
                dYh = dY[:, hs]
                dYhb = dYh.astype(BF16)
                dM = _dot_nt(dYhb, xdt[:, hs].astype(BF16))
                dxdt[:, hs] += _dot_tn(M, dYhb)
                dG = dG + dM * dk
                Wm = dM * M
                pj = slice(j * SSM_P, (j + 1) * SSM_P)
                cd_h = cd_row[:, hh:hh + 1]
                dcd = jnp.sum(dh_g[pj, :] * hp_g[pj, :]) * cd_h
                rsum[:, hh:hh + 1] = jnp.sum(Wm, axis=1, keepdims=True)
                csum[hh:hh + 1, :] = jnp.sum(Wm, axis=0, keepdims=True) - jnp.where(last_col, dcd, 0.0)
                dH[hs, :] = dh_g[pj, :] * cd_h + dh_from_off[pj, :]
            dGb = dG.astype(BF16)
            dC = dC + _dot(dGb, bg)
            dB = dB + _dot_tn(dG, cg)
            dxc[:, SSM_W + g * SSM_N:SSM_W + (g + 1) * SSM_N] = dB
            dxc[:, SSM_W + 512 + g * SSM_N:SSM_W + 512 + (g + 1) * SSM_N] = dC

        row = lax.broadcasted_iota(jnp.int32, (L, 128), 0)
        tv = tsc[...]
        t_last = _dot_sel(jnp.broadcast_to(jnp.sum(tv, axis=0, keepdims=True), (8, SSM_W)), ex_t, 3)[0:1, :]
        dacs = (rsum[...] - csum[...].T + _dot_sel(dY * yoff - tv, ex_t, 2) + jnp.where(row == L - 1, t_last, 0.0))
        triu = lax.broadcasted_iota(jnp.int32, (L, L), 0) <= lax.broadcasted_iota(jnp.int32, (L, L), 1)
        dadt = _sel_dot(triu, dacs, 3)
        dxdt_v = dxdt[...]
        ddt = _dot_sel(dxdt_v * xs, ex_t, 1) + dadt * a
        dalog_ref[...] += jnp.sum(dadt * dt * a, axis=0, keepdims=True)
        ddt_raw = jnp.where(lane < SSM_H, ddt * _sigmoid(dt_raw + dtb), 0.0)
        ddtb_ref[...] += jnp.sum(ddt_raw, axis=0, keepdims=True)
        dout_ref[:, 5120:5248] = ddt_raw.astype(BF16)
        dout_ref[:, 5248:5376] = jnp.zeros((L, 128), BF16)

        dxc[:, 0:SSM_W] = dxdt_v * dt_x + d_x * dY
        dpre = dxc[...] * (sg_pre * (1.0 + pre * (1.0 - sg_pre)))
        dcb_ref[...] += jnp.sum(dpre, axis=0, keepdims=True)
        dext[0:L, :] = dpre
        x_cur = ext[8:8 + L, :]
        dx = None
        for k in range(CONV_K):
            dsh = _rows_from(dext, 3 - k)
            term = cw_ref[k:k + 1, :] * dsh
            dx = term if dx is None else dx + term
            dcw_ref[k:k + 1, :] += jnp.sum(dsh * x_cur, axis=0, keepdims=True)
        dout_ref[:, SSM_W:SSM_W + CONV_C] = dx.astype(BF16)
        dext[L:L + 8, :] = dpre[0:8, :]

    big = lambda w: pl.BlockSpec((L, w), lambda c: (cidx(c), 0))
    return pl.pallas_call(
        body, name="ssd_bwd", grid=(nc,),
        in_specs=[big(SSM_W)] + _ssd_in_specs(True, nc) + [
            pl.BlockSpec((16, 1024), lambda c, k=k: (jnp.maximum(8 * cidx(c) - 1, 0), k)) for k in (2, 3, 4)] + [
            pl.BlockSpec((1, SSM_W, SSM_N), lambda c: (cidx(c), 0, 0))],
        out_specs=[big(5376), _full((8, CONV_C)), _full((1, CONV_C)),
                   _full((1, 128)), _full((1, 128)), _full((1, 128)), _full((1, SSM_W))],
        out_shape=[jax.ShapeDtypeStruct((S, 5376), BF16), jax.ShapeDtypeStruct((8, CONV_C), F32),
                   jax.ShapeDtypeStruct((1, CONV_C), F32), jax.ShapeDtypeStruct((1, 128), F32),
                   jax.ShapeDtypeStruct((1, 128), F32), jax.ShapeDtypeStruct((1, 128), F32),
                   jax.ShapeDtypeStruct((1, SSM_W), F32)],
        scratch_shapes=[pltpu.VMEM((SSM_W, SSM_N), F32), pltpu.VMEM((8 + L, CONV_C), F32),
                        pltpu.VMEM((L + 8, CONV_C), F32), pltpu.VMEM((L, SSM_W), F32),
                        pltpu.VMEM((L, SSM_W), F32), pltpu.VMEM((L, SSM_W), F32),
                        pltpu.VMEM((L, CONV_C), F32), pltpu.VMEM((L, SSM_W), F32),
                        pltpu.VMEM((L, 128), F32), pltpu.VMEM((128, L), F32)],
        compiler_params=_params(("arbitrary",)),
    )(dy, proj, proj, proj, proj, proj, conv_w, conv_b, dt_bias, a_log, d_skip, norm_g, *_expand_matrices(),
      proj, proj, proj, hstates)


def _resident(shape):
    nd = len(shape)
    return pl.BlockSpec(shape, lambda *_: (0,) * nd, pipeline_mode=pl.Buffered(1))


def merge_fwd(y_att, y_sg, y_ssm, proj, x, w_a, w_s, w_m, w_o, g_post):
    S = x.shape[0]
    tm = 256

    def body(ya_ref, ys_ref, ym_ref, gate_ref, x_ref, wa_ref, ws_ref, wm_ref, wo_ref, gp_ref,
             xn_ref, bra_ref, brs_ref, brm_ref, mg_ref, out_ref):
        bra = _dot(ya_ref[...], wa_ref[...])
        brs = _dot(ys_ref[...], ws_ref[...])
        brm = _dot(ym_ref[...], wm_ref[...])
        bra_ref[...] = bra.astype(BF16)
        brs_ref[...] = brs.astype(BF16)
        brm_ref[...] = brm.astype(BF16)
        gate = gate_ref[...].astype(F32)
        merged = (_sigmoid(gate[:, 0:1024]) * bra + _sigmoid(gate[:, 1024:2048]) * brs
                  + _sigmoid(gate[:, 2048:3072]) * brm)
        mb = merged.astype(BF16)
        mg_ref[...] = mb
        o = _dot(mb, wo_ref[...])
        out_ref[...] = o
        r = lax.rsqrt(jnp.mean(o * o, axis=-1, keepdims=True) + EPS)
        xn_ref[...] = x_ref[...] + o * r * gp_ref[...]

    row = lambda w: pl.BlockSpec((tm, w), lambda i: (i, 0))
    return pl.pallas_call(
        body, name="merge_fwd", grid=(S // tm,),
        in_specs=[row(1024), row(1024), row(2048), pl.BlockSpec((tm, 3072), lambda i: (i, 0)),
                  row(D), _resident((1024, D)), _resident((1024, D)), _resident((2048, D)), _resident((D, D)),
                  _full((1, D))],
        out_specs=[row(D)] * 6,
        out_shape=[jax.ShapeDtypeStruct((S, D), F32)] + [jax.ShapeDtypeStruct((S, D), BF16)] * 4
        + [jax.ShapeDtypeStruct((S, D), F32)],
        compiler_params=_params(("arbitrary",)),
    )(y_att, y_sg, y_ssm, proj, x, w_a, w_s, w_m, w_o, g_post)


def merge_bwd(dy, out, g_post, proj, br_a, br_s, br_m, w_a, w_s, w_m, w_o):
    S = dy.shape[0]
    tm = 256

    def body(dy_ref, o_ref, gp_ref, gate_ref, bra_ref, brs_ref, brm_ref, wa_ref, ws_ref, wm_ref, wo_ref,
             dout_ref, dba_ref, dbs_ref, dbm_ref, dgate_ref, dya_ref, dys_ref, dym_ref, dgp_ref):
        @pl.when(pl.program_id(0) == 0)
        def _():
            dgp_ref[...] = jnp.zeros_like(dgp_ref)

        o = o_ref[...]
        dyv = dy_ref[...]
        r = lax.rsqrt(jnp.mean(o * o, axis=-1, keepdims=True) + EPS)
        dyg = dyv * gp_ref[...]
        do = r * dyg - o * (r * r * r) * jnp.mean(dyg * o, axis=-1, keepdims=True)
        dgp_ref[...] += jnp.sum(dyv * o * r, axis=0, keepdims=True)
        dob = do.astype(BF16)
        dout_ref[...] = dob
        dmerged = _dot_nt(dob, wo_ref[...])
        for idx, (br_ref, dbr_ref, w_ref, dyi_ref) in enumerate((
                (bra_ref, dba_ref, wa_ref, dya_ref), (brs_ref, dbs_ref, ws_ref, dys_ref),
                (brm_ref, dbm_ref, wm_ref, dym_ref))):
            s = _sigmoid(gate_ref[:, idx * 1024:(idx + 1) * 1024].astype(F32))
            dbr = (dmerged * s).astype(BF16)
            dbr_ref[...] = dbr
            dgate_ref[:, idx * 1024:(idx + 1) * 1024] = (dmerged * br_ref[...].astype(F32) * s * (1.0 - s)).astype(BF16)
            dyi_ref[...] = _dot_nt(dbr, w_ref[...])

    row = lambda w: pl.BlockSpec((tm, w), lambda i: (i, 0))
    return pl.pallas_call(
        body, name="merge_bwd", grid=(S // tm,),
        in_specs=[row(D), row(D), _full((1, D)), pl.BlockSpec((tm, 3072), lambda i: (i, 0)),
                  row(D), row(D), row(D),
                  _resident((1024, D)), _resident((1024, D)), _resident((2048, D)), _resident((D, D))],
        out_specs=[row(D), row(D), row(D), row(D), row(3072), row(1024), row(1024), row(2048), _full((1, D))],
        out_shape=[jax.ShapeDtypeStruct((S, D), BF16)] * 4 + [
            jax.ShapeDtypeStruct((S, 3072), BF16), jax.ShapeDtypeStruct((S, 1024), F32),
            jax.ShapeDtypeStruct((S, 1024), F32), jax.ShapeDtypeStruct((S, 2048), F32),
            jax.ShapeDtypeStruct((1, D), F32)],
        compiler_params=_params(("arbitrary",)),
    )(dy, out, g_post, proj, br_a, br_s, br_m, w_a, w_s, w_m, w_o)


def loss_head(y, target):
    S = y.shape[0]
    tm = 512

    def body(y_ref, t_ref, dy_ref, loss_ref):
        @pl.when(pl.program_id(0) == 0)
        def _():
            loss_ref[...] = jnp.zeros_like(loss_ref)
        e = y_ref[...] - t_ref[...]
        dy_ref[...] = e * (1.0 / D)
        loss_ref[...] += 0.5 * jnp.sum(jnp.mean(e * e, axis=-1, keepdims=True))

    row = pl.BlockSpec((tm, D), lambda i: (i, 0))
    return pl.pallas_call(
        body, name="loss_head", grid=(S // tm,),
        in_specs=[row, row], out_specs=[row, _full((1, 128))],
        out_shape=[jax.ShapeDtypeStruct((S, D), F32), jax.ShapeDtypeStruct((1, 128), F32)],
        compiler_params=_params(("arbitrary",)),
    )(y, target)


def _adam(w, g, m, v):
    mn = ADAM_B1 * m + (1.0 - ADAM_B1) * g
    vn = ADAM_B2 * v + (1.0 - ADAM_B2) * (g * g)
    m_hat = mn / (1.0 - ADAM_B1 ** ADAM_STEP)
    v_hat = vn / (1.0 - ADAM_B2 ** ADAM_STEP)
    return -ADAM_LR * (m_hat / (jnp.sqrt(v_hat) + ADAM_EPS) + ADAM_WD * w), mn, vn


def adamw_big(w, m, v, halves0, sum1, cc, name, tr):
    _, R, C = w.shape
    nper = R // tr
    f, fb, n0, off_a, off_b = halves0
    p, pb, off1 = sum1

    def body(c_ref, w_ref, m_ref, v_ref, f_ref, fb_ref, p_ref, pb_ref, g_ref, d_ref, nm_ref, nv_ref):
        i = pl.program_id(0)
        half = jnp.where(i % nper >= n0, 1, 0)
        g0 = jnp.where(c_ref[0] == half, f_ref[...], fb_ref[...])
        g = jnp.where(i < nper, g0, p_ref[...] + pb_ref[...])
        g_ref[0] = g
        d_ref[0], nm_ref[0], nv_ref[0] = _adam(w_ref[0], g, m_ref[0], v_ref[0])

    def blk0(i, c):
        il = jnp.minimum(i, nper - 1)
        return (jnp.where(il >= n0, off_b + il - n0, off_a + il), 0)

    wblk = pl.BlockSpec((1, tr, C), lambda i, c: (i // nper, i % nper, 0))
    b0 = pl.BlockSpec((tr, C), blk0)
    b1 = pl.BlockSpec((tr, C), lambda i, c: (off1 + jnp.maximum(i - nper, 0), 0))
    grid_spec = pltpu.PrefetchScalarGridSpec(
        num_scalar_prefetch=1, grid=(2 * nper,),
        in_specs=[wblk, wblk, wblk, b0, b0, b1, b1], out_specs=[wblk] * 4)
    return pl.pallas_call(
        body, name=name, grid_spec=grid_spec,
        out_shape=[jax.ShapeDtypeStruct(w.shape, F32)] * 4,
        compiler_params=_params(("arbitrary",)),
    )(cc, w, m, v, f, fb, p, pb)


def adamw_plain(w, g, m, v, name):
    def body(w_ref, g_ref, m_ref, v_ref, d_ref, nm_ref, nv_ref):
        d_ref[...], nm_ref[...], nv_ref[...] = _adam(w_ref[...], g_ref[...], m_ref[...], v_ref[...])

    return pl.pallas_call(
        body, name=name, out_shape=[jax.ShapeDtypeStruct(w.shape, F32)] * 3, compiler_params=_params(),
    )(w, g, m, v)


SMALL = {"norm_pre": ("g_pre", 8), "norm_post": ("g_post", 8), "att_sinks": ("sinks", 8), "sg_ln_g": ("ln_g", 8),
         "sg_ln_b": ("ln_b", 8), "sg_w": ("sg_w", 1024), "sg_b": ("sg_bt", 8), "ssm_conv_b": ("conv_b", 24),
         "ssm_dt_bias": ("dt_bias", 8), "ssm_a_log": ("a_log", 8), "ssm_d": ("d_skip", 8), "ssm_norm_g": ("norm_g", 16)}
SMALL_LAYER_ROWS = sum(r for _, r in SMALL.values())
REL_ROW = DEPTH * SMALL_LAYER_ROWS
LOSS_ROW = REL_ROW + 32
SMALL_ROWS = LOSS_ROW + 8


def _small_rows():
    rows, r = {}, 0
    for l in range(DEPTH):
        for name, (_, n) in SMALL.items():
            rows[(l, name)] = r
            r += n
    return rows


def adamw_small(red, rel, small):
    names = list(SMALL) + ["rel_bias"]
    params = dict(small, rel_bias=rel)
    rows = _small_rows()

    def grad_of(red_ref, l, name, n):
        r0 = rows[(l, name)]
        if name == "sg_b":
            return red_ref[r0:r0 + 8, :]
        if n < 128:
            return red_ref[r0:r0 + 1, 0:n]
        return jnp.concatenate([red_ref[r0 + j:r0 + j + 1, :] for j in range(n // 128)], axis=1)

    def body(red_ref, *refs):
        ins, outs = refs[:3 * len(names)], refs[3 * len(names):]
        for i, name in enumerate(names):
            w_ref, m_ref, v_ref = ins[3 * i:3 * i + 3]
            o = outs[4 * i:4 * i + 4]
            if name == "rel_bias":
                g = red_ref[REL_ROW:REL_ROW + 32, 0:16]
                o[0][...] = g
                o[1][...], o[2][...], o[3][...] = _adam(w_ref[...], g, m_ref[...], v_ref[...])
                continue
            for l in range(DEPTH):
                if name == "sg_w":
                    for grp in range(8):
                        r0 = rows[(l, name)] + grp * 128
                        g = red_ref[r0:r0 + 128, :]
                        o[0][l, grp] = g
                        o[1][l, grp], o[2][l, grp], o[3][l, grp] = _adam(w_ref[l, grp], g, m_ref[l, grp], v_ref[l, grp])
                elif name == "sg_b":
                    g = grad_of(red_ref, l, name, 128)
                    o[0][l] = g
                    o[1][l], o[2][l], o[3][l] = _adam(w_ref[l], g, m_ref[l], v_ref[l])
                else:
                    sl = slice(l, l + 1)
                    g = grad_of(red_ref, l, name, w_ref.shape[-1])
                    o[0][sl, :] = g
                    o[1][sl, :], o[2][sl, :], o[3][sl, :] = _adam(w_ref[sl, :], g, m_ref[sl, :], v_ref[sl, :])

    flat_in = [a for name in names for a in params[name]]
    out_shape = [jax.ShapeDtypeStruct(params[name][0].shape, F32) for name in names for _ in range(4)]
    res = pl.pallas_call(body, name="adamw_small", out_shape=out_shape, compiler_params=_params())(red, *flat_in)
    return {name: tuple(res[4 * i:4 * i + 4]) for i, name in enumerate(names)}


ANY = pl.BlockSpec(memory_space=pl.ANY)


def _place():
    x, y, c = lax.axis_index("x"), lax.axis_index("y"), lax.axis_index("c")
    others = [(1 - x, y), (x, 1 - y), (1 - x, 1 - y)]
    return x, y, c, others


def _rcopy(src, dst, ssem, rsem, to):
    return pltpu.make_async_remote_copy(src_ref=src, dst_ref=dst, send_sem=ssem, recv_sem=rsem,
                                        device_id=to, device_id_type=MESH)


def gather_weights(arrs):
    n = len(arrs)

    def body(*refs):
        srcs, outs, ssem, rsem = refs[:n], refs[n:2 * n], refs[2 * n], refs[2 * n + 1]
        x, y, c, others = _place()
        me = 2 * x + y
        sib = (x, y, 1 - c)
        first = [_rcopy(srcs[i].at[c], outs[i].at[c, me], ssem.at[6 * i + k], rsem.at[6 * i + k], (ox, oy, c))
                 for i in range(n) for k, (ox, oy) in enumerate(others)]
        for cp in first:
            cp.start()
        passed = []
        for k, (ox, oy) in enumerate(others):
            for i in range(n):
                slot = outs[i].at[c, 2 * ox + oy]
                _rcopy(slot, slot, ssem.at[6 * i + k], rsem.at[6 * i + k], sib).wait_recv()
                fw = _rcopy(slot, slot, ssem.at[6 * i + 3 + k], rsem.at[6 * i + 3 + k], sib)
                fw.start()
                passed.append(fw)
        for k, (ox, oy) in enumerate(others):
            for i in range(n):
                slot = outs[i].at[1 - c, 2 * ox + oy]
                _rcopy(slot, slot, ssem.at[6 * i + 3 + k], rsem.at[6 * i + 3 + k], sib).wait_recv()
        for cp in first + passed:
            cp.wait_send()

    return pl.pallas_call(
        body, name="gather_weights",
        in_specs=[ANY] * n, out_specs=[ANY] * n,
        out_shape=[jax.ShapeDtypeStruct((2, SHARDS) + a.shape[1:], a.dtype) for a in arrs],
        scratch_shapes=[pltpu.SemaphoreType.DMA((6 * n,)), pltpu.SemaphoreType.DMA((6 * n,))],
    )(*arrs)


HBM = pl.BlockSpec(memory_space=pltpu.HBM)
SEM = pl.BlockSpec(memory_space=pltpu.SEMAPHORE)
EFFECT = pltpu.SideEffectType.DATAFLOW_SIDE_EFFECTING


def _in_hbm(a):
    return pltpu.with_memory_space_constraint(a, pltpu.HBM)


def gather_start(srcs, after, name, by_dest=False):
    n = len(srcs)
    lands = [_in_hbm(lax.empty((SHARDS,) + a.shape[-2:], a.dtype)) for a in srcs]
    na = len(after)

    def body(*refs):
        src, land = refs[:n], refs[n:2 * n]
        ssem, rsem, token = refs[2 * n + na], refs[2 * n + na + 1], refs[-1]
        x, y, c, others = _place()
        me = 2 * x + y
        for i in range(n):
            for k, (ox, oy) in enumerate(others):
                s = src[i].at[2 * ox + oy] if by_dest else src[i]
                _rcopy(s, land[i].at[me], ssem.at[3 * i + k], rsem.at[3 * i + k], (ox, oy, c)).start()
        token[...] = jnp.zeros_like(token)

    bufs = [_in_hbm(a) for a in srcs] + lands
    out = pl.pallas_call(
        body, name=name,
        out_shape=(pltpu.SemaphoreType.DMA((3 * n,)), pltpu.SemaphoreType.DMA((3 * n,)),
                   *[pltpu.HBM(b.shape, b.dtype) for b in bufs], jax.ShapeDtypeStruct((8, 128), F32)),
        in_specs=[HBM] * (2 * n) + [ANY] * na,
        out_specs=(SEM, SEM, *[HBM] * (2 * n), pl.BlockSpec(memory_space=pltpu.VMEM)),
        input_output_aliases={i: 2 + i for i in range(2 * n)},
        compiler_params=pltpu.CompilerParams(has_side_effects=EFFECT),
    )(*bufs, *after)
    return out[0], out[1], list(out[2:2 + n]), list(out[2 + n:2 + 2 * n]), out[-1]


def gather_wait(ssem, rsem, srcs, lands, after, name, by_dest=False):
    n = len(srcs)

    def body(*refs):
        src, land = refs[:n], refs[n:2 * n]
        s_sem, r_sem = refs[2 * n], refs[2 * n + 1]
        x, y, c, others = _place()
        for i in range(n):
            for k, (ox, oy) in enumerate(others):
                s = src[i].at[2 * ox + oy] if by_dest else src[i]
                cp = _rcopy(s, land[i].at[2 * ox + oy], s_sem.at[3 * i + k], r_sem.at[3 * i + k], (ox, oy, c))
                cp.wait_send()
                cp.wait_recv()

    bufs = list(srcs) + list(lands)
    out = pl.pallas_call(
        body, name=name,
        out_shape=tuple(pltpu.HBM(b.shape, b.dtype) for b in bufs),
        in_specs=[HBM] * (2 * n) + [SEM, SEM, ANY],
        out_specs=tuple([HBM] * (2 * n)),
        input_output_aliases={i: i for i in range(2 * n)},
        compiler_params=pltpu.CompilerParams(has_side_effects=EFFECT),
    )(*bufs, ssem, rsem, after)
    return list(out[n:2 * n])


def grad_sibling_exchange(arrs):
    n = len(arrs)

    def body(*refs):
        srcs, outs, ssem, rsem = refs[:n], refs[n:2 * n], refs[2 * n], refs[2 * n + 1]
        x, y, c, _ = _place()
        cps = [_rcopy(srcs[i].at[1 - c], outs[i], ssem.at[i], rsem.at[i], (x, y, 1 - c)) for i in range(n)]
        for cp in cps:
            cp.start()
        for cp in cps:
            cp.wait()

    return pl.pallas_call(
        body, name="grad_sibling_exchange",
        in_specs=[ANY] * n, out_specs=[ANY] * n,
        out_shape=[jax.ShapeDtypeStruct(a.shape[1:], F32) for a in arrs],
        scratch_shapes=[pltpu.SemaphoreType.DMA((n,)), pltpu.SemaphoreType.DMA((n,))],
    )(*arrs)


def grad_chip_sum(g, sb, cc, tr, name):
    _, _, R, C = g.shape
    blk = pl.BlockSpec((1, tr, C), lambda s, r, c: (s, r, 0))
    grid_spec = pltpu.PrefetchScalarGridSpec(
        num_scalar_prefetch=1, grid=(SHARDS, R // tr),
        in_specs=[pl.BlockSpec((1, 1, tr, C), lambda s, r, c: (c[0], s, r, 0)), blk],
        out_specs=[blk, blk])

    def body(c_ref, a_ref, b_ref, o_ref, ob_ref):
        t = a_ref[0] + b_ref[...]
        o_ref[...] = t
        ob_ref[...] = t.astype(BF16)

    return pl.pallas_call(
        body, name=name, grid_spec=grid_spec,
        out_shape=[jax.ShapeDtypeStruct((SHARDS, R, C), F32), jax.ShapeDtypeStruct((SHARDS, R, C), BF16)],
        compiler_params=_params(("arbitrary", "arbitrary")),
    )(cc, g, sb)


def grad_shard_sum(t, rb, me, tr, name):
    _, R, C = t.shape
    grid_spec = pltpu.PrefetchScalarGridSpec(
        num_scalar_prefetch=1, grid=(R // tr,),
        in_specs=[pl.BlockSpec((1, tr, C), lambda r, m: (m[0], r, 0)),
                  pl.BlockSpec((SHARDS, tr, C), lambda r, m: (0, r, 0))],
        out_specs=pl.BlockSpec((tr, C), lambda r, m: (r, 0)))

    def body(m_ref, t_ref, r_ref, o_ref):
        part = [jnp.where(m_ref[0] == s, t_ref[0], r_ref[s].astype(F32)) for s in range(SHARDS)]
        o_ref[...] = ((part[0] + part[1]) + part[2]) + part[3]

    return pl.pallas_call(
        body, name=name, grid_spec=grid_spec,
        out_shape=jax.ShapeDtypeStruct((R, C), F32),
        compiler_params=_params(("arbitrary",)),
    )(me, t, rb)


def grad_sibling_share(arrs, name):
    n = len(arrs)

    def body(*refs):
        srcs, outs, ssem, rsem = refs[:n], refs[n:2 * n], refs[2 * n], refs[2 * n + 1]
        x, y, c, _ = _place()
        cps = [_rcopy(srcs[i], outs[i], ssem.at[i], rsem.at[i], (x, y, 1 - c)) for i in range(n)]
        for cp in cps:
            cp.start()
        for cp in cps:
            cp.wait()

    return pl.pallas_call(
        body, name=name,
        in_specs=[ANY] * n, out_specs=[ANY] * n,
        out_shape=[jax.ShapeDtypeStruct(a.shape, F32) for a in arrs],
        scratch_shapes=[pltpu.SemaphoreType.DMA((n,)), pltpu.SemaphoreType.DMA((n,))],
    )(*arrs)


def _allreduce_rows(src, sib_buf, chips, out_ref, ssem, rsem):
    x, y, c, others = _place()
    me = 2 * x + y
    cp = _rcopy(src, sib_buf, ssem.at[0], rsem.at[0], (x, y, 1 - c))
    cp.start()
    cp.wait()
    chips[me] = src[...] + sib_buf[...]
    sends = [_rcopy(chips.at[me], chips.at[me], ssem.at[1 + k], rsem.at[1 + k], (ox, oy, c))
             for k, (ox, oy) in enumerate(others)]
    for s in sends:
        s.start()
    for k, (ox, oy) in enumerate(others):
        slot = chips.at[2 * ox + oy]
        _rcopy(slot, slot, ssem.at[1 + k], rsem.at[1 + k], (ox, oy, c)).wait_recv()
    for s in sends:
        s.wait_send()
    out_ref[...] = ((chips[0] + chips[1]) + chips[2]) + chips[3]


def _allreduce_scratch(rows):
    return [pltpu.VMEM((rows, 128), F32), pltpu.VMEM((SHARDS, rows, 128), F32),
            pltpu.SemaphoreType.DMA((4,)), pltpu.SemaphoreType.DMA((4,))]


def allreduce_rows(buf, name):
    rows = buf.shape[0]
    VM = pl.BlockSpec(memory_space=pltpu.VMEM)

    def body(src_ref, out_ref, sib_buf, chips, ssem, rsem):
        _allreduce_rows(src_ref, sib_buf, chips, out_ref, ssem, rsem)

    return pl.pallas_call(
        body, name=name, in_specs=[VM], out_specs=VM,
        out_shape=jax.ShapeDtypeStruct((rows, 128), F32),
        scratch_shapes=_allreduce_scratch(rows), compiler_params=_params(),
    )(buf)


def small_allreduce(grads, rel, loss_part):
    rows = _small_rows()
    keys = [(l, name) for l in range(DEPTH) for name in SMALL]
    flat = [grads[l][SMALL[name][0]] for l, name in keys] + [rel, loss_part]

    def body(*refs):
        ins = refs[:len(flat)]
        out_ref, src, sib_buf, chips, ssem, rsem = refs[len(flat):]
        src[...] = jnp.zeros_like(src)
        for (l, name), ref in zip(keys, ins):
            r0 = rows[(l, name)]
            if name == "sg_w":
                for grp in range(8):
                    src[r0 + grp * 128:r0 + (grp + 1) * 128, :] = ref[grp]
            elif name == "sg_b":
                src[r0:r0 + 8, :] = ref[...].T[0:8, :]
            else:
                for j in range(ref.shape[1] // 128):
                    src[r0 + j:r0 + j + 1, :] = ref[:, j * 128:(j + 1) * 128]
        src[REL_ROW:REL_ROW + 32, 0:16] = ins[-2][...]
        src[LOSS_ROW:LOSS_ROW + 1, :] = ins[-1][...]
        _allreduce_rows(src, sib_buf, chips, out_ref, ssem, rsem)

    return pl.pallas_call(
        body, name="small_allreduce",
        out_shape=jax.ShapeDtypeStruct((SMALL_ROWS, 128), F32),
        scratch_shapes=[pltpu.VMEM((SMALL_ROWS, 128), F32)] + _allreduce_scratch(SMALL_ROWS),
        compiler_params=_params(),
    )(*flat)


def _pad_lanes(v):
    return jnp.zeros((1, 128), F32).at[0, :v.shape[0]].set(v)


def layer_fwd(x, wts, bias):
    wt = wts["wt"]
    tn = {name: t for name, _, t in GROUPS}
    p_gate, h = inproj_first(x, wts["g_pre"], wt["gate"], tn["gate"], "inproj_gate")
    p_sgu, p_att, p_ssd = (inproj_group(h, wt[n], tn[n], "inproj_" + n, F32 if n == "att" else BF16)
                           for n in ("sgu", "att", "ssd"))
    y_att = att_fwd(p_att, bias, wts["sinks"])
    y_sg = sgu_fwd(p_sgu, wts["ln_g"], wts["ln_b"], wts["sg_w"], wts["sg_bt"])
    y_ssm, hst = ssd_fwd(p_ssd, wts["conv_w"], wts["conv_b"], wts["dt_bias"], wts["a_log"], wts["d_skip"],
                         wts["norm_g"])
    x_new, br_a, br_s, br_m, merged, out = merge_fwd(
        y_att, y_sg, y_ssm, p_gate, x, wts["w_a"], wts["w_s"], wts["w_m"], wts["w_o"], wts["g_post"])
    saved = dict(x=x, p_gate=p_gate, p_sgu=p_sgu, p_att=p_att, p_ssd=p_ssd, h=h,
                 y_att=y_att, y_sg=y_sg, y_ssm=y_ssm, hst=hst,
                 br_a=br_a, br_s=br_s, br_m=br_m, merged=merged, out=out)
    return x_new, saved


def layer_bwd(dy, wts, bias, sv):
    dps, grads = layer_bwd_params(dy, wts, bias, sv)
    dx, grads["g_pre"] = layer_bwd_input(dy, dps, wts, sv, wts["g_pre"])
    return dx, grads


def layer_bwd_input(dy, dps, wts, sv, g_pre):
    wt = wts["wt"]
    tn = {name: t for name, _, t in GROUPS}
    acc = None
    for n in ("gate", "sgu", "ssd"):
        acc = dh_group(dps[n], wt[n], acc, DH_TILE[n], "dh_" + n)
    return dh_last(dps["att"], wt["att"], acc, sv["x"], g_pre, dy, tn["att"], "dh_att")


def layer_bwd_params(dy, wts, bias, sv):
    dout, dba, dbs, dbm, d_gate, dya, dys, dym, dg_post = merge_bwd(
        dy, sv["out"], wts["g_post"], sv["p_gate"], sv["br_a"], sv["br_s"], sv["br_m"],
        wts["w_a"], wts["w_s"], wts["w_m"], wts["w_o"])
    d_att, dbias, dsinks = att_bwd(dya, sv["p_att"], bias, wts["sinks"])
    d_sgu, dsg_w, dsg_bt, dln_g, dln_b = sgu_bwd(dys, sv["p_sgu"], wts["ln_g"], wts["ln_b"], wts["sg_w"],
                                                 wts["sg_bt"])
    d_ssd, dcw, dcb, ddtb, dalog, ddsk, dng = ssd_bwd(
        dym, sv["p_ssd"], sv["hst"], wts["conv_w"], wts["conv_b"], wts["dt_bias"], wts["a_log"], wts["d_skip"],
        wts["norm_g"])
    dps = dict(gate=d_gate, sgu=d_sgu, att=d_att, ssd=d_ssd)
    tn = {name: t for name, _, t in GROUPS}
    grads = dict(
        w_in={n: dw_group(dps[n], sv["h"], tn[n], "dw_in_" + n) for n in dps},
        w_a=matmul_tn(sv["y_att"], dba, "dw_att"),
        w_s=matmul_tn(sv["y_sg"], dbs, "dw_sg"),
        w_m=matmul_tn(sv["y_ssm"], dbm, "dw_ssm"),
        w_o=matmul_tn(sv["merged"], dout, "dw_out"),
        g_post=dg_post, sinks=dsinks, ln_g=dln_g, ln_b=dln_b, sg_w=dsg_w, sg_bt=dsg_bt,
        conv_w=dcw, conv_b=dcb, dt_bias=ddtb, a_log=dalog, d_skip=ddsk, norm_g=dng, bias=dbias)
    return dps, grads


REST_OFF = (0, 256, 512, 1024, 1280)
GR_ROWS = 1536
GR_CONV = 1280
W_IN_SPLIT = 1600
W_IN_HALF = 1824


def kernel(x, w_in, norm_pre, norm_post, rel_bias, att_sinks, sg_ln_g, sg_ln_b, sg_w, sg_b, ssm_conv_w, ssm_conv_b, ssm_dt_bias, ssm_a_log, ssm_d, ssm_norm_g, w_br_att, w_br_sg, w_br_ssm, w_out, loss_target, m_w_in, m_norm_pre, m_norm_post, m_rel_bias, m_att_sinks, m_sg_ln_g, m_sg_ln_b, m_sg_w, m_sg_b, m_ssm_conv_w, m_ssm_conv_b, m_ssm_dt_bias, m_ssm_a_log, m_ssm_d, m_ssm_norm_g, m_w_br_att, m_w_br_sg, m_w_br_ssm, m_w_out, v_w_in, v_norm_pre, v_norm_post, v_rel_bias, v_att_sinks, v_sg_ln_g, v_sg_ln_b, v_sg_w, v_sg_b, v_ssm_conv_w, v_ssm_conv_b, v_ssm_dt_bias, v_ssm_a_log, v_ssm_d, v_ssm_norm_g, v_w_br_att, v_w_br_sg, v_w_br_ssm, v_w_out):
    cx, cy, cc = lax.axis_index("x"), lax.axis_index("y"), lax.axis_index("c")
    me = 2 * cx + cy
    xs = x[0]
    S = xs.shape[0]

    tr = lambda a: jnp.transpose(a, (0, 2, 1))
    w_in_b = tr(w_in).astype(BF16)
    w_rest_b = jnp.concatenate([w_br_att, w_br_sg, w_br_ssm, w_out], axis=1).astype(BF16)
    halves = lambda a: a.reshape(2, a.shape[0] // 2, a.shape[1])
    w_in0 = jnp.pad(w_in_b[0], ((0, W_IN_ROWS - 3400), (0, 0)))
    all0_in, all0_rest = gather_weights([halves(w_in0), halves(w_rest_b[0])])
    convw_slot = jnp.zeros((SHARDS, DEPTH * CONV_K * 768 // 128, 128), F32)
    convw_slot = lax.dynamic_update_index_in_dim(
        convw_slot, jnp.where(cc == 0, 1.0, 0.0) * ssm_conv_w.reshape(-1, 128), me, 0)
    convw_rows = allreduce_rows(convw_slot.reshape(-1, 128), "gather_conv_w")
    convw_all = convw_rows.reshape(SHARDS, DEPTH, CONV_K, 768).transpose(1, 2, 0, 3).reshape(DEPTH, CONV_K, CONV_C)
    g1_ssem, g1_rsem, g1_srcs, g1_lands, g1_token = gather_start(
        [w_in_b[1], w_rest_b[1]], [convw_rows, all0_rest], "gather_l1_start")

    o = REST_OFF

    def layer_weights(l, gathered_in, gathered_rest, g_pre):
        sh_in = [jnp.where(me == s, w_in_b[l], gathered_in[s]) for s in range(SHARDS)]
        sh_rest = [jnp.where(me == s, w_rest_b[l], gathered_rest[s]) for s in range(SHARDS)]
        rest = lambda k: jnp.concatenate([r[o[k]:o[k + 1]] for r in sh_rest], axis=0)
        return dict(
            wt=group_weights(jnp.concatenate(sh_in, axis=0)),
            w_a=rest(0), w_s=rest(1), w_m=rest(2), w_o=rest(3),
            g_pre=g_pre, g_post=norm_post[l][None], sinks=att_sinks[l],
            ln_g=sg_ln_g[l][None], ln_b=sg_ln_b[l][None], sg_w=sg_w[l],
            sg_bt=sg_b[l].T,
            conv_w=jnp.concatenate([convw_all[l], jnp.zeros((4, CONV_C), F32)], axis=0),
            conv_b=ssm_conv_b[l][None], dt_bias=_pad_lanes(ssm_dt_bias[l]), a_log=_pad_lanes(ssm_a_log[l]),
            d_skip=_pad_lanes(ssm_d[l]), norm_g=ssm_norm_g[l][None])

    bias = bias_table(rel_bias)
    layers = [layer_weights(0, [all0_in[:, s].reshape(W_IN_ROWS, D)[0:3400] for s in range(SHARDS)],
                            [all0_rest[:, s].reshape(1280, D) for s in range(SHARDS)],
                            (norm_pre[0] + g1_token[0, 0])[None])]
    act, sv0 = layer_fwd(xs, layers[0], bias)
    land_in, land_rest = gather_wait(g1_ssem, g1_rsem, g1_srcs, g1_lands, act, "gather_l1_wait")
    layers.append(layer_weights(1, land_in, land_rest, norm_pre[1][None]))
    act, sv1 = layer_fwd(act, layers[1], bias)
    saved = [sv0, sv1]
    dy, loss_part = loss_head(act, loss_target[0])
    cvec = jnp.reshape(cc, (1,)).astype(jnp.int32)
    mvec = jnp.reshape(me, (1,)).astype(jnp.int32)

    def by_shard(g):
        gcw = g["conv_w"][0:CONV_K].reshape(CONV_K, SHARDS, 768).transpose(1, 0, 2).reshape(SHARDS, 3, 1024)
        rest = jnp.concatenate([
            g["w_a"].reshape(SHARDS, 256, D), g["w_s"].reshape(SHARDS, 256, D), g["w_o"].reshape(SHARDS, 256, D),
            g["w_m"].reshape(SHARDS, 512, D), jnp.pad(gcw, ((0, 0), (0, GR_ROWS - GR_CONV - 3), (0, 0)))], axis=1)
        return ungroup_grads(g["w_in"]).reshape(SHARDS, 3400, D), rest

    grads = [None] * DEPTH
    dy, grads[1] = layer_bwd(dy, layers[1], bias, saved[1])
    g1_in, g1_rest = by_shard(grads[1])
    g1_in = jnp.pad(g1_in, ((0, 0), (0, W_IN_ROWS - 3400), (0, 0)))
    x1_ssem, x1_rsem, x1_srcs, x1_lands, x1_token = gather_start(
        [g1_in.astype(BF16), g1_rest.astype(BF16)], [], "grads_l1_start", by_dest=True)
    wts0 = dict(layers[0], g_post=layers[0]["g_post"] + x1_token[0, 0])
    dps0, grads[0] = layer_bwd_params(dy, wts0, bias, saved[0])
    r1_in, r1_rest = gather_wait(x1_ssem, x1_rsem, x1_srcs, x1_lands, grads[0]["w_in"]["ssd"], "grads_l1_wait",
                                 by_dest=True)
    p_in = grad_shard_sum(g1_in, r1_in, mvec, 384, "l1_sum_w_in")
    p_rest = grad_shard_sum(g1_rest, r1_rest, mvec, 512, "l1_sum_rest")
    pb_in, pb_rest = grad_sibling_share([p_in, p_rest], "l1_sibling_share")

    g0_in, g0_rest = by_shard(grads[0])
    pad_to = lambda a, rows: jnp.pad(a, ((0, 0), (0, rows - a.shape[1]), (0, 0)))
    g0_in = jnp.stack([pad_to(g0_in[:, 0:W_IN_SPLIT], W_IN_HALF), pad_to(g0_in[:, W_IN_SPLIT:3400], W_IN_HALF)])
    g0_rest = jnp.stack([g0_rest[:, 0:GR_ROWS // 2], g0_rest[:, GR_ROWS // 2:GR_ROWS]])
    sb_in, sb_rest = grad_sibling_exchange([g0_in, g0_rest])
    t_in, t_in_b = grad_chip_sum(g0_in, sb_in, cvec, 608, "chip_sum_w_in")
    t_rest, t_rest_b = grad_chip_sum(g0_rest, sb_rest, cvec, 384, "chip_sum_rest")
    x0_ssem, x0_rsem, x0_srcs, x0_lands, x0_token = gather_start([t_in_b, t_rest_b], [], "grads_l0_start", by_dest=True)
    dy, grads[0]["g_pre"] = layer_bwd_input(dy, dps0, layers[0], saved[0], layers[0]["g_pre"] + x0_token[0, 0])
    grad_x = dy[None]
    rb_in, rb_rest = gather_wait(x0_ssem, x0_rsem, x0_srcs, x0_lands, dy, "grads_l0_wait", by_dest=True)
    grad_rel_local = bias_grad(grads[0]["bias"] + grads[1]["bias"])
    f_in = grad_shard_sum(t_in, rb_in, mvec, 608, "shard_sum_w_in")
    f_rest = grad_shard_sum(t_rest, rb_rest, mvec, 384, "shard_sum_rest")
    fb_in, fb_rest = grad_sibling_share([f_in, f_rest], "l0_sibling_share")

    red = small_allreduce(grads, grad_rel_local, loss_part + 0.0 * f_rest[0:1, 0:128])
    loss = red[LOSS_ROW, 0]

    res = adamw_small(red, (rel_bias, m_rel_bias, v_rel_bias), dict(
        norm_pre=(norm_pre, m_norm_pre, v_norm_pre), norm_post=(norm_post, m_norm_post, v_norm_post),
        att_sinks=(att_sinks, m_att_sinks, v_att_sinks), sg_ln_g=(sg_ln_g, m_sg_ln_g, v_sg_ln_g),
        sg_ln_b=(sg_ln_b, m_sg_ln_b, v_sg_ln_b), sg_w=(sg_w, m_sg_w, v_sg_w), sg_b=(sg_b, m_sg_b, v_sg_b),
        ssm_conv_b=(ssm_conv_b, m_ssm_conv_b, v_ssm_conv_b), ssm_dt_bias=(ssm_dt_bias, m_ssm_dt_bias, v_ssm_dt_bias),
        ssm_a_log=(ssm_a_log, m_ssm_a_log, v_ssm_a_log), ssm_d=(ssm_d, m_ssm_d, v_ssm_d),
        ssm_norm_g=(ssm_norm_g, m_ssm_norm_g, v_ssm_norm_g)))
    res["w_in"] = tuple(tr(a) for a in adamw_big(
        tr(w_in), tr(m_w_in), tr(v_w_in), (f_in, fb_in, W_IN_SPLIT // 200, 0, 0), (p_in, pb_in, 0), cvec, "adamw_w_in", 200))
    rest_upd = lambda w, m, v, name, n0, off0, off1: adamw_big(
        w, m, v, (f_rest, fb_rest, n0, off0, off0), (p_rest, pb_rest, off1), cvec, name, 256)
    res["w_br_att"] = rest_upd(w_br_att, m_w_br_att, v_w_br_att, "adamw_w_br_att", 1, 0, 0)
    res["w_br_sg"] = rest_upd(w_br_sg, m_w_br_sg, v_w_br_sg, "adamw_w_br_sg", 1, 1, 1)
    res["w_out"] = rest_upd(w_out, m_w_out, v_w_out, "adamw_w_out", 1, 2, 2)
    res["w_br_ssm"] = rest_upd(w_br_ssm, m_w_br_ssm, v_w_br_ssm, "adamw_w_br_ssm", 0, 0, 3)
    cw0 = jnp.where(cc == 1, f_rest, fb_rest)[GR_CONV - GR_ROWS // 2:GR_CONV - GR_ROWS // 2 + 3]
    cw1 = (p_rest + pb_rest)[GR_CONV:GR_CONV + 3]
    g_conv_w = jnp.stack([cw0.reshape(CONV_K, 768), cw1.reshape(CONV_K, 768)])
    res["ssm_conv_w"] = (g_conv_w,) + tuple(adamw_plain(ssm_conv_w, g_conv_w, m_ssm_conv_w, v_ssm_conv_w, "adamw_conv_w"))

    order = ["w_in", "norm_pre", "norm_post", "rel_bias", "att_sinks", "sg_ln_g", "sg_ln_b", "sg_w", "sg_b",
             "ssm_conv_w", "ssm_conv_b", "ssm_dt_bias", "ssm_a_log", "ssm_d", "ssm_norm_g",
             "w_br_att", "w_br_sg", "w_br_ssm", "w_out"]
    return (loss, grad_x, *[res[n][0] for n in order], *[res[n][1] for n in order],
            *[res[n][2] for n in order], *[res[n][3] for n in order])
```

```python
import functools
import math

import numpy as np
import jax
import jax.numpy as jnp
from jax import lax
from jax.experimental import pallas as pl
from jax.experimental.pallas import tpu as pltpu

F32 = jnp.float32
BF16 = jnp.bfloat16
MESH = pl.DeviceIdType.MESH

D = 1024
DEPTH = 2
EPS = 1e-6
L = 128
HEADS = 16
KV = 2
DH = 64
SSM_W = 2048
SSM_H = 32
SSM_P = 64
SSM_G = 4
SSM_N = 128
CONV_K = 4
CONV_C = 3072
NEG = -1e30
IN_COLS = 13600

GROUPS = (("gate", 3072, 1536), ("sgu", 3072, 1536), ("att", 2304, 2304), ("ssd", 5376, 1792))
W_IN_ROWS = 3456
DH_TILE = {"gate": 3072, "sgu": 3072, "ssd": 2688}

ADAM_LR = 0.001
ADAM_B1 = 0.9
ADAM_B2 = 0.999
ADAM_EPS = 1e-08
ADAM_WD = 0.01
ADAM_STEP = 10

VMEM_LIMIT = 56 * 1024 * 1024

SHARDS = 4


def _dot(a, b):
    return jnp.dot(a, b, preferred_element_type=F32)


def _dot_nt(a, b):
    return lax.dot_general(a, b, (((1,), (1,)), ((), ())), preferred_element_type=F32)


def _dot_tn(a_f32, b):
    return jnp.dot(a_f32.T.astype(BF16), b, preferred_element_type=F32)


def _dot_t(a, b):
    return lax.dot_general(a, b, (((0,), (0,)), ((), ())), preferred_element_type=F32)


def _dot_hi(a, b):
    return jnp.dot(a, b, preferred_element_type=F32, precision=lax.Precision.HIGHEST)


def _pieces(x, n):
    out = []
    for _ in range(n - 1):
        p = x.astype(BF16)
        out.append(p)
        x = x - p.astype(F32)
    out.append(x.astype(BF16))
    return out


def _dot_sel(a, sel, n):
    sel = sel.astype(BF16)
    acc = None
    for p in _pieces(a, n):
        t = _dot(p, sel)
        acc = t if acc is None else acc + t
    return acc


def _sel_dot(sel, b, n):
    sel = sel.astype(BF16)
    acc = None
    for p in _pieces(b, n):
        t = _dot(sel, p)
        acc = t if acc is None else acc + t
    return acc


def _sigmoid(x):
    return 1.0 / (1.0 + jnp.exp(-x))


def _softplus(x):
    return jnp.maximum(x, 0.0) + jnp.log(1.0 + jnp.exp(-jnp.abs(x)))


def _params(sem=None, vmem=VMEM_LIMIT):
    kw = dict(vmem_limit_bytes=vmem)
    if sem is not None:
        kw["dimension_semantics"] = sem
    return pltpu.CompilerParams(**kw)


def _full(shape):
    nd = len(shape)
    return pl.BlockSpec(shape, lambda *_: (0,) * nd)


def group_weights(wt):
    return dict(
        gate=wt[10528:13600],
        sgu=wt[2304:5376],
        att=jnp.concatenate([wt[0:1024], wt[1280:2304], wt[1024:1280]], axis=0),
        ssd=jnp.concatenate([wt[5376:10496], wt[10496:10528], jnp.zeros((224, D), wt.dtype)], axis=0))


def ungroup_grads(g):
    a, s = g["att"], g["ssd"]
    return jnp.concatenate([a[0:1024], a[2048:2304], a[1024:2048], g["sgu"], s[0:5152], g["gate"]], axis=0)


def _bucket_table():
    qi = np.arange(L)[:, None]
    kj = np.arange(2 * L)[None, :]
    dist = np.maximum(qi + L - kj, 0)
    dist_f = np.maximum(dist, 1).astype(np.float32)
    large = 16 + (np.log(dist_f / np.float32(16)) / np.float32(math.log(128 / 16)) * np.float32(16)).astype(np.int32)
    large = np.minimum(large, 31)
    return np.where(dist < 16, dist, large).astype(np.int32)


def bias_table(rel_bias):
    buckets = jnp.asarray(_bucket_table().reshape(1, L * 2 * L))

    def body(rb_ref, bk_ref, out_ref):
        onehot = (lax.broadcasted_iota(jnp.int32, (32, L * 2 * L), 0) == bk_ref[...]).astype(F32)
        out_ref[...] = lax.dot_general(rb_ref[...], onehot, (((0,), (0,)), ((), ())),
                                       preferred_element_type=F32, precision=lax.Precision.HIGHEST)

    out = pl.pallas_call(
        body, name="bias_table",
        out_shape=jax.ShapeDtypeStruct((HEADS, L * 2 * L), F32),
        compiler_params=_params(),
    )(rel_bias, buckets)
    out = out.reshape(HEADS, L, 2 * L)
    win = _window_mask()
    first = win & (np.arange(2 * L)[None, :] >= L)
    return jnp.stack([jnp.where(first, out, NEG), jnp.where(win, out, NEG)])


def _window_mask():
    dist = np.arange(L)[:, None] + L - np.arange(2 * L)[None, :]
    return (dist >= 0) & (dist < L)


def bias_grad(dbias):
    buckets = jnp.asarray(_bucket_table().reshape(1, L * 2 * L))

    def body(db_ref, bk_ref, out_ref):
        onehot = (lax.broadcasted_iota(jnp.int32, (32, L * 2 * L), 0) == bk_ref[...]).astype(F32)
        out_ref[...] = lax.dot_general(onehot, db_ref[...], (((1,), (1,)), ((), ())),
                                       preferred_element_type=F32, precision=lax.Precision.HIGHEST)

    return pl.pallas_call(
        body, name="bias_grad",
        out_shape=jax.ShapeDtypeStruct((32, HEADS), F32),
        compiler_params=_params(),
    )(dbias.reshape(HEADS, L * 2 * L), buckets)


def _row_tile(S):
    return 1024 if S % 1024 == 0 else 512


def inproj_first(x, g_pre, wt, tn, name):
    S, W = x.shape[0], wt.shape[0]
    tm = _row_tile(S)

    def body(x_ref, g_ref, w_ref, o_ref, h_ref):
        @pl.when(pl.program_id(1) == 0)
        def _():
            xv = x_ref[...]
            r = lax.rsqrt(jnp.mean(xv * xv, axis=-1, keepdims=True) + EPS)
            h_ref[...] = (xv * r * g_ref[...]).astype(BF16)
        o_ref[...] = _dot_nt(h_ref[...], w_ref[...]).astype(BF16)

    return pl.pallas_call(
        body, name=name, grid=(S // tm, W // tn),
        in_specs=[pl.BlockSpec((tm, D), lambda i, j: (i, 0)), _full((1, D)),
                  pl.BlockSpec((tn, D), lambda i, j: (j, 0))],
        out_specs=[pl.BlockSpec((tm, tn), lambda i, j: (i, j)), pl.BlockSpec((tm, D), lambda i, j: (i, 0))],
        out_shape=[jax.ShapeDtypeStruct((S, W), BF16), jax.ShapeDtypeStruct((S, D), BF16)],
        compiler_params=_params(("arbitrary", "arbitrary")),
    )(x, g_pre, wt)


def inproj_group(h, wt, tn, name, dtype):
    S, W = h.shape[0], wt.shape[0]
    tm = _row_tile(S)

    def body(h_ref, w_ref, o_ref):
        o_ref[...] = _dot_nt(h_ref[...], w_ref[...]).astype(dtype)

    return pl.pallas_call(
        body, name=name, grid=(S // tm, W // tn),
        in_specs=[pl.BlockSpec((tm, D), lambda i, j: (i, 0)), pl.BlockSpec((tn, D), lambda i, j: (j, 0))],
        out_specs=pl.BlockSpec((tm, tn), lambda i, j: (i, j)),
        out_shape=jax.ShapeDtypeStruct((S, W), dtype),
        compiler_params=_params(("arbitrary", "arbitrary")),
    )(h, wt)


def dh_group(dp, wt, acc, tk, name):
    S, W = dp.shape
    tm = _row_tile(S)

    def body(*refs):
        dp_ref, w_ref, o_ref = refs[0], refs[1], refs[-1]
        first = pl.program_id(1) == 0
        if acc is None:
            @pl.when(first)
            def _():
                o_ref[...] = jnp.zeros_like(o_ref)
        else:
            @pl.when(first)
            def _():
                o_ref[...] = refs[2][...]
        o_ref[...] += _dot(dp_ref[...], w_ref[...])

    row = pl.BlockSpec((tm, D), lambda i, k: (i, 0))
    return pl.pallas_call(
        body, name=name, grid=(S // tm, W // tk),
        in_specs=[pl.BlockSpec((tm, tk), lambda i, k: (i, k)), pl.BlockSpec((tk, D), lambda i, k: (k, 0))]
        + ([] if acc is None else [row]),
        out_specs=row, out_shape=jax.ShapeDtypeStruct((S, D), F32),
        input_output_aliases={} if acc is None else {2: 0},
        compiler_params=_params(("arbitrary", "arbitrary")),
    )(*((dp, wt) if acc is None else (dp, wt, acc)))


def dh_last(dp, wt, acc_in, x, g_pre, dy, tk, name):
    S, W = dp.shape
    tm = 512
    nk = W // tk

    def body(dp_ref, w_ref, a_ref, x_ref, g_ref, dy_ref, dx_ref, dg_ref, acc):
        i, k = pl.program_id(0), pl.program_id(1)

        @pl.when(k == 0)
        def _():
            acc[...] = a_ref[...]

        acc[...] += _dot(dp_ref[...], w_ref[...])

        @pl.when((k == nk - 1) & (i == 0))
        def _():
            dg_ref[...] = jnp.zeros_like(dg_ref)

        @pl.when(k == nk - 1)
        def _():
            xv = x_ref[...]
            dh = acc[...]
            g = g_ref[...]
            r = lax.rsqrt(jnp.mean(xv * xv, axis=-1, keepdims=True) + EPS)
            dhg = dh * g
            dx_ref[...] = dy_ref[...] + r * dhg - xv * (r * r * r) * jnp.mean(dhg * xv, axis=-1, keepdims=True)
            dg_ref[...] += jnp.sum(dh * xv * r, axis=0, keepdims=True)

    row = pl.BlockSpec((tm, D), lambda i, k: (i, 0))
    return pl.pallas_call(
        body, name=name, grid=(S // tm, nk),
        in_specs=[pl.BlockSpec((tm, tk), lambda i, k: (i, k)), pl.BlockSpec((tk, D), lambda i, k: (k, 0)),
                  row, row, _full((1, D)), row],
        out_specs=[row, _full((1, D))],
        out_shape=[jax.ShapeDtypeStruct((S, D), F32), jax.ShapeDtypeStruct((1, D), F32)],
        scratch_shapes=[pltpu.VMEM((tm, D), F32)],
        compiler_params=_params(("arbitrary", "arbitrary")),
    )(dp, wt, acc_in, x, g_pre, dy)


def dw_group(dp, h, tn, name):
    S, W = dp.shape
    ts = _row_tile(S)

    def body(dp_ref, h_ref, o_ref):
        @pl.when(pl.program_id(1) == 0)
        def _():
            o_ref[...] = jnp.zeros_like(o_ref)
        o_ref[...] += _dot_t(dp_ref[...], h_ref[...])

    return pl.pallas_call(
        body, name=name, grid=(W // tn, S // ts),
        in_specs=[pl.BlockSpec((ts, tn), lambda j, s: (s, j)), pl.BlockSpec((ts, D), lambda j, s: (s, 0))],
        out_specs=pl.BlockSpec((tn, D), lambda j, s: (j, 0)),
        out_shape=jax.ShapeDtypeStruct((W, D), F32),
        compiler_params=_params(("arbitrary", "arbitrary")),
    )(dp, h)


def matmul_tn(a, b, name, tn=1024):
    S, K = a.shape
    N = b.shape[1]
    ts = _row_tile(S)
    ns = S // ts

    def body(a_ref, b_ref, o_ref):
        @pl.when(pl.program_id(1) == 0)
        def _():
            o_ref[...] = jnp.zeros_like(o_ref)
        o_ref[...] += _dot_t(a_ref[...], b_ref[...])

    return pl.pallas_call(
        body, name=name, grid=(N // tn, ns),
        in_specs=[pl.BlockSpec((ts, K), lambda j, s: (s, 0)), pl.BlockSpec((ts, tn), lambda j, s: (s, j))],
        out_specs=pl.BlockSpec((K, tn), lambda j, s: (0, j)),
        out_shape=jax.ShapeDtypeStruct((K, N), F32),
        compiler_params=_params(("arbitrary", "arbitrary")),
    )(a, b)


def _att_in_specs(nb):
    last = nb - 1
    cur = lambda n: jnp.minimum(n, last)
    prev = lambda n: jnp.maximum(jnp.minimum(n, last) - 1, 0)
    return [
        pl.BlockSpec((L, 1024), lambda n: (cur(n), 0)),
        pl.BlockSpec((L, 128), lambda n: (prev(n), 16)),
        pl.BlockSpec((L, 128), lambda n: (cur(n), 16)),
        pl.BlockSpec((L, 128), lambda n: (prev(n), 17)),
        pl.BlockSpec((L, 128), lambda n: (cur(n), 17)),
        pl.BlockSpec((L, 1024), lambda n: (cur(n), 1)),
        _full((2, HEADS, L, 2 * L)),
        pl.BlockSpec(memory_space=pltpu.SMEM),
    ]


GH = HEADS // KV
GB = 8


def _stack_heads(ref, h0, nh, scr):
    for g in range(nh):
        scr[(h0 + g) * L:(h0 + g + 1) * L, :] = ref[:, (h0 + g) * DH:(h0 + g + 1) * DH].astype(F32)
    return scr[h0 * L:(h0 + nh) * L, :]


def _unstack_heads(val, h0, nh, ref):
    for g in range(nh):
        ref[:, (h0 + g) * DH:(h0 + g + 1) * DH] = val[g * L:(g + 1) * L, :]


def _sink_rows(s_ref, h0, nh):
    return jnp.concatenate([jnp.full((L, 1), s_ref[h0 + g], F32) for g in range(nh)], axis=0)


def _att_probs(qh, kk, bias_h, sk):
    logits = _dot_nt(qh, kk) + bias_h
    m =jnp.maximum(jnp.max(logits, axis=-1, keepdims=True), sk)
    p = jnp.exp(logits - m)
    es = jnp.exp(sk - m)
    den = jnp.sum(p, axis=-1, keepdims=True) + es
    return p / den, es / den


def att_fwd(proj, bias, sinks):
    S = proj.shape[0]
    nb = S // L

    def body(q_ref, kp_ref, kc_ref, vp_ref, vc_ref, z_ref, bias_ref, s_ref, y_ref, o_scr):
        table = jnp.where(pl.program_id(0) > 0, 1, 0)
        for kv in range(KV):
            sl = slice(kv * DH, (kv + 1) * DH)
            kk = jnp.concatenate([kp_ref[:, sl], kc_ref[:, sl]], axis=0).astype(BF16)
            vv = jnp.concatenate([vp_ref[:, sl], vc_ref[:, sl]], axis=0).astype(BF16)
            for g in range(GH):
                h = kv * GH + g
                hs = slice(h * DH, (h + 1) * DH)
                qh = (q_ref[:, hs] * 0.125).astype(BF16)
                P, _ = _att_probs(qh, kk, bias_ref[table, h], s_ref[h])
                o_scr[:, hs] = _dot(P.astype(BF16), vv)
        z = z_ref[...].astype(F32)
        y_ref[...] = (o_scr[...] * (z * _sigmoid(z))).astype(BF16)

    return pl.pallas_call(
        body, name="att_fwd", grid=(nb,),
        in_specs=_att_in_specs(nb),
        out_specs=pl.BlockSpec((L, 1024), lambda n: (n, 0)),
        out_shape=jax.ShapeDtypeStruct((S, 1024), BF16),
        scratch_shapes=[pltpu.VMEM((L, 1024), F32)],
        compiler_params=_params(("arbitrary",)),
    )(proj, proj, proj, proj, proj, proj, bias, sinks)


def att_bwd(dy, proj, bias, sinks):
    S = proj.shape[0]
    nb = S // L
    last = nb - 1

    def body(dy_ref, q_ref, kp_ref, kc_ref, vp_ref, vc_ref, z_ref, bias_ref, s_ref,
             dout_ref, dbias_ref, dsink_ref, carry, band, dq_scr, dz_scr, qs_scr, zs_scr, dys_scr):
        n = pl.program_id(0)

        @pl.when(n == 0)
        def _():
            carry[...] = jnp.zeros_like(carry)
            dq_scr[...] = jnp.zeros_like(dq_scr)
            dz_scr[...] = jnp.zeros_like(dz_scr)
            dbias_ref[...] = jnp.zeros_like(dbias_ref)
            dsink_ref[...] = jnp.zeros_like(dsink_ref)

        dout_ref[:, 0:1024] = dq_scr[...].astype(BF16)
        dout_ref[:, 1024:2048] = dz_scr[...].astype(BF16)
        band[...] = jnp.zeros_like(band)

        @pl.when(n < nb)
        def _():
            table = jnp.where(n > 0, 1, 0)
            lane = lax.broadcasted_iota(jnp.int32, (1, 128), 1)
            dsink = jnp.zeros((1, 128), F32)
            for kv in range(KV):
                sl = slice(kv * DH, (kv + 1) * DH)
                kk = jnp.concatenate([kp_ref[:, sl], kc_ref[:, sl]], axis=0).astype(BF16)
                vv = jnp.concatenate([vp_ref[:, sl], vc_ref[:, sl]], axis=0).astype(BF16)
                dk_acc = jnp.zeros((2 * L, DH), F32)
                dv_acc = jnp.zeros((2 * L, DH), F32)
                for h0 in range(kv * GH, (kv + 1) * GH, GB):
                    qs = (_stack_heads(q_ref, h0, GB, qs_scr) * 0.125).astype(BF16)
                    bias_g = bias_ref[table, h0:h0 + GB].reshape(GB * L, 2 * L)
                    P, psink = _att_probs(qs, kk, bias_g, _sink_rows(s_ref, h0, GB))
                    zs = _stack_heads(z_ref, h0, GB, zs_scr)
                    dys = _stack_heads(dy_ref, h0, GB, dys_scr)
                    sg = _sigmoid(zs)
                    O = _dot(P.astype(BF16), vv)
                    _unstack_heads(dys * O * (sg * (1.0 + zs * (1.0 - sg))), h0, GB, dz_scr)
                    dOb = (dys * (zs * sg)).astype(BF16)
                    dP = _dot_nt(dOb, vv)
                    delta = jnp.sum(P * dP, axis=-1, keepdims=True)
                    dS = P * (dP - delta)
                    sd = psink * delta
                    for g in range(GB):
                        dsink = dsink + jnp.where(lane == h0 + g, -jnp.sum(sd[g * L:(g + 1) * L, :]), 0.0)
                    _unstack_heads(_dot(dS.astype(BF16), kk) * 0.125, h0, GB, dq_scr)
                    dbias_ref[h0:h0 + GB] += dS.reshape(GB, L, 2 * L)
                    dk_acc = dk_acc + _dot_tn(dS, qs)
                    dv_acc = dv_acc + _dot_tn(P, dOb)
                band[:, sl] = dk_acc
                band[:, 128 + kv * DH:128 + (kv + 1) * DH] = dv_acc
            dsink_ref[...] += dsink

        out = carry[...] + band[0:L, :]
        dout_ref[:, 2048:2304] = out.astype(BF16)
        carry[...] = band[L:2 * L, :]

    cur = lambda n: jnp.minimum(n, last)
    lag = lambda n: jnp.maximum(n - 1, 0)
    return pl.pallas_call(
        body, name="att_bwd", grid=(nb + 1,),
        in_specs=[pl.BlockSpec((L, 1024), lambda n: (cur(n), 0))] + _att_in_specs(nb),
        out_specs=[pl.BlockSpec((L, 2304), lambda n: (lag(n), 0)), _full((HEADS, L, 2 * L)), _full((1, 128))],
        out_shape=[jax.ShapeDtypeStruct((S, 2304), BF16),
                   jax.ShapeDtypeStruct((HEADS, L, 2 * L), F32), jax.ShapeDtypeStruct((1, 128), F32)],
        scratch_shapes=[pltpu.VMEM((L, 256), F32), pltpu.VMEM((2 * L, 256), F32),
                        pltpu.VMEM((L, 1024), F32), pltpu.VMEM((L, 1024), F32)]
        + [pltpu.VMEM((HEADS * L, DH), F32)] * 3,
        compiler_params=_params(("arbitrary",)),
    )(dy, proj, proj, proj, proj, proj, proj, bias, sinks)


SGU_CH = 2


def _sgu_in_specs():
    return [
        pl.BlockSpec((SGU_CH * L, 1024), lambda c: (c, 0)),
        pl.BlockSpec((SGU_CH * L, 1024), lambda c: (c, 1)),
        pl.BlockSpec((SGU_CH * L, 1024), lambda c: (c, 2)),
        _full((1, 1024)), _full((1, 1024)), _full((8, L, L)), _full((L, 8)),
    ]


def _sgu_norm(v, lg, lb):
    mu = jnp.mean(v, axis=-1, keepdims=True)
    vc = v - mu
    rstd = lax.rsqrt(jnp.mean(vc * vc, axis=-1, keepdims=True) + EPS)
    xhat = vc * rstd
    return xhat * lg + lb, xhat, rstd


def _tril():
    return lax.broadcasted_iota(jnp.int32, (L, L), 0) >= lax.broadcasted_iota(jnp.int32, (L, L), 1)


def _sgu_side(a, g):
    return jnp.concatenate([a[c * L:(c + 1) * L, g * 128:(g + 1) * 128] for c in range(SGU_CH)], axis=1)


def _sgu_stack(parts):
    return jnp.concatenate([jnp.concatenate([p[:, c * L:(c + 1) * L] for p in parts], axis=1)
                            for c in range(SGU_CH)], axis=0)


def sgu_fwd(proj, ln_g, ln_b, w, b_t):
    S = proj.shape[0]

    def body(u_ref, v_ref, z_ref, lg_ref, lb_ref, w_ref, bt_ref, y_ref):
        vn, _, _ = _sgu_norm(v_ref[...].astype(F32), lg_ref[...], lb_ref[...])
        tri = _tril()
        parts = []
        for g in range(8):
            wg = jnp.where(tri, w_ref[g], 0.0).astype(BF16)
            parts.append(_dot(wg, _sgu_side(vn, g).astype(BF16)) + bt_ref[:, g:g + 1])
        mixed = _sgu_stack(parts)
        z = z_ref[...].astype(F32)
        y_ref[...] = (u_ref[...].astype(F32) * mixed * (z * _sigmoid(z))).astype(BF16)

    return pl.pallas_call(
        body, name="sgu_fwd", grid=(S // (SGU_CH * L),),
        in_specs=_sgu_in_specs(),
        out_specs=pl.BlockSpec((SGU_CH * L, 1024), lambda c: (c, 0)),
        out_shape=jax.ShapeDtypeStruct((S, 1024), BF16),
        compiler_params=_params(("arbitrary",)),
    )(proj, proj, proj, ln_g, ln_b, w, b_t)


def sgu_bwd(dy, proj, ln_g, ln_b, w, b_t):
    S = proj.shape[0]

    def body(dy_ref, u_ref, v_ref, z_ref, lg_ref, lb_ref, w_ref, bt_ref,
             dout_ref, dw_ref, dbt_ref, dlg_ref, dlb_ref):
        @pl.when(pl.program_id(0) == 0)
        def _():
            dw_ref[...] = jnp.zeros_like(dw_ref)
            dbt_ref[...] = jnp.zeros_like(dbt_ref)
            dlg_ref[...] = jnp.zeros_like(dlg_ref)
            dlb_ref[...] = jnp.zeros_like(dlb_ref)

        lg = lg_ref[...]
        vn, xhat, rstd = _sgu_norm(v_ref[...].astype(F32), lg, lb_ref[...])
        tri = _tril()
        lane = lax.broadcasted_iota(jnp.int32, (L, 128), 1)
        wgs, vns, parts = [], [], []
        for g in range(8):
            wg = jnp.where(tri, w_ref[g], 0.0)
            wgs.append(wg)
            vns.append(_sgu_side(vn, g).astype(BF16))
            parts.append(_dot(wg.astype(BF16), vns[g]) + bt_ref[:, g:g + 1])
        mixed = _sgu_stack(parts)
        z = z_ref[...].astype(F32)
        sg = _sigmoid(z)
        silu = z * sg
        dy_v = dy_ref[...]
        u = u_ref[...].astype(F32)
        dout_ref[:, 0:1024] = (dy_v * mixed * silu).astype(BF16)
        dout_ref[:, 2048:3072] = (dy_v * u * mixed * (sg * (1.0 + z * (1.0 - sg)))).astype(BF16)
        dmixed = dy_v * u * silu
        dbt = jnp.zeros((L, 128), F32)
        dvn_parts = []
        for g in range(8):
            dm = _sgu_side(dmixed, g)
            dmb = dm.astype(BF16)
            dbt = dbt + jnp.where(lane == g, jnp.sum(dm, axis=1, keepdims=True), 0.0)
            dw_ref[g] += jnp.where(tri, _dot_nt(dmb, vns[g]), 0.0)
            dvn_parts.append(_dot_tn(wgs[g], dmb))
        dbt_ref[...] += dbt
        dvn = _sgu_stack(dvn_parts)
        dlg_ref[...] += jnp.sum(dvn * xhat, axis=0, keepdims=True)
        dlb_ref[...] += jnp.sum(dvn, axis=0, keepdims=True)
        dxh = dvn * lg
        dv = rstd * (dxh - jnp.mean(dxh, axis=-1, keepdims=True)
                     - xhat * jnp.mean(dxh * xhat, axis=-1, keepdims=True))
        dout_ref[:, 1024:2048] = dv.astype(BF16)

    return pl.pallas_call(
        body, name="sgu_bwd", grid=(S // (SGU_CH * L),),
        in_specs=[pl.BlockSpec((SGU_CH * L, 1024), lambda c: (c, 0))] + _sgu_in_specs(),
        out_specs=[pl.BlockSpec((SGU_CH * L, 3072), lambda c: (c, 0)), _full((8, L, L)), _full((L, 128)),
                   _full((1, 1024)), _full((1, 1024))],
        out_shape=[jax.ShapeDtypeStruct((S, 3072), BF16), jax.ShapeDtypeStruct((8, L, L), F32),
                   jax.ShapeDtypeStruct((L, 128), F32), jax.ShapeDtypeStruct((1, 1024), F32),
                   jax.ShapeDtypeStruct((1, 1024), F32)],
        compiler_params=_params(("arbitrary",)),
    )(dy, proj, proj, proj, ln_g, ln_b, w, b_t)


def _expand_matrices():
    e = (np.arange(SSM_W)[None, :] // SSM_P == np.arange(128)[:, None]).astype(np.float32)
    return jnp.asarray(e, BF16), jnp.asarray(e.T, BF16)


def _rows_from(ref, start):
    C = ref.shape[1]
    tiles = ref[...].reshape(17, 8, C)
    q, s = divmod(start, 8)
    if s == 0:
        return tiles[q:q + 16].reshape(L, C)
    rolled = pltpu.roll(tiles, 8 - s, axis=1)
    sub = lax.broadcasted_iota(jnp.int32, (16, 8, C), 1)
    return jnp.where(sub < 8 - s, rolled[q:q + 16], rolled[q + 1:q + 17]).reshape(L, C)


def _ssd_common(ext_ref, cw_ref, cb_ref, dt_raw, dtb, alog):
    taps = [_rows_from(ext_ref, 5 + k) for k in range(CONV_K)]
    pre = cb_ref[...]
    for k in range(CONV_K):
        pre = pre + cw_ref[k:k + 1, :] * taps[k]
    sg_pre = _sigmoid(pre)
    xc = pre * sg_pre
    dt = _softplus(dt_raw + dtb)
    a = -jnp.exp(alog)
    adt = dt * a
    acs = _sel_dot(_tril(), adt, 3)
    return pre, sg_pre, xc, dt, a, acs, taps


def _ssd_in_specs(rev, nc):
    cidx = (lambda c: nc - 1 - c) if rev else (lambda c: c)
    return [
        pl.BlockSpec((L, 2048), lambda c: (cidx(c), 0)),
        pl.BlockSpec((L, 1024), lambda c: (cidx(c), 2)),
        pl.BlockSpec((L, 1024), lambda c: (cidx(c), 3)),
        pl.BlockSpec((L, 1024), lambda c: (cidx(c), 4)),
        pl.BlockSpec((L, 128), lambda c: (cidx(c), 40)),
        _full((8, CONV_C)), _full((1, CONV_C)), _full((1, 128)), _full((1, 128)), _full((1, 128)),
        _full((1, SSM_W)), _full((128, SSM_W)), _full((SSM_W, 128)),
    ]


def ssd_fwd(proj, conv_w, conv_b, dt_bias, a_log, d_skip, norm_g):
    S = proj.shape[0]
    nc = S // L

    def body(z_ref, xa_ref, xb_ref, xc_ref, dt_ref, cw_ref, cb_ref, dtb_ref, alog_ref, dsk_ref, ng_ref,
             ex_ref, ext_ref, y_ref, hs_ref, H, ext, ysc):
        @pl.when(pl.program_id(0) == 0)
        def _():
            H[...] = jnp.zeros_like(H)
            ext[0:8, :] = jnp.zeros((8, CONV_C), F32)

        for k, ref in enumerate((xa_ref, xb_ref, xc_ref)):
            ext[8:8 + L, k * 1024:(k + 1) * 1024] = ref[...].astype(F32)
        pre, sg_pre, xc, dt, a, acs, _ = _ssd_common(ext, cw_ref, cb_ref, dt_ref[...].astype(F32), dtb_ref[...],
                                                     alog_ref[...])
        for k, ref in enumerate((xa_ref, xb_ref, xc_ref)):
            ext[0:8, k * 1024:(k + 1) * 1024] = ref[L - 8:L, :].astype(F32)
        xs = xc[:, 0:SSM_W]
        acs_t = acs.T
        ex = ex_ref[...]
        dt_x = _dot_sel(dt, ex, 2)
        xdt = xs * dt_x
        eacs_x = _dot_sel(jnp.exp(acs), ex, 2)
        xw = xdt * _dot_sel(jnp.exp(acs[L - 1:L, :] - acs), ex, 2)
        cd_row = jnp.exp(acs[L - 1:L, :])
        hs_ref[0] = H[...]
        tri = _tril()
        for g in range(SSM_G):
            gs = slice(g * 512, (g + 1) * 512)
            bg = xc[:, SSM_W + g * SSM_N:SSM_W + (g + 1) * SSM_N].astype(BF16)
            cg = xc[:, SSM_W + 512 + g * SSM_N:SSM_W + 512 + (g + 1) * SSM_N].astype(BF16)
            G = _dot_nt(cg, bg)
            yoff = _dot_nt(cg, H[gs, :].astype(BF16)) * eacs_x[:, gs]
            Sg = _dot_tn(xw[:, gs], bg)
            for j in range(8):
                hh = g * 8 + j
                hs = slice(hh * SSM_P, (hh + 1) * SSM_P)
                seg = acs[:, hh:hh + 1] - acs_t[hh:hh + 1, :]
                dk = jnp.where(tri, jnp.exp(seg), 0.0)
                yd = _dot((G * dk).astype(BF16), xdt[:, hs].astype(BF16))
                ysc[:, hs] = yd + yoff[:, j * SSM_P:(j + 1) * SSM_P]
                H[hs, :] = H[hs, :] * cd_row[:, hh:hh + 1] + Sg[j * SSM_P:(j + 1) * SSM_P, :]
        d_x = _dot_sel(jnp.broadcast_to(dsk_ref[...], (8, 128)), ex, 3)[0:1, :]
        Y = ysc[...] + d_x * xs
        z = z_ref[...].astype(F32)
        yz = Y * (z * _sigmoid(z))
        ng = ng_ref[...]
        for g in range(SSM_G):
            gs = slice(g * 512, (g + 1) * 512)
            t = yz[:, gs]
            rstd = lax.rsqrt(jnp.mean(t * t, axis=-1, keepdims=True) + EPS)
            y_ref[:, gs] = (t * rstd * ng[:, gs]).astype(BF16)

    return pl.pallas_call(
        body, name="ssd_fwd", grid=(nc,),
        in_specs=_ssd_in_specs(False, nc),
        out_specs=[pl.BlockSpec((L, SSM_W), lambda c: (c, 0)), pl.BlockSpec((1, SSM_W, SSM_N), lambda c: (c, 0, 0))],
        out_shape=[jax.ShapeDtypeStruct((S, SSM_W), BF16), jax.ShapeDtypeStruct((nc, SSM_W, SSM_N), F32)],
        scratch_shapes=[pltpu.VMEM((SSM_W, SSM_N), F32), pltpu.VMEM((8 + L, CONV_C), F32),
                        pltpu.VMEM((L, SSM_W), F32)],
        compiler_params=_params(("arbitrary",)),
    )(proj, proj, proj, proj, proj, conv_w, conv_b, dt_bias, a_log, d_skip, norm_g, *_expand_matrices())


def ssd_bwd(dy, proj, hstates, conv_w, conv_b, dt_bias, a_log, d_skip, norm_g):
    S = proj.shape[0]
    nc = S // L
    cidx = lambda c: nc - 1 - c

    def body(dy_ref, z_ref, xa_ref, xb_ref, xc_ref, dt_ref, cw_ref, cb_ref, dtb_ref, alog_ref, dsk_ref, ng_ref,
             ex_ref, ext_ref, pa_ref, pb_ref, pc_ref, hp_ref,
             dout_ref, dcw_ref, dcb_ref, ddtb_ref, dalog_ref, ddsk_ref, dng_ref,
             dH, ext, dext, ysc, yoffsc, dxdt, dxc, tsc, rsum, csum):
        step = pl.program_id(0)
        c = nc - 1 - step

        @pl.when(step == 0)
        def _():
            dH[...] = jnp.zeros_like(dH)
            dext[L:L + 8, :] = jnp.zeros((8, CONV_C), F32)
            rsum[...] = jnp.zeros_like(rsum)
            csum[...] = jnp.zeros_like(csum)
            for r in (dcw_ref, dcb_ref, ddtb_ref, dalog_ref, ddsk_ref, dng_ref):
                r[...] = jnp.zeros_like(r)

        for k, (ref, prev) in enumerate(((xa_ref, pa_ref), (xb_ref, pb_ref), (xc_ref, pc_ref))):
            ext[0:8, k * 1024:(k + 1) * 1024] = jnp.where(c > 0, prev[8:16, :].astype(F32), 0.0)
            ext[8:8 + L, k * 1024:(k + 1) * 1024] = ref[...].astype(F32)
        dtb = dtb_ref[...]
        dt_raw = dt_ref[...].astype(F32)
        pre, sg_pre, xc, dt, a, acs, taps = _ssd_common(ext, cw_ref, cb_ref, dt_raw, dtb, alog_ref[...])
        xs = xc[:, 0:SSM_W]
        acs_t = acs.T
        ex = ex_ref[...]
        dt_x = _dot_sel(dt, ex, 2)
        xdt = xs * dt_x
        eacs_x = _dot_sel(jnp.exp(acs), ex, 2)
        dte_x = _dot_sel(jnp.exp(acs[L - 1:L, :] - acs), ex, 2)
        xw = xdt * dte_x
        cd_row = jnp.exp(acs[L - 1:L, :])
        tri = _tril()

        Gs, Cs, Bs = [], [], []
        for g in range(SSM_G):
            gs = slice(g * 512, (g + 1) * 512)
            bg = xc[:, SSM_W + g * SSM_N:SSM_W + (g + 1) * SSM_N].astype(BF16)
            cg = xc[:, SSM_W + 512 + g * SSM_N:SSM_W + 512 + (g + 1) * SSM_N].astype(BF16)
            G = _dot_nt(cg, bg)
            Gs.append(G), Cs.append(cg), Bs.append(bg)
            yoffsc[:, gs] = _dot_nt(cg, hp_ref[0, gs, :].astype(BF16)) * eacs_x[:, gs]
            for j in range(8):
                hh = g * 8 + j
                hs = slice(hh * SSM_P, (hh + 1) * SSM_P)
                seg = acs[:, hh:hh + 1] - acs_t[hh:hh + 1, :]
                dk = jnp.where(tri, jnp.exp(seg), 0.0)
                ysc[:, hs] = _dot((G * dk).astype(BF16), xdt[:, hs].astype(BF16))
        d_x = _dot_sel(jnp.broadcast_to(dsk_ref[...], (8, 128)), ex, 3)[0:1, :]
        yoff = yoffsc[...]
        Y = ysc[...] + yoff + d_x * xs

        z = z_ref[...].astype(F32)
        sgz = _sigmoid(z)
        silu_z = z * sgz
        yz = Y * silu_z
        ng = ng_ref[...]
        dout = dy_ref[...]
        dyn = dout * ng
        dyz_parts, dng_parts = [], []
        for g in range(SSM_G):
            gs = slice(g * 512, (g + 1) * 512)
            t = yz[:, gs]
            rstd = lax.rsqrt(jnp.mean(t * t, axis=-1, keepdims=True) + EPS)
            dng_parts.append(jnp.sum(dout[:, gs] * t * rstd, axis=0, keepdims=True))
            dn = dyn[:, gs]
            dyz_parts.append(rstd * dn - t * (rstd * rstd * rstd) * jnp.mean(dn * t, axis=-1, keepdims=True))
        dng_ref[...] += jnp.concatenate(dng_parts, axis=1)
        dyz = jnp.concatenate(dyz_parts, axis=1)
        dY = dyz * silu_z
        dout_ref[:, 0:SSM_W] = (dyz * Y * (sgz * (1.0 + z * (1.0 - sgz)))).astype(BF16)

        ex_t = ext_ref[...]
        ddsk_ref[...] += _dot_sel(jnp.broadcast_to(jnp.sum(dY * xs, axis=0, keepdims=True), (8, SSM_W)), ex_t, 3)[0:1, :]

        lane = lax.broadcasted_iota(jnp.int32, (L, 128), 1)
        last_col = lax.broadcasted_iota(jnp.int32, (1, L), 1) == L - 1
        for g in range(SSM_G):
            gs = slice(g * 512, (g + 1) * 512)
            G, cg, bg = Gs[g], Cs[g], Bs[g]
            hp_g = hp_ref[0, gs, :]
            dh_g = dH[gs, :]
            dY_g = dY[:, gs]
            dZ = dY_g * eacs_x[:, gs]
            dZb = dZ.astype(BF16)
            dC = _dot(dZb, hp_g.astype(BF16))
            dh_from_off = _dot_tn(dZ, cg)
            dhb = dh_g.astype(BF16)
            Q = _dot_nt(bg, dhb)
            dB = _dot(xw[:, gs].astype(BF16), dhb)
            qd = Q * dte_x[:, gs]
            dxdt[:, gs] = qd
            tsc[:, gs] = qd * xdt[:, gs]
            dG = jnp.zeros((L, L), F32)
            for j in range(8):
                hh = g * 8 + j
                hs = slice(hh * SSM_P, (hh + 1) * SSM_P)
                seg = acs[:, hh:hh + 1] - acs_t[hh:hh + 1, :]
                dk = jnp.where(tri, jnp.exp(seg), 0.0)
                M = G * dk
                dYh = dY[:, hs]
                dYhb = dYh.astype(BF16)
                dM = _dot_nt(dYhb, xdt[:, hs].astype(BF16))
                dxdt[:, hs] += _dot_tn(M, dYhb)
                dG = dG + dM * dk
                Wm = dM * M
                pj = slice(j * SSM_P, (j + 1) * SSM_P)
                cd_h = cd_row[:, hh:hh + 1]
                dcd = jnp.sum(dh_g[pj, :] * hp_g[pj, :]) * cd_h
                rsum[:, hh:hh + 1] = jnp.sum(Wm, axis=1, keepdims=True)
                csum[hh:hh + 1, :] = jnp.sum(Wm, axis=0, keepdims=True) - jnp.where(last_col, dcd, 0.0)
                dH[hs, :] = dh_g[pj, :] * cd_h + dh_from_off[pj, :]
            dGb = dG.astype(BF16)
            dC = dC + _dot(dGb, bg)
            dB = dB + _dot_tn(dG, cg)
            dxc[:, SSM_W + g * SSM_N:SSM_W + (g + 1) * SSM_N] = dB
            dxc[:, SSM_W + 512 + g * SSM_N:SSM_W + 512 + (g + 1) * SSM_N] = dC

        row = lax.broadcasted_iota(jnp.int32, (L, 128), 0)
        tv = tsc[...]
        t_last = _dot_sel(jnp.broadcast_to(jnp.sum(tv, axis=0, keepdims=True), (8, SSM_W)), ex_t, 3)[0:1, :]
        dacs = (rsum[...] - csum[...].T + _dot_sel(dY * yoff - tv, ex_t, 2) + jnp.where(row == L - 1, t_last, 0.0))
        triu = lax.broadcasted_iota(jnp.int32, (L, L), 0) <= lax.broadcasted_iota(jnp.int32, (L, L), 1)
        dadt = _sel_dot(triu, dacs, 3)
        dxdt_v = dxdt[...]
        ddt = _dot_sel(dxdt_v * xs, ex_t, 1) + dadt * a
        dalog_ref[...] += jnp.sum(dadt * dt * a, axis=0, keepdims=True)
        ddt_raw = jnp.where(lane < SSM_H, ddt * _sigmoid(dt_raw + dtb), 0.0)
        ddtb_ref[...] += jnp.sum(ddt_raw, axis=0, keepdims=True)
        dout_ref[:, 5120:5248] = ddt_raw.astype(BF16)
        dout_ref[:, 5248:5376] = jnp.zeros((L, 128), BF16)

        dxc[:, 0:SSM_W] = dxdt_v * dt_x + d_x * dY
        dpre = dxc[...] * (sg_pre * (1.0 + pre * (1.0 - sg_pre)))
        dcb_ref[...] += jnp.sum(dpre, axis=0, keepdims=True)
        dext[0:L, :] = dpre
        x_cur = ext[8:8 + L, :]
        dx = None
        for k in range(CONV_K):
            dsh = _rows_from(dext, 3 - k)
            term = cw_ref[k:k + 1, :] * dsh
            dx = term if dx is None else dx + term
            dcw_ref[k:k + 1, :] += jnp.sum(dsh * x_cur, axis=0, keepdims=True)
        dout_ref[:, SSM_W:SSM_W + CONV_C] = dx.astype(BF16)
        dext[L:L + 8, :] = dpre[0:8, :]

    big = lambda w: pl.BlockSpec((L, w), lambda c: (cidx(c), 0))
    return pl.pallas_call(
        body, name="ssd_bwd", grid=(nc,),
        in_specs=[big(SSM_W)] + _ssd_in_specs(True, nc) + [
            pl.BlockSpec((16, 1024), lambda c, k=k: (jnp.maximum(8 * cidx(c) - 1, 0), k)) for k in (2, 3, 4)] + [
            pl.BlockSpec((1, SSM_W, SSM_N), lambda c: (cidx(c), 0, 0))],
        out_specs=[big(5376), _full((8, CONV_C)), _full((1, CONV_C)),
                   _full((1, 128)), _full((1, 128)), _full((1, 128)), _full((1, SSM_W))],
        out_shape=[jax.ShapeDtypeStruct((S, 5376), BF16), jax.ShapeDtypeStruct((8, CONV_C), F32),
                   jax.ShapeDtypeStruct((1, CONV_C), F32), jax.ShapeDtypeStruct((1, 128), F32),
                   jax.ShapeDtypeStruct((1, 128), F32), jax.ShapeDtypeStruct((1, 128), F32),
                   jax.ShapeDtypeStruct((1, SSM_W), F32)],
        scratch_shapes=[pltpu.VMEM((SSM_W, SSM_N), F32), pltpu.VMEM((8 + L, CONV_C), F32),
                        pltpu.VMEM((L + 8, CONV_C), F32), pltpu.VMEM((L, SSM_W), F32),
                        pltpu.VMEM((L, SSM_W), F32), pltpu.VMEM((L, SSM_W), F32),
                        pltpu.VMEM((L, CONV_C), F32), pltpu.VMEM((L, SSM_W), F32),
                        pltpu.VMEM((L, 128), F32), pltpu.VMEM((128, L), F32)],
        compiler_params=_params(("arbitrary",)),
    )(dy, proj, proj, proj, proj, proj, conv_w, conv_b, dt_bias, a_log, d_skip, norm_g, *_expand_matrices(),
      proj, proj, proj, hstates)


def _resident(shape):
    nd = len(shape)
    return pl.BlockSpec(shape, lambda *_: (0,) * nd, pipeline_mode=pl.Buffered(1))


def merge_fwd(y_att, y_sg, y_ssm, proj, x, w_a, w_s, w_m, w_o, g_post):
    S = x.shape[0]
    tm = 256

    def body(ya_ref, ys_ref, ym_ref, gate_ref, x_ref, wa_ref, ws_ref, wm_ref, wo_ref, gp_ref,
             xn_ref, bra_ref, brs_ref, brm_ref, mg_ref, out_ref):
        bra = _dot(ya_ref[...], wa_ref[...])
        brs = _dot(ys_ref[...], ws_ref[...])
        brm = _dot(ym_ref[...], wm_ref[...])
        bra_ref[...] = bra.astype(BF16)
        brs_ref[...] = brs.astype(BF16)
        brm_ref[...] = brm.astype(BF16)
        gate = gate_ref[...].astype(F32)
        merged = (_sigmoid(gate[:, 0:1024]) * bra + _sigmoid(gate[:, 1024:2048]) * brs
                  + _sigmoid(gate[:, 2048:3072]) * brm)
        mb = merged.astype(BF16)
        mg_ref[...] = mb
        o = _dot(mb, wo_ref[...])
        out_ref[...] = o
        r = lax.rsqrt(jnp.mean(o * o, axis=-1, keepdims=True) + EPS)
        xn_ref[...] = x_ref[...] + o * r * gp_ref[...]

    row = lambda w: pl.BlockSpec((tm, w), lambda i: (i, 0))
    return pl.pallas_call(
        body, name="merge_fwd", grid=(S // tm,),
        in_specs=[row(1024), row(1024), row(2048), pl.BlockSpec((tm, 3072), lambda i: (i, 0)),
                  row(D), _resident((1024, D)), _resident((1024, D)), _resident((2048, D)), _resident((D, D)),
                  _full((1, D))],
        out_specs=[row(D)] * 6,
        out_shape=[jax.ShapeDtypeStruct((S, D), F32)] + [jax.ShapeDtypeStruct((S, D), BF16)] * 4
        + [jax.ShapeDtypeStruct((S, D), F32)],
        compiler_params=_params(("arbitrary",)),
    )(y_att, y_sg, y_ssm, proj, x, w_a, w_s, w_m, w_o, g_post)


def merge_bwd(dy, out, g_post, proj, br_a, br_s, br_m, w_a, w_s, w_m, w_o):
    S = dy.shape[0]
    tm = 256

    def body(dy_ref, o_ref, gp_ref, gate_ref, bra_ref, brs_ref, brm_ref, wa_ref, ws_ref, wm_ref, wo_ref,
             dout_ref, dba_ref, dbs_ref, dbm_ref, dgate_ref, dya_ref, dys_ref, dym_ref, dgp_ref):
        @pl.when(pl.program_id(0) == 0)
        def _():
            dgp_ref[...] = jnp.zeros_like(dgp_ref)

        o = o_ref[...]
        dyv = dy_ref[...]
        r = lax.rsqrt(jnp.mean(o * o, axis=-1, keepdims=True) + EPS)
        dyg = dyv * gp_ref[...]
        do = r * dyg - o * (r * r * r) * jnp.mean(dyg * o, axis=-1, keepdims=True)
        dgp_ref[...] += jnp.sum(dyv * o * r, axis=0, keepdims=True)
        dob = do.astype(BF16)
        dout_ref[...] = dob
        dmerged = _dot_nt(dob, wo_ref[...])
        for idx, (br_ref, dbr_ref, w_ref, dyi_ref) in enumerate((
                (bra_ref, dba_ref, wa_ref, dya_ref), (brs_ref, dbs_ref, ws_ref, dys_ref),
                (brm_ref, dbm_ref, wm_ref, dym_ref))):
            s = _sigmoid(gate_ref[:, idx * 1024:(idx + 1) * 1024].astype(F32))
            dbr = (dmerged * s).astype(BF16)
            dbr_ref[...] = dbr
            dgate_ref[:, idx * 1024:(idx + 1) * 1024] = (dmerged * br_ref[...].astype(F32) * s * (1.0 - s)).astype(BF16)
            dyi_ref[...] = _dot_nt(dbr, w_ref[...])

    row = lambda w: pl.BlockSpec((tm, w), lambda i: (i, 0))
    return pl.pallas_call(
        body, name="merge_bwd", grid=(S // tm,),
        in_specs=[row(D), row(D), _full((1, D)), pl.BlockSpec((tm, 3072), lambda i: (i, 0)),
                  row(D), row(D), row(D),
                  _resident((1024, D)), _resident((1024, D)), _resident((2048, D)), _resident((D, D))],
        out_specs=[row(D), row(D), row(D), row(D), row(3072), row(1024), row(1024), row(2048), _full((1, D))],
        out_shape=[jax.ShapeDtypeStruct((S, D), BF16)] * 4 + [
            jax.ShapeDtypeStruct((S, 3072), BF16), jax.ShapeDtypeStruct((S, 1024), F32),
            jax.ShapeDtypeStruct((S, 1024), F32), jax.ShapeDtypeStruct((S, 2048), F32),
            jax.ShapeDtypeStruct((1, D), F32)],
        compiler_params=_params(("arbitrary",)),
    )(dy, out, g_post, proj, br_a, br_s, br_m, w_a, w_s, w_m, w_o)


def loss_head(y, target):
    S = y.shape[0]
    tm = 512

    def body(y_ref, t_ref, dy_ref, loss_ref):
        @pl.when(pl.program_id(0) == 0)
        def _():
            loss_ref[...] = jnp.zeros_like(loss_ref)
        e = y_ref[...] - t_ref[...]
        dy_ref[...] = e * (1.0 / D)
        loss_ref[...] += 0.5 * jnp.sum(jnp.mean(e * e, axis=-1, keepdims=True))

    row = pl.BlockSpec((tm, D), lambda i: (i, 0))
    return pl.pallas_call(
        body, name="loss_head", grid=(S // tm,),
        in_specs=[row, row], out_specs=[row, _full((1, 128))],
        out_shape=[jax.ShapeDtypeStruct((S, D), F32), jax.ShapeDtypeStruct((1, 128), F32)],
        compiler_params=_params(("arbitrary",)),
    )(y, target)


def _adam(w, g, m, v):
    mn = ADAM_B1 * m + (1.0 - ADAM_B1) * g
    vn = ADAM_B2 * v + (1.0 - ADAM_B2) * (g * g)
    m_hat = mn / (1.0 - ADAM_B1 ** ADAM_STEP)
    v_hat = vn / (1.0 - ADAM_B2 ** ADAM_STEP)
    return -ADAM_LR * (m_hat / (jnp.sqrt(v_hat) + ADAM_EPS) + ADAM_WD * w), mn, vn


def adamw_big(w, m, v, halves0, sum1, cc, name, tr):
    _, R, C = w.shape
    nper = R // tr
    f, fb, n0, off_a, off_b = halves0
    p, pb, off1 = sum1

    def body(c_ref, w_ref, m_ref, v_ref, f_ref, fb_ref, p_ref, pb_ref, g_ref, d_ref, nm_ref, nv_ref):
        i = pl.program_id(0)
        half = jnp.where(i % nper >= n0, 1, 0)
        g0 = jnp.where(c_ref[0] == half, f_ref[...], fb_ref[...])
        g = jnp.where(i < nper, g0, p_ref[...] + pb_ref[...])
        g_ref[0] = g
        d_ref[0], nm_ref[0], nv_ref[0] = _adam(w_ref[0], g, m_ref[0], v_ref[0])

    def blk0(i, c):
        il = jnp.minimum(i, nper - 1)
        return (jnp.where(il >= n0, off_b + il - n0, off_a + il), 0)

    wblk = pl.BlockSpec((1, tr, C), lambda i, c: (i // nper, i % nper, 0))
    b0 = pl.BlockSpec((tr, C), blk0)
    b1 = pl.BlockSpec((tr, C), lambda i, c: (off1 + jnp.maximum(i - nper, 0), 0))
    grid_spec = pltpu.PrefetchScalarGridSpec(
        num_scalar_prefetch=1, grid=(2 * nper,),
        in_specs=[wblk, wblk, wblk, b0, b0, b1, b1], out_specs=[wblk] * 4)
    return pl.pallas_call(
        body, name=name, grid_spec=grid_spec,
        out_shape=[jax.ShapeDtypeStruct(w.shape, F32)] * 4,
        compiler_params=_params(("arbitrary",)),
    )(cc, w, m, v, f, fb, p, pb)


def adamw_plain(w, g, m, v, name):
    def body(w_ref, g_ref, m_ref, v_ref, d_ref, nm_ref, nv_ref):
        d_ref[...], nm_ref[...], nv_ref[...] = _adam(w_ref[...], g_ref[...], m_ref[...], v_ref[...])

    return pl.pallas_call(
        body, name=name, out_shape=[jax.ShapeDtypeStruct(w.shape, F32)] * 3, compiler_params=_params(),
    )(w, g, m, v)


SMALL = {"norm_pre": ("g_pre", 8), "norm_post": ("g_post", 8), "att_sinks": ("sinks", 8), "sg_ln_g": ("ln_g", 8),
         "sg_ln_b": ("ln_b", 8), "sg_w": ("sg_w", 1024), "sg_b": ("sg_bt", 8), "ssm_conv_b": ("conv_b", 24),
         "ssm_dt_bias": ("dt_bias", 8), "ssm_a_log": ("a_log", 8), "ssm_d": ("d_skip", 8), "ssm_norm_g": ("norm_g", 16)}
SMALL_LAYER_ROWS = sum(r for _, r in SMALL.values())
REL_ROW = DEPTH * SMALL_LAYER_ROWS
LOSS_ROW = REL_ROW + 32
SMALL_ROWS = LOSS_ROW + 8


def _small_rows():
    rows, r = {}, 0
    for l in range(DEPTH):
        for name, (_, n) in SMALL.items():
            rows[(l, name)] = r
            r += n
    return rows


def adamw_small(red, rel, small):
    names = list(SMALL) + ["rel_bias"]
    params = dict(small, rel_bias=rel)
    rows = _small_rows()

    def grad_of(red_ref, l, name, n):
        r0 = rows[(l, name)]
        if name == "sg_b":
            return red_ref[r0:r0 + 8, :]
        if n < 128:
            return red_ref[r0:r0 + 1, 0:n]
        return jnp.concatenate([red_ref[r0 + j:r0 + j + 1, :] for j in range(n // 128)], axis=1)

    def body(red_ref, *refs):
        ins, outs = refs[:3 * len(names)], refs[3 * len(names):]
        for i, name in enumerate(names):
            w_ref, m_ref, v_ref = ins[3 * i:3 * i + 3]
            o = outs[4 * i:4 * i + 4]
            if name == "rel_bias":
                g = red_ref[REL_ROW:REL_ROW + 32, 0:16]
                o[0][...] = g
                o[1][...], o[2][...], o[3][...] = _adam(w_ref[...], g, m_ref[...], v_ref[...])
                continue
            for l in range(DEPTH):
                if name == "sg_w":
                    for grp in range(8):
                        r0 = rows[(l, name)] + grp * 128
                        g = red_ref[r0:r0 + 128, :]
                        o[0][l, grp] = g
                        o[1][l, grp], o[2][l, grp], o[3][l, grp] = _adam(w_ref[l, grp], g, m_ref[l, grp], v_ref[l, grp])
                elif name == "sg_b":
                    g = grad_of(red_ref, l, name, 128)
                    o[0][l] = g
                    o[1][l], o[2][l], o[3][l] = _adam(w_ref[l], g, m_ref[l], v_ref[l])
                else:
                    sl = slice(l, l + 1)
                    g = grad_of(red_ref, l, name, w_ref.shape[-1])
                    o[0][sl, :] = g
                    o[1][sl, :], o[2][sl, :], o[3][sl, :] = _adam(w_ref[sl, :], g, m_ref[sl, :], v_ref[sl, :])

    flat_in = [a for name in names for a in params[name]]
    out_shape = [jax.ShapeDtypeStruct(params[name][0].shape, F32) for name in names for _ in range(4)]
    res = pl.pallas_call(body, name="adamw_small", out_shape=out_shape, compiler_params=_params())(red, *flat_in)
    return {name: tuple(res[4 * i:4 * i + 4]) for i, name in enumerate(names)}


ANY = pl.BlockSpec(memory_space=pl.ANY)


def _place():
    x, y, c = lax.axis_index("x"), lax.axis_index("y"), lax.axis_index("c")
    others = [(1 - x, y), (x, 1 - y), (1 - x, 1 - y)]
    return x, y, c, others


def _rcopy(src, dst, ssem, rsem, to):
    return pltpu.make_async_remote_copy(src_ref=src, dst_ref=dst, send_sem=ssem, recv_sem=rsem,
                                        device_id=to, device_id_type=MESH)


def gather_weights(arrs):
    n = len(arrs)

    def body(*refs):
        srcs, outs, ssem, rsem = refs[:n], refs[n:2 * n], refs[2 * n], refs[2 * n + 1]
        x, y, c, others = _place()
        me = 2 * x + y
        sib = (x, y, 1 - c)
        first = [_rcopy(srcs[i].at[c], outs[i].at[c, me], ssem.at[6 * i + k], rsem.at[6 * i + k], (ox, oy, c))
                 for i in range(n) for k, (ox, oy) in enumerate(others)]
        for cp in first:
            cp.start()
        passed = []
        for k, (ox, oy) in enumerate(others):
            for i in range(n):
                slot = outs[i].at[c, 2 * ox + oy]
                _rcopy(slot, slot, ssem.at[6 * i + k], rsem.at[6 * i + k], sib).wait_recv()
                fw = _rcopy(slot, slot, ssem.at[6 * i + 3 + k], rsem.at[6 * i + 3 + k], sib)
                fw.start()
                passed.append(fw)
        for k, (ox, oy) in enumerate(others):
            for i in range(n):
                slot = outs[i].at[1 - c, 2 * ox + oy]
                _rcopy(slot, slot, ssem.at[6 * i + 3 + k], rsem.at[6 * i + 3 + k], sib).wait_recv()
        for cp in first + passed:
            cp.wait_send()

    return pl.pallas_call(
        body, name="gather_weights",
        in_specs=[ANY] * n, out_specs=[ANY] * n,
        out_shape=[jax.ShapeDtypeStruct((2, SHARDS) + a.shape[1:], a.dtype) for a in arrs],
        scratch_shapes=[pltpu.SemaphoreType.DMA((6 * n,)), pltpu.SemaphoreType.DMA((6 * n,))],
    )(*arrs)


HBM = pl.BlockSpec(memory_space=pltpu.HBM)
SEM = pl.BlockSpec(memory_space=pltpu.SEMAPHORE)
EFFECT = pltpu.SideEffectType.DATAFLOW_SIDE_EFFECTING


def _in_hbm(a):
    return pltpu.with_memory_space_constraint(a, pltpu.HBM)


def gather_start(srcs, after, name, by_dest=False):
    n = len(srcs)
    lands = [_in_hbm(lax.empty((SHARDS,) + a.shape[-2:], a.dtype)) for a in srcs]
    na = len(after)

    def body(*refs):
        src, land = refs[:n], refs[n:2 * n]
        ssem, rsem, token = refs[2 * n + na], refs[2 * n + na + 1], refs[-1]
        x, y, c, others = _place()
        me = 2 * x + y
        for i in range(n):
            for k, (ox, oy) in enumerate(others):
                s = src[i].at[2 * ox + oy] if by_dest else src[i]
                _rcopy(s, land[i].at[me], ssem.at[3 * i + k], rsem.at[3 * i + k], (ox, oy, c)).start()
        token[...] = jnp.zeros_like(token)

    bufs = [_in_hbm(a) for a in srcs] + lands
    out = pl.pallas_call(
        body, name=name,
        out_shape=(pltpu.SemaphoreType.DMA((3 * n,)), pltpu.SemaphoreType.DMA((3 * n,)),
                   *[pltpu.HBM(b.shape, b.dtype) for b in bufs], jax.ShapeDtypeStruct((8, 128), F32)),
        in_specs=[HBM] * (2 * n) + [ANY] * na,
        out_specs=(SEM, SEM, *[HBM] * (2 * n), pl.BlockSpec(memory_space=pltpu.VMEM)),
        input_output_aliases={i: 2 + i for i in range(2 * n)},
        compiler_params=pltpu.CompilerParams(has_side_effects=EFFECT),
    )(*bufs, *after)
    return out[0], out[1], list(out[2:2 + n]), list(out[2 + n:2 + 2 * n]), out[-1]


def gather_wait(ssem, rsem, srcs, lands, after, name, by_dest=False):
    n = len(srcs)

    def body(*refs):
        src, land = refs[:n], refs[n:2 * n]
        s_sem, r_sem = refs[2 * n], refs[2 * n + 1]
        x, y, c, others = _place()
        for i in range(n):
            for k, (ox, oy) in enumerate(others):
                s = src[i].at[2 * ox + oy] if by_dest else src[i]
                cp = _rcopy(s, land[i].at[2 * ox + oy], s_sem.at[3 * i + k], r_sem.at[3 * i + k], (ox, oy, c))
                cp.wait_send()
                cp.wait_recv()

    bufs = list(srcs) + list(lands)
    out = pl.pallas_call(
        body, name=name,
        out_shape=tuple(pltpu.HBM(b.shape, b.dtype) for b in bufs),
        in_specs=[HBM] * (2 * n) + [SEM, SEM, ANY],
        out_specs=tuple([HBM] * (2 * n)),
        input_output_aliases={i: i for i in range(2 * n)},
        compiler_params=pltpu.CompilerParams(has_side_effects=EFFECT),
    )(*bufs, ssem, rsem, after)
    return list(out[n:2 * n])


def grad_sibling_exchange(arrs):
    n = len(arrs)

    def body(*refs):
        srcs, outs, ssem, rsem = refs[:n], refs[n:2 * n], refs[2 * n], refs[2 * n + 1]
        x, y, c, _ = _place()
        cps = [_rcopy(srcs[i].at[1 - c], outs[i], ssem.at[i], rsem.at[i], (x, y, 1 - c)) for i in range(n)]
        for cp in cps:
            cp.start()
        for cp in cps:
            cp.wait()

    return pl.pallas_call(
        body, name="grad_sibling_exchange",
        in_specs=[ANY] * n, out_specs=[ANY] * n,
        out_shape=[jax.ShapeDtypeStruct(a.shape[1:], F32) for a in arrs],
        scratch_shapes=[pltpu.SemaphoreType.DMA((n,)), pltpu.SemaphoreType.DMA((n,))],
    )(*arrs)


def grad_chip_sum(g, sb, cc, tr, name):
    _, _, R, C = g.shape
    blk = pl.BlockSpec((1, tr, C), lambda s, r, c: (s, r, 0))
    grid_spec = pltpu.PrefetchScalarGridSpec(
        num_scalar_prefetch=1, grid=(SHARDS, R // tr),
        in_specs=[pl.BlockSpec((1, 1, tr, C), lambda s, r, c: (c[0], s, r, 0)), blk],
        out_specs=[blk, blk])

    def body(c_ref, a_ref, b_ref, o_ref, ob_ref):
        t = a_ref[0] + b_ref[...]
        o_ref[...] = t
        ob_ref[...] = t.astype(BF16)

    return pl.pallas_call(
        body, name=name, grid_spec=grid_spec,
        out_shape=[jax.ShapeDtypeStruct((SHARDS, R, C), F32), jax.ShapeDtypeStruct((SHARDS, R, C), BF16)],
        compiler_params=_params(("arbitrary", "arbitrary")),
    )(cc, g, sb)


def grad_shard_sum(t, rb, me, tr, name):
    _, R, C = t.shape
    grid_spec = pltpu.PrefetchScalarGridSpec(
        num_scalar_prefetch=1, grid=(R // tr,),
        in_specs=[pl.BlockSpec((1, tr, C), lambda r, m: (m[0], r, 0)),
                  pl.BlockSpec((SHARDS, tr, C), lambda r, m: (0, r, 0))],
        out_specs=pl.BlockSpec((tr, C), lambda r, m: (r, 0)))

    def body(m_ref, t_ref, r_ref, o_ref):
        part = [jnp.where(m_ref[0] == s, t_ref[0], r_ref[s].astype(F32)) for s in range(SHARDS)]
        o_ref[...] = ((part[0] + part[1]) + part[2]) + part[3]

    return pl.pallas_call(
        body, name=name, grid_spec=grid_spec,
        out_shape=jax.ShapeDtypeStruct((R, C), F32),
        compiler_params=_params(("arbitrary",)),
    )(me, t, rb)


def grad_sibling_share(arrs, name):
    n = len(arrs)

    def body(*refs):
        srcs, outs, ssem, rsem = refs[:n], refs[n:2 * n], refs[2 * n], refs[2 * n + 1]
        x, y, c, _ = _place()
        cps = [_rcopy(srcs[i], outs[i], ssem.at[i], rsem.at[i], (x, y, 1 - c)) for i in range(n)]
        for cp in cps:
            cp.start()
        for cp in cps:
            cp.wait()

    return pl.pallas_call(
        body, name=name,
        in_specs=[ANY] * n, out_specs=[ANY] * n,
        out_shape=[jax.ShapeDtypeStruct(a.shape, F32) for a in arrs],
        scratch_shapes=[pltpu.SemaphoreType.DMA((n,)), pltpu.SemaphoreType.DMA((n,))],
    )(*arrs)


def _allreduce_rows(src, sib_buf, chips, out_ref, ssem, rsem):
    x, y, c, others = _place()
    me = 2 * x + y
    cp = _rcopy(src, sib_buf, ssem.at[0], rsem.at[0], (x, y, 1 - c))
    cp.start()
    cp.wait()
    chips[me] = src[...] + sib_buf[...]
    sends = [_rcopy(chips.at[me], chips.at[me], ssem.at[1 + k], rsem.at[1 + k], (ox, oy, c))
             for k, (ox, oy) in enumerate(others)]
    for s in sends:
        s.start()
    for k, (ox, oy) in enumerate(others):
        slot = chips.at[2 * ox + oy]
        _rcopy(slot, slot, ssem.at[1 + k], rsem.at[1 + k], (ox, oy, c)).wait_recv()
    for s in sends:
        s.wait_send()
    out_ref[...] = ((chips[0] + chips[1]) + chips[2]) + chips[3]


def _allreduce_scratch(rows):
    return [pltpu.VMEM((rows, 128), F32), pltpu.VMEM((SHARDS, rows, 128), F32),
            pltpu.SemaphoreType.DMA((4,)), pltpu.SemaphoreType.DMA((4,))]


def allreduce_rows(buf, name):
    rows = buf.shape[0]
    VM = pl.BlockSpec(memory_space=pltpu.VMEM)

    def body(src_ref, out_ref, sib_buf, chips, ssem, rsem):
        _allreduce_rows(src_ref, sib_buf, chips, out_ref, ssem, rsem)

    return pl.pallas_call(
        body, name=name, in_specs=[VM], out_specs=VM,
        out_shape=jax.ShapeDtypeStruct((rows, 128), F32),
        scratch_shapes=_allreduce_scratch(rows), compiler_params=_params(),
    )(buf)


def small_allreduce(grads, rel, loss_part):
    rows = _small_rows()
    keys = [(l, name) for l in range(DEPTH) for name in SMALL]
    flat = [grads[l][SMALL[name][0]] for l, name in keys] + [rel, loss_part]

    def body(*refs):
        ins = refs[:len(flat)]
        out_ref, src, sib_buf, chips, ssem, rsem = refs[len(flat):]
        src[...] = jnp.zeros_like(src)
        for (l, name), ref in zip(keys, ins):
            r0 = rows[(l, name)]
            if name == "sg_w":
                for grp in range(8):
                    src[r0 + grp * 128:r0 + (grp + 1) * 128, :] = ref[grp]
            elif name == "sg_b":
                src[r0:r0 + 8, :] = ref[...].T[0:8, :]
            else:
                for j in range(ref.shape[1] // 128):
                    src[r0 + j:r0 + j + 1, :] = ref[:, j * 128:(j + 1) * 128]
        src[REL_ROW:REL_ROW + 32, 0:16] = ins[-2][...]
        src[LOSS_ROW:LOSS_ROW + 1, :] = ins[-1][...]
        _allreduce_rows(src, sib_buf, chips, out_ref, ssem, rsem)

    return pl.pallas_call(
        body, name="small_allreduce",
        out_shape=jax.ShapeDtypeStruct((SMALL_ROWS, 128), F32),
        scratch_shapes=[pltpu.VMEM((SMALL_ROWS, 128), F32)] + _allreduce_scratch(SMALL_ROWS),
        compiler_params=_params(),
    )(*flat)


def _pad_lanes(v):
    return jnp.zeros((1, 128), F32).at[0, :v.shape[0]].set(v)


def layer_fwd(x, wts, bias):
    wt = wts["wt"]
    tn = {name: t for name, _, t in GROUPS}
    p_gate, h = inproj_first(x, wts["g_pre"], wt["gate"], tn["gate"], "inproj_gate")
    p_sgu, p_att, p_ssd = (inproj_group(h, wt[n], tn[n], "inproj_" + n, F32 if n == "att" else BF16)
                           for n in ("sgu", "att", "ssd"))
    y_att = att_fwd(p_att, bias, wts["sinks"])
    y_sg = sgu_fwd(p_sgu, wts["ln_g"], wts["ln_b"], wts["sg_w"], wts["sg_bt"])
    y_ssm, hst = ssd_fwd(p_ssd, wts["conv_w"], wts["conv_b"], wts["dt_bias"], wts["a_log"], wts["d_skip"],
                         wts["norm_g"])
    x_new, br_a, br_s, br_m, merged, out = merge_fwd(
        y_att, y_sg, y_ssm, p_gate, x, wts["w_a"], wts["w_s"], wts["w_m"], wts["w_o"], wts["g_post"])
    saved = dict(x=x, p_gate=p_gate, p_sgu=p_sgu, p_att=p_att, p_ssd=p_ssd, h=h,
                 y_att=y_att, y_sg=y_sg, y_ssm=y_ssm, hst=hst,
                 br_a=br_a, br_s=br_s, br_m=br_m, merged=merged, out=out)
    return x_new, saved


def layer_bwd(dy, wts, bias, sv):
    dps, grads = layer_bwd_params(dy, wts, bias, sv)
    dx, grads["g_pre"] = layer_bwd_input(dy, dps, wts, sv, wts["g_pre"])
    return dx, grads


def layer_bwd_input(dy, dps, wts, sv, g_pre):
    wt = wts["wt"]
    tn = {name: t for name, _, t in GROUPS}
    acc = None
    for n in ("gate", "sgu", "ssd"):
        acc = dh_group(dps[n], wt[n], acc, DH_TILE[n], "dh_" + n)
    return dh_last(dps["att"], wt["att"], acc, sv["x"], g_pre, dy, tn["att"], "dh_att")


def layer_bwd_params(dy, wts, bias, sv):
    dout, dba, dbs, dbm, d_gate, dya, dys, dym, dg_post = merge_bwd(
        dy, sv["out"], wts["g_post"], sv["p_gate"], sv["br_a"], sv["br_s"], sv["br_m"],
        wts["w_a"], wts["w_s"], wts["w_m"], wts["w_o"])
    d_att, dbias, dsinks = att_bwd(dya, sv["p_att"], bias, wts["sinks"])
    d_sgu, dsg_w, dsg_bt, dln_g, dln_b = sgu_bwd(dys, sv["p_sgu"], wts["ln_g"], wts["ln_b"], wts["sg_w"],
                                                 wts["sg_bt"])
    d_ssd, dcw, dcb, ddtb, dalog, ddsk, dng = ssd_bwd(
        dym, sv["p_ssd"], sv["hst"], wts["conv_w"], wts["conv_b"], wts["dt_bias"], wts["a_log"], wts["d_skip"],
        wts["norm_g"])
    dps = dict(gate=d_gate, sgu=d_sgu, att=d_att, ssd=d_ssd)
    tn = {name: t for name, _, t in GROUPS}
    grads = dict(
        w_in={n: dw_group(dps[n], sv["h"], tn[n], "dw_in_" + n) for n in dps},
        w_a=matmul_tn(sv["y_att"], dba, "dw_att"),
        w_s=matmul_tn(sv["y_sg"], dbs, "dw_sg"),
        w_m=matmul_tn(sv["y_ssm"], dbm, "dw_ssm"),
        w_o=matmul_tn(sv["merged"], dout, "dw_out"),
        g_post=dg_post, sinks=dsinks, ln_g=dln_g, ln_b=dln_b, sg_w=dsg_w, sg_bt=dsg_bt,
        conv_w=dcw, conv_b=dcb, dt_bias=ddtb, a_log=dalog, d_skip=ddsk, norm_g=dng, bias=dbias)
    return dps, grads


REST_OFF = (0, 256, 512, 1024, 1280)
GR_ROWS = 1536
GR_CONV = 1280
W_IN_SPLIT = 1600
W_IN_HALF = 1824


def kernel(x, w_in, norm_pre, norm_post, rel_bias, att_sinks, sg_ln_g, sg_ln_b, sg_w, sg_b, ssm_conv_w, ssm_conv_b, ssm_dt_bias, ssm_a_log, ssm_d, ssm_norm_g, w_br_att, w_br_sg, w_br_ssm, w_out, loss_target, m_w_in, m_norm_pre, m_norm_post, m_rel_bias, m_att_sinks, m_sg_ln_g, m_sg_ln_b, m_sg_w, m_sg_b, m_ssm_conv_w, m_ssm_conv_b, m_ssm_dt_bias, m_ssm_a_log, m_ssm_d, m_ssm_norm_g, m_w_br_att, m_w_br_sg, m_w_br_ssm, m_w_out, v_w_in, v_norm_pre, v_norm_post, v_rel_bias, v_att_sinks, v_sg_ln_g, v_sg_ln_b, v_sg_w, v_sg_b, v_ssm_conv_w, v_ssm_conv_b, v_ssm_dt_bias, v_ssm_a_log, v_ssm_d, v_ssm_norm_g, v_w_br_att, v_w_br_sg, v_w_br_ssm, v_w_out):
    cx, cy, cc = lax.axis_index("x"), lax.axis_index("y"), lax.axis_index("c")
    me = 2 * cx + cy
    xs = x[0]
    S = xs.shape[0]

    tr = lambda a: jnp.transpose(a, (0, 2, 1))
    w_in_b = tr(w_in).astype(BF16)
    w_rest_b = jnp.concatenate([w_br_att, w_br_sg, w_br_ssm, w_out], axis=1).astype(BF16)
    halves = lambda a: a.reshape(2, a.shape[0] // 2, a.shape[1])
    w_in0 = jnp.pad(w_in_b[0], ((0, W_IN_ROWS - 3400), (0, 0)))
    all0_in, all0_rest = gather_weights([halves(w_in0), halves(w_rest_b[0])])
    convw_slot = jnp.zeros((SHARDS, DEPTH * CONV_K * 768 // 128, 128), F32)
    convw_slot = lax.dynamic_update_index_in_dim(
        convw_slot, jnp.where(cc == 0, 1.0, 0.0) * ssm_conv_w.reshape(-1, 128), me, 0)
    convw_rows = allreduce_rows(convw_slot.reshape(-1, 128), "gather_conv_w")
    convw_all = convw_rows.reshape(SHARDS, DEPTH, CONV_K, 768).transpose(1, 2, 0, 3).reshape(DEPTH, CONV_K, CONV_C)
    g1_ssem, g1_rsem, g1_srcs, g1_lands, g1_token = gather_start(
        [w_in_b[1], w_rest_b[1]], [convw_rows, all0_rest], "gather_l1_start")

    o = REST_OFF

    def layer_weights(l, gathered_in, gathered_rest, g_pre):
        sh_in = [jnp.where(me == s, w_in_b[l], gathered_in[s]) for s in range(SHARDS)]
        sh_rest = [jnp.where(me == s, w_rest_b[l], gathered_rest[s]) for s in range(SHARDS)]
        rest = lambda k: jnp.concatenate([r[o[k]:o[k + 1]] for r in sh_rest], axis=0)
        return dict(
            wt=group_weights(jnp.concatenate(sh_in, axis=0)),
            w_a=rest(0), w_s=rest(1), w_m=rest(2), w_o=rest(3),
            g_pre=g_pre, g_post=norm_post[l][None], sinks=att_sinks[l],
            ln_g=sg_ln_g[l][None], ln_b=sg_ln_b[l][None], sg_w=sg_w[l],
            sg_bt=sg_b[l].T,
            conv_w=jnp.concatenate([convw_all[l], jnp.zeros((4, CONV_C), F32)], axis=0),
            conv_b=ssm_conv_b[l][None], dt_bias=_pad_lanes(ssm_dt_bias[l]), a_log=_pad_lanes(ssm_a_log[l]),
            d_skip=_pad_lanes(ssm_d[l]), norm_g=ssm_norm_g[l][None])

    bias = bias_table(rel_bias)
    layers = [layer_weights(0, [all0_in[:, s].reshape(W_IN_ROWS, D)[0:3400] for s in range(SHARDS)],
                            [all0_rest[:, s].reshape(1280, D) for s in range(SHARDS)],
                            (norm_pre[0] + g1_token[0, 0])[None])]
    act, sv0 = layer_fwd(xs, layers[0], bias)
    land_in, land_rest = gather_wait(g1_ssem, g1_rsem, g1_srcs, g1_lands, act, "gather_l1_wait")
    layers.append(layer_weights(1, land_in, land_rest, norm_pre[1][None]))
    act, sv1 = layer_fwd(act, layers[1], bias)
    saved = [sv0, sv1]
    dy, loss_part = loss_head(act, loss_target[0])
    cvec = jnp.reshape(cc, (1,)).astype(jnp.int32)
    mvec = jnp.reshape(me, (1,)).astype(jnp.int32)

    def by_shard(g):
        gcw = g["conv_w"][0:CONV_K].reshape(CONV_K, SHARDS, 768).transpose(1, 0, 2).reshape(SHARDS, 3, 1024)
        rest = jnp.concatenate([
            g["w_a"].reshape(SHARDS, 256, D), g["w_s"].reshape(SHARDS, 256, D), g["w_o"].reshape(SHARDS, 256, D),
            g["w_m"].reshape(SHARDS, 512, D), jnp.pad(gcw, ((0, 0), (0, GR_ROWS - GR_CONV - 3), (0, 0)))], axis=1)
        return ungroup_grads(g["w_in"]).reshape(SHARDS, 3400, D), rest

    grads = [None] * DEPTH
    dy, grads[1] = layer_bwd(dy, layers[1], bias, saved[1])
    g1_in, g1_rest = by_shard(grads[1])
    g1_in = jnp.pad(g1_in, ((0, 0), (0, W_IN_ROWS - 3400), (0, 0)))
    x1_ssem, x1_rsem, x1_srcs, x1_lands, x1_token = gather_start(
        [g1_in.astype(BF16), g1_rest.astype(BF16)], [], "grads_l1_start", by_dest=True)
    wts0 = dict(layers[0], g_post=layers[0]["g_post"] + x1_token[0, 0])
    dps0, grads[0] = layer_bwd_params(dy, wts0, bias, saved[0])
    r1_in, r1_rest = gather_wait(x1_ssem, x1_rsem, x1_srcs, x1_lands, grads[0]["w_in"]["ssd"], "grads_l1_wait",
                                 by_dest=True)
    p_in = grad_shard_sum(g1_in, r1_in, mvec, 384, "l1_sum_w_in")
    p_rest = grad_shard_sum(g1_rest, r1_rest, mvec, 512, "l1_sum_rest")
    pb_in, pb_rest = grad_sibling_share([p_in, p_rest], "l1_sibling_share")

    g0_in, g0_rest = by_shard(grads[0])
    pad_to = lambda a, rows: jnp.pad(a, ((0, 0), (0, rows - a.shape[1]), (0, 0)))
    g0_in = jnp.stack([pad_to(g0_in[:, 0:W_IN_SPLIT], W_IN_HALF), pad_to(g0_in[:, W_IN_SPLIT:3400], W_IN_HALF)])
    g0_rest = jnp.stack([g0_rest[:, 0:GR_ROWS // 2], g0_rest[:, GR_ROWS // 2:GR_ROWS]])
    sb_in, sb_rest = grad_sibling_exchange([g0_in, g0_rest])
    t_in, t_in_b = grad_chip_sum(g0_in, sb_in, cvec, 608, "chip_sum_w_in")
    t_rest, t_rest_b = grad_chip_sum(g0_rest, sb_rest, cvec, 384, "chip_sum_rest")
    x0_ssem, x0_rsem, x0_srcs, x0_lands, x0_token = gather_start([t_in_b, t_rest_b], [], "grads_l0_start", by_dest=True)
    dy, grads[0]["g_pre"] = layer_bwd_input(dy, dps0, layers[0], saved[0], layers[0]["g_pre"] + x0_token[0, 0])
    grad_x = dy[None]
    rb_in, rb_rest = gather_wait(x0_ssem, x0_rsem, x0_srcs, x0_lands, dy, "grads_l0_wait", by_dest=True)
    grad_rel_local = bias_grad(grads[0]["bias"] + grads[1]["bias"])
    f_in = grad_shard_sum(t_in, rb_in, mvec, 608, "shard_sum_w_in")
    f_rest = grad_shard_sum(t_rest, rb_rest, mvec, 384, "shard_sum_rest")
    fb_in, fb_rest = grad_sibling_share([f_in, f_rest], "l0_sibling_share")

    red = small_allreduce(grads, grad_rel_local, loss_part + 0.0 * f_rest[0:1, 0:128])
    loss = red[LOSS_ROW, 0]

    res = adamw_small(red, (rel_bias, m_rel_bias, v_rel_bias), dict(
        norm_pre=(norm_pre, m_norm_pre, v_norm_pre), norm_post=(norm_post, m_norm_post, v_norm_post),
        att_sinks=(att_sinks, m_att_sinks, v_att_sinks), sg_ln_g=(sg_ln_g, m_sg_ln_g, v_sg_ln_g),
        sg_ln_b=(sg_ln_b, m_sg_ln_b, v_sg_ln_b), sg_w=(sg_w, m_sg_w, v_sg_w), sg_b=(sg_b, m_sg_b, v_sg_b),
        ssm_conv_b=(ssm_conv_b, m_ssm_conv_b, v_ssm_conv_b), ssm_dt_bias=(ssm_dt_bias, m_ssm_dt_bias, v_ssm_dt_bias),
        ssm_a_log=(ssm_a_log, m_ssm_a_log, v_ssm_a_log), ssm_d=(ssm_d, m_ssm_d, v_ssm_d),
        ssm_norm_g=(ssm_norm_g, m_ssm_norm_g, v_ssm_norm_g)))
    res["w_in"] = tuple(tr(a) for a in adamw_big(
        tr(w_in), tr(m_w_in), tr(v_w_in), (f_in, fb_in, W_IN_SPLIT // 200, 0, 0), (p_in, pb_in, 0), cvec, "adamw_w_in", 200))
    rest_upd = lambda w, m, v, name, n0, off0, off1: adamw_big(
        w, m, v, (f_rest, fb_rest, n0, off0, off0), (p_rest, pb_rest, off1), cvec, name, 256)
    res["w_br_att"] = rest_upd(w_br_att, m_w_br_att, v_w_br_att, "adamw_w_br_att", 1, 0, 0)
    res["w_br_sg"] = rest_upd(w_br_sg, m_w_br_sg, v_w_br_sg, "adamw_w_br_sg", 1, 1, 1)
    res["w_out"] = rest_upd(w_out, m_w_out, v_w_out, "adamw_w_out", 1, 2, 2)
    res["w_br_ssm"] = rest_upd(w_br_ssm, m_w_br_ssm, v_w_br_ssm, "adamw_w_br_ssm", 0, 0, 3)
    cw0 = jnp.where(cc == 1, f_rest, fb_rest)[GR_CONV - GR_ROWS // 2:GR_CONV - GR_ROWS // 2 + 3]
    cw1 = (p_rest + pb_rest)[GR_CONV:GR_CONV + 3]
    g_conv_w = jnp.stack([cw0.reshape(CONV_K, 768), cw1.reshape(CONV_K, 768)])
    res["ssm_conv_w"] = (g_conv_w,) + tuple(adamw_plain(ssm_conv_w, g_conv_w, m_ssm_conv_w, v_ssm_conv_w, "adamw_conv_w"))

    order = ["w_in", "norm_pre", "norm_post", "rel_bias", "att_sinks", "sg_ln_g", "sg_ln_b", "sg_w", "sg_b",
             "ssm_conv_w", "ssm_conv_b", "ssm_dt_bias", "ssm_a_log", "ssm_d", "ssm_norm_g",
             "w_br_att", "w_br_sg", "w_br_ssm", "w_out"]
    return (loss, grad_x, *[res[n][0] for n in order], *[res[n][1] for n in order],
            *[res[n][2] for n in order], *[res[n][3] for n in order])
```

```python
import functools
import math

import numpy as np
import jax
import jax.numpy as jnp
from jax import lax
from jax.experimental import pallas as pl
from jax.experimental.pallas import tpu as pltpu

F32 = jnp.float32
BF16 = jnp.bfloat16
MESH = pl.DeviceIdType.MESH

D = 1024
DEPTH = 2
EPS = 1e-6
L = 128
HEADS = 16
KV = 2
DH = 64
SSM_W = 2048
SSM_H = 32
SSM_P = 64
SSM_G = 4
SSM_N = 128
CONV_K = 4
CONV_C = 3072
NEG = -1e30
IN_COLS = 13600

GROUPS = (("gate", 3072, 1536), ("sgu", 3072, 1536), ("att", 2304, 2304), ("ssd", 5376, 1792))
W_IN_ROWS = 3456
DH_TILE = {"gate": 3072, "sgu": 3072, "ssd": 2688}

ADAM_LR = 0.001
ADAM_B1 = 0.9
ADAM_B2 = 0.999
ADAM_EPS = 1e-08
ADAM_WD = 0.01
ADAM_STEP = 10

VMEM_LIMIT = 56 * 1024 * 1024

SHARDS = 4


def _dot(a, b):
    return jnp.dot(a, b, preferred_element_type=F32)


def _dot_nt(a, b):
    return lax.dot_general(a, b, (((1,), (1,)), ((), ())), preferred_element_type=F32)


def _dot_tn(a_f32, b):
    return jnp.dot(a_f32.T.astype(BF16), b, preferred_element_type=F32)


def _dot_t(a, b):
    return lax.dot_general(a, b, (((0,), (0,)), ((), ())), preferred_element_type=F32)


def _dot_hi(a, b):
    return jnp.dot(a, b, preferred_element_type=F32, precision=lax.Precision.HIGHEST)


def _pieces(x, n):
    out = []
    for _ in range(n - 1):
        p = x.astype(BF16)
        out.append(p)
        x = x - p.astype(F32)
    out.append(x.astype(BF16))
    return out


def _dot_sel(a, sel, n):
    sel = sel.astype(BF16)
    acc = None
    for p in _pieces(a, n):
        t = _dot(p, sel)
        acc = t if acc is None else acc + t
    return acc


def _sel_dot(sel, b, n):
    sel = sel.astype(BF16)
    acc = None
    for p in _pieces(b, n):
        t = _dot(sel, p)
        acc = t if acc is None else acc + t
    return acc


def _sigmoid(x):
    return 1.0 / (1.0 + jnp.exp(-x))


def _softplus(x):
    return jnp.maximum(x, 0.0) + jnp.log(1.0 + jnp.exp(-jnp.abs(x)))


def _params(sem=None, vmem=VMEM_LIMIT):
    kw = dict(vmem_limit_bytes=vmem)
    if sem is not None:
        kw["dimension_semantics"] = sem
    return pltpu.CompilerParams(**kw)


def _full(shape):
    nd = len(shape)
    return pl.BlockSpec(shape, lambda *_: (0,) * nd)


def group_weights(wt):
    return dict(
        gate=wt[10528:13600],
        sgu=wt[2304:5376],
        att=jnp.concatenate([wt[0:1024], wt[1280:2304], wt[1024:1280]], axis=0),
        ssd=jnp.concatenate([wt[5376:10496], wt[10496:10528], jnp.zeros((224, D), wt.dtype)], axis=0))


def ungroup_grads(g):
    a, s = g["att"], g["ssd"]
    return jnp.concatenate([a[0:1024], a[2048:2304], a[1024:2048], g["sgu"], s[0:5152], g["gate"]], axis=0)


def _bucket_table():
    qi = np.arange(L)[:, None]
    kj = np.arange(2 * L)[None, :]
    dist = np.maximum(qi + L - kj, 0)
    dist_f = np.maximum(dist, 1).astype(np.float32)
    large = 16 + (np.log(dist_f / np.float32(16)) / np.float32(math.log(128 / 16)) * np.float32(16)).astype(np.int32)
    large = np.minimum(large, 31)
    return np.where(dist < 16, dist, large).astype(np.int32)


def bias_table(rel_bias):
    buckets = jnp.asarray(_bucket_table().reshape(1, L * 2 * L))

    def body(rb_ref, bk_ref, out_ref):
        onehot = (lax.broadcasted_iota(jnp.int32, (32, L * 2 * L), 0) == bk_ref[...]).astype(F32)
        out_ref[...] = lax.dot_general(rb_ref[...], onehot, (((0,), (0,)), ((), ())),
                                       preferred_element_type=F32, precision=lax.Precision.HIGHEST)

    out = pl.pallas_call(
        body, name="bias_table",
        out_shape=jax.ShapeDtypeStruct((HEADS, L * 2 * L), F32),
        compiler_params=_params(),
    )(rel_bias, buckets)
    out = out.reshape(HEADS, L, 2 * L)
    win = _window_mask()
    first = win & (np.arange(2 * L)[None, :] >= L)
    return jnp.stack([jnp.where(first, out, NEG), jnp.where(win, out, NEG)])


def _window_mask():
    dist = np.arange(L)[:, None] + L - np.arange(2 * L)[None, :]
    return (dist >= 0) & (dist < L)


def bias_grad(dbias):
    buckets = jnp.asarray(_bucket_table().reshape(1, L * 2 * L))

    def body(db_ref, bk_ref, out_ref):
        onehot = (lax.broadcasted_iota(jnp.int32, (32, L * 2 * L), 0) == bk_ref[...]).astype(F32)
        out_ref[...] = lax.dot_general(onehot, db_ref[...], (((1,), (1,)), ((), ())),
                                       preferred_element_type=F32, precision=lax.Precision.HIGHEST)

    return pl.pallas_call(
        body, name="bias_grad",
        out_shape=jax.ShapeDtypeStruct((32, HEADS), F32),
        compiler_params=_params(),
    )(dbias.reshape(HEADS, L * 2 * L), buckets)


def _row_tile(S):
    return 1024 if S % 1024 == 0 else 512


def inproj_first(x, g_pre, wt, tn, name):
    S, W = x.shape[0], wt.shape[0]
    tm = _row_tile(S)

    def body(x_ref, g_ref, w_ref, o_ref, h_ref):
        @pl.when(pl.program_id(1) == 0)
        def _():
            xv = x_ref[...]
            r = lax.rsqrt(jnp.mean(xv * xv, axis=-1, keepdims=True) + EPS)
            h_ref[...] = (xv * r * g_ref[...]).astype(BF16)
        o_ref[...] = _dot_nt(h_ref[...], w_ref[...]).astype(BF16)

    return pl.pallas_call(
        body, name=name, grid=(S // tm, W // tn),
        in_specs=[pl.BlockSpec((tm, D), lambda i, j: (i, 0)), _full((1, D)),
                  pl.BlockSpec((tn, D), lambda i, j: (j, 0))],
        out_specs=[pl.BlockSpec((tm, tn), lambda i, j: (i, j)), pl.BlockSpec((tm, D), lambda i, j: (i, 0))],
        out_shape=[jax.ShapeDtypeStruct((S, W), BF16), jax.ShapeDtypeStruct((S, D), BF16)],
        compiler_params=_params(("arbitrary", "arbitrary")),
    )(x, g_pre, wt)


def inproj_group(h, wt, tn, name, dtype):
    S, W = h.shape[0], wt.shape[0]
    tm = _row_tile(S)

    def body(h_ref, w_ref, o_ref):
        o_ref[...] = _dot_nt(h_ref[...], w_ref[...]).astype(dtype)

    return pl.pallas_call(
        body, name=name, grid=(S // tm, W // tn),
        in_specs=[pl.BlockSpec((tm, D), lambda i, j: (i, 0)), pl.BlockSpec((tn, D), lambda i, j: (j, 0))],
        out_specs=pl.BlockSpec((tm, tn), lambda i, j: (i, j)),
        out_shape=jax.ShapeDtypeStruct((S, W), dtype),
        compiler_params=_params(("arbitrary", "arbitrary")),
    )(h, wt)


def dh_group(dp, wt, acc, tk, name):
    S, W = dp.shape
    tm = _row_tile(S)

    def body(*refs):
        dp_ref, w_ref, o_ref = refs[0], refs[1], refs[-1]
        first = pl.program_id(1) == 0
        if acc is None:
            @pl.when(first)
            def _():
                o_ref[...] = jnp.zeros_like(o_ref)
        else:
            @pl.when(first)
            def _():
                o_ref[...] = refs[2][...]
        o_ref[...] += _dot(dp_ref[...], w_ref[...])

    row = pl.BlockSpec((tm, D), lambda i, k: (i, 0))
    return pl.pallas_call(
        body, name=name, grid=(S // tm, W // tk),
        in_specs=[pl.BlockSpec((tm, tk), lambda i, k: (i, k)), pl.BlockSpec((tk, D), lambda i, k: (k, 0))]
        + ([] if acc is None else [row]),
        out_specs=row, out_shape=jax.ShapeDtypeStruct((S, D), F32),
        input_output_aliases={} if acc is None else {2: 0},
        compiler_params=_params(("arbitrary", "arbitrary")),
    )(*((dp, wt) if acc is None else (dp, wt, acc)))


def dh_last(dp, wt, acc_in, x, g_pre, dy, tk, name):
    S, W = dp.shape
    tm = 512
    nk = W // tk

    def body(dp_ref, w_ref, a_ref, x_ref, g_ref, dy_ref, dx_ref, dg_ref, acc):
        i, k = pl.program_id(0), pl.program_id(1)

        @pl.when(k == 0)
        def _():
            acc[...] = a_ref[...]

        acc[...] += _dot(dp_ref[...], w_ref[...])

        @pl.when((k == nk - 1) & (i == 0))
        def _():
            dg_ref[...] = jnp.zeros_like(dg_ref)

        @pl.when(k == nk - 1)
        def _():
            xv = x_ref[...]
            dh = acc[...]
            g = g_ref[...]
            r = lax.rsqrt(jnp.mean(xv * xv, axis=-1, keepdims=True) + EPS)
            dhg = dh * g
            dx_ref[...] = dy_ref[...] + r * dhg - xv * (r * r * r) * jnp.mean(dhg * xv, axis=-1, keepdims=True)
            dg_ref[...] += jnp.sum(dh * xv * r, axis=0, keepdims=True)

    row = pl.BlockSpec((tm, D), lambda i, k: (i, 0))
    return pl.pallas_call(
        body, name=name, grid=(S // tm, nk),
        in_specs=[pl.BlockSpec((tm, tk), lambda i, k: (i, k)), pl.BlockSpec((tk, D), lambda i, k: (k, 0)),
                  row, row, _full((1, D)), row],
        out_specs=[row, _full((1, D))],
        out_shape=[jax.ShapeDtypeStruct((S, D), F32), jax.ShapeDtypeStruct((1, D), F32)],
        scratch_shapes=[pltpu.VMEM((tm, D), F32)],
        compiler_params=_params(("arbitrary", "arbitrary")),
    )(dp, wt, acc_in, x, g_pre, dy)


def dw_group(dp, h, tn, name):
    S, W = dp.shape
    ts = _row_tile(S)

    def body(dp_ref, h_ref, o_ref):
        @pl.when(pl.program_id(1) == 0)
        def _():
            o_ref[...] = jnp.zeros_like(o_ref)
        o_ref[...] += _dot_t(dp_ref[...], h_ref[...])

    return pl.pallas_call(
        body, name=name, grid=(W // tn, S // ts),
        in_specs=[pl.BlockSpec((ts, tn), lambda j, s: (s, j)), pl.BlockSpec((ts, D), lambda j, s: (s, 0))],
        out_specs=pl.BlockSpec((tn, D), lambda j, s: (j, 0)),
        out_shape=jax.ShapeDtypeStruct((W, D), F32),
        compiler_params=_params(("arbitrary", "arbitrary")),
    )(dp, h)


def matmul_tn(a, b, name, tn=1024):
    S, K = a.shape
    N = b.shape[1]
    ts = _row_tile(S)
    ns = S // ts

    def body(a_ref, b_ref, o_ref):
        @pl.when(pl.program_id(1) == 0)
        def _():
            o_ref[...] = jnp.zeros_like(o_ref)
        o_ref[...] += _dot_t(a_ref[...], b_ref[...])

    return pl.pallas_call(
        body, name=name, grid=(N // tn, ns),
        in_specs=[pl.BlockSpec((ts, K), lambda j, s: (s, 0)), pl.BlockSpec((ts, tn), lambda j, s: (s, j))],
        out_specs=pl.BlockSpec((K, tn), lambda j, s: (0, j)),
        out_shape=jax.ShapeDtypeStruct((K, N), F32),
        compiler_params=_params(("arbitrary", "arbitrary")),
    )(a, b)


def _att_in_specs(nb):
    last = nb - 1
    cur = lambda n: jnp.minimum(n, last)
    prev = lambda n: jnp.maximum(jnp.minimum(n, last) - 1, 0)
    return [
        pl.BlockSpec((L, 1024), lambda n: (cur(n), 0)),
        pl.BlockSpec((L, 128), lambda n: (prev(n), 16)),
        pl.BlockSpec((L, 128), lambda n: (cur(n), 16)),
        pl.BlockSpec((L, 128), lambda n: (prev(n), 17)),
        pl.BlockSpec((L, 128), lambda n: (cur(n), 17)),
        pl.BlockSpec((L, 1024), lambda n: (cur(n), 1)),
        _full((2, HEADS, L, 2 * L)),
        pl.BlockSpec(memory_space=pltpu.SMEM),
    ]


GH = HEADS // KV
GB = 8


def _stack_heads(ref, h0, nh, scr):
    for g in range(nh):
        scr[(h0 + g) * L:(h0 + g + 1) * L, :] = ref[:, (h0 + g) * DH:(h0 + g + 1) * DH].astype(F32)
    return scr[h0 * L:(h0 + nh) * L, :]


def _unstack_heads(val, h0, nh, ref):
    for g in range(nh):
        ref[:, (h0 + g) * DH:(h0 + g + 1) * DH] = val[g * L:(g + 1) * L, :]


def _sink_rows(s_ref, h0, nh):
    return jnp.concatenate([jnp.full((L, 1), s_ref[h0 + g], F32) for g in range(nh)], axis=0)


def _att_probs(qh, kk, bias_h, sk):
    logits = _dot_nt(qh, kk) + bias_h
    m =jnp.maximum(jnp.max(logits, axis=-1, keepdims=True), sk)
    p = jnp.exp(logits - m)
    es = jnp.exp(sk - m)
    den = jnp.sum(p, axis=-1, keepdims=True) + es
    return p / den, es / den


def att_fwd(proj, bias, sinks):
    S = proj.shape[0]
    nb = S // L

    def body(q_ref, kp_ref, kc_ref, vp_ref, vc_ref, z_ref, bias_ref, s_ref, y_ref, o_scr):
        table = jnp.where(pl.program_id(0) > 0, 1, 0)
        for kv in range(KV):
            sl = slice(kv * DH, (kv + 1) * DH)
            kk = jnp.concatenate([kp_ref[:, sl], kc_ref[:, sl]], axis=0).astype(BF16)
            vv = jnp.concatenate([vp_ref[:, sl], vc_ref[:, sl]], axis=0).astype(BF16)
            for g in range(GH):
                h = kv * GH + g
                hs = slice(h * DH, (h + 1) * DH)
                qh = (q_ref[:, hs] * 0.125).astype(BF16)
                P, _ = _att_probs(qh, kk, bias_ref[table, h], s_ref[h])
                o_scr[:, hs] = _dot(P.astype(BF16), vv)
        z = z_ref[...].astype(F32)
        y_ref[...] = (o_scr[...] * (z * _sigmoid(z))).astype(BF16)

    return pl.pallas_call(
        body, name="att_fwd", grid=(nb,),
        in_specs=_att_in_specs(nb),
        out_specs=pl.BlockSpec((L, 1024), lambda n: (n, 0)),
        out_shape=jax.ShapeDtypeStruct((S, 1024), BF16),
        scratch_shapes=[pltpu.VMEM((L, 1024), F32)],
        compiler_params=_params(("arbitrary",)),
    )(proj, proj, proj, proj, proj, proj, bias, sinks)


def att_bwd(dy, proj, bias, sinks):
    S = proj.shape[0]
    nb = S // L
    last = nb - 1

    def body(dy_ref, q_ref, kp_ref, kc_ref, vp_ref, vc_ref, z_ref, bias_ref, s_ref,
             dout_ref, dbias_ref, dsink_ref, carry, band, dq_scr, dz_scr, qs_scr, zs_scr, dys_scr):
        n = pl.program_id(0)

        @pl.when(n == 0)
        def _():
            carry[...] = jnp.zeros_like(carry)
            dq_scr[...] = jnp.zeros_like(dq_scr)
            dz_scr[...] = jnp.zeros_like(dz_scr)
            dbias_ref[...] = jnp.zeros_like(dbias_ref)
            dsink_ref[...] = jnp.zeros_like(dsink_ref)

        dout_ref[:, 0:1024] = dq_scr[...].astype(BF16)
        dout_ref[:, 1024:2048] = dz_scr[...].astype(BF16)
        band[...] = jnp.zeros_like(band)

        @pl.when(n < nb)
        def _():
            table = jnp.where(n > 0, 1, 0)
            lane = lax.broadcasted_iota(jnp.int32, (1, 128), 1)
            dsink = jnp.zeros((1, 128), F32)
            for kv in range(KV):
                sl = slice(kv * DH, (kv + 1) * DH)
                kk = jnp.concatenate([kp_ref[:, sl], kc_ref[:, sl]], axis=0).astype(BF16)
                vv = jnp.concatenate([vp_ref[:, sl], vc_ref[:, sl]], axis=0).astype(BF16)
                dk_acc = jnp.zeros((2 * L, DH), F32)
                dv_acc = jnp.zeros((2 * L, DH), F32)
                for h0 in range(kv * GH, (kv + 1) * GH, GB):
                    qs = (_stack_heads(q_ref, h0, GB, qs_scr) * 0.125).astype(BF16)
                    bias_g = bias_ref[table, h0:h0 + GB].reshape(GB * L, 2 * L)
                    P, psink = _att_probs(qs, kk, bias_g, _sink_rows(s_ref, h0, GB))
                    zs = _stack_heads(z_ref, h0, GB, zs_scr)
                    dys = _stack_heads(dy_ref, h0, GB, dys_scr)
                    sg = _sigmoid(zs)
                    O = _dot(P.astype(BF16), vv)
                    _unstack_heads(dys * O * (sg * (1.0 + zs * (1.0 - sg))), h0, GB, dz_scr)
                    dOb = (dys * (zs * sg)).astype(BF16)
                    dP = _dot_nt(dOb, vv)
                    delta = jnp.sum(P * dP, axis=-1, keepdims=True)
                    dS = P * (dP - delta)
                    sd = psink * delta
                    for g in range(GB):
                        dsink = dsink + jnp.where(lane == h0 + g, -jnp.sum(sd[g * L:(g + 1) * L, :]), 0.0)
                    _unstack_heads(_dot(dS.astype(BF16), kk) * 0.125, h0, GB, dq_scr)
                    dbias_ref[h0:h0 + GB] += dS.reshape(GB, L, 2 * L)
                    dk_acc = dk_acc + _dot_tn(dS, qs)
                    dv_acc = dv_acc + _dot_tn(P, dOb)
                band[:, sl] = dk_acc
                band[:, 128 + kv * DH:128 + (kv + 1) * DH] = dv_acc
            dsink_ref[...] += dsink

        out = carry[...] + band[0:L, :]
        dout_ref[:, 2048:2304] = out.astype(BF16)
        carry[...] = band[L:2 * L, :]

    cur = lambda n: jnp.minimum(n, last)
    lag = lambda n: jnp.maximum(n - 1, 0)
    return pl.pallas_call(
        body, name="att_bwd", grid=(nb + 1,),
        in_specs=[pl.BlockSpec((L, 1024), lambda n: (cur(n), 0))] + _att_in_specs(nb),
        out_specs=[pl.BlockSpec((L, 2304), lambda n: (lag(n), 0)), _full((HEADS, L, 2 * L)), _full((1, 128))],
        out_shape=[jax.ShapeDtypeStruct((S, 2304), BF16),
                   jax.ShapeDtypeStruct((HEADS, L, 2 * L), F32), jax.ShapeDtypeStruct((1, 128), F32)],
        scratch_shapes=[pltpu.VMEM((L, 256), F32), pltpu.VMEM((2 * L, 256), F32),
                        pltpu.VMEM((L, 1024), F32), pltpu.VMEM((L, 1024), F32)]
        + [pltpu.VMEM((HEADS * L, DH), F32)] * 3,
        compiler_params=_params(("arbitrary",)),
    )(dy, proj, proj, proj, proj, proj, proj, bias, sinks)


SGU_CH = 4


def _sgu_in_specs():
    return [
        pl.BlockSpec((SGU_CH * L, 1024), lambda c: (c, 0)),
        pl.BlockSpec((SGU_CH * L, 1024), lambda c: (c, 1)),
        pl.BlockSpec((SGU_CH * L, 1024), lambda c: (c, 2)),
        _full((1, 1024)), _full((1, 1024)), _full((8, L, L)), _full((L, 8)),
    ]


def _sgu_norm(v, lg, lb):
    mu = jnp.mean(v, axis=-1, keepdims=True)
    vc = v - mu
    rstd = lax.rsqrt(jnp.mean(vc * vc, axis=-1, keepdims=True) + EPS)
    xhat = vc * rstd
    return xhat * lg + lb, xhat, rstd


def _tril():
    return lax.broadcasted_iota(jnp.int32, (L, L), 0) >= lax.broadcasted_iota(jnp.int32, (L, L), 1)


def _sgu_side(a, g):
    return jnp.concatenate([a[c * L:(c + 1) * L, g * 128:(g + 1) * 128] for c in range(SGU_CH)], axis=1)


def _sgu_stack(parts):
    return jnp.concatenate([jnp.concatenate([p[:, c * L:(c + 1) * L] for p in parts], axis=1)
                            for c in range(SGU_CH)], axis=0)


def sgu_fwd(proj, ln_g, ln_b, w, b_t):
    S = proj.shape[0]

    def body(u_ref, v_ref, z_ref, lg_ref, lb_ref, w_ref, bt_ref, y_ref):
        vn, _, _ = _sgu_norm(v_ref[...].astype(F32), lg_ref[...], lb_ref[...])
        tri = _tril()
        parts = []
        for g in range(8):
            wg = jnp.where(tri, w_ref[g], 0.0).astype(BF16)
            parts.append(_dot(wg, _sgu_side(vn, g).astype(BF16)) + bt_ref[:, g:g + 1])
        mixed = _sgu_stack(parts)
        z = z_ref[...].astype(F32)
        y_ref[...] = (u_ref[...].astype(F32) * mixed * (z * _sigmoid(z))).astype(BF16)

    return pl.pallas_call(
        body, name="sgu_fwd", grid=(S // (SGU_CH * L),),
        in_specs=_sgu_in_specs(),
        out_specs=pl.BlockSpec((SGU_CH * L, 1024), lambda c: (c, 0)),
        out_shape=jax.ShapeDtypeStruct((S, 1024), BF16),
        compiler_params=_params(("arbitrary",)),
    )(proj, proj, proj, ln_g, ln_b, w, b_t)


def sgu_bwd(dy, proj, ln_g, ln_b, w, b_t):
    S = proj.shape[0]

    def body(dy_ref, u_ref, v_ref, z_ref, lg_ref, lb_ref, w_ref, bt_ref,
             dout_ref, dw_ref, dbt_ref, dlg_ref, dlb_ref):
        @pl.when(pl.program_id(0) == 0)
        def _():
            dw_ref[...] = jnp.zeros_like(dw_ref)
            dbt_ref[...] = jnp.zeros_like(dbt_ref)
            dlg_ref[...] = jnp.zeros_like(dlg_ref)
            dlb_ref[...] = jnp.zeros_like(dlb_ref)

        lg = lg_ref[...]
        vn, xhat, rstd = _sgu_norm(v_ref[...].astype(F32), lg, lb_ref[...])
        tri = _tril()
        lane = lax.broadcasted_iota(jnp.int32, (L, 128), 1)
        wgs, vns, parts = [], [], []
        for g in range(8):
            wg = jnp.where(tri, w_ref[g], 0.0)
            wgs.append(wg)
            vns.append(_sgu_side(vn, g).astype(BF16))
            parts.append(_dot(wg.astype(BF16), vns[g]) + bt_ref[:, g:g + 1])
        mixed = _sgu_stack(parts)
        z = z_ref[...].astype(F32)
        sg = _sigmoid(z)
        silu = z * sg
        dy_v = dy_ref[...]
        u = u_ref[...].astype(F32)
        dout_ref[:, 0:1024] = (dy_v * mixed * silu).astype(BF16)
        dout_ref[:, 2048:3072] = (dy_v * u * mixed * (sg * (1.0 + z * (1.0 - sg)))).astype(BF16)
        dmixed = dy_v * u * silu
        dbt = jnp.zeros((L, 128), F32)
        dvn_parts = []
        for g in range(8):
            dm = _sgu_side(dmixed, g)
            dmb = dm.astype(BF16)
            dbt = dbt + jnp.where(lane == g, jnp.sum(dm, axis=1, keepdims=True), 0.0)
            dw_ref[g] += jnp.where(tri, _dot_nt(dmb, vns[g]), 0.0)
            dvn_parts.append(_dot_tn(wgs[g], dmb))
        dbt_ref[...] += dbt
        dvn = _sgu_stack(dvn_parts)
        dlg_ref[...] += jnp.sum(dvn * xhat, axis=0, keepdims=True)
        dlb_ref[...] += jnp.sum(dvn, axis=0, keepdims=True)
        dxh = dvn * lg
        dv = rstd * (dxh - jnp.mean(dxh, axis=-1, keepdims=True)
                     - xhat * jnp.mean(dxh * xhat, axis=-1, keepdims=True))
        dout_ref[:, 1024:2048] = dv.astype(BF16)

    return pl.pallas_call(
        body, name="sgu_bwd", grid=(S // (SGU_CH * L),),
        in_specs=[pl.BlockSpec((SGU_CH * L, 1024), lambda c: (c, 0))] + _sgu_in_specs(),
        out_specs=[pl.BlockSpec((SGU_CH * L, 3072), lambda c: (c, 0)), _full((8, L, L)), _full((L, 128)),
                   _full((1, 1024)), _full((1, 1024))],
        out_shape=[jax.ShapeDtypeStruct((S, 3072), BF16), jax.ShapeDtypeStruct((8, L, L), F32),
                   jax.ShapeDtypeStruct((L, 128), F32), jax.ShapeDtypeStruct((1, 1024), F32),
                   jax.ShapeDtypeStruct((1, 1024), F32)],
        compiler_params=_params(("arbitrary",)),
    )(dy, proj, proj, proj, ln_g, ln_b, w, b_t)


def _expand_matrices():
    e = (np.arange(SSM_W)[None, :] // SSM_P == np.arange(128)[:, None]).astype(np.float32)
    return jnp.asarray(e, BF16), jnp.asarray(e.T, BF16)


def _rows_from(ref, start):
    C = ref.shape[1]
    tiles = ref[...].reshape(17, 8, C)
    q, s = divmod(start, 8)
    if s == 0:
        return tiles[q:q + 16].reshape(L, C)
    rolled = pltpu.roll(tiles, 8 - s, axis=1)
    sub = lax.broadcasted_iota(jnp.int32, (16, 8, C), 1)
    return jnp.where(sub < 8 - s, rolled[q:q + 16], rolled[q + 1:q + 17]).reshape(L, C)


def _ssd_common(ext_ref, cw_ref, cb_ref, dt_raw, dtb, alog):
    taps = [_rows_from(ext_ref, 5 + k) for k in range(CONV_K)]
    pre = cb_ref[...]
    for k in range(CONV_K):
        pre = pre + cw_ref[k:k + 1, :] * taps[k]
    sg_pre = _sigmoid(pre)
    xc = pre * sg_pre
    dt = _softplus(dt_raw + dtb)
    a = -jnp.exp(alog)
    adt = dt * a
    acs = _sel_dot(_tril(), adt, 3)
    return pre, sg_pre, xc, dt, a, acs, taps


def _ssd_in_specs(rev, nc):
    cidx = (lambda c: nc - 1 - c) if rev else (lambda c: c)
    return [
        pl.BlockSpec((L, 2048), lambda c: (cidx(c), 0)),
        pl.BlockSpec((L, 1024), lambda c: (cidx(c), 2)),
        pl.BlockSpec((L, 1024), lambda c: (cidx(c), 3)),
        pl.BlockSpec((L, 1024), lambda c: (cidx(c), 4)),
        pl.BlockSpec((L, 128), lambda c: (cidx(c), 40)),
        _full((8, CONV_C)), _full((1, CONV_C)), _full((1, 128)), _full((1, 128)), _full((1, 128)),
        _full((1, SSM_W)), _full((128, SSM_W)), _full((SSM_W, 128)),
    ]


def ssd_fwd(proj, conv_w, conv_b, dt_bias, a_log, d_skip, norm_g):
    S = proj.shape[0]
    nc = S // L

    def body(z_ref, xa_ref, xb_ref, xc_ref, dt_ref, cw_ref, cb_ref, dtb_ref, alog_ref, dsk_ref, ng_ref,
             ex_ref, ext_ref, y_ref, hs_ref, H, ext, ysc):
        @pl.when(pl.program_id(0) == 0)
        def _():
            H[...] = jnp.zeros_like(H)
            ext[0:8, :] = jnp.zeros((8, CONV_C), F32)

        for k, ref in enumerate((xa_ref, xb_ref, xc_ref)):
            ext[8:8 + L, k * 1024:(k + 1) * 1024] = ref[...].astype(F32)
        pre, sg_pre, xc, dt, a, acs, _ = _ssd_common(ext, cw_ref, cb_ref, dt_ref[...].astype(F32), dtb_ref[...],
                                                     alog_ref[...])
        for k, ref in enumerate((xa_ref, xb_ref, xc_ref)):
            ext[0:8, k * 1024:(k + 1) * 1024] = ref[L - 8:L, :].astype(F32)
        xs = xc[:, 0:SSM_W]
        acs_t = acs.T
        ex = ex_ref[...]
        dt_x = _dot_sel(dt, ex, 2)
        xdt = xs * dt_x
        eacs_x = _dot_sel(jnp.exp(acs), ex, 2)
        xw = xdt * _dot_sel(jnp.exp(acs[L - 1:L, :] - acs), ex, 2)
        cd_row = jnp.exp(acs[L - 1:L, :])
        hs_ref[0] = H[...]
        tri = _tril()
        for g in range(SSM_G):
            gs = slice(g * 512, (g + 1) * 512)
            bg = xc[:, SSM_W + g * SSM_N:SSM_W + (g + 1) * SSM_N].astype(BF16)
            cg = xc[:, SSM_W + 512 + g * SSM_N:SSM_W + 512 + (g + 1) * SSM_N].astype(BF16)
            G = _dot_nt(cg, bg)
            yoff = _dot_nt(cg, H[gs, :].astype(BF16)) * eacs_x[:, gs]
            Sg = _dot_tn(xw[:, gs], bg)
            for j in range(8):
                hh = g * 8 + j
                hs = slice(hh * SSM_P, (hh + 1) * SSM_P)
                seg = acs[:, hh:hh + 1] - acs_t[hh:hh + 1, :]
                dk = jnp.where(tri, jnp.exp(seg), 0.0)
                yd = _dot((G * dk).astype(BF16), xdt[:, hs].astype(BF16))
                ysc[:, hs] = yd + yoff[:, j * SSM_P:(j + 1) * SSM_P]
                H[hs, :] = H[hs, :] * cd_row[:, hh:hh + 1] + Sg[j * SSM_P:(j + 1) * SSM_P, :]
        d_x = _dot_sel(jnp.broadcast_to(dsk_ref[...], (8, 128)), ex, 3)[0:1, :]
        Y = ysc[...] + d_x * xs
        z = z_ref[...].astype(F32)
        yz = Y * (z * _sigmoid(z))
        ng = ng_ref[...]
        for g in range(SSM_G):
            gs = slice(g * 512, (g + 1) * 512)
            t = yz[:, gs]
            rstd = lax.rsqrt(jnp.mean(t * t, axis=-1, keepdims=True) + EPS)
            y_ref[:, gs] = (t * rstd * ng[:, gs]).astype(BF16)

    return pl.pallas_call(
        body, name="ssd_fwd", grid=(nc,),
        in_specs=_ssd_in_specs(False, nc),
        out_specs=[pl.BlockSpec((L, SSM_W), lambda c: (c, 0)), pl.BlockSpec((1, SSM_W, SSM_N), lambda c: (c, 0, 0))],
        out_shape=[jax.ShapeDtypeStruct((S, SSM_W), BF16), jax.ShapeDtypeStruct((nc, SSM_W, SSM_N), F32)],
        scratch_shapes=[pltpu.VMEM((SSM_W, SSM_N), F32), pltpu.VMEM((8 + L, CONV_C), F32),
                        pltpu.VMEM((L, SSM_W), F32)],
        compiler_params=_params(("arbitrary",)),
    )(proj, proj, proj, proj, proj, conv_w, conv_b, dt_bias, a_log, d_skip, norm_g, *_expand_matrices())


def ssd_bwd(dy, proj, hstates, conv_w, conv_b, dt_bias, a_log, d_skip, norm_g):
    S = proj.shape[0]
    nc = S // L
    cidx = lambda c: nc - 1 - c

    def body(dy_ref, z_ref, xa_ref, xb_ref, xc_ref, dt_ref, cw_ref, cb_ref, dtb_ref, alog_ref, dsk_ref, ng_ref,
             ex_ref, ext_ref, pa_ref, pb_ref, pc_ref, hp_ref,
             dout_ref, dcw_ref, dcb_ref, ddtb_ref, dalog_ref, ddsk_ref, dng_ref,
             dH, ext, dext, ysc, yoffsc, dxdt, dxc, tsc, rsum, csum):
        step = pl.program_id(0)
        c = nc - 1 - step

        @pl.when(step == 0)
        def _():
            dH[...] = jnp.zeros_like(dH)
            dext[L:L + 8, :] = jnp.zeros((8, CONV_C), F32)
            rsum[...] = jnp.zeros_like(rsum)
            csum[...] = jnp.zeros_like(csum)
            for r in (dcw_ref, dcb_ref, ddtb_ref, dalog_ref, ddsk_ref, dng_ref):
                r[...] = jnp.zeros_like(r)

        for k, (ref, prev) in enumerate(((xa_ref, pa_ref), (xb_ref, pb_ref), (xc_ref, pc_ref))):
            ext[0:8, k * 1024:(k + 1) * 1024] = jnp.where(c > 0, prev[8:16, :].astype(F32), 0.0)
            ext[8:8 + L, k * 1024:(k + 1) * 1024] = ref[...].astype(F32)
        dtb = dtb_ref[...]
        dt_raw = dt_ref[...].astype(F32)
        pre, sg_pre, xc, dt, a, acs, taps = _ssd_common(ext, cw_ref, cb_ref, dt_raw, dtb, alog_ref[...])
        xs = xc[:, 0:SSM_W]
        acs_t = acs.T
        ex = ex_ref[...]
        dt_x = _dot_sel(dt, ex, 2)
        xdt = xs * dt_x
        eacs_x = _dot_sel(jnp.exp(acs), ex, 2)
        dte_x = _dot_sel(jnp.exp(acs[L - 1:L, :] - acs), ex, 2)
        xw = xdt * dte_x
        cd_row = jnp.exp(acs[L - 1:L, :])
        tri = _tril()

        Gs, Cs, Bs = [], [], []
        for g in range(SSM_G):
            gs = slice(g * 512, (g + 1) * 512)
            bg = xc[:, SSM_W + g * SSM_N:SSM_W + (g + 1) * SSM_N].astype(BF16)
            cg = xc[:, SSM_W + 512 + g * SSM_N:SSM_W + 512 + (g + 1) * SSM_N].astype(BF16)
            G = _dot_nt(cg, bg)
            Gs.append(G), Cs.append(cg), Bs.append(bg)
            yoffsc[:, gs] = _dot_nt(cg, hp_ref[0, gs, :].astype(BF16)) * eacs_x[:, gs]
            for j in range(8):
                hh = g * 8 + j
                hs = slice(hh * SSM_P, (hh + 1) * SSM_P)
                seg = acs[:, hh:hh + 1] - acs_t[hh:hh + 1, :]
                dk = jnp.where(tri, jnp.exp(seg), 0.0)
                ysc[:, hs] = _dot((G * dk).astype(BF16), xdt[:, hs].astype(BF16))
        d_x = _dot_sel(jnp.broadcast_to(dsk_ref[...], (8, 128)), ex, 3)[0:1, :]
        yoff = yoffsc[...]
        Y = ysc[...] + yoff + d_x * xs

        z = z_ref[...].astype(F32)
        sgz = _sigmoid(z)
        silu_z = z * sgz
        yz = Y * silu_z
        ng = ng_ref[...]
        dout = dy_ref[...]
        dyn = dout * ng
        dyz_parts, dng_parts = [], []
        for g in range(SSM_G):
            gs = slice(g * 512, (g + 1) * 512)
            t = yz[:, gs]
            rstd = lax.rsqrt(jnp.mean(t * t, axis=-1, keepdims=True) + EPS)
            dng_parts.append(jnp.sum(dout[:, gs] * t * rstd, axis=0, keepdims=True))
            dn = dyn[:, gs]
            dyz_parts.append(rstd * dn - t * (rstd * rstd * rstd) * jnp.mean(dn * t, axis=-1, keepdims=True))
        dng_ref[...] += jnp.concatenate(dng_parts, axis=1)
        dyz = jnp.concatenate(dyz_parts, axis=1)
        dY = dyz * silu_z
        dout_ref[:, 0:SSM_W] = (dyz * Y * (sgz * (1.0 + z * (1.0 - sgz)))).astype(BF16)

        ex_t = ext_ref[...]
        ddsk_ref[...] += _dot_sel(jnp.broadcast_to(jnp.sum(dY * xs, axis=0, keepdims=True), (8, SSM_W)), ex_t, 3)[0:1, :]

        lane = lax.broadcasted_iota(jnp.int32, (L, 128), 1)
        last_col = lax.broadcasted_iota(jnp.int32, (1, L), 1) == L - 1
        for g in range(SSM_G):
            gs = slice(g * 512, (g + 1) * 512)
            G, cg, bg = Gs[g], Cs[g], Bs[g]
            hp_g = hp_ref[0, gs, :]
            dh_g = dH[gs, :]
            dY_g = dY[:, gs]
            dZ = dY_g * eacs_x[:, gs]
            dZb = dZ.astype(BF16)
            dC = _dot(dZb, hp_g.astype(BF16))
            dh_from_off = _dot_tn(dZ, cg)
            dhb = dh_g.astype(BF16)
            Q = _dot_nt(bg, dhb)
            dB = _dot(xw[:, gs].astype(BF16), dhb)
            qd = Q * dte_x[:, gs]
            dxdt[:, gs] = qd
            tsc[:, gs] = qd * xdt[:, gs]
            dG = jnp.zeros((L, L), F32)
            for j in range(8):
                hh = g * 8 + j
                hs = slice(hh * SSM_P, (hh + 1) * SSM_P)
                seg = acs[:, hh:hh + 1] - acs_t[hh:hh + 1, :]
                dk = jnp.where(tri, jnp.exp(seg), 0.0)
                M = G * dk
                dYh = dY[:, hs]
                dYhb = dYh.astype(BF16)
                dM = _dot_nt(dYhb, xdt[:, hs].astype(BF16))
                dxdt[:, hs] += _dot_tn(M, dYhb)
                dG = dG + dM * dk
                Wm = dM * M
                pj = slice(j * SSM_P, (j + 1) * SSM_P)
                cd_h = cd_row[:, hh:hh + 1]
                dcd = jnp.sum(dh_g[pj, :] * hp_g[pj, :]) * cd_h
                rsum[:, hh:hh + 1] = jnp.sum(Wm, axis=1, keepdims=True)
                csum[hh:hh + 1, :] = jnp.sum(Wm, axis=0, keepdims=True) - jnp.where(last_col, dcd, 0.0)
                dH[hs, :] = dh_g[pj, :] * cd_h + dh_from_off[pj, :]
            dGb = dG.astype(BF16)
            dC = dC + _dot(dGb, bg)
            dB = dB + _dot_tn(dG, cg)
            dxc[:, SSM_W + g * SSM_N:SSM_W + (g + 1) * SSM_N] = dB
            dxc[:, SSM_W + 512 + g * SSM_N:SSM_W + 512 + (g + 1) * SSM_N] = dC

        row = lax.broadcasted_iota(jnp.int32, (L, 128), 0)
        tv = tsc[...]
        t_last = _dot_sel(jnp.broadcast_to(jnp.sum(tv, axis=0, keepdims=True), (8, SSM_W)), ex_t, 3)[0:1, :]
        dacs = (rsum[...] - csum[...].T + _dot_sel(dY * yoff - tv, ex_t, 2) + jnp.where(row == L - 1, t_last, 0.0))
        triu = lax.broadcasted_iota(jnp.int32, (L, L), 0) <= lax.broadcasted_iota(jnp.int32, (L, L), 1)
        dadt = _sel_dot(triu, dacs, 3)
        dxdt_v = dxdt[...]
        ddt = _dot_sel(dxdt_v * xs, ex_t, 1) + dadt * a
        dalog_ref[...] += jnp.sum(dadt * dt * a, axis=0, keepdims=True)
        ddt_raw = jnp.where(lane < SSM_H, ddt * _sigmoid(dt_raw + dtb), 0.0)
        ddtb_ref[...] += jnp.sum(ddt_raw, axis=0, keepdims=True)
        dout_ref[:, 5120:5248] = ddt_raw.astype(BF16)
        dout_ref[:, 5248:5376] = jnp.zeros((L, 128), BF16)

        dxc[:, 0:SSM_W] = dxdt_v * dt_x + d_x * dY
        dpre = dxc[...] * (sg_pre * (1.0 + pre * (1.0 - sg_pre)))
        dcb_ref[...] += jnp.sum(dpre, axis=0, keepdims=True)
        dext[0:L, :] = dpre
        x_cur = ext[8:8 + L, :]
        dx = None
        for k in range(CONV_K):
            dsh = _rows_from(dext, 3 - k)
            term = cw_ref[k:k + 1, :] * dsh
            dx = term if dx is None else dx + term
            dcw_ref[k:k + 1, :] += jnp.sum(dsh * x_cur, axis=0, keepdims=True)
        dout_ref[:, SSM_W:SSM_W + CONV_C] = dx.astype(BF16)
        dext[L:L + 8, :] = dpre[0:8, :]

    big = lambda w: pl.BlockSpec((L, w), lambda c: (cidx(c), 0))
    return pl.pallas_call(
        body, name="ssd_bwd", grid=(nc,),
        in_specs=[big(SSM_W)] + _ssd_in_specs(True, nc) + [
            pl.BlockSpec((16, 1024), lambda c, k=k: (jnp.maximum(8 * cidx(c) - 1, 0), k)) for k in (2, 3, 4)] + [
            pl.BlockSpec((1, SSM_W, SSM_N), lambda c: (cidx(c), 0, 0))],
        out_specs=[big(5376), _full((8, CONV_C)), _full((1, CONV_C)),
                   _full((1, 128)), _full((1, 128)), _full((1, 128)), _full((1, SSM_W))],
        out_shape=[jax.ShapeDtypeStruct((S, 5376), BF16), jax.ShapeDtypeStruct((8, CONV_C), F32),
                   jax.ShapeDtypeStruct((1, CONV_C), F32), jax.ShapeDtypeStruct((1, 128), F32),
                   jax.ShapeDtypeStruct((1, 128), F32), jax.ShapeDtypeStruct((1, 128), F32),
                   jax.ShapeDtypeStruct((1, SSM_W), F32)],
        scratch_shapes=[pltpu.VMEM((SSM_W, SSM_N), F32), pltpu.VMEM((8 + L, CONV_C), F32),
                        pltpu.VMEM((L + 8, CONV_C), F32), pltpu.VMEM((L, SSM_W), F32),
                        pltpu.VMEM((L, SSM_W), F32), pltpu.VMEM((L, SSM_W), F32),
                        pltpu.VMEM((L, CONV_C), F32), pltpu.VMEM((L, SSM_W), F32),
                        pltpu.VMEM((L, 128), F32), pltpu.VMEM((128, L), F32)],
        compiler_params=_params(("arbitrary",)),
    )(dy, proj, proj, proj, proj, proj, conv_w, conv_b, dt_bias, a_log, d_skip, norm_g, *_expand_matrices(),
      proj, proj, proj, hstates)


def _resident(shape):
    nd = len(shape)
    return pl.BlockSpec(shape, lambda *_: (0,) * nd, pipeline_mode=pl.Buffered(1))


def merge_fwd(y_att, y_sg, y_ssm, proj, x, w_a, w_s, w_m, w_o, g_post):
    S = x.shape[0]
    tm = 256

    def body(ya_ref, ys_ref, ym_ref, gate_ref, x_ref, wa_ref, ws_ref, wm_ref, wo_ref, gp_ref,
             xn_ref, bra_ref, brs_ref, brm_ref, mg_ref, out_ref):
        bra = _dot(ya_ref[...], wa_ref[...])
        brs = _dot(ys_ref[...], ws_ref[...])
        brm = _dot(ym_ref[...], wm_ref[...])
        bra_ref[...] = bra.astype(BF16)
        brs_ref[...] = brs.astype(BF16)
        brm_ref[...] = brm.astype(BF16)
        gate = gate_ref[...].astype(F32)
        merged = (_sigmoid(gate[:, 0:1024]) * bra + _sigmoid(gate[:, 1024:2048]) * brs
                  + _sigmoid(gate[:, 2048:3072]) * brm)
        mb = merged.astype(BF16)
        mg_ref[...] = mb
        o = _dot(mb, wo_ref[...])
        out_ref[...] = o
        r = lax.rsqrt(jnp.mean(o * o, axis=-1, keepdims=True) + EPS)
        xn_ref[...] = x_ref[...] + o * r * gp_ref[...]

    row = lambda w: pl.BlockSpec((tm, w), lambda i: (i, 0))
    return pl.pallas_call(
        body, name="merge_fwd", grid=(S // tm,),
        in_specs=[row(1024), row(1024), row(2048), pl.BlockSpec((tm, 3072), lambda i: (i, 0)),
                  row(D), _resident((1024, D)), _resident((1024, D)), _resident((2048, D)), _resident((D, D)),
                  _full((1, D))],
        out_specs=[row(D)] * 6,
        out_shape=[jax.ShapeDtypeStruct((S, D), F32)] + [jax.ShapeDtypeStruct((S, D), BF16)] * 4
        + [jax.ShapeDtypeStruct((S, D), F32)],
        compiler_params=_params(("arbitrary",)),
    )(y_att, y_sg, y_ssm, proj, x, w_a, w_s, w_m, w_o, g_post)


def merge_bwd(dy, out, g_post, proj, br_a, br_s, br_m, w_a, w_s, w_m, w_o):
    S = dy.shape[0]
    tm = 256

    def body(dy_ref, o_ref, gp_ref, gate_ref, bra_ref, brs_ref, brm_ref, wa_ref, ws_ref, wm_ref, wo_ref,
             dout_ref, dba_ref, dbs_ref, dbm_ref, dgate_ref, dya_ref, dys_ref, dym_ref, dgp_ref):
        @pl.when(pl.program_id(0) == 0)
        def _():
            dgp_ref[...] = jnp.zeros_like(dgp_ref)

        o = o_ref[...]
        dyv = dy_ref[...]
        r = lax.rsqrt(jnp.mean(o * o, axis=-1, keepdims=True) + EPS)
        dyg = dyv * gp_ref[...]
        do = r * dyg - o * (r * r * r) * jnp.mean(dyg * o, axis=-1, keepdims=True)
        dgp_ref[...] += jnp.sum(dyv * o * r, axis=0, keepdims=True)
        dob = do.astype(BF16)
        dout_ref[...] = dob
        dmerged = _dot_nt(dob, wo_ref[...])
        for idx, (br_ref, dbr_ref, w_ref, dyi_ref) in enumerate((
                (bra_ref, dba_ref, wa_ref, dya_ref), (brs_ref, dbs_ref, ws_ref, dys_ref),
                (brm_ref, dbm_ref, wm_ref, dym_ref))):
            s = _sigmoid(gate_ref[:, idx * 1024:(idx + 1) * 1024].astype(F32))
            dbr = (dmerged * s).astype(BF16)
            dbr_ref[...] = dbr
            dgate_ref[:, idx * 1024:(idx + 1) * 1024] = (dmerged * br_ref[...].astype(F32) * s * (1.0 - s)).astype(BF16)
            dyi_ref[...] = _dot_nt(dbr, w_ref[...])

    row = lambda w: pl.BlockSpec((tm, w), lambda i: (i, 0))
    return pl.pallas_call(
        body, name="merge_bwd", grid=(S // tm,),
        in_specs=[row(D), row(D), _full((1, D)), pl.BlockSpec((tm, 3072), lambda i: (i, 0)),
                  row(D), row(D), row(D),
                  _resident((1024, D)), _resident((1024, D)), _resident((2048, D)), _resident((D, D))],
        out_specs=[row(D), row(D), row(D), row(D), row(3072), row(1024), row(1024), row(2048), _full((1, D))],
        out_shape=[jax.ShapeDtypeStruct((S, D), BF16)] * 4 + [
            jax.ShapeDtypeStruct((S, 3072), BF16), jax.ShapeDtypeStruct((S, 1024), F32),
            jax.ShapeDtypeStruct((S, 1024), F32), jax.ShapeDtypeStruct((S, 2048), F32),
            jax.ShapeDtypeStruct((1, D), F32)],
        compiler_params=_params(("arbitrary",)),
    )(dy, out, g_post, proj, br_a, br_s, br_m, w_a, w_s, w_m, w_o)


def loss_head(y, target):
    S = y.shape[0]
    tm = 512

    def body(y_ref, t_ref, dy_ref, loss_ref):
        @pl.when(pl.program_id(0) == 0)
        def _():
            loss_ref[...] = jnp.zeros_like(loss_ref)
        e = y_ref[...] - t_ref[...]
        dy_ref[...] = e * (1.0 / D)
        loss_ref[...] += 0.5 * jnp.sum(jnp.mean(e * e, axis=-1, keepdims=True))

    row = pl.BlockSpec((tm, D), lambda i: (i, 0))
    return pl.pallas_call(
        body, name="loss_head", grid=(S // tm,),
        in_specs=[row, row], out_specs=[row, _full((1, 128))],
        out_shape=[jax.ShapeDtypeStruct((S, D), F32), jax.ShapeDtypeStruct((1, 128), F32)],
        compiler_params=_params(("arbitrary",)),
    )(y, target)


def _adam(w, g, m, v):
    mn = ADAM_B1 * m + (1.0 - ADAM_B1) * g
    vn = ADAM_B2 * v + (1.0 - ADAM_B2) * (g * g)
    m_hat = mn / (1.0 - ADAM_B1 ** ADAM_STEP)
    v_hat = vn / (1.0 - ADAM_B2 ** ADAM_STEP)
    return -ADAM_LR * (m_hat / (jnp.sqrt(v_hat) + ADAM_EPS) + ADAM_WD * w), mn, vn


def adamw_big(w, m, v, halves0, sum1, cc, name, tr):
    _, R, C = w.shape
    nper = R // tr
    f, fb, n0, off_a, off_b = halves0
    p, pb, off1 = sum1

    def body(c_ref, w_ref, m_ref, v_ref, f_ref, fb_ref, p_ref, pb_ref, g_ref, d_ref, nm_ref, nv_ref):
        i = pl.program_id(0)
        half = jnp.where(i % nper >= n0, 1, 0)
        g0 = jnp.where(c_ref[0] == half, f_ref[...], fb_ref[...])
        g = jnp.where(i < nper, g0, p_ref[...] + pb_ref[...])
        g_ref[0] = g
        d_ref[0], nm_ref[0], nv_ref[0] = _adam(w_ref[0], g, m_ref[0], v_ref[0])

    def blk0(i, c):
        il = jnp.minimum(i, nper - 1)
        return (jnp.where(il >= n0, off_b + il - n0, off_a + il), 0)

    wblk = pl.BlockSpec((1, tr, C), lambda i, c: (i // nper, i % nper, 0))
    b0 = pl.BlockSpec((tr, C), blk0)
    b1 = pl.BlockSpec((tr, C), lambda i, c: (off1 + jnp.maximum(i - nper, 0), 0))
    grid_spec = pltpu.PrefetchScalarGridSpec(
        num_scalar_prefetch=1, grid=(2 * nper,),
        in_specs=[wblk, wblk, wblk, b0, b0, b1, b1], out_specs=[wblk] * 4)
    return pl.pallas_call(
        body, name=name, grid_spec=grid_spec,
        out_shape=[jax.ShapeDtypeStruct(w.shape, F32)] * 4,
        compiler_params=_params(("arbitrary",)),
    )(cc, w, m, v, f, fb, p, pb)


def adamw_plain(w, g, m, v, name):
    def body(w_ref, g_ref, m_ref, v_ref, d_ref, nm_ref, nv_ref):
        d_ref[...], nm_ref[...], nv_ref[...] = _adam(w_ref[...], g_ref[...], m_ref[...], v_ref[...])

    return pl.pallas_call(
        body, name=name, out_shape=[jax.ShapeDtypeStruct(w.shape, F32)] * 3, compiler_params=_params(),
    )(w, g, m, v)


SMALL = {"norm_pre": ("g_pre", 8), "norm_post": ("g_post", 8), "att_sinks": ("sinks", 8), "sg_ln_g": ("ln_g", 8),
         "sg_ln_b": ("ln_b", 8), "sg_w": ("sg_w", 1024), "sg_b": ("sg_bt", 8), "ssm_conv_b": ("conv_b", 24),
         "ssm_dt_bias": ("dt_bias", 8), "ssm_a_log": ("a_log", 8), "ssm_d": ("d_skip", 8), "ssm_norm_g": ("norm_g", 16)}
SMALL_LAYER_ROWS = sum(r for _, r in SMALL.values())
REL_ROW = DEPTH * SMALL_LAYER_ROWS
LOSS_ROW = REL_ROW + 32
SMALL_ROWS = LOSS_ROW + 8


def _small_rows():
    rows, r = {}, 0
    for l in range(DEPTH):
        for name, (_, n) in SMALL.items():
            rows[(l, name)] = r
            r += n
    return rows


def adamw_small(red, rel, small):
    names = list(SMALL) + ["rel_bias"]
    params = dict(small, rel_bias=rel)
    rows = _small_rows()

    def grad_of(red_ref, l, name, n):
        r0 = rows[(l, name)]
        if name == "sg_b":
            return red_ref[r0:r0 + 8, :]
        if n < 128:
            return red_ref[r0:r0 + 1, 0:n]
        return jnp.concatenate([red_ref[r0 + j:r0 + j + 1, :] for j in range(n // 128)], axis=1)

    def body(red_ref, *refs):
        ins, outs = refs[:3 * len(names)], refs[3 * len(names):]
        for i, name in enumerate(names):
            w_ref, m_ref, v_ref = ins[3 * i:3 * i + 3]
            o = outs[4 * i:4 * i + 4]
            if name == "rel_bias":
                g = red_ref[REL_ROW:REL_ROW + 32, 0:16]
                o[0][...] = g
                o[1][...], o[2][...], o[3][...] = _adam(w_ref[...], g, m_ref[...], v_ref[...])
                continue
            for l in range(DEPTH):
                if name == "sg_w":
                    for grp in range(8):
                        r0 = rows[(l, name)] + grp * 128
                        g = red_ref[r0:r0 + 128, :]
                        o[0][l, grp] = g
                        o[1][l, grp], o[2][l, grp], o[3][l, grp] = _adam(w_ref[l, grp], g, m_ref[l, grp], v_ref[l, grp])
                elif name == "sg_b":
                    g = grad_of(red_ref, l, name, 128)
                    o[0][l] = g
                    o[1][l], o[2][l], o[3][l] = _adam(w_ref[l], g, m_ref[l], v_ref[l])
                else:
                    sl = slice(l, l + 1)
                    g = grad_of(red_ref, l, name, w_ref.shape[-1])
                    o[0][sl, :] = g
                    o[1][sl, :], o[2][sl, :], o[3][sl, :] = _adam(w_ref[sl, :], g, m_ref[sl, :], v_ref[sl, :])

    flat_in = [a for name in names for a in params[name]]
    out_shape = [jax.ShapeDtypeStruct(params[name][0].shape, F32) for name in names for _ in range(4)]
    res = pl.pallas_call(body, name="adamw_small", out_shape=out_shape, compiler_params=_params())(red, *flat_in)
    return {name: tuple(res[4 * i:4 * i + 4]) for i, name in enumerate(names)}


ANY = pl.BlockSpec(memory_space=pl.ANY)


def _place():
    x, y, c = lax.axis_index("x"), lax.axis_index("y"), lax.axis_index("c")
    others = [(1 - x, y), (x, 1 - y), (1 - x, 1 - y)]
    return x, y, c, others


def _rcopy(src, dst, ssem, rsem, to):
    return pltpu.make_async_remote_copy(src_ref=src, dst_ref=dst, send_sem=ssem, recv_sem=rsem,
                                        device_id=to, device_id_type=MESH)


def gather_weights(arrs):
    n = len(arrs)

    def body(*refs):
        srcs, outs, ssem, rsem = refs[:n], refs[n:2 * n], refs[2 * n], refs[2 * n + 1]
        x, y, c, others = _place()
        me = 2 * x + y
        sib = (x, y, 1 - c)
        first = [_rcopy(srcs[i].at[c], outs[i].at[c, me], ssem.at[6 * i + k], rsem.at[6 * i + k], (ox, oy, c))
                 for i in range(n) for k, (ox, oy) in enumerate(others)]
        for cp in first:
            cp.start()
        passed = []
        for k, (ox, oy) in enumerate(others):
            for i in range(n):
                slot = outs[i].at[c, 2 * ox + oy]
                _rcopy(slot, slot, ssem.at[6 * i + k], rsem.at[6 * i + k], sib).wait_recv()
                fw = _rcopy(slot, slot, ssem.at[6 * i + 3 + k], rsem.at[6 * i + 3 + k], sib)
                fw.start()
                passed.append(fw)
        for k, (ox, oy) in enumerate(others):
            for i in range(n):
                slot = outs[i].at[1 - c, 2 * ox + oy]
                _rcopy(slot, slot, ssem.at[6 * i + 3 + k], rsem.at[6 * i + 3 + k], sib).wait_recv()
        for cp in first + passed:
            cp.wait_send()

    return pl.pallas_call(
        body, name="gather_weights",
        in_specs=[ANY] * n, out_specs=[ANY] * n,
        out_shape=[jax.ShapeDtypeStruct((2, SHARDS) + a.shape[1:], a.dtype) for a in arrs],
        scratch_shapes=[pltpu.SemaphoreType.DMA((6 * n,)), pltpu.SemaphoreType.DMA((6 * n,))],
    )(*arrs)


HBM = pl.BlockSpec(memory_space=pltpu.HBM)
SEM = pl.BlockSpec(memory_space=pltpu.SEMAPHORE)
EFFECT = pltpu.SideEffectType.DATAFLOW_SIDE_EFFECTING


def _in_hbm(a):
    return pltpu.with_memory_space_constraint(a, pltpu.HBM)


def gather_start(srcs, after, name, by_dest=False):
    n = len(srcs)
    lands = [_in_hbm(lax.empty((SHARDS,) + a.shape[-2:], a.dtype)) for a in srcs]
    na = len(after)

    def body(*refs):
        src, land = refs[:n], refs[n:2 * n]
        ssem, rsem, token = refs[2 * n + na], refs[2 * n + na + 1], refs[-1]
        x, y, c, others = _place()
        me = 2 * x + y
        for i in range(n):
            for k, (ox, oy) in enumerate(others):
                s = src[i].at[2 * ox + oy] if by_dest else src[i]
                _rcopy(s, land[i].at[me], ssem.at[3 * i + k], rsem.at[3 * i + k], (ox, oy, c)).start()
        token[...] = jnp.zeros_like(token)

    bufs = [_in_hbm(a) for a in srcs] + lands
    out = pl.pallas_call(
        body, name=name,
        out_shape=(pltpu.SemaphoreType.DMA((3 * n,)), pltpu.SemaphoreType.DMA((3 * n,)),
                   *[pltpu.HBM(b.shape, b.dtype) for b in bufs], jax.ShapeDtypeStruct((8, 128), F32)),
        in_specs=[HBM] * (2 * n) + [ANY] * na,
        out_specs=(SEM, SEM, *[HBM] * (2 * n), pl.BlockSpec(memory_space=pltpu.VMEM)),
        input_output_aliases={i: 2 + i for i in range(2 * n)},
        compiler_params=pltpu.CompilerParams(has_side_effects=EFFECT),
    )(*bufs, *after)
    return out[0], out[1], list(out[2:2 + n]), list(out[2 + n:2 + 2 * n]), out[-1]


def gather_wait(ssem, rsem, srcs, lands, after, name, by_dest=False):
    n = len(srcs)

    def body(*refs):
        src, land = refs[:n], refs[n:2 * n]
        s_sem, r_sem = refs[2 * n], refs[2 * n + 1]
        x, y, c, others = _place()
        for i in range(n):
            for k, (ox, oy) in enumerate(others):
                s = src[i].at[2 * ox + oy] if by_dest else src[i]
                cp = _rcopy(s, land[i].at[2 * ox + oy], s_sem.at[3 * i + k], r_sem.at[3 * i + k], (ox, oy, c))
                cp.wait_send()
                cp.wait_recv()

    bufs = list(srcs) + list(lands)
    out = pl.pallas_call(
        body, name=name,
        out_shape=tuple(pltpu.HBM(b.shape, b.dtype) for b in bufs),
        in_specs=[HBM] * (2 * n) + [SEM, SEM, ANY],
        out_specs=tuple([HBM] * (2 * n)),
        input_output_aliases={i: i for i in range(2 * n)},
        compiler_params=pltpu.CompilerParams(has_side_effects=EFFECT),
    )(*bufs, ssem, rsem, after)
    return list(out[n:2 * n])


def grad_sibling_exchange(arrs):
    n = len(arrs)

    def body(*refs):
        srcs, outs, ssem, rsem = refs[:n], refs[n:2 * n], refs[2 * n], refs[2 * n + 1]
        x, y, c, _ = _place()
        cps = [_rcopy(srcs[i].at[1 - c], outs[i], ssem.at[i], rsem.at[i], (x, y, 1 - c)) for i in range(n)]
        for cp in cps:
            cp.start()
        for cp in cps:
            cp.wait()

    return pl.pallas_call(
        body, name="grad_sibling_exchange",
        in_specs=[ANY] * n, out_specs=[ANY] * n,
        out_shape=[jax.ShapeDtypeStruct(a.shape[1:], F32) for a in arrs],
        scratch_shapes=[pltpu.SemaphoreType.DMA((n,)), pltpu.SemaphoreType.DMA((n,))],
    )(*arrs)


def grad_chip_sum(g, sb, cc, tr, name):
    _, _, R, C = g.shape
    blk = pl.BlockSpec((1, tr, C), lambda s, r, c: (s, r, 0))
    grid_spec = pltpu.PrefetchScalarGridSpec(
        num_scalar_prefetch=1, grid=(SHARDS, R // tr),
        in_specs=[pl.BlockSpec((1, 1, tr, C), lambda s, r, c: (c[0], s, r, 0)), blk],
        out_specs=[blk, blk])

    def body(c_ref, a_ref, b_ref, o_ref, ob_ref):
        t = a_ref[0] + b_ref[...]
        o_ref[...] = t
        ob_ref[...] = t.astype(BF16)

    return pl.pallas_call(
        body, name=name, grid_spec=grid_spec,
        out_shape=[jax.ShapeDtypeStruct((SHARDS, R, C), F32), jax.ShapeDtypeStruct((SHARDS, R, C), BF16)],
        compiler_params=_params(("arbitrary", "arbitrary")),
    )(cc, g, sb)


def grad_shard_sum(t, rb, me, tr, name):
    _, R, C = t.shape
    grid_spec = pltpu.PrefetchScalarGridSpec(
        num_scalar_prefetch=1, grid=(R // tr,),
        in_specs=[pl.BlockSpec((1, tr, C), lambda r, m: (m[0], r, 0)),
                  pl.BlockSpec((SHARDS, tr, C), lambda r, m: (0, r, 0))],
        out_specs=pl.BlockSpec((tr, C), lambda r, m: (r, 0)))

    def body(m_ref, t_ref, r_ref, o_ref):
        part = [jnp.where(m_ref[0] == s, t_ref[0], r_ref[s].astype(F32)) for s in range(SHARDS)]
        o_ref[...] = ((part[0] + part[1]) + part[2]) + part[3]

    return pl.pallas_call(
        body, name=name, grid_spec=grid_spec,
        out_shape=jax.ShapeDtypeStruct((R, C), F32),
        compiler_params=_params(("arbitrary",)),
    )(me, t, rb)


def grad_sibling_share(arrs, name):
    n = len(arrs)

    def body(*refs):
        srcs, outs, ssem, rsem = refs[:n], refs[n:2 * n], refs[2 * n], refs[2 * n + 1]
        x, y, c, _ = _place()
        cps = [_rcopy(srcs[i], outs[i], ssem.at[i], rsem.at[i], (x, y, 1 - c)) for i in range(n)]
        for cp in cps:
            cp.start()
        for cp in cps:
            cp.wait()

    return pl.pallas_call(
        body, name=name,
        in_specs=[ANY] * n, out_specs=[ANY] * n,
        out_shape=[jax.ShapeDtypeStruct(a.shape, F32) for a in arrs],
        scratch_shapes=[pltpu.SemaphoreType.DMA((n,)), pltpu.SemaphoreType.DMA((n,))],
    )(*arrs)


def _allreduce_rows(src, sib_buf, chips, out_ref, ssem, rsem):
    x, y, c, others = _place()
    me = 2 * x + y
    cp = _rcopy(src, sib_buf, ssem.at[0], rsem.at[0], (x, y, 1 - c))
    cp.start()
    cp.wait()
    chips[me] = src[...] + sib_buf[...]
    sends = [_rcopy(chips.at[me], chips.at[me], ssem.at[1 + k], rsem.at[1 + k], (ox, oy, c))
             for k, (ox, oy) in enumerate(others)]
    for s in sends:
        s.start()
    for k, (ox, oy) in enumerate(others):
        slot = chips.at[2 * ox + oy]
        _rcopy(slot, slot, ssem.at[1 + k], rsem.at[1 + k], (ox, oy, c)).wait_recv()
    for s in sends:
        s.wait_send()
    out_ref[...] = ((chips[0] + chips[1]) + chips[2]) + chips[3]


def _allreduce_scratch(rows):
    return [pltpu.VMEM((rows, 128), F32), pltpu.VMEM((SHARDS, rows, 128), F32),
            pltpu.SemaphoreType.DMA((4,)), pltpu.SemaphoreType.DMA((4,))]


def allreduce_rows(buf, name):
    rows = buf.shape[0]
    VM = pl.BlockSpec(memory_space=pltpu.VMEM)

    def body(src_ref, out_ref, sib_buf, chips, ssem, rsem):
        _allreduce_rows(src_ref, sib_buf, chips, out_ref, ssem, rsem)

    return pl.pallas_call(
        body, name=name, in_specs=[VM], out_specs=VM,
        out_shape=jax.ShapeDtypeStruct((rows, 128), F32),
        scratch_shapes=_allreduce_scratch(rows), compiler_params=_params(),
    )(buf)


def small_allreduce(grads, rel, loss_part):
    rows = _small_rows()
    keys = [(l, name) for l in range(DEPTH) for name in SMALL]
    flat = [grads[l][SMALL[name][0]] for l, name in keys] + [rel, loss_part]

    def body(*refs):
        ins = refs[:len(flat)]
        out_ref, src, sib_buf, chips, ssem, rsem = refs[len(flat):]
        src[...] = jnp.zeros_like(src)
        for (l, name), ref in zip(keys, ins):
            r0 = rows[(l, name)]
            if name == "sg_w":
                for grp in range(8):
                    src[r0 + grp * 128:r0 + (grp + 1) * 128, :] = ref[grp]
            elif name == "sg_b":
                src[r0:r0 + 8, :] = ref[...].T[0:8, :]
            else:
                for j in range(ref.shape[1] // 128):
                    src[r0 + j:r0 + j + 1, :] = ref[:, j * 128:(j + 1) * 128]
        src[REL_ROW:REL_ROW + 32, 0:16] = ins[-2][...]
        src[LOSS_ROW:LOSS_ROW + 1, :] = ins[-1][...]
        _allreduce_rows(src, sib_buf, chips, out_ref, ssem, rsem)

    return pl.pallas_call(
        body, name="small_allreduce",
        out_shape=jax.ShapeDtypeStruct((SMALL_ROWS, 128), F32),
        scratch_shapes=[pltpu.VMEM((SMALL_ROWS, 128), F32)] + _allreduce_scratch(SMALL_ROWS),
        compiler_params=_params(),
    )(*flat)


def _pad_lanes(v):
    return jnp.zeros((1, 128), F32).at[0, :v.shape[0]].set(v)


def layer_fwd(x, wts, bias):
    wt = wts["wt"]
    tn = {name: t for name, _, t in GROUPS}
    p_gate, h = inproj_first(x, wts["g_pre"], wt["gate"], tn["gate"], "inproj_gate")
    p_sgu, p_att, p_ssd = (inproj_group(h, wt[n], tn[n], "inproj_" + n, F32 if n == "att" else BF16)
                           for n in ("sgu", "att", "ssd"))
    y_att = att_fwd(p_att, bias, wts["sinks"])
    y_sg = sgu_fwd(p_sgu, wts["ln_g"], wts["ln_b"], wts["sg_w"], wts["sg_bt"])
    y_ssm, hst = ssd_fwd(p_ssd, wts["conv_w"], wts["conv_b"], wts["dt_bias"], wts["a_log"], wts["d_skip"],
                         wts["norm_g"])
    x_new, br_a, br_s, br_m, merged, out = merge_fwd(
        y_att, y_sg, y_ssm, p_gate, x, wts["w_a"], wts["w_s"], wts["w_m"], wts["w_o"], wts["g_post"])
    saved = dict(x=x, p_gate=p_gate, p_sgu=p_sgu, p_att=p_att, p_ssd=p_ssd, h=h,
                 y_att=y_att, y_sg=y_sg, y_ssm=y_ssm, hst=hst,
                 br_a=br_a, br_s=br_s, br_m=br_m, merged=merged, out=out)
    return x_new, saved


def layer_bwd(dy, wts, bias, sv):
    dps, grads = layer_bwd_params(dy, wts, bias, sv)
    dx, grads["g_pre"] = layer_bwd_input(dy, dps, wts, sv, wts["g_pre"])
    return dx, grads


def layer_bwd_input(dy, dps, wts, sv, g_pre):
    wt = wts["wt"]
    tn = {name: t for name, _, t in GROUPS}
    acc = None
    for n in ("gate", "sgu", "ssd"):
        acc = dh_group(dps[n], wt[n], acc, DH_TILE[n], "dh_" + n)
    return dh_last(dps["att"], wt["att"], acc, sv["x"], g_pre, dy, tn["att"], "dh_att")


def layer_bwd_params(dy, wts, bias, sv):
    dout, dba, dbs, dbm, d_gate, dya, dys, dym, dg_post = merge_bwd(
        dy, sv["out"], wts["g_post"], sv["p_gate"], sv["br_a"], sv["br_s"], sv["br_m"],
        wts["w_a"], wts["w_s"], wts["w_m"], wts["w_o"])
    d_att, dbias, dsinks = att_bwd(dya, sv["p_att"], bias, wts["sinks"])
    d_sgu, dsg_w, dsg_bt, dln_g, dln_b = sgu_bwd(dys, sv["p_sgu"], wts["ln_g"], wts["ln_b"], wts["sg_w"],
                                                 wts["sg_bt"])
    d_ssd, dcw, dcb, ddtb, dalog, ddsk, dng = ssd_bwd(
        dym, sv["p_ssd"], sv["hst"], wts["conv_w"], wts["conv_b"], wts["dt_bias"], wts["a_log"], wts["d_skip"],
        wts["norm_g"])
    dps = dict(gate=d_gate, sgu=d_sgu, att=d_att, ssd=d_ssd)
    tn = {name: t for name, _, t in GROUPS}
    grads = dict(
        w_in={n: dw_group(dps[n], sv["h"], tn[n], "dw_in_" + n) for n in dps},
        w_a=matmul_tn(sv["y_att"], dba, "dw_att"),
        w_s=matmul_tn(sv["y_sg"], dbs, "dw_sg"),
        w_m=matmul_tn(sv["y_ssm"], dbm, "dw_ssm"),
        w_o=matmul_tn(sv["merged"], dout, "dw_out"),
        g_post=dg_post, sinks=dsinks, ln_g=dln_g, ln_b=dln_b, sg_w=dsg_w, sg_bt=dsg_bt,
        conv_w=dcw, conv_b=dcb, dt_bias=ddtb, a_log=dalog, d_skip=ddsk, norm_g=dng, bias=dbias)
    return dps, grads


REST_OFF = (0, 256, 512, 1024, 1280)
GR_ROWS = 1536
GR_CONV = 1280
W_IN_SPLIT = 1600
W_IN_HALF = 1824


def kernel(x, w_in, norm_pre, norm_post, rel_bias, att_sinks, sg_ln_g, sg_ln_b, sg_w, sg_b, ssm_conv_w, ssm_conv_b, ssm_dt_bias, ssm_a_log, ssm_d, ssm_norm_g, w_br_att, w_br_sg, w_br_ssm, w_out, loss_target, m_w_in, m_norm_pre, m_norm_post, m_rel_bias, m_att_sinks, m_sg_ln_g, m_sg_ln_b, m_sg_w, m_sg_b, m_ssm_conv_w, m_ssm_conv_b, m_ssm_dt_bias, m_ssm_a_log, m_ssm_d, m_ssm_norm_g, m_w_br_att, m_w_br_sg, m_w_br_ssm, m_w_out, v_w_in, v_norm_pre, v_norm_post, v_rel_bias, v_att_sinks, v_sg_ln_g, v_sg_ln_b, v_sg_w, v_sg_b, v_ssm_conv_w, v_ssm_conv_b, v_ssm_dt_bias, v_ssm_a_log, v_ssm_d, v_ssm_norm_g, v_w_br_att, v_w_br_sg, v_w_br_ssm, v_w_out):
    cx, cy, cc = lax.axis_index("x"), lax.axis_index("y"), lax.axis_index("c")
    me = 2 * cx + cy
    xs = x[0]
    S = xs.shape[0]

    tr = lambda a: jnp.transpose(a, (0, 2, 1))
    w_in_b = tr(w_in).astype(BF16)
    w_rest_b = jnp.concatenate([w_br_att, w_br_sg, w_br_ssm, w_out], axis=1).astype(BF16)
    halves = lambda a: a.reshape(2, a.shape[0] // 2, a.shape[1])
    w_in0 = jnp.pad(w_in_b[0], ((0, W_IN_ROWS - 3400), (0, 0)))
    all0_in, all0_rest = gather_weights([halves(w_in0), halves(w_rest_b[0])])
    convw_slot = jnp.zeros((SHARDS, DEPTH * CONV_K * 768 // 128, 128), F32)
    convw_slot = lax.dynamic_update_index_in_dim(
        convw_slot, jnp.where(cc == 0, 1.0, 0.0) * ssm_conv_w.reshape(-1, 128), me, 0)
    convw_rows = allreduce_rows(convw_slot.reshape(-1, 128), "gather_conv_w")
    convw_all = convw_rows.reshape(SHARDS, DEPTH, CONV_K, 768).transpose(1, 2, 0, 3).reshape(DEPTH, CONV_K, CONV_C)
    g1_ssem, g1_rsem, g1_srcs, g1_lands, g1_token = gather_start(
        [w_in_b[1], w_rest_b[1]], [convw_rows, all0_rest], "gather_l1_start")

    o = REST_OFF

    def layer_weights(l, gathered_in, gathered_rest, g_pre):
        sh_in = [jnp.where(me == s, w_in_b[l], gathered_in[s]) for s in range(SHARDS)]
        sh_rest = [jnp.where(me == s, w_rest_b[l], gathered_rest[s]) for s in range(SHARDS)]
        rest = lambda k: jnp.concatenate([r[o[k]:o[k + 1]] for r in sh_rest], axis=0)
        return dict(
            wt=group_weights(jnp.concatenate(sh_in, axis=0)),
            w_a=rest(0), w_s=rest(1), w_m=rest(2), w_o=rest(3),
            g_pre=g_pre, g_post=norm_post[l][None], sinks=att_sinks[l],
            ln_g=sg_ln_g[l][None], ln_b=sg_ln_b[l][None], sg_w=sg_w[l],
            sg_bt=sg_b[l].T,
            conv_w=jnp.concatenate([convw_all[l], jnp.zeros((4, CONV_C), F32)], axis=0),
            conv_b=ssm_conv_b[l][None], dt_bias=_pad_lanes(ssm_dt_bias[l]), a_log=_pad_lanes(ssm_a_log[l]),
            d_skip=_pad_lanes(ssm_d[l]), norm_g=ssm_norm_g[l][None])

    bias = bias_table(rel_bias)
    layers = [layer_weights(0, [all0_in[:, s].reshape(W_IN_ROWS, D)[0:3400] for s in range(SHARDS)],
                            [all0_rest[:, s].reshape(1280, D) for s in range(SHARDS)],
                            (norm_pre[0] + g1_token[0, 0])[None])]
    act, sv0 = layer_fwd(xs, layers[0], bias)
    land_in, land_rest = gather_wait(g1_ssem, g1_rsem, g1_srcs, g1_lands, act, "gather_l1_wait")
    layers.append(layer_weights(1, land_in, land_rest, norm_pre[1][None]))
    act, sv1 = layer_fwd(act, layers[1], bias)
    saved = [sv0, sv1]
    dy, loss_part = loss_head(act, loss_target[0])
    cvec = jnp.reshape(cc, (1,)).astype(jnp.int32)
    mvec = jnp.reshape(me, (1,)).astype(jnp.int32)

    def by_shard(g):
        gcw = g["conv_w"][0:CONV_K].reshape(CONV_K, SHARDS, 768).transpose(1, 0, 2).reshape(SHARDS, 3, 1024)
        rest = jnp.concatenate([
            g["w_a"].reshape(SHARDS, 256, D), g["w_s"].reshape(SHARDS, 256, D), g["w_o"].reshape(SHARDS, 256, D),
            g["w_m"].reshape(SHARDS, 512, D), jnp.pad(gcw, ((0, 0), (0, GR_ROWS - GR_CONV - 3), (0, 0)))], axis=1)
        return ungroup_grads(g["w_in"]).reshape(SHARDS, 3400, D), rest

    grads = [None] * DEPTH
    dy, grads[1] = layer_bwd(dy, layers[1], bias, saved[1])
    g1_in, g1_rest = by_shard(grads[1])
    g1_in = jnp.pad(g1_in, ((0, 0), (0, W_IN_ROWS - 3400), (0, 0)))
    x1_ssem, x1_rsem, x1_srcs, x1_lands, x1_token = gather_start(
        [g1_in.astype(BF16), g1_rest.astype(BF16)], [], "grads_l1_start", by_dest=True)
    wts0 = dict(layers[0], g_post=layers[0]["g_post"] + x1_token[0, 0])
    dps0, grads[0] = layer_bwd_params(dy, wts0, bias, saved[0])
    r1_in, r1_rest = gather_wait(x1_ssem, x1_rsem, x1_srcs, x1_lands, grads[0]["w_in"]["ssd"], "grads_l1_wait",
                                 by_dest=True)
    p_in = grad_shard_sum(g1_in, r1_in, mvec, 384, "l1_sum_w_in")
    p_rest = grad_shard_sum(g1_rest, r1_rest, mvec, 512, "l1_sum_rest")
    pb_in, pb_rest = grad_sibling_share([p_in, p_rest], "l1_sibling_share")

    g0_in, g0_rest = by_shard(grads[0])
    pad_to = lambda a, rows: jnp.pad(a, ((0, 0), (0, rows - a.shape[1]), (0, 0)))
    g0_in = jnp.stack([pad_to(g0_in[:, 0:W_IN_SPLIT], W_IN_HALF), pad_to(g0_in[:, W_IN_SPLIT:3400], W_IN_HALF)])
    g0_rest = jnp.stack([g0_rest[:, 0:GR_ROWS // 2], g0_rest[:, GR_ROWS // 2:GR_ROWS]])
    sb_in, sb_rest = grad_sibling_exchange([g0_in, g0_rest])
    t_in, t_in_b = grad_chip_sum(g0_in, sb_in, cvec, 608, "chip_sum_w_in")
    t_rest, t_rest_b = grad_chip_sum(g0_rest, sb_rest, cvec, 384, "chip_sum_rest")
    x0_ssem, x0_rsem, x0_srcs, x0_lands, x0_token = gather_start([t_in_b, t_rest_b], [], "grads_l0_start", by_dest=True)
    dy, grads[0]["g_pre"] = layer_bwd_input(dy, dps0, layers[0], saved[0], layers[0]["g_pre"] + x0_token[0, 0])
    grad_x = dy[None]
    rb_in, rb_rest = gather_wait(x0_ssem, x0_rsem, x0_srcs, x0_lands, dy, "grads_l0_wait", by_dest=True)
    grad_rel_local = bias_grad(grads[0]["bias"] + grads[1]["bias"])
    f_in = grad_shard_sum(t_in, rb_in, mvec, 608, "shard_sum_w_in")
    f_rest = grad_shard_sum(t_rest, rb_rest, mvec, 384, "shard_sum_rest")
    fb_in, fb_rest = grad_sibling_share([f_in, f_rest], "l0_sibling_share")

    red = small_allreduce(grads, grad_rel_local, loss_part + 0.0 * f_rest[0:1, 0:128])
    loss = red[LOSS_ROW, 0]

    res = adamw_small(red, (rel_bias, m_rel_bias, v_rel_bias), dict(
        norm_pre=(norm_pre, m_norm_pre, v_norm_pre), norm_post=(norm_post, m_norm_post, v_norm_post),
        att_sinks=(att_sinks, m_att_sinks, v_att_sinks), sg_ln_g=(sg_ln_g, m_sg_ln_g, v_sg_ln_g),
        sg_ln_b=(sg_ln_b, m_sg_ln_b, v_sg_ln_b), sg_w=(sg_w, m_sg_w, v_sg_w), sg_b=(sg_b, m_sg_b, v_sg_b),
        ssm_conv_b=(ssm_conv_b, m_ssm_conv_b, v_ssm_conv_b), ssm_dt_bias=(ssm_dt_bias, m_ssm_dt_bias, v_ssm_dt_bias),
        ssm_a_log=(ssm_a_log, m_ssm_a_log, v_ssm_a_log), ssm_d=(ssm_d, m_ssm_d, v_ssm_d),
        ssm_norm_g=(ssm_norm_g, m_ssm_norm_g, v_ssm_norm_g)))
    res["w_in"] = tuple(tr(a) for a in adamw_big(
        tr(w_in), tr(m_w_in), tr(v_w_in), (f_in, fb_in, W_IN_SPLIT // 200, 0, 0), (p_in, pb_in, 0), cvec, "adamw_w_in", 200))
    rest_upd = lambda w, m, v, name, n0, off0, off1: adamw_big(
        w, m, v, (f_rest, fb_rest, n0, off0, off0), (p_rest, pb_rest, off1), cvec, name, 256)
    res["w_br_att"] = rest_upd(w_br_att, m_w_br_att, v_w_br_att, "adamw_w_br_att", 1, 0, 0)
    res["w_br_sg"] = rest_upd(w_br_sg, m_w_br_sg, v_w_br_sg, "adamw_w_br_sg", 1, 1, 1)
    res["w_out"] = rest_upd(w_out, m_w_out, v_w_out, "adamw_w_out", 1, 2, 2)
    res["w_br_ssm"] = rest_upd(w_br_ssm, m_w_br_ssm, v_w_br_ssm, "adamw_w_br_ssm", 0, 0, 3)
    cw0 = jnp.where(cc == 1, f_rest, fb_rest)[GR_CONV - GR_ROWS // 2:GR_CONV - GR_ROWS // 2 + 3]
    cw1 = (p_rest + pb_rest)[GR_CONV:GR_CONV + 3]
    g_conv_w = jnp.stack([cw0.reshape(CONV_K, 768), cw1.reshape(CONV_K, 768)])
    res["ssm_conv_w"] = (g_conv_w,) + tuple(adamw_plain(ssm_conv_w, g_conv_w, m_ssm_conv_w, v_ssm_conv_w, "adamw_conv_w"))

    order = ["w_in", "norm_pre", "norm_post", "rel_bias", "att_sinks", "sg_ln_g", "sg_ln_b", "sg_w", "sg_b",
             "ssm_conv_w", "ssm_conv_b", "ssm_dt_bias", "ssm_a_log", "ssm_d", "ssm_norm_g",
             "w_br_att", "w_br_sg", "w_br_ssm", "w_out"]
    return (loss, grad_x, *[res[n][0] for n in order], *[res[n][1] for n in order],
            *[res[n][2] for n in order], *[res[n][3] for n in order])
```

```python
import functools
import math

import numpy as np
import jax
import jax.numpy as jnp
from jax import lax
from jax.experimental import pallas as pl
from jax.experimental.pallas import tpu as pltpu

F32 = jnp.float32
BF16 = jnp.bfloat16
MESH = pl.DeviceIdType.MESH

D = 1024
DEPTH = 2
EPS = 1e-6
L = 128
HEADS = 16
KV = 2
DH = 64
SSM_W = 2048
SSM_H = 32
SSM_P = 64
SSM_G = 4
SSM_N = 128
CONV_K = 4
CONV_C = 3072
NEG = -1e30
IN_COLS = 13600

GROUPS = (("gate", 3072, 3072), ("sgu", 3072, 3072), ("att", 2304, 2304), ("ssd", 5376, 1792))
W_IN_ROWS = 3456
DH_TILE = {"gate": 3072, "sgu": 3072, "ssd": 2688}

ADAM_LR = 0.001
ADAM_B1 = 0.9
ADAM_B2 = 0.999
ADAM_EPS = 1e-08
ADAM_WD = 0.01
ADAM_STEP = 10

VMEM_LIMIT = 56 * 1024 * 1024

SHARDS = 4


def _dot(a, b):
    return jnp.dot(a, b, preferred_element_type=F32)


def _dot_nt(a, b):
    return lax.dot_general(a, b, (((1,), (1,)), ((), ())), preferred_element_type=F32)


def _dot_tn(a_f32, b):
    return jnp.dot(a_f32.T.astype(BF16), b, preferred_element_type=F32)


def _dot_t(a, b):
    return lax.dot_general(a, b, (((0,), (0,)), ((), ())), preferred_element_type=F32)


def _dot_hi(a, b):
    return jnp.dot(a, b, preferred_element_type=F32, precision=lax.Precision.HIGHEST)


def _pieces(x, n):
    out = []
    for _ in range(n - 1):
        p = x.astype(BF16)
        out.append(p)
        x = x - p.astype(F32)
    out.append(x.astype(BF16))
    return out


def _dot_sel(a, sel, n):
    sel = sel.astype(BF16)
    acc = None
    for p in _pieces(a, n):
        t = _dot(p, sel)
        acc = t if acc is None else acc + t
    return acc


def _sel_dot(sel, b, n):
    sel = sel.astype(BF16)
    acc = None
    for p in _pieces(b, n):
        t = _dot(sel, p)
        acc = t if acc is None else acc + t
    return acc


def _sigmoid(x):
    return 1.0 / (1.0 + jnp.exp(-x))


def _softplus(x):
    return jnp.maximum(x, 0.0) + jnp.log(1.0 + jnp.exp(-jnp.abs(x)))


def _params(sem=None, vmem=VMEM_LIMIT):
    kw = dict(vmem_limit_bytes=vmem)
    if sem is not None:
        kw["dimension_semantics"] = sem
    return pltpu.CompilerParams(**kw)


def _full(shape):
    nd = len(shape)
    return pl.BlockSpec(shape, lambda *_: (0,) * nd)


def group_weights(wt):
    return dict(
        gate=wt[10528:13600],
        sgu=wt[2304:5376],
        att=jnp.concatenate([wt[0:1024], wt[1280:2304], wt[1024:1280]], axis=0),
        ssd=jnp.concatenate([wt[5376:10496], wt[10496:10528], jnp.zeros((224, D), wt.dtype)], axis=0))


def ungroup_grads(g):
    a, s = g["att"], g["ssd"]
    return jnp.concatenate([a[0:1024], a[2048:2304], a[1024:2048], g["sgu"], s[0:5152], g["gate"]], axis=0)


def _bucket_table():
    qi = np.arange(L)[:, None]
    kj = np.arange(2 * L)[None, :]
    dist = np.maximum(qi + L - kj, 0)
    dist_f = np.maximum(dist, 1).astype(np.float32)
    large = 16 + (np.log(dist_f / np.float32(16)) / np.float32(math.log(128 / 16)) * np.float32(16)).astype(np.int32)
    large = np.minimum(large, 31)
    return np.where(dist < 16, dist, large).astype(np.int32)


def bias_table(rel_bias):
    buckets = jnp.asarray(_bucket_table().reshape(1, L * 2 * L))

    def body(rb_ref, bk_ref, out_ref):
        onehot = (lax.broadcasted_iota(jnp.int32, (32, L * 2 * L), 0) == bk_ref[...]).astype(F32)
        out_ref[...] = lax.dot_general(rb_ref[...], onehot, (((0,), (0,)), ((), ())),
                                       preferred_element_type=F32, precision=lax.Precision.HIGHEST)

    out = pl.pallas_call(
        body, name="bias_table",
        out_shape=jax.ShapeDtypeStruct((HEADS, L * 2 * L), F32),
        compiler_params=_params(),
    )(rel_bias, buckets)
    out = out.reshape(HEADS, L, 2 * L)
    win = _window_mask()
    first = win & (np.arange(2 * L)[None, :] >= L)
    return jnp.stack([jnp.where(first, out, NEG), jnp.where(win, out, NEG)])


def _window_mask():
    dist = np.arange(L)[:, None] + L - np.arange(2 * L)[None, :]
    return (dist >= 0) & (dist < L)


def bias_grad(dbias):
    buckets = jnp.asarray(_bucket_table().reshape(1, L * 2 * L))

    def body(db_ref, bk_ref, out_ref):
        onehot = (lax.broadcasted_iota(jnp.int32, (32, L * 2 * L), 0) == bk_ref[...]).astype(F32)
        out_ref[...] = lax.dot_general(onehot, db_ref[...], (((1,), (1,)), ((), ())),
                                       preferred_element_type=F32, precision=lax.Precision.HIGHEST)

    return pl.pallas_call(
        body, name="bias_grad",
        out_shape=jax.ShapeDtypeStruct((32, HEADS), F32),
        compiler_params=_params(),
    )(dbias.reshape(HEADS, L * 2 * L), buckets)


def _row_tile(S):
    return 1024 if S % 1024 == 0 else 512


def inproj_first(x, g_pre, wt, tn, name):
    S, W = x.shape[0], wt.shape[0]
    tm = _row_tile(S)

    def body(x_ref, g_ref, w_ref, o_ref, h_ref):
        @pl.when(pl.program_id(1) == 0)
        def _():
            xv = x_ref[...]
            r = lax.rsqrt(jnp.mean(xv * xv, axis=-1, keepdims=True) + EPS)
            h_ref[...] = (xv * r * g_ref[...]).astype(BF16)
        o_ref[...] = _dot_nt(h_ref[...], w_ref[...]).astype(BF16)

    return pl.pallas_call(
        body, name=name, grid=(S // tm, W // tn),
        in_specs=[pl.BlockSpec((tm, D), lambda i, j: (i, 0)), _full((1, D)),
                  pl.BlockSpec((tn, D), lambda i, j: (j, 0))],
        out_specs=[pl.BlockSpec((tm, tn), lambda i, j: (i, j)), pl.BlockSpec((tm, D), lambda i, j: (i, 0))],
        out_shape=[jax.ShapeDtypeStruct((S, W), BF16), jax.ShapeDtypeStruct((S, D), BF16)],
        compiler_params=_params(("arbitrary", "arbitrary")),
    )(x, g_pre, wt)


def inproj_group(h, wt, tn, name, dtype):
    S, W = h.shape[0], wt.shape[0]
    tm = _row_tile(S)

    def body(h_ref, w_ref, o_ref):
        o_ref[...] = _dot_nt(h_ref[...], w_ref[...]).astype(dtype)

    return pl.pallas_call(
        body, name=name, grid=(S // tm, W // tn),
        in_specs=[pl.BlockSpec((tm, D), lambda i, j: (i, 0)), pl.BlockSpec((tn, D), lambda i, j: (j, 0))],
        out_specs=pl.BlockSpec((tm, tn), lambda i, j: (i, j)),
        out_shape=jax.ShapeDtypeStruct((S, W), dtype),
        compiler_params=_params(("arbitrary", "arbitrary")),
    )(h, wt)


def dh_group(dp, wt, acc, tk, name):
    S, W = dp.shape
    tm = _row_tile(S)

    def body(*refs):
        dp_ref, w_ref, o_ref = refs[0], refs[1], refs[-1]
        first = pl.program_id(1) == 0
        if acc is None:
            @pl.when(first)
            def _():
                o_ref[...] = jnp.zeros_like(o_ref)
        else:
            @pl.when(first)
            def _():
                o_ref[...] = refs[2][...]
        o_ref[...] += _dot(dp_ref[...], w_ref[...])

    row = pl.BlockSpec((tm, D), lambda i, k: (i, 0))
    return pl.pallas_call(
        body, name=name, grid=(S // tm, W // tk),
        in_specs=[pl.BlockSpec((tm, tk), lambda i, k: (i, k)), pl.BlockSpec((tk, D), lambda i, k: (k, 0))]
        + ([] if acc is None else [row]),
        out_specs=row, out_shape=jax.ShapeDtypeStruct((S, D), F32),
        input_output_aliases={} if acc is None else {2: 0},
        compiler_params=_params(("arbitrary", "arbitrary")),
    )(*((dp, wt) if acc is None else (dp, wt, acc)))


def dh_last(dp, wt, acc_in, x, g_pre, dy, tk, name):
    S, W = dp.shape
    tm = 512
    nk = W // tk

    def body(dp_ref, w_ref, a_ref, x_ref, g_ref, dy_ref, dx_ref, dg_ref, acc):
        i, k = pl.program_id(0), pl.program_id(1)

        @pl.when(k == 0)
        def _():
            acc[...] = a_ref[...]

        acc[...] += _dot(dp_ref[...], w_ref[...])

        @pl.when((k == nk - 1) & (i == 0))
        def _():
            dg_ref[...] = jnp.zeros_like(dg_ref)

        @pl.when(k == nk - 1)
        def _():
            xv = x_ref[...]
            dh = acc[...]
            g = g_ref[...]
            r = lax.rsqrt(jnp.mean(xv * xv, axis=-1, keepdims=True) + EPS)
            dhg = dh * g
            dx_ref[...] = dy_ref[...] + r * dhg - xv * (r * r * r) * jnp.mean(dhg * xv, axis=-1, keepdims=True)
            dg_ref[...] += jnp.sum(dh * xv * r, axis=0, keepdims=True)

    row = pl.BlockSpec((tm, D), lambda i, k: (i, 0))
    return pl.pallas_call(
        body, name=name, grid=(S // tm, nk),
        in_specs=[pl.BlockSpec((tm, tk), lambda i, k: (i, k)), pl.BlockSpec((tk, D), lambda i, k: (k, 0)),
                  row, row, _full((1, D)), row],
        out_specs=[row, _full((1, D))],
        out_shape=[jax.ShapeDtypeStruct((S, D), F32), jax.ShapeDtypeStruct((1, D), F32)],
        scratch_shapes=[pltpu.VMEM((tm, D), F32)],
        compiler_params=_params(("arbitrary", "arbitrary")),
    )(dp, wt, acc_in, x, g_pre, dy)


def dw_group(dp, h, tn, name):
    S, W = dp.shape
    ts = _row_tile(S)

    def body(dp_ref, h_ref, o_ref):
        @pl.when(pl.program_id(1) == 0)
        def _():
            o_ref[...] = jnp.zeros_like(o_ref)
        o_ref[...] += _dot_t(dp_ref[...], h_ref[...])

    return pl.pallas_call(
        body, name=name, grid=(W // tn, S // ts),
        in_specs=[pl.BlockSpec((ts, tn), lambda j, s: (s, j)), pl.BlockSpec((ts, D), lambda j, s: (s, 0))],
        out_specs=pl.BlockSpec((tn, D), lambda j, s: (j, 0)),
        out_shape=jax.ShapeDtypeStruct((W, D), F32),
        compiler_params=_params(("arbitrary", "arbitrary")),
    )(dp, h)


def matmul_tn(a, b, name, tn=1024):
    S, K = a.shape
    N = b.shape[1]
    ts = _row_tile(S)
    ns = S // ts

    def body(a_ref, b_ref, o_ref):
        @pl.when(pl.program_id(1) == 0)
        def _():
            o_ref[...] = jnp.zeros_like(o_ref)
        o_ref[...] += _dot_t(a_ref[...], b_ref[...])

    return pl.pallas_call(
        body, name=name, grid=(N // tn, ns),
        in_specs=[pl.BlockSpec((ts, K), lambda j, s: (s, 0)), pl.BlockSpec((ts, tn), lambda j, s: (s, j))],
        out_specs=pl.BlockSpec((K, tn), lambda j, s: (0, j)),
        out_shape=jax.ShapeDtypeStruct((K, N), F32),
        compiler_params=_params(("arbitrary", "arbitrary")),
    )(a, b)


def _att_in_specs(nb):
    last = nb - 1
    cur = lambda n: jnp.minimum(n, last)
    prev = lambda n: jnp.maximum(jnp.minimum(n, last) - 1, 0)
    return [
        pl.BlockSpec((L, 1024), lambda n: (cur(n), 0)),
        pl.BlockSpec((L, 128), lambda n: (prev(n), 16)),
        pl.BlockSpec((L, 128), lambda n: (cur(n), 16)),
        pl.BlockSpec((L, 128), lambda n: (prev(n), 17)),
        pl.BlockSpec((L, 128), lambda n: (cur(n), 17)),
        pl.BlockSpec((L, 1024), lambda n: (cur(n), 1)),
        _full((2, HEADS, L, 2 * L)),
        pl.BlockSpec(memory_space=pltpu.SMEM),
    ]


GH = HEADS // KV
GB = 8


def _stack_heads(ref, h0, nh, scr):
    for g in range(nh):
        scr[(h0 + g) * L:(h0 + g + 1) * L, :] = ref[:, (h0 + g) * DH:(h0 + g + 1) * DH].astype(F32)
    return scr[h0 * L:(h0 + nh) * L, :]


def _unstack_heads(val, h0, nh, ref):
    for g in range(nh):
        ref[:, (h0 + g) * DH:(h0 + g + 1) * DH] = val[g * L:(g + 1) * L, :]


def _sink_rows(s_ref, h0, nh):
    return jnp.concatenate([jnp.full((L, 1), s_ref[h0 + g], F32) for g in range(nh)], axis=0)


def _att_probs(qh, kk, bias_h, sk):
    logits = _dot_nt(qh, kk) + bias_h
    m =jnp.maximum(jnp.max(logits, axis=-1, keepdims=True), sk)
    p = jnp.exp(logits - m)
    es = jnp.exp(sk - m)
    den = jnp.sum(p, axis=-1, keepdims=True) + es
    return p / den, es / den


def att_fwd(proj, bias, sinks):
    S = proj.shape[0]
    nb = S // L

    def body(q_ref, kp_ref, kc_ref, vp_ref, vc_ref, z_ref, bias_ref, s_ref, y_ref, o_scr):
        table = jnp.where(pl.program_id(0) > 0, 1, 0)
        for kv in range(KV):
            sl = slice(kv * DH, (kv + 1) * DH)
            kk = jnp.concatenate([kp_ref[:, sl], kc_ref[:, sl]], axis=0).astype(BF16)
            vv = jnp.concatenate([vp_ref[:, sl], vc_ref[:, sl]], axis=0).astype(BF16)
            for g in range(GH):
                h = kv * GH + g
                hs = slice(h * DH, (h + 1) * DH)
                qh = (q_ref[:, hs] * 0.125).astype(BF16)
                P, _ = _att_probs(qh, kk, bias_ref[table, h], s_ref[h])
                o_scr[:, hs] = _dot(P.astype(BF16), vv)
        z = z_ref[...].astype(F32)
        y_ref[...] = (o_scr[...] * (z * _sigmoid(z))).astype(BF16)

    return pl.pallas_call(
        body, name="att_fwd", grid=(nb,),
        in_specs=_att_in_specs(nb),
        out_specs=pl.BlockSpec((L, 1024), lambda n: (n, 0)),
        out_shape=jax.ShapeDtypeStruct((S, 1024), BF16),
        scratch_shapes=[pltpu.VMEM((L, 1024), F32)],
        compiler_params=_params(("arbitrary",)),
    )(proj, proj, proj, proj, proj, proj, bias, sinks)


def att_bwd(dy, proj, bias, sinks):
    S = proj.shape[0]
    nb = S // L
    last = nb - 1

    def body(dy_ref, q_ref, kp_ref, kc_ref, vp_ref, vc_ref, z_ref, bias_ref, s_ref,
             dout_ref, dbias_ref, dsink_ref, carry, band, dq_scr, dz_scr, qs_scr, zs_scr, dys_scr):
        n = pl.program_id(0)

        @pl.when(n == 0)
        def _():
            carry[...] = jnp.zeros_like(carry)
            dq_scr[...] = jnp.zeros_like(dq_scr)
            dz_scr[...] = jnp.zeros_like(dz_scr)
            dbias_ref[...] = jnp.zeros_like(dbias_ref)
            dsink_ref[...] = jnp.zeros_like(dsink_ref)

        dout_ref[:, 0:1024] = dq_scr[...].astype(BF16)
        dout_ref[:, 1024:2048] = dz_scr[...].astype(BF16)
        band[...] = jnp.zeros_like(band)

        @pl.when(n < nb)
        def _():
            table = jnp.where(n > 0, 1, 0)
            lane = lax.broadcasted_iota(jnp.int32, (1, 128), 1)
            dsink = jnp.zeros((1, 128), F32)
            for kv in range(KV):
                sl = slice(kv * DH, (kv + 1) * DH)
                kk = jnp.concatenate([kp_ref[:, sl], kc_ref[:, sl]], axis=0).astype(BF16)
                vv = jnp.concatenate([vp_ref[:, sl], vc_ref[:, sl]], axis=0).astype(BF16)
                dk_acc = jnp.zeros((2 * L, DH), F32)
                dv_acc = jnp.zeros((2 * L, DH), F32)
                for h0 in range(kv * GH, (kv + 1) * GH, GB):
                    qs = (_stack_heads(q_ref, h0, GB, qs_scr) * 0.125).astype(BF16)
                    bias_g = bias_ref[table, h0:h0 + GB].reshape(GB * L, 2 * L)
                    P, psink = _att_probs(qs, kk, bias_g, _sink_rows(s_ref, h0, GB))
                    zs = _stack_heads(z_ref, h0, GB, zs_scr)
                    dys = _stack_heads(dy_ref, h0, GB, dys_scr)
                    sg = _sigmoid(zs)
                    O = _dot(P.astype(BF16), vv)
                    _unstack_heads(dys * O * (sg * (1.0 + zs * (1.0 - sg))), h0, GB, dz_scr)
                    dOb = (dys * (zs * sg)).astype(BF16)
                    dP = _dot_nt(dOb, vv)
                    delta = jnp.sum(P * dP, axis=-1, keepdims=True)
                    dS = P * (dP - delta)
                    sd = psink * delta
                    for g in range(GB):
                        dsink = dsink + jnp.where(lane == h0 + g, -jnp.sum(sd[g * L:(g + 1) * L, :]), 0.0)
                    _unstack_heads(_dot(dS.astype(BF16), kk) * 0.125, h0, GB, dq_scr)
                    dbias_ref[h0:h0 + GB] += dS.reshape(GB, L, 2 * L)
                    dk_acc = dk_acc + _dot_tn(dS, qs)
                    dv_acc = dv_acc + _dot_tn(P, dOb)
                band[:, sl] = dk_acc
                band[:, 128 + kv * DH:128 + (kv + 1) * DH] = dv_acc
            dsink_ref[...] += dsink

        out = carry[...] + band[0:L, :]
        dout_ref[:, 2048:2304] = out.astype(BF16)
        carry[...] = band[L:2 * L, :]

    cur = lambda n: jnp.minimum(n, last)
    lag = lambda n: jnp.maximum(n - 1, 0)
    return pl.pallas_call(
        body, name="att_bwd", grid=(nb + 1,),
        in_specs=[pl.BlockSpec((L, 1024), lambda n: (cur(n), 0))] + _att_in_specs(nb),
        out_specs=[pl.BlockSpec((L, 2304), lambda n: (lag(n), 0)), _full((HEADS, L, 2 * L)), _full((1, 128))],
        out_shape=[jax.ShapeDtypeStruct((S, 2304), BF16),
                   jax.ShapeDtypeStruct((HEADS, L, 2 * L), F32), jax.ShapeDtypeStruct((1, 128), F32)],
        scratch_shapes=[pltpu.VMEM((L, 256), F32), pltpu.VMEM((2 * L, 256), F32),
                        pltpu.VMEM((L, 1024), F32), pltpu.VMEM((L, 1024), F32)]
        + [pltpu.VMEM((HEADS * L, DH), F32)] * 3,
        compiler_params=_params(("arbitrary",)),
    )(dy, proj, proj, proj, proj, proj, proj, bias, sinks)


SGU_CH = 4


def _sgu_in_specs():
    return [
        pl.BlockSpec((SGU_CH * L, 1024), lambda c: (c, 0)),
        pl.BlockSpec((SGU_CH * L, 1024), lambda c: (c, 1)),
        pl.BlockSpec((SGU_CH * L, 1024), lambda c: (c, 2)),
        _full((1, 1024)), _full((1, 1024)), _full((8, L, L)), _full((L, 8)),
    ]


def _sgu_norm(v, lg, lb):
    mu = jnp.mean(v, axis=-1, keepdims=True)
    vc = v - mu
    rstd = lax.rsqrt(jnp.mean(vc * vc, axis=-1, keepdims=True) + EPS)
    xhat = vc * rstd
    return xhat * lg + lb, xhat, rstd


def _tril():
    return lax.broadcasted_iota(jnp.int32, (L, L), 0) >= lax.broadcasted_iota(jnp.int32, (L, L), 1)


def _sgu_side(a, g):
    return jnp.concatenate([a[c * L:(c + 1) * L, g * 128:(g + 1) * 128] for c in range(SGU_CH)], axis=1)


def _sgu_stack(parts):
    return jnp.concatenate([jnp.concatenate([p[:, c * L:(c + 1) * L] for p in parts], axis=1)
                            for c in range(SGU_CH)], axis=0)


def sgu_fwd(proj, ln_g, ln_b, w, b_t):
    S = proj.shape[0]

    def body(u_ref, v_ref, z_ref, lg_ref, lb_ref, w_ref, bt_ref, y_ref):
        vn, _, _ = _sgu_norm(v_ref[...].astype(F32), lg_ref[...], lb_ref[...])
        tri = _tril()
        parts = []
        for g in range(8):
            wg = jnp.where(tri, w_ref[g], 0.0).astype(BF16)
            parts.append(_dot(wg, _sgu_side(vn, g).astype(BF16)) + bt_ref[:, g:g + 1])
        mixed = _sgu_stack(parts)
        z = z_ref[...].astype(F32)
        y_ref[...] = (u_ref[...].astype(F32) * mixed * (z * _sigmoid(z))).astype(BF16)

    return pl.pallas_call(
        body, name="sgu_fwd", grid=(S // (SGU_CH * L),),
        in_specs=_sgu_in_specs(),
        out_specs=pl.BlockSpec((SGU_CH * L, 1024), lambda c: (c, 0)),
        out_shape=jax.ShapeDtypeStruct((S, 1024), BF16),
        compiler_params=_params(("arbitrary",)),
    )(proj, proj, proj, ln_g, ln_b, w, b_t)


def sgu_bwd(dy, proj, ln_g, ln_b, w, b_t):
    S = proj.shape[0]

    def body(dy_ref, u_ref, v_ref, z_ref, lg_ref, lb_ref, w_ref, bt_ref,
             dout_ref, dw_ref, dbt_ref, dlg_ref, dlb_ref):
        @pl.when(pl.program_id(0) == 0)
        def _():
            dw_ref[...] = jnp.zeros_like(dw_ref)
            dbt_ref[...] = jnp.zeros_like(dbt_ref)
            dlg_ref[...] = jnp.zeros_like(dlg_ref)
            dlb_ref[...] = jnp.zeros_like(dlb_ref)

        lg = lg_ref[...]
        vn, xhat, rstd = _sgu_norm(v_ref[...].astype(F32), lg, lb_ref[...])
        tri = _tril()
        lane = lax.broadcasted_iota(jnp.int32, (L, 128), 1)
        wgs, vns, parts = [], [], []
        for g in range(8):
            wg = jnp.where(tri, w_ref[g], 0.0)
            wgs.append(wg)
            vns.append(_sgu_side(vn, g).astype(BF16))
            parts.append(_dot(wg.astype(BF16), vns[g]) + bt_ref[:, g:g + 1])
        mixed = _sgu_stack(parts)
        z = z_ref[...].astype(F32)
        sg = _sigmoid(z)
        silu = z * sg
        dy_v = dy_ref[...]
        u = u_ref[...].astype(F32)
        dout_ref[:, 0:1024] = (dy_v * mixed * silu).astype(BF16)
        dout_ref[:, 2048:3072] = (dy_v * u * mixed * (sg * (1.0 + z * (1.0 - sg)))).astype(BF16)
        dmixed = dy_v * u * silu
        dbt = jnp.zeros((L, 128), F32)
        dvn_parts = []
        for g in range(8):
            dm = _sgu_side(dmixed, g)
            dmb = dm.astype(BF16)
            dbt = dbt + jnp.where(lane == g, jnp.sum(dm, axis=1, keepdims=True), 0.0)
            dw_ref[g] += jnp.where(tri, _dot_nt(dmb, vns[g]), 0.0)
            dvn_parts.append(_dot_tn(wgs[g], dmb))
        dbt_ref[...] += dbt
        dvn = _sgu_stack(dvn_parts)
        dlg_ref[...] += jnp.sum(dvn * xhat, axis=0, keepdims=True)
        dlb_ref[...] += jnp.sum(dvn, axis=0, keepdims=True)
        dxh = dvn * lg
        dv = rstd * (dxh - jnp.mean(dxh, axis=-1, keepdims=True)
                     - xhat * jnp.mean(dxh * xhat, axis=-1, keepdims=True))
        dout_ref[:, 1024:2048] = dv.astype(BF16)

    return pl.pallas_call(
        body, name="sgu_bwd", grid=(S // (SGU_CH * L),),
        in_specs=[pl.BlockSpec((SGU_CH * L, 1024), lambda c: (c, 0))] + _sgu_in_specs(),
        out_specs=[pl.BlockSpec((SGU_CH * L, 3072), lambda c: (c, 0)), _full((8, L, L)), _full((L, 128)),
                   _full((1, 1024)), _full((1, 1024))],
        out_shape=[jax.ShapeDtypeStruct((S, 3072), BF16), jax.ShapeDtypeStruct((8, L, L), F32),
                   jax.ShapeDtypeStruct((L, 128), F32), jax.ShapeDtypeStruct((1, 1024), F32),
                   jax.ShapeDtypeStruct((1, 1024), F32)],
        compiler_params=_params(("arbitrary",)),
    )(dy, proj, proj, proj, ln_g, ln_b, w, b_t)


def _expand_matrices():
    e = (np.arange(SSM_W)[None, :] // SSM_P == np.arange(128)[:, None]).astype(np.float32)
    return jnp.asarray(e, BF16), jnp.asarray(e.T, BF16)


def _rows_from(ref, start):
    C = ref.shape[1]
    tiles = ref[...].reshape(17, 8, C)
    q, s = divmod(start, 8)
    if s == 0:
        return tiles[q:q + 16].reshape(L, C)
    rolled = pltpu.roll(tiles, 8 - s, axis=1)
    sub = lax.broadcasted_iota(jnp.int32, (16, 8, C), 1)
    return jnp.where(sub < 8 - s, rolled[q:q + 16], rolled[q + 1:q + 17]).reshape(L, C)


def _ssd_common(ext_ref, cw_ref, cb_ref, dt_raw, dtb, alog):
    taps = [_rows_from(ext_ref, 5 + k) for k in range(CONV_K)]
    pre = cb_ref[...]
    for k in range(CONV_K):
        pre = pre + cw_ref[k:k + 1, :] * taps[k]
    sg_pre = _sigmoid(pre)
    xc = pre * sg_pre
    dt = _softplus(dt_raw + dtb)
    a = -jnp.exp(alog)
    adt = dt * a
    acs = _sel_dot(_tril(), adt, 3)
    return pre, sg_pre, xc, dt, a, acs, taps


def _ssd_in_specs(rev, nc):
    cidx = (lambda c: nc - 1 - c) if rev else (lambda c: c)
    return [
        pl.BlockSpec((L, 2048), lambda c: (cidx(c), 0)),
        pl.BlockSpec((L, 1024), lambda c: (cidx(c), 2)),
        pl.BlockSpec((L, 1024), lambda c: (cidx(c), 3)),
        pl.BlockSpec((L, 1024), lambda c: (cidx(c), 4)),
        pl.BlockSpec((L, 128), lambda c: (cidx(c), 40)),
        _full((8, CONV_C)), _full((1, CONV_C)), _full((1, 128)), _full((1, 128)), _full((1, 128)),
        _full((1, SSM_W)), _full((128, SSM_W)), _full((SSM_W, 128)),
    ]


def ssd_fwd(proj, conv_w, conv_b, dt_bias, a_log, d_skip, norm_g):
    S = proj.shape[0]
    nc = S // L

    def body(z_ref, xa_ref, xb_ref, xc_ref, dt_ref, cw_ref, cb_ref, dtb_ref, alog_ref, dsk_ref, ng_ref,
             ex_ref, ext_ref, y_ref, hs_ref, H, ext, ysc):
        @pl.when(pl.program_id(0) == 0)
        def _():
            H[...] = jnp.zeros_like(H)
            ext[0:8, :] = jnp.zeros((8, CONV_C), F32)

        for k, ref in enumerate((xa_ref, xb_ref, xc_ref)):
            ext[8:8 + L, k * 1024:(k + 1) * 1024] = ref[...].astype(F32)
        pre, sg_pre, xc, dt, a, acs, _ = _ssd_common(ext, cw_ref, cb_ref, dt_ref[...].astype(F32), dtb_ref[...],
                                                     alog_ref[...])
        for k, ref in enumerate((xa_ref, xb_ref, xc_ref)):
            ext[0:8, k * 1024:(k + 1) * 1024] = ref[L - 8:L, :].astype(F32)
        xs = xc[:, 0:SSM_W]
        acs_t = acs.T
        ex = ex_ref[...]
        dt_x = _dot_sel(dt, ex, 2)
        xdt = xs * dt_x
        eacs_x = _dot_sel(jnp.exp(acs), ex, 2)
        xw = xdt * _dot_sel(jnp.exp(acs[L - 1:L, :] - acs), ex, 2)
        cd_row = jnp.exp(acs[L - 1:L, :])
        hs_ref[0] = H[...]
        tri = _tril()
        for g in range(SSM_G):
            gs = slice(g * 512, (g + 1) * 512)
            bg = xc[:, SSM_W + g * SSM_N:SSM_W + (g + 1) * SSM_N].astype(BF16)
            cg = xc[:, SSM_W + 512 + g * SSM_N:SSM_W + 512 + (g + 1) * SSM_N].astype(BF16)
            G = _dot_nt(cg, bg)
            yoff = _dot_nt(cg, H[gs, :].astype(BF16)) * eacs_x[:, gs]
            Sg = _dot_tn(xw[:, gs], bg)
            for j in range(8):
                hh = g * 8 + j
                hs = slice(hh * SSM_P, (hh + 1) * SSM_P)
                seg = acs[:, hh:hh + 1] - acs_t[hh:hh + 1, :]
                dk = jnp.where(tri, jnp.exp(seg), 0.0)
                yd = _dot((G * dk).astype(BF16), xdt[:, hs].astype(BF16))
                ysc[:, hs] = yd + yoff[:, j * SSM_P:(j + 1) * SSM_P]
                H[hs, :] = H[hs, :] * cd_row[:, hh:hh + 1] + Sg[j * SSM_P:(j + 1) * SSM_P, :]
        d_x = _dot_sel(jnp.broadcast_to(dsk_ref[...], (8, 128)), ex, 3)[0:1, :]
        Y = ysc[...] + d_x * xs
        z = z_ref[...].astype(F32)
        yz = Y * (z * _sigmoid(z))
        ng = ng_ref[...]
        for g in range(SSM_G):
            gs = slice(g * 512, (g + 1) * 512)
            t = yz[:, gs]
            rstd = lax.rsqrt(jnp.mean(t * t, axis=-1, keepdims=True) + EPS)
            y_ref[:, gs] = (t * rstd * ng[:, gs]).astype(BF16)

    return pl.pallas_call(
        body, name="ssd_fwd", grid=(nc,),
        in_specs=_ssd_in_specs(False, nc),
        out_specs=[pl.BlockSpec((L, SSM_W), lambda c: (c, 0)), pl.BlockSpec((1, SSM_W, SSM_N), lambda c: (c, 0, 0))],
        out_shape=[jax.ShapeDtypeStruct((S, SSM_W), BF16), jax.ShapeDtypeStruct((nc, SSM_W, SSM_N), F32)],
        scratch_shapes=[pltpu.VMEM((SSM_W, SSM_N), F32), pltpu.VMEM((8 + L, CONV_C), F32),
                        pltpu.VMEM((L, SSM_W), F32)],
        compiler_params=_params(("arbitrary",)),
    )(proj, proj, proj, proj, proj, conv_w, conv_b, dt_bias, a_log, d_skip, norm_g, *_expand_matrices())


def ssd_bwd(dy, proj, hstates, conv_w, conv_b, dt_bias, a_log, d_skip, norm_g):
    S = proj.shape[0]
    nc = S // L
    cidx = lambda c: nc - 1 - c

    def body(dy_ref, z_ref, xa_ref, xb_ref, xc_ref, dt_ref, cw_ref, cb_ref, dtb_ref, alog_ref, dsk_ref, ng_ref,
             ex_ref, ext_ref, pa_ref, pb_ref, pc_ref, hp_ref,
             dout_ref, dcw_ref, dcb_ref, ddtb_ref, dalog_ref, ddsk_ref, dng_ref,
             dH, ext, dext, ysc, yoffsc, dxdt, dxc, tsc, rsum, csum):
        step = pl.program_id(0)
        c = nc - 1 - step

        @pl.when(step == 0)
        def _():
            dH[...] = jnp.zeros_like(dH)
            dext[L:L + 8, :] = jnp.zeros((8, CONV_C), F32)
            rsum[...] = jnp.zeros_like(rsum)
            csum[...] = jnp.zeros_like(csum)
            for r in (dcw_ref, dcb_ref, ddtb_ref, dalog_ref, ddsk_ref, dng_ref):
                r[...] = jnp.zeros_like(r)

        for k, (ref, prev) in enumerate(((xa_ref, pa_ref), (xb_ref, pb_ref), (xc_ref, pc_ref))):
            ext[0:8, k * 1024:(k + 1) * 1024] = jnp.where(c > 0, prev[8:16, :].astype(F32), 0.0)
            ext[8:8 + L, k * 1024:(k + 1) * 1024] = ref[...].astype(F32)
        dtb = dtb_ref[...]
        dt_raw = dt_ref[...].astype(F32)
        pre, sg_pre, xc, dt, a, acs, taps = _ssd_common(ext, cw_ref, cb_ref, dt_raw, dtb, alog_ref[...])
        xs = xc[:, 0:SSM_W]
        acs_t = acs.T
        ex = ex_ref[...]
        dt_x = _dot_sel(dt, ex, 2)
        xdt = xs * dt_x
        eacs_x = _dot_sel(jnp.exp(acs), ex, 2)
        dte_x = _dot_sel(jnp.exp(acs[L - 1:L, :] - acs), ex, 2)
        xw = xdt * dte_x
        cd_row = jnp.exp(acs[L - 1:L, :])
        tri = _tril()

        Gs, Cs, Bs = [], [], []
        for g in range(SSM_G):
            gs = slice(g * 512, (g + 1) * 512)
            bg = xc[:, SSM_W + g * SSM_N:SSM_W + (g + 1) * SSM_N].astype(BF16)
            cg = xc[:, SSM_W + 512 + g * SSM_N:SSM_W + 512 + (g + 1) * SSM_N].astype(BF16)
            G = _dot_nt(cg, bg)
            Gs.append(G), Cs.append(cg), Bs.append(bg)
            yoffsc[:, gs] = _dot_nt(cg, hp_ref[0, gs, :].astype(BF16)) * eacs_x[:, gs]
            for j in range(8):
                hh = g * 8 + j
                hs = slice(hh * SSM_P, (hh + 1) * SSM_P)
                seg = acs[:, hh:hh + 1] - acs_t[hh:hh + 1, :]
                dk = jnp.where(tri, jnp.exp(seg), 0.0)
                ysc[:, hs] = _dot((G * dk).astype(BF16), xdt[:, hs].astype(BF16))
        d_x = _dot_sel(jnp.broadcast_to(dsk_ref[...], (8, 128)), ex, 3)[0:1, :]
        yoff = yoffsc[...]
        Y = ysc[...] + yoff + d_x * xs

        z = z_ref[...].astype(F32)
        sgz = _sigmoid(z)
        silu_z = z * sgz
        yz = Y * silu_z
        ng = ng_ref[...]
        dout = dy_ref[...]
        dyn = dout * ng
        dyz_parts, dng_parts = [], []
        for g in range(SSM_G):
            gs = slice(g * 512, (g + 1) * 512)
            t = yz[:, gs]
            rstd = lax.rsqrt(jnp.mean(t * t, axis=-1, keepdims=True) + EPS)
            dng_parts.append(jnp.sum(dout[:, gs] * t * rstd, axis=0, keepdims=True))
            dn = dyn[:, gs]
            dyz_parts.append(rstd * dn - t * (rstd * rstd * rstd) * jnp.mean(dn * t, axis=-1, keepdims=True))
        dng_ref[...] += jnp.concatenate(dng_parts, axis=1)
        dyz = jnp.concatenate(dyz_parts, axis=1)
        dY = dyz * silu_z
        dout_ref[:, 0:SSM_W] = (dyz * Y * (sgz * (1.0 + z * (1.0 - sgz)))).astype(BF16)

        ex_t = ext_ref[...]
        ddsk_ref[...] += _dot_sel(jnp.broadcast_to(jnp.sum(dY * xs, axis=0, keepdims=True), (8, SSM_W)), ex_t, 3)[0:1, :]

        lane = lax.broadcasted_iota(jnp.int32, (L, 128), 1)
        last_col = lax.broadcasted_iota(jnp.int32, (1, L), 1) == L - 1
        for g in range(SSM_G):
            gs = slice(g * 512, (g + 1) * 512)
            G, cg, bg = Gs[g], Cs[g], Bs[g]
            hp_g = hp_ref[0, gs, :]
            dh_g = dH[gs, :]
            dY_g = dY[:, gs]
            dZ = dY_g * eacs_x[:, gs]
            dZb = dZ.astype(BF16)
            dC = _dot(dZb, hp_g.astype(BF16))
            dh_from_off = _dot_tn(dZ, cg)
            dhb = dh_g.astype(BF16)
            Q = _dot_nt(bg, dhb)
            dB = _dot(xw[:, gs].astype(BF16), dhb)
            qd = Q * dte_x[:, gs]
            dxdt[:, gs] = qd
            tsc[:, gs] = qd * xdt[:, gs]
            dG = jnp.zeros((L, L), F32)
            for j in range(8):
                hh = g * 8 + j
                hs = slice(hh * SSM_P, (hh + 1) * SSM_P)
                seg = acs[:, hh:hh + 1] - acs_t[hh:hh + 1, :]
                dk = jnp.where(tri, jnp.exp(seg), 0.0)
                M = G * dk
                dYh = dY[:, hs]
                dYhb = dYh.astype(BF16)
                dM = _dot_nt(dYhb, xdt[:, hs].astype(BF16))
                dxdt[:, hs] += _dot_tn(M, dYhb)
                dG = dG + dM * dk
                Wm = dM * M
                pj = slice(j * SSM_P, (j + 1) * SSM_P)
                cd_h = cd_row[:, hh:hh + 1]
                dcd = jnp.sum(dh_g[pj, :] * hp_g[pj, :]) * cd_h
                rsum[:, hh:hh + 1] = jnp.sum(Wm, axis=1, keepdims=True)
                csum[hh:hh + 1, :] = jnp.sum(Wm, axis=0, keepdims=True) - jnp.where(last_col, dcd, 0.0)
                dH[hs, :] = dh_g[pj, :] * cd_h + dh_from_off[pj, :]
            dGb = dG.astype(BF16)
            dC = dC + _dot(dGb, bg)
            dB = dB + _dot_tn(dG, cg)
            dxc[:, SSM_W + g * SSM_N:SSM_W + (g + 1) * SSM_N] = dB
            dxc[:, SSM_W + 512 + g * SSM_N:SSM_W + 512 + (g + 1) * SSM_N] = dC

        row = lax.broadcasted_iota(jnp.int32, (L, 128), 0)
        tv = tsc[...]
        t_last = _dot_sel(jnp.broadcast_to(jnp.sum(tv, axis=0, keepdims=True), (8, SSM_W)), ex_t, 3)[0:1, :]
        dacs = (rsum[...] - csum[...].T + _dot_sel(dY * yoff - tv, ex_t, 2) + jnp.where(row == L - 1, t_last, 0.0))
        triu = lax.broadcasted_iota(jnp.int32, (L, L), 0) <= lax.broadcasted_iota(jnp.int32, (L, L), 1)
        dadt = _sel_dot(triu, dacs, 3)
        dxdt_v = dxdt[...]
        ddt = _dot_sel(dxdt_v * xs, ex_t, 1) + dadt * a
        dalog_ref[...] += jnp.sum(dadt * dt * a, axis=0, keepdims=True)
        ddt_raw = jnp.where(lane < SSM_H, ddt * _sigmoid(dt_raw + dtb), 0.0)
        ddtb_ref[...] += jnp.sum(ddt_raw, axis=0, keepdims=True)
        dout_ref[:, 5120:5248] = ddt_raw.astype(BF16)
        dout_ref[:, 5248:5376] = jnp.zeros((L, 128), BF16)

        dxc[:, 0:SSM_W] = dxdt_v * dt_x + d_x * dY
        dpre = dxc[...] * (sg_pre * (1.0 + pre * (1.0 - sg_pre)))
        dcb_ref[...] += jnp.sum(dpre, axis=0, keepdims=True)
        dext[0:L, :] = dpre
        x_cur = ext[8:8 + L, :]
        dx = None
        for k in range(CONV_K):
            dsh = _rows_from(dext, 3 - k)
            term = cw_ref[k:k + 1, :] * dsh
            dx = term if dx is None else dx + term
            dcw_ref[k:k + 1, :] += jnp.sum(dsh * x_cur, axis=0, keepdims=True)
        dout_ref[:, SSM_W:SSM_W + CONV_C] = dx.astype(BF16)
        dext[L:L + 8, :] = dpre[0:8, :]

    big = lambda w: pl.BlockSpec((L, w), lambda c: (cidx(c), 0))
    return pl.pallas_call(
        body, name="ssd_bwd", grid=(nc,),
        in_specs=[big(SSM_W)] + _ssd_in_specs(True, nc) + [
            pl.BlockSpec((16, 1024), lambda c, k=k: (jnp.maximum(8 * cidx(c) - 1, 0), k)) for k in (2, 3, 4)] + [
            pl.BlockSpec((1, SSM_W, SSM_N), lambda c: (cidx(c), 0, 0))],
        out_specs=[big(5376), _full((8, CONV_C)), _full((1, CONV_C)),
                   _full((1, 128)), _full((1, 128)), _full((1, 128)), _full((1, SSM_W))],
        out_shape=[jax.ShapeDtypeStruct((S, 5376), BF16), jax.ShapeDtypeStruct((8, CONV_C), F32),
                   jax.ShapeDtypeStruct((1, CONV_C), F32), jax.ShapeDtypeStruct((1, 128), F32),
                   jax.ShapeDtypeStruct((1, 128), F32), jax.ShapeDtypeStruct((1, 128), F32),
                   jax.ShapeDtypeStruct((1, SSM_W), F32)],
        scratch_shapes=[pltpu.VMEM((SSM_W, SSM_N), F32), pltpu.VMEM((8 + L, CONV_C), F32),
                        pltpu.VMEM((L + 8, CONV_C), F32), pltpu.VMEM((L, SSM_W), F32),
                        pltpu.VMEM((L, SSM_W), F32), pltpu.VMEM((L, SSM_W), F32),
                        pltpu.VMEM((L, CONV_C), F32), pltpu.VMEM((L, SSM_W), F32),
                        pltpu.VMEM((L, 128), F32), pltpu.VMEM((128, L), F32)],
        compiler_params=_params(("arbitrary",)),
    )(dy, proj, proj, proj, proj, proj, conv_w, conv_b, dt_bias, a_log, d_skip, norm_g, *_expand_matrices(),
      proj, proj, proj, hstates)


def _resident(shape):
    nd = len(shape)
    return pl.BlockSpec(shape, lambda *_: (0,) * nd, pipeline_mode=pl.Buffered(1))


def merge_fwd(y_att, y_sg, y_ssm, proj, x, w_a, w_s, w_m, w_o, g_post):
    S = x.shape[0]
    tm = 256

    def body(ya_ref, ys_ref, ym_ref, gate_ref, x_ref, wa_ref, ws_ref, wm_ref, wo_ref, gp_ref,
             xn_ref, bra_ref, brs_ref, brm_ref, mg_ref, out_ref):
        bra = _dot(ya_ref[...], wa_ref[...])
        brs = _dot(ys_ref[...], ws_ref[...])
        brm = _dot(ym_ref[...], wm_ref[...])
        bra_ref[...] = bra.astype(BF16)
        brs_ref[...] = brs.astype(BF16)
        brm_ref[...] = brm.astype(BF16)
        gate = gate_ref[...].astype(F32)
        merged = (_sigmoid(gate[:, 0:1024]) * bra + _sigmoid(gate[:, 1024:2048]) * brs
                  + _sigmoid(gate[:, 2048:3072]) * brm)
        mb = merged.astype(BF16)
        mg_ref[...] = mb
        o = _dot(mb, wo_ref[...])
        out_ref[...] = o
        r = lax.rsqrt(jnp.mean(o * o, axis=-1, keepdims=True) + EPS)
        xn_ref[...] = x_ref[...] + o * r * gp_ref[...]

    row = lambda w: pl.BlockSpec((tm, w), lambda i: (i, 0))
    return pl.pallas_call(
        body, name="merge_fwd", grid=(S // tm,),
        in_specs=[row(1024), row(1024), row(2048), pl.BlockSpec((tm, 3072), lambda i: (i, 0)),
                  row(D), _resident((1024, D)), _resident((1024, D)), _resident((2048, D)), _resident((D, D)),
                  _full((1, D))],
        out_specs=[row(D)] * 6,
        out_shape=[jax.ShapeDtypeStruct((S, D), F32)] + [jax.ShapeDtypeStruct((S, D), BF16)] * 4
        + [jax.ShapeDtypeStruct((S, D), F32)],
        compiler_params=_params(("arbitrary",)),
    )(y_att, y_sg, y_ssm, proj, x, w_a, w_s, w_m, w_o, g_post)


def merge_bwd(dy, out, g_post, proj, br_a, br_s, br_m, w_a, w_s, w_m, w_o):
    S = dy.shape[0]
    tm = 256

    def body(dy_ref, o_ref, gp_ref, gate_ref, bra_ref, brs_ref, brm_ref, wa_ref, ws_ref, wm_ref, wo_ref,
             dout_ref, dba_ref, dbs_ref, dbm_ref, dgate_ref, dya_ref, dys_ref, dym_ref, dgp_ref):
        @pl.when(pl.program_id(0) == 0)
        def _():
            dgp_ref[...] = jnp.zeros_like(dgp_ref)

        o = o_ref[...]
        dyv = dy_ref[...]
        r = lax.rsqrt(jnp.mean(o * o, axis=-1, keepdims=True) + EPS)
        dyg = dyv * gp_ref[...]
        do = r * dyg - o * (r * r * r) * jnp.mean(dyg * o, axis=-1, keepdims=True)
        dgp_ref[...] += jnp.sum(dyv * o * r, axis=0, keepdims=True)
        dob = do.astype(BF16)
        dout_ref[...] = dob
        dmerged = _dot_nt(dob, wo_ref[...])
        for idx, (br_ref, dbr_ref, w_ref, dyi_ref) in enumerate((
                (bra_ref, dba_ref, wa_ref, dya_ref), (brs_ref, dbs_ref, ws_ref, dys_ref),
                (brm_ref, dbm_ref, wm_ref, dym_ref))):
            s = _sigmoid(gate_ref[:, idx * 1024:(idx + 1) * 1024].astype(F32))
            dbr = (dmerged * s).astype(BF16)
            dbr_ref[...] = dbr
            dgate_ref[:, idx * 1024:(idx + 1) * 1024] = (dmerged * br_ref[...].astype(F32) * s * (1.0 - s)).astype(BF16)
            dyi_ref[...] = _dot_nt(dbr, w_ref[...])

    row = lambda w: pl.BlockSpec((tm, w), lambda i: (i, 0))
    return pl.pallas_call(
        body, name="merge_bwd", grid=(S // tm,),
        in_specs=[row(D), row(D), _full((1, D)), pl.BlockSpec((tm, 3072), lambda i: (i, 0)),
                  row(D), row(D), row(D),
                  _resident((1024, D)), _resident((1024, D)), _resident((2048, D)), _resident((D, D))],
        out_specs=[row(D), row(D), row(D), row(D), row(3072), row(1024), row(1024), row(2048), _full((1, D))],
        out_shape=[jax.ShapeDtypeStruct((S, D), BF16)] * 4 + [
            jax.ShapeDtypeStruct((S, 3072), BF16), jax.ShapeDtypeStruct((S, 1024), F32),
            jax.ShapeDtypeStruct((S, 1024), F32), jax.ShapeDtypeStruct((S, 2048), F32),
            jax.ShapeDtypeStruct((1, D), F32)],
        compiler_params=_params(("arbitrary",)),
    )(dy, out, g_post, proj, br_a, br_s, br_m, w_a, w_s, w_m, w_o)


def loss_head(y, target):
    S = y.shape[0]
    tm = 512

    def body(y_ref, t_ref, dy_ref, loss_ref):
        @pl.when(pl.program_id(0) == 0)
        def _():
            loss_ref[...] = jnp.zeros_like(loss_ref)
        e = y_ref[...] - t_ref[...]
        dy_ref[...] = e * (1.0 / D)
        loss_ref[...] += 0.5 * jnp.sum(jnp.mean(e * e, axis=-1, keepdims=True))

    row = pl.BlockSpec((tm, D), lambda i: (i, 0))
    return pl.pallas_call(
        body, name="loss_head", grid=(S // tm,),
        in_specs=[row, row], out_specs=[row, _full((1, 128))],
        out_shape=[jax.ShapeDtypeStruct((S, D), F32), jax.ShapeDtypeStruct((1, 128), F32)],
        compiler_params=_params(("arbitrary",)),
    )(y, target)


def _adam(w, g, m, v):
    mn = ADAM_B1 * m + (1.0 - ADAM_B1) * g
    vn = ADAM_B2 * v + (1.0 - ADAM_B2) * (g * g)
    m_hat = mn / (1.0 - ADAM_B1 ** ADAM_STEP)
    v_hat = vn / (1.0 - ADAM_B2 ** ADAM_STEP)
    return -ADAM_LR * (m_hat / (jnp.sqrt(v_hat) + ADAM_EPS) + ADAM_WD * w), mn, vn


def adamw_big(w, m, v, halves0, sum1, cc, name, tr):
    _, R, C = w.shape
    nper = R // tr
    f, fb, n0, off_a, off_b = halves0
    p, pb, off1 = sum1

    def body(c_ref, w_ref, m_ref, v_ref, f_ref, fb_ref, p_ref, pb_ref, g_ref, d_ref, nm_ref, nv_ref):
        i = pl.program_id(0)
        half = jnp.where(i % nper >= n0, 1, 0)
        g0 = jnp.where(c_ref[0] == half, f_ref[...], fb_ref[...])
        g = jnp.where(i < nper, g0, p_ref[...] + pb_ref[...])
        g_ref[0] = g
        d_ref[0], nm_ref[0], nv_ref[0] = _adam(w_ref[0], g, m_ref[0], v_ref[0])

    def blk0(i, c):
        il = jnp.minimum(i, nper - 1)
        return (jnp.where(il >= n0, off_b + il - n0, off_a + il), 0)

    wblk = pl.BlockSpec((1, tr, C), lambda i, c: (i // nper, i % nper, 0))
    b0 = pl.BlockSpec((tr, C), blk0)
    b1 = pl.BlockSpec((tr, C), lambda i, c: (off1 + jnp.maximum(i - nper, 0), 0))
    grid_spec = pltpu.PrefetchScalarGridSpec(
        num_scalar_prefetch=1, grid=(2 * nper,),
        in_specs=[wblk, wblk, wblk, b0, b0, b1, b1], out_specs=[wblk] * 4)
    return pl.pallas_call(
        body, name=name, grid_spec=grid_spec,
        out_shape=[jax.ShapeDtypeStruct(w.shape, F32)] * 4,
        compiler_params=_params(("arbitrary",)),
    )(cc, w, m, v, f, fb, p, pb)


def adamw_plain(w, g, m, v, name):
    def body(w_ref, g_ref, m_ref, v_ref, d_ref, nm_ref, nv_ref):
        d_ref[...], nm_ref[...], nv_ref[...] = _adam(w_ref[...], g_ref[...], m_ref[...], v_ref[...])

    return pl.pallas_call(
        body, name=name, out_shape=[jax.ShapeDtypeStruct(w.shape, F32)] * 3, compiler_params=_params(),
    )(w, g, m, v)


SMALL = {"norm_pre": ("g_pre", 8), "norm_post": ("g_post", 8), "att_sinks": ("sinks", 8), "sg_ln_g": ("ln_g", 8),
         "sg_ln_b": ("ln_b", 8), "sg_w": ("sg_w", 1024), "sg_b": ("sg_bt", 8), "ssm_conv_b": ("conv_b", 24),
         "ssm_dt_bias": ("dt_bias", 8), "ssm_a_log": ("a_log", 8), "ssm_d": ("d_skip", 8), "ssm_norm_g": ("norm_g", 16)}
SMALL_LAYER_ROWS = sum(r for _, r in SMALL.values())
REL_ROW = DEPTH * SMALL_LAYER_ROWS
LOSS_ROW = REL_ROW + 32
SMALL_ROWS = LOSS_ROW + 8


def _small_rows():
    rows, r = {}, 0
    for l in range(DEPTH):
        for name, (_, n) in SMALL.items():
            rows[(l, name)] = r
            r += n
    return rows


def adamw_small(red, rel, small):
    names = list(SMALL) + ["rel_bias"]
    params = dict(small, rel_bias=rel)
    rows = _small_rows()

    def grad_of(red_ref, l, name, n):
        r0 = rows[(l, name)]
        if name == "sg_b":
            return red_ref[r0:r0 + 8, :]
        if n < 128:
            return red_ref[r0:r0 + 1, 0:n]
        return jnp.concatenate([red_ref[r0 + j:r0 + j + 1, :] for j in range(n // 128)], axis=1)

    def body(red_ref, *refs):
        ins, outs = refs[:3 * len(names)], refs[3 * len(names):]
        for i, name in enumerate(names):
            w_ref, m_ref, v_ref = ins[3 * i:3 * i + 3]
            o = outs[4 * i:4 * i + 4]
            if name == "rel_bias":
                g = red_ref[REL_ROW:REL_ROW + 32, 0:16]
                o[0][...] = g
                o[1][...], o[2][...], o[3][...] = _adam(w_ref[...], g, m_ref[...], v_ref[...])
                continue
            for l in range(DEPTH):
                if name == "sg_w":
                    for grp in range(8):
                        r0 = rows[(l, name)] + grp * 128
                        g = red_ref[r0:r0 + 128, :]
                        o[0][l, grp] = g
                        o[1][l, grp], o[2][l, grp], o[3][l, grp] = _adam(w_ref[l, grp], g, m_ref[l, grp], v_ref[l, grp])
                elif name == "sg_b":
                    g = grad_of(red_ref, l, name, 128)
                    o[0][l] = g
                    o[1][l], o[2][l], o[3][l] = _adam(w_ref[l], g, m_ref[l], v_ref[l])
                else:
                    sl = slice(l, l + 1)
                    g = grad_of(red_ref, l, name, w_ref.shape[-1])
                    o[0][sl, :] = g
                    o[1][sl, :], o[2][sl, :], o[3][sl, :] = _adam(w_ref[sl, :], g, m_ref[sl, :], v_ref[sl, :])

    flat_in = [a for name in names for a in params[name]]
    out_shape = [jax.ShapeDtypeStruct(params[name][0].shape, F32) for name in names for _ in range(4)]
    res = pl.pallas_call(body, name="adamw_small", out_shape=out_shape, compiler_params=_params())(red, *flat_in)
    return {name: tuple(res[4 * i:4 * i + 4]) for i, name in enumerate(names)}


ANY = pl.BlockSpec(memory_space=pl.ANY)


def _place():
    x, y, c = lax.axis_index("x"), lax.axis_index("y"), lax.axis_index("c")
    others = [(1 - x, y), (x, 1 - y), (1 - x, 1 - y)]
    return x, y, c, others


def _rcopy(src, dst, ssem, rsem, to):
    return pltpu.make_async_remote_copy(src_ref=src, dst_ref=dst, send_sem=ssem, recv_sem=rsem,
                                        device_id=to, device_id_type=MESH)


def gather_weights(arrs):
    n = len(arrs)

    def body(*refs):
        srcs, outs, ssem, rsem = refs[:n], refs[n:2 * n], refs[2 * n], refs[2 * n + 1]
        x, y, c, others = _place()
        me = 2 * x + y
        sib = (x, y, 1 - c)
        first = [_rcopy(srcs[i].at[c], outs[i].at[c, me], ssem.at[6 * i + k], rsem.at[6 * i + k], (ox, oy, c))
                 for i in range(n) for k, (ox, oy) in enumerate(others)]
        for cp in first:
            cp.start()
        passed = []
        for k, (ox, oy) in enumerate(others):
            for i in range(n):
                slot = outs[i].at[c, 2 * ox + oy]
                _rcopy(slot, slot, ssem.at[6 * i + k], rsem.at[6 * i + k], sib).wait_recv()
                fw = _rcopy(slot, slot, ssem.at[6 * i + 3 + k], rsem.at[6 * i + 3 + k], sib)
                fw.start()
                passed.append(fw)
        for k, (ox, oy) in enumerate(others):
            for i in range(n):
                slot = outs[i].at[1 - c, 2 * ox + oy]
                _rcopy(slot, slot, ssem.at[6 * i + 3 + k], rsem.at[6 * i + 3 + k], sib).wait_recv()
        for cp in first + passed:
            cp.wait_send()

    return pl.pallas_call(
        body, name="gather_weights",
        in_specs=[ANY] * n, out_specs=[ANY] * n,
        out_shape=[jax.ShapeDtypeStruct((2, SHARDS) + a.shape[1:], a.dtype) for a in arrs],
        scratch_shapes=[pltpu.SemaphoreType.DMA((6 * n,)), pltpu.SemaphoreType.DMA((6 * n,))],
    )(*arrs)


HBM = pl.BlockSpec(memory_space=pltpu.HBM)
SEM = pl.BlockSpec(memory_space=pltpu.SEMAPHORE)
EFFECT = pltpu.SideEffectType.DATAFLOW_SIDE_EFFECTING


def _in_hbm(a):
    return pltpu.with_memory_space_constraint(a, pltpu.HBM)


def gather_start(srcs, after, name, by_dest=False):
    n = len(srcs)
    lands = [_in_hbm(lax.empty((SHARDS,) + a.shape[-2:], a.dtype)) for a in srcs]
    na = len(after)

    def body(*refs):
        src, land = refs[:n], refs[n:2 * n]
        ssem, rsem, token = refs[2 * n + na], refs[2 * n + na + 1], refs[-1]
        x, y, c, others = _place()
        me = 2 * x + y
        for i in range(n):
            for k, (ox, oy) in enumerate(others):
                s = src[i].at[2 * ox + oy] if by_dest else src[i]
                _rcopy(s, land[i].at[me], ssem.at[3 * i + k], rsem.at[3 * i + k], (ox, oy, c)).start()
        token[...] = jnp.zeros_like(token)

    bufs = [_in_hbm(a) for a in srcs] + lands
    out = pl.pallas_call(
        body, name=name,
        out_shape=(pltpu.SemaphoreType.DMA((3 * n,)), pltpu.SemaphoreType.DMA((3 * n,)),
                   *[pltpu.HBM(b.shape, b.dtype) for b in bufs], jax.ShapeDtypeStruct((8, 128), F32)),
        in_specs=[HBM] * (2 * n) + [ANY] * na,
        out_specs=(SEM, SEM, *[HBM] * (2 * n), pl.BlockSpec(memory_space=pltpu.VMEM)),
        input_output_aliases={i: 2 + i for i in range(2 * n)},
        compiler_params=pltpu.CompilerParams(has_side_effects=EFFECT),
    )(*bufs, *after)
    return out[0], out[1], list(out[2:2 + n]), list(out[2 + n:2 + 2 * n]), out[-1]


def gather_wait(ssem, rsem, srcs, lands, after, name, by_dest=False):
    n = len(srcs)

    def body(*refs):
        src, land = refs[:n], refs[n:2 * n]
        s_sem, r_sem = refs[2 * n], refs[2 * n + 1]
        x, y, c, others = _place()
        for i in range(n):
            for k, (ox, oy) in enumerate(others):
                s = src[i].at[2 * ox + oy] if by_dest else src[i]
                cp = _rcopy(s, land[i].at[2 * ox + oy], s_sem.at[3 * i + k], r_sem.at[3 * i + k], (ox, oy, c))
                cp.wait_send()
                cp.wait_recv()

    bufs = list(srcs) + list(lands)
    out = pl.pallas_call(
        body, name=name,
        out_shape=tuple(pltpu.HBM(b.shape, b.dtype) for b in bufs),
        in_specs=[HBM] * (2 * n) + [SEM, SEM, ANY],
        out_specs=tuple([HBM] * (2 * n)),
        input_output_aliases={i: i for i in range(2 * n)},
        compiler_params=pltpu.CompilerParams(has_side_effects=EFFECT),
    )(*bufs, ssem, rsem, after)
    return list(out[n:2 * n])


def grad_sibling_exchange(arrs):
    n = len(arrs)

    def body(*refs):
        srcs, outs, ssem, rsem = refs[:n], refs[n:2 * n], refs[2 * n], refs[2 * n + 1]
        x, y, c, _ = _place()
        cps = [_rcopy(srcs[i].at[1 - c], outs[i], ssem.at[i], rsem.at[i], (x, y, 1 - c)) for i in range(n)]
        for cp in cps:
            cp.start()
        for cp in cps:
            cp.wait()

    return pl.pallas_call(
        body, name="grad_sibling_exchange",
        in_specs=[ANY] * n, out_specs=[ANY] * n,
        out_shape=[jax.ShapeDtypeStruct(a.shape[1:], F32) for a in arrs],
        scratch_shapes=[pltpu.SemaphoreType.DMA((n,)), pltpu.SemaphoreType.DMA((n,))],
    )(*arrs)


def grad_chip_sum(g, sb, cc, tr, name):
    _, _, R, C = g.shape
    blk = pl.BlockSpec((1, tr, C), lambda s, r, c: (s, r, 0))
    grid_spec = pltpu.PrefetchScalarGridSpec(
        num_scalar_prefetch=1, grid=(SHARDS, R // tr),
        in_specs=[pl.BlockSpec((1, 1, tr, C), lambda s, r, c: (c[0], s, r, 0)), blk],
        out_specs=[blk, blk])

    def body(c_ref, a_ref, b_ref, o_ref, ob_ref):
        t = a_ref[0] + b_ref[...]
        o_ref[...] = t
        ob_ref[...] = t.astype(BF16)

    return pl.pallas_call(
        body, name=name, grid_spec=grid_spec,
        out_shape=[jax.ShapeDtypeStruct((SHARDS, R, C), F32), jax.ShapeDtypeStruct((SHARDS, R, C), BF16)],
        compiler_params=_params(("arbitrary", "arbitrary")),
    )(cc, g, sb)


def grad_shard_sum(t, rb, me, tr, name):
    _, R, C = t.shape
    grid_spec = pltpu.PrefetchScalarGridSpec(
        num_scalar_prefetch=1, grid=(R // tr,),
        in_specs=[pl.BlockSpec((1, tr, C), lambda r, m: (m[0], r, 0)),
                  pl.BlockSpec((SHARDS, tr, C), lambda r, m: (0, r, 0))],
        out_specs=pl.BlockSpec((tr, C), lambda r, m: (r, 0)))

    def body(m_ref, t_ref, r_ref, o_ref):
        part = [jnp.where(m_ref[0] == s, t_ref[0], r_ref[s].astype(F32)) for s in range(SHARDS)]
        o_ref[...] = ((part[0] + part[1]) + part[2]) + part[3]

    return pl.pallas_call(
        body, name=name, grid_spec=grid_spec,
        out_shape=jax.ShapeDtypeStruct((R, C), F32),
        compiler_params=_params(("arbitrary",)),
    )(me, t, rb)


def grad_sibling_share(arrs, name):
    n = len(arrs)

    def body(*refs):
        srcs, outs, ssem, rsem = refs[:n], refs[n:2 * n], refs[2 * n], refs[2 * n + 1]
        x, y, c, _ = _place()
        cps = [_rcopy(srcs[i], outs[i], ssem.at[i], rsem.at[i], (x, y, 1 - c)) for i in range(n)]
        for cp in cps:
            cp.start()
        for cp in cps:
            cp.wait()

    return pl.pallas_call(
        body, name=name,
        in_specs=[ANY] * n, out_specs=[ANY] * n,
        out_shape=[jax.ShapeDtypeStruct(a.shape, F32) for a in arrs],
        scratch_shapes=[pltpu.SemaphoreType.DMA((n,)), pltpu.SemaphoreType.DMA((n,))],
    )(*arrs)


def _allreduce_rows(src, sib_buf, chips, out_ref, ssem, rsem):
    x, y, c, others = _place()
    me = 2 * x + y
    cp = _rcopy(src, sib_buf, ssem.at[0], rsem.at[0], (x, y, 1 - c))
    cp.start()
    cp.wait()
    chips[me] = src[...] + sib_buf[...]
    sends = [_rcopy(chips.at[me], chips.at[me], ssem.at[1 + k], rsem.at[1 + k], (ox, oy, c))
             for k, (ox, oy) in enumerate(others)]
    for s in sends:
        s.start()
    for k, (ox, oy) in enumerate(others):
        slot = chips.at[2 * ox + oy]
        _rcopy(slot, slot, ssem.at[1 + k], rsem.at[1 + k], (ox, oy, c)).wait_recv()
    for s in sends:
        s.wait_send()
    out_ref[...] = ((chips[0] + chips[1]) + chips[2]) + chips[3]


def _allreduce_scratch(rows):
    return [pltpu.VMEM((rows, 128), F32), pltpu.VMEM((SHARDS, rows, 128), F32),
            pltpu.SemaphoreType.DMA((4,)), pltpu.SemaphoreType.DMA((4,))]


def allreduce_rows(buf, name):
    rows = buf.shape[0]
    VM = pl.BlockSpec(memory_space=pltpu.VMEM)

    def body(src_ref, out_ref, sib_buf, chips, ssem, rsem):
        _allreduce_rows(src_ref, sib_buf, chips, out_ref, ssem, rsem)

    return pl.pallas_call(
        body, name=name, in_specs=[VM], out_specs=VM,
        out_shape=jax.ShapeDtypeStruct((rows, 128), F32),
        scratch_shapes=_allreduce_scratch(rows), compiler_params=_params(),
    )(buf)


def small_allreduce(grads, rel, loss_part):
    rows = _small_rows()
    keys = [(l, name) for l in range(DEPTH) for name in SMALL]
    flat = [grads[l][SMALL[name][0]] for l, name in keys] + [rel, loss_part]

    def body(*refs):
        ins = refs[:len(flat)]
        out_ref, src, sib_buf, chips, ssem, rsem = refs[len(flat):]
        src[...] = jnp.zeros_like(src)
        for (l, name), ref in zip(keys, ins):
            r0 = rows[(l, name)]
            if name == "sg_w":
                for grp in range(8):
                    src[r0 + grp * 128:r0 + (grp + 1) * 128, :] = ref[grp]
            elif name == "sg_b":
                src[r0:r0 + 8, :] = ref[...].T[0:8, :]
            else:
                for j in range(ref.shape[1] // 128):
                    src[r0 + j:r0 + j + 1, :] = ref[:, j * 128:(j + 1) * 128]
        src[REL_ROW:REL_ROW + 32, 0:16] = ins[-2][...]
        src[LOSS_ROW:LOSS_ROW + 1, :] = ins[-1][...]
        _allreduce_rows(src, sib_buf, chips, out_ref, ssem, rsem)

    return pl.pallas_call(
        body, name="small_allreduce",
        out_shape=jax.ShapeDtypeStruct((SMALL_ROWS, 128), F32),
        scratch_shapes=[pltpu.VMEM((SMALL_ROWS, 128), F32)] + _allreduce_scratch(SMALL_ROWS),
        compiler_params=_params(),
    )(*flat)


def _pad_lanes(v):
    return jnp.zeros((1, 128), F32).at[0, :v.shape[0]].set(v)


def layer_fwd(x, wts, bias):
    wt = wts["wt"]
    tn = {name: t for name, _, t in GROUPS}
    p_gate, h = inproj_first(x, wts["g_pre"], wt["gate"], tn["gate"], "inproj_gate")
    p_sgu, p_att, p_ssd = (inproj_group(h, wt[n], tn[n], "inproj_" + n, F32 if n == "att" else BF16)
                           for n in ("sgu", "att", "ssd"))
    y_att = att_fwd(p_att, bias, wts["sinks"])
    y_sg = sgu_fwd(p_sgu, wts["ln_g"], wts["ln_b"], wts["sg_w"], wts["sg_bt"])
    y_ssm, hst = ssd_fwd(p_ssd, wts["conv_w"], wts["conv_b"], wts["dt_bias"], wts["a_log"], wts["d_skip"],
                         wts["norm_g"])
    x_new, br_a, br_s, br_m, merged, out = merge_fwd(
        y_att, y_sg, y_ssm, p_gate, x, wts["w_a"], wts["w_s"], wts["w_m"], wts["w_o"], wts["g_post"])
    saved = dict(x=x, p_gate=p_gate, p_sgu=p_sgu, p_att=p_att, p_ssd=p_ssd, h=h,
                 y_att=y_att, y_sg=y_sg, y_ssm=y_ssm, hst=hst,
                 br_a=br_a, br_s=br_s, br_m=br_m, merged=merged, out=out)
    return x_new, saved


def layer_bwd(dy, wts, bias, sv):
    dps, grads = layer_bwd_params(dy, wts, bias, sv)
    dx, grads["g_pre"] = layer_bwd_input(dy, dps, wts, sv, wts["g_pre"])
    return dx, grads


def layer_bwd_input(dy, dps, wts, sv, g_pre):
    wt = wts["wt"]
    tn = {name: t for name, _, t in GROUPS}
    acc = None
    for n in ("gate", "sgu", "ssd"):
        acc = dh_group(dps[n], wt[n], acc, DH_TILE[n], "dh_" + n)
    return dh_last(dps["att"], wt["att"], acc, sv["x"], g_pre, dy, tn["att"], "dh_att")


def layer_bwd_params(dy, wts, bias, sv):
    dout, dba, dbs, dbm, d_gate, dya, dys, dym, dg_post = merge_bwd(
        dy, sv["out"], wts["g_post"], sv["p_gate"], sv["br_a"], sv["br_s"], sv["br_m"],
        wts["w_a"], wts["w_s"], wts["w_m"], wts["w_o"])
    d_att, dbias, dsinks = att_bwd(dya, sv["p_att"], bias, wts["sinks"])
    d_sgu, dsg_w, dsg_bt, dln_g, dln_b = sgu_bwd(dys, sv["p_sgu"], wts["ln_g"], wts["ln_b"], wts["sg_w"],
                                                 wts["sg_bt"])
    d_ssd, dcw, dcb, ddtb, dalog, ddsk, dng = ssd_bwd(
        dym, sv["p_ssd"], sv["hst"], wts["conv_w"], wts["conv_b"], wts["dt_bias"], wts["a_log"], wts["d_skip"],
        wts["norm_g"])
    dps = dict(gate=d_gate, sgu=d_sgu, att=d_att, ssd=d_ssd)
    tn = {name: t for name, _, t in GROUPS}
    grads = dict(
        w_in={n: dw_group(dps[n], sv["h"], tn[n], "dw_in_" + n) for n in dps},
        w_a=matmul_tn(sv["y_att"], dba, "dw_att"),
        w_s=matmul_tn(sv["y_sg"], dbs, "dw_sg"),
        w_m=matmul_tn(sv["y_ssm"], dbm, "dw_ssm"),
        w_o=matmul_tn(sv["merged"], dout, "dw_out"),
        g_post=dg_post, sinks=dsinks, ln_g=dln_g, ln_b=dln_b, sg_w=dsg_w, sg_bt=dsg_bt,
        conv_w=dcw, conv_b=dcb, dt_bias=ddtb, a_log=dalog, d_skip=ddsk, norm_g=dng, bias=dbias)
    return dps, grads


REST_OFF = (0, 256, 512, 1024, 1280)
GR_ROWS = 1536
GR_CONV = 1280
W_IN_SPLIT = 1600
W_IN_HALF = 1824


def kernel(x, w_in, norm_pre, norm_post, rel_bias, att_sinks, sg_ln_g, sg_ln_b, sg_w, sg_b, ssm_conv_w, ssm_conv_b, ssm_dt_bias, ssm_a_log, ssm_d, ssm_norm_g, w_br_att, w_br_sg, w_br_ssm, w_out, loss_target, m_w_in, m_norm_pre, m_norm_post, m_rel_bias, m_att_sinks, m_sg_ln_g, m_sg_ln_b, m_sg_w, m_sg_b, m_ssm_conv_w, m_ssm_conv_b, m_ssm_dt_bias, m_ssm_a_log, m_ssm_d, m_ssm_norm_g, m_w_br_att, m_w_br_sg, m_w_br_ssm, m_w_out, v_w_in, v_norm_pre, v_norm_post, v_rel_bias, v_att_sinks, v_sg_ln_g, v_sg_ln_b, v_sg_w, v_sg_b, v_ssm_conv_w, v_ssm_conv_b, v_ssm_dt_bias, v_ssm_a_log, v_ssm_d, v_ssm_norm_g, v_w_br_att, v_w_br_sg, v_w_br_ssm, v_w_out):
    cx, cy, cc = lax.axis_index("x"), lax.axis_index("y"), lax.axis_index("c")
    me = 2 * cx + cy
    xs = x[0]
    S = xs.shape[0]

    tr = lambda a: jnp.transpose(a, (0, 2, 1))
    w_in_b = tr(w_in).astype(BF16)
    w_rest_b = jnp.concatenate([w_br_att, w_br_sg, w_br_ssm, w_out], axis=1).astype(BF16)
    halves = lambda a: a.reshape(2, a.shape[0] // 2, a.shape[1])
    w_in0 = jnp.pad(w_in_b[0], ((0, W_IN_ROWS - 3400), (0, 0)))
    all0_in, all0_rest = gather_weights([halves(w_in0), halves(w_rest_b[0])])
    convw_slot = jnp.zeros((SHARDS, DEPTH * CONV_K * 768 // 128, 128), F32)
    convw_slot = lax.dynamic_update_index_in_dim(
        convw_slot, jnp.where(cc == 0, 1.0, 0.0) * ssm_conv_w.reshape(-1, 128), me, 0)
    convw_rows = allreduce_rows(convw_slot.reshape(-1, 128), "gather_conv_w")
    convw_all = convw_rows.reshape(SHARDS, DEPTH, CONV_K, 768).transpose(1, 2, 0, 3).reshape(DEPTH, CONV_K, CONV_C)
    g1_ssem, g1_rsem, g1_srcs, g1_lands, g1_token = gather_start(
        [w_in_b[1], w_rest_b[1]], [convw_rows, all0_rest], "gather_l1_start")

    o = REST_OFF

    def layer_weights(l, gathered_in, gathered_rest, g_pre):
        sh_in = [jnp.where(me == s, w_in_b[l], gathered_in[s]) for s in range(SHARDS)]
        sh_rest = [jnp.where(me == s, w_rest_b[l], gathered_rest[s]) for s in range(SHARDS)]
        rest = lambda k: jnp.concatenate([r[o[k]:o[k + 1]] for r in sh_rest], axis=0)
        return dict(
            wt=group_weights(jnp.concatenate(sh_in, axis=0)),
            w_a=rest(0), w_s=rest(1), w_m=rest(2), w_o=rest(3),
            g_pre=g_pre, g_post=norm_post[l][None], sinks=att_sinks[l],
            ln_g=sg_ln_g[l][None], ln_b=sg_ln_b[l][None], sg_w=sg_w[l],
            sg_bt=sg_b[l].T,
            conv_w=jnp.concatenate([convw_all[l], jnp.zeros((4, CONV_C), F32)], axis=0),
            conv_b=ssm_conv_b[l][None], dt_bias=_pad_lanes(ssm_dt_bias[l]), a_log=_pad_lanes(ssm_a_log[l]),
            d_skip=_pad_lanes(ssm_d[l]), norm_g=ssm_norm_g[l][None])

    bias = bias_table(rel_bias)
    layers = [layer_weights(0, [all0_in[:, s].reshape(W_IN_ROWS, D)[0:3400] for s in range(SHARDS)],
                            [all0_rest[:, s].reshape(1280, D) for s in range(SHARDS)],
                            (norm_pre[0] + g1_token[0, 0])[None])]
    act, sv0 = layer_fwd(xs, layers[0], bias)
    land_in, land_rest = gather_wait(g1_ssem, g1_rsem, g1_srcs, g1_lands, act, "gather_l1_wait")
    layers.append(layer_weights(1, land_in, land_rest, norm_pre[1][None]))
    act, sv1 = layer_fwd(act, layers[1], bias)
    saved = [sv0, sv1]
    dy, loss_part = loss_head(act, loss_target[0])
    cvec = jnp.reshape(cc, (1,)).astype(jnp.int32)
    mvec = jnp.reshape(me, (1,)).astype(jnp.int32)

    def by_shard(g):
        gcw = g["conv_w"][0:CONV_K].reshape(CONV_K, SHARDS, 768).transpose(1, 0, 2).reshape(SHARDS, 3, 1024)
        rest = jnp.concatenate([
            g["w_a"].reshape(SHARDS, 256, D), g["w_s"].reshape(SHARDS, 256, D), g["w_o"].reshape(SHARDS, 256, D),
            g["w_m"].reshape(SHARDS, 512, D), jnp.pad(gcw, ((0, 0), (0, GR_ROWS - GR_CONV - 3), (0, 0)))], axis=1)
        return ungroup_grads(g["w_in"]).reshape(SHARDS, 3400, D), rest

    grads = [None] * DEPTH
    dy, grads[1] = layer_bwd(dy, layers[1], bias, saved[1])
    g1_in, g1_rest = by_shard(grads[1])
    g1_in = jnp.pad(g1_in, ((0, 0), (0, W_IN_ROWS - 3400), (0, 0)))
    x1_ssem, x1_rsem, x1_srcs, x1_lands, x1_token = gather_start(
        [g1_in.astype(BF16), g1_rest.astype(BF16)], [], "grads_l1_start", by_dest=True)
    wts0 = dict(layers[0], g_post=layers[0]["g_post"] + x1_token[0, 0])
    dps0, grads[0] = layer_bwd_params(dy, wts0, bias, saved[0])
    r1_in, r1_rest = gather_wait(x1_ssem, x1_rsem, x1_srcs, x1_lands, grads[0]["w_in"]["ssd"], "grads_l1_wait",
                                 by_dest=True)
    p_in = grad_shard_sum(g1_in, r1_in, mvec, 384, "l1_sum_w_in")
    p_rest = grad_shard_sum(g1_rest, r1_rest, mvec, 512, "l1_sum_rest")
    pb_in, pb_rest = grad_sibling_share([p_in, p_rest], "l1_sibling_share")

    g0_in, g0_rest = by_shard(grads[0])
    pad_to = lambda a, rows: jnp.pad(a, ((0, 0), (0, rows - a.shape[1]), (0, 0)))
    g0_in = jnp.stack([pad_to(g0_in[:, 0:W_IN_SPLIT], W_IN_HALF), pad_to(g0_in[:, W_IN_SPLIT:3400], W_IN_HALF)])
    g0_rest = jnp.stack([g0_rest[:, 0:GR_ROWS // 2], g0_rest[:, GR_ROWS // 2:GR_ROWS]])
    sb_in, sb_rest = grad_sibling_exchange([g0_in, g0_rest])
    t_in, t_in_b = grad_chip_sum(g0_in, sb_in, cvec, 608, "chip_sum_w_in")
    t_rest, t_rest_b = grad_chip_sum(g0_rest, sb_rest, cvec, 384, "chip_sum_rest")
    x0_ssem, x0_rsem, x0_srcs, x0_lands, x0_token = gather_start([t_in_b, t_rest_b], [], "grads_l0_start", by_dest=True)
    dy, grads[0]["g_pre"] = layer_bwd_input(dy, dps0, layers[0], saved[0], layers[0]["g_pre"] + x0_token[0, 0])
    grad_x = dy[None]
    rb_in, rb_rest = gather_wait(x0_ssem, x0_rsem, x0_srcs, x0_lands, dy, "grads_l0_wait", by_dest=True)
    grad_rel_local = bias_grad(grads[0]["bias"] + grads[1]["bias"])
    f_in = grad_shard_sum(t_in, rb_in, mvec, 608, "shard_sum_w_in")
    f_rest = grad_shard_sum(t_rest, rb_rest, mvec, 384, "shard_sum_rest")
    fb_in, fb_rest = grad_sibling_share([f_in, f_rest], "l0_sibling_share")

    red = small_allreduce(grads, grad_rel_local, loss_part + 0.0 * f_rest[0:1, 0:128])
    loss = red[LOSS_ROW, 0]

    res = adamw_small(red, (rel_bias, m_rel_bias, v_rel_bias), dict(
        norm_pre=(norm_pre, m_norm_pre, v_norm_pre), norm_post=(norm_post, m_norm_post, v_norm_post),
        att_sinks=(att_sinks, m_att_sinks, v_att_sinks), sg_ln_g=(sg_ln_g, m_sg_ln_g, v_sg_ln_g),
        sg_ln_b=(sg_ln_b, m_sg_ln_b, v_sg_ln_b), sg_w=(sg_w, m_sg_w, v_sg_w), sg_b=(sg_b, m_sg_b, v_sg_b),
        ssm_conv_b=(ssm_conv_b, m_ssm_conv_b, v_ssm_conv_b), ssm_dt_bias=(ssm_dt_bias, m_ssm_dt_bias, v_ssm_dt_bias),
        ssm_a_log=(ssm_a_log, m_ssm_a_log, v_ssm_a_log), ssm_d=(ssm_d, m_ssm_d, v_ssm_d),
        ssm_norm_g=(ssm_norm_g, m_ssm_norm_g, v_ssm_norm_g)))
    res["w_in"] = tuple(tr(a) for a in adamw_big(
        tr(w_in), tr(m_w_in), tr(v_w_in), (f_in, fb_in, W_IN_SPLIT // 200, 0, 0), (p_in, pb_in, 0), cvec, "adamw_w_in", 200))
    rest_upd = lambda w, m, v, name, n0, off0, off1: adamw_big(
        w, m, v, (f_rest, fb_rest, n0, off0, off0), (p_rest, pb_rest, off1), cvec, name, 256)
    res["w_br_att"] = rest_upd(w_br_att, m_w_br_att, v_w_br_att, "adamw_w_br_att", 1, 0, 0)
    res["w_br_sg"] = rest_upd(w_br_sg, m_w_br_sg, v_w_br_sg, "adamw_w_br_sg", 1, 1, 1)
    res["w_out"] = rest_upd(w_out, m_w_out, v_w_out, "adamw_w_out", 1, 2, 2)
    res["w_br_ssm"] = rest_upd(w_br_ssm, m_w_br_ssm, v_w_br_ssm, "adamw_w_br_ssm", 0, 0, 3)
    cw0 = jnp.where(cc == 1, f_rest, fb_rest)[GR_CONV - GR_ROWS // 2:GR_CONV - GR_ROWS // 2 + 3]
    cw1 = (p_rest + pb_rest)[GR_CONV:GR_CONV + 3]
    g_conv_w = jnp.stack([cw0.reshape(CONV_K, 768), cw1.reshape(CONV_K, 768)])
    res["ssm_conv_w"] = (g_conv_w,) + tuple(adamw_plain(ssm_conv_w, g_conv_w, m_ssm_conv_w, v_ssm_conv_w, "adamw_conv_w"))

    order = ["w_in", "norm_pre", "norm_post", "rel_bias", "att_sinks", "sg_ln_g", "sg_ln_b", "sg_w", "sg_b",
             "ssm_conv_w", "ssm_conv_b", "ssm_dt_bias", "ssm_a_log", "ssm_d", "ssm_norm_g",
             "w_br_att", "w_br_sg", "w_br_ssm", "w_out"]
    return (loss, grad_x, *[res[n][0] for n in order], *[res[n][1] for n in order],
            *[res[n][2] for n in order], *[res[n][3] for n in order])
```

```python
import functools
import math

import numpy as np
import jax
import jax.numpy as jnp
from jax import lax
from jax.experimental import pallas as pl
from jax.experimental.pallas import tpu as pltpu

F32 = jnp.float32
BF16 = jnp.bfloat16
MESH = pl.DeviceIdType.MESH

D = 1024
DEPTH = 2
EPS = 1e-6
L = 128
HEADS = 16
KV = 2
DH = 64
SSM_W = 2048
SSM_H = 32
SSM_P = 64
SSM_G = 4
SSM_N = 128
CONV_K = 4
CONV_C = 3072
NEG = -1e30
IN_COLS = 13600

GROUPS = (("gate", 3072, 3072), ("sgu", 3072, 3072), ("att", 2304, 2304), ("ssd", 5376, 2688))
W_IN_ROWS = 3456
DH_TILE = {"gate": 3072, "sgu": 3072, "ssd": 2688}

ADAM_LR = 0.001
ADAM_B1 = 0.9
ADAM_B2 = 0.999
ADAM_EPS = 1e-08
ADAM_WD = 0.01
ADAM_STEP = 10

VMEM_LIMIT = 56 * 1024 * 1024

SHARDS = 4


def _dot(a, b):
    return jnp.dot(a, b, preferred_element_type=F32)


def _dot_nt(a, b):
    return lax.dot_general(a, b, (((1,), (1,)), ((), ())), preferred_element_type=F32)


def _dot_tn(a_f32, b):
    return jnp.dot(a_f32.T.astype(BF16), b, preferred_element_type=F32)


def _dot_t(a, b):
    return lax.dot_general(a, b, (((0,), (0,)), ((), ())), preferred_element_type=F32)


def _dot_hi(a, b):
    return jnp.dot(a, b, preferred_element_type=F32, precision=lax.Precision.HIGHEST)


def _pieces(x, n):
    out = []
    for _ in range(n - 1):
        p = x.astype(BF16)
        out.append(p)
        x = x - p.astype(F32)
    out.append(x.astype(BF16))
    return out


def _dot_sel(a, sel, n):
    sel = sel.astype(BF16)
    acc = None
    for p in _pieces(a, n):
        t = _dot(p, sel)
        acc = t if acc is None else acc + t
    return acc


def _sel_dot(sel, b, n):
    sel = sel.astype(BF16)
    acc = None
    for p in _pieces(b, n):
        t = _dot(sel, p)
        acc = t if acc is None else acc + t
    return acc


def _sigmoid(x):
    return 1.0 / (1.0 + jnp.exp(-x))


def _softplus(x):
    return jnp.maximum(x, 0.0) + jnp.log(1.0 + jnp.exp(-jnp.abs(x)))


def _params(sem=None, vmem=VMEM_LIMIT):
    kw = dict(vmem_limit_bytes=vmem)
    if sem is not None:
        kw["dimension_semantics"] = sem
    return pltpu.CompilerParams(**kw)


def _full(shape):
    nd = len(shape)
    return pl.BlockSpec(shape, lambda *_: (0,) * nd)


def group_weights(wt):
    return dict(
        gate=wt[10528:13600],
        sgu=wt[2304:5376],
        att=jnp.concatenate([wt[0:1024], wt[1280:2304], wt[1024:1280]], axis=0),
        ssd=jnp.concatenate([wt[5376:10496], wt[10496:10528], jnp.zeros((224, D), wt.dtype)], axis=0))


def ungroup_grads(g):
    a, s = g["att"], g["ssd"]
    return jnp.concatenate([a[0:1024], a[2048:2304], a[1024:2048], g["sgu"], s[0:5152], g["gate"]], axis=0)


def _bucket_table():
    qi = np.arange(L)[:, None]
    kj = np.arange(2 * L)[None, :]
    dist = np.maximum(qi + L - kj, 0)
    dist_f = np.maximum(dist, 1).astype(np.float32)
    large = 16 + (np.log(dist_f / np.float32(16)) / np.float32(math.log(128 / 16)) * np.float32(16)).astype(np.int32)
    large = np.minimum(large, 31)
    return np.where(dist < 16, dist, large).astype(np.int32)


def bias_table(rel_bias):
    buckets = jnp.asarray(_bucket_table().reshape(1, L * 2 * L))

    def body(rb_ref, bk_ref, out_ref):
        onehot = (lax.broadcasted_iota(jnp.int32, (32, L * 2 * L), 0) == bk_ref[...]).astype(F32)
        out_ref[...] = lax.dot_general(rb_ref[...], onehot, (((0,), (0,)), ((), ())),
                                       preferred_element_type=F32, precision=lax.Precision.HIGHEST)

    out = pl.pallas_call(
        body, name="bias_table",
        out_shape=jax.ShapeDtypeStruct((HEADS, L * 2 * L), F32),
        compiler_params=_params(),
    )(rel_bias, buckets)
    out = out.reshape(HEADS, L, 2 * L)
    win = _window_mask()
    first = win & (np.arange(2 * L)[None, :] >= L)
    return jnp.stack([jnp.where(first, out, NEG), jnp.where(win, out, NEG)])


def _window_mask():
    dist = np.arange(L)[:, None] + L - np.arange(2 * L)[None, :]
    return (dist >= 0) & (dist < L)


def bias_grad(dbias):
    buckets = jnp.asarray(_bucket_table().reshape(1, L * 2 * L))

    def body(db_ref, bk_ref, out_ref):
        onehot = (lax.broadcasted_iota(jnp.int32, (32, L * 2 * L), 0) == bk_ref[...]).astype(F32)
        out_ref[...] = lax.dot_general(onehot, db_ref[...], (((1,), (1,)), ((), ())),
                                       preferred_element_type=F32, precision=lax.Precision.HIGHEST)

    return pl.pallas_call(
        body, name="bias_grad",
        out_shape=jax.ShapeDtypeStruct((32, HEADS), F32),
        compiler_params=_params(),
    )(dbias.reshape(HEADS, L * 2 * L), buckets)


def _row_tile(S):
    return 1024 if S % 1024 == 0 else 512


def inproj_first(x, g_pre, wt, tn, name):
    S, W = x.shape[0], wt.shape[0]
    tm = _row_tile(S)

    def body(x_ref, g_ref, w_ref, o_ref, h_ref):
        @pl.when(pl.program_id(1) == 0)
        def _():
            xv = x_ref[...]
            r = lax.rsqrt(jnp.mean(xv * xv, axis=-1, keepdims=True) + EPS)
            h_ref[...] = (xv * r * g_ref[...]).astype(BF16)
        o_ref[...] = _dot_nt(h_ref[...], w_ref[...]).astype(BF16)

    return pl.pallas_call(
        body, name=name, grid=(S // tm, W // tn),
        in_specs=[pl.BlockSpec((tm, D), lambda i, j: (i, 0)), _full((1, D)),
                  pl.BlockSpec((tn, D), lambda i, j: (j, 0))],
        out_specs=[pl.BlockSpec((tm, tn), lambda i, j: (i, j)), pl.BlockSpec((tm, D), lambda i, j: (i, 0))],
        out_shape=[jax.ShapeDtypeStruct((S, W), BF16), jax.ShapeDtypeStruct((S, D), BF16)],
        compiler_params=_params(("arbitrary", "arbitrary")),
    )(x, g_pre, wt)


def inproj_group(h, wt, tn, name, dtype):
    S, W = h.shape[0], wt.shape[0]
    tm = _row_tile(S)

    def body(h_ref, w_ref, o_ref):
        o_ref[...] = _dot_nt(h_ref[...], w_ref[...]).astype(dtype)

    return pl.pallas_call(
        body, name=name, grid=(S // tm, W // tn),
        in_specs=[pl.BlockSpec((tm, D), lambda i, j: (i, 0)), pl.BlockSpec((tn, D), lambda i, j: (j, 0))],
        out_specs=pl.BlockSpec((tm, tn), lambda i, j: (i, j)),
        out_shape=jax.ShapeDtypeStruct((S, W), dtype),
        compiler_params=_params(("arbitrary", "arbitrary")),
    )(h, wt)


def dh_group(dp, wt, acc, tk, name):
    S, W = dp.shape
    tm = _row_tile(S)

    def body(*refs):
        dp_ref, w_ref, o_ref = refs[0], refs[1], refs[-1]
        first = pl.program_id(1) == 0
        if acc is None:
            @pl.when(first)
            def _():
                o_ref[...] = jnp.zeros_like(o_ref)
        else:
            @pl.when(first)
            def _():
                o_ref[...] = refs[2][...]
        o_ref[...] += _dot(dp_ref[...], w_ref[...])

    row = pl.BlockSpec((tm, D), lambda i, k: (i, 0))
    return pl.pallas_call(
        body, name=name, grid=(S // tm, W // tk),
        in_specs=[pl.BlockSpec((tm, tk), lambda i, k: (i, k)), pl.BlockSpec((tk, D), lambda i, k: (k, 0))]
        + ([] if acc is None else [row]),
        out_specs=row, out_shape=jax.ShapeDtypeStruct((S, D), F32),
        input_output_aliases={} if acc is None else {2: 0},
        compiler_params=_params(("arbitrary", "arbitrary")),
    )(*((dp, wt) if acc is None else (dp, wt, acc)))


def dh_last(dp, wt, acc_in, x, g_pre, dy, tk, name):
    S, W = dp.shape
    tm = 512
    nk = W // tk

    def body(dp_ref, w_ref, a_ref, x_ref, g_ref, dy_ref, dx_ref, dg_ref, acc):
        i, k = pl.program_id(0), pl.program_id(1)

        @pl.when(k == 0)
        def _():
            acc[...] = a_ref[...]

        acc[...] += _dot(dp_ref[...], w_ref[...])

        @pl.when((k == nk - 1) & (i == 0))
        def _():
            dg_ref[...] = jnp.zeros_like(dg_ref)

        @pl.when(k == nk - 1)
        def _():
            xv = x_ref[...]
            dh = acc[...]
            g = g_ref[...]
            r = lax.rsqrt(jnp.mean(xv * xv, axis=-1, keepdims=True) + EPS)
            dhg = dh * g
            dx_ref[...] = dy_ref[...] + r * dhg - xv * (r * r * r) * jnp.mean(dhg * xv, axis=-1, keepdims=True)
            dg_ref[...] += jnp.sum(dh * xv * r, axis=0, keepdims=True)

    row = pl.BlockSpec((tm, D), lambda i, k: (i, 0))
    return pl.pallas_call(
        body, name=name, grid=(S // tm, nk),
        in_specs=[pl.BlockSpec((tm, tk), lambda i, k: (i, k)), pl.BlockSpec((tk, D), lambda i, k: (k, 0)),
                  row, row, _full((1, D)), row],
        out_specs=[row, _full((1, D))],
        out_shape=[jax.ShapeDtypeStruct((S, D), F32), jax.ShapeDtypeStruct((1, D), F32)],
        scratch_shapes=[pltpu.VMEM((tm, D), F32)],
        compiler_params=_params(("arbitrary", "arbitrary")),
    )(dp, wt, acc_in, x, g_pre, dy)


def dw_group(dp, h, tn, name):
    S, W = dp.shape
    ts = _row_tile(S)

    def body(dp_ref, h_ref, o_ref):
        @pl.when(pl.program_id(1) == 0)
        def _():
            o_ref[...] = jnp.zeros_like(o_ref)
        o_ref[...] += _dot_t(dp_ref[...], h_ref[...])

    return pl.pallas_call(
        body, name=name, grid=(W // tn, S // ts),
        in_specs=[pl.BlockSpec((ts, tn), lambda j, s: (s, j)), pl.BlockSpec((ts, D), lambda j, s: (s, 0))],
        out_specs=pl.BlockSpec((tn, D), lambda j, s: (j, 0)),
        out_shape=jax.ShapeDtypeStruct((W, D), F32),
        compiler_params=_params(("arbitrary", "arbitrary")),
    )(dp, h)


def matmul_tn(a, b, name, tn=1024):
    S, K = a.shape
    N = b.shape[1]
    ts = _row_tile(S)
    ns = S // ts

    def body(a_ref, b_ref, o_ref):
        @pl.when(pl.program_id(1) == 0)
        def _():
            o_ref[...] = jnp.zeros_like(o_ref)
        o_ref[...] += _dot_t(a_ref[...], b_ref[...])

    return pl.pallas_call(
        body, name=name, grid=(N // tn, ns),
        in_specs=[pl.BlockSpec((ts, K), lambda j, s: (s, 0)), pl.BlockSpec((ts, tn), lambda j, s: (s, j))],
        out_specs=pl.BlockSpec((K, tn), lambda j, s: (0, j)),
        out_shape=jax.ShapeDtypeStruct((K, N), F32),
        compiler_params=_params(("arbitrary", "arbitrary")),
    )(a, b)


def _att_in_specs(nb):
    last = nb - 1
    cur = lambda n: jnp.minimum(n, last)
    prev = lambda n: jnp.maximum(jnp.minimum(n, last) - 1, 0)
    return [
        pl.BlockSpec((L, 1024), lambda n: (cur(n), 0)),
        pl.BlockSpec((L, 128), lambda n: (prev(n), 16)),
        pl.BlockSpec((L, 128), lambda n: (cur(n), 16)),
        pl.BlockSpec((L, 128), lambda n: (prev(n), 17)),
        pl.BlockSpec((L, 128), lambda n: (cur(n), 17)),
        pl.BlockSpec((L, 1024), lambda n: (cur(n), 1)),
        _full((2, HEADS, L, 2 * L)),
        pl.BlockSpec(memory_space=pltpu.SMEM),
    ]


GH = HEADS // KV
GB = 8


def _stack_heads(ref, h0, nh, scr):
    for g in range(nh):
        scr[(h0 + g) * L:(h0 + g + 1) * L, :] = ref[:, (h0 + g) * DH:(h0 + g + 1) * DH].astype(F32)
    return scr[h0 * L:(h0 + nh) * L, :]


def _unstack_heads(val, h0, nh, ref):
    for g in range(nh):
        ref[:, (h0 + g) * DH:(h0 + g + 1) * DH] = val[g * L:(g + 1) * L, :]


def _sink_rows(s_ref, h0, nh):
    return jnp.concatenate([jnp.full((L, 1), s_ref[h0 + g], F32) for g in range(nh)], axis=0)


def _att_probs(qh, kk, bias_h, sk):
    logits = _dot_nt(qh, kk) + bias_h
    m =jnp.maximum(jnp.max(logits, axis=-1, keepdims=True), sk)
    p = jnp.exp(logits - m)
    es = jnp.exp(sk - m)
    den = jnp.sum(p, axis=-1, keepdims=True) + es
    return p / den, es / den


def att_fwd(proj, bias, sinks):
    S = proj.shape[0]
    nb = S // L

    def body(q_ref, kp_ref, kc_ref, vp_ref, vc_ref, z_ref, bias_ref, s_ref, y_ref, o_scr):
        table = jnp.where(pl.program_id(0) > 0, 1, 0)
        for kv in range(KV):
            sl = slice(kv * DH, (kv + 1) * DH)
            kk = jnp.concatenate([kp_ref[:, sl], kc_ref[:, sl]], axis=0).astype(BF16)
            vv = jnp.concatenate([vp_ref[:, sl], vc_ref[:, sl]], axis=0).astype(BF16)
            for g in range(GH):
                h = kv * GH + g
                hs = slice(h * DH, (h + 1) * DH)
                qh = (q_ref[:, hs] * 0.125).astype(BF16)
                P, _ = _att_probs(qh, kk, bias_ref[table, h], s_ref[h])
                o_scr[:, hs] = _dot(P.astype(BF16), vv)
        z = z_ref[...].astype(F32)
        y_ref[...] = (o_scr[...] * (z * _sigmoid(z))).astype(BF16)

    return pl.pallas_call(
        body, name="att_fwd", grid=(nb,),
        in_specs=_att_in_specs(nb),
        out_specs=pl.BlockSpec((L, 1024), lambda n: (n, 0)),
        out_shape=jax.ShapeDtypeStruct((S, 1024), BF16),
        scratch_shapes=[pltpu.VMEM((L, 1024), F32)],
        compiler_params=_params(("arbitrary",)),
    )(proj, proj, proj, proj, proj, proj, bias, sinks)


def att_bwd(dy, proj, bias, sinks):
    S = proj.shape[0]
    nb = S // L
    last = nb - 1

    def body(dy_ref, q_ref, kp_ref, kc_ref, vp_ref, vc_ref, z_ref, bias_ref, s_ref,
             dout_ref, dbias_ref, dsink_ref, carry, band, dq_scr, dz_scr, qs_scr, zs_scr, dys_scr):
        n = pl.program_id(0)

        @pl.when(n == 0)
        def _():
            carry[...] = jnp.zeros_like(carry)
            dq_scr[...] = jnp.zeros_like(dq_scr)
            dz_scr[...] = jnp.zeros_like(dz_scr)
            dbias_ref[...] = jnp.zeros_like(dbias_ref)
            dsink_ref[...] = jnp.zeros_like(dsink_ref)

        dout_ref[:, 0:1024] = dq_scr[...].astype(BF16)
        dout_ref[:, 1024:2048] = dz_scr[...].astype(BF16)
        band[...] = jnp.zeros_like(band)

        @pl.when(n < nb)
        def _():
            table = jnp.where(n > 0, 1, 0)
            lane = lax.broadcasted_iota(jnp.int32, (1, 128), 1)
            dsink = jnp.zeros((1, 128), F32)
            for kv in range(KV):
                sl = slice(kv * DH, (kv + 1) * DH)
                kk = jnp.concatenate([kp_ref[:, sl], kc_ref[:, sl]], axis=0).astype(BF16)
                vv = jnp.concatenate([vp_ref[:, sl], vc_ref[:, sl]], axis=0).astype(BF16)
                dk_acc = jnp.zeros((2 * L, DH), F32)
                dv_acc = jnp.zeros((2 * L, DH), F32)
                for h0 in range(kv * GH, (kv + 1) * GH, GB):
                    qs = (_stack_heads(q_ref, h0, GB, qs_scr) * 0.125).astype(BF16)
                    bias_g = bias_ref[table, h0:h0 + GB].reshape(GB * L, 2 * L)
                    P, psink = _att_probs(qs, kk, bias_g, _sink_rows(s_ref, h0, GB))
                    zs = _stack_heads(z_ref, h0, GB, zs_scr)
                    dys = _stack_heads(dy_ref, h0, GB, dys_scr)
                    sg = _sigmoid(zs)
                    O = _dot(P.astype(BF16), vv)
                    _unstack_heads(dys * O * (sg * (1.0 + zs * (1.0 - sg))), h0, GB, dz_scr)
                    dOb = (dys * (zs * sg)).astype(BF16)
                    dP = _dot_nt(dOb, vv)
                    delta = jnp.sum(P * dP, axis=-1, keepdims=True)
                    dS = P * (dP - delta)
                    sd = psink * delta
                    for g in range(GB):
                        dsink = dsink + jnp.where(lane == h0 + g, -jnp.sum(sd[g * L:(g + 1) * L, :]), 0.0)
                    _unstack_heads(_dot(dS.astype(BF16), kk) * 0.125, h0, GB, dq_scr)
                    dbias_ref[h0:h0 + GB] += dS.reshape(GB, L, 2 * L)
                    dk_acc = dk_acc + _dot_tn(dS, qs)
                    dv_acc = dv_acc + _dot_tn(P, dOb)
                band[:, sl] = dk_acc
                band[:, 128 + kv * DH:128 + (kv + 1) * DH] = dv_acc
            dsink_ref[...] += dsink

        out = carry[...] + band[0:L, :]
        dout_ref[:, 2048:2304] = out.astype(BF16)
        carry[...] = band[L:2 * L, :]

    cur = lambda n: jnp.minimum(n, last)
    lag = lambda n: jnp.maximum(n - 1, 0)
    return pl.pallas_call(
        body, name="att_bwd", grid=(nb + 1,),
        in_specs=[pl.BlockSpec((L, 1024), lambda n: (cur(n), 0))] + _att_in_specs(nb),
        out_specs=[pl.BlockSpec((L, 2304), lambda n: (lag(n), 0)), _full((HEADS, L, 2 * L)), _full((1, 128))],
        out_shape=[jax.ShapeDtypeStruct((S, 2304), BF16),
                   jax.ShapeDtypeStruct((HEADS, L, 2 * L), F32), jax.ShapeDtypeStruct((1, 128), F32)],
        scratch_shapes=[pltpu.VMEM((L, 256), F32), pltpu.VMEM((2 * L, 256), F32),
                        pltpu.VMEM((L, 1024), F32), pltpu.VMEM((L, 1024), F32)]
        + [pltpu.VMEM((HEADS * L, DH), F32)] * 3,
        compiler_params=_params(("arbitrary",)),
    )(dy, proj, proj, proj, proj, proj, proj, bias, sinks)


SGU_CH = 4


def _sgu_in_specs():
    return [
        pl.BlockSpec((SGU_CH * L, 1024), lambda c: (c, 0)),
        pl.BlockSpec((SGU_CH * L, 1024), lambda c: (c, 1)),
        pl.BlockSpec((SGU_CH * L, 1024), lambda c: (c, 2)),
        _full((1, 1024)), _full((1, 1024)), _full((8, L, L)), _full((L, 8)),
    ]


def _sgu_norm(v, lg, lb):
    mu = jnp.mean(v, axis=-1, keepdims=True)
    vc = v - mu
    rstd = lax.rsqrt(jnp.mean(vc * vc, axis=-1, keepdims=True) + EPS)
    xhat = vc * rstd
    return xhat * lg + lb, xhat, rstd


def _tril():
    return lax.broadcasted_iota(jnp.int32, (L, L), 0) >= lax.broadcasted_iota(jnp.int32, (L, L), 1)


def _sgu_side(a, g):
    return jnp.concatenate([a[c * L:(c + 1) * L, g * 128:(g + 1) * 128] for c in range(SGU_CH)], axis=1)


def _sgu_stack(parts):
    return jnp.concatenate([jnp.concatenate([p[:, c * L:(c + 1) * L] for p in parts], axis=1)
                            for c in range(SGU_CH)], axis=0)


def sgu_fwd(proj, ln_g, ln_b, w, b_t):
    S = proj.shape[0]

    def body(u_ref, v_ref, z_ref, lg_ref, lb_ref, w_ref, bt_ref, y_ref):
        vn, _, _ = _sgu_norm(v_ref[...].astype(F32), lg_ref[...], lb_ref[...])
        tri = _tril()
        parts = []
        for g in range(8):
            wg = jnp.where(tri, w_ref[g], 0.0).astype(BF16)
            parts.append(_dot(wg, _sgu_side(vn, g).astype(BF16)) + bt_ref[:, g:g + 1])
        mixed = _sgu_stack(parts)
        z = z_ref[...].astype(F32)
        y_ref[...] = (u_ref[...].astype(F32) * mixed * (z * _sigmoid(z))).astype(BF16)

    return pl.pallas_call(
        body, name="sgu_fwd", grid=(S // (SGU_CH * L),),
        in_specs=_sgu_in_specs(),
        out_specs=pl.BlockSpec((SGU_CH * L, 1024), lambda c: (c, 0)),
        out_shape=jax.ShapeDtypeStruct((S, 1024), BF16),
        compiler_params=_params(("arbitrary",)),
    )(proj, proj, proj, ln_g, ln_b, w, b_t)


def sgu_bwd(dy, proj, ln_g, ln_b, w, b_t):
    S = proj.shape[0]

    def body(dy_ref, u_ref, v_ref, z_ref, lg_ref, lb_ref, w_ref, bt_ref,
             dout_ref, dw_ref, dbt_ref, dlg_ref, dlb_ref):
        @pl.when(pl.program_id(0) == 0)
        def _():
            dw_ref[...] = jnp.zeros_like(dw_ref)
            dbt_ref[...] = jnp.zeros_like(dbt_ref)
            dlg_ref[...] = jnp.zeros_like(dlg_ref)
            dlb_ref[...] = jnp.zeros_like(dlb_ref)

        lg = lg_ref[...]
        vn, xhat, rstd = _sgu_norm(v_ref[...].astype(F32), lg, lb_ref[...])
        tri = _tril()
        lane = lax.broadcasted_iota(jnp.int32, (L, 128), 1)
        wgs, vns, parts = [], [], []
        for g in range(8):
            wg = jnp.where(tri, w_ref[g], 0.0)
            wgs.append(wg)
            vns.append(_sgu_side(vn, g).astype(BF16))
            parts.append(_dot(wg.astype(BF16), vns[g]) + bt_ref[:, g:g + 1])
        mixed = _sgu_stack(parts)
        z = z_ref[...].astype(F32)
        sg = _sigmoid(z)
        silu = z * sg
        dy_v = dy_ref[...]
        u = u_ref[...].astype(F32)
        dout_ref[:, 0:1024] = (dy_v * mixed * silu).astype(BF16)
        dout_ref[:, 2048:3072] = (dy_v * u * mixed * (sg * (1.0 + z * (1.0 - sg)))).astype(BF16)
        dmixed = dy_v * u * silu
        dbt = jnp.zeros((L, 128), F32)
        dvn_parts = []
        for g in range(8):
            dm = _sgu_side(dmixed, g)
            dmb = dm.astype(BF16)
            dbt = dbt + jnp.where(lane == g, jnp.sum(dm, axis=1, keepdims=True), 0.0)
            dw_ref[g] += jnp.where(tri, _dot_nt(dmb, vns[g]), 0.0)
            dvn_parts.append(_dot_tn(wgs[g], dmb))
        dbt_ref[...] += dbt
        dvn = _sgu_stack(dvn_parts)
        dlg_ref[...] += jnp.sum(dvn * xhat, axis=0, keepdims=True)
        dlb_ref[...] += jnp.sum(dvn, axis=0, keepdims=True)
        dxh = dvn * lg
        dv = rstd * (dxh - jnp.mean(dxh, axis=-1, keepdims=True)
                     - xhat * jnp.mean(dxh * xhat, axis=-1, keepdims=True))
        dout_ref[:, 1024:2048] = dv.astype(BF16)

    return pl.pallas_call(
        body, name="sgu_bwd", grid=(S // (SGU_CH * L),),
        in_specs=[pl.BlockSpec((SGU_CH * L, 1024), lambda c: (c, 0))] + _sgu_in_specs(),
        out_specs=[pl.BlockSpec((SGU_CH * L, 3072), lambda c: (c, 0)), _full((8, L, L)), _full((L, 128)),
                   _full((1, 1024)), _full((1, 1024))],
        out_shape=[jax.ShapeDtypeStruct((S, 3072), BF16), jax.ShapeDtypeStruct((8, L, L), F32),
                   jax.ShapeDtypeStruct((L, 128), F32), jax.ShapeDtypeStruct((1, 1024), F32),
                   jax.ShapeDtypeStruct((1, 1024), F32)],
        compiler_params=_params(("arbitrary",)),
    )(dy, proj, proj, proj, ln_g, ln_b, w, b_t)


def _expand_matrices():
    e = (np.arange(SSM_W)[None, :] // SSM_P == np.arange(128)[:, None]).astype(np.float32)
    return jnp.asarray(e, BF16), jnp.asarray(e.T, BF16)


def _rows_from(ref, start):
    C = ref.shape[1]
    tiles = ref[...].reshape(17, 8, C)
    q, s = divmod(start, 8)
    if s == 0:
        return tiles[q:q + 16].reshape(L, C)
    rolled = pltpu.roll(tiles, 8 - s, axis=1)
    sub = lax.broadcasted_iota(jnp.int32, (16, 8, C), 1)
    return jnp.where(sub < 8 - s, rolled[q:q + 16], rolled[q + 1:q + 17]).reshape(L, C)


def _ssd_common(ext_ref, cw_ref, cb_ref, dt_raw, dtb, alog):
    taps = [_rows_from(ext_ref, 5 + k) for k in range(CONV_K)]
    pre = cb_ref[...]
    for k in range(CONV_K):
        pre = pre + cw_ref[k:k + 1, :] * taps[k]
    sg_pre = _sigmoid(pre)
    xc = pre * sg_pre
    dt = _softplus(dt_raw + dtb)
    a = -jnp.exp(alog)
    adt = dt * a
    acs = _sel_dot(_tril(), adt, 3)
    return pre, sg_pre, xc, dt, a, acs, taps


def _ssd_in_specs(rev, nc):
    cidx = (lambda c: nc - 1 - c) if rev else (lambda c: c)
    return [
        pl.BlockSpec((L, 2048), lambda c: (cidx(c), 0)),
        pl.BlockSpec((L, 1024), lambda c: (cidx(c), 2)),
        pl.BlockSpec((L, 1024), lambda c: (cidx(c), 3)),
        pl.BlockSpec((L, 1024), lambda c: (cidx(c), 4)),
        pl.BlockSpec((L, 128), lambda c: (cidx(c), 40)),
        _full((8, CONV_C)), _full((1, CONV_C)), _full((1, 128)), _full((1, 128)), _full((1, 128)),
        _full((1, SSM_W)), _full((128, SSM_W)), _full((SSM_W, 128)),
    ]


def ssd_fwd(proj, conv_w, conv_b, dt_bias, a_log, d_skip, norm_g):
    S = proj.shape[0]
    nc = S // L

    def body(z_ref, xa_ref, xb_ref, xc_ref, dt_ref, cw_ref, cb_ref, dtb_ref, alog_ref, dsk_ref, ng_ref,
             ex_ref, ext_ref, y_ref, hs_ref, H, ext, ysc):
        @pl.when(pl.program_id(0) == 0)
        def _():
            H[...] = jnp.zeros_like(H)
            ext[0:8, :] = jnp.zeros((8, CONV_C), F32)

        for k, ref in enumerate((xa_ref, xb_ref, xc_ref)):
            ext[8:8 + L, k * 1024:(k + 1) * 1024] = ref[...].astype(F32)
        pre, sg_pre, xc, dt, a, acs, _ = _ssd_common(ext, cw_ref, cb_ref, dt_ref[...].astype(F32), dtb_ref[...],
                                                     alog_ref[...])
        for k, ref in enumerate((xa_ref, xb_ref, xc_ref)):
            ext[0:8, k * 1024:(k + 1) * 1024] = ref[L - 8:L, :].astype(F32)
        xs = xc[:, 0:SSM_W]
        acs_t = acs.T
        ex = ex_ref[...]
        dt_x = _dot_sel(dt, ex, 2)
        xdt = xs * dt_x
        eacs_x = _dot_sel(jnp.exp(acs), ex, 2)
        xw = xdt * _dot_sel(jnp.exp(acs[L - 1:L, :] - acs), ex, 2)
        cd_row = jnp.exp(acs[L - 1:L, :])
        hs_ref[0] = H[...]
        tri = _tril()
        for g in range(SSM_G):
            gs = slice(g * 512, (g + 1) * 512)
            bg = xc[:, SSM_W + g * SSM_N:SSM_W + (g + 1) * SSM_N].astype(BF16)
            cg = xc[:, SSM_W + 512 + g * SSM_N:SSM_W + 512 + (g + 1) * SSM_N].astype(BF16)
            G = _dot_nt(cg, bg)
            yoff = _dot_nt(cg, H[gs, :].astype(BF16)) * eacs_x[:, gs]
            Sg = _dot_tn(xw[:, gs], bg)
            for j in range(8):
                hh = g * 8 + j
                hs = slice(hh * SSM_P, (hh + 1) * SSM_P)
                seg = acs[:, hh:hh + 1] - acs_t[hh:hh + 1, :]
                dk = jnp.where(tri, jnp.exp(seg), 0.0)
                yd = _dot((G * dk).astype(BF16), xdt[:, hs].astype(BF16))
                ysc[:, hs] = yd + yoff[:, j * SSM_P:(j + 1) * SSM_P]
                H[hs, :] = H[hs, :] * cd_row[:, hh:hh + 1] + Sg[j * SSM_P:(j + 1) * SSM_P, :]
        d_x = _dot_sel(jnp.broadcast_to(dsk_ref[...], (8, 128)), ex, 3)[0:1, :]
        Y = ysc[...] + d_x * xs
        z = z_ref[...].astype(F32)
        yz = Y * (z * _sigmoid(z))
        ng = ng_ref[...]
        for g in range(SSM_G):
            gs = slice(g * 512, (g + 1) * 512)
            t = yz[:, gs]
            rstd = lax.rsqrt(jnp.mean(t * t, axis=-1, keepdims=True) + EPS)
            y_ref[:, gs] = (t * rstd * ng[:, gs]).astype(BF16)

    return pl.pallas_call(
        body, name="ssd_fwd", grid=(nc,),
        in_specs=_ssd_in_specs(False, nc),
        out_specs=[pl.BlockSpec((L, SSM_W), lambda c: (c, 0)), pl.BlockSpec((1, SSM_W, SSM_N), lambda c: (c, 0, 0))],
        out_shape=[jax.ShapeDtypeStruct((S, SSM_W), BF16), jax.ShapeDtypeStruct((nc, SSM_W, SSM_N), F32)],
        scratch_shapes=[pltpu.VMEM((SSM_W, SSM_N), F32), pltpu.VMEM((8 + L, CONV_C), F32),
                        pltpu.VMEM((L, SSM_W), F32)],
        compiler_params=_params(("arbitrary",)),
    )(proj, proj, proj, proj, proj, conv_w, conv_b, dt_bias, a_log, d_skip, norm_g, *_expand_matrices())


def ssd_bwd(dy, proj, hstates, conv_w, conv_b, dt_bias, a_log, d_skip, norm_g):
    S = proj.shape[0]
    nc = S // L
    cidx = lambda c: nc - 1 - c

    def body(dy_ref, z_ref, xa_ref, xb_ref, xc_ref, dt_ref, cw_ref, cb_ref, dtb_ref, alog_ref, dsk_ref, ng_ref,
             ex_ref, ext_ref, pa_ref, pb_ref, pc_ref, hp_ref,
             dout_ref, dcw_ref, dcb_ref, ddtb_ref, dalog_ref, ddsk_ref, dng_ref,
             dH, ext, dext, ysc, yoffsc, dxdt, dxc, tsc, rsum, csum):
        step = pl.program_id(0)
        c = nc - 1 - step

        @pl.when(step == 0)
        def _():
            dH[...] = jnp.zeros_like(dH)
            dext[L:L + 8, :] = jnp.zeros((8, CONV_C), F32)
            rsum[...] = jnp.zeros_like(rsum)
            csum[...] = jnp.zeros_like(csum)
            for r in (dcw_ref, dcb_ref, ddtb_ref, dalog_ref, ddsk_ref, dng_ref):
                r[...] = jnp.zeros_like(r)

        for k, (ref, prev) in enumerate(((xa_ref, pa_ref), (xb_ref, pb_ref), (xc_ref, pc_ref))):
            ext[0:8, k * 1024:(k + 1) * 1024] = jnp.where(c > 0, prev[8:16, :].astype(F32), 0.0)
            ext[8:8 + L, k * 1024:(k + 1) * 1024] = ref[...].astype(F32)
        dtb = dtb_ref[...]
        dt_raw = dt_ref[...].astype(F32)
        pre, sg_pre, xc, dt, a, acs, taps = _ssd_common(ext, cw_ref, cb_ref, dt_raw, dtb, alog_ref[...])
        xs = xc[:, 0:SSM_W]
        acs_t = acs.T
        ex = ex_ref[...]
        dt_x = _dot_sel(dt, ex, 2)
        xdt = xs * dt_x
        eacs_x = _dot_sel(jnp.exp(acs), ex, 2)
        dte_x = _dot_sel(jnp.exp(acs[L - 1:L, :] - acs), ex, 2)
        xw = xdt * dte_x
        cd_row = jnp.exp(acs[L - 1:L, :])
        tri = _tril()

        Gs, Cs, Bs = [], [], []
        for g in range(SSM_G):
            gs = slice(g * 512, (g + 1) * 512)
            bg = xc[:, SSM_W + g * SSM_N:SSM_W + (g + 1) * SSM_N].astype(BF16)
            cg = xc[:, SSM_W + 512 + g * SSM_N:SSM_W + 512 + (g + 1) * SSM_N].astype(BF16)
            G = _dot_nt(cg, bg)
            Gs.append(G), Cs.append(cg), Bs.append(bg)
            yoffsc[:, gs] = _dot_nt(cg, hp_ref[0, gs, :].astype(BF16)) * eacs_x[:, gs]
            for j in range(8):
                hh = g * 8 + j
                hs = slice(hh * SSM_P, (hh + 1) * SSM_P)
                seg = acs[:, hh:hh + 1] - acs_t[hh:hh + 1, :]
                dk = jnp.where(tri, jnp.exp(seg), 0.0)
                ysc[:, hs] = _dot((G * dk).astype(BF16), xdt[:, hs].astype(BF16))
        d_x = _dot_sel(jnp.broadcast_to(dsk_ref[...], (8, 128)), ex, 3)[0:1, :]
        yoff = yoffsc[...]
        Y = ysc[...] + yoff + d_x * xs

        z = z_ref[...].astype(F32)
        sgz = _sigmoid(z)
        silu_z = z * sgz
        yz = Y * silu_z
        ng = ng_ref[...]
        dout = dy_ref[...]
        dyn = dout * ng
        dyz_parts, dng_parts = [], []
        for g in range(SSM_G):
            gs = slice(g * 512, (g + 1) * 512)
            t = yz[:, gs]
            rstd = lax.rsqrt(jnp.mean(t * t, axis=-1, keepdims=True) + EPS)
            dng_parts.append(jnp.sum(dout[:, gs] * t * rstd, axis=0, keepdims=True))
            dn = dyn[:, gs]
            dyz_parts.append(rstd * dn - t * (rstd * rstd * rstd) * jnp.mean(dn * t, axis=-1, keepdims=True))
        dng_ref[...] += jnp.concatenate(dng_parts, axis=1)
        dyz = jnp.concatenate(dyz_parts, axis=1)
        dY = dyz * silu_z
        dout_ref[:, 0:SSM_W] = (dyz * Y * (sgz * (1.0 + z * (1.0 - sgz)))).astype(BF16)

        ex_t = ext_ref[...]
        ddsk_ref[...] += _dot_sel(jnp.broadcast_to(jnp.sum(dY * xs, axis=0, keepdims=True), (8, SSM_W)), ex_t, 3)[0:1, :]

        lane = lax.broadcasted_iota(jnp.int32, (L, 128), 1)
        last_col = lax.broadcasted_iota(jnp.int32, (1, L), 1) == L - 1
        for g in range(SSM_G):
            gs = slice(g * 512, (g + 1) * 512)
            G, cg, bg = Gs[g], Cs[g], Bs[g]
            hp_g = hp_ref[0, gs, :]
            dh_g = dH[gs, :]
            dY_g = dY[:, gs]
            dZ = dY_g * eacs_x[:, gs]
            dZb = dZ.astype(BF16)
            dC = _dot(dZb, hp_g.astype(BF16))
            dh_from_off = _dot_tn(dZ, cg)
            dhb = dh_g.astype(BF16)
            Q = _dot_nt(bg, dhb)
            dB = _dot(xw[:, gs].astype(BF16), dhb)
            qd = Q * dte_x[:, gs]
            dxdt[:, gs] = qd
            tsc[:, gs] = qd * xdt[:, gs]
            dG = jnp.zeros((L, L), F32)
            for j in range(8):
                hh = g * 8 + j
                hs = slice(hh * SSM_P, (hh + 1) * SSM_P)
                seg = acs[:, hh:hh + 1] - acs_t[hh:hh + 1, :]
                dk = jnp.where(tri, jnp.exp(seg), 0.0)
                M = G * dk
                dYh = dY[:, hs]
                dYhb = dYh.astype(BF16)
                dM = _dot_nt(dYhb, xdt[:, hs].astype(BF16))
                dxdt[:, hs] += _dot_tn(M, dYhb)
                dG = dG + dM * dk
                Wm = dM * M
                pj = slice(j * SSM_P, (j + 1) * SSM_P)
                cd_h = cd_row[:, hh:hh + 1]
                dcd = jnp.sum(dh_g[pj, :] * hp_g[pj, :]) * cd_h
                rsum[:, hh:hh + 1] = jnp.sum(Wm, axis=1, keepdims=True)
                csum[hh:hh + 1, :] = jnp.sum(Wm, axis=0, keepdims=True) - jnp.where(last_col, dcd, 0.0)
                dH[hs, :] = dh_g[pj, :] * cd_h + dh_from_off[pj, :]
            dGb = dG.astype(BF16)
            dC = dC + _dot(dGb, bg)
            dB = dB + _dot_tn(dG, cg)
            dxc[:, SSM_W + g * SSM_N:SSM_W + (g + 1) * SSM_N] = dB
            dxc[:, SSM_W + 512 + g * SSM_N:SSM_W + 512 + (g + 1) * SSM_N] = dC

        row = lax.broadcasted_iota(jnp.int32, (L, 128), 0)
        tv = tsc[...]
        t_last = _dot_sel(jnp.broadcast_to(jnp.sum(tv, axis=0, keepdims=True), (8, SSM_W)), ex_t, 3)[0:1, :]
        dacs = (rsum[...] - csum[...].T + _dot_sel(dY * yoff - tv, ex_t, 2) + jnp.where(row == L - 1, t_last, 0.0))
        triu = lax.broadcasted_iota(jnp.int32, (L, L), 0) <= lax.broadcasted_iota(jnp.int32, (L, L), 1)
        dadt = _sel_dot(triu, dacs, 3)
        dxdt_v = dxdt[...]
        ddt = _dot_sel(dxdt_v * xs, ex_t, 1) + dadt * a
        dalog_ref[...] += jnp.sum(dadt * dt * a, axis=0, keepdims=True)
        ddt_raw = jnp.where(lane < SSM_H, ddt * _sigmoid(dt_raw + dtb), 0.0)
        ddtb_ref[...] += jnp.sum(ddt_raw, axis=0, keepdims=True)
        dout_ref[:, 5120:5248] = ddt_raw.astype(BF16)
        dout_ref[:, 5248:5376] = jnp.zeros((L, 128), BF16)

        dxc[:, 0:SSM_W] = dxdt_v * dt_x + d_x * dY
        dpre = dxc[...] * (sg_pre * (1.0 + pre * (1.0 - sg_pre)))
        dcb_ref[...] += jnp.sum(dpre, axis=0, keepdims=True)
        dext[0:L, :] = dpre
        x_cur = ext[8:8 + L, :]
        dx = None
        for k in range(CONV_K):
            dsh = _rows_from(dext, 3 - k)
            term = cw_ref[k:k + 1, :] * dsh
            dx = term if dx is None else dx + term
            dcw_ref[k:k + 1, :] += jnp.sum(dsh * x_cur, axis=0, keepdims=True)
        dout_ref[:, SSM_W:SSM_W + CONV_C] = dx.astype(BF16)
        dext[L:L + 8, :] = dpre[0:8, :]

    big = lambda w: pl.BlockSpec((L, w), lambda c: (cidx(c), 0))
    return pl.pallas_call(
        body, name="ssd_bwd", grid=(nc,),
        in_specs=[big(SSM_W)] + _ssd_in_specs(True, nc) + [
            pl.BlockSpec((16, 1024), lambda c, k=k: (jnp.maximum(8 * cidx(c) - 1, 0), k)) for k in (2, 3, 4)] + [
            pl.BlockSpec((1, SSM_W, SSM_N), lambda c: (cidx(c), 0, 0))],
        out_specs=[big(5376), _full((8, CONV_C)), _full((1, CONV_C)),
                   _full((1, 128)), _full((1, 128)), _full((1, 128)), _full((1, SSM_W))],
        out_shape=[jax.ShapeDtypeStruct((S, 5376), BF16), jax.ShapeDtypeStruct((8, CONV_C), F32),
                   jax.ShapeDtypeStruct((1, CONV_C), F32), jax.ShapeDtypeStruct((1, 128), F32),
                   jax.ShapeDtypeStruct((1, 128), F32), jax.ShapeDtypeStruct((1, 128), F32),
                   jax.ShapeDtypeStruct((1, SSM_W), F32)],
        scratch_shapes=[pltpu.VMEM((SSM_W, SSM_N), F32), pltpu.VMEM((8 + L, CONV_C), F32),
                        pltpu.VMEM((L + 8, CONV_C), F32), pltpu.VMEM((L, SSM_W), F32),
                        pltpu.VMEM((L, SSM_W), F32), pltpu.VMEM((L, SSM_W), F32),
                        pltpu.VMEM((L, CONV_C), F32), pltpu.VMEM((L, SSM_W), F32),
                        pltpu.VMEM((L, 128), F32), pltpu.VMEM((128, L), F32)],
        compiler_params=_params(("arbitrary",)),
    )(dy, proj, proj, proj, proj, proj, conv_w, conv_b, dt_bias, a_log, d_skip, norm_g, *_expand_matrices(),
      proj, proj, proj, hstates)


def _resident(shape):
    nd = len(shape)
    return pl.BlockSpec(shape, lambda *_: (0,) * nd, pipeline_mode=pl.Buffered(1))


def merge_fwd(y_att, y_sg, y_ssm, proj, x, w_a, w_s, w_m, w_o, g_post):
    S = x.shape[0]
    tm = 256

    def body(ya_ref, ys_ref, ym_ref, gate_ref, x_ref, wa_ref, ws_ref, wm_ref, wo_ref, gp_ref,
             xn_ref, bra_ref, brs_ref, brm_ref, mg_ref, out_ref):
        bra = _dot(ya_ref[...], wa_ref[...])
        brs = _dot(ys_ref[...], ws_ref[...])
        brm = _dot(ym_ref[...], wm_ref[...])
        bra_ref[...] = bra.astype(BF16)
        brs_ref[...] = brs.astype(BF16)
        brm_ref[...] = brm.astype(BF16)
        gate = gate_ref[...].astype(F32)
        merged = (_sigmoid(gate[:, 0:1024]) * bra + _sigmoid(gate[:, 1024:2048]) * brs
                  + _sigmoid(gate[:, 2048:3072]) * brm)
        mb = merged.astype(BF16)
        mg_ref[...] = mb
        o = _dot(mb, wo_ref[...])
        out_ref[...] = o
        r = lax.rsqrt(jnp.mean(o * o, axis=-1, keepdims=True) + EPS)
        xn_ref[...] = x_ref[...] + o * r * gp_ref[...]

    row = lambda w: pl.BlockSpec((tm, w), lambda i: (i, 0))
    return pl.pallas_call(
        body, name="merge_fwd", grid=(S // tm,),
        in_specs=[row(1024), row(1024), row(2048), pl.BlockSpec((tm, 3072), lambda i: (i, 0)),
                  row(D), _resident((1024, D)), _resident((1024, D)), _resident((2048, D)), _resident((D, D)),
                  _full((1, D))],
        out_specs=[row(D)] * 6,
        out_shape=[jax.ShapeDtypeStruct((S, D), F32)] + [jax.ShapeDtypeStruct((S, D), BF16)] * 4
        + [jax.ShapeDtypeStruct((S, D), F32)],
        compiler_params=_params(("arbitrary",)),
    )(y_att, y_sg, y_ssm, proj, x, w_a, w_s, w_m, w_o, g_post)


def merge_bwd(dy, out, g_post, proj, br_a, br_s, br_m, w_a, w_s, w_m, w_o):
    S = dy.shape[0]
    tm = 256

    def body(dy_ref, o_ref, gp_ref, gate_ref, bra_ref, brs_ref, brm_ref, wa_ref, ws_ref, wm_ref, wo_ref,
             dout_ref, dba_ref, dbs_ref, dbm_ref, dgate_ref, dya_ref, dys_ref, dym_ref, dgp_ref):
        @pl.when(pl.program_id(0) == 0)
        def _():
            dgp_ref[...] = jnp.zeros_like(dgp_ref)

        o = o_ref[...]
        dyv = dy_ref[...]
        r = lax.rsqrt(jnp.mean(o * o, axis=-1, keepdims=True) + EPS)
        dyg = dyv * gp_ref[...]
        do = r * dyg - o * (r * r * r) * jnp.mean(dyg * o, axis=-1, keepdims=True)
        dgp_ref[...] += jnp.sum(dyv * o * r, axis=0, keepdims=True)
        dob = do.astype(BF16)
        dout_ref[...] = dob
        dmerged = _dot_nt(dob, wo_ref[...])
        for idx, (br_ref, dbr_ref, w_ref, dyi_ref) in enumerate((
                (bra_ref, dba_ref, wa_ref, dya_ref), (brs_ref, dbs_ref, ws_ref, dys_ref),
                (brm_ref, dbm_ref, wm_ref, dym_ref))):
            s = _sigmoid(gate_ref[:, idx * 1024:(idx + 1) * 1024].astype(F32))
            dbr = (dmerged * s).astype(BF16)
            dbr_ref[...] = dbr
            dgate_ref[:, idx * 1024:(idx + 1) * 1024] = (dmerged * br_ref[...].astype(F32) * s * (1.0 - s)).astype(BF16)
            dyi_ref[...] = _dot_nt(dbr, w_ref[...])

    row = lambda w: pl.BlockSpec((tm, w), lambda i: (i, 0))
    return pl.pallas_call(
        body, name="merge_bwd", grid=(S // tm,),
        in_specs=[row(D), row(D), _full((1, D)), pl.BlockSpec((tm, 3072), lambda i: (i, 0)),
                  row(D), row(D), row(D),
                  _resident((1024, D)), _resident((1024, D)), _resident((2048, D)), _resident((D, D))],
        out_specs=[row(D), row(D), row(D), row(D), row(3072), row(1024), row(1024), row(2048), _full((1, D))],
        out_shape=[jax.ShapeDtypeStruct((S, D), BF16)] * 4 + [
            jax.ShapeDtypeStruct((S, 3072), BF16), jax.ShapeDtypeStruct((S, 1024), F32),
            jax.ShapeDtypeStruct((S, 1024), F32), jax.ShapeDtypeStruct((S, 2048), F32),
            jax.ShapeDtypeStruct((1, D), F32)],
        compiler_params=_params(("arbitrary",)),
    )(dy, out, g_post, proj, br_a, br_s, br_m, w_a, w_s, w_m, w_o)


def loss_head(y, target):
    S = y.shape[0]
    tm = 512

    def body(y_ref, t_ref, dy_ref, loss_ref):
        @pl.when(pl.program_id(0) == 0)
        def _():
            loss_ref[...] = jnp.zeros_like(loss_ref)
        e = y_ref[...] - t_ref[...]
        dy_ref[...] = e * (1.0 / D)
        loss_ref[...] += 0.5 * jnp.sum(jnp.mean(e * e, axis=-1, keepdims=True))

    row = pl.BlockSpec((tm, D), lambda i: (i, 0))
    return pl.pallas_call(
        body, name="loss_head", grid=(S // tm,),
        in_specs=[row, row], out_specs=[row, _full((1, 128))],
        out_shape=[jax.ShapeDtypeStruct((S, D), F32), jax.ShapeDtypeStruct((1, 128), F32)],
        compiler_params=_params(("arbitrary",)),
    )(y, target)


def _adam(w, g, m, v):
    mn = ADAM_B1 * m + (1.0 - ADAM_B1) * g
    vn = ADAM_B2 * v + (1.0 - ADAM_B2) * (g * g)
    m_hat = mn / (1.0 - ADAM_B1 ** ADAM_STEP)
    v_hat = vn / (1.0 - ADAM_B2 ** ADAM_STEP)
    return -ADAM_LR * (m_hat / (jnp.sqrt(v_hat) + ADAM_EPS) + ADAM_WD * w), mn, vn


def adamw_big(w, m, v, halves0, sum1, cc, name, tr):
    _, R, C = w.shape
    nper = R // tr
    f, fb, n0, off_a, off_b = halves0
    p, pb, off1 = sum1

    def body(c_ref, w_ref, m_ref, v_ref, f_ref, fb_ref, p_ref, pb_ref, g_ref, d_ref, nm_ref, nv_ref):
        i = pl.program_id(0)
        half = jnp.where(i % nper >= n0, 1, 0)
        g0 = jnp.where(c_ref[0] == half, f_ref[...], fb_ref[...])
        g = jnp.where(i < nper, g0, p_ref[...] + pb_ref[...])
        g_ref[0] = g
        d_ref[0], nm_ref[0], nv_ref[0] = _adam(w_ref[0], g, m_ref[0], v_ref[0])

    def blk0(i, c):
        il = jnp.minimum(i, nper - 1)
        return (jnp.where(il >= n0, off_b + il - n0, off_a + il), 0)

    wblk = pl.BlockSpec((1, tr, C), lambda i, c: (i // nper, i % nper, 0))
    b0 = pl.BlockSpec((tr, C), blk0)
    b1 = pl.BlockSpec((tr, C), lambda i, c: (off1 + jnp.maximum(i - nper, 0), 0))
    grid_spec = pltpu.PrefetchScalarGridSpec(
        num_scalar_prefetch=1, grid=(2 * nper,),
        in_specs=[wblk, wblk, wblk, b0, b0, b1, b1], out_specs=[wblk] * 4)
    return pl.pallas_call(
        body, name=name, grid_spec=grid_spec,
        out_shape=[jax.ShapeDtypeStruct(w.shape, F32)] * 4,
        compiler_params=_params(("arbitrary",)),
    )(cc, w, m, v, f, fb, p, pb)


def adamw_plain(w, g, m, v, name):
    def body(w_ref, g_ref, m_ref, v_ref, d_ref, nm_ref, nv_ref):
        d_ref[...], nm_ref[...], nv_ref[...] = _adam(w_ref[...], g_ref[...], m_ref[...], v_ref[...])

    return pl.pallas_call(
        body, name=name, out_shape=[jax.ShapeDtypeStruct(w.shape, F32)] * 3, compiler_params=_params(),
    )(w, g, m, v)


SMALL = {"norm_pre": ("g_pre", 8), "norm_post": ("g_post", 8), "att_sinks": ("sinks", 8), "sg_ln_g": ("ln_g", 8),
         "sg_ln_b": ("ln_b", 8), "sg_w": ("sg_w", 1024), "sg_b": ("sg_bt", 8), "ssm_conv_b": ("conv_b", 24),
         "ssm_dt_bias": ("dt_bias", 8), "ssm_a_log": ("a_log", 8), "ssm_d": ("d_skip", 8), "ssm_norm_g": ("norm_g", 16)}
SMALL_LAYER_ROWS = sum(r for _, r in SMALL.values())
REL_ROW = DEPTH * SMALL_LAYER_ROWS
LOSS_ROW = REL_ROW + 32
SMALL_ROWS = LOSS_ROW + 8


def _small_rows():
    rows, r = {}, 0
    for l in range(DEPTH):
        for name, (_, n) in SMALL.items():
            rows[(l, name)] = r
            r += n
    return rows


def adamw_small(red, rel, small):
    names = list(SMALL) + ["rel_bias"]
    params = dict(small, rel_bias=rel)
    rows = _small_rows()

    def grad_of(red_ref, l, name, n):
        r0 = rows[(l, name)]
        if name == "sg_b":
            return red_ref[r0:r0 + 8, :]
        if n < 128:
            return red_ref[r0:r0 + 1, 0:n]
        return jnp.concatenate([red_ref[r0 + j:r0 + j + 1, :] for j in range(n // 128)], axis=1)

    def body(red_ref, *refs):
        ins, outs = refs[:3 * len(names)], refs[3 * len(names):]
        for i, name in enumerate(names):
            w_ref, m_ref, v_ref = ins[3 * i:3 * i + 3]
            o = outs[4 * i:4 * i + 4]
            if name == "rel_bias":
                g = red_ref[REL_ROW:REL_ROW + 32, 0:16]
                o[0][...] = g
                o[1][...], o[2][...], o[3][...] = _adam(w_ref[...], g, m_ref[...], v_ref[...])
                continue
            for l in range(DEPTH):
                if name == "sg_w":
                    for grp in range(8):
                        r0 = rows[(l, name)] + grp * 128
                        g = red_ref[r0:r0 + 128, :]
                        o[0][l, grp] = g
                        o[1][l, grp], o[2][l, grp], o[3][l, grp] = _adam(w_ref[l, grp], g, m_ref[l, grp], v_ref[l, grp])
                elif name == "sg_b":
                    g = grad_of(red_ref, l, name, 128)
                    o[0][l] = g
                    o[1][l], o[2][l], o[3][l] = _adam(w_ref[l], g, m_ref[l], v_ref[l])
                else:
                    sl = slice(l, l + 1)
                    g = grad_of(red_ref, l, name, w_ref.shape[-1])
                    o[0][sl, :] = g
                    o[1][sl, :], o[2][sl, :], o[3][sl, :] = _adam(w_ref[sl, :], g, m_ref[sl, :], v_ref[sl, :])

    flat_in = [a for name in names for a in params[name]]
    out_shape = [jax.ShapeDtypeStruct(params[name][0].shape, F32) for name in names for _ in range(4)]
    res = pl.pallas_call(body, name="adamw_small", out_shape=out_shape, compiler_params=_params())(red, *flat_in)
    return {name: tuple(res[4 * i:4 * i + 4]) for i, name in enumerate(names)}


ANY = pl.BlockSpec(memory_space=pl.ANY)


def _place():
    x, y, c = lax.axis_index("x"), lax.axis_index("y"), lax.axis_index("c")
    others = [(1 - x, y), (x, 1 - y), (1 - x, 1 - y)]
    return x, y, c, others


def _rcopy(src, dst, ssem, rsem, to):
    return pltpu.make_async_remote_copy(src_ref=src, dst_ref=dst, send_sem=ssem, recv_sem=rsem,
                                        device_id=to, device_id_type=MESH)


def gather_weights(arrs):
    n = len(arrs)

    def body(*refs):
        srcs, outs, ssem, rsem = refs[:n], refs[n:2 * n], refs[2 * n], refs[2 * n + 1]
        x, y, c, others = _place()
        me = 2 * x + y
        sib = (x, y, 1 - c)
        first = [_rcopy(srcs[i].at[c], outs[i].at[c, me], ssem.at[6 * i + k], rsem.at[6 * i + k], (ox, oy, c))
                 for i in range(n) for k, (ox, oy) in enumerate(others)]
        for cp in first:
            cp.start()
        passed = []
        for k, (ox, oy) in enumerate(others):
            for i in range(n):
                slot = outs[i].at[c, 2 * ox + oy]
                _rcopy(slot, slot, ssem.at[6 * i + k], rsem.at[6 * i + k], sib).wait_recv()
                fw = _rcopy(slot, slot, ssem.at[6 * i + 3 + k], rsem.at[6 * i + 3 + k], sib)
                fw.start()
                passed.append(fw)
        for k, (ox, oy) in enumerate(others):
            for i in range(n):
                slot = outs[i].at[1 - c, 2 * ox + oy]
                _rcopy(slot, slot, ssem.at[6 * i + 3 + k], rsem.at[6 * i + 3 + k], sib).wait_recv()
        for cp in first + passed:
            cp.wait_send()

    return pl.pallas_call(
        body, name="gather_weights",
        in_specs=[ANY] * n, out_specs=[ANY] * n,
        out_shape=[jax.ShapeDtypeStruct((2, SHARDS) + a.shape[1:], a.dtype) for a in arrs],
        scratch_shapes=[pltpu.SemaphoreType.DMA((6 * n,)), pltpu.SemaphoreType.DMA((6 * n,))],
    )(*arrs)


HBM = pl.BlockSpec(memory_space=pltpu.HBM)
SEM = pl.BlockSpec(memory_space=pltpu.SEMAPHORE)
EFFECT = pltpu.SideEffectType.DATAFLOW_SIDE_EFFECTING


def _in_hbm(a):
    return pltpu.with_memory_space_constraint(a, pltpu.HBM)


def gather_start(srcs, after, name, by_dest=False):
    n = len(srcs)
    lands = [_in_hbm(lax.empty((SHARDS,) + a.shape[-2:], a.dtype)) for a in srcs]
    na = len(after)

    def body(*refs):
        src, land = refs[:n], refs[n:2 * n]
        ssem, rsem, token = refs[2 * n + na], refs[2 * n + na + 1], refs[-1]
        x, y, c, others = _place()
        me = 2 * x + y
        for i in range(n):
            for k, (ox, oy) in enumerate(others):
                s = src[i].at[2 * ox + oy] if by_dest else src[i]
                _rcopy(s, land[i].at[me], ssem.at[3 * i + k], rsem.at[3 * i + k], (ox, oy, c)).start()
        token[...] = jnp.zeros_like(token)

    bufs = [_in_hbm(a) for a in srcs] + lands
    out = pl.pallas_call(
        body, name=name,
        out_shape=(pltpu.SemaphoreType.DMA((3 * n,)), pltpu.SemaphoreType.DMA((3 * n,)),
                   *[pltpu.HBM(b.shape, b.dtype) for b in bufs], jax.ShapeDtypeStruct((8, 128), F32)),
        in_specs=[HBM] * (2 * n) + [ANY] * na,
        out_specs=(SEM, SEM, *[HBM] * (2 * n), pl.BlockSpec(memory_space=pltpu.VMEM)),
        input_output_aliases={i: 2 + i for i in range(2 * n)},
        compiler_params=pltpu.CompilerParams(has_side_effects=EFFECT),
    )(*bufs, *after)
    return out[0], out[1], list(out[2:2 + n]), list(out[2 + n:2 + 2 * n]), out[-1]


def gather_wait(ssem, rsem, srcs, lands, after, name, by_dest=False):
    n = len(srcs)

    def body(*refs):
        src, land = refs[:n], refs[n:2 * n]
        s_sem, r_sem = refs[2 * n], refs[2 * n + 1]
        x, y, c, others = _place()
        for i in range(n):
            for k, (ox, oy) in enumerate(others):
                s = src[i].at[2 * ox + oy] if by_dest else src[i]
                cp = _rcopy(s, land[i].at[2 * ox + oy], s_sem.at[3 * i + k], r_sem.at[3 * i + k], (ox, oy, c))
                cp.wait_send()
                cp.wait_recv()

    bufs = list(srcs) + list(lands)
    out = pl.pallas_call(
        body, name=name,
        out_shape=tuple(pltpu.HBM(b.shape, b.dtype) for b in bufs),
        in_specs=[HBM] * (2 * n) + [SEM, SEM, ANY],
        out_specs=tuple([HBM] * (2 * n)),
        input_output_aliases={i: i for i in range(2 * n)},
        compiler_params=pltpu.CompilerParams(has_side_effects=EFFECT),
    )(*bufs, ssem, rsem, after)
    return list(out[n:2 * n])


def grad_sibling_exchange(arrs):
    n = len(arrs)

    def body(*refs):
        srcs, outs, ssem, rsem = refs[:n], refs[n:2 * n], refs[2 * n], refs[2 * n + 1]
        x, y, c, _ = _place()
        cps = [_rcopy(srcs[i].at[1 - c], outs[i], ssem.at[i], rsem.at[i], (x, y, 1 - c)) for i in range(n)]
        for cp in cps:
            cp.start()
        for cp in cps:
            cp.wait()

    return pl.pallas_call(
        body, name="grad_sibling_exchange",
        in_specs=[ANY] * n, out_specs=[ANY] * n,
        out_shape=[jax.ShapeDtypeStruct(a.shape[1:], F32) for a in arrs],
        scratch_shapes=[pltpu.SemaphoreType.DMA((n,)), pltpu.SemaphoreType.DMA((n,))],
    )(*arrs)


def grad_chip_sum(g, sb, cc, tr, name):
    _, _, R, C = g.shape
    blk = pl.BlockSpec((1, tr, C), lambda s, r, c: (s, r, 0))
    grid_spec = pltpu.PrefetchScalarGridSpec(
        num_scalar_prefetch=1, grid=(SHARDS, R // tr),
        in_specs=[pl.BlockSpec((1, 1, tr, C), lambda s, r, c: (c[0], s, r, 0)), blk],
        out_specs=[blk, blk])

    def body(c_ref, a_ref, b_ref, o_ref, ob_ref):
        t = a_ref[0] + b_ref[...]
        o_ref[...] = t
        ob_ref[...] = t.astype(BF16)

    return pl.pallas_call(
        body, name=name, grid_spec=grid_spec,
        out_shape=[jax.ShapeDtypeStruct((SHARDS, R, C), F32), jax.ShapeDtypeStruct((SHARDS, R, C), BF16)],
        compiler_params=_params(("arbitrary", "arbitrary")),
    )(cc, g, sb)


def grad_shard_sum(t, rb, me, tr, name):
    _, R, C = t.shape
    grid_spec = pltpu.PrefetchScalarGridSpec(
        num_scalar_prefetch=1, grid=(R // tr,),
        in_specs=[pl.BlockSpec((1, tr, C), lambda r, m: (m[0], r, 0)),
                  pl.BlockSpec((SHARDS, tr, C), lambda r, m: (0, r, 0))],
        out_specs=pl.BlockSpec((tr, C), lambda r, m: (r, 0)))

    def body(m_ref, t_ref, r_ref, o_ref):
        part = [jnp.where(m_ref[0] == s, t_ref[0], r_ref[s].astype(F32)) for s in range(SHARDS)]
        o_ref[...] = ((part[0] + part[1]) + part[2]) + part[3]

    return pl.pallas_call(
        body, name=name, grid_spec=grid_spec,
        out_shape=jax.ShapeDtypeStruct((R, C), F32),
        compiler_params=_params(("arbitrary",)),
    )(me, t, rb)


def grad_sibling_share(arrs, name):
    n = len(arrs)

    def body(*refs):
        srcs, outs, ssem, rsem = refs[:n], refs[n:2 * n], refs[2 * n], refs[2 * n + 1]
        x, y, c, _ = _place()
        cps = [_rcopy(srcs[i], outs[i], ssem.at[i], rsem.at[i], (x, y, 1 - c)) for i in range(n)]
        for cp in cps:
            cp.start()
        for cp in cps:
            cp.wait()

    return pl.pallas_call(
        body, name=name,
        in_specs=[ANY] * n, out_specs=[ANY] * n,
        out_shape=[jax.ShapeDtypeStruct(a.shape, F32) for a in arrs],
        scratch_shapes=[pltpu.SemaphoreType.DMA((n,)), pltpu.SemaphoreType.DMA((n,))],
    )(*arrs)


def _allreduce_rows(src, sib_buf, chips, out_ref, ssem, rsem):
    x, y, c, others = _place()
    me = 2 * x + y
    cp = _rcopy(src, sib_buf, ssem.at[0], rsem.at[0], (x, y, 1 - c))
    cp.start()
    cp.wait()
    chips[me] = src[...] + sib_buf[...]
    sends = [_rcopy(chips.at[me], chips.at[me], ssem.at[1 + k], rsem.at[1 + k], (ox, oy, c))
             for k, (ox, oy) in enumerate(others)]
    for s in sends:
        s.start()
    for k, (ox, oy) in enumerate(others):
        slot = chips.at[2 * ox + oy]
        _rcopy(slot, slot, ssem.at[1 + k], rsem.at[1 + k], (ox, oy, c)).wait_recv()
    for s in sends:
        s.wait_send()
    out_ref[...] = ((chips[0] + chips[1]) + chips[2]) + chips[3]


def _allreduce_scratch(rows):
    return [pltpu.VMEM((rows, 128), F32), pltpu.VMEM((SHARDS, rows, 128), F32),
            pltpu.SemaphoreType.DMA((4,)), pltpu.SemaphoreType.DMA((4,))]


def allreduce_rows(buf, name):
    rows = buf.shape[0]
    VM = pl.BlockSpec(memory_space=pltpu.VMEM)

    def body(src_ref, out_ref, sib_buf, chips, ssem, rsem):
        _allreduce_rows(src_ref, sib_buf, chips, out_ref, ssem, rsem)

    return pl.pallas_call(
        body, name=name, in_specs=[VM], out_specs=VM,
        out_shape=jax.ShapeDtypeStruct((rows, 128), F32),
        scratch_shapes=_allreduce_scratch(rows), compiler_params=_params(),
    )(buf)


def small_allreduce(grads, rel, loss_part):
    rows = _small_rows()
    keys = [(l, name) for l in range(DEPTH) for name in SMALL]
    flat = [grads[l][SMALL[name][0]] for l, name in keys] + [rel, loss_part]

    def body(*refs):
        ins = refs[:len(flat)]
        out_ref, src, sib_buf, chips, ssem, rsem = refs[len(flat):]
        src[...] = jnp.zeros_like(src)
        for (l, name), ref in zip(keys, ins):
            r0 = rows[(l, name)]
            if name == "sg_w":
                for grp in range(8):
                    src[r0 + grp * 128:r0 + (grp + 1) * 128, :] = ref[grp]
            elif name == "sg_b":
                src[r0:r0 + 8, :] = ref[...].T[0:8, :]
            else:
                for j in range(ref.shape[1] // 128):
                    src[r0 + j:r0 + j + 1, :] = ref[:, j * 128:(j + 1) * 128]
        src[REL_ROW:REL_ROW + 32, 0:16] = ins[-2][...]
        src[LOSS_ROW:LOSS_ROW + 1, :] = ins[-1][...]
        _allreduce_rows(src, sib_buf, chips, out_ref, ssem, rsem)

    return pl.pallas_call(
        body, name="small_allreduce",
        out_shape=jax.ShapeDtypeStruct((SMALL_ROWS, 128), F32),
        scratch_shapes=[pltpu.VMEM((SMALL_ROWS, 128), F32)] + _allreduce_scratch(SMALL_ROWS),
        compiler_params=_params(),
    )(*flat)


def _pad_lanes(v):
    return jnp.zeros((1, 128), F32).at[0, :v.shape[0]].set(v)


def layer_fwd(x, wts, bias):
    wt = wts["wt"]
    tn = {name: t for name, _, t in GROUPS}
    p_gate, h = inproj_first(x, wts["g_pre"], wt["gate"], tn["gate"], "inproj_gate")
    p_sgu, p_att, p_ssd = (inproj_group(h, wt[n], tn[n], "inproj_" + n, F32 if n == "att" else BF16)
                           for n in ("sgu", "att", "ssd"))
    y_att = att_fwd(p_att, bias, wts["sinks"])
    y_sg = sgu_fwd(p_sgu, wts["ln_g"], wts["ln_b"], wts["sg_w"], wts["sg_bt"])
    y_ssm, hst = ssd_fwd(p_ssd, wts["conv_w"], wts["conv_b"], wts["dt_bias"], wts["a_log"], wts["d_skip"],
                         wts["norm_g"])
    x_new, br_a, br_s, br_m, merged, out = merge_fwd(
        y_att, y_sg, y_ssm, p_gate, x, wts["w_a"], wts["w_s"], wts["w_m"], wts["w_o"], wts["g_post"])
    saved = dict(x=x, p_gate=p_gate, p_sgu=p_sgu, p_att=p_att, p_ssd=p_ssd, h=h,
                 y_att=y_att, y_sg=y_sg, y_ssm=y_ssm, hst=hst,
                 br_a=br_a, br_s=br_s, br_m=br_m, merged=merged, out=out)
    return x_new, saved


def layer_bwd(dy, wts, bias, sv):
    dps, grads = layer_bwd_params(dy, wts, bias, sv)
    dx, grads["g_pre"] = layer_bwd_input(dy, dps, wts, sv, wts["g_pre"])
    return dx, grads


def layer_bwd_input(dy, dps, wts, sv, g_pre):
    wt = wts["wt"]
    tn = {name: t for name, _, t in GROUPS}
    acc = None
    for n in ("gate", "sgu", "ssd"):
        acc = dh_group(dps[n], wt[n], acc, DH_TILE[n], "dh_" + n)
    return dh_last(dps["att"], wt["att"], acc, sv["x"], g_pre, dy, tn["att"], "dh_att")


def layer_bwd_params(dy, wts, bias, sv):
    dout, dba, dbs, dbm, d_gate, dya, dys, dym, dg_post = merge_bwd(
        dy, sv["out"], wts["g_post"], sv["p_gate"], sv["br_a"], sv["br_s"], sv["br_m"],
        wts["w_a"], wts["w_s"], wts["w_m"], wts["w_o"])
    d_att, dbias, dsinks = att_bwd(dya, sv["p_att"], bias, wts["sinks"])
    d_sgu, dsg_w, dsg_bt, dln_g, dln_b = sgu_bwd(dys, sv["p_sgu"], wts["ln_g"], wts["ln_b"], wts["sg_w"],
                                                 wts["sg_bt"])
    d_ssd, dcw, dcb, ddtb, dalog, ddsk, dng = ssd_bwd(
        dym, sv["p_ssd"], sv["hst"], wts["conv_w"], wts["conv_b"], wts["dt_bias"], wts["a_log"], wts["d_skip"],
        wts["norm_g"])
    dps = dict(gate=d_gate, sgu=d_sgu, att=d_att, ssd=d_ssd)
    tn = {name: t for name, _, t in GROUPS}
    grads = dict(
        w_in={n: dw_group(dps[n], sv["h"], tn[n], "dw_in_" + n) for n in dps},
        w_a=matmul_tn(sv["y_att"], dba, "dw_att"),
        w_s=matmul_tn(sv["y_sg"], dbs, "dw_sg"),
        w_m=matmul_tn(sv["y_ssm"], dbm, "dw_ssm"),
        w_o=matmul_tn(sv["merged"], dout, "dw_out"),
        g_post=dg_post, sinks=dsinks, ln_g=dln_g, ln_b=dln_b, sg_w=dsg_w, sg_bt=dsg_bt,
        conv_w=dcw, conv_b=dcb, dt_bias=ddtb, a_log=dalog, d_skip=ddsk, norm_g=dng, bias=dbias)
    return dps, grads


REST_OFF = (0, 256, 512, 1024, 1280)
GR_ROWS = 1536
GR_CONV = 1280
W_IN_SPLIT = 1600
W_IN_HALF = 1824


def kernel(x, w_in, norm_pre, norm_post, rel_bias, att_sinks, sg_ln_g, sg_ln_b, sg_w, sg_b, ssm_conv_w, ssm_conv_b, ssm_dt_bias, ssm_a_log, ssm_d, ssm_norm_g, w_br_att, w_br_sg, w_br_ssm, w_out, loss_target, m_w_in, m_norm_pre, m_norm_post, m_rel_bias, m_att_sinks, m_sg_ln_g, m_sg_ln_b, m_sg_w, m_sg_b, m_ssm_conv_w, m_ssm_conv_b, m_ssm_dt_bias, m_ssm_a_log, m_ssm_d, m_ssm_norm_g, m_w_br_att, m_w_br_sg, m_w_br_ssm, m_w_out, v_w_in, v_norm_pre, v_norm_post, v_rel_bias, v_att_sinks, v_sg_ln_g, v_sg_ln_b, v_sg_w, v_sg_b, v_ssm_conv_w, v_ssm_conv_b, v_ssm_dt_bias, v_ssm_a_log, v_ssm_d, v_ssm_norm_g, v_w_br_att, v_w_br_sg, v_w_br_ssm, v_w_out):
    cx, cy, cc = lax.axis_index("x"), lax.axis_index("y"), lax.axis_index("c")
    me = 2 * cx + cy
    xs = x[0]
    S = xs.shape[0]

    tr = lambda a: jnp.transpose(a, (0, 2, 1))
    w_in_b = tr(w_in).astype(BF16)
    w_rest_b = jnp.concatenate([w_br_att, w_br_sg, w_br_ssm, w_out], axis=1).astype(BF16)
    halves = lambda a: a.reshape(2, a.shape[0] // 2, a.shape[1])
    w_in0 = jnp.pad(w_in_b[0], ((0, W_IN_ROWS - 3400), (0, 0)))
    all0_in, all0_rest = gather_weights([halves(w_in0), halves(w_rest_b[0])])
    convw_slot = jnp.zeros((SHARDS, DEPTH * CONV_K * 768 // 128, 128), F32)
    convw_slot = lax.dynamic_update_index_in_dim(
        convw_slot, jnp.where(cc == 0, 1.0, 0.0) * ssm_conv_w.reshape(-1, 128), me, 0)
    convw_rows = allreduce_rows(convw_slot.reshape(-1, 128), "gather_conv_w")
    convw_all = convw_rows.reshape(SHARDS, DEPTH, CONV_K, 768).transpose(1, 2, 0, 3).reshape(DEPTH, CONV_K, CONV_C)
    g1_ssem, g1_rsem, g1_srcs, g1_lands, g1_token = gather_start(
        [w_in_b[1], w_rest_b[1]], [convw_rows, all0_rest], "gather_l1_start")

    o = REST_OFF

    def layer_weights(l, gathered_in, gathered_rest, g_pre):
        sh_in = [jnp.where(me == s, w_in_b[l], gathered_in[s]) for s in range(SHARDS)]
        sh_rest = [jnp.where(me == s, w_rest_b[l], gathered_rest[s]) for s in range(SHARDS)]
        rest = lambda k: jnp.concatenate([r[o[k]:o[k + 1]] for r in sh_rest], axis=0)
        return dict(
            wt=group_weights(jnp.concatenate(sh_in, axis=0)),
            w_a=rest(0), w_s=rest(1), w_m=rest(2), w_o=rest(3),
            g_pre=g_pre, g_post=norm_post[l][None], sinks=att_sinks[l],
            ln_g=sg_ln_g[l][None], ln_b=sg_ln_b[l][None], sg_w=sg_w[l],
            sg_bt=sg_b[l].T,
            conv_w=jnp.concatenate([convw_all[l], jnp.zeros((4, CONV_C), F32)], axis=0),
            conv_b=ssm_conv_b[l][None], dt_bias=_pad_lanes(ssm_dt_bias[l]), a_log=_pad_lanes(ssm_a_log[l]),
            d_skip=_pad_lanes(ssm_d[l]), norm_g=ssm_norm_g[l][None])

    bias = bias_table(rel_bias)
    layers = [layer_weights(0, [all0_in[:, s].reshape(W_IN_ROWS, D)[0:3400] for s in range(SHARDS)],
                            [all0_rest[:, s].reshape(1280, D) for s in range(SHARDS)],
                            (norm_pre[0] + g1_token[0, 0])[None])]
    act, sv0 = layer_fwd(xs, layers[0], bias)
    land_in, land_rest = gather_wait(g1_ssem, g1_rsem, g1_srcs, g1_lands, act, "gather_l1_wait")
    layers.append(layer_weights(1, land_in, land_rest, norm_pre[1][None]))
    act, sv1 = layer_fwd(act, layers[1], bias)
    saved = [sv0, sv1]
    dy, loss_part = loss_head(act, loss_target[0])
    cvec = jnp.reshape(cc, (1,)).astype(jnp.int32)
    mvec = jnp.reshape(me, (1,)).astype(jnp.int32)

    def by_shard(g):
        gcw = g["conv_w"][0:CONV_K].reshape(CONV_K, SHARDS, 768).transpose(1, 0, 2).reshape(SHARDS, 3, 1024)
        rest = jnp.concatenate([
            g["w_a"].reshape(SHARDS, 256, D), g["w_s"].reshape(SHARDS, 256, D), g["w_o"].reshape(SHARDS, 256, D),
            g["w_m"].reshape(SHARDS, 512, D), jnp.pad(gcw, ((0, 0), (0, GR_ROWS - GR_CONV - 3), (0, 0)))], axis=1)
        return ungroup_grads(g["w_in"]).reshape(SHARDS, 3400, D), rest

    grads = [None] * DEPTH
    dy, grads[1] = layer_bwd(dy, layers[1], bias, saved[1])
    g1_in, g1_rest = by_shard(grads[1])
    g1_in = jnp.pad(g1_in, ((0, 0), (0, W_IN_ROWS - 3400), (0, 0)))
    x1_ssem, x1_rsem, x1_srcs, x1_lands, x1_token = gather_start(
        [g1_in.astype(BF16), g1_rest.astype(BF16)], [], "grads_l1_start", by_dest=True)
    wts0 = dict(layers[0], g_post=layers[0]["g_post"] + x1_token[0, 0])
    dps0, grads[0] = layer_bwd_params(dy, wts0, bias, saved[0])
    r1_in, r1_rest = gather_wait(x1_ssem, x1_rsem, x1_srcs, x1_lands, grads[0]["w_in"]["ssd"], "grads_l1_wait",
                                 by_dest=True)
    p_in = grad_shard_sum(g1_in, r1_in, mvec, 384, "l1_sum_w_in")
    p_rest = grad_shard_sum(g1_rest, r1_rest, mvec, 512, "l1_sum_rest")
    pb_in, pb_rest = grad_sibling_share([p_in, p_rest], "l1_sibling_share")

    g0_in, g0_rest = by_shard(grads[0])
    pad_to = lambda a, rows: jnp.pad(a, ((0, 0), (0, rows - a.shape[1]), (0, 0)))
    g0_in = jnp.stack([pad_to(g0_in[:, 0:W_IN_SPLIT], W_IN_HALF), pad_to(g0_in[:, W_IN_SPLIT:3400], W_IN_HALF)])
    g0_rest = jnp.stack([g0_rest[:, 0:GR_ROWS // 2], g0_rest[:, GR_ROWS // 2:GR_ROWS]])
    sb_in, sb_rest = grad_sibling_exchange([g0_in, g0_rest])
    t_in, t_in_b = grad_chip_sum(g0_in, sb_in, cvec, 608, "chip_sum_w_in")
    t_rest, t_rest_b = grad_chip_sum(g0_rest, sb_rest, cvec, 384, "chip_sum_rest")
    x0_ssem, x0_rsem, x0_srcs, x0_lands, x0_token = gather_start([t_in_b, t_rest_b], [], "grads_l0_start", by_dest=True)
    dy, grads[0]["g_pre"] = layer_bwd_input(dy, dps0, layers[0], saved[0], layers[0]["g_pre"] + x0_token[0, 0])
    grad_x = dy[None]
    rb_in, rb_rest = gather_wait(x0_ssem, x0_rsem, x0_srcs, x0_lands, dy, "grads_l0_wait", by_dest=True)
    grad_rel_local = bias_grad(grads[0]["bias"] + grads[1]["bias"])
    f_in = grad_shard_sum(t_in, rb_in, mvec, 608, "shard_sum_w_in")
    f_rest = grad_shard_sum(t_rest, rb_rest, mvec, 384, "shard_sum_rest")
    fb_in, fb_rest = grad_sibling_share([f_in, f_rest], "l0_sibling_share")

    red = small_allreduce(grads, grad_rel_local, loss_part + 0.0 * f_rest[0:1, 0:128])
    loss = red[LOSS_ROW, 0]

    res = adamw_small(red, (rel_bias, m_rel_bias, v_rel_bias), dict(
        norm_pre=(norm_pre, m_norm_pre, v_norm_pre), norm_post=(norm_post, m_norm_post, v_norm_post),
        att_sinks=(att_sinks, m_att_sinks, v_att_sinks), sg_ln_g=(sg_ln_g, m_sg_ln_g, v_sg_ln_g),
        sg_ln_b=(sg_ln_b, m_sg_ln_b, v_sg_ln_b), sg_w=(sg_w, m_sg_w, v_sg_w), sg_b=(sg_b, m_sg_b, v_sg_b),
        ssm_conv_b=(ssm_conv_b, m_ssm_conv_b, v_ssm_conv_b), ssm_dt_bias=(ssm_dt_bias, m_ssm_dt_bias, v_ssm_dt_bias),
        ssm_a_log=(ssm_a_log, m_ssm_a_log, v_ssm_a_log), ssm_d=(ssm_d, m_ssm_d, v_ssm_d),
        ssm_norm_g=(ssm_norm_g, m_ssm_norm_g, v_ssm_norm_g)))
    res["w_in"] = tuple(tr(a) for a in adamw_big(
        tr(w_in), tr(m_w_in), tr(v_w_in), (f_in, fb_in, W_IN_SPLIT // 200, 0, 0), (p_in, pb_in, 0), cvec, "adamw_w_in", 200))
    rest_upd = lambda w, m, v, name, n0, off0, off1: adamw_big(
        w, m, v, (f_rest, fb_rest, n0, off0, off0), (p_rest, pb_rest, off1), cvec, name, 256)
    res["w_br_att"] = rest_upd(w_br_att, m_w_br_att, v_w_br_att, "adamw_w_br_att", 1, 0, 0)
    res["w_br_sg"] = rest_upd(w_br_sg, m_w_br_sg, v_w_br_sg, "adamw_w_br_sg", 1, 1, 1)
    res["w_out"] = rest_upd(w_out, m_w_out, v_w_out, "adamw_w_out", 1, 2, 2)
    res["w_br_ssm"] = rest_upd(w_br_ssm, m_w_br_ssm, v_w_br_ssm, "adamw_w_br_ssm", 0, 0, 3)
    cw0 = jnp.where(cc == 1, f_rest, fb_rest)[GR_CONV - GR_ROWS // 2:GR_CONV - GR_ROWS // 2 + 3]
    cw1 = (p_rest + pb_rest)[GR_CONV:GR_CONV + 3]
    g_conv_w = jnp.stack([cw0.reshape(CONV_K, 768), cw1.reshape(CONV_K, 768)])
    res["ssm_conv_w"] = (g_conv_w,) + tuple(adamw_plain(ssm_conv_w, g_conv_w, m_ssm_conv_w, v_ssm_conv_w, "adamw_conv_w"))

    order = ["w_in", "norm_pre", "norm_post", "rel_bias", "att_sinks", "sg_ln_g", "sg_ln_b", "sg_w", "sg_b",
             "ssm_conv_w", "ssm_conv_b", "ssm_dt_bias", "ssm_a_log", "ssm_d", "ssm_norm_g",
             "w_br_att", "w_br_sg", "w_br_ssm", "w_out"]
    return (loss, grad_x, *[res[n][0] for n in order], *[res[n][1] for n in order],
            *[res[n][2] for n in order], *[res[n][3] for n in order])
```
